```python
import jax, jax.numpy as jnp
from jax import lax
import numpy as np

D_MODEL = 1024
BATCH = 32
SEQ = 2048
DEPTH = 1

D_MIX = D_MODEL
D_CONV = D_MIX // 2
D_ATT = D_MIX - D_CONV
N_ATT_HEADS = 8
HEAD_DIM = D_ATT // N_ATT_HEADS
CONV_WIDTH = 31
CONV_PAD = CONV_WIDTH // 2
DILATED_PATTERNS = ((128, 1), (512, 4), (2048, 16))
D_IN = 2 * D_CONV + 3 * D_ATT
D_FF = -(-8 * D_MODEL // (3 * 256)) * 256
N_MOD = 6
EPS = 1e-6
NEG_INF = -1e30

kernel_name = "hybrid_conformer_dilated_attn_block"


def _rms_norm(x, g):
    xf = x.astype(jnp.float32)
    y = xf * lax.rsqrt(jnp.mean(xf * xf, axis=-1, keepdims=True) + EPS)
    return (y * g.astype(jnp.float32)).astype(x.dtype)


def _layer_norm(x, g, b):
    xf = x.astype(jnp.float32)
    mu = jnp.mean(xf, axis=-1, keepdims=True)
    var = jnp.mean(jnp.square(xf - mu), axis=-1, keepdims=True)
    y = (xf - mu) * lax.rsqrt(var + EPS)
    return (y * g.astype(jnp.float32) + b.astype(jnp.float32)).astype(x.dtype)


def _modulate(h, shift, scale):
    return h * (1 + scale) + shift


def _alibi_slopes(n_heads):
    return jnp.asarray(2.0 ** (-8.0 * np.arange(1, n_heads + 1) / n_heads), dtype=jnp.float32)


def _dilated_band_attention(q, k, v, slopes, window, dilation):
    B, S, H, E = q.shape
    radius = window // (2 * dilation)
    blk = radius
    n_units = -(-S // dilation)
    nb = -(-n_units // blk)
    s_pad = nb * blk * dilation
    pad = ((0, 0), (0, s_pad - S), (0, 0), (0, 0))

    def to_blocks(t):
        return jnp.pad(t, pad).reshape(B, nb, blk, dilation, H, E)

    def band(t):
        tp = jnp.pad(t, ((0, 0), (1, 1), (0, 0), (0, 0), (0, 0), (0, 0)))
        return jnp.concatenate([tp[:, :-2], tp[:, 1:-1], tp[:, 2:]], axis=2)

    qb = to_blocks(q)
    kw = band(to_blocks(k))
    vw = band(to_blocks(v))
    s = jnp.einsum('bnqrhe,bnkrhe->bnrhqk', qb, kw)

    rel = jnp.arange(3 * blk)[None, :] - blk - jnp.arange(blk)[:, None]
    n_idx = jnp.arange(nb)[:, None, None]
    r_idx = jnp.arange(dilation)[None, :, None]
    key_pos = ((n_idx - 1) * blk + jnp.arange(3 * blk)[None, None, :]) * dilation + r_idx
    valid = (jnp.abs(rel) <= radius)[None, None] & ((key_pos >= 0) & (key_pos < S))[:, :, None, :]
    bias = -slopes[:, None, None] * (dilation * jnp.abs(rel)).astype(jnp.float32)[None]
    s = jnp.where(valid[None, :, :, None], s + bias[None, None, None], NEG_INF)

    lse = jax.nn.logsumexp(s, axis=-1)
    p = jnp.exp(s - lse[..., None])
    o = jnp.einsum('bnrhqk,bnkrhe->bnqrhe', p, vw).reshape(B, s_pad, H, E)[:, :S]
    lse = lse.transpose(0, 1, 4, 2, 3).reshape(B, s_pad, H)[:, :S]
    return o, lse


def _conv_module(a, g, w_dw, b_dw, g_ln, b_ln):
    u = a * jax.nn.sigmoid(g)
    u = lax.conv_general_dilated(u, w_dw[:, None, :].astype(u.dtype), window_strides=(1,),
                                 padding=[(CONV_PAD, CONV_PAD)],
                                 dimension_numbers=('NWC', 'WIO', 'NWC'),
                                 feature_group_count=D_CONV) + b_dw
    return jax.nn.silu(_layer_norm(u, g_ln, b_ln))


def _dilated_attention(q, k, v, g_q, g_k):
    B, S, _ = q.shape
    q = q.reshape(B, S, N_ATT_HEADS, HEAD_DIM)
    k = k.reshape(B, S, N_ATT_HEADS, HEAD_DIM)
    v = v.reshape(B, S, N_ATT_HEADS, HEAD_DIM).astype(jnp.float32)
    q = _rms_norm(q, g_q).astype(jnp.float32) * (HEAD_DIM ** -0.5)
    k = _rms_norm(k, g_k).astype(jnp.float32)
    slopes = _alibi_slopes(N_ATT_HEADS)
    results = [_dilated_band_attention(q, k, v, slopes, w, d) for (w, d) in DILATED_PATTERNS]
    lses = jnp.stack([r[1] for r in results], axis=0)
    wts = jax.nn.softmax(lses, axis=0)
    o = sum(wts[i][..., None] * results[i][0] for i in range(len(results)))
    return o.reshape(B, S, D_ATT)


def _fwd_setup_inputs(seed: int = 0) -> dict:
    key = jax.random.key(seed)
    ks = jax.random.split(key, 18)
    f32 = jnp.float32
    nrm = lambda k, shape, s: jax.random.normal(k, shape, f32) * s
    gain = lambda k, shape: 1.0 + 0.02 * jax.random.normal(k, shape, f32)
    return {
        "x": jax.random.normal(ks[0], (BATCH, SEQ, D_MODEL), f32),
        "c": jax.random.normal(ks[1], (BATCH, D_MODEL), f32),
        "w_ada": nrm(ks[2], (DEPTH, D_MODEL, N_MOD * D_MODEL), D_MODEL ** -0.5),
        "b_ada": nrm(ks[3], (DEPTH, N_MOD * D_MODEL), 0.02),
        "g_mix": gain(ks[4], (DEPTH, D_MODEL)),
        "w_in": nrm(ks[5], (DEPTH, D_MODEL, D_IN), D_MODEL ** -0.5),
        "w_dw": nrm(ks[6], (DEPTH, CONV_WIDTH, D_CONV), CONV_WIDTH ** -0.5),
        "b_dw": nrm(ks[7], (DEPTH, D_CONV), 0.02),
        "g_conv_ln": gain(ks[8], (DEPTH, D_CONV)),
        "b_conv_ln": nrm(ks[9], (DEPTH, D_CONV), 0.02),
        "g_q": gain(ks[10], (DEPTH, HEAD_DIM)),
        "g_k": gain(ks[11], (DEPTH, HEAD_DIM)),
        "w_out": nrm(ks[12], (DEPTH, D_MIX, D_MODEL), D_MIX ** -0.5),
        "g_ffn": gain(ks[13], (DEPTH, D_MODEL)),
        "w_gate": nrm(ks[14], (DEPTH, D_MODEL, D_FF), D_MODEL ** -0.5),
        "w_up": nrm(ks[15], (DEPTH, D_MODEL, D_FF), D_MODEL ** -0.5),
        "w_down": nrm(ks[16], (DEPTH, D_FF, D_MODEL), D_FF ** -0.5),
    }


def _fwd_reference(x, c, w_ada, b_ada, g_mix, w_in, w_dw, b_dw, g_conv_ln, b_conv_ln,
              g_q, g_k, w_out, g_ffn, w_gate, w_up, w_down):
    B, S, D = x.shape
    split_at = [D_CONV, 2 * D_CONV, 2 * D_CONV + D_ATT, 2 * D_CONV + 2 * D_ATT]
    for l in range(DEPTH):
        mod = jax.nn.silu(c) @ w_ada[l] + b_ada[l]
        shift_m, scale_m, gate_m, shift_f, scale_f, gate_f = [
            m[:, None, :] for m in jnp.split(mod, N_MOD, axis=-1)]

        h = _modulate(_rms_norm(x, g_mix[l]), shift_m, scale_m)
        proj = h @ w_in[l]
        a, g, q, k, v = jnp.split(proj, split_at, axis=-1)
        y_conv = _conv_module(a, g, w_dw[l], b_dw[l], g_conv_ln[l], b_conv_ln[l])
        y_att = _dilated_attention(q, k, v, g_q[l], g_k[l]).astype(x.dtype)
        mix = jnp.concatenate([y_conv, y_att], axis=-1) @ w_out[l]
        x = x + gate_m * mix

        h = _modulate(_rms_norm(x, g_ffn[l]), shift_f, scale_f)
        f = (jax.nn.silu(h @ w_gate[l]) * (h @ w_up[l])) @ w_down[l]
        x = x + gate_f * f
    return x


import jax as _jax
import jax.numpy as _jnp

TWIN_FORMAT = 'train_step'
FWD_PARAMS = ['x', 'c', 'w_ada', 'b_ada', 'g_mix', 'w_in', 'w_dw', 'b_dw', 'g_conv_ln', 'b_conv_ln', 'g_q', 'g_k', 'w_out', 'g_ffn', 'w_gate', 'w_up', 'w_down']
TWIN_WEIGHTS = ['w_ada', 'b_ada', 'g_mix', 'w_in', 'w_dw', 'b_dw', 'g_conv_ln', 'b_conv_ln', 'g_q', 'g_k', 'w_out', 'g_ffn', 'w_gate', 'w_up', 'w_down']
TWIN_DIFF_INPUT = 'x'
TWIN_INPUTS = ['x', 'c', 'w_ada', 'b_ada', 'g_mix', 'w_in', 'w_dw', 'b_dw', 'g_conv_ln', 'b_conv_ln', 'g_q', 'g_k', 'w_out', 'g_ffn', 'w_gate', 'w_up', 'w_down', 'loss_target', 'm_w_ada', 'm_b_ada', 'm_g_mix', 'm_w_in', 'm_w_dw', 'm_b_dw', 'm_g_conv_ln', 'm_b_conv_ln', 'm_g_q', 'm_g_k', 'm_w_out', 'm_g_ffn', 'm_w_gate', 'm_w_up', 'm_w_down', 'v_w_ada', 'v_b_ada', 'v_g_mix', 'v_w_in', 'v_w_dw', 'v_b_dw', 'v_g_conv_ln', 'v_b_conv_ln', 'v_g_q', 'v_g_k', 'v_w_out', 'v_g_ffn', 'v_w_gate', 'v_w_up', 'v_w_down']
TWIN_OUTPUTS = ['loss', 'grad_x', 'grad_w_ada', 'grad_b_ada', 'grad_g_mix', 'grad_w_in', 'grad_w_dw', 'grad_b_dw', 'grad_g_conv_ln', 'grad_b_conv_ln', 'grad_g_q', 'grad_g_k', 'grad_w_out', 'grad_g_ffn', 'grad_w_gate', 'grad_w_up', 'grad_w_down', 'delta_w_ada', 'delta_b_ada', 'delta_g_mix', 'delta_w_in', 'delta_w_dw', 'delta_b_dw', 'delta_g_conv_ln', 'delta_b_conv_ln', 'delta_g_q', 'delta_g_k', 'delta_w_out', 'delta_g_ffn', 'delta_w_gate', 'delta_w_up', 'delta_w_down', 'new_m_w_ada', 'new_m_b_ada', 'new_m_g_mix', 'new_m_w_in', 'new_m_w_dw', 'new_m_b_dw', 'new_m_g_conv_ln', 'new_m_b_conv_ln', 'new_m_g_q', 'new_m_g_k', 'new_m_w_out', 'new_m_g_ffn', 'new_m_w_gate', 'new_m_w_up', 'new_m_w_down', 'new_v_w_ada', 'new_v_b_ada', 'new_v_g_mix', 'new_v_w_in', 'new_v_w_dw', 'new_v_b_dw', 'new_v_g_conv_ln', 'new_v_b_conv_ln', 'new_v_g_q', 'new_v_g_k', 'new_v_w_out', 'new_v_g_ffn', 'new_v_w_gate', 'new_v_w_up', 'new_v_w_down']
TWIN_LEAF_KINDS = {'loss': 'loss', 'grad_x': 'grad_x', 'grad_w_ada': 'grad_w', 'grad_b_ada': 'grad_w', 'grad_g_mix': 'grad_w', 'grad_w_in': 'grad_w', 'grad_w_dw': 'grad_w', 'grad_b_dw': 'grad_w', 'grad_g_conv_ln': 'grad_w', 'grad_b_conv_ln': 'grad_w', 'grad_g_q': 'grad_w', 'grad_g_k': 'grad_w', 'grad_w_out': 'grad_w', 'grad_g_ffn': 'grad_w', 'grad_w_gate': 'grad_w', 'grad_w_up': 'grad_w', 'grad_w_down': 'grad_w', 'delta_w_ada': 'delta_w', 'delta_b_ada': 'delta_w', 'delta_g_mix': 'delta_w', 'delta_w_in': 'delta_w', 'delta_w_dw': 'delta_w', 'delta_b_dw': 'delta_w', 'delta_g_conv_ln': 'delta_w', 'delta_b_conv_ln': 'delta_w', 'delta_g_q': 'delta_w', 'delta_g_k': 'delta_w', 'delta_w_out': 'delta_w', 'delta_g_ffn': 'delta_w', 'delta_w_gate': 'delta_w', 'delta_w_up': 'delta_w', 'delta_w_down': 'delta_w', 'new_m_w_ada': 'new_m', 'new_m_b_ada': 'new_m', 'new_m_g_mix': 'new_m', 'new_m_w_in': 'new_m', 'new_m_w_dw': 'new_m', 'new_m_b_dw': 'new_m', 'new_m_g_conv_ln': 'new_m', 'new_m_b_conv_ln': 'new_m', 'new_m_g_q': 'new_m', 'new_m_g_k': 'new_m', 'new_m_w_out': 'new_m', 'new_m_g_ffn': 'new_m', 'new_m_w_gate': 'new_m', 'new_m_w_up': 'new_m', 'new_m_w_down': 'new_m', 'new_v_w_ada': 'new_v', 'new_v_b_ada': 'new_v', 'new_v_g_mix': 'new_v', 'new_v_w_in': 'new_v', 'new_v_w_dw': 'new_v', 'new_v_b_dw': 'new_v', 'new_v_g_conv_ln': 'new_v', 'new_v_b_conv_ln': 'new_v', 'new_v_g_q': 'new_v', 'new_v_g_k': 'new_v', 'new_v_w_out': 'new_v', 'new_v_g_ffn': 'new_v', 'new_v_w_gate': 'new_v', 'new_v_w_up': 'new_v', 'new_v_w_down': 'new_v'}


def _forward(args):
    return _fwd_reference(*[args[k] for k in FWD_PARAMS])


def _output_shape():
    out = _jax.eval_shape(lambda: _forward(_fwd_setup_inputs(0)))
    return out.shape, out.dtype

N_MICROBATCH = 1
ADAM_LR = 0.001
ADAM_B1 = 0.9
ADAM_B2 = 0.999
ADAM_EPS = 1e-08
ADAM_WD = 0.01
ADAM_STEP = 10
PER_EXAMPLE_BATCH_AXIS = {'x': 0, 'c': 0, 'loss_target': 0}
SHARED_INPUTS = []
_WEIGHT_DTYPES = {'w_ada': _jnp.float32, 'b_ada': _jnp.float32, 'g_mix': _jnp.float32, 'w_in': _jnp.float32, 'w_dw': _jnp.float32, 'b_dw': _jnp.float32, 'g_conv_ln': _jnp.float32, 'b_conv_ln': _jnp.float32, 'g_q': _jnp.float32, 'g_k': _jnp.float32, 'w_out': _jnp.float32, 'g_ffn': _jnp.float32, 'w_gate': _jnp.float32, 'w_up': _jnp.float32, 'w_down': _jnp.float32}
MOMENT_SCALE = {'w_ada': 1.157639e+01, 'b_ada': 2.406982e+01, 'g_mix': 1.000391e+00, 'w_in': 5.483779e+00, 'w_dw': 4.340747e+00, 'b_dw': 1.419316e+01, 'g_conv_ln': 1.761111e+01, 'b_conv_ln': 1.343131e+01, 'g_q': 5.238115e+00, 'g_k': 5.266902e+00, 'w_out': 8.902238e+00, 'g_ffn': 5.879944e+01, 'w_gate': 3.807492e+00, 'w_up': 2.866331e+00, 'w_down': 3.220232e+00}


def _to_microbatches(a, axis):
    t = _jnp.moveaxis(a, axis, 0)
    t = t.reshape((N_MICROBATCH, t.shape[0] // N_MICROBATCH) + t.shape[1:])
    return _jnp.moveaxis(t, 1, axis + 1)


def setup_inputs(seed: int = 0) -> dict:
    inp = _fwd_setup_inputs(seed)
    key = _jax.random.fold_in(_jax.random.key(seed), 7919)
    shape, _ = _output_shape()
    out = dict(inp)
    out["loss_target"] = _jax.random.normal(_jax.random.fold_in(key, 0), shape, _jnp.float32)
    for i, name in enumerate(TWIN_WEIGHTS):
        w = inp[name].astype(_jnp.float32)
        if MOMENT_SCALE is None:
            s = _jnp.sqrt(_jnp.mean(_jnp.square(w)) + 1e-30)
        else:
            s = MOMENT_SCALE[name]
        km, kv = _jax.random.split(_jax.random.fold_in(key, i + 1))
        out[name] = w
        out["m_" + name] = s * _jax.random.normal(km, w.shape, _jnp.float32)
        out["v_" + name] = (s * s) * _jax.random.uniform(kv, w.shape, _jnp.float32, 0.5, 1.5)
    if N_MICROBATCH > 1:
        for name, axis in PER_EXAMPLE_BATCH_AXIS.items():
            out[name] = _to_microbatches(out[name], axis)
    return {'x': out['x'], 'c': out['c'], 'w_ada': out['w_ada'], 'b_ada': out['b_ada'], 'g_mix': out['g_mix'], 'w_in': out['w_in'], 'w_dw': out['w_dw'], 'b_dw': out['b_dw'], 'g_conv_ln': out['g_conv_ln'], 'b_conv_ln': out['b_conv_ln'], 'g_q': out['g_q'], 'g_k': out['g_k'], 'w_out': out['w_out'], 'g_ffn': out['g_ffn'], 'w_gate': out['w_gate'], 'w_up': out['w_up'], 'w_down': out['w_down'], 'loss_target': out['loss_target'], 'm_w_ada': out['m_w_ada'], 'm_b_ada': out['m_b_ada'], 'm_g_mix': out['m_g_mix'], 'm_w_in': out['m_w_in'], 'm_w_dw': out['m_w_dw'], 'm_b_dw': out['m_b_dw'], 'm_g_conv_ln': out['m_g_conv_ln'], 'm_b_conv_ln': out['m_b_conv_ln'], 'm_g_q': out['m_g_q'], 'm_g_k': out['m_g_k'], 'm_w_out': out['m_w_out'], 'm_g_ffn': out['m_g_ffn'], 'm_w_gate': out['m_w_gate'], 'm_w_up': out['m_w_up'], 'm_w_down': out['m_w_down'], 'v_w_ada': out['v_w_ada'], 'v_b_ada': out['v_b_ada'], 'v_g_mix': out['v_g_mix'], 'v_w_in': out['v_w_in'], 'v_w_dw': out['v_w_dw'], 'v_b_dw': out['v_b_dw'], 'v_g_conv_ln': out['v_g_conv_ln'], 'v_b_conv_ln': out['v_b_conv_ln'], 'v_g_q': out['v_g_q'], 'v_g_k': out['v_g_k'], 'v_w_out': out['v_w_out'], 'v_g_ffn': out['v_g_ffn'], 'v_w_gate': out['v_w_gate'], 'v_w_up': out['v_w_up'], 'v_w_down': out['v_w_down']}


def _loss(weights, diff, rest, loss_target):
    with _jax.named_scope("forward"):
        args = {**rest, TWIN_DIFF_INPUT: diff, **{k: w.astype(_WEIGHT_DTYPES[k]) for k, w in weights.items()}}
        y = _forward(args)
    with _jax.named_scope("loss_head"):
        err = _jnp.square(y.astype(_jnp.float32) - loss_target)
        return 0.5 * _jnp.sum(_jnp.mean(err, axis=-1)) if err.ndim else 0.5 * err


def _adamw(w, g, m, v):
    m = ADAM_B1 * m + (1.0 - ADAM_B1) * g
    v = ADAM_B2 * v + (1.0 - ADAM_B2) * _jnp.square(g)
    m_hat = m / (1.0 - ADAM_B1 ** ADAM_STEP)
    v_hat = v / (1.0 - ADAM_B2 ** ADAM_STEP)
    delta = -ADAM_LR * (m_hat / (_jnp.sqrt(v_hat) + ADAM_EPS) + ADAM_WD * w)
    return delta, m, v


def reference(x, c, w_ada, b_ada, g_mix, w_in, w_dw, b_dw, g_conv_ln, b_conv_ln, g_q, g_k, w_out, g_ffn, w_gate, w_up, w_down, loss_target, m_w_ada, m_b_ada, m_g_mix, m_w_in, m_w_dw, m_b_dw, m_g_conv_ln, m_b_conv_ln, m_g_q, m_g_k, m_w_out, m_g_ffn, m_w_gate, m_w_up, m_w_down, v_w_ada, v_b_ada, v_g_mix, v_w_in, v_w_dw, v_b_dw, v_g_conv_ln, v_b_conv_ln, v_g_q, v_g_k, v_w_out, v_g_ffn, v_w_gate, v_w_up, v_w_down):
    given = dict(x=x, c=c, w_ada=w_ada, b_ada=b_ada, g_mix=g_mix, w_in=w_in, w_dw=w_dw, b_dw=b_dw, g_conv_ln=g_conv_ln, b_conv_ln=b_conv_ln, g_q=g_q, g_k=g_k, w_out=w_out, g_ffn=g_ffn, w_gate=w_gate, w_up=w_up, w_down=w_down, loss_target=loss_target, m_w_ada=m_w_ada, m_b_ada=m_b_ada, m_g_mix=m_g_mix, m_w_in=m_w_in, m_w_dw=m_w_dw, m_b_dw=m_b_dw, m_g_conv_ln=m_g_conv_ln, m_b_conv_ln=m_b_conv_ln, m_g_q=m_g_q, m_g_k=m_g_k, m_w_out=m_w_out, m_g_ffn=m_g_ffn, m_w_gate=m_w_gate, m_w_up=m_w_up, m_w_down=m_w_down, v_w_ada=v_w_ada, v_b_ada=v_b_ada, v_g_mix=v_g_mix, v_w_in=v_w_in, v_w_dw=v_w_dw, v_b_dw=v_b_dw, v_g_conv_ln=v_g_conv_ln, v_b_conv_ln=v_b_conv_ln, v_g_q=v_g_q, v_g_k=v_g_k, v_w_out=v_w_out, v_g_ffn=v_g_ffn, v_w_gate=v_w_gate, v_w_up=v_w_up, v_w_down=v_w_down)
    weights = {n: given[n] for n in TWIN_WEIGHTS}
    shared = {n: given[n] for n in SHARED_INPUTS}
    per_example = {n: given[n] for n in ['x', 'c']}
    grad_fn = _jax.value_and_grad(_loss, argnums=(0, 1))

    def one_microbatch(ex, loss_target):
        ex = dict(ex)
        diff = ex.pop(TWIN_DIFF_INPUT)
        return grad_fn(weights, diff, {**shared, **ex}, loss_target)

    if N_MICROBATCH == 1:
        loss, (grad_w, grad_x) = one_microbatch(per_example, given["loss_target"])
    else:
        def body(carry, xs):
            loss_sum, grad_sum = carry
            l_k, (gw_k, gx_k) = one_microbatch(xs[0], xs[1])
            with _jax.named_scope("update"):
                return (loss_sum + l_k, _jax.tree.map(_jnp.add, grad_sum, gw_k)), gx_k

        init = (_jnp.zeros((), _jnp.float32), _jax.tree.map(_jnp.zeros_like, weights))
        (loss, grad_w), grad_x = _jax.lax.scan(body, init, (per_example, given["loss_target"]))
    with _jax.named_scope("update"):
        delta_w, new_m, new_v = {}, {}, {}
        for n in TWIN_WEIGHTS:
            delta_w[n], new_m[n], new_v[n] = _adamw(weights[n], grad_w[n], given["m_" + n], given["v_" + n])
    return (loss, grad_x, *[grad_w[n] for n in TWIN_WEIGHTS], *[delta_w[n] for n in TWIN_WEIGHTS],
            *[new_m[n] for n in TWIN_WEIGHTS], *[new_v[n] for n in TWIN_WEIGHTS])
```

```python
import numpy as np
import jax
import jax.numpy as jnp
from jax import lax
from jax.experimental import pallas as pl
from jax.experimental.pallas import tpu as pltpu

F32 = jnp.float32
BF16 = jnp.bfloat16

N_DEV = 8
D_MODEL = 1024
D_CONV = 512
D_ATT = 512
HEAD_DIM = 64
CONV_WIDTH = 31
D_IN = 2 * D_CONV + 3 * D_ATT
D_FF = 2816
N_MOD = 6
EPS = 1e-6
RADIUS = 64
DILATIONS = (1, 4, 16)
Q_BLOCK = 128
LANES = 128
VMEM_LIMIT = 56 * 1024 * 1024

ADAM_LR = 0.001
ADAM_B1 = 0.9
ADAM_B2 = 0.999
ADAM_EPS = 1e-08
ADAM_WD = 0.01
ADAM_STEP = 10

NT = (((1,), (1,)), ((), ()))
TN = (((0,), (0,)), ((), ()))


def _call(body, **kw):
    return pl.pallas_call(body, **kw)


def _params(sem=None, vmem=VMEM_LIMIT):
    return pltpu.CompilerParams(dimension_semantics=sem, vmem_limit_bytes=vmem)


def _sig(x):
    return 1.0 / (1.0 + jnp.exp(-x))


def _sds(shape, dtype):
    return jax.ShapeDtypeStruct(shape, dtype)


def _exchange(items, name):
    n = len(items)
    n_peer = N_DEV - 1

    def body(*refs):
        ins, outs = refs[:n], refs[n:2 * n]
        send_sems, recv_sems, local_sems = refs[2 * n:]
        x, y, c = lax.axis_index("x"), lax.axis_index("y"), lax.axis_index("c")
        me = 4 * x + 2 * y + c

        def src(a, slot):
            return ins[a].at[slot] if items[a][1] else ins[a]

        local = []
        for a in range(n):
            cp = pltpu.make_async_copy(src(a, me), outs[a].at[me], local_sems.at[a])
            cp.start()
            local.append(cp)
        flights = []
        for k in range(1, N_DEV):
            px = 1 - x if k & 4 else x
            py = 1 - y if k & 2 else y
            pc = 1 - c if k & 1 else c
            peer = (px, py, pc)
            pid = 4 * px + 2 * py + pc
            for a in range(n):
                i = a * n_peer + k - 1
                send = pltpu.make_async_remote_copy(
                    src_ref=src(a, pid), dst_ref=outs[a].at[me],
                    send_sem=send_sems.at[i], recv_sem=recv_sems.at[i],
                    device_id=peer, device_id_type=pl.DeviceIdType.MESH)
                send.start()
                recv = pltpu.make_async_remote_copy(
                    src_ref=src(a, pid), dst_ref=outs[a].at[pid],
                    send_sem=send_sems.at[i], recv_sem=recv_sems.at[i],
                    device_id=peer, device_id_type=pl.DeviceIdType.MESH)
                flights.append((send, recv))
        for send, recv in flights:
            send.wait_send()
            recv.wait_recv()
        for cp in local:
            cp.wait()

    out_shape = []
    for arr, scatter in items:
        blk = arr.shape[1:] if scatter else arr.shape
        out_shape.append(_sds((N_DEV,) + tuple(blk), arr.dtype))
    any_spec = pl.BlockSpec(memory_space=pl.ANY)
    return _call(
        body, name=name, out_shape=out_shape,
        in_specs=[any_spec] * n, out_specs=[any_spec] * n,
        scratch_shapes=[pltpu.SemaphoreType.DMA((n * n_peer,)),
                        pltpu.SemaphoreType.DMA((n * n_peer,)),
                        pltpu.SemaphoreType.DMA((n,))],
    )(*[a for a, _ in items])


def _ada_fwd(c_all, w_ada, b_cols):
    def body(c_ref, w_ref, b_ref, o_ref):
        cv = c_ref[...]
        sc = (cv * _sig(cv)).astype(BF16)
        o_ref[...] = jnp.dot(sc, w_ref[...].astype(BF16), preferred_element_type=F32) + b_ref[...]

    return _call(body, name="ada_fwd", out_shape=_sds((c_all.shape[0], w_ada.shape[1]), F32),
                 compiler_params=_params())(c_all, w_ada, b_cols)


def _ada_bwd(c_all, dmod_cols, dmod_all):
    def body(c_ref, dc_ref, da_ref, gw_ref, gb_ref):
        cv = c_ref[...]
        sc = (cv * _sig(cv)).astype(BF16)
        gw_ref[...] = lax.dot_general(sc, dc_ref[...].astype(BF16), TN, preferred_element_type=F32)
        gb_ref[...] = jnp.sum(da_ref[...], axis=0, keepdims=True)

    return _call(body, name="ada_bwd",
                 out_shape=[_sds((c_all.shape[1], dmod_cols.shape[1]), F32), _sds((1, dmod_all.shape[1]), F32)],
                 compiler_params=_params())(c_all, dmod_cols, dmod_all)


def _mix_in(x2, mod8, g_mix, w_in, seq, tm=512):
    tokens = x2.shape[0]
    per_seq = seq // tm

    def body(x_ref, m_ref, g_ref, w_ref, h_ref, p_ref):
        xv = x_ref[...]
        r = lax.rsqrt(jnp.mean(xv * xv, axis=-1, keepdims=True) + EPS)
        h = (xv * r * g_ref[...]) * (1.0 + m_ref[1:2, :]) + m_ref[0:1, :]
        hb = h.astype(BF16)
        h_ref[...] = hb
        p_ref[...] = jnp.dot(hb, w_ref[...], preferred_element_type=F32)

    return _call(
        body, name="mix_in", grid=(tokens // tm,),
        in_specs=[pl.BlockSpec((tm, D_MODEL), lambda i: (i, 0)),
                  pl.BlockSpec((None, 8, D_MODEL), lambda i: (i // per_seq, 0, 0)),
                  pl.BlockSpec((1, D_MODEL), lambda i: (0, 0)),
                  pl.BlockSpec((D_MODEL, D_IN), lambda i: (0, 0))],
        out_specs=[pl.BlockSpec((tm, D_MODEL), lambda i: (i, 0)),
                   pl.BlockSpec((tm, D_IN), lambda i: (i, 0))],
        out_shape=[_sds((tokens, D_MODEL), BF16), _sds((tokens, D_IN), F32)],
        compiler_params=_params(("parallel",)),
    )(x2, mod8, g_mix, w_in)


CONV_ROWS = 32
CONV_HALO = 16


def _fill_shifted(xp, sh, seq):
    for b in range(8):
        sh[b, pl.ds(0, seq + 24), :] = xp[pl.ds(b, seq + 24), :]


def _conv_fwd(proj3, w_dw, b_dw):
    n_seq, seq, _ = proj3.shape
    n_cb = D_CONV // LANES

    def body(a_ref, g_ref, w_ref, b_ref, uc_ref, xp, sh):
        zeros = jnp.zeros((CONV_HALO, LANES), F32)
        xp[pl.ds(0, CONV_HALO), :] = zeros
        xp[pl.ds(CONV_HALO + seq, CONV_HALO), :] = zeros
        xp[pl.ds(CONV_HALO, seq), :] = a_ref[...] * _sig(g_ref[...])
        _fill_shifted(xp, sh, seq)

        def blk(i, carry):
            t0 = pl.multiple_of(i * CONV_ROWS, CONV_ROWS)
            acc = jnp.zeros((CONV_ROWS, LANES), F32)
            for j in range(CONV_WIDTH):
                jj = j + 1
                acc = acc + sh[jj % 8, pl.ds(t0 + 8 * (jj // 8), CONV_ROWS), :] * w_ref[j:j + 1, :]
            uc_ref[pl.ds(t0, CONV_ROWS), :] = acc + b_ref[...]
            return carry

        lax.fori_loop(0, seq // CONV_ROWS, blk, 0)

    return _call(
        body, name="conv_fwd", grid=(n_seq, n_cb),
        in_specs=[pl.BlockSpec((None, seq, LANES), lambda b, cb: (b, 0, cb)),
                  pl.BlockSpec((None, seq, LANES), lambda b, cb: (b, 0, n_cb + cb)),
                  pl.BlockSpec((CONV_WIDTH, LANES), lambda b, cb: (0, cb)),
                  pl.BlockSpec((1, LANES), lambda b, cb: (0, cb))],
        out_specs=pl.BlockSpec((None, seq, LANES), lambda b, cb: (b, 0, cb)),
        out_shape=_sds((n_seq, seq, D_CONV), F32),
        scratch_shapes=[pltpu.VMEM((seq + 2 * CONV_HALO, LANES), F32),
                        pltpu.VMEM((8, seq + 2 * CONV_HALO, LANES), F32)],
        compiler_params=_params(("parallel", "parallel")),
    )(proj3, proj3, w_dw, b_dw)


def _conv_bwd(duc3, proj3, w_dw):
    n_seq, seq, _ = proj3.shape
    n_cb = D_CONV // LANES

    def body(duc_ref, a_ref, g_ref, w_ref, da_ref, dg_ref, dw_ref, db_ref, xp, sh):
        @pl.when(pl.program_id(1) == 0)
        def _():
            dw_ref[...] = jnp.zeros_like(dw_ref)
            db_ref[...] = jnp.zeros_like(db_ref)

        zeros = jnp.zeros((CONV_HALO, LANES), F32)
        xp[pl.ds(0, CONV_HALO), :] = zeros
        xp[pl.ds(CONV_HALO + seq, CONV_HALO), :] = zeros
        xp[pl.ds(CONV_HALO, seq), :] = a_ref[...] * _sig(g_ref[...])
        _fill_shifted(xp, sh, seq)
        for j in range(CONV_WIDTH):
            jj = j + 1

            def wblk(i, acc, jj=jj):
                t0 = pl.multiple_of(i * CONV_ROWS, CONV_ROWS)
                return acc + duc_ref[pl.ds(t0, CONV_ROWS), :] * sh[jj % 8, pl.ds(t0 + 8 * (jj // 8), CONV_ROWS), :]

            acc = lax.fori_loop(0, seq // CONV_ROWS, wblk, jnp.zeros((CONV_ROWS, LANES), F32))
            dw_ref[j:j + 1, :] += jnp.sum(acc, axis=0, keepdims=True)
        db_ref[0:1, :] += jnp.sum(duc_ref[...], axis=0, keepdims=True)
        xp[pl.ds(CONV_HALO, seq), :] = duc_ref[...]
        _fill_shifted(xp, sh, seq)

        def ublk(i, carry):
            t0 = pl.multiple_of(i * CONV_ROWS, CONV_ROWS)
            acc = jnp.zeros((CONV_ROWS, LANES), F32)
            for j in range(CONV_WIDTH):
                jj = CONV_WIDTH - j
                acc = acc + sh[jj % 8, pl.ds(t0 + 8 * (jj // 8), CONV_ROWS), :] * w_ref[j:j + 1, :]
            av = a_ref[pl.ds(t0, CONV_ROWS), :]
            sg = _sig(g_ref[pl.ds(t0, CONV_ROWS), :])
            da_ref[pl.ds(t0, CONV_ROWS), :] = (acc * sg).astype(BF16)
            dg_ref[pl.ds(t0, CONV_ROWS), :] = (acc * av * sg * (1.0 - sg)).astype(BF16)
            return carry

        lax.fori_loop(0, seq // CONV_ROWS, ublk, 0)

    return _call(
        body, name="conv_bwd", grid=(n_cb, n_seq),
        in_specs=[pl.BlockSpec((None, seq, LANES), lambda cb, b: (b, 0, cb)),
                  pl.BlockSpec((None, seq, LANES), lambda cb, b: (b, 0, cb)),
                  pl.BlockSpec((None, seq, LANES), lambda cb, b: (b, 0, n_cb + cb)),
                  pl.BlockSpec((CONV_WIDTH, LANES), lambda cb, b: (0, cb))],
        out_specs=[pl.BlockSpec((None, seq, LANES), lambda cb, b: (b, 0, cb)),
                   pl.BlockSpec((None, seq, LANES), lambda cb, b: (b, 0, cb)),
                   pl.BlockSpec((32, LANES), lambda cb, b: (0, cb)),
                   pl.BlockSpec((8, LANES), lambda cb, b: (0, cb))],
        out_shape=[_sds((n_seq, seq, D_CONV), BF16), _sds((n_seq, seq, D_CONV), BF16),
                   _sds((32, D_CONV), F32), _sds((8, D_CONV), F32)],
        scratch_shapes=[pltpu.VMEM((seq + 2 * CONV_HALO, LANES), F32),
                        pltpu.VMEM((8, seq + 2 * CONV_HALO, LANES), F32)],
        compiler_params=_params(("parallel", "arbitrary")),
    )(duc3, proj3, proj3, w_dw)


MASKED = 1e30


def _distance_mats(dil, seg_len):
    kw = min(2 * Q_BLOCK, seg_len)
    offsets = (0, -RADIUS, -2 * RADIUS) if kw == 2 * Q_BLOCK else (0,)
    a = np.arange(Q_BLOCK)[:, None]
    b = np.arange(kw)[None, :]
    mats = []
    for off in offsets:
        rel = np.abs(b + off - a)
        mats.append(np.where(rel <= RADIUS, dil * rel, MASKED))
    return jnp.asarray(np.stack(mats).astype(np.float32))


def _alibi_rows():
    s = np.zeros((4, 8, LANES), np.float32)
    for hp in range(4):
        for hl in range(2):
            s[hp, hl, :] = 2.0 ** (-(2 * hp + hl + 1))
    return jnp.asarray(s)


def _window(n, seg_len):
    i0 = pl.multiple_of(n * Q_BLOCK, Q_BLOCK)
    if seg_len <= Q_BLOCK:
        return i0, i0, 0
    per_seg = seg_len // Q_BLOCK
    j = n % per_seg
    seg0 = (n // per_seg) * seg_len
    ks_local = jnp.clip(j * Q_BLOCK - RADIUS, 0, seg_len - 2 * Q_BLOCK)
    ks = pl.multiple_of(seg0 + ks_local, RADIUS)
    var = jnp.where(j == 0, 0, jnp.where(j == per_seg - 1, 2, 1))
    return i0, ks, var


def _permute_in(dst, src_fn, dil, seq, dtype):
    seg = seq // dil
    for h in range(2):
        for r in range(dil):
            rows = pl.ds(r, seg, stride=dil) if dil > 1 else pl.ds(0, seq)
            dst[h, pl.ds(r * seg, seg), :] = src_fn(h, rows).astype(dtype)


def _qk_normalise(q_ref, g_ref, dst, seq, scale):
    def chunk(ci, carry):
        r0 = pl.multiple_of(ci * 256, 256)
        for h in range(2):
            qh = q_ref[pl.ds(r0, 256), h * HEAD_DIM:(h + 1) * HEAD_DIM]
            r = lax.rsqrt(jnp.mean(qh * qh, axis=1, keepdims=True) + EPS)
            dst[h, pl.ds(r0, 256), :] = qh * r * (g_ref[...] * scale)
        return carry

    lax.fori_loop(0, seq // 256, chunk, 0)


def _attn_fwd(proj3, g_q, g_k):
    n_seq, seq, _ = proj3.shape
    dms = [_distance_mats(d, seq // d) for d in DILATIONS]
    col0 = 2 * D_CONV // LANES
    n_hp = D_ATT // LANES

    def body(q_ref, k_ref, v_ref, gq_ref, gk_ref, sl_ref, dm1, dm4, dm16, y_ref, lse_ref,
             qf, kf, qp, kp, vp, op, sp, o1, o4, o16, s1, s4, s16):
        dm_refs = (dm1, dm4, dm16)
        o_nat = (o1, o4, o16)
        s_nat = (s1, s4, s16)
        _qk_normalise(q_ref, gq_ref, qf, seq, HEAD_DIM ** -0.5)
        _qk_normalise(k_ref, gk_ref, kf, seq, 1.0)
        for pi, dil in enumerate(DILATIONS):
            seg = seq // dil
            kw = min(2 * Q_BLOCK, seg)
            _permute_in(qp, lambda h, rows: qf[h, rows, :], dil, seq, BF16)
            _permute_in(kp, lambda h, rows: kf[h, rows, :], dil, seq, BF16)
            _permute_in(vp, lambda h, rows: v_ref[rows, :][:, h * HEAD_DIM:(h + 1) * HEAD_DIM], dil, seq, BF16)
            o_dst =o_nat[pi] if dil == 1 else op
            s_dst = s_nat[pi] if dil == 1 else sp
            for h in range(2):
                slope = sl_ref[h:h + 1, 0:1]

                def blk(n, carry, h=h, slope=slope, seg=seg, kw=kw, pi=pi, o_dst=o_dst, s_dst=s_dst):
                    i0, ks, var = _window(n, seg)
                    q = qp[h, pl.ds(i0, Q_BLOCK), :]
                    k = kp[h, pl.ds(ks, kw), :]
                    v = vp[h, pl.ds(ks, kw), :]
                    s = lax.dot_general(q, k, NT, preferred_element_type=F32) - slope * dm_refs[pi][var]
                    m = jnp.max(s, axis=1, keepdims=True)
                    p = jnp.exp(s - m)
                    o_dst[pl.ds(i0, Q_BLOCK), h * HEAD_DIM:(h + 1) * HEAD_DIM] = jnp.dot(
                        p.astype(BF16), v, preferred_element_type=F32)
                    s_dst[pl.ds(i0, Q_BLOCK), h:h + 1] = m
                    s_dst[pl.ds(i0, Q_BLOCK), 2 + h:3 + h] = jnp.sum(p, axis=1, keepdims=True)
                    return carry

                lax.fori_loop(0, seq // Q_BLOCK, blk, 0)
            if dil > 1:
                for r in range(dil):
                    o_nat[pi][pl.ds(r, seg, stride=dil), :] = op[pl.ds(r * seg, seg), :]
                    s_nat[pi][pl.ds(r, seg, stride=dil), :] = sp[pl.ds(r * seg, seg), :]

        def merge(ci, carry):
            r0 = pl.multiple_of(ci * 256, 256)
            rows = pl.ds(r0, 256)
            for h in range(2):
                ms = [s_nat[pi][rows, h:h + 1] for pi in range(3)]
                ls = [s_nat[pi][rows, 2 + h:3 + h] for pi in range(3)]
                m_all = jnp.maximum(jnp.maximum(ms[0], ms[1]), ms[2])
                es = [jnp.exp(m - m_all) for m in ms]
                l_all = ls[0] * es[0] + ls[1] * es[1] + ls[2] * es[2]
                inv = 1.0 / l_all
                lanes = slice(h * HEAD_DIM, (h + 1) * HEAD_DIM)
                o = (o_nat[0][rows, lanes] * (es[0] * inv) + o_nat[1][rows, lanes] * (es[1] * inv)
                     + o_nat[2][rows, lanes] * (es[2] * inv))
                y_ref[rows, lanes] = o.astype(BF16)
                lse_ref[rows, lanes] = jnp.broadcast_to(m_all + jnp.log(l_all), (256, HEAD_DIM))
            return carry

        lax.fori_loop(0, seq // 256, merge, 0)

    def col(off):
        return pl.BlockSpec((None, seq, LANES), lambda b, hp: (b, 0, col0 + off * n_hp + hp))

    def whole(arr):
        return pl.BlockSpec(arr.shape, lambda b, hp: (0,) * arr.ndim)

    head_f32 = pltpu.VMEM((2, seq, HEAD_DIM), F32)
    head_bf16 = pltpu.VMEM((2, seq, HEAD_DIM), BF16)
    rows_f32 = pltpu.VMEM((seq, LANES), F32)
    return _call(
        body, name="attn_fwd", grid=(n_seq, n_hp),
        in_specs=[col(0), col(1), col(2), whole(g_q), whole(g_k),
                  pl.BlockSpec((None, 8, LANES), lambda b, hp: (hp, 0, 0)),
                  whole(dms[0]), whole(dms[1]), whole(dms[2])],
        out_specs=[pl.BlockSpec((None, seq, LANES), lambda b, hp: (b, 0, hp)),
                   pl.BlockSpec((None, seq, LANES), lambda b, hp: (b, 0, hp))],
        out_shape=[_sds((n_seq, seq, D_ATT), BF16), _sds((n_seq, seq, D_ATT), F32)],
        scratch_shapes=[head_f32, head_f32, head_bf16, head_bf16, head_bf16, rows_f32, rows_f32,
                        rows_f32, rows_f32, rows_f32, rows_f32, rows_f32, rows_f32],
        compiler_params=_params(("parallel", "parallel")),
    )(proj3, proj3, proj3, g_q, g_k, _alibi_rows(), *dms)


def _attn_bwd(proj3, do3, y_att3, lse3, g_q, g_k):
    n_seq, seq, _ = proj3.shape
    dms = [_distance_mats(d, seq // d) for d in DILATIONS]
    col0 = 2 * D_CONV // LANES
    n_hp = D_ATT // LANES

    def body(q_ref, k_ref, v_ref, do_ref, o_ref, lse_ref, gq_ref, gk_ref, sl_ref, dm1, dm4, dm16,
             dq_ref, dk_ref, dv_ref, dg_ref,
             qf, kf, qp, kp, vp, dop, sn, sp, dqp, dkp, dvp, dqn, dkn, dvn):
        dm_refs = (dm1, dm4, dm16)

        @pl.when((pl.program_id(0) == 0) & (pl.program_id(1) == 0))
        def _():
            dg_ref[...] = jnp.zeros_like(dg_ref)

        _qk_normalise(q_ref, gq_ref, qf, seq, HEAD_DIM ** -0.5)
        _qk_normalise(k_ref, gk_ref, kf, seq, 1.0)

        def stats(ci, carry):
            rows = pl.ds(pl.multiple_of(ci * 256, 256), 256)
            for h in range(2):
                lanes = slice(h * HEAD_DIM, (h + 1) * HEAD_DIM)
                sn[rows, h:h + 1] = lse_ref[rows, h * HEAD_DIM:h * HEAD_DIM + 1]
                sn[rows, 2 + h:3 + h] = jnp.sum(do_ref[rows, lanes] * o_ref[rows, lanes].astype(F32),
                                                axis=1, keepdims=True)
            return carry

        lax.fori_loop(0, seq // 256, stats, 0)

        for pi, dil in enumerate(DILATIONS):
            seg = seq // dil
            kw = min(2 * Q_BLOCK, seg)
            _permute_in(qp, lambda h, rows: qf[h, rows, :], dil, seq, BF16)
            _permute_in(kp, lambda h, rows: kf[h, rows, :], dil, seq, BF16)
            _permute_in(vp, lambda h, rows: v_ref[rows, :][:, h * HEAD_DIM:(h + 1) * HEAD_DIM], dil, seq, BF16)
            _permute_in(dop, lambda h, rows: do_ref[rows, :][:, h * HEAD_DIM:(h + 1) * HEAD_DIM], dil, seq, BF16)
            if dil == 1:
                st, dq_dst, dk_dst, dv_dst = sn, dqn, dkn, dvn
            else:
                st, dq_dst, dk_dst, dv_dst = sp, dqp, dkp, dvp
                for r in range(dil):
                    sp[pl.ds(r * seg, seg), :] = sn[pl.ds(r, seg, stride=dil), :]
            dk_dst[...] = jnp.zeros_like(dk_dst)
            dv_dst[...] = jnp.zeros_like(dv_dst)
            for h in range(2):
                slope = sl_ref[h:h + 1, 0:1]

                def blk(n, carry, h=h, slope=slope, seg=seg, kw=kw, pi=pi, st=st,
                        dq_dst=dq_dst, dk_dst=dk_dst, dv_dst=dv_dst):
                    i0, ks, var = _window(n, seg)
                    qrows, krows = pl.ds(i0, Q_BLOCK), pl.ds(ks, kw)
                    q = qp[h, qrows, :]
                    k = kp[h, krows, :]
                    v = vp[h, krows, :]
                    do = dop[h, qrows, :]
                    s = lax.dot_general(q, k, NT, preferred_element_type=F32) - slope * dm_refs[pi][var]
                    p = jnp.exp(s - st[qrows, h:h + 1])
                    dp = lax.dot_general(do, v, NT, preferred_element_type=F32)
                    ds = (p * (dp - st[qrows, 2 + h:3 + h])).astype(BF16)
                    dq_dst[h, qrows, :] = jnp.dot(ds, k, preferred_element_type=F32)
                    dk_dst[h, krows, :] += lax.dot_general(ds, q, TN, preferred_element_type=F32)
                    dv_dst[h, krows, :] += lax.dot_general(p.astype(BF16), do, TN, preferred_element_type=F32)
                    return carry

                lax.fori_loop(0, seq // Q_BLOCK, blk, 0)
            if dil > 1:
                for h in range(2):
                    for r in range(dil):
                        nat, perm = pl.ds(r, seg, stride=dil), pl.ds(r * seg, seg)
                        dqn[h, nat, :] += dqp[h, perm, :]
                        dkn[h, nat, :] += dkp[h, perm, :]
                        dvn[h, nat, :] += dvp[h, perm, :]

        def finish(ci, carry):
            rows = pl.ds(pl.multiple_of(ci * 256, 256), 256)
            for h in range(2):
                lanes = slice(h * HEAD_DIM, (h + 1) * HEAD_DIM)
                for src_ref, g_ref, dn, dst_ref, scale, row in (
                        (q_ref, gq_ref, dqn, dq_ref, HEAD_DIM ** -0.5, 0), (k_ref, gk_ref, dkn, dk_ref, 1.0, 1)):
                    xv = src_ref[rows, lanes]
                    r = lax.rsqrt(jnp.mean(xv * xv, axis=1, keepdims=True) + EPS)
                    xhat = xv * r
                    d = dn[h, rows, :] * scale
                    dg_ref[row:row + 1, lanes] += jnp.sum(d * xhat, axis=0, keepdims=True)
                    dxh = d * g_ref[...]
                    dst_ref[rows, lanes] = (r * (dxh - xhat * jnp.mean(dxh * xhat, axis=1, keepdims=True))).astype(BF16)
                dv_ref[rows, lanes] = dvn[h, rows, :].astype(BF16)
            return carry

        lax.fori_loop(0, seq // 256, finish, 0)

    def col(off):
        return pl.BlockSpec((None, seq, LANES), lambda b, hp: (b, 0, col0 + off * n_hp + hp))

    def whole(arr):
        return pl.BlockSpec(arr.shape, lambda b, hp: (0,) * arr.ndim)

    att = pl.BlockSpec((None, seq, LANES), lambda b, hp: (b, 0, hp))
    head_f32 = pltpu.VMEM((2, seq, HEAD_DIM), F32)
    head_bf16 = pltpu.VMEM((2, seq, HEAD_DIM), BF16)
    rows_f32 = pltpu.VMEM((seq, LANES), F32)
    return _call(
        body, name="attn_bwd", grid=(n_seq, n_hp),
        in_specs=[col(0), col(1), col(2), att, att, att, whole(g_q), whole(g_k),
                  pl.BlockSpec((None, 8, LANES), lambda b, hp: (hp, 0, 0)),
                  whole(dms[0]), whole(dms[1]), whole(dms[2])],
        out_specs=[att, att, att, pl.BlockSpec((8, LANES), lambda b, hp: (0, 0))],
        out_shape=[_sds((n_seq, seq, D_ATT), BF16)] * 3 + [_sds((8, LANES), F32)],
        scratch_shapes=[head_f32, head_f32, head_bf16, head_bf16, head_bf16, head_bf16, rows_f32, rows_f32,
                        head_f32, head_f32, head_f32, head_f32, head_f32, head_f32],
        compiler_params=_params(("arbitrary", "arbitrary")),
    )(proj3, proj3, proj3, do3, y_att3, lse3, g_q, g_k, _alibi_rows(), *dms)


def _mix_out(uc2, y_att2, x2, mod8, g_ln, b_ln, g_ffn, w_out, seq, tm=512):
    tokens = x2.shape[0]
    per_seq = seq // tm

    def body(uc_ref, ya_ref, x_ref, m_ref, gl_ref, bl_ref, gf_ref, w_ref, yc_ref, mix_ref, x1_ref, h2_ref):
        uc = uc_ref[...]
        mu = jnp.mean(uc, axis=-1, keepdims=True)
        cen = uc - mu
        rs = lax.rsqrt(jnp.mean(cen * cen, axis=-1, keepdims=True) + EPS)
        z = cen * rs * gl_ref[...] + bl_ref[...]
        yc = (z * _sig(z)).astype(BF16)
        yc_ref[...] = yc
        mix = (jnp.dot(yc, w_ref[pl.ds(0, D_CONV), :], preferred_element_type=F32)
               + jnp.dot(ya_ref[...], w_ref[pl.ds(D_CONV, D_ATT), :], preferred_element_type=F32))
        mix_ref[...] = mix
        x1 = x_ref[...] + m_ref[2:3, :] * mix
        x1_ref[...] = x1
        r = lax.rsqrt(jnp.mean(x1 * x1, axis=-1, keepdims=True) + EPS)
        h2_ref[...] = ((x1 * r * gf_ref[...]) * (1.0 + m_ref[4:5, :]) + m_ref[3:4, :]).astype(BF16)

    def rows(width):
        return pl.BlockSpec((tm, width), lambda i: (i, 0))

    def vec(width):
        return pl.BlockSpec((1, width), lambda i: (0, 0))

    return _call(
        body, name="mix_out", grid=(tokens // tm,),
        in_specs=[rows(D_CONV), rows(D_ATT), rows(D_MODEL),
                  pl.BlockSpec((None, 8, D_MODEL), lambda i: (i // per_seq, 0, 0)),
                  vec(D_CONV), vec(D_CONV), vec(D_MODEL),
                  pl.BlockSpec((D_MODEL, D_MODEL), lambda i: (0, 0))],
        out_specs=[rows(D_CONV), rows(D_MODEL), rows(D_MODEL), rows(D_MODEL)],
        out_shape=[_sds((tokens, D_CONV), BF16), _sds((tokens, D_MODEL), F32),
                   _sds((tokens, D_MODEL), F32), _sds((tokens, D_MODEL), BF16)],
        compiler_params=_params(("parallel",)),
    )(uc2, y_att2, x2, mod8, g_ln, b_ln, g_ffn, w_out)


def _mix_out_bwd(dmix, uc2, g_ln, b_ln, w_out, tm=512):
    tokens = dmix.shape[0]

    def body(dm_ref, uc_ref, gl_ref, bl_ref, w_ref, duc_ref, do_ref, dgb_ref):
        @pl.when(pl.program_id(0) == 0)
        def _():
            dgb_ref[...] = jnp.zeros_like(dgb_ref)

        dmv = dm_ref[...]
        dyc = lax.dot_general(dmv, w_ref[pl.ds(0, D_CONV), :], NT, preferred_element_type=F32)
        do_ref[...] = lax.dot_general(dmv, w_ref[pl.ds(D_CONV, D_ATT), :], NT, preferred_element_type=F32)
        uc = uc_ref[...]
        mu = jnp.mean(uc, axis=-1, keepdims=True)
        cen = uc - mu
        rs = lax.rsqrt(jnp.mean(cen * cen, axis=-1, keepdims=True) + EPS)
        xh = cen * rs
        z = xh * gl_ref[...] + bl_ref[...]
        sg = _sig(z)
        dz = dyc * (sg * (1.0 + z * (1.0 - sg)))
        dgb_ref[0:1, :] += jnp.sum(dz * xh, axis=0, keepdims=True)
        dgb_ref[1:2, :] += jnp.sum(dz, axis=0, keepdims=True)
        dxh = dz * gl_ref[...]
        duc_ref[...] = rs * (dxh - jnp.mean(dxh, axis=-1, keepdims=True)
                             - xh * jnp.mean(dxh * xh, axis=-1, keepdims=True))

    return _call(
        body, name="mix_out_bwd", grid=(tokens // tm,),
        in_specs=[pl.BlockSpec((tm, D_MODEL), lambda i: (i, 0)),
                  pl.BlockSpec((tm, D_CONV), lambda i: (i, 0)),
                  pl.BlockSpec((1, D_CONV), lambda i: (0, 0)),
                  pl.BlockSpec((1, D_CONV), lambda i: (0, 0)),
                  pl.BlockSpec((D_MODEL, D_MODEL), lambda i: (0, 0))],
        out_specs=[pl.BlockSpec((tm, D_CONV), lambda i: (i, 0)),
                   pl.BlockSpec((tm, D_ATT), lambda i: (i, 0)),
                   pl.BlockSpec((8, D_CONV), lambda i: (0, 0))],
        out_shape=[_sds((tokens, D_CONV), F32), _sds((tokens, D_ATT), F32), _sds((8, D_CONV), F32)],
        compiler_params=_params(("arbitrary",)),
    )(dmix, uc2, g_ln, b_ln, w_out)


FF_TILE = 256


def _ffn_fwd(h2, w_gate, w_up, w_down, tm=1024):
    tokens = h2.shape[0]

    def body(h_ref, wg_ref, wu_ref, wd_ref, gate_ref, up_ref, f_ref):
        @pl.when(pl.program_id(1) == 0)
        def _():
            f_ref[...] = jnp.zeros_like(f_ref)

        hv = h_ref[...]
        gate = jnp.dot(hv, wg_ref[...], preferred_element_type=F32)
        up = jnp.dot(hv, wu_ref[...], preferred_element_type=F32)
        gate_ref[...] = gate
        up_ref[...] = up
        act = (gate * _sig(gate) * up).astype(BF16)
        f_ref[...] += jnp.dot(act, wd_ref[...], preferred_element_type=F32)

    return _call(
        body, name="ffn_fwd", grid=(tokens // tm, D_FF // FF_TILE),
        in_specs=[pl.BlockSpec((tm, D_MODEL), lambda i, j: (i, 0)),
                  pl.BlockSpec((D_MODEL, FF_TILE), lambda i, j: (0, j)),
                  pl.BlockSpec((D_MODEL, FF_TILE), lambda i, j: (0, j)),
                  pl.BlockSpec((FF_TILE, D_MODEL), lambda i, j: (j, 0))],
        out_specs=[pl.BlockSpec((tm, FF_TILE), lambda i, j: (i, j)),
                   pl.BlockSpec((tm, FF_TILE), lambda i, j: (i, j)),
                   pl.BlockSpec((tm, D_MODEL), lambda i, j: (i, 0))],
        out_shape=[_sds((tokens, D_FF), F32), _sds((tokens, D_FF), F32), _sds((tokens, D_MODEL), F32)],
        compiler_params=_params(("parallel", "arbitrary")),
    )(h2, w_gate, w_up, w_down)


def _ffn_bwd(df, gate, up, w_gate, w_up, w_down, tm=1024):
    tokens = df.shape[0]

    def body(df_ref, gate_ref, up_ref, wg_ref, wu_ref, wd_ref, dgate_ref, dup_ref, act_ref, dh_ref):
        @pl.when(pl.program_id(1) == 0)
        def _():
            dh_ref[...] = jnp.zeros_like(dh_ref)

        dact = lax.dot_general(df_ref[...], wd_ref[...], NT, preferred_element_type=F32)
        gate = gate_ref[...]
        up = up_ref[...]
        sg = _sig(gate)
        silu = gate * sg
        act_ref[...] = (silu * up).astype(BF16)
        dup = (dact * silu).astype(BF16)
        dgate = (dact * up * (sg * (1.0 + gate * (1.0 - sg)))).astype(BF16)
        dup_ref[...] = dup
        dgate_ref[...] = dgate
        dh_ref[...] += (lax.dot_general(dgate, wg_ref[...], NT, preferred_element_type=F32)
                        + lax.dot_general(dup, wu_ref[...], NT, preferred_element_type=F32))

    tile = pl.BlockSpec((tm, FF_TILE), lambda i, j: (i, j))
    return _call(
        body, name="ffn_bwd", grid=(tokens // tm, D_FF // FF_TILE),
        in_specs=[pl.BlockSpec((tm, D_MODEL), lambda i, j: (i, 0)), tile, tile,
                  pl.BlockSpec((D_MODEL, FF_TILE), lambda i, j: (0, j)),
                  pl.BlockSpec((D_MODEL, FF_TILE), lambda i, j: (0, j)),
                  pl.BlockSpec((FF_TILE, D_MODEL), lambda i, j: (j, 0))],
        out_specs=[tile, tile, tile, pl.BlockSpec((tm, D_MODEL), lambda i, j: (i, 0))],
        out_shape=[_sds((tokens, D_FF), BF16)] * 3 + [_sds((tokens, D_MODEL), F32)],
        compiler_params=_params(("parallel", "arbitrary")),
    )(df, gate, up, w_gate, w_up, w_down)


def _loss_head(f, x1, target, mod8, seq, tm=512):
    tokens = f.shape[0]
    per_seq = seq // tm
    n_seq = tokens // seq

    def body(f_ref, x1_ref, t_ref, m_ref, dy_ref, df_ref, sq_ref, dgf_ref):
        i = pl.program_id(0)

        @pl.when(i == 0)
        def _():
            sq_ref[...] = jnp.zeros_like(sq_ref)

        @pl.when(i % per_seq == 0)
        def _():
            dgf_ref[...] = jnp.zeros_like(dgf_ref)

        fv = f_ref[...]
        gate = m_ref[5:6, :]
        diff = x1_ref[...] + gate * fv - t_ref[...]
        sq_ref[0:1, :] += jnp.sum(diff * diff, axis=0, keepdims=True)
        dy = diff * (1.0 / D_MODEL)
        dy_ref[...] = dy
        df_ref[...] = (gate * dy).astype(BF16)
        dgf_ref[0:1, :] += jnp.sum(dy * fv, axis=0, keepdims=True)

    rows = pl.BlockSpec((tm, D_MODEL), lambda i: (i, 0))
    per = pl.BlockSpec((None, 8, D_MODEL), lambda i: (i // per_seq, 0, 0))
    return _call(
        body, name="loss_head", grid=(tokens // tm,),
        in_specs=[rows, rows, rows, per],
        out_specs=[rows, rows, pl.BlockSpec((8, D_MODEL), lambda i: (0, 0)), per],
        out_shape=[_sds((tokens, D_MODEL), F32), _sds((tokens, D_MODEL), BF16),
                   _sds((8, D_MODEL), F32), _sds((n_seq, 8, D_MODEL), F32)],
        compiler_params=_params(("arbitrary",)),
    )(f, x1, target, mod8)


def _norm2_bwd(dh2, x1, dy, mix, mod8, g_ffn, seq, tm=512):
    tokens = dh2.shape[0]
    per_seq = seq // tm
    n_seq = tokens // seq

    def body(dh_ref, x1_ref, dy_ref, mix_ref, m_ref, g_ref, dx1_ref, dmix_ref, dg_ref, dm_ref):
        i = pl.program_id(0)

        @pl.when(i == 0)
        def _():
            dg_ref[...] = jnp.zeros_like(dg_ref)

        @pl.when(i % per_seq == 0)
        def _():
            dm_ref[...] = jnp.zeros_like(dm_ref)

        dh = dh_ref[...]
        x1 = x1_ref[...]
        r = lax.rsqrt(jnp.mean(x1 * x1, axis=-1, keepdims=True) + EPS)
        xhat = x1 * r
        g = g_ref[...]
        dm_ref[0:1, :] += jnp.sum(dh, axis=0, keepdims=True)
        dm_ref[1:2, :] += jnp.sum(dh * (xhat * g), axis=0, keepdims=True)
        dn = dh * (1.0 + m_ref[4:5, :])
        dg_ref[0:1, :] += jnp.sum(dn * xhat, axis=0, keepdims=True)
        dxh = dn * g
        dx1 = dy_ref[...] + r * (dxh - xhat * jnp.mean(dxh * xhat, axis=-1, keepdims=True))
        dx1_ref[...] = dx1
        dm_ref[2:3, :] += jnp.sum(dx1 * mix_ref[...], axis=0, keepdims=True)
        dmix_ref[...] = (m_ref[2:3, :] * dx1).astype(BF16)

    rows = pl.BlockSpec((tm, D_MODEL), lambda i: (i, 0))
    per = pl.BlockSpec((None, 8, D_MODEL), lambda i: (i // per_seq, 0, 0))
    return _call(
        body, name="norm2_bwd", grid=(tokens // tm,),
        in_specs=[rows, rows, rows, rows, per, pl.BlockSpec((1, D_MODEL), lambda i: (0, 0))],
        out_specs=[rows, rows, pl.BlockSpec((8, D_MODEL), lambda i: (0, 0)), per],
        out_shape=[_sds((tokens, D_MODEL), F32), _sds((tokens, D_MODEL), BF16),
                   _sds((8, D_MODEL), F32), _sds((n_seq, 8, D_MODEL), F32)],
        compiler_params=_params(("arbitrary",)),
    )(dh2, x1, dy, mix, mod8, g_ffn)


def _mix_in_bwd(d_a, d_g, d_q, d_k, d_v, w_in, x2, dx1, mod8, g_mix, seq, tm=512):
    tokens = x2.shape[0]
    per_seq = seq // tm
    n_seq = tokens // seq
    parts = (d_a, d_g, d_q, d_k, d_v)
    width = D_CONV

    def body(da_ref, dg_ref, dq_ref, dk_ref, dv_ref, w_ref, x_ref, dx1_ref, m_ref, g_ref, gx_ref, dgm_ref, dm_ref):
        i = pl.program_id(0)

        @pl.when(i == 0)
        def _():
            dgm_ref[...] = jnp.zeros_like(dgm_ref)

        @pl.when(i % per_seq == 0)
        def _():
            dm_ref[...] = jnp.zeros_like(dm_ref)

        dh = jnp.zeros((tm, D_MODEL), F32)
        for n, ref in enumerate((da_ref, dg_ref, dq_ref, dk_ref, dv_ref)):
            dh = dh + lax.dot_general(ref[...], w_ref[:, pl.ds(n * width, width)], NT, preferred_element_type=F32)
        xv = x_ref[...]
        r = lax.rsqrt(jnp.mean(xv * xv, axis=-1, keepdims=True) + EPS)
        xhat = xv * r
        g = g_ref[...]
        dm_ref[0:1, :] += jnp.sum(dh, axis=0, keepdims=True)
        dm_ref[1:2, :] += jnp.sum(dh * (xhat * g), axis=0, keepdims=True)
        dn = dh * (1.0 + m_ref[1:2, :])
        dgm_ref[0:1, :] += jnp.sum(dn * xhat, axis=0, keepdims=True)
        dxh = dn * g
        gx_ref[...] = dx1_ref[...] + r * (dxh - xhat * jnp.mean(dxh * xhat, axis=-1, keepdims=True))

    rows = pl.BlockSpec((tm, D_MODEL), lambda i: (i, 0))
    half = pl.BlockSpec((tm, width), lambda i: (i, 0))
    per = pl.BlockSpec((None, 8, D_MODEL), lambda i: (i // per_seq, 0, 0))
    return _call(
        body, name="mix_in_bwd", grid=(tokens // tm,),
        in_specs=[half] * 5 + [pl.BlockSpec((D_MODEL, D_IN), lambda i: (0, 0)), rows, rows, per,
                               pl.BlockSpec((1, D_MODEL), lambda i: (0, 0))],
        out_specs=[rows, pl.BlockSpec((8, D_MODEL), lambda i: (0, 0)), per],
        out_shape=[_sds((tokens, D_MODEL), F32), _sds((8, D_MODEL), F32), _sds((n_seq, 8, D_MODEL), F32)],
        compiler_params=_params(("arbitrary",)),
    )(*parts, w_in, x2, dx1, mod8, g_mix)


def _grad_matmul(a, b, name, tmo, tno, tk=512):
    tokens, m = a.shape
    n = b.shape[1]

    def body(a_ref, b_ref, o_ref):
        @pl.when(pl.program_id(2) == 0)
        def _():
            o_ref[...] = jnp.zeros_like(o_ref)

        o_ref[...] += lax.dot_general(a_ref[...], b_ref[...], TN, preferred_element_type=F32)

    return _call(
        body, name=name, grid=(m // tmo, n // tno, tokens // tk),
        in_specs=[pl.BlockSpec((tk, tmo), lambda i, j, k: (k, i)),
                  pl.BlockSpec((tk, tno), lambda i, j, k: (k, j))],
        out_specs=pl.BlockSpec((tmo, tno), lambda i, j, k: (i, j)),
        out_shape=_sds((m, n), F32),
        compiler_params=_params(("parallel", "parallel", "arbitrary")),
    )(a, b)


def _adamw(w, m, v, g, name, n_parts=0, tr=256):
    rows, cols = w.shape
    tr = min(tr, rows)
    c1 = 1.0 - ADAM_B1 ** ADAM_STEP
    c2 = 1.0 - ADAM_B2 ** ADAM_STEP

    def body(w_ref, m_ref, v_ref, g_ref, go_ref, d_ref, mo_ref, vo_ref):
        if n_parts:
            gv = g_ref[0].astype(F32)
            for p in range(1, n_parts):
                gv = gv + g_ref[p].astype(F32)
        else:
            gv = g_ref[...]
        go_ref[...] = gv
        mn = ADAM_B1 * m_ref[...] + (1.0 - ADAM_B1) * gv
        vn = ADAM_B2 * v_ref[...] + (1.0 - ADAM_B2) * (gv * gv)
        mo_ref[...] = mn
        vo_ref[...] = vn
        d_ref[...] = -ADAM_LR * ((mn / c1) / (jnp.sqrt(vn / c2) + ADAM_EPS) + ADAM_WD * w_ref[...])

    blk = pl.BlockSpec((tr, cols), lambda i: (i, 0))
    g_spec = pl.BlockSpec((n_parts, tr, cols), lambda i: (0, i, 0)) if n_parts else blk
    return _call(
        body, name=name, grid=(rows // tr,),
        in_specs=[blk, blk, blk, g_spec], out_specs=[blk] * 4,
        out_shape=[_sds((rows, cols), F32)] * 4,
        compiler_params=_params(("parallel",)),
    )(w, m, v, g)


def _cols_to_full(blocks):
    n, r, c = blocks.shape
    return jnp.transpose(blocks, (1, 0, 2)).reshape(r, n * c)


def _full_to_cols(full, n=N_DEV):
    r, c = full.shape
    return jnp.transpose(full.reshape(r, n, c // n), (1, 0, 2))


def _pad_lanes(v, width):
    return jnp.pad(v, ((0, 0), (0, width - v.shape[1])))


def kernel(x, c, w_ada, b_ada, g_mix, w_in, w_dw, b_dw, g_conv_ln, b_conv_ln, g_q, g_k, w_out, g_ffn, w_gate, w_up, w_down, loss_target, m_w_ada, m_b_ada, m_g_mix, m_w_in, m_w_dw, m_b_dw, m_g_conv_ln, m_b_conv_ln, m_g_q, m_g_k, m_w_out, m_g_ffn, m_w_gate, m_w_up, m_w_down, v_w_ada, v_b_ada, v_g_mix, v_w_in, v_w_dw, v_b_dw, v_g_conv_ln, v_b_conv_ln, v_g_q, v_g_k, v_w_out, v_g_ffn, v_w_gate, v_w_up, v_w_down):
    n_seq, seq, _ = x.shape
    tokens = n_seq * seq
    me = 4 * lax.axis_index("x") + 2 * lax.axis_index("y") + lax.axis_index("c")
    ada_cols = w_ada.shape[2]
    dw_cols = w_dw.shape[2]

    (c_g, w_in_g, w_out_g, w_gate_g, w_up_g, w_down_g, w_dw_g) = _exchange(
        [(c, False), (w_in[0].astype(BF16), False), (w_out[0].astype(BF16), False),
         (w_gate[0].astype(BF16), False), (w_up[0].astype(BF16), False),
         (w_down[0].astype(BF16), False), (w_dw[0], False)], "gather_weights")
    c_all = c_g.reshape(N_DEV * n_seq, D_MODEL)
    w_in_f = _cols_to_full(w_in_g)
    w_out_f = w_out_g.reshape(D_MODEL, D_MODEL)
    w_gate_f = _cols_to_full(w_gate_g)
    w_up_f = _cols_to_full(w_up_g)
    w_down_f = w_down_g.reshape(D_FF, D_MODEL)
    w_dw_f = _cols_to_full(w_dw_g)

    b_cols = lax.dynamic_slice(b_ada, (0, me * ada_cols), (1, ada_cols))
    mod_cols = _ada_fwd(c_all, w_ada[0], b_cols)
    (mod_g,) = _exchange([(mod_cols, False)], "gather_mod")
    mod_mine = lax.dynamic_slice(mod_g, (0, me * n_seq, 0), (N_DEV, n_seq, ada_cols))
    mod = jnp.transpose(mod_mine, (1, 0, 2)).reshape(n_seq, N_MOD, D_MODEL)
    mod8 = jnp.pad(mod, ((0, 0), (0, 8 - N_MOD), (0, 0)))

    x2 = x.reshape(tokens, D_MODEL)
    h1, proj = _mix_in(x2, mod8, g_mix, w_in_f, seq)
    proj3 = proj.reshape(n_seq, seq, D_IN)
    uc3 = _conv_fwd(proj3, w_dw_f, b_dw)
    y_att3, lse3 = _attn_fwd(proj3, g_q, g_k)
    uc2 = uc3.reshape(tokens, D_CONV)
    y_att2 = y_att3.reshape(tokens, D_ATT)
    y_conv, mix, x1, h2 = _mix_out(uc2, y_att2, x2, mod8, g_conv_ln, b_conv_ln, g_ffn, w_out_f, seq)
    gate, up, f = _ffn_fwd(h2, w_gate_f, w_up_f, w_down_f)
    dy, df, sq, dgate_f = _loss_head(f, x1, loss_target.reshape(tokens, D_MODEL), mod8, seq)

    dgate, dup, act, dh2 = _ffn_bwd(df, gate, up, w_gate_f, w_up_f, w_down_f)
    dx1, dmix, dg_ffn, dmod_f = _norm2_bwd(dh2, x1, dy, mix, mod8, g_ffn, seq)
    duc2, do2, dgb_ln = _mix_out_bwd(dmix, uc2, g_conv_ln, b_conv_ln, w_out_f)
    d_a3, d_g3, dw_dw_p, db_dw_p = _conv_bwd(duc2.reshape(n_seq, seq, D_CONV), proj3, w_dw_f)
    d_q3, d_k3, d_v3, dg_qk = _attn_bwd(proj3, do2.reshape(n_seq, seq, D_ATT), y_att3, lse3, g_q, g_k)
    flat = lambda t: t.reshape(tokens, t.shape[-1])
    d_a, d_g, d_q, d_k, d_v = flat(d_a3), flat(d_g3), flat(d_q3), flat(d_k3), flat(d_v3)
    grad_x2, dg_mix, dmod_m = _mix_in_bwd(d_a, d_g, d_q, d_k, d_v, w_in_f, x2, dx1, mod8, g_mix, seq)

    gw_in = jnp.concatenate(
        [_grad_matmul(h1, t, "grad_w_in_%d" % i, D_MODEL, D_CONV) for i, t in enumerate((d_a, d_g, d_q, d_k, d_v))], axis=1)
    gw_out = jnp.concatenate(
        [_grad_matmul(y_conv, dmix, "grad_w_out_conv", D_CONV, D_MODEL),
         _grad_matmul(y_att2, dmix, "grad_w_out_att", D_ATT, D_MODEL)], axis=0)
    gw_gate = _grad_matmul(h2, dgate, "grad_w_gate", D_MODEL, D_FF // 2)
    gw_up = _grad_matmul(h2, dup, "grad_w_up", D_MODEL, D_FF // 2)
    gw_down = _grad_matmul(act, df, "grad_w_down", D_FF // 2, D_MODEL)

    dmod = jnp.concatenate([dmod_m[:, 0], dmod_m[:, 1], dmod_f[:, 2], dmod_f[:, 0], dmod_f[:, 1], dgate_f[:, 0]], axis=1)
    dg_q = dg_qk[0:1, 0:HEAD_DIM] + dg_qk[0:1, HEAD_DIM:]
    dg_k = dg_qk[1:2, 0:HEAD_DIM] + dg_qk[1:2, HEAD_DIM:]
    loss_part = (0.5 / D_MODEL) * jnp.sum(sq[0:1, :], axis=1, keepdims=True)
    small = jnp.concatenate(
        [dg_mix[0:1], dg_ffn[0:1], db_dw_p[0:1], dgb_ln[0:1], dgb_ln[1:2],
         _pad_lanes(dg_q, LANES), _pad_lanes(dg_k, LANES), _pad_lanes(loss_part, LANES)], axis=1)
    n_small = small.shape[1] - LANES

    (p_in, p_out, p_gate, p_up, p_down, dmod_g, small_g, dw_g) = _exchange(
        [(_full_to_cols(gw_in).astype(BF16), True),
         (gw_out.astype(BF16).reshape(N_DEV, D_MODEL // N_DEV, D_MODEL), True),
         (_full_to_cols(gw_gate).astype(BF16), True),
         (_full_to_cols(gw_up).astype(BF16), True),
         (gw_down.astype(BF16).reshape(N_DEV, D_FF // N_DEV, D_MODEL), True),
         (dmod, False), (small, False), (dw_dw_p, False)], "scatter_grads")

    dmod_all = dmod_g.reshape(N_DEV * n_seq, N_MOD * D_MODEL)
    dmod_cols = lax.dynamic_slice(dmod_all, (0, me * ada_cols), (N_DEV * n_seq, ada_cols))
    gw_ada, gb_ada = _ada_bwd(c_all, dmod_cols, dmod_all)

    res = {}
    res["w_ada"] = _adamw(w_ada[0], m_w_ada[0], v_w_ada[0], gw_ada, "adamw_w_ada")
    res["b_ada"] = _adamw(b_ada, m_b_ada, v_b_ada, gb_ada, "adamw_b_ada")
    res["w_in"] = _adamw(w_in[0], m_w_in[0], v_w_in[0], p_in, "adamw_w_in", N_DEV)
    res["w_out"] = _adamw(w_out[0], m_w_out[0], v_w_out[0], p_out, "adamw_w_out", N_DEV)
    res["w_gate"] = _adamw(w_gate[0], m_w_gate[0], v_w_gate[0], p_gate, "adamw_w_gate", N_DEV)
    res["w_up"] = _adamw(w_up[0], m_w_up[0], v_w_up[0], p_up, "adamw_w_up", N_DEV)
    res["w_down"] = _adamw(w_down[0], m_w_down[0], v_w_down[0], p_down, "adamw_w_down", N_DEV, tr=176)
    dw_mine = lax.dynamic_slice(dw_g, (0, 0, me * dw_cols), (N_DEV, CONV_WIDTH, dw_cols))
    res["w_dw"] = _adamw(w_dw[0], m_w_dw[0], v_w_dw[0], dw_mine, "adamw_w_dw", N_DEV)

    small_names = ["g_mix", "g_ffn", "b_dw", "g_conv_ln", "b_conv_ln", "g_q", "g_k"]
    small_w = {"g_mix": (g_mix, m_g_mix, v_g_mix), "g_ffn": (g_ffn, m_g_ffn, v_g_ffn), "b_dw": (b_dw, m_b_dw, v_b_dw),
               "g_conv_ln": (g_conv_ln, m_g_conv_ln, v_g_conv_ln), "b_conv_ln": (b_conv_ln, m_b_conv_ln, v_b_conv_ln),
               "g_q": (g_q, m_g_q, v_g_q), "g_k": (g_k, m_g_k, v_g_k)}
    widths = [max(small_w[n][0].shape[1], LANES) for n in small_names]
    packed = [jnp.concatenate([_pad_lanes(small_w[n][i], wd) for n, wd in zip(small_names, widths)], axis=1) for i in range(3)]
    outs = _adamw(packed[0], packed[1], packed[2], small_g[:, :, :n_small], "adamw_small", N_DEV)
    off = 0
    for n, wd in zip(small_names, widths):
        real = small_w[n][0].shape[1]
        res[n] = tuple(o[:, off:off + real] for o in outs)
        off += wd
    loss = jnp.sum(small_g[:, 0, n_small])

    order = ["w_ada", "b_ada", "g_mix", "w_in", "w_dw", "b_dw", "g_conv_ln", "b_conv_ln", "g_q", "g_k",
             "w_out", "g_ffn", "w_gate", "w_up", "w_down"]
    lead = {"w_ada", "w_in", "w_dw", "w_out", "w_gate", "w_up", "w_down"}
    grads, deltas, new_m, new_v = [], [], [], []
    for n in order:
        g, d, mn, vn = res[n]
        g, d, mn, vn = (t[None] if n in lead else t for t in (g, d, mn, vn))
        grads.append(g)
        deltas.append(d)
        new_m.append(mn)
        new_v.append(vn)
    return (loss, grad_x2.reshape(n_seq, seq, D_MODEL), *grads, *deltas, *new_m, *new_v)
```

```python
import numpy as np
import jax
import jax.numpy as jnp
from jax import lax
from jax.experimental import pallas as pl
from jax.experimental.pallas import tpu as pltpu

F32 = jnp.float32
BF16 = jnp.bfloat16

N_DEV = 8
D_MODEL = 1024
D_CONV = 512
D_ATT = 512
HEAD_DIM = 64
CONV_WIDTH = 31
D_IN = 2 * D_CONV + 3 * D_ATT
D_FF = 2816
N_MOD = 6
EPS = 1e-6
RADIUS = 64
DILATIONS = (1, 4, 16)
Q_BLOCK = 128
LANES = 128
VMEM_LIMIT = 56 * 1024 * 1024

ADAM_LR = 0.001
ADAM_B1 = 0.9
ADAM_B2 = 0.999
ADAM_EPS = 1e-08
ADAM_WD = 0.01
ADAM_STEP = 10

NT = (((1,), (1,)), ((), ()))
TN = (((0,), (0,)), ((), ()))


def _call(body, **kw):
    return pl.pallas_call(body, **kw)


def _params(sem=None, vmem=VMEM_LIMIT):
    return pltpu.CompilerParams(dimension_semantics=sem, vmem_limit_bytes=vmem)


def _sig(x):
    return 1.0 / (1.0 + jnp.exp(-x))


def _sds(shape, dtype):
    return jax.ShapeDtypeStruct(shape, dtype)


def _exchange(items, name):
    n = len(items)
    n_peer = N_DEV - 1

    def body(*refs):
        ins, outs = refs[:n], refs[n:2 * n]
        send_sems, recv_sems, local_sems = refs[2 * n:]
        x, y, c = lax.axis_index("x"), lax.axis_index("y"), lax.axis_index("c")
        me = 4 * x + 2 * y + c

        def src(a, slot):
            return ins[a].at[slot] if items[a][1] else ins[a]

        local = []
        for a in range(n):
            cp = pltpu.make_async_copy(src(a, me), outs[a].at[me], local_sems.at[a])
            cp.start()
            local.append(cp)
        flights = []
        for k in range(1, N_DEV):
            px = 1 - x if k & 4 else x
            py = 1 - y if k & 2 else y
            pc = 1 - c if k & 1 else c
            peer = (px, py, pc)
            pid = 4 * px + 2 * py + pc
            for a in range(n):
                i = a * n_peer + k - 1
                send = pltpu.make_async_remote_copy(
                    src_ref=src(a, pid), dst_ref=outs[a].at[me],
                    send_sem=send_sems.at[i], recv_sem=recv_sems.at[i],
                    device_id=peer, device_id_type=pl.DeviceIdType.MESH)
                send.start()
                recv = pltpu.make_async_remote_copy(
                    src_ref=src(a, pid), dst_ref=outs[a].at[pid],
                    send_sem=send_sems.at[i], recv_sem=recv_sems.at[i],
                    device_id=peer, device_id_type=pl.DeviceIdType.MESH)
                flights.append((send, recv))
        for send, recv in flights:
            send.wait_send()
            recv.wait_recv()
        for cp in local:
            cp.wait()

    out_shape = []
    for arr, scatter in items:
        blk = arr.shape[1:] if scatter else arr.shape
        out_shape.append(_sds((N_DEV,) + tuple(blk), arr.dtype))
    any_spec = pl.BlockSpec(memory_space=pl.ANY)
    return _call(
        body, name=name, out_shape=out_shape,
        in_specs=[any_spec] * n, out_specs=[any_spec] * n,
        scratch_shapes=[pltpu.SemaphoreType.DMA((n * n_peer,)),
                        pltpu.SemaphoreType.DMA((n * n_peer,)),
                        pltpu.SemaphoreType.DMA((n,))],
    )(*[a for a, _ in items])


def _ada_fwd(c_all, w_ada, b_cols):
    def body(c_ref, w_ref, b_ref, o_ref):
        cv = c_ref[...]
        sc = (cv * _sig(cv)).astype(BF16)
        o_ref[...] = jnp.dot(sc, w_ref[...].astype(BF16), preferred_element_type=F32) + b_ref[...]

    return _call(body, name="ada_fwd", out_shape=_sds((c_all.shape[0], w_ada.shape[1]), F32),
                 compiler_params=_params())(c_all, w_ada, b_cols)


def _ada_bwd(c_all, dmod_cols, dmod_all):
    def body(c_ref, dc_ref, da_ref, gw_ref, gb_ref):
        cv = c_ref[...]
        sc = (cv * _sig(cv)).astype(BF16)
        gw_ref[...] = lax.dot_general(sc, dc_ref[...].astype(BF16), TN, preferred_element_type=F32)
        gb_ref[...] = jnp.sum(da_ref[...], axis=0, keepdims=True)

    return _call(body, name="ada_bwd",
                 out_shape=[_sds((c_all.shape[1], dmod_cols.shape[1]), F32), _sds((1, dmod_all.shape[1]), F32)],
                 compiler_params=_params())(c_all, dmod_cols, dmod_all)


def _mix_in(x2, mod8, g_mix, w_in, seq, tm=512):
    tokens = x2.shape[0]
    per_seq = seq // tm

    def body(x_ref, m_ref, g_ref, w_ref, h_ref, p_ref):
        xv = x_ref[...]
        r = lax.rsqrt(jnp.mean(xv * xv, axis=-1, keepdims=True) + EPS)
        h = (xv * r * g_ref[...]) * (1.0 + m_ref[1:2, :]) + m_ref[0:1, :]
        hb = h.astype(BF16)
        h_ref[...] = hb
        p_ref[...] = jnp.dot(hb, w_ref[...], preferred_element_type=F32)

    return _call(
        body, name="mix_in", grid=(tokens // tm,),
        in_specs=[pl.BlockSpec((tm, D_MODEL), lambda i: (i, 0)),
                  pl.BlockSpec((None, 8, D_MODEL), lambda i: (i // per_seq, 0, 0)),
                  pl.BlockSpec((1, D_MODEL), lambda i: (0, 0)),
                  pl.BlockSpec((D_MODEL, D_IN), lambda i: (0, 0))],
        out_specs=[pl.BlockSpec((tm, D_MODEL), lambda i: (i, 0)),
                   pl.BlockSpec((tm, D_IN), lambda i: (i, 0))],
        out_shape=[_sds((tokens, D_MODEL), BF16), _sds((tokens, D_IN), F32)],
        compiler_params=_params(("parallel",)),
    )(x2, mod8, g_mix, w_in)


CONV_ROWS = 32
CONV_HALO = 16


def _fill_shifted(xp, sh, seq):
    for b in range(8):
        sh[b, pl.ds(0, seq + 24), :] = xp[pl.ds(b, seq + 24), :]


def _conv_fwd(proj3, w_dw, b_dw):
    n_seq, seq, _ = proj3.shape
    n_cb = D_CONV // LANES

    def body(a_ref, g_ref, w_ref, b_ref, uc_ref, xp, sh):
        zeros = jnp.zeros((CONV_HALO, LANES), F32)
        xp[pl.ds(0, CONV_HALO), :] = zeros
        xp[pl.ds(CONV_HALO + seq, CONV_HALO), :] = zeros
        xp[pl.ds(CONV_HALO, seq), :] = a_ref[...] * _sig(g_ref[...])
        _fill_shifted(xp, sh, seq)

        def blk(i, carry):
            t0 = pl.multiple_of(i * CONV_ROWS, CONV_ROWS)
            acc = jnp.zeros((CONV_ROWS, LANES), F32)
            for j in range(CONV_WIDTH):
                jj = j + 1
                acc = acc + sh[jj % 8, pl.ds(t0 + 8 * (jj // 8), CONV_ROWS), :] * w_ref[j:j + 1, :]
            uc_ref[pl.ds(t0, CONV_ROWS), :] = acc + b_ref[...]
            return carry

        lax.fori_loop(0, seq // CONV_ROWS, blk, 0)

    return _call(
        body, name="conv_fwd", grid=(n_seq, n_cb),
        in_specs=[pl.BlockSpec((None, seq, LANES), lambda b, cb: (b, 0, cb)),
                  pl.BlockSpec((None, seq, LANES), lambda b, cb: (b, 0, n_cb + cb)),
                  pl.BlockSpec((CONV_WIDTH, LANES), lambda b, cb: (0, cb)),
                  pl.BlockSpec((1, LANES), lambda b, cb: (0, cb))],
        out_specs=pl.BlockSpec((None, seq, LANES), lambda b, cb: (b, 0, cb)),
        out_shape=_sds((n_seq, seq, D_CONV), F32),
        scratch_shapes=[pltpu.VMEM((seq + 2 * CONV_HALO, LANES), F32),
                        pltpu.VMEM((8, seq + 2 * CONV_HALO, LANES), F32)],
        compiler_params=_params(("parallel", "parallel")),
    )(proj3, proj3, w_dw, b_dw)


def _conv_bwd(duc3, proj3, w_dw):
    n_seq, seq, _ = proj3.shape
    n_cb = D_CONV // LANES

    def body(duc_ref, a_ref, g_ref, w_ref, da_ref, dg_ref, dw_ref, db_ref, xp, sh):
        @pl.when(pl.program_id(1) == 0)
        def _():
            dw_ref[...] = jnp.zeros_like(dw_ref)
            db_ref[...] = jnp.zeros_like(db_ref)

        zeros = jnp.zeros((CONV_HALO, LANES), F32)
        xp[pl.ds(0, CONV_HALO), :] = zeros
        xp[pl.ds(CONV_HALO + seq, CONV_HALO), :] = zeros
        xp[pl.ds(CONV_HALO, seq), :] = a_ref[...] * _sig(g_ref[...])
        _fill_shifted(xp, sh, seq)
        for j in range(CONV_WIDTH):
            jj = j + 1

            def wblk(i, acc, jj=jj):
                t0 = pl.multiple_of(i * CONV_ROWS, CONV_ROWS)
                return acc + duc_ref[pl.ds(t0, CONV_ROWS), :] * sh[jj % 8, pl.ds(t0 + 8 * (jj // 8), CONV_ROWS), :]

            acc = lax.fori_loop(0, seq // CONV_ROWS, wblk, jnp.zeros((CONV_ROWS, LANES), F32))
            dw_ref[j:j + 1, :] += jnp.sum(acc, axis=0, keepdims=True)
        db_ref[0:1, :] += jnp.sum(duc_ref[...], axis=0, keepdims=True)
        xp[pl.ds(CONV_HALO, seq), :] = duc_ref[...]
        _fill_shifted(xp, sh, seq)

        def ublk(i, carry):
            t0 = pl.multiple_of(i * CONV_ROWS, CONV_ROWS)
            acc = jnp.zeros((CONV_ROWS, LANES), F32)
            for j in range(CONV_WIDTH):
                jj = CONV_WIDTH - j
                acc = acc + sh[jj % 8, pl.ds(t0 + 8 * (jj // 8), CONV_ROWS), :] * w_ref[j:j + 1, :]
            av = a_ref[pl.ds(t0, CONV_ROWS), :]
            sg = _sig(g_ref[pl.ds(t0, CONV_ROWS), :])
            da_ref[pl.ds(t0, CONV_ROWS), :] = (acc * sg).astype(BF16)
            dg_ref[pl.ds(t0, CONV_ROWS), :] = (acc * av * sg * (1.0 - sg)).astype(BF16)
            return carry

        lax.fori_loop(0, seq // CONV_ROWS, ublk, 0)

    return _call(
        body, name="conv_bwd", grid=(n_cb, n_seq),
        in_specs=[pl.BlockSpec((None, seq, LANES), lambda cb, b: (b, 0, cb)),
                  pl.BlockSpec((None, seq, LANES), lambda cb, b: (b, 0, cb)),
                  pl.BlockSpec((None, seq, LANES), lambda cb, b: (b, 0, n_cb + cb)),
                  pl.BlockSpec((CONV_WIDTH, LANES), lambda cb, b: (0, cb))],
        out_specs=[pl.BlockSpec((None, seq, LANES), lambda cb, b: (b, 0, cb)),
                   pl.BlockSpec((None, seq, LANES), lambda cb, b: (b, 0, cb)),
                   pl.BlockSpec((32, LANES), lambda cb, b: (0, cb)),
                   pl.BlockSpec((8, LANES), lambda cb, b: (0, cb))],
        out_shape=[_sds((n_seq, seq, D_CONV), BF16), _sds((n_seq, seq, D_CONV), BF16),
                   _sds((32, D_CONV), F32), _sds((8, D_CONV), F32)],
        scratch_shapes=[pltpu.VMEM((seq + 2 * CONV_HALO, LANES), F32),
                        pltpu.VMEM((8, seq + 2 * CONV_HALO, LANES), F32)],
        compiler_params=_params(("parallel", "arbitrary")),
    )(duc3, proj3, proj3, w_dw)


MASKED = 1e30
ATT_UNROLL = 2


def _distance_mats(dil, seg_len):
    kw = min(2 * Q_BLOCK, seg_len)
    offsets = (0, -RADIUS, -2 * RADIUS) if kw == 2 * Q_BLOCK else (0,)
    a = np.arange(Q_BLOCK)[:, None]
    b = np.arange(kw)[None, :]
    mats = []
    for off in offsets:
        rel = np.abs(b + off - a)
        mats.append(np.where(rel <= RADIUS, dil * rel, MASKED))
    return jnp.asarray(np.stack(mats).astype(np.float32))


def _alibi_rows():
    s = np.zeros((4, 8, LANES), np.float32)
    for hp in range(4):
        for hl in range(2):
            s[hp, hl, :] = 2.0 ** (-(2 * hp + hl + 1))
    return jnp.asarray(s)


def _window(n, seg_len):
    i0 = pl.multiple_of(n * Q_BLOCK, Q_BLOCK)
    if seg_len <= Q_BLOCK:
        return i0, i0, 0
    per_seg = seg_len // Q_BLOCK
    j = n % per_seg
    seg0 = (n // per_seg) * seg_len
    ks_local = jnp.clip(j * Q_BLOCK - RADIUS, 0, seg_len - 2 * Q_BLOCK)
    ks = pl.multiple_of(seg0 + ks_local, RADIUS)
    var = jnp.where(j == 0, 0, jnp.where(j == per_seg - 1, 2, 1))
    return i0, ks, var


def _first_head(rows):
    return lax.broadcasted_iota(jnp.int32, (rows, LANES), 1) < HEAD_DIM


def _head_mean(x, first):
    s0 = jnp.sum(jnp.where(first, x, 0.0), axis=1, keepdims=True)
    s1 = jnp.sum(jnp.where(first, 0.0, x), axis=1, keepdims=True)
    return jnp.where(first, s0, s1) * (1.0 / HEAD_DIM)


def _permute_rows(dst, src, dil, seq):
    seg = seq // dil
    for r in range(dil):
        rows = pl.ds(r, seg, stride=dil) if dil > 1 else pl.ds(0, seq)
        dst[pl.ds(r * seg, seg), :] = src[rows, :].astype(dst.dtype)


def _permute_rows_by_head(dst, src, dil, seq):
    seg = seq // dil
    first = _first_head(seg)
    for r in range(dil):
        rows = pl.ds(r, seg, stride=dil) if dil > 1 else pl.ds(0, seq)
        val = src[rows, :]
        dst[0, pl.ds(r * seg, seg), :] = jnp.where(first, val, 0.0).astype(dst.dtype)
        dst[1, pl.ds(r * seg, seg), :] = jnp.where(first, 0.0, val).astype(dst.dtype)


def _qk_normalise(q_ref, g2_ref, dst, seq, scale):
    def chunk(ci, carry):
        rows = pl.ds(pl.multiple_of(ci * 256, 256), 256)
        qv = q_ref[rows, :]
        r = lax.rsqrt(_head_mean(qv * qv, _first_head(256)) + EPS)
        dst[rows, :] = qv * r * (g2_ref[...] * scale)
        return carry

    lax.fori_loop(0, seq // 256, chunk, 0)


def _attn_fwd(proj3, g_q2, g_k2):
    n_seq, seq, _ = proj3.shape
    dms = [_distance_mats(d, seq // d) for d in DILATIONS]
    col0 = 2 * D_CONV // LANES
    n_hp = D_ATT // LANES

    def body(q_ref, k_ref, v_ref, gq_ref, gk_ref, sl_ref, dm1, dm4, dm16, y_ref, lse_ref,
             qf, kf, qp, kp, vp, op, mp, lp, o1, o4, o16, m1, m4, m16, l1, l4, l16):
        dm_refs = (dm1, dm4, dm16)
        o_nat, m_nat, l_nat = (o1, o4, o16), (m1, m4, m16), (l1, l4, l16)
        _qk_normalise(q_ref, gq_ref, qf, seq, HEAD_DIM ** -0.5)
        _qk_normalise(k_ref, gk_ref, kf, seq, 1.0)
        slopes = (sl_ref[0:1, 0:1], sl_ref[1:2, 0:1])
        for pi, dil in enumerate(DILATIONS):
            seg = seq // dil
            kw = min(2 * Q_BLOCK, seg)
            _permute_rows_by_head(qp, qf, dil, seq)
            _permute_rows(kp, kf, dil, seq)
            _permute_rows(vp, v_ref, dil, seq)
            o_dst, m_dst, l_dst = (o_nat[pi], m_nat[pi], l_nat[pi]) if dil == 1 else (op, mp, lp)

            def blk(it, carry, seg=seg, kw=kw, pi=pi, o_dst=o_dst, m_dst=m_dst, l_dst=l_dst):
                first = _first_head(Q_BLOCK)
                for sub in range(ATT_UNROLL):
                    i0, ks, var = _window(it * ATT_UNROLL + sub, seg)
                    qrows, krows = pl.ds(i0, Q_BLOCK), pl.ds(ks, kw)
                    k = kp[krows, :]
                    v = vp[krows, :]
                    dm = dm_refs[pi][var]
                    o, m, l = [], [], []
                    for h in range(2):
                        s = lax.dot_general(qp[h, qrows, :], k, NT, preferred_element_type=F32) - slopes[h] * dm
                        m.append(jnp.max(s, axis=1, keepdims=True))
                        p = jnp.exp(s - m[h])
                        l.append(jnp.sum(p, axis=1, keepdims=True))
                        o.append(jnp.dot(p.astype(BF16), v, preferred_element_type=F32))
                    o_dst[qrows, :] = jnp.where(first, o[0], o[1])
                    m_dst[qrows, :] = jnp.where(first, m[0], m[1])
                    l_dst[qrows, :] = jnp.where(first, l[0], l[1])
                return carry

            lax.fori_loop(0, seq // (Q_BLOCK * ATT_UNROLL), blk, 0)
            if dil > 1:
                for r in range(dil):
                    nat, perm = pl.ds(r, seg, stride=dil), pl.ds(r * seg, seg)
                    o_nat[pi][nat, :] = op[perm, :]
                    m_nat[pi][nat, :] = mp[perm, :]
                    l_nat[pi][nat, :] = lp[perm, :]

        def merge(ci, carry):
            rows = pl.ds(pl.multiple_of(ci * 256, 256), 256)
            ms = [m_nat[pi][rows, :] for pi in range(3)]
            m_all = jnp.maximum(jnp.maximum(ms[0], ms[1]), ms[2])
            es = [jnp.exp(m - m_all) for m in ms]
            l_all = l_nat[0][rows, :] * es[0] + l_nat[1][rows, :] * es[1] + l_nat[2][rows, :] * es[2]
            inv = 1.0 / l_all
            o = (o_nat[0][rows, :] * (es[0] * inv) + o_nat[1][rows, :] * (es[1] * inv)
                 + o_nat[2][rows, :] * (es[2] * inv))
            y_ref[rows, :] = o.astype(BF16)
            lse_ref[rows, :] = m_all + jnp.log(l_all)
            return carry

        lax.fori_loop(0, seq // 256, merge, 0)

    def col(off):
        return pl.BlockSpec((None, seq, LANES), lambda b, hp: (b, 0, col0 + off * n_hp + hp))

    def whole(arr):
        return pl.BlockSpec(arr.shape, lambda b, hp: (0,) * arr.ndim)

    rows_f32 = pltpu.VMEM((seq, LANES), F32)
    rows_bf16 = pltpu.VMEM((seq, LANES), BF16)
    return _call(
        body, name="attn_fwd", grid=(n_seq, n_hp),
        in_specs=[col(0), col(1), col(2), whole(g_q2), whole(g_k2),
                  pl.BlockSpec((None, 8, LANES), lambda b, hp: (hp, 0, 0)),
                  whole(dms[0]), whole(dms[1]), whole(dms[2])],
        out_specs=[pl.BlockSpec((None, seq, LANES), lambda b, hp: (b, 0, hp)),
                   pl.BlockSpec((None, seq, LANES), lambda b, hp: (b, 0, hp))],
        out_shape=[_sds((n_seq, seq, D_ATT), BF16), _sds((n_seq, seq, D_ATT), F32)],
        scratch_shapes=[rows_f32, rows_f32, pltpu.VMEM((2, seq, LANES), BF16), rows_bf16, rows_bf16]
        + [rows_f32] * 12,
        compiler_params=_params(("parallel", "parallel")),
    )(proj3, proj3, proj3, g_q2, g_k2, _alibi_rows(), *dms)


def _attn_bwd(proj3, do3, y_att3, lse3, g_q2, g_k2):
    n_seq, seq, _ = proj3.shape
    dms = [_distance_mats(d, seq // d) for d in DILATIONS]
    col0 = 2 * D_CONV // LANES
    n_hp = D_ATT // LANES

    def body(q_ref, k_ref, v_ref, do_ref, o_ref, lse_ref, gq_ref, gk_ref, sl_ref, dm1, dm4, dm16,
             dq_ref, dk_ref, dv_ref, dg_ref,
             qf, kf, qp, dop, kp, vp, sn, sp, dqp, dkp, dvp, dqn, dkn, dvn):
        dm_refs = (dm1, dm4, dm16)

        @pl.when((pl.program_id(0) == 0) & (pl.program_id(1) == 0))
        def _():
            dg_ref[...] = jnp.zeros_like(dg_ref)

        _qk_normalise(q_ref, gq_ref, qf, seq, HEAD_DIM ** -0.5)
        _qk_normalise(k_ref, gk_ref, kf, seq, 1.0)

        def stats(ci, carry):
            rows = pl.ds(pl.multiple_of(ci * 256, 256), 256)
            first = _first_head(256)
            prod = do_ref[rows, :] * o_ref[rows, :].astype(F32)
            sn[rows, 0:1] = lse_ref[rows, 0:1]
            sn[rows, 1:2] = lse_ref[rows, HEAD_DIM:HEAD_DIM + 1]
            sn[rows, 2:3] = jnp.sum(jnp.where(first, prod, 0.0), axis=1, keepdims=True)
            sn[rows, 3:4] = jnp.sum(jnp.where(first, 0.0, prod), axis=1, keepdims=True)
            return carry

        lax.fori_loop(0, seq // 256, stats, 0)
        slopes = (sl_ref[0:1, 0:1], sl_ref[1:2, 0:1])
        half = seq // (Q_BLOCK * ATT_UNROLL)

        for pi, dil in enumerate(DILATIONS):
            seg = seq // dil
            kw = min(2 * Q_BLOCK, seg)
            _permute_rows_by_head(qp, qf, dil, seq)
            _permute_rows_by_head(dop, do_ref, dil, seq)
            _permute_rows(kp, kf, dil, seq)
            _permute_rows(vp, v_ref, dil, seq)
            _permute_rows(sp, sn, dil, seq)
            dkp[...] = jnp.zeros_like(dkp)
            dvp[...] = jnp.zeros_like(dvp)

            def blk(it, carry, seg=seg, kw=kw, pi=pi):
                first = _first_head(Q_BLOCK)
                for sub in range(ATT_UNROLL):
                    i0, ks, var = _window(it + sub * half, seg)
                    qrows, krows = pl.ds(i0, Q_BLOCK), pl.ds(ks, kw)
                    k = kp[krows, :]
                    v = vp[krows, :]
                    dm = dm_refs[pi][var]
                    dq, dk, dv = [], [], []
                    for h in range(2):
                        q = qp[h, qrows, :]
                        do = dop[h, qrows, :]
                        s = lax.dot_general(q, k, NT, preferred_element_type=F32) - slopes[h] * dm
                        p = jnp.exp(s - sp[qrows, h:h + 1])
                        dp = lax.dot_general(do, v, NT, preferred_element_type=F32)
                        ds = (p * (dp - sp[qrows, 2 + h:3 + h])).astype(BF16)
                        dq.append(jnp.dot(ds, k, preferred_element_type=F32))
                        dk.append(lax.dot_general(ds, q, TN, preferred_element_type=F32))
                        dv.append(lax.dot_general(p.astype(BF16), do, TN, preferred_element_type=F32))
                    dqp[qrows, :] = jnp.where(first, dq[0], dq[1])
                    dkp[sub, krows, :] += dk[0] + dk[1]
                    dvp[sub, krows, :] += dv[0] + dv[1]
                return carry

            lax.fori_loop(0, half, blk, 0)
            for r in range(dil):
                nat = pl.ds(r, seg, stride=dil) if dil > 1 else pl.ds(0, seq)
                perm = pl.ds(r * seg, seg)
                dq_new = dqp[perm, :]
                dk_new = dkp[0, perm, :] + dkp[1, perm, :]
                dv_new = dvp[0, perm, :] + dvp[1, perm, :]
                if pi == 0:
                    dqn[nat, :], dkn[nat, :], dvn[nat, :] = dq_new, dk_new, dv_new
                else:
                    dqn[nat, :] += dq_new
                    dkn[nat, :] += dk_new
                    dvn[nat, :] += dv_new

        def finish(ci, carry):
            rows = pl.ds(pl.multiple_of(ci * 256, 256), 256)
            first = _first_head(256)
            for src_ref, g_ref, dn, dst_ref, scale, row in (
                    (q_ref, gq_ref, dqn, dq_ref, HEAD_DIM ** -0.5, 0), (k_ref, gk_ref, dkn, dk_ref, 1.0, 1)):
                xv = src_ref[rows, :]
                r = lax.rsqrt(_head_mean(xv * xv, first) + EPS)
                xhat = xv * r
                d = dn[rows, :] * scale
                dg_ref[row:row + 1, :] += jnp.sum(d * xhat, axis=0, keepdims=True)
                dxh = d * g_ref[...]
                dst_ref[rows, :] = (r * (dxh - xhat * _head_mean(dxh * xhat, first))).astype(BF16)
            dv_ref[rows, :] = dvn[rows, :].astype(BF16)
            return carry

        lax.fori_loop(0, seq // 256, finish, 0)

    def col(off):
        return pl.BlockSpec((None, seq, LANES), lambda b, hp: (b, 0, col0 + off * n_hp + hp))

    def whole(arr):
        return pl.BlockSpec(arr.shape, lambda b, hp: (0,) * arr.ndim)

    att = pl.BlockSpec((None, seq, LANES), lambda b, hp: (b, 0, hp))
    rows_f32 = pltpu.VMEM((seq, LANES), F32)
    rows_bf16 = pltpu.VMEM((seq, LANES), BF16)
    by_head_bf16 = pltpu.VMEM((2, seq, LANES), BF16)
    per_sub_f32 = pltpu.VMEM((ATT_UNROLL, seq, LANES), F32)
    return _call(
        body, name="attn_bwd", grid=(n_seq, n_hp),
        in_specs=[col(0), col(1), col(2), att, att, att, whole(g_q2), whole(g_k2),
                  pl.BlockSpec((None, 8, LANES), lambda b, hp: (hp, 0, 0)),
                  whole(dms[0]), whole(dms[1]), whole(dms[2])],
        out_specs=[att, att, att, pl.BlockSpec((8, LANES), lambda b, hp: (0, 0))],
        out_shape=[_sds((n_seq, seq, D_ATT), BF16)] * 3 + [_sds((8, LANES), F32)],
        scratch_shapes=[rows_f32, rows_f32, by_head_bf16, by_head_bf16, rows_bf16, rows_bf16, rows_f32, rows_f32,
                        rows_f32, per_sub_f32, per_sub_f32, rows_f32, rows_f32, rows_f32],
        compiler_params=_params(("arbitrary", "arbitrary")),
    )(proj3, proj3, proj3, do3, y_att3, lse3, g_q2, g_k2, _alibi_rows(), *dms)


def _mix_out(uc2, y_att2, x2, mod8, g_ln, b_ln, g_ffn, w_out, seq, tm=512):
    tokens = x2.shape[0]
    per_seq = seq // tm

    def body(uc_ref, ya_ref, x_ref, m_ref, gl_ref, bl_ref, gf_ref, w_ref, yc_ref, mix_ref, x1_ref, h2_ref):
        uc = uc_ref[...]
        mu = jnp.mean(uc, axis=-1, keepdims=True)
        cen = uc - mu
        rs = lax.rsqrt(jnp.mean(cen * cen, axis=-1, keepdims=True) + EPS)
        z = cen * rs * gl_ref[...] + bl_ref[...]
        yc = (z * _sig(z)).astype(BF16)
        yc_ref[...] = yc
        mix = (jnp.dot(yc, w_ref[pl.ds(0, D_CONV), :], preferred_element_type=F32)
               + jnp.dot(ya_ref[...], w_ref[pl.ds(D_CONV, D_ATT), :], preferred_element_type=F32))
        mix_ref[...] = mix
        x1 = x_ref[...] + m_ref[2:3, :] * mix
        x1_ref[...] = x1
        r = lax.rsqrt(jnp.mean(x1 * x1, axis=-1, keepdims=True) + EPS)
        h2_ref[...] = ((x1 * r * gf_ref[...]) * (1.0 + m_ref[4:5, :]) + m_ref[3:4, :]).astype(BF16)

    def rows(width):
        return pl.BlockSpec((tm, width), lambda i: (i, 0))

    def vec(width):
        return pl.BlockSpec((1, width), lambda i: (0, 0))

    return _call(
        body, name="mix_out", grid=(tokens // tm,),
        in_specs=[rows(D_CONV), rows(D_ATT), rows(D_MODEL),
                  pl.BlockSpec((None, 8, D_MODEL), lambda i: (i // per_seq, 0, 0)),
                  vec(D_CONV), vec(D_CONV), vec(D_MODEL),
                  pl.BlockSpec((D_MODEL, D_MODEL), lambda i: (0, 0))],
        out_specs=[rows(D_CONV), rows(D_MODEL), rows(D_MODEL), rows(D_MODEL)],
        out_shape=[_sds((tokens, D_CONV), BF16), _sds((tokens, D_MODEL), F32),
                   _sds((tokens, D_MODEL), F32), _sds((tokens, D_MODEL), BF16)],
        compiler_params=_params(("parallel",)),
    )(uc2, y_att2, x2, mod8, g_ln, b_ln, g_ffn, w_out)


def _mix_out_bwd(dmix, uc2, g_ln, b_ln, w_out, tm=512):
    tokens = dmix.shape[0]

    def body(dm_ref, uc_ref, gl_ref, bl_ref, w_ref, duc_ref, do_ref, dgb_ref):
        @pl.when(pl.program_id(0) == 0)
        def _():
            dgb_ref[...] = jnp.zeros_like(dgb_ref)

        dmv = dm_ref[...]
        dyc = lax.dot_general(dmv, w_ref[pl.ds(0, D_CONV), :], NT, preferred_element_type=F32)
        do_ref[...] = lax.dot_general(dmv, w_ref[pl.ds(D_CONV, D_ATT), :], NT, preferred_element_type=F32)
        uc = uc_ref[...]
        mu = jnp.mean(uc, axis=-1, keepdims=True)
        cen = uc - mu
        rs = lax.rsqrt(jnp.mean(cen * cen, axis=-1, keepdims=True) + EPS)
        xh = cen * rs
        z = xh * gl_ref[...] + bl_ref[...]
        sg = _sig(z)
        dz = dyc * (sg * (1.0 + z * (1.0 - sg)))
        dgb_ref[0:1, :] += jnp.sum(dz * xh, axis=0, keepdims=True)
        dgb_ref[1:2, :] += jnp.sum(dz, axis=0, keepdims=True)
        dxh = dz * gl_ref[...]
        duc_ref[...] = rs * (dxh - jnp.mean(dxh, axis=-1, keepdims=True)
                             - xh * jnp.mean(dxh * xh, axis=-1, keepdims=True))

    return _call(
        body, name="mix_out_bwd", grid=(tokens // tm,),
        in_specs=[pl.BlockSpec((tm, D_MODEL), lambda i: (i, 0)),
                  pl.BlockSpec((tm, D_CONV), lambda i: (i, 0)),
                  pl.BlockSpec((1, D_CONV), lambda i: (0, 0)),
                  pl.BlockSpec((1, D_CONV), lambda i: (0, 0)),
                  pl.BlockSpec((D_MODEL, D_MODEL), lambda i: (0, 0))],
        out_specs=[pl.BlockSpec((tm, D_CONV), lambda i: (i, 0)),
                   pl.BlockSpec((tm, D_ATT), lambda i: (i, 0)),
                   pl.BlockSpec((8, D_CONV), lambda i: (0, 0))],
        out_shape=[_sds((tokens, D_CONV), F32), _sds((tokens, D_ATT), F32), _sds((8, D_CONV), F32)],
        compiler_params=_params(("arbitrary",)),
    )(dmix, uc2, g_ln, b_ln, w_out)


FF_TILE = 256


def _ffn_fwd(h2, w_gate, w_up, w_down, tm=1024):
    tokens = h2.shape[0]

    def body(h_ref, wg_ref, wu_ref, wd_ref, gate_ref, up_ref, f_ref):
        @pl.when(pl.program_id(1) == 0)
        def _():
            f_ref[...] = jnp.zeros_like(f_ref)

        hv = h_ref[...]
        gate = jnp.dot(hv, wg_ref[...], preferred_element_type=F32)
        up = jnp.dot(hv, wu_ref[...], preferred_element_type=F32)
        gate_ref[...] = gate
        up_ref[...] = up
        act = (gate * _sig(gate) * up).astype(BF16)
        f_ref[...] += jnp.dot(act, wd_ref[...], preferred_element_type=F32)

    return _call(
        body, name="ffn_fwd", grid=(tokens // tm, D_FF // FF_TILE),
        in_specs=[pl.BlockSpec((tm, D_MODEL), lambda i, j: (i, 0)),
                  pl.BlockSpec((D_MODEL, FF_TILE), lambda i, j: (0, j)),
                  pl.BlockSpec((D_MODEL, FF_TILE), lambda i, j: (0, j)),
                  pl.BlockSpec((FF_TILE, D_MODEL), lambda i, j: (j, 0))],
        out_specs=[pl.BlockSpec((tm, FF_TILE), lambda i, j: (i, j)),
                   pl.BlockSpec((tm, FF_TILE), lambda i, j: (i, j)),
                   pl.BlockSpec((tm, D_MODEL), lambda i, j: (i, 0))],
        out_shape=[_sds((tokens, D_FF), F32), _sds((tokens, D_FF), F32), _sds((tokens, D_MODEL), F32)],
        compiler_params=_params(("parallel", "arbitrary")),
    )(h2, w_gate, w_up, w_down)


def _ffn_bwd(df, gate, up, w_gate, w_up, w_down, tm=1024):
    tokens = df.shape[0]

    def body(df_ref, gate_ref, up_ref, wg_ref, wu_ref, wd_ref, dgate_ref, dup_ref, act_ref, dh_ref):
        @pl.when(pl.program_id(1) == 0)
        def _():
            dh_ref[...] = jnp.zeros_like(dh_ref)

        dact = lax.dot_general(df_ref[...], wd_ref[...], NT, preferred_element_type=F32)
        gate = gate_ref[...]
        up = up_ref[...]
        sg = _sig(gate)
        silu = gate * sg
        act_ref[...] = (silu * up).astype(BF16)
        dup = (dact * silu).astype(BF16)
        dgate = (dact * up * (sg * (1.0 + gate * (1.0 - sg)))).astype(BF16)
        dup_ref[...] = dup
        dgate_ref[...] = dgate
        dh_ref[...] += (lax.dot_general(dgate, wg_ref[...], NT, preferred_element_type=F32)
                        + lax.dot_general(dup, wu_ref[...], NT, preferred_element_type=F32))

    tile = pl.BlockSpec((tm, FF_TILE), lambda i, j: (i, j))
    return _call(
        body, name="ffn_bwd", grid=(tokens // tm, D_FF // FF_TILE),
        in_specs=[pl.BlockSpec((tm, D_MODEL), lambda i, j: (i, 0)), tile, tile,
                  pl.BlockSpec((D_MODEL, FF_TILE), lambda i, j: (0, j)),
                  pl.BlockSpec((D_MODEL, FF_TILE), lambda i, j: (0, j)),
                  pl.BlockSpec((FF_TILE, D_MODEL), lambda i, j: (j, 0))],
        out_specs=[tile, tile, tile, pl.BlockSpec((tm, D_MODEL), lambda i, j: (i, 0))],
        out_shape=[_sds((tokens, D_FF), BF16)] * 3 + [_sds((tokens, D_MODEL), F32)],
        compiler_params=_params(("parallel", "arbitrary")),
    )(df, gate, up, w_gate, w_up, w_down)


def _loss_head(f, x1, target, mod8, seq, tm=512):
    tokens = f.shape[0]
    per_seq = seq // tm
    n_seq = tokens // seq

    def body(f_ref, x1_ref, t_ref, m_ref, dy_ref, df_ref, sq_ref, dgf_ref):
        i = pl.program_id(0)

        @pl.when(i == 0)
        def _():
            sq_ref[...] = jnp.zeros_like(sq_ref)

        @pl.when(i % per_seq == 0)
        def _():
            dgf_ref[...] = jnp.zeros_like(dgf_ref)

        fv = f_ref[...]
        gate = m_ref[5:6, :]
        diff = x1_ref[...] + gate * fv - t_ref[...]
        sq_ref[0:1, :] += jnp.sum(diff * diff, axis=0, keepdims=True)
        dy = diff * (1.0 / D_MODEL)
        dy_ref[...] = dy
        df_ref[...] = (gate * dy).astype(BF16)
        dgf_ref[0:1, :] += jnp.sum(dy * fv, axis=0, keepdims=True)

    rows = pl.BlockSpec((tm, D_MODEL), lambda i: (i, 0))
    per = pl.BlockSpec((None, 8, D_MODEL), lambda i: (i // per_seq, 0, 0))
    return _call(
        body, name="loss_head", grid=(tokens // tm,),
        in_specs=[rows, rows, rows, per],
        out_specs=[rows, rows, pl.BlockSpec((8, D_MODEL), lambda i: (0, 0)), per],
        out_shape=[_sds((tokens, D_MODEL), F32), _sds((tokens, D_MODEL), BF16),
                   _sds((8, D_MODEL), F32), _sds((n_seq, 8, D_MODEL), F32)],
        compiler_params=_params(("arbitrary",)),
    )(f, x1, target, mod8)


def _norm2_bwd(dh2, x1, dy, mix, mod8, g_ffn, seq, tm=512):
    tokens = dh2.shape[0]
    per_seq = seq // tm
    n_seq = tokens // seq

    def body(dh_ref, x1_ref, dy_ref, mix_ref, m_ref, g_ref, dx1_ref, dmix_ref, dg_ref, dm_ref):
        i = pl.program_id(0)

        @pl.when(i == 0)
        def _():
            dg_ref[...] = jnp.zeros_like(dg_ref)

        @pl.when(i % per_seq == 0)
        def _():
            dm_ref[...] = jnp.zeros_like(dm_ref)

        dh = dh_ref[...]
        x1 = x1_ref[...]
        r = lax.rsqrt(jnp.mean(x1 * x1, axis=-1, keepdims=True) + EPS)
        xhat = x1 * r
        g = g_ref[...]
        dm_ref[0:1, :] += jnp.sum(dh, axis=0, keepdims=True)
        dm_ref[1:2, :] += jnp.sum(dh * (xhat * g), axis=0, keepdims=True)
        dn = dh * (1.0 + m_ref[4:5, :])
        dg_ref[0:1, :] += jnp.sum(dn * xhat, axis=0, keepdims=True)
        dxh = dn * g
        dx1 = dy_ref[...] + r * (dxh - xhat * jnp.mean(dxh * xhat, axis=-1, keepdims=True))
        dx1_ref[...] = dx1
        dm_ref[2:3, :] += jnp.sum(dx1 * mix_ref[...], axis=0, keepdims=True)
        dmix_ref[...] = (m_ref[2:3, :] * dx1).astype(BF16)

    rows = pl.BlockSpec((tm, D_MODEL), lambda i: (i, 0))
    per = pl.BlockSpec((None, 8, D_MODEL), lambda i: (i // per_seq, 0, 0))
    return _call(
        body, name="norm2_bwd", grid=(tokens // tm,),
        in_specs=[rows, rows, rows, rows, per, pl.BlockSpec((1, D_MODEL), lambda i: (0, 0))],
        out_specs=[rows, rows, pl.BlockSpec((8, D_MODEL), lambda i: (0, 0)), per],
        out_shape=[_sds((tokens, D_MODEL), F32), _sds((tokens, D_MODEL), BF16),
                   _sds((8, D_MODEL), F32), _sds((n_seq, 8, D_MODEL), F32)],
        compiler_params=_params(("arbitrary",)),
    )(dh2, x1, dy, mix, mod8, g_ffn)


def _mix_in_bwd(d_a, d_g, d_q, d_k, d_v, w_in, x2, dx1, mod8, g_mix, seq, tm=512):
    tokens = x2.shape[0]
    per_seq = seq // tm
    n_seq = tokens // seq
    parts = (d_a, d_g, d_q, d_k, d_v)
    width = D_CONV

    def body(da_ref, dg_ref, dq_ref, dk_ref, dv_ref, w_ref, x_ref, dx1_ref, m_ref, g_ref, gx_ref, dgm_ref, dm_ref):
        i = pl.program_id(0)

        @pl.when(i == 0)
        def _():
            dgm_ref[...] = jnp.zeros_like(dgm_ref)

        @pl.when(i % per_seq == 0)
        def _():
            dm_ref[...] = jnp.zeros_like(dm_ref)

        dh = jnp.zeros((tm, D_MODEL), F32)
        for n, ref in enumerate((da_ref, dg_ref, dq_ref, dk_ref, dv_ref)):
            dh = dh + lax.dot_general(ref[...], w_ref[:, pl.ds(n * width, width)], NT, preferred_element_type=F32)
        xv = x_ref[...]
        r = lax.rsqrt(jnp.mean(xv * xv, axis=-1, keepdims=True) + EPS)
        xhat = xv * r
        g = g_ref[...]
        dm_ref[0:1, :] += jnp.sum(dh, axis=0, keepdims=True)
        dm_ref[1:2, :] += jnp.sum(dh * (xhat * g), axis=0, keepdims=True)
        dn = dh * (1.0 + m_ref[1:2, :])
        dgm_ref[0:1, :] += jnp.sum(dn * xhat, axis=0, keepdims=True)
        dxh = dn * g
        gx_ref[...] = dx1_ref[...] + r * (dxh - xhat * jnp.mean(dxh * xhat, axis=-1, keepdims=True))

    rows = pl.BlockSpec((tm, D_MODEL), lambda i: (i, 0))
    half = pl.BlockSpec((tm, width), lambda i: (i, 0))
    per = pl.BlockSpec((None, 8, D_MODEL), lambda i: (i // per_seq, 0, 0))
    return _call(
        body, name="mix_in_bwd", grid=(tokens // tm,),
        in_specs=[half] * 5 + [pl.BlockSpec((D_MODEL, D_IN), lambda i: (0, 0)), rows, rows, per,
                               pl.BlockSpec((1, D_MODEL), lambda i: (0, 0))],
        out_specs=[rows, pl.BlockSpec((8, D_MODEL), lambda i: (0, 0)), per],
        out_shape=[_sds((tokens, D_MODEL), F32), _sds((8, D_MODEL), F32), _sds((n_seq, 8, D_MODEL), F32)],
        compiler_params=_params(("arbitrary",)),
    )(*parts, w_in, x2, dx1, mod8, g_mix)


def _grad_matmul(a, b, name, tmo, tno, tk=512):
    tokens, m = a.shape
    n = b.shape[1]

    def body(a_ref, b_ref, o_ref):
        @pl.when(pl.program_id(2) == 0)
        def _():
            o_ref[...] = jnp.zeros_like(o_ref)

        o_ref[...] += lax.dot_general(a_ref[...], b_ref[...], TN, preferred_element_type=F32)

    return _call(
        body, name=name, grid=(m // tmo, n // tno, tokens // tk),
        in_specs=[pl.BlockSpec((tk, tmo), lambda i, j, k: (k, i)),
                  pl.BlockSpec((tk, tno), lambda i, j, k: (k, j))],
        out_specs=pl.BlockSpec((tmo, tno), lambda i, j, k: (i, j)),
        out_shape=_sds((m, n), F32),
        compiler_params=_params(("parallel", "parallel", "arbitrary")),
    )(a, b)


def _adamw(w, m, v, g, name, n_parts=0, tr=256):
    rows, cols = w.shape
    tr = min(tr, rows)
    c1 = 1.0 - ADAM_B1 ** ADAM_STEP
    c2 = 1.0 - ADAM_B2 ** ADAM_STEP

    def body(w_ref, m_ref, v_ref, g_ref, go_ref, d_ref, mo_ref, vo_ref):
        if n_parts:
            gv = g_ref[0].astype(F32)
            for p in range(1, n_parts):
                gv = gv + g_ref[p].astype(F32)
        else:
            gv = g_ref[...]
        go_ref[...] = gv
        mn = ADAM_B1 * m_ref[...] + (1.0 - ADAM_B1) * gv
        vn = ADAM_B2 * v_ref[...] + (1.0 - ADAM_B2) * (gv * gv)
        mo_ref[...] = mn
        vo_ref[...] = vn
        d_ref[...] = -ADAM_LR * ((mn / c1) / (jnp.sqrt(vn / c2) + ADAM_EPS) + ADAM_WD * w_ref[...])

    blk = pl.BlockSpec((tr, cols), lambda i: (i, 0))
    g_spec = pl.BlockSpec((n_parts, tr, cols), lambda i: (0, i, 0)) if n_parts else blk
    return _call(
        body, name=name, grid=(rows // tr,),
        in_specs=[blk, blk, blk, g_spec], out_specs=[blk] * 4,
        out_shape=[_sds((rows, cols), F32)] * 4,
        compiler_params=_params(("parallel",)),
    )(w, m, v, g)


def _cols_to_full(blocks):
    n, r, c = blocks.shape
    return jnp.transpose(blocks, (1, 0, 2)).reshape(r, n * c)


def _full_to_cols(full, n=N_DEV):
    r, c = full.shape
    return jnp.transpose(full.reshape(r, n, c // n), (1, 0, 2))


def _pad_lanes(v, width):
    return jnp.pad(v, ((0, 0), (0, width - v.shape[1])))


def kernel(x, c, w_ada, b_ada, g_mix, w_in, w_dw, b_dw, g_conv_ln, b_conv_ln, g_q, g_k, w_out, g_ffn, w_gate, w_up, w_down, loss_target, m_w_ada, m_b_ada, m_g_mix, m_w_in, m_w_dw, m_b_dw, m_g_conv_ln, m_b_conv_ln, m_g_q, m_g_k, m_w_out, m_g_ffn, m_w_gate, m_w_up, m_w_down, v_w_ada, v_b_ada, v_g_mix, v_w_in, v_w_dw, v_b_dw, v_g_conv_ln, v_b_conv_ln, v_g_q, v_g_k, v_w_out, v_g_ffn, v_w_gate, v_w_up, v_w_down):
    n_seq, seq, _ = x.shape
    tokens = n_seq * seq
    me = 4 * lax.axis_index("x") + 2 * lax.axis_index("y") + lax.axis_index("c")
    ada_cols = w_ada.shape[2]
    dw_cols = w_dw.shape[2]

    (c_g, w_in_g, w_out_g, w_gate_g, w_up_g, w_down_g, w_dw_g) = _exchange(
        [(c, False), (w_in[0].astype(BF16), False), (w_out[0].astype(BF16), False),
         (w_gate[0].astype(BF16), False), (w_up[0].astype(BF16), False),
         (w_down[0].astype(BF16), False), (w_dw[0], False)], "gather_weights")
    c_all = c_g.reshape(N_DEV * n_seq, D_MODEL)
    w_in_f = _cols_to_full(w_in_g)
    w_out_f = w_out_g.reshape(D_MODEL, D_MODEL)
    w_gate_f = _cols_to_full(w_gate_g)
    w_up_f = _cols_to_full(w_up_g)
    w_down_f = w_down_g.reshape(D_FF, D_MODEL)
    w_dw_f = _cols_to_full(w_dw_g)

    b_cols = lax.dynamic_slice(b_ada, (0, me * ada_cols), (1, ada_cols))
    mod_cols = _ada_fwd(c_all, w_ada[0], b_cols)
    (mod_g,) = _exchange([(mod_cols, False)], "gather_mod")
    mod_mine = lax.dynamic_slice(mod_g, (0, me * n_seq, 0), (N_DEV, n_seq, ada_cols))
    mod = jnp.transpose(mod_mine, (1, 0, 2)).reshape(n_seq, N_MOD, D_MODEL)
    mod8 = jnp.pad(mod, ((0, 0), (0, 8 - N_MOD), (0, 0)))

    x2 = x.reshape(tokens, D_MODEL)
    h1, proj = _mix_in(x2, mod8, g_mix, w_in_f, seq)
    proj3 = proj.reshape(n_seq, seq, D_IN)
    uc3 = _conv_fwd(proj3, w_dw_f, b_dw)
    g_q2, g_k2 = jnp.tile(g_q, (1, 2)), jnp.tile(g_k, (1, 2))
    y_att3, lse3 = _attn_fwd(proj3, g_q2, g_k2)
    uc2 = uc3.reshape(tokens, D_CONV)
    y_att2 = y_att3.reshape(tokens, D_ATT)
    y_conv, mix, x1, h2 = _mix_out(uc2, y_att2, x2, mod8, g_conv_ln, b_conv_ln, g_ffn, w_out_f, seq)
    gate, up, f = _ffn_fwd(h2, w_gate_f, w_up_f, w_down_f)
    dy, df, sq, dgate_f = _loss_head(f, x1, loss_target.reshape(tokens, D_MODEL), mod8, seq)

    dgate, dup, act, dh2 = _ffn_bwd(df, gate, up, w_gate_f, w_up_f, w_down_f)
    dx1, dmix, dg_ffn, dmod_f = _norm2_bwd(dh2, x1, dy, mix, mod8, g_ffn, seq)
    duc2, do2, dgb_ln = _mix_out_bwd(dmix, uc2, g_conv_ln, b_conv_ln, w_out_f)
    d_a3, d_g3, dw_dw_p, db_dw_p = _conv_bwd(duc2.reshape(n_seq, seq, D_CONV), proj3, w_dw_f)
    d_q3, d_k3, d_v3, dg_qk = _attn_bwd(proj3, do2.reshape(n_seq, seq, D_ATT), y_att3, lse3, g_q2, g_k2)
    flat = lambda t: t.reshape(tokens, t.shape[-1])
    d_a, d_g, d_q, d_k, d_v = flat(d_a3), flat(d_g3), flat(d_q3), flat(d_k3), flat(d_v3)
    grad_x2, dg_mix, dmod_m = _mix_in_bwd(d_a, d_g, d_q, d_k, d_v, w_in_f, x2, dx1, mod8, g_mix, seq)

    gw_in = jnp.concatenate(
        [_grad_matmul(h1, t, "grad_w_in_%d" % i, D_MODEL, D_CONV) for i, t in enumerate((d_a, d_g, d_q, d_k, d_v))], axis=1)
    gw_out = jnp.concatenate(
        [_grad_matmul(y_conv, dmix, "grad_w_out_conv", D_CONV, D_MODEL),
         _grad_matmul(y_att2, dmix, "grad_w_out_att", D_ATT, D_MODEL)], axis=0)
    gw_gate = _grad_matmul(h2, dgate, "grad_w_gate", D_MODEL, D_FF // 2)
    gw_up = _grad_matmul(h2, dup, "grad_w_up", D_MODEL, D_FF // 2)
    gw_down = _grad_matmul(act, df, "grad_w_down", D_FF // 2, D_MODEL)

    dmod = jnp.concatenate([dmod_m[:, 0], dmod_m[:, 1], dmod_f[:, 2], dmod_f[:, 0], dmod_f[:, 1], dgate_f[:, 0]], axis=1)
    dg_q = dg_qk[0:1, 0:HEAD_DIM] + dg_qk[0:1, HEAD_DIM:]
    dg_k = dg_qk[1:2, 0:HEAD_DIM] + dg_qk[1:2, HEAD_DIM:]
    loss_part = (0.5 / D_MODEL) * jnp.sum(sq[0:1, :], axis=1, keepdims=True)
    small = jnp.concatenate(
        [dg_mix[0:1], dg_ffn[0:1], db_dw_p[0:1], dgb_ln[0:1], dgb_ln[1:2],
         _pad_lanes(dg_q, LANES), _pad_lanes(dg_k, LANES), _pad_lanes(loss_part, LANES)], axis=1)
    n_small = small.shape[1] - LANES

    (p_in, p_out, p_gate, p_up, p_down, dmod_g, small_g, dw_g) = _exchange(
        [(_full_to_cols(gw_in).astype(BF16), True),
         (gw_out.astype(BF16).reshape(N_DEV, D_MODEL // N_DEV, D_MODEL), True),
         (_full_to_cols(gw_gate).astype(BF16), True),
         (_full_to_cols(gw_up).astype(BF16), True),
         (gw_down.astype(BF16).reshape(N_DEV, D_FF // N_DEV, D_MODEL), True),
         (dmod, False), (small, False), (dw_dw_p, False)], "scatter_grads")

    dmod_all = dmod_g.reshape(N_DEV * n_seq, N_MOD * D_MODEL)
    dmod_cols = lax.dynamic_slice(dmod_all, (0, me * ada_cols), (N_DEV * n_seq, ada_cols))
    gw_ada, gb_ada = _ada_bwd(c_all, dmod_cols, dmod_all)

    res = {}
    res["w_ada"] = _adamw(w_ada[0], m_w_ada[0], v_w_ada[0], gw_ada, "adamw_w_ada")
    res["b_ada"] = _adamw(b_ada, m_b_ada, v_b_ada, gb_ada, "adamw_b_ada")
    res["w_in"] = _adamw(w_in[0], m_w_in[0], v_w_in[0], p_in, "adamw_w_in", N_DEV)
    res["w_out"] = _adamw(w_out[0], m_w_out[0], v_w_out[0], p_out, "adamw_w_out", N_DEV)
    res["w_gate"] = _adamw(w_gate[0], m_w_gate[0], v_w_gate[0], p_gate, "adamw_w_gate", N_DEV)
    res["w_up"] = _adamw(w_up[0], m_w_up[0], v_w_up[0], p_up, "adamw_w_up", N_DEV)
    res["w_down"] = _adamw(w_down[0], m_w_down[0], v_w_down[0], p_down, "adamw_w_down", N_DEV, tr=176)
    dw_mine = lax.dynamic_slice(dw_g, (0, 0, me * dw_cols), (N_DEV, CONV_WIDTH, dw_cols))
    res["w_dw"] = _adamw(w_dw[0], m_w_dw[0], v_w_dw[0], dw_mine, "adamw_w_dw", N_DEV)

    small_names = ["g_mix", "g_ffn", "b_dw", "g_conv_ln", "b_conv_ln", "g_q", "g_k"]
    small_w = {"g_mix": (g_mix, m_g_mix, v_g_mix), "g_ffn": (g_ffn, m_g_ffn, v_g_ffn), "b_dw": (b_dw, m_b_dw, v_b_dw),
               "g_conv_ln": (g_conv_ln, m_g_conv_ln, v_g_conv_ln), "b_conv_ln": (b_conv_ln, m_b_conv_ln, v_b_conv_ln),
               "g_q": (g_q, m_g_q, v_g_q), "g_k": (g_k, m_g_k, v_g_k)}
    widths = [max(small_w[n][0].shape[1], LANES) for n in small_names]
    packed = [jnp.concatenate([_pad_lanes(small_w[n][i], wd) for n, wd in zip(small_names, widths)], axis=1) for i in range(3)]
    outs = _adamw(packed[0], packed[1], packed[2], small_g[:, :, :n_small], "adamw_small", N_DEV)
    off = 0
    for n, wd in zip(small_names, widths):
        real = small_w[n][0].shape[1]
        res[n] = tuple(o[:, off:off + real] for o in outs)
        off += wd
    loss = jnp.sum(small_g[:, 0, n_small])

    order = ["w_ada", "b_ada", "g_mix", "w_in", "w_dw", "b_dw", "g_conv_ln", "b_conv_ln", "g_q", "g_k",
             "w_out", "g_ffn", "w_gate", "w_up", "w_down"]
    lead = {"w_ada", "w_in", "w_dw", "w_out", "w_gate", "w_up", "w_down"}
    grads, deltas, new_m, new_v = [], [], [], []
    for n in order:
        g, d, mn, vn = res[n]
        g, d, mn, vn = (t[None] if n in lead else t for t in (g, d, mn, vn))
        grads.append(g)
        deltas.append(d)
        new_m.append(mn)
        new_v.append(vn)
    return (loss, grad_x2.reshape(n_seq, seq, D_MODEL), *grads, *deltas, *new_m, *new_v)
```

```python
import numpy as np
import jax
import jax.numpy as jnp
from jax import lax
from jax.experimental import pallas as pl
from jax.experimental.pallas import tpu as pltpu

F32 = jnp.float32
BF16 = jnp.bfloat16

N_DEV = 8
D_MODEL = 1024
D_CONV = 512
D_ATT = 512
HEAD_DIM = 64
CONV_WIDTH = 31
D_IN = 2 * D_CONV + 3 * D_ATT
D_FF = 2816
N_MOD = 6
EPS = 1e-6
RADIUS = 64
DILATIONS = (1, 4, 16)
Q_BLOCK = 128
LANES = 128
VMEM_LIMIT = 56 * 1024 * 1024

ADAM_LR = 0.001
ADAM_B1 = 0.9
ADAM_B2 = 0.999
ADAM_EPS = 1e-08
ADAM_WD = 0.01
ADAM_STEP = 10

NT = (((1,), (1,)), ((), ()))
TN = (((0,), (0,)), ((), ()))


def _call(body, **kw):
    return pl.pallas_call(body, **kw)


def _params(sem=None, vmem=VMEM_LIMIT):
    return pltpu.CompilerParams(dimension_semantics=sem, vmem_limit_bytes=vmem)


def _sig(x):
    return 1.0 / (1.0 + jnp.exp(-x))


def _sds(shape, dtype):
    return jax.ShapeDtypeStruct(shape, dtype)


N_PEER = N_DEV - 1
ANY_SPEC = pl.BlockSpec(memory_space=pl.ANY)


def _exchange_copies(scatter, ins, outs, *sems):
    n = len(ins)
    if n == 0:
        return [], []
    send_sems, recv_sems, local_sems = sems
    x, y, c = lax.axis_index("x"), lax.axis_index("y"), lax.axis_index("c")
    me = 4 * x + 2 * y + c

    def src(a, slot):
        return ins[a].at[slot] if scatter[a] else ins[a]

    local = [pltpu.make_async_copy(src(a, me), outs[a].at[me], local_sems.at[a]) for a in range(n)]
    flights = []
    for k in range(1, N_DEV):
        px = 1 - x if k & 4 else x
        py = 1 - y if k & 2 else y
        pc = 1 - c if k & 1 else c
        pid = 4 * px + 2 * py + pc
        for a in range(n):
            i = a * N_PEER + k - 1
            send, recv = (pltpu.make_async_remote_copy(
                src_ref=src(a, pid), dst_ref=outs[a].at[slot],
                send_sem=send_sems.at[i], recv_sem=recv_sems.at[i],
                device_id=(px, py, pc), device_id_type=pl.DeviceIdType.MESH) for slot in (me, pid))
            flights.append((send, recv))
    return local, flights


def _exchange_start(*args):
    local, flights = _exchange_copies(*args)
    for cp in local:
        cp.start()
    for send, _ in flights:
        send.start()


def _exchange_wait(*args):
    local, flights = _exchange_copies(*args)
    for send, recv in flights:
        send.wait_send()
        recv.wait_recv()
    for cp in local:
        cp.wait()


def _exchange_shapes(items):
    return [_sds((N_DEV,) + tuple(arr.shape[1:] if scatter else arr.shape), arr.dtype) for arr, scatter in items]


def _exchange_sems(n):
    if n == 0:
        return []
    return [pltpu.SemaphoreType.DMA((n * N_PEER,)), pltpu.SemaphoreType.DMA((n * N_PEER,)),
            pltpu.SemaphoreType.DMA((n,))]


def _exchange(items, name):
    n = len(items)
    scatter = [s for _, s in items]

    def body(*refs):
        args = (scatter, refs[:n], refs[n:2 * n]) + tuple(refs[2 * n:])
        _exchange_start(*args)
        _exchange_wait(*args)

    return _call(
        body, name=name, out_shape=_exchange_shapes(items),
        in_specs=[ANY_SPEC] * n, out_specs=[ANY_SPEC] * n, scratch_shapes=_exchange_sems(n),
    )(*[a for a, _ in items])


def _ada_fwd(c_all, w_ada, b_cols):
    def body(c_ref, w_ref, b_ref, o_ref):
        cv = c_ref[...]
        sc = (cv * _sig(cv)).astype(BF16)
        o_ref[...] = jnp.dot(sc, w_ref[...].astype(BF16), preferred_element_type=F32) + b_ref[...]

    return _call(body, name="ada_fwd", out_shape=_sds((c_all.shape[0], w_ada.shape[1]), F32),
                 compiler_params=_params())(c_all, w_ada, b_cols)


def _ada_bwd(c_all, dmod_cols, dmod_all):
    def body(c_ref, dc_ref, da_ref, gw_ref, gb_ref):
        cv = c_ref[...]
        sc = (cv * _sig(cv)).astype(BF16)
        gw_ref[...] = lax.dot_general(sc, dc_ref[...].astype(BF16), TN, preferred_element_type=F32)
        gb_ref[...] = jnp.sum(da_ref[...], axis=0, keepdims=True)

    return _call(body, name="ada_bwd",
                 out_shape=[_sds((c_all.shape[1], dmod_cols.shape[1]), F32), _sds((1, dmod_all.shape[1]), F32)],
                 compiler_params=_params())(c_all, dmod_cols, dmod_all)


def _mix_in(x2, mod8, g_mix, w_in, seq, tm=512):
    tokens = x2.shape[0]
    per_seq = seq // tm

    def body(x_ref, m_ref, g_ref, w_ref, h_ref, p_ref):
        xv = x_ref[...]
        r = lax.rsqrt(jnp.mean(xv * xv, axis=-1, keepdims=True) + EPS)
        h = (xv * r * g_ref[...]) * (1.0 + m_ref[1:2, :]) + m_ref[0:1, :]
        hb = h.astype(BF16)
        h_ref[...] = hb
        p_ref[...] = jnp.dot(hb, w_ref[...], preferred_element_type=F32)

    return _call(
        body, name="mix_in", grid=(tokens // tm,),
        in_specs=[pl.BlockSpec((tm, D_MODEL), lambda i: (i, 0)),
                  pl.BlockSpec((None, 8, D_MODEL), lambda i: (i // per_seq, 0, 0)),
                  pl.BlockSpec((1, D_MODEL), lambda i: (0, 0)),
                  pl.BlockSpec((D_MODEL, D_IN), lambda i: (0, 0))],
        out_specs=[pl.BlockSpec((tm, D_MODEL), lambda i: (i, 0)),
                   pl.BlockSpec((tm, D_IN), lambda i: (i, 0))],
        out_shape=[_sds((tokens, D_MODEL), BF16), _sds((tokens, D_IN), F32)],
        compiler_params=_params(("parallel",)),
    )(x2, mod8, g_mix, w_in)


CONV_ROWS = 32
CONV_HALO = 16


def _fill_shifted(xp, sh, seq):
    for b in range(8):
        sh[b, pl.ds(0, seq + 24), :] = xp[pl.ds(b, seq + 24), :]


def _conv_fwd(proj3, w_dw, b_dw):
    n_seq, seq, _ = proj3.shape
    n_cb = D_CONV // LANES

    def body(a_ref, g_ref, w_ref, b_ref, uc_ref, xp, sh):
        zeros = jnp.zeros((CONV_HALO, LANES), F32)
        xp[pl.ds(0, CONV_HALO), :] = zeros
        xp[pl.ds(CONV_HALO + seq, CONV_HALO), :] = zeros
        xp[pl.ds(CONV_HALO, seq), :] = a_ref[...] * _sig(g_ref[...])
        _fill_shifted(xp, sh, seq)

        def blk(i, carry):
            t0 = pl.multiple_of(i * CONV_ROWS, CONV_ROWS)
            acc = jnp.zeros((CONV_ROWS, LANES), F32)
            for j in range(CONV_WIDTH):
                jj = j + 1
                acc = acc + sh[jj % 8, pl.ds(t0 + 8 * (jj // 8), CONV_ROWS), :] * w_ref[j:j + 1, :]
            uc_ref[pl.ds(t0, CONV_ROWS), :] = acc + b_ref[...]
            return carry

        lax.fori_loop(0, seq // CONV_ROWS, blk, 0)

    return _call(
        body, name="conv_fwd", grid=(n_seq, n_cb),
        in_specs=[pl.BlockSpec((None, seq, LANES), lambda b, cb: (b, 0, cb)),
                  pl.BlockSpec((None, seq, LANES), lambda b, cb: (b, 0, n_cb + cb)),
                  pl.BlockSpec((CONV_WIDTH, LANES), lambda b, cb: (0, cb)),
                  pl.BlockSpec((1, LANES), lambda b, cb: (0, cb))],
        out_specs=pl.BlockSpec((None, seq, LANES), lambda b, cb: (b, 0, cb)),
        out_shape=_sds((n_seq, seq, D_CONV), F32),
        scratch_shapes=[pltpu.VMEM((seq + 2 * CONV_HALO, LANES), F32),
                        pltpu.VMEM((8, seq + 2 * CONV_HALO, LANES), F32)],
        compiler_params=_params(("parallel", "parallel")),
    )(proj3, proj3, w_dw, b_dw)


def _conv_bwd(duc3, proj3, w_dw):
    n_seq, seq, _ = proj3.shape
    n_cb = D_CONV // LANES

    def body(duc_ref, a_ref, g_ref, w_ref, da_ref, dg_ref, dw_ref, db_ref, xp, sh):
        @pl.when(pl.program_id(1) == 0)
        def _():
            dw_ref[...] = jnp.zeros_like(dw_ref)
            db_ref[...] = jnp.zeros_like(db_ref)

        zeros = jnp.zeros((CONV_HALO, LANES), F32)
        xp[pl.ds(0, CONV_HALO), :] = zeros
        xp[pl.ds(CONV_HALO + seq, CONV_HALO), :] = zeros
        xp[pl.ds(CONV_HALO, seq), :] = a_ref[...] * _sig(g_ref[...])
        _fill_shifted(xp, sh, seq)
        for j in range(CONV_WIDTH):
            jj = j + 1

            def wblk(i, acc, jj=jj):
                t0 = pl.multiple_of(i * CONV_ROWS, CONV_ROWS)
                return acc + duc_ref[pl.ds(t0, CONV_ROWS), :] * sh[jj % 8, pl.ds(t0 + 8 * (jj // 8), CONV_ROWS), :]

            acc = lax.fori_loop(0, seq // CONV_ROWS, wblk, jnp.zeros((CONV_ROWS, LANES), F32))
            dw_ref[j:j + 1, :] += jnp.sum(acc, axis=0, keepdims=True)
        db_ref[0:1, :] += jnp.sum(duc_ref[...], axis=0, keepdims=True)
        xp[pl.ds(CONV_HALO, seq), :] = duc_ref[...]
        _fill_shifted(xp, sh, seq)

        def ublk(i, carry):
            t0 = pl.multiple_of(i * CONV_ROWS, CONV_ROWS)
            acc = jnp.zeros((CONV_ROWS, LANES), F32)
            for j in range(CONV_WIDTH):
                jj = CONV_WIDTH - j
                acc = acc + sh[jj % 8, pl.ds(t0 + 8 * (jj // 8), CONV_ROWS), :] * w_ref[j:j + 1, :]
            av = a_ref[pl.ds(t0, CONV_ROWS), :]
            sg = _sig(g_ref[pl.ds(t0, CONV_ROWS), :])
            da_ref[pl.ds(t0, CONV_ROWS), :] = (acc * sg).astype(BF16)
            dg_ref[pl.ds(t0, CONV_ROWS), :] = (acc * av * sg * (1.0 - sg)).astype(BF16)
            return carry

        lax.fori_loop(0, seq // CONV_ROWS, ublk, 0)

    return _call(
        body, name="conv_bwd", grid=(n_cb, n_seq),
        in_specs=[pl.BlockSpec((None, seq, LANES), lambda cb, b: (b, 0, cb)),
                  pl.BlockSpec((None, seq, LANES), lambda cb, b: (b, 0, cb)),
                  pl.BlockSpec((None, seq, LANES), lambda cb, b: (b, 0, n_cb + cb)),
                  pl.BlockSpec((CONV_WIDTH, LANES), lambda cb, b: (0, cb))],
        out_specs=[pl.BlockSpec((None, seq, LANES), lambda cb, b: (b, 0, cb)),
                   pl.BlockSpec((None, seq, LANES), lambda cb, b: (b, 0, cb)),
                   pl.BlockSpec((32, LANES), lambda cb, b: (0, cb)),
                   pl.BlockSpec((8, LANES), lambda cb, b: (0, cb))],
        out_shape=[_sds((n_seq, seq, D_CONV), BF16), _sds((n_seq, seq, D_CONV), BF16),
                   _sds((32, D_CONV), F32), _sds((8, D_CONV), F32)],
        scratch_shapes=[pltpu.VMEM((seq + 2 * CONV_HALO, LANES), F32),
                        pltpu.VMEM((8, seq + 2 * CONV_HALO, LANES), F32)],
        compiler_params=_params(("parallel", "arbitrary")),
    )(duc3, proj3, proj3, w_dw)


MASKED = 1e30
ATT_UNROLL = 2


def _distance_mats(dil, seg_len):
    kw = min(2 * Q_BLOCK, seg_len)
    offsets = (0, -RADIUS, -2 * RADIUS) if kw == 2 * Q_BLOCK else (0,)
    a = np.arange(Q_BLOCK)[:, None]
    b = np.arange(kw)[None, :]
    mats = []
    for off in offsets:
        rel = np.abs(b + off - a)
        mats.append(np.where(rel <= RADIUS, dil * rel, MASKED))
    return jnp.asarray(np.stack(mats).astype(np.float32))


def _alibi_rows():
    s = np.zeros((4, 8, LANES), np.float32)
    for hp in range(4):
        for hl in range(2):
            s[hp, hl, :] = 2.0 ** (-(2 * hp + hl + 1))
    return jnp.asarray(s)


def _window(n, seg_len):
    i0 = pl.multiple_of(n * Q_BLOCK, Q_BLOCK)
    if seg_len <= Q_BLOCK:
        return i0, i0, 0
    per_seg = seg_len // Q_BLOCK
    j = n % per_seg
    seg0 = (n // per_seg) * seg_len
    ks_local = jnp.clip(j * Q_BLOCK - RADIUS, 0, seg_len - 2 * Q_BLOCK)
    ks = pl.multiple_of(seg0 + ks_local, RADIUS)
    var = jnp.where(j == 0, 0, jnp.where(j == per_seg - 1, 2, 1))
    return i0, ks, var


def _first_head(rows):
    return lax.broadcasted_iota(jnp.int32, (rows, LANES), 1) < HEAD_DIM


def _head_mean(x, first):
    s0 = jnp.sum(jnp.where(first, x, 0.0), axis=1, keepdims=True)
    s1 = jnp.sum(jnp.where(first, 0.0, x), axis=1, keepdims=True)
    return jnp.where(first, s0, s1) * (1.0 / HEAD_DIM)


def _permute_rows(dst, src, dil, seq):
    seg = seq // dil
    for r in range(dil):
        rows = pl.ds(r, seg, stride=dil) if dil > 1 else pl.ds(0, seq)
        dst[pl.ds(r * seg, seg), :] = src[rows, :].astype(dst.dtype)


def _permute_rows_by_head(dst, src, dil, seq):
    seg = seq // dil
    first = _first_head(seg)
    for r in range(dil):
        rows = pl.ds(r, seg, stride=dil) if dil > 1 else pl.ds(0, seq)
        val = src[rows, :]
        dst[0, pl.ds(r * seg, seg), :] = jnp.where(first, val, 0.0).astype(dst.dtype)
        dst[1, pl.ds(r * seg, seg), :] = jnp.where(first, 0.0, val).astype(dst.dtype)


def _qk_normalise(q_ref, g2_ref, dst, seq, scale):
    def chunk(ci, carry):
        rows = pl.ds(pl.multiple_of(ci * 256, 256), 256)
        qv = q_ref[rows, :]
        r = lax.rsqrt(_head_mean(qv * qv, _first_head(256)) + EPS)
        dst[rows, :] = qv * r * (g2_ref[...] * scale)
        return carry

    lax.fori_loop(0, seq // 256, chunk, 0)


def _attn_fwd(proj3, g_q2, g_k2, ride):
    n_seq, seq, _ = proj3.shape
    dms = [_distance_mats(d, seq // d) for d in DILATIONS]
    col0 = 2 * D_CONV // LANES
    n_hp = D_ATT // LANES

    n_ride = len(ride)
    ride_scatter = [s for _, s in ride]

    def body(*refs):
        q_ref, k_ref, v_ref, gq_ref, gk_ref, sl_ref, dm1, dm4, dm16 = refs[:9]
        ride_in = refs[9:9 + n_ride]
        y_ref, lse_ref = refs[9 + n_ride:11 + n_ride]
        ride_out = refs[11 + n_ride:11 + 2 * n_ride]
        (qf, kf, qp, kp, vp, op, mp, lp, o1, o4, o16, m1, m4, m16, l1, l4, l16) = refs[11 + 2 * n_ride:28 + 2 * n_ride]
        ride_args = (ride_scatter, ride_in, ride_out) + tuple(refs[28 + 2 * n_ride:])
        step = pl.program_id(0) * n_hp + pl.program_id(1)

        @pl.when(step == 0)
        def _():
            _exchange_start(*ride_args)

        dm_refs = (dm1, dm4, dm16)
        o_nat, m_nat, l_nat = (o1, o4, o16), (m1, m4, m16), (l1, l4, l16)
        _qk_normalise(q_ref, gq_ref, qf, seq, HEAD_DIM ** -0.5)
        _qk_normalise(k_ref, gk_ref, kf, seq, 1.0)
        slopes = (sl_ref[0:1, 0:1], sl_ref[1:2, 0:1])
        for pi, dil in enumerate(DILATIONS):
            seg = seq // dil
            kw = min(2 * Q_BLOCK, seg)
            _permute_rows_by_head(qp, qf, dil, seq)
            _permute_rows(kp, kf, dil, seq)
            _permute_rows(vp, v_ref, dil, seq)
            o_dst, m_dst, l_dst = (o_nat[pi], m_nat[pi], l_nat[pi]) if dil == 1 else (op, mp, lp)

            def blk(it, carry, seg=seg, kw=kw, pi=pi, o_dst=o_dst, m_dst=m_dst, l_dst=l_dst):
                first = _first_head(Q_BLOCK)
                for sub in range(ATT_UNROLL):
                    i0, ks, var = _window(it * ATT_UNROLL + sub, seg)
                    qrows, krows = pl.ds(i0, Q_BLOCK), pl.ds(ks, kw)
                    k = kp[krows, :]
                    v = vp[krows, :]
                    dm = dm_refs[pi][var]
                    o, m, l = [], [], []
                    for h in range(2):
                        s = lax.dot_general(qp[h, qrows, :], k, NT, preferred_element_type=F32) - slopes[h] * dm
                        m.append(jnp.max(s, axis=1, keepdims=True))
                        p = jnp.exp(s - m[h])
                        l.append(jnp.sum(p, axis=1, keepdims=True))
                        o.append(jnp.dot(p.astype(BF16), v, preferred_element_type=F32))
                    o_dst[qrows, :] = jnp.where(first, o[0], o[1])
                    m_dst[qrows, :] = jnp.where(first, m[0], m[1])
                    l_dst[qrows, :] = jnp.where(first, l[0], l[1])
                return carry

            lax.fori_loop(0, seq // (Q_BLOCK * ATT_UNROLL), blk, 0)
            if dil > 1:
                for r in range(dil):
                    nat, perm = pl.ds(r, seg, stride=dil), pl.ds(r * seg, seg)
                    o_nat[pi][nat, :] = op[perm, :]
                    m_nat[pi][nat, :] = mp[perm, :]
                    l_nat[pi][nat, :] = lp[perm, :]

        def merge(ci, carry):
            rows = pl.ds(pl.multiple_of(ci * 256, 256), 256)
            ms = [m_nat[pi][rows, :] for pi in range(3)]
            m_all = jnp.maximum(jnp.maximum(ms[0], ms[1]), ms[2])
            es = [jnp.exp(m - m_all) for m in ms]
            l_all = l_nat[0][rows, :] * es[0] + l_nat[1][rows, :] * es[1] + l_nat[2][rows, :] * es[2]
            inv = 1.0 / l_all
            o = (o_nat[0][rows, :] * (es[0] * inv) + o_nat[1][rows, :] * (es[1] * inv)
                 + o_nat[2][rows, :] * (es[2] * inv))
            y_ref[rows, :] = o.astype(BF16)
            lse_ref[rows, :] = m_all + jnp.log(l_all)
            return carry

        lax.fori_loop(0, seq // 256, merge, 0)

        @pl.when(step == n_seq * n_hp - 1)
        def _():
            _exchange_wait(*ride_args)

    def col(off):
        return pl.BlockSpec((None, seq, LANES), lambda b, hp: (b, 0, col0 + off * n_hp + hp))

    def whole(arr):
        return pl.BlockSpec(arr.shape, lambda b, hp: (0,) * arr.ndim)

    rows_f32 = pltpu.VMEM((seq, LANES), F32)
    rows_bf16 = pltpu.VMEM((seq, LANES), BF16)
    return _call(
        body, name="attn_fwd", grid=(n_seq, n_hp),
        in_specs=[col(0), col(1), col(2), whole(g_q2), whole(g_k2),
                  pl.BlockSpec((None, 8, LANES), lambda b, hp: (hp, 0, 0)),
                  whole(dms[0]), whole(dms[1]), whole(dms[2])] + [ANY_SPEC] * n_ride,
        out_specs=[pl.BlockSpec((None, seq, LANES), lambda b, hp: (b, 0, hp)),
                   pl.BlockSpec((None, seq, LANES), lambda b, hp: (b, 0, hp))] + [ANY_SPEC] * n_ride,
        out_shape=[_sds((n_seq, seq, D_ATT), BF16), _sds((n_seq, seq, D_ATT), F32)] + _exchange_shapes(ride),
        scratch_shapes=[rows_f32, rows_f32, pltpu.VMEM((2, seq, LANES), BF16), rows_bf16, rows_bf16]
        + [rows_f32] * 12 + _exchange_sems(n_ride),
        compiler_params=_params(("arbitrary", "arbitrary")),
    )(proj3, proj3, proj3, g_q2, g_k2, _alibi_rows(), *dms, *[a for a, _ in ride])


def _attn_bwd(proj3, do3, y_att3, lse3, g_q2, g_k2, ride):
    n_seq, seq, _ = proj3.shape
    dms = [_distance_mats(d, seq // d) for d in DILATIONS]
    col0 = 2 * D_CONV // LANES
    n_hp = D_ATT // LANES

    n_ride = len(ride)
    ride_scatter = [s for _, s in ride]

    def body(*refs):
        q_ref, k_ref, v_ref, do_ref, o_ref, lse_ref, gq_ref, gk_ref, sl_ref, dm1, dm4, dm16 = refs[:12]
        ride_in = refs[12:12 + n_ride]
        dq_ref, dk_ref, dv_ref, dg_ref = refs[12 + n_ride:16 + n_ride]
        ride_out = refs[16 + n_ride:16 + 2 * n_ride]
        (qf, kf, qp, dop, kp, vp, sn, sp, dqp, dkp, dvp, dqn, dkn, dvn) = refs[16 + 2 * n_ride:30 + 2 * n_ride]
        ride_args = (ride_scatter, ride_in, ride_out) + tuple(refs[30 + 2 * n_ride:])
        dm_refs = (dm1, dm4, dm16)
        step = pl.program_id(0) * n_hp + pl.program_id(1)

        @pl.when(step == 0)
        def _():
            _exchange_start(*ride_args)
            dg_ref[...] = jnp.zeros_like(dg_ref)

        _qk_normalise(q_ref, gq_ref, qf, seq, HEAD_DIM ** -0.5)
        _qk_normalise(k_ref, gk_ref, kf, seq, 1.0)

        def stats(ci, carry):
            rows = pl.ds(pl.multiple_of(ci * 256, 256), 256)
            first = _first_head(256)
            prod = do_ref[rows, :] * o_ref[rows, :].astype(F32)
            sn[rows, 0:1] = lse_ref[rows, 0:1]
            sn[rows, 1:2] = lse_ref[rows, HEAD_DIM:HEAD_DIM + 1]
            sn[rows, 2:3] = jnp.sum(jnp.where(first, prod, 0.0), axis=1, keepdims=True)
            sn[rows, 3:4] = jnp.sum(jnp.where(first, 0.0, prod), axis=1, keepdims=True)
            return carry

        lax.fori_loop(0, seq // 256, stats, 0)
        slopes = (sl_ref[0:1, 0:1], sl_ref[1:2, 0:1])
        half = seq // (Q_BLOCK * ATT_UNROLL)

        for pi, dil in enumerate(DILATIONS):
            seg = seq // dil
            kw = min(2 * Q_BLOCK, seg)
            _permute_rows_by_head(qp, qf, dil, seq)
            _permute_rows_by_head(dop, do_ref, dil, seq)
            _permute_rows(kp, kf, dil, seq)
            _permute_rows(vp, v_ref, dil, seq)
            _permute_rows(sp, sn, dil, seq)
            dkp[...] = jnp.zeros_like(dkp)
            dvp[...] = jnp.zeros_like(dvp)

            def blk(it, carry, seg=seg, kw=kw, pi=pi):
                first = _first_head(Q_BLOCK)
                for sub in range(ATT_UNROLL):
                    i0, ks, var = _window(it + sub * half, seg)
                    qrows, krows = pl.ds(i0, Q_BLOCK), pl.ds(ks, kw)
                    k = kp[krows, :]
                    v = vp[krows, :]
                    dm = dm_refs[pi][var]
                    dq, dk, dv = [], [], []
                    for h in range(2):
                        q = qp[h, qrows, :]
                        do = dop[h, qrows, :]
                        s = lax.dot_general(q, k, NT, preferred_element_type=F32) - slopes[h] * dm
                        p = jnp.exp(s - sp[qrows, h:h + 1])
                        dp = lax.dot_general(do, v, NT, preferred_element_type=F32)
                        ds = (p * (dp - sp[qrows, 2 + h:3 + h])).astype(BF16)
                        dq.append(jnp.dot(ds, k, preferred_element_type=F32))
                        dk.append(lax.dot_general(ds, q, TN, preferred_element_type=F32))
                        dv.append(lax.dot_general(p.astype(BF16), do, TN, preferred_element_type=F32))
                    dqp[qrows, :] = jnp.where(first, dq[0], dq[1])
                    dkp[sub, krows, :] += dk[0] + dk[1]
                    dvp[sub, krows, :] += dv[0] + dv[1]
                return carry

            lax.fori_loop(0, half, blk, 0)
            for r in range(dil):
                nat = pl.ds(r, seg, stride=dil) if dil > 1 else pl.ds(0, seq)
                perm = pl.ds(r * seg, seg)
                dq_new = dqp[perm, :]
                dk_new = dkp[0, perm, :] + dkp[1, perm, :]
                dv_new = dvp[0, perm, :] + dvp[1, perm, :]
                if pi == 0:
                    dqn[nat, :], dkn[nat, :], dvn[nat, :] = dq_new, dk_new, dv_new
                else:
                    dqn[nat, :] += dq_new
                    dkn[nat, :] += dk_new
                    dvn[nat, :] += dv_new

        def finish(ci, carry):
            rows = pl.ds(pl.multiple_of(ci * 256, 256), 256)
            first = _first_head(256)
            for src_ref, g_ref, dn, dst_ref, scale, row in (
                    (q_ref, gq_ref, dqn, dq_ref, HEAD_DIM ** -0.5, 0), (k_ref, gk_ref, dkn, dk_ref, 1.0, 1)):
                xv = src_ref[rows, :]
                r = lax.rsqrt(_head_mean(xv * xv, first) + EPS)
                xhat = xv * r
                d = dn[rows, :] * scale
                dg_ref[row:row + 1, :] += jnp.sum(d * xhat, axis=0, keepdims=True)
                dxh = d * g_ref[...]
                dst_ref[rows, :] = (r * (dxh - xhat * _head_mean(dxh * xhat, first))).astype(BF16)
            dv_ref[rows, :] = dvn[rows, :].astype(BF16)
            return carry

        lax.fori_loop(0, seq // 256, finish, 0)

        @pl.when(step == n_seq * n_hp - 1)
        def _():
            _exchange_wait(*ride_args)

    def col(off):
        return pl.BlockSpec((None, seq, LANES), lambda b, hp: (b, 0, col0 + off * n_hp + hp))

    def whole(arr):
        return pl.BlockSpec(arr.shape, lambda b, hp: (0,) * arr.ndim)

    att = pl.BlockSpec((None, seq, LANES), lambda b, hp: (b, 0, hp))
    rows_f32 = pltpu.VMEM((seq, LANES), F32)
    rows_bf16 = pltpu.VMEM((seq, LANES), BF16)
    by_head_bf16 = pltpu.VMEM((2, seq, LANES), BF16)
    per_sub_f32 = pltpu.VMEM((ATT_UNROLL, seq, LANES), F32)
    return _call(
        body, name="attn_bwd", grid=(n_seq, n_hp),
        in_specs=[col(0), col(1), col(2), att, att, att, whole(g_q2), whole(g_k2),
                  pl.BlockSpec((None, 8, LANES), lambda b, hp: (hp, 0, 0)),
                  whole(dms[0]), whole(dms[1]), whole(dms[2])] + [ANY_SPEC] * n_ride,
        out_specs=[att, att, att, pl.BlockSpec((8, LANES), lambda b, hp: (0, 0))] + [ANY_SPEC] * n_ride,
        out_shape=[_sds((n_seq, seq, D_ATT), BF16)] * 3 + [_sds((8, LANES), F32)] + _exchange_shapes(ride),
        scratch_shapes=[rows_f32, rows_f32, by_head_bf16, by_head_bf16, rows_bf16, rows_bf16, rows_f32, rows_f32,
                        rows_f32, per_sub_f32, per_sub_f32, rows_f32, rows_f32, rows_f32] + _exchange_sems(n_ride),
        compiler_params=_params(("arbitrary", "arbitrary")),
    )(proj3, proj3, proj3, do3, y_att3, lse3, g_q2, g_k2, _alibi_rows(), *dms, *[a for a, _ in ride])


def _mix_out(uc2, y_att2, x2, mod8, g_ln, b_ln, g_ffn, w_out, seq, tm=512):
    tokens = x2.shape[0]
    per_seq = seq // tm

    def body(uc_ref, ya_ref, x_ref, m_ref, gl_ref, bl_ref, gf_ref, w_ref, yc_ref, mix_ref, x1_ref, h2_ref):
        uc = uc_ref[...]
        mu = jnp.mean(uc, axis=-1, keepdims=True)
        cen = uc - mu
        rs = lax.rsqrt(jnp.mean(cen * cen, axis=-1, keepdims=True) + EPS)
        z = cen * rs * gl_ref[...] + bl_ref[...]
        yc = (z * _sig(z)).astype(BF16)
        yc_ref[...] = yc
        mix = (jnp.dot(yc, w_ref[pl.ds(0, D_CONV), :], preferred_element_type=F32)
               + jnp.dot(ya_ref[...], w_ref[pl.ds(D_CONV, D_ATT), :], preferred_element_type=F32))
        mix_ref[...] = mix
        x1 = x_ref[...] + m_ref[2:3, :] * mix
        x1_ref[...] = x1
        r = lax.rsqrt(jnp.mean(x1 * x1, axis=-1, keepdims=True) + EPS)
        h2_ref[...] = ((x1 * r * gf_ref[...]) * (1.0 + m_ref[4:5, :]) + m_ref[3:4, :]).astype(BF16)

    def rows(width):
        return pl.BlockSpec((tm, width), lambda i: (i, 0))

    def vec(width):
        return pl.BlockSpec((1, width), lambda i: (0, 0))

    return _call(
        body, name="mix_out", grid=(tokens // tm,),
        in_specs=[rows(D_CONV), rows(D_ATT), rows(D_MODEL),
                  pl.BlockSpec((None, 8, D_MODEL), lambda i: (i // per_seq, 0, 0)),
                  vec(D_CONV), vec(D_CONV), vec(D_MODEL),
                  pl.BlockSpec((D_MODEL, D_MODEL), lambda i: (0, 0))],
        out_specs=[rows(D_CONV), rows(D_MODEL), rows(D_MODEL), rows(D_MODEL)],
        out_shape=[_sds((tokens, D_CONV), BF16), _sds((tokens, D_MODEL), F32),
                   _sds((tokens, D_MODEL), F32), _sds((tokens, D_MODEL), BF16)],
        compiler_params=_params(("parallel",)),
    )(uc2, y_att2, x2, mod8, g_ln, b_ln, g_ffn, w_out)


def _mix_out_bwd(dmix, uc2, g_ln, b_ln, w_out, tm=512):
    tokens = dmix.shape[0]

    def body(dm_ref, uc_ref, gl_ref, bl_ref, w_ref, duc_ref, do_ref, dgb_ref):
        @pl.when(pl.program_id(0) == 0)
        def _():
            dgb_ref[...] = jnp.zeros_like(dgb_ref)

        dmv = dm_ref[...]
        dyc = lax.dot_general(dmv, w_ref[pl.ds(0, D_CONV), :], NT, preferred_element_type=F32)
        do_ref[...] = lax.dot_general(dmv, w_ref[pl.ds(D_CONV, D_ATT), :], NT, preferred_element_type=F32)
        uc = uc_ref[...]
        mu = jnp.mean(uc, axis=-1, keepdims=True)
        cen = uc - mu
        rs = lax.rsqrt(jnp.mean(cen * cen, axis=-1, keepdims=True) + EPS)
        xh = cen * rs
        z = xh * gl_ref[...] + bl_ref[...]
        sg = _sig(z)
        dz = dyc * (sg * (1.0 + z * (1.0 - sg)))
        dgb_ref[0:1, :] += jnp.sum(dz * xh, axis=0, keepdims=True)
        dgb_ref[1:2, :] += jnp.sum(dz, axis=0, keepdims=True)
        dxh = dz * gl_ref[...]
        duc_ref[...] = rs * (dxh - jnp.mean(dxh, axis=-1, keepdims=True)
                             - xh * jnp.mean(dxh * xh, axis=-1, keepdims=True))

    return _call(
        body, name="mix_out_bwd", grid=(tokens // tm,),
        in_specs=[pl.BlockSpec((tm, D_MODEL), lambda i: (i, 0)),
                  pl.BlockSpec((tm, D_CONV), lambda i: (i, 0)),
                  pl.BlockSpec((1, D_CONV), lambda i: (0, 0)),
                  pl.BlockSpec((1, D_CONV), lambda i: (0, 0)),
                  pl.BlockSpec((D_MODEL, D_MODEL), lambda i: (0, 0))],
        out_specs=[pl.BlockSpec((tm, D_CONV), lambda i: (i, 0)),
                   pl.BlockSpec((tm, D_ATT), lambda i: (i, 0)),
                   pl.BlockSpec((8, D_CONV), lambda i: (0, 0))],
        out_shape=[_sds((tokens, D_CONV), F32), _sds((tokens, D_ATT), F32), _sds((8, D_CONV), F32)],
        compiler_params=_params(("arbitrary",)),
    )(dmix, uc2, g_ln, b_ln, w_out)


FF_TILE = 256


def _ffn_fwd(h2, w_gate, w_up, w_down, tm=1024):
    tokens = h2.shape[0]

    def body(h_ref, wg_ref, wu_ref, wd_ref, gate_ref, up_ref, f_ref):
        @pl.when(pl.program_id(1) == 0)
        def _():
            f_ref[...] = jnp.zeros_like(f_ref)

        hv = h_ref[...]
        gate = jnp.dot(hv, wg_ref[...], preferred_element_type=F32)
        up = jnp.dot(hv, wu_ref[...], preferred_element_type=F32)
        gate_ref[...] = gate
        up_ref[...] = up
        act = (gate * _sig(gate) * up).astype(BF16)
        f_ref[...] += jnp.dot(act, wd_ref[...], preferred_element_type=F32)

    return _call(
        body, name="ffn_fwd", grid=(tokens // tm, D_FF // FF_TILE),
        in_specs=[pl.BlockSpec((tm, D_MODEL), lambda i, j: (i, 0)),
                  pl.BlockSpec((D_MODEL, FF_TILE), lambda i, j: (0, j)),
                  pl.BlockSpec((D_MODEL, FF_TILE), lambda i, j: (0, j)),
                  pl.BlockSpec((FF_TILE, D_MODEL), lambda i, j: (j, 0))],
        out_specs=[pl.BlockSpec((tm, FF_TILE), lambda i, j: (i, j)),
                   pl.BlockSpec((tm, FF_TILE), lambda i, j: (i, j)),
                   pl.BlockSpec((tm, D_MODEL), lambda i, j: (i, 0))],
        out_shape=[_sds((tokens, D_FF), F32), _sds((tokens, D_FF), F32), _sds((tokens, D_MODEL), F32)],
        compiler_params=_params(("parallel", "arbitrary")),
    )(h2, w_gate, w_up, w_down)


def _ffn_bwd(df, gate, up, w_gate, w_up, w_down, tm=1024):
    tokens = df.shape[0]

    def body(df_ref, gate_ref, up_ref, wg_ref, wu_ref, wd_ref, dgate_ref, dup_ref, act_ref, dh_ref):
        @pl.when(pl.program_id(1) == 0)
        def _():
            dh_ref[...] = jnp.zeros_like(dh_ref)

        dact = lax.dot_general(df_ref[...], wd_ref[...], NT, preferred_element_type=F32)
        gate = gate_ref[...]
        up = up_ref[...]
        sg = _sig(gate)
        silu = gate * sg
        act_ref[...] = (silu * up).astype(BF16)
        dup = (dact * silu).astype(BF16)
        dgate = (dact * up * (sg * (1.0 + gate * (1.0 - sg)))).astype(BF16)
        dup_ref[...] = dup
        dgate_ref[...] = dgate
        dh_ref[...] += (lax.dot_general(dgate, wg_ref[...], NT, preferred_element_type=F32)
                        + lax.dot_general(dup, wu_ref[...], NT, preferred_element_type=F32))

    tile = pl.BlockSpec((tm, FF_TILE), lambda i, j: (i, j))
    return _call(
        body, name="ffn_bwd", grid=(tokens // tm, D_FF // FF_TILE),
        in_specs=[pl.BlockSpec((tm, D_MODEL), lambda i, j: (i, 0)), tile, tile,
                  pl.BlockSpec((D_MODEL, FF_TILE), lambda i, j: (0, j)),
                  pl.BlockSpec((D_MODEL, FF_TILE), lambda i, j: (0, j)),
                  pl.BlockSpec((FF_TILE, D_MODEL), lambda i, j: (j, 0))],
        out_specs=[tile, tile, tile, pl.BlockSpec((tm, D_MODEL), lambda i, j: (i, 0))],
        out_shape=[_sds((tokens, D_FF), BF16)] * 3 + [_sds((tokens, D_MODEL), F32)],
        compiler_params=_params(("parallel", "arbitrary")),
    )(df, gate, up, w_gate, w_up, w_down)


def _loss_head(f, x1, target, mod8, seq, tm=512):
    tokens = f.shape[0]
    per_seq = seq // tm
    n_seq = tokens // seq

    def body(f_ref, x1_ref, t_ref, m_ref, dy_ref, df_ref, sq_ref, dgf_ref):
        i = pl.program_id(0)

        @pl.when(i == 0)
        def _():
            sq_ref[...] = jnp.zeros_like(sq_ref)

        @pl.when(i % per_seq == 0)
        def _():
            dgf_ref[...] = jnp.zeros_like(dgf_ref)

        fv = f_ref[...]
        gate = m_ref[5:6, :]
        diff = x1_ref[...] + gate * fv - t_ref[...]
        sq_ref[0:1, :] += jnp.sum(diff * diff, axis=0, keepdims=True)
        dy = diff * (1.0 / D_MODEL)
        dy_ref[...] = dy
        df_ref[...] = (gate * dy).astype(BF16)
        dgf_ref[0:1, :] += jnp.sum(dy * fv, axis=0, keepdims=True)

    rows = pl.BlockSpec((tm, D_MODEL), lambda i: (i, 0))
    per = pl.BlockSpec((None, 8, D_MODEL), lambda i: (i // per_seq, 0, 0))
    return _call(
        body, name="loss_head", grid=(tokens // tm,),
        in_specs=[rows, rows, rows, per],
        out_specs=[rows, rows, pl.BlockSpec((8, D_MODEL), lambda i: (0, 0)), per],
        out_shape=[_sds((tokens, D_MODEL), F32), _sds((tokens, D_MODEL), BF16),
                   _sds((8, D_MODEL), F32), _sds((n_seq, 8, D_MODEL), F32)],
        compiler_params=_params(("arbitrary",)),
    )(f, x1, target, mod8)


def _norm2_bwd(dh2, x1, dy, mix, mod8, g_ffn, seq, tm=512):
    tokens = dh2.shape[0]
    per_seq = seq // tm
    n_seq = tokens // seq

    def body(dh_ref, x1_ref, dy_ref, mix_ref, m_ref, g_ref, dx1_ref, dmix_ref, dg_ref, dm_ref):
        i = pl.program_id(0)

        @pl.when(i == 0)
        def _():
            dg_ref[...] = jnp.zeros_like(dg_ref)

        @pl.when(i % per_seq == 0)
        def _():
            dm_ref[...] = jnp.zeros_like(dm_ref)

        dh = dh_ref[...]
        x1 = x1_ref[...]
        r = lax.rsqrt(jnp.mean(x1 * x1, axis=-1, keepdims=True) + EPS)
        xhat = x1 * r
        g = g_ref[...]
        dm_ref[0:1, :] += jnp.sum(dh, axis=0, keepdims=True)
        dm_ref[1:2, :] += jnp.sum(dh * (xhat * g), axis=0, keepdims=True)
        dn = dh * (1.0 + m_ref[4:5, :])
        dg_ref[0:1, :] += jnp.sum(dn * xhat, axis=0, keepdims=True)
        dxh = dn * g
        dx1 = dy_ref[...] + r * (dxh - xhat * jnp.mean(dxh * xhat, axis=-1, keepdims=True))
        dx1_ref[...] = dx1
        dm_ref[2:3, :] += jnp.sum(dx1 * mix_ref[...], axis=0, keepdims=True)
        dmix_ref[...] = (m_ref[2:3, :] * dx1).astype(BF16)

    rows = pl.BlockSpec((tm, D_MODEL), lambda i: (i, 0))
    per = pl.BlockSpec((None, 8, D_MODEL), lambda i: (i // per_seq, 0, 0))
    return _call(
        body, name="norm2_bwd", grid=(tokens // tm,),
        in_specs=[rows, rows, rows, rows, per, pl.BlockSpec((1, D_MODEL), lambda i: (0, 0))],
        out_specs=[rows, rows, pl.BlockSpec((8, D_MODEL), lambda i: (0, 0)), per],
        out_shape=[_sds((tokens, D_MODEL), F32), _sds((tokens, D_MODEL), BF16),
                   _sds((8, D_MODEL), F32), _sds((n_seq, 8, D_MODEL), F32)],
        compiler_params=_params(("arbitrary",)),
    )(dh2, x1, dy, mix, mod8, g_ffn)


def _mix_in_bwd(d_a, d_g, d_q, d_k, d_v, w_in, x2, dx1, mod8, g_mix, seq, tm=512):
    tokens = x2.shape[0]
    per_seq = seq // tm
    n_seq = tokens // seq
    parts = (d_a, d_g, d_q, d_k, d_v)
    width = D_CONV

    def body(da_ref, dg_ref, dq_ref, dk_ref, dv_ref, w_ref, x_ref, dx1_ref, m_ref, g_ref, gx_ref, dgm_ref, dm_ref):
        i = pl.program_id(0)

        @pl.when(i == 0)
        def _():
            dgm_ref[...] = jnp.zeros_like(dgm_ref)

        @pl.when(i % per_seq == 0)
        def _():
            dm_ref[...] = jnp.zeros_like(dm_ref)

        dh = jnp.zeros((tm, D_MODEL), F32)
        for n, ref in enumerate((da_ref, dg_ref, dq_ref, dk_ref, dv_ref)):
            dh = dh + lax.dot_general(ref[...], w_ref[:, pl.ds(n * width, width)], NT, preferred_element_type=F32)
        xv = x_ref[...]
        r = lax.rsqrt(jnp.mean(xv * xv, axis=-1, keepdims=True) + EPS)
        xhat = xv * r
        g = g_ref[...]
        dm_ref[0:1, :] += jnp.sum(dh, axis=0, keepdims=True)
        dm_ref[1:2, :] += jnp.sum(dh * (xhat * g), axis=0, keepdims=True)
        dn = dh * (1.0 + m_ref[1:2, :])
        dgm_ref[0:1, :] += jnp.sum(dn * xhat, axis=0, keepdims=True)
        dxh = dn * g
        gx_ref[...] = dx1_ref[...] + r * (dxh - xhat * jnp.mean(dxh * xhat, axis=-1, keepdims=True))

    rows = pl.BlockSpec((tm, D_MODEL), lambda i: (i, 0))
    half = pl.BlockSpec((tm, width), lambda i: (i, 0))
    per = pl.BlockSpec((None, 8, D_MODEL), lambda i: (i // per_seq, 0, 0))
    return _call(
        body, name="mix_in_bwd", grid=(tokens // tm,),
        in_specs=[half] * 5 + [pl.BlockSpec((D_MODEL, D_IN), lambda i: (0, 0)), rows, rows, per,
                               pl.BlockSpec((1, D_MODEL), lambda i: (0, 0))],
        out_specs=[rows, pl.BlockSpec((8, D_MODEL), lambda i: (0, 0)), per],
        out_shape=[_sds((tokens, D_MODEL), F32), _sds((8, D_MODEL), F32), _sds((n_seq, 8, D_MODEL), F32)],
        compiler_params=_params(("arbitrary",)),
    )(*parts, w_in, x2, dx1, mod8, g_mix)


def _grad_matmul(a, b, name, tmo, tno, tk=512):
    tokens, m = a.shape
    n = b.shape[1]

    def body(a_ref, b_ref, o_ref):
        @pl.when(pl.program_id(2) == 0)
        def _():
            o_ref[...] = jnp.zeros_like(o_ref)

        o_ref[...] += lax.dot_general(a_ref[...], b_ref[...], TN, preferred_element_type=F32)

    return _call(
        body, name=name, grid=(m // tmo, n // tno, tokens // tk),
        in_specs=[pl.BlockSpec((tk, tmo), lambda i, j, k: (k, i)),
                  pl.BlockSpec((tk, tno), lambda i, j, k: (k, j))],
        out_specs=pl.BlockSpec((tmo, tno), lambda i, j, k: (i, j)),
        out_shape=_sds((m, n), F32),
        compiler_params=_params(("parallel", "parallel", "arbitrary")),
    )(a, b)


def _adamw(w, m, v, g, name, n_parts=0, tr=256):
    rows, cols = w.shape
    tr = min(tr, rows)
    c1 = 1.0 - ADAM_B1 ** ADAM_STEP
    c2 = 1.0 - ADAM_B2 ** ADAM_STEP

    def body(w_ref, m_ref, v_ref, g_ref, go_ref, d_ref, mo_ref, vo_ref):
        if n_parts:
            gv = g_ref[0].astype(F32)
            for p in range(1, n_parts):
                gv = gv + g_ref[p].astype(F32)
        else:
            gv = g_ref[...]
        go_ref[...] = gv
        mn = ADAM_B1 * m_ref[...] + (1.0 - ADAM_B1) * gv
        vn = ADAM_B2 * v_ref[...] + (1.0 - ADAM_B2) * (gv * gv)
        mo_ref[...] = mn
        vo_ref[...] = vn
        d_ref[...] = -ADAM_LR * ((mn / c1) / (jnp.sqrt(vn / c2) + ADAM_EPS) + ADAM_WD * w_ref[...])

    blk = pl.BlockSpec((tr, cols), lambda i: (i, 0))
    g_spec = pl.BlockSpec((n_parts, tr, cols), lambda i: (0, i, 0)) if n_parts else blk
    return _call(
        body, name=name, grid=(rows // tr,),
        in_specs=[blk, blk, blk, g_spec], out_specs=[blk] * 4,
        out_shape=[_sds((rows, cols), F32)] * 4,
        compiler_params=_params(("parallel",)),
    )(w, m, v, g)


def _cols_to_full(blocks):
    n, r, c = blocks.shape
    return jnp.transpose(blocks, (1, 0, 2)).reshape(r, n * c)


def _full_to_cols(full, n=N_DEV):
    r, c = full.shape
    return jnp.transpose(full.reshape(r, n, c // n), (1, 0, 2))


def _pad_lanes(v, width):
    return jnp.pad(v, ((0, 0), (0, width - v.shape[1])))


def kernel(x, c, w_ada, b_ada, g_mix, w_in, w_dw, b_dw, g_conv_ln, b_conv_ln, g_q, g_k, w_out, g_ffn, w_gate, w_up, w_down, loss_target, m_w_ada, m_b_ada, m_g_mix, m_w_in, m_w_dw, m_b_dw, m_g_conv_ln, m_b_conv_ln, m_g_q, m_g_k, m_w_out, m_g_ffn, m_w_gate, m_w_up, m_w_down, v_w_ada, v_b_ada, v_g_mix, v_w_in, v_w_dw, v_b_dw, v_g_conv_ln, v_b_conv_ln, v_g_q, v_g_k, v_w_out, v_g_ffn, v_w_gate, v_w_up, v_w_down):
    n_seq, seq, _ = x.shape
    tokens = n_seq * seq
    me = 4 * lax.axis_index("x") + 2 * lax.axis_index("y") + lax.axis_index("c")
    ada_cols = w_ada.shape[2]
    dw_cols = w_dw.shape[2]

    (c_g, w_in_g, w_dw_g) = _exchange(
        [(c, False), (w_in[0].astype(BF16), False), (w_dw[0], False)], "gather_weights")
    c_all = c_g.reshape(N_DEV * n_seq, D_MODEL)
    w_in_f = _cols_to_full(w_in_g)
    w_dw_f = _cols_to_full(w_dw_g)

    b_cols = lax.dynamic_slice(b_ada, (0, me * ada_cols), (1, ada_cols))
    mod_cols = _ada_fwd(c_all, w_ada[0], b_cols)
    (mod_g,) = _exchange([(mod_cols, False)], "gather_mod")
    mod_mine = lax.dynamic_slice(mod_g, (0, me * n_seq, 0), (N_DEV, n_seq, ada_cols))
    mod = jnp.transpose(mod_mine, (1, 0, 2)).reshape(n_seq, N_MOD, D_MODEL)
    mod8 = jnp.pad(mod, ((0, 0), (0, 8 - N_MOD), (0, 0)))

    x2 = x.reshape(tokens, D_MODEL)
    h1, proj = _mix_in(x2, mod8, g_mix, w_in_f, seq)
    proj3 = proj.reshape(n_seq, seq, D_IN)
    uc3 = _conv_fwd(proj3, w_dw_f, b_dw)
    g_q2, g_k2 = jnp.tile(g_q, (1, 2)), jnp.tile(g_k, (1, 2))
    y_att3, lse3, w_out_g, w_gate_g, w_up_g, w_down_g = _attn_fwd(
        proj3, g_q2, g_k2,
        [(w_out[0].astype(BF16), False), (w_gate[0].astype(BF16), False), (w_up[0].astype(BF16), False),
         (w_down[0].astype(BF16), False)])
    w_out_f = w_out_g.reshape(D_MODEL, D_MODEL)
    w_gate_f = _cols_to_full(w_gate_g)
    w_up_f = _cols_to_full(w_up_g)
    w_down_f = w_down_g.reshape(D_FF, D_MODEL)
    uc2 = uc3.reshape(tokens, D_CONV)
    y_att2 = y_att3.reshape(tokens, D_ATT)
    y_conv, mix, x1, h2 = _mix_out(uc2, y_att2, x2, mod8, g_conv_ln, b_conv_ln, g_ffn, w_out_f, seq)
    gate, up, f = _ffn_fwd(h2, w_gate_f, w_up_f, w_down_f)
    dy, df, sq, dgate_f = _loss_head(f, x1, loss_target.reshape(tokens, D_MODEL), mod8, seq)

    dgate, dup, act, dh2 = _ffn_bwd(df, gate, up, w_gate_f, w_up_f, w_down_f)
    dx1, dmix, dg_ffn, dmod_f = _norm2_bwd(dh2, x1, dy, mix, mod8, g_ffn, seq)
    duc2, do2, dgb_ln = _mix_out_bwd(dmix, uc2, g_conv_ln, b_conv_ln, w_out_f)
    d_a3, d_g3, dw_dw_p, db_dw_p = _conv_bwd(duc2.reshape(n_seq, seq, D_CONV), proj3, w_dw_f)
    gw_gate = _grad_matmul(h2, dgate, "grad_w_gate", D_MODEL, D_FF // 2)
    gw_up = _grad_matmul(h2, dup, "grad_w_up", D_MODEL, D_FF // 2)
    gw_down = _grad_matmul(act, df, "grad_w_down", D_FF // 2, D_MODEL)
    d_q3, d_k3, d_v3, dg_qk, p_gate, p_up, p_down = _attn_bwd(
        proj3, do2.reshape(n_seq, seq, D_ATT), y_att3, lse3, g_q2, g_k2,
        [(_full_to_cols(gw_gate).astype(BF16), True), (_full_to_cols(gw_up).astype(BF16), True),
         (gw_down.astype(BF16).reshape(N_DEV, D_FF // N_DEV, D_MODEL), True)])
    flat = lambda t: t.reshape(tokens, t.shape[-1])
    d_a, d_g, d_q, d_k, d_v = flat(d_a3), flat(d_g3), flat(d_q3), flat(d_k3), flat(d_v3)
    grad_x2, dg_mix, dmod_m = _mix_in_bwd(d_a, d_g, d_q, d_k, d_v, w_in_f, x2, dx1, mod8, g_mix, seq)

    gw_in = jnp.concatenate(
        [_grad_matmul(h1, t, "grad_w_in_%d" % i, D_MODEL, D_CONV) for i, t in enumerate((d_a, d_g, d_q, d_k, d_v))], axis=1)
    gw_out = jnp.concatenate(
        [_grad_matmul(y_conv, dmix, "grad_w_out_conv", D_CONV, D_MODEL),
         _grad_matmul(y_att2, dmix, "grad_w_out_att", D_ATT, D_MODEL)], axis=0)

    dmod = jnp.concatenate([dmod_m[:, 0], dmod_m[:, 1], dmod_f[:, 2], dmod_f[:, 0], dmod_f[:, 1], dgate_f[:, 0]], axis=1)
    dg_q = dg_qk[0:1, 0:HEAD_DIM] + dg_qk[0:1, HEAD_DIM:]
    dg_k = dg_qk[1:2, 0:HEAD_DIM] + dg_qk[1:2, HEAD_DIM:]
    loss_part = (0.5 / D_MODEL) * jnp.sum(sq[0:1, :], axis=1, keepdims=True)
    small = jnp.concatenate(
        [dg_mix[0:1], dg_ffn[0:1], db_dw_p[0:1], dgb_ln[0:1], dgb_ln[1:2],
         _pad_lanes(dg_q, LANES), _pad_lanes(dg_k, LANES), _pad_lanes(loss_part, LANES)], axis=1)
    n_small = small.shape[1] - LANES

    (p_in, p_out, dmod_g, small_g, dw_g) = _exchange(
        [(_full_to_cols(gw_in).astype(BF16), True),
         (gw_out.astype(BF16).reshape(N_DEV, D_MODEL // N_DEV, D_MODEL), True),
         (dmod, False), (small, False), (dw_dw_p, False)], "scatter_grads")

    dmod_all = dmod_g.reshape(N_DEV * n_seq, N_MOD * D_MODEL)
    dmod_cols = lax.dynamic_slice(dmod_all, (0, me * ada_cols), (N_DEV * n_seq, ada_cols))
    gw_ada, gb_ada = _ada_bwd(c_all, dmod_cols, dmod_all)

    res = {}
    res["w_ada"] = _adamw(w_ada[0], m_w_ada[0], v_w_ada[0], gw_ada, "adamw_w_ada")
    res["b_ada"] = _adamw(b_ada, m_b_ada, v_b_ada, gb_ada, "adamw_b_ada")
    res["w_in"] = _adamw(w_in[0], m_w_in[0], v_w_in[0], p_in, "adamw_w_in", N_DEV)
    res["w_out"] = _adamw(w_out[0], m_w_out[0], v_w_out[0], p_out, "adamw_w_out", N_DEV)
    res["w_gate"] = _adamw(w_gate[0], m_w_gate[0], v_w_gate[0], p_gate, "adamw_w_gate", N_DEV)
    res["w_up"] = _adamw(w_up[0], m_w_up[0], v_w_up[0], p_up, "adamw_w_up", N_DEV)
    res["w_down"] = _adamw(w_down[0], m_w_down[0], v_w_down[0], p_down, "adamw_w_down", N_DEV, tr=176)
    dw_mine = lax.dynamic_slice(dw_g, (0, 0, me * dw_cols), (N_DEV, CONV_WIDTH, dw_cols))
    res["w_dw"] = _adamw(w_dw[0], m_w_dw[0], v_w_dw[0], dw_mine, "adamw_w_dw", N_DEV)

    small_names = ["g_mix", "g_ffn", "b_dw", "g_conv_ln", "b_conv_ln", "g_q", "g_k"]
    small_w = {"g_mix": (g_mix, m_g_mix, v_g_mix), "g_ffn": (g_ffn, m_g_ffn, v_g_ffn), "b_dw": (b_dw, m_b_dw, v_b_dw),
               "g_conv_ln": (g_conv_ln, m_g_conv_ln, v_g_conv_ln), "b_conv_ln": (b_conv_ln, m_b_conv_ln, v_b_conv_ln),
               "g_q": (g_q, m_g_q, v_g_q), "g_k": (g_k, m_g_k, v_g_k)}
    widths = [max(small_w[n][0].shape[1], LANES) for n in small_names]
    packed = [jnp.concatenate([_pad_lanes(small_w[n][i], wd) for n, wd in zip(small_names, widths)], axis=1) for i in range(3)]
    outs = _adamw(packed[0], packed[1], packed[2], small_g[:, :, :n_small], "adamw_small", N_DEV)
    off = 0
    for n, wd in zip(small_names, widths):
        real = small_w[n][0].shape[1]
        res[n] = tuple(o[:, off:off + real] for o in outs)
        off += wd
    loss = jnp.sum(small_g[:, 0, n_small])

    order = ["w_ada", "b_ada", "g_mix", "w_in", "w_dw", "b_dw", "g_conv_ln", "b_conv_ln", "g_q", "g_k",
             "w_out", "g_ffn", "w_gate", "w_up", "w_down"]
    lead = {"w_ada", "w_in", "w_dw", "w_out", "w_gate", "w_up", "w_down"}
    grads, deltas, new_m, new_v = [], [], [], []
    for n in order:
        g, d, mn, vn = res[n]
        g, d, mn, vn = (t[None] if n in lead else t for t in (g, d, mn, vn))
        grads.append(g)
        deltas.append(d)
        new_m.append(mn)
        new_v.append(vn)
    return (loss, grad_x2.reshape(n_seq, seq, D_MODEL), *grads, *deltas, *new_m, *new_v)
```

```python
import numpy as np
import jax
import jax.numpy as jnp
from jax import lax
from jax.experimental import pallas as pl
from jax.experimental.pallas import tpu as pltpu

F32 = jnp.float32
BF16 = jnp.bfloat16

N_DEV = 8
D_MODEL = 1024
D_CONV = 512
D_ATT = 512
HEAD_DIM = 64
CONV_WIDTH = 31
D_IN = 2 * D_CONV + 3 * D_ATT
D_FF = 2816
N_MOD = 6
EPS = 1e-6
RADIUS = 64
DILATIONS = (1, 4, 16)
Q_BLOCK = 128
LANES = 128
VMEM_LIMIT = 56 * 1024 * 1024

ADAM_LR = 0.001
ADAM_B1 = 0.9
ADAM_B2 = 0.999
ADAM_EPS = 1e-08
ADAM_WD = 0.01
ADAM_STEP = 10

NT = (((1,), (1,)), ((), ()))
TN = (((0,), (0,)), ((), ()))


def _call(body, **kw):
    return pl.pallas_call(body, **kw)


def _params(sem=None, vmem=VMEM_LIMIT):
    return pltpu.CompilerParams(dimension_semantics=sem, vmem_limit_bytes=vmem)


def _sig(x):
    return 1.0 / (1.0 + jnp.exp(-x))


def _sds(shape, dtype):
    return jax.ShapeDtypeStruct(shape, dtype)


N_PEER = N_DEV - 1
ANY_SPEC = pl.BlockSpec(memory_space=pl.ANY)


def _exchange_copies(scatter, ins, outs, *sems):
    n = len(ins)
    if n == 0:
        return [], []
    send_sems, recv_sems, local_sems = sems
    x, y, c = lax.axis_index("x"), lax.axis_index("y"), lax.axis_index("c")
    me = 4 * x + 2 * y + c

    def src(a, slot):
        return ins[a].at[slot] if scatter[a] else ins[a]

    local = [pltpu.make_async_copy(src(a, me), outs[a].at[me], local_sems.at[a]) for a in range(n)]
    flights = []
    for k in range(1, N_DEV):
        px = 1 - x if k & 4 else x
        py = 1 - y if k & 2 else y
        pc = 1 - c if k & 1 else c
        pid = 4 * px + 2 * py + pc
        for a in range(n):
            i = a * N_PEER + k - 1
            send, recv = (pltpu.make_async_remote_copy(
                src_ref=src(a, pid), dst_ref=outs[a].at[slot],
                send_sem=send_sems.at[i], recv_sem=recv_sems.at[i],
                device_id=(px, py, pc), device_id_type=pl.DeviceIdType.MESH) for slot in (me, pid))
            flights.append((send, recv))
    return local, flights


def _exchange_start(*args):
    local, flights = _exchange_copies(*args)
    for cp in local:
        cp.start()
    for send, _ in flights:
        send.start()


def _exchange_wait(*args):
    local, flights = _exchange_copies(*args)
    for send, recv in flights:
        send.wait_send()
        recv.wait_recv()
    for cp in local:
        cp.wait()


def _exchange_shapes(items):
    return [_sds((N_DEV,) + tuple(arr.shape[1:] if scatter else arr.shape), arr.dtype) for arr, scatter in items]


def _exchange_sems(n):
    if n == 0:
        return []
    return [pltpu.SemaphoreType.DMA((n * N_PEER,)), pltpu.SemaphoreType.DMA((n * N_PEER,)),
            pltpu.SemaphoreType.DMA((n,))]


def _exchange(items, name):
    n = len(items)
    scatter = [s for _, s in items]

    def body(*refs):
        args = (scatter, refs[:n], refs[n:2 * n]) + tuple(refs[2 * n:])
        _exchange_start(*args)
        _exchange_wait(*args)

    return _call(
        body, name=name, out_shape=_exchange_shapes(items),
        in_specs=[ANY_SPEC] * n, out_specs=[ANY_SPEC] * n, scratch_shapes=_exchange_sems(n),
    )(*[a for a, _ in items])


def _ada_fwd(c_all, w_ada, b_cols):
    def body(c_ref, w_ref, b_ref, o_ref):
        cv = c_ref[...]
        sc = (cv * _sig(cv)).astype(BF16)
        o_ref[...] = jnp.dot(sc, w_ref[...].astype(BF16), preferred_element_type=F32) + b_ref[...]

    return _call(body, name="ada_fwd", out_shape=_sds((c_all.shape[0], w_ada.shape[1]), F32),
                 compiler_params=_params())(c_all, w_ada, b_cols)


def _ada_bwd(c_all, dmod_cols, dmod_all):
    def body(c_ref, dc_ref, da_ref, gw_ref, gb_ref):
        cv = c_ref[...]
        sc = (cv * _sig(cv)).astype(BF16)
        gw_ref[...] = lax.dot_general(sc, dc_ref[...].astype(BF16), TN, preferred_element_type=F32)
        gb_ref[...] = jnp.sum(da_ref[...], axis=0, keepdims=True)

    return _call(body, name="ada_bwd",
                 out_shape=[_sds((c_all.shape[1], dmod_cols.shape[1]), F32), _sds((1, dmod_all.shape[1]), F32)],
                 compiler_params=_params())(c_all, dmod_cols, dmod_all)


def _mix_in(x2, mod8, g_mix, w_in, seq, tm=512):
    tokens = x2.shape[0]
    per_seq = seq // tm

    def body(x_ref, m_ref, g_ref, w_ref, h_ref, p_ref):
        xv = x_ref[...]
        r = lax.rsqrt(jnp.mean(xv * xv, axis=-1, keepdims=True) + EPS)
        h = (xv * r * g_ref[...]) * (1.0 + m_ref[1:2, :]) + m_ref[0:1, :]
        hb = h.astype(BF16)
        h_ref[...] = hb
        p_ref[...] = jnp.dot(hb, w_ref[...], preferred_element_type=F32)

    return _call(
        body, name="mix_in", grid=(tokens // tm,),
        in_specs=[pl.BlockSpec((tm, D_MODEL), lambda i: (i, 0)),
                  pl.BlockSpec((None, 8, D_MODEL), lambda i: (i // per_seq, 0, 0)),
                  pl.BlockSpec((1, D_MODEL), lambda i: (0, 0)),
                  pl.BlockSpec((D_MODEL, D_IN), lambda i: (0, 0))],
        out_specs=[pl.BlockSpec((tm, D_MODEL), lambda i: (i, 0)),
                   pl.BlockSpec((tm, D_IN), lambda i: (i, 0))],
        out_shape=[_sds((tokens, D_MODEL), BF16), _sds((tokens, D_IN), F32)],
        compiler_params=_params(("parallel",)),
    )(x2, mod8, g_mix, w_in)


CONV_ROWS = 32
CONV_HALO = 16


def _fill_shifted(xp, sh, seq):
    for b in range(8):
        sh[b, pl.ds(0, seq + 24), :] = xp[pl.ds(b, seq + 24), :]


def _conv_fwd(proj3, w_dw, b_dw):
    n_seq, seq, _ = proj3.shape
    n_cb = D_CONV // LANES

    def body(a_ref, g_ref, w_ref, b_ref, uc_ref, xp, sh):
        zeros = jnp.zeros((CONV_HALO, LANES), F32)
        xp[pl.ds(0, CONV_HALO), :] = zeros
        xp[pl.ds(CONV_HALO + seq, CONV_HALO), :] = zeros
        xp[pl.ds(CONV_HALO, seq), :] = a_ref[...] * _sig(g_ref[...])
        _fill_shifted(xp, sh, seq)

        def blk(i, carry):
            t0 = pl.multiple_of(i * CONV_ROWS, CONV_ROWS)
            acc = jnp.zeros((CONV_ROWS, LANES), F32)
            for j in range(CONV_WIDTH):
                jj = j + 1
                acc = acc + sh[jj % 8, pl.ds(t0 + 8 * (jj // 8), CONV_ROWS), :] * w_ref[j:j + 1, :]
            uc_ref[pl.ds(t0, CONV_ROWS), :] = acc + b_ref[...]
            return carry

        lax.fori_loop(0, seq // CONV_ROWS, blk, 0)

    return _call(
        body, name="conv_fwd", grid=(n_seq, n_cb),
        in_specs=[pl.BlockSpec((None, seq, LANES), lambda b, cb: (b, 0, cb)),
                  pl.BlockSpec((None, seq, LANES), lambda b, cb: (b, 0, n_cb + cb)),
                  pl.BlockSpec((CONV_WIDTH, LANES), lambda b, cb: (0, cb)),
                  pl.BlockSpec((1, LANES), lambda b, cb: (0, cb))],
        out_specs=pl.BlockSpec((None, seq, LANES), lambda b, cb: (b, 0, cb)),
        out_shape=_sds((n_seq, seq, D_CONV), F32),
        scratch_shapes=[pltpu.VMEM((seq + 2 * CONV_HALO, LANES), F32),
                        pltpu.VMEM((8, seq + 2 * CONV_HALO, LANES), F32)],
        compiler_params=_params(("parallel", "parallel")),
    )(proj3, proj3, w_dw, b_dw)


def _conv_bwd(duc3, proj3, w_dw):
    n_seq, seq, _ = proj3.shape
    n_cb = D_CONV // LANES

    def body(duc_ref, a_ref, g_ref, w_ref, da_ref, dg_ref, dw_ref, db_ref, xp, sh):
        @pl.when(pl.program_id(1) == 0)
        def _():
            dw_ref[...] = jnp.zeros_like(dw_ref)
            db_ref[...] = jnp.zeros_like(db_ref)

        zeros = jnp.zeros((CONV_HALO, LANES), F32)
        xp[pl.ds(0, CONV_HALO), :] = zeros
        xp[pl.ds(CONV_HALO + seq, CONV_HALO), :] = zeros
        xp[pl.ds(CONV_HALO, seq), :] = a_ref[...] * _sig(g_ref[...])
        _fill_shifted(xp, sh, seq)
        for j in range(CONV_WIDTH):
            jj = j + 1

            def wblk(i, acc, jj=jj):
                t0 = pl.multiple_of(i * CONV_ROWS, CONV_ROWS)
                return acc + duc_ref[pl.ds(t0, CONV_ROWS), :] * sh[jj % 8, pl.ds(t0 + 8 * (jj // 8), CONV_ROWS), :]

            acc = lax.fori_loop(0, seq // CONV_ROWS, wblk, jnp.zeros((CONV_ROWS, LANES), F32))
            dw_ref[j:j + 1, :] += jnp.sum(acc, axis=0, keepdims=True)
        db_ref[0:1, :] += jnp.sum(duc_ref[...], axis=0, keepdims=True)
        xp[pl.ds(CONV_HALO, seq), :] = duc_ref[...]
        _fill_shifted(xp, sh, seq)

        def ublk(i, carry):
            t0 = pl.multiple_of(i * CONV_ROWS, CONV_ROWS)
            acc = jnp.zeros((CONV_ROWS, LANES), F32)
            for j in range(CONV_WIDTH):
                jj = CONV_WIDTH - j
                acc = acc + sh[jj % 8, pl.ds(t0 + 8 * (jj // 8), CONV_ROWS), :] * w_ref[j:j + 1, :]
            av = a_ref[pl.ds(t0, CONV_ROWS), :]
            sg = _sig(g_ref[pl.ds(t0, CONV_ROWS), :])
            da_ref[pl.ds(t0, CONV_ROWS), :] = (acc * sg).astype(BF16)
            dg_ref[pl.ds(t0, CONV_ROWS), :] = (acc * av * sg * (1.0 - sg)).astype(BF16)
            return carry

        lax.fori_loop(0, seq // CONV_ROWS, ublk, 0)

    return _call(
        body, name="conv_bwd", grid=(n_cb, n_seq),
        in_specs=[pl.BlockSpec((None, seq, LANES), lambda cb, b: (b, 0, cb)),
                  pl.BlockSpec((None, seq, LANES), lambda cb, b: (b, 0, cb)),
                  pl.BlockSpec((None, seq, LANES), lambda cb, b: (b, 0, n_cb + cb)),
                  pl.BlockSpec((CONV_WIDTH, LANES), lambda cb, b: (0, cb))],
        out_specs=[pl.BlockSpec((None, seq, LANES), lambda cb, b: (b, 0, cb)),
                   pl.BlockSpec((None, seq, LANES), lambda cb, b: (b, 0, cb)),
                   pl.BlockSpec((32, LANES), lambda cb, b: (0, cb)),
                   pl.BlockSpec((8, LANES), lambda cb, b: (0, cb))],
        out_shape=[_sds((n_seq, seq, D_CONV), BF16), _sds((n_seq, seq, D_CONV), BF16),
                   _sds((32, D_CONV), F32), _sds((8, D_CONV), F32)],
        scratch_shapes=[pltpu.VMEM((seq + 2 * CONV_HALO, LANES), F32),
                        pltpu.VMEM((8, seq + 2 * CONV_HALO, LANES), F32)],
        compiler_params=_params(("parallel", "arbitrary")),
    )(duc3, proj3, proj3, w_dw)


MASKED = 1e30
ATT_UNROLL = 4


def _distance_mats(dil, seg_len):
    kw = min(2 * Q_BLOCK, seg_len)
    offsets = (0, -RADIUS, -2 * RADIUS) if kw == 2 * Q_BLOCK else (0,)
    a = np.arange(Q_BLOCK)[:, None]
    b = np.arange(kw)[None, :]
    mats = []
    for off in offsets:
        rel = np.abs(b + off - a)
        mats.append(np.where(rel <= RADIUS, dil * rel, MASKED))
    return jnp.asarray(np.stack(mats).astype(np.float32))


def _alibi_rows():
    s = np.zeros((4, 8, LANES), np.float32)
    for hp in range(4):
        for hl in range(2):
            s[hp, hl, :] = 2.0 ** (-(2 * hp + hl + 1))
    return jnp.asarray(s)


def _window(n, seg_len):
    i0 = pl.multiple_of(n * Q_BLOCK, Q_BLOCK)
    if seg_len <= Q_BLOCK:
        return i0, i0, 0
    per_seg = seg_len // Q_BLOCK
    j = n % per_seg
    seg0 = (n // per_seg) * seg_len
    ks_local = jnp.clip(j * Q_BLOCK - RADIUS, 0, seg_len - 2 * Q_BLOCK)
    ks = pl.multiple_of(seg0 + ks_local, RADIUS)
    var = jnp.where(j == 0, 0, jnp.where(j == per_seg - 1, 2, 1))
    return i0, ks, var


def _first_head(rows):
    return lax.broadcasted_iota(jnp.int32, (rows, LANES), 1) < HEAD_DIM


def _head_mean(x, first):
    s0 = jnp.sum(jnp.where(first, x, 0.0), axis=1, keepdims=True)
    s1 = jnp.sum(jnp.where(first, 0.0, x), axis=1, keepdims=True)
    return jnp.where(first, s0, s1) * (1.0 / HEAD_DIM)


def _permute_rows(dst, src, dil, seq):
    seg = seq // dil
    for r in range(dil):
        rows = pl.ds(r, seg, stride=dil) if dil > 1 else pl.ds(0, seq)
        dst[pl.ds(r * seg, seg), :] = src[rows, :].astype(dst.dtype)


def _permute_rows_by_head(dst, src, dil, seq):
    seg = seq // dil
    first = _first_head(seg)
    for r in range(dil):
        rows = pl.ds(r, seg, stride=dil) if dil > 1 else pl.ds(0, seq)
        val = src[rows, :]
        dst[0, pl.ds(r * seg, seg), :] = jnp.where(first, val, 0.0).astype(dst.dtype)
        dst[1, pl.ds(r * seg, seg), :] = jnp.where(first, 0.0, val).astype(dst.dtype)


def _qk_normalise(q_ref, g2_ref, dst, seq, scale):
    def chunk(ci, carry):
        rows = pl.ds(pl.multiple_of(ci * 256, 256), 256)
        qv = q_ref[rows, :]
        r = lax.rsqrt(_head_mean(qv * qv, _first_head(256)) + EPS)
        dst[rows, :] = qv * r * (g2_ref[...] * scale)
        return carry

    lax.fori_loop(0, seq // 256, chunk, 0)


def _attn_fwd(proj3, g_q2, g_k2, ride):
    n_seq, seq, _ = proj3.shape
    dms = [_distance_mats(d, seq // d) for d in DILATIONS]
    col0 = 2 * D_CONV // LANES
    n_hp = D_ATT // LANES

    n_ride = len(ride)
    ride_scatter = [s for _, s in ride]

    def body(*refs):
        q_ref, k_ref, v_ref, gq_ref, gk_ref, sl_ref, dm1, dm4, dm16 = refs[:9]
        ride_in = refs[9:9 + n_ride]
        y_ref, lse_ref = refs[9 + n_ride:11 + n_ride]
        ride_out = refs[11 + n_ride:11 + 2 * n_ride]
        (qf, kf, qp, kp, vp, oml_p, o1, o4, o16, m1, m4, m16, l1, l4, l16) = refs[11 + 2 * n_ride:26 + 2 * n_ride]
        o_nat, m_nat, l_nat = (o1, o4, o16), (m1, m4, m16), (l1, l4, l16)
        ride_args = (ride_scatter, ride_in, ride_out) + tuple(refs[26 + 2 * n_ride:])
        step = pl.program_id(0) * n_hp + pl.program_id(1)

        @pl.when(step == 0)
        def _():
            _exchange_start(*ride_args)

        dm_refs = (dm1, dm4, dm16)
        _qk_normalise(q_ref, gq_ref, qf, seq, HEAD_DIM ** -0.5)
        _qk_normalise(k_ref, gk_ref, kf, seq, 1.0)
        slopes = (sl_ref[0:1, 0:1], sl_ref[1:2, 0:1])
        for pi, dil in enumerate(DILATIONS):
            seg = seq // dil
            kw = min(2 * Q_BLOCK, seg)
            _permute_rows_by_head(qp, qf, dil, seq)
            _permute_rows(kp, kf, dil, seq)
            _permute_rows(vp, v_ref, dil, seq)

            def blk(it, carry, seg=seg, kw=kw, pi=pi, dst=oml_p):
                first = _first_head(Q_BLOCK)
                chains = [(sub, h) for sub in range(ATT_UNROLL) for h in range(2)]
                win = [_window(it * ATT_UNROLL + sub, seg) for sub in range(ATT_UNROLL)]
                s = {}
                for sub, h in chains:
                    i0, ks, var = win[sub]
                    s[sub, h] = lax.dot_general(qp[h, pl.ds(i0, Q_BLOCK), :], kp[pl.ds(ks, kw), :], NT,
                                                preferred_element_type=F32) - slopes[h] * dm_refs[pi][var]
                m, l, p = {}, {}, {}
                for c in chains:
                    m[c] = jnp.max(s[c], axis=1, keepdims=True)
                    e = jnp.exp(s[c] - m[c])
                    l[c] = jnp.sum(e, axis=1, keepdims=True)
                    p[c] = e.astype(BF16)
                o = {}
                for sub, h in chains:
                    o[sub, h] = jnp.dot(p[sub, h], vp[pl.ds(win[sub][1], kw), :], preferred_element_type=F32)
                packed = [jnp.concatenate([jnp.where(first, t[sub, 0], t[sub, 1]) for t in (o, m, l)], axis=1)
                          for sub in range(ATT_UNROLL)]
                rows = pl.ds(pl.multiple_of(it * (ATT_UNROLL * Q_BLOCK), ATT_UNROLL * Q_BLOCK), ATT_UNROLL * Q_BLOCK)
                dst[rows, :] = jnp.concatenate(packed, axis=0)
                return carry

            lax.fori_loop(0, seq // (Q_BLOCK * ATT_UNROLL), blk, 0)
            for r in range(dil):
                nat = pl.ds(r, seg, stride=dil) if dil > 1 else pl.ds(0, seq)
                perm = pl.ds(r * seg, seg)
                o_nat[pi][nat, :] = oml_p[perm, pl.ds(0, LANES)]
                m_nat[pi][nat, :] = oml_p[perm, pl.ds(LANES, LANES)]
                l_nat[pi][nat, :] = oml_p[perm, pl.ds(2 * LANES, LANES)]

        def merge(ci, carry):
            rows = pl.ds(pl.multiple_of(ci * 256, 256), 256)
            ms = [m_nat[pi][rows, :] for pi in range(3)]
            m_all = jnp.maximum(jnp.maximum(ms[0], ms[1]), ms[2])
            es = [jnp.exp(m - m_all) for m in ms]
            l_all = sum(l_nat[pi][rows, :] * es[pi] for pi in range(3))
            inv = 1.0 / l_all
            o = sum(o_nat[pi][rows, :] * (es[pi] * inv) for pi in range(3))
            y_ref[rows, :] = o.astype(BF16)
            lse_ref[rows, :] = m_all + jnp.log(l_all)
            return carry

        lax.fori_loop(0, seq // 256, merge, 0)

        @pl.when(step == n_seq * n_hp - 1)
        def _():
            _exchange_wait(*ride_args)

    def col(off):
        return pl.BlockSpec((None, seq, LANES), lambda b, hp: (b, 0, col0 + off * n_hp + hp))

    def whole(arr):
        return pl.BlockSpec(arr.shape, lambda b, hp: (0,) * arr.ndim)

    rows_f32 = pltpu.VMEM((seq, LANES), F32)
    rows_bf16 = pltpu.VMEM((seq, LANES), BF16)
    return _call(
        body, name="attn_fwd", grid=(n_seq, n_hp),
        in_specs=[col(0), col(1), col(2), whole(g_q2), whole(g_k2),
                  pl.BlockSpec((None, 8, LANES), lambda b, hp: (hp, 0, 0)),
                  whole(dms[0]), whole(dms[1]), whole(dms[2])] + [ANY_SPEC] * n_ride,
        out_specs=[pl.BlockSpec((None, seq, LANES), lambda b, hp: (b, 0, hp)),
                   pl.BlockSpec((None, seq, LANES), lambda b, hp: (b, 0, hp))] + [ANY_SPEC] * n_ride,
        out_shape=[_sds((n_seq, seq, D_ATT), BF16), _sds((n_seq, seq, D_ATT), F32)] + _exchange_shapes(ride),
        scratch_shapes=[rows_f32, rows_f32, pltpu.VMEM((2, seq, LANES), BF16), rows_bf16, rows_bf16]
        + [pltpu.VMEM((seq, 3 * LANES), F32)] + [rows_f32] * 9 + _exchange_sems(n_ride),
        compiler_params=_params(("arbitrary", "arbitrary")),
    )(proj3, proj3, proj3, g_q2, g_k2, _alibi_rows(), *dms, *[a for a, _ in ride])


def _attn_bwd(proj3, do3, y_att3, lse3, g_q2, g_k2, ride):
    n_seq, seq, _ = proj3.shape
    dms = [_distance_mats(d, seq // d) for d in DILATIONS]
    col0 = 2 * D_CONV // LANES
    n_hp = D_ATT // LANES

    n_ride = len(ride)
    ride_scatter = [s for _, s in ride]

    def body(*refs):
        q_ref, k_ref, v_ref, do_ref, o_ref, lse_ref, gq_ref, gk_ref, sl_ref, dm1, dm4, dm16 = refs[:12]
        ride_in = refs[12:12 + n_ride]
        dq_ref, dk_ref, dv_ref, dg_ref = refs[12 + n_ride:16 + n_ride]
        ride_out = refs[16 + n_ride:16 + 2 * n_ride]
        (qf, kf, qp, dop, kp, vp, sn, sp, dqp, dkp, dvp, dqn, dkn, dvn) = refs[16 + 2 * n_ride:30 + 2 * n_ride]
        ride_args = (ride_scatter, ride_in, ride_out) + tuple(refs[30 + 2 * n_ride:])
        dm_refs = (dm1, dm4, dm16)
        step = pl.program_id(0) * n_hp + pl.program_id(1)

        @pl.when(step == 0)
        def _():
            _exchange_start(*ride_args)
            dg_ref[...] = jnp.zeros_like(dg_ref)

        _qk_normalise(q_ref, gq_ref, qf, seq, HEAD_DIM ** -0.5)
        _qk_normalise(k_ref, gk_ref, kf, seq, 1.0)

        def stats(ci, carry):
            rows = pl.ds(pl.multiple_of(ci * 256, 256), 256)
            first = _first_head(256)
            prod = do_ref[rows, :] * o_ref[rows, :].astype(F32)
            sn[rows, 0:1] = lse_ref[rows, 0:1]
            sn[rows, 1:2] = lse_ref[rows, HEAD_DIM:HEAD_DIM + 1]
            sn[rows, 2:3] = jnp.sum(jnp.where(first, prod, 0.0), axis=1, keepdims=True)
            sn[rows, 3:4] = jnp.sum(jnp.where(first, 0.0, prod), axis=1, keepdims=True)
            return carry

        lax.fori_loop(0, seq // 256, stats, 0)
        slopes = (sl_ref[0:1, 0:1], sl_ref[1:2, 0:1])
        half = seq // (Q_BLOCK * ATT_UNROLL)
        region = seq // ATT_UNROLL

        for pi, dil in enumerate(DILATIONS):
            seg = seq // dil
            kw = min(2 * Q_BLOCK, seg)
            _permute_rows_by_head(qp, qf, dil, seq)
            _permute_rows_by_head(dop, do_ref, dil, seq)
            _permute_rows(kp, kf, dil, seq)
            _permute_rows(vp, v_ref, dil, seq)
            _permute_rows(sp, sn, dil, seq)
            for sub in range(ATT_UNROLL):
                lo, hi = max(sub * region - RADIUS, 0), min((sub + 1) * region + RADIUS, seq)
                dkp[sub, pl.ds(lo, hi - lo), :] = jnp.zeros((hi - lo, LANES), F32)
                dvp[sub, pl.ds(lo, hi - lo), :] = jnp.zeros((hi - lo, LANES), F32)

            def blk(it, carry, seg=seg, kw=kw, pi=pi):
                first = _first_head(Q_BLOCK)
                chains = [(sub, h) for sub in range(ATT_UNROLL) for h in range(2)]
                win = [_window(it + sub * half, seg) for sub in range(ATT_UNROLL)]
                qrows = [pl.ds(w[0], Q_BLOCK) for w in win]
                krows = [pl.ds(w[1], kw) for w in win]
                s, dp = {}, {}
                for sub, h in chains:
                    s[sub, h] = lax.dot_general(qp[h, qrows[sub], :], kp[krows[sub], :], NT,
                                                preferred_element_type=F32) - slopes[h] * dm_refs[pi][win[sub][2]]
                    dp[sub, h] = lax.dot_general(dop[h, qrows[sub], :], vp[krows[sub], :], NT,
                                                 preferred_element_type=F32)
                p, ds = {}, {}
                for sub, h in chains:
                    e = jnp.exp(s[sub, h] - sp[qrows[sub], h:h + 1])
                    ds[sub, h] = (e * (dp[sub, h] - sp[qrows[sub], 2 + h:3 + h])).astype(BF16)
                    p[sub, h] = e.astype(BF16)
                dq, dk, dv = {}, {}, {}
                for sub, h in chains:
                    dq[sub, h] = jnp.dot(ds[sub, h], kp[krows[sub], :], preferred_element_type=F32)
                    dk[sub, h] = lax.dot_general(ds[sub, h], qp[h, qrows[sub], :], TN, preferred_element_type=F32)
                    dv[sub, h] = lax.dot_general(p[sub, h], dop[h, qrows[sub], :], TN, preferred_element_type=F32)
                for sub in range(ATT_UNROLL):
                    dqp[qrows[sub], :] = jnp.where(first, dq[sub, 0], dq[sub, 1])
                    dkp[sub, krows[sub], :] += dk[sub, 0] + dk[sub, 1]
                    dvp[sub, krows[sub], :] += dv[sub, 0] + dv[sub, 1]
                return carry

            lax.fori_loop(0, half, blk, 0)
            if dil == 1:
                for sub in range(ATT_UNROLL):
                    r0 = sub * region
                    pieces = [(r0, RADIUS, [sub - 1, sub] if sub > 0 else [sub]),
                              (r0 + RADIUS, region - 2 * RADIUS, [sub]),
                              (r0 + region - RADIUS, RADIUS, [sub, sub + 1] if sub < ATT_UNROLL - 1 else [sub])]
                    for start, size, owners in pieces:
                        rows = pl.ds(start, size)
                        dqn[rows, :] = dqp[rows, :]
                        dkn[rows, :] = sum(dkp[o, rows, :] for o in owners)
                        dvn[rows, :] = sum(dvp[o, rows, :] for o in owners)
            else:
                for r in range(dil):
                    nat, perm = pl.ds(r, seg, stride=dil), pl.ds(r * seg, seg)
                    owner = (r * seg) // region
                    dqn[nat, :] += dqp[perm, :]
                    dkn[nat, :] += dkp[owner, perm, :]
                    dvn[nat, :] += dvp[owner, perm, :]

        def finish(ci, carry):
            rows = pl.ds(pl.multiple_of(ci * 256, 256), 256)
            first = _first_head(256)
            for src_ref, g_ref, dn, dst_ref, scale, row in (
                    (q_ref, gq_ref, dqn, dq_ref, HEAD_DIM ** -0.5, 0), (k_ref, gk_ref, dkn, dk_ref, 1.0, 1)):
                xv = src_ref[rows, :]
                r = lax.rsqrt(_head_mean(xv * xv, first) + EPS)
                xhat = xv * r
                d = dn[rows, :] * scale
                dg_ref[row:row + 1, :] += jnp.sum(d * xhat, axis=0, keepdims=True)
                dxh = d * g_ref[...]
                dst_ref[rows, :] = (r * (dxh - xhat * _head_mean(dxh * xhat, first))).astype(BF16)
            dv_ref[rows, :] = dvn[rows, :].astype(BF16)
            return carry

        lax.fori_loop(0, seq // 256, finish, 0)

        @pl.when(step == n_seq * n_hp - 1)
        def _():
            _exchange_wait(*ride_args)

    def col(off):
        return pl.BlockSpec((None, seq, LANES), lambda b, hp: (b, 0, col0 + off * n_hp + hp))

    def whole(arr):
        return pl.BlockSpec(arr.shape, lambda b, hp: (0,) * arr.ndim)

    att = pl.BlockSpec((None, seq, LANES), lambda b, hp: (b, 0, hp))
    rows_f32 = pltpu.VMEM((seq, LANES), F32)
    rows_bf16 = pltpu.VMEM((seq, LANES), BF16)
    by_head_bf16 = pltpu.VMEM((2, seq, LANES), BF16)
    per_sub_f32 = pltpu.VMEM((ATT_UNROLL, seq, LANES), F32)
    return _call(
        body, name="attn_bwd", grid=(n_seq, n_hp),
        in_specs=[col(0), col(1), col(2), att, att, att, whole(g_q2), whole(g_k2),
                  pl.BlockSpec((None, 8, LANES), lambda b, hp: (hp, 0, 0)),
                  whole(dms[0]), whole(dms[1]), whole(dms[2])] + [ANY_SPEC] * n_ride,
        out_specs=[att, att, att, pl.BlockSpec((8, LANES), lambda b, hp: (0, 0))] + [ANY_SPEC] * n_ride,
        out_shape=[_sds((n_seq, seq, D_ATT), BF16)] * 3 + [_sds((8, LANES), F32)] + _exchange_shapes(ride),
        scratch_shapes=[rows_f32, rows_f32, by_head_bf16, by_head_bf16, rows_bf16, rows_bf16, rows_f32, rows_f32,
                        rows_f32, per_sub_f32, per_sub_f32, rows_f32, rows_f32, rows_f32] + _exchange_sems(n_ride),
        compiler_params=_params(("arbitrary", "arbitrary")),
    )(proj3, proj3, proj3, do3, y_att3, lse3, g_q2, g_k2, _alibi_rows(), *dms, *[a for a, _ in ride])


def _mix_out(uc2, y_att2, x2, mod8, g_ln, b_ln, g_ffn, w_out, seq, tm=512):
    tokens = x2.shape[0]
    per_seq = seq // tm

    def body(uc_ref, ya_ref, x_ref, m_ref, gl_ref, bl_ref, gf_ref, w_ref, yc_ref, mix_ref, x1_ref, h2_ref):
        uc = uc_ref[...]
        mu = jnp.mean(uc, axis=-1, keepdims=True)
        cen = uc - mu
        rs = lax.rsqrt(jnp.mean(cen * cen, axis=-1, keepdims=True) + EPS)
        z = cen * rs * gl_ref[...] + bl_ref[...]
        yc = (z * _sig(z)).astype(BF16)
        yc_ref[...] = yc
        mix = (jnp.dot(yc, w_ref[pl.ds(0, D_CONV), :], preferred_element_type=F32)
               + jnp.dot(ya_ref[...], w_ref[pl.ds(D_CONV, D_ATT), :], preferred_element_type=F32))
        mix_ref[...] = mix
        x1 = x_ref[...] + m_ref[2:3, :] * mix
        x1_ref[...] = x1
        r = lax.rsqrt(jnp.mean(x1 * x1, axis=-1, keepdims=True) + EPS)
        h2_ref[...] = ((x1 * r * gf_ref[...]) * (1.0 + m_ref[4:5, :]) + m_ref[3:4, :]).astype(BF16)

    def rows(width):
        return pl.BlockSpec((tm, width), lambda i: (i, 0))

    def vec(width):
        return pl.BlockSpec((1, width), lambda i: (0, 0))

    return _call(
        body, name="mix_out", grid=(tokens // tm,),
        in_specs=[rows(D_CONV), rows(D_ATT), rows(D_MODEL),
                  pl.BlockSpec((None, 8, D_MODEL), lambda i: (i // per_seq, 0, 0)),
                  vec(D_CONV), vec(D_CONV), vec(D_MODEL),
                  pl.BlockSpec((D_MODEL, D_MODEL), lambda i: (0, 0))],
        out_specs=[rows(D_CONV), rows(D_MODEL), rows(D_MODEL), rows(D_MODEL)],
        out_shape=[_sds((tokens, D_CONV), BF16), _sds((tokens, D_MODEL), F32),
                   _sds((tokens, D_MODEL), F32), _sds((tokens, D_MODEL), BF16)],
        compiler_params=_params(("parallel",)),
    )(uc2, y_att2, x2, mod8, g_ln, b_ln, g_ffn, w_out)


def _mix_out_bwd(dmix, uc2, g_ln, b_ln, w_out, tm=512):
    tokens = dmix.shape[0]

    def body(dm_ref, uc_ref, gl_ref, bl_ref, w_ref, duc_ref, do_ref, dgb_ref):
        @pl.when(pl.program_id(0) == 0)
        def _():
            dgb_ref[...] = jnp.zeros_like(dgb_ref)

        dmv = dm_ref[...]
        dyc = lax.dot_general(dmv, w_ref[pl.ds(0, D_CONV), :], NT, preferred_element_type=F32)
        do_ref[...] = lax.dot_general(dmv, w_ref[pl.ds(D_CONV, D_ATT), :], NT, preferred_element_type=F32)
        uc = uc_ref[...]
        mu = jnp.mean(uc, axis=-1, keepdims=True)
        cen = uc - mu
        rs = lax.rsqrt(jnp.mean(cen * cen, axis=-1, keepdims=True) + EPS)
        xh = cen * rs
        z = xh * gl_ref[...] + bl_ref[...]
        sg = _sig(z)
        dz = dyc * (sg * (1.0 + z * (1.0 - sg)))
        dgb_ref[0:1, :] += jnp.sum(dz * xh, axis=0, keepdims=True)
        dgb_ref[1:2, :] += jnp.sum(dz, axis=0, keepdims=True)
        dxh = dz * gl_ref[...]
        duc_ref[...] = rs * (dxh - jnp.mean(dxh, axis=-1, keepdims=True)
                             - xh * jnp.mean(dxh * xh, axis=-1, keepdims=True))

    return _call(
        body, name="mix_out_bwd", grid=(tokens // tm,),
        in_specs=[pl.BlockSpec((tm, D_MODEL), lambda i: (i, 0)),
                  pl.BlockSpec((tm, D_CONV), lambda i: (i, 0)),
                  pl.BlockSpec((1, D_CONV), lambda i: (0, 0)),
                  pl.BlockSpec((1, D_CONV), lambda i: (0, 0)),
                  pl.BlockSpec((D_MODEL, D_MODEL), lambda i: (0, 0))],
        out_specs=[pl.BlockSpec((tm, D_CONV), lambda i: (i, 0)),
                   pl.BlockSpec((tm, D_ATT), lambda i: (i, 0)),
                   pl.BlockSpec((8, D_CONV), lambda i: (0, 0))],
        out_shape=[_sds((tokens, D_CONV), F32), _sds((tokens, D_ATT), F32), _sds((8, D_CONV), F32)],
        compiler_params=_params(("arbitrary",)),
    )(dmix, uc2, g_ln, b_ln, w_out)


FF_TILE = 256


def _ffn_fwd(h2, w_gate, w_up, w_down, tm=1024):
    tokens = h2.shape[0]

    def body(h_ref, wg_ref, wu_ref, wd_ref, gate_ref, up_ref, f_ref):
        @pl.when(pl.program_id(1) == 0)
        def _():
            f_ref[...] = jnp.zeros_like(f_ref)

        hv = h_ref[...]
        gate = jnp.dot(hv, wg_ref[...], preferred_element_type=F32)
        up = jnp.dot(hv, wu_ref[...], preferred_element_type=F32)
        gate_ref[...] = gate
        up_ref[...] = up
        act = (gate * _sig(gate) * up).astype(BF16)
        f_ref[...] += jnp.dot(act, wd_ref[...], preferred_element_type=F32)

    return _call(
        body, name="ffn_fwd", grid=(tokens // tm, D_FF // FF_TILE),
        in_specs=[pl.BlockSpec((tm, D_MODEL), lambda i, j: (i, 0)),
                  pl.BlockSpec((D_MODEL, FF_TILE), lambda i, j: (0, j)),
                  pl.BlockSpec((D_MODEL, FF_TILE), lambda i, j: (0, j)),
                  pl.BlockSpec((FF_TILE, D_MODEL), lambda i, j: (j, 0))],
        out_specs=[pl.BlockSpec((tm, FF_TILE), lambda i, j: (i, j)),
                   pl.BlockSpec((tm, FF_TILE), lambda i, j: (i, j)),
                   pl.BlockSpec((tm, D_MODEL), lambda i, j: (i, 0))],
        out_shape=[_sds((tokens, D_FF), F32), _sds((tokens, D_FF), F32), _sds((tokens, D_MODEL), F32)],
        compiler_params=_params(("parallel", "arbitrary")),
    )(h2, w_gate, w_up, w_down)


def _ffn_bwd(df, gate, up, w_gate, w_up, w_down, tm=1024):
    tokens = df.shape[0]

    def body(df_ref, gate_ref, up_ref, wg_ref, wu_ref, wd_ref, dgate_ref, dup_ref, act_ref, dh_ref):
        @pl.when(pl.program_id(1) == 0)
        def _():
            dh_ref[...] = jnp.zeros_like(dh_ref)

        dact = lax.dot_general(df_ref[...], wd_ref[...], NT, preferred_element_type=F32)
        gate = gate_ref[...]
        up = up_ref[...]
        sg = _sig(gate)
        silu = gate * sg
        act_ref[...] = (silu * up).astype(BF16)
        dup = (dact * silu).astype(BF16)
        dgate = (dact * up * (sg * (1.0 + gate * (1.0 - sg)))).astype(BF16)
        dup_ref[...] = dup
        dgate_ref[...] = dgate
        dh_ref[...] += (lax.dot_general(dgate, wg_ref[...], NT, preferred_element_type=F32)
                        + lax.dot_general(dup, wu_ref[...], NT, preferred_element_type=F32))

    tile = pl.BlockSpec((tm, FF_TILE), lambda i, j: (i, j))
    return _call(
        body, name="ffn_bwd", grid=(tokens // tm, D_FF // FF_TILE),
        in_specs=[pl.BlockSpec((tm, D_MODEL), lambda i, j: (i, 0)), tile, tile,
                  pl.BlockSpec((D_MODEL, FF_TILE), lambda i, j: (0, j)),
                  pl.BlockSpec((D_MODEL, FF_TILE), lambda i, j: (0, j)),
                  pl.BlockSpec((FF_TILE, D_MODEL), lambda i, j: (j, 0))],
        out_specs=[tile, tile, tile, pl.BlockSpec((tm, D_MODEL), lambda i, j: (i, 0))],
        out_shape=[_sds((tokens, D_FF), BF16)] * 3 + [_sds((tokens, D_MODEL), F32)],
        compiler_params=_params(("parallel", "arbitrary")),
    )(df, gate, up, w_gate, w_up, w_down)


def _loss_head(f, x1, target, mod8, seq, tm=512):
    tokens = f.shape[0]
    per_seq = seq // tm
    n_seq = tokens // seq

    def body(f_ref, x1_ref, t_ref, m_ref, dy_ref, df_ref, sq_ref, dgf_ref):
        i = pl.program_id(0)

        @pl.when(i == 0)
        def _():
            sq_ref[...] = jnp.zeros_like(sq_ref)

        @pl.when(i % per_seq == 0)
        def _():
            dgf_ref[...] = jnp.zeros_like(dgf_ref)

        fv = f_ref[...]
        gate = m_ref[5:6, :]
        diff = x1_ref[...] + gate * fv - t_ref[...]
        sq_ref[0:1, :] += jnp.sum(diff * diff, axis=0, keepdims=True)
        dy = diff * (1.0 / D_MODEL)
        dy_ref[...] = dy
        df_ref[...] = (gate * dy).astype(BF16)
        dgf_ref[0:1, :] += jnp.sum(dy * fv, axis=0, keepdims=True)

    rows = pl.BlockSpec((tm, D_MODEL), lambda i: (i, 0))
    per = pl.BlockSpec((None, 8, D_MODEL), lambda i: (i // per_seq, 0, 0))
    return _call(
        body, name="loss_head", grid=(tokens // tm,),
        in_specs=[rows, rows, rows, per],
        out_specs=[rows, rows, pl.BlockSpec((8, D_MODEL), lambda i: (0, 0)), per],
        out_shape=[_sds((tokens, D_MODEL), F32), _sds((tokens, D_MODEL), BF16),
                   _sds((8, D_MODEL), F32), _sds((n_seq, 8, D_MODEL), F32)],
        compiler_params=_params(("arbitrary",)),
    )(f, x1, target, mod8)


def _norm2_bwd(dh2, x1, dy, mix, mod8, g_ffn, seq, tm=512):
    tokens = dh2.shape[0]
    per_seq = seq // tm
    n_seq = tokens // seq

    def body(dh_ref, x1_ref, dy_ref, mix_ref, m_ref, g_ref, dx1_ref, dmix_ref, dg_ref, dm_ref):
        i = pl.program_id(0)

        @pl.when(i == 0)
        def _():
            dg_ref[...] = jnp.zeros_like(dg_ref)

        @pl.when(i % per_seq == 0)
        def _():
            dm_ref[...] = jnp.zeros_like(dm_ref)

        dh = dh_ref[...]
        x1 = x1_ref[...]
        r = lax.rsqrt(jnp.mean(x1 * x1, axis=-1, keepdims=True) + EPS)
        xhat = x1 * r
        g = g_ref[...]
        dm_ref[0:1, :] += jnp.sum(dh, axis=0, keepdims=True)
        dm_ref[1:2, :] += jnp.sum(dh * (xhat * g), axis=0, keepdims=True)
        dn = dh * (1.0 + m_ref[4:5, :])
        dg_ref[0:1, :] += jnp.sum(dn * xhat, axis=0, keepdims=True)
        dxh = dn * g
        dx1 = dy_ref[...] + r * (dxh - xhat * jnp.mean(dxh * xhat, axis=-1, keepdims=True))
        dx1_ref[...] = dx1
        dm_ref[2:3, :] += jnp.sum(dx1 * mix_ref[...], axis=0, keepdims=True)
        dmix_ref[...] = (m_ref[2:3, :] * dx1).astype(BF16)

    rows = pl.BlockSpec((tm, D_MODEL), lambda i: (i, 0))
    per = pl.BlockSpec((None, 8, D_MODEL), lambda i: (i // per_seq, 0, 0))
    return _call(
        body, name="norm2_bwd", grid=(tokens // tm,),
        in_specs=[rows, rows, rows, rows, per, pl.BlockSpec((1, D_MODEL), lambda i: (0, 0))],
        out_specs=[rows, rows, pl.BlockSpec((8, D_MODEL), lambda i: (0, 0)), per],
        out_shape=[_sds((tokens, D_MODEL), F32), _sds((tokens, D_MODEL), BF16),
                   _sds((8, D_MODEL), F32), _sds((n_seq, 8, D_MODEL), F32)],
        compiler_params=_params(("arbitrary",)),
    )(dh2, x1, dy, mix, mod8, g_ffn)


def _mix_in_bwd(d_a, d_g, d_q, d_k, d_v, w_in, x2, dx1, mod8, g_mix, seq, tm=512):
    tokens = x2.shape[0]
    per_seq = seq // tm
    n_seq = tokens // seq
    parts = (d_a, d_g, d_q, d_k, d_v)
    width = D_CONV

    def body(da_ref, dg_ref, dq_ref, dk_ref, dv_ref, w_ref, x_ref, dx1_ref, m_ref, g_ref, gx_ref, dgm_ref, dm_ref):
        i = pl.program_id(0)

        @pl.when(i == 0)
        def _():
            dgm_ref[...] = jnp.zeros_like(dgm_ref)

        @pl.when(i % per_seq == 0)
        def _():
            dm_ref[...] = jnp.zeros_like(dm_ref)

        dh = jnp.zeros((tm, D_MODEL), F32)
        for n, ref in enumerate((da_ref, dg_ref, dq_ref, dk_ref, dv_ref)):
            dh = dh + lax.dot_general(ref[...], w_ref[:, pl.ds(n * width, width)], NT, preferred_element_type=F32)
        xv = x_ref[...]
        r = lax.rsqrt(jnp.mean(xv * xv, axis=-1, keepdims=True) + EPS)
        xhat = xv * r
        g = g_ref[...]
        dm_ref[0:1, :] += jnp.sum(dh, axis=0, keepdims=True)
        dm_ref[1:2, :] += jnp.sum(dh * (xhat * g), axis=0, keepdims=True)
        dn = dh * (1.0 + m_ref[1:2, :])
        dgm_ref[0:1, :] += jnp.sum(dn * xhat, axis=0, keepdims=True)
        dxh = dn * g
        gx_ref[...] = dx1_ref[...] + r * (dxh - xhat * jnp.mean(dxh * xhat, axis=-1, keepdims=True))

    rows = pl.BlockSpec((tm, D_MODEL), lambda i: (i, 0))
    half = pl.BlockSpec((tm, width), lambda i: (i, 0))
    per = pl.BlockSpec((None, 8, D_MODEL), lambda i: (i // per_seq, 0, 0))
    return _call(
        body, name="mix_in_bwd", grid=(tokens // tm,),
        in_specs=[half] * 5 + [pl.BlockSpec((D_MODEL, D_IN), lambda i: (0, 0)), rows, rows, per,
                               pl.BlockSpec((1, D_MODEL), lambda i: (0, 0))],
        out_specs=[rows, pl.BlockSpec((8, D_MODEL), lambda i: (0, 0)), per],
        out_shape=[_sds((tokens, D_MODEL), F32), _sds((8, D_MODEL), F32), _sds((n_seq, 8, D_MODEL), F32)],
        compiler_params=_params(("arbitrary",)),
    )(*parts, w_in, x2, dx1, mod8, g_mix)


def _grad_matmul(a, b, name, tmo, tno, tk=512):
    tokens, m = a.shape
    n = b.shape[1]

    def body(a_ref, b_ref, o_ref):
        @pl.when(pl.program_id(2) == 0)
        def _():
            o_ref[...] = jnp.zeros_like(o_ref)

        o_ref[...] += lax.dot_general(a_ref[...], b_ref[...], TN, preferred_element_type=F32)

    return _call(
        body, name=name, grid=(m // tmo, n // tno, tokens // tk),
        in_specs=[pl.BlockSpec((tk, tmo), lambda i, j, k: (k, i)),
                  pl.BlockSpec((tk, tno), lambda i, j, k: (k, j))],
        out_specs=pl.BlockSpec((tmo, tno), lambda i, j, k: (i, j)),
        out_shape=_sds((m, n), F32),
        compiler_params=_params(("parallel", "parallel", "arbitrary")),
    )(a, b)


def _adamw(w, m, v, g, name, n_parts=0, tr=256):
    rows, cols = w.shape
    tr = min(tr, rows)
    c1 = 1.0 - ADAM_B1 ** ADAM_STEP
    c2 = 1.0 - ADAM_B2 ** ADAM_STEP

    def body(w_ref, m_ref, v_ref, g_ref, go_ref, d_ref, mo_ref, vo_ref):
        if n_parts:
            gv = g_ref[0].astype(F32)
            for p in range(1, n_parts):
                gv = gv + g_ref[p].astype(F32)
        else:
            gv = g_ref[...]
        go_ref[...] = gv
        mn = ADAM_B1 * m_ref[...] + (1.0 - ADAM_B1) * gv
        vn = ADAM_B2 * v_ref[...] + (1.0 - ADAM_B2) * (gv * gv)
        mo_ref[...] = mn
        vo_ref[...] = vn
        d_ref[...] = -ADAM_LR * ((mn / c1) / (jnp.sqrt(vn / c2) + ADAM_EPS) + ADAM_WD * w_ref[...])

    blk = pl.BlockSpec((tr, cols), lambda i: (i, 0))
    g_spec = pl.BlockSpec((n_parts, tr, cols), lambda i: (0, i, 0)) if n_parts else blk
    return _call(
        body, name=name, grid=(rows // tr,),
        in_specs=[blk, blk, blk, g_spec], out_specs=[blk] * 4,
        out_shape=[_sds((rows, cols), F32)] * 4,
        compiler_params=_params(("parallel",)),
    )(w, m, v, g)


def _cols_to_full(blocks):
    n, r, c = blocks.shape
    return jnp.transpose(blocks, (1, 0, 2)).reshape(r, n * c)


def _full_to_cols(full, n=N_DEV):
    r, c = full.shape
    return jnp.transpose(full.reshape(r, n, c // n), (1, 0, 2))


def _pad_lanes(v, width):
    return jnp.pad(v, ((0, 0), (0, width - v.shape[1])))


def kernel(x, c, w_ada, b_ada, g_mix, w_in, w_dw, b_dw, g_conv_ln, b_conv_ln, g_q, g_k, w_out, g_ffn, w_gate, w_up, w_down, loss_target, m_w_ada, m_b_ada, m_g_mix, m_w_in, m_w_dw, m_b_dw, m_g_conv_ln, m_b_conv_ln, m_g_q, m_g_k, m_w_out, m_g_ffn, m_w_gate, m_w_up, m_w_down, v_w_ada, v_b_ada, v_g_mix, v_w_in, v_w_dw, v_b_dw, v_g_conv_ln, v_b_conv_ln, v_g_q, v_g_k, v_w_out, v_g_ffn, v_w_gate, v_w_up, v_w_down):
    n_seq, seq, _ = x.shape
    tokens = n_seq * seq
    me = 4 * lax.axis_index("x") + 2 * lax.axis_index("y") + lax.axis_index("c")
    ada_cols = w_ada.shape[2]
    dw_cols = w_dw.shape[2]

    (c_g, w_in_g, w_dw_g) = _exchange(
        [(c, False), (w_in[0].astype(BF16), False), (w_dw[0], False)], "gather_weights")
    c_all = c_g.reshape(N_DEV * n_seq, D_MODEL)
    w_in_f = _cols_to_full(w_in_g)
    w_dw_f = _cols_to_full(w_dw_g)

    b_cols = lax.dynamic_slice(b_ada, (0, me * ada_cols), (1, ada_cols))
    mod_cols = _ada_fwd(c_all, w_ada[0], b_cols)
    (mod_g,) = _exchange([(mod_cols, False)], "gather_mod")
    mod_mine = lax.dynamic_slice(mod_g, (0, me * n_seq, 0), (N_DEV, n_seq, ada_cols))
    mod = jnp.transpose(mod_mine, (1, 0, 2)).reshape(n_seq, N_MOD, D_MODEL)
    mod8 = jnp.pad(mod, ((0, 0), (0, 8 - N_MOD), (0, 0)))

    x2 = x.reshape(tokens, D_MODEL)
    h1, proj = _mix_in(x2, mod8, g_mix, w_in_f, seq)
    proj3 = proj.reshape(n_seq, seq, D_IN)
    uc3 = _conv_fwd(proj3, w_dw_f, b_dw)
    g_q2, g_k2 = jnp.tile(g_q, (1, 2)), jnp.tile(g_k, (1, 2))
    y_att3, lse3, w_out_g, w_gate_g, w_up_g, w_down_g = _attn_fwd(
        proj3, g_q2, g_k2,
        [(w_out[0].astype(BF16), False), (w_gate[0].astype(BF16), False), (w_up[0].astype(BF16), False),
         (w_down[0].astype(BF16), False)])
    w_out_f = w_out_g.reshape(D_MODEL, D_MODEL)
    w_gate_f = _cols_to_full(w_gate_g)
    w_up_f = _cols_to_full(w_up_g)
    w_down_f = w_down_g.reshape(D_FF, D_MODEL)
    uc2 = uc3.reshape(tokens, D_CONV)
    y_att2 = y_att3.reshape(tokens, D_ATT)
    y_conv, mix, x1, h2 = _mix_out(uc2, y_att2, x2, mod8, g_conv_ln, b_conv_ln, g_ffn, w_out_f, seq)
    gate, up, f = _ffn_fwd(h2, w_gate_f, w_up_f, w_down_f)
    dy, df, sq, dgate_f = _loss_head(f, x1, loss_target.reshape(tokens, D_MODEL), mod8, seq)

    dgate, dup, act, dh2 = _ffn_bwd(df, gate, up, w_gate_f, w_up_f, w_down_f)
    dx1, dmix, dg_ffn, dmod_f = _norm2_bwd(dh2, x1, dy, mix, mod8, g_ffn, seq)
    duc2, do2, dgb_ln = _mix_out_bwd(dmix, uc2, g_conv_ln, b_conv_ln, w_out_f)
    d_a3, d_g3, dw_dw_p, db_dw_p = _conv_bwd(duc2.reshape(n_seq, seq, D_CONV), proj3, w_dw_f)
    gw_gate = _grad_matmul(h2, dgate, "grad_w_gate", D_MODEL, D_FF // 2)
    gw_up = _grad_matmul(h2, dup, "grad_w_up", D_MODEL, D_FF // 2)
    gw_down = _grad_matmul(act, df, "grad_w_down", D_FF // 2, D_MODEL)
    d_q3, d_k3, d_v3, dg_qk, p_gate, p_up, p_down = _attn_bwd(
        proj3, do2.reshape(n_seq, seq, D_ATT), y_att3, lse3, g_q2, g_k2,
        [(_full_to_cols(gw_gate).astype(BF16), True), (_full_to_cols(gw_up).astype(BF16), True),
         (gw_down.astype(BF16).reshape(N_DEV, D_FF // N_DEV, D_MODEL), True)])
    flat = lambda t: t.reshape(tokens, t.shape[-1])
    d_a, d_g, d_q, d_k, d_v = flat(d_a3), flat(d_g3), flat(d_q3), flat(d_k3), flat(d_v3)
    grad_x2, dg_mix, dmod_m = _mix_in_bwd(d_a, d_g, d_q, d_k, d_v, w_in_f, x2, dx1, mod8, g_mix, seq)

    gw_in = jnp.concatenate(
        [_grad_matmul(h1, t, "grad_w_in_%d" % i, D_MODEL, D_CONV) for i, t in enumerate((d_a, d_g, d_q, d_k, d_v))], axis=1)
    gw_out = jnp.concatenate(
        [_grad_matmul(y_conv, dmix, "grad_w_out_conv", D_CONV, D_MODEL),
         _grad_matmul(y_att2, dmix, "grad_w_out_att", D_ATT, D_MODEL)], axis=0)

    dmod = jnp.concatenate([dmod_m[:, 0], dmod_m[:, 1], dmod_f[:, 2], dmod_f[:, 0], dmod_f[:, 1], dgate_f[:, 0]], axis=1)
    dg_q = dg_qk[0:1, 0:HEAD_DIM] + dg_qk[0:1, HEAD_DIM:]
    dg_k = dg_qk[1:2, 0:HEAD_DIM] + dg_qk[1:2, HEAD_DIM:]
    loss_part = (0.5 / D_MODEL) * jnp.sum(sq[0:1, :], axis=1, keepdims=True)
    small = jnp.concatenate(
        [dg_mix[0:1], dg_ffn[0:1], db_dw_p[0:1], dgb_ln[0:1], dgb_ln[1:2],
         _pad_lanes(dg_q, LANES), _pad_lanes(dg_k, LANES), _pad_lanes(loss_part, LANES)], axis=1)
    n_small = small.shape[1] - LANES

    (p_in, p_out, dmod_g, small_g, dw_g) = _exchange(
        [(_full_to_cols(gw_in).astype(BF16), True),
         (gw_out.astype(BF16).reshape(N_DEV, D_MODEL // N_DEV, D_MODEL), True),
         (dmod, False), (small, False), (dw_dw_p, False)], "scatter_grads")

    dmod_all = dmod_g.reshape(N_DEV * n_seq, N_MOD * D_MODEL)
    dmod_cols = lax.dynamic_slice(dmod_all, (0, me * ada_cols), (N_DEV * n_seq, ada_cols))
    gw_ada, gb_ada = _ada_bwd(c_all, dmod_cols, dmod_all)

    res = {}
    res["w_ada"] = _adamw(w_ada[0], m_w_ada[0], v_w_ada[0], gw_ada, "adamw_w_ada")
    res["b_ada"] = _adamw(b_ada, m_b_ada, v_b_ada, gb_ada, "adamw_b_ada")
    res["w_in"] = _adamw(w_in[0], m_w_in[0], v_w_in[0], p_in, "adamw_w_in", N_DEV)
    res["w_out"] = _adamw(w_out[0], m_w_out[0], v_w_out[0], p_out, "adamw_w_out", N_DEV)
    res["w_gate"] = _adamw(w_gate[0], m_w_gate[0], v_w_gate[0], p_gate, "adamw_w_gate", N_DEV)
    res["w_up"] = _adamw(w_up[0], m_w_up[0], v_w_up[0], p_up, "adamw_w_up", N_DEV)
    res["w_down"] = _adamw(w_down[0], m_w_down[0], v_w_down[0], p_down, "adamw_w_down", N_DEV, tr=176)
    dw_mine = lax.dynamic_slice(dw_g, (0, 0, me * dw_cols), (N_DEV, CONV_WIDTH, dw_cols))
    res["w_dw"] = _adamw(w_dw[0], m_w_dw[0], v_w_dw[0], dw_mine, "adamw_w_dw", N_DEV)

    small_names = ["g_mix", "g_ffn", "b_dw", "g_conv_ln", "b_conv_ln", "g_q", "g_k"]
    small_w = {"g_mix": (g_mix, m_g_mix, v_g_mix), "g_ffn": (g_ffn, m_g_ffn, v_g_ffn), "b_dw": (b_dw, m_b_dw, v_b_dw),
               "g_conv_ln": (g_conv_ln, m_g_conv_ln, v_g_conv_ln), "b_conv_ln": (b_conv_ln, m_b_conv_ln, v_b_conv_ln),
               "g_q": (g_q, m_g_q, v_g_q), "g_k": (g_k, m_g_k, v_g_k)}
    widths = [max(small_w[n][0].shape[1], LANES) for n in small_names]
    packed = [jnp.concatenate([_pad_lanes(small_w[n][i], wd) for n, wd in zip(small_names, widths)], axis=1) for i in range(3)]
    outs = _adamw(packed[0], packed[1], packed[2], small_g[:, :, :n_small], "adamw_small", N_DEV)
    off = 0
    for n, wd in zip(small_names, widths):
        real = small_w[n][0].shape[1]
        res[n] = tuple(o[:, off:off + real] for o in outs)
        off += wd
    loss = jnp.sum(small_g[:, 0, n_small])

    order = ["w_ada", "b_ada", "g_mix", "w_in", "w_dw", "b_dw", "g_conv_ln", "b_conv_ln", "g_q", "g_k",
             "w_out", "g_ffn", "w_gate", "w_up", "w_down"]
    lead = {"w_ada", "w_in", "w_dw", "w_out", "w_gate", "w_up", "w_down"}
    grads, deltas, new_m, new_v = [], [], [], []
    for n in order:
        g, d, mn, vn = res[n]
        g, d, mn, vn = (t[None] if n in lead else t for t in (g, d, mn, vn))
        grads.append(g)
        deltas.append(d)
        new_m.append(mn)
        new_v.append(vn)
    return (loss, grad_x2.reshape(n_seq, seq, D_MODEL), *grads, *deltas, *new_m, *new_v)
```

```python
import numpy as np
import jax
import jax.numpy as jnp
from jax import lax
from jax.experimental import pallas as pl
from jax.experimental.pallas import tpu as pltpu

F32 = jnp.float32
BF16 = jnp.bfloat16

N_DEV = 8
D_MODEL = 1024
D_CONV = 512
D_ATT = 512
HEAD_DIM = 64
CONV_WIDTH = 31
D_IN = 2 * D_CONV + 3 * D_ATT
D_FF = 2816
N_MOD = 6
EPS = 1e-6
RADIUS = 64
DILATIONS = (1, 4, 16)
Q_BLOCK = 128
LANES = 128
VMEM_LIMIT = 56 * 1024 * 1024

ADAM_LR = 0.001
ADAM_B1 = 0.9
ADAM_B2 = 0.999
ADAM_EPS = 1e-08
ADAM_WD = 0.01
ADAM_STEP = 10

NT = (((1,), (1,)), ((), ()))
TN = (((0,), (0,)), ((), ()))


def _call(body, **kw):
    return pl.pallas_call(body, **kw)


def _params(sem=None, vmem=VMEM_LIMIT):
    return pltpu.CompilerParams(dimension_semantics=sem, vmem_limit_bytes=vmem)


def _sig(x):
    return 1.0 / (1.0 + jnp.exp(-x))


def _sds(shape, dtype):
    return jax.ShapeDtypeStruct(shape, dtype)


N_PEER = N_DEV - 1
ANY_SPEC = pl.BlockSpec(memory_space=pl.ANY)


def _exchange_copies(scatter, ins, outs, *sems):
    n = len(ins)
    if n == 0:
        return [], []
    send_sems, recv_sems, local_sems = sems
    x, y, c = lax.axis_index("x"), lax.axis_index("y"), lax.axis_index("c")
    me = 4 * x + 2 * y + c

    def src(a, slot):
        return ins[a].at[slot] if scatter[a] else ins[a]

    local = [pltpu.make_async_copy(src(a, me), outs[a].at[me], local_sems.at[a]) for a in range(n)]
    flights = []
    for k in range(1, N_DEV):
        px = 1 - x if k & 4 else x
        py = 1 - y if k & 2 else y
        pc = 1 - c if k & 1 else c
        pid = 4 * px + 2 * py + pc
        for a in range(n):
            i = a * N_PEER + k - 1
            send, recv = (pltpu.make_async_remote_copy(
                src_ref=src(a, pid), dst_ref=outs[a].at[slot],
                send_sem=send_sems.at[i], recv_sem=recv_sems.at[i],
                device_id=(px, py, pc), device_id_type=pl.DeviceIdType.MESH) for slot in (me, pid))
            flights.append((send, recv))
    return local, flights


def _exchange_start(*args):
    local, flights = _exchange_copies(*args)
    for cp in local:
        cp.start()
    for send, _ in flights:
        send.start()


def _exchange_wait(*args):
    local, flights = _exchange_copies(*args)
    for send, recv in flights:
        send.wait_send()
        recv.wait_recv()
    for cp in local:
        cp.wait()


def _exchange_shapes(items):
    return [_sds((N_DEV,) + tuple(arr.shape[1:] if scatter else arr.shape), arr.dtype) for arr, scatter in items]


def _exchange_sems(n):
    if n == 0:
        return []
    return [pltpu.SemaphoreType.DMA((n * N_PEER,)), pltpu.SemaphoreType.DMA((n * N_PEER,)),
            pltpu.SemaphoreType.DMA((n,))]


def _exchange(items, name):
    n = len(items)
    scatter = [s for _, s in items]

    def body(*refs):
        args = (scatter, refs[:n], refs[n:2 * n]) + tuple(refs[2 * n:])
        _exchange_start(*args)
        _exchange_wait(*args)

    return _call(
        body, name=name, out_shape=_exchange_shapes(items),
        in_specs=[ANY_SPEC] * n, out_specs=[ANY_SPEC] * n, scratch_shapes=_exchange_sems(n),
    )(*[a for a, _ in items])


def _ada_fwd(c_all, w_ada, b_cols):
    def body(c_ref, w_ref, b_ref, o_ref):
        cv = c_ref[...]
        sc = (cv * _sig(cv)).astype(BF16)
        o_ref[...] = jnp.dot(sc, w_ref[...].astype(BF16), preferred_element_type=F32) + b_ref[...]

    return _call(body, name="ada_fwd", out_shape=_sds((c_all.shape[0], w_ada.shape[1]), F32),
                 compiler_params=_params())(c_all, w_ada, b_cols)


def _ada_bwd(c_all, dmod_cols, dmod_all):
    def body(c_ref, dc_ref, da_ref, gw_ref, gb_ref):
        cv = c_ref[...]
        sc = (cv * _sig(cv)).astype(BF16)
        gw_ref[...] = lax.dot_general(sc, dc_ref[...].astype(BF16), TN, preferred_element_type=F32)
        gb_ref[...] = jnp.sum(da_ref[...], axis=0, keepdims=True)

    return _call(body, name="ada_bwd",
                 out_shape=[_sds((c_all.shape[1], dmod_cols.shape[1]), F32), _sds((1, dmod_all.shape[1]), F32)],
                 compiler_params=_params())(c_all, dmod_cols, dmod_all)


def _mix_in(x2, mod8, g_mix, w_in, seq, tm=512):
    tokens = x2.shape[0]
    per_seq = seq // tm

    def body(x_ref, m_ref, g_ref, w_ref, h_ref, p_ref):
        xv = x_ref[...]
        r = lax.rsqrt(jnp.mean(xv * xv, axis=-1, keepdims=True) + EPS)
        h = (xv * r * g_ref[...]) * (1.0 + m_ref[1:2, :]) + m_ref[0:1, :]
        hb = h.astype(BF16)
        h_ref[...] = hb
        p_ref[...] = jnp.dot(hb, w_ref[...], preferred_element_type=F32)

    return _call(
        body, name="mix_in", grid=(tokens // tm,),
        in_specs=[pl.BlockSpec((tm, D_MODEL), lambda i: (i, 0)),
                  pl.BlockSpec((None, 8, D_MODEL), lambda i: (i // per_seq, 0, 0)),
                  pl.BlockSpec((1, D_MODEL), lambda i: (0, 0)),
                  pl.BlockSpec((D_MODEL, D_IN), lambda i: (0, 0))],
        out_specs=[pl.BlockSpec((tm, D_MODEL), lambda i: (i, 0)),
                   pl.BlockSpec((tm, D_IN), lambda i: (i, 0))],
        out_shape=[_sds((tokens, D_MODEL), BF16), _sds((tokens, D_IN), F32)],
        compiler_params=_params(("parallel",)),
    )(x2, mod8, g_mix, w_in)


CONV_ROWS = 32
CONV_HALO = 16


def _fill_shifted(xp, sh, seq):
    for b in range(8):
        sh[b, pl.ds(0, seq + 24), :] = xp[pl.ds(b, seq + 24), :]


def _conv_fwd(proj3, w_dw, b_dw):
    n_seq, seq, _ = proj3.shape
    n_cb = D_CONV // LANES

    def body(a_ref, g_ref, w_ref, b_ref, uc_ref, xp, sh):
        zeros = jnp.zeros((CONV_HALO, LANES), F32)
        xp[pl.ds(0, CONV_HALO), :] = zeros
        xp[pl.ds(CONV_HALO + seq, CONV_HALO), :] = zeros
        xp[pl.ds(CONV_HALO, seq), :] = a_ref[...] * _sig(g_ref[...])
        _fill_shifted(xp, sh, seq)

        def blk(i, carry):
            t0 = pl.multiple_of(i * CONV_ROWS, CONV_ROWS)
            acc = jnp.zeros((CONV_ROWS, LANES), F32)
            for j in range(CONV_WIDTH):
                jj = j + 1
                acc = acc + sh[jj % 8, pl.ds(t0 + 8 * (jj // 8), CONV_ROWS), :] * w_ref[j:j + 1, :]
            uc_ref[pl.ds(t0, CONV_ROWS), :] = acc + b_ref[...]
            return carry

        lax.fori_loop(0, seq // CONV_ROWS, blk, 0)

    return _call(
        body, name="conv_fwd", grid=(n_seq, n_cb),
        in_specs=[pl.BlockSpec((None, seq, LANES), lambda b, cb: (b, 0, cb)),
                  pl.BlockSpec((None, seq, LANES), lambda b, cb: (b, 0, n_cb + cb)),
                  pl.BlockSpec((CONV_WIDTH, LANES), lambda b, cb: (0, cb)),
                  pl.BlockSpec((1, LANES), lambda b, cb: (0, cb))],
        out_specs=pl.BlockSpec((None, seq, LANES), lambda b, cb: (b, 0, cb)),
        out_shape=_sds((n_seq, seq, D_CONV), F32),
        scratch_shapes=[pltpu.VMEM((seq + 2 * CONV_HALO, LANES), F32),
                        pltpu.VMEM((8, seq + 2 * CONV_HALO, LANES), F32)],
        compiler_params=_params(("parallel", "parallel")),
    )(proj3, proj3, w_dw, b_dw)


def _conv_bwd(duc3, proj3, w_dw):
    n_seq, seq, _ = proj3.shape
    n_cb = D_CONV // LANES

    def body(duc_ref, a_ref, g_ref, w_ref, da_ref, dg_ref, dw_ref, db_ref, xp, sh):
        @pl.when(pl.program_id(1) == 0)
        def _():
            dw_ref[...] = jnp.zeros_like(dw_ref)
            db_ref[...] = jnp.zeros_like(db_ref)

        zeros = jnp.zeros((CONV_HALO, LANES), F32)
        xp[pl.ds(0, CONV_HALO), :] = zeros
        xp[pl.ds(CONV_HALO + seq, CONV_HALO), :] = zeros
        xp[pl.ds(CONV_HALO, seq), :] = a_ref[...] * _sig(g_ref[...])
        _fill_shifted(xp, sh, seq)
        for j0 in range(0, CONV_WIDTH, 8):
            taps = range(j0, min(j0 + 8, CONV_WIDTH))

            def wblk(i, accs, taps=taps):
                t0 = pl.multiple_of(i * CONV_ROWS, CONV_ROWS)
                d = duc_ref[pl.ds(t0, CONV_ROWS), :]
                return tuple(acc + d * sh[(j + 1) % 8, pl.ds(t0 + 8 * ((j + 1) // 8), CONV_ROWS), :]
                             for acc, j in zip(accs, taps))

            accs = lax.fori_loop(0, seq // CONV_ROWS, wblk,
                                 tuple(jnp.zeros((CONV_ROWS, LANES), F32) for _ in taps))
            for acc, j in zip(accs, taps):
                dw_ref[j:j + 1, :] += jnp.sum(acc, axis=0, keepdims=True)
        db_ref[0:1, :] += jnp.sum(duc_ref[...], axis=0, keepdims=True)
        xp[pl.ds(CONV_HALO, seq), :] = duc_ref[...]
        _fill_shifted(xp, sh, seq)

        def ublk(i, carry):
            t0 = pl.multiple_of(i * CONV_ROWS, CONV_ROWS)
            acc = jnp.zeros((CONV_ROWS, LANES), F32)
            for j in range(CONV_WIDTH):
                jj = CONV_WIDTH - j
                acc = acc + sh[jj % 8, pl.ds(t0 + 8 * (jj // 8), CONV_ROWS), :] * w_ref[j:j + 1, :]
            av = a_ref[pl.ds(t0, CONV_ROWS), :]
            sg = _sig(g_ref[pl.ds(t0, CONV_ROWS), :])
            da_ref[pl.ds(t0, CONV_ROWS), :] = (acc * sg).astype(BF16)
            dg_ref[pl.ds(t0, CONV_ROWS), :] = (acc * av * sg * (1.0 - sg)).astype(BF16)
            return carry

        lax.fori_loop(0, seq // CONV_ROWS, ublk, 0)

    return _call(
        body, name="conv_bwd", grid=(n_cb, n_seq),
        in_specs=[pl.BlockSpec((None, seq, LANES), lambda cb, b: (b, 0, cb)),
                  pl.BlockSpec((None, seq, LANES), lambda cb, b: (b, 0, cb)),
                  pl.BlockSpec((None, seq, LANES), lambda cb, b: (b, 0, n_cb + cb)),
                  pl.BlockSpec((CONV_WIDTH, LANES), lambda cb, b: (0, cb))],
        out_specs=[pl.BlockSpec((None, seq, LANES), lambda cb, b: (b, 0, cb)),
                   pl.BlockSpec((None, seq, LANES), lambda cb, b: (b, 0, cb)),
                   pl.BlockSpec((32, LANES), lambda cb, b: (0, cb)),
                   pl.BlockSpec((8, LANES), lambda cb, b: (0, cb))],
        out_shape=[_sds((n_seq, seq, D_CONV), BF16), _sds((n_seq, seq, D_CONV), BF16),
                   _sds((32, D_CONV), F32), _sds((8, D_CONV), F32)],
        scratch_shapes=[pltpu.VMEM((seq + 2 * CONV_HALO, LANES), F32),
                        pltpu.VMEM((8, seq + 2 * CONV_HALO, LANES), F32)],
        compiler_params=_params(("parallel", "arbitrary")),
    )(duc3, proj3, proj3, w_dw)


MASKED = 1e30
ATT_UNROLL = 4


def _distance_mats(dil, seg_len):
    kw = min(2 * Q_BLOCK, seg_len)
    offsets = (0, -RADIUS, -2 * RADIUS) if kw == 2 * Q_BLOCK else (0,)
    a = np.arange(Q_BLOCK)[:, None]
    b = np.arange(kw)[None, :]
    mats = []
    for off in offsets:
        rel = np.abs(b + off - a)
        mats.append(np.where(rel <= RADIUS, dil * rel, MASKED))
    return jnp.asarray(np.stack(mats).astype(np.float32))


def _alibi_rows():
    s = np.zeros((4, 8, LANES), np.float32)
    for hp in range(4):
        for hl in range(2):
            s[hp, hl, :] = 2.0 ** (-(2 * hp + hl + 1))
    return jnp.asarray(s)


def _window(n, seg_len):
    i0 = pl.multiple_of(n * Q_BLOCK, Q_BLOCK)
    if seg_len <= Q_BLOCK:
        return i0, i0, 0
    per_seg = seg_len // Q_BLOCK
    j = n % per_seg
    seg0 = (n // per_seg) * seg_len
    ks_local = jnp.clip(j * Q_BLOCK - RADIUS, 0, seg_len - 2 * Q_BLOCK)
    ks = pl.multiple_of(seg0 + ks_local, RADIUS)
    var = jnp.where(j == 0, 0, jnp.where(j == per_seg - 1, 2, 1))
    return i0, ks, var


def _first_head(rows):
    return lax.broadcasted_iota(jnp.int32, (rows, LANES), 1) < HEAD_DIM


def _head_mean(x, first):
    s0 = jnp.sum(jnp.where(first, x, 0.0), axis=1, keepdims=True)
    s1 = jnp.sum(jnp.where(first, 0.0, x), axis=1, keepdims=True)
    return jnp.where(first, s0, s1) * (1.0 / HEAD_DIM)


def _permute_rows(dst, src, dil, seq):
    seg = seq // dil
    for r in range(dil):
        rows = pl.ds(r, seg, stride=dil) if dil > 1 else pl.ds(0, seq)
        dst[pl.ds(r * seg, seg), :] = src[rows, :].astype(dst.dtype)


def _permute_rows_by_head(dst, src, dil, seq):
    seg = seq // dil
    first = _first_head(seg)
    for r in range(dil):
        rows = pl.ds(r, seg, stride=dil) if dil > 1 else pl.ds(0, seq)
        val = src[rows, :]
        dst[0, pl.ds(r * seg, seg), :] = jnp.where(first, val, 0.0).astype(dst.dtype)
        dst[1, pl.ds(r * seg, seg), :] = jnp.where(first, 0.0, val).astype(dst.dtype)


def _qk_normalise(q_ref, g2_ref, dst, seq, scale):
    def chunk(ci, carry):
        rows = pl.ds(pl.multiple_of(ci * 256, 256), 256)
        qv = q_ref[rows, :]
        r = lax.rsqrt(_head_mean(qv * qv, _first_head(256)) + EPS)
        dst[rows, :] = qv * r * (g2_ref[...] * scale)
        return carry

    lax.fori_loop(0, seq // 256, chunk, 0)


def _attn_fwd(proj3, g_q2, g_k2, ride):
    n_seq, seq, _ = proj3.shape
    dms = [_distance_mats(d, seq // d) for d in DILATIONS]
    col0 = 2 * D_CONV // LANES
    n_hp = D_ATT // LANES

    n_ride = len(ride)
    ride_scatter = [s for _, s in ride]

    def body(*refs):
        q_ref, k_ref, v_ref, gq_ref, gk_ref, sl_ref, dm1, dm4, dm16 = refs[:9]
        ride_in = refs[9:9 + n_ride]
        y_ref, lse_ref = refs[9 + n_ride:11 + n_ride]
        ride_out = refs[11 + n_ride:11 + 2 * n_ride]
        (qf, kf, qp, kp, vp, oml_p, o1, o4, o16, m1, m4, m16, l1, l4, l16) = refs[11 + 2 * n_ride:26 + 2 * n_ride]
        o_nat, m_nat, l_nat = (o1, o4, o16), (m1, m4, m16), (l1, l4, l16)
        ride_args = (ride_scatter, ride_in, ride_out) + tuple(refs[26 + 2 * n_ride:])
        step = pl.program_id(0) * n_hp + pl.program_id(1)

        @pl.when(step == 0)
        def _():
            _exchange_start(*ride_args)

        dm_refs = (dm1, dm4, dm16)
        _qk_normalise(q_ref, gq_ref, qf, seq, HEAD_DIM ** -0.5)
        _qk_normalise(k_ref, gk_ref, kf, seq, 1.0)
        slopes = (sl_ref[0:1, 0:1], sl_ref[1:2, 0:1])
        for pi, dil in enumerate(DILATIONS):
            seg = seq // dil
            kw = min(2 * Q_BLOCK, seg)
            _permute_rows_by_head(qp, qf, dil, seq)
            _permute_rows(kp, kf, dil, seq)
            _permute_rows(vp, v_ref, dil, seq)

            def blk(it, carry, seg=seg, kw=kw, pi=pi, dst=oml_p):
                first = _first_head(Q_BLOCK)
                chains = [(sub, h) for sub in range(ATT_UNROLL) for h in range(2)]
                win = [_window(it * ATT_UNROLL + sub, seg) for sub in range(ATT_UNROLL)]
                s = {}
                for sub, h in chains:
                    i0, ks, var = win[sub]
                    s[sub, h] = lax.dot_general(qp[h, pl.ds(i0, Q_BLOCK), :], kp[pl.ds(ks, kw), :], NT,
                                                preferred_element_type=F32) - slopes[h] * dm_refs[pi][var]
                m, l, p = {}, {}, {}
                for c in chains:
                    m[c] = jnp.max(s[c], axis=1, keepdims=True)
                    e = jnp.exp(s[c] - m[c])
                    l[c] = jnp.sum(e, axis=1, keepdims=True)
                    p[c] = e.astype(BF16)
                o = {}
                for sub, h in chains:
                    o[sub, h] = jnp.dot(p[sub, h], vp[pl.ds(win[sub][1], kw), :], preferred_element_type=F32)
                packed = [jnp.concatenate([jnp.where(first, t[sub, 0], t[sub, 1]) for t in (o, m, l)], axis=1)
                          for sub in range(ATT_UNROLL)]
                rows = pl.ds(pl.multiple_of(it * (ATT_UNROLL * Q_BLOCK), ATT_UNROLL * Q_BLOCK), ATT_UNROLL * Q_BLOCK)
                dst[rows, :] = jnp.concatenate(packed, axis=0)
                return carry

            lax.fori_loop(0, seq // (Q_BLOCK * ATT_UNROLL), blk, 0)
            for r in range(dil):
                nat = pl.ds(r, seg, stride=dil) if dil > 1 else pl.ds(0, seq)
                perm = pl.ds(r * seg, seg)
                o_nat[pi][nat, :] = oml_p[perm, pl.ds(0, LANES)]
                m_nat[pi][nat, :] = oml_p[perm, pl.ds(LANES, LANES)]
                l_nat[pi][nat, :] = oml_p[perm, pl.ds(2 * LANES, LANES)]

        def merge(ci, carry):
            rows = pl.ds(pl.multiple_of(ci * 256, 256), 256)
            ms = [m_nat[pi][rows, :] for pi in range(3)]
            m_all = jnp.maximum(jnp.maximum(ms[0], ms[1]), ms[2])
            es = [jnp.exp(m - m_all) for m in ms]
            l_all = sum(l_nat[pi][rows, :] * es[pi] for pi in range(3))
            inv = 1.0 / l_all
            o = sum(o_nat[pi][rows, :] * (es[pi] * inv) for pi in range(3))
            y_ref[rows, :] = o.astype(BF16)
            lse_ref[rows, :] = m_all + jnp.log(l_all)
            return carry

        lax.fori_loop(0, seq // 256, merge, 0)

        @pl.when(step == n_seq * n_hp - 1)
        def _():
            _exchange_wait(*ride_args)

    def col(off):
        return pl.BlockSpec((None, seq, LANES), lambda b, hp: (b, 0, col0 + off * n_hp + hp))

    def whole(arr):
        return pl.BlockSpec(arr.shape, lambda b, hp: (0,) * arr.ndim)

    rows_f32 = pltpu.VMEM((seq, LANES), F32)
    rows_bf16 = pltpu.VMEM((seq, LANES), BF16)
    return _call(
        body, name="attn_fwd", grid=(n_seq, n_hp),
        in_specs=[col(0), col(1), col(2), whole(g_q2), whole(g_k2),
                  pl.BlockSpec((None, 8, LANES), lambda b, hp: (hp, 0, 0)),
                  whole(dms[0]), whole(dms[1]), whole(dms[2])] + [ANY_SPEC] * n_ride,
        out_specs=[pl.BlockSpec((None, seq, LANES), lambda b, hp: (b, 0, hp)),
                   pl.BlockSpec((None, seq, LANES), lambda b, hp: (b, 0, hp))] + [ANY_SPEC] * n_ride,
        out_shape=[_sds((n_seq, seq, D_ATT), BF16), _sds((n_seq, seq, D_ATT), F32)] + _exchange_shapes(ride),
        scratch_shapes=[rows_f32, rows_f32, pltpu.VMEM((2, seq, LANES), BF16), rows_bf16, rows_bf16]
        + [pltpu.VMEM((seq, 3 * LANES), F32)] + [rows_f32] * 9 + _exchange_sems(n_ride),
        compiler_params=_params(("arbitrary", "arbitrary")),
    )(proj3, proj3, proj3, g_q2, g_k2, _alibi_rows(), *dms, *[a for a, _ in ride])


def _attn_bwd(proj3, do3, y_att3, lse3, g_q2, g_k2, ride):
    n_seq, seq, _ = proj3.shape
    dms = [_distance_mats(d, seq // d) for d in DILATIONS]
    col0 = 2 * D_CONV // LANES
    n_hp = D_ATT // LANES

    n_ride = len(ride)
    ride_scatter = [s for _, s in ride]

    def body(*refs):
        q_ref, k_ref, v_ref, do_ref, o_ref, lse_ref, gq_ref, gk_ref, sl_ref, dm1, dm4, dm16 = refs[:12]
        ride_in = refs[12:12 + n_ride]
        dq_ref, dk_ref, dv_ref, dg_ref = refs[12 + n_ride:16 + n_ride]
        ride_out = refs[16 + n_ride:16 + 2 * n_ride]
        (qf, kf, qp, dop, kp, vp, sn, sp, dqp, dkp, dvp, dqn, dkn, dvn) = refs[16 + 2 * n_ride:30 + 2 * n_ride]
        ride_args = (ride_scatter, ride_in, ride_out) + tuple(refs[30 + 2 * n_ride:])
        dm_refs = (dm1, dm4, dm16)
        step = pl.program_id(0) * n_hp + pl.program_id(1)

        @pl.when(step == 0)
        def _():
            _exchange_start(*ride_args)
            dg_ref[...] = jnp.zeros_like(dg_ref)

        _qk_normalise(q_ref, gq_ref, qf, seq, HEAD_DIM ** -0.5)
        _qk_normalise(k_ref, gk_ref, kf, seq, 1.0)

        def stats(ci, carry):
            rows = pl.ds(pl.multiple_of(ci * 256, 256), 256)
            first = _first_head(256)
            prod = do_ref[rows, :] * o_ref[rows, :].astype(F32)
            sn[rows, 0:1] = lse_ref[rows, 0:1]
            sn[rows, 1:2] = lse_ref[rows, HEAD_DIM:HEAD_DIM + 1]
            sn[rows, 2:3] = jnp.sum(jnp.where(first, prod, 0.0), axis=1, keepdims=True)
            sn[rows, 3:4] = jnp.sum(jnp.where(first, 0.0, prod), axis=1, keepdims=True)
            return carry

        lax.fori_loop(0, seq // 256, stats, 0)
        slopes = (sl_ref[0:1, 0:1], sl_ref[1:2, 0:1])
        half = seq // (Q_BLOCK * ATT_UNROLL)
        region = seq // ATT_UNROLL

        for pi, dil in enumerate(DILATIONS):
            seg = seq // dil
            kw = min(2 * Q_BLOCK, seg)
            _permute_rows_by_head(qp, qf, dil, seq)
            _permute_rows_by_head(dop, do_ref, dil, seq)
            _permute_rows(kp, kf, dil, seq)
            _permute_rows(vp, v_ref, dil, seq)
            _permute_rows(sp, sn, dil, seq)
            for sub in range(ATT_UNROLL):
                lo, hi = max(sub * region - RADIUS, 0), min((sub + 1) * region + RADIUS, seq)
                dkp[sub, pl.ds(lo, hi - lo), :] = jnp.zeros((hi - lo, LANES), F32)
                dvp[sub, pl.ds(lo, hi - lo), :] = jnp.zeros((hi - lo, LANES), F32)

            def blk(it, carry, seg=seg, kw=kw, pi=pi):
                first = _first_head(Q_BLOCK)
                chains = [(sub, h) for sub in range(ATT_UNROLL) for h in range(2)]
                win = [_window(it + sub * half, seg) for sub in range(ATT_UNROLL)]
                qrows = [pl.ds(w[0], Q_BLOCK) for w in win]
                krows = [pl.ds(w[1], kw) for w in win]
                s, dp = {}, {}
                for sub, h in chains:
                    s[sub, h] = lax.dot_general(qp[h, qrows[sub], :], kp[krows[sub], :], NT,
                                                preferred_element_type=F32) - slopes[h] * dm_refs[pi][win[sub][2]]
                    dp[sub, h] = lax.dot_general(dop[h, qrows[sub], :], vp[krows[sub], :], NT,
                                                 preferred_element_type=F32)
                p, ds = {}, {}
                for sub, h in chains:
                    e = jnp.exp(s[sub, h] - sp[qrows[sub], h:h + 1])
                    ds[sub, h] = (e * (dp[sub, h] - sp[qrows[sub], 2 + h:3 + h])).astype(BF16)
                    p[sub, h] = e.astype(BF16)
                dq, dk, dv = {}, {}, {}
                for sub, h in chains:
                    dq[sub, h] = jnp.dot(ds[sub, h], kp[krows[sub], :], preferred_element_type=F32)
                    dk[sub, h] = lax.dot_general(ds[sub, h], qp[h, qrows[sub], :], TN, preferred_element_type=F32)
                    dv[sub, h] = lax.dot_general(p[sub, h], dop[h, qrows[sub], :], TN, preferred_element_type=F32)
                for sub in range(ATT_UNROLL):
                    dqp[qrows[sub], :] = jnp.where(first, dq[sub, 0], dq[sub, 1])
                    dkp[sub, krows[sub], :] += dk[sub, 0] + dk[sub, 1]
                    dvp[sub, krows[sub], :] += dv[sub, 0] + dv[sub, 1]
                return carry

            lax.fori_loop(0, half, blk, 0)
            if dil == 1:
                for sub in range(ATT_UNROLL):
                    r0 = sub * region
                    pieces = [(r0, RADIUS, [sub - 1, sub] if sub > 0 else [sub]),
                              (r0 + RADIUS, region - 2 * RADIUS, [sub]),
                              (r0 + region - RADIUS, RADIUS, [sub, sub + 1] if sub < ATT_UNROLL - 1 else [sub])]
                    for start, size, owners in pieces:
                        rows = pl.ds(start, size)
                        dqn[rows, :] = dqp[rows, :]
                        dkn[rows, :] = sum(dkp[o, rows, :] for o in owners)
                        dvn[rows, :] = sum(dvp[o, rows, :] for o in owners)
            else:
                for r in range(dil):
                    nat, perm = pl.ds(r, seg, stride=dil), pl.ds(r * seg, seg)
                    owner = (r * seg) // region
                    dqn[nat, :] += dqp[perm, :]
                    dkn[nat, :] += dkp[owner, perm, :]
                    dvn[nat, :] += dvp[owner, perm, :]

        def finish(ci, carry):
            rows = pl.ds(pl.multiple_of(ci * 256, 256), 256)
            first = _first_head(256)
            for src_ref, g_ref, dn, dst_ref, scale, row in (
                    (q_ref, gq_ref, dqn, dq_ref, HEAD_DIM ** -0.5, 0), (k_ref, gk_ref, dkn, dk_ref, 1.0, 1)):
                xv = src_ref[rows, :]
                r = lax.rsqrt(_head_mean(xv * xv, first) + EPS)
                xhat = xv * r
                d = dn[rows, :] * scale
                dg_ref[row:row + 1, :] += jnp.sum(d * xhat, axis=0, keepdims=True)
                dxh = d * g_ref[...]
                dst_ref[rows, :] = (r * (dxh - xhat * _head_mean(dxh * xhat, first))).astype(BF16)
            dv_ref[rows, :] = dvn[rows, :].astype(BF16)
            return carry

        lax.fori_loop(0, seq // 256, finish, 0)

        @pl.when(step == n_seq * n_hp - 1)
        def _():
            _exchange_wait(*ride_args)

    def col(off):
        return pl.BlockSpec((None, seq, LANES), lambda b, hp: (b, 0, col0 + off * n_hp + hp))

    def whole(arr):
        return pl.BlockSpec(arr.shape, lambda b, hp: (0,) * arr.ndim)

    att = pl.BlockSpec((None, seq, LANES), lambda b, hp: (b, 0, hp))
    rows_f32 = pltpu.VMEM((seq, LANES), F32)
    rows_bf16 = pltpu.VMEM((seq, LANES), BF16)
    by_head_bf16 = pltpu.VMEM((2, seq, LANES), BF16)
    per_sub_f32 = pltpu.VMEM((ATT_UNROLL, seq, LANES), F32)
    return _call(
        body, name="attn_bwd", grid=(n_seq, n_hp),
        in_specs=[col(0), col(1), col(2), att, att, att, whole(g_q2), whole(g_k2),
                  pl.BlockSpec((None, 8, LANES), lambda b, hp: (hp, 0, 0)),
                  whole(dms[0]), whole(dms[1]), whole(dms[2])] + [ANY_SPEC] * n_ride,
        out_specs=[att, att, att, pl.BlockSpec((8, LANES), lambda b, hp: (0, 0))] + [ANY_SPEC] * n_ride,
        out_shape=[_sds((n_seq, seq, D_ATT), BF16)] * 3 + [_sds((8, LANES), F32)] + _exchange_shapes(ride),
        scratch_shapes=[rows_f32, rows_f32, by_head_bf16, by_head_bf16, rows_bf16, rows_bf16, rows_f32, rows_f32,
                        rows_f32, per_sub_f32, per_sub_f32, rows_f32, rows_f32, rows_f32] + _exchange_sems(n_ride),
        compiler_params=_params(("arbitrary", "arbitrary")),
    )(proj3, proj3, proj3, do3, y_att3, lse3, g_q2, g_k2, _alibi_rows(), *dms, *[a for a, _ in ride])


def _mix_out(uc2, y_att2, x2, mod8, g_ln, b_ln, g_ffn, w_out, seq, tm=512):
    tokens = x2.shape[0]
    per_seq = seq // tm

    def body(uc_ref, ya_ref, x_ref, m_ref, gl_ref, bl_ref, gf_ref, w_ref, yc_ref, mix_ref, x1_ref, h2_ref):
        uc = uc_ref[...]
        mu = jnp.mean(uc, axis=-1, keepdims=True)
        cen = uc - mu
        rs = lax.rsqrt(jnp.mean(cen * cen, axis=-1, keepdims=True) + EPS)
        z = cen * rs * gl_ref[...] + bl_ref[...]
        yc = (z * _sig(z)).astype(BF16)
        yc_ref[...] = yc
        mix = (jnp.dot(yc, w_ref[pl.ds(0, D_CONV), :], preferred_element_type=F32)
               + jnp.dot(ya_ref[...], w_ref[pl.ds(D_CONV, D_ATT), :], preferred_element_type=F32))
        mix_ref[...] = mix
        x1 = x_ref[...] + m_ref[2:3, :] * mix
        x1_ref[...] = x1
        r = lax.rsqrt(jnp.mean(x1 * x1, axis=-1, keepdims=True) + EPS)
        h2_ref[...] = ((x1 * r * gf_ref[...]) * (1.0 + m_ref[4:5, :]) + m_ref[3:4, :]).astype(BF16)

    def rows(width):
        return pl.BlockSpec((tm, width), lambda i: (i, 0))

    def vec(width):
        return pl.BlockSpec((1, width), lambda i: (0, 0))

    return _call(
        body, name="mix_out", grid=(tokens // tm,),
        in_specs=[rows(D_CONV), rows(D_ATT), rows(D_MODEL),
                  pl.BlockSpec((None, 8, D_MODEL), lambda i: (i // per_seq, 0, 0)),
                  vec(D_CONV), vec(D_CONV), vec(D_MODEL),
                  pl.BlockSpec((D_MODEL, D_MODEL), lambda i: (0, 0))],
        out_specs=[rows(D_CONV), rows(D_MODEL), rows(D_MODEL), rows(D_MODEL)],
        out_shape=[_sds((tokens, D_CONV), BF16), _sds((tokens, D_MODEL), F32),
                   _sds((tokens, D_MODEL), F32), _sds((tokens, D_MODEL), BF16)],
        compiler_params=_params(("parallel",)),
    )(uc2, y_att2, x2, mod8, g_ln, b_ln, g_ffn, w_out)


def _mix_out_bwd(dmix, uc2, g_ln, b_ln, w_out, tm=512):
    tokens = dmix.shape[0]

    def body(dm_ref, uc_ref, gl_ref, bl_ref, w_ref, duc_ref, do_ref, dgb_ref):
        @pl.when(pl.program_id(0) == 0)
        def _():
            dgb_ref[...] = jnp.zeros_like(dgb_ref)

        dmv = dm_ref[...]
        dyc = lax.dot_general(dmv, w_ref[pl.ds(0, D_CONV), :], NT, preferred_element_type=F32)
        do_ref[...] = lax.dot_general(dmv, w_ref[pl.ds(D_CONV, D_ATT), :], NT, preferred_element_type=F32)
        uc = uc_ref[...]
        mu = jnp.mean(uc, axis=-1, keepdims=True)
        cen = uc - mu
        rs = lax.rsqrt(jnp.mean(cen * cen, axis=-1, keepdims=True) + EPS)
        xh = cen * rs
        z = xh * gl_ref[...] + bl_ref[...]
        sg = _sig(z)
        dz = dyc * (sg * (1.0 + z * (1.0 - sg)))
        dgb_ref[0:1, :] += jnp.sum(dz * xh, axis=0, keepdims=True)
        dgb_ref[1:2, :] += jnp.sum(dz, axis=0, keepdims=True)
        dxh = dz * gl_ref[...]
        duc_ref[...] = rs * (dxh - jnp.mean(dxh, axis=-1, keepdims=True)
                             - xh * jnp.mean(dxh * xh, axis=-1, keepdims=True))

    return _call(
        body, name="mix_out_bwd", grid=(tokens // tm,),
        in_specs=[pl.BlockSpec((tm, D_MODEL), lambda i: (i, 0)),
                  pl.BlockSpec((tm, D_CONV), lambda i: (i, 0)),
                  pl.BlockSpec((1, D_CONV), lambda i: (0, 0)),
                  pl.BlockSpec((1, D_CONV), lambda i: (0, 0)),
                  pl.BlockSpec((D_MODEL, D_MODEL), lambda i: (0, 0))],
        out_specs=[pl.BlockSpec((tm, D_CONV), lambda i: (i, 0)),
                   pl.BlockSpec((tm, D_ATT), lambda i: (i, 0)),
                   pl.BlockSpec((8, D_CONV), lambda i: (0, 0))],
        out_shape=[_sds((tokens, D_CONV), F32), _sds((tokens, D_ATT), F32), _sds((8, D_CONV), F32)],
        compiler_params=_params(("arbitrary",)),
    )(dmix, uc2, g_ln, b_ln, w_out)


FF_TILE = 256
FF_ROWS = 256


def _ffn_fwd(h2, w_gate, w_up, w_down, tm=1024):
    tokens = h2.shape[0]

    def body(h_ref, wg_ref, wu_ref, wd_ref, gate_ref, up_ref, f_ref):
        @pl.when(pl.program_id(1) == 0)
        def _():
            f_ref[...] = jnp.zeros_like(f_ref)

        def gate_up(r):
            hv = h_ref[pl.ds(r * FF_ROWS, FF_ROWS), :]
            return (jnp.dot(hv, wg_ref[...], preferred_element_type=F32),
                    jnp.dot(hv, wu_ref[...], preferred_element_type=F32))

        ahead = gate_up(0)
        for r in range(tm // FF_ROWS):
            gate, up = ahead
            if r + 1 < tm // FF_ROWS:
                ahead = gate_up(r + 1)
            rows = pl.ds(r * FF_ROWS, FF_ROWS)
            gate_ref[rows, :] = gate
            up_ref[rows, :] = up
            act = (gate * _sig(gate) * up).astype(BF16)
            f_ref[rows, :] += jnp.dot(act, wd_ref[...], preferred_element_type=F32)

    return _call(
        body, name="ffn_fwd", grid=(tokens // tm, D_FF // FF_TILE),
        in_specs=[pl.BlockSpec((tm, D_MODEL), lambda i, j: (i, 0)),
                  pl.BlockSpec((D_MODEL, FF_TILE), lambda i, j: (0, j)),
                  pl.BlockSpec((D_MODEL, FF_TILE), lambda i, j: (0, j)),
                  pl.BlockSpec((FF_TILE, D_MODEL), lambda i, j: (j, 0))],
        out_specs=[pl.BlockSpec((tm, FF_TILE), lambda i, j: (i, j)),
                   pl.BlockSpec((tm, FF_TILE), lambda i, j: (i, j)),
                   pl.BlockSpec((tm, D_MODEL), lambda i, j: (i, 0))],
        out_shape=[_sds((tokens, D_FF), F32), _sds((tokens, D_FF), F32), _sds((tokens, D_MODEL), F32)],
        compiler_params=_params(("parallel", "arbitrary")),
    )(h2, w_gate, w_up, w_down)


def _ffn_bwd(df, gate, up, w_gate, w_up, w_down, tm=1024):
    tokens = df.shape[0]

    def body(df_ref, gate_ref, up_ref, wg_ref, wu_ref, wd_ref, dgate_ref, dup_ref, act_ref, dh_ref):
        @pl.when(pl.program_id(1) == 0)
        def _():
            dh_ref[...] = jnp.zeros_like(dh_ref)

        def d_act(r):
            return lax.dot_general(df_ref[pl.ds(r * FF_ROWS, FF_ROWS), :], wd_ref[...], NT,
                                   preferred_element_type=F32)

        ahead = d_act(0)
        for r in range(tm // FF_ROWS):
            dact = ahead
            if r + 1 < tm // FF_ROWS:
                ahead = d_act(r + 1)
            rows = pl.ds(r * FF_ROWS, FF_ROWS)
            gate = gate_ref[rows, :]
            up = up_ref[rows, :]
            sg = _sig(gate)
            silu = gate * sg
            act_ref[rows, :] = (silu * up).astype(BF16)
            dup = (dact * silu).astype(BF16)
            dgate = (dact * up * (sg * (1.0 + gate * (1.0 - sg)))).astype(BF16)
            dup_ref[rows, :] = dup
            dgate_ref[rows, :] = dgate
            dh_ref[rows, :] += (lax.dot_general(dgate, wg_ref[...], NT, preferred_element_type=F32)
                                + lax.dot_general(dup, wu_ref[...], NT, preferred_element_type=F32))

    tile = pl.BlockSpec((tm, FF_TILE), lambda i, j: (i, j))
    return _call(
        body, name="ffn_bwd", grid=(tokens // tm, D_FF // FF_TILE),
        in_specs=[pl.BlockSpec((tm, D_MODEL), lambda i, j: (i, 0)), tile, tile,
                  pl.BlockSpec((D_MODEL, FF_TILE), lambda i, j: (0, j)),
                  pl.BlockSpec((D_MODEL, FF_TILE), lambda i, j: (0, j)),
                  pl.BlockSpec((FF_TILE, D_MODEL), lambda i, j: (j, 0))],
        out_specs=[tile, tile, tile, pl.BlockSpec((tm, D_MODEL), lambda i, j: (i, 0))],
        out_shape=[_sds((tokens, D_FF), BF16)] * 3 + [_sds((tokens, D_MODEL), F32)],
        compiler_params=_params(("parallel", "arbitrary")),
    )(df, gate, up, w_gate, w_up, w_down)


def _loss_head(f, x1, target, mod8, seq, tm=512):
    tokens = f.shape[0]
    per_seq = seq // tm
    n_seq = tokens // seq

    def body(f_ref, x1_ref, t_ref, m_ref, dy_ref, df_ref, sq_ref, dgf_ref):
        i = pl.program_id(0)

        @pl.when(i == 0)
        def _():
            sq_ref[...] = jnp.zeros_like(sq_ref)

        @pl.when(i % per_seq == 0)
        def _():
            dgf_ref[...] = jnp.zeros_like(dgf_ref)

        fv = f_ref[...]
        gate = m_ref[5:6, :]
        diff = x1_ref[...] + gate * fv - t_ref[...]
        sq_ref[0:1, :] += jnp.sum(diff * diff, axis=0, keepdims=True)
        dy = diff * (1.0 / D_MODEL)
        dy_ref[...] = dy
        df_ref[...] = (gate * dy).astype(BF16)
        dgf_ref[0:1, :] += jnp.sum(dy * fv, axis=0, keepdims=True)

    rows = pl.BlockSpec((tm, D_MODEL), lambda i: (i, 0))
    per = pl.BlockSpec((None, 8, D_MODEL), lambda i: (i // per_seq, 0, 0))
    return _call(
        body, name="loss_head", grid=(tokens // tm,),
        in_specs=[rows, rows, rows, per],
        out_specs=[rows, rows, pl.BlockSpec((8, D_MODEL), lambda i: (0, 0)), per],
        out_shape=[_sds((tokens, D_MODEL), F32), _sds((tokens, D_MODEL), BF16),
                   _sds((8, D_MODEL), F32), _sds((n_seq, 8, D_MODEL), F32)],
        compiler_params=_params(("arbitrary",)),
    )(f, x1, target, mod8)


def _norm2_bwd(dh2, x1, dy, mix, mod8, g_ffn, seq, tm=512):
    tokens = dh2.shape[0]
    per_seq = seq // tm
    n_seq = tokens // seq

    def body(dh_ref, x1_ref, dy_ref, mix_ref, m_ref, g_ref, dx1_ref, dmix_ref, dg_ref, dm_ref):
        i = pl.program_id(0)

        @pl.when(i == 0)
        def _():
            dg_ref[...] = jnp.zeros_like(dg_ref)

        @pl.when(i % per_seq == 0)
        def _():
            dm_ref[...] = jnp.zeros_like(dm_ref)

        dh = dh_ref[...]
        x1 = x1_ref[...]
        r = lax.rsqrt(jnp.mean(x1 * x1, axis=-1, keepdims=True) + EPS)
        xhat = x1 * r
        g = g_ref[...]
        dm_ref[0:1, :] += jnp.sum(dh, axis=0, keepdims=True)
        dm_ref[1:2, :] += jnp.sum(dh * (xhat * g), axis=0, keepdims=True)
        dn = dh * (1.0 + m_ref[4:5, :])
        dg_ref[0:1, :] += jnp.sum(dn * xhat, axis=0, keepdims=True)
        dxh = dn * g
        dx1 = dy_ref[...] + r * (dxh - xhat * jnp.mean(dxh * xhat, axis=-1, keepdims=True))
        dx1_ref[...] = dx1
        dm_ref[2:3, :] += jnp.sum(dx1 * mix_ref[...], axis=0, keepdims=True)
        dmix_ref[...] = (m_ref[2:3, :] * dx1).astype(BF16)

    rows = pl.BlockSpec((tm, D_MODEL), lambda i: (i, 0))
    per = pl.BlockSpec((None, 8, D_MODEL), lambda i: (i // per_seq, 0, 0))
    return _call(
        body, name="norm2_bwd", grid=(tokens // tm,),
        in_specs=[rows, rows, rows, rows, per, pl.BlockSpec((1, D_MODEL), lambda i: (0, 0))],
        out_specs=[rows, rows, pl.BlockSpec((8, D_MODEL), lambda i: (0, 0)), per],
        out_shape=[_sds((tokens, D_MODEL), F32), _sds((tokens, D_MODEL), BF16),
                   _sds((8, D_MODEL), F32), _sds((n_seq, 8, D_MODEL), F32)],
        compiler_params=_params(("arbitrary",)),
    )(dh2, x1, dy, mix, mod8, g_ffn)


def _mix_in_bwd(d_a, d_g, d_q, d_k, d_v, w_in, x2, dx1, mod8, g_mix, seq, tm=512):
    tokens = x2.shape[0]
    per_seq = seq // tm
    n_seq = tokens // seq
    parts = (d_a, d_g, d_q, d_k, d_v)
    width = D_CONV

    def body(da_ref, dg_ref, dq_ref, dk_ref, dv_ref, w_ref, x_ref, dx1_ref, m_ref, g_ref, gx_ref, dgm_ref, dm_ref):
        i = pl.program_id(0)

        @pl.when(i == 0)
        def _():
            dgm_ref[...] = jnp.zeros_like(dgm_ref)

        @pl.when(i % per_seq == 0)
        def _():
            dm_ref[...] = jnp.zeros_like(dm_ref)

        dh = jnp.zeros((tm, D_MODEL), F32)
        for n, ref in enumerate((da_ref, dg_ref, dq_ref, dk_ref, dv_ref)):
            dh = dh + lax.dot_general(ref[...], w_ref[:, pl.ds(n * width, width)], NT, preferred_element_type=F32)
        xv = x_ref[...]
        r = lax.rsqrt(jnp.mean(xv * xv, axis=-1, keepdims=True) + EPS)
        xhat = xv * r
        g = g_ref[...]
        dm_ref[0:1, :] += jnp.sum(dh, axis=0, keepdims=True)
        dm_ref[1:2, :] += jnp.sum(dh * (xhat * g), axis=0, keepdims=True)
        dn = dh * (1.0 + m_ref[1:2, :])
        dgm_ref[0:1, :] += jnp.sum(dn * xhat, axis=0, keepdims=True)
        dxh = dn * g
        gx_ref[...] = dx1_ref[...] + r * (dxh - xhat * jnp.mean(dxh * xhat, axis=-1, keepdims=True))

    rows = pl.BlockSpec((tm, D_MODEL), lambda i: (i, 0))
    half = pl.BlockSpec((tm, width), lambda i: (i, 0))
    per = pl.BlockSpec((None, 8, D_MODEL), lambda i: (i // per_seq, 0, 0))
    return _call(
        body, name="mix_in_bwd", grid=(tokens // tm,),
        in_specs=[half] * 5 + [pl.BlockSpec((D_MODEL, D_IN), lambda i: (0, 0)), rows, rows, per,
                               pl.BlockSpec((1, D_MODEL), lambda i: (0, 0))],
        out_specs=[rows, pl.BlockSpec((8, D_MODEL), lambda i: (0, 0)), per],
        out_shape=[_sds((tokens, D_MODEL), F32), _sds((8, D_MODEL), F32), _sds((n_seq, 8, D_MODEL), F32)],
        compiler_params=_params(("arbitrary",)),
    )(*parts, w_in, x2, dx1, mod8, g_mix)


def _grad_matmul(a, b, name, tmo, tno, tk=512):
    tokens, m = a.shape
    n = b.shape[1]

    def body(a_ref, b_ref, o_ref):
        @pl.when(pl.program_id(2) == 0)
        def _():
            o_ref[...] = jnp.zeros_like(o_ref)

        o_ref[...] += lax.dot_general(a_ref[...], b_ref[...], TN, preferred_element_type=F32)

    return _call(
        body, name=name, grid=(m // tmo, n // tno, tokens // tk),
        in_specs=[pl.BlockSpec((tk, tmo), lambda i, j, k: (k, i)),
                  pl.BlockSpec((tk, tno), lambda i, j, k: (k, j))],
        out_specs=pl.BlockSpec((tmo, tno), lambda i, j, k: (i, j)),
        out_shape=_sds((m, n), F32),
        compiler_params=_params(("parallel", "parallel", "arbitrary")),
    )(a, b)


def _adamw(w, m, v, g, name, n_parts=0, tr=256):
    rows, cols = w.shape
    tr = min(tr, rows)
    c1 = 1.0 - ADAM_B1 ** ADAM_STEP
    c2 = 1.0 - ADAM_B2 ** ADAM_STEP

    def body(w_ref, m_ref, v_ref, g_ref, go_ref, d_ref, mo_ref, vo_ref):
        if n_parts:
            gv = g_ref[0].astype(F32)
            for p in range(1, n_parts):
                gv = gv + g_ref[p].astype(F32)
        else:
            gv = g_ref[...]
        go_ref[...] = gv
        mn = ADAM_B1 * m_ref[...] + (1.0 - ADAM_B1) * gv
        vn = ADAM_B2 * v_ref[...] + (1.0 - ADAM_B2) * (gv * gv)
        mo_ref[...] = mn
        vo_ref[...] = vn
        d_ref[...] = -ADAM_LR * ((mn / c1) / (jnp.sqrt(vn / c2) + ADAM_EPS) + ADAM_WD * w_ref[...])

    blk = pl.BlockSpec((tr, cols), lambda i: (i, 0))
    g_spec = pl.BlockSpec((n_parts, tr, cols), lambda i: (0, i, 0)) if n_parts else blk
    return _call(
        body, name=name, grid=(rows // tr,),
        in_specs=[blk, blk, blk, g_spec], out_specs=[blk] * 4,
        out_shape=[_sds((rows, cols), F32)] * 4,
        compiler_params=_params(("parallel",)),
    )(w, m, v, g)


def _cols_to_full(blocks):
    n, r, c = blocks.shape
    return jnp.transpose(blocks, (1, 0, 2)).reshape(r, n * c)


def _full_to_cols(full, n=N_DEV):
    r, c = full.shape
    return jnp.transpose(full.reshape(r, n, c // n), (1, 0, 2))


def _pad_lanes(v, width):
    return jnp.pad(v, ((0, 0), (0, width - v.shape[1])))


def kernel(x, c, w_ada, b_ada, g_mix, w_in, w_dw, b_dw, g_conv_ln, b_conv_ln, g_q, g_k, w_out, g_ffn, w_gate, w_up, w_down, loss_target, m_w_ada, m_b_ada, m_g_mix, m_w_in, m_w_dw, m_b_dw, m_g_conv_ln, m_b_conv_ln, m_g_q, m_g_k, m_w_out, m_g_ffn, m_w_gate, m_w_up, m_w_down, v_w_ada, v_b_ada, v_g_mix, v_w_in, v_w_dw, v_b_dw, v_g_conv_ln, v_b_conv_ln, v_g_q, v_g_k, v_w_out, v_g_ffn, v_w_gate, v_w_up, v_w_down):
    n_seq, seq, _ = x.shape
    tokens = n_seq * seq
    me = 4 * lax.axis_index("x") + 2 * lax.axis_index("y") + lax.axis_index("c")
    ada_cols = w_ada.shape[2]
    dw_cols = w_dw.shape[2]

    (c_g, w_in_g, w_dw_g) = _exchange(
        [(c, False), (w_in[0].astype(BF16), False), (w_dw[0], False)], "gather_weights")
    c_all = c_g.reshape(N_DEV * n_seq, D_MODEL)
    w_in_f = _cols_to_full(w_in_g)
    w_dw_f = _cols_to_full(w_dw_g)

    b_cols = lax.dynamic_slice(b_ada, (0, me * ada_cols), (1, ada_cols))
    mod_cols = _ada_fwd(c_all, w_ada[0], b_cols)
    (mod_g,) = _exchange([(mod_cols, False)], "gather_mod")
    mod_mine = lax.dynamic_slice(mod_g, (0, me * n_seq, 0), (N_DEV, n_seq, ada_cols))
    mod = jnp.transpose(mod_mine, (1, 0, 2)).reshape(n_seq, N_MOD, D_MODEL)
    mod8 = jnp.pad(mod, ((0, 0), (0, 8 - N_MOD), (0, 0)))

    x2 = x.reshape(tokens, D_MODEL)
    h1, proj = _mix_in(x2, mod8, g_mix, w_in_f, seq)
    proj3 = proj.reshape(n_seq, seq, D_IN)
    uc3 = _conv_fwd(proj3, w_dw_f, b_dw)
    g_q2, g_k2 = jnp.tile(g_q, (1, 2)), jnp.tile(g_k, (1, 2))
    y_att3, lse3, w_out_g, w_gate_g, w_up_g, w_down_g = _attn_fwd(
        proj3, g_q2, g_k2,
        [(w_out[0].astype(BF16), False), (w_gate[0].astype(BF16), False), (w_up[0].astype(BF16), False),
         (w_down[0].astype(BF16), False)])
    w_out_f = w_out_g.reshape(D_MODEL, D_MODEL)
    w_gate_f = _cols_to_full(w_gate_g)
    w_up_f = _cols_to_full(w_up_g)
    w_down_f = w_down_g.reshape(D_FF, D_MODEL)
    uc2 = uc3.reshape(tokens, D_CONV)
    y_att2 = y_att3.reshape(tokens, D_ATT)
    y_conv, mix, x1, h2 = _mix_out(uc2, y_att2, x2, mod8, g_conv_ln, b_conv_ln, g_ffn, w_out_f, seq)
    gate, up, f = _ffn_fwd(h2, w_gate_f, w_up_f, w_down_f)
    dy, df, sq, dgate_f = _loss_head(f, x1, loss_target.reshape(tokens, D_MODEL), mod8, seq)

    dgate, dup, act, dh2 = _ffn_bwd(df, gate, up, w_gate_f, w_up_f, w_down_f)
    dx1, dmix, dg_ffn, dmod_f = _norm2_bwd(dh2, x1, dy, mix, mod8, g_ffn, seq)
    duc2, do2, dgb_ln = _mix_out_bwd(dmix, uc2, g_conv_ln, b_conv_ln, w_out_f)
    d_a3, d_g3, dw_dw_p, db_dw_p = _conv_bwd(duc2.reshape(n_seq, seq, D_CONV), proj3, w_dw_f)
    gw_gate = _grad_matmul(h2, dgate, "grad_w_gate", D_MODEL, D_FF // 2)
    gw_up = _grad_matmul(h2, dup, "grad_w_up", D_MODEL, D_FF // 2)
    gw_down = _grad_matmul(act, df, "grad_w_down", D_FF // 2, D_MODEL)
    d_q3, d_k3, d_v3, dg_qk, p_gate, p_up, p_down = _attn_bwd(
        proj3, do2.reshape(n_seq, seq, D_ATT), y_att3, lse3, g_q2, g_k2,
        [(_full_to_cols(gw_gate).astype(BF16), True), (_full_to_cols(gw_up).astype(BF16), True),
         (gw_down.astype(BF16).reshape(N_DEV, D_FF // N_DEV, D_MODEL), True)])
    flat = lambda t: t.reshape(tokens, t.shape[-1])
    d_a, d_g, d_q, d_k, d_v = flat(d_a3), flat(d_g3), flat(d_q3), flat(d_k3), flat(d_v3)
    grad_x2, dg_mix, dmod_m = _mix_in_bwd(d_a, d_g, d_q, d_k, d_v, w_in_f, x2, dx1, mod8, g_mix, seq)

    gw_in = jnp.concatenate(
        [_grad_matmul(h1, t, "grad_w_in_%d" % i, D_MODEL, D_CONV) for i, t in enumerate((d_a, d_g, d_q, d_k, d_v))], axis=1)
    gw_out = jnp.concatenate(
        [_grad_matmul(y_conv, dmix, "grad_w_out_conv", D_CONV, D_MODEL),
         _grad_matmul(y_att2, dmix, "grad_w_out_att", D_ATT, D_MODEL)], axis=0)

    dmod = jnp.concatenate([dmod_m[:, 0], dmod_m[:, 1], dmod_f[:, 2], dmod_f[:, 0], dmod_f[:, 1], dgate_f[:, 0]], axis=1)
    dg_q = dg_qk[0:1, 0:HEAD_DIM] + dg_qk[0:1, HEAD_DIM:]
    dg_k = dg_qk[1:2, 0:HEAD_DIM] + dg_qk[1:2, HEAD_DIM:]
    loss_part = (0.5 / D_MODEL) * jnp.sum(sq[0:1, :], axis=1, keepdims=True)
    small = jnp.concatenate(
        [dg_mix[0:1], dg_ffn[0:1], db_dw_p[0:1], dgb_ln[0:1], dgb_ln[1:2],
         _pad_lanes(dg_q, LANES), _pad_lanes(dg_k, LANES), _pad_lanes(loss_part, LANES)], axis=1)
    n_small = small.shape[1] - LANES

    (p_in, p_out, dmod_g, small_g, dw_g) = _exchange(
        [(_full_to_cols(gw_in).astype(BF16), True),
         (gw_out.astype(BF16).reshape(N_DEV, D_MODEL // N_DEV, D_MODEL), True),
         (dmod, False), (small, False), (dw_dw_p, False)], "scatter_grads")

    dmod_all = dmod_g.reshape(N_DEV * n_seq, N_MOD * D_MODEL)
    dmod_cols = lax.dynamic_slice(dmod_all, (0, me * ada_cols), (N_DEV * n_seq, ada_cols))
    gw_ada, gb_ada = _ada_bwd(c_all, dmod_cols, dmod_all)

    res = {}
    res["w_ada"] = _adamw(w_ada[0], m_w_ada[0], v_w_ada[0], gw_ada, "adamw_w_ada")
    res["b_ada"] = _adamw(b_ada, m_b_ada, v_b_ada, gb_ada, "adamw_b_ada")
    res["w_in"] = _adamw(w_in[0], m_w_in[0], v_w_in[0], p_in, "adamw_w_in", N_DEV)
    res["w_out"] = _adamw(w_out[0], m_w_out[0], v_w_out[0], p_out, "adamw_w_out", N_DEV)
    res["w_gate"] = _adamw(w_gate[0], m_w_gate[0], v_w_gate[0], p_gate, "adamw_w_gate", N_DEV)
    res["w_up"] = _adamw(w_up[0], m_w_up[0], v_w_up[0], p_up, "adamw_w_up", N_DEV)
    res["w_down"] = _adamw(w_down[0], m_w_down[0], v_w_down[0], p_down, "adamw_w_down", N_DEV, tr=176)
    dw_mine = lax.dynamic_slice(dw_g, (0, 0, me * dw_cols), (N_DEV, CONV_WIDTH, dw_cols))
    res["w_dw"] = _adamw(w_dw[0], m_w_dw[0], v_w_dw[0], dw_mine, "adamw_w_dw", N_DEV)

    small_names = ["g_mix", "g_ffn", "b_dw", "g_conv_ln", "b_conv_ln", "g_q", "g_k"]
    small_w = {"g_mix": (g_mix, m_g_mix, v_g_mix), "g_ffn": (g_ffn, m_g_ffn, v_g_ffn), "b_dw": (b_dw, m_b_dw, v_b_dw),
               "g_conv_ln": (g_conv_ln, m_g_conv_ln, v_g_conv_ln), "b_conv_ln": (b_conv_ln, m_b_conv_ln, v_b_conv_ln),
               "g_q": (g_q, m_g_q, v_g_q), "g_k": (g_k, m_g_k, v_g_k)}
    widths = [max(small_w[n][0].shape[1], LANES) for n in small_names]
    packed = [jnp.concatenate([_pad_lanes(small_w[n][i], wd) for n, wd in zip(small_names, widths)], axis=1) for i in range(3)]
    outs = _adamw(packed[0], packed[1], packed[2], small_g[:, :, :n_small], "adamw_small", N_DEV)
    off = 0
    for n, wd in zip(small_names, widths):
        real = small_w[n][0].shape[1]
        res[n] = tuple(o[:, off:off + real] for o in outs)
        off += wd
    loss = jnp.sum(small_g[:, 0, n_small])

    order = ["w_ada", "b_ada", "g_mix", "w_in", "w_dw", "b_dw", "g_conv_ln", "b_conv_ln", "g_q", "g_k",
             "w_out", "g_ffn", "w_gate", "w_up", "w_down"]
    lead = {"w_ada", "w_in", "w_dw", "w_out", "w_gate", "w_up", "w_down"}
    grads, deltas, new_m, new_v = [], [], [], []
    for n in order:
        g, d, mn, vn = res[n]
        g, d, mn, vn = (t[None] if n in lead else t for t in (g, d, mn, vn))
        grads.append(g)
        deltas.append(d)
        new_m.append(mn)
        new_v.append(vn)
    return (loss, grad_x2.reshape(n_seq, seq, D_MODEL), *grads, *deltas, *new_m, *new_v)
```

```python
import numpy as np
import jax
import jax.numpy as jnp
from jax import lax
from jax.experimental import pallas as pl
from jax.experimental.pallas import tpu as pltpu

F32 = jnp.float32
BF16 = jnp.bfloat16

N_DEV = 8
D_MODEL = 1024
D_CONV = 512
D_ATT = 512
HEAD_DIM = 64
CONV_WIDTH = 31
D_IN = 2 * D_CONV + 3 * D_ATT
D_FF = 2816
N_MOD = 6
EPS = 1e-6
RADIUS = 64
DILATIONS = (1, 4, 16)
Q_BLOCK = 128
LANES = 128
VMEM_LIMIT = 56 * 1024 * 1024

ADAM_LR = 0.001
ADAM_B1 = 0.9
ADAM_B2 = 0.999
ADAM_EPS = 1e-08
ADAM_WD = 0.01
ADAM_STEP = 10

NT = (((1,), (1,)), ((), ()))
TN = (((0,), (0,)), ((), ()))


def _call(body, **kw):
    return pl.pallas_call(body, **kw)


def _params(sem=None, vmem=VMEM_LIMIT):
    return pltpu.CompilerParams(dimension_semantics=sem, vmem_limit_bytes=vmem)


def _sig(x):
    return 1.0 / (1.0 + jnp.exp(-x))


def _sds(shape, dtype):
    return jax.ShapeDtypeStruct(shape, dtype)


N_PEER = N_DEV - 1
ANY_SPEC = pl.BlockSpec(memory_space=pl.ANY)


def _exchange_copies(scatter, ins, outs, *sems):
    n = len(ins)
    if n == 0:
        return [], []
    send_sems, recv_sems, local_sems = sems
    x, y, c = lax.axis_index("x"), lax.axis_index("y"), lax.axis_index("c")
    me = 4 * x + 2 * y + c

    def src(a, slot):
        return ins[a].at[slot] if scatter[a] else ins[a]

    local = [pltpu.make_async_copy(src(a, me), outs[a].at[me], local_sems.at[a]) for a in range(n)]
    flights = []
    for k in range(1, N_DEV):
        px = 1 - x if k & 4 else x
        py = 1 - y if k & 2 else y
        pc = 1 - c if k & 1 else c
        pid = 4 * px + 2 * py + pc
        for a in range(n):
            i = a * N_PEER + k - 1
            send, recv = (pltpu.make_async_remote_copy(
                src_ref=src(a, pid), dst_ref=outs[a].at[slot],
                send_sem=send_sems.at[i], recv_sem=recv_sems.at[i],
                device_id=(px, py, pc), device_id_type=pl.DeviceIdType.MESH) for slot in (me, pid))
            flights.append((send, recv))
    return local, flights


def _exchange_start(*args):
    local, flights = _exchange_copies(*args)
    for cp in local:
        cp.start()
    for send, _ in flights:
        send.start()


def _exchange_wait(*args):
    local, flights = _exchange_copies(*args)
    for send, recv in flights:
        send.wait_send()
        recv.wait_recv()
    for cp in local:
        cp.wait()


def _exchange_shapes(items):
    return [_sds((N_DEV,) + tuple(arr.shape[1:] if scatter else arr.shape), arr.dtype) for arr, scatter in items]


def _exchange_sems(n):
    if n == 0:
        return []
    return [pltpu.SemaphoreType.DMA((n * N_PEER,)), pltpu.SemaphoreType.DMA((n * N_PEER,)),
            pltpu.SemaphoreType.DMA((n,))]


def _exchange(items, name):
    n = len(items)
    scatter = [s for _, s in items]

    def body(*refs):
        args = (scatter, refs[:n], refs[n:2 * n]) + tuple(refs[2 * n:])
        _exchange_start(*args)
        _exchange_wait(*args)

    return _call(
        body, name=name, out_shape=_exchange_shapes(items),
        in_specs=[ANY_SPEC] * n, out_specs=[ANY_SPEC] * n, scratch_shapes=_exchange_sems(n),
    )(*[a for a, _ in items])


def _ada_fwd(c_all, w_ada, b_cols):
    def body(c_ref, w_ref, b_ref, o_ref):
        cv = c_ref[...]
        sc = (cv * _sig(cv)).astype(BF16)
        o_ref[...] = jnp.dot(sc, w_ref[...].astype(BF16), preferred_element_type=F32) + b_ref[...]

    return _call(body, name="ada_fwd", out_shape=_sds((c_all.shape[0], w_ada.shape[1]), F32),
                 compiler_params=_params())(c_all, w_ada, b_cols)


def _ada_bwd(c_all, dmod_cols, dmod_all):
    def body(c_ref, dc_ref, da_ref, gw_ref, gb_ref):
        cv = c_ref[...]
        sc = (cv * _sig(cv)).astype(BF16)
        gw_ref[...] = lax.dot_general(sc, dc_ref[...].astype(BF16), TN, preferred_element_type=F32)
        gb_ref[...] = jnp.sum(da_ref[...], axis=0, keepdims=True)

    return _call(body, name="ada_bwd",
                 out_shape=[_sds((c_all.shape[1], dmod_cols.shape[1]), F32), _sds((1, dmod_all.shape[1]), F32)],
                 compiler_params=_params())(c_all, dmod_cols, dmod_all)


def _mix_in(x2, mod8, g_mix, w_in, seq, tm=512):
    tokens = x2.shape[0]
    per_seq = seq // tm

    def body(x_ref, m_ref, g_ref, w_ref, h_ref, p_ref):
        xv = x_ref[...]
        r = lax.rsqrt(jnp.mean(xv * xv, axis=-1, keepdims=True) + EPS)
        h = (xv * r * g_ref[...]) * (1.0 + m_ref[1:2, :]) + m_ref[0:1, :]
        hb = h.astype(BF16)
        h_ref[...] = hb
        p_ref[...] = jnp.dot(hb, w_ref[...], preferred_element_type=F32)

    return _call(
        body, name="mix_in", grid=(tokens // tm,),
        in_specs=[pl.BlockSpec((tm, D_MODEL), lambda i: (i, 0)),
                  pl.BlockSpec((None, 8, D_MODEL), lambda i: (i // per_seq, 0, 0)),
                  pl.BlockSpec((1, D_MODEL), lambda i: (0, 0)),
                  pl.BlockSpec((D_MODEL, D_IN), lambda i: (0, 0))],
        out_specs=[pl.BlockSpec((tm, D_MODEL), lambda i: (i, 0)),
                   pl.BlockSpec((tm, D_IN), lambda i: (i, 0))],
        out_shape=[_sds((tokens, D_MODEL), BF16), _sds((tokens, D_IN), F32)],
        compiler_params=_params(("parallel",)),
    )(x2, mod8, g_mix, w_in)


CONV_ROWS = 32
CONV_HALO = 16


def _fill_shifted(xp, sh, seq):
    for b in range(8):
        sh[b, pl.ds(0, seq + 24), :] = xp[pl.ds(b, seq + 24), :]


def _conv_fwd(proj3, w_dw, b_dw):
    n_seq, seq, _ = proj3.shape
    n_cb = D_CONV // LANES

    def body(a_ref, g_ref, w_ref, b_ref, uc_ref, xp, sh):
        zeros = jnp.zeros((CONV_HALO, LANES), F32)
        xp[pl.ds(0, CONV_HALO), :] = zeros
        xp[pl.ds(CONV_HALO + seq, CONV_HALO), :] = zeros
        xp[pl.ds(CONV_HALO, seq), :] = a_ref[...] * _sig(g_ref[...])
        _fill_shifted(xp, sh, seq)

        def blk(i, carry):
            t0 = pl.multiple_of(i * CONV_ROWS, CONV_ROWS)
            acc = jnp.zeros((CONV_ROWS, LANES), F32)
            for j in range(CONV_WIDTH):
                jj = j + 1
                acc = acc + sh[jj % 8, pl.ds(t0 + 8 * (jj // 8), CONV_ROWS), :] * w_ref[j:j + 1, :]
            uc_ref[pl.ds(t0, CONV_ROWS), :] = acc + b_ref[...]
            return carry

        lax.fori_loop(0, seq // CONV_ROWS, blk, 0)

    return _call(
        body, name="conv_fwd", grid=(n_seq, n_cb),
        in_specs=[pl.BlockSpec((None, seq, LANES), lambda b, cb: (b, 0, cb)),
                  pl.BlockSpec((None, seq, LANES), lambda b, cb: (b, 0, n_cb + cb)),
                  pl.BlockSpec((CONV_WIDTH, LANES), lambda b, cb: (0, cb)),
                  pl.BlockSpec((1, LANES), lambda b, cb: (0, cb))],
        out_specs=pl.BlockSpec((None, seq, LANES), lambda b, cb: (b, 0, cb)),
        out_shape=_sds((n_seq, seq, D_CONV), F32),
        scratch_shapes=[pltpu.VMEM((seq + 2 * CONV_HALO, LANES), F32),
                        pltpu.VMEM((8, seq + 2 * CONV_HALO, LANES), F32)],
        compiler_params=_params(("parallel", "parallel")),
    )(proj3, proj3, w_dw, b_dw)


def _conv_bwd(duc3, proj3, w_dw):
    n_seq, seq, _ = proj3.shape
    n_cb = D_CONV // LANES

    def body(duc_ref, a_ref, g_ref, w_ref, da_ref, dg_ref, dw_ref, db_ref, xp, sh):
        @pl.when(pl.program_id(1) == 0)
        def _():
            dw_ref[...] = jnp.zeros_like(dw_ref)
            db_ref[...] = jnp.zeros_like(db_ref)

        zeros = jnp.zeros((CONV_HALO, LANES), F32)
        xp[pl.ds(0, CONV_HALO), :] = zeros
        xp[pl.ds(CONV_HALO + seq, CONV_HALO), :] = zeros
        xp[pl.ds(CONV_HALO, seq), :] = a_ref[...] * _sig(g_ref[...])
        _fill_shifted(xp, sh, seq)
        for j0 in range(0, CONV_WIDTH, 8):
            taps = range(j0, min(j0 + 8, CONV_WIDTH))

            def wblk(i, accs, taps=taps):
                t0 = pl.multiple_of(i * CONV_ROWS, CONV_ROWS)
                d = duc_ref[pl.ds(t0, CONV_ROWS), :]
                return tuple(acc + d * sh[(j + 1) % 8, pl.ds(t0 + 8 * ((j + 1) // 8), CONV_ROWS), :]
                             for acc, j in zip(accs, taps))

            accs = lax.fori_loop(0, seq // CONV_ROWS, wblk,
                                 tuple(jnp.zeros((CONV_ROWS, LANES), F32) for _ in taps))
            for acc, j in zip(accs, taps):
                dw_ref[j:j + 1, :] += jnp.sum(acc, axis=0, keepdims=True)
        db_ref[0:1, :] += jnp.sum(duc_ref[...], axis=0, keepdims=True)
        xp[pl.ds(CONV_HALO, seq), :] = duc_ref[...]
        _fill_shifted(xp, sh, seq)

        def ublk(i, carry):
            t0 = pl.multiple_of(i * CONV_ROWS, CONV_ROWS)
            acc = jnp.zeros((CONV_ROWS, LANES), F32)
            for j in range(CONV_WIDTH):
                jj = CONV_WIDTH - j
                acc = acc + sh[jj % 8, pl.ds(t0 + 8 * (jj // 8), CONV_ROWS), :] * w_ref[j:j + 1, :]
            av = a_ref[pl.ds(t0, CONV_ROWS), :]
            sg = _sig(g_ref[pl.ds(t0, CONV_ROWS), :])
            da_ref[pl.ds(t0, CONV_ROWS), :] = (acc * sg).astype(BF16)
            dg_ref[pl.ds(t0, CONV_ROWS), :] = (acc * av * sg * (1.0 - sg)).astype(BF16)
            return carry

        lax.fori_loop(0, seq // CONV_ROWS, ublk, 0)

    return _call(
        body, name="conv_bwd", grid=(n_cb, n_seq),
        in_specs=[pl.BlockSpec((None, seq, LANES), lambda cb, b: (b, 0, cb)),
                  pl.BlockSpec((None, seq, LANES), lambda cb, b: (b, 0, cb)),
                  pl.BlockSpec((None, seq, LANES), lambda cb, b: (b, 0, n_cb + cb)),
                  pl.BlockSpec((CONV_WIDTH, LANES), lambda cb, b: (0, cb))],
        out_specs=[pl.BlockSpec((None, seq, LANES), lambda cb, b: (b, 0, cb)),
                   pl.BlockSpec((None, seq, LANES), lambda cb, b: (b, 0, cb)),
                   pl.BlockSpec((32, LANES), lambda cb, b: (0, cb)),
                   pl.BlockSpec((8, LANES), lambda cb, b: (0, cb))],
        out_shape=[_sds((n_seq, seq, D_CONV), BF16), _sds((n_seq, seq, D_CONV), BF16),
                   _sds((32, D_CONV), F32), _sds((8, D_CONV), F32)],
        scratch_shapes=[pltpu.VMEM((seq + 2 * CONV_HALO, LANES), F32),
                        pltpu.VMEM((8, seq + 2 * CONV_HALO, LANES), F32)],
        compiler_params=_params(("parallel", "arbitrary")),
    )(duc3, proj3, proj3, w_dw)


MASKED = 1e30
ATT_UNROLL = 4


def _distance_mats(dil, seg_len):
    kw = min(2 * Q_BLOCK, seg_len)
    offsets = (0, -RADIUS, -2 * RADIUS) if kw == 2 * Q_BLOCK else (0,)
    a = np.arange(Q_BLOCK)[:, None]
    b = np.arange(kw)[None, :]
    mats = []
    for off in offsets:
        rel = np.abs(b + off - a)
        mats.append(np.where(rel <= RADIUS, dil * rel, MASKED))
    return jnp.asarray(np.stack(mats).astype(np.float32))


def _alibi_rows():
    s = np.zeros((4, 8, LANES), np.float32)
    for hp in range(4):
        for hl in range(2):
            s[hp, hl, :] = 2.0 ** (-(2 * hp + hl + 1))
    return jnp.asarray(s)


def _window(n, seg_len):
    i0 = pl.multiple_of(n * Q_BLOCK, Q_BLOCK)
    if seg_len <= Q_BLOCK:
        return i0, i0, 0
    per_seg = seg_len // Q_BLOCK
    j = n % per_seg
    seg0 = (n // per_seg) * seg_len
    ks_local = jnp.clip(j * Q_BLOCK - RADIUS, 0, seg_len - 2 * Q_BLOCK)
    ks = pl.multiple_of(seg0 + ks_local, RADIUS)
    var = jnp.where(j == 0, 0, jnp.where(j == per_seg - 1, 2, 1))
    return i0, ks, var


def _first_head(rows):
    return lax.broadcasted_iota(jnp.int32, (rows, LANES), 1) < HEAD_DIM


def _head_mean(x, first):
    s0 = jnp.sum(jnp.where(first, x, 0.0), axis=1, keepdims=True)
    s1 = jnp.sum(jnp.where(first, 0.0, x), axis=1, keepdims=True)
    return jnp.where(first, s0, s1) * (1.0 / HEAD_DIM)


def _permute_rows(dst, src, dil, seq):
    seg = seq // dil
    for r in range(dil):
        rows = pl.ds(r, seg, stride=dil) if dil > 1 else pl.ds(0, seq)
        dst[pl.ds(r * seg, seg), :] = src[rows, :].astype(dst.dtype)


def _permute_rows_by_head(dst, src, dil, seq):
    seg = seq // dil
    first = _first_head(seg)
    for r in range(dil):
        rows = pl.ds(r, seg, stride=dil) if dil > 1 else pl.ds(0, seq)
        val = src[rows, :]
        dst[0, pl.ds(r * seg, seg), :] = jnp.where(first, val, 0.0).astype(dst.dtype)
        dst[1, pl.ds(r * seg, seg), :] = jnp.where(first, 0.0, val).astype(dst.dtype)


def _qk_normalise(q_ref, g2_ref, dst, seq, scale):
    def chunk(ci, carry):
        rows = pl.ds(pl.multiple_of(ci * 256, 256), 256)
        qv = q_ref[rows, :]
        r = lax.rsqrt(_head_mean(qv * qv, _first_head(256)) + EPS)
        dst[rows, :] = qv * r * (g2_ref[...] * scale)
        return carry

    lax.fori_loop(0, seq // 256, chunk, 0)


def _attn_fwd(proj3, g_q2, g_k2, ride):
    n_seq, seq, _ = proj3.shape
    dms = [_distance_mats(d, seq // d) for d in DILATIONS]
    col0 = 2 * D_CONV // LANES
    n_hp = D_ATT // LANES

    n_ride = len(ride)
    ride_scatter = [s for _, s in ride]

    def body(*refs):
        q_ref, k_ref, v_ref, gq_ref, gk_ref, sl_ref, dm1, dm4, dm16 = refs[:9]
        ride_in = refs[9:9 + n_ride]
        y_ref, lse_ref = refs[9 + n_ride:11 + n_ride]
        ride_out = refs[11 + n_ride:11 + 2 * n_ride]
        (qf, kf, qp, kp, vp, oml_p, o1, o4, o16, m1, m4, m16, l1, l4, l16) = refs[11 + 2 * n_ride:26 + 2 * n_ride]
        o_nat, m_nat, l_nat = (o1, o4, o16), (m1, m4, m16), (l1, l4, l16)
        ride_args = (ride_scatter, ride_in, ride_out) + tuple(refs[26 + 2 * n_ride:])
        step = pl.program_id(0) * n_hp + pl.program_id(1)

        @pl.when(step == 0)
        def _():
            _exchange_start(*ride_args)

        dm_refs = (dm1, dm4, dm16)
        _qk_normalise(q_ref, gq_ref, qf, seq, HEAD_DIM ** -0.5)
        _qk_normalise(k_ref, gk_ref, kf, seq, 1.0)
        slopes = (sl_ref[0:1, 0:1], sl_ref[1:2, 0:1])
        for pi, dil in enumerate(DILATIONS):
            seg = seq // dil
            kw = min(2 * Q_BLOCK, seg)
            _permute_rows_by_head(qp, qf, dil, seq)
            _permute_rows(kp, kf, dil, seq)
            _permute_rows(vp, v_ref, dil, seq)

            def blk(it, carry, seg=seg, kw=kw, pi=pi, dst=oml_p):
                first = _first_head(Q_BLOCK)
                chains = [(sub, h) for sub in range(ATT_UNROLL) for h in range(2)]
                win = [_window(it * ATT_UNROLL + sub, seg) for sub in range(ATT_UNROLL)]
                s = {}
                for sub, h in chains:
                    i0, ks, var = win[sub]
                    s[sub, h] = lax.dot_general(qp[h, pl.ds(i0, Q_BLOCK), :], kp[pl.ds(ks, kw), :], NT,
                                                preferred_element_type=F32) - slopes[h] * dm_refs[pi][var]
                m, l, p = {}, {}, {}
                for c in chains:
                    m[c] = jnp.max(s[c], axis=1, keepdims=True)
                    e = jnp.exp(s[c] - m[c])
                    l[c] = jnp.sum(e, axis=1, keepdims=True)
                    p[c] = e.astype(BF16)
                o = {}
                for sub, h in chains:
                    o[sub, h] = jnp.dot(p[sub, h], vp[pl.ds(win[sub][1], kw), :], preferred_element_type=F32)
                packed = [jnp.concatenate([jnp.where(first, t[sub, 0], t[sub, 1]) for t in (o, m, l)], axis=1)
                          for sub in range(ATT_UNROLL)]
                rows = pl.ds(pl.multiple_of(it * (ATT_UNROLL * Q_BLOCK), ATT_UNROLL * Q_BLOCK), ATT_UNROLL * Q_BLOCK)
                dst[rows, :] = jnp.concatenate(packed, axis=0)
                return carry

            lax.fori_loop(0, seq // (Q_BLOCK * ATT_UNROLL), blk, 0)
            for r in range(dil):
                nat = pl.ds(r, seg, stride=dil) if dil > 1 else pl.ds(0, seq)
                perm = pl.ds(r * seg, seg)
                o_nat[pi][nat, :] = oml_p[perm, pl.ds(0, LANES)]
                m_nat[pi][nat, :] = oml_p[perm, pl.ds(LANES, LANES)]
                l_nat[pi][nat, :] = oml_p[perm, pl.ds(2 * LANES, LANES)]

        def merge(ci, carry):
            rows = pl.ds(pl.multiple_of(ci * 256, 256), 256)
            ms = [m_nat[pi][rows, :] for pi in range(3)]
            m_all = jnp.maximum(jnp.maximum(ms[0], ms[1]), ms[2])
            es = [jnp.exp(m - m_all) for m in ms]
            l_all = sum(l_nat[pi][rows, :] * es[pi] for pi in range(3))
            inv = 1.0 / l_all
            o = sum(o_nat[pi][rows, :] * (es[pi] * inv) for pi in range(3))
            y_ref[rows, :] = o.astype(BF16)
            lse_ref[rows, :] = m_all + jnp.log(l_all)
            return carry

        lax.fori_loop(0, seq // 256, merge, 0)

        @pl.when(step == n_seq * n_hp - 1)
        def _():
            _exchange_wait(*ride_args)

    def col(off):
        return pl.BlockSpec((None, seq, LANES), lambda b, hp: (b, 0, col0 + off * n_hp + hp))

    def whole(arr):
        return pl.BlockSpec(arr.shape, lambda b, hp: (0,) * arr.ndim)

    rows_f32 = pltpu.VMEM((seq, LANES), F32)
    rows_bf16 = pltpu.VMEM((seq, LANES), BF16)
    return _call(
        body, name="attn_fwd", grid=(n_seq, n_hp),
        in_specs=[col(0), col(1), col(2), whole(g_q2), whole(g_k2),
                  pl.BlockSpec((None, 8, LANES), lambda b, hp: (hp, 0, 0)),
                  whole(dms[0]), whole(dms[1]), whole(dms[2])] + [ANY_SPEC] * n_ride,
        out_specs=[pl.BlockSpec((None, seq, LANES), lambda b, hp: (b, 0, hp)),
                   pl.BlockSpec((None, seq, LANES), lambda b, hp: (b, 0, hp))] + [ANY_SPEC] * n_ride,
        out_shape=[_sds((n_seq, seq, D_ATT), BF16), _sds((n_seq, seq, D_ATT), F32)] + _exchange_shapes(ride),
        scratch_shapes=[rows_f32, rows_f32, pltpu.VMEM((2, seq, LANES), BF16), rows_bf16, rows_bf16]
        + [pltpu.VMEM((seq, 3 * LANES), F32)] + [rows_f32] * 9 + _exchange_sems(n_ride),
        compiler_params=_params(("arbitrary", "arbitrary")),
    )(proj3, proj3, proj3, g_q2, g_k2, _alibi_rows(), *dms, *[a for a, _ in ride])


def _attn_bwd(proj3, do3, y_att3, lse3, g_q2, g_k2, ride):
    n_seq, seq, _ = proj3.shape
    dms = [_distance_mats(d, seq // d) for d in DILATIONS]
    col0 = 2 * D_CONV // LANES
    n_hp = D_ATT // LANES

    n_ride = len(ride)
    ride_scatter = [s for _, s in ride]

    def body(*refs):
        q_ref, k_ref, v_ref, do_ref, o_ref, lse_ref, gq_ref, gk_ref, sl_ref, dm1, dm4, dm16 = refs[:12]
        ride_in = refs[12:12 + n_ride]
        dq_ref, dk_ref, dv_ref, dg_ref = refs[12 + n_ride:16 + n_ride]
        ride_out = refs[16 + n_ride:16 + 2 * n_ride]
        (qf, kf, qp, dop, kp, vp, sn, sp, dqp, dkp, dvp, dqn, dkn, dvn) = refs[16 + 2 * n_ride:30 + 2 * n_ride]
        ride_args = (ride_scatter, ride_in, ride_out) + tuple(refs[30 + 2 * n_ride:])
        dm_refs = (dm1, dm4, dm16)
        step = pl.program_id(0) * n_hp + pl.program_id(1)

        @pl.when(step == 0)
        def _():
            _exchange_start(*ride_args)
            dg_ref[...] = jnp.zeros_like(dg_ref)

        _qk_normalise(q_ref, gq_ref, qf, seq, HEAD_DIM ** -0.5)
        _qk_normalise(k_ref, gk_ref, kf, seq, 1.0)

        def stats(ci, carry):
            rows = pl.ds(pl.multiple_of(ci * 256, 256), 256)
            first = _first_head(256)
            prod = do_ref[rows, :] * o_ref[rows, :].astype(F32)
            sn[rows, 0:1] = lse_ref[rows, 0:1]
            sn[rows, 1:2] = lse_ref[rows, HEAD_DIM:HEAD_DIM + 1]
            sn[rows, 2:3] = jnp.sum(jnp.where(first, prod, 0.0), axis=1, keepdims=True)
            sn[rows, 3:4] = jnp.sum(jnp.where(first, 0.0, prod), axis=1, keepdims=True)
            return carry

        lax.fori_loop(0, seq // 256, stats, 0)
        slopes = (sl_ref[0:1, 0:1], sl_ref[1:2, 0:1])
        half = seq // (Q_BLOCK * ATT_UNROLL)
        region = seq // ATT_UNROLL

        for pi, dil in enumerate(DILATIONS):
            seg = seq // dil
            kw = min(2 * Q_BLOCK, seg)
            _permute_rows_by_head(qp, qf, dil, seq)
            _permute_rows_by_head(dop, do_ref, dil, seq)
            _permute_rows(kp, kf, dil, seq)
            _permute_rows(vp, v_ref, dil, seq)
            _permute_rows(sp, sn, dil, seq)
            for sub in range(ATT_UNROLL):
                lo, hi = max(sub * region - RADIUS, 0), min((sub + 1) * region + RADIUS, seq)
                dkp[sub, pl.ds(lo, hi - lo), :] = jnp.zeros((hi - lo, LANES), F32)
                dvp[sub, pl.ds(lo, hi - lo), :] = jnp.zeros((hi - lo, LANES), F32)

            def blk(it, carry, seg=seg, kw=kw, pi=pi):
                first = _first_head(Q_BLOCK)
                chains = [(sub, h) for sub in range(ATT_UNROLL) for h in range(2)]
                win = [_window(it + sub * half, seg) for sub in range(ATT_UNROLL)]
                qrows = [pl.ds(w[0], Q_BLOCK) for w in win]
                krows = [pl.ds(w[1], kw) for w in win]
                s, dp = {}, {}
                for sub, h in chains:
                    s[sub, h] = lax.dot_general(qp[h, qrows[sub], :], kp[krows[sub], :], NT,
                                                preferred_element_type=F32) - slopes[h] * dm_refs[pi][win[sub][2]]
                    dp[sub, h] = lax.dot_general(dop[h, qrows[sub], :], vp[krows[sub], :], NT,
                                                 preferred_element_type=F32)
                p, ds = {}, {}
                for sub, h in chains:
                    e = jnp.exp(s[sub, h] - sp[qrows[sub], h:h + 1])
                    ds[sub, h] = (e * (dp[sub, h] - sp[qrows[sub], 2 + h:3 + h])).astype(BF16)
                    p[sub, h] = e.astype(BF16)
                dq, dk, dv = {}, {}, {}
                for sub, h in chains:
                    dq[sub, h] = jnp.dot(ds[sub, h], kp[krows[sub], :], preferred_element_type=F32)
                    dk[sub, h] = lax.dot_general(ds[sub, h], qp[h, qrows[sub], :], TN, preferred_element_type=F32)
                    dv[sub, h] = lax.dot_general(p[sub, h], dop[h, qrows[sub], :], TN, preferred_element_type=F32)
                for sub in range(ATT_UNROLL):
                    dqp[qrows[sub], :] = jnp.where(first, dq[sub, 0], dq[sub, 1])
                    dkp[sub, krows[sub], :] += dk[sub, 0] + dk[sub, 1]
                    dvp[sub, krows[sub], :] += dv[sub, 0] + dv[sub, 1]
                return carry

            lax.fori_loop(0, half, blk, 0)
            if dil == 1:
                for sub in range(ATT_UNROLL):
                    r0 = sub * region
                    pieces = [(r0, RADIUS, [sub - 1, sub] if sub > 0 else [sub]),
                              (r0 + RADIUS, region - 2 * RADIUS, [sub]),
                              (r0 + region - RADIUS, RADIUS, [sub, sub + 1] if sub < ATT_UNROLL - 1 else [sub])]
                    for start, size, owners in pieces:
                        rows = pl.ds(start, size)
                        dqn[rows, :] = dqp[rows, :]
                        dkn[rows, :] = sum(dkp[o, rows, :] for o in owners)
                        dvn[rows, :] = sum(dvp[o, rows, :] for o in owners)
            else:
                for r in range(dil):
                    nat, perm = pl.ds(r, seg, stride=dil), pl.ds(r * seg, seg)
                    owner = (r * seg) // region
                    dqn[nat, :] += dqp[perm, :]
                    dkn[nat, :] += dkp[owner, perm, :]
                    dvn[nat, :] += dvp[owner, perm, :]

        def finish(ci, carry):
            rows = pl.ds(pl.multiple_of(ci * 256, 256), 256)
            first = _first_head(256)
            for src_ref, g_ref, dn, dst_ref, scale, row in (
                    (q_ref, gq_ref, dqn, dq_ref, HEAD_DIM ** -0.5, 0), (k_ref, gk_ref, dkn, dk_ref, 1.0, 1)):
                xv = src_ref[rows, :]
                r = lax.rsqrt(_head_mean(xv * xv, first) + EPS)
                xhat = xv * r
                d = dn[rows, :] * scale
                dg_ref[row:row + 1, :] += jnp.sum(d * xhat, axis=0, keepdims=True)
                dxh = d * g_ref[...]
                dst_ref[rows, :] = (r * (dxh - xhat * _head_mean(dxh * xhat, first))).astype(BF16)
            dv_ref[rows, :] = dvn[rows, :].astype(BF16)
            return carry

        lax.fori_loop(0, seq // 256, finish, 0)

        @pl.when(step == n_seq * n_hp - 1)
        def _():
            _exchange_wait(*ride_args)

    def col(off):
        return pl.BlockSpec((None, seq, LANES), lambda b, hp: (b, 0, col0 + off * n_hp + hp))

    def whole(arr):
        return pl.BlockSpec(arr.shape, lambda b, hp: (0,) * arr.ndim)

    att = pl.BlockSpec((None, seq, LANES), lambda b, hp: (b, 0, hp))
    rows_f32 = pltpu.VMEM((seq, LANES), F32)
    rows_bf16 = pltpu.VMEM((seq, LANES), BF16)
    by_head_bf16 = pltpu.VMEM((2, seq, LANES), BF16)
    per_sub_f32 = pltpu.VMEM((ATT_UNROLL, seq, LANES), F32)
    return _call(
        body, name="attn_bwd", grid=(n_seq, n_hp),
        in_specs=[col(0), col(1), col(2), att, att, att, whole(g_q2), whole(g_k2),
                  pl.BlockSpec((None, 8, LANES), lambda b, hp: (hp, 0, 0)),
                  whole(dms[0]), whole(dms[1]), whole(dms[2])] + [ANY_SPEC] * n_ride,
        out_specs=[att, att, att, pl.BlockSpec((8, LANES), lambda b, hp: (0, 0))] + [ANY_SPEC] * n_ride,
        out_shape=[_sds((n_seq, seq, D_ATT), BF16)] * 3 + [_sds((8, LANES), F32)] + _exchange_shapes(ride),
        scratch_shapes=[rows_f32, rows_f32, by_head_bf16, by_head_bf16, rows_bf16, rows_bf16, rows_f32, rows_f32,
                        rows_f32, per_sub_f32, per_sub_f32, rows_f32, rows_f32, rows_f32] + _exchange_sems(n_ride),
        compiler_params=_params(("arbitrary", "arbitrary")),
    )(proj3, proj3, proj3, do3, y_att3, lse3, g_q2, g_k2, _alibi_rows(), *dms, *[a for a, _ in ride])


def _mix_out(uc2, y_att2, x2, mod8, g_ln, b_ln, g_ffn, w_out, seq, tm=512):
    tokens = x2.shape[0]
    per_seq = seq // tm

    def body(uc_ref, ya_ref, x_ref, m_ref, gl_ref, bl_ref, gf_ref, w_ref, yc_ref, mix_ref, x1_ref, h2_ref):
        uc = uc_ref[...]
        mu = jnp.mean(uc, axis=-1, keepdims=True)
        cen = uc - mu
        rs = lax.rsqrt(jnp.mean(cen * cen, axis=-1, keepdims=True) + EPS)
        z = cen * rs * gl_ref[...] + bl_ref[...]
        yc = (z * _sig(z)).astype(BF16)
        yc_ref[...] = yc
        mix = (jnp.dot(yc, w_ref[pl.ds(0, D_CONV), :], preferred_element_type=F32)
               + jnp.dot(ya_ref[...], w_ref[pl.ds(D_CONV, D_ATT), :], preferred_element_type=F32))
        mix_ref[...] = mix
        x1 = x_ref[...] + m_ref[2:3, :] * mix
        x1_ref[...] = x1
        r = lax.rsqrt(jnp.mean(x1 * x1, axis=-1, keepdims=True) + EPS)
        h2_ref[...] = ((x1 * r * gf_ref[...]) * (1.0 + m_ref[4:5, :]) + m_ref[3:4, :]).astype(BF16)

    def rows(width):
        return pl.BlockSpec((tm, width), lambda i: (i, 0))

    def vec(width):
        return pl.BlockSpec((1, width), lambda i: (0, 0))

    return _call(
        body, name="mix_out", grid=(tokens // tm,),
        in_specs=[rows(D_CONV), rows(D_ATT), rows(D_MODEL),
                  pl.BlockSpec((None, 8, D_MODEL), lambda i: (i // per_seq, 0, 0)),
                  vec(D_CONV), vec(D_CONV), vec(D_MODEL),
                  pl.BlockSpec((D_MODEL, D_MODEL), lambda i: (0, 0))],
        out_specs=[rows(D_CONV), rows(D_MODEL), rows(D_MODEL), rows(D_MODEL)],
        out_shape=[_sds((tokens, D_CONV), BF16), _sds((tokens, D_MODEL), F32),
                   _sds((tokens, D_MODEL), F32), _sds((tokens, D_MODEL), BF16)],
        compiler_params=_params(("parallel",)),
    )(uc2, y_att2, x2, mod8, g_ln, b_ln, g_ffn, w_out)


def _mix_out_bwd(dmix, uc2, g_ln, b_ln, w_out, tm=512):
    tokens = dmix.shape[0]

    def body(dm_ref, uc_ref, gl_ref, bl_ref, w_ref, duc_ref, do_ref, dgb_ref):
        @pl.when(pl.program_id(0) == 0)
        def _():
            dgb_ref[...] = jnp.zeros_like(dgb_ref)

        dmv = dm_ref[...]
        dyc = lax.dot_general(dmv, w_ref[pl.ds(0, D_CONV), :], NT, preferred_element_type=F32)
        do_ref[...] = lax.dot_general(dmv, w_ref[pl.ds(D_CONV, D_ATT), :], NT, preferred_element_type=F32)
        uc = uc_ref[...]
        mu = jnp.mean(uc, axis=-1, keepdims=True)
        cen = uc - mu
        rs = lax.rsqrt(jnp.mean(cen * cen, axis=-1, keepdims=True) + EPS)
        xh = cen * rs
        z = xh * gl_ref[...] + bl_ref[...]
        sg = _sig(z)
        dz = dyc * (sg * (1.0 + z * (1.0 - sg)))
        dgb_ref[0:1, :] += jnp.sum(dz * xh, axis=0, keepdims=True)
        dgb_ref[1:2, :] += jnp.sum(dz, axis=0, keepdims=True)
        dxh = dz * gl_ref[...]
        duc_ref[...] = rs * (dxh - jnp.mean(dxh, axis=-1, keepdims=True)
                             - xh * jnp.mean(dxh * xh, axis=-1, keepdims=True))

    return _call(
        body, name="mix_out_bwd", grid=(tokens // tm,),
        in_specs=[pl.BlockSpec((tm, D_MODEL), lambda i: (i, 0)),
                  pl.BlockSpec((tm, D_CONV), lambda i: (i, 0)),
                  pl.BlockSpec((1, D_CONV), lambda i: (0, 0)),
                  pl.BlockSpec((1, D_CONV), lambda i: (0, 0)),
                  pl.BlockSpec((D_MODEL, D_MODEL), lambda i: (0, 0))],
        out_specs=[pl.BlockSpec((tm, D_CONV), lambda i: (i, 0)),
                   pl.BlockSpec((tm, D_ATT), lambda i: (i, 0)),
                   pl.BlockSpec((8, D_CONV), lambda i: (0, 0))],
        out_shape=[_sds((tokens, D_CONV), F32), _sds((tokens, D_ATT), F32), _sds((8, D_CONV), F32)],
        compiler_params=_params(("arbitrary",)),
    )(dmix, uc2, g_ln, b_ln, w_out)


FF_TILE = 256
FF_ROWS = 256


def _ffn_fwd(h2, w_gate, w_up, w_down, tm=1024):
    tokens = h2.shape[0]

    def body(h_ref, wg_ref, wu_ref, wd_ref, gate_ref, up_ref, f_ref):
        @pl.when(pl.program_id(1) == 0)
        def _():
            f_ref[...] = jnp.zeros_like(f_ref)

        def gate_up(r):
            hv = h_ref[pl.ds(r * FF_ROWS, FF_ROWS), :]
            return (jnp.dot(hv, wg_ref[...], preferred_element_type=F32),
                    jnp.dot(hv, wu_ref[...], preferred_element_type=F32))

        ahead = gate_up(0)
        for r in range(tm // FF_ROWS):
            gate, up = ahead
            if r + 1 < tm // FF_ROWS:
                ahead = gate_up(r + 1)
            rows = pl.ds(r * FF_ROWS, FF_ROWS)
            gate_ref[rows, :] = gate
            up_ref[rows, :] = up
            act = (gate * _sig(gate) * up).astype(BF16)
            f_ref[rows, :] += jnp.dot(act, wd_ref[...], preferred_element_type=F32)

    return _call(
        body, name="ffn_fwd", grid=(tokens // tm, D_FF // FF_TILE),
        in_specs=[pl.BlockSpec((tm, D_MODEL), lambda i, j: (i, 0)),
                  pl.BlockSpec((D_MODEL, FF_TILE), lambda i, j: (0, j)),
                  pl.BlockSpec((D_MODEL, FF_TILE), lambda i, j: (0, j)),
                  pl.BlockSpec((FF_TILE, D_MODEL), lambda i, j: (j, 0))],
        out_specs=[pl.BlockSpec((tm, FF_TILE), lambda i, j: (i, j)),
                   pl.BlockSpec((tm, FF_TILE), lambda i, j: (i, j)),
                   pl.BlockSpec((tm, D_MODEL), lambda i, j: (i, 0))],
        out_shape=[_sds((tokens, D_FF), F32), _sds((tokens, D_FF), F32), _sds((tokens, D_MODEL), F32)],
        compiler_params=_params(("parallel", "arbitrary")),
    )(h2, w_gate, w_up, w_down)


def _ffn_bwd(df, gate, up, w_gate, w_up, w_down, tm=1024):
    tokens = df.shape[0]

    def body(df_ref, gate_ref, up_ref, wg_ref, wu_ref, wd_ref, dgate_ref, dup_ref, act_ref, dh_ref):
        @pl.when(pl.program_id(1) == 0)
        def _():
            dh_ref[...] = jnp.zeros_like(dh_ref)

        def d_act(r):
            return lax.dot_general(df_ref[pl.ds(r * FF_ROWS, FF_ROWS), :], wd_ref[...], NT,
                                   preferred_element_type=F32)

        ahead = d_act(0)
        for r in range(tm // FF_ROWS):
            dact = ahead
            if r + 1 < tm // FF_ROWS:
                ahead = d_act(r + 1)
            rows = pl.ds(r * FF_ROWS, FF_ROWS)
            gate = gate_ref[rows, :]
            up = up_ref[rows, :]
            sg = _sig(gate)
            silu = gate * sg
            act_ref[rows, :] = (silu * up).astype(BF16)
            dup = (dact * silu).astype(BF16)
            dgate = (dact * up * (sg * (1.0 + gate * (1.0 - sg)))).astype(BF16)
            dup_ref[rows, :] = dup
            dgate_ref[rows, :] = dgate
            dh_ref[rows, :] += (lax.dot_general(dgate, wg_ref[...], NT, preferred_element_type=F32)
                                + lax.dot_general(dup, wu_ref[...], NT, preferred_element_type=F32))

    tile = pl.BlockSpec((tm, FF_TILE), lambda i, j: (i, j))
    return _call(
        body, name="ffn_bwd", grid=(tokens // tm, D_FF // FF_TILE),
        in_specs=[pl.BlockSpec((tm, D_MODEL), lambda i, j: (i, 0)), tile, tile,
                  pl.BlockSpec((D_MODEL, FF_TILE), lambda i, j: (0, j)),
                  pl.BlockSpec((D_MODEL, FF_TILE), lambda i, j: (0, j)),
                  pl.BlockSpec((FF_TILE, D_MODEL), lambda i, j: (j, 0))],
        out_specs=[tile, tile, tile, pl.BlockSpec((tm, D_MODEL), lambda i, j: (i, 0))],
        out_shape=[_sds((tokens, D_FF), BF16)] * 3 + [_sds((tokens, D_MODEL), F32)],
        compiler_params=_params(("parallel", "arbitrary")),
    )(df, gate, up, w_gate, w_up, w_down)


def _loss_head(f, x1, target, mod8, seq, tm=512):
    tokens = f.shape[0]
    per_seq = seq // tm
    n_seq = tokens // seq

    def body(f_ref, x1_ref, t_ref, m_ref, dy_ref, df_ref, sq_ref, dgf_ref):
        i = pl.program_id(0)

        @pl.when(i == 0)
        def _():
            sq_ref[...] = jnp.zeros_like(sq_ref)

        @pl.when(i % per_seq == 0)
        def _():
            dgf_ref[...] = jnp.zeros_like(dgf_ref)

        fv = f_ref[...]
        gate = m_ref[5:6, :]
        diff = x1_ref[...] + gate * fv - t_ref[...]
        sq_ref[0:1, :] += jnp.sum(diff * diff, axis=0, keepdims=True)
        dy = diff * (1.0 / D_MODEL)
        dy_ref[...] = dy
        df_ref[...] = (gate * dy).astype(BF16)
        dgf_ref[0:1, :] += jnp.sum(dy * fv, axis=0, keepdims=True)

    rows = pl.BlockSpec((tm, D_MODEL), lambda i: (i, 0))
    per = pl.BlockSpec((None, 8, D_MODEL), lambda i: (i // per_seq, 0, 0))
    return _call(
        body, name="loss_head", grid=(tokens // tm,),
        in_specs=[rows, rows, rows, per],
        out_specs=[rows, rows, pl.BlockSpec((8, D_MODEL), lambda i: (0, 0)), per],
        out_shape=[_sds((tokens, D_MODEL), F32), _sds((tokens, D_MODEL), BF16),
                   _sds((8, D_MODEL), F32), _sds((n_seq, 8, D_MODEL), F32)],
        compiler_params=_params(("arbitrary",)),
    )(f, x1, target, mod8)


def _norm2_bwd(dh2, x1, dy, mix, mod8, g_ffn, seq, tm=512):
    tokens = dh2.shape[0]
    per_seq = seq // tm
    n_seq = tokens // seq

    def body(dh_ref, x1_ref, dy_ref, mix_ref, m_ref, g_ref, dx1_ref, dmix_ref, dg_ref, dm_ref):
        i = pl.program_id(0)

        @pl.when(i == 0)
        def _():
            dg_ref[...] = jnp.zeros_like(dg_ref)

        @pl.when(i % per_seq == 0)
        def _():
            dm_ref[...] = jnp.zeros_like(dm_ref)

        dh = dh_ref[...]
        x1 = x1_ref[...]
        r = lax.rsqrt(jnp.mean(x1 * x1, axis=-1, keepdims=True) + EPS)
        xhat = x1 * r
        g = g_ref[...]
        dm_ref[0:1, :] += jnp.sum(dh, axis=0, keepdims=True)
        dm_ref[1:2, :] += jnp.sum(dh * (xhat * g), axis=0, keepdims=True)
        dn = dh * (1.0 + m_ref[4:5, :])
        dg_ref[0:1, :] += jnp.sum(dn * xhat, axis=0, keepdims=True)
        dxh = dn * g
        dx1 = dy_ref[...] + r * (dxh - xhat * jnp.mean(dxh * xhat, axis=-1, keepdims=True))
        dx1_ref[...] = dx1
        dm_ref[2:3, :] += jnp.sum(dx1 * mix_ref[...], axis=0, keepdims=True)
        dmix_ref[...] = (m_ref[2:3, :] * dx1).astype(BF16)

    rows = pl.BlockSpec((tm, D_MODEL), lambda i: (i, 0))
    per = pl.BlockSpec((None, 8, D_MODEL), lambda i: (i // per_seq, 0, 0))
    return _call(
        body, name="norm2_bwd", grid=(tokens // tm,),
        in_specs=[rows, rows, rows, rows, per, pl.BlockSpec((1, D_MODEL), lambda i: (0, 0))],
        out_specs=[rows, rows, pl.BlockSpec((8, D_MODEL), lambda i: (0, 0)), per],
        out_shape=[_sds((tokens, D_MODEL), F32), _sds((tokens, D_MODEL), BF16),
                   _sds((8, D_MODEL), F32), _sds((n_seq, 8, D_MODEL), F32)],
        compiler_params=_params(("arbitrary",)),
    )(dh2, x1, dy, mix, mod8, g_ffn)


def _mix_in_bwd(d_a, d_g, d_q, d_k, d_v, w_in, x2, dx1, mod8, g_mix, seq, ride, tm=512):
    tokens = x2.shape[0]
    per_seq = seq // tm
    n_seq = tokens // seq
    parts = (d_a, d_g, d_q, d_k, d_v)
    width = D_CONV
    n_ride = len(ride)
    ride_scatter = [s for _, s in ride]

    def body(*refs):
        da_ref, dg_ref, dq_ref, dk_ref, dv_ref, w_ref, x_ref, dx1_ref, m_ref, g_ref = refs[:10]
        ride_in = refs[10:10 + n_ride]
        gx_ref, dgm_ref, dm_ref = refs[10 + n_ride:13 + n_ride]
        ride_args = (ride_scatter, ride_in, refs[13 + n_ride:13 + 2 * n_ride]) + tuple(refs[13 + 2 * n_ride:])
        i = pl.program_id(0)

        @pl.when(i == 0)
        def _():
            _exchange_start(*ride_args)
            dgm_ref[...] = jnp.zeros_like(dgm_ref)

        @pl.when(i % per_seq == 0)
        def _():
            dm_ref[...] = jnp.zeros_like(dm_ref)

        dh = jnp.zeros((tm, D_MODEL), F32)
        for n, ref in enumerate((da_ref, dg_ref, dq_ref, dk_ref, dv_ref)):
            dh = dh + lax.dot_general(ref[...], w_ref[:, pl.ds(n * width, width)], NT, preferred_element_type=F32)
        xv = x_ref[...]
        r = lax.rsqrt(jnp.mean(xv * xv, axis=-1, keepdims=True) + EPS)
        xhat = xv * r
        g = g_ref[...]
        dm_ref[0:1, :] += jnp.sum(dh, axis=0, keepdims=True)
        dm_ref[1:2, :] += jnp.sum(dh * (xhat * g), axis=0, keepdims=True)
        dn = dh * (1.0 + m_ref[1:2, :])
        dgm_ref[0:1, :] += jnp.sum(dn * xhat, axis=0, keepdims=True)
        dxh = dn * g
        gx_ref[...] = dx1_ref[...] + r * (dxh - xhat * jnp.mean(dxh * xhat, axis=-1, keepdims=True))

        @pl.when(i == tokens // tm - 1)
        def _():
            _exchange_wait(*ride_args)

    rows = pl.BlockSpec((tm, D_MODEL), lambda i: (i, 0))
    half = pl.BlockSpec((tm, width), lambda i: (i, 0))
    per = pl.BlockSpec((None, 8, D_MODEL), lambda i: (i // per_seq, 0, 0))
    return _call(
        body, name="mix_in_bwd", grid=(tokens // tm,),
        in_specs=[half] * 5 + [pl.BlockSpec((D_MODEL, D_IN), lambda i: (0, 0)), rows, rows, per,
                               pl.BlockSpec((1, D_MODEL), lambda i: (0, 0))] + [ANY_SPEC] * n_ride,
        out_specs=[rows, pl.BlockSpec((8, D_MODEL), lambda i: (0, 0)), per] + [ANY_SPEC] * n_ride,
        out_shape=[_sds((tokens, D_MODEL), F32), _sds((8, D_MODEL), F32), _sds((n_seq, 8, D_MODEL), F32)]
        + _exchange_shapes(ride),
        scratch_shapes=_exchange_sems(n_ride),
        compiler_params=_params(("arbitrary",)),
    )(*parts, w_in, x2, dx1, mod8, g_mix, *[a for a, _ in ride])


def _grad_matmul_parts(a_parts, b_parts, name, tk=1024):
    tokens = a_parts[0].shape[0]
    na, nb = len(a_parts), len(b_parts)
    ma, nbw = a_parts[0].shape[1], b_parts[0].shape[1]

    def body(*refs):
        a_refs, b_refs, o_ref = refs[:na], refs[na:na + nb], refs[na + nb]

        @pl.when(pl.program_id(0) == 0)
        def _():
            o_ref[...] = jnp.zeros_like(o_ref)

        for i in range(na):
            for j in range(nb):
                o_ref[pl.ds(i * ma, ma), pl.ds(j * nbw, nbw)] += lax.dot_general(
                    a_refs[i][...], b_refs[j][...], TN, preferred_element_type=F32)

    return _call(
        body, name=name, grid=(tokens // tk,),
        in_specs=[pl.BlockSpec((tk, ma), lambda k: (k, 0))] * na + [pl.BlockSpec((tk, nbw), lambda k: (k, 0))] * nb,
        out_specs=pl.BlockSpec((na * ma, nb * nbw), lambda k: (0, 0)),
        out_shape=_sds((na * ma, nb * nbw), F32),
        compiler_params=_params(("arbitrary",)),
    )(*a_parts, *b_parts)


def _grad_matmul(a, b, name, tmo, tno, tk=1024):
    tokens, m = a.shape
    n = b.shape[1]

    def body(a_ref, b_ref, o_ref):
        @pl.when(pl.program_id(2) == 0)
        def _():
            o_ref[...] = jnp.zeros_like(o_ref)

        o_ref[...] += lax.dot_general(a_ref[...], b_ref[...], TN, preferred_element_type=F32)

    return _call(
        body, name=name, grid=(m // tmo, n // tno, tokens // tk),
        in_specs=[pl.BlockSpec((tk, tmo), lambda i, j, k: (k, i)),
                  pl.BlockSpec((tk, tno), lambda i, j, k: (k, j))],
        out_specs=pl.BlockSpec((tmo, tno), lambda i, j, k: (i, j)),
        out_shape=_sds((m, n), F32),
        compiler_params=_params(("parallel", "parallel", "arbitrary")),
    )(a, b)


def _adamw(w, m, v, g, name, n_parts=0, tr=256):
    rows, cols = w.shape
    tr = min(tr, rows)
    c1 = 1.0 - ADAM_B1 ** ADAM_STEP
    c2 = 1.0 - ADAM_B2 ** ADAM_STEP

    def body(w_ref, m_ref, v_ref, g_ref, go_ref, d_ref, mo_ref, vo_ref):
        if n_parts:
            gv = g_ref[0].astype(F32)
            for p in range(1, n_parts):
                gv = gv + g_ref[p].astype(F32)
        else:
            gv = g_ref[...]
        go_ref[...] = gv
        mn = ADAM_B1 * m_ref[...] + (1.0 - ADAM_B1) * gv
        vn = ADAM_B2 * v_ref[...] + (1.0 - ADAM_B2) * (gv * gv)
        mo_ref[...] = mn
        vo_ref[...] = vn
        d_ref[...] = -ADAM_LR * ((mn / c1) / (jnp.sqrt(vn / c2) + ADAM_EPS) + ADAM_WD * w_ref[...])

    blk = pl.BlockSpec((tr, cols), lambda i: (i, 0))
    g_spec = pl.BlockSpec((n_parts, tr, cols), lambda i: (0, i, 0)) if n_parts else blk
    return _call(
        body, name=name, grid=(rows // tr,),
        in_specs=[blk, blk, blk, g_spec], out_specs=[blk] * 4,
        out_shape=[_sds((rows, cols), F32)] * 4,
        compiler_params=_params(("parallel",)),
    )(w, m, v, g)


def _cols_to_full(blocks):
    n, r, c = blocks.shape
    return jnp.transpose(blocks, (1, 0, 2)).reshape(r, n * c)


def _full_to_cols(full, n=N_DEV):
    r, c = full.shape
    return jnp.transpose(full.reshape(r, n, c // n), (1, 0, 2))


def _pad_lanes(v, width):
    return jnp.pad(v, ((0, 0), (0, width - v.shape[1])))


def kernel(x, c, w_ada, b_ada, g_mix, w_in, w_dw, b_dw, g_conv_ln, b_conv_ln, g_q, g_k, w_out, g_ffn, w_gate, w_up, w_down, loss_target, m_w_ada, m_b_ada, m_g_mix, m_w_in, m_w_dw, m_b_dw, m_g_conv_ln, m_b_conv_ln, m_g_q, m_g_k, m_w_out, m_g_ffn, m_w_gate, m_w_up, m_w_down, v_w_ada, v_b_ada, v_g_mix, v_w_in, v_w_dw, v_b_dw, v_g_conv_ln, v_b_conv_ln, v_g_q, v_g_k, v_w_out, v_g_ffn, v_w_gate, v_w_up, v_w_down):
    n_seq, seq, _ = x.shape
    tokens = n_seq * seq
    me = 4 * lax.axis_index("x") + 2 * lax.axis_index("y") + lax.axis_index("c")
    ada_cols = w_ada.shape[2]
    dw_cols = w_dw.shape[2]

    (c_g, w_in_g, w_dw_g) = _exchange(
        [(c, False), (w_in[0].astype(BF16), False), (w_dw[0], False)], "gather_weights")
    c_all = c_g.reshape(N_DEV * n_seq, D_MODEL)
    w_in_f = _cols_to_full(w_in_g)
    w_dw_f = _cols_to_full(w_dw_g)

    b_cols = lax.dynamic_slice(b_ada, (0, me * ada_cols), (1, ada_cols))
    mod_cols = _ada_fwd(c_all, w_ada[0], b_cols)
    (mod_g,) = _exchange([(mod_cols, False)], "gather_mod")
    mod_mine = lax.dynamic_slice(mod_g, (0, me * n_seq, 0), (N_DEV, n_seq, ada_cols))
    mod = jnp.transpose(mod_mine, (1, 0, 2)).reshape(n_seq, N_MOD, D_MODEL)
    mod8 = jnp.pad(mod, ((0, 0), (0, 8 - N_MOD), (0, 0)))

    x2 = x.reshape(tokens, D_MODEL)
    h1, proj = _mix_in(x2, mod8, g_mix, w_in_f, seq)
    proj3 = proj.reshape(n_seq, seq, D_IN)
    uc3 = _conv_fwd(proj3, w_dw_f, b_dw)
    g_q2, g_k2 = jnp.tile(g_q, (1, 2)), jnp.tile(g_k, (1, 2))
    y_att3, lse3, w_out_g, w_gate_g, w_up_g, w_down_g = _attn_fwd(
        proj3, g_q2, g_k2,
        [(w_out[0].astype(BF16), False), (w_gate[0].astype(BF16), False), (w_up[0].astype(BF16), False),
         (w_down[0].astype(BF16), False)])
    w_out_f = w_out_g.reshape(D_MODEL, D_MODEL)
    w_gate_f = _cols_to_full(w_gate_g)
    w_up_f = _cols_to_full(w_up_g)
    w_down_f = w_down_g.reshape(D_FF, D_MODEL)
    uc2 = uc3.reshape(tokens, D_CONV)
    y_att2 = y_att3.reshape(tokens, D_ATT)
    y_conv, mix, x1, h2 = _mix_out(uc2, y_att2, x2, mod8, g_conv_ln, b_conv_ln, g_ffn, w_out_f, seq)
    gate, up, f = _ffn_fwd(h2, w_gate_f, w_up_f, w_down_f)
    dy, df, sq, dgate_f = _loss_head(f, x1, loss_target.reshape(tokens, D_MODEL), mod8, seq)

    dgate, dup, act, dh2 = _ffn_bwd(df, gate, up, w_gate_f, w_up_f, w_down_f)
    dx1, dmix, dg_ffn, dmod_f = _norm2_bwd(dh2, x1, dy, mix, mod8, g_ffn, seq)
    duc2, do2, dgb_ln = _mix_out_bwd(dmix, uc2, g_conv_ln, b_conv_ln, w_out_f)
    d_a3, d_g3, dw_dw_p, db_dw_p = _conv_bwd(duc2.reshape(n_seq, seq, D_CONV), proj3, w_dw_f)
    gw_gate = _grad_matmul(h2, dgate, "grad_w_gate", D_MODEL, D_FF // 2)
    gw_up = _grad_matmul(h2, dup, "grad_w_up", D_MODEL, D_FF // 2)
    gw_down = _grad_matmul(act, df, "grad_w_down", D_FF // 2, D_MODEL)
    gw_out = _grad_matmul_parts([y_conv, y_att2], [dmix], "grad_w_out")
    d_q3, d_k3, d_v3, dg_qk, p_gate, p_up, p_down, p_out = _attn_bwd(
        proj3, do2.reshape(n_seq, seq, D_ATT), y_att3, lse3, g_q2, g_k2,
        [(_full_to_cols(gw_gate).astype(BF16), True), (_full_to_cols(gw_up).astype(BF16), True),
         (gw_down.astype(BF16).reshape(N_DEV, D_FF // N_DEV, D_MODEL), True),
         (gw_out.astype(BF16).reshape(N_DEV, D_MODEL // N_DEV, D_MODEL), True)])
    flat = lambda t: t.reshape(tokens, t.shape[-1])
    d_a, d_g, d_q, d_k, d_v = flat(d_a3), flat(d_g3), flat(d_q3), flat(d_k3), flat(d_v3)
    gw_in = _grad_matmul_parts([h1], [d_a, d_g, d_q, d_k, d_v], "grad_w_in")
    grad_x2, dg_mix, dmod_m, p_in = _mix_in_bwd(
        d_a, d_g, d_q, d_k, d_v, w_in_f, x2, dx1, mod8, g_mix, seq, [(_full_to_cols(gw_in).astype(BF16), True)])

    dmod = jnp.concatenate([dmod_m[:, 0], dmod_m[:, 1], dmod_f[:, 2], dmod_f[:, 0], dmod_f[:, 1], dgate_f[:, 0]], axis=1)
    dg_q = dg_qk[0:1, 0:HEAD_DIM] + dg_qk[0:1, HEAD_DIM:]
    dg_k = dg_qk[1:2, 0:HEAD_DIM] + dg_qk[1:2, HEAD_DIM:]
    loss_part = (0.5 / D_MODEL) * jnp.sum(sq[0:1, :], axis=1, keepdims=True)
    small = jnp.concatenate(
        [dg_mix[0:1], dg_ffn[0:1], db_dw_p[0:1], dgb_ln[0:1], dgb_ln[1:2],
         _pad_lanes(dg_q, LANES), _pad_lanes(dg_k, LANES), _pad_lanes(loss_part, LANES)], axis=1)
    n_small = small.shape[1] - LANES

    (dmod_g, small_g, dw_g) = _exchange([(dmod, False), (small, False), (dw_dw_p, False)], "gather_small_grads")

    dmod_all = dmod_g.reshape(N_DEV * n_seq, N_MOD * D_MODEL)
    dmod_cols = lax.dynamic_slice(dmod_all, (0, me * ada_cols), (N_DEV * n_seq, ada_cols))
    gw_ada, gb_ada = _ada_bwd(c_all, dmod_cols, dmod_all)

    res = {}
    res["w_ada"] = _adamw(w_ada[0], m_w_ada[0], v_w_ada[0], gw_ada, "adamw_w_ada")
    res["b_ada"] = _adamw(b_ada, m_b_ada, v_b_ada, gb_ada, "adamw_b_ada")
    res["w_in"] = _adamw(w_in[0], m_w_in[0], v_w_in[0], p_in, "adamw_w_in", N_DEV)
    res["w_out"] = _adamw(w_out[0], m_w_out[0], v_w_out[0], p_out, "adamw_w_out", N_DEV)
    res["w_gate"] = _adamw(w_gate[0], m_w_gate[0], v_w_gate[0], p_gate, "adamw_w_gate", N_DEV)
    res["w_up"] = _adamw(w_up[0], m_w_up[0], v_w_up[0], p_up, "adamw_w_up", N_DEV)
    res["w_down"] = _adamw(w_down[0], m_w_down[0], v_w_down[0], p_down, "adamw_w_down", N_DEV, tr=176)
    dw_mine = lax.dynamic_slice(dw_g, (0, 0, me * dw_cols), (N_DEV, CONV_WIDTH, dw_cols))
    res["w_dw"] = _adamw(w_dw[0], m_w_dw[0], v_w_dw[0], dw_mine, "adamw_w_dw", N_DEV)

    small_names = ["g_mix", "g_ffn", "b_dw", "g_conv_ln", "b_conv_ln", "g_q", "g_k"]
    small_w = {"g_mix": (g_mix, m_g_mix, v_g_mix), "g_ffn": (g_ffn, m_g_ffn, v_g_ffn), "b_dw": (b_dw, m_b_dw, v_b_dw),
               "g_conv_ln": (g_conv_ln, m_g_conv_ln, v_g_conv_ln), "b_conv_ln": (b_conv_ln, m_b_conv_ln, v_b_conv_ln),
               "g_q": (g_q, m_g_q, v_g_q), "g_k": (g_k, m_g_k, v_g_k)}
    widths = [max(small_w[n][0].shape[1], LANES) for n in small_names]
    packed = [jnp.concatenate([_pad_lanes(small_w[n][i], wd) for n, wd in zip(small_names, widths)], axis=1) for i in range(3)]
    outs = _adamw(packed[0], packed[1], packed[2], small_g[:, :, :n_small], "adamw_small", N_DEV)
    off = 0
    for n, wd in zip(small_names, widths):
        real = small_w[n][0].shape[1]
        res[n] = tuple(o[:, off:off + real] for o in outs)
        off += wd
    loss = jnp.sum(small_g[:, 0, n_small])

    order = ["w_ada", "b_ada", "g_mix", "w_in", "w_dw", "b_dw", "g_conv_ln", "b_conv_ln", "g_q", "g_k",
             "w_out", "g_ffn", "w_gate", "w_up", "w_down"]
    lead = {"w_ada", "w_in", "w_dw", "w_out", "w_gate", "w_up", "w_down"}
    grads, deltas, new_m, new_v = [], [], [], []
    for n in order:
        g, d, mn, vn = res[n]
        g, d, mn, vn = (t[None] if n in lead else t for t in (g, d, mn, vn))
        grads.append(g)
        deltas.append(d)
        new_m.append(mn)
        new_v.append(vn)
    return (loss, grad_x2.reshape(n_seq, seq, D_MODEL), *grads, *deltas, *new_m, *new_v)
```

```python
import numpy as np
import jax
import jax.numpy as jnp
from jax import lax
from jax.experimental import pallas as pl
from jax.experimental.pallas import tpu as pltpu

F32 = jnp.float32
BF16 = jnp.bfloat16

N_DEV = 8
D_MODEL = 1024
D_CONV = 512
D_ATT = 512
HEAD_DIM = 64
CONV_WIDTH = 31
D_IN = 2 * D_CONV + 3 * D_ATT
D_FF = 2816
N_MOD = 6
EPS = 1e-6
RADIUS = 64
DILATIONS = (1, 4, 16)
Q_BLOCK = 128
LANES = 128
VMEM_LIMIT = 56 * 1024 * 1024

ADAM_LR = 0.001
ADAM_B1 = 0.9
ADAM_B2 = 0.999
ADAM_EPS = 1e-08
ADAM_WD = 0.01
ADAM_STEP = 10

NT = (((1,), (1,)), ((), ()))
TN = (((0,), (0,)), ((), ()))


def _call(body, **kw):
    return pl.pallas_call(body, **kw)


def _params(sem=None, vmem=VMEM_LIMIT):
    return pltpu.CompilerParams(dimension_semantics=sem, vmem_limit_bytes=vmem)


def _sig(x):
    return 1.0 / (1.0 + jnp.exp(-x))


def _sds(shape, dtype):
    return jax.ShapeDtypeStruct(shape, dtype)


N_PEER = N_DEV - 1
ANY_SPEC = pl.BlockSpec(memory_space=pl.ANY)


def _exchange_copies(scatter, ins, outs, *sems):
    n = len(ins)
    if n == 0:
        return [], []
    send_sems, recv_sems, local_sems = sems
    x, y, c = lax.axis_index("x"), lax.axis_index("y"), lax.axis_index("c")
    me = 4 * x + 2 * y + c

    def src(a, slot):
        return ins[a].at[slot] if scatter[a] else ins[a]

    local = [pltpu.make_async_copy(src(a, me), outs[a].at[me], local_sems.at[a]) for a in range(n)]
    flights = []
    for k in range(1, N_DEV):
        px = 1 - x if k & 4 else x
        py = 1 - y if k & 2 else y
        pc = 1 - c if k & 1 else c
        pid = 4 * px + 2 * py + pc
        for a in range(n):
            i = a * N_PEER + k - 1
            send, recv = (pltpu.make_async_remote_copy(
                src_ref=src(a, pid), dst_ref=outs[a].at[slot],
                send_sem=send_sems.at[i], recv_sem=recv_sems.at[i],
                device_id=(px, py, pc), device_id_type=pl.DeviceIdType.MESH) for slot in (me, pid))
            flights.append((send, recv))
    return local, flights


def _exchange_start(*args):
    local, flights = _exchange_copies(*args)
    for cp in local:
        cp.start()
    for send, _ in flights:
        send.start()


def _exchange_wait(*args):
    local, flights = _exchange_copies(*args)
    for send, recv in flights:
        send.wait_send()
        recv.wait_recv()
    for cp in local:
        cp.wait()


def _exchange_shapes(items):
    return [_sds((N_DEV,) + tuple(arr.shape[1:] if scatter else arr.shape), arr.dtype) for arr, scatter in items]


def _exchange_sems(n):
    if n == 0:
        return []
    return [pltpu.SemaphoreType.DMA((n * N_PEER,)), pltpu.SemaphoreType.DMA((n * N_PEER,)),
            pltpu.SemaphoreType.DMA((n,))]


def _exchange(items, name):
    n = len(items)
    scatter = [s for _, s in items]

    def body(*refs):
        args = (scatter, refs[:n], refs[n:2 * n]) + tuple(refs[2 * n:])
        _exchange_start(*args)
        _exchange_wait(*args)

    return _call(
        body, name=name, out_shape=_exchange_shapes(items),
        in_specs=[ANY_SPEC] * n, out_specs=[ANY_SPEC] * n, scratch_shapes=_exchange_sems(n),
    )(*[a for a, _ in items])


def _ada_fwd(c_all, w_ada, b_cols):
    def body(c_ref, w_ref, b_ref, o_ref):
        cv = c_ref[...]
        sc = (cv * _sig(cv)).astype(BF16)
        o_ref[...] = jnp.dot(sc, w_ref[...].astype(BF16), preferred_element_type=F32) + b_ref[...]

    return _call(body, name="ada_fwd", out_shape=_sds((c_all.shape[0], w_ada.shape[1]), F32),
                 compiler_params=_params())(c_all, w_ada, b_cols)


def _ada_bwd(c_all, dmod_cols, dmod_all):
    def body(c_ref, dc_ref, da_ref, gw_ref, gb_ref):
        cv = c_ref[...]
        sc = (cv * _sig(cv)).astype(BF16)
        gw_ref[...] = lax.dot_general(sc, dc_ref[...].astype(BF16), TN, preferred_element_type=F32)
        gb_ref[...] = jnp.sum(da_ref[...], axis=0, keepdims=True)

    return _call(body, name="ada_bwd",
                 out_shape=[_sds((c_all.shape[1], dmod_cols.shape[1]), F32), _sds((1, dmod_all.shape[1]), F32)],
                 compiler_params=_params())(c_all, dmod_cols, dmod_all)


def _mix_in(x2, mod8, g_mix, w_in, seq, tm=512):
    tokens = x2.shape[0]
    per_seq = seq // tm

    def body(x_ref, m_ref, g_ref, w_ref, h_ref, p_ref):
        xv = x_ref[...]
        r = lax.rsqrt(jnp.mean(xv * xv, axis=-1, keepdims=True) + EPS)
        h = (xv * r * g_ref[...]) * (1.0 + m_ref[1:2, :]) + m_ref[0:1, :]
        hb = h.astype(BF16)
        h_ref[...] = hb
        p_ref[...] = jnp.dot(hb, w_ref[...], preferred_element_type=F32)

    return _call(
        body, name="mix_in", grid=(tokens // tm,),
        in_specs=[pl.BlockSpec((tm, D_MODEL), lambda i: (i, 0)),
                  pl.BlockSpec((None, 8, D_MODEL), lambda i: (i // per_seq, 0, 0)),
                  pl.BlockSpec((1, D_MODEL), lambda i: (0, 0)),
                  pl.BlockSpec((D_MODEL, D_IN), lambda i: (0, 0))],
        out_specs=[pl.BlockSpec((tm, D_MODEL), lambda i: (i, 0)),
                   pl.BlockSpec((tm, D_IN), lambda i: (i, 0))],
        out_shape=[_sds((tokens, D_MODEL), BF16), _sds((tokens, D_IN), F32)],
        compiler_params=_params(("parallel",)),
    )(x2, mod8, g_mix, w_in)


CONV_ROWS = 32
CONV_HALO = 16


def _fill_shifted(xp, sh, seq):
    for b in range(8):
        sh[b, pl.ds(0, seq + 24), :] = xp[pl.ds(b, seq + 24), :]


def _conv_fwd(proj3, w_dw, b_dw):
    n_seq, seq, _ = proj3.shape
    n_cb = D_CONV // LANES

    def body(a_ref, g_ref, w_ref, b_ref, uc_ref, xp, sh):
        zeros = jnp.zeros((CONV_HALO, LANES), F32)
        xp[pl.ds(0, CONV_HALO), :] = zeros
        xp[pl.ds(CONV_HALO + seq, CONV_HALO), :] = zeros
        xp[pl.ds(CONV_HALO, seq), :] = a_ref[...] * _sig(g_ref[...])
        _fill_shifted(xp, sh, seq)

        def blk(i, carry):
            t0 = pl.multiple_of(i * CONV_ROWS, CONV_ROWS)
            acc = jnp.zeros((CONV_ROWS, LANES), F32)
            for j in range(CONV_WIDTH):
                jj = j + 1
                acc = acc + sh[jj % 8, pl.ds(t0 + 8 * (jj // 8), CONV_ROWS), :] * w_ref[j:j + 1, :]
            uc_ref[pl.ds(t0, CONV_ROWS), :] = acc + b_ref[...]
            return carry

        lax.fori_loop(0, seq // CONV_ROWS, blk, 0)

    return _call(
        body, name="conv_fwd", grid=(n_seq, n_cb),
        in_specs=[pl.BlockSpec((None, seq, LANES), lambda b, cb: (b, 0, cb)),
                  pl.BlockSpec((None, seq, LANES), lambda b, cb: (b, 0, n_cb + cb)),
                  pl.BlockSpec((CONV_WIDTH, LANES), lambda b, cb: (0, cb)),
                  pl.BlockSpec((1, LANES), lambda b, cb: (0, cb))],
        out_specs=pl.BlockSpec((None, seq, LANES), lambda b, cb: (b, 0, cb)),
        out_shape=_sds((n_seq, seq, D_CONV), F32),
        scratch_shapes=[pltpu.VMEM((seq + 2 * CONV_HALO, LANES), F32),
                        pltpu.VMEM((8, seq + 2 * CONV_HALO, LANES), F32)],
        compiler_params=_params(("parallel", "parallel")),
    )(proj3, proj3, w_dw, b_dw)


def _conv_bwd(duc3, proj3, w_dw):
    n_seq, seq, _ = proj3.shape
    n_cb = D_CONV // LANES

    def body(duc_ref, a_ref, g_ref, w_ref, da_ref, dg_ref, dw_ref, db_ref, xp, sh):
        @pl.when(pl.program_id(1) == 0)
        def _():
            dw_ref[...] = jnp.zeros_like(dw_ref)
            db_ref[...] = jnp.zeros_like(db_ref)

        zeros = jnp.zeros((CONV_HALO, LANES), F32)
        xp[pl.ds(0, CONV_HALO), :] = zeros
        xp[pl.ds(CONV_HALO + seq, CONV_HALO), :] = zeros
        xp[pl.ds(CONV_HALO, seq), :] = a_ref[...] * _sig(g_ref[...])
        _fill_shifted(xp, sh, seq)
        for j0 in range(0, CONV_WIDTH, 8):
            taps = range(j0, min(j0 + 8, CONV_WIDTH))

            def wblk(i, accs, taps=taps):
                t0 = pl.multiple_of(i * CONV_ROWS, CONV_ROWS)
                d = duc_ref[pl.ds(t0, CONV_ROWS), :]
                return tuple(acc + d * sh[(j + 1) % 8, pl.ds(t0 + 8 * ((j + 1) // 8), CONV_ROWS), :]
                             for acc, j in zip(accs, taps))

            accs = lax.fori_loop(0, seq // CONV_ROWS, wblk,
                                 tuple(jnp.zeros((CONV_ROWS, LANES), F32) for _ in taps))
            for acc, j in zip(accs, taps):
                dw_ref[j:j + 1, :] += jnp.sum(acc, axis=0, keepdims=True)
        db_ref[0:1, :] += jnp.sum(duc_ref[...], axis=0, keepdims=True)
        xp[pl.ds(CONV_HALO, seq), :] = duc_ref[...]
        _fill_shifted(xp, sh, seq)

        def ublk(i, carry):
            t0 = pl.multiple_of(i * CONV_ROWS, CONV_ROWS)
            acc = jnp.zeros((CONV_ROWS, LANES), F32)
            for j in range(CONV_WIDTH):
                jj = CONV_WIDTH - j
                acc = acc + sh[jj % 8, pl.ds(t0 + 8 * (jj // 8), CONV_ROWS), :] * w_ref[j:j + 1, :]
            av = a_ref[pl.ds(t0, CONV_ROWS), :]
            sg = _sig(g_ref[pl.ds(t0, CONV_ROWS), :])
            da_ref[pl.ds(t0, CONV_ROWS), :] = (acc * sg).astype(BF16)
            dg_ref[pl.ds(t0, CONV_ROWS), :] = (acc * av * sg * (1.0 - sg)).astype(BF16)
            return carry

        lax.fori_loop(0, seq // CONV_ROWS, ublk, 0)

    return _call(
        body, name="conv_bwd", grid=(n_cb, n_seq),
        in_specs=[pl.BlockSpec((None, seq, LANES), lambda cb, b: (b, 0, cb)),
                  pl.BlockSpec((None, seq, LANES), lambda cb, b: (b, 0, cb)),
                  pl.BlockSpec((None, seq, LANES), lambda cb, b: (b, 0, n_cb + cb)),
                  pl.BlockSpec((CONV_WIDTH, LANES), lambda cb, b: (0, cb))],
        out_specs=[pl.BlockSpec((None, seq, LANES), lambda cb, b: (b, 0, cb)),
                   pl.BlockSpec((None, seq, LANES), lambda cb, b: (b, 0, cb)),
                   pl.BlockSpec((32, LANES), lambda cb, b: (0, cb)),
                   pl.BlockSpec((8, LANES), lambda cb, b: (0, cb))],
        out_shape=[_sds((n_seq, seq, D_CONV), BF16), _sds((n_seq, seq, D_CONV), BF16),
                   _sds((32, D_CONV), F32), _sds((8, D_CONV), F32)],
        scratch_shapes=[pltpu.VMEM((seq + 2 * CONV_HALO, LANES), F32),
                        pltpu.VMEM((8, seq + 2 * CONV_HALO, LANES), F32)],
        compiler_params=_params(("parallel", "arbitrary")),
    )(duc3, proj3, proj3, w_dw)


MASKED = 1e30
ATT_UNROLL = 4


def _distance_mats(dil, seg_len):
    kw = min(2 * Q_BLOCK, seg_len)
    offsets = (0, -RADIUS, -2 * RADIUS) if kw == 2 * Q_BLOCK else (0,)
    a = np.arange(Q_BLOCK)[:, None]
    b = np.arange(kw)[None, :]
    mats = []
    for off in offsets:
        rel = np.abs(b + off - a)
        mats.append(np.where(rel <= RADIUS, dil * rel, MASKED))
    return jnp.asarray(np.stack(mats).astype(np.float32))


def _alibi_rows():
    s = np.zeros((4, 8, LANES), np.float32)
    for hp in range(4):
        for hl in range(2):
            s[hp, hl, :] = 2.0 ** (-(2 * hp + hl + 1))
    return jnp.asarray(s)


def _window(n, seg_len):
    i0 = pl.multiple_of(n * Q_BLOCK, Q_BLOCK)
    if seg_len <= Q_BLOCK:
        return i0, i0, 0
    per_seg = seg_len // Q_BLOCK
    j = n % per_seg
    seg0 = (n // per_seg) * seg_len
    ks_local = jnp.clip(j * Q_BLOCK - RADIUS, 0, seg_len - 2 * Q_BLOCK)
    ks = pl.multiple_of(seg0 + ks_local, RADIUS)
    var = jnp.where(j == 0, 0, jnp.where(j == per_seg - 1, 2, 1))
    return i0, ks, var


def _first_head(rows):
    return lax.broadcasted_iota(jnp.int32, (rows, LANES), 1) < HEAD_DIM


def _head_mean(x, first):
    s0 = jnp.sum(jnp.where(first, x, 0.0), axis=1, keepdims=True)
    s1 = jnp.sum(jnp.where(first, 0.0, x), axis=1, keepdims=True)
    return jnp.where(first, s0, s1) * (1.0 / HEAD_DIM)


def _permute_rows(dst, src, dil, seq):
    seg = seq // dil
    for r in range(dil):
        rows = pl.ds(r, seg, stride=dil) if dil > 1 else pl.ds(0, seq)
        dst[pl.ds(r * seg, seg), :] = src[rows, :].astype(dst.dtype)


def _permute_rows_by_head(dst, src, dil, seq):
    seg = seq // dil
    first = _first_head(seg)
    for r in range(dil):
        rows = pl.ds(r, seg, stride=dil) if dil > 1 else pl.ds(0, seq)
        val = src[rows, :]
        dst[0, pl.ds(r * seg, seg), :] = jnp.where(first, val, 0.0).astype(dst.dtype)
        dst[1, pl.ds(r * seg, seg), :] = jnp.where(first, 0.0, val).astype(dst.dtype)


def _qk_normalise(q_ref, g2_ref, dst, seq, scale):
    def chunk(ci, carry):
        rows = pl.ds(pl.multiple_of(ci * 256, 256), 256)
        qv = q_ref[rows, :]
        r = lax.rsqrt(_head_mean(qv * qv, _first_head(256)) + EPS)
        dst[rows, :] = qv * r * (g2_ref[...] * scale)
        return carry

    lax.fori_loop(0, seq // 256, chunk, 0)


def _attn_fwd(proj3, g_q2, g_k2, ride):
    n_seq, seq, _ = proj3.shape
    dms = [_distance_mats(d, seq // d) for d in DILATIONS]
    col0 = 2 * D_CONV // LANES
    n_hp = D_ATT // LANES

    n_ride = len(ride)
    ride_scatter = [s for _, s in ride]

    def body(*refs):
        q_ref, k_ref, v_ref, gq_ref, gk_ref, sl_ref, dm1, dm4, dm16 = refs[:9]
        ride_in = refs[9:9 + n_ride]
        y_ref, lse_ref = refs[9 + n_ride:11 + n_ride]
        ride_out = refs[11 + n_ride:11 + 2 * n_ride]
        (qf, kf, qp, kp, vp, oml_p, o1, o4, o16, m1, m4, m16, l1, l4, l16) = refs[11 + 2 * n_ride:26 + 2 * n_ride]
        o_nat, m_nat, l_nat = (o1, o4, o16), (m1, m4, m16), (l1, l4, l16)
        ride_args = (ride_scatter, ride_in, ride_out) + tuple(refs[26 + 2 * n_ride:])
        step = pl.program_id(0) * n_hp + pl.program_id(1)

        @pl.when(step == 0)
        def _():
            _exchange_start(*ride_args)

        dm_refs = (dm1, dm4, dm16)
        _qk_normalise(q_ref, gq_ref, qf, seq, HEAD_DIM ** -0.5)
        _qk_normalise(k_ref, gk_ref, kf, seq, 1.0)
        slopes = (sl_ref[0:1, 0:1], sl_ref[1:2, 0:1])
        for pi, dil in enumerate(DILATIONS):
            seg = seq // dil
            kw = min(2 * Q_BLOCK, seg)
            _permute_rows_by_head(qp, qf, dil, seq)
            _permute_rows(kp, kf, dil, seq)
            _permute_rows(vp, v_ref, dil, seq)

            def blk(it, carry, seg=seg, kw=kw, pi=pi, dst=oml_p):
                first = _first_head(Q_BLOCK)
                chains = [(sub, h) for sub in range(ATT_UNROLL) for h in range(2)]
                win = [_window(it * ATT_UNROLL + sub, seg) for sub in range(ATT_UNROLL)]
                s = {}
                for sub, h in chains:
                    i0, ks, var = win[sub]
                    s[sub, h] = lax.dot_general(qp[h, pl.ds(i0, Q_BLOCK), :], kp[pl.ds(ks, kw), :], NT,
                                                preferred_element_type=F32) - slopes[h] * dm_refs[pi][var]
                m, l, p = {}, {}, {}
                for c in chains:
                    m[c] = jnp.max(s[c], axis=1, keepdims=True)
                    e = jnp.exp(s[c] - m[c])
                    l[c] = jnp.sum(e, axis=1, keepdims=True)
                    p[c] = e.astype(BF16)
                o = {}
                for sub, h in chains:
                    o[sub, h] = jnp.dot(p[sub, h], vp[pl.ds(win[sub][1], kw), :], preferred_element_type=F32)
                packed = [jnp.concatenate([jnp.where(first, t[sub, 0], t[sub, 1]) for t in (o, m, l)], axis=1)
                          for sub in range(ATT_UNROLL)]
                rows = pl.ds(pl.multiple_of(it * (ATT_UNROLL * Q_BLOCK), ATT_UNROLL * Q_BLOCK), ATT_UNROLL * Q_BLOCK)
                dst[rows, :] = jnp.concatenate(packed, axis=0)
                return carry

            lax.fori_loop(0, seq // (Q_BLOCK * ATT_UNROLL), blk, 0)
            for r in range(dil):
                nat = pl.ds(r, seg, stride=dil) if dil > 1 else pl.ds(0, seq)
                perm = pl.ds(r * seg, seg)
                o_nat[pi][nat, :] = oml_p[perm, pl.ds(0, LANES)]
                m_nat[pi][nat, :] = oml_p[perm, pl.ds(LANES, LANES)]
                l_nat[pi][nat, :] = oml_p[perm, pl.ds(2 * LANES, LANES)]

        def merge(ci, carry):
            rows = pl.ds(pl.multiple_of(ci * 256, 256), 256)
            ms = [m_nat[pi][rows, :] for pi in range(3)]
            m_all = jnp.maximum(jnp.maximum(ms[0], ms[1]), ms[2])
            es = [jnp.exp(m - m_all) for m in ms]
            l_all = sum(l_nat[pi][rows, :] * es[pi] for pi in range(3))
            inv = 1.0 / l_all
            o = sum(o_nat[pi][rows, :] * (es[pi] * inv) for pi in range(3))
            y_ref[rows, :] = o.astype(BF16)
            lse_ref[rows, :] = m_all + jnp.log(l_all)
            return carry

        lax.fori_loop(0, seq // 256, merge, 0)

        @pl.when(step == n_seq * n_hp - 1)
        def _():
            _exchange_wait(*ride_args)

    def col(off):
        return pl.BlockSpec((None, seq, LANES), lambda b, hp: (b, 0, col0 + off * n_hp + hp))

    def whole(arr):
        return pl.BlockSpec(arr.shape, lambda b, hp: (0,) * arr.ndim)

    rows_f32 = pltpu.VMEM((seq, LANES), F32)
    rows_bf16 = pltpu.VMEM((seq, LANES), BF16)
    return _call(
        body, name="attn_fwd", grid=(n_seq, n_hp),
        in_specs=[col(0), col(1), col(2), whole(g_q2), whole(g_k2),
                  pl.BlockSpec((None, 8, LANES), lambda b, hp: (hp, 0, 0)),
                  whole(dms[0]), whole(dms[1]), whole(dms[2])] + [ANY_SPEC] * n_ride,
        out_specs=[pl.BlockSpec((None, seq, LANES), lambda b, hp: (b, 0, hp)),
                   pl.BlockSpec((None, seq, LANES), lambda b, hp: (b, 0, hp))] + [ANY_SPEC] * n_ride,
        out_shape=[_sds((n_seq, seq, D_ATT), BF16), _sds((n_seq, seq, D_ATT), F32)] + _exchange_shapes(ride),
        scratch_shapes=[rows_f32, rows_f32, pltpu.VMEM((2, seq, LANES), BF16), rows_bf16, rows_bf16]
        + [pltpu.VMEM((seq, 3 * LANES), F32)] + [rows_f32] * 9 + _exchange_sems(n_ride),
        compiler_params=_params(("arbitrary", "arbitrary")),
    )(proj3, proj3, proj3, g_q2, g_k2, _alibi_rows(), *dms, *[a for a, _ in ride])


def _attn_bwd(proj3, do3, y_att3, lse3, g_q2, g_k2, ride):
    n_seq, seq, _ = proj3.shape
    dms = [_distance_mats(d, seq // d) for d in DILATIONS]
    col0 = 2 * D_CONV // LANES
    n_hp = D_ATT // LANES

    n_ride = len(ride)
    ride_scatter = [s for _, s in ride]

    def body(*refs):
        q_ref, k_ref, v_ref, do_ref, o_ref, lse_ref, gq_ref, gk_ref, sl_ref, dm1, dm4, dm16 = refs[:12]
        ride_in = refs[12:12 + n_ride]
        dq_ref, dk_ref, dv_ref, dg_ref = refs[12 + n_ride:16 + n_ride]
        ride_out = refs[16 + n_ride:16 + 2 * n_ride]
        (qf, kf, qp, dop, kp, vp, sn, sp, dqp, dkp, dvp, dqn, dkn, dvn) = refs[16 + 2 * n_ride:30 + 2 * n_ride]
        ride_args = (ride_scatter, ride_in, ride_out) + tuple(refs[30 + 2 * n_ride:])
        dm_refs = (dm1, dm4, dm16)
        step = pl.program_id(0) * n_hp + pl.program_id(1)

        @pl.when(step == 0)
        def _():
            _exchange_start(*ride_args)
            dg_ref[...] = jnp.zeros_like(dg_ref)

        _qk_normalise(q_ref, gq_ref, qf, seq, HEAD_DIM ** -0.5)
        _qk_normalise(k_ref, gk_ref, kf, seq, 1.0)

        def stats(ci, carry):
            rows = pl.ds(pl.multiple_of(ci * 256, 256), 256)
            first = _first_head(256)
            prod = do_ref[rows, :] * o_ref[rows, :].astype(F32)
            sn[rows, 0:1] = lse_ref[rows, 0:1]
            sn[rows, 1:2] = lse_ref[rows, HEAD_DIM:HEAD_DIM + 1]
            sn[rows, 2:3] = jnp.sum(jnp.where(first, prod, 0.0), axis=1, keepdims=True)
            sn[rows, 3:4] = jnp.sum(jnp.where(first, 0.0, prod), axis=1, keepdims=True)
            return carry

        lax.fori_loop(0, seq // 256, stats, 0)
        slopes = (sl_ref[0:1, 0:1], sl_ref[1:2, 0:1])
        half = seq // (Q_BLOCK * ATT_UNROLL)
        region = seq // ATT_UNROLL

        for pi, dil in enumerate(DILATIONS):
            seg = seq // dil
            kw = min(2 * Q_BLOCK, seg)
            _permute_rows_by_head(qp, qf, dil, seq)
            _permute_rows_by_head(dop, do_ref, dil, seq)
            _permute_rows(kp, kf, dil, seq)
            _permute_rows(vp, v_ref, dil, seq)
            _permute_rows(sp, sn, dil, seq)
            for sub in range(ATT_UNROLL):
                lo, hi = max(sub * region - RADIUS, 0), min((sub + 1) * region + RADIUS, seq)
                dkp[sub, pl.ds(lo, hi - lo), :] = jnp.zeros((hi - lo, LANES), F32)
                dvp[sub, pl.ds(lo, hi - lo), :] = jnp.zeros((hi - lo, LANES), F32)

            def blk(it, carry, seg=seg, kw=kw, pi=pi):
                first = _first_head(Q_BLOCK)
                chains = [(sub, h) for sub in range(ATT_UNROLL) for h in range(2)]
                win = [_window(it + sub * half, seg) for sub in range(ATT_UNROLL)]
                qrows = [pl.ds(w[0], Q_BLOCK) for w in win]
                krows = [pl.ds(w[1], kw) for w in win]
                s, dp = {}, {}
                for sub, h in chains:
                    s[sub, h] = lax.dot_general(qp[h, qrows[sub], :], kp[krows[sub], :], NT,
                                                preferred_element_type=F32) - slopes[h] * dm_refs[pi][win[sub][2]]
                    dp[sub, h] = lax.dot_general(dop[h, qrows[sub], :], vp[krows[sub], :], NT,
                                                 preferred_element_type=F32)
                p, ds = {}, {}
                for sub, h in chains:
                    e = jnp.exp(s[sub, h] - sp[qrows[sub], h:h + 1])
                    ds[sub, h] = (e * (dp[sub, h] - sp[qrows[sub], 2 + h:3 + h])).astype(BF16)
                    p[sub, h] = e.astype(BF16)
                dq, dk, dv = {}, {}, {}
                for sub, h in chains:
                    dq[sub, h] = jnp.dot(ds[sub, h], kp[krows[sub], :], preferred_element_type=F32)
                    dk[sub, h] = lax.dot_general(ds[sub, h], qp[h, qrows[sub], :], TN, preferred_element_type=F32)
                    dv[sub, h] = lax.dot_general(p[sub, h], dop[h, qrows[sub], :], TN, preferred_element_type=F32)
                for sub in range(ATT_UNROLL):
                    dqp[qrows[sub], :] = jnp.where(first, dq[sub, 0], dq[sub, 1])
                    dkp[sub, krows[sub], :] += dk[sub, 0] + dk[sub, 1]
                    dvp[sub, krows[sub], :] += dv[sub, 0] + dv[sub, 1]
                return carry

            lax.fori_loop(0, half, blk, 0)
            if dil == 1:
                for sub in range(ATT_UNROLL):
                    r0 = sub * region
                    pieces = [(r0, RADIUS, [sub - 1, sub] if sub > 0 else [sub]),
                              (r0 + RADIUS, region - 2 * RADIUS, [sub]),
                              (r0 + region - RADIUS, RADIUS, [sub, sub + 1] if sub < ATT_UNROLL - 1 else [sub])]
                    for start, size, owners in pieces:
                        rows = pl.ds(start, size)
                        dqn[rows, :] = dqp[rows, :]
                        dkn[rows, :] = sum(dkp[o, rows, :] for o in owners)
                        dvn[rows, :] = sum(dvp[o, rows, :] for o in owners)
            else:
                for r in range(dil):
                    nat, perm = pl.ds(r, seg, stride=dil), pl.ds(r * seg, seg)
                    owner = (r * seg) // region
                    dqn[nat, :] += dqp[perm, :]
                    dkn[nat, :] += dkp[owner, perm, :]
                    dvn[nat, :] += dvp[owner, perm, :]

        def finish(ci, carry):
            rows = pl.ds(pl.multiple_of(ci * 256, 256), 256)
            first = _first_head(256)
            for src_ref, g_ref, dn, dst_ref, scale, row in (
                    (q_ref, gq_ref, dqn, dq_ref, HEAD_DIM ** -0.5, 0), (k_ref, gk_ref, dkn, dk_ref, 1.0, 1)):
                xv = src_ref[rows, :]
                r = lax.rsqrt(_head_mean(xv * xv, first) + EPS)
                xhat = xv * r
                d = dn[rows, :] * scale
                dg_ref[row:row + 1, :] += jnp.sum(d * xhat, axis=0, keepdims=True)
                dxh = d * g_ref[...]
                dst_ref[rows, :] = (r * (dxh - xhat * _head_mean(dxh * xhat, first))).astype(BF16)
            dv_ref[rows, :] = dvn[rows, :].astype(BF16)
            return carry

        lax.fori_loop(0, seq // 256, finish, 0)

        @pl.when(step == n_seq * n_hp - 1)
        def _():
            _exchange_wait(*ride_args)

    def col(off):
        return pl.BlockSpec((None, seq, LANES), lambda b, hp: (b, 0, col0 + off * n_hp + hp))

    def whole(arr):
        return pl.BlockSpec(arr.shape, lambda b, hp: (0,) * arr.ndim)

    att = pl.BlockSpec((None, seq, LANES), lambda b, hp: (b, 0, hp))
    rows_f32 = pltpu.VMEM((seq, LANES), F32)
    rows_bf16 = pltpu.VMEM((seq, LANES), BF16)
    by_head_bf16 = pltpu.VMEM((2, seq, LANES), BF16)
    per_sub_f32 = pltpu.VMEM((ATT_UNROLL, seq, LANES), F32)
    return _call(
        body, name="attn_bwd", grid=(n_seq, n_hp),
        in_specs=[col(0), col(1), col(2), att, att, att, whole(g_q2), whole(g_k2),
                  pl.BlockSpec((None, 8, LANES), lambda b, hp: (hp, 0, 0)),
                  whole(dms[0]), whole(dms[1]), whole(dms[2])] + [ANY_SPEC] * n_ride,
        out_specs=[att, att, att, pl.BlockSpec((8, LANES), lambda b, hp: (0, 0))] + [ANY_SPEC] * n_ride,
        out_shape=[_sds((n_seq, seq, D_ATT), BF16)] * 3 + [_sds((8, LANES), F32)] + _exchange_shapes(ride),
        scratch_shapes=[rows_f32, rows_f32, by_head_bf16, by_head_bf16, rows_bf16, rows_bf16, rows_f32, rows_f32,
                        rows_f32, per_sub_f32, per_sub_f32, rows_f32, rows_f32, rows_f32] + _exchange_sems(n_ride),
        compiler_params=_params(("arbitrary", "arbitrary")),
    )(proj3, proj3, proj3, do3, y_att3, lse3, g_q2, g_k2, _alibi_rows(), *dms, *[a for a, _ in ride])


def _mix_out(uc2, y_att2, x2, mod8, g_ln, b_ln, g_ffn, w_out, seq, tm=512):
    tokens = x2.shape[0]
    per_seq = seq // tm

    def body(uc_ref, ya_ref, x_ref, m_ref, gl_ref, bl_ref, gf_ref, w_ref, yc_ref, mix_ref, x1_ref, h2_ref):
        uc = uc_ref[...]
        mu = jnp.mean(uc, axis=-1, keepdims=True)
        cen = uc - mu
        rs = lax.rsqrt(jnp.mean(cen * cen, axis=-1, keepdims=True) + EPS)
        z = cen * rs * gl_ref[...] + bl_ref[...]
        yc = (z * _sig(z)).astype(BF16)
        yc_ref[...] = yc
        mix = (jnp.dot(yc, w_ref[pl.ds(0, D_CONV), :], preferred_element_type=F32)
               + jnp.dot(ya_ref[...], w_ref[pl.ds(D_CONV, D_ATT), :], preferred_element_type=F32))
        mix_ref[...] = mix
        x1 = x_ref[...] + m_ref[2:3, :] * mix
        x1_ref[...] = x1
        r = lax.rsqrt(jnp.mean(x1 * x1, axis=-1, keepdims=True) + EPS)
        h2_ref[...] = ((x1 * r * gf_ref[...]) * (1.0 + m_ref[4:5, :]) + m_ref[3:4, :]).astype(BF16)

    def rows(width):
        return pl.BlockSpec((tm, width), lambda i: (i, 0))

    def vec(width):
        return pl.BlockSpec((1, width), lambda i: (0, 0))

    return _call(
        body, name="mix_out", grid=(tokens // tm,),
        in_specs=[rows(D_CONV), rows(D_ATT), rows(D_MODEL),
                  pl.BlockSpec((None, 8, D_MODEL), lambda i: (i // per_seq, 0, 0)),
                  vec(D_CONV), vec(D_CONV), vec(D_MODEL),
                  pl.BlockSpec((D_MODEL, D_MODEL), lambda i: (0, 0))],
        out_specs=[rows(D_CONV), rows(D_MODEL), rows(D_MODEL), rows(D_MODEL)],
        out_shape=[_sds((tokens, D_CONV), BF16), _sds((tokens, D_MODEL), F32),
                   _sds((tokens, D_MODEL), F32), _sds((tokens, D_MODEL), BF16)],
        compiler_params=_params(("parallel",)),
    )(uc2, y_att2, x2, mod8, g_ln, b_ln, g_ffn, w_out)


def _mix_out_bwd(dmix, uc2, g_ln, b_ln, w_out, tm=512):
    tokens = dmix.shape[0]

    def body(dm_ref, uc_ref, gl_ref, bl_ref, w_ref, duc_ref, do_ref, dgb_ref):
        @pl.when(pl.program_id(0) == 0)
        def _():
            dgb_ref[...] = jnp.zeros_like(dgb_ref)

        dmv = dm_ref[...]
        dyc = lax.dot_general(dmv, w_ref[pl.ds(0, D_CONV), :], NT, preferred_element_type=F32)
        do_ref[...] = lax.dot_general(dmv, w_ref[pl.ds(D_CONV, D_ATT), :], NT, preferred_element_type=F32)
        uc = uc_ref[...]
        mu = jnp.mean(uc, axis=-1, keepdims=True)
        cen = uc - mu
        rs = lax.rsqrt(jnp.mean(cen * cen, axis=-1, keepdims=True) + EPS)
        xh = cen * rs
        z = xh * gl_ref[...] + bl_ref[...]
        sg = _sig(z)
        dz = dyc * (sg * (1.0 + z * (1.0 - sg)))
        dgb_ref[0:1, :] += jnp.sum(dz * xh, axis=0, keepdims=True)
        dgb_ref[1:2, :] += jnp.sum(dz, axis=0, keepdims=True)
        dxh = dz * gl_ref[...]
        duc_ref[...] = rs * (dxh - jnp.mean(dxh, axis=-1, keepdims=True)
                             - xh * jnp.mean(dxh * xh, axis=-1, keepdims=True))

    return _call(
        body, name="mix_out_bwd", grid=(tokens // tm,),
        in_specs=[pl.BlockSpec((tm, D_MODEL), lambda i: (i, 0)),
                  pl.BlockSpec((tm, D_CONV), lambda i: (i, 0)),
                  pl.BlockSpec((1, D_CONV), lambda i: (0, 0)),
                  pl.BlockSpec((1, D_CONV), lambda i: (0, 0)),
                  pl.BlockSpec((D_MODEL, D_MODEL), lambda i: (0, 0))],
        out_specs=[pl.BlockSpec((tm, D_CONV), lambda i: (i, 0)),
                   pl.BlockSpec((tm, D_ATT), lambda i: (i, 0)),
                   pl.BlockSpec((8, D_CONV), lambda i: (0, 0))],
        out_shape=[_sds((tokens, D_CONV), F32), _sds((tokens, D_ATT), F32), _sds((8, D_CONV), F32)],
        compiler_params=_params(("arbitrary",)),
    )(dmix, uc2, g_ln, b_ln, w_out)


FF_TILE = 256
FF_ROWS = 256


def _ffn_fwd(h2, w_gate, w_up, w_down, x1, target, mod8, seq, tm=1024):
    tokens = h2.shape[0]
    per_seq = seq // tm
    n_seq = tokens // seq
    last = D_FF // FF_TILE - 1

    def body(h_ref, wg_ref, wu_ref, wd_ref, x1_ref, t_ref, m_ref, gate_ref, up_ref, dy_ref, df_ref, sq_ref, dgf_ref,
             f_ref):
        i, j = pl.program_id(0), pl.program_id(1)

        @pl.when(j == 0)
        def _():
            f_ref[...] = jnp.zeros_like(f_ref)

        @pl.when((j == 0) & (i == 0))
        def _():
            sq_ref[...] = jnp.zeros_like(sq_ref)

        @pl.when((j == 0) & (i % per_seq == 0))
        def _():
            dgf_ref[...] = jnp.zeros_like(dgf_ref)

        def gate_up(r):
            hv = h_ref[pl.ds(r * FF_ROWS, FF_ROWS), :]
            return (jnp.dot(hv, wg_ref[...], preferred_element_type=F32),
                    jnp.dot(hv, wu_ref[...], preferred_element_type=F32))

        ahead = gate_up(0)
        for r in range(tm // FF_ROWS):
            gate, up = ahead
            if r + 1 < tm // FF_ROWS:
                ahead = gate_up(r + 1)
            rows = pl.ds(r * FF_ROWS, FF_ROWS)
            gate_ref[rows, :] = gate
            up_ref[rows, :] = up
            act = (gate * _sig(gate) * up).astype(BF16)
            f_ref[rows, :] += jnp.dot(act, wd_ref[...], preferred_element_type=F32)

        @pl.when(j == last)
        def _():
            gate_f = m_ref[5:6, :]
            for r in range(tm // FF_ROWS):
                rows = pl.ds(r * FF_ROWS, FF_ROWS)
                fv = f_ref[rows, :]
                diff = x1_ref[rows, :] + gate_f * fv - t_ref[rows, :]
                sq_ref[0:1, :] += jnp.sum(diff * diff, axis=0, keepdims=True)
                dy = diff * (1.0 / D_MODEL)
                dy_ref[rows, :] = dy
                df_ref[rows, :] = (gate_f * dy).astype(BF16)
                dgf_ref[0:1, :] += jnp.sum(dy * fv, axis=0, keepdims=True)

    rows_spec = pl.BlockSpec((tm, D_MODEL), lambda i, j: (i, 0))
    per = pl.BlockSpec((None, 8, D_MODEL), lambda i, j: (i // per_seq, 0, 0))
    tile = pl.BlockSpec((tm, FF_TILE), lambda i, j: (i, j))
    return _call(
        body, name="ffn_fwd", grid=(tokens // tm, D_FF // FF_TILE),
        in_specs=[rows_spec,
                  pl.BlockSpec((D_MODEL, FF_TILE), lambda i, j: (0, j)),
                  pl.BlockSpec((D_MODEL, FF_TILE), lambda i, j: (0, j)),
                  pl.BlockSpec((FF_TILE, D_MODEL), lambda i, j: (j, 0)),
                  rows_spec, rows_spec, per],
        out_specs=[tile, tile, rows_spec, rows_spec, pl.BlockSpec((8, D_MODEL), lambda i, j: (0, 0)), per],
        out_shape=[_sds((tokens, D_FF), F32), _sds((tokens, D_FF), F32), _sds((tokens, D_MODEL), F32),
                   _sds((tokens, D_MODEL), BF16), _sds((8, D_MODEL), F32), _sds((n_seq, 8, D_MODEL), F32)],
        scratch_shapes=[pltpu.VMEM((tm, D_MODEL), F32)],
        compiler_params=_params(("arbitrary", "arbitrary")),
    )(h2, w_gate, w_up, w_down, x1, target, mod8)


def _ffn_bwd(df, gate, up, w_gate, w_up, w_down, x1, dy, mix, mod8, g_ffn, seq, tm=1024):
    tokens = df.shape[0]
    per_seq = seq // tm
    n_seq = tokens // seq
    last = D_FF // FF_TILE - 1

    def body(df_ref, gate_ref, up_ref, wg_ref, wu_ref, wd_ref, m_ref, g_ref, x1_hbm, dy_hbm, mix_hbm,
             dgate_ref, dup_ref, act_ref, dx1_ref, dmix_ref, dg_ref, dm_ref, dh_ref, late, late_sems):
        i, j = pl.program_id(0), pl.program_id(1)
        my_rows = pl.ds(pl.multiple_of(i * tm, tm), tm)
        fetches = [pltpu.make_async_copy(src.at[my_rows, :], late.at[n], late_sems.at[n])
                   for n, src in enumerate((x1_hbm, dy_hbm, mix_hbm))]

        @pl.when(j == 0)
        def _():
            dh_ref[...] = jnp.zeros_like(dh_ref)
            for cp in fetches:
                cp.start()

        @pl.when((j == 0) & (i == 0))
        def _():
            dg_ref[...] = jnp.zeros_like(dg_ref)

        @pl.when((j == 0) & (i % per_seq == 0))
        def _():
            dm_ref[...] = jnp.zeros_like(dm_ref)

        def d_act(r):
            return lax.dot_general(df_ref[pl.ds(r * FF_ROWS, FF_ROWS), :], wd_ref[...], NT,
                                   preferred_element_type=F32)

        ahead = d_act(0)
        for r in range(tm // FF_ROWS):
            dact = ahead
            if r + 1 < tm // FF_ROWS:
                ahead = d_act(r + 1)
            rows = pl.ds(r * FF_ROWS, FF_ROWS)
            gate = gate_ref[rows, :]
            up = up_ref[rows, :]
            sg = _sig(gate)
            silu = gate * sg
            act_ref[rows, :] = (silu * up).astype(BF16)
            dup = (dact * silu).astype(BF16)
            dgate = (dact * up * (sg * (1.0 + gate * (1.0 - sg)))).astype(BF16)
            dup_ref[rows, :] = dup
            dgate_ref[rows, :] = dgate
            dh_ref[rows, :] += (lax.dot_general(dgate, wg_ref[...], NT, preferred_element_type=F32)
                                + lax.dot_general(dup, wu_ref[...], NT, preferred_element_type=F32))

        @pl.when(j == last)
        def _():
            for cp in fetches:
                cp.wait()
            g = g_ref[...]
            for r in range(tm // FF_ROWS):
                rows = pl.ds(r * FF_ROWS, FF_ROWS)
                dh = dh_ref[rows, :]
                x1v = late[0, rows, :]
                rs = lax.rsqrt(jnp.mean(x1v * x1v, axis=-1, keepdims=True) + EPS)
                xhat = x1v * rs
                dm_ref[0:1, :] += jnp.sum(dh, axis=0, keepdims=True)
                dm_ref[1:2, :] += jnp.sum(dh * (xhat * g), axis=0, keepdims=True)
                dn = dh * (1.0 + m_ref[4:5, :])
                dg_ref[0:1, :] += jnp.sum(dn * xhat, axis=0, keepdims=True)
                dxh = dn * g
                dx1 = late[1, rows, :] + rs * (dxh - xhat * jnp.mean(dxh * xhat, axis=-1, keepdims=True))
                dx1_ref[rows, :] = dx1
                dm_ref[2:3, :] += jnp.sum(dx1 * late[2, rows, :], axis=0, keepdims=True)
                dmix_ref[rows, :] = (m_ref[2:3, :] * dx1).astype(BF16)

    tile = pl.BlockSpec((tm, FF_TILE), lambda i, j: (i, j))
    rows_spec = pl.BlockSpec((tm, D_MODEL), lambda i, j: (i, 0))
    per = pl.BlockSpec((None, 8, D_MODEL), lambda i, j: (i // per_seq, 0, 0))
    return _call(
        body, name="ffn_bwd", grid=(tokens // tm, D_FF // FF_TILE),
        in_specs=[rows_spec, tile, tile,
                  pl.BlockSpec((D_MODEL, FF_TILE), lambda i, j: (0, j)),
                  pl.BlockSpec((D_MODEL, FF_TILE), lambda i, j: (0, j)),
                  pl.BlockSpec((FF_TILE, D_MODEL), lambda i, j: (j, 0)),
                  per, pl.BlockSpec((1, D_MODEL), lambda i, j: (0, 0)), ANY_SPEC, ANY_SPEC, ANY_SPEC],
        out_specs=[tile, tile, tile, rows_spec, rows_spec, pl.BlockSpec((8, D_MODEL), lambda i, j: (0, 0)), per],
        out_shape=[_sds((tokens, D_FF), BF16)] * 3 + [_sds((tokens, D_MODEL), F32), _sds((tokens, D_MODEL), BF16),
                                                    _sds((8, D_MODEL), F32), _sds((n_seq, 8, D_MODEL), F32)],
        scratch_shapes=[pltpu.VMEM((tm, D_MODEL), F32), pltpu.VMEM((3, tm, D_MODEL), F32),
                        pltpu.SemaphoreType.DMA((3,))],
        compiler_params=_params(("arbitrary", "arbitrary")),
    )(df, gate, up, w_gate, w_up, w_down, mod8, g_ffn, x1, dy, mix)


def _mix_in_bwd(d_a, d_g, d_q, d_k, d_v, w_in, x2, dx1, mod8, g_mix, seq, ride, tm=512):
    tokens = x2.shape[0]
    per_seq = seq // tm
    n_seq = tokens // seq
    parts = (d_a, d_g, d_q, d_k, d_v)
    width = D_CONV
    n_ride = len(ride)
    ride_scatter = [s for _, s in ride]

    def body(*refs):
        da_ref, dg_ref, dq_ref, dk_ref, dv_ref, w_ref, x_ref, dx1_ref, m_ref, g_ref = refs[:10]
        ride_in = refs[10:10 + n_ride]
        gx_ref, dgm_ref, dm_ref = refs[10 + n_ride:13 + n_ride]
        ride_args = (ride_scatter, ride_in, refs[13 + n_ride:13 + 2 * n_ride]) + tuple(refs[13 + 2 * n_ride:])
        i = pl.program_id(0)

        @pl.when(i == 0)
        def _():
            _exchange_start(*ride_args)
            dgm_ref[...] = jnp.zeros_like(dgm_ref)

        @pl.when(i % per_seq == 0)
        def _():
            dm_ref[...] = jnp.zeros_like(dm_ref)

        dh = jnp.zeros((tm, D_MODEL), F32)
        for n, ref in enumerate((da_ref, dg_ref, dq_ref, dk_ref, dv_ref)):
            dh = dh + lax.dot_general(ref[...], w_ref[:, pl.ds(n * width, width)], NT, preferred_element_type=F32)
        xv = x_ref[...]
        r = lax.rsqrt(jnp.mean(xv * xv, axis=-1, keepdims=True) + EPS)
        xhat = xv * r
        g = g_ref[...]
        dm_ref[0:1, :] += jnp.sum(dh, axis=0, keepdims=True)
        dm_ref[1:2, :] += jnp.sum(dh * (xhat * g), axis=0, keepdims=True)
        dn = dh * (1.0 + m_ref[1:2, :])
        dgm_ref[0:1, :] += jnp.sum(dn * xhat, axis=0, keepdims=True)
        dxh = dn * g
        gx_ref[...] = dx1_ref[...] + r * (dxh - xhat * jnp.mean(dxh * xhat, axis=-1, keepdims=True))

        @pl.when(i == tokens // tm - 1)
        def _():
            _exchange_wait(*ride_args)

    rows = pl.BlockSpec((tm, D_MODEL), lambda i: (i, 0))
    half = pl.BlockSpec((tm, width), lambda i: (i, 0))
    per = pl.BlockSpec((None, 8, D_MODEL), lambda i: (i // per_seq, 0, 0))
    return _call(
        body, name="mix_in_bwd", grid=(tokens // tm,),
        in_specs=[half] * 5 + [pl.BlockSpec((D_MODEL, D_IN), lambda i: (0, 0)), rows, rows, per,
                               pl.BlockSpec((1, D_MODEL), lambda i: (0, 0))] + [ANY_SPEC] * n_ride,
        out_specs=[rows, pl.BlockSpec((8, D_MODEL), lambda i: (0, 0)), per] + [ANY_SPEC] * n_ride,
        out_shape=[_sds((tokens, D_MODEL), F32), _sds((8, D_MODEL), F32), _sds((n_seq, 8, D_MODEL), F32)]
        + _exchange_shapes(ride),
        scratch_shapes=_exchange_sems(n_ride),
        compiler_params=_params(("arbitrary",)),
    )(*parts, w_in, x2, dx1, mod8, g_mix, *[a for a, _ in ride])


def _grad_matmul_parts(a_parts, b_parts, name, tk=1024):
    tokens = a_parts[0].shape[0]
    na, nb = len(a_parts), len(b_parts)
    ma, nbw = a_parts[0].shape[1], b_parts[0].shape[1]

    n_k = tokens // tk

    def body(*refs):
        a_refs, b_refs, o_ref, acc = refs[:na], refs[na:na + nb], refs[na + nb], refs[na + nb + 1]

        @pl.when(pl.program_id(0) == 0)
        def _():
            acc[...] = jnp.zeros_like(acc)

        for i in range(na):
            for j in range(nb):
                acc[pl.ds(i * ma, ma), pl.ds(j * nbw, nbw)] += lax.dot_general(
                    a_refs[i][...], b_refs[j][...], TN, preferred_element_type=F32)

        @pl.when(pl.program_id(0) == n_k - 1)
        def _():
            o_ref[...] = acc[...].astype(o_ref.dtype)

    return _call(
        body, name=name, grid=(n_k,),
        in_specs=[pl.BlockSpec((tk, ma), lambda k: (k, 0))] * na + [pl.BlockSpec((tk, nbw), lambda k: (k, 0))] * nb,
        out_specs=pl.BlockSpec((na * ma, nb * nbw), lambda k: (0, 0)),
        out_shape=_sds((na * ma, nb * nbw), BF16),
        scratch_shapes=[pltpu.VMEM((na * ma, nb * nbw), F32)],
        compiler_params=_params(("arbitrary",)),
    )(*a_parts, *b_parts)


def _grad_matmul(a, b, name, tmo, tno, tk=1024):
    tokens, m = a.shape
    n = b.shape[1]
    n_k = tokens // tk

    def body(a_ref, b_ref, o_ref, acc):
        @pl.when(pl.program_id(2) == 0)
        def _():
            acc[...] = jnp.zeros_like(acc)

        acc[...] += lax.dot_general(a_ref[...], b_ref[...], TN, preferred_element_type=F32)

        @pl.when(pl.program_id(2) == n_k - 1)
        def _():
            o_ref[...] = acc[...].astype(o_ref.dtype)

    return _call(
        body, name=name, grid=(m // tmo, n // tno, n_k),
        in_specs=[pl.BlockSpec((tk, tmo), lambda i, j, k: (k, i)),
                  pl.BlockSpec((tk, tno), lambda i, j, k: (k, j))],
        out_specs=pl.BlockSpec((tmo, tno), lambda i, j, k: (i, j)),
        out_shape=_sds((m, n), BF16),
        scratch_shapes=[pltpu.VMEM((tmo, tno), F32)],
        compiler_params=_params(("parallel", "parallel", "arbitrary")),
    )(a, b)


def _adamw(w, m, v, g, name, n_parts=0, tr=256):
    rows, cols = w.shape
    tr = min(tr, rows)
    c1 = 1.0 - ADAM_B1 ** ADAM_STEP
    c2 = 1.0 - ADAM_B2 ** ADAM_STEP

    def body(w_ref, m_ref, v_ref, g_ref, go_ref, d_ref, mo_ref, vo_ref):
        if n_parts:
            gv = g_ref[0].astype(F32)
            for p in range(1, n_parts):
                gv = gv + g_ref[p].astype(F32)
        else:
            gv = g_ref[...]
        go_ref[...] = gv
        mn = ADAM_B1 * m_ref[...] + (1.0 - ADAM_B1) * gv
        vn = ADAM_B2 * v_ref[...] + (1.0 - ADAM_B2) * (gv * gv)
        mo_ref[...] = mn
        vo_ref[...] = vn
        d_ref[...] = -ADAM_LR * ((mn / c1) / (jnp.sqrt(vn / c2) + ADAM_EPS) + ADAM_WD * w_ref[...])

    blk = pl.BlockSpec((tr, cols), lambda i: (i, 0))
    g_spec = pl.BlockSpec((n_parts, tr, cols), lambda i: (0, i, 0)) if n_parts else blk
    return _call(
        body, name=name, grid=(rows // tr,),
        in_specs=[blk, blk, blk, g_spec], out_specs=[blk] * 4,
        out_shape=[_sds((rows, cols), F32)] * 4,
        compiler_params=_params(("parallel",)),
    )(w, m, v, g)


def _cols_to_full(blocks):
    n, r, c = blocks.shape
    return jnp.transpose(blocks, (1, 0, 2)).reshape(r, n * c)


def _full_to_cols(full, n=N_DEV):
    r, c = full.shape
    return jnp.transpose(full.reshape(r, n, c // n), (1, 0, 2))


def _pad_lanes(v, width):
    return jnp.pad(v, ((0, 0), (0, width - v.shape[1])))


def kernel(x, c, w_ada, b_ada, g_mix, w_in, w_dw, b_dw, g_conv_ln, b_conv_ln, g_q, g_k, w_out, g_ffn, w_gate, w_up, w_down, loss_target, m_w_ada, m_b_ada, m_g_mix, m_w_in, m_w_dw, m_b_dw, m_g_conv_ln, m_b_conv_ln, m_g_q, m_g_k, m_w_out, m_g_ffn, m_w_gate, m_w_up, m_w_down, v_w_ada, v_b_ada, v_g_mix, v_w_in, v_w_dw, v_b_dw, v_g_conv_ln, v_b_conv_ln, v_g_q, v_g_k, v_w_out, v_g_ffn, v_w_gate, v_w_up, v_w_down):
    n_seq, seq, _ = x.shape
    tokens = n_seq * seq
    me = 4 * lax.axis_index("x") + 2 * lax.axis_index("y") + lax.axis_index("c")
    ada_cols = w_ada.shape[2]
    dw_cols = w_dw.shape[2]

    (c_g, w_in_g, w_dw_g) = _exchange(
        [(c, False), (w_in[0].astype(BF16), False), (w_dw[0], False)], "gather_weights")
    c_all = c_g.reshape(N_DEV * n_seq, D_MODEL)
    w_in_f = _cols_to_full(w_in_g)
    w_dw_f = _cols_to_full(w_dw_g)

    b_cols = lax.dynamic_slice(b_ada, (0, me * ada_cols), (1, ada_cols))
    mod_cols = _ada_fwd(c_all, w_ada[0], b_cols)
    (mod_g,) = _exchange([(mod_cols, False)], "gather_mod")
    mod_mine = lax.dynamic_slice(mod_g, (0, me * n_seq, 0), (N_DEV, n_seq, ada_cols))
    mod = jnp.transpose(mod_mine, (1, 0, 2)).reshape(n_seq, N_MOD, D_MODEL)
    mod8 = jnp.pad(mod, ((0, 0), (0, 8 - N_MOD), (0, 0)))

    x2 = x.reshape(tokens, D_MODEL)
    h1, proj = _mix_in(x2, mod8, g_mix, w_in_f, seq)
    proj3 = proj.reshape(n_seq, seq, D_IN)
    uc3 = _conv_fwd(proj3, w_dw_f, b_dw)
    g_q2, g_k2 = jnp.tile(g_q, (1, 2)), jnp.tile(g_k, (1, 2))
    y_att3, lse3, w_out_g, w_gate_g, w_up_g, w_down_g = _attn_fwd(
        proj3, g_q2, g_k2,
        [(w_out[0].astype(BF16), False), (w_gate[0].astype(BF16), False), (w_up[0].astype(BF16), False),
         (w_down[0].astype(BF16), False)])
    w_out_f = w_out_g.reshape(D_MODEL, D_MODEL)
    w_gate_f = _cols_to_full(w_gate_g)
    w_up_f = _cols_to_full(w_up_g)
    w_down_f = w_down_g.reshape(D_FF, D_MODEL)
    uc2 = uc3.reshape(tokens, D_CONV)
    y_att2 = y_att3.reshape(tokens, D_ATT)
    y_conv, mix, x1, h2 = _mix_out(uc2, y_att2, x2, mod8, g_conv_ln, b_conv_ln, g_ffn, w_out_f, seq)
    gate, up, dy, df, sq, dgate_f = _ffn_fwd(
        h2, w_gate_f, w_up_f, w_down_f, x1, loss_target.reshape(tokens, D_MODEL), mod8, seq)

    dgate, dup, act, dx1, dmix, dg_ffn, dmod_f = _ffn_bwd(
        df, gate, up, w_gate_f, w_up_f, w_down_f, x1, dy, mix, mod8, g_ffn, seq)
    duc2, do2, dgb_ln = _mix_out_bwd(dmix, uc2, g_conv_ln, b_conv_ln, w_out_f)
    d_a3, d_g3, dw_dw_p, db_dw_p = _conv_bwd(duc2.reshape(n_seq, seq, D_CONV), proj3, w_dw_f)
    gw_gate = _grad_matmul(h2, dgate, "grad_w_gate", D_MODEL, D_FF // 2)
    gw_up = _grad_matmul(h2, dup, "grad_w_up", D_MODEL, D_FF // 2)
    gw_down = _grad_matmul(act, df, "grad_w_down", D_FF // 2, D_MODEL)
    gw_out = _grad_matmul_parts([y_conv, y_att2], [dmix], "grad_w_out")
    d_q3, d_k3, d_v3, dg_qk, p_gate, p_up, p_down, p_out = _attn_bwd(
        proj3, do2.reshape(n_seq, seq, D_ATT), y_att3, lse3, g_q2, g_k2,
        [(_full_to_cols(gw_gate).astype(BF16), True), (_full_to_cols(gw_up).astype(BF16), True),
         (gw_down.astype(BF16).reshape(N_DEV, D_FF // N_DEV, D_MODEL), True),
         (gw_out.astype(BF16).reshape(N_DEV, D_MODEL // N_DEV, D_MODEL), True)])
    flat = lambda t: t.reshape(tokens, t.shape[-1])
    d_a, d_g, d_q, d_k, d_v = flat(d_a3), flat(d_g3), flat(d_q3), flat(d_k3), flat(d_v3)
    gw_in = _grad_matmul_parts([h1], [d_a, d_g, d_q, d_k, d_v], "grad_w_in")
    grad_x2, dg_mix, dmod_m, p_in = _mix_in_bwd(
        d_a, d_g, d_q, d_k, d_v, w_in_f, x2, dx1, mod8, g_mix, seq, [(_full_to_cols(gw_in).astype(BF16), True)])

    dmod = jnp.concatenate([dmod_m[:, 0], dmod_m[:, 1], dmod_f[:, 2], dmod_f[:, 0], dmod_f[:, 1], dgate_f[:, 0]], axis=1)
    dg_q = dg_qk[0:1, 0:HEAD_DIM] + dg_qk[0:1, HEAD_DIM:]
    dg_k = dg_qk[1:2, 0:HEAD_DIM] + dg_qk[1:2, HEAD_DIM:]
    loss_part = (0.5 / D_MODEL) * jnp.sum(sq[0:1, :], axis=1, keepdims=True)
    small = jnp.concatenate(
        [dg_mix[0:1], dg_ffn[0:1], db_dw_p[0:1], dgb_ln[0:1], dgb_ln[1:2],
         _pad_lanes(dg_q, LANES), _pad_lanes(dg_k, LANES), _pad_lanes(loss_part, LANES)], axis=1)
    n_small = small.shape[1] - LANES

    (dmod_g, small_g, dw_g) = _exchange([(dmod, False), (small, False), (dw_dw_p, False)], "gather_small_grads")

    dmod_all = dmod_g.reshape(N_DEV * n_seq, N_MOD * D_MODEL)
    dmod_cols = lax.dynamic_slice(dmod_all, (0, me * ada_cols), (N_DEV * n_seq, ada_cols))
    gw_ada, gb_ada = _ada_bwd(c_all, dmod_cols, dmod_all)

    res = {}
    res["w_ada"] = _adamw(w_ada[0], m_w_ada[0], v_w_ada[0], gw_ada, "adamw_w_ada")
    res["b_ada"] = _adamw(b_ada, m_b_ada, v_b_ada, gb_ada, "adamw_b_ada")
    res["w_in"] = _adamw(w_in[0], m_w_in[0], v_w_in[0], p_in, "adamw_w_in", N_DEV)
    res["w_out"] = _adamw(w_out[0], m_w_out[0], v_w_out[0], p_out, "adamw_w_out", N_DEV)
    res["w_gate"] = _adamw(w_gate[0], m_w_gate[0], v_w_gate[0], p_gate, "adamw_w_gate", N_DEV)
    res["w_up"] = _adamw(w_up[0], m_w_up[0], v_w_up[0], p_up, "adamw_w_up", N_DEV)
    res["w_down"] = _adamw(w_down[0], m_w_down[0], v_w_down[0], p_down, "adamw_w_down", N_DEV, tr=176)
    dw_mine = lax.dynamic_slice(dw_g, (0, 0, me * dw_cols), (N_DEV, CONV_WIDTH, dw_cols))
    res["w_dw"] = _adamw(w_dw[0], m_w_dw[0], v_w_dw[0], dw_mine, "adamw_w_dw", N_DEV)

    small_names = ["g_mix", "g_ffn", "b_dw", "g_conv_ln", "b_conv_ln", "g_q", "g_k"]
    small_w = {"g_mix": (g_mix, m_g_mix, v_g_mix), "g_ffn": (g_ffn, m_g_ffn, v_g_ffn), "b_dw": (b_dw, m_b_dw, v_b_dw),
               "g_conv_ln": (g_conv_ln, m_g_conv_ln, v_g_conv_ln), "b_conv_ln": (b_conv_ln, m_b_conv_ln, v_b_conv_ln),
               "g_q": (g_q, m_g_q, v_g_q), "g_k": (g_k, m_g_k, v_g_k)}
    widths = [max(small_w[n][0].shape[1], LANES) for n in small_names]
    packed = [jnp.concatenate([_pad_lanes(small_w[n][i], wd) for n, wd in zip(small_names, widths)], axis=1) for i in range(3)]
    outs = _adamw(packed[0], packed[1], packed[2], small_g[:, :, :n_small], "adamw_small", N_DEV)
    off = 0
    for n, wd in zip(small_names, widths):
        real = small_w[n][0].shape[1]
        res[n] = tuple(o[:, off:off + real] for o in outs)
        off += wd
    loss = jnp.sum(small_g[:, 0, n_small])

    order = ["w_ada", "b_ada", "g_mix", "w_in", "w_dw", "b_dw", "g_conv_ln", "b_conv_ln", "g_q", "g_k",
             "w_out", "g_ffn", "w_gate", "w_up", "w_down"]
    lead = {"w_ada", "w_in", "w_dw", "w_out", "w_gate", "w_up", "w_down"}
    grads, deltas, new_m, new_v = [], [], [], []
    for n in order:
        g, d, mn, vn = res[n]
        g, d, mn, vn = (t[None] if n in lead else t for t in (g, d, mn, vn))
        grads.append(g)
        deltas.append(d)
        new_m.append(mn)
        new_v.append(vn)
    return (loss, grad_x2.reshape(n_seq, seq, D_MODEL), *grads, *deltas, *new_m, *new_v)
```

```python
import numpy as np
import jax
import jax.numpy as jnp
from jax import lax
from jax.experimental import pallas as pl
from jax.experimental.pallas import tpu as pltpu

F32 = jnp.float32
BF16 = jnp.bfloat16

N_DEV = 8
D_MODEL = 1024
D_CONV = 512
D_ATT = 512
HEAD_DIM = 64
CONV_WIDTH = 31
D_IN = 2 * D_CONV + 3 * D_ATT
D_FF = 2816
N_MOD = 6
EPS = 1e-6
RADIUS = 64
DILATIONS = (1, 4, 16)
Q_BLOCK = 128
LANES = 128
VMEM_LIMIT = 56 * 1024 * 1024

ADAM_LR = 0.001
ADAM_B1 = 0.9
ADAM_B2 = 0.999
ADAM_EPS = 1e-08
ADAM_WD = 0.01
ADAM_STEP = 10

NT = (((1,), (1,)), ((), ()))
TN = (((0,), (0,)), ((), ()))


def _call(body, **kw):
    return pl.pallas_call(body, **kw)


def _params(sem=None, vmem=VMEM_LIMIT):
    return pltpu.CompilerParams(dimension_semantics=sem, vmem_limit_bytes=vmem)


def _sig(x):
    return 1.0 / (1.0 + jnp.exp(-x))


def _sds(shape, dtype):
    return jax.ShapeDtypeStruct(shape, dtype)


N_PEER = N_DEV - 1
ANY_SPEC = pl.BlockSpec(memory_space=pl.ANY)


def _exchange_copies(scatter, ins, outs, *sems):
    n = len(ins)
    if n == 0:
        return [], []
    send_sems, recv_sems, local_sems = sems
    x, y, c = lax.axis_index("x"), lax.axis_index("y"), lax.axis_index("c")
    me = 4 * x + 2 * y + c

    def src(a, slot):
        return ins[a].at[slot] if scatter[a] else ins[a]

    local = [pltpu.make_async_copy(src(a, me), outs[a].at[me], local_sems.at[a]) for a in range(n)]
    flights = []
    for k in range(1, N_DEV):
        px = 1 - x if k & 4 else x
        py = 1 - y if k & 2 else y
        pc = 1 - c if k & 1 else c
        pid = 4 * px + 2 * py + pc
        for a in range(n):
            i = a * N_PEER + k - 1
            send, recv = (pltpu.make_async_remote_copy(
                src_ref=src(a, pid), dst_ref=outs[a].at[slot],
                send_sem=send_sems.at[i], recv_sem=recv_sems.at[i],
                device_id=(px, py, pc), device_id_type=pl.DeviceIdType.MESH) for slot in (me, pid))
            flights.append((send, recv))
    return local, flights


def _exchange_start(*args):
    local, flights = _exchange_copies(*args)
    for cp in local:
        cp.start()
    for send, _ in flights:
        send.start()


def _exchange_wait(*args):
    local, flights = _exchange_copies(*args)
    for send, recv in flights:
        send.wait_send()
        recv.wait_recv()
    for cp in local:
        cp.wait()


def _exchange_shapes(items):
    return [_sds((N_DEV,) + tuple(arr.shape[1:] if scatter else arr.shape), arr.dtype) for arr, scatter in items]


def _exchange_sems(n):
    if n == 0:
        return []
    return [pltpu.SemaphoreType.DMA((n * N_PEER,)), pltpu.SemaphoreType.DMA((n * N_PEER,)),
            pltpu.SemaphoreType.DMA((n,))]


def _gather_by_chip(arrays, name):
    n = len(arrays)
    per = N_PEER

    def body(*refs):
        ins, outs = refs[:n], refs[n:2 * n]
        send_sems, recv_sems, local_sems = refs[2 * n:]
        x, y, c = lax.axis_index("x"), lax.axis_index("y"), lax.axis_index("c")
        sibling = (x, y, 1 - c)
        chips = [(1 - x, y), (x, 1 - y), (1 - x, 1 - y)]

        def slot(px, py, pc):
            return 4 * px + 2 * py + pc

        def copy(a, k, block, to, src=None):
            dst = outs[a].at[slot(*block)]
            return pltpu.make_async_remote_copy(
                src_ref=dst if src is None else src, dst_ref=dst,
                send_sem=send_sems.at[a * per + k], recv_sem=recv_sems.at[a * per + k],
                device_id=to, device_id_type=pl.DeviceIdType.MESH)

        me = (x, y, c)
        local = [pltpu.make_async_copy(ins[a], outs[a].at[slot(*me)], local_sems.at[a]) for a in range(n)]
        for cp in local:
            cp.start()
        first = []
        for a in range(n):
            first.append(copy(a, 0, me, sibling, src=ins[a]))
            first += [copy(a, 1 + j, me, (*chip, c), src=ins[a]) for j, chip in enumerate(chips)]
        for cp in first:
            cp.start()
        passed = []
        for j, chip in enumerate(chips):
            for a in range(n):
                copy(a, 1 + j, (*chip, c), me).wait_recv()
                fwd = copy(a, 4 + j, (*chip, c), sibling)
                fwd.start()
                passed.append(fwd)
        for a in range(n):
            copy(a, 0, sibling, me).wait_recv()
            for j, chip in enumerate(chips):
                copy(a, 4 + j, (*chip, 1 - c), me).wait_recv()
        for cp in first + passed:
            cp.wait_send()
        for cp in local:
            cp.wait()

    return _call(
        body, name=name, out_shape=_exchange_shapes([(arr, False) for arr in arrays]),
        in_specs=[ANY_SPEC] * n, out_specs=[ANY_SPEC] * n, scratch_shapes=_exchange_sems(n),
    )(*arrays)


def _exchange(items, name):
    n = len(items)
    scatter = [s for _, s in items]

    def body(*refs):
        args = (scatter, refs[:n], refs[n:2 * n]) + tuple(refs[2 * n:])
        _exchange_start(*args)
        _exchange_wait(*args)

    return _call(
        body, name=name, out_shape=_exchange_shapes(items),
        in_specs=[ANY_SPEC] * n, out_specs=[ANY_SPEC] * n, scratch_shapes=_exchange_sems(n),
    )(*[a for a, _ in items])


def _ada_fwd(c_all, w_ada, b_cols):
    def body(c_ref, w_ref, b_ref, o_ref):
        cv = c_ref[...]
        sc = (cv * _sig(cv)).astype(BF16)
        o_ref[...] = jnp.dot(sc, w_ref[...].astype(BF16), preferred_element_type=F32) + b_ref[...]

    return _call(body, name="ada_fwd", out_shape=_sds((c_all.shape[0], w_ada.shape[1]), F32),
                 compiler_params=_params())(c_all, w_ada, b_cols)


def _ada_bwd(c_all, dmod_cols, dmod_all):
    def body(c_ref, dc_ref, da_ref, gw_ref, gb_ref):
        cv = c_ref[...]
        sc = (cv * _sig(cv)).astype(BF16)
        gw_ref[...] = lax.dot_general(sc, dc_ref[...].astype(BF16), TN, preferred_element_type=F32)
        gb_ref[...] = jnp.sum(da_ref[...], axis=0, keepdims=True)

    return _call(body, name="ada_bwd",
                 out_shape=[_sds((c_all.shape[1], dmod_cols.shape[1]), F32), _sds((1, dmod_all.shape[1]), F32)],
                 compiler_params=_params())(c_all, dmod_cols, dmod_all)


def _mix_in(x2, mod8, g_mix, w_in, seq, tm=512):
    tokens = x2.shape[0]
    per_seq = seq // tm

    def body(x_ref, m_ref, g_ref, w_ref, h_ref, p_ref):
        xv = x_ref[...]
        r = lax.rsqrt(jnp.mean(xv * xv, axis=-1, keepdims=True) + EPS)
        h = (xv * r * g_ref[...]) * (1.0 + m_ref[1:2, :]) + m_ref[0:1, :]
        hb = h.astype(BF16)
        h_ref[...] = hb
        p_ref[...] = jnp.dot(hb, w_ref[...], preferred_element_type=F32)

    return _call(
        body, name="mix_in", grid=(tokens // tm,),
        in_specs=[pl.BlockSpec((tm, D_MODEL), lambda i: (i, 0)),
                  pl.BlockSpec((None, 8, D_MODEL), lambda i: (i // per_seq, 0, 0)),
                  pl.BlockSpec((1, D_MODEL), lambda i: (0, 0)),
                  pl.BlockSpec((D_MODEL, D_IN), lambda i: (0, 0))],
        out_specs=[pl.BlockSpec((tm, D_MODEL), lambda i: (i, 0)),
                   pl.BlockSpec((tm, D_IN), lambda i: (i, 0))],
        out_shape=[_sds((tokens, D_MODEL), BF16), _sds((tokens, D_IN), F32)],
        compiler_params=_params(("parallel",)),
    )(x2, mod8, g_mix, w_in)


CONV_ROWS = 64
CONV_DW_ROWS = 32
CONV_HALO = 16


def _fill_shifted(xp, sh, seq):
    for b in range(8):
        sh[b, pl.ds(0, seq + 24), :] = xp[pl.ds(b, seq + 24), :]


def _conv_fwd(proj3, w_dw, b_dw):
    n_seq, seq, _ = proj3.shape
    n_cb = D_CONV // LANES

    def body(a_ref, g_ref, w_ref, b_ref, uc_ref, xp, sh):
        zeros = jnp.zeros((CONV_HALO, LANES), F32)
        xp[pl.ds(0, CONV_HALO), :] = zeros
        xp[pl.ds(CONV_HALO + seq, CONV_HALO), :] = zeros
        xp[pl.ds(CONV_HALO, seq), :] = a_ref[...] * _sig(g_ref[...])
        _fill_shifted(xp, sh, seq)

        def blk(i, carry):
            t0 = pl.multiple_of(i * CONV_ROWS, CONV_ROWS)
            acc = jnp.zeros((CONV_ROWS, LANES), F32)
            for j in range(CONV_WIDTH):
                jj = j + 1
                acc = acc + sh[jj % 8, pl.ds(t0 + 8 * (jj // 8), CONV_ROWS), :] * w_ref[j:j + 1, :]
            uc_ref[pl.ds(t0, CONV_ROWS), :] = acc + b_ref[...]
            return carry

        lax.fori_loop(0, seq // CONV_ROWS, blk, 0)

    return _call(
        body, name="conv_fwd", grid=(n_seq, n_cb),
        in_specs=[pl.BlockSpec((None, seq, LANES), lambda b, cb: (b, 0, cb)),
                  pl.BlockSpec((None, seq, LANES), lambda b, cb: (b, 0, n_cb + cb)),
                  pl.BlockSpec((CONV_WIDTH, LANES), lambda b, cb: (0, cb)),
                  pl.BlockSpec((1, LANES), lambda b, cb: (0, cb))],
        out_specs=pl.BlockSpec((None, seq, LANES), lambda b, cb: (b, 0, cb)),
        out_shape=_sds((n_seq, seq, D_CONV), F32),
        scratch_shapes=[pltpu.VMEM((seq + 2 * CONV_HALO, LANES), F32),
                        pltpu.VMEM((8, seq + 2 * CONV_HALO, LANES), F32)],
        compiler_params=_params(("parallel", "parallel")),
    )(proj3, proj3, w_dw, b_dw)


def _conv_bwd(duc3, proj3, w_dw):
    n_seq, seq, _ = proj3.shape
    n_cb = D_CONV // LANES

    def body(duc_ref, a_ref, g_ref, w_ref, da_ref, dg_ref, dw_ref, db_ref, xp, sh):
        @pl.when(pl.program_id(1) == 0)
        def _():
            dw_ref[...] = jnp.zeros_like(dw_ref)
            db_ref[...] = jnp.zeros_like(db_ref)

        zeros = jnp.zeros((CONV_HALO, LANES), F32)
        xp[pl.ds(0, CONV_HALO), :] = zeros
        xp[pl.ds(CONV_HALO + seq, CONV_HALO), :] = zeros
        xp[pl.ds(CONV_HALO, seq), :] = a_ref[...] * _sig(g_ref[...])
        _fill_shifted(xp, sh, seq)
        for j0 in range(0, CONV_WIDTH, 8):
            taps = range(j0, min(j0 + 8, CONV_WIDTH))

            def wblk(i, accs, taps=taps):
                t0 = pl.multiple_of(i * CONV_DW_ROWS, CONV_DW_ROWS)
                d = duc_ref[pl.ds(t0, CONV_DW_ROWS), :]
                return tuple(acc + d * sh[(j + 1) % 8, pl.ds(t0 + 8 * ((j + 1) // 8), CONV_DW_ROWS), :]
                             for acc, j in zip(accs, taps))

            accs = lax.fori_loop(0, seq // CONV_DW_ROWS, wblk,
                                 tuple(jnp.zeros((CONV_DW_ROWS, LANES), F32) for _ in taps))
            for acc, j in zip(accs, taps):
                dw_ref[j:j + 1, :] += jnp.sum(acc, axis=0, keepdims=True)
        db_ref[0:1, :] += jnp.sum(duc_ref[...], axis=0, keepdims=True)
        xp[pl.ds(CONV_HALO, seq), :] = duc_ref[...]
        _fill_shifted(xp, sh, seq)

        def ublk(i, carry):
            t0 = pl.multiple_of(i * CONV_ROWS, CONV_ROWS)
            acc = jnp.zeros((CONV_ROWS, LANES), F32)
            for j in range(CONV_WIDTH):
                jj = CONV_WIDTH - j
                acc = acc + sh[jj % 8, pl.ds(t0 + 8 * (jj // 8), CONV_ROWS), :] * w_ref[j:j + 1, :]
            av = a_ref[pl.ds(t0, CONV_ROWS), :]
            sg = _sig(g_ref[pl.ds(t0, CONV_ROWS), :])
            da_ref[pl.ds(t0, CONV_ROWS), :] = (acc * sg).astype(BF16)
            dg_ref[pl.ds(t0, CONV_ROWS), :] = (acc * av * sg * (1.0 - sg)).astype(BF16)
            return carry

        lax.fori_loop(0, seq // CONV_ROWS, ublk, 0)

    return _call(
        body, name="conv_bwd", grid=(n_cb, n_seq),
        in_specs=[pl.BlockSpec((None, seq, LANES), lambda cb, b: (b, 0, cb)),
                  pl.BlockSpec((None, seq, LANES), lambda cb, b: (b, 0, cb)),
                  pl.BlockSpec((None, seq, LANES), lambda cb, b: (b, 0, n_cb + cb)),
                  pl.BlockSpec((CONV_WIDTH, LANES), lambda cb, b: (0, cb))],
        out_specs=[pl.BlockSpec((None, seq, LANES), lambda cb, b: (b, 0, cb)),
                   pl.BlockSpec((None, seq, LANES), lambda cb, b: (b, 0, cb)),
                   pl.BlockSpec((32, LANES), lambda cb, b: (0, cb)),
                   pl.BlockSpec((8, LANES), lambda cb, b: (0, cb))],
        out_shape=[_sds((n_seq, seq, D_CONV), BF16), _sds((n_seq, seq, D_CONV), BF16),
                   _sds((32, D_CONV), F32), _sds((8, D_CONV), F32)],
        scratch_shapes=[pltpu.VMEM((seq + 2 * CONV_HALO, LANES), F32),
                        pltpu.VMEM((8, seq + 2 * CONV_HALO, LANES), F32)],
        compiler_params=_params(("parallel", "arbitrary")),
    )(duc3, proj3, proj3, w_dw)


MASKED = 1e30
ATT_ROWS = 512
ATT_UNROLL = 4


def _distance_mats(dil, seg_len):
    kw = min(2 * Q_BLOCK, seg_len)
    offsets = (0, -RADIUS, -2 * RADIUS) if kw == 2 * Q_BLOCK else (0,)
    a = np.arange(Q_BLOCK)[:, None]
    b = np.arange(kw)[None, :]
    mats = []
    for off in offsets:
        rel = np.abs(b + off - a)
        mats.append(np.where(rel <= RADIUS, dil * rel, MASKED))
    return jnp.asarray(np.stack(mats).astype(np.float32))


def _alibi_rows():
    s = np.zeros((4, 8, LANES), np.float32)
    for hp in range(4):
        for hl in range(2):
            s[hp, hl, :] = 2.0 ** (-(2 * hp + hl + 1))
    return jnp.asarray(s)


def _window(n, seg_len):
    i0 = pl.multiple_of(n * Q_BLOCK, Q_BLOCK)
    if seg_len <= Q_BLOCK:
        return i0, i0, 0
    per_seg = seg_len // Q_BLOCK
    j = n % per_seg
    seg0 = (n // per_seg) * seg_len
    ks_local = jnp.clip(j * Q_BLOCK - RADIUS, 0, seg_len - 2 * Q_BLOCK)
    ks = pl.multiple_of(seg0 + ks_local, RADIUS)
    var = jnp.where(j == 0, 0, jnp.where(j == per_seg - 1, 2, 1))
    return i0, ks, var


def _first_head(rows):
    return lax.broadcasted_iota(jnp.int32, (rows, LANES), 1) < HEAD_DIM


def _same_head():
    head = np.arange(LANES) // HEAD_DIM
    return jnp.asarray((head[:, None] == head[None, :]).astype(np.float32)).astype(BF16)


def _head_sum(x, same_ref):
    hi = x.astype(BF16)
    lo = (x - hi.astype(F32)).astype(BF16)
    return (jnp.dot(hi, same_ref[...], preferred_element_type=F32)
            + jnp.dot(lo, same_ref[...], preferred_element_type=F32))


def _head_mean(x, same_ref):
    return _head_sum(x, same_ref) * (1.0 / HEAD_DIM)


def _per_head(x, first):
    swapped = pltpu.roll(x, HEAD_DIM, 1)
    return jnp.where(first, x, swapped), jnp.where(first, swapped, x)


def _permute_rows(dst, src, dil, seq):
    seg = seq // dil
    for r in range(dil):
        rows = pl.ds(r, seg, stride=dil) if dil > 1 else pl.ds(0, seq)
        dst[pl.ds(r * seg, seg), :] = src[rows, :].astype(dst.dtype)


def _permute_rows_by_head(dst, src, dil, seq):
    seg = seq // dil
    first = _first_head(seg)
    for r in range(dil):
        rows = pl.ds(r, seg, stride=dil) if dil > 1 else pl.ds(0, seq)
        val = src[rows, :]
        dst[0, pl.ds(r * seg, seg), :] = jnp.where(first, val, 0.0).astype(dst.dtype)
        dst[1, pl.ds(r * seg, seg), :] = jnp.where(first, 0.0, val).astype(dst.dtype)


def _qk_normalise(q_ref, g2_ref, same_ref, dst, seq, scale):
    def chunk(ci, carry):
        rows = pl.ds(pl.multiple_of(ci * ATT_ROWS, ATT_ROWS), ATT_ROWS)
        qv = q_ref[rows, :]
        r = lax.rsqrt(_head_mean(qv * qv, same_ref) + EPS)
        dst[rows, :] = qv * r * (g2_ref[...] * scale)
        return carry

    lax.fori_loop(0, seq // ATT_ROWS, chunk, 0)


def _attn_fwd(proj3, g_q2, g_k2, ride):
    n_seq, seq, _ = proj3.shape
    dms = [_distance_mats(d, seq // d) for d in DILATIONS]
    same = _same_head()
    col0 = 2 * D_CONV // LANES
    n_hp = D_ATT // LANES

    n_ride = len(ride)
    ride_scatter = [s for _, s in ride]

    def body(*refs):
        q_ref, k_ref, v_ref, gq_ref, gk_ref, sl_ref, dm1, dm4, dm16, same_ref = refs[:10]
        ride_in = refs[10:10 + n_ride]
        y_ref, lse_ref = refs[10 + n_ride:12 + n_ride]
        ride_out = refs[12 + n_ride:12 + 2 * n_ride]
        (qf, kf, qp, kp, vp, oml_p, o1, o4, o16, m1, m4, m16, l1, l4, l16) = refs[12 + 2 * n_ride:27 + 2 * n_ride]
        o_nat, m_nat, l_nat = (o1, o4, o16), (m1, m4, m16), (l1, l4, l16)
        ride_args = (ride_scatter, ride_in, ride_out) + tuple(refs[27 + 2 * n_ride:])
        step = pl.program_id(0) * n_hp + pl.program_id(1)

        @pl.when(step == 0)
        def _():
            _exchange_start(*ride_args)

        dm_refs = (dm1, dm4, dm16)
        _qk_normalise(q_ref, gq_ref, same_ref, qf, seq, HEAD_DIM ** -0.5)
        _qk_normalise(k_ref, gk_ref, same_ref, kf, seq, 1.0)
        slopes = (sl_ref[0:1, 0:1], sl_ref[1:2, 0:1])
        for pi, dil in enumerate(DILATIONS):
            seg = seq // dil
            kw = min(2 * Q_BLOCK, seg)
            _permute_rows_by_head(qp, qf, dil, seq)
            _permute_rows(kp, kf, dil, seq)
            _permute_rows(vp, v_ref, dil, seq)

            def blk(it, carry, seg=seg, kw=kw, pi=pi, dst=oml_p):
                first = _first_head(Q_BLOCK)
                chains = [(sub, h) for sub in range(ATT_UNROLL) for h in range(2)]
                win = [_window(it * ATT_UNROLL + sub, seg) for sub in range(ATT_UNROLL)]
                s = {}
                for sub, h in chains:
                    i0, ks, var = win[sub]
                    s[sub, h] = lax.dot_general(qp[h, pl.ds(i0, Q_BLOCK), :], kp[pl.ds(ks, kw), :], NT,
                                                preferred_element_type=F32) - slopes[h] * dm_refs[pi][var]
                m, l, p = {}, {}, {}
                for c in chains:
                    m[c] = jnp.max(s[c], axis=1, keepdims=True)
                    e = jnp.exp(s[c] - m[c])
                    l[c] = jnp.sum(e, axis=1, keepdims=True)
                    p[c] = e.astype(BF16)
                o = {}
                for sub, h in chains:
                    o[sub, h] = jnp.dot(p[sub, h], vp[pl.ds(win[sub][1], kw), :], preferred_element_type=F32)
                packed = [jnp.concatenate([jnp.where(first, t[sub, 0], t[sub, 1]) for t in (o, m, l)], axis=1)
                          for sub in range(ATT_UNROLL)]
                rows = pl.ds(pl.multiple_of(it * (ATT_UNROLL * Q_BLOCK), ATT_UNROLL * Q_BLOCK), ATT_UNROLL * Q_BLOCK)
                dst[rows, :] = jnp.concatenate(packed, axis=0)
                return carry

            lax.fori_loop(0, seq // (Q_BLOCK * ATT_UNROLL), blk, 0)
            for r in range(dil):
                nat = pl.ds(r, seg, stride=dil) if dil > 1 else pl.ds(0, seq)
                perm = pl.ds(r * seg, seg)
                o_nat[pi][nat, :] = oml_p[perm, pl.ds(0, LANES)]
                m_nat[pi][nat, :] = oml_p[perm, pl.ds(LANES, LANES)]
                l_nat[pi][nat, :] = oml_p[perm, pl.ds(2 * LANES, LANES)]

        def merge(ci, carry):
            rows = pl.ds(pl.multiple_of(ci * ATT_ROWS, ATT_ROWS), ATT_ROWS)
            ms = [m_nat[pi][rows, :] for pi in range(3)]
            m_all = jnp.maximum(jnp.maximum(ms[0], ms[1]), ms[2])
            es = [jnp.exp(m - m_all) for m in ms]
            l_all = sum(l_nat[pi][rows, :] * es[pi] for pi in range(3))
            inv = 1.0 / l_all
            o = sum(o_nat[pi][rows, :] * (es[pi] * inv) for pi in range(3))
            y_ref[rows, :] = o.astype(BF16)
            lse_ref[rows, :] = m_all + jnp.log(l_all)
            return carry

        lax.fori_loop(0, seq // ATT_ROWS, merge, 0)

        @pl.when(step == n_seq * n_hp - 1)
        def _():
            _exchange_wait(*ride_args)

    def col(off):
        return pl.BlockSpec((None, seq, LANES), lambda b, hp: (b, 0, col0 + off * n_hp + hp))

    def whole(arr):
        return pl.BlockSpec(arr.shape, lambda b, hp: (0,) * arr.ndim)

    rows_f32 = pltpu.VMEM((seq, LANES), F32)
    rows_bf16 = pltpu.VMEM((seq, LANES), BF16)
    return _call(
        body, name="attn_fwd", grid=(n_seq, n_hp),
        in_specs=[col(0), col(1), col(2), whole(g_q2), whole(g_k2),
                  pl.BlockSpec((None, 8, LANES), lambda b, hp: (hp, 0, 0)),
                  whole(dms[0]), whole(dms[1]), whole(dms[2]), whole(same)] + [ANY_SPEC] * n_ride,
        out_specs=[pl.BlockSpec((None, seq, LANES), lambda b, hp: (b, 0, hp)),
                   pl.BlockSpec((None, seq, LANES), lambda b, hp: (b, 0, hp))] + [ANY_SPEC] * n_ride,
        out_shape=[_sds((n_seq, seq, D_ATT), BF16), _sds((n_seq, seq, D_ATT), F32)] + _exchange_shapes(ride),
        scratch_shapes=[rows_f32, rows_f32, pltpu.VMEM((2, seq, LANES), BF16), rows_bf16, rows_bf16]
        + [pltpu.VMEM((seq, 3 * LANES), F32)] + [rows_f32] * 9 + _exchange_sems(n_ride),
        compiler_params=_params(("arbitrary", "arbitrary")),
    )(proj3, proj3, proj3, g_q2, g_k2, _alibi_rows(), *dms, same, *[a for a, _ in ride])


def _attn_bwd(proj3, do3, y_att3, lse3, g_q2, g_k2, ride):
    n_seq, seq, _ = proj3.shape
    dms = [_distance_mats(d, seq // d) for d in DILATIONS]
    same = _same_head()
    col0 = 2 * D_CONV // LANES
    n_hp = D_ATT // LANES

    n_ride = len(ride)
    ride_scatter = [s for _, s in ride]

    def body(*refs):
        (q_ref, k_ref, v_ref, do_ref, o_ref, lse_ref, gq_ref, gk_ref, sl_ref, dm1, dm4, dm16,
         same_ref) = refs[:13]
        ride_in = refs[13:13 + n_ride]
        dq_ref, dk_ref, dv_ref, dg_ref = refs[13 + n_ride:17 + n_ride]
        ride_out = refs[17 + n_ride:17 + 2 * n_ride]
        (qf, kf, qp, dop, kp, vp, sn, sp, dqp, dkp, dvp, dqn, dkn, dvn) = refs[17 + 2 * n_ride:31 + 2 * n_ride]
        ride_args = (ride_scatter, ride_in, ride_out) + tuple(refs[31 + 2 * n_ride:])
        dm_refs = (dm1, dm4, dm16)
        step = pl.program_id(0) * n_hp + pl.program_id(1)

        @pl.when(step == 0)
        def _():
            _exchange_start(*ride_args)
            dg_ref[...] = jnp.zeros_like(dg_ref)

        _qk_normalise(q_ref, gq_ref, same_ref, qf, seq, HEAD_DIM ** -0.5)
        _qk_normalise(k_ref, gk_ref, same_ref, kf, seq, 1.0)

        def stats(ci, carry):
            rows = pl.ds(pl.multiple_of(ci * ATT_ROWS, ATT_ROWS), ATT_ROWS)
            first = _first_head(ATT_ROWS)
            sn[0, rows, :], sn[1, rows, :] = _per_head(lse_ref[rows, :], first)
            prod = do_ref[rows, :] * o_ref[rows, :].astype(F32)
            sn[2, rows, :], sn[3, rows, :] = _per_head(_head_sum(prod, same_ref), first)
            return carry

        lax.fori_loop(0, seq // ATT_ROWS, stats, 0)
        slopes = (sl_ref[0:1, 0:1], sl_ref[1:2, 0:1])
        half = seq // (Q_BLOCK * ATT_UNROLL)
        region = seq // ATT_UNROLL

        for pi, dil in enumerate(DILATIONS):
            seg = seq // dil
            kw = min(2 * Q_BLOCK, seg)
            _permute_rows_by_head(qp, qf, dil, seq)
            _permute_rows_by_head(dop, do_ref, dil, seq)
            _permute_rows(kp, kf, dil, seq)
            _permute_rows(vp, v_ref, dil, seq)
            if dil == 1:
                st = sn
            else:
                st = sp
                for n in range(4):
                    _permute_rows(sp.at[n], sn.at[n], dil, seq)
            for sub in range(ATT_UNROLL):
                lo, hi = max(sub * region - RADIUS, 0), min((sub + 1) * region + RADIUS, seq)
                dkp[sub, pl.ds(lo, hi - lo), :] = jnp.zeros((hi - lo, LANES), F32)
                dvp[sub, pl.ds(lo, hi - lo), :] = jnp.zeros((hi - lo, LANES), F32)

            def blk(it, carry, seg=seg, kw=kw, pi=pi, st=st):
                first = _first_head(Q_BLOCK)
                chains = [(sub, h) for sub in range(ATT_UNROLL) for h in range(2)]
                win = [_window(it + sub * half, seg) for sub in range(ATT_UNROLL)]
                qrows = [pl.ds(w[0], Q_BLOCK) for w in win]
                krows = [pl.ds(w[1], kw) for w in win]

                def over_keys(n, sub):
                    t = st[n, qrows[sub], :]
                    return t if kw == LANES else jnp.concatenate([t] * (kw // LANES), axis=1)

                s, dp = {}, {}
                for sub, h in chains:
                    s[sub, h] = lax.dot_general(qp[h, qrows[sub], :], kp[krows[sub], :], NT,
                                                preferred_element_type=F32) - slopes[h] * dm_refs[pi][win[sub][2]]
                    dp[sub, h] = lax.dot_general(dop[h, qrows[sub], :], vp[krows[sub], :], NT,
                                                 preferred_element_type=F32)
                p, ds = {}, {}
                for sub, h in chains:
                    e = jnp.exp(s[sub, h] - over_keys(h, sub))
                    ds[sub, h] = (e * (dp[sub, h] - over_keys(2 + h, sub))).astype(BF16)
                    p[sub, h] = e.astype(BF16)
                dq, dk, dv = {}, {}, {}
                for sub, h in chains:
                    dq[sub, h] = jnp.dot(ds[sub, h], kp[krows[sub], :], preferred_element_type=F32)
                    dk[sub, h] = lax.dot_general(ds[sub, h], qp[h, qrows[sub], :], TN, preferred_element_type=F32)
                    dv[sub, h] = lax.dot_general(p[sub, h], dop[h, qrows[sub], :], TN, preferred_element_type=F32)
                for sub in range(ATT_UNROLL):
                    dqp[qrows[sub], :] = jnp.where(first, dq[sub, 0], dq[sub, 1])
                    dkp[sub, krows[sub], :] += dk[sub, 0] + dk[sub, 1]
                    dvp[sub, krows[sub], :] += dv[sub, 0] + dv[sub, 1]
                return carry

            lax.fori_loop(0, half, blk, 0)
            if dil == 1:
                for sub in range(ATT_UNROLL):
                    r0 = sub * region
                    pieces = [(r0, RADIUS, [sub - 1, sub] if sub > 0 else [sub]),
                              (r0 + RADIUS, region - 2 * RADIUS, [sub]),
                              (r0 + region - RADIUS, RADIUS, [sub, sub + 1] if sub < ATT_UNROLL - 1 else [sub])]
                    for start, size, owners in pieces:
                        rows = pl.ds(start, size)
                        dqn[rows, :] = dqp[rows, :]
                        dkn[rows, :] = sum(dkp[o, rows, :] for o in owners)
                        dvn[rows, :] = sum(dvp[o, rows, :] for o in owners)
            else:
                for r in range(dil):
                    nat, perm = pl.ds(r, seg, stride=dil), pl.ds(r * seg, seg)
                    owner = (r * seg) // region
                    dqn[nat, :] += dqp[perm, :]
                    dkn[nat, :] += dkp[owner, perm, :]
                    dvn[nat, :] += dvp[owner, perm, :]

        def finish(ci, carry):
            rows = pl.ds(pl.multiple_of(ci * ATT_ROWS, ATT_ROWS), ATT_ROWS)
            for src_ref, g_ref, dn, dst_ref, scale, row in (
                    (q_ref, gq_ref, dqn, dq_ref, HEAD_DIM ** -0.5, 0), (k_ref, gk_ref, dkn, dk_ref, 1.0, 1)):
                xv = src_ref[rows, :]
                r = lax.rsqrt(_head_mean(xv * xv, same_ref) + EPS)
                xhat = xv * r
                d = dn[rows, :] * scale
                dg_ref[row:row + 1, :] += jnp.sum(d * xhat, axis=0, keepdims=True)
                dxh = d * g_ref[...]
                dst_ref[rows, :] = (r * (dxh - xhat * _head_mean(dxh * xhat, same_ref))).astype(BF16)
            dv_ref[rows, :] = dvn[rows, :].astype(BF16)
            return carry

        lax.fori_loop(0, seq // ATT_ROWS, finish, 0)

        @pl.when(step == n_seq * n_hp - 1)
        def _():
            _exchange_wait(*ride_args)

    def col(off):
        return pl.BlockSpec((None, seq, LANES), lambda b, hp: (b, 0, col0 + off * n_hp + hp))

    def whole(arr):
        return pl.BlockSpec(arr.shape, lambda b, hp: (0,) * arr.ndim)

    att = pl.BlockSpec((None, seq, LANES), lambda b, hp: (b, 0, hp))
    rows_f32 = pltpu.VMEM((seq, LANES), F32)
    rows_bf16 = pltpu.VMEM((seq, LANES), BF16)
    by_head_bf16 = pltpu.VMEM((2, seq, LANES), BF16)
    per_sub_f32 = pltpu.VMEM((ATT_UNROLL, seq, LANES), F32)
    stats_f32 = pltpu.VMEM((4, seq, LANES), F32)
    return _call(
        body, name="attn_bwd", grid=(n_seq, n_hp),
        in_specs=[col(0), col(1), col(2), att, att, att, whole(g_q2), whole(g_k2),
                  pl.BlockSpec((None, 8, LANES), lambda b, hp: (hp, 0, 0)),
                  whole(dms[0]), whole(dms[1]), whole(dms[2]), whole(same)] + [ANY_SPEC] * n_ride,
        out_specs=[att, att, att, pl.BlockSpec((8, LANES), lambda b, hp: (0, 0))] + [ANY_SPEC] * n_ride,
        out_shape=[_sds((n_seq, seq, D_ATT), BF16)] * 3 + [_sds((8, LANES), F32)] + _exchange_shapes(ride),
        scratch_shapes=[rows_f32, rows_f32, by_head_bf16, by_head_bf16, rows_bf16, rows_bf16, stats_f32, stats_f32,
                        rows_f32, per_sub_f32, per_sub_f32, rows_f32, rows_f32, rows_f32] + _exchange_sems(n_ride),
        compiler_params=_params(("arbitrary", "arbitrary")),
    )(proj3, proj3, proj3, do3, y_att3, lse3, g_q2, g_k2, _alibi_rows(), *dms, same, *[a for a, _ in ride])


def _mix_out(uc2, y_att2, x2, mod8, g_ln, b_ln, g_ffn, w_out, seq, tm=512):
    tokens = x2.shape[0]
    per_seq = seq // tm

    def body(uc_ref, ya_ref, x_ref, m_ref, gl_ref, bl_ref, gf_ref, w_ref, yc_ref, mix_ref, x1_ref, h2_ref):
        uc = uc_ref[...]
        mu = jnp.mean(uc, axis=-1, keepdims=True)
        cen = uc - mu
        rs = lax.rsqrt(jnp.mean(cen * cen, axis=-1, keepdims=True) + EPS)
        z = cen * rs * gl_ref[...] + bl_ref[...]
        yc = (z * _sig(z)).astype(BF16)
        yc_ref[...] = yc
        mix = (jnp.dot(yc, w_ref[pl.ds(0, D_CONV), :], preferred_element_type=F32)
               + jnp.dot(ya_ref[...], w_ref[pl.ds(D_CONV, D_ATT), :], preferred_element_type=F32))
        mix_ref[...] = mix
        x1 = x_ref[...] + m_ref[2:3, :] * mix
        x1_ref[...] = x1
        r = lax.rsqrt(jnp.mean(x1 * x1, axis=-1, keepdims=True) + EPS)
        h2_ref[...] = ((x1 * r * gf_ref[...]) * (1.0 + m_ref[4:5, :]) + m_ref[3:4, :]).astype(BF16)

    def rows(width):
        return pl.BlockSpec((tm, width), lambda i: (i, 0))

    def vec(width):
        return pl.BlockSpec((1, width), lambda i: (0, 0))

    return _call(
        body, name="mix_out", grid=(tokens // tm,),
        in_specs=[rows(D_CONV), rows(D_ATT), rows(D_MODEL),
                  pl.BlockSpec((None, 8, D_MODEL), lambda i: (i // per_seq, 0, 0)),
                  vec(D_CONV), vec(D_CONV), vec(D_MODEL),
                  pl.BlockSpec((D_MODEL, D_MODEL), lambda i: (0, 0))],
        out_specs=[rows(D_CONV), rows(D_MODEL), rows(D_MODEL), rows(D_MODEL)],
        out_shape=[_sds((tokens, D_CONV), BF16), _sds((tokens, D_MODEL), F32),
                   _sds((tokens, D_MODEL), F32), _sds((tokens, D_MODEL), BF16)],
        compiler_params=_params(("parallel",)),
    )(uc2, y_att2, x2, mod8, g_ln, b_ln, g_ffn, w_out)


def _mix_out_bwd(dmix, uc2, g_ln, b_ln, w_out, tm=512):
    tokens = dmix.shape[0]

    def body(dm_ref, uc_ref, gl_ref, bl_ref, w_ref, duc_ref, do_ref, dgb_ref):
        @pl.when(pl.program_id(0) == 0)
        def _():
            dgb_ref[...] = jnp.zeros_like(dgb_ref)

        dmv = dm_ref[...]
        dyc = lax.dot_general(dmv, w_ref[pl.ds(0, D_CONV), :], NT, preferred_element_type=F32)
        do_ref[...] = lax.dot_general(dmv, w_ref[pl.ds(D_CONV, D_ATT), :], NT, preferred_element_type=F32)
        uc = uc_ref[...]
        mu = jnp.mean(uc, axis=-1, keepdims=True)
        cen = uc - mu
        rs = lax.rsqrt(jnp.mean(cen * cen, axis=-1, keepdims=True) + EPS)
        xh = cen * rs
        z = xh * gl_ref[...] + bl_ref[...]
        sg = _sig(z)
        dz = dyc * (sg * (1.0 + z * (1.0 - sg)))
        dgb_ref[0:1, :] += jnp.sum(dz * xh, axis=0, keepdims=True)
        dgb_ref[1:2, :] += jnp.sum(dz, axis=0, keepdims=True)
        dxh = dz * gl_ref[...]
        duc_ref[...] = rs * (dxh - jnp.mean(dxh, axis=-1, keepdims=True)
                             - xh * jnp.mean(dxh * xh, axis=-1, keepdims=True))

    return _call(
        body, name="mix_out_bwd", grid=(tokens // tm,),
        in_specs=[pl.BlockSpec((tm, D_MODEL), lambda i: (i, 0)),
                  pl.BlockSpec((tm, D_CONV), lambda i: (i, 0)),
                  pl.BlockSpec((1, D_CONV), lambda i: (0, 0)),
                  pl.BlockSpec((1, D_CONV), lambda i: (0, 0)),
                  pl.BlockSpec((D_MODEL, D_MODEL), lambda i: (0, 0))],
        out_specs=[pl.BlockSpec((tm, D_CONV), lambda i: (i, 0)),
                   pl.BlockSpec((tm, D_ATT), lambda i: (i, 0)),
                   pl.BlockSpec((8, D_CONV), lambda i: (0, 0))],
        out_shape=[_sds((tokens, D_CONV), F32), _sds((tokens, D_ATT), F32), _sds((8, D_CONV), F32)],
        compiler_params=_params(("arbitrary",)),
    )(dmix, uc2, g_ln, b_ln, w_out)


FF_TILE = 256
FF_ROWS = 256


def _ffn_fwd(h2, w_gate, w_up, w_down, x1, target, mod8, seq, tm=1024):
    tokens = h2.shape[0]
    per_seq = seq // tm
    n_seq = tokens // seq
    last = D_FF // FF_TILE - 1

    def body(h_ref, wg_ref, wu_ref, wd_ref, x1_ref, t_ref, m_ref, gate_ref, up_ref, dy_ref, df_ref, sq_ref, dgf_ref,
             f_ref):
        i, j = pl.program_id(0), pl.program_id(1)

        @pl.when(j == 0)
        def _():
            f_ref[...] = jnp.zeros_like(f_ref)

        @pl.when((j == 0) & (i == 0))
        def _():
            sq_ref[...] = jnp.zeros_like(sq_ref)

        @pl.when((j == 0) & (i % per_seq == 0))
        def _():
            dgf_ref[...] = jnp.zeros_like(dgf_ref)

        def gate_up(r):
            hv = h_ref[pl.ds(r * FF_ROWS, FF_ROWS), :]
            return (jnp.dot(hv, wg_ref[...], preferred_element_type=F32),
                    jnp.dot(hv, wu_ref[...], preferred_element_type=F32))

        ahead = gate_up(0)
        for r in range(tm // FF_ROWS):
            gate, up = ahead
            if r + 1 < tm // FF_ROWS:
                ahead = gate_up(r + 1)
            rows = pl.ds(r * FF_ROWS, FF_ROWS)
            gate_ref[rows, :] = gate
            up_ref[rows, :] = up
            act = (gate * _sig(gate) * up).astype(BF16)
            f_ref[rows, :] += jnp.dot(act, wd_ref[...], preferred_element_type=F32)

        @pl.when(j == last)
        def _():
            gate_f = m_ref[5:6, :]
            for r in range(tm // FF_ROWS):
                rows = pl.ds(r * FF_ROWS, FF_ROWS)
                fv = f_ref[rows, :]
                diff = x1_ref[rows, :] + gate_f * fv - t_ref[rows, :]
                sq_ref[0:1, :] += jnp.sum(diff * diff, axis=0, keepdims=True)
                dy = diff * (1.0 / D_MODEL)
                dy_ref[rows, :] = dy
                df_ref[rows, :] = (gate_f * dy).astype(BF16)
                dgf_ref[0:1, :] += jnp.sum(dy * fv, axis=0, keepdims=True)

    rows_spec = pl.BlockSpec((tm, D_MODEL), lambda i, j: (i, 0))
    per = pl.BlockSpec((None, 8, D_MODEL), lambda i, j: (i // per_seq, 0, 0))
    tile = pl.BlockSpec((tm, FF_TILE), lambda i, j: (i, j))
    return _call(
        body, name="ffn_fwd", grid=(tokens // tm, D_FF // FF_TILE),
        in_specs=[rows_spec,
                  pl.BlockSpec((D_MODEL, FF_TILE), lambda i, j: (0, j)),
                  pl.BlockSpec((D_MODEL, FF_TILE), lambda i, j: (0, j)),
                  pl.BlockSpec((FF_TILE, D_MODEL), lambda i, j: (j, 0)),
                  rows_spec, rows_spec, per],
        out_specs=[tile, tile, rows_spec, rows_spec, pl.BlockSpec((8, D_MODEL), lambda i, j: (0, 0)), per],
        out_shape=[_sds((tokens, D_FF), F32), _sds((tokens, D_FF), F32), _sds((tokens, D_MODEL), F32),
                   _sds((tokens, D_MODEL), BF16), _sds((8, D_MODEL), F32), _sds((n_seq, 8, D_MODEL), F32)],
        scratch_shapes=[pltpu.VMEM((tm, D_MODEL), F32)],
        compiler_params=_params(("arbitrary", "arbitrary")),
    )(h2, w_gate, w_up, w_down, x1, target, mod8)


def _ffn_bwd(df, gate, up, w_gate, w_up, w_down, x1, dy, mix, mod8, g_ffn, seq, tm=1024):
    tokens = df.shape[0]
    per_seq = seq // tm
    n_seq = tokens // seq
    last = D_FF // FF_TILE - 1

    def body(df_ref, gate_ref, up_ref, wg_ref, wu_ref, wd_ref, m_ref, g_ref, x1_hbm, dy_hbm, mix_hbm,
             dgate_ref, dup_ref, act_ref, dx1_ref, dmix_ref, dg_ref, dm_ref, dh_ref, late, late_sems):
        i, j = pl.program_id(0), pl.program_id(1)
        my_rows = pl.ds(pl.multiple_of(i * tm, tm), tm)
        fetches = [pltpu.make_async_copy(src.at[my_rows, :], late.at[n], late_sems.at[n])
                   for n, src in enumerate((x1_hbm, dy_hbm, mix_hbm))]

        @pl.when(j == 0)
        def _():
            dh_ref[...] = jnp.zeros_like(dh_ref)
            for cp in fetches:
                cp.start()

        @pl.when((j == 0) & (i == 0))
        def _():
            dg_ref[...] = jnp.zeros_like(dg_ref)

        @pl.when((j == 0) & (i % per_seq == 0))
        def _():
            dm_ref[...] = jnp.zeros_like(dm_ref)

        def d_act(r):
            return lax.dot_general(df_ref[pl.ds(r * FF_ROWS, FF_ROWS), :], wd_ref[...], NT,
                                   preferred_element_type=F32)

        ahead = d_act(0)
        for r in range(tm // FF_ROWS):
            dact = ahead
            if r + 1 < tm // FF_ROWS:
                ahead = d_act(r + 1)
            rows = pl.ds(r * FF_ROWS, FF_ROWS)
            gate = gate_ref[rows, :]
            up = up_ref[rows, :]
            sg = _sig(gate)
            silu = gate * sg
            act_ref[rows, :] = (silu * up).astype(BF16)
            dup = (dact * silu).astype(BF16)
            dgate = (dact * up * (sg * (1.0 + gate * (1.0 - sg)))).astype(BF16)
            dup_ref[rows, :] = dup
            dgate_ref[rows, :] = dgate
            dh_ref[rows, :] += (lax.dot_general(dgate, wg_ref[...], NT, preferred_element_type=F32)
                                + lax.dot_general(dup, wu_ref[...], NT, preferred_element_type=F32))

        @pl.when(j == last)
        def _():
            for cp in fetches:
                cp.wait()
            g = g_ref[...]
            for r in range(tm // FF_ROWS):
                rows = pl.ds(r * FF_ROWS, FF_ROWS)
                dh = dh_ref[rows, :]
                x1v = late[0, rows, :]
                rs = lax.rsqrt(jnp.mean(x1v * x1v, axis=-1, keepdims=True) + EPS)
                xhat = x1v * rs
                dm_ref[0:1, :] += jnp.sum(dh, axis=0, keepdims=True)
                dm_ref[1:2, :] += jnp.sum(dh * (xhat * g), axis=0, keepdims=True)
                dn = dh * (1.0 + m_ref[4:5, :])
                dg_ref[0:1, :] += jnp.sum(dn * xhat, axis=0, keepdims=True)
                dxh = dn * g
                dx1 = late[1, rows, :] + rs * (dxh - xhat * jnp.mean(dxh * xhat, axis=-1, keepdims=True))
                dx1_ref[rows, :] = dx1
                dm_ref[2:3, :] += jnp.sum(dx1 * late[2, rows, :], axis=0, keepdims=True)
                dmix_ref[rows, :] = (m_ref[2:3, :] * dx1).astype(BF16)

    tile = pl.BlockSpec((tm, FF_TILE), lambda i, j: (i, j))
    rows_spec = pl.BlockSpec((tm, D_MODEL), lambda i, j: (i, 0))
    per = pl.BlockSpec((None, 8, D_MODEL), lambda i, j: (i // per_seq, 0, 0))
    return _call(
        body, name="ffn_bwd", grid=(tokens // tm, D_FF // FF_TILE),
        in_specs=[rows_spec, tile, tile,
                  pl.BlockSpec((D_MODEL, FF_TILE), lambda i, j: (0, j)),
                  pl.BlockSpec((D_MODEL, FF_TILE), lambda i, j: (0, j)),
                  pl.BlockSpec((FF_TILE, D_MODEL), lambda i, j: (j, 0)),
                  per, pl.BlockSpec((1, D_MODEL), lambda i, j: (0, 0)), ANY_SPEC, ANY_SPEC, ANY_SPEC],
        out_specs=[tile, tile, tile, rows_spec, rows_spec, pl.BlockSpec((8, D_MODEL), lambda i, j: (0, 0)), per],
        out_shape=[_sds((tokens, D_FF), BF16)] * 3 + [_sds((tokens, D_MODEL), F32), _sds((tokens, D_MODEL), BF16),
                                                    _sds((8, D_MODEL), F32), _sds((n_seq, 8, D_MODEL), F32)],
        scratch_shapes=[pltpu.VMEM((tm, D_MODEL), F32), pltpu.VMEM((3, tm, D_MODEL), F32),
                        pltpu.SemaphoreType.DMA((3,))],
        compiler_params=_params(("arbitrary", "arbitrary")),
    )(df, gate, up, w_gate, w_up, w_down, mod8, g_ffn, x1, dy, mix)


def _mix_in_bwd(d_a, d_g, d_q, d_k, d_v, w_in, x2, dx1, mod8, g_mix, seq, ride, tm=512):
    tokens = x2.shape[0]
    per_seq = seq // tm
    n_seq = tokens // seq
    parts = (d_a, d_g, d_q, d_k, d_v)
    width = D_CONV
    n_ride = len(ride)
    ride_scatter = [s for _, s in ride]

    def body(*refs):
        da_ref, dg_ref, dq_ref, dk_ref, dv_ref, w_ref, x_ref, dx1_ref, m_ref, g_ref = refs[:10]
        ride_in = refs[10:10 + n_ride]
        gx_ref, dgm_ref, dm_ref = refs[10 + n_ride:13 + n_ride]
        ride_args = (ride_scatter, ride_in, refs[13 + n_ride:13 + 2 * n_ride]) + tuple(refs[13 + 2 * n_ride:])
        i = pl.program_id(0)

        @pl.when(i == 0)
        def _():
            _exchange_start(*ride_args)
            dgm_ref[...] = jnp.zeros_like(dgm_ref)

        @pl.when(i % per_seq == 0)
        def _():
            dm_ref[...] = jnp.zeros_like(dm_ref)

        dh = jnp.zeros((tm, D_MODEL), F32)
        for n, ref in enumerate((da_ref, dg_ref, dq_ref, dk_ref, dv_ref)):
            dh = dh + lax.dot_general(ref[...], w_ref[:, pl.ds(n * width, width)], NT, preferred_element_type=F32)
        xv = x_ref[...]
        r = lax.rsqrt(jnp.mean(xv * xv, axis=-1, keepdims=True) + EPS)
        xhat = xv * r
        g = g_ref[...]
        dm_ref[0:1, :] += jnp.sum(dh, axis=0, keepdims=True)
        dm_ref[1:2, :] += jnp.sum(dh * (xhat * g), axis=0, keepdims=True)
        dn = dh * (1.0 + m_ref[1:2, :])
        dgm_ref[0:1, :] += jnp.sum(dn * xhat, axis=0, keepdims=True)
        dxh = dn * g
        gx_ref[...] = dx1_ref[...] + r * (dxh - xhat * jnp.mean(dxh * xhat, axis=-1, keepdims=True))

        @pl.when(i == tokens // tm - 1)
        def _():
            _exchange_wait(*ride_args)

    rows = pl.BlockSpec((tm, D_MODEL), lambda i: (i, 0))
    half = pl.BlockSpec((tm, width), lambda i: (i, 0))
    per = pl.BlockSpec((None, 8, D_MODEL), lambda i: (i // per_seq, 0, 0))
    return _call(
        body, name="mix_in_bwd", grid=(tokens // tm,),
        in_specs=[half] * 5 + [pl.BlockSpec((D_MODEL, D_IN), lambda i: (0, 0)), rows, rows, per,
                               pl.BlockSpec((1, D_MODEL), lambda i: (0, 0))] + [ANY_SPEC] * n_ride,
        out_specs=[rows, pl.BlockSpec((8, D_MODEL), lambda i: (0, 0)), per] + [ANY_SPEC] * n_ride,
        out_shape=[_sds((tokens, D_MODEL), F32), _sds((8, D_MODEL), F32), _sds((n_seq, 8, D_MODEL), F32)]
        + _exchange_shapes(ride),
        scratch_shapes=_exchange_sems(n_ride),
        compiler_params=_params(("arbitrary",)),
    )(*parts, w_in, x2, dx1, mod8, g_mix, *[a for a, _ in ride])


def _grad_matmul_parts(a_parts, b_parts, name, tk=1024):
    tokens = a_parts[0].shape[0]
    na, nb = len(a_parts), len(b_parts)
    ma, nbw = a_parts[0].shape[1], b_parts[0].shape[1]

    n_k = tokens // tk

    def body(*refs):
        a_refs, b_refs, o_ref, acc = refs[:na], refs[na:na + nb], refs[na + nb], refs[na + nb + 1]

        @pl.when(pl.program_id(0) == 0)
        def _():
            acc[...] = jnp.zeros_like(acc)

        for i in range(na):
            for j in range(nb):
                acc[pl.ds(i * ma, ma), pl.ds(j * nbw, nbw)] += lax.dot_general(
                    a_refs[i][...], b_refs[j][...], TN, preferred_element_type=F32)

        @pl.when(pl.program_id(0) == n_k - 1)
        def _():
            o_ref[...] = acc[...].astype(o_ref.dtype)

    return _call(
        body, name=name, grid=(n_k,),
        in_specs=[pl.BlockSpec((tk, ma), lambda k: (k, 0))] * na + [pl.BlockSpec((tk, nbw), lambda k: (k, 0))] * nb,
        out_specs=pl.BlockSpec((na * ma, nb * nbw), lambda k: (0, 0)),
        out_shape=_sds((na * ma, nb * nbw), BF16),
        scratch_shapes=[pltpu.VMEM((na * ma, nb * nbw), F32)],
        compiler_params=_params(("arbitrary",)),
    )(*a_parts, *b_parts)


def _grad_matmul(a, b, name, tmo, tno, tk=1024):
    tokens, m = a.shape
    n = b.shape[1]
    n_k = tokens // tk

    def body(a_ref, b_ref, o_ref, acc):
        @pl.when(pl.program_id(2) == 0)
        def _():
            acc[...] = jnp.zeros_like(acc)

        acc[...] += lax.dot_general(a_ref[...], b_ref[...], TN, preferred_element_type=F32)

        @pl.when(pl.program_id(2) == n_k - 1)
        def _():
            o_ref[...] = acc[...].astype(o_ref.dtype)

    return _call(
        body, name=name, grid=(m // tmo, n // tno, n_k),
        in_specs=[pl.BlockSpec((tk, tmo), lambda i, j, k: (k, i)),
                  pl.BlockSpec((tk, tno), lambda i, j, k: (k, j))],
        out_specs=pl.BlockSpec((tmo, tno), lambda i, j, k: (i, j)),
        out_shape=_sds((m, n), BF16),
        scratch_shapes=[pltpu.VMEM((tmo, tno), F32)],
        compiler_params=_params(("parallel", "parallel", "arbitrary")),
    )(a, b)


def _adamw(w, m, v, g, name, n_parts=0, tr=256):
    rows, cols = w.shape
    tr = min(tr, rows)
    c1 = 1.0 - ADAM_B1 ** ADAM_STEP
    c2 = 1.0 - ADAM_B2 ** ADAM_STEP

    def body(w_ref, m_ref, v_ref, g_ref, go_ref, d_ref, mo_ref, vo_ref):
        if n_parts:
            gv = g_ref[0].astype(F32)
            for p in range(1, n_parts):
                gv = gv + g_ref[p].astype(F32)
        else:
            gv = g_ref[...]
        go_ref[...] = gv
        mn = ADAM_B1 * m_ref[...] + (1.0 - ADAM_B1) * gv
        vn = ADAM_B2 * v_ref[...] + (1.0 - ADAM_B2) * (gv * gv)
        mo_ref[...] = mn
        vo_ref[...] = vn
        d_ref[...] = -ADAM_LR * ((mn / c1) / (jnp.sqrt(vn / c2) + ADAM_EPS) + ADAM_WD * w_ref[...])

    blk = pl.BlockSpec((tr, cols), lambda i: (i, 0))
    g_spec = pl.BlockSpec((n_parts, tr, cols), lambda i: (0, i, 0)) if n_parts else blk
    return _call(
        body, name=name, grid=(rows // tr,),
        in_specs=[blk, blk, blk, g_spec], out_specs=[blk] * 4,
        out_shape=[_sds((rows, cols), F32)] * 4,
        compiler_params=_params(("parallel",)),
    )(w, m, v, g)


def _cols_to_full(blocks):
    n, r, c = blocks.shape
    return jnp.transpose(blocks, (1, 0, 2)).reshape(r, n * c)


def _full_to_cols(full, n=N_DEV):
    r, c = full.shape
    return jnp.transpose(full.reshape(r, n, c // n), (1, 0, 2))


def _pad_lanes(v, width):
    return jnp.pad(v, ((0, 0), (0, width - v.shape[1])))


def kernel(x, c, w_ada, b_ada, g_mix, w_in, w_dw, b_dw, g_conv_ln, b_conv_ln, g_q, g_k, w_out, g_ffn, w_gate, w_up, w_down, loss_target, m_w_ada, m_b_ada, m_g_mix, m_w_in, m_w_dw, m_b_dw, m_g_conv_ln, m_b_conv_ln, m_g_q, m_g_k, m_w_out, m_g_ffn, m_w_gate, m_w_up, m_w_down, v_w_ada, v_b_ada, v_g_mix, v_w_in, v_w_dw, v_b_dw, v_g_conv_ln, v_b_conv_ln, v_g_q, v_g_k, v_w_out, v_g_ffn, v_w_gate, v_w_up, v_w_down):
    n_seq, seq, _ = x.shape
    tokens = n_seq * seq
    me = 4 * lax.axis_index("x") + 2 * lax.axis_index("y") + lax.axis_index("c")
    ada_cols = w_ada.shape[2]
    dw_cols = w_dw.shape[2]

    (c_g, w_in_g, w_dw_g) = _gather_by_chip([c, w_in[0].astype(BF16), w_dw[0]], "gather_weights")
    c_all = c_g.reshape(N_DEV * n_seq, D_MODEL)
    w_in_f = _cols_to_full(w_in_g)
    w_dw_f = _cols_to_full(w_dw_g)

    b_cols = lax.dynamic_slice(b_ada, (0, me * ada_cols), (1, ada_cols))
    mod_cols = _ada_fwd(c_all, w_ada[0], b_cols)
    (mod_g,) = _exchange([(mod_cols, False)], "gather_mod")
    mod_mine = lax.dynamic_slice(mod_g, (0, me * n_seq, 0), (N_DEV, n_seq, ada_cols))
    mod = jnp.transpose(mod_mine, (1, 0, 2)).reshape(n_seq, N_MOD, D_MODEL)
    mod8 = jnp.pad(mod, ((0, 0), (0, 8 - N_MOD), (0, 0)))

    x2 = x.reshape(tokens, D_MODEL)
    h1, proj = _mix_in(x2, mod8, g_mix, w_in_f, seq)
    proj3 = proj.reshape(n_seq, seq, D_IN)
    uc3 = _conv_fwd(proj3, w_dw_f, b_dw)
    g_q2, g_k2 = jnp.tile(g_q, (1, 2)), jnp.tile(g_k, (1, 2))
    y_att3, lse3, w_out_g, w_gate_g, w_up_g, w_down_g = _attn_fwd(
        proj3, g_q2, g_k2,
        [(w_out[0].astype(BF16), False), (w_gate[0].astype(BF16), False), (w_up[0].astype(BF16), False),
         (w_down[0].astype(BF16), False)])
    w_out_f = w_out_g.reshape(D_MODEL, D_MODEL)
    w_gate_f = _cols_to_full(w_gate_g)
    w_up_f = _cols_to_full(w_up_g)
    w_down_f = w_down_g.reshape(D_FF, D_MODEL)
    uc2 = uc3.reshape(tokens, D_CONV)
    y_att2 = y_att3.reshape(tokens, D_ATT)
    y_conv, mix, x1, h2 = _mix_out(uc2, y_att2, x2, mod8, g_conv_ln, b_conv_ln, g_ffn, w_out_f, seq)
    gate, up, dy, df, sq, dgate_f = _ffn_fwd(
        h2, w_gate_f, w_up_f, w_down_f, x1, loss_target.reshape(tokens, D_MODEL), mod8, seq)

    dgate, dup, act, dx1, dmix, dg_ffn, dmod_f = _ffn_bwd(
        df, gate, up, w_gate_f, w_up_f, w_down_f, x1, dy, mix, mod8, g_ffn, seq)
    duc2, do2, dgb_ln = _mix_out_bwd(dmix, uc2, g_conv_ln, b_conv_ln, w_out_f)
    d_a3, d_g3, dw_dw_p, db_dw_p = _conv_bwd(duc2.reshape(n_seq, seq, D_CONV), proj3, w_dw_f)
    gw_gate = _grad_matmul(h2, dgate, "grad_w_gate", D_MODEL, D_FF // 2)
    gw_up = _grad_matmul(h2, dup, "grad_w_up", D_MODEL, D_FF // 2)
    gw_down = _grad_matmul(act, df, "grad_w_down", D_FF // 2, D_MODEL)
    gw_out = _grad_matmul_parts([y_conv, y_att2], [dmix], "grad_w_out")
    d_q3, d_k3, d_v3, dg_qk, p_gate, p_up, p_down, p_out = _attn_bwd(
        proj3, do2.reshape(n_seq, seq, D_ATT), y_att3, lse3, g_q2, g_k2,
        [(_full_to_cols(gw_gate).astype(BF16), True), (_full_to_cols(gw_up).astype(BF16), True),
         (gw_down.astype(BF16).reshape(N_DEV, D_FF // N_DEV, D_MODEL), True),
         (gw_out.astype(BF16).reshape(N_DEV, D_MODEL // N_DEV, D_MODEL), True)])
    flat = lambda t: t.reshape(tokens, t.shape[-1])
    d_a, d_g, d_q, d_k, d_v = flat(d_a3), flat(d_g3), flat(d_q3), flat(d_k3), flat(d_v3)
    gw_in = _grad_matmul_parts([h1], [d_a, d_g, d_q, d_k, d_v], "grad_w_in")
    grad_x2, dg_mix, dmod_m, p_in = _mix_in_bwd(
        d_a, d_g, d_q, d_k, d_v, w_in_f, x2, dx1, mod8, g_mix, seq, [(_full_to_cols(gw_in).astype(BF16), True)])

    dmod = jnp.concatenate([dmod_m[:, 0], dmod_m[:, 1], dmod_f[:, 2], dmod_f[:, 0], dmod_f[:, 1], dgate_f[:, 0]], axis=1)
    dg_q = dg_qk[0:1, 0:HEAD_DIM] + dg_qk[0:1, HEAD_DIM:]
    dg_k = dg_qk[1:2, 0:HEAD_DIM] + dg_qk[1:2, HEAD_DIM:]
    loss_part = (0.5 / D_MODEL) * jnp.sum(sq[0:1, :], axis=1, keepdims=True)
    small = jnp.concatenate(
        [dg_mix[0:1], dg_ffn[0:1], db_dw_p[0:1], dgb_ln[0:1], dgb_ln[1:2],
         _pad_lanes(dg_q, LANES), _pad_lanes(dg_k, LANES), _pad_lanes(loss_part, LANES)], axis=1)
    n_small = small.shape[1] - LANES

    (dmod_g, small_g, dw_g) = _exchange([(dmod, False), (small, False), (dw_dw_p, False)], "gather_small_grads")

    dmod_all = dmod_g.reshape(N_DEV * n_seq, N_MOD * D_MODEL)
    dmod_cols = lax.dynamic_slice(dmod_all, (0, me * ada_cols), (N_DEV * n_seq, ada_cols))
    gw_ada, gb_ada = _ada_bwd(c_all, dmod_cols, dmod_all)

    res = {}
    res["w_ada"] = _adamw(w_ada[0], m_w_ada[0], v_w_ada[0], gw_ada, "adamw_w_ada")
    res["b_ada"] = _adamw(b_ada, m_b_ada, v_b_ada, gb_ada, "adamw_b_ada")
    res["w_in"] = _adamw(w_in[0], m_w_in[0], v_w_in[0], p_in, "adamw_w_in", N_DEV)
    res["w_out"] = _adamw(w_out[0], m_w_out[0], v_w_out[0], p_out, "adamw_w_out", N_DEV)
    res["w_gate"] = _adamw(w_gate[0], m_w_gate[0], v_w_gate[0], p_gate, "adamw_w_gate", N_DEV)
    res["w_up"] = _adamw(w_up[0], m_w_up[0], v_w_up[0], p_up, "adamw_w_up", N_DEV)
    res["w_down"] = _adamw(w_down[0], m_w_down[0], v_w_down[0], p_down, "adamw_w_down", N_DEV, tr=176)
    dw_mine = lax.dynamic_slice(dw_g, (0, 0, me * dw_cols), (N_DEV, CONV_WIDTH, dw_cols))
    res["w_dw"] = _adamw(w_dw[0], m_w_dw[0], v_w_dw[0], dw_mine, "adamw_w_dw", N_DEV)

    small_names = ["g_mix", "g_ffn", "b_dw", "g_conv_ln", "b_conv_ln", "g_q", "g_k"]
    small_w = {"g_mix": (g_mix, m_g_mix, v_g_mix), "g_ffn": (g_ffn, m_g_ffn, v_g_ffn), "b_dw": (b_dw, m_b_dw, v_b_dw),
               "g_conv_ln": (g_conv_ln, m_g_conv_ln, v_g_conv_ln), "b_conv_ln": (b_conv_ln, m_b_conv_ln, v_b_conv_ln),
               "g_q": (g_q, m_g_q, v_g_q), "g_k": (g_k, m_g_k, v_g_k)}
    widths = [max(small_w[n][0].shape[1], LANES) for n in small_names]
    packed = [jnp.concatenate([_pad_lanes(small_w[n][i], wd) for n, wd in zip(small_names, widths)], axis=1) for i in range(3)]
    outs = _adamw(packed[0], packed[1], packed[2], small_g[:, :, :n_small], "adamw_small", N_DEV)
    off = 0
    for n, wd in zip(small_names, widths):
        real = small_w[n][0].shape[1]
        res[n] = tuple(o[:, off:off + real] for o in outs)
        off += wd
    loss = jnp.sum(small_g[:, 0, n_small])

    order = ["w_ada", "b_ada", "g_mix", "w_in", "w_dw", "b_dw", "g_conv_ln", "b_conv_ln", "g_q", "g_k",
             "w_out", "g_ffn", "w_gate", "w_up", "w_down"]
    lead = {"w_ada", "w_in", "w_dw", "w_out", "w_gate", "w_up", "w_down"}
    grads, deltas, new_m, new_v = [], [], [], []
    for n in order:
        g, d, mn, vn = res[n]
        g, d, mn, vn = (t[None] if n in lead else t for t in (g, d, mn, vn))
        grads.append(g)
        deltas.append(d)
        new_m.append(mn)
        new_v.append(vn)
    return (loss, grad_x2.reshape(n_seq, seq, D_MODEL), *grads, *deltas, *new_m, *new_v)
```

```python
import numpy as np
import jax
import jax.numpy as jnp
from jax import lax
from jax.experimental import pallas as pl
from jax.experimental.pallas import tpu as pltpu

F32 = jnp.float32
BF16 = jnp.bfloat16

N_DEV = 8
D_MODEL = 1024
D_CONV = 512
D_ATT = 512
HEAD_DIM = 64
CONV_WIDTH = 31
D_IN = 2 * D_CONV + 3 * D_ATT
D_FF = 2816
N_MOD = 6
EPS = 1e-6
RADIUS = 64
DILATIONS = (1, 4, 16)
Q_BLOCK = 128
LANES = 128
VMEM_LIMIT = 56 * 1024 * 1024

ADAM_LR = 0.001
ADAM_B1 = 0.9
ADAM_B2 = 0.999
ADAM_EPS = 1e-08
ADAM_WD = 0.01
ADAM_STEP = 10

NT = (((1,), (1,)), ((), ()))
TN = (((0,), (0,)), ((), ()))


def _call(body, **kw):
    return pl.pallas_call(body, **kw)


def _params(sem=None, vmem=VMEM_LIMIT):
    return pltpu.CompilerParams(dimension_semantics=sem, vmem_limit_bytes=vmem)


def _sig(x):
    return 1.0 / (1.0 + jnp.exp(-x))


def _sds(shape, dtype):
    return jax.ShapeDtypeStruct(shape, dtype)


N_PEER = N_DEV - 1
ANY_SPEC = pl.BlockSpec(memory_space=pl.ANY)


def _exchange_copies(scatter, ins, outs, *sems):
    n = len(ins)
    if n == 0:
        return [], []
    send_sems, recv_sems, local_sems = sems
    x, y, c = lax.axis_index("x"), lax.axis_index("y"), lax.axis_index("c")
    me = 4 * x + 2 * y + c

    def src(a, slot):
        return ins[a].at[slot] if scatter[a] else ins[a]

    local = [pltpu.make_async_copy(src(a, me), outs[a].at[me], local_sems.at[a]) for a in range(n)]
    flights = []
    for k in range(1, N_DEV):
        px = 1 - x if k & 4 else x
        py = 1 - y if k & 2 else y
        pc = 1 - c if k & 1 else c
        pid = 4 * px + 2 * py + pc
        for a in range(n):
            i = a * N_PEER + k - 1
            send, recv = (pltpu.make_async_remote_copy(
                src_ref=src(a, pid), dst_ref=outs[a].at[slot],
                send_sem=send_sems.at[i], recv_sem=recv_sems.at[i],
                device_id=(px, py, pc), device_id_type=pl.DeviceIdType.MESH) for slot in (me, pid))
            flights.append((send, recv))
    return local, flights


def _exchange_start(*args):
    local, flights = _exchange_copies(*args)
    for cp in local:
        cp.start()
    for send, _ in flights:
        send.start()


def _exchange_wait(*args):
    local, flights = _exchange_copies(*args)
    for send, recv in flights:
        send.wait_send()
        recv.wait_recv()
    for cp in local:
        cp.wait()


def _exchange_shapes(items):
    return [_sds((N_DEV,) + tuple(arr.shape[1:] if scatter else arr.shape), arr.dtype) for arr, scatter in items]


def _exchange_sems(n):
    if n == 0:
        return []
    return [pltpu.SemaphoreType.DMA((n * N_PEER,)), pltpu.SemaphoreType.DMA((n * N_PEER,)),
            pltpu.SemaphoreType.DMA((n,))]


def _gather_by_chip(arrays, name):
    n = len(arrays)
    per = N_PEER

    def body(*refs):
        ins, outs = refs[:n], refs[n:2 * n]
        send_sems, recv_sems, local_sems = refs[2 * n:]
        x, y, c = lax.axis_index("x"), lax.axis_index("y"), lax.axis_index("c")
        sibling = (x, y, 1 - c)
        chips = [(1 - x, y), (x, 1 - y), (1 - x, 1 - y)]

        def slot(px, py, pc):
            return 4 * px + 2 * py + pc

        def copy(a, k, block, to, src=None):
            dst = outs[a].at[slot(*block)]
            return pltpu.make_async_remote_copy(
                src_ref=dst if src is None else src, dst_ref=dst,
                send_sem=send_sems.at[a * per + k], recv_sem=recv_sems.at[a * per + k],
                device_id=to, device_id_type=pl.DeviceIdType.MESH)

        me = (x, y, c)
        local = [pltpu.make_async_copy(ins[a], outs[a].at[slot(*me)], local_sems.at[a]) for a in range(n)]
        for cp in local:
            cp.start()
        first = []
        for a in range(n):
            first.append(copy(a, 0, me, sibling, src=ins[a]))
            first += [copy(a, 1 + j, me, (*chip, c), src=ins[a]) for j, chip in enumerate(chips)]
        for cp in first:
            cp.start()
        passed = []
        for j, chip in enumerate(chips):
            for a in range(n):
                copy(a, 1 + j, (*chip, c), me).wait_recv()
                fwd = copy(a, 4 + j, (*chip, c), sibling)
                fwd.start()
                passed.append(fwd)
        for a in range(n):
            copy(a, 0, sibling, me).wait_recv()
            for j, chip in enumerate(chips):
                copy(a, 4 + j, (*chip, 1 - c), me).wait_recv()
        for cp in first + passed:
            cp.wait_send()
        for cp in local:
            cp.wait()

    return _call(
        body, name=name, out_shape=_exchange_shapes([(arr, False) for arr in arrays]),
        in_specs=[ANY_SPEC] * n, out_specs=[ANY_SPEC] * n, scratch_shapes=_exchange_sems(n),
    )(*arrays)


def _exchange(items, name):
    n = len(items)
    scatter = [s for _, s in items]

    def body(*refs):
        args = (scatter, refs[:n], refs[n:2 * n]) + tuple(refs[2 * n:])
        _exchange_start(*args)
        _exchange_wait(*args)

    return _call(
        body, name=name, out_shape=_exchange_shapes(items),
        in_specs=[ANY_SPEC] * n, out_specs=[ANY_SPEC] * n, scratch_shapes=_exchange_sems(n),
    )(*[a for a, _ in items])


def _ada_fwd(c_all, w_ada, b_cols):
    def body(c_ref, w_ref, b_ref, o_ref):
        cv = c_ref[...]
        sc = (cv * _sig(cv)).astype(BF16)
        o_ref[...] = jnp.dot(sc, w_ref[...].astype(BF16), preferred_element_type=F32) + b_ref[...]

    return _call(body, name="ada_fwd", out_shape=_sds((c_all.shape[0], w_ada.shape[1]), F32),
                 compiler_params=_params())(c_all, w_ada, b_cols)


def _ada_bwd(c_all, dmod_cols, dmod_all):
    def body(c_ref, dc_ref, da_ref, gw_ref, gb_ref):
        cv = c_ref[...]
        sc = (cv * _sig(cv)).astype(BF16)
        gw_ref[...] = lax.dot_general(sc, dc_ref[...].astype(BF16), TN, preferred_element_type=F32)
        gb_ref[...] = jnp.sum(da_ref[...], axis=0, keepdims=True)

    return _call(body, name="ada_bwd",
                 out_shape=[_sds((c_all.shape[1], dmod_cols.shape[1]), F32), _sds((1, dmod_all.shape[1]), F32)],
                 compiler_params=_params())(c_all, dmod_cols, dmod_all)


MIX_ROWS = 128


def _mix_in(x2, mod8, g_mix, w_in, seq, tm=512):
    tokens = x2.shape[0]
    per_seq = seq // tm

    def body(x_ref, m_ref, g_ref, w_ref, h_ref, p_ref):
        def normed(c):
            rows = pl.ds(c * MIX_ROWS, MIX_ROWS)
            xv = x_ref[rows, :]
            r = lax.rsqrt(jnp.mean(xv * xv, axis=-1, keepdims=True) + EPS)
            hb = ((xv * r * g_ref[...]) * (1.0 + m_ref[1:2, :]) + m_ref[0:1, :]).astype(BF16)
            h_ref[rows, :] = hb
            return hb

        ahead = normed(0)
        for c in range(tm // MIX_ROWS):
            hb = ahead
            if c + 1 < tm // MIX_ROWS:
                ahead = normed(c + 1)
            p_ref[pl.ds(c * MIX_ROWS, MIX_ROWS), :] = jnp.dot(hb, w_ref[...], preferred_element_type=F32)

    return _call(
        body, name="mix_in", grid=(tokens // tm,),
        in_specs=[pl.BlockSpec((tm, D_MODEL), lambda i: (i, 0)),
                  pl.BlockSpec((None, 8, D_MODEL), lambda i: (i // per_seq, 0, 0)),
                  pl.BlockSpec((1, D_MODEL), lambda i: (0, 0)),
                  pl.BlockSpec((D_MODEL, D_IN), lambda i: (0, 0))],
        out_specs=[pl.BlockSpec((tm, D_MODEL), lambda i: (i, 0)),
                   pl.BlockSpec((tm, D_IN), lambda i: (i, 0))],
        out_shape=[_sds((tokens, D_MODEL), BF16), _sds((tokens, D_IN), F32)],
        compiler_params=_params(("parallel",)),
    )(x2, mod8, g_mix, w_in)


CONV_ROWS = 64
CONV_DW_ROWS = 32
CONV_DW_UNROLL = 4
CONV_HALO = 16


def _fill_shifted(xp, sh, seq):
    for b in range(8):
        sh[b, pl.ds(0, seq + 24), :] = xp[pl.ds(b, seq + 24), :]


def _conv_fwd(proj3, w_dw, b_dw):
    n_seq, seq, _ = proj3.shape
    n_cb = D_CONV // LANES

    def body(a_ref, g_ref, w_ref, b_ref, uc_ref, xp, sh):
        zeros = jnp.zeros((CONV_HALO, LANES), F32)
        xp[pl.ds(0, CONV_HALO), :] = zeros
        xp[pl.ds(CONV_HALO + seq, CONV_HALO), :] = zeros
        xp[pl.ds(CONV_HALO, seq), :] = a_ref[...] * _sig(g_ref[...])
        _fill_shifted(xp, sh, seq)

        def blk(i, carry):
            t0 = pl.multiple_of(i * CONV_ROWS, CONV_ROWS)
            acc = jnp.zeros((CONV_ROWS, LANES), F32)
            for j in range(CONV_WIDTH):
                jj = j + 1
                acc = acc + sh[jj % 8, pl.ds(t0 + 8 * (jj // 8), CONV_ROWS), :] * w_ref[j:j + 1, :]
            uc_ref[pl.ds(t0, CONV_ROWS), :] = acc + b_ref[...]
            return carry

        lax.fori_loop(0, seq // CONV_ROWS, blk, 0)

    return _call(
        body, name="conv_fwd", grid=(n_seq, n_cb),
        in_specs=[pl.BlockSpec((None, seq, LANES), lambda b, cb: (b, 0, cb)),
                  pl.BlockSpec((None, seq, LANES), lambda b, cb: (b, 0, n_cb + cb)),
                  pl.BlockSpec((CONV_WIDTH, LANES), lambda b, cb: (0, cb)),
                  pl.BlockSpec((1, LANES), lambda b, cb: (0, cb))],
        out_specs=pl.BlockSpec((None, seq, LANES), lambda b, cb: (b, 0, cb)),
        out_shape=_sds((n_seq, seq, D_CONV), F32),
        scratch_shapes=[pltpu.VMEM((seq + 2 * CONV_HALO, LANES), F32),
                        pltpu.VMEM((8, seq + 2 * CONV_HALO, LANES), F32)],
        compiler_params=_params(("parallel", "parallel")),
    )(proj3, proj3, w_dw, b_dw)


def _conv_bwd(duc3, proj3, w_dw):
    n_seq, seq, _ = proj3.shape
    n_cb = D_CONV // LANES

    def body(duc_ref, a_ref, g_ref, w_ref, da_ref, dg_ref, dw_ref, db_ref, xp, sh):
        @pl.when(pl.program_id(1) == 0)
        def _():
            dw_ref[...] = jnp.zeros_like(dw_ref)
            db_ref[...] = jnp.zeros_like(db_ref)

        zeros = jnp.zeros((CONV_HALO, LANES), F32)
        xp[pl.ds(0, CONV_HALO), :] = zeros
        xp[pl.ds(CONV_HALO + seq, CONV_HALO), :] = zeros
        xp[pl.ds(CONV_HALO, seq), :] = a_ref[...] * _sig(g_ref[...])
        _fill_shifted(xp, sh, seq)
        for j0 in range(0, CONV_WIDTH, 8):
            taps = range(j0, min(j0 + 8, CONV_WIDTH))

            def wblk(i, accs, taps=taps):
                for u in range(CONV_DW_UNROLL):
                    t0 = pl.multiple_of((i * CONV_DW_UNROLL + u) * CONV_DW_ROWS, CONV_DW_ROWS)
                    d = duc_ref[pl.ds(t0, CONV_DW_ROWS), :]
                    accs = tuple(acc + d * sh[(j + 1) % 8, pl.ds(t0 + 8 * ((j + 1) // 8), CONV_DW_ROWS), :]
                                 for acc, j in zip(accs, taps))
                return accs

            accs = lax.fori_loop(0, seq // (CONV_DW_ROWS * CONV_DW_UNROLL), wblk,
                                 tuple(jnp.zeros((CONV_DW_ROWS, LANES), F32) for _ in taps))
            for acc, j in zip(accs, taps):
                dw_ref[j:j + 1, :] += jnp.sum(acc, axis=0, keepdims=True)
        db_ref[0:1, :] += jnp.sum(duc_ref[...], axis=0, keepdims=True)
        xp[pl.ds(CONV_HALO, seq), :] = duc_ref[...]
        _fill_shifted(xp, sh, seq)

        def ublk(i, carry):
            t0 = pl.multiple_of(i * CONV_ROWS, CONV_ROWS)
            acc = jnp.zeros((CONV_ROWS, LANES), F32)
            for j in range(CONV_WIDTH):
                jj = CONV_WIDTH - j
                acc = acc + sh[jj % 8, pl.ds(t0 + 8 * (jj // 8), CONV_ROWS), :] * w_ref[j:j + 1, :]
            av = a_ref[pl.ds(t0, CONV_ROWS), :]
            sg = _sig(g_ref[pl.ds(t0, CONV_ROWS), :])
            da_ref[pl.ds(t0, CONV_ROWS), :] = (acc * sg).astype(BF16)
            dg_ref[pl.ds(t0, CONV_ROWS), :] = (acc * av * sg * (1.0 - sg)).astype(BF16)
            return carry

        lax.fori_loop(0, seq // CONV_ROWS, ublk, 0)

    return _call(
        body, name="conv_bwd", grid=(n_cb, n_seq),
        in_specs=[pl.BlockSpec((None, seq, LANES), lambda cb, b: (b, 0, cb)),
                  pl.BlockSpec((None, seq, LANES), lambda cb, b: (b, 0, cb)),
                  pl.BlockSpec((None, seq, LANES), lambda cb, b: (b, 0, n_cb + cb)),
                  pl.BlockSpec((CONV_WIDTH, LANES), lambda cb, b: (0, cb))],
        out_specs=[pl.BlockSpec((None, seq, LANES), lambda cb, b: (b, 0, cb)),
                   pl.BlockSpec((None, seq, LANES), lambda cb, b: (b, 0, cb)),
                   pl.BlockSpec((32, LANES), lambda cb, b: (0, cb)),
                   pl.BlockSpec((8, LANES), lambda cb, b: (0, cb))],
        out_shape=[_sds((n_seq, seq, D_CONV), BF16), _sds((n_seq, seq, D_CONV), BF16),
                   _sds((32, D_CONV), F32), _sds((8, D_CONV), F32)],
        scratch_shapes=[pltpu.VMEM((seq + 2 * CONV_HALO, LANES), F32),
                        pltpu.VMEM((8, seq + 2 * CONV_HALO, LANES), F32)],
        compiler_params=_params(("parallel", "arbitrary")),
    )(duc3, proj3, proj3, w_dw)


MASKED = 1e30
ATT_ROWS = 512
ATT_UNROLL = 4


def _distance_mats(dil, seg_len):
    kw = min(2 * Q_BLOCK, seg_len)
    offsets = (0, -RADIUS, -2 * RADIUS) if kw == 2 * Q_BLOCK else (0,)
    a = np.arange(Q_BLOCK)[:, None]
    b = np.arange(kw)[None, :]
    mats = []
    for off in offsets:
        rel = np.abs(b + off - a)
        mats.append(np.where(rel <= RADIUS, dil * rel, MASKED))
    return jnp.asarray(np.stack(mats).astype(np.float32))


def _alibi_rows():
    s = np.zeros((4, 8, LANES), np.float32)
    for hp in range(4):
        for hl in range(2):
            s[hp, hl, :] = 2.0 ** (-(2 * hp + hl + 1))
    return jnp.asarray(s)


def _window(n, seg_len):
    i0 = pl.multiple_of(n * Q_BLOCK, Q_BLOCK)
    if seg_len <= Q_BLOCK:
        return i0, i0, 0
    per_seg = seg_len // Q_BLOCK
    j = n % per_seg
    seg0 = (n // per_seg) * seg_len
    ks_local = jnp.clip(j * Q_BLOCK - RADIUS, 0, seg_len - 2 * Q_BLOCK)
    ks = pl.multiple_of(seg0 + ks_local, RADIUS)
    var = jnp.where(j == 0, 0, jnp.where(j == per_seg - 1, 2, 1))
    return i0, ks, var


def _first_head(rows):
    return lax.broadcasted_iota(jnp.int32, (rows, LANES), 1) < HEAD_DIM


def _same_head():
    head = np.arange(LANES) // HEAD_DIM
    return jnp.asarray((head[:, None] == head[None, :]).astype(np.float32)).astype(BF16)


def _head_sum(x, same_ref):
    hi = x.astype(BF16)
    lo = (x - hi.astype(F32)).astype(BF16)
    return (jnp.dot(hi, same_ref[...], preferred_element_type=F32)
            + jnp.dot(lo, same_ref[...], preferred_element_type=F32))


def _head_mean(x, same_ref):
    return _head_sum(x, same_ref) * (1.0 / HEAD_DIM)


def _per_head(x, first):
    swapped = pltpu.roll(x, HEAD_DIM, 1)
    return jnp.where(first, x, swapped), jnp.where(first, swapped, x)


def _permute_rows(dst, src, dil, seq):
    seg = seq // dil
    for r in range(dil):
        rows = pl.ds(r, seg, stride=dil) if dil > 1 else pl.ds(0, seq)
        dst[pl.ds(r * seg, seg), :] = src[rows, :].astype(dst.dtype)


def _permute_rows_by_head(dst, src, dil, seq):
    seg = seq // dil
    first = _first_head(seg)
    for r in range(dil):
        rows = pl.ds(r, seg, stride=dil) if dil > 1 else pl.ds(0, seq)
        val = src[rows, :]
        dst[0, pl.ds(r * seg, seg), :] = jnp.where(first, val, 0.0).astype(dst.dtype)
        dst[1, pl.ds(r * seg, seg), :] = jnp.where(first, 0.0, val).astype(dst.dtype)


def _qk_normalise(q_ref, g2_ref, same_ref, dst, seq, scale):
    def chunk(ci, carry):
        rows = pl.ds(pl.multiple_of(ci * ATT_ROWS, ATT_ROWS), ATT_ROWS)
        qv = q_ref[rows, :]
        r = lax.rsqrt(_head_mean(qv * qv, same_ref) + EPS)
        dst[rows, :] = qv * r * (g2_ref[...] * scale)
        return carry

    lax.fori_loop(0, seq // ATT_ROWS, chunk, 0)


def _attn_fwd(proj3, g_q2, g_k2, ride):
    n_seq, seq, _ = proj3.shape
    dms = [_distance_mats(d, seq // d) for d in DILATIONS]
    same = _same_head()
    col0 = 2 * D_CONV // LANES
    n_hp = D_ATT // LANES

    n_ride = len(ride)
    ride_scatter = [s for _, s in ride]

    def body(*refs):
        q_ref, k_ref, v_ref, gq_ref, gk_ref, sl_ref, dm1, dm4, dm16, same_ref = refs[:10]
        ride_in = refs[10:10 + n_ride]
        y_ref, lse_ref = refs[10 + n_ride:12 + n_ride]
        ride_out = refs[12 + n_ride:12 + 2 * n_ride]
        (qf, kf, qp, kp, vp, oml_p, o1, o4, o16, m1, m4, m16, l1, l4, l16) = refs[12 + 2 * n_ride:27 + 2 * n_ride]
        o_nat, m_nat, l_nat = (o1, o4, o16), (m1, m4, m16), (l1, l4, l16)
        ride_args = (ride_scatter, ride_in, ride_out) + tuple(refs[27 + 2 * n_ride:])
        step = pl.program_id(0) * n_hp + pl.program_id(1)

        @pl.when(step == 0)
        def _():
            _exchange_start(*ride_args)

        dm_refs = (dm1, dm4, dm16)
        _qk_normalise(q_ref, gq_ref, same_ref, qf, seq, HEAD_DIM ** -0.5)
        _qk_normalise(k_ref, gk_ref, same_ref, kf, seq, 1.0)
        slopes = (sl_ref[0:1, 0:1], sl_ref[1:2, 0:1])
        for pi, dil in enumerate(DILATIONS):
            seg = seq // dil
            kw = min(2 * Q_BLOCK, seg)
            _permute_rows_by_head(qp, qf, dil, seq)
            _permute_rows(kp, kf, dil, seq)
            _permute_rows(vp, v_ref, dil, seq)

            def blk(it, carry, seg=seg, kw=kw, pi=pi, dst=oml_p):
                first = _first_head(Q_BLOCK)
                chains = [(sub, h) for sub in range(ATT_UNROLL) for h in range(2)]
                win = [_window(it * ATT_UNROLL + sub, seg) for sub in range(ATT_UNROLL)]
                s = {}
                for sub, h in chains:
                    i0, ks, var = win[sub]
                    s[sub, h] = lax.dot_general(qp[h, pl.ds(i0, Q_BLOCK), :], kp[pl.ds(ks, kw), :], NT,
                                                preferred_element_type=F32) - slopes[h] * dm_refs[pi][var]
                m, l, p = {}, {}, {}
                for c in chains:
                    m[c] = jnp.max(s[c], axis=1, keepdims=True)
                    e = jnp.exp(s[c] - m[c])
                    l[c] = jnp.sum(e, axis=1, keepdims=True)
                    p[c] = e.astype(BF16)
                o = {}
                for sub, h in chains:
                    o[sub, h] = jnp.dot(p[sub, h], vp[pl.ds(win[sub][1], kw), :], preferred_element_type=F32)
                packed = [jnp.concatenate([jnp.where(first, t[sub, 0], t[sub, 1]) for t in (o, m, l)], axis=1)
                          for sub in range(ATT_UNROLL)]
                rows = pl.ds(pl.multiple_of(it * (ATT_UNROLL * Q_BLOCK), ATT_UNROLL * Q_BLOCK), ATT_UNROLL * Q_BLOCK)
                dst[rows, :] = jnp.concatenate(packed, axis=0)
                return carry

            lax.fori_loop(0, seq // (Q_BLOCK * ATT_UNROLL), blk, 0)
            for r in range(dil):
                nat = pl.ds(r, seg, stride=dil) if dil > 1 else pl.ds(0, seq)
                perm = pl.ds(r * seg, seg)
                o_nat[pi][nat, :] = oml_p[perm, pl.ds(0, LANES)]
                m_nat[pi][nat, :] = oml_p[perm, pl.ds(LANES, LANES)]
                l_nat[pi][nat, :] = oml_p[perm, pl.ds(2 * LANES, LANES)]

        def merge(ci, carry):
            rows = pl.ds(pl.multiple_of(ci * ATT_ROWS, ATT_ROWS), ATT_ROWS)
            ms = [m_nat[pi][rows, :] for pi in range(3)]
            m_all = jnp.maximum(jnp.maximum(ms[0], ms[1]), ms[2])
            es = [jnp.exp(m - m_all) for m in ms]
            l_all = sum(l_nat[pi][rows, :] * es[pi] for pi in range(3))
            inv = 1.0 / l_all
            o = sum(o_nat[pi][rows, :] * (es[pi] * inv) for pi in range(3))
            y_ref[rows, :] = o.astype(BF16)
            lse_ref[rows, :] = m_all + jnp.log(l_all)
            return carry

        lax.fori_loop(0, seq // ATT_ROWS, merge, 0)

        @pl.when(step == n_seq * n_hp - 1)
        def _():
            _exchange_wait(*ride_args)

    def col(off):
        return pl.BlockSpec((None, seq, LANES), lambda b, hp: (b, 0, col0 + off * n_hp + hp))

    def whole(arr):
        return pl.BlockSpec(arr.shape, lambda b, hp: (0,) * arr.ndim)

    rows_f32 = pltpu.VMEM((seq, LANES), F32)
    rows_bf16 = pltpu.VMEM((seq, LANES), BF16)
    return _call(
        body, name="attn_fwd", grid=(n_seq, n_hp),
        in_specs=[col(0), col(1), col(2), whole(g_q2), whole(g_k2),
                  pl.BlockSpec((None, 8, LANES), lambda b, hp: (hp, 0, 0)),
                  whole(dms[0]), whole(dms[1]), whole(dms[2]), whole(same)] + [ANY_SPEC] * n_ride,
        out_specs=[pl.BlockSpec((None, seq, LANES), lambda b, hp: (b, 0, hp)),
                   pl.BlockSpec((None, seq, LANES), lambda b, hp: (b, 0, hp))] + [ANY_SPEC] * n_ride,
        out_shape=[_sds((n_seq, seq, D_ATT), BF16), _sds((n_seq, seq, D_ATT), F32)] + _exchange_shapes(ride),
        scratch_shapes=[rows_f32, rows_f32, pltpu.VMEM((2, seq, LANES), BF16), rows_bf16, rows_bf16]
        + [pltpu.VMEM((seq, 3 * LANES), F32)] + [rows_f32] * 9 + _exchange_sems(n_ride),
        compiler_params=_params(("arbitrary", "arbitrary")),
    )(proj3, proj3, proj3, g_q2, g_k2, _alibi_rows(), *dms, same, *[a for a, _ in ride])


def _attn_bwd(proj3, do3, y_att3, lse3, g_q2, g_k2, ride):
    n_seq, seq, _ = proj3.shape
    dms = [_distance_mats(d, seq // d) for d in DILATIONS]
    same = _same_head()
    col0 = 2 * D_CONV // LANES
    n_hp = D_ATT // LANES

    n_ride = len(ride)
    ride_scatter = [s for _, s in ride]

    def body(*refs):
        (q_ref, k_ref, v_ref, do_ref, o_ref, lse_ref, gq_ref, gk_ref, sl_ref, dm1, dm4, dm16,
         same_ref) = refs[:13]
        ride_in = refs[13:13 + n_ride]
        dq_ref, dk_ref, dv_ref, dg_ref = refs[13 + n_ride:17 + n_ride]
        ride_out = refs[17 + n_ride:17 + 2 * n_ride]
        (qf, kf, qp, dop, kp, vp, sn, sp, dqp, dkp, dvp, dqn, dkn, dvn) = refs[17 + 2 * n_ride:31 + 2 * n_ride]
        ride_args = (ride_scatter, ride_in, ride_out) + tuple(refs[31 + 2 * n_ride:])
        dm_refs = (dm1, dm4, dm16)
        step = pl.program_id(0) * n_hp + pl.program_id(1)

        @pl.when(step == 0)
        def _():
            _exchange_start(*ride_args)
            dg_ref[...] = jnp.zeros_like(dg_ref)

        _qk_normalise(q_ref, gq_ref, same_ref, qf, seq, HEAD_DIM ** -0.5)
        _qk_normalise(k_ref, gk_ref, same_ref, kf, seq, 1.0)

        def stats(ci, carry):
            rows = pl.ds(pl.multiple_of(ci * ATT_ROWS, ATT_ROWS), ATT_ROWS)
            first = _first_head(ATT_ROWS)
            sn[0, rows, :], sn[1, rows, :] = _per_head(lse_ref[rows, :], first)
            prod = do_ref[rows, :] * o_ref[rows, :].astype(F32)
            sn[2, rows, :], sn[3, rows, :] = _per_head(_head_sum(prod, same_ref), first)
            return carry

        lax.fori_loop(0, seq // ATT_ROWS, stats, 0)
        slopes = (sl_ref[0:1, 0:1], sl_ref[1:2, 0:1])
        half = seq // (Q_BLOCK * ATT_UNROLL)
        region = seq // ATT_UNROLL

        for pi, dil in enumerate(DILATIONS):
            seg = seq // dil
            kw = min(2 * Q_BLOCK, seg)
            _permute_rows_by_head(qp, qf, dil, seq)
            _permute_rows_by_head(dop, do_ref, dil, seq)
            _permute_rows(kp, kf, dil, seq)
            _permute_rows(vp, v_ref, dil, seq)
            if dil == 1:
                st = sn
            else:
                st = sp
                for n in range(4):
                    _permute_rows(sp.at[n], sn.at[n], dil, seq)
            for sub in range(ATT_UNROLL):
                lo, hi = max(sub * region - RADIUS, 0), min((sub + 1) * region + RADIUS, seq)
                dkp[sub, pl.ds(lo, hi - lo), :] = jnp.zeros((hi - lo, LANES), F32)
                dvp[sub, pl.ds(lo, hi - lo), :] = jnp.zeros((hi - lo, LANES), F32)

            def blk(it, carry, seg=seg, kw=kw, pi=pi, st=st):
                first = _first_head(Q_BLOCK)
                chains = [(sub, h) for sub in range(ATT_UNROLL) for h in range(2)]
                win = [_window(it + sub * half, seg) for sub in range(ATT_UNROLL)]
                qrows = [pl.ds(w[0], Q_BLOCK) for w in win]
                krows = [pl.ds(w[1], kw) for w in win]

                def over_keys(n, sub):
                    t = st[n, qrows[sub], :]
                    return t if kw == LANES else jnp.concatenate([t] * (kw // LANES), axis=1)

                s, dp = {}, {}
                for sub, h in chains:
                    s[sub, h] = lax.dot_general(qp[h, qrows[sub], :], kp[krows[sub], :], NT,
                                                preferred_element_type=F32) - slopes[h] * dm_refs[pi][win[sub][2]]
                    dp[sub, h] = lax.dot_general(dop[h, qrows[sub], :], vp[krows[sub], :], NT,
                                                 preferred_element_type=F32)
                p, ds = {}, {}
                for sub, h in chains:
                    e = jnp.exp(s[sub, h] - over_keys(h, sub))
                    ds[sub, h] = (e * (dp[sub, h] - over_keys(2 + h, sub))).astype(BF16)
                    p[sub, h] = e.astype(BF16)
                dq, dk, dv = {}, {}, {}
                for sub, h in chains:
                    dq[sub, h] = jnp.dot(ds[sub, h], kp[krows[sub], :], preferred_element_type=F32)
                    dk[sub, h] = lax.dot_general(ds[sub, h], qp[h, qrows[sub], :], TN, preferred_element_type=F32)
                    dv[sub, h] = lax.dot_general(p[sub, h], dop[h, qrows[sub], :], TN, preferred_element_type=F32)
                for sub in range(ATT_UNROLL):
                    dqp[qrows[sub], :] = jnp.where(first, dq[sub, 0], dq[sub, 1])
                    dkp[sub, krows[sub], :] += dk[sub, 0] + dk[sub, 1]
                    dvp[sub, krows[sub], :] += dv[sub, 0] + dv[sub, 1]
                return carry

            lax.fori_loop(0, half, blk, 0)
            if dil == 1:
                for sub in range(ATT_UNROLL):
                    r0 = sub * region
                    pieces = [(r0, RADIUS, [sub - 1, sub] if sub > 0 else [sub]),
                              (r0 + RADIUS, region - 2 * RADIUS, [sub]),
                              (r0 + region - RADIUS, RADIUS, [sub, sub + 1] if sub < ATT_UNROLL - 1 else [sub])]
                    for start, size, owners in pieces:
                        rows = pl.ds(start, size)
                        dqn[rows, :] = dqp[rows, :]
                        dkn[rows, :] = sum(dkp[o, rows, :] for o in owners)
                        dvn[rows, :] = sum(dvp[o, rows, :] for o in owners)
            else:
                for r in range(dil):
                    nat, perm = pl.ds(r, seg, stride=dil), pl.ds(r * seg, seg)
                    owner = (r * seg) // region
                    dqn[nat, :] += dqp[perm, :]
                    dkn[nat, :] += dkp[owner, perm, :]
                    dvn[nat, :] += dvp[owner, perm, :]

        def finish(ci, carry):
            rows = pl.ds(pl.multiple_of(ci * ATT_ROWS, ATT_ROWS), ATT_ROWS)
            for src_ref, g_ref, dn, dst_ref, scale, row in (
                    (q_ref, gq_ref, dqn, dq_ref, HEAD_DIM ** -0.5, 0), (k_ref, gk_ref, dkn, dk_ref, 1.0, 1)):
                xv = src_ref[rows, :]
                r = lax.rsqrt(_head_mean(xv * xv, same_ref) + EPS)
                xhat = xv * r
                d = dn[rows, :] * scale
                dg_ref[row:row + 1, :] += jnp.sum(d * xhat, axis=0, keepdims=True)
                dxh = d * g_ref[...]
                dst_ref[rows, :] = (r * (dxh - xhat * _head_mean(dxh * xhat, same_ref))).astype(BF16)
            dv_ref[rows, :] = dvn[rows, :].astype(BF16)
            return carry

        lax.fori_loop(0, seq // ATT_ROWS, finish, 0)

        @pl.when(step == n_seq * n_hp - 1)
        def _():
            _exchange_wait(*ride_args)

    def col(off):
        return pl.BlockSpec((None, seq, LANES), lambda b, hp: (b, 0, col0 + off * n_hp + hp))

    def whole(arr):
        return pl.BlockSpec(arr.shape, lambda b, hp: (0,) * arr.ndim)

    att = pl.BlockSpec((None, seq, LANES), lambda b, hp: (b, 0, hp))
    rows_f32 = pltpu.VMEM((seq, LANES), F32)
    rows_bf16 = pltpu.VMEM((seq, LANES), BF16)
    by_head_bf16 = pltpu.VMEM((2, seq, LANES), BF16)
    per_sub_f32 = pltpu.VMEM((ATT_UNROLL, seq, LANES), F32)
    stats_f32 = pltpu.VMEM((4, seq, LANES), F32)
    return _call(
        body, name="attn_bwd", grid=(n_seq, n_hp),
        in_specs=[col(0), col(1), col(2), att, att, att, whole(g_q2), whole(g_k2),
                  pl.BlockSpec((None, 8, LANES), lambda b, hp: (hp, 0, 0)),
                  whole(dms[0]), whole(dms[1]), whole(dms[2]), whole(same)] + [ANY_SPEC] * n_ride,
        out_specs=[att, att, att, pl.BlockSpec((8, LANES), lambda b, hp: (0, 0))] + [ANY_SPEC] * n_ride,
        out_shape=[_sds((n_seq, seq, D_ATT), BF16)] * 3 + [_sds((8, LANES), F32)] + _exchange_shapes(ride),
        scratch_shapes=[rows_f32, rows_f32, by_head_bf16, by_head_bf16, rows_bf16, rows_bf16, stats_f32, stats_f32,
                        rows_f32, per_sub_f32, per_sub_f32, rows_f32, rows_f32, rows_f32] + _exchange_sems(n_ride),
        compiler_params=_params(("arbitrary", "arbitrary")),
    )(proj3, proj3, proj3, do3, y_att3, lse3, g_q2, g_k2, _alibi_rows(), *dms, same, *[a for a, _ in ride])


def _mix_out(uc2, y_att2, x2, mod8, g_ln, b_ln, g_ffn, w_out, seq, tm=512):
    tokens = x2.shape[0]
    per_seq = seq // tm

    def body(uc_ref, ya_ref, x_ref, m_ref, gl_ref, bl_ref, gf_ref, w_ref, yc_ref, mix_ref, x1_ref, h2_ref):
        uc = uc_ref[...]
        mu = jnp.mean(uc, axis=-1, keepdims=True)
        cen = uc - mu
        rs = lax.rsqrt(jnp.mean(cen * cen, axis=-1, keepdims=True) + EPS)
        z = cen * rs * gl_ref[...] + bl_ref[...]
        yc = (z * _sig(z)).astype(BF16)
        yc_ref[...] = yc
        mix = (jnp.dot(yc, w_ref[pl.ds(0, D_CONV), :], preferred_element_type=F32)
               + jnp.dot(ya_ref[...], w_ref[pl.ds(D_CONV, D_ATT), :], preferred_element_type=F32))
        mix_ref[...] = mix
        x1 = x_ref[...] + m_ref[2:3, :] * mix
        x1_ref[...] = x1
        r = lax.rsqrt(jnp.mean(x1 * x1, axis=-1, keepdims=True) + EPS)
        h2_ref[...] = ((x1 * r * gf_ref[...]) * (1.0 + m_ref[4:5, :]) + m_ref[3:4, :]).astype(BF16)

    def rows(width):
        return pl.BlockSpec((tm, width), lambda i: (i, 0))

    def vec(width):
        return pl.BlockSpec((1, width), lambda i: (0, 0))

    return _call(
        body, name="mix_out", grid=(tokens // tm,),
        in_specs=[rows(D_CONV), rows(D_ATT), rows(D_MODEL),
                  pl.BlockSpec((None, 8, D_MODEL), lambda i: (i // per_seq, 0, 0)),
                  vec(D_CONV), vec(D_CONV), vec(D_MODEL),
                  pl.BlockSpec((D_MODEL, D_MODEL), lambda i: (0, 0))],
        out_specs=[rows(D_CONV), rows(D_MODEL), rows(D_MODEL), rows(D_MODEL)],
        out_shape=[_sds((tokens, D_CONV), BF16), _sds((tokens, D_MODEL), F32),
                   _sds((tokens, D_MODEL), F32), _sds((tokens, D_MODEL), BF16)],
        compiler_params=_params(("parallel",)),
    )(uc2, y_att2, x2, mod8, g_ln, b_ln, g_ffn, w_out)


def _mix_out_bwd(dmix, uc2, g_ln, b_ln, w_out, tm=512):
    tokens = dmix.shape[0]

    def body(dm_ref, uc_ref, gl_ref, bl_ref, w_ref, duc_ref, do_ref, dgb_ref):
        @pl.when(pl.program_id(0) == 0)
        def _():
            dgb_ref[...] = jnp.zeros_like(dgb_ref)

        dmv = dm_ref[...]
        dyc = lax.dot_general(dmv, w_ref[pl.ds(0, D_CONV), :], NT, preferred_element_type=F32)
        do_ref[...] = lax.dot_general(dmv, w_ref[pl.ds(D_CONV, D_ATT), :], NT, preferred_element_type=F32)
        uc = uc_ref[...]
        mu = jnp.mean(uc, axis=-1, keepdims=True)
        cen = uc - mu
        rs = lax.rsqrt(jnp.mean(cen * cen, axis=-1, keepdims=True) + EPS)
        xh = cen * rs
        z = xh * gl_ref[...] + bl_ref[...]
        sg = _sig(z)
        dz = dyc * (sg * (1.0 + z * (1.0 - sg)))
        dgb_ref[0:1, :] += jnp.sum(dz * xh, axis=0, keepdims=True)
        dgb_ref[1:2, :] += jnp.sum(dz, axis=0, keepdims=True)
        dxh = dz * gl_ref[...]
        duc_ref[...] = rs * (dxh - jnp.mean(dxh, axis=-1, keepdims=True)
                             - xh * jnp.mean(dxh * xh, axis=-1, keepdims=True))

    return _call(
        body, name="mix_out_bwd", grid=(tokens // tm,),
        in_specs=[pl.BlockSpec((tm, D_MODEL), lambda i: (i, 0)),
                  pl.BlockSpec((tm, D_CONV), lambda i: (i, 0)),
                  pl.BlockSpec((1, D_CONV), lambda i: (0, 0)),
                  pl.BlockSpec((1, D_CONV), lambda i: (0, 0)),
                  pl.BlockSpec((D_MODEL, D_MODEL), lambda i: (0, 0))],
        out_specs=[pl.BlockSpec((tm, D_CONV), lambda i: (i, 0)),
                   pl.BlockSpec((tm, D_ATT), lambda i: (i, 0)),
                   pl.BlockSpec((8, D_CONV), lambda i: (0, 0))],
        out_shape=[_sds((tokens, D_CONV), F32), _sds((tokens, D_ATT), F32), _sds((8, D_CONV), F32)],
        compiler_params=_params(("arbitrary",)),
    )(dmix, uc2, g_ln, b_ln, w_out)


FF_TILE = 256
FF_ROWS = 256


def _ffn_fwd(h2, w_gate, w_up, w_down, x1, target, mod8, seq, tm=1024):
    tokens = h2.shape[0]
    per_seq = seq // tm
    n_seq = tokens // seq
    last = D_FF // FF_TILE - 1

    def body(h_ref, wg_ref, wu_ref, wd_ref, x1_ref, t_ref, m_ref, gate_ref, up_ref, dy_ref, df_ref, sq_ref, dgf_ref,
             f_ref):
        i, j = pl.program_id(0), pl.program_id(1)

        @pl.when(j == 0)
        def _():
            f_ref[...] = jnp.zeros_like(f_ref)

        @pl.when((j == 0) & (i == 0))
        def _():
            sq_ref[...] = jnp.zeros_like(sq_ref)

        @pl.when((j == 0) & (i % per_seq == 0))
        def _():
            dgf_ref[...] = jnp.zeros_like(dgf_ref)

        def gate_up(r):
            hv = h_ref[pl.ds(r * FF_ROWS, FF_ROWS), :]
            return (jnp.dot(hv, wg_ref[...], preferred_element_type=F32),
                    jnp.dot(hv, wu_ref[...], preferred_element_type=F32))

        ahead = gate_up(0)
        for r in range(tm // FF_ROWS):
            gate, up = ahead
            if r + 1 < tm // FF_ROWS:
                ahead = gate_up(r + 1)
            rows = pl.ds(r * FF_ROWS, FF_ROWS)
            gate_ref[rows, :] = gate
            up_ref[rows, :] = up
            act = (gate * _sig(gate) * up).astype(BF16)
            f_ref[rows, :] += jnp.dot(act, wd_ref[...], preferred_element_type=F32)

        @pl.when(j == last)
        def _():
            gate_f = m_ref[5:6, :]
            for r in range(tm // FF_ROWS):
                rows = pl.ds(r * FF_ROWS, FF_ROWS)
                fv = f_ref[rows, :]
                diff = x1_ref[rows, :] + gate_f * fv - t_ref[rows, :]
                sq_ref[0:1, :] += jnp.sum(diff * diff, axis=0, keepdims=True)
                dy = diff * (1.0 / D_MODEL)
                dy_ref[rows, :] = dy
                df_ref[rows, :] = (gate_f * dy).astype(BF16)
                dgf_ref[0:1, :] += jnp.sum(dy * fv, axis=0, keepdims=True)

    rows_spec = pl.BlockSpec((tm, D_MODEL), lambda i, j: (i, 0))
    per = pl.BlockSpec((None, 8, D_MODEL), lambda i, j: (i // per_seq, 0, 0))
    tile = pl.BlockSpec((tm, FF_TILE), lambda i, j: (i, j))
    return _call(
        body, name="ffn_fwd", grid=(tokens // tm, D_FF // FF_TILE),
        in_specs=[rows_spec,
                  pl.BlockSpec((D_MODEL, FF_TILE), lambda i, j: (0, j)),
                  pl.BlockSpec((D_MODEL, FF_TILE), lambda i, j: (0, j)),
                  pl.BlockSpec((FF_TILE, D_MODEL), lambda i, j: (j, 0)),
                  rows_spec, rows_spec, per],
        out_specs=[tile, tile, rows_spec, rows_spec, pl.BlockSpec((8, D_MODEL), lambda i, j: (0, 0)), per],
        out_shape=[_sds((tokens, D_FF), F32), _sds((tokens, D_FF), F32), _sds((tokens, D_MODEL), F32),
                   _sds((tokens, D_MODEL), BF16), _sds((8, D_MODEL), F32), _sds((n_seq, 8, D_MODEL), F32)],
        scratch_shapes=[pltpu.VMEM((tm, D_MODEL), F32)],
        compiler_params=_params(("arbitrary", "arbitrary")),
    )(h2, w_gate, w_up, w_down, x1, target, mod8)


def _ffn_bwd(df, gate, up, w_gate, w_up, w_down, x1, dy, mix, mod8, g_ffn, seq, tm=1024):
    tokens = df.shape[0]
    per_seq = seq // tm
    n_seq = tokens // seq
    last = D_FF // FF_TILE - 1

    def body(df_ref, gate_ref, up_ref, wg_ref, wu_ref, wd_ref, m_ref, g_ref, x1_hbm, dy_hbm, mix_hbm,
             dgate_ref, dup_ref, act_ref, dx1_ref, dmix_ref, dg_ref, dm_ref, dh_ref, late, late_sems):
        i, j = pl.program_id(0), pl.program_id(1)
        my_rows = pl.ds(pl.multiple_of(i * tm, tm), tm)
        fetches = [pltpu.make_async_copy(src.at[my_rows, :], late.at[n], late_sems.at[n])
                   for n, src in enumerate((x1_hbm, dy_hbm, mix_hbm))]

        @pl.when(j == 0)
        def _():
            dh_ref[...] = jnp.zeros_like(dh_ref)
            for cp in fetches:
                cp.start()

        @pl.when((j == 0) & (i == 0))
        def _():
            dg_ref[...] = jnp.zeros_like(dg_ref)

        @pl.when((j == 0) & (i % per_seq == 0))
        def _():
            dm_ref[...] = jnp.zeros_like(dm_ref)

        def d_act(r):
            return lax.dot_general(df_ref[pl.ds(r * FF_ROWS, FF_ROWS), :], wd_ref[...], NT,
                                   preferred_element_type=F32)

        ahead = d_act(0)
        for r in range(tm // FF_ROWS):
            dact = ahead
            if r + 1 < tm // FF_ROWS:
                ahead = d_act(r + 1)
            rows = pl.ds(r * FF_ROWS, FF_ROWS)
            gate = gate_ref[rows, :]
            up = up_ref[rows, :]
            sg = _sig(gate)
            silu = gate * sg
            act_ref[rows, :] = (silu * up).astype(BF16)
            dup = (dact * silu).astype(BF16)
            dgate = (dact * up * (sg * (1.0 + gate * (1.0 - sg)))).astype(BF16)
            dup_ref[rows, :] = dup
            dgate_ref[rows, :] = dgate
            dh_ref[rows, :] += (lax.dot_general(dgate, wg_ref[...], NT, preferred_element_type=F32)
                                + lax.dot_general(dup, wu_ref[...], NT, preferred_element_type=F32))

        @pl.when(j == last)
        def _():
            for cp in fetches:
                cp.wait()
            g = g_ref[...]
            for r in range(tm // FF_ROWS):
                rows = pl.ds(r * FF_ROWS, FF_ROWS)
                dh = dh_ref[rows, :]
                x1v = late[0, rows, :]
                rs = lax.rsqrt(jnp.mean(x1v * x1v, axis=-1, keepdims=True) + EPS)
                xhat = x1v * rs
                dm_ref[0:1, :] += jnp.sum(dh, axis=0, keepdims=True)
                dm_ref[1:2, :] += jnp.sum(dh * (xhat * g), axis=0, keepdims=True)
                dn = dh * (1.0 + m_ref[4:5, :])
                dg_ref[0:1, :] += jnp.sum(dn * xhat, axis=0, keepdims=True)
                dxh = dn * g
                dx1 = late[1, rows, :] + rs * (dxh - xhat * jnp.mean(dxh * xhat, axis=-1, keepdims=True))
                dx1_ref[rows, :] = dx1
                dm_ref[2:3, :] += jnp.sum(dx1 * late[2, rows, :], axis=0, keepdims=True)
                dmix_ref[rows, :] = (m_ref[2:3, :] * dx1).astype(BF16)

    tile = pl.BlockSpec((tm, FF_TILE), lambda i, j: (i, j))
    rows_spec = pl.BlockSpec((tm, D_MODEL), lambda i, j: (i, 0))
    per = pl.BlockSpec((None, 8, D_MODEL), lambda i, j: (i // per_seq, 0, 0))
    return _call(
        body, name="ffn_bwd", grid=(tokens // tm, D_FF // FF_TILE),
        in_specs=[rows_spec, tile, tile,
                  pl.BlockSpec((D_MODEL, FF_TILE), lambda i, j: (0, j)),
                  pl.BlockSpec((D_MODEL, FF_TILE), lambda i, j: (0, j)),
                  pl.BlockSpec((FF_TILE, D_MODEL), lambda i, j: (j, 0)),
                  per, pl.BlockSpec((1, D_MODEL), lambda i, j: (0, 0)), ANY_SPEC, ANY_SPEC, ANY_SPEC],
        out_specs=[tile, tile, tile, rows_spec, rows_spec, pl.BlockSpec((8, D_MODEL), lambda i, j: (0, 0)), per],
        out_shape=[_sds((tokens, D_FF), BF16)] * 3 + [_sds((tokens, D_MODEL), F32), _sds((tokens, D_MODEL), BF16),
                                                    _sds((8, D_MODEL), F32), _sds((n_seq, 8, D_MODEL), F32)],
        scratch_shapes=[pltpu.VMEM((tm, D_MODEL), F32), pltpu.VMEM((3, tm, D_MODEL), F32),
                        pltpu.SemaphoreType.DMA((3,))],
        compiler_params=_params(("arbitrary", "arbitrary")),
    )(df, gate, up, w_gate, w_up, w_down, mod8, g_ffn, x1, dy, mix)


def _mix_in_bwd(d_a, d_g, d_q, d_k, d_v, w_in, x2, dx1, mod8, g_mix, seq, ride, tm=512):
    tokens = x2.shape[0]
    per_seq = seq // tm
    n_seq = tokens // seq
    parts = (d_a, d_g, d_q, d_k, d_v)
    width = D_CONV
    n_ride = len(ride)
    ride_scatter = [s for _, s in ride]

    def body(*refs):
        da_ref, dg_ref, dq_ref, dk_ref, dv_ref, w_ref, x_ref, dx1_ref, m_ref, g_ref = refs[:10]
        ride_in = refs[10:10 + n_ride]
        gx_ref, dgm_ref, dm_ref = refs[10 + n_ride:13 + n_ride]
        ride_args = (ride_scatter, ride_in, refs[13 + n_ride:13 + 2 * n_ride]) + tuple(refs[13 + 2 * n_ride:])
        i = pl.program_id(0)

        @pl.when(i == 0)
        def _():
            _exchange_start(*ride_args)
            dgm_ref[...] = jnp.zeros_like(dgm_ref)

        @pl.when(i % per_seq == 0)
        def _():
            dm_ref[...] = jnp.zeros_like(dm_ref)

        dh = jnp.zeros((tm, D_MODEL), F32)
        for n, ref in enumerate((da_ref, dg_ref, dq_ref, dk_ref, dv_ref)):
            dh = dh + lax.dot_general(ref[...], w_ref[:, pl.ds(n * width, width)], NT, preferred_element_type=F32)
        xv = x_ref[...]
        r = lax.rsqrt(jnp.mean(xv * xv, axis=-1, keepdims=True) + EPS)
        xhat = xv * r
        g = g_ref[...]
        dm_ref[0:1, :] += jnp.sum(dh, axis=0, keepdims=True)
        dm_ref[1:2, :] += jnp.sum(dh * (xhat * g), axis=0, keepdims=True)
        dn = dh * (1.0 + m_ref[1:2, :])
        dgm_ref[0:1, :] += jnp.sum(dn * xhat, axis=0, keepdims=True)
        dxh = dn * g
        gx_ref[...] = dx1_ref[...] + r * (dxh - xhat * jnp.mean(dxh * xhat, axis=-1, keepdims=True))

        @pl.when(i == tokens // tm - 1)
        def _():
            _exchange_wait(*ride_args)

    rows = pl.BlockSpec((tm, D_MODEL), lambda i: (i, 0))
    half = pl.BlockSpec((tm, width), lambda i: (i, 0))
    per = pl.BlockSpec((None, 8, D_MODEL), lambda i: (i // per_seq, 0, 0))
    return _call(
        body, name="mix_in_bwd", grid=(tokens // tm,),
        in_specs=[half] * 5 + [pl.BlockSpec((D_MODEL, D_IN), lambda i: (0, 0)), rows, rows, per,
                               pl.BlockSpec((1, D_MODEL), lambda i: (0, 0))] + [ANY_SPEC] * n_ride,
        out_specs=[rows, pl.BlockSpec((8, D_MODEL), lambda i: (0, 0)), per] + [ANY_SPEC] * n_ride,
        out_shape=[_sds((tokens, D_MODEL), F32), _sds((8, D_MODEL), F32), _sds((n_seq, 8, D_MODEL), F32)]
        + _exchange_shapes(ride),
        scratch_shapes=_exchange_sems(n_ride),
        compiler_params=_params(("arbitrary",)),
    )(*parts, w_in, x2, dx1, mod8, g_mix, *[a for a, _ in ride])


def _grad_matmul_parts(a_parts, b_parts, name, tk=1024):
    tokens = a_parts[0].shape[0]
    na, nb = len(a_parts), len(b_parts)
    ma, nbw = a_parts[0].shape[1], b_parts[0].shape[1]

    n_k = tokens // tk

    def body(*refs):
        a_refs, b_refs, o_ref, acc = refs[:na], refs[na:na + nb], refs[na + nb], refs[na + nb + 1]

        @pl.when(pl.program_id(0) == 0)
        def _():
            acc[...] = jnp.zeros_like(acc)

        for i in range(na):
            for j in range(nb):
                acc[pl.ds(i * ma, ma), pl.ds(j * nbw, nbw)] += lax.dot_general(
                    a_refs[i][...], b_refs[j][...], TN, preferred_element_type=F32)

        @pl.when(pl.program_id(0) == n_k - 1)
        def _():
            o_ref[...] = acc[...].astype(o_ref.dtype)

    return _call(
        body, name=name, grid=(n_k,),
        in_specs=[pl.BlockSpec((tk, ma), lambda k: (k, 0))] * na + [pl.BlockSpec((tk, nbw), lambda k: (k, 0))] * nb,
        out_specs=pl.BlockSpec((na * ma, nb * nbw), lambda k: (0, 0)),
        out_shape=_sds((na * ma, nb * nbw), BF16),
        scratch_shapes=[pltpu.VMEM((na * ma, nb * nbw), F32)],
        compiler_params=_params(("arbitrary",)),
    )(*a_parts, *b_parts)


def _grad_matmul(a, b, name, tmo, tno, tk=1024):
    tokens, m = a.shape
    n = b.shape[1]
    n_k = tokens // tk

    def body(a_ref, b_ref, o_ref, acc):
        @pl.when(pl.program_id(2) == 0)
        def _():
            acc[...] = jnp.zeros_like(acc)

        acc[...] += lax.dot_general(a_ref[...], b_ref[...], TN, preferred_element_type=F32)

        @pl.when(pl.program_id(2) == n_k - 1)
        def _():
            o_ref[...] = acc[...].astype(o_ref.dtype)

    return _call(
        body, name=name, grid=(m // tmo, n // tno, n_k),
        in_specs=[pl.BlockSpec((tk, tmo), lambda i, j, k: (k, i)),
                  pl.BlockSpec((tk, tno), lambda i, j, k: (k, j))],
        out_specs=pl.BlockSpec((tmo, tno), lambda i, j, k: (i, j)),
        out_shape=_sds((m, n), BF16),
        scratch_shapes=[pltpu.VMEM((tmo, tno), F32)],
        compiler_params=_params(("parallel", "parallel", "arbitrary")),
    )(a, b)


def _adamw(w, m, v, g, name, n_parts=0, tr=256):
    rows, cols = w.shape
    tr = min(tr, rows)
    c1 = 1.0 - ADAM_B1 ** ADAM_STEP
    c2 = 1.0 - ADAM_B2 ** ADAM_STEP

    def body(w_ref, m_ref, v_ref, g_ref, go_ref, d_ref, mo_ref, vo_ref):
        if n_parts:
            gv = g_ref[0].astype(F32)
            for p in range(1, n_parts):
                gv = gv + g_ref[p].astype(F32)
        else:
            gv = g_ref[...]
        go_ref[...] = gv
        mn = ADAM_B1 * m_ref[...] + (1.0 - ADAM_B1) * gv
        vn = ADAM_B2 * v_ref[...] + (1.0 - ADAM_B2) * (gv * gv)
        mo_ref[...] = mn
        vo_ref[...] = vn
        d_ref[...] = -ADAM_LR * ((mn / c1) / (jnp.sqrt(vn / c2) + ADAM_EPS) + ADAM_WD * w_ref[...])

    blk = pl.BlockSpec((tr, cols), lambda i: (i, 0))
    g_spec = pl.BlockSpec((n_parts, tr, cols), lambda i: (0, i, 0)) if n_parts else blk
    return _call(
        body, name=name, grid=(rows // tr,),
        in_specs=[blk, blk, blk, g_spec], out_specs=[blk] * 4,
        out_shape=[_sds((rows, cols), F32)] * 4,
        compiler_params=_params(("parallel",)),
    )(w, m, v, g)


def _cols_to_full(blocks):
    n, r, c = blocks.shape
    return jnp.transpose(blocks, (1, 0, 2)).reshape(r, n * c)


def _full_to_cols(full, n=N_DEV):
    r, c = full.shape
    return jnp.transpose(full.reshape(r, n, c // n), (1, 0, 2))


def _pad_lanes(v, width):
    return jnp.pad(v, ((0, 0), (0, width - v.shape[1])))


def kernel(x, c, w_ada, b_ada, g_mix, w_in, w_dw, b_dw, g_conv_ln, b_conv_ln, g_q, g_k, w_out, g_ffn, w_gate, w_up, w_down, loss_target, m_w_ada, m_b_ada, m_g_mix, m_w_in, m_w_dw, m_b_dw, m_g_conv_ln, m_b_conv_ln, m_g_q, m_g_k, m_w_out, m_g_ffn, m_w_gate, m_w_up, m_w_down, v_w_ada, v_b_ada, v_g_mix, v_w_in, v_w_dw, v_b_dw, v_g_conv_ln, v_b_conv_ln, v_g_q, v_g_k, v_w_out, v_g_ffn, v_w_gate, v_w_up, v_w_down):
    n_seq, seq, _ = x.shape
    tokens = n_seq * seq
    me = 4 * lax.axis_index("x") + 2 * lax.axis_index("y") + lax.axis_index("c")
    ada_cols = w_ada.shape[2]
    dw_cols = w_dw.shape[2]

    (c_g, w_in_g, w_dw_g) = _gather_by_chip([c, w_in[0].astype(BF16), w_dw[0]], "gather_weights")
    c_all = c_g.reshape(N_DEV * n_seq, D_MODEL)
    w_in_f = _cols_to_full(w_in_g)
    w_dw_f = _cols_to_full(w_dw_g)

    b_cols = lax.dynamic_slice(b_ada, (0, me * ada_cols), (1, ada_cols))
    mod_cols = _ada_fwd(c_all, w_ada[0], b_cols)
    (mod_g,) = _exchange([(mod_cols, False)], "gather_mod")
    mod_mine = lax.dynamic_slice(mod_g, (0, me * n_seq, 0), (N_DEV, n_seq, ada_cols))
    mod = jnp.transpose(mod_mine, (1, 0, 2)).reshape(n_seq, N_MOD, D_MODEL)
    mod8 = jnp.pad(mod, ((0, 0), (0, 8 - N_MOD), (0, 0)))

    x2 = x.reshape(tokens, D_MODEL)
    h1, proj = _mix_in(x2, mod8, g_mix, w_in_f, seq)
    proj3 = proj.reshape(n_seq, seq, D_IN)
    uc3 = _conv_fwd(proj3, w_dw_f, b_dw)
    g_q2, g_k2 = jnp.tile(g_q, (1, 2)), jnp.tile(g_k, (1, 2))
    y_att3, lse3, w_out_g, w_gate_g, w_up_g, w_down_g = _attn_fwd(
        proj3, g_q2, g_k2,
        [(w_out[0].astype(BF16), False), (w_gate[0].astype(BF16), False), (w_up[0].astype(BF16), False),
         (w_down[0].astype(BF16), False)])
    w_out_f = w_out_g.reshape(D_MODEL, D_MODEL)
    w_gate_f = _cols_to_full(w_gate_g)
    w_up_f = _cols_to_full(w_up_g)
    w_down_f = w_down_g.reshape(D_FF, D_MODEL)
    uc2 = uc3.reshape(tokens, D_CONV)
    y_att2 = y_att3.reshape(tokens, D_ATT)
    y_conv, mix, x1, h2 = _mix_out(uc2, y_att2, x2, mod8, g_conv_ln, b_conv_ln, g_ffn, w_out_f, seq)
    gate, up, dy, df, sq, dgate_f = _ffn_fwd(
        h2, w_gate_f, w_up_f, w_down_f, x1, loss_target.reshape(tokens, D_MODEL), mod8, seq)

    dgate, dup, act, dx1, dmix, dg_ffn, dmod_f = _ffn_bwd(
        df, gate, up, w_gate_f, w_up_f, w_down_f, x1, dy, mix, mod8, g_ffn, seq)
    duc2, do2, dgb_ln = _mix_out_bwd(dmix, uc2, g_conv_ln, b_conv_ln, w_out_f)
    d_a3, d_g3, dw_dw_p, db_dw_p = _conv_bwd(duc2.reshape(n_seq, seq, D_CONV), proj3, w_dw_f)
    gw_gate = _grad_matmul(h2, dgate, "grad_w_gate", D_MODEL, D_FF // 2)
    gw_up = _grad_matmul(h2, dup, "grad_w_up", D_MODEL, D_FF // 2)
    gw_down = _grad_matmul(act, df, "grad_w_down", D_FF // 2, D_MODEL)
    gw_out = _grad_matmul_parts([y_conv, y_att2], [dmix], "grad_w_out")
    d_q3, d_k3, d_v3, dg_qk, p_gate, p_up, p_down, p_out = _attn_bwd(
        proj3, do2.reshape(n_seq, seq, D_ATT), y_att3, lse3, g_q2, g_k2,
        [(_full_to_cols(gw_gate).astype(BF16), True), (_full_to_cols(gw_up).astype(BF16), True),
         (gw_down.astype(BF16).reshape(N_DEV, D_FF // N_DEV, D_MODEL), True),
         (gw_out.astype(BF16).reshape(N_DEV, D_MODEL // N_DEV, D_MODEL), True)])
    flat = lambda t: t.reshape(tokens, t.shape[-1])
    d_a, d_g, d_q, d_k, d_v = flat(d_a3), flat(d_g3), flat(d_q3), flat(d_k3), flat(d_v3)
    gw_in = _grad_matmul_parts([h1], [d_a, d_g, d_q, d_k, d_v], "grad_w_in")
    grad_x2, dg_mix, dmod_m, p_in = _mix_in_bwd(
        d_a, d_g, d_q, d_k, d_v, w_in_f, x2, dx1, mod8, g_mix, seq, [(_full_to_cols(gw_in).astype(BF16), True)])

    dmod = jnp.concatenate([dmod_m[:, 0], dmod_m[:, 1], dmod_f[:, 2], dmod_f[:, 0], dmod_f[:, 1], dgate_f[:, 0]], axis=1)
    dg_q = dg_qk[0:1, 0:HEAD_DIM] + dg_qk[0:1, HEAD_DIM:]
    dg_k = dg_qk[1:2, 0:HEAD_DIM] + dg_qk[1:2, HEAD_DIM:]
    loss_part = (0.5 / D_MODEL) * jnp.sum(sq[0:1, :], axis=1, keepdims=True)
    small = jnp.concatenate(
        [dg_mix[0:1], dg_ffn[0:1], db_dw_p[0:1], dgb_ln[0:1], dgb_ln[1:2],
         _pad_lanes(dg_q, LANES), _pad_lanes(dg_k, LANES), _pad_lanes(loss_part, LANES)], axis=1)
    n_small = small.shape[1] - LANES

    (dmod_g, small_g, dw_g) = _exchange([(dmod, False), (small, False), (dw_dw_p, False)], "gather_small_grads")

    dmod_all = dmod_g.reshape(N_DEV * n_seq, N_MOD * D_MODEL)
    dmod_cols = lax.dynamic_slice(dmod_all, (0, me * ada_cols), (N_DEV * n_seq, ada_cols))
    gw_ada, gb_ada = _ada_bwd(c_all, dmod_cols, dmod_all)

    res = {}
    res["w_ada"] = _adamw(w_ada[0], m_w_ada[0], v_w_ada[0], gw_ada, "adamw_w_ada")
    res["b_ada"] = _adamw(b_ada, m_b_ada, v_b_ada, gb_ada, "adamw_b_ada")
    res["w_in"] = _adamw(w_in[0], m_w_in[0], v_w_in[0], p_in, "adamw_w_in", N_DEV)
    res["w_out"] = _adamw(w_out[0], m_w_out[0], v_w_out[0], p_out, "adamw_w_out", N_DEV)
    res["w_gate"] = _adamw(w_gate[0], m_w_gate[0], v_w_gate[0], p_gate, "adamw_w_gate", N_DEV)
    res["w_up"] = _adamw(w_up[0], m_w_up[0], v_w_up[0], p_up, "adamw_w_up", N_DEV)
    res["w_down"] = _adamw(w_down[0], m_w_down[0], v_w_down[0], p_down, "adamw_w_down", N_DEV, tr=176)
    dw_mine = lax.dynamic_slice(dw_g, (0, 0, me * dw_cols), (N_DEV, CONV_WIDTH, dw_cols))
    res["w_dw"] = _adamw(w_dw[0], m_w_dw[0], v_w_dw[0], dw_mine, "adamw_w_dw", N_DEV)

    small_names = ["g_mix", "g_ffn", "b_dw", "g_conv_ln", "b_conv_ln", "g_q", "g_k"]
    small_w = {"g_mix": (g_mix, m_g_mix, v_g_mix), "g_ffn": (g_ffn, m_g_ffn, v_g_ffn), "b_dw": (b_dw, m_b_dw, v_b_dw),
               "g_conv_ln": (g_conv_ln, m_g_conv_ln, v_g_conv_ln), "b_conv_ln": (b_conv_ln, m_b_conv_ln, v_b_conv_ln),
               "g_q": (g_q, m_g_q, v_g_q), "g_k": (g_k, m_g_k, v_g_k)}
    widths = [max(small_w[n][0].shape[1], LANES) for n in small_names]
    packed = [jnp.concatenate([_pad_lanes(small_w[n][i], wd) for n, wd in zip(small_names, widths)], axis=1) for i in range(3)]
    outs = _adamw(packed[0], packed[1], packed[2], small_g[:, :, :n_small], "adamw_small", N_DEV)
    off = 0
    for n, wd in zip(small_names, widths):
        real = small_w[n][0].shape[1]
        res[n] = tuple(o[:, off:off + real] for o in outs)
        off += wd
    loss = jnp.sum(small_g[:, 0, n_small])

    order = ["w_ada", "b_ada", "g_mix", "w_in", "w_dw", "b_dw", "g_conv_ln", "b_conv_ln", "g_q", "g_k",
             "w_out", "g_ffn", "w_gate", "w_up", "w_down"]
    lead = {"w_ada", "w_in", "w_dw", "w_out", "w_gate", "w_up", "w_down"}
    grads, deltas, new_m, new_v = [], [], [], []
    for n in order:
        g, d, mn, vn = res[n]
        g, d, mn, vn = (t[None] if n in lead else t for t in (g, d, mn, vn))
        grads.append(g)
        deltas.append(d)
        new_m.append(mn)
        new_v.append(vn)
    return (loss, grad_x2.reshape(n_seq, seq, D_MODEL), *grads, *deltas, *new_m, *new_v)
```

```python
import numpy as np
import jax
import jax.numpy as jnp
from jax import lax
from jax.experimental import pallas as pl
from jax.experimental.pallas import tpu as pltpu

F32 = jnp.float32
BF16 = jnp.bfloat16

N_DEV = 8
D_MODEL = 1024
D_CONV = 512
D_ATT = 512
HEAD_DIM = 64
CONV_WIDTH = 31
D_IN = 2 * D_CONV + 3 * D_ATT
D_FF = 2816
N_MOD = 6
EPS = 1e-6
RADIUS = 64
DILATIONS = (1, 4, 16)
Q_BLOCK = 128
LANES = 128
VMEM_LIMIT = 56 * 1024 * 1024

ADAM_LR = 0.001
ADAM_B1 = 0.9
ADAM_B2 = 0.999
ADAM_EPS = 1e-08
ADAM_WD = 0.01
ADAM_STEP = 10

NT = (((1,), (1,)), ((), ()))
TN = (((0,), (0,)), ((), ()))


def _call(body, **kw):
    return pl.pallas_call(body, **kw)


def _params(sem=None, vmem=VMEM_LIMIT):
    return pltpu.CompilerParams(dimension_semantics=sem, vmem_limit_bytes=vmem)


def _sig(x):
    return 1.0 / (1.0 + jnp.exp(-x))


def _sds(shape, dtype):
    return jax.ShapeDtypeStruct(shape, dtype)


N_PEER = N_DEV - 1
ANY_SPEC = pl.BlockSpec(memory_space=pl.ANY)


def _exchange_copies(scatter, ins, outs, *sems):
    n = len(ins)
    if n == 0:
        return [], []
    send_sems, recv_sems, local_sems = sems
    x, y, c = lax.axis_index("x"), lax.axis_index("y"), lax.axis_index("c")
    me = 4 * x + 2 * y + c

    def src(a, slot):
        return ins[a].at[slot] if scatter[a] else ins[a]

    local = [pltpu.make_async_copy(src(a, me), outs[a].at[me], local_sems.at[a]) for a in range(n)]
    flights = []
    for k in range(1, N_DEV):
        px = 1 - x if k & 4 else x
        py = 1 - y if k & 2 else y
        pc = 1 - c if k & 1 else c
        pid = 4 * px + 2 * py + pc
        for a in range(n):
            i = a * N_PEER + k - 1
            send, recv = (pltpu.make_async_remote_copy(
                src_ref=src(a, pid), dst_ref=outs[a].at[slot],
                send_sem=send_sems.at[i], recv_sem=recv_sems.at[i],
                device_id=(px, py, pc), device_id_type=pl.DeviceIdType.MESH) for slot in (me, pid))
            flights.append((send, recv))
    return local, flights


def _exchange_start(*args):
    local, flights = _exchange_copies(*args)
    for cp in local:
        cp.start()
    for send, _ in flights:
        send.start()


def _exchange_wait(*args):
    local, flights = _exchange_copies(*args)
    for send, recv in flights:
        send.wait_send()
        recv.wait_recv()
    for cp in local:
        cp.wait()


def _exchange_shapes(items):
    return [_sds((N_DEV,) + tuple(arr.shape[1:] if scatter else arr.shape), arr.dtype) for arr, scatter in items]


def _exchange_sems(n):
    if n == 0:
        return []
    return [pltpu.SemaphoreType.DMA((n * N_PEER,)), pltpu.SemaphoreType.DMA((n * N_PEER,)),
            pltpu.SemaphoreType.DMA((n,))]


def _gather_by_chip(arrays, name):
    n = len(arrays)
    per = N_PEER

    def body(*refs):
        ins, outs = refs[:n], refs[n:2 * n]
        send_sems, recv_sems, local_sems = refs[2 * n:]
        x, y, c = lax.axis_index("x"), lax.axis_index("y"), lax.axis_index("c")
        sibling = (x, y, 1 - c)
        chips = [(1 - x, y), (x, 1 - y), (1 - x, 1 - y)]

        def slot(px, py, pc):
            return 4 * px + 2 * py + pc

        def copy(a, k, block, to, src=None):
            dst = outs[a].at[slot(*block)]
            return pltpu.make_async_remote_copy(
                src_ref=dst if src is None else src, dst_ref=dst,
                send_sem=send_sems.at[a * per + k], recv_sem=recv_sems.at[a * per + k],
                device_id=to, device_id_type=pl.DeviceIdType.MESH)

        me = (x, y, c)
        local = [pltpu.make_async_copy(ins[a], outs[a].at[slot(*me)], local_sems.at[a]) for a in range(n)]
        for cp in local:
            cp.start()
        first = []
        for a in range(n):
            first.append(copy(a, 0, me, sibling, src=ins[a]))
            first += [copy(a, 1 + j, me, (*chip, c), src=ins[a]) for j, chip in enumerate(chips)]
        for cp in first:
            cp.start()
        passed = []
        for j, chip in enumerate(chips):
            for a in range(n):
                copy(a, 1 + j, (*chip, c), me).wait_recv()
                fwd = copy(a, 4 + j, (*chip, c), sibling)
                fwd.start()
                passed.append(fwd)
        for a in range(n):
            copy(a, 0, sibling, me).wait_recv()
            for j, chip in enumerate(chips):
                copy(a, 4 + j, (*chip, 1 - c), me).wait_recv()
        for cp in first + passed:
            cp.wait_send()
        for cp in local:
            cp.wait()

    return _call(
        body, name=name, out_shape=_exchange_shapes([(arr, False) for arr in arrays]),
        in_specs=[ANY_SPEC] * n, out_specs=[ANY_SPEC] * n, scratch_shapes=_exchange_sems(n),
    )(*arrays)


def _exchange(items, name):
    n = len(items)
    scatter = [s for _, s in items]

    def body(*refs):
        args = (scatter, refs[:n], refs[n:2 * n]) + tuple(refs[2 * n:])
        _exchange_start(*args)
        _exchange_wait(*args)

    return _call(
        body, name=name, out_shape=_exchange_shapes(items),
        in_specs=[ANY_SPEC] * n, out_specs=[ANY_SPEC] * n, scratch_shapes=_exchange_sems(n),
    )(*[a for a, _ in items])


def _ada_fwd(c_all, w_ada, b_cols):
    def body(c_ref, w_ref, b_ref, o_ref):
        cv = c_ref[...]
        sc = (cv * _sig(cv)).astype(BF16)
        o_ref[...] = jnp.dot(sc, w_ref[...].astype(BF16), preferred_element_type=F32) + b_ref[...]

    return _call(body, name="ada_fwd", out_shape=_sds((c_all.shape[0], w_ada.shape[1]), F32),
                 compiler_params=_params())(c_all, w_ada, b_cols)


def _ada_bwd(c_all, dmod_cols, dmod_all):
    def body(c_ref, dc_ref, da_ref, gw_ref, gb_ref):
        cv = c_ref[...]
        sc = (cv * _sig(cv)).astype(BF16)
        gw_ref[...] = lax.dot_general(sc, dc_ref[...].astype(BF16), TN, preferred_element_type=F32)
        gb_ref[...] = jnp.sum(da_ref[...], axis=0, keepdims=True)

    return _call(body, name="ada_bwd",
                 out_shape=[_sds((c_all.shape[1], dmod_cols.shape[1]), F32), _sds((1, dmod_all.shape[1]), F32)],
                 compiler_params=_params())(c_all, dmod_cols, dmod_all)


MIX_ROWS = 128


def _mix_in(x2, mod8, g_mix, w_in, seq, tm=512):
    tokens = x2.shape[0]
    per_seq = seq // tm

    def body(x_ref, m_ref, g_ref, w_ref, h_ref, p_ref):
        def normed(c):
            rows = pl.ds(c * MIX_ROWS, MIX_ROWS)
            xv = x_ref[rows, :]
            r = lax.rsqrt(jnp.mean(xv * xv, axis=-1, keepdims=True) + EPS)
            hb = ((xv * r * g_ref[...]) * (1.0 + m_ref[1:2, :]) + m_ref[0:1, :]).astype(BF16)
            h_ref[rows, :] = hb
            return hb

        ahead = normed(0)
        for c in range(tm // MIX_ROWS):
            hb = ahead
            if c + 1 < tm // MIX_ROWS:
                ahead = normed(c + 1)
            p_ref[pl.ds(c * MIX_ROWS, MIX_ROWS), :] = jnp.dot(hb, w_ref[...], preferred_element_type=F32)

    return _call(
        body, name="mix_in", grid=(tokens // tm,),
        in_specs=[pl.BlockSpec((tm, D_MODEL), lambda i: (i, 0)),
                  pl.BlockSpec((None, 8, D_MODEL), lambda i: (i // per_seq, 0, 0)),
                  pl.BlockSpec((1, D_MODEL), lambda i: (0, 0)),
                  pl.BlockSpec((D_MODEL, D_IN), lambda i: (0, 0))],
        out_specs=[pl.BlockSpec((tm, D_MODEL), lambda i: (i, 0)),
                   pl.BlockSpec((tm, D_IN), lambda i: (i, 0))],
        out_shape=[_sds((tokens, D_MODEL), BF16), _sds((tokens, D_IN), F32)],
        compiler_params=_params(("parallel",)),
    )(x2, mod8, g_mix, w_in)


CONV_ROWS = 64
CONV_DW_ROWS = 32
CONV_DW_UNROLL = 4
CONV_HALO = 16


def _fill_shifted(xp, sh, seq):
    for b in range(8):
        sh[b, pl.ds(0, seq + 24), :] = xp[pl.ds(b, seq + 24), :]


def _conv_fwd(proj3, w_dw, b_dw):
    n_seq, seq, _ = proj3.shape
    n_cb = D_CONV // LANES

    def body(a_ref, g_ref, w_ref, b_ref, uc_ref, xp, sh):
        zeros = jnp.zeros((CONV_HALO, LANES), F32)
        xp[pl.ds(0, CONV_HALO), :] = zeros
        xp[pl.ds(CONV_HALO + seq, CONV_HALO), :] = zeros
        xp[pl.ds(CONV_HALO, seq), :] = a_ref[...] * _sig(g_ref[...])
        _fill_shifted(xp, sh, seq)

        def blk(i, carry):
            t0 = pl.multiple_of(i * CONV_ROWS, CONV_ROWS)
            acc = jnp.zeros((CONV_ROWS, LANES), F32)
            for j in range(CONV_WIDTH):
                jj = j + 1
                acc = acc + sh[jj % 8, pl.ds(t0 + 8 * (jj // 8), CONV_ROWS), :] * w_ref[j:j + 1, :]
            uc_ref[pl.ds(t0, CONV_ROWS), :] = acc + b_ref[...]
            return carry

        lax.fori_loop(0, seq // CONV_ROWS, blk, 0)

    return _call(
        body, name="conv_fwd", grid=(n_seq, n_cb),
        in_specs=[pl.BlockSpec((None, seq, LANES), lambda b, cb: (b, 0, cb)),
                  pl.BlockSpec((None, seq, LANES), lambda b, cb: (b, 0, n_cb + cb)),
                  pl.BlockSpec((CONV_WIDTH, LANES), lambda b, cb: (0, cb)),
                  pl.BlockSpec((1, LANES), lambda b, cb: (0, cb))],
        out_specs=pl.BlockSpec((None, seq, LANES), lambda b, cb: (b, 0, cb)),
        out_shape=_sds((n_seq, seq, D_CONV), F32),
        scratch_shapes=[pltpu.VMEM((seq + 2 * CONV_HALO, LANES), F32),
                        pltpu.VMEM((8, seq + 2 * CONV_HALO, LANES), F32)],
        compiler_params=_params(("parallel", "parallel")),
    )(proj3, proj3, w_dw, b_dw)


def _conv_bwd(duc3, proj3, w_dw):
    n_seq, seq, _ = proj3.shape
    n_cb = D_CONV // LANES

    def body(duc_ref, a_ref, g_ref, w_ref, da_ref, dg_ref, dw_ref, db_ref, xp, sh):
        @pl.when(pl.program_id(1) == 0)
        def _():
            dw_ref[...] = jnp.zeros_like(dw_ref)
            db_ref[...] = jnp.zeros_like(db_ref)

        zeros = jnp.zeros((CONV_HALO, LANES), F32)
        xp[pl.ds(0, CONV_HALO), :] = zeros
        xp[pl.ds(CONV_HALO + seq, CONV_HALO), :] = zeros
        xp[pl.ds(CONV_HALO, seq), :] = a_ref[...] * _sig(g_ref[...])
        _fill_shifted(xp, sh, seq)
        for j0 in range(0, CONV_WIDTH, 8):
            taps = range(j0, min(j0 + 8, CONV_WIDTH))

            def wblk(i, accs, taps=taps):
                for u in range(CONV_DW_UNROLL):
                    t0 = pl.multiple_of((i * CONV_DW_UNROLL + u) * CONV_DW_ROWS, CONV_DW_ROWS)
                    d = duc_ref[pl.ds(t0, CONV_DW_ROWS), :]
                    accs = tuple(acc + d * sh[(j + 1) % 8, pl.ds(t0 + 8 * ((j + 1) // 8), CONV_DW_ROWS), :]
                                 for acc, j in zip(accs, taps))
                return accs

            accs = lax.fori_loop(0, seq // (CONV_DW_ROWS * CONV_DW_UNROLL), wblk,
                                 tuple(jnp.zeros((CONV_DW_ROWS, LANES), F32) for _ in taps))
            for acc, j in zip(accs, taps):
                dw_ref[j:j + 1, :] += jnp.sum(acc, axis=0, keepdims=True)
        db_ref[0:1, :] += jnp.sum(duc_ref[...], axis=0, keepdims=True)
        xp[pl.ds(CONV_HALO, seq), :] = duc_ref[...]
        _fill_shifted(xp, sh, seq)

        def ublk(i, carry):
            t0 = pl.multiple_of(i * CONV_ROWS, CONV_ROWS)
            acc = jnp.zeros((CONV_ROWS, LANES), F32)
            for j in range(CONV_WIDTH):
                jj = CONV_WIDTH - j
                acc = acc + sh[jj % 8, pl.ds(t0 + 8 * (jj // 8), CONV_ROWS), :] * w_ref[j:j + 1, :]
            av = a_ref[pl.ds(t0, CONV_ROWS), :]
            sg = _sig(g_ref[pl.ds(t0, CONV_ROWS), :])
            da_ref[pl.ds(t0, CONV_ROWS), :] = (acc * sg).astype(BF16)
            dg_ref[pl.ds(t0, CONV_ROWS), :] = (acc * av * sg * (1.0 - sg)).astype(BF16)
            return carry

        lax.fori_loop(0, seq // CONV_ROWS, ublk, 0)

    return _call(
        body, name="conv_bwd", grid=(n_cb, n_seq),
        in_specs=[pl.BlockSpec((None, seq, LANES), lambda cb, b: (b, 0, cb)),
                  pl.BlockSpec((None, seq, LANES), lambda cb, b: (b, 0, cb)),
                  pl.BlockSpec((None, seq, LANES), lambda cb, b: (b, 0, n_cb + cb)),
                  pl.BlockSpec((CONV_WIDTH, LANES), lambda cb, b: (0, cb))],
        out_specs=[pl.BlockSpec((None, seq, LANES), lambda cb, b: (b, 0, cb)),
                   pl.BlockSpec((None, seq, LANES), lambda cb, b: (b, 0, cb)),
                   pl.BlockSpec((32, LANES), lambda cb, b: (0, cb)),
                   pl.BlockSpec((8, LANES), lambda cb, b: (0, cb))],
        out_shape=[_sds((n_seq, seq, D_CONV), BF16), _sds((n_seq, seq, D_CONV), BF16),
                   _sds((32, D_CONV), F32), _sds((8, D_CONV), F32)],
        scratch_shapes=[pltpu.VMEM((seq + 2 * CONV_HALO, LANES), F32),
                        pltpu.VMEM((8, seq + 2 * CONV_HALO, LANES), F32)],
        compiler_params=_params(("parallel", "arbitrary")),
    )(duc3, proj3, proj3, w_dw)


MASKED = 1e30
ATT_ROWS = 512
ATT_UNROLL = 4


def _distance_mats(dil, seg_len):
    kw = min(2 * Q_BLOCK, seg_len)
    offsets = (0, -RADIUS, -2 * RADIUS) if kw == 2 * Q_BLOCK else (0,)
    a = np.arange(Q_BLOCK)[:, None]
    b = np.arange(kw)[None, :]
    mats = []
    for off in offsets:
        rel = np.abs(b + off - a)
        mats.append(np.where(rel <= RADIUS, dil * rel, MASKED))
    return jnp.asarray(np.stack(mats).astype(np.float32))


def _alibi_rows():
    s = np.zeros((4, 8, LANES), np.float32)
    for hp in range(4):
        for hl in range(2):
            s[hp, hl, :] = 2.0 ** (-(2 * hp + hl + 1))
    return jnp.asarray(s)


def _window(n, seg_len):
    i0 = pl.multiple_of(n * Q_BLOCK, Q_BLOCK)
    if seg_len <= Q_BLOCK:
        return i0, i0, 0
    per_seg = seg_len // Q_BLOCK
    j = n % per_seg
    seg0 = (n // per_seg) * seg_len
    ks_local = jnp.clip(j * Q_BLOCK - RADIUS, 0, seg_len - 2 * Q_BLOCK)
    ks = pl.multiple_of(seg0 + ks_local, RADIUS)
    var = jnp.where(j == 0, 0, jnp.where(j == per_seg - 1, 2, 1))
    return i0, ks, var


def _first_head(rows):
    return lax.broadcasted_iota(jnp.int32, (rows, LANES), 1) < HEAD_DIM


def _same_head():
    head = np.arange(LANES) // HEAD_DIM
    return jnp.asarray((head[:, None] == head[None, :]).astype(np.float32)).astype(BF16)


def _head_sum(x, same_ref):
    hi = x.astype(BF16)
    lo = (x - hi.astype(F32)).astype(BF16)
    return (jnp.dot(hi, same_ref[...], preferred_element_type=F32)
            + jnp.dot(lo, same_ref[...], preferred_element_type=F32))


def _head_mean(x, same_ref):
    return _head_sum(x, same_ref) * (1.0 / HEAD_DIM)


def _per_head(x, first):
    swapped = pltpu.roll(x, HEAD_DIM, 1)
    return jnp.where(first, x, swapped), jnp.where(first, swapped, x)


STRIDE = 4


def _gather_segments(src, dil, seq, tmp, put):
    if dil == 1:
        put(0, seq, src[pl.ds(0, seq), :])
    elif dil == STRIDE:
        seg = seq // dil
        for r in range(dil):
            put(r * seg, seg, src[pl.ds(r, seg, stride=dil), :])
    else:
        part, seg = seq // STRIDE, seq // dil
        for b in range(STRIDE):
            tmp[pl.ds(b * part, part), :] = src[pl.ds(b, part, stride=STRIDE), :]
        for b in range(STRIDE):
            for a in range(dil // STRIDE):
                put(b * part + a * seg, seg, tmp[pl.ds(b * part + a, seg, stride=dil // STRIDE), :])


def _scatter_segments(dst, get, dil, seq, tmp, accumulate):
    def write(rows, val):
        if accumulate:
            dst[rows, :] += val
        else:
            dst[rows, :] = val

    if dil == 1:
        write(pl.ds(0, seq), get(0, seq))
    elif dil == STRIDE:
        seg = seq // dil
        for r in range(dil):
            write(pl.ds(r, seg, stride=dil), get(r * seg, seg))
    else:
        part, seg = seq // STRIDE, seq // dil
        for b in range(STRIDE):
            for a in range(dil // STRIDE):
                tmp[pl.ds(b * part + a, seg, stride=dil // STRIDE), :] = get(b * part + a * seg, seg)
        for b in range(STRIDE):
            write(pl.ds(b, part, stride=STRIDE), tmp[pl.ds(b * part, part), :])


def _permute_rows(dst, src, dil, seq, tmp):
    def put(start, size, val):
        dst[pl.ds(start, size), :] = val.astype(dst.dtype)

    _gather_segments(src, dil, seq, tmp, put)


def _permute_rows_by_head(dst, src, dil, seq, tmp):
    def put(start, size, val):
        first = _first_head(size)
        dst[0, pl.ds(start, size), :] = jnp.where(first, val, 0.0).astype(dst.dtype)
        dst[1, pl.ds(start, size), :] = jnp.where(first, 0.0, val).astype(dst.dtype)

    _gather_segments(src, dil, seq, tmp, put)


def _qk_normalise(q_ref, g2_ref, same_ref, dst, seq, scale):
    def chunk(ci, carry):
        rows = pl.ds(pl.multiple_of(ci * ATT_ROWS, ATT_ROWS), ATT_ROWS)
        qv = q_ref[rows, :]
        r = lax.rsqrt(_head_mean(qv * qv, same_ref) + EPS)
        dst[rows, :] = qv * r * (g2_ref[...] * scale)
        return carry

    lax.fori_loop(0, seq // ATT_ROWS, chunk, 0)


def _attn_fwd(proj3, g_q2, g_k2, ride):
    n_seq, seq, _ = proj3.shape
    dms = [_distance_mats(d, seq // d) for d in DILATIONS]
    same = _same_head()
    col0 = 2 * D_CONV // LANES
    n_hp = D_ATT // LANES

    n_ride = len(ride)
    ride_scatter = [s for _, s in ride]

    def body(*refs):
        q_ref, k_ref, v_ref, gq_ref, gk_ref, sl_ref, dm1, dm4, dm16, same_ref = refs[:10]
        ride_in = refs[10:10 + n_ride]
        y_ref, lse_ref = refs[10 + n_ride:12 + n_ride]
        ride_out = refs[12 + n_ride:12 + 2 * n_ride]
        (qf, kf, qp, kp, vp, oml_p, o1, o4, o16, m1, m4, m16, l1, l4, l16,
         tmp) = refs[12 + 2 * n_ride:28 + 2 * n_ride]
        o_nat, m_nat, l_nat = (o1, o4, o16), (m1, m4, m16), (l1, l4, l16)
        ride_args = (ride_scatter, ride_in, ride_out) + tuple(refs[28 + 2 * n_ride:])
        step = pl.program_id(0) * n_hp + pl.program_id(1)

        @pl.when(step == 0)
        def _():
            _exchange_start(*ride_args)

        dm_refs = (dm1, dm4, dm16)
        _qk_normalise(q_ref, gq_ref, same_ref, qf, seq, HEAD_DIM ** -0.5)
        _qk_normalise(k_ref, gk_ref, same_ref, kf, seq, 1.0)
        slopes = (sl_ref[0:1, 0:1], sl_ref[1:2, 0:1])
        for pi, dil in enumerate(DILATIONS):
            seg = seq // dil
            kw = min(2 * Q_BLOCK, seg)
            _permute_rows_by_head(qp, qf, dil, seq, tmp)
            _permute_rows(kp, kf, dil, seq, tmp)
            _permute_rows(vp, v_ref, dil, seq, tmp)

            def blk(it, carry, seg=seg, kw=kw, pi=pi, dst=oml_p):
                first = _first_head(Q_BLOCK)
                chains = [(sub, h) for sub in range(ATT_UNROLL) for h in range(2)]
                win = [_window(it * ATT_UNROLL + sub, seg) for sub in range(ATT_UNROLL)]
                s = {}
                for sub, h in chains:
                    i0, ks, var = win[sub]
                    s[sub, h] = lax.dot_general(qp[h, pl.ds(i0, Q_BLOCK), :], kp[pl.ds(ks, kw), :], NT,
                                                preferred_element_type=F32) - slopes[h] * dm_refs[pi][var]
                m, l, p = {}, {}, {}
                for c in chains:
                    m[c] = jnp.max(s[c], axis=1, keepdims=True)
                    e = jnp.exp(s[c] - m[c])
                    l[c] = jnp.sum(e, axis=1, keepdims=True)
                    p[c] = e.astype(BF16)
                o = {}
                for sub, h in chains:
                    o[sub, h] = jnp.dot(p[sub, h], vp[pl.ds(win[sub][1], kw), :], preferred_element_type=F32)
                packed = [jnp.concatenate([jnp.where(first, t[sub, 0], t[sub, 1]) for t in (o, m, l)], axis=1)
                          for sub in range(ATT_UNROLL)]
                rows = pl.ds(pl.multiple_of(it * (ATT_UNROLL * Q_BLOCK), ATT_UNROLL * Q_BLOCK), ATT_UNROLL * Q_BLOCK)
                dst[rows, :] = jnp.concatenate(packed, axis=0)
                return carry

            lax.fori_loop(0, seq // (Q_BLOCK * ATT_UNROLL), blk, 0)
            for n, nat in enumerate((o_nat[pi], m_nat[pi], l_nat[pi])):
                _scatter_segments(nat, lambda start, size, n=n: oml_p[pl.ds(start, size), pl.ds(n * LANES, LANES)],
                                  dil, seq, tmp, accumulate=False)

        def merge(ci, carry):
            rows = pl.ds(pl.multiple_of(ci * ATT_ROWS, ATT_ROWS), ATT_ROWS)
            ms = [m_nat[pi][rows, :] for pi in range(3)]
            m_all = jnp.maximum(jnp.maximum(ms[0], ms[1]), ms[2])
            es = [jnp.exp(m - m_all) for m in ms]
            l_all = sum(l_nat[pi][rows, :] * es[pi] for pi in range(3))
            inv = 1.0 / l_all
            o = sum(o_nat[pi][rows, :] * (es[pi] * inv) for pi in range(3))
            y_ref[rows, :] = o.astype(BF16)
            lse_ref[rows, :] = m_all + jnp.log(l_all)
            return carry

        lax.fori_loop(0, seq // ATT_ROWS, merge, 0)

        @pl.when(step == n_seq * n_hp - 1)
        def _():
            _exchange_wait(*ride_args)

    def col(off):
        return pl.BlockSpec((None, seq, LANES), lambda b, hp: (b, 0, col0 + off * n_hp + hp))

    def whole(arr):
        return pl.BlockSpec(arr.shape, lambda b, hp: (0,) * arr.ndim)

    rows_f32 = pltpu.VMEM((seq, LANES), F32)
    rows_bf16 = pltpu.VMEM((seq, LANES), BF16)
    return _call(
        body, name="attn_fwd", grid=(n_seq, n_hp),
        in_specs=[col(0), col(1), col(2), whole(g_q2), whole(g_k2),
                  pl.BlockSpec((None, 8, LANES), lambda b, hp: (hp, 0, 0)),
                  whole(dms[0]), whole(dms[1]), whole(dms[2]), whole(same)] + [ANY_SPEC] * n_ride,
        out_specs=[pl.BlockSpec((None, seq, LANES), lambda b, hp: (b, 0, hp)),
                   pl.BlockSpec((None, seq, LANES), lambda b, hp: (b, 0, hp))] + [ANY_SPEC] * n_ride,
        out_shape=[_sds((n_seq, seq, D_ATT), BF16), _sds((n_seq, seq, D_ATT), F32)] + _exchange_shapes(ride),
        scratch_shapes=[rows_f32, rows_f32, pltpu.VMEM((2, seq, LANES), BF16), rows_bf16, rows_bf16]
        + [pltpu.VMEM((seq, 3 * LANES), F32)] + [rows_f32] * 10 + _exchange_sems(n_ride),
        compiler_params=_params(("arbitrary", "arbitrary")),
    )(proj3, proj3, proj3, g_q2, g_k2, _alibi_rows(), *dms, same, *[a for a, _ in ride])


def _attn_bwd(proj3, do3, y_att3, lse3, g_q2, g_k2, ride):
    n_seq, seq, _ = proj3.shape
    dms = [_distance_mats(d, seq // d) for d in DILATIONS]
    same = _same_head()
    col0 = 2 * D_CONV // LANES
    n_hp = D_ATT // LANES

    n_ride = len(ride)
    ride_scatter = [s for _, s in ride]

    def body(*refs):
        (q_ref, k_ref, v_ref, do_ref, o_ref, lse_ref, gq_ref, gk_ref, sl_ref, dm1, dm4, dm16,
         same_ref) = refs[:13]
        ride_in = refs[13:13 + n_ride]
        dq_ref, dk_ref, dv_ref, dg_ref = refs[13 + n_ride:17 + n_ride]
        ride_out = refs[17 + n_ride:17 + 2 * n_ride]
        (qf, kf, qp, dop, kp, vp, sn, sp, dqp, dkp, dvp, dqn, dkn, dvn,
         tmp) = refs[17 + 2 * n_ride:32 + 2 * n_ride]
        ride_args = (ride_scatter, ride_in, ride_out) + tuple(refs[32 + 2 * n_ride:])
        dm_refs = (dm1, dm4, dm16)
        step = pl.program_id(0) * n_hp + pl.program_id(1)

        @pl.when(step == 0)
        def _():
            _exchange_start(*ride_args)
            dg_ref[...] = jnp.zeros_like(dg_ref)

        _qk_normalise(q_ref, gq_ref, same_ref, qf, seq, HEAD_DIM ** -0.5)
        _qk_normalise(k_ref, gk_ref, same_ref, kf, seq, 1.0)

        def stats(ci, carry):
            rows = pl.ds(pl.multiple_of(ci * ATT_ROWS, ATT_ROWS), ATT_ROWS)
            first = _first_head(ATT_ROWS)
            sn[0, rows, :], sn[1, rows, :] = _per_head(lse_ref[rows, :], first)
            prod = do_ref[rows, :] * o_ref[rows, :].astype(F32)
            sn[2, rows, :], sn[3, rows, :] = _per_head(_head_sum(prod, same_ref), first)
            return carry

        lax.fori_loop(0, seq // ATT_ROWS, stats, 0)
        slopes = (sl_ref[0:1, 0:1], sl_ref[1:2, 0:1])
        half = seq // (Q_BLOCK * ATT_UNROLL)
        region = seq // ATT_UNROLL

        for pi, dil in enumerate(DILATIONS):
            seg = seq // dil
            kw = min(2 * Q_BLOCK, seg)
            _permute_rows_by_head(qp, qf, dil, seq, tmp)
            _permute_rows_by_head(dop, do_ref, dil, seq, tmp)
            _permute_rows(kp, kf, dil, seq, tmp)
            _permute_rows(vp, v_ref, dil, seq, tmp)
            if dil == 1:
                st = sn
            else:
                st = sp
                for n in range(4):
                    _permute_rows(sp.at[n], sn.at[n], dil, seq, tmp)
            for sub in range(ATT_UNROLL):
                lo, hi = max(sub * region - RADIUS, 0), min((sub + 1) * region + RADIUS, seq)
                dkp[sub, pl.ds(lo, hi - lo), :] = jnp.zeros((hi - lo, LANES), F32)
                dvp[sub, pl.ds(lo, hi - lo), :] = jnp.zeros((hi - lo, LANES), F32)

            def blk(it, carry, seg=seg, kw=kw, pi=pi, st=st):
                first = _first_head(Q_BLOCK)
                chains = [(sub, h) for sub in range(ATT_UNROLL) for h in range(2)]
                win = [_window(it + sub * half, seg) for sub in range(ATT_UNROLL)]
                qrows = [pl.ds(w[0], Q_BLOCK) for w in win]
                krows = [pl.ds(w[1], kw) for w in win]

                def over_keys(n, sub):
                    t = st[n, qrows[sub], :]
                    return t if kw == LANES else jnp.concatenate([t] * (kw // LANES), axis=1)

                s, dp = {}, {}
                for sub, h in chains:
                    s[sub, h] = lax.dot_general(qp[h, qrows[sub], :], kp[krows[sub], :], NT,
                                                preferred_element_type=F32) - slopes[h] * dm_refs[pi][win[sub][2]]
                    dp[sub, h] = lax.dot_general(dop[h, qrows[sub], :], vp[krows[sub], :], NT,
                                                 preferred_element_type=F32)
                p, ds = {}, {}
                for sub, h in chains:
                    e = jnp.exp(s[sub, h] - over_keys(h, sub))
                    ds[sub, h] = (e * (dp[sub, h] - over_keys(2 + h, sub))).astype(BF16)
                    p[sub, h] = e.astype(BF16)
                dq, dk, dv = {}, {}, {}
                for sub, h in chains:
                    dq[sub, h] = jnp.dot(ds[sub, h], kp[krows[sub], :], preferred_element_type=F32)
                    dk[sub, h] = lax.dot_general(ds[sub, h], qp[h, qrows[sub], :], TN, preferred_element_type=F32)
                    dv[sub, h] = lax.dot_general(p[sub, h], dop[h, qrows[sub], :], TN, preferred_element_type=F32)
                for sub in range(ATT_UNROLL):
                    dqp[qrows[sub], :] = jnp.where(first, dq[sub, 0], dq[sub, 1])
                    dkp[sub, krows[sub], :] += dk[sub, 0] + dk[sub, 1]
                    dvp[sub, krows[sub], :] += dv[sub, 0] + dv[sub, 1]
                return carry

            lax.fori_loop(0, half, blk, 0)
            if dil == 1:
                for sub in range(ATT_UNROLL):
                    r0 = sub * region
                    pieces = [(r0, RADIUS, [sub - 1, sub] if sub > 0 else [sub]),
                              (r0 + RADIUS, region - 2 * RADIUS, [sub]),
                              (r0 + region - RADIUS, RADIUS, [sub, sub + 1] if sub < ATT_UNROLL - 1 else [sub])]
                    for start, size, owners in pieces:
                        rows = pl.ds(start, size)
                        dqn[rows, :] = dqp[rows, :]
                        dkn[rows, :] = sum(dkp[o, rows, :] for o in owners)
                        dvn[rows, :] = sum(dvp[o, rows, :] for o in owners)
            else:
                _scatter_segments(dqn, lambda start, size: dqp[pl.ds(start, size), :], dil, seq, tmp, accumulate=True)
                for nat, acc in ((dkn, dkp), (dvn, dvp)):
                    _scatter_segments(nat, lambda start, size, acc=acc: acc[start // region, pl.ds(start, size), :],
                                      dil, seq, tmp, accumulate=True)

        def finish(ci, carry):
            rows = pl.ds(pl.multiple_of(ci * ATT_ROWS, ATT_ROWS), ATT_ROWS)
            for src_ref, g_ref, dn, dst_ref, scale, row in (
                    (q_ref, gq_ref, dqn, dq_ref, HEAD_DIM ** -0.5, 0), (k_ref, gk_ref, dkn, dk_ref, 1.0, 1)):
                xv = src_ref[rows, :]
                r = lax.rsqrt(_head_mean(xv * xv, same_ref) + EPS)
                xhat = xv * r
                d = dn[rows, :] * scale
                dg_ref[row:row + 1, :] += jnp.sum(d * xhat, axis=0, keepdims=True)
                dxh = d * g_ref[...]
                dst_ref[rows, :] = (r * (dxh - xhat * _head_mean(dxh * xhat, same_ref))).astype(BF16)
            dv_ref[rows, :] = dvn[rows, :].astype(BF16)
            return carry

        lax.fori_loop(0, seq // ATT_ROWS, finish, 0)

        @pl.when(step == n_seq * n_hp - 1)
        def _():
            _exchange_wait(*ride_args)

    def col(off):
        return pl.BlockSpec((None, seq, LANES), lambda b, hp: (b, 0, col0 + off * n_hp + hp))

    def whole(arr):
        return pl.BlockSpec(arr.shape, lambda b, hp: (0,) * arr.ndim)

    att = pl.BlockSpec((None, seq, LANES), lambda b, hp: (b, 0, hp))
    rows_f32 = pltpu.VMEM((seq, LANES), F32)
    rows_bf16 = pltpu.VMEM((seq, LANES), BF16)
    by_head_bf16 = pltpu.VMEM((2, seq, LANES), BF16)
    per_sub_f32 = pltpu.VMEM((ATT_UNROLL, seq, LANES), F32)
    stats_f32 = pltpu.VMEM((4, seq, LANES), F32)
    return _call(
        body, name="attn_bwd", grid=(n_seq, n_hp),
        in_specs=[col(0), col(1), col(2), att, att, att, whole(g_q2), whole(g_k2),
                  pl.BlockSpec((None, 8, LANES), lambda b, hp: (hp, 0, 0)),
                  whole(dms[0]), whole(dms[1]), whole(dms[2]), whole(same)] + [ANY_SPEC] * n_ride,
        out_specs=[att, att, att, pl.BlockSpec((8, LANES), lambda b, hp: (0, 0))] + [ANY_SPEC] * n_ride,
        out_shape=[_sds((n_seq, seq, D_ATT), BF16)] * 3 + [_sds((8, LANES), F32)] + _exchange_shapes(ride),
        scratch_shapes=[rows_f32, rows_f32, by_head_bf16, by_head_bf16, rows_bf16, rows_bf16, stats_f32, stats_f32,
                        rows_f32, per_sub_f32, per_sub_f32, rows_f32, rows_f32, rows_f32, rows_f32]
        + _exchange_sems(n_ride),
        compiler_params=_params(("arbitrary", "arbitrary")),
    )(proj3, proj3, proj3, do3, y_att3, lse3, g_q2, g_k2, _alibi_rows(), *dms, same, *[a for a, _ in ride])


def _mix_out(uc2, y_att2, x2, mod8, g_ln, b_ln, g_ffn, w_out, seq, tm=512):
    tokens = x2.shape[0]
    per_seq = seq // tm

    def body(uc_ref, ya_ref, x_ref, m_ref, gl_ref, bl_ref, gf_ref, w_ref, yc_ref, mix_ref, x1_ref, h2_ref):
        uc = uc_ref[...]
        mu = jnp.mean(uc, axis=-1, keepdims=True)
        cen = uc - mu
        rs = lax.rsqrt(jnp.mean(cen * cen, axis=-1, keepdims=True) + EPS)
        z = cen * rs * gl_ref[...] + bl_ref[...]
        yc = (z * _sig(z)).astype(BF16)
        yc_ref[...] = yc
        mix = (jnp.dot(yc, w_ref[pl.ds(0, D_CONV), :], preferred_element_type=F32)
               + jnp.dot(ya_ref[...], w_ref[pl.ds(D_CONV, D_ATT), :], preferred_element_type=F32))
        mix_ref[...] = mix
        x1 = x_ref[...] + m_ref[2:3, :] * mix
        x1_ref[...] = x1
        r = lax.rsqrt(jnp.mean(x1 * x1, axis=-1, keepdims=True) + EPS)
        h2_ref[...] = ((x1 * r * gf_ref[...]) * (1.0 + m_ref[4:5, :]) + m_ref[3:4, :]).astype(BF16)

    def rows(width):
        return pl.BlockSpec((tm, width), lambda i: (i, 0))

    def vec(width):
        return pl.BlockSpec((1, width), lambda i: (0, 0))

    return _call(
        body, name="mix_out", grid=(tokens // tm,),
        in_specs=[rows(D_CONV), rows(D_ATT), rows(D_MODEL),
                  pl.BlockSpec((None, 8, D_MODEL), lambda i: (i // per_seq, 0, 0)),
                  vec(D_CONV), vec(D_CONV), vec(D_MODEL),
                  pl.BlockSpec((D_MODEL, D_MODEL), lambda i: (0, 0))],
        out_specs=[rows(D_CONV), rows(D_MODEL), rows(D_MODEL), rows(D_MODEL)],
        out_shape=[_sds((tokens, D_CONV), BF16), _sds((tokens, D_MODEL), F32),
                   _sds((tokens, D_MODEL), F32), _sds((tokens, D_MODEL), BF16)],
        compiler_params=_params(("parallel",)),
    )(uc2, y_att2, x2, mod8, g_ln, b_ln, g_ffn, w_out)


def _mix_out_bwd(dmix, uc2, g_ln, b_ln, w_out, tm=512):
    tokens = dmix.shape[0]

    def body(dm_ref, uc_ref, gl_ref, bl_ref, w_ref, duc_ref, do_ref, dgb_ref):
        @pl.when(pl.program_id(0) == 0)
        def _():
            dgb_ref[...] = jnp.zeros_like(dgb_ref)

        dmv = dm_ref[...]
        dyc = lax.dot_general(dmv, w_ref[pl.ds(0, D_CONV), :], NT, preferred_element_type=F32)
        do_ref[...] = lax.dot_general(dmv, w_ref[pl.ds(D_CONV, D_ATT), :], NT, preferred_element_type=F32)
        uc = uc_ref[...]
        mu = jnp.mean(uc, axis=-1, keepdims=True)
        cen = uc - mu
        rs = lax.rsqrt(jnp.mean(cen * cen, axis=-1, keepdims=True) + EPS)
        xh = cen * rs
        z = xh * gl_ref[...] + bl_ref[...]
        sg = _sig(z)
        dz = dyc * (sg * (1.0 + z * (1.0 - sg)))
        dgb_ref[0:1, :] += jnp.sum(dz * xh, axis=0, keepdims=True)
        dgb_ref[1:2, :] += jnp.sum(dz, axis=0, keepdims=True)
        dxh = dz * gl_ref[...]
        duc_ref[...] = rs * (dxh - jnp.mean(dxh, axis=-1, keepdims=True)
                             - xh * jnp.mean(dxh * xh, axis=-1, keepdims=True))

    return _call(
        body, name="mix_out_bwd", grid=(tokens // tm,),
        in_specs=[pl.BlockSpec((tm, D_MODEL), lambda i: (i, 0)),
                  pl.BlockSpec((tm, D_CONV), lambda i: (i, 0)),
                  pl.BlockSpec((1, D_CONV), lambda i: (0, 0)),
                  pl.BlockSpec((1, D_CONV), lambda i: (0, 0)),
                  pl.BlockSpec((D_MODEL, D_MODEL), lambda i: (0, 0))],
        out_specs=[pl.BlockSpec((tm, D_CONV), lambda i: (i, 0)),
                   pl.BlockSpec((tm, D_ATT), lambda i: (i, 0)),
                   pl.BlockSpec((8, D_CONV), lambda i: (0, 0))],
        out_shape=[_sds((tokens, D_CONV), F32), _sds((tokens, D_ATT), F32), _sds((8, D_CONV), F32)],
        compiler_params=_params(("arbitrary",)),
    )(dmix, uc2, g_ln, b_ln, w_out)


FF_TILE = 256
FF_ROWS = 256


def _ffn_fwd(h2, w_gate, w_up, w_down, x1, target, mod8, seq, tm=1024):
    tokens = h2.shape[0]
    per_seq = seq // tm
    n_seq = tokens // seq
    last = D_FF // FF_TILE - 1

    def body(h_ref, wg_ref, wu_ref, wd_ref, x1_ref, t_ref, m_ref, gate_ref, up_ref, dy_ref, df_ref, sq_ref, dgf_ref,
             f_ref):
        i, j = pl.program_id(0), pl.program_id(1)

        @pl.when(j == 0)
        def _():
            f_ref[...] = jnp.zeros_like(f_ref)

        @pl.when((j == 0) & (i == 0))
        def _():
            sq_ref[...] = jnp.zeros_like(sq_ref)

        @pl.when((j == 0) & (i % per_seq == 0))
        def _():
            dgf_ref[...] = jnp.zeros_like(dgf_ref)

        def gate_up(r):
            hv = h_ref[pl.ds(r * FF_ROWS, FF_ROWS), :]
            return (jnp.dot(hv, wg_ref[...], preferred_element_type=F32),
                    jnp.dot(hv, wu_ref[...], preferred_element_type=F32))

        ahead = gate_up(0)
        for r in range(tm // FF_ROWS):
            gate, up = ahead
            if r + 1 < tm // FF_ROWS:
                ahead = gate_up(r + 1)
            rows = pl.ds(r * FF_ROWS, FF_ROWS)
            gate_ref[rows, :] = gate
            up_ref[rows, :] = up
            act = (gate * _sig(gate) * up).astype(BF16)
            f_ref[rows, :] += jnp.dot(act, wd_ref[...], preferred_element_type=F32)

        @pl.when(j == last)
        def _():
            gate_f = m_ref[5:6, :]
            for r in range(tm // FF_ROWS):
                rows = pl.ds(r * FF_ROWS, FF_ROWS)
                fv = f_ref[rows, :]
                diff = x1_ref[rows, :] + gate_f * fv - t_ref[rows, :]
                sq_ref[0:1, :] += jnp.sum(diff * diff, axis=0, keepdims=True)
                dy = diff * (1.0 / D_MODEL)
                dy_ref[rows, :] = dy
                df_ref[rows, :] = (gate_f * dy).astype(BF16)
                dgf_ref[0:1, :] += jnp.sum(dy * fv, axis=0, keepdims=True)

    rows_spec = pl.BlockSpec((tm, D_MODEL), lambda i, j: (i, 0))
    per = pl.BlockSpec((None, 8, D_MODEL), lambda i, j: (i // per_seq, 0, 0))
    tile = pl.BlockSpec((tm, FF_TILE), lambda i, j: (i, j))
    return _call(
        body, name="ffn_fwd", grid=(tokens // tm, D_FF // FF_TILE),
        in_specs=[rows_spec,
                  pl.BlockSpec((D_MODEL, FF_TILE), lambda i, j: (0, j)),
                  pl.BlockSpec((D_MODEL, FF_TILE), lambda i, j: (0, j)),
                  pl.BlockSpec((FF_TILE, D_MODEL), lambda i, j: (j, 0)),
                  rows_spec, rows_spec, per],
        out_specs=[tile, tile, rows_spec, rows_spec, pl.BlockSpec((8, D_MODEL), lambda i, j: (0, 0)), per],
        out_shape=[_sds((tokens, D_FF), F32), _sds((tokens, D_FF), F32), _sds((tokens, D_MODEL), F32),
                   _sds((tokens, D_MODEL), BF16), _sds((8, D_MODEL), F32), _sds((n_seq, 8, D_MODEL), F32)],
        scratch_shapes=[pltpu.VMEM((tm, D_MODEL), F32)],
        compiler_params=_params(("arbitrary", "arbitrary")),
    )(h2, w_gate, w_up, w_down, x1, target, mod8)


def _ffn_bwd(df, gate, up, w_gate, w_up, w_down, x1, dy, mix, mod8, g_ffn, seq, tm=1024):
    tokens = df.shape[0]
    per_seq = seq // tm
    n_seq = tokens // seq
    last = D_FF // FF_TILE - 1

    def body(df_ref, gate_ref, up_ref, wg_ref, wu_ref, wd_ref, m_ref, g_ref, x1_hbm, dy_hbm, mix_hbm,
             dgate_ref, dup_ref, act_ref, dx1_ref, dmix_ref, dg_ref, dm_ref, dh_ref, late, late_sems):
        i, j = pl.program_id(0), pl.program_id(1)
        my_rows = pl.ds(pl.multiple_of(i * tm, tm), tm)
        fetches = [pltpu.make_async_copy(src.at[my_rows, :], late.at[n], late_sems.at[n])
                   for n, src in enumerate((x1_hbm, dy_hbm, mix_hbm))]

        @pl.when(j == 0)
        def _():
            dh_ref[...] = jnp.zeros_like(dh_ref)
            for cp in fetches:
                cp.start()

        @pl.when((j == 0) & (i == 0))
        def _():
            dg_ref[...] = jnp.zeros_like(dg_ref)

        @pl.when((j == 0) & (i % per_seq == 0))
        def _():
            dm_ref[...] = jnp.zeros_like(dm_ref)

        def d_act(r):
            return lax.dot_general(df_ref[pl.ds(r * FF_ROWS, FF_ROWS), :], wd_ref[...], NT,
                                   preferred_element_type=F32)

        ahead = d_act(0)
        for r in range(tm // FF_ROWS):
            dact = ahead
            if r + 1 < tm // FF_ROWS:
                ahead = d_act(r + 1)
            rows = pl.ds(r * FF_ROWS, FF_ROWS)
            gate = gate_ref[rows, :]
            up = up_ref[rows, :]
            sg = _sig(gate)
            silu = gate * sg
            act_ref[rows, :] = (silu * up).astype(BF16)
            dup = (dact * silu).astype(BF16)
            dgate = (dact * up * (sg * (1.0 + gate * (1.0 - sg)))).astype(BF16)
            dup_ref[rows, :] = dup
            dgate_ref[rows, :] = dgate
            dh_ref[rows, :] += (lax.dot_general(dgate, wg_ref[...], NT, preferred_element_type=F32)
                                + lax.dot_general(dup, wu_ref[...], NT, preferred_element_type=F32))

        @pl.when(j == last)
        def _():
            for cp in fetches:
                cp.wait()
            g = g_ref[...]
            for r in range(tm // FF_ROWS):
                rows = pl.ds(r * FF_ROWS, FF_ROWS)
                dh = dh_ref[rows, :]
                x1v = late[0, rows, :]
                rs = lax.rsqrt(jnp.mean(x1v * x1v, axis=-1, keepdims=True) + EPS)
                xhat = x1v * rs
                dm_ref[0:1, :] += jnp.sum(dh, axis=0, keepdims=True)
                dm_ref[1:2, :] += jnp.sum(dh * (xhat * g), axis=0, keepdims=True)
                dn = dh * (1.0 + m_ref[4:5, :])
                dg_ref[0:1, :] += jnp.sum(dn * xhat, axis=0, keepdims=True)
                dxh = dn * g
                dx1 = late[1, rows, :] + rs * (dxh - xhat * jnp.mean(dxh * xhat, axis=-1, keepdims=True))
                dx1_ref[rows, :] = dx1
                dm_ref[2:3, :] += jnp.sum(dx1 * late[2, rows, :], axis=0, keepdims=True)
                dmix_ref[rows, :] = (m_ref[2:3, :] * dx1).astype(BF16)

    tile = pl.BlockSpec((tm, FF_TILE), lambda i, j: (i, j))
    rows_spec = pl.BlockSpec((tm, D_MODEL), lambda i, j: (i, 0))
    per = pl.BlockSpec((None, 8, D_MODEL), lambda i, j: (i // per_seq, 0, 0))
    return _call(
        body, name="ffn_bwd", grid=(tokens // tm, D_FF // FF_TILE),
        in_specs=[rows_spec, tile, tile,
                  pl.BlockSpec((D_MODEL, FF_TILE), lambda i, j: (0, j)),
                  pl.BlockSpec((D_MODEL, FF_TILE), lambda i, j: (0, j)),
                  pl.BlockSpec((FF_TILE, D_MODEL), lambda i, j: (j, 0)),
                  per, pl.BlockSpec((1, D_MODEL), lambda i, j: (0, 0)), ANY_SPEC, ANY_SPEC, ANY_SPEC],
        out_specs=[tile, tile, tile, rows_spec, rows_spec, pl.BlockSpec((8, D_MODEL), lambda i, j: (0, 0)), per],
        out_shape=[_sds((tokens, D_FF), BF16)] * 3 + [_sds((tokens, D_MODEL), F32), _sds((tokens, D_MODEL), BF16),
                                                    _sds((8, D_MODEL), F32), _sds((n_seq, 8, D_MODEL), F32)],
        scratch_shapes=[pltpu.VMEM((tm, D_MODEL), F32), pltpu.VMEM((3, tm, D_MODEL), F32),
                        pltpu.SemaphoreType.DMA((3,))],
        compiler_params=_params(("arbitrary", "arbitrary")),
    )(df, gate, up, w_gate, w_up, w_down, mod8, g_ffn, x1, dy, mix)


def _mix_in_bwd(d_a, d_g, d_q, d_k, d_v, w_in, x2, dx1, mod8, g_mix, seq, ride, tm=512):
    tokens = x2.shape[0]
    per_seq = seq // tm
    n_seq = tokens // seq
    parts = (d_a, d_g, d_q, d_k, d_v)
    width = D_CONV
    n_ride = len(ride)
    ride_scatter = [s for _, s in ride]

    def body(*refs):
        da_ref, dg_ref, dq_ref, dk_ref, dv_ref, w_ref, x_ref, dx1_ref, m_ref, g_ref = refs[:10]
        ride_in = refs[10:10 + n_ride]
        gx_ref, dgm_ref, dm_ref = refs[10 + n_ride:13 + n_ride]
        ride_args = (ride_scatter, ride_in, refs[13 + n_ride:13 + 2 * n_ride]) + tuple(refs[13 + 2 * n_ride:])
        i = pl.program_id(0)

        @pl.when(i == 0)
        def _():
            _exchange_start(*ride_args)
            dgm_ref[...] = jnp.zeros_like(dgm_ref)

        @pl.when(i % per_seq == 0)
        def _():
            dm_ref[...] = jnp.zeros_like(dm_ref)

        dh = jnp.zeros((tm, D_MODEL), F32)
        for n, ref in enumerate((da_ref, dg_ref, dq_ref, dk_ref, dv_ref)):
            dh = dh + lax.dot_general(ref[...], w_ref[:, pl.ds(n * width, width)], NT, preferred_element_type=F32)
        xv = x_ref[...]
        r = lax.rsqrt(jnp.mean(xv * xv, axis=-1, keepdims=True) + EPS)
        xhat = xv * r
        g = g_ref[...]
        dm_ref[0:1, :] += jnp.sum(dh, axis=0, keepdims=True)
        dm_ref[1:2, :] += jnp.sum(dh * (xhat * g), axis=0, keepdims=True)
        dn = dh * (1.0 + m_ref[1:2, :])
        dgm_ref[0:1, :] += jnp.sum(dn * xhat, axis=0, keepdims=True)
        dxh = dn * g
        gx_ref[...] = dx1_ref[...] + r * (dxh - xhat * jnp.mean(dxh * xhat, axis=-1, keepdims=True))

        @pl.when(i == tokens // tm - 1)
        def _():
            _exchange_wait(*ride_args)

    rows = pl.BlockSpec((tm, D_MODEL), lambda i: (i, 0))
    half = pl.BlockSpec((tm, width), lambda i: (i, 0))
    per = pl.BlockSpec((None, 8, D_MODEL), lambda i: (i // per_seq, 0, 0))
    return _call(
        body, name="mix_in_bwd", grid=(tokens // tm,),
        in_specs=[half] * 5 + [pl.BlockSpec((D_MODEL, D_IN), lambda i: (0, 0)), rows, rows, per,
                               pl.BlockSpec((1, D_MODEL), lambda i: (0, 0))] + [ANY_SPEC] * n_ride,
        out_specs=[rows, pl.BlockSpec((8, D_MODEL), lambda i: (0, 0)), per] + [ANY_SPEC] * n_ride,
        out_shape=[_sds((tokens, D_MODEL), F32), _sds((8, D_MODEL), F32), _sds((n_seq, 8, D_MODEL), F32)]
        + _exchange_shapes(ride),
        scratch_shapes=_exchange_sems(n_ride),
        compiler_params=_params(("arbitrary",)),
    )(*parts, w_in, x2, dx1, mod8, g_mix, *[a for a, _ in ride])


def _grad_matmul_parts(a_parts, b_parts, name, tk=1024):
    tokens = a_parts[0].shape[0]
    na, nb = len(a_parts), len(b_parts)
    ma, nbw = a_parts[0].shape[1], b_parts[0].shape[1]

    n_k = tokens // tk

    def body(*refs):
        a_refs, b_refs, o_ref, acc = refs[:na], refs[na:na + nb], refs[na + nb], refs[na + nb + 1]

        @pl.when(pl.program_id(0) == 0)
        def _():
            acc[...] = jnp.zeros_like(acc)

        for i in range(na):
            for j in range(nb):
                acc[pl.ds(i * ma, ma), pl.ds(j * nbw, nbw)] += lax.dot_general(
                    a_refs[i][...], b_refs[j][...], TN, preferred_element_type=F32)

        @pl.when(pl.program_id(0) == n_k - 1)
        def _():
            o_ref[...] = acc[...].astype(o_ref.dtype)

    return _call(
        body, name=name, grid=(n_k,),
        in_specs=[pl.BlockSpec((tk, ma), lambda k: (k, 0))] * na + [pl.BlockSpec((tk, nbw), lambda k: (k, 0))] * nb,
        out_specs=pl.BlockSpec((na * ma, nb * nbw), lambda k: (0, 0)),
        out_shape=_sds((na * ma, nb * nbw), BF16),
        scratch_shapes=[pltpu.VMEM((na * ma, nb * nbw), F32)],
        compiler_params=_params(("arbitrary",)),
    )(*a_parts, *b_parts)


def _grad_matmul(a, b, name, tmo, tno, tk=1024):
    tokens, m = a.shape
    n = b.shape[1]
    n_k = tokens // tk

    def body(a_ref, b_ref, o_ref, acc):
        @pl.when(pl.program_id(2) == 0)
        def _():
            acc[...] = jnp.zeros_like(acc)

        acc[...] += lax.dot_general(a_ref[...], b_ref[...], TN, preferred_element_type=F32)

        @pl.when(pl.program_id(2) == n_k - 1)
        def _():
            o_ref[...] = acc[...].astype(o_ref.dtype)

    return _call(
        body, name=name, grid=(m // tmo, n // tno, n_k),
        in_specs=[pl.BlockSpec((tk, tmo), lambda i, j, k: (k, i)),
                  pl.BlockSpec((tk, tno), lambda i, j, k: (k, j))],
        out_specs=pl.BlockSpec((tmo, tno), lambda i, j, k: (i, j)),
        out_shape=_sds((m, n), BF16),
        scratch_shapes=[pltpu.VMEM((tmo, tno), F32)],
        compiler_params=_params(("parallel", "parallel", "arbitrary")),
    )(a, b)


def _adamw(w, m, v, g, name, n_parts=0, tr=256):
    rows, cols = w.shape
    tr = min(tr, rows)
    c1 = 1.0 - ADAM_B1 ** ADAM_STEP
    c2 = 1.0 - ADAM_B2 ** ADAM_STEP

    def body(w_ref, m_ref, v_ref, g_ref, go_ref, d_ref, mo_ref, vo_ref):
        if n_parts:
            gv = g_ref[0].astype(F32)
            for p in range(1, n_parts):
                gv = gv + g_ref[p].astype(F32)
        else:
            gv = g_ref[...]
        go_ref[...] = gv
        mn = ADAM_B1 * m_ref[...] + (1.0 - ADAM_B1) * gv
        vn = ADAM_B2 * v_ref[...] + (1.0 - ADAM_B2) * (gv * gv)
        mo_ref[...] = mn
        vo_ref[...] = vn
        d_ref[...] = -ADAM_LR * ((mn / c1) / (jnp.sqrt(vn / c2) + ADAM_EPS) + ADAM_WD * w_ref[...])

    blk = pl.BlockSpec((tr, cols), lambda i: (i, 0))
    g_spec = pl.BlockSpec((n_parts, tr, cols), lambda i: (0, i, 0)) if n_parts else blk
    return _call(
        body, name=name, grid=(rows // tr,),
        in_specs=[blk, blk, blk, g_spec], out_specs=[blk] * 4,
        out_shape=[_sds((rows, cols), F32)] * 4,
        compiler_params=_params(("parallel",)),
    )(w, m, v, g)


def _cols_to_full(blocks):
    n, r, c = blocks.shape
    return jnp.transpose(blocks, (1, 0, 2)).reshape(r, n * c)


def _full_to_cols(full, n=N_DEV):
    r, c = full.shape
    return jnp.transpose(full.reshape(r, n, c // n), (1, 0, 2))


def _pad_lanes(v, width):
    return jnp.pad(v, ((0, 0), (0, width - v.shape[1])))


def kernel(x, c, w_ada, b_ada, g_mix, w_in, w_dw, b_dw, g_conv_ln, b_conv_ln, g_q, g_k, w_out, g_ffn, w_gate, w_up, w_down, loss_target, m_w_ada, m_b_ada, m_g_mix, m_w_in, m_w_dw, m_b_dw, m_g_conv_ln, m_b_conv_ln, m_g_q, m_g_k, m_w_out, m_g_ffn, m_w_gate, m_w_up, m_w_down, v_w_ada, v_b_ada, v_g_mix, v_w_in, v_w_dw, v_b_dw, v_g_conv_ln, v_b_conv_ln, v_g_q, v_g_k, v_w_out, v_g_ffn, v_w_gate, v_w_up, v_w_down):
    n_seq, seq, _ = x.shape
    tokens = n_seq * seq
    me = 4 * lax.axis_index("x") + 2 * lax.axis_index("y") + lax.axis_index("c")
    ada_cols = w_ada.shape[2]
    dw_cols = w_dw.shape[2]

    (c_g, w_in_g, w_dw_g) = _gather_by_chip([c, w_in[0].astype(BF16), w_dw[0]], "gather_weights")
    c_all = c_g.reshape(N_DEV * n_seq, D_MODEL)
    w_in_f = _cols_to_full(w_in_g)
    w_dw_f = _cols_to_full(w_dw_g)

    b_cols = lax.dynamic_slice(b_ada, (0, me * ada_cols), (1, ada_cols))
    mod_cols = _ada_fwd(c_all, w_ada[0], b_cols)
    (mod_g,) = _exchange([(mod_cols, False)], "gather_mod")
    mod_mine = lax.dynamic_slice(mod_g, (0, me * n_seq, 0), (N_DEV, n_seq, ada_cols))
    mod = jnp.transpose(mod_mine, (1, 0, 2)).reshape(n_seq, N_MOD, D_MODEL)
    mod8 = jnp.pad(mod, ((0, 0), (0, 8 - N_MOD), (0, 0)))

    x2 = x.reshape(tokens, D_MODEL)
    h1, proj = _mix_in(x2, mod8, g_mix, w_in_f, seq)
    proj3 = proj.reshape(n_seq, seq, D_IN)
    uc3 = _conv_fwd(proj3, w_dw_f, b_dw)
    g_q2, g_k2 = jnp.tile(g_q, (1, 2)), jnp.tile(g_k, (1, 2))
    y_att3, lse3, w_out_g, w_gate_g, w_up_g, w_down_g = _attn_fwd(
        proj3, g_q2, g_k2,
        [(w_out[0].astype(BF16), False), (w_gate[0].astype(BF16), False), (w_up[0].astype(BF16), False),
         (w_down[0].astype(BF16), False)])
    w_out_f = w_out_g.reshape(D_MODEL, D_MODEL)
    w_gate_f = _cols_to_full(w_gate_g)
    w_up_f = _cols_to_full(w_up_g)
    w_down_f = w_down_g.reshape(D_FF, D_MODEL)
    uc2 = uc3.reshape(tokens, D_CONV)
    y_att2 = y_att3.reshape(tokens, D_ATT)
    y_conv, mix, x1, h2 = _mix_out(uc2, y_att2, x2, mod8, g_conv_ln, b_conv_ln, g_ffn, w_out_f, seq)
    gate, up, dy, df, sq, dgate_f = _ffn_fwd(
        h2, w_gate_f, w_up_f, w_down_f, x1, loss_target.reshape(tokens, D_MODEL), mod8, seq)

    dgate, dup, act, dx1, dmix, dg_ffn, dmod_f = _ffn_bwd(
        df, gate, up, w_gate_f, w_up_f, w_down_f, x1, dy, mix, mod8, g_ffn, seq)
    duc2, do2, dgb_ln = _mix_out_bwd(dmix, uc2, g_conv_ln, b_conv_ln, w_out_f)
    d_a3, d_g3, dw_dw_p, db_dw_p = _conv_bwd(duc2.reshape(n_seq, seq, D_CONV), proj3, w_dw_f)
    gw_gate = _grad_matmul(h2, dgate, "grad_w_gate", D_MODEL, D_FF // 2)
    gw_up = _grad_matmul(h2, dup, "grad_w_up", D_MODEL, D_FF // 2)
    gw_down = _grad_matmul(act, df, "grad_w_down", D_FF // 2, D_MODEL)
    gw_out = _grad_matmul_parts([y_conv, y_att2], [dmix], "grad_w_out")
    d_q3, d_k3, d_v3, dg_qk, p_gate, p_up, p_down, p_out = _attn_bwd(
        proj3, do2.reshape(n_seq, seq, D_ATT), y_att3, lse3, g_q2, g_k2,
        [(_full_to_cols(gw_gate).astype(BF16), True), (_full_to_cols(gw_up).astype(BF16), True),
         (gw_down.astype(BF16).reshape(N_DEV, D_FF // N_DEV, D_MODEL), True),
         (gw_out.astype(BF16).reshape(N_DEV, D_MODEL // N_DEV, D_MODEL), True)])
    flat = lambda t: t.reshape(tokens, t.shape[-1])
    d_a, d_g, d_q, d_k, d_v = flat(d_a3), flat(d_g3), flat(d_q3), flat(d_k3), flat(d_v3)
    gw_in = _grad_matmul_parts([h1], [d_a, d_g, d_q, d_k, d_v], "grad_w_in")
    grad_x2, dg_mix, dmod_m, p_in = _mix_in_bwd(
        d_a, d_g, d_q, d_k, d_v, w_in_f, x2, dx1, mod8, g_mix, seq, [(_full_to_cols(gw_in).astype(BF16), True)])

    dmod = jnp.concatenate([dmod_m[:, 0], dmod_m[:, 1], dmod_f[:, 2], dmod_f[:, 0], dmod_f[:, 1], dgate_f[:, 0]], axis=1)
    dg_q = dg_qk[0:1, 0:HEAD_DIM] + dg_qk[0:1, HEAD_DIM:]
    dg_k = dg_qk[1:2, 0:HEAD_DIM] + dg_qk[1:2, HEAD_DIM:]
    loss_part = (0.5 / D_MODEL) * jnp.sum(sq[0:1, :], axis=1, keepdims=True)
    small = jnp.concatenate(
        [dg_mix[0:1], dg_ffn[0:1], db_dw_p[0:1], dgb_ln[0:1], dgb_ln[1:2],
         _pad_lanes(dg_q, LANES), _pad_lanes(dg_k, LANES), _pad_lanes(loss_part, LANES)], axis=1)
    n_small = small.shape[1] - LANES

    (dmod_g, small_g, dw_g) = _exchange([(dmod, False), (small, False), (dw_dw_p, False)], "gather_small_grads")

    dmod_all = dmod_g.reshape(N_DEV * n_seq, N_MOD * D_MODEL)
    dmod_cols = lax.dynamic_slice(dmod_all, (0, me * ada_cols), (N_DEV * n_seq, ada_cols))
    gw_ada, gb_ada = _ada_bwd(c_all, dmod_cols, dmod_all)

    res = {}
    res["w_ada"] = _adamw(w_ada[0], m_w_ada[0], v_w_ada[0], gw_ada, "adamw_w_ada")
    res["b_ada"] = _adamw(b_ada, m_b_ada, v_b_ada, gb_ada, "adamw_b_ada")
    res["w_in"] = _adamw(w_in[0], m_w_in[0], v_w_in[0], p_in, "adamw_w_in", N_DEV)
    res["w_out"] = _adamw(w_out[0], m_w_out[0], v_w_out[0], p_out, "adamw_w_out", N_DEV)
    res["w_gate"] = _adamw(w_gate[0], m_w_gate[0], v_w_gate[0], p_gate, "adamw_w_gate", N_DEV)
    res["w_up"] = _adamw(w_up[0], m_w_up[0], v_w_up[0], p_up, "adamw_w_up", N_DEV)
    res["w_down"] = _adamw(w_down[0], m_w_down[0], v_w_down[0], p_down, "adamw_w_down", N_DEV, tr=176)
    dw_mine = lax.dynamic_slice(dw_g, (0, 0, me * dw_cols), (N_DEV, CONV_WIDTH, dw_cols))
    res["w_dw"] = _adamw(w_dw[0], m_w_dw[0], v_w_dw[0], dw_mine, "adamw_w_dw", N_DEV)

    small_names = ["g_mix", "g_ffn", "b_dw", "g_conv_ln", "b_conv_ln", "g_q", "g_k"]
    small_w = {"g_mix": (g_mix, m_g_mix, v_g_mix), "g_ffn": (g_ffn, m_g_ffn, v_g_ffn), "b_dw": (b_dw, m_b_dw, v_b_dw),
               "g_conv_ln": (g_conv_ln, m_g_conv_ln, v_g_conv_ln), "b_conv_ln": (b_conv_ln, m_b_conv_ln, v_b_conv_ln),
               "g_q": (g_q, m_g_q, v_g_q), "g_k": (g_k, m_g_k, v_g_k)}
    widths = [max(small_w[n][0].shape[1], LANES) for n in small_names]
    packed = [jnp.concatenate([_pad_lanes(small_w[n][i], wd) for n, wd in zip(small_names, widths)], axis=1) for i in range(3)]
    outs = _adamw(packed[0], packed[1], packed[2], small_g[:, :, :n_small], "adamw_small", N_DEV)
    off = 0
    for n, wd in zip(small_names, widths):
        real = small_w[n][0].shape[1]
        res[n] = tuple(o[:, off:off + real] for o in outs)
        off += wd
    loss = jnp.sum(small_g[:, 0, n_small])

    order = ["w_ada", "b_ada", "g_mix", "w_in", "w_dw", "b_dw", "g_conv_ln", "b_conv_ln", "g_q", "g_k",
             "w_out", "g_ffn", "w_gate", "w_up", "w_down"]
    lead = {"w_ada", "w_in", "w_dw", "w_out", "w_gate", "w_up", "w_down"}
    grads, deltas, new_m, new_v = [], [], [], []
    for n in order:
        g, d, mn, vn = res[n]
        g, d, mn, vn = (t[None] if n in lead else t for t in (g, d, mn, vn))
        grads.append(g)
        deltas.append(d)
        new_m.append(mn)
        new_v.append(vn)
    return (loss, grad_x2.reshape(n_seq, seq, D_MODEL), *grads, *deltas, *new_m, *new_v)
```

```python
import numpy as np
import jax
import jax.numpy as jnp
from jax import lax
from jax.experimental import pallas as pl
from jax.experimental.pallas import tpu as pltpu

F32 = jnp.float32
BF16 = jnp.bfloat16

N_DEV = 8
D_MODEL = 1024
D_CONV = 512
D_ATT = 512
HEAD_DIM = 64
CONV_WIDTH = 31
D_IN = 2 * D_CONV + 3 * D_ATT
D_FF = 2816
N_MOD = 6
EPS = 1e-6
RADIUS = 64
DILATIONS = (1, 4, 16)
Q_BLOCK = 128
LANES = 128
VMEM_LIMIT = 56 * 1024 * 1024

ADAM_LR = 0.001
ADAM_B1 = 0.9
ADAM_B2 = 0.999
ADAM_EPS = 1e-08
ADAM_WD = 0.01
ADAM_STEP = 10

NT = (((1,), (1,)), ((), ()))
TN = (((0,), (0,)), ((), ()))


def _call(body, **kw):
    return pl.pallas_call(body, **kw)


def _params(sem=None, vmem=VMEM_LIMIT):
    return pltpu.CompilerParams(dimension_semantics=sem, vmem_limit_bytes=vmem)


def _sig(x):
    return 1.0 / (1.0 + jnp.exp(-x))


def _sds(shape, dtype):
    return jax.ShapeDtypeStruct(shape, dtype)


N_PEER = N_DEV - 1
ANY_SPEC = pl.BlockSpec(memory_space=pl.ANY)


def _exchange_copies(scatter, ins, outs, *sems):
    n = len(ins)
    if n == 0:
        return [], []
    send_sems, recv_sems, local_sems = sems
    x, y, c = lax.axis_index("x"), lax.axis_index("y"), lax.axis_index("c")
    me = 4 * x + 2 * y + c

    def src(a, slot):
        return ins[a].at[slot] if scatter[a] else ins[a]

    local = [pltpu.make_async_copy(src(a, me), outs[a].at[me], local_sems.at[a]) for a in range(n)]
    flights = []
    for k in range(1, N_DEV):
        px = 1 - x if k & 4 else x
        py = 1 - y if k & 2 else y
        pc = 1 - c if k & 1 else c
        pid = 4 * px + 2 * py + pc
        for a in range(n):
            i = a * N_PEER + k - 1
            send, recv = (pltpu.make_async_remote_copy(
                src_ref=src(a, pid), dst_ref=outs[a].at[slot],
                send_sem=send_sems.at[i], recv_sem=recv_sems.at[i],
                device_id=(px, py, pc), device_id_type=pl.DeviceIdType.MESH) for slot in (me, pid))
            flights.append((send, recv))
    return local, flights


def _exchange_start(*args):
    local, flights = _exchange_copies(*args)
    for cp in local:
        cp.start()
    for send, _ in flights:
        send.start()


def _exchange_wait(*args):
    local, flights = _exchange_copies(*args)
    for send, recv in flights:
        send.wait_send()
        recv.wait_recv()
    for cp in local:
        cp.wait()


def _exchange_shapes(items):
    return [_sds((N_DEV,) + tuple(arr.shape[1:] if scatter else arr.shape), arr.dtype) for arr, scatter in items]


def _exchange_sems(n):
    if n == 0:
        return []
    return [pltpu.SemaphoreType.DMA((n * N_PEER,)), pltpu.SemaphoreType.DMA((n * N_PEER,)),
            pltpu.SemaphoreType.DMA((n,))]


def _gather_by_chip(arrays, name):
    n = len(arrays)
    per = N_PEER

    def body(*refs):
        ins, outs = refs[:n], refs[n:2 * n]
        send_sems, recv_sems, local_sems = refs[2 * n:]
        x, y, c = lax.axis_index("x"), lax.axis_index("y"), lax.axis_index("c")
        sibling = (x, y, 1 - c)
        chips = [(1 - x, y), (x, 1 - y), (1 - x, 1 - y)]

        def slot(px, py, pc):
            return 4 * px + 2 * py + pc

        def copy(a, k, block, to, src=None):
            dst = outs[a].at[slot(*block)]
            return pltpu.make_async_remote_copy(
                src_ref=dst if src is None else src, dst_ref=dst,
                send_sem=send_sems.at[a * per + k], recv_sem=recv_sems.at[a * per + k],
                device_id=to, device_id_type=pl.DeviceIdType.MESH)

        me = (x, y, c)
        local = [pltpu.make_async_copy(ins[a], outs[a].at[slot(*me)], local_sems.at[a]) for a in range(n)]
        for cp in local:
            cp.start()
        first = []
        for a in range(n):
            first.append(copy(a, 0, me, sibling, src=ins[a]))
            first += [copy(a, 1 + j, me, (*chip, c), src=ins[a]) for j, chip in enumerate(chips)]
        for cp in first:
            cp.start()
        passed = []
        for j, chip in enumerate(chips):
            for a in range(n):
                copy(a, 1 + j, (*chip, c), me).wait_recv()
                fwd = copy(a, 4 + j, (*chip, c), sibling)
                fwd.start()
                passed.append(fwd)
        for a in range(n):
            copy(a, 0, sibling, me).wait_recv()
            for j, chip in enumerate(chips):
                copy(a, 4 + j, (*chip, 1 - c), me).wait_recv()
        for cp in first + passed:
            cp.wait_send()
        for cp in local:
            cp.wait()

    return _call(
        body, name=name, out_shape=_exchange_shapes([(arr, False) for arr in arrays]),
        in_specs=[ANY_SPEC] * n, out_specs=[ANY_SPEC] * n, scratch_shapes=_exchange_sems(n),
    )(*arrays)


def _exchange(items, name):
    n = len(items)
    scatter = [s for _, s in items]

    def body(*refs):
        args = (scatter, refs[:n], refs[n:2 * n]) + tuple(refs[2 * n:])
        _exchange_start(*args)
        _exchange_wait(*args)

    return _call(
        body, name=name, out_shape=_exchange_shapes(items),
        in_specs=[ANY_SPEC] * n, out_specs=[ANY_SPEC] * n, scratch_shapes=_exchange_sems(n),
    )(*[a for a, _ in items])


def _ada_fwd(c_all, w_ada, b_cols):
    def body(c_ref, w_ref, b_ref, o_ref):
        cv = c_ref[...]
        sc = (cv * _sig(cv)).astype(BF16)
        o_ref[...] = jnp.dot(sc, w_ref[...].astype(BF16), preferred_element_type=F32) + b_ref[...]

    return _call(body, name="ada_fwd", out_shape=_sds((c_all.shape[0], w_ada.shape[1]), F32),
                 compiler_params=_params())(c_all, w_ada, b_cols)


def _ada_bwd(c_all, dmod_cols, dmod_all):
    def body(c_ref, dc_ref, da_ref, gw_ref, gb_ref):
        cv = c_ref[...]
        sc = (cv * _sig(cv)).astype(BF16)
        gw_ref[...] = lax.dot_general(sc, dc_ref[...].astype(BF16), TN, preferred_element_type=F32)
        gb_ref[...] = jnp.sum(da_ref[...], axis=0, keepdims=True)

    return _call(body, name="ada_bwd",
                 out_shape=[_sds((c_all.shape[1], dmod_cols.shape[1]), F32), _sds((1, dmod_all.shape[1]), F32)],
                 compiler_params=_params())(c_all, dmod_cols, dmod_all)


MIX_ROWS = 128


def _mix_in(x2, mod8, g_mix, w_in, seq, tm=512):
    tokens = x2.shape[0]
    per_seq = seq // tm

    def body(x_ref, m_ref, g_ref, w_ref, h_ref, p_ref):
        def normed(c):
            rows = pl.ds(c * MIX_ROWS, MIX_ROWS)
            xv = x_ref[rows, :]
            r = lax.rsqrt(jnp.mean(xv * xv, axis=-1, keepdims=True) + EPS)
            hb = ((xv * r * g_ref[...]) * (1.0 + m_ref[1:2, :]) + m_ref[0:1, :]).astype(BF16)
            h_ref[rows, :] = hb
            return hb

        ahead = normed(0)
        for c in range(tm // MIX_ROWS):
            hb = ahead
            if c + 1 < tm // MIX_ROWS:
                ahead = normed(c + 1)
            p_ref[pl.ds(c * MIX_ROWS, MIX_ROWS), :] = jnp.dot(hb, w_ref[...], preferred_element_type=F32)

    return _call(
        body, name="mix_in", grid=(tokens // tm,),
        in_specs=[pl.BlockSpec((tm, D_MODEL), lambda i: (i, 0)),
                  pl.BlockSpec((None, 8, D_MODEL), lambda i: (i // per_seq, 0, 0)),
                  pl.BlockSpec((1, D_MODEL), lambda i: (0, 0)),
                  pl.BlockSpec((D_MODEL, D_IN), lambda i: (0, 0))],
        out_specs=[pl.BlockSpec((tm, D_MODEL), lambda i: (i, 0)),
                   pl.BlockSpec((tm, D_IN), lambda i: (i, 0))],
        out_shape=[_sds((tokens, D_MODEL), BF16), _sds((tokens, D_IN), F32)],
        compiler_params=_params(("parallel",)),
    )(x2, mod8, g_mix, w_in)


CONV_ROWS = 64
CONV_DW_ROWS = 32
CONV_DW_UNROLL = 4
CONV_HALO = 16


def _fill_shifted(xp, sh, seq):
    for b in range(8):
        sh[b, pl.ds(0, seq + 24), :] = xp[pl.ds(b, seq + 24), :]


def _conv_fwd(proj3, w_dw, b_dw):
    n_seq, seq, _ = proj3.shape
    n_cb = D_CONV // LANES

    def body(a_ref, g_ref, w_ref, b_ref, uc_ref, xp, sh):
        zeros = jnp.zeros((CONV_HALO, LANES), F32)
        xp[pl.ds(0, CONV_HALO), :] = zeros
        xp[pl.ds(CONV_HALO + seq, CONV_HALO), :] = zeros
        xp[pl.ds(CONV_HALO, seq), :] = a_ref[...] * _sig(g_ref[...])
        _fill_shifted(xp, sh, seq)

        def blk(i, carry):
            t0 = pl.multiple_of(i * CONV_ROWS, CONV_ROWS)
            acc = jnp.zeros((CONV_ROWS, LANES), F32)
            for j in range(CONV_WIDTH):
                jj = j + 1
                acc = acc + sh[jj % 8, pl.ds(t0 + 8 * (jj // 8), CONV_ROWS), :] * w_ref[j:j + 1, :]
            uc_ref[pl.ds(t0, CONV_ROWS), :] = acc + b_ref[...]
            return carry

        lax.fori_loop(0, seq // CONV_ROWS, blk, 0)

    return _call(
        body, name="conv_fwd", grid=(n_seq, n_cb),
        in_specs=[pl.BlockSpec((None, seq, LANES), lambda b, cb: (b, 0, cb)),
                  pl.BlockSpec((None, seq, LANES), lambda b, cb: (b, 0, n_cb + cb)),
                  pl.BlockSpec((CONV_WIDTH, LANES), lambda b, cb: (0, cb)),
                  pl.BlockSpec((1, LANES), lambda b, cb: (0, cb))],
        out_specs=pl.BlockSpec((None, seq, LANES), lambda b, cb: (b, 0, cb)),
        out_shape=_sds((n_seq, seq, D_CONV), F32),
        scratch_shapes=[pltpu.VMEM((seq + 2 * CONV_HALO, LANES), F32),
                        pltpu.VMEM((8, seq + 2 * CONV_HALO, LANES), F32)],
        compiler_params=_params(("parallel", "parallel")),
    )(proj3, proj3, w_dw, b_dw)


def _conv_bwd(duc3, proj3, w_dw):
    n_seq, seq, _ = proj3.shape
    n_cb = D_CONV // LANES

    def body(duc_ref, a_ref, g_ref, w_ref, da_ref, dg_ref, dw_ref, db_ref, xp, sh):
        @pl.when(pl.program_id(1) == 0)
        def _():
            dw_ref[...] = jnp.zeros_like(dw_ref)
            db_ref[...] = jnp.zeros_like(db_ref)

        zeros = jnp.zeros((CONV_HALO, LANES), F32)
        xp[pl.ds(0, CONV_HALO), :] = zeros
        xp[pl.ds(CONV_HALO + seq, CONV_HALO), :] = zeros
        xp[pl.ds(CONV_HALO, seq), :] = a_ref[...] * _sig(g_ref[...])
        _fill_shifted(xp, sh, seq)
        for j0 in range(0, CONV_WIDTH, 8):
            taps = range(j0, min(j0 + 8, CONV_WIDTH))

            def wblk(i, accs, taps=taps):
                for u in range(CONV_DW_UNROLL):
                    t0 = pl.multiple_of((i * CONV_DW_UNROLL + u) * CONV_DW_ROWS, CONV_DW_ROWS)
                    d = duc_ref[pl.ds(t0, CONV_DW_ROWS), :]
                    accs = tuple(acc + d * sh[(j + 1) % 8, pl.ds(t0 + 8 * ((j + 1) // 8), CONV_DW_ROWS), :]
                                 for acc, j in zip(accs, taps))
                return accs

            accs = lax.fori_loop(0, seq // (CONV_DW_ROWS * CONV_DW_UNROLL), wblk,
                                 tuple(jnp.zeros((CONV_DW_ROWS, LANES), F32) for _ in taps))
            for acc, j in zip(accs, taps):
                dw_ref[j:j + 1, :] += jnp.sum(acc, axis=0, keepdims=True)
        db_ref[0:1, :] += jnp.sum(duc_ref[...], axis=0, keepdims=True)
        xp[pl.ds(CONV_HALO, seq), :] = duc_ref[...]
        _fill_shifted(xp, sh, seq)

        def ublk(i, carry):
            t0 = pl.multiple_of(i * CONV_ROWS, CONV_ROWS)
            acc = jnp.zeros((CONV_ROWS, LANES), F32)
            for j in range(CONV_WIDTH):
                jj = CONV_WIDTH - j
                acc = acc + sh[jj % 8, pl.ds(t0 + 8 * (jj // 8), CONV_ROWS), :] * w_ref[j:j + 1, :]
            av = a_ref[pl.ds(t0, CONV_ROWS), :]
            sg = _sig(g_ref[pl.ds(t0, CONV_ROWS), :])
            da_ref[pl.ds(t0, CONV_ROWS), :] = (acc * sg).astype(BF16)
            dg_ref[pl.ds(t0, CONV_ROWS), :] = (acc * av * sg * (1.0 - sg)).astype(BF16)
            return carry

        lax.fori_loop(0, seq // CONV_ROWS, ublk, 0)

    return _call(
        body, name="conv_bwd", grid=(n_cb, n_seq),
        in_specs=[pl.BlockSpec((None, seq, LANES), lambda cb, b: (b, 0, cb)),
                  pl.BlockSpec((None, seq, LANES), lambda cb, b: (b, 0, cb)),
                  pl.BlockSpec((None, seq, LANES), lambda cb, b: (b, 0, n_cb + cb)),
                  pl.BlockSpec((CONV_WIDTH, LANES), lambda cb, b: (0, cb))],
        out_specs=[pl.BlockSpec((None, seq, LANES), lambda cb, b: (b, 0, cb)),
                   pl.BlockSpec((None, seq, LANES), lambda cb, b: (b, 0, cb)),
                   pl.BlockSpec((32, LANES), lambda cb, b: (0, cb)),
                   pl.BlockSpec((8, LANES), lambda cb, b: (0, cb))],
        out_shape=[_sds((n_seq, seq, D_CONV), BF16), _sds((n_seq, seq, D_CONV), BF16),
                   _sds((32, D_CONV), F32), _sds((8, D_CONV), F32)],
        scratch_shapes=[pltpu.VMEM((seq + 2 * CONV_HALO, LANES), F32),
                        pltpu.VMEM((8, seq + 2 * CONV_HALO, LANES), F32)],
        compiler_params=_params(("parallel", "arbitrary")),
    )(duc3, proj3, proj3, w_dw)


MASKED = 1e30
ATT_ROWS = 512
ATT_UNROLL = 4
ATT_FWD_UNROLL = 8


def _distance_mats(dil, seg_len):
    kw = min(2 * Q_BLOCK, seg_len)
    offsets = (0, -RADIUS, -2 * RADIUS) if kw == 2 * Q_BLOCK else (0,)
    a = np.arange(Q_BLOCK)[:, None]
    b = np.arange(kw)[None, :]
    mats = []
    for off in offsets:
        rel = np.abs(b + off - a)
        mats.append(np.where(rel <= RADIUS, dil * rel, MASKED))
    return jnp.asarray(np.stack(mats).astype(np.float32))


def _alibi_rows():
    s = np.zeros((4, 8, LANES), np.float32)
    for hp in range(4):
        for hl in range(2):
            s[hp, hl, :] = 2.0 ** (-(2 * hp + hl + 1))
    return jnp.asarray(s)


def _window(n, seg_len):
    i0 = pl.multiple_of(n * Q_BLOCK, Q_BLOCK)
    if seg_len <= Q_BLOCK:
        return i0, i0, 0
    per_seg = seg_len // Q_BLOCK
    j = n % per_seg
    seg0 = (n // per_seg) * seg_len
    ks_local = jnp.clip(j * Q_BLOCK - RADIUS, 0, seg_len - 2 * Q_BLOCK)
    ks = pl.multiple_of(seg0 + ks_local, RADIUS)
    var = jnp.where(j == 0, 0, jnp.where(j == per_seg - 1, 2, 1))
    return i0, ks, var


def _first_head(rows):
    return lax.broadcasted_iota(jnp.int32, (rows, LANES), 1) < HEAD_DIM


def _same_head():
    head = np.arange(LANES) // HEAD_DIM
    return jnp.asarray((head[:, None] == head[None, :]).astype(np.float32)).astype(BF16)


def _head_sum(x, same_ref):
    hi = x.astype(BF16)
    lo = (x - hi.astype(F32)).astype(BF16)
    return (jnp.dot(hi, same_ref[...], preferred_element_type=F32)
            + jnp.dot(lo, same_ref[...], preferred_element_type=F32))


def _head_mean(x, same_ref):
    return _head_sum(x, same_ref) * (1.0 / HEAD_DIM)


def _per_head(x, first):
    swapped = pltpu.roll(x, HEAD_DIM, 1)
    return jnp.where(first, x, swapped), jnp.where(first, swapped, x)


STRIDE = 4


def _gather_segments(src, dil, seq, tmp, put):
    if dil == 1:
        put(0, seq, src[pl.ds(0, seq), :])
    elif dil == STRIDE:
        seg = seq // dil
        for r in range(dil):
            put(r * seg, seg, src[pl.ds(r, seg, stride=dil), :])
    else:
        part, seg = seq // STRIDE, seq // dil
        for b in range(STRIDE):
            tmp[pl.ds(b * part, part), :] = src[pl.ds(b, part, stride=STRIDE), :]
        for b in range(STRIDE):
            for a in range(dil // STRIDE):
                put(b * part + a * seg, seg, tmp[pl.ds(b * part + a, seg, stride=dil // STRIDE), :])


def _scatter_segments(dst, get, dil, seq, tmp, accumulate):
    def write(rows, val):
        if accumulate:
            dst[rows, :] += val
        else:
            dst[rows, :] = val

    if dil == 1:
        write(pl.ds(0, seq), get(0, seq))
    elif dil == STRIDE:
        seg = seq // dil
        for r in range(dil):
            write(pl.ds(r, seg, stride=dil), get(r * seg, seg))
    else:
        part, seg = seq // STRIDE, seq // dil
        for b in range(STRIDE):
            for a in range(dil // STRIDE):
                tmp[pl.ds(b * part + a, seg, stride=dil // STRIDE), :] = get(b * part + a * seg, seg)
        for b in range(STRIDE):
            write(pl.ds(b, part, stride=STRIDE), tmp[pl.ds(b * part, part), :])


def _permute_rows(dst, src, dil, seq, tmp):
    def put(start, size, val):
        dst[pl.ds(start, size), :] = val.astype(dst.dtype)

    _gather_segments(src, dil, seq, tmp, put)


def _permute_rows_by_head(dst, src, dil, seq, tmp):
    def put(start, size, val):
        first = _first_head(size)
        dst[0, pl.ds(start, size), :] = jnp.where(first, val, 0.0).astype(dst.dtype)
        dst[1, pl.ds(start, size), :] = jnp.where(first, 0.0, val).astype(dst.dtype)

    _gather_segments(src, dil, seq, tmp, put)


def _qk_normalise(q_ref, g2_ref, same_ref, dst, seq, scale):
    def chunk(ci, carry):
        rows = pl.ds(pl.multiple_of(ci * ATT_ROWS, ATT_ROWS), ATT_ROWS)
        qv = q_ref[rows, :]
        r = lax.rsqrt(_head_mean(qv * qv, same_ref) + EPS)
        dst[rows, :] = qv * r * (g2_ref[...] * scale)
        return carry

    lax.fori_loop(0, seq // ATT_ROWS, chunk, 0)


def _attn_fwd(proj3, g_q2, g_k2, ride):
    n_seq, seq, _ = proj3.shape
    dms = [_distance_mats(d, seq // d) for d in DILATIONS]
    same = _same_head()
    col0 = 2 * D_CONV // LANES
    n_hp = D_ATT // LANES

    n_ride = len(ride)
    ride_scatter = [s for _, s in ride]

    def body(*refs):
        q_ref, k_ref, v_ref, gq_ref, gk_ref, sl_ref, dm1, dm4, dm16, same_ref = refs[:10]
        ride_in = refs[10:10 + n_ride]
        y_ref, lse_ref = refs[10 + n_ride:12 + n_ride]
        ride_out = refs[12 + n_ride:12 + 2 * n_ride]
        (qf, kf, qp, kp, vp, oml_p, o1, o4, o16, m1, m4, m16, l1, l4, l16,
         tmp) = refs[12 + 2 * n_ride:28 + 2 * n_ride]
        o_nat, m_nat, l_nat = (o1, o4, o16), (m1, m4, m16), (l1, l4, l16)
        ride_args = (ride_scatter, ride_in, ride_out) + tuple(refs[28 + 2 * n_ride:])
        step = pl.program_id(0) * n_hp + pl.program_id(1)

        @pl.when(step == 0)
        def _():
            _exchange_start(*ride_args)

        dm_refs = (dm1, dm4, dm16)
        _qk_normalise(q_ref, gq_ref, same_ref, qf, seq, HEAD_DIM ** -0.5)
        _qk_normalise(k_ref, gk_ref, same_ref, kf, seq, 1.0)
        slopes = (sl_ref[0:1, 0:1], sl_ref[1:2, 0:1])
        for pi, dil in enumerate(DILATIONS):
            seg = seq // dil
            kw = min(2 * Q_BLOCK, seg)
            _permute_rows_by_head(qp, qf, dil, seq, tmp)
            _permute_rows(kp, kf, dil, seq, tmp)
            _permute_rows(vp, v_ref, dil, seq, tmp)

            def blk(it, carry, seg=seg, kw=kw, pi=pi, dst=oml_p):
                first = _first_head(Q_BLOCK)
                chains = [(sub, h) for sub in range(ATT_FWD_UNROLL) for h in range(2)]
                win = [_window(it * ATT_FWD_UNROLL + sub, seg) for sub in range(ATT_FWD_UNROLL)]
                s = {}
                for sub, h in chains:
                    i0, ks, var = win[sub]
                    s[sub, h] = lax.dot_general(qp[h, pl.ds(i0, Q_BLOCK), :], kp[pl.ds(ks, kw), :], NT,
                                                preferred_element_type=F32) - slopes[h] * dm_refs[pi][var]
                m, l, p = {}, {}, {}
                for c in chains:
                    m[c] = jnp.max(s[c], axis=1, keepdims=True)
                    e = jnp.exp(s[c] - m[c])
                    l[c] = jnp.sum(e, axis=1, keepdims=True)
                    p[c] = e.astype(BF16)
                o = {}
                for sub, h in chains:
                    o[sub, h] = jnp.dot(p[sub, h], vp[pl.ds(win[sub][1], kw), :], preferred_element_type=F32)
                packed = [jnp.concatenate([jnp.where(first, t[sub, 0], t[sub, 1]) for t in (o, m, l)], axis=1)
                          for sub in range(ATT_FWD_UNROLL)]
                span = ATT_FWD_UNROLL * Q_BLOCK
                dst[pl.ds(pl.multiple_of(it * span, span), span), :] = jnp.concatenate(packed, axis=0)
                return carry

            lax.fori_loop(0, seq // (Q_BLOCK * ATT_FWD_UNROLL), blk, 0)
            for n, nat in enumerate((o_nat[pi], m_nat[pi], l_nat[pi])):
                _scatter_segments(nat, lambda start, size, n=n: oml_p[pl.ds(start, size), pl.ds(n * LANES, LANES)],
                                  dil, seq, tmp, accumulate=False)

        def merge(ci, carry):
            rows = pl.ds(pl.multiple_of(ci * ATT_ROWS, ATT_ROWS), ATT_ROWS)
            ms = [m_nat[pi][rows, :] for pi in range(3)]
            m_all = jnp.maximum(jnp.maximum(ms[0], ms[1]), ms[2])
            es = [jnp.exp(m - m_all) for m in ms]
            l_all = sum(l_nat[pi][rows, :] * es[pi] for pi in range(3))
            inv = 1.0 / l_all
            o = sum(o_nat[pi][rows, :] * (es[pi] * inv) for pi in range(3))
            y_ref[rows, :] = o.astype(BF16)
            lse_ref[rows, :] = m_all + jnp.log(l_all)
            return carry

        lax.fori_loop(0, seq // ATT_ROWS, merge, 0)

        @pl.when(step == n_seq * n_hp - 1)
        def _():
            _exchange_wait(*ride_args)

    def col(off):
        return pl.BlockSpec((None, seq, LANES), lambda b, hp: (b, 0, col0 + off * n_hp + hp))

    def whole(arr):
        return pl.BlockSpec(arr.shape, lambda b, hp: (0,) * arr.ndim)

    rows_f32 = pltpu.VMEM((seq, LANES), F32)
    rows_bf16 = pltpu.VMEM((seq, LANES), BF16)
    return _call(
        body, name="attn_fwd", grid=(n_seq, n_hp),
        in_specs=[col(0), col(1), col(2), whole(g_q2), whole(g_k2),
                  pl.BlockSpec((None, 8, LANES), lambda b, hp: (hp, 0, 0)),
                  whole(dms[0]), whole(dms[1]), whole(dms[2]), whole(same)] + [ANY_SPEC] * n_ride,
        out_specs=[pl.BlockSpec((None, seq, LANES), lambda b, hp: (b, 0, hp)),
                   pl.BlockSpec((None, seq, LANES), lambda b, hp: (b, 0, hp))] + [ANY_SPEC] * n_ride,
        out_shape=[_sds((n_seq, seq, D_ATT), BF16), _sds((n_seq, seq, D_ATT), F32)] + _exchange_shapes(ride),
        scratch_shapes=[rows_f32, rows_f32, pltpu.VMEM((2, seq, LANES), BF16), rows_bf16, rows_bf16]
        + [pltpu.VMEM((seq, 3 * LANES), F32)] + [rows_f32] * 10 + _exchange_sems(n_ride),
        compiler_params=_params(("arbitrary", "arbitrary")),
    )(proj3, proj3, proj3, g_q2, g_k2, _alibi_rows(), *dms, same, *[a for a, _ in ride])


def _attn_bwd(proj3, do3, y_att3, lse3, g_q2, g_k2, ride):
    n_seq, seq, _ = proj3.shape
    dms = [_distance_mats(d, seq // d) for d in DILATIONS]
    same = _same_head()
    col0 = 2 * D_CONV // LANES
    n_hp = D_ATT // LANES

    n_ride = len(ride)
    ride_scatter = [s for _, s in ride]

    def body(*refs):
        (q_ref, k_ref, v_ref, do_ref, o_ref, lse_ref, gq_ref, gk_ref, sl_ref, dm1, dm4, dm16,
         same_ref) = refs[:13]
        ride_in = refs[13:13 + n_ride]
        dq_ref, dk_ref, dv_ref, dg_ref = refs[13 + n_ride:17 + n_ride]
        ride_out = refs[17 + n_ride:17 + 2 * n_ride]
        (qf, kf, qp, dop, kp, vp, sn, sp, dqp, dkp, dvp, dqn, dkn, dvn,
         tmp) = refs[17 + 2 * n_ride:32 + 2 * n_ride]
        ride_args = (ride_scatter, ride_in, ride_out) + tuple(refs[32 + 2 * n_ride:])
        dm_refs = (dm1, dm4, dm16)
        step = pl.program_id(0) * n_hp + pl.program_id(1)

        @pl.when(step == 0)
        def _():
            _exchange_start(*ride_args)
            dg_ref[...] = jnp.zeros_like(dg_ref)

        _qk_normalise(q_ref, gq_ref, same_ref, qf, seq, HEAD_DIM ** -0.5)
        _qk_normalise(k_ref, gk_ref, same_ref, kf, seq, 1.0)

        def stats(ci, carry):
            rows = pl.ds(pl.multiple_of(ci * ATT_ROWS, ATT_ROWS), ATT_ROWS)
            first = _first_head(ATT_ROWS)
            sn[0, rows, :], sn[1, rows, :] = _per_head(lse_ref[rows, :], first)
            prod = do_ref[rows, :] * o_ref[rows, :].astype(F32)
            sn[2, rows, :], sn[3, rows, :] = _per_head(_head_sum(prod, same_ref), first)
            return carry

        lax.fori_loop(0, seq // ATT_ROWS, stats, 0)
        slopes = (sl_ref[0:1, 0:1], sl_ref[1:2, 0:1])
        half = seq // (Q_BLOCK * ATT_UNROLL)
        region = seq // ATT_UNROLL

        for pi, dil in enumerate(DILATIONS):
            seg = seq // dil
            kw = min(2 * Q_BLOCK, seg)
            _permute_rows_by_head(qp, qf, dil, seq, tmp)
            _permute_rows_by_head(dop, do_ref, dil, seq, tmp)
            _permute_rows(kp, kf, dil, seq, tmp)
            _permute_rows(vp, v_ref, dil, seq, tmp)
            if dil == 1:
                st = sn
            else:
                st = sp
                for n in range(4):
                    _permute_rows(sp.at[n], sn.at[n], dil, seq, tmp)
            for sub in range(ATT_UNROLL):
                lo, hi = max(sub * region - RADIUS, 0), min((sub + 1) * region + RADIUS, seq)
                dkp[sub, pl.ds(lo, hi - lo), :] = jnp.zeros((hi - lo, LANES), F32)
                dvp[sub, pl.ds(lo, hi - lo), :] = jnp.zeros((hi - lo, LANES), F32)

            def blk(it, carry, seg=seg, kw=kw, pi=pi, st=st):
                first = _first_head(Q_BLOCK)
                chains = [(sub, h) for sub in range(ATT_UNROLL) for h in range(2)]
                win = [_window(it + sub * half, seg) for sub in range(ATT_UNROLL)]
                qrows = [pl.ds(w[0], Q_BLOCK) for w in win]
                krows = [pl.ds(w[1], kw) for w in win]

                def over_keys(n, sub):
                    t = st[n, qrows[sub], :]
                    return t if kw == LANES else jnp.concatenate([t] * (kw // LANES), axis=1)

                s, dp = {}, {}
                for sub, h in chains:
                    s[sub, h] = lax.dot_general(qp[h, qrows[sub], :], kp[krows[sub], :], NT,
                                                preferred_element_type=F32) - slopes[h] * dm_refs[pi][win[sub][2]]
                    dp[sub, h] = lax.dot_general(dop[h, qrows[sub], :], vp[krows[sub], :], NT,
                                                 preferred_element_type=F32)
                p, ds = {}, {}
                for sub, h in chains:
                    e = jnp.exp(s[sub, h] - over_keys(h, sub))
                    ds[sub, h] = (e * (dp[sub, h] - over_keys(2 + h, sub))).astype(BF16)
                    p[sub, h] = e.astype(BF16)
                dq, dk, dv = {}, {}, {}
                for sub, h in chains:
                    dq[sub, h] = jnp.dot(ds[sub, h], kp[krows[sub], :], preferred_element_type=F32)
                    dk[sub, h] = lax.dot_general(ds[sub, h], qp[h, qrows[sub], :], TN, preferred_element_type=F32)
                    dv[sub, h] = lax.dot_general(p[sub, h], dop[h, qrows[sub], :], TN, preferred_element_type=F32)
                for sub in range(ATT_UNROLL):
                    dqp[qrows[sub], :] = jnp.where(first, dq[sub, 0], dq[sub, 1])
                    dkp[sub, krows[sub], :] += dk[sub, 0] + dk[sub, 1]
                    dvp[sub, krows[sub], :] += dv[sub, 0] + dv[sub, 1]
                return carry

            lax.fori_loop(0, half, blk, 0)
            if dil == 1:
                for sub in range(ATT_UNROLL):
                    r0 = sub * region
                    pieces = [(r0, RADIUS, [sub - 1, sub] if sub > 0 else [sub]),
                              (r0 + RADIUS, region - 2 * RADIUS, [sub]),
                              (r0 + region - RADIUS, RADIUS, [sub, sub + 1] if sub < ATT_UNROLL - 1 else [sub])]
                    for start, size, owners in pieces:
                        rows = pl.ds(start, size)
                        dqn[rows, :] = dqp[rows, :]
                        dkn[rows, :] = sum(dkp[o, rows, :] for o in owners)
                        dvn[rows, :] = sum(dvp[o, rows, :] for o in owners)
            else:
                _scatter_segments(dqn, lambda start, size: dqp[pl.ds(start, size), :], dil, seq, tmp, accumulate=True)
                for nat, acc in ((dkn, dkp), (dvn, dvp)):
                    _scatter_segments(nat, lambda start, size, acc=acc: acc[start // region, pl.ds(start, size), :],
                                      dil, seq, tmp, accumulate=True)

        def finish(ci, carry):
            rows = pl.ds(pl.multiple_of(ci * ATT_ROWS, ATT_ROWS), ATT_ROWS)
            for src_ref, g_ref, dn, dst_ref, scale, row in (
                    (q_ref, gq_ref, dqn, dq_ref, HEAD_DIM ** -0.5, 0), (k_ref, gk_ref, dkn, dk_ref, 1.0, 1)):
                xv = src_ref[rows, :]
                r = lax.rsqrt(_head_mean(xv * xv, same_ref) + EPS)
                xhat = xv * r
                d = dn[rows, :] * scale
                dg_ref[row:row + 1, :] += jnp.sum(d * xhat, axis=0, keepdims=True)
                dxh = d * g_ref[...]
                dst_ref[rows, :] = (r * (dxh - xhat * _head_mean(dxh * xhat, same_ref))).astype(BF16)
            dv_ref[rows, :] = dvn[rows, :].astype(BF16)
            return carry

        lax.fori_loop(0, seq // ATT_ROWS, finish, 0)

        @pl.when(step == n_seq * n_hp - 1)
        def _():
            _exchange_wait(*ride_args)

    def col(off):
        return pl.BlockSpec((None, seq, LANES), lambda b, hp: (b, 0, col0 + off * n_hp + hp))

    def whole(arr):
        return pl.BlockSpec(arr.shape, lambda b, hp: (0,) * arr.ndim)

    att = pl.BlockSpec((None, seq, LANES), lambda b, hp: (b, 0, hp))
    rows_f32 = pltpu.VMEM((seq, LANES), F32)
    rows_bf16 = pltpu.VMEM((seq, LANES), BF16)
    by_head_bf16 = pltpu.VMEM((2, seq, LANES), BF16)
    per_sub_f32 = pltpu.VMEM((ATT_UNROLL, seq, LANES), F32)
    stats_f32 = pltpu.VMEM((4, seq, LANES), F32)
    return _call(
        body, name="attn_bwd", grid=(n_seq, n_hp),
        in_specs=[col(0), col(1), col(2), att, att, att, whole(g_q2), whole(g_k2),
                  pl.BlockSpec((None, 8, LANES), lambda b, hp: (hp, 0, 0)),
                  whole(dms[0]), whole(dms[1]), whole(dms[2]), whole(same)] + [ANY_SPEC] * n_ride,
        out_specs=[att, att, att, pl.BlockSpec((8, LANES), lambda b, hp: (0, 0))] + [ANY_SPEC] * n_ride,
        out_shape=[_sds((n_seq, seq, D_ATT), BF16)] * 3 + [_sds((8, LANES), F32)] + _exchange_shapes(ride),
        scratch_shapes=[rows_f32, rows_f32, by_head_bf16, by_head_bf16, rows_bf16, rows_bf16, stats_f32, stats_f32,
                        rows_f32, per_sub_f32, per_sub_f32, rows_f32, rows_f32, rows_f32, rows_f32]
        + _exchange_sems(n_ride),
        compiler_params=_params(("arbitrary", "arbitrary")),
    )(proj3, proj3, proj3, do3, y_att3, lse3, g_q2, g_k2, _alibi_rows(), *dms, same, *[a for a, _ in ride])


def _mix_out(uc2, y_att2, x2, mod8, g_ln, b_ln, g_ffn, w_out, seq, tm=512):
    tokens = x2.shape[0]
    per_seq = seq // tm

    def body(uc_ref, ya_ref, x_ref, m_ref, gl_ref, bl_ref, gf_ref, w_ref, yc_ref, mix_ref, x1_ref, h2_ref):
        uc = uc_ref[...]
        mu = jnp.mean(uc, axis=-1, keepdims=True)
        cen = uc - mu
        rs = lax.rsqrt(jnp.mean(cen * cen, axis=-1, keepdims=True) + EPS)
        z = cen * rs * gl_ref[...] + bl_ref[...]
        yc = (z * _sig(z)).astype(BF16)
        yc_ref[...] = yc
        mix = (jnp.dot(yc, w_ref[pl.ds(0, D_CONV), :], preferred_element_type=F32)
               + jnp.dot(ya_ref[...], w_ref[pl.ds(D_CONV, D_ATT), :], preferred_element_type=F32))
        mix_ref[...] = mix
        x1 = x_ref[...] + m_ref[2:3, :] * mix
        x1_ref[...] = x1
        r = lax.rsqrt(jnp.mean(x1 * x1, axis=-1, keepdims=True) + EPS)
        h2_ref[...] = ((x1 * r * gf_ref[...]) * (1.0 + m_ref[4:5, :]) + m_ref[3:4, :]).astype(BF16)

    def rows(width):
        return pl.BlockSpec((tm, width), lambda i: (i, 0))

    def vec(width):
        return pl.BlockSpec((1, width), lambda i: (0, 0))

    return _call(
        body, name="mix_out", grid=(tokens // tm,),
        in_specs=[rows(D_CONV), rows(D_ATT), rows(D_MODEL),
                  pl.BlockSpec((None, 8, D_MODEL), lambda i: (i // per_seq, 0, 0)),
                  vec(D_CONV), vec(D_CONV), vec(D_MODEL),
                  pl.BlockSpec((D_MODEL, D_MODEL), lambda i: (0, 0))],
        out_specs=[rows(D_CONV), rows(D_MODEL), rows(D_MODEL), rows(D_MODEL)],
        out_shape=[_sds((tokens, D_CONV), BF16), _sds((tokens, D_MODEL), F32),
                   _sds((tokens, D_MODEL), F32), _sds((tokens, D_MODEL), BF16)],
        compiler_params=_params(("parallel",)),
    )(uc2, y_att2, x2, mod8, g_ln, b_ln, g_ffn, w_out)


def _mix_out_bwd(dmix, uc2, g_ln, b_ln, w_out, tm=512):
    tokens = dmix.shape[0]

    def body(dm_ref, uc_ref, gl_ref, bl_ref, w_ref, duc_ref, do_ref, dgb_ref):
        @pl.when(pl.program_id(0) == 0)
        def _():
            dgb_ref[...] = jnp.zeros_like(dgb_ref)

        dmv = dm_ref[...]
        dyc = lax.dot_general(dmv, w_ref[pl.ds(0, D_CONV), :], NT, preferred_element_type=F32)
        do_ref[...] = lax.dot_general(dmv, w_ref[pl.ds(D_CONV, D_ATT), :], NT, preferred_element_type=F32)
        uc = uc_ref[...]
        mu = jnp.mean(uc, axis=-1, keepdims=True)
        cen = uc - mu
        rs = lax.rsqrt(jnp.mean(cen * cen, axis=-1, keepdims=True) + EPS)
        xh = cen * rs
        z = xh * gl_ref[...] + bl_ref[...]
        sg = _sig(z)
        dz = dyc * (sg * (1.0 + z * (1.0 - sg)))
        dgb_ref[0:1, :] += jnp.sum(dz * xh, axis=0, keepdims=True)
        dgb_ref[1:2, :] += jnp.sum(dz, axis=0, keepdims=True)
        dxh = dz * gl_ref[...]
        duc_ref[...] = rs * (dxh - jnp.mean(dxh, axis=-1, keepdims=True)
                             - xh * jnp.mean(dxh * xh, axis=-1, keepdims=True))

    return _call(
        body, name="mix_out_bwd", grid=(tokens // tm,),
        in_specs=[pl.BlockSpec((tm, D_MODEL), lambda i: (i, 0)),
                  pl.BlockSpec((tm, D_CONV), lambda i: (i, 0)),
                  pl.BlockSpec((1, D_CONV), lambda i: (0, 0)),
                  pl.BlockSpec((1, D_CONV), lambda i: (0, 0)),
                  pl.BlockSpec((D_MODEL, D_MODEL), lambda i: (0, 0))],
        out_specs=[pl.BlockSpec((tm, D_CONV), lambda i: (i, 0)),
                   pl.BlockSpec((tm, D_ATT), lambda i: (i, 0)),
                   pl.BlockSpec((8, D_CONV), lambda i: (0, 0))],
        out_shape=[_sds((tokens, D_CONV), F32), _sds((tokens, D_ATT), F32), _sds((8, D_CONV), F32)],
        compiler_params=_params(("arbitrary",)),
    )(dmix, uc2, g_ln, b_ln, w_out)


FF_TILE = 256
FF_ROWS = 256


def _ffn_fwd(h2, w_gate, w_up, w_down, x1, target, mod8, seq, tm=1024):
    tokens = h2.shape[0]
    per_seq = seq // tm
    n_seq = tokens // seq
    last = D_FF // FF_TILE - 1

    def body(h_ref, wg_ref, wu_ref, wd_ref, x1_ref, t_ref, m_ref, gate_ref, up_ref, dy_ref, df_ref, sq_ref, dgf_ref,
             f_ref):
        i, j = pl.program_id(0), pl.program_id(1)

        @pl.when(j == 0)
        def _():
            f_ref[...] = jnp.zeros_like(f_ref)

        @pl.when((j == 0) & (i == 0))
        def _():
            sq_ref[...] = jnp.zeros_like(sq_ref)

        @pl.when((j == 0) & (i % per_seq == 0))
        def _():
            dgf_ref[...] = jnp.zeros_like(dgf_ref)

        def gate_up(r):
            hv = h_ref[pl.ds(r * FF_ROWS, FF_ROWS), :]
            return (jnp.dot(hv, wg_ref[...], preferred_element_type=F32),
                    jnp.dot(hv, wu_ref[...], preferred_element_type=F32))

        ahead = gate_up(0)
        for r in range(tm // FF_ROWS):
            gate, up = ahead
            if r + 1 < tm // FF_ROWS:
                ahead = gate_up(r + 1)
            rows = pl.ds(r * FF_ROWS, FF_ROWS)
            gate_ref[rows, :] = gate.astype(BF16)
            up_ref[rows, :] = up.astype(BF16)
            act = (gate * _sig(gate) * up).astype(BF16)
            f_ref[rows, :] += jnp.dot(act, wd_ref[...], preferred_element_type=F32)

        @pl.when(j == last)
        def _():
            gate_f = m_ref[5:6, :]
            for r in range(tm // FF_ROWS):
                rows = pl.ds(r * FF_ROWS, FF_ROWS)
                fv = f_ref[rows, :]
                diff = x1_ref[rows, :] + gate_f * fv - t_ref[rows, :]
                sq_ref[0:1, :] += jnp.sum(diff * diff, axis=0, keepdims=True)
                dy = diff * (1.0 / D_MODEL)
                dy_ref[rows, :] = dy
                df_ref[rows, :] = (gate_f * dy).astype(BF16)
                dgf_ref[0:1, :] += jnp.sum(dy * fv, axis=0, keepdims=True)

    rows_spec = pl.BlockSpec((tm, D_MODEL), lambda i, j: (i, 0))
    per = pl.BlockSpec((None, 8, D_MODEL), lambda i, j: (i // per_seq, 0, 0))
    tile = pl.BlockSpec((tm, FF_TILE), lambda i, j: (i, j))
    return _call(
        body, name="ffn_fwd", grid=(tokens // tm, D_FF // FF_TILE),
        in_specs=[rows_spec,
                  pl.BlockSpec((D_MODEL, FF_TILE), lambda i, j: (0, j)),
                  pl.BlockSpec((D_MODEL, FF_TILE), lambda i, j: (0, j)),
                  pl.BlockSpec((FF_TILE, D_MODEL), lambda i, j: (j, 0)),
                  rows_spec, rows_spec, per],
        out_specs=[tile, tile, rows_spec, rows_spec, pl.BlockSpec((8, D_MODEL), lambda i, j: (0, 0)), per],
        out_shape=[_sds((tokens, D_FF), BF16), _sds((tokens, D_FF), BF16), _sds((tokens, D_MODEL), F32),
                   _sds((tokens, D_MODEL), BF16), _sds((8, D_MODEL), F32), _sds((n_seq, 8, D_MODEL), F32)],
        scratch_shapes=[pltpu.VMEM((tm, D_MODEL), F32)],
        compiler_params=_params(("arbitrary", "arbitrary")),
    )(h2, w_gate, w_up, w_down, x1, target, mod8)


def _ffn_bwd(df, gate, up, w_gate, w_up, w_down, x1, dy, mix, mod8, g_ffn, seq, tm=1024):
    tokens = df.shape[0]
    per_seq = seq // tm
    n_seq = tokens // seq
    last = D_FF // FF_TILE - 1

    def body(df_ref, gate_ref, up_ref, wg_ref, wu_ref, wd_ref, m_ref, g_ref, x1_hbm, dy_hbm, mix_hbm,
             dgate_ref, dup_ref, act_ref, dx1_ref, dmix_ref, dg_ref, dm_ref, dh_ref, late, late_sems):
        i, j = pl.program_id(0), pl.program_id(1)
        my_rows = pl.ds(pl.multiple_of(i * tm, tm), tm)
        fetches = [pltpu.make_async_copy(src.at[my_rows, :], late.at[n], late_sems.at[n])
                   for n, src in enumerate((x1_hbm, dy_hbm, mix_hbm))]

        @pl.when(j == 0)
        def _():
            dh_ref[...] = jnp.zeros_like(dh_ref)
            for cp in fetches:
                cp.start()

        @pl.when((j == 0) & (i == 0))
        def _():
            dg_ref[...] = jnp.zeros_like(dg_ref)

        @pl.when((j == 0) & (i % per_seq == 0))
        def _():
            dm_ref[...] = jnp.zeros_like(dm_ref)

        def d_act(r):
            return lax.dot_general(df_ref[pl.ds(r * FF_ROWS, FF_ROWS), :], wd_ref[...], NT,
                                   preferred_element_type=F32)

        ahead = d_act(0)
        for r in range(tm // FF_ROWS):
            dact = ahead
            if r + 1 < tm // FF_ROWS:
                ahead = d_act(r + 1)
            rows = pl.ds(r * FF_ROWS, FF_ROWS)
            gate = gate_ref[rows, :].astype(F32)
            up = up_ref[rows, :].astype(F32)
            sg = _sig(gate)
            silu = gate * sg
            act_ref[rows, :] = (silu * up).astype(BF16)
            dup = (dact * silu).astype(BF16)
            dgate = (dact * up * (sg * (1.0 + gate * (1.0 - sg)))).astype(BF16)
            dup_ref[rows, :] = dup
            dgate_ref[rows, :] = dgate
            dh_ref[rows, :] += (lax.dot_general(dgate, wg_ref[...], NT, preferred_element_type=F32)
                                + lax.dot_general(dup, wu_ref[...], NT, preferred_element_type=F32))

        @pl.when(j == last)
        def _():
            for cp in fetches:
                cp.wait()
            g = g_ref[...]
            for r in range(tm // FF_ROWS):
                rows = pl.ds(r * FF_ROWS, FF_ROWS)
                dh = dh_ref[rows, :]
                x1v = late[0, rows, :]
                rs = lax.rsqrt(jnp.mean(x1v * x1v, axis=-1, keepdims=True) + EPS)
                xhat = x1v * rs
                dm_ref[0:1, :] += jnp.sum(dh, axis=0, keepdims=True)
                dm_ref[1:2, :] += jnp.sum(dh * (xhat * g), axis=0, keepdims=True)
                dn = dh * (1.0 + m_ref[4:5, :])
                dg_ref[0:1, :] += jnp.sum(dn * xhat, axis=0, keepdims=True)
                dxh = dn * g
                dx1 = late[1, rows, :] + rs * (dxh - xhat * jnp.mean(dxh * xhat, axis=-1, keepdims=True))
                dx1_ref[rows, :] = dx1
                dm_ref[2:3, :] += jnp.sum(dx1 * late[2, rows, :], axis=0, keepdims=True)
                dmix_ref[rows, :] = (m_ref[2:3, :] * dx1).astype(BF16)

    tile = pl.BlockSpec((tm, FF_TILE), lambda i, j: (i, j))
    rows_spec = pl.BlockSpec((tm, D_MODEL), lambda i, j: (i, 0))
    per = pl.BlockSpec((None, 8, D_MODEL), lambda i, j: (i // per_seq, 0, 0))
    return _call(
        body, name="ffn_bwd", grid=(tokens // tm, D_FF // FF_TILE),
        in_specs=[rows_spec, tile, tile,
                  pl.BlockSpec((D_MODEL, FF_TILE), lambda i, j: (0, j)),
                  pl.BlockSpec((D_MODEL, FF_TILE), lambda i, j: (0, j)),
                  pl.BlockSpec((FF_TILE, D_MODEL), lambda i, j: (j, 0)),
                  per, pl.BlockSpec((1, D_MODEL), lambda i, j: (0, 0)), ANY_SPEC, ANY_SPEC, ANY_SPEC],
        out_specs=[tile, tile, tile, rows_spec, rows_spec, pl.BlockSpec((8, D_MODEL), lambda i, j: (0, 0)), per],
        out_shape=[_sds((tokens, D_FF), BF16)] * 3 + [_sds((tokens, D_MODEL), F32), _sds((tokens, D_MODEL), BF16),
                                                    _sds((8, D_MODEL), F32), _sds((n_seq, 8, D_MODEL), F32)],
        scratch_shapes=[pltpu.VMEM((tm, D_MODEL), F32), pltpu.VMEM((3, tm, D_MODEL), F32),
                        pltpu.SemaphoreType.DMA((3,))],
        compiler_params=_params(("arbitrary", "arbitrary")),
    )(df, gate, up, w_gate, w_up, w_down, mod8, g_ffn, x1, dy, mix)


def _mix_in_bwd(d_a, d_g, d_q, d_k, d_v, w_in, x2, dx1, mod8, g_mix, seq, ride, tm=512):
    tokens = x2.shape[0]
    per_seq = seq // tm
    n_seq = tokens // seq
    parts = (d_a, d_g, d_q, d_k, d_v)
    width = D_CONV
    n_ride = len(ride)
    ride_scatter = [s for _, s in ride]

    def body(*refs):
        da_ref, dg_ref, dq_ref, dk_ref, dv_ref, w_ref, x_ref, dx1_ref, m_ref, g_ref = refs[:10]
        ride_in = refs[10:10 + n_ride]
        gx_ref, dgm_ref, dm_ref = refs[10 + n_ride:13 + n_ride]
        ride_args = (ride_scatter, ride_in, refs[13 + n_ride:13 + 2 * n_ride]) + tuple(refs[13 + 2 * n_ride:])
        i = pl.program_id(0)

        @pl.when(i == 0)
        def _():
            _exchange_start(*ride_args)
            dgm_ref[...] = jnp.zeros_like(dgm_ref)

        @pl.when(i % per_seq == 0)
        def _():
            dm_ref[...] = jnp.zeros_like(dm_ref)

        dh = jnp.zeros((tm, D_MODEL), F32)
        for n, ref in enumerate((da_ref, dg_ref, dq_ref, dk_ref, dv_ref)):
            dh = dh + lax.dot_general(ref[...], w_ref[:, pl.ds(n * width, width)], NT, preferred_element_type=F32)
        xv = x_ref[...]
        r = lax.rsqrt(jnp.mean(xv * xv, axis=-1, keepdims=True) + EPS)
        xhat = xv * r
        g = g_ref[...]
        dm_ref[0:1, :] += jnp.sum(dh, axis=0, keepdims=True)
        dm_ref[1:2, :] += jnp.sum(dh * (xhat * g), axis=0, keepdims=True)
        dn = dh * (1.0 + m_ref[1:2, :])
        dgm_ref[0:1, :] += jnp.sum(dn * xhat, axis=0, keepdims=True)
        dxh = dn * g
        gx_ref[...] = dx1_ref[...] + r * (dxh - xhat * jnp.mean(dxh * xhat, axis=-1, keepdims=True))

        @pl.when(i == tokens // tm - 1)
        def _():
            _exchange_wait(*ride_args)

    rows = pl.BlockSpec((tm, D_MODEL), lambda i: (i, 0))
    half = pl.BlockSpec((tm, width), lambda i: (i, 0))
    per = pl.BlockSpec((None, 8, D_MODEL), lambda i: (i // per_seq, 0, 0))
    return _call(
        body, name="mix_in_bwd", grid=(tokens // tm,),
        in_specs=[half] * 5 + [pl.BlockSpec((D_MODEL, D_IN), lambda i: (0, 0)), rows, rows, per,
                               pl.BlockSpec((1, D_MODEL), lambda i: (0, 0))] + [ANY_SPEC] * n_ride,
        out_specs=[rows, pl.BlockSpec((8, D_MODEL), lambda i: (0, 0)), per] + [ANY_SPEC] * n_ride,
        out_shape=[_sds((tokens, D_MODEL), F32), _sds((8, D_MODEL), F32), _sds((n_seq, 8, D_MODEL), F32)]
        + _exchange_shapes(ride),
        scratch_shapes=_exchange_sems(n_ride),
        compiler_params=_params(("arbitrary",)),
    )(*parts, w_in, x2, dx1, mod8, g_mix, *[a for a, _ in ride])


def _grad_matmul_parts(a_parts, b_parts, name, tk=1024):
    tokens = a_parts[0].shape[0]
    na, nb = len(a_parts), len(b_parts)
    ma, nbw = a_parts[0].shape[1], b_parts[0].shape[1]

    n_k = tokens // tk

    def body(*refs):
        a_refs, b_refs, o_ref, acc = refs[:na], refs[na:na + nb], refs[na + nb], refs[na + nb + 1]

        @pl.when(pl.program_id(0) == 0)
        def _():
            acc[...] = jnp.zeros_like(acc)

        for i in range(na):
            for j in range(nb):
                acc[pl.ds(i * ma, ma), pl.ds(j * nbw, nbw)] += lax.dot_general(
                    a_refs[i][...], b_refs[j][...], TN, preferred_element_type=F32)

        @pl.when(pl.program_id(0) == n_k - 1)
        def _():
            o_ref[...] = acc[...].astype(o_ref.dtype)

    return _call(
        body, name=name, grid=(n_k,),
        in_specs=[pl.BlockSpec((tk, ma), lambda k: (k, 0))] * na + [pl.BlockSpec((tk, nbw), lambda k: (k, 0))] * nb,
        out_specs=pl.BlockSpec((na * ma, nb * nbw), lambda k: (0, 0)),
        out_shape=_sds((na * ma, nb * nbw), BF16),
        scratch_shapes=[pltpu.VMEM((na * ma, nb * nbw), F32)],
        compiler_params=_params(("arbitrary",)),
    )(*a_parts, *b_parts)


def _grad_matmul(a, b, name, tmo, tno, tk=1024):
    tokens, m = a.shape
    n = b.shape[1]
    n_k = tokens // tk

    def body(a_ref, b_ref, o_ref, acc):
        @pl.when(pl.program_id(2) == 0)
        def _():
            acc[...] = jnp.zeros_like(acc)

        acc[...] += lax.dot_general(a_ref[...], b_ref[...], TN, preferred_element_type=F32)

        @pl.when(pl.program_id(2) == n_k - 1)
        def _():
            o_ref[...] = acc[...].astype(o_ref.dtype)

    return _call(
        body, name=name, grid=(m // tmo, n // tno, n_k),
        in_specs=[pl.BlockSpec((tk, tmo), lambda i, j, k: (k, i)),
                  pl.BlockSpec((tk, tno), lambda i, j, k: (k, j))],
        out_specs=pl.BlockSpec((tmo, tno), lambda i, j, k: (i, j)),
        out_shape=_sds((m, n), BF16),
        scratch_shapes=[pltpu.VMEM((tmo, tno), F32)],
        compiler_params=_params(("parallel", "parallel", "arbitrary")),
    )(a, b)


def _adamw(w, m, v, g, name, n_parts=0, tr=256):
    rows, cols = w.shape
    tr = min(tr, rows)
    c1 = 1.0 - ADAM_B1 ** ADAM_STEP
    c2 = 1.0 - ADAM_B2 ** ADAM_STEP

    def body(w_ref, m_ref, v_ref, g_ref, go_ref, d_ref, mo_ref, vo_ref):
        if n_parts:
            gv = g_ref[0].astype(F32)
            for p in range(1, n_parts):
                gv = gv + g_ref[p].astype(F32)
        else:
            gv = g_ref[...]
        go_ref[...] = gv
        mn = ADAM_B1 * m_ref[...] + (1.0 - ADAM_B1) * gv
        vn = ADAM_B2 * v_ref[...] + (1.0 - ADAM_B2) * (gv * gv)
        mo_ref[...] = mn
        vo_ref[...] = vn
        d_ref[...] = -ADAM_LR * ((mn / c1) / (jnp.sqrt(vn / c2) + ADAM_EPS) + ADAM_WD * w_ref[...])

    blk = pl.BlockSpec((tr, cols), lambda i: (i, 0))
    g_spec = pl.BlockSpec((n_parts, tr, cols), lambda i: (0, i, 0)) if n_parts else blk
    return _call(
        body, name=name, grid=(rows // tr,),
        in_specs=[blk, blk, blk, g_spec], out_specs=[blk] * 4,
        out_shape=[_sds((rows, cols), F32)] * 4,
        compiler_params=_params(("parallel",)),
    )(w, m, v, g)


def _cols_to_full(blocks):
    n, r, c = blocks.shape
    return jnp.transpose(blocks, (1, 0, 2)).reshape(r, n * c)


def _full_to_cols(full, n=N_DEV):
    r, c = full.shape
    return jnp.transpose(full.reshape(r, n, c // n), (1, 0, 2))


def _pad_lanes(v, width):
    return jnp.pad(v, ((0, 0), (0, width - v.shape[1])))


def kernel(x, c, w_ada, b_ada, g_mix, w_in, w_dw, b_dw, g_conv_ln, b_conv_ln, g_q, g_k, w_out, g_ffn, w_gate, w_up, w_down, loss_target, m_w_ada, m_b_ada, m_g_mix, m_w_in, m_w_dw, m_b_dw, m_g_conv_ln, m_b_conv_ln, m_g_q, m_g_k, m_w_out, m_g_ffn, m_w_gate, m_w_up, m_w_down, v_w_ada, v_b_ada, v_g_mix, v_w_in, v_w_dw, v_b_dw, v_g_conv_ln, v_b_conv_ln, v_g_q, v_g_k, v_w_out, v_g_ffn, v_w_gate, v_w_up, v_w_down):
    n_seq, seq, _ = x.shape
    tokens = n_seq * seq
    me = 4 * lax.axis_index("x") + 2 * lax.axis_index("y") + lax.axis_index("c")
    ada_cols = w_ada.shape[2]
    dw_cols = w_dw.shape[2]

    (c_g, w_in_g, w_dw_g) = _gather_by_chip([c, w_in[0].astype(BF16), w_dw[0]], "gather_weights")
    c_all = c_g.reshape(N_DEV * n_seq, D_MODEL)
    w_in_f = _cols_to_full(w_in_g)
    w_dw_f = _cols_to_full(w_dw_g)

    b_cols = lax.dynamic_slice(b_ada, (0, me * ada_cols), (1, ada_cols))
    mod_cols = _ada_fwd(c_all, w_ada[0], b_cols)
    (mod_g,) = _exchange([(mod_cols, False)], "gather_mod")
    mod_mine = lax.dynamic_slice(mod_g, (0, me * n_seq, 0), (N_DEV, n_seq, ada_cols))
    mod = jnp.transpose(mod_mine, (1, 0, 2)).reshape(n_seq, N_MOD, D_MODEL)
    mod8 = jnp.pad(mod, ((0, 0), (0, 8 - N_MOD), (0, 0)))

    x2 = x.reshape(tokens, D_MODEL)
    h1, proj = _mix_in(x2, mod8, g_mix, w_in_f, seq)
    proj3 = proj.reshape(n_seq, seq, D_IN)
    uc3 = _conv_fwd(proj3, w_dw_f, b_dw)
    g_q2, g_k2 = jnp.tile(g_q, (1, 2)), jnp.tile(g_k, (1, 2))
    y_att3, lse3, w_out_g, w_gate_g, w_up_g, w_down_g = _attn_fwd(
        proj3, g_q2, g_k2,
        [(w_out[0].astype(BF16), False), (w_gate[0].astype(BF16), False), (w_up[0].astype(BF16), False),
         (w_down[0].astype(BF16), False)])
    w_out_f = w_out_g.reshape(D_MODEL, D_MODEL)
    w_gate_f = _cols_to_full(w_gate_g)
    w_up_f = _cols_to_full(w_up_g)
    w_down_f = w_down_g.reshape(D_FF, D_MODEL)
    uc2 = uc3.reshape(tokens, D_CONV)
    y_att2 = y_att3.reshape(tokens, D_ATT)
    y_conv, mix, x1, h2 = _mix_out(uc2, y_att2, x2, mod8, g_conv_ln, b_conv_ln, g_ffn, w_out_f, seq)
    gate, up, dy, df, sq, dgate_f = _ffn_fwd(
        h2, w_gate_f, w_up_f, w_down_f, x1, loss_target.reshape(tokens, D_MODEL), mod8, seq)

    dgate, dup, act, dx1, dmix, dg_ffn, dmod_f = _ffn_bwd(
        df, gate, up, w_gate_f, w_up_f, w_down_f, x1, dy, mix, mod8, g_ffn, seq)
    duc2, do2, dgb_ln = _mix_out_bwd(dmix, uc2, g_conv_ln, b_conv_ln, w_out_f)
    d_a3, d_g3, dw_dw_p, db_dw_p = _conv_bwd(duc2.reshape(n_seq, seq, D_CONV), proj3, w_dw_f)
    gw_gate = _grad_matmul(h2, dgate, "grad_w_gate", D_MODEL, D_FF // 2)
    gw_up = _grad_matmul(h2, dup, "grad_w_up", D_MODEL, D_FF // 2)
    gw_down = _grad_matmul(act, df, "grad_w_down", D_FF // 2, D_MODEL)
    gw_out = _grad_matmul_parts([y_conv, y_att2], [dmix], "grad_w_out")
    d_q3, d_k3, d_v3, dg_qk, p_gate, p_up, p_down, p_out = _attn_bwd(
        proj3, do2.reshape(n_seq, seq, D_ATT), y_att3, lse3, g_q2, g_k2,
        [(_full_to_cols(gw_gate).astype(BF16), True), (_full_to_cols(gw_up).astype(BF16), True),
         (gw_down.astype(BF16).reshape(N_DEV, D_FF // N_DEV, D_MODEL), True),
         (gw_out.astype(BF16).reshape(N_DEV, D_MODEL // N_DEV, D_MODEL), True)])
    flat = lambda t: t.reshape(tokens, t.shape[-1])
    d_a, d_g, d_q, d_k, d_v = flat(d_a3), flat(d_g3), flat(d_q3), flat(d_k3), flat(d_v3)
    gw_in = _grad_matmul_parts([h1], [d_a, d_g, d_q, d_k, d_v], "grad_w_in")
    grad_x2, dg_mix, dmod_m, p_in = _mix_in_bwd(
        d_a, d_g, d_q, d_k, d_v, w_in_f, x2, dx1, mod8, g_mix, seq, [(_full_to_cols(gw_in).astype(BF16), True)])

    dmod = jnp.concatenate([dmod_m[:, 0], dmod_m[:, 1], dmod_f[:, 2], dmod_f[:, 0], dmod_f[:, 1], dgate_f[:, 0]], axis=1)
    dg_q = dg_qk[0:1, 0:HEAD_DIM] + dg_qk[0:1, HEAD_DIM:]
    dg_k = dg_qk[1:2, 0:HEAD_DIM] + dg_qk[1:2, HEAD_DIM:]
    loss_part = (0.5 / D_MODEL) * jnp.sum(sq[0:1, :], axis=1, keepdims=True)
    small = jnp.concatenate(
        [dg_mix[0:1], dg_ffn[0:1], db_dw_p[0:1], dgb_ln[0:1], dgb_ln[1:2],
         _pad_lanes(dg_q, LANES), _pad_lanes(dg_k, LANES), _pad_lanes(loss_part, LANES)], axis=1)
    n_small = small.shape[1] - LANES

    (dmod_g, small_g, dw_g) = _exchange([(dmod, False), (small, False), (dw_dw_p, False)], "gather_small_grads")

    dmod_all = dmod_g.reshape(N_DEV * n_seq, N_MOD * D_MODEL)
    dmod_cols = lax.dynamic_slice(dmod_all, (0, me * ada_cols), (N_DEV * n_seq, ada_cols))
    gw_ada, gb_ada = _ada_bwd(c_all, dmod_cols, dmod_all)

    res = {}
    res["w_ada"] = _adamw(w_ada[0], m_w_ada[0], v_w_ada[0], gw_ada, "adamw_w_ada")
    res["b_ada"] = _adamw(b_ada, m_b_ada, v_b_ada, gb_ada, "adamw_b_ada")
    res["w_in"] = _adamw(w_in[0], m_w_in[0], v_w_in[0], p_in, "adamw_w_in", N_DEV)
    res["w_out"] = _adamw(w_out[0], m_w_out[0], v_w_out[0], p_out, "adamw_w_out", N_DEV)
    res["w_gate"] = _adamw(w_gate[0], m_w_gate[0], v_w_gate[0], p_gate, "adamw_w_gate", N_DEV)
    res["w_up"] = _adamw(w_up[0], m_w_up[0], v_w_up[0], p_up, "adamw_w_up", N_DEV)
    res["w_down"] = _adamw(w_down[0], m_w_down[0], v_w_down[0], p_down, "adamw_w_down", N_DEV, tr=176)
    dw_mine = lax.dynamic_slice(dw_g, (0, 0, me * dw_cols), (N_DEV, CONV_WIDTH, dw_cols))
    res["w_dw"] = _adamw(w_dw[0], m_w_dw[0], v_w_dw[0], dw_mine, "adamw_w_dw", N_DEV)

    small_names = ["g_mix", "g_ffn", "b_dw", "g_conv_ln", "b_conv_ln", "g_q", "g_k"]
    small_w = {"g_mix": (g_mix, m_g_mix, v_g_mix), "g_ffn": (g_ffn, m_g_ffn, v_g_ffn), "b_dw": (b_dw, m_b_dw, v_b_dw),
               "g_conv_ln": (g_conv_ln, m_g_conv_ln, v_g_conv_ln), "b_conv_ln": (b_conv_ln, m_b_conv_ln, v_b_conv_ln),
               "g_q": (g_q, m_g_q, v_g_q), "g_k": (g_k, m_g_k, v_g_k)}
    widths = [max(small_w[n][0].shape[1], LANES) for n in small_names]
    packed = [jnp.concatenate([_pad_lanes(small_w[n][i], wd) for n, wd in zip(small_names, widths)], axis=1) for i in range(3)]
    outs = _adamw(packed[0], packed[1], packed[2], small_g[:, :, :n_small], "adamw_small", N_DEV)
    off = 0
    for n, wd in zip(small_names, widths):
        real = small_w[n][0].shape[1]
        res[n] = tuple(o[:, off:off + real] for o in outs)
        off += wd
    loss = jnp.sum(small_g[:, 0, n_small])

    order = ["w_ada", "b_ada", "g_mix", "w_in", "w_dw", "b_dw", "g_conv_ln", "b_conv_ln", "g_q", "g_k",
             "w_out", "g_ffn", "w_gate", "w_up", "w_down"]
    lead = {"w_ada", "w_in", "w_dw", "w_out", "w_gate", "w_up", "w_down"}
    grads, deltas, new_m, new_v = [], [], [], []
    for n in order:
        g, d, mn, vn = res[n]
        g, d, mn, vn = (t[None] if n in lead else t for t in (g, d, mn, vn))
        grads.append(g)
        deltas.append(d)
        new_m.append(mn)
        new_v.append(vn)
    return (loss, grad_x2.reshape(n_seq, seq, D_MODEL), *grads, *deltas, *new_m, *new_v)
```

```python
import numpy as np
import jax
import jax.numpy as jnp
from jax import lax
from jax.experimental import pallas as pl
from jax.experimental.pallas import tpu as pltpu

F32 = jnp.float32
BF16 = jnp.bfloat16

N_DEV = 8
D_MODEL = 1024
D_CONV = 512
D_ATT = 512
HEAD_DIM = 64
CONV_WIDTH = 31
D_IN = 2 * D_CONV + 3 * D_ATT
D_FF = 2816
N_MOD = 6
EPS = 1e-6
RADIUS = 64
DILATIONS = (1, 4, 16)
Q_BLOCK = 128
LANES = 128
VMEM_LIMIT = 56 * 1024 * 1024

ADAM_LR = 0.001
ADAM_B1 = 0.9
ADAM_B2 = 0.999
ADAM_EPS = 1e-08
ADAM_WD = 0.01
ADAM_STEP = 10

NT = (((1,), (1,)), ((), ()))
TN = (((0,), (0,)), ((), ()))


def _call(body, **kw):
    return pl.pallas_call(body, **kw)


def _params(sem=None, vmem=VMEM_LIMIT):
    return pltpu.CompilerParams(dimension_semantics=sem, vmem_limit_bytes=vmem)


def _sig(x):
    return 1.0 / (1.0 + jnp.exp(-x))


def _sds(shape, dtype):
    return jax.ShapeDtypeStruct(shape, dtype)


N_PEER = N_DEV - 1
ANY_SPEC = pl.BlockSpec(memory_space=pl.ANY)


def _exchange_copies(scatter, ins, outs, *sems):
    n = len(ins)
    if n == 0:
        return [], []
    send_sems, recv_sems, local_sems = sems
    x, y, c = lax.axis_index("x"), lax.axis_index("y"), lax.axis_index("c")
    me = 4 * x + 2 * y + c

    def src(a, slot):
        return ins[a].at[slot] if scatter[a] else ins[a]

    local = [pltpu.make_async_copy(src(a, me), outs[a].at[me], local_sems.at[a]) for a in range(n)]
    flights = []
    for k in range(1, N_DEV):
        px = 1 - x if k & 4 else x
        py = 1 - y if k & 2 else y
        pc = 1 - c if k & 1 else c
        pid = 4 * px + 2 * py + pc
        for a in range(n):
            i = a * N_PEER + k - 1
            send, recv = (pltpu.make_async_remote_copy(
                src_ref=src(a, pid), dst_ref=outs[a].at[slot],
                send_sem=send_sems.at[i], recv_sem=recv_sems.at[i],
                device_id=(px, py, pc), device_id_type=pl.DeviceIdType.MESH) for slot in (me, pid))
            flights.append((send, recv))
    return local, flights


def _exchange_start(*args):
    local, flights = _exchange_copies(*args)
    for cp in local:
        cp.start()
    for send, _ in flights:
        send.start()


def _exchange_wait(*args):
    local, flights = _exchange_copies(*args)
    for send, recv in flights:
        send.wait_send()
        recv.wait_recv()
    for cp in local:
        cp.wait()


def _exchange_shapes(items):
    return [_sds((N_DEV,) + tuple(arr.shape[1:] if scatter else arr.shape), arr.dtype) for arr, scatter in items]


def _exchange_sems(n):
    if n == 0:
        return []
    return [pltpu.SemaphoreType.DMA((n * N_PEER,)), pltpu.SemaphoreType.DMA((n * N_PEER,)),
            pltpu.SemaphoreType.DMA((n,))]


def _gather_by_chip(arrays, name):
    n = len(arrays)
    per = N_PEER

    def body(*refs):
        ins, outs = refs[:n], refs[n:2 * n]
        send_sems, recv_sems, local_sems = refs[2 * n:]
        x, y, c = lax.axis_index("x"), lax.axis_index("y"), lax.axis_index("c")
        sibling = (x, y, 1 - c)
        chips = [(1 - x, y), (x, 1 - y), (1 - x, 1 - y)]

        def slot(px, py, pc):
            return 4 * px + 2 * py + pc

        def copy(a, k, block, to, src=None):
            dst = outs[a].at[slot(*block)]
            return pltpu.make_async_remote_copy(
                src_ref=dst if src is None else src, dst_ref=dst,
                send_sem=send_sems.at[a * per + k], recv_sem=recv_sems.at[a * per + k],
                device_id=to, device_id_type=pl.DeviceIdType.MESH)

        me = (x, y, c)
        local = [pltpu.make_async_copy(ins[a], outs[a].at[slot(*me)], local_sems.at[a]) for a in range(n)]
        for cp in local:
            cp.start()
        first = []
        for a in range(n):
            first.append(copy(a, 0, me, sibling, src=ins[a]))
            first += [copy(a, 1 + j, me, (*chip, c), src=ins[a]) for j, chip in enumerate(chips)]
        for cp in first:
            cp.start()
        passed = []
        for j, chip in enumerate(chips):
            for a in range(n):
                copy(a, 1 + j, (*chip, c), me).wait_recv()
                fwd = copy(a, 4 + j, (*chip, c), sibling)
                fwd.start()
                passed.append(fwd)
        for a in range(n):
            copy(a, 0, sibling, me).wait_recv()
            for j, chip in enumerate(chips):
                copy(a, 4 + j, (*chip, 1 - c), me).wait_recv()
        for cp in first + passed:
            cp.wait_send()
        for cp in local:
            cp.wait()

    return _call(
        body, name=name, out_shape=_exchange_shapes([(arr, False) for arr in arrays]),
        in_specs=[ANY_SPEC] * n, out_specs=[ANY_SPEC] * n, scratch_shapes=_exchange_sems(n),
    )(*arrays)


def _exchange(items, name):
    n = len(items)
    scatter = [s for _, s in items]

    def body(*refs):
        args = (scatter, refs[:n], refs[n:2 * n]) + tuple(refs[2 * n:])
        _exchange_start(*args)
        _exchange_wait(*args)

    return _call(
        body, name=name, out_shape=_exchange_shapes(items),
        in_specs=[ANY_SPEC] * n, out_specs=[ANY_SPEC] * n, scratch_shapes=_exchange_sems(n),
    )(*[a for a, _ in items])


def _ada_fwd(c_all, w_ada, b_cols):
    def body(c_ref, w_ref, b_ref, o_ref):
        cv = c_ref[...]
        sc = (cv * _sig(cv)).astype(BF16)
        o_ref[...] = jnp.dot(sc, w_ref[...].astype(BF16), preferred_element_type=F32) + b_ref[...]

    return _call(body, name="ada_fwd", out_shape=_sds((c_all.shape[0], w_ada.shape[1]), F32),
                 compiler_params=_params())(c_all, w_ada, b_cols)


def _ada_bwd(c_all, dmod_cols, dmod_all):
    def body(c_ref, dc_ref, da_ref, gw_ref, gb_ref):
        cv = c_ref[...]
        sc = (cv * _sig(cv)).astype(BF16)
        gw_ref[...] = lax.dot_general(sc, dc_ref[...].astype(BF16), TN, preferred_element_type=F32)
        gb_ref[...] = jnp.sum(da_ref[...], axis=0, keepdims=True)

    return _call(body, name="ada_bwd",
                 out_shape=[_sds((c_all.shape[1], dmod_cols.shape[1]), F32), _sds((1, dmod_all.shape[1]), F32)],
                 compiler_params=_params())(c_all, dmod_cols, dmod_all)


MIX_ROWS = 128


def _mix_in(x2, mod8, g_mix, w_in, seq, tm=512):
    tokens = x2.shape[0]
    per_seq = seq // tm

    def body(x_ref, m_ref, g_ref, w_ref, h_ref, p_ref):
        def normed(c):
            rows = pl.ds(c * MIX_ROWS, MIX_ROWS)
            xv = x_ref[rows, :]
            r = lax.rsqrt(jnp.mean(xv * xv, axis=-1, keepdims=True) + EPS)
            hb = ((xv * r * g_ref[...]) * (1.0 + m_ref[1:2, :]) + m_ref[0:1, :]).astype(BF16)
            h_ref[rows, :] = hb
            return hb

        ahead = normed(0)
        for c in range(tm // MIX_ROWS):
            hb = ahead
            if c + 1 < tm // MIX_ROWS:
                ahead = normed(c + 1)
            p_ref[pl.ds(c * MIX_ROWS, MIX_ROWS), :] = jnp.dot(hb, w_ref[...], preferred_element_type=F32)

    return _call(
        body, name="mix_in", grid=(tokens // tm,),
        in_specs=[pl.BlockSpec((tm, D_MODEL), lambda i: (i, 0)),
                  pl.BlockSpec((None, 8, D_MODEL), lambda i: (i // per_seq, 0, 0)),
                  pl.BlockSpec((1, D_MODEL), lambda i: (0, 0)),
                  pl.BlockSpec((D_MODEL, D_IN), lambda i: (0, 0))],
        out_specs=[pl.BlockSpec((tm, D_MODEL), lambda i: (i, 0)),
                   pl.BlockSpec((tm, D_IN), lambda i: (i, 0))],
        out_shape=[_sds((tokens, D_MODEL), BF16), _sds((tokens, D_IN), F32)],
        compiler_params=_params(("parallel",)),
    )(x2, mod8, g_mix, w_in)


CONV_ROWS = 64
CONV_DW_ROWS = 32
CONV_DW_UNROLL = 4
CONV_HALO = 16


def _fill_shifted(xp, sh, seq):
    for b in range(8):
        sh[b, pl.ds(0, seq + 24), :] = xp[pl.ds(b, seq + 24), :]


def _conv_fwd(proj3, w_dw, b_dw):
    n_seq, seq, _ = proj3.shape
    n_cb = D_CONV // LANES

    def body(a_ref, g_ref, w_ref, b_ref, uc_ref, xp, sh):
        zeros = jnp.zeros((CONV_HALO, LANES), F32)
        xp[pl.ds(0, CONV_HALO), :] = zeros
        xp[pl.ds(CONV_HALO + seq, CONV_HALO), :] = zeros
        xp[pl.ds(CONV_HALO, seq), :] = a_ref[...] * _sig(g_ref[...])
        _fill_shifted(xp, sh, seq)

        def blk(i, carry):
            t0 = pl.multiple_of(i * CONV_ROWS, CONV_ROWS)
            acc = jnp.zeros((CONV_ROWS, LANES), F32)
            for j in range(CONV_WIDTH):
                jj = j + 1
                acc = acc + sh[jj % 8, pl.ds(t0 + 8 * (jj // 8), CONV_ROWS), :] * w_ref[j:j + 1, :]
            uc_ref[pl.ds(t0, CONV_ROWS), :] = acc + b_ref[...]
            return carry

        lax.fori_loop(0, seq // CONV_ROWS, blk, 0)

    return _call(
        body, name="conv_fwd", grid=(n_seq, n_cb),
        in_specs=[pl.BlockSpec((None, seq, LANES), lambda b, cb: (b, 0, cb)),
                  pl.BlockSpec((None, seq, LANES), lambda b, cb: (b, 0, n_cb + cb)),
                  pl.BlockSpec((CONV_WIDTH, LANES), lambda b, cb: (0, cb)),
                  pl.BlockSpec((1, LANES), lambda b, cb: (0, cb))],
        out_specs=pl.BlockSpec((None, seq, LANES), lambda b, cb: (b, 0, cb)),
        out_shape=_sds((n_seq, seq, D_CONV), F32),
        scratch_shapes=[pltpu.VMEM((seq + 2 * CONV_HALO, LANES), F32),
                        pltpu.VMEM((8, seq + 2 * CONV_HALO, LANES), F32)],
        compiler_params=_params(("parallel", "parallel")),
    )(proj3, proj3, w_dw, b_dw)


def _conv_bwd(duc3, proj3, w_dw):
    n_seq, seq, _ = proj3.shape
    n_cb = D_CONV // LANES

    def body(duc_ref, a_ref, g_ref, w_ref, da_ref, dg_ref, dw_ref, db_ref, xp, sh):
        @pl.when(pl.program_id(1) == 0)
        def _():
            dw_ref[...] = jnp.zeros_like(dw_ref)
            db_ref[...] = jnp.zeros_like(db_ref)

        zeros = jnp.zeros((CONV_HALO, LANES), F32)
        xp[pl.ds(0, CONV_HALO), :] = zeros
        xp[pl.ds(CONV_HALO + seq, CONV_HALO), :] = zeros
        xp[pl.ds(CONV_HALO, seq), :] = a_ref[...] * _sig(g_ref[...])
        _fill_shifted(xp, sh, seq)
        for j0 in range(0, CONV_WIDTH, 8):
            taps = range(j0, min(j0 + 8, CONV_WIDTH))

            def wblk(i, accs, taps=taps):
                for u in range(CONV_DW_UNROLL):
                    t0 = pl.multiple_of((i * CONV_DW_UNROLL + u) * CONV_DW_ROWS, CONV_DW_ROWS)
                    d = duc_ref[pl.ds(t0, CONV_DW_ROWS), :]
                    accs = tuple(acc + d * sh[(j + 1) % 8, pl.ds(t0 + 8 * ((j + 1) // 8), CONV_DW_ROWS), :]
                                 for acc, j in zip(accs, taps))
                return accs

            accs = lax.fori_loop(0, seq // (CONV_DW_ROWS * CONV_DW_UNROLL), wblk,
                                 tuple(jnp.zeros((CONV_DW_ROWS, LANES), F32) for _ in taps))
            for acc, j in zip(accs, taps):
                dw_ref[j:j + 1, :] += jnp.sum(acc, axis=0, keepdims=True)
        db_ref[0:1, :] += jnp.sum(duc_ref[...], axis=0, keepdims=True)
        xp[pl.ds(CONV_HALO, seq), :] = duc_ref[...]
        _fill_shifted(xp, sh, seq)

        def ublk(i, carry):
            t0 = pl.multiple_of(i * CONV_ROWS, CONV_ROWS)
            acc = jnp.zeros((CONV_ROWS, LANES), F32)
            for j in range(CONV_WIDTH):
                jj = CONV_WIDTH - j
                acc = acc + sh[jj % 8, pl.ds(t0 + 8 * (jj // 8), CONV_ROWS), :] * w_ref[j:j + 1, :]
            av = a_ref[pl.ds(t0, CONV_ROWS), :]
            sg = _sig(g_ref[pl.ds(t0, CONV_ROWS), :])
            da_ref[pl.ds(t0, CONV_ROWS), :] = (acc * sg).astype(BF16)
            dg_ref[pl.ds(t0, CONV_ROWS), :] = (acc * av * sg * (1.0 - sg)).astype(BF16)
            return carry

        lax.fori_loop(0, seq // CONV_ROWS, ublk, 0)

    return _call(
        body, name="conv_bwd", grid=(n_cb, n_seq),
        in_specs=[pl.BlockSpec((None, seq, LANES), lambda cb, b: (b, 0, cb)),
                  pl.BlockSpec((None, seq, LANES), lambda cb, b: (b, 0, cb)),
                  pl.BlockSpec((None, seq, LANES), lambda cb, b: (b, 0, n_cb + cb)),
                  pl.BlockSpec((CONV_WIDTH, LANES), lambda cb, b: (0, cb))],
        out_specs=[pl.BlockSpec((None, seq, LANES), lambda cb, b: (b, 0, cb)),
                   pl.BlockSpec((None, seq, LANES), lambda cb, b: (b, 0, cb)),
                   pl.BlockSpec((32, LANES), lambda cb, b: (0, cb)),
                   pl.BlockSpec((8, LANES), lambda cb, b: (0, cb))],
        out_shape=[_sds((n_seq, seq, D_CONV), BF16), _sds((n_seq, seq, D_CONV), BF16),
                   _sds((32, D_CONV), F32), _sds((8, D_CONV), F32)],
        scratch_shapes=[pltpu.VMEM((seq + 2 * CONV_HALO, LANES), F32),
                        pltpu.VMEM((8, seq + 2 * CONV_HALO, LANES), F32)],
        compiler_params=_params(("parallel", "arbitrary")),
    )(duc3, proj3, proj3, w_dw)


MASKED = 1e30
ATT_ROWS = 512
ATT_UNROLL = 8
ATT_FWD_UNROLL = 8


def _distance_mats(dil, seg_len):
    kw = min(2 * Q_BLOCK, seg_len)
    offsets = (0, -RADIUS, -2 * RADIUS) if kw == 2 * Q_BLOCK else (0,)
    a = np.arange(Q_BLOCK)[:, None]
    b = np.arange(kw)[None, :]
    mats = []
    for off in offsets:
        rel = np.abs(b + off - a)
        mats.append(np.where(rel <= RADIUS, dil * rel, MASKED))
    return jnp.asarray(np.stack(mats).astype(np.float32))


def _alibi_rows():
    s = np.zeros((4, 8, LANES), np.float32)
    for hp in range(4):
        for hl in range(2):
            s[hp, hl, :] = 2.0 ** (-(2 * hp + hl + 1))
    return jnp.asarray(s)


def _window(n, seg_len):
    i0 = pl.multiple_of(n * Q_BLOCK, Q_BLOCK)
    if seg_len <= Q_BLOCK:
        return i0, i0, 0
    per_seg = seg_len // Q_BLOCK
    j = n % per_seg
    seg0 = (n // per_seg) * seg_len
    ks_local = jnp.clip(j * Q_BLOCK - RADIUS, 0, seg_len - 2 * Q_BLOCK)
    ks = pl.multiple_of(seg0 + ks_local, RADIUS)
    var = jnp.where(j == 0, 0, jnp.where(j == per_seg - 1, 2, 1))
    return i0, ks, var


def _first_head(rows):
    return lax.broadcasted_iota(jnp.int32, (rows, LANES), 1) < HEAD_DIM


def _same_head():
    head = np.arange(LANES) // HEAD_DIM
    return jnp.asarray((head[:, None] == head[None, :]).astype(np.float32)).astype(BF16)


def _head_sum(x, same_ref):
    hi = x.astype(BF16)
    lo = (x - hi.astype(F32)).astype(BF16)
    return (jnp.dot(hi, same_ref[...], preferred_element_type=F32)
            + jnp.dot(lo, same_ref[...], preferred_element_type=F32))


def _head_mean(x, same_ref):
    return _head_sum(x, same_ref) * (1.0 / HEAD_DIM)


def _per_head(x, first):
    swapped = pltpu.roll(x, HEAD_DIM, 1)
    return jnp.where(first, x, swapped), jnp.where(first, swapped, x)


STRIDE = 4


def _gather_segments(src, dil, seq, tmp, put):
    if dil == 1:
        put(0, seq, src[pl.ds(0, seq), :])
    elif dil == STRIDE:
        seg = seq // dil
        for r in range(dil):
            put(r * seg, seg, src[pl.ds(r, seg, stride=dil), :])
    else:
        part, seg = seq // STRIDE, seq // dil
        for b in range(STRIDE):
            tmp[pl.ds(b * part, part), :] = src[pl.ds(b, part, stride=STRIDE), :]
        for b in range(STRIDE):
            for a in range(dil // STRIDE):
                put(b * part + a * seg, seg, tmp[pl.ds(b * part + a, seg, stride=dil // STRIDE), :])


def _scatter_segments(dst, get, dil, seq, tmp, accumulate):
    def write(rows, val):
        if accumulate:
            dst[rows, :] += val
        else:
            dst[rows, :] = val

    if dil == 1:
        write(pl.ds(0, seq), get(0, seq))
    elif dil == STRIDE:
        seg = seq // dil
        for r in range(dil):
            write(pl.ds(r, seg, stride=dil), get(r * seg, seg))
    else:
        part, seg = seq // STRIDE, seq // dil
        for b in range(STRIDE):
            for a in range(dil // STRIDE):
                tmp[pl.ds(b * part + a, seg, stride=dil // STRIDE), :] = get(b * part + a * seg, seg)
        for b in range(STRIDE):
            write(pl.ds(b, part, stride=STRIDE), tmp[pl.ds(b * part, part), :])


def _permute_rows(dst, src, dil, seq, tmp):
    def put(start, size, val):
        dst[pl.ds(start, size), :] = val.astype(dst.dtype)

    _gather_segments(src, dil, seq, tmp, put)


def _permute_rows_by_head(dst, src, dil, seq, tmp):
    def put(start, size, val):
        first = _first_head(size)
        dst[0, pl.ds(start, size), :] = jnp.where(first, val, 0.0).astype(dst.dtype)
        dst[1, pl.ds(start, size), :] = jnp.where(first, 0.0, val).astype(dst.dtype)

    _gather_segments(src, dil, seq, tmp, put)


def _qk_normalise(q_ref, g2_ref, same_ref, dst, seq, scale):
    def chunk(ci, carry):
        rows = pl.ds(pl.multiple_of(ci * ATT_ROWS, ATT_ROWS), ATT_ROWS)
        qv = q_ref[rows, :]
        r = lax.rsqrt(_head_mean(qv * qv, same_ref) + EPS)
        dst[rows, :] = qv * r * (g2_ref[...] * scale)
        return carry

    lax.fori_loop(0, seq // ATT_ROWS, chunk, 0)


def _attn_fwd(proj3, g_q2, g_k2, ride):
    n_seq, seq, _ = proj3.shape
    dms = [_distance_mats(d, seq // d) for d in DILATIONS]
    same = _same_head()
    col0 = 2 * D_CONV // LANES
    n_hp = D_ATT // LANES

    n_ride = len(ride)
    ride_scatter = [s for _, s in ride]

    def body(*refs):
        q_ref, k_ref, v_ref, gq_ref, gk_ref, sl_ref, dm1, dm4, dm16, same_ref = refs[:10]
        ride_in = refs[10:10 + n_ride]
        y_ref, lse_ref = refs[10 + n_ride:12 + n_ride]
        ride_out = refs[12 + n_ride:12 + 2 * n_ride]
        (qf, kf, qp, kp, vp, oml_p, o1, o4, o16, m1, m4, m16, l1, l4, l16,
         tmp) = refs[12 + 2 * n_ride:28 + 2 * n_ride]
        o_nat, m_nat, l_nat = (o1, o4, o16), (m1, m4, m16), (l1, l4, l16)
        ride_args = (ride_scatter, ride_in, ride_out) + tuple(refs[28 + 2 * n_ride:])
        step = pl.program_id(0) * n_hp + pl.program_id(1)

        @pl.when(step == 0)
        def _():
            _exchange_start(*ride_args)

        dm_refs = (dm1, dm4, dm16)
        _qk_normalise(q_ref, gq_ref, same_ref, qf, seq, HEAD_DIM ** -0.5)
        _qk_normalise(k_ref, gk_ref, same_ref, kf, seq, 1.0)
        slopes = (sl_ref[0:1, 0:1], sl_ref[1:2, 0:1])
        for pi, dil in enumerate(DILATIONS):
            seg = seq // dil
            kw = min(2 * Q_BLOCK, seg)
            _permute_rows_by_head(qp, qf, dil, seq, tmp)
            _permute_rows(kp, kf, dil, seq, tmp)
            _permute_rows(vp, v_ref, dil, seq, tmp)

            def blk(it, carry, seg=seg, kw=kw, pi=pi, dst=oml_p):
                first = _first_head(Q_BLOCK)
                chains = [(sub, h) for sub in range(ATT_FWD_UNROLL) for h in range(2)]
                win = [_window(it * ATT_FWD_UNROLL + sub, seg) for sub in range(ATT_FWD_UNROLL)]
                s = {}
                for sub, h in chains:
                    i0, ks, var = win[sub]
                    s[sub, h] = lax.dot_general(qp[h, pl.ds(i0, Q_BLOCK), :], kp[pl.ds(ks, kw), :], NT,
                                                preferred_element_type=F32) - slopes[h] * dm_refs[pi][var]
                m, l, p = {}, {}, {}
                for c in chains:
                    m[c] = jnp.max(s[c], axis=1, keepdims=True)
                    e = jnp.exp(s[c] - m[c])
                    l[c] = jnp.sum(e, axis=1, keepdims=True)
                    p[c] = e.astype(BF16)
                o = {}
                for sub, h in chains:
                    o[sub, h] = jnp.dot(p[sub, h], vp[pl.ds(win[sub][1], kw), :], preferred_element_type=F32)
                packed = [jnp.concatenate([jnp.where(first, t[sub, 0], t[sub, 1]) for t in (o, m, l)], axis=1)
                          for sub in range(ATT_FWD_UNROLL)]
                span = ATT_FWD_UNROLL * Q_BLOCK
                dst[pl.ds(pl.multiple_of(it * span, span), span), :] = jnp.concatenate(packed, axis=0)
                return carry

            lax.fori_loop(0, seq // (Q_BLOCK * ATT_FWD_UNROLL), blk, 0)
            for n, nat in enumerate((o_nat[pi], m_nat[pi], l_nat[pi])):
                _scatter_segments(nat, lambda start, size, n=n: oml_p[pl.ds(start, size), pl.ds(n * LANES, LANES)],
                                  dil, seq, tmp, accumulate=False)

        def merge(ci, carry):
            rows = pl.ds(pl.multiple_of(ci * ATT_ROWS, ATT_ROWS), ATT_ROWS)
            ms = [m_nat[pi][rows, :] for pi in range(3)]
            m_all = jnp.maximum(jnp.maximum(ms[0], ms[1]), ms[2])
            es = [jnp.exp(m - m_all) for m in ms]
            l_all = sum(l_nat[pi][rows, :] * es[pi] for pi in range(3))
            inv = 1.0 / l_all
            o = sum(o_nat[pi][rows, :] * (es[pi] * inv) for pi in range(3))
            y_ref[rows, :] = o.astype(BF16)
            lse_ref[rows, :] = m_all + jnp.log(l_all)
            return carry

        lax.fori_loop(0, seq // ATT_ROWS, merge, 0)

        @pl.when(step == n_seq * n_hp - 1)
        def _():
            _exchange_wait(*ride_args)

    def col(off):
        return pl.BlockSpec((None, seq, LANES), lambda b, hp: (b, 0, col0 + off * n_hp + hp))

    def whole(arr):
        return pl.BlockSpec(arr.shape, lambda b, hp: (0,) * arr.ndim)

    rows_f32 = pltpu.VMEM((seq, LANES), F32)
    rows_bf16 = pltpu.VMEM((seq, LANES), BF16)
    return _call(
        body, name="attn_fwd", grid=(n_seq, n_hp),
        in_specs=[col(0), col(1), col(2), whole(g_q2), whole(g_k2),
                  pl.BlockSpec((None, 8, LANES), lambda b, hp: (hp, 0, 0)),
                  whole(dms[0]), whole(dms[1]), whole(dms[2]), whole(same)] + [ANY_SPEC] * n_ride,
        out_specs=[pl.BlockSpec((None, seq, LANES), lambda b, hp: (b, 0, hp)),
                   pl.BlockSpec((None, seq, LANES), lambda b, hp: (b, 0, hp))] + [ANY_SPEC] * n_ride,
        out_shape=[_sds((n_seq, seq, D_ATT), BF16), _sds((n_seq, seq, D_ATT), F32)] + _exchange_shapes(ride),
        scratch_shapes=[rows_f32, rows_f32, pltpu.VMEM((2, seq, LANES), BF16), rows_bf16, rows_bf16]
        + [pltpu.VMEM((seq, 3 * LANES), F32)] + [rows_f32] * 10 + _exchange_sems(n_ride),
        compiler_params=_params(("arbitrary", "arbitrary")),
    )(proj3, proj3, proj3, g_q2, g_k2, _alibi_rows(), *dms, same, *[a for a, _ in ride])


def _attn_bwd(proj3, do3, y_att3, lse3, g_q2, g_k2, ride):
    n_seq, seq, _ = proj3.shape
    dms = [_distance_mats(d, seq // d) for d in DILATIONS]
    same = _same_head()
    col0 = 2 * D_CONV // LANES
    n_hp = D_ATT // LANES

    n_ride = len(ride)
    ride_scatter = [s for _, s in ride]

    def body(*refs):
        (q_ref, k_ref, v_ref, do_ref, o_ref, lse_ref, gq_ref, gk_ref, sl_ref, dm1, dm4, dm16,
         same_ref) = refs[:13]
        ride_in = refs[13:13 + n_ride]
        dq_ref, dk_ref, dv_ref, dg_ref = refs[13 + n_ride:17 + n_ride]
        ride_out = refs[17 + n_ride:17 + 2 * n_ride]
        (qf, kf, qp, dop, kp, vp, sn, sp, dqp, dkp, dvp, dqn, dkn, dvn,
         tmp) = refs[17 + 2 * n_ride:32 + 2 * n_ride]
        ride_args = (ride_scatter, ride_in, ride_out) + tuple(refs[32 + 2 * n_ride:])
        dm_refs = (dm1, dm4, dm16)
        step = pl.program_id(0) * n_hp + pl.program_id(1)

        @pl.when(step == 0)
        def _():
            _exchange_start(*ride_args)
            dg_ref[...] = jnp.zeros_like(dg_ref)

        _qk_normalise(q_ref, gq_ref, same_ref, qf, seq, HEAD_DIM ** -0.5)
        _qk_normalise(k_ref, gk_ref, same_ref, kf, seq, 1.0)

        def stats(ci, carry):
            rows = pl.ds(pl.multiple_of(ci * ATT_ROWS, ATT_ROWS), ATT_ROWS)
            first = _first_head(ATT_ROWS)
            sn[0, rows, :], sn[1, rows, :] = _per_head(lse_ref[rows, :], first)
            prod = do_ref[rows, :] * o_ref[rows, :].astype(F32)
            sn[2, rows, :], sn[3, rows, :] = _per_head(_head_sum(prod, same_ref), first)
            return carry

        lax.fori_loop(0, seq // ATT_ROWS, stats, 0)
        slopes = (sl_ref[0:1, 0:1], sl_ref[1:2, 0:1])
        half = seq // (Q_BLOCK * ATT_UNROLL)
        region = seq // ATT_UNROLL

        for pi, dil in enumerate(DILATIONS):
            seg = seq // dil
            kw = min(2 * Q_BLOCK, seg)
            _permute_rows_by_head(qp, qf, dil, seq, tmp)
            _permute_rows_by_head(dop, do_ref, dil, seq, tmp)
            _permute_rows(kp, kf, dil, seq, tmp)
            _permute_rows(vp, v_ref, dil, seq, tmp)
            if dil == 1:
                st = sn
            else:
                st = sp
                for n in range(4):
                    _permute_rows(sp.at[n], sn.at[n], dil, seq, tmp)
            def touched(sub, seg=seg):
                lo, hi = sub * region, (sub + 1) * region
                if seg < region:
                    return lo, hi
                seg0 = lo // seg * seg
                return max(lo - RADIUS, seg0), min(hi + RADIUS, seg0 + seg)

            def summed(acc, start, size, touched=touched):
                pieces = []
                for c0 in range(start, start + size, RADIUS):
                    owners = [s for s in range(ATT_UNROLL) if touched(s)[0] <= c0 and c0 + RADIUS <= touched(s)[1]]
                    if pieces and pieces[-1][2] == owners:
                        pieces[-1][1] += RADIUS
                    else:
                        pieces.append([c0, RADIUS, owners])
                vals = [sum(acc[o, pl.ds(c0, n), :] for o in owners) for c0, n, owners in pieces]
                return vals[0] if len(vals) == 1 else jnp.concatenate(vals, axis=0)

            for sub in range(ATT_UNROLL):
                lo, hi = touched(sub)
                dkp[sub, pl.ds(lo, hi - lo), :] = jnp.zeros((hi - lo, LANES), F32)
                dvp[sub, pl.ds(lo, hi - lo), :] = jnp.zeros((hi - lo, LANES), F32)

            def blk(it, carry, seg=seg, kw=kw, pi=pi, st=st):
                first = _first_head(Q_BLOCK)
                chains = [(sub, h) for sub in range(ATT_UNROLL) for h in range(2)]
                win = [_window(it + sub * half, seg) for sub in range(ATT_UNROLL)]
                qrows = [pl.ds(w[0], Q_BLOCK) for w in win]
                krows = [pl.ds(w[1], kw) for w in win]

                def over_keys(n, sub):
                    t = st[n, qrows[sub], :]
                    return t if kw == LANES else jnp.concatenate([t] * (kw // LANES), axis=1)

                s, dp = {}, {}
                for sub, h in chains:
                    s[sub, h] = lax.dot_general(qp[h, qrows[sub], :], kp[krows[sub], :], NT,
                                                preferred_element_type=F32) - slopes[h] * dm_refs[pi][win[sub][2]]
                    dp[sub, h] = lax.dot_general(dop[h, qrows[sub], :], vp[krows[sub], :], NT,
                                                 preferred_element_type=F32)
                p, ds = {}, {}
                for sub, h in chains:
                    e = jnp.exp(s[sub, h] - over_keys(h, sub))
                    ds[sub, h] = (e * (dp[sub, h] - over_keys(2 + h, sub))).astype(BF16)
                    p[sub, h] = e.astype(BF16)
                dq, dk, dv = {}, {}, {}
                for sub, h in chains:
                    dq[sub, h] = jnp.dot(ds[sub, h], kp[krows[sub], :], preferred_element_type=F32)
                    dk[sub, h] = lax.dot_general(ds[sub, h], qp[h, qrows[sub], :], TN, preferred_element_type=F32)
                    dv[sub, h] = lax.dot_general(p[sub, h], dop[h, qrows[sub], :], TN, preferred_element_type=F32)
                for sub in range(ATT_UNROLL):
                    dqp[qrows[sub], :] = jnp.where(first, dq[sub, 0], dq[sub, 1])
                    dkp[sub, krows[sub], :] += dk[sub, 0] + dk[sub, 1]
                    dvp[sub, krows[sub], :] += dv[sub, 0] + dv[sub, 1]
                return carry

            lax.fori_loop(0, half, blk, 0)
            first_pattern = pi == 0
            if first_pattern:
                for r0 in range(0, seq, region):
                    rows = pl.ds(r0, region)
                    dqn[rows, :] = dqp[rows, :]
                    dkn[rows, :] = summed(dkp, r0, region)
                    dvn[rows, :] = summed(dvp, r0, region)
            else:
                _scatter_segments(dqn, lambda start, size: dqp[pl.ds(start, size), :], dil, seq, tmp, accumulate=True)
                for nat, acc in ((dkn, dkp), (dvn, dvp)):
                    _scatter_segments(nat, lambda start, size, acc=acc: summed(acc, start, size),
                                      dil, seq, tmp, accumulate=True)

        def finish(ci, carry):
            rows = pl.ds(pl.multiple_of(ci * ATT_ROWS, ATT_ROWS), ATT_ROWS)
            for src_ref, g_ref, dn, dst_ref, scale, row in (
                    (q_ref, gq_ref, dqn, dq_ref, HEAD_DIM ** -0.5, 0), (k_ref, gk_ref, dkn, dk_ref, 1.0, 1)):
                xv = src_ref[rows, :]
                r = lax.rsqrt(_head_mean(xv * xv, same_ref) + EPS)
                xhat = xv * r
                d = dn[rows, :] * scale
                dg_ref[row:row + 1, :] += jnp.sum(d * xhat, axis=0, keepdims=True)
                dxh = d * g_ref[...]
                dst_ref[rows, :] = (r * (dxh - xhat * _head_mean(dxh * xhat, same_ref))).astype(BF16)
            dv_ref[rows, :] = dvn[rows, :].astype(BF16)
            return carry

        lax.fori_loop(0, seq // ATT_ROWS, finish, 0)

        @pl.when(step == n_seq * n_hp - 1)
        def _():
            _exchange_wait(*ride_args)

    def col(off):
        return pl.BlockSpec((None, seq, LANES), lambda b, hp: (b, 0, col0 + off * n_hp + hp))

    def whole(arr):
        return pl.BlockSpec(arr.shape, lambda b, hp: (0,) * arr.ndim)

    att = pl.BlockSpec((None, seq, LANES), lambda b, hp: (b, 0, hp))
    rows_f32 = pltpu.VMEM((seq, LANES), F32)
    rows_bf16 = pltpu.VMEM((seq, LANES), BF16)
    by_head_bf16 = pltpu.VMEM((2, seq, LANES), BF16)
    per_sub_f32 = pltpu.VMEM((ATT_UNROLL, seq, LANES), F32)
    stats_f32 = pltpu.VMEM((4, seq, LANES), F32)
    return _call(
        body, name="attn_bwd", grid=(n_seq, n_hp),
        in_specs=[col(0), col(1), col(2), att, att, att, whole(g_q2), whole(g_k2),
                  pl.BlockSpec((None, 8, LANES), lambda b, hp: (hp, 0, 0)),
                  whole(dms[0]), whole(dms[1]), whole(dms[2]), whole(same)] + [ANY_SPEC] * n_ride,
        out_specs=[att, att, att, pl.BlockSpec((8, LANES), lambda b, hp: (0, 0))] + [ANY_SPEC] * n_ride,
        out_shape=[_sds((n_seq, seq, D_ATT), BF16)] * 3 + [_sds((8, LANES), F32)] + _exchange_shapes(ride),
        scratch_shapes=[rows_f32, rows_f32, by_head_bf16, by_head_bf16, rows_bf16, rows_bf16, stats_f32, stats_f32,
                        rows_f32, per_sub_f32, per_sub_f32, rows_f32, rows_f32, rows_f32, rows_f32]
        + _exchange_sems(n_ride),
        compiler_params=_params(("arbitrary", "arbitrary")),
    )(proj3, proj3, proj3, do3, y_att3, lse3, g_q2, g_k2, _alibi_rows(), *dms, same, *[a for a, _ in ride])


def _mix_out(uc2, y_att2, x2, mod8, g_ln, b_ln, g_ffn, w_out, seq, tm=512):
    tokens = x2.shape[0]
    per_seq = seq // tm

    def body(uc_ref, ya_ref, x_ref, m_ref, gl_ref, bl_ref, gf_ref, w_ref, yc_ref, mix_ref, x1_ref, h2_ref):
        uc = uc_ref[...]
        mu = jnp.mean(uc, axis=-1, keepdims=True)
        cen = uc - mu
        rs = lax.rsqrt(jnp.mean(cen * cen, axis=-1, keepdims=True) + EPS)
        z = cen * rs * gl_ref[...] + bl_ref[...]
        yc = (z * _sig(z)).astype(BF16)
        yc_ref[...] = yc
        mix = (jnp.dot(yc, w_ref[pl.ds(0, D_CONV), :], preferred_element_type=F32)
               + jnp.dot(ya_ref[...], w_ref[pl.ds(D_CONV, D_ATT), :], preferred_element_type=F32))
        mix_ref[...] = mix
        x1 = x_ref[...] + m_ref[2:3, :] * mix
        x1_ref[...] = x1
        r = lax.rsqrt(jnp.mean(x1 * x1, axis=-1, keepdims=True) + EPS)
        h2_ref[...] = ((x1 * r * gf_ref[...]) * (1.0 + m_ref[4:5, :]) + m_ref[3:4, :]).astype(BF16)

    def rows(width):
        return pl.BlockSpec((tm, width), lambda i: (i, 0))

    def vec(width):
        return pl.BlockSpec((1, width), lambda i: (0, 0))

    return _call(
        body, name="mix_out", grid=(tokens // tm,),
        in_specs=[rows(D_CONV), rows(D_ATT), rows(D_MODEL),
                  pl.BlockSpec((None, 8, D_MODEL), lambda i: (i // per_seq, 0, 0)),
                  vec(D_CONV), vec(D_CONV), vec(D_MODEL),
                  pl.BlockSpec((D_MODEL, D_MODEL), lambda i: (0, 0))],
        out_specs=[rows(D_CONV), rows(D_MODEL), rows(D_MODEL), rows(D_MODEL)],
        out_shape=[_sds((tokens, D_CONV), BF16), _sds((tokens, D_MODEL), F32),
                   _sds((tokens, D_MODEL), F32), _sds((tokens, D_MODEL), BF16)],
        compiler_params=_params(("parallel",)),
    )(uc2, y_att2, x2, mod8, g_ln, b_ln, g_ffn, w_out)


def _mix_out_bwd(dmix, uc2, g_ln, b_ln, w_out, tm=512):
    tokens = dmix.shape[0]

    def body(dm_ref, uc_ref, gl_ref, bl_ref, w_ref, duc_ref, do_ref, dgb_ref):
        @pl.when(pl.program_id(0) == 0)
        def _():
            dgb_ref[...] = jnp.zeros_like(dgb_ref)

        dmv = dm_ref[...]
        dyc = lax.dot_general(dmv, w_ref[pl.ds(0, D_CONV), :], NT, preferred_element_type=F32)
        do_ref[...] = lax.dot_general(dmv, w_ref[pl.ds(D_CONV, D_ATT), :], NT, preferred_element_type=F32)
        uc = uc_ref[...]
        mu = jnp.mean(uc, axis=-1, keepdims=True)
        cen = uc - mu
        rs = lax.rsqrt(jnp.mean(cen * cen, axis=-1, keepdims=True) + EPS)
        xh = cen * rs
        z = xh * gl_ref[...] + bl_ref[...]
        sg = _sig(z)
        dz = dyc * (sg * (1.0 + z * (1.0 - sg)))
        dgb_ref[0:1, :] += jnp.sum(dz * xh, axis=0, keepdims=True)
        dgb_ref[1:2, :] += jnp.sum(dz, axis=0, keepdims=True)
        dxh = dz * gl_ref[...]
        duc_ref[...] = rs * (dxh - jnp.mean(dxh, axis=-1, keepdims=True)
                             - xh * jnp.mean(dxh * xh, axis=-1, keepdims=True))

    return _call(
        body, name="mix_out_bwd", grid=(tokens // tm,),
        in_specs=[pl.BlockSpec((tm, D_MODEL), lambda i: (i, 0)),
                  pl.BlockSpec((tm, D_CONV), lambda i: (i, 0)),
                  pl.BlockSpec((1, D_CONV), lambda i: (0, 0)),
                  pl.BlockSpec((1, D_CONV), lambda i: (0, 0)),
                  pl.BlockSpec((D_MODEL, D_MODEL), lambda i: (0, 0))],
        out_specs=[pl.BlockSpec((tm, D_CONV), lambda i: (i, 0)),
                   pl.BlockSpec((tm, D_ATT), lambda i: (i, 0)),
                   pl.BlockSpec((8, D_CONV), lambda i: (0, 0))],
        out_shape=[_sds((tokens, D_CONV), F32), _sds((tokens, D_ATT), F32), _sds((8, D_CONV), F32)],
        compiler_params=_params(("arbitrary",)),
    )(dmix, uc2, g_ln, b_ln, w_out)


FF_TILE = 256
FF_ROWS = 256


def _ffn_fwd(h2, w_gate, w_up, w_down, x1, target, mod8, seq, tm=1024):
    tokens = h2.shape[0]
    per_seq = seq // tm
    n_seq = tokens // seq
    last = D_FF // FF_TILE - 1

    def body(h_ref, wg_ref, wu_ref, wd_ref, x1_ref, t_ref, m_ref, gate_ref, up_ref, dy_ref, df_ref, sq_ref, dgf_ref,
             f_ref):
        i, j = pl.program_id(0), pl.program_id(1)

        @pl.when(j == 0)
        def _():
            f_ref[...] = jnp.zeros_like(f_ref)

        @pl.when((j == 0) & (i == 0))
        def _():
            sq_ref[...] = jnp.zeros_like(sq_ref)

        @pl.when((j == 0) & (i % per_seq == 0))
        def _():
            dgf_ref[...] = jnp.zeros_like(dgf_ref)

        def gate_up(r):
            hv = h_ref[pl.ds(r * FF_ROWS, FF_ROWS), :]
            return (jnp.dot(hv, wg_ref[...], preferred_element_type=F32),
                    jnp.dot(hv, wu_ref[...], preferred_element_type=F32))

        ahead = gate_up(0)
        for r in range(tm // FF_ROWS):
            gate, up = ahead
            if r + 1 < tm // FF_ROWS:
                ahead = gate_up(r + 1)
            rows = pl.ds(r * FF_ROWS, FF_ROWS)
            gate_ref[rows, :] = gate.astype(BF16)
            up_ref[rows, :] = up.astype(BF16)
            act = (gate * _sig(gate) * up).astype(BF16)
            f_ref[rows, :] += jnp.dot(act, wd_ref[...], preferred_element_type=F32)

        @pl.when(j == last)
        def _():
            gate_f = m_ref[5:6, :]
            for r in range(tm // FF_ROWS):
                rows = pl.ds(r * FF_ROWS, FF_ROWS)
                fv = f_ref[rows, :]
                diff = x1_ref[rows, :] + gate_f * fv - t_ref[rows, :]
                sq_ref[0:1, :] += jnp.sum(diff * diff, axis=0, keepdims=True)
                dy = diff * (1.0 / D_MODEL)
                dy_ref[rows, :] = dy
                df_ref[rows, :] = (gate_f * dy).astype(BF16)
                dgf_ref[0:1, :] += jnp.sum(dy * fv, axis=0, keepdims=True)

    rows_spec = pl.BlockSpec((tm, D_MODEL), lambda i, j: (i, 0))
    per = pl.BlockSpec((None, 8, D_MODEL), lambda i, j: (i // per_seq, 0, 0))
    tile = pl.BlockSpec((tm, FF_TILE), lambda i, j: (i, j))
    return _call(
        body, name="ffn_fwd", grid=(tokens // tm, D_FF // FF_TILE),
        in_specs=[rows_spec,
                  pl.BlockSpec((D_MODEL, FF_TILE), lambda i, j: (0, j)),
                  pl.BlockSpec((D_MODEL, FF_TILE), lambda i, j: (0, j)),
                  pl.BlockSpec((FF_TILE, D_MODEL), lambda i, j: (j, 0)),
                  rows_spec, rows_spec, per],
        out_specs=[tile, tile, rows_spec, rows_spec, pl.BlockSpec((8, D_MODEL), lambda i, j: (0, 0)), per],
        out_shape=[_sds((tokens, D_FF), BF16), _sds((tokens, D_FF), BF16), _sds((tokens, D_MODEL), F32),
                   _sds((tokens, D_MODEL), BF16), _sds((8, D_MODEL), F32), _sds((n_seq, 8, D_MODEL), F32)],
        scratch_shapes=[pltpu.VMEM((tm, D_MODEL), F32)],
        compiler_params=_params(("arbitrary", "arbitrary")),
    )(h2, w_gate, w_up, w_down, x1, target, mod8)


def _ffn_bwd(df, gate, up, w_gate, w_up, w_down, x1, dy, mix, mod8, g_ffn, seq, tm=1024):
    tokens = df.shape[0]
    per_seq = seq // tm
    n_seq = tokens // seq
    last = D_FF // FF_TILE - 1

    def body(df_ref, gate_ref, up_ref, wg_ref, wu_ref, wd_ref, m_ref, g_ref, x1_hbm, dy_hbm, mix_hbm,
             dgate_ref, dup_ref, act_ref, dx1_ref, dmix_ref, dg_ref, dm_ref, dh_ref, late, late_sems):
        i, j = pl.program_id(0), pl.program_id(1)
        my_rows = pl.ds(pl.multiple_of(i * tm, tm), tm)
        fetches = [pltpu.make_async_copy(src.at[my_rows, :], late.at[n], late_sems.at[n])
                   for n, src in enumerate((x1_hbm, dy_hbm, mix_hbm))]

        @pl.when(j == 0)
        def _():
            dh_ref[...] = jnp.zeros_like(dh_ref)
            for cp in fetches:
                cp.start()

        @pl.when((j == 0) & (i == 0))
        def _():
            dg_ref[...] = jnp.zeros_like(dg_ref)

        @pl.when((j == 0) & (i % per_seq == 0))
        def _():
            dm_ref[...] = jnp.zeros_like(dm_ref)

        def d_act(r):
            return lax.dot_general(df_ref[pl.ds(r * FF_ROWS, FF_ROWS), :], wd_ref[...], NT,
                                   preferred_element_type=F32)

        ahead = d_act(0)
        for r in range(tm // FF_ROWS):
            dact = ahead
            if r + 1 < tm // FF_ROWS:
                ahead = d_act(r + 1)
            rows = pl.ds(r * FF_ROWS, FF_ROWS)
            gate = gate_ref[rows, :].astype(F32)
            up = up_ref[rows, :].astype(F32)
            sg = _sig(gate)
            silu = gate * sg
            act_ref[rows, :] = (silu * up).astype(BF16)
            dup = (dact * silu).astype(BF16)
            dgate = (dact * up * (sg * (1.0 + gate * (1.0 - sg)))).astype(BF16)
            dup_ref[rows, :] = dup
            dgate_ref[rows, :] = dgate
            dh_ref[rows, :] += (lax.dot_general(dgate, wg_ref[...], NT, preferred_element_type=F32)
                                + lax.dot_general(dup, wu_ref[...], NT, preferred_element_type=F32))

        @pl.when(j == last)
        def _():
            for cp in fetches:
                cp.wait()
            g = g_ref[...]
            for r in range(tm // FF_ROWS):
                rows = pl.ds(r * FF_ROWS, FF_ROWS)
                dh = dh_ref[rows, :]
                x1v = late[0, rows, :]
                rs = lax.rsqrt(jnp.mean(x1v * x1v, axis=-1, keepdims=True) + EPS)
                xhat = x1v * rs
                dm_ref[0:1, :] += jnp.sum(dh, axis=0, keepdims=True)
                dm_ref[1:2, :] += jnp.sum(dh * (xhat * g), axis=0, keepdims=True)
                dn = dh * (1.0 + m_ref[4:5, :])
                dg_ref[0:1, :] += jnp.sum(dn * xhat, axis=0, keepdims=True)
                dxh = dn * g
                dx1 = late[1, rows, :] + rs * (dxh - xhat * jnp.mean(dxh * xhat, axis=-1, keepdims=True))
                dx1_ref[rows, :] = dx1
                dm_ref[2:3, :] += jnp.sum(dx1 * late[2, rows, :], axis=0, keepdims=True)
                dmix_ref[rows, :] = (m_ref[2:3, :] * dx1).astype(BF16)

    tile = pl.BlockSpec((tm, FF_TILE), lambda i, j: (i, j))
    rows_spec = pl.BlockSpec((tm, D_MODEL), lambda i, j: (i, 0))
    per = pl.BlockSpec((None, 8, D_MODEL), lambda i, j: (i // per_seq, 0, 0))
    return _call(
        body, name="ffn_bwd", grid=(tokens // tm, D_FF // FF_TILE),
        in_specs=[rows_spec, tile, tile,
                  pl.BlockSpec((D_MODEL, FF_TILE), lambda i, j: (0, j)),
                  pl.BlockSpec((D_MODEL, FF_TILE), lambda i, j: (0, j)),
                  pl.BlockSpec((FF_TILE, D_MODEL), lambda i, j: (j, 0)),
                  per, pl.BlockSpec((1, D_MODEL), lambda i, j: (0, 0)), ANY_SPEC, ANY_SPEC, ANY_SPEC],
        out_specs=[tile, tile, tile, rows_spec, rows_spec, pl.BlockSpec((8, D_MODEL), lambda i, j: (0, 0)), per],
        out_shape=[_sds((tokens, D_FF), BF16)] * 3 + [_sds((tokens, D_MODEL), F32), _sds((tokens, D_MODEL), BF16),
                                                    _sds((8, D_MODEL), F32), _sds((n_seq, 8, D_MODEL), F32)],
        scratch_shapes=[pltpu.VMEM((tm, D_MODEL), F32), pltpu.VMEM((3, tm, D_MODEL), F32),
                        pltpu.SemaphoreType.DMA((3,))],
        compiler_params=_params(("arbitrary", "arbitrary")),
    )(df, gate, up, w_gate, w_up, w_down, mod8, g_ffn, x1, dy, mix)


def _mix_in_bwd(d_a, d_g, d_q, d_k, d_v, w_in, x2, dx1, mod8, g_mix, seq, ride, tm=512):
    tokens = x2.shape[0]
    per_seq = seq // tm
    n_seq = tokens // seq
    parts = (d_a, d_g, d_q, d_k, d_v)
    width = D_CONV
    n_ride = len(ride)
    ride_scatter = [s for _, s in ride]

    def body(*refs):
        da_ref, dg_ref, dq_ref, dk_ref, dv_ref, w_ref, x_ref, dx1_ref, m_ref, g_ref = refs[:10]
        ride_in = refs[10:10 + n_ride]
        gx_ref, dgm_ref, dm_ref = refs[10 + n_ride:13 + n_ride]
        ride_args = (ride_scatter, ride_in, refs[13 + n_ride:13 + 2 * n_ride]) + tuple(refs[13 + 2 * n_ride:])
        i = pl.program_id(0)

        @pl.when(i == 0)
        def _():
            _exchange_start(*ride_args)
            dgm_ref[...] = jnp.zeros_like(dgm_ref)

        @pl.when(i % per_seq == 0)
        def _():
            dm_ref[...] = jnp.zeros_like(dm_ref)

        dh = jnp.zeros((tm, D_MODEL), F32)
        for n, ref in enumerate((da_ref, dg_ref, dq_ref, dk_ref, dv_ref)):
            dh = dh + lax.dot_general(ref[...], w_ref[:, pl.ds(n * width, width)], NT, preferred_element_type=F32)
        xv = x_ref[...]
        r = lax.rsqrt(jnp.mean(xv * xv, axis=-1, keepdims=True) + EPS)
        xhat = xv * r
        g = g_ref[...]
        dm_ref[0:1, :] += jnp.sum(dh, axis=0, keepdims=True)
        dm_ref[1:2, :] += jnp.sum(dh * (xhat * g), axis=0, keepdims=True)
        dn = dh * (1.0 + m_ref[1:2, :])
        dgm_ref[0:1, :] += jnp.sum(dn * xhat, axis=0, keepdims=True)
        dxh = dn * g
        gx_ref[...] = dx1_ref[...] + r * (dxh - xhat * jnp.mean(dxh * xhat, axis=-1, keepdims=True))

        @pl.when(i == tokens // tm - 1)
        def _():
            _exchange_wait(*ride_args)

    rows = pl.BlockSpec((tm, D_MODEL), lambda i: (i, 0))
    half = pl.BlockSpec((tm, width), lambda i: (i, 0))
    per = pl.BlockSpec((None, 8, D_MODEL), lambda i: (i // per_seq, 0, 0))
    return _call(
        body, name="mix_in_bwd", grid=(tokens // tm,),
        in_specs=[half] * 5 + [pl.BlockSpec((D_MODEL, D_IN), lambda i: (0, 0)), rows, rows, per,
                               pl.BlockSpec((1, D_MODEL), lambda i: (0, 0))] + [ANY_SPEC] * n_ride,
        out_specs=[rows, pl.BlockSpec((8, D_MODEL), lambda i: (0, 0)), per] + [ANY_SPEC] * n_ride,
        out_shape=[_sds((tokens, D_MODEL), F32), _sds((8, D_MODEL), F32), _sds((n_seq, 8, D_MODEL), F32)]
        + _exchange_shapes(ride),
        scratch_shapes=_exchange_sems(n_ride),
        compiler_params=_params(("arbitrary",)),
    )(*parts, w_in, x2, dx1, mod8, g_mix, *[a for a, _ in ride])


def _grad_matmul_parts(a_parts, b_parts, name, tk=1024):
    tokens = a_parts[0].shape[0]
    na, nb = len(a_parts), len(b_parts)
    ma, nbw = a_parts[0].shape[1], b_parts[0].shape[1]

    n_k = tokens // tk

    def body(*refs):
        a_refs, b_refs, o_ref, acc = refs[:na], refs[na:na + nb], refs[na + nb], refs[na + nb + 1]

        @pl.when(pl.program_id(0) == 0)
        def _():
            acc[...] = jnp.zeros_like(acc)

        for i in range(na):
            for j in range(nb):
                acc[pl.ds(i * ma, ma), pl.ds(j * nbw, nbw)] += lax.dot_general(
                    a_refs[i][...], b_refs[j][...], TN, preferred_element_type=F32)

        @pl.when(pl.program_id(0) == n_k - 1)
        def _():
            o_ref[...] = acc[...].astype(o_ref.dtype)

    return _call(
        body, name=name, grid=(n_k,),
        in_specs=[pl.BlockSpec((tk, ma), lambda k: (k, 0))] * na + [pl.BlockSpec((tk, nbw), lambda k: (k, 0))] * nb,
        out_specs=pl.BlockSpec((na * ma, nb * nbw), lambda k: (0, 0)),
        out_shape=_sds((na * ma, nb * nbw), BF16),
        scratch_shapes=[pltpu.VMEM((na * ma, nb * nbw), F32)],
        compiler_params=_params(("arbitrary",)),
    )(*a_parts, *b_parts)


def _grad_matmul(a, b, name, tmo, tno, tk=1024):
    tokens, m = a.shape
    n = b.shape[1]
    n_k = tokens // tk

    def body(a_ref, b_ref, o_ref, acc):
        @pl.when(pl.program_id(2) == 0)
        def _():
            acc[...] = jnp.zeros_like(acc)

        acc[...] += lax.dot_general(a_ref[...], b_ref[...], TN, preferred_element_type=F32)

        @pl.when(pl.program_id(2) == n_k - 1)
        def _():
            o_ref[...] = acc[...].astype(o_ref.dtype)

    return _call(
        body, name=name, grid=(m // tmo, n // tno, n_k),
        in_specs=[pl.BlockSpec((tk, tmo), lambda i, j, k: (k, i)),
                  pl.BlockSpec((tk, tno), lambda i, j, k: (k, j))],
        out_specs=pl.BlockSpec((tmo, tno), lambda i, j, k: (i, j)),
        out_shape=_sds((m, n), BF16),
        scratch_shapes=[pltpu.VMEM((tmo, tno), F32)],
        compiler_params=_params(("parallel", "parallel", "arbitrary")),
    )(a, b)


def _adamw(w, m, v, g, name, n_parts=0, tr=256):
    rows, cols = w.shape
    tr = min(tr, rows)
    c1 = 1.0 - ADAM_B1 ** ADAM_STEP
    c2 = 1.0 - ADAM_B2 ** ADAM_STEP

    def body(w_ref, m_ref, v_ref, g_ref, go_ref, d_ref, mo_ref, vo_ref):
        if n_parts:
            gv = g_ref[0].astype(F32)
            for p in range(1, n_parts):
                gv = gv + g_ref[p].astype(F32)
        else:
            gv = g_ref[...]
        go_ref[...] = gv
        mn = ADAM_B1 * m_ref[...] + (1.0 - ADAM_B1) * gv
        vn = ADAM_B2 * v_ref[...] + (1.0 - ADAM_B2) * (gv * gv)
        mo_ref[...] = mn
        vo_ref[...] = vn
        d_ref[...] = -ADAM_LR * ((mn / c1) / (jnp.sqrt(vn / c2) + ADAM_EPS) + ADAM_WD * w_ref[...])

    blk = pl.BlockSpec((tr, cols), lambda i: (i, 0))
    g_spec = pl.BlockSpec((n_parts, tr, cols), lambda i: (0, i, 0)) if n_parts else blk
    return _call(
        body, name=name, grid=(rows // tr,),
        in_specs=[blk, blk, blk, g_spec], out_specs=[blk] * 4,
        out_shape=[_sds((rows, cols), F32)] * 4,
        compiler_params=_params(("parallel",)),
    )(w, m, v, g)


def _cols_to_full(blocks):
    n, r, c = blocks.shape
    return jnp.transpose(blocks, (1, 0, 2)).reshape(r, n * c)


def _full_to_cols(full, n=N_DEV):
    r, c = full.shape
    return jnp.transpose(full.reshape(r, n, c // n), (1, 0, 2))


def _pad_lanes(v, width):
    return jnp.pad(v, ((0, 0), (0, width - v.shape[1])))


def kernel(x, c, w_ada, b_ada, g_mix, w_in, w_dw, b_dw, g_conv_ln, b_conv_ln, g_q, g_k, w_out, g_ffn, w_gate, w_up, w_down, loss_target, m_w_ada, m_b_ada, m_g_mix, m_w_in, m_w_dw, m_b_dw, m_g_conv_ln, m_b_conv_ln, m_g_q, m_g_k, m_w_out, m_g_ffn, m_w_gate, m_w_up, m_w_down, v_w_ada, v_b_ada, v_g_mix, v_w_in, v_w_dw, v_b_dw, v_g_conv_ln, v_b_conv_ln, v_g_q, v_g_k, v_w_out, v_g_ffn, v_w_gate, v_w_up, v_w_down):
    n_seq, seq, _ = x.shape
    tokens = n_seq * seq
    me = 4 * lax.axis_index("x") + 2 * lax.axis_index("y") + lax.axis_index("c")
    ada_cols = w_ada.shape[2]
    dw_cols = w_dw.shape[2]

    (c_g, w_in_g, w_dw_g) = _gather_by_chip([c, w_in[0].astype(BF16), w_dw[0]], "gather_weights")
    c_all = c_g.reshape(N_DEV * n_seq, D_MODEL)
    w_in_f = _cols_to_full(w_in_g)
    w_dw_f = _cols_to_full(w_dw_g)

    b_cols = lax.dynamic_slice(b_ada, (0, me * ada_cols), (1, ada_cols))
    mod_cols = _ada_fwd(c_all, w_ada[0], b_cols)
    (mod_g,) = _exchange([(mod_cols, False)], "gather_mod")
    mod_mine = lax.dynamic_slice(mod_g, (0, me * n_seq, 0), (N_DEV, n_seq, ada_cols))
    mod = jnp.transpose(mod_mine, (1, 0, 2)).reshape(n_seq, N_MOD, D_MODEL)
    mod8 = jnp.pad(mod, ((0, 0), (0, 8 - N_MOD), (0, 0)))

    x2 = x.reshape(tokens, D_MODEL)
    h1, proj = _mix_in(x2, mod8, g_mix, w_in_f, seq)
    proj3 = proj.reshape(n_seq, seq, D_IN)
    uc3 = _conv_fwd(proj3, w_dw_f, b_dw)
    g_q2, g_k2 = jnp.tile(g_q, (1, 2)), jnp.tile(g_k, (1, 2))
    y_att3, lse3, w_out_g, w_gate_g, w_up_g, w_down_g = _attn_fwd(
        proj3, g_q2, g_k2,
        [(w_out[0].astype(BF16), False), (w_gate[0].astype(BF16), False), (w_up[0].astype(BF16), False),
         (w_down[0].astype(BF16), False)])
    w_out_f = w_out_g.reshape(D_MODEL, D_MODEL)
    w_gate_f = _cols_to_full(w_gate_g)
    w_up_f = _cols_to_full(w_up_g)
    w_down_f = w_down_g.reshape(D_FF, D_MODEL)
    uc2 = uc3.reshape(tokens, D_CONV)
    y_att2 = y_att3.reshape(tokens, D_ATT)
    y_conv, mix, x1, h2 = _mix_out(uc2, y_att2, x2, mod8, g_conv_ln, b_conv_ln, g_ffn, w_out_f, seq)
    gate, up, dy, df, sq, dgate_f = _ffn_fwd(
        h2, w_gate_f, w_up_f, w_down_f, x1, loss_target.reshape(tokens, D_MODEL), mod8, seq)

    dgate, dup, act, dx1, dmix, dg_ffn, dmod_f = _ffn_bwd(
        df, gate, up, w_gate_f, w_up_f, w_down_f, x1, dy, mix, mod8, g_ffn, seq)
    duc2, do2, dgb_ln = _mix_out_bwd(dmix, uc2, g_conv_ln, b_conv_ln, w_out_f)
    d_a3, d_g3, dw_dw_p, db_dw_p = _conv_bwd(duc2.reshape(n_seq, seq, D_CONV), proj3, w_dw_f)
    gw_gate = _grad_matmul(h2, dgate, "grad_w_gate", D_MODEL, D_FF // 2)
    gw_up = _grad_matmul(h2, dup, "grad_w_up", D_MODEL, D_FF // 2)
    gw_down = _grad_matmul(act, df, "grad_w_down", D_FF // 2, D_MODEL)
    gw_out = _grad_matmul_parts([y_conv, y_att2], [dmix], "grad_w_out")
    d_q3, d_k3, d_v3, dg_qk, p_gate, p_up, p_down, p_out = _attn_bwd(
        proj3, do2.reshape(n_seq, seq, D_ATT), y_att3, lse3, g_q2, g_k2,
        [(_full_to_cols(gw_gate).astype(BF16), True), (_full_to_cols(gw_up).astype(BF16), True),
         (gw_down.astype(BF16).reshape(N_DEV, D_FF // N_DEV, D_MODEL), True),
         (gw_out.astype(BF16).reshape(N_DEV, D_MODEL // N_DEV, D_MODEL), True)])
    flat = lambda t: t.reshape(tokens, t.shape[-1])
    d_a, d_g, d_q, d_k, d_v = flat(d_a3), flat(d_g3), flat(d_q3), flat(d_k3), flat(d_v3)
    gw_in = _grad_matmul_parts([h1], [d_a, d_g, d_q, d_k, d_v], "grad_w_in")
    grad_x2, dg_mix, dmod_m, p_in = _mix_in_bwd(
        d_a, d_g, d_q, d_k, d_v, w_in_f, x2, dx1, mod8, g_mix, seq, [(_full_to_cols(gw_in).astype(BF16), True)])

    dmod = jnp.concatenate([dmod_m[:, 0], dmod_m[:, 1], dmod_f[:, 2], dmod_f[:, 0], dmod_f[:, 1], dgate_f[:, 0]], axis=1)
    dg_q = dg_qk[0:1, 0:HEAD_DIM] + dg_qk[0:1, HEAD_DIM:]
    dg_k = dg_qk[1:2, 0:HEAD_DIM] + dg_qk[1:2, HEAD_DIM:]
    loss_part = (0.5 / D_MODEL) * jnp.sum(sq[0:1, :], axis=1, keepdims=True)
    small = jnp.concatenate(
        [dg_mix[0:1], dg_ffn[0:1], db_dw_p[0:1], dgb_ln[0:1], dgb_ln[1:2],
         _pad_lanes(dg_q, LANES), _pad_lanes(dg_k, LANES), _pad_lanes(loss_part, LANES)], axis=1)
    n_small = small.shape[1] - LANES

    (dmod_g, small_g, dw_g) = _exchange([(dmod, False), (small, False), (dw_dw_p, False)], "gather_small_grads")

    dmod_all = dmod_g.reshape(N_DEV * n_seq, N_MOD * D_MODEL)
    dmod_cols = lax.dynamic_slice(dmod_all, (0, me * ada_cols), (N_DEV * n_seq, ada_cols))
    gw_ada, gb_ada = _ada_bwd(c_all, dmod_cols, dmod_all)

    res = {}
    res["w_ada"] = _adamw(w_ada[0], m_w_ada[0], v_w_ada[0], gw_ada, "adamw_w_ada")
    res["b_ada"] = _adamw(b_ada, m_b_ada, v_b_ada, gb_ada, "adamw_b_ada")
    res["w_in"] = _adamw(w_in[0], m_w_in[0], v_w_in[0], p_in, "adamw_w_in", N_DEV)
    res["w_out"] = _adamw(w_out[0], m_w_out[0], v_w_out[0], p_out, "adamw_w_out", N_DEV)
    res["w_gate"] = _adamw(w_gate[0], m_w_gate[0], v_w_gate[0], p_gate, "adamw_w_gate", N_DEV)
    res["w_up"] = _adamw(w_up[0], m_w_up[0], v_w_up[0], p_up, "adamw_w_up", N_DEV)
    res["w_down"] = _adamw(w_down[0], m_w_down[0], v_w_down[0], p_down, "adamw_w_down", N_DEV, tr=176)
    dw_mine = lax.dynamic_slice(dw_g, (0, 0, me * dw_cols), (N_DEV, CONV_WIDTH, dw_cols))
    res["w_dw"] = _adamw(w_dw[0], m_w_dw[0], v_w_dw[0], dw_mine, "adamw_w_dw", N_DEV)

    small_names = ["g_mix", "g_ffn", "b_dw", "g_conv_ln", "b_conv_ln", "g_q", "g_k"]
    small_w = {"g_mix": (g_mix, m_g_mix, v_g_mix), "g_ffn": (g_ffn, m_g_ffn, v_g_ffn), "b_dw": (b_dw, m_b_dw, v_b_dw),
               "g_conv_ln": (g_conv_ln, m_g_conv_ln, v_g_conv_ln), "b_conv_ln": (b_conv_ln, m_b_conv_ln, v_b_conv_ln),
               "g_q": (g_q, m_g_q, v_g_q), "g_k": (g_k, m_g_k, v_g_k)}
    widths = [max(small_w[n][0].shape[1], LANES) for n in small_names]
    packed = [jnp.concatenate([_pad_lanes(small_w[n][i], wd) for n, wd in zip(small_names, widths)], axis=1) for i in range(3)]
    outs = _adamw(packed[0], packed[1], packed[2], small_g[:, :, :n_small], "adamw_small", N_DEV)
    off = 0
    for n, wd in zip(small_names, widths):
        real = small_w[n][0].shape[1]
        res[n] = tuple(o[:, off:off + real] for o in outs)
        off += wd
    loss = jnp.sum(small_g[:, 0, n_small])

    order = ["w_ada", "b_ada", "g_mix", "w_in", "w_dw", "b_dw", "g_conv_ln", "b_conv_ln", "g_q", "g_k",
             "w_out", "g_ffn", "w_gate", "w_up", "w_down"]
    lead = {"w_ada", "w_in", "w_dw", "w_out", "w_gate", "w_up", "w_down"}
    grads, deltas, new_m, new_v = [], [], [], []
    for n in order:
        g, d, mn, vn = res[n]
        g, d, mn, vn = (t[None] if n in lead else t for t in (g, d, mn, vn))
        grads.append(g)
        deltas.append(d)
        new_m.append(mn)
        new_v.append(vn)
    return (loss, grad_x2.reshape(n_seq, seq, D_MODEL), *grads, *deltas, *new_m, *new_v)
```

```python
import numpy as np
import jax
import jax.numpy as jnp
from jax import lax
from jax.experimental import pallas as pl
from jax.experimental.pallas import tpu as pltpu

F32 = jnp.float32
BF16 = jnp.bfloat16

N_DEV = 8
D_MODEL = 1024
D_CONV = 512
D_ATT = 512
HEAD_DIM = 64
CONV_WIDTH = 31
D_IN = 2 * D_CONV + 3 * D_ATT
D_FF = 2816
N_MOD = 6
EPS = 1e-6
RADIUS = 64
DILATIONS = (1, 4, 16)
Q_BLOCK = 128
LANES = 128
VMEM_LIMIT = 56 * 1024 * 1024

ADAM_LR = 0.001
ADAM_B1 = 0.9
ADAM_B2 = 0.999
ADAM_EPS = 1e-08
ADAM_WD = 0.01
ADAM_STEP = 10

NT = (((1,), (1,)), ((), ()))
TN = (((0,), (0,)), ((), ()))


def _call(body, **kw):
    return pl.pallas_call(body, **kw)


def _params(sem=None, vmem=VMEM_LIMIT):
    return pltpu.CompilerParams(dimension_semantics=sem, vmem_limit_bytes=vmem)


def _sig(x):
    return 1.0 / (1.0 + jnp.exp(-x))


def _sds(shape, dtype):
    return jax.ShapeDtypeStruct(shape, dtype)


N_PEER = N_DEV - 1
ANY_SPEC = pl.BlockSpec(memory_space=pl.ANY)


def _exchange_copies(scatter, ins, outs, *sems):
    n = len(ins)
    if n == 0:
        return [], []
    send_sems, recv_sems, local_sems = sems
    x, y, c = lax.axis_index("x"), lax.axis_index("y"), lax.axis_index("c")
    me = 4 * x + 2 * y + c

    def src(a, slot):
        return ins[a].at[slot] if scatter[a] else ins[a]

    local = [pltpu.make_async_copy(src(a, me), outs[a].at[me], local_sems.at[a]) for a in range(n)]
    flights = []
    for k in range(1, N_DEV):
        px = 1 - x if k & 4 else x
        py = 1 - y if k & 2 else y
        pc = 1 - c if k & 1 else c
        pid = 4 * px + 2 * py + pc
        for a in range(n):
            i = a * N_PEER + k - 1
            send, recv = (pltpu.make_async_remote_copy(
                src_ref=src(a, pid), dst_ref=outs[a].at[slot],
                send_sem=send_sems.at[i], recv_sem=recv_sems.at[i],
                device_id=(px, py, pc), device_id_type=pl.DeviceIdType.MESH) for slot in (me, pid))
            flights.append((send, recv))
    return local, flights


def _exchange_start(*args):
    local, flights = _exchange_copies(*args)
    for cp in local:
        cp.start()
    for send, _ in flights:
        send.start()


def _exchange_wait(*args):
    local, flights = _exchange_copies(*args)
    for send, recv in flights:
        send.wait_send()
        recv.wait_recv()
    for cp in local:
        cp.wait()


def _exchange_shapes(items):
    return [_sds((N_DEV,) + tuple(arr.shape[1:] if scatter else arr.shape), arr.dtype) for arr, scatter in items]


def _exchange_sems(n):
    if n == 0:
        return []
    return [pltpu.SemaphoreType.DMA((n * N_PEER,)), pltpu.SemaphoreType.DMA((n * N_PEER,)),
            pltpu.SemaphoreType.DMA((n,))]


def _gather_by_chip(arrays, name):
    n = len(arrays)
    per = N_PEER

    def body(*refs):
        ins, outs = refs[:n], refs[n:2 * n]
        send_sems, recv_sems, local_sems = refs[2 * n:]
        x, y, c = lax.axis_index("x"), lax.axis_index("y"), lax.axis_index("c")
        sibling = (x, y, 1 - c)
        chips = [(1 - x, y), (x, 1 - y), (1 - x, 1 - y)]

        def slot(px, py, pc):
            return 4 * px + 2 * py + pc

        def copy(a, k, block, to, src=None):
            dst = outs[a].at[slot(*block)]
            return pltpu.make_async_remote_copy(
                src_ref=dst if src is None else src, dst_ref=dst,
                send_sem=send_sems.at[a * per + k], recv_sem=recv_sems.at[a * per + k],
                device_id=to, device_id_type=pl.DeviceIdType.MESH)

        me = (x, y, c)
        local = [pltpu.make_async_copy(ins[a], outs[a].at[slot(*me)], local_sems.at[a]) for a in range(n)]
        for cp in local:
            cp.start()
        first = []
        for a in range(n):
            first.append(copy(a, 0, me, sibling, src=ins[a]))
            first += [copy(a, 1 + j, me, (*chip, c), src=ins[a]) for j, chip in enumerate(chips)]
        for cp in first:
            cp.start()
        passed = []
        for j, chip in enumerate(chips):
            for a in range(n):
                copy(a, 1 + j, (*chip, c), me).wait_recv()
                fwd = copy(a, 4 + j, (*chip, c), sibling)
                fwd.start()
                passed.append(fwd)
        for a in range(n):
            copy(a, 0, sibling, me).wait_recv()
            for j, chip in enumerate(chips):
                copy(a, 4 + j, (*chip, 1 - c), me).wait_recv()
        for cp in first + passed:
            cp.wait_send()
        for cp in local:
            cp.wait()

    return _call(
        body, name=name, out_shape=_exchange_shapes([(arr, False) for arr in arrays]),
        in_specs=[ANY_SPEC] * n, out_specs=[ANY_SPEC] * n, scratch_shapes=_exchange_sems(n),
    )(*arrays)


def _exchange(items, name):
    n = len(items)
    scatter = [s for _, s in items]

    def body(*refs):
        args = (scatter, refs[:n], refs[n:2 * n]) + tuple(refs[2 * n:])
        _exchange_start(*args)
        _exchange_wait(*args)

    return _call(
        body, name=name, out_shape=_exchange_shapes(items),
        in_specs=[ANY_SPEC] * n, out_specs=[ANY_SPEC] * n, scratch_shapes=_exchange_sems(n),
    )(*[a for a, _ in items])


def _ada_fwd(c_all, w_ada, b_cols):
    def body(c_ref, w_ref, b_ref, o_ref):
        cv = c_ref[...]
        sc = (cv * _sig(cv)).astype(BF16)
        o_ref[...] = jnp.dot(sc, w_ref[...].astype(BF16), preferred_element_type=F32) + b_ref[...]

    return _call(body, name="ada_fwd", out_shape=_sds((c_all.shape[0], w_ada.shape[1]), F32),
                 compiler_params=_params())(c_all, w_ada, b_cols)


def _ada_bwd(c_all, dmod_cols, dmod_all):
    def body(c_ref, dc_ref, da_ref, gw_ref, gb_ref):
        cv = c_ref[...]
        sc = (cv * _sig(cv)).astype(BF16)
        gw_ref[...] = lax.dot_general(sc, dc_ref[...].astype(BF16), TN, preferred_element_type=F32)
        gb_ref[...] = jnp.sum(da_ref[...], axis=0, keepdims=True)

    return _call(body, name="ada_bwd",
                 out_shape=[_sds((c_all.shape[1], dmod_cols.shape[1]), F32), _sds((1, dmod_all.shape[1]), F32)],
                 compiler_params=_params())(c_all, dmod_cols, dmod_all)


MIX_ROWS = 128


def _mix_in(x2, mod8, g_mix, w_in, seq, tm=512):
    tokens = x2.shape[0]
    per_seq = seq // tm

    def body(x_ref, m_ref, g_ref, w_ref, h_ref, p_ref):
        def normed(c):
            rows = pl.ds(c * MIX_ROWS, MIX_ROWS)
            xv = x_ref[rows, :]
            r = lax.rsqrt(jnp.mean(xv * xv, axis=-1, keepdims=True) + EPS)
            hb = ((xv * r * g_ref[...]) * (1.0 + m_ref[1:2, :]) + m_ref[0:1, :]).astype(BF16)
            h_ref[rows, :] = hb
            return hb

        ahead = normed(0)
        for c in range(tm // MIX_ROWS):
            hb = ahead
            if c + 1 < tm // MIX_ROWS:
                ahead = normed(c + 1)
            p = jnp.dot(hb, w_ref[...], preferred_element_type=F32)
            for cb in range(D_IN // LANES):
                p_ref[cb, pl.ds(c * MIX_ROWS, MIX_ROWS), :] = p[:, cb * LANES:(cb + 1) * LANES]

    return _call(
        body, name="mix_in", grid=(tokens // tm,),
        in_specs=[pl.BlockSpec((tm, D_MODEL), lambda i: (i, 0)),
                  pl.BlockSpec((None, 8, D_MODEL), lambda i: (i // per_seq, 0, 0)),
                  pl.BlockSpec((1, D_MODEL), lambda i: (0, 0)),
                  pl.BlockSpec((D_MODEL, D_IN), lambda i: (0, 0))],
        out_specs=[pl.BlockSpec((tm, D_MODEL), lambda i: (i, 0)),
                   pl.BlockSpec((D_IN // LANES, tm, LANES), lambda i: (0, i, 0))],
        out_shape=[_sds((tokens, D_MODEL), BF16), _sds((D_IN // LANES, tokens, LANES), F32)],
        compiler_params=_params(("parallel",)),
    )(x2, mod8, g_mix, w_in)


CONV_ROWS = 64
CONV_DW_ROWS = 32
CONV_DW_UNROLL = 4
CONV_HALO = 16


def _fill_shifted(xp, sh, seq):
    for b in range(8):
        sh[b, pl.ds(0, seq + 24), :] = xp[pl.ds(b, seq + 24), :]


def _conv_fwd(proj3, w_dw, b_dw):
    _, n_seq, seq, _ = proj3.shape
    n_cb = D_CONV // LANES

    def body(a_ref, g_ref, w_ref, b_ref, uc_ref, xp, sh):
        zeros = jnp.zeros((CONV_HALO, LANES), F32)
        xp[pl.ds(0, CONV_HALO), :] = zeros
        xp[pl.ds(CONV_HALO + seq, CONV_HALO), :] = zeros
        xp[pl.ds(CONV_HALO, seq), :] = a_ref[...] * _sig(g_ref[...])
        _fill_shifted(xp, sh, seq)

        def blk(i, carry):
            t0 = pl.multiple_of(i * CONV_ROWS, CONV_ROWS)
            acc = jnp.zeros((CONV_ROWS, LANES), F32)
            for j in range(CONV_WIDTH):
                jj = j + 1
                acc = acc + sh[jj % 8, pl.ds(t0 + 8 * (jj // 8), CONV_ROWS), :] * w_ref[j:j + 1, :]
            uc_ref[pl.ds(t0, CONV_ROWS), :] = acc + b_ref[...]
            return carry

        lax.fori_loop(0, seq // CONV_ROWS, blk, 0)

    return _call(
        body, name="conv_fwd", grid=(n_seq, n_cb),
        in_specs=[pl.BlockSpec((None, None, seq, LANES), lambda b, cb: (cb, b, 0, 0)),
                  pl.BlockSpec((None, None, seq, LANES), lambda b, cb: (n_cb + cb, b, 0, 0)),
                  pl.BlockSpec((CONV_WIDTH, LANES), lambda b, cb: (0, cb)),
                  pl.BlockSpec((1, LANES), lambda b, cb: (0, cb))],
        out_specs=pl.BlockSpec((None, seq, LANES), lambda b, cb: (b, 0, cb)),
        out_shape=_sds((n_seq, seq, D_CONV), F32),
        scratch_shapes=[pltpu.VMEM((seq + 2 * CONV_HALO, LANES), F32),
                        pltpu.VMEM((8, seq + 2 * CONV_HALO, LANES), F32)],
        compiler_params=_params(("parallel", "parallel")),
    )(proj3, proj3, w_dw, b_dw)


def _conv_bwd(duc3, proj3, w_dw):
    _, n_seq, seq, _ = proj3.shape
    n_cb = D_CONV // LANES

    def body(duc_ref, a_ref, g_ref, w_ref, da_ref, dg_ref, dw_ref, db_ref, xp, sh):
        @pl.when(pl.program_id(1) == 0)
        def _():
            dw_ref[...] = jnp.zeros_like(dw_ref)
            db_ref[...] = jnp.zeros_like(db_ref)

        zeros = jnp.zeros((CONV_HALO, LANES), F32)
        xp[pl.ds(0, CONV_HALO), :] = zeros
        xp[pl.ds(CONV_HALO + seq, CONV_HALO), :] = zeros
        xp[pl.ds(CONV_HALO, seq), :] = a_ref[...] * _sig(g_ref[...])
        _fill_shifted(xp, sh, seq)
        for j0 in range(0, CONV_WIDTH, 8):
            taps = range(j0, min(j0 + 8, CONV_WIDTH))

            def wblk(i, accs, taps=taps):
                for u in range(CONV_DW_UNROLL):
                    t0 = pl.multiple_of((i * CONV_DW_UNROLL + u) * CONV_DW_ROWS, CONV_DW_ROWS)
                    d = duc_ref[pl.ds(t0, CONV_DW_ROWS), :]
                    accs = tuple(acc + d * sh[(j + 1) % 8, pl.ds(t0 + 8 * ((j + 1) // 8), CONV_DW_ROWS), :]
                                 for acc, j in zip(accs, taps))
                return accs

            accs = lax.fori_loop(0, seq // (CONV_DW_ROWS * CONV_DW_UNROLL), wblk,
                                 tuple(jnp.zeros((CONV_DW_ROWS, LANES), F32) for _ in taps))
            for acc, j in zip(accs, taps):
                dw_ref[j:j + 1, :] += jnp.sum(acc, axis=0, keepdims=True)
        db_ref[0:1, :] += jnp.sum(duc_ref[...], axis=0, keepdims=True)
        xp[pl.ds(CONV_HALO, seq), :] = duc_ref[...]
        _fill_shifted(xp, sh, seq)

        def ublk(i, carry):
            t0 = pl.multiple_of(i * CONV_ROWS, CONV_ROWS)
            acc = jnp.zeros((CONV_ROWS, LANES), F32)
            for j in range(CONV_WIDTH):
                jj = CONV_WIDTH - j
                acc = acc + sh[jj % 8, pl.ds(t0 + 8 * (jj // 8), CONV_ROWS), :] * w_ref[j:j + 1, :]
            av = a_ref[pl.ds(t0, CONV_ROWS), :]
            sg = _sig(g_ref[pl.ds(t0, CONV_ROWS), :])
            da_ref[pl.ds(t0, CONV_ROWS), :] = (acc * sg).astype(BF16)
            dg_ref[pl.ds(t0, CONV_ROWS), :] = (acc * av * sg * (1.0 - sg)).astype(BF16)
            return carry

        lax.fori_loop(0, seq // CONV_ROWS, ublk, 0)

    return _call(
        body, name="conv_bwd", grid=(n_cb, n_seq),
        in_specs=[pl.BlockSpec((None, seq, LANES), lambda cb, b: (b, 0, cb)),
                  pl.BlockSpec((None, None, seq, LANES), lambda cb, b: (cb, b, 0, 0)),
                  pl.BlockSpec((None, None, seq, LANES), lambda cb, b: (n_cb + cb, b, 0, 0)),
                  pl.BlockSpec((CONV_WIDTH, LANES), lambda cb, b: (0, cb))],
        out_specs=[pl.BlockSpec((None, seq, LANES), lambda cb, b: (b, 0, cb)),
                   pl.BlockSpec((None, seq, LANES), lambda cb, b: (b, 0, cb)),
                   pl.BlockSpec((32, LANES), lambda cb, b: (0, cb)),
                   pl.BlockSpec((8, LANES), lambda cb, b: (0, cb))],
        out_shape=[_sds((n_seq, seq, D_CONV), BF16), _sds((n_seq, seq, D_CONV), BF16),
                   _sds((32, D_CONV), F32), _sds((8, D_CONV), F32)],
        scratch_shapes=[pltpu.VMEM((seq + 2 * CONV_HALO, LANES), F32),
                        pltpu.VMEM((8, seq + 2 * CONV_HALO, LANES), F32)],
        compiler_params=_params(("parallel", "arbitrary")),
    )(duc3, proj3, proj3, w_dw)


MASKED = 1e30
ATT_ROWS = 512
ATT_UNROLL = 8
ATT_FWD_UNROLL = 8


def _distance_mats(dil, seg_len):
    kw = min(2 * Q_BLOCK, seg_len)
    offsets = (0, -RADIUS, -2 * RADIUS) if kw == 2 * Q_BLOCK else (0,)
    a = np.arange(Q_BLOCK)[:, None]
    b = np.arange(kw)[None, :]
    mats = []
    for off in offsets:
        rel = np.abs(b + off - a)
        mats.append(np.where(rel <= RADIUS, dil * rel, MASKED))
    return jnp.asarray(np.stack(mats).astype(np.float32))


def _alibi_rows():
    s = np.zeros((4, 8, LANES), np.float32)
    for hp in range(4):
        for hl in range(2):
            s[hp, hl, :] = 2.0 ** (-(2 * hp + hl + 1))
    return jnp.asarray(s)


def _window(n, seg_len):
    i0 = pl.multiple_of(n * Q_BLOCK, Q_BLOCK)
    if seg_len <= Q_BLOCK:
        return i0, i0, 0
    per_seg = seg_len // Q_BLOCK
    j = n % per_seg
    seg0 = (n // per_seg) * seg_len
    ks_local = jnp.clip(j * Q_BLOCK - RADIUS, 0, seg_len - 2 * Q_BLOCK)
    ks = pl.multiple_of(seg0 + ks_local, RADIUS)
    var = jnp.where(j == 0, 0, jnp.where(j == per_seg - 1, 2, 1))
    return i0, ks, var


def _first_head(rows):
    return lax.broadcasted_iota(jnp.int32, (rows, LANES), 1) < HEAD_DIM


def _same_head():
    head = np.arange(LANES) // HEAD_DIM
    return jnp.asarray((head[:, None] == head[None, :]).astype(np.float32)).astype(BF16)


def _head_sum(x, same_ref):
    hi = x.astype(BF16)
    lo = (x - hi.astype(F32)).astype(BF16)
    return (jnp.dot(hi, same_ref[...], preferred_element_type=F32)
            + jnp.dot(lo, same_ref[...], preferred_element_type=F32))


def _head_mean(x, same_ref):
    return _head_sum(x, same_ref) * (1.0 / HEAD_DIM)


def _per_head(x, first):
    swapped = pltpu.roll(x, HEAD_DIM, 1)
    return jnp.where(first, x, swapped), jnp.where(first, swapped, x)


STRIDE = 4


def _gather_segments(src, dil, seq, tmp, put):
    if dil == 1:
        put(0, seq, src[pl.ds(0, seq), :])
    elif dil == STRIDE:
        seg = seq // dil
        for r in range(dil):
            put(r * seg, seg, src[pl.ds(r, seg, stride=dil), :])
    else:
        part, seg = seq // STRIDE, seq // dil
        for b in range(STRIDE):
            tmp[pl.ds(b * part, part), :] = src[pl.ds(b, part, stride=STRIDE), :]
        for b in range(STRIDE):
            for a in range(dil // STRIDE):
                put(b * part + a * seg, seg, tmp[pl.ds(b * part + a, seg, stride=dil // STRIDE), :])


def _scatter_segments(dst, get, dil, seq, tmp, accumulate):
    def write(rows, val):
        if accumulate:
            dst[rows, :] += val
        else:
            dst[rows, :] = val

    if dil == 1:
        write(pl.ds(0, seq), get(0, seq))
    elif dil == STRIDE:
        seg = seq // dil
        for r in range(dil):
            write(pl.ds(r, seg, stride=dil), get(r * seg, seg))
    else:
        part, seg = seq // STRIDE, seq // dil
        for b in range(STRIDE):
            for a in range(dil // STRIDE):
                tmp[pl.ds(b * part + a, seg, stride=dil // STRIDE), :] = get(b * part + a * seg, seg)
        for b in range(STRIDE):
            write(pl.ds(b, part, stride=STRIDE), tmp[pl.ds(b * part, part), :])


def _permute_rows(dst, src, dil, seq, tmp):
    def put(start, size, val):
        dst[pl.ds(start, size), :] = val.astype(dst.dtype)

    _gather_segments(src, dil, seq, tmp, put)


def _permute_rows_by_head(dst, src, dil, seq, tmp):
    def put(start, size, val):
        first = _first_head(size)
        dst[0, pl.ds(start, size), :] = jnp.where(first, val, 0.0).astype(dst.dtype)
        dst[1, pl.ds(start, size), :] = jnp.where(first, 0.0, val).astype(dst.dtype)

    _gather_segments(src, dil, seq, tmp, put)


def _qk_normalise(q_ref, g2_ref, same_ref, dst, seq, scale):
    def chunk(ci, carry):
        rows = pl.ds(pl.multiple_of(ci * ATT_ROWS, ATT_ROWS), ATT_ROWS)
        qv = q_ref[rows, :]
        r = lax.rsqrt(_head_mean(qv * qv, same_ref) + EPS)
        dst[rows, :] = qv * r * (g2_ref[...] * scale)
        return carry

    lax.fori_loop(0, seq // ATT_ROWS, chunk, 0)


def _attn_fwd(proj3, g_q2, g_k2, ride):
    _, n_seq, seq, _ = proj3.shape
    dms = [_distance_mats(d, seq // d) for d in DILATIONS]
    same = _same_head()
    col0 = 2 * D_CONV // LANES
    n_hp = D_ATT // LANES

    n_ride = len(ride)
    ride_scatter = [s for _, s in ride]

    def body(*refs):
        q_ref, k_ref, v_ref, gq_ref, gk_ref, sl_ref, dm1, dm4, dm16, same_ref = refs[:10]
        ride_in = refs[10:10 + n_ride]
        y_ref, lse_ref = refs[10 + n_ride:12 + n_ride]
        ride_out = refs[12 + n_ride:12 + 2 * n_ride]
        (qf, kf, qp, kp, vp, oml_p, o1, o4, o16, m1, m4, m16, l1, l4, l16,
         tmp) = refs[12 + 2 * n_ride:28 + 2 * n_ride]
        o_nat, m_nat, l_nat = (o1, o4, o16), (m1, m4, m16), (l1, l4, l16)
        ride_args = (ride_scatter, ride_in, ride_out) + tuple(refs[28 + 2 * n_ride:])
        step = pl.program_id(0) * n_hp + pl.program_id(1)

        @pl.when(step == 0)
        def _():
            _exchange_start(*ride_args)

        dm_refs = (dm1, dm4, dm16)
        _qk_normalise(q_ref, gq_ref, same_ref, qf, seq, HEAD_DIM ** -0.5)
        _qk_normalise(k_ref, gk_ref, same_ref, kf, seq, 1.0)
        slopes = (sl_ref[0:1, 0:1], sl_ref[1:2, 0:1])
        for pi, dil in enumerate(DILATIONS):
            seg = seq // dil
            kw = min(2 * Q_BLOCK, seg)
            _permute_rows_by_head(qp, qf, dil, seq, tmp)
            _permute_rows(kp, kf, dil, seq, tmp)
            _permute_rows(vp, v_ref, dil, seq, tmp)

            def blk(it, carry, seg=seg, kw=kw, pi=pi, dst=oml_p):
                first = _first_head(Q_BLOCK)
                chains = [(sub, h) for sub in range(ATT_FWD_UNROLL) for h in range(2)]
                win = [_window(it * ATT_FWD_UNROLL + sub, seg) for sub in range(ATT_FWD_UNROLL)]
                s = {}
                for sub, h in chains:
                    i0, ks, var = win[sub]
                    s[sub, h] = lax.dot_general(qp[h, pl.ds(i0, Q_BLOCK), :], kp[pl.ds(ks, kw), :], NT,
                                                preferred_element_type=F32) - slopes[h] * dm_refs[pi][var]
                m, l, p = {}, {}, {}
                for c in chains:
                    m[c] = jnp.max(s[c], axis=1, keepdims=True)
                    e = jnp.exp(s[c] - m[c])
                    l[c] = jnp.sum(e, axis=1, keepdims=True)
                    p[c] = e.astype(BF16)
                o = {}
                for sub, h in chains:
                    o[sub, h] = jnp.dot(p[sub, h], vp[pl.ds(win[sub][1], kw), :], preferred_element_type=F32)
                packed = [jnp.concatenate([jnp.where(first, t[sub, 0], t[sub, 1]) for t in (o, m, l)], axis=1)
                          for sub in range(ATT_FWD_UNROLL)]
                span = ATT_FWD_UNROLL * Q_BLOCK
                dst[pl.ds(pl.multiple_of(it * span, span), span), :] = jnp.concatenate(packed, axis=0)
                return carry

            lax.fori_loop(0, seq // (Q_BLOCK * ATT_FWD_UNROLL), blk, 0)
            for n, nat in enumerate((o_nat[pi], m_nat[pi], l_nat[pi])):
                _scatter_segments(nat, lambda start, size, n=n: oml_p[pl.ds(start, size), pl.ds(n * LANES, LANES)],
                                  dil, seq, tmp, accumulate=False)

        def merge(ci, carry):
            rows = pl.ds(pl.multiple_of(ci * ATT_ROWS, ATT_ROWS), ATT_ROWS)
            ms = [m_nat[pi][rows, :] for pi in range(3)]
            m_all = jnp.maximum(jnp.maximum(ms[0], ms[1]), ms[2])
            es = [jnp.exp(m - m_all) for m in ms]
            l_all = sum(l_nat[pi][rows, :] * es[pi] for pi in range(3))
            inv = 1.0 / l_all
            o = sum(o_nat[pi][rows, :] * (es[pi] * inv) for pi in range(3))
            y_ref[rows, :] = o.astype(BF16)
            lse_ref[rows, :] = m_all + jnp.log(l_all)
            return carry

        lax.fori_loop(0, seq // ATT_ROWS, merge, 0)

        @pl.when(step == n_seq * n_hp - 1)
        def _():
            _exchange_wait(*ride_args)

    def col(off):
        return pl.BlockSpec((None, None, seq, LANES), lambda b, hp: (col0 + off * n_hp + hp, b, 0, 0))

    def whole(arr):
        return pl.BlockSpec(arr.shape, lambda b, hp: (0,) * arr.ndim)

    rows_f32 = pltpu.VMEM((seq, LANES), F32)
    rows_bf16 = pltpu.VMEM((seq, LANES), BF16)
    return _call(
        body, name="attn_fwd", grid=(n_seq, n_hp),
        in_specs=[col(0), col(1), col(2), whole(g_q2), whole(g_k2),
                  pl.BlockSpec((None, 8, LANES), lambda b, hp: (hp, 0, 0)),
                  whole(dms[0]), whole(dms[1]), whole(dms[2]), whole(same)] + [ANY_SPEC] * n_ride,
        out_specs=[pl.BlockSpec((None, seq, LANES), lambda b, hp: (b, 0, hp)),
                   pl.BlockSpec((None, seq, LANES), lambda b, hp: (b, 0, hp))] + [ANY_SPEC] * n_ride,
        out_shape=[_sds((n_seq, seq, D_ATT), BF16), _sds((n_seq, seq, D_ATT), F32)] + _exchange_shapes(ride),
        scratch_shapes=[rows_f32, rows_f32, pltpu.VMEM((2, seq, LANES), BF16), rows_bf16, rows_bf16]
        + [pltpu.VMEM((seq, 3 * LANES), F32)] + [rows_f32] * 10 + _exchange_sems(n_ride),
        compiler_params=_params(("arbitrary", "arbitrary")),
    )(proj3, proj3, proj3, g_q2, g_k2, _alibi_rows(), *dms, same, *[a for a, _ in ride])


def _attn_bwd(proj3, do3, y_att3, lse3, g_q2, g_k2, ride):
    _, n_seq, seq, _ = proj3.shape
    dms = [_distance_mats(d, seq // d) for d in DILATIONS]
    same = _same_head()
    col0 = 2 * D_CONV // LANES
    n_hp = D_ATT // LANES

    n_ride = len(ride)
    ride_scatter = [s for _, s in ride]

    def body(*refs):
        (q_ref, k_ref, v_ref, do_ref, o_ref, lse_ref, gq_ref, gk_ref, sl_ref, dm1, dm4, dm16,
         same_ref) = refs[:13]
        ride_in = refs[13:13 + n_ride]
        dq_ref, dk_ref, dv_ref, dg_ref = refs[13 + n_ride:17 + n_ride]
        ride_out = refs[17 + n_ride:17 + 2 * n_ride]
        (qf, kf, qp, dop, kp, vp, sn, sp, dqp, dkp, dvp, dqn, dkn, dvn,
         tmp) = refs[17 + 2 * n_ride:32 + 2 * n_ride]
        ride_args = (ride_scatter, ride_in, ride_out) + tuple(refs[32 + 2 * n_ride:])
        dm_refs = (dm1, dm4, dm16)
        step = pl.program_id(0) * n_hp + pl.program_id(1)

        @pl.when(step == 0)
        def _():
            _exchange_start(*ride_args)
            dg_ref[...] = jnp.zeros_like(dg_ref)

        _qk_normalise(q_ref, gq_ref, same_ref, qf, seq, HEAD_DIM ** -0.5)
        _qk_normalise(k_ref, gk_ref, same_ref, kf, seq, 1.0)

        def stats(ci, carry):
            rows = pl.ds(pl.multiple_of(ci * ATT_ROWS, ATT_ROWS), ATT_ROWS)
            first = _first_head(ATT_ROWS)
            sn[0, rows, :], sn[1, rows, :] = _per_head(lse_ref[rows, :], first)
            prod = do_ref[rows, :] * o_ref[rows, :].astype(F32)
            sn[2, rows, :], sn[3, rows, :] = _per_head(_head_sum(prod, same_ref), first)
            return carry

        lax.fori_loop(0, seq // ATT_ROWS, stats, 0)
        slopes = (sl_ref[0:1, 0:1], sl_ref[1:2, 0:1])
        half = seq // (Q_BLOCK * ATT_UNROLL)
        region = seq // ATT_UNROLL

        for pi, dil in enumerate(DILATIONS):
            seg = seq // dil
            kw = min(2 * Q_BLOCK, seg)
            _permute_rows_by_head(qp, qf, dil, seq, tmp)
            _permute_rows_by_head(dop, do_ref, dil, seq, tmp)
            _permute_rows(kp, kf, dil, seq, tmp)
            _permute_rows(vp, v_ref, dil, seq, tmp)
            if dil == 1:
                st = sn
            else:
                st = sp
                for n in range(4):
                    _permute_rows(sp.at[n], sn.at[n], dil, seq, tmp)
            def touched(sub, seg=seg):
                lo, hi = sub * region, (sub + 1) * region
                if seg < region:
                    return lo, hi
                seg0 = lo // seg * seg
                return max(lo - RADIUS, seg0), min(hi + RADIUS, seg0 + seg)

            def summed(acc, start, size, touched=touched):
                pieces = []
                for c0 in range(start, start + size, RADIUS):
                    owners = [s for s in range(ATT_UNROLL) if touched(s)[0] <= c0 and c0 + RADIUS <= touched(s)[1]]
                    if pieces and pieces[-1][2] == owners:
                        pieces[-1][1] += RADIUS
                    else:
                        pieces.append([c0, RADIUS, owners])
                vals = [sum(acc[o, pl.ds(c0, n), :] for o in owners) for c0, n, owners in pieces]
                return vals[0] if len(vals) == 1 else jnp.concatenate(vals, axis=0)

            for sub in range(ATT_UNROLL):
                lo, hi = touched(sub)
                dkp[sub, pl.ds(lo, hi - lo), :] = jnp.zeros((hi - lo, LANES), F32)
                dvp[sub, pl.ds(lo, hi - lo), :] = jnp.zeros((hi - lo, LANES), F32)

            def blk(it, carry, seg=seg, kw=kw, pi=pi, st=st):
                first = _first_head(Q_BLOCK)
                chains = [(sub, h) for sub in range(ATT_UNROLL) for h in range(2)]
                win = [_window(it + sub * half, seg) for sub in range(ATT_UNROLL)]
                qrows = [pl.ds(w[0], Q_BLOCK) for w in win]
                krows = [pl.ds(w[1], kw) for w in win]

                def over_keys(n, sub):
                    t = st[n, qrows[sub], :]
                    return t if kw == LANES else jnp.concatenate([t] * (kw // LANES), axis=1)

                s, dp = {}, {}
                for sub, h in chains:
                    s[sub, h] = lax.dot_general(qp[h, qrows[sub], :], kp[krows[sub], :], NT,
                                                preferred_element_type=F32) - slopes[h] * dm_refs[pi][win[sub][2]]
                    dp[sub, h] = lax.dot_general(dop[h, qrows[sub], :], vp[krows[sub], :], NT,
                                                 preferred_element_type=F32)
                p, ds = {}, {}
                for sub, h in chains:
                    e = jnp.exp(s[sub, h] - over_keys(h, sub))
                    ds[sub, h] = (e * (dp[sub, h] - over_keys(2 + h, sub))).astype(BF16)
                    p[sub, h] = e.astype(BF16)
                dq, dk, dv = {}, {}, {}
                for sub, h in chains:
                    dq[sub, h] = jnp.dot(ds[sub, h], kp[krows[sub], :], preferred_element_type=F32)
                    dk[sub, h] = lax.dot_general(ds[sub, h], qp[h, qrows[sub], :], TN, preferred_element_type=F32)
                    dv[sub, h] = lax.dot_general(p[sub, h], dop[h, qrows[sub], :], TN, preferred_element_type=F32)
                for sub in range(ATT_UNROLL):
                    dqp[qrows[sub], :] = jnp.where(first, dq[sub, 0], dq[sub, 1])
                    dkp[sub, krows[sub], :] += dk[sub, 0] + dk[sub, 1]
                    dvp[sub, krows[sub], :] += dv[sub, 0] + dv[sub, 1]
                return carry

            lax.fori_loop(0, half, blk, 0)
            first_pattern = pi == 0
            if first_pattern:
                for r0 in range(0, seq, region):
                    rows = pl.ds(r0, region)
                    dqn[rows, :] = dqp[rows, :]
                    dkn[rows, :] = summed(dkp, r0, region)
                    dvn[rows, :] = summed(dvp, r0, region)
            else:
                _scatter_segments(dqn, lambda start, size: dqp[pl.ds(start, size), :], dil, seq, tmp, accumulate=True)
                for nat, acc in ((dkn, dkp), (dvn, dvp)):
                    _scatter_segments(nat, lambda start, size, acc=acc: summed(acc, start, size),
                                      dil, seq, tmp, accumulate=True)

        def finish(ci, carry):
            rows = pl.ds(pl.multiple_of(ci * ATT_ROWS, ATT_ROWS), ATT_ROWS)
            for src_ref, g_ref, dn, dst_ref, scale, row in (
                    (q_ref, gq_ref, dqn, dq_ref, HEAD_DIM ** -0.5, 0), (k_ref, gk_ref, dkn, dk_ref, 1.0, 1)):
                xv = src_ref[rows, :]
                r = lax.rsqrt(_head_mean(xv * xv, same_ref) + EPS)
                xhat = xv * r
                d = dn[rows, :] * scale
                dg_ref[row:row + 1, :] += jnp.sum(d * xhat, axis=0, keepdims=True)
                dxh = d * g_ref[...]
                dst_ref[rows, :] = (r * (dxh - xhat * _head_mean(dxh * xhat, same_ref))).astype(BF16)
            dv_ref[rows, :] = dvn[rows, :].astype(BF16)
            return carry

        lax.fori_loop(0, seq // ATT_ROWS, finish, 0)

        @pl.when(step == n_seq * n_hp - 1)
        def _():
            _exchange_wait(*ride_args)

    def col(off):
        return pl.BlockSpec((None, None, seq, LANES), lambda b, hp: (col0 + off * n_hp + hp, b, 0, 0))

    def whole(arr):
        return pl.BlockSpec(arr.shape, lambda b, hp: (0,) * arr.ndim)

    att = pl.BlockSpec((None, seq, LANES), lambda b, hp: (b, 0, hp))
    rows_f32 = pltpu.VMEM((seq, LANES), F32)
    rows_bf16 = pltpu.VMEM((seq, LANES), BF16)
    by_head_bf16 = pltpu.VMEM((2, seq, LANES), BF16)
    per_sub_f32 = pltpu.VMEM((ATT_UNROLL, seq, LANES), F32)
    stats_f32 = pltpu.VMEM((4, seq, LANES), F32)
    return _call(
        body, name="attn_bwd", grid=(n_seq, n_hp),
        in_specs=[col(0), col(1), col(2), att, att, att, whole(g_q2), whole(g_k2),
                  pl.BlockSpec((None, 8, LANES), lambda b, hp: (hp, 0, 0)),
                  whole(dms[0]), whole(dms[1]), whole(dms[2]), whole(same)] + [ANY_SPEC] * n_ride,
        out_specs=[att, att, att, pl.BlockSpec((8, LANES), lambda b, hp: (0, 0))] + [ANY_SPEC] * n_ride,
        out_shape=[_sds((n_seq, seq, D_ATT), BF16)] * 3 + [_sds((8, LANES), F32)] + _exchange_shapes(ride),
        scratch_shapes=[rows_f32, rows_f32, by_head_bf16, by_head_bf16, rows_bf16, rows_bf16, stats_f32, stats_f32,
                        rows_f32, per_sub_f32, per_sub_f32, rows_f32, rows_f32, rows_f32, rows_f32]
        + _exchange_sems(n_ride),
        compiler_params=_params(("arbitrary", "arbitrary")),
    )(proj3, proj3, proj3, do3, y_att3, lse3, g_q2, g_k2, _alibi_rows(), *dms, same, *[a for a, _ in ride])


def _mix_out(uc2, y_att2, x2, mod8, g_ln, b_ln, g_ffn, w_out, seq, tm=512):
    tokens = x2.shape[0]
    per_seq = seq // tm

    def body(uc_ref, ya_ref, x_ref, m_ref, gl_ref, bl_ref, gf_ref, w_ref, yc_ref, mix_ref, x1_ref, h2_ref):
        uc = uc_ref[...]
        mu = jnp.mean(uc, axis=-1, keepdims=True)
        cen = uc - mu
        rs = lax.rsqrt(jnp.mean(cen * cen, axis=-1, keepdims=True) + EPS)
        z = cen * rs * gl_ref[...] + bl_ref[...]
        yc = (z * _sig(z)).astype(BF16)
        yc_ref[...] = yc
        mix = (jnp.dot(yc, w_ref[pl.ds(0, D_CONV), :], preferred_element_type=F32)
               + jnp.dot(ya_ref[...], w_ref[pl.ds(D_CONV, D_ATT), :], preferred_element_type=F32))
        mix_ref[...] = mix
        x1 = x_ref[...] + m_ref[2:3, :] * mix
        x1_ref[...] = x1
        r = lax.rsqrt(jnp.mean(x1 * x1, axis=-1, keepdims=True) + EPS)
        h2_ref[...] = ((x1 * r * gf_ref[...]) * (1.0 + m_ref[4:5, :]) + m_ref[3:4, :]).astype(BF16)

    def rows(width):
        return pl.BlockSpec((tm, width), lambda i: (i, 0))

    def vec(width):
        return pl.BlockSpec((1, width), lambda i: (0, 0))

    return _call(
        body, name="mix_out", grid=(tokens // tm,),
        in_specs=[rows(D_CONV), rows(D_ATT), rows(D_MODEL),
                  pl.BlockSpec((None, 8, D_MODEL), lambda i: (i // per_seq, 0, 0)),
                  vec(D_CONV), vec(D_CONV), vec(D_MODEL),
                  pl.BlockSpec((D_MODEL, D_MODEL), lambda i: (0, 0))],
        out_specs=[rows(D_CONV), rows(D_MODEL), rows(D_MODEL), rows(D_MODEL)],
        out_shape=[_sds((tokens, D_CONV), BF16), _sds((tokens, D_MODEL), F32),
                   _sds((tokens, D_MODEL), F32), _sds((tokens, D_MODEL), BF16)],
        compiler_params=_params(("parallel",)),
    )(uc2, y_att2, x2, mod8, g_ln, b_ln, g_ffn, w_out)


def _mix_out_bwd(dmix, uc2, g_ln, b_ln, w_out, tm=512):
    tokens = dmix.shape[0]

    def body(dm_ref, uc_ref, gl_ref, bl_ref, w_ref, duc_ref, do_ref, dgb_ref):
        @pl.when(pl.program_id(0) == 0)
        def _():
            dgb_ref[...] = jnp.zeros_like(dgb_ref)

        dmv = dm_ref[...]
        dyc = lax.dot_general(dmv, w_ref[pl.ds(0, D_CONV), :], NT, preferred_element_type=F32)
        do_ref[...] = lax.dot_general(dmv, w_ref[pl.ds(D_CONV, D_ATT), :], NT, preferred_element_type=F32)
        uc = uc_ref[...]
        mu = jnp.mean(uc, axis=-1, keepdims=True)
        cen = uc - mu
        rs = lax.rsqrt(jnp.mean(cen * cen, axis=-1, keepdims=True) + EPS)
        xh = cen * rs
        z = xh * gl_ref[...] + bl_ref[...]
        sg = _sig(z)
        dz = dyc * (sg * (1.0 + z * (1.0 - sg)))
        dgb_ref[0:1, :] += jnp.sum(dz * xh, axis=0, keepdims=True)
        dgb_ref[1:2, :] += jnp.sum(dz, axis=0, keepdims=True)
        dxh = dz * gl_ref[...]
        duc_ref[...] = rs * (dxh - jnp.mean(dxh, axis=-1, keepdims=True)
                             - xh * jnp.mean(dxh * xh, axis=-1, keepdims=True))

    return _call(
        body, name="mix_out_bwd", grid=(tokens // tm,),
        in_specs=[pl.BlockSpec((tm, D_MODEL), lambda i: (i, 0)),
                  pl.BlockSpec((tm, D_CONV), lambda i: (i, 0)),
                  pl.BlockSpec((1, D_CONV), lambda i: (0, 0)),
                  pl.BlockSpec((1, D_CONV), lambda i: (0, 0)),
                  pl.BlockSpec((D_MODEL, D_MODEL), lambda i: (0, 0))],
        out_specs=[pl.BlockSpec((tm, D_CONV), lambda i: (i, 0)),
                   pl.BlockSpec((tm, D_ATT), lambda i: (i, 0)),
                   pl.BlockSpec((8, D_CONV), lambda i: (0, 0))],
        out_shape=[_sds((tokens, D_CONV), F32), _sds((tokens, D_ATT), F32), _sds((8, D_CONV), F32)],
        compiler_params=_params(("arbitrary",)),
    )(dmix, uc2, g_ln, b_ln, w_out)


FF_TILE = 256
FF_ROWS = 256


def _ffn_fwd(h2, w_gate, w_up, w_down, x1, target, mod8, seq, tm=1024):
    tokens = h2.shape[0]
    per_seq = seq // tm
    n_seq = tokens // seq
    last = D_FF // FF_TILE - 1

    def body(h_ref, wg_ref, wu_ref, wd_ref, x1_ref, t_ref, m_ref, gate_ref, up_ref, dy_ref, df_ref, sq_ref, dgf_ref,
             f_ref):
        i, j = pl.program_id(0), pl.program_id(1)

        @pl.when(j == 0)
        def _():
            f_ref[...] = jnp.zeros_like(f_ref)

        @pl.when((j == 0) & (i == 0))
        def _():
            sq_ref[...] = jnp.zeros_like(sq_ref)

        @pl.when((j == 0) & (i % per_seq == 0))
        def _():
            dgf_ref[...] = jnp.zeros_like(dgf_ref)

        def gate_up(r):
            hv = h_ref[pl.ds(r * FF_ROWS, FF_ROWS), :]
            return (jnp.dot(hv, wg_ref[...], preferred_element_type=F32),
                    jnp.dot(hv, wu_ref[...], preferred_element_type=F32))

        ahead = gate_up(0)
        for r in range(tm // FF_ROWS):
            gate, up = ahead
            if r + 1 < tm // FF_ROWS:
                ahead = gate_up(r + 1)
            rows = pl.ds(r * FF_ROWS, FF_ROWS)
            gate_ref[rows, :] = gate.astype(BF16)
            up_ref[rows, :] = up.astype(BF16)
            act = (gate * _sig(gate) * up).astype(BF16)
            f_ref[rows, :] += jnp.dot(act, wd_ref[...], preferred_element_type=F32)

        @pl.when(j == last)
        def _():
            gate_f = m_ref[5:6, :]
            for r in range(tm // FF_ROWS):
                rows = pl.ds(r * FF_ROWS, FF_ROWS)
                fv = f_ref[rows, :]
                diff = x1_ref[rows, :] + gate_f * fv - t_ref[rows, :]
                sq_ref[0:1, :] += jnp.sum(diff * diff, axis=0, keepdims=True)
                dy = diff * (1.0 / D_MODEL)
                dy_ref[rows, :] = dy
                df_ref[rows, :] = (gate_f * dy).astype(BF16)
                dgf_ref[0:1, :] += jnp.sum(dy * fv, axis=0, keepdims=True)

    rows_spec = pl.BlockSpec((tm, D_MODEL), lambda i, j: (i, 0))
    per = pl.BlockSpec((None, 8, D_MODEL), lambda i, j: (i // per_seq, 0, 0))
    tile = pl.BlockSpec((tm, FF_TILE), lambda i, j: (i, j))
    return _call(
        body, name="ffn_fwd", grid=(tokens // tm, D_FF // FF_TILE),
        in_specs=[rows_spec,
                  pl.BlockSpec((D_MODEL, FF_TILE), lambda i, j: (0, j)),
                  pl.BlockSpec((D_MODEL, FF_TILE), lambda i, j: (0, j)),
                  pl.BlockSpec((FF_TILE, D_MODEL), lambda i, j: (j, 0)),
                  rows_spec, rows_spec, per],
        out_specs=[tile, tile, rows_spec, rows_spec, pl.BlockSpec((8, D_MODEL), lambda i, j: (0, 0)), per],
        out_shape=[_sds((tokens, D_FF), BF16), _sds((tokens, D_FF), BF16), _sds((tokens, D_MODEL), F32),
                   _sds((tokens, D_MODEL), BF16), _sds((8, D_MODEL), F32), _sds((n_seq, 8, D_MODEL), F32)],
        scratch_shapes=[pltpu.VMEM((tm, D_MODEL), F32)],
        compiler_params=_params(("arbitrary", "arbitrary")),
    )(h2, w_gate, w_up, w_down, x1, target, mod8)


def _ffn_bwd(df, gate, up, w_gate, w_up, w_down, x1, dy, mix, mod8, g_ffn, seq, tm=1024):
    tokens = df.shape[0]
    per_seq = seq // tm
    n_seq = tokens // seq
    last = D_FF // FF_TILE - 1

    def body(df_ref, gate_ref, up_ref, wg_ref, wu_ref, wd_ref, m_ref, g_ref, x1_hbm, dy_hbm, mix_hbm,
             dgate_ref, dup_ref, act_ref, dx1_ref, dmix_ref, dg_ref, dm_ref, dh_ref, late, late_sems):
        i, j = pl.program_id(0), pl.program_id(1)
        my_rows = pl.ds(pl.multiple_of(i * tm, tm), tm)
        fetches = [pltpu.make_async_copy(src.at[my_rows, :], late.at[n], late_sems.at[n])
                   for n, src in enumerate((x1_hbm, dy_hbm, mix_hbm))]

        @pl.when(j == 0)
        def _():
            dh_ref[...] = jnp.zeros_like(dh_ref)
            for cp in fetches:
                cp.start()

        @pl.when((j == 0) & (i == 0))
        def _():
            dg_ref[...] = jnp.zeros_like(dg_ref)

        @pl.when((j == 0) & (i % per_seq == 0))
        def _():
            dm_ref[...] = jnp.zeros_like(dm_ref)

        def d_act(r):
            return lax.dot_general(df_ref[pl.ds(r * FF_ROWS, FF_ROWS), :], wd_ref[...], NT,
                                   preferred_element_type=F32)

        ahead = d_act(0)
        for r in range(tm // FF_ROWS):
            dact = ahead
            if r + 1 < tm // FF_ROWS:
                ahead = d_act(r + 1)
            rows = pl.ds(r * FF_ROWS, FF_ROWS)
            gate = gate_ref[rows, :].astype(F32)
            up = up_ref[rows, :].astype(F32)
            sg = _sig(gate)
            silu = gate * sg
            act_ref[rows, :] = (silu * up).astype(BF16)
            dup = (dact * silu).astype(BF16)
            dgate = (dact * up * (sg * (1.0 + gate * (1.0 - sg)))).astype(BF16)
            dup_ref[rows, :] = dup
            dgate_ref[rows, :] = dgate
            dh_ref[rows, :] += (lax.dot_general(dgate, wg_ref[...], NT, preferred_element_type=F32)
                                + lax.dot_general(dup, wu_ref[...], NT, preferred_element_type=F32))

        @pl.when(j == last)
        def _():
            for cp in fetches:
                cp.wait()
            g = g_ref[...]
            for r in range(tm // FF_ROWS):
                rows = pl.ds(r * FF_ROWS, FF_ROWS)
                dh = dh_ref[rows, :]
                x1v = late[0, rows, :]
                rs = lax.rsqrt(jnp.mean(x1v * x1v, axis=-1, keepdims=True) + EPS)
                xhat = x1v * rs
                dm_ref[0:1, :] += jnp.sum(dh, axis=0, keepdims=True)
                dm_ref[1:2, :] += jnp.sum(dh * (xhat * g), axis=0, keepdims=True)
                dn = dh * (1.0 + m_ref[4:5, :])
                dg_ref[0:1, :] += jnp.sum(dn * xhat, axis=0, keepdims=True)
                dxh = dn * g
                dx1 = late[1, rows, :] + rs * (dxh - xhat * jnp.mean(dxh * xhat, axis=-1, keepdims=True))
                dx1_ref[rows, :] = dx1
                dm_ref[2:3, :] += jnp.sum(dx1 * late[2, rows, :], axis=0, keepdims=True)
                dmix_ref[rows, :] = (m_ref[2:3, :] * dx1).astype(BF16)

    tile = pl.BlockSpec((tm, FF_TILE), lambda i, j: (i, j))
    rows_spec = pl.BlockSpec((tm, D_MODEL), lambda i, j: (i, 0))
    per = pl.BlockSpec((None, 8, D_MODEL), lambda i, j: (i // per_seq, 0, 0))
    return _call(
        body, name="ffn_bwd", grid=(tokens // tm, D_FF // FF_TILE),
        in_specs=[rows_spec, tile, tile,
                  pl.BlockSpec((D_MODEL, FF_TILE), lambda i, j: (0, j)),
                  pl.BlockSpec((D_MODEL, FF_TILE), lambda i, j: (0, j)),
                  pl.BlockSpec((FF_TILE, D_MODEL), lambda i, j: (j, 0)),
                  per, pl.BlockSpec((1, D_MODEL), lambda i, j: (0, 0)), ANY_SPEC, ANY_SPEC, ANY_SPEC],
        out_specs=[tile, tile, tile, rows_spec, rows_spec, pl.BlockSpec((8, D_MODEL), lambda i, j: (0, 0)), per],
        out_shape=[_sds((tokens, D_FF), BF16)] * 3 + [_sds((tokens, D_MODEL), F32), _sds((tokens, D_MODEL), BF16),
                                                    _sds((8, D_MODEL), F32), _sds((n_seq, 8, D_MODEL), F32)],
        scratch_shapes=[pltpu.VMEM((tm, D_MODEL), F32), pltpu.VMEM((3, tm, D_MODEL), F32),
                        pltpu.SemaphoreType.DMA((3,))],
        compiler_params=_params(("arbitrary", "arbitrary")),
    )(df, gate, up, w_gate, w_up, w_down, mod8, g_ffn, x1, dy, mix)


def _mix_in_bwd(d_a, d_g, d_q, d_k, d_v, w_in, x2, dx1, mod8, g_mix, seq, ride, tm=512):
    tokens = x2.shape[0]
    per_seq = seq // tm
    n_seq = tokens // seq
    parts = (d_a, d_g, d_q, d_k, d_v)
    width = D_CONV
    n_ride = len(ride)
    ride_scatter = [s for _, s in ride]

    def body(*refs):
        da_ref, dg_ref, dq_ref, dk_ref, dv_ref, w_ref, x_ref, dx1_ref, m_ref, g_ref = refs[:10]
        ride_in = refs[10:10 + n_ride]
        gx_ref, dgm_ref, dm_ref = refs[10 + n_ride:13 + n_ride]
        ride_args = (ride_scatter, ride_in, refs[13 + n_ride:13 + 2 * n_ride]) + tuple(refs[13 + 2 * n_ride:])
        i = pl.program_id(0)

        @pl.when(i == 0)
        def _():
            _exchange_start(*ride_args)
            dgm_ref[...] = jnp.zeros_like(dgm_ref)

        @pl.when(i % per_seq == 0)
        def _():
            dm_ref[...] = jnp.zeros_like(dm_ref)

        dh = jnp.zeros((tm, D_MODEL), F32)
        for n, ref in enumerate((da_ref, dg_ref, dq_ref, dk_ref, dv_ref)):
            dh = dh + lax.dot_general(ref[...], w_ref[:, pl.ds(n * width, width)], NT, preferred_element_type=F32)
        xv = x_ref[...]
        r = lax.rsqrt(jnp.mean(xv * xv, axis=-1, keepdims=True) + EPS)
        xhat = xv * r
        g = g_ref[...]
        dm_ref[0:1, :] += jnp.sum(dh, axis=0, keepdims=True)
        dm_ref[1:2, :] += jnp.sum(dh * (xhat * g), axis=0, keepdims=True)
        dn = dh * (1.0 + m_ref[1:2, :])
        dgm_ref[0:1, :] += jnp.sum(dn * xhat, axis=0, keepdims=True)
        dxh = dn * g
        gx_ref[...] = dx1_ref[...] + r * (dxh - xhat * jnp.mean(dxh * xhat, axis=-1, keepdims=True))

        @pl.when(i == tokens // tm - 1)
        def _():
            _exchange_wait(*ride_args)

    rows = pl.BlockSpec((tm, D_MODEL), lambda i: (i, 0))
    half = pl.BlockSpec((tm, width), lambda i: (i, 0))
    per = pl.BlockSpec((None, 8, D_MODEL), lambda i: (i // per_seq, 0, 0))
    return _call(
        body, name="mix_in_bwd", grid=(tokens // tm,),
        in_specs=[half] * 5 + [pl.BlockSpec((D_MODEL, D_IN), lambda i: (0, 0)), rows, rows, per,
                               pl.BlockSpec((1, D_MODEL), lambda i: (0, 0))] + [ANY_SPEC] * n_ride,
        out_specs=[rows, pl.BlockSpec((8, D_MODEL), lambda i: (0, 0)), per] + [ANY_SPEC] * n_ride,
        out_shape=[_sds((tokens, D_MODEL), F32), _sds((8, D_MODEL), F32), _sds((n_seq, 8, D_MODEL), F32)]
        + _exchange_shapes(ride),
        scratch_shapes=_exchange_sems(n_ride),
        compiler_params=_params(("arbitrary",)),
    )(*parts, w_in, x2, dx1, mod8, g_mix, *[a for a, _ in ride])


def _grad_matmul_parts(a_parts, b_parts, name, tk=1024):
    tokens = a_parts[0].shape[0]
    na, nb = len(a_parts), len(b_parts)
    ma, nbw = a_parts[0].shape[1], b_parts[0].shape[1]

    n_k = tokens // tk

    def body(*refs):
        a_refs, b_refs, o_ref, acc = refs[:na], refs[na:na + nb], refs[na + nb], refs[na + nb + 1]

        @pl.when(pl.program_id(0) == 0)
        def _():
            acc[...] = jnp.zeros_like(acc)

        for i in range(na):
            for j in range(nb):
                acc[pl.ds(i * ma, ma), pl.ds(j * nbw, nbw)] += lax.dot_general(
                    a_refs[i][...], b_refs[j][...], TN, preferred_element_type=F32)

        @pl.when(pl.program_id(0) == n_k - 1)
        def _():
            o_ref[...] = acc[...].astype(o_ref.dtype)

    return _call(
        body, name=name, grid=(n_k,),
        in_specs=[pl.BlockSpec((tk, ma), lambda k: (k, 0))] * na + [pl.BlockSpec((tk, nbw), lambda k: (k, 0))] * nb,
        out_specs=pl.BlockSpec((na * ma, nb * nbw), lambda k: (0, 0)),
        out_shape=_sds((na * ma, nb * nbw), BF16),
        scratch_shapes=[pltpu.VMEM((na * ma, nb * nbw), F32)],
        compiler_params=_params(("arbitrary",)),
    )(*a_parts, *b_parts)


def _grad_matmul(a, b, name, tmo, tno, tk=1024):
    tokens, m = a.shape
    n = b.shape[1]
    n_k = tokens // tk

    def body(a_ref, b_ref, o_ref, acc):
        @pl.when(pl.program_id(2) == 0)
        def _():
            acc[...] = jnp.zeros_like(acc)

        acc[...] += lax.dot_general(a_ref[...], b_ref[...], TN, preferred_element_type=F32)

        @pl.when(pl.program_id(2) == n_k - 1)
        def _():
            o_ref[...] = acc[...].astype(o_ref.dtype)

    return _call(
        body, name=name, grid=(m // tmo, n // tno, n_k),
        in_specs=[pl.BlockSpec((tk, tmo), lambda i, j, k: (k, i)),
                  pl.BlockSpec((tk, tno), lambda i, j, k: (k, j))],
        out_specs=pl.BlockSpec((tmo, tno), lambda i, j, k: (i, j)),
        out_shape=_sds((m, n), BF16),
        scratch_shapes=[pltpu.VMEM((tmo, tno), F32)],
        compiler_params=_params(("parallel", "parallel", "arbitrary")),
    )(a, b)


def _adamw(w, m, v, g, name, n_parts=0, tr=256):
    rows, cols = w.shape
    tr = min(tr, rows)
    c1 = 1.0 - ADAM_B1 ** ADAM_STEP
    c2 = 1.0 - ADAM_B2 ** ADAM_STEP

    def body(w_ref, m_ref, v_ref, g_ref, go_ref, d_ref, mo_ref, vo_ref):
        if n_parts:
            gv = g_ref[0].astype(F32)
            for p in range(1, n_parts):
                gv = gv + g_ref[p].astype(F32)
        else:
            gv = g_ref[...]
        go_ref[...] = gv
        mn = ADAM_B1 * m_ref[...] + (1.0 - ADAM_B1) * gv
        vn = ADAM_B2 * v_ref[...] + (1.0 - ADAM_B2) * (gv * gv)
        mo_ref[...] = mn
        vo_ref[...] = vn
        d_ref[...] = -ADAM_LR * ((mn / c1) / (jnp.sqrt(vn / c2) + ADAM_EPS) + ADAM_WD * w_ref[...])

    blk = pl.BlockSpec((tr, cols), lambda i: (i, 0))
    g_spec = pl.BlockSpec((n_parts, tr, cols), lambda i: (0, i, 0)) if n_parts else blk
    return _call(
        body, name=name, grid=(rows // tr,),
        in_specs=[blk, blk, blk, g_spec], out_specs=[blk] * 4,
        out_shape=[_sds((rows, cols), F32)] * 4,
        compiler_params=_params(("parallel",)),
    )(w, m, v, g)


def _cols_to_full(blocks):
    n, r, c = blocks.shape
    return jnp.transpose(blocks, (1, 0, 2)).reshape(r, n * c)


def _full_to_cols(full, n=N_DEV):
    r, c = full.shape
    return jnp.transpose(full.reshape(r, n, c // n), (1, 0, 2))


def _pad_lanes(v, width):
    return jnp.pad(v, ((0, 0), (0, width - v.shape[1])))


def kernel(x, c, w_ada, b_ada, g_mix, w_in, w_dw, b_dw, g_conv_ln, b_conv_ln, g_q, g_k, w_out, g_ffn, w_gate, w_up, w_down, loss_target, m_w_ada, m_b_ada, m_g_mix, m_w_in, m_w_dw, m_b_dw, m_g_conv_ln, m_b_conv_ln, m_g_q, m_g_k, m_w_out, m_g_ffn, m_w_gate, m_w_up, m_w_down, v_w_ada, v_b_ada, v_g_mix, v_w_in, v_w_dw, v_b_dw, v_g_conv_ln, v_b_conv_ln, v_g_q, v_g_k, v_w_out, v_g_ffn, v_w_gate, v_w_up, v_w_down):
    n_seq, seq, _ = x.shape
    tokens = n_seq * seq
    me = 4 * lax.axis_index("x") + 2 * lax.axis_index("y") + lax.axis_index("c")
    ada_cols = w_ada.shape[2]
    dw_cols = w_dw.shape[2]

    (c_g, w_in_g, w_dw_g) = _gather_by_chip([c, w_in[0].astype(BF16), w_dw[0]], "gather_weights")
    c_all = c_g.reshape(N_DEV * n_seq, D_MODEL)
    w_in_f = _cols_to_full(w_in_g)
    w_dw_f = _cols_to_full(w_dw_g)

    b_cols = lax.dynamic_slice(b_ada, (0, me * ada_cols), (1, ada_cols))
    mod_cols = _ada_fwd(c_all, w_ada[0], b_cols)
    (mod_g,) = _exchange([(mod_cols, False)], "gather_mod")
    mod_mine = lax.dynamic_slice(mod_g, (0, me * n_seq, 0), (N_DEV, n_seq, ada_cols))
    mod = jnp.transpose(mod_mine, (1, 0, 2)).reshape(n_seq, N_MOD, D_MODEL)
    mod8 = jnp.pad(mod, ((0, 0), (0, 8 - N_MOD), (0, 0)))

    x2 = x.reshape(tokens, D_MODEL)
    h1, proj = _mix_in(x2, mod8, g_mix, w_in_f, seq)
    proj3 = proj.reshape(D_IN // LANES, n_seq, seq, LANES)
    uc3 = _conv_fwd(proj3, w_dw_f, b_dw)
    g_q2, g_k2 = jnp.tile(g_q, (1, 2)), jnp.tile(g_k, (1, 2))
    y_att3, lse3, w_out_g, w_gate_g, w_up_g, w_down_g = _attn_fwd(
        proj3, g_q2, g_k2,
        [(w_out[0].astype(BF16), False), (w_gate[0].astype(BF16), False), (w_up[0].astype(BF16), False),
         (w_down[0].astype(BF16), False)])
    w_out_f = w_out_g.reshape(D_MODEL, D_MODEL)
    w_gate_f = _cols_to_full(w_gate_g)
    w_up_f = _cols_to_full(w_up_g)
    w_down_f = w_down_g.reshape(D_FF, D_MODEL)
    uc2 = uc3.reshape(tokens, D_CONV)
    y_att2 = y_att3.reshape(tokens, D_ATT)
    y_conv, mix, x1, h2 = _mix_out(uc2, y_att2, x2, mod8, g_conv_ln, b_conv_ln, g_ffn, w_out_f, seq)
    gate, up, dy, df, sq, dgate_f = _ffn_fwd(
        h2, w_gate_f, w_up_f, w_down_f, x1, loss_target.reshape(tokens, D_MODEL), mod8, seq)

    dgate, dup, act, dx1, dmix, dg_ffn, dmod_f = _ffn_bwd(
        df, gate, up, w_gate_f, w_up_f, w_down_f, x1, dy, mix, mod8, g_ffn, seq)
    duc2, do2, dgb_ln = _mix_out_bwd(dmix, uc2, g_conv_ln, b_conv_ln, w_out_f)
    d_a3, d_g3, dw_dw_p, db_dw_p = _conv_bwd(duc2.reshape(n_seq, seq, D_CONV), proj3, w_dw_f)
    gw_gate = _grad_matmul(h2, dgate, "grad_w_gate", D_MODEL, D_FF // 2)
    gw_up = _grad_matmul(h2, dup, "grad_w_up", D_MODEL, D_FF // 2)
    gw_down = _grad_matmul(act, df, "grad_w_down", D_FF // 2, D_MODEL)
    gw_out = _grad_matmul_parts([y_conv, y_att2], [dmix], "grad_w_out")
    d_q3, d_k3, d_v3, dg_qk, p_gate, p_up, p_down, p_out = _attn_bwd(
        proj3, do2.reshape(n_seq, seq, D_ATT), y_att3, lse3, g_q2, g_k2,
        [(_full_to_cols(gw_gate).astype(BF16), True), (_full_to_cols(gw_up).astype(BF16), True),
         (gw_down.astype(BF16).reshape(N_DEV, D_FF // N_DEV, D_MODEL), True),
         (gw_out.astype(BF16).reshape(N_DEV, D_MODEL // N_DEV, D_MODEL), True)])
    flat = lambda t: t.reshape(tokens, t.shape[-1])
    d_a, d_g, d_q, d_k, d_v = flat(d_a3), flat(d_g3), flat(d_q3), flat(d_k3), flat(d_v3)
    gw_in = _grad_matmul_parts([h1], [d_a, d_g, d_q, d_k, d_v], "grad_w_in")
    grad_x2, dg_mix, dmod_m, p_in = _mix_in_bwd(
        d_a, d_g, d_q, d_k, d_v, w_in_f, x2, dx1, mod8, g_mix, seq, [(_full_to_cols(gw_in).astype(BF16), True)])

    dmod = jnp.concatenate([dmod_m[:, 0], dmod_m[:, 1], dmod_f[:, 2], dmod_f[:, 0], dmod_f[:, 1], dgate_f[:, 0]], axis=1)
    dg_q = dg_qk[0:1, 0:HEAD_DIM] + dg_qk[0:1, HEAD_DIM:]
    dg_k = dg_qk[1:2, 0:HEAD_DIM] + dg_qk[1:2, HEAD_DIM:]
    loss_part = (0.5 / D_MODEL) * jnp.sum(sq[0:1, :], axis=1, keepdims=True)
    small = jnp.concatenate(
        [dg_mix[0:1], dg_ffn[0:1], db_dw_p[0:1], dgb_ln[0:1], dgb_ln[1:2],
         _pad_lanes(dg_q, LANES), _pad_lanes(dg_k, LANES), _pad_lanes(loss_part, LANES)], axis=1)
    n_small = small.shape[1] - LANES

    (dmod_g, small_g, dw_g) = _exchange([(dmod, False), (small, False), (dw_dw_p, False)], "gather_small_grads")

    dmod_all = dmod_g.reshape(N_DEV * n_seq, N_MOD * D_MODEL)
    dmod_cols = lax.dynamic_slice(dmod_all, (0, me * ada_cols), (N_DEV * n_seq, ada_cols))
    gw_ada, gb_ada = _ada_bwd(c_all, dmod_cols, dmod_all)

    res = {}
    res["w_ada"] = _adamw(w_ada[0], m_w_ada[0], v_w_ada[0], gw_ada, "adamw_w_ada")
    res["b_ada"] = _adamw(b_ada, m_b_ada, v_b_ada, gb_ada, "adamw_b_ada")
    res["w_in"] = _adamw(w_in[0], m_w_in[0], v_w_in[0], p_in, "adamw_w_in", N_DEV)
    res["w_out"] = _adamw(w_out[0], m_w_out[0], v_w_out[0], p_out, "adamw_w_out", N_DEV)
    res["w_gate"] = _adamw(w_gate[0], m_w_gate[0], v_w_gate[0], p_gate, "adamw_w_gate", N_DEV)
    res["w_up"] = _adamw(w_up[0], m_w_up[0], v_w_up[0], p_up, "adamw_w_up", N_DEV)
    res["w_down"] = _adamw(w_down[0], m_w_down[0], v_w_down[0], p_down, "adamw_w_down", N_DEV, tr=176)
    dw_mine = lax.dynamic_slice(dw_g, (0, 0, me * dw_cols), (N_DEV, CONV_WIDTH, dw_cols))
    res["w_dw"] = _adamw(w_dw[0], m_w_dw[0], v_w_dw[0], dw_mine, "adamw_w_dw", N_DEV)

    small_names = ["g_mix", "g_ffn", "b_dw", "g_conv_ln", "b_conv_ln", "g_q", "g_k"]
    small_w = {"g_mix": (g_mix, m_g_mix, v_g_mix), "g_ffn": (g_ffn, m_g_ffn, v_g_ffn), "b_dw": (b_dw, m_b_dw, v_b_dw),
               "g_conv_ln": (g_conv_ln, m_g_conv_ln, v_g_conv_ln), "b_conv_ln": (b_conv_ln, m_b_conv_ln, v_b_conv_ln),
               "g_q": (g_q, m_g_q, v_g_q), "g_k": (g_k, m_g_k, v_g_k)}
    widths = [max(small_w[n][0].shape[1], LANES) for n in small_names]
    packed = [jnp.concatenate([_pad_lanes(small_w[n][i], wd) for n, wd in zip(small_names, widths)], axis=1) for i in range(3)]
    outs = _adamw(packed[0], packed[1], packed[2], small_g[:, :, :n_small], "adamw_small", N_DEV)
    off = 0
    for n, wd in zip(small_names, widths):
        real = small_w[n][0].shape[1]
        res[n] = tuple(o[:, off:off + real] for o in outs)
        off += wd
    loss = jnp.sum(small_g[:, 0, n_small])

    order = ["w_ada", "b_ada", "g_mix", "w_in", "w_dw", "b_dw", "g_conv_ln", "b_conv_ln", "g_q", "g_k",
             "w_out", "g_ffn", "w_gate", "w_up", "w_down"]
    lead = {"w_ada", "w_in", "w_dw", "w_out", "w_gate", "w_up", "w_down"}
    grads, deltas, new_m, new_v = [], [], [], []
    for n in order:
        g, d, mn, vn = res[n]
        g, d, mn, vn = (t[None] if n in lead else t for t in (g, d, mn, vn))
        grads.append(g)
        deltas.append(d)
        new_m.append(mn)
        new_v.append(vn)
    return (loss, grad_x2.reshape(n_seq, seq, D_MODEL), *grads, *deltas, *new_m, *new_v)
```

```python
import numpy as np
import jax
import jax.numpy as jnp
from jax import lax
from jax.experimental import pallas as pl
from jax.experimental.pallas import tpu as pltpu

F32 = jnp.float32
BF16 = jnp.bfloat16

N_DEV = 8
D_MODEL = 1024
D_CONV = 512
D_ATT = 512
HEAD_DIM = 64
CONV_WIDTH = 31
D_IN = 2 * D_CONV + 3 * D_ATT
D_FF = 2816
N_MOD = 6
EPS = 1e-6
RADIUS = 64
DILATIONS = (1, 4, 16)
Q_BLOCK = 128
LANES = 128
VMEM_LIMIT = 56 * 1024 * 1024

ADAM_LR = 0.001
ADAM_B1 = 0.9
ADAM_B2 = 0.999
ADAM_EPS = 1e-08
ADAM_WD = 0.01
ADAM_STEP = 10

NT = (((1,), (1,)), ((), ()))
TN = (((0,), (0,)), ((), ()))


def _call(body, **kw):
    return pl.pallas_call(body, **kw)


def _params(sem=None, vmem=VMEM_LIMIT):
    return pltpu.CompilerParams(dimension_semantics=sem, vmem_limit_bytes=vmem)


def _sig(x):
    return 1.0 / (1.0 + jnp.exp(-x))


def _sds(shape, dtype):
    return jax.ShapeDtypeStruct(shape, dtype)


N_PEER = N_DEV - 1
ANY_SPEC = pl.BlockSpec(memory_space=pl.ANY)


def _exchange_copies(scatter, ins, outs, *sems):
    n = len(ins)
    if n == 0:
        return [], []
    send_sems, recv_sems, local_sems = sems
    x, y, c = lax.axis_index("x"), lax.axis_index("y"), lax.axis_index("c")
    me = 4 * x + 2 * y + c

    def src(a, slot):
        return ins[a].at[slot] if scatter[a] else ins[a]

    local = [pltpu.make_async_copy(src(a, me), outs[a].at[me], local_sems.at[a]) for a in range(n)]
    flights = []
    for k in range(1, N_DEV):
        px = 1 - x if k & 4 else x
        py = 1 - y if k & 2 else y
        pc = 1 - c if k & 1 else c
        pid = 4 * px + 2 * py + pc
        for a in range(n):
            i = a * N_PEER + k - 1
            send, recv = (pltpu.make_async_remote_copy(
                src_ref=src(a, pid), dst_ref=outs[a].at[slot],
                send_sem=send_sems.at[i], recv_sem=recv_sems.at[i],
                device_id=(px, py, pc), device_id_type=pl.DeviceIdType.MESH) for slot in (me, pid))
            flights.append((send, recv))
    return local, flights


def _exchange_start(*args):
    local, flights = _exchange_copies(*args)
    for cp in local:
        cp.start()
    for send, _ in flights:
        send.start()


def _exchange_wait(*args):
    local, flights = _exchange_copies(*args)
    for send, recv in flights:
        send.wait_send()
        recv.wait_recv()
    for cp in local:
        cp.wait()


def _exchange_shapes(items):
    return [_sds((N_DEV,) + tuple(arr.shape[1:] if scatter else arr.shape), arr.dtype) for arr, scatter in items]


def _exchange_sems(n):
    if n == 0:
        return []
    return [pltpu.SemaphoreType.DMA((n * N_PEER,)), pltpu.SemaphoreType.DMA((n * N_PEER,)),
            pltpu.SemaphoreType.DMA((n,))]


def _gather_by_chip_phase(phase, ins, outs, send_sems, recv_sems, local_sems):
    n = len(ins)
    per = N_PEER
    x, y, c = lax.axis_index("x"), lax.axis_index("y"), lax.axis_index("c")
    me, sibling = (x, y, c), (x, y, 1 - c)
    chips = [(1 - x, y), (x, 1 - y), (1 - x, 1 - y)]

    def slot(px, py, pc):
        return 4 * px + 2 * py + pc

    def copy(a, k, block, to, src=None):
        dst = outs[a].at[slot(*block)]
        return pltpu.make_async_remote_copy(
            src_ref=dst if src is None else src, dst_ref=dst,
            send_sem=send_sems.at[a * per + k], recv_sem=recv_sems.at[a * per + k],
            device_id=to, device_id_type=pl.DeviceIdType.MESH)

    local = [pltpu.make_async_copy(ins[a], outs[a].at[slot(*me)], local_sems.at[a]) for a in range(n)]
    first = []
    for a in range(n):
        first.append(copy(a, 0, me, sibling, src=ins[a]))
        first += [copy(a, 1 + j, me, (*chip, c), src=ins[a]) for j, chip in enumerate(chips)]
    passed = [copy(a, 4 + j, (*chip, c), sibling) for j, chip in enumerate(chips) for a in range(n)]
    if phase == 0:
        for cp in local + first:
            cp.start()
    elif phase == 1:
        for j, chip in enumerate(chips):
            for a in range(n):
                copy(a, 1 + j, (*chip, c), me).wait_recv()
        for cp in passed:
            cp.start()
    else:
        for a in range(n):
            copy(a, 0, sibling, me).wait_recv()
            for j, chip in enumerate(chips):
                copy(a, 4 + j, (*chip, 1 - c), me).wait_recv()
        for cp in first + passed:
            cp.wait_send()
        for cp in local:
            cp.wait()


def _gather_by_chip(arrays, name):
    n = len(arrays)

    def body(*refs):
        for phase in range(3):
            _gather_by_chip_phase(phase, refs[:n], refs[n:2 * n], *refs[2 * n:])

    return _call(
        body, name=name, out_shape=_exchange_shapes([(arr, False) for arr in arrays]),
        in_specs=[ANY_SPEC] * n, out_specs=[ANY_SPEC] * n, scratch_shapes=_exchange_sems(n),
    )(*arrays)


def _exchange(items, name):
    n = len(items)
    scatter = [s for _, s in items]

    def body(*refs):
        args = (scatter, refs[:n], refs[n:2 * n]) + tuple(refs[2 * n:])
        _exchange_start(*args)
        _exchange_wait(*args)

    return _call(
        body, name=name, out_shape=_exchange_shapes(items),
        in_specs=[ANY_SPEC] * n, out_specs=[ANY_SPEC] * n, scratch_shapes=_exchange_sems(n),
    )(*[a for a, _ in items])


def _ada_fwd(c_all, w_ada, b_cols):
    def body(c_ref, w_ref, b_ref, o_ref):
        cv = c_ref[...]
        sc = (cv * _sig(cv)).astype(BF16)
        o_ref[...] = jnp.dot(sc, w_ref[...].astype(BF16), preferred_element_type=F32) + b_ref[...]

    return _call(body, name="ada_fwd", out_shape=_sds((c_all.shape[0], w_ada.shape[1]), F32),
                 compiler_params=_params())(c_all, w_ada, b_cols)


def _ada_bwd(c_all, dmod_cols, dmod_all):
    def body(c_ref, dc_ref, da_ref, gw_ref, gb_ref):
        cv = c_ref[...]
        sc = (cv * _sig(cv)).astype(BF16)
        gw_ref[...] = lax.dot_general(sc, dc_ref[...].astype(BF16), TN, preferred_element_type=F32)
        gb_ref[...] = jnp.sum(da_ref[...], axis=0, keepdims=True)

    return _call(body, name="ada_bwd",
                 out_shape=[_sds((c_all.shape[1], dmod_cols.shape[1]), F32), _sds((1, dmod_all.shape[1]), F32)],
                 compiler_params=_params())(c_all, dmod_cols, dmod_all)


MIX_ROWS = 128


def _mix_in(x2, mod8, g_mix, w_in, seq, tm=512):
    tokens = x2.shape[0]
    per_seq = seq // tm

    def body(x_ref, m_ref, g_ref, w_ref, h_ref, p_ref):
        def normed(c):
            rows = pl.ds(c * MIX_ROWS, MIX_ROWS)
            xv = x_ref[rows, :]
            r = lax.rsqrt(jnp.mean(xv * xv, axis=-1, keepdims=True) + EPS)
            hb = ((xv * r * g_ref[...]) * (1.0 + m_ref[1:2, :]) + m_ref[0:1, :]).astype(BF16)
            h_ref[rows, :] = hb
            return hb

        ahead = normed(0)
        for c in range(tm // MIX_ROWS):
            hb = ahead
            if c + 1 < tm // MIX_ROWS:
                ahead = normed(c + 1)
            p = jnp.dot(hb, w_ref[...], preferred_element_type=F32)
            for cb in range(D_IN // LANES):
                p_ref[cb, pl.ds(c * MIX_ROWS, MIX_ROWS), :] = p[:, cb * LANES:(cb + 1) * LANES]

    return _call(
        body, name="mix_in", grid=(tokens // tm,),
        in_specs=[pl.BlockSpec((tm, D_MODEL), lambda i: (i, 0)),
                  pl.BlockSpec((None, 8, D_MODEL), lambda i: (i // per_seq, 0, 0)),
                  pl.BlockSpec((1, D_MODEL), lambda i: (0, 0)),
                  pl.BlockSpec((D_MODEL, D_IN), lambda i: (0, 0))],
        out_specs=[pl.BlockSpec((tm, D_MODEL), lambda i: (i, 0)),
                   pl.BlockSpec((D_IN // LANES, tm, LANES), lambda i: (0, i, 0))],
        out_shape=[_sds((tokens, D_MODEL), BF16), _sds((D_IN // LANES, tokens, LANES), F32)],
        compiler_params=_params(("parallel",)),
    )(x2, mod8, g_mix, w_in)


CONV_ROWS = 64
CONV_DW_ROWS = 32
CONV_DW_UNROLL = 4
CONV_HALO = 16


def _fill_shifted(xp, sh, seq):
    for b in range(8):
        sh[b, pl.ds(0, seq + 24), :] = xp[pl.ds(b, seq + 24), :]


def _conv_fwd(proj3, w_dw, b_dw):
    _, n_seq, seq, _ = proj3.shape
    n_cb = D_CONV // LANES

    def body(a_ref, g_ref, w_ref, b_ref, uc_ref, xp, sh):
        zeros = jnp.zeros((CONV_HALO, LANES), F32)
        xp[pl.ds(0, CONV_HALO), :] = zeros
        xp[pl.ds(CONV_HALO + seq, CONV_HALO), :] = zeros
        xp[pl.ds(CONV_HALO, seq), :] = a_ref[...] * _sig(g_ref[...])
        _fill_shifted(xp, sh, seq)

        def blk(i, carry):
            t0 = pl.multiple_of(i * CONV_ROWS, CONV_ROWS)
            acc = jnp.zeros((CONV_ROWS, LANES), F32)
            for j in range(CONV_WIDTH):
                jj = j + 1
                acc = acc + sh[jj % 8, pl.ds(t0 + 8 * (jj // 8), CONV_ROWS), :] * w_ref[j:j + 1, :]
            uc_ref[pl.ds(t0, CONV_ROWS), :] = acc + b_ref[...]
            return carry

        lax.fori_loop(0, seq // CONV_ROWS, blk, 0)

    return _call(
        body, name="conv_fwd", grid=(n_seq, n_cb),
        in_specs=[pl.BlockSpec((None, None, seq, LANES), lambda b, cb: (cb, b, 0, 0)),
                  pl.BlockSpec((None, None, seq, LANES), lambda b, cb: (n_cb + cb, b, 0, 0)),
                  pl.BlockSpec((CONV_WIDTH, LANES), lambda b, cb: (0, cb)),
                  pl.BlockSpec((1, LANES), lambda b, cb: (0, cb))],
        out_specs=pl.BlockSpec((None, seq, LANES), lambda b, cb: (b, 0, cb)),
        out_shape=_sds((n_seq, seq, D_CONV), F32),
        scratch_shapes=[pltpu.VMEM((seq + 2 * CONV_HALO, LANES), F32),
                        pltpu.VMEM((8, seq + 2 * CONV_HALO, LANES), F32)],
        compiler_params=_params(("parallel", "parallel")),
    )(proj3, proj3, w_dw, b_dw)


def _conv_bwd(duc3, proj3, w_dw):
    _, n_seq, seq, _ = proj3.shape
    n_cb = D_CONV // LANES

    def body(duc_ref, a_ref, g_ref, w_ref, da_ref, dg_ref, dw_ref, db_ref, xp, sh):
        @pl.when(pl.program_id(1) == 0)
        def _():
            dw_ref[...] = jnp.zeros_like(dw_ref)
            db_ref[...] = jnp.zeros_like(db_ref)

        zeros = jnp.zeros((CONV_HALO, LANES), F32)
        xp[pl.ds(0, CONV_HALO), :] = zeros
        xp[pl.ds(CONV_HALO + seq, CONV_HALO), :] = zeros
        xp[pl.ds(CONV_HALO, seq), :] = a_ref[...] * _sig(g_ref[...])
        _fill_shifted(xp, sh, seq)
        for j0 in range(0, CONV_WIDTH, 8):
            taps = range(j0, min(j0 + 8, CONV_WIDTH))

            def wblk(i, accs, taps=taps):
                for u in range(CONV_DW_UNROLL):
                    t0 = pl.multiple_of((i * CONV_DW_UNROLL + u) * CONV_DW_ROWS, CONV_DW_ROWS)
                    d = duc_ref[pl.ds(t0, CONV_DW_ROWS), :]
                    accs = tuple(acc + d * sh[(j + 1) % 8, pl.ds(t0 + 8 * ((j + 1) // 8), CONV_DW_ROWS), :]
                                 for acc, j in zip(accs, taps))
                return accs

            accs = lax.fori_loop(0, seq // (CONV_DW_ROWS * CONV_DW_UNROLL), wblk,
                                 tuple(jnp.zeros((CONV_DW_ROWS, LANES), F32) for _ in taps))
            for acc, j in zip(accs, taps):
                dw_ref[j:j + 1, :] += jnp.sum(acc, axis=0, keepdims=True)
        db_ref[0:1, :] += jnp.sum(duc_ref[...], axis=0, keepdims=True)
        xp[pl.ds(CONV_HALO, seq), :] = duc_ref[...]
        _fill_shifted(xp, sh, seq)

        def ublk(i, carry):
            t0 = pl.multiple_of(i * CONV_ROWS, CONV_ROWS)
            acc = jnp.zeros((CONV_ROWS, LANES), F32)
            for j in range(CONV_WIDTH):
                jj = CONV_WIDTH - j
                acc = acc + sh[jj % 8, pl.ds(t0 + 8 * (jj // 8), CONV_ROWS), :] * w_ref[j:j + 1, :]
            av = a_ref[pl.ds(t0, CONV_ROWS), :]
            sg = _sig(g_ref[pl.ds(t0, CONV_ROWS), :])
            da_ref[pl.ds(t0, CONV_ROWS), :] = (acc * sg).astype(BF16)
            dg_ref[pl.ds(t0, CONV_ROWS), :] = (acc * av * sg * (1.0 - sg)).astype(BF16)
            return carry

        lax.fori_loop(0, seq // CONV_ROWS, ublk, 0)

    return _call(
        body, name="conv_bwd", grid=(n_cb, n_seq),
        in_specs=[pl.BlockSpec((None, seq, LANES), lambda cb, b: (b, 0, cb)),
                  pl.BlockSpec((None, None, seq, LANES), lambda cb, b: (cb, b, 0, 0)),
                  pl.BlockSpec((None, None, seq, LANES), lambda cb, b: (n_cb + cb, b, 0, 0)),
                  pl.BlockSpec((CONV_WIDTH, LANES), lambda cb, b: (0, cb))],
        out_specs=[pl.BlockSpec((None, seq, LANES), lambda cb, b: (b, 0, cb)),
                   pl.BlockSpec((None, seq, LANES), lambda cb, b: (b, 0, cb)),
                   pl.BlockSpec((32, LANES), lambda cb, b: (0, cb)),
                   pl.BlockSpec((8, LANES), lambda cb, b: (0, cb))],
        out_shape=[_sds((n_seq, seq, D_CONV), BF16), _sds((n_seq, seq, D_CONV), BF16),
                   _sds((32, D_CONV), F32), _sds((8, D_CONV), F32)],
        scratch_shapes=[pltpu.VMEM((seq + 2 * CONV_HALO, LANES), F32),
                        pltpu.VMEM((8, seq + 2 * CONV_HALO, LANES), F32)],
        compiler_params=_params(("parallel", "arbitrary")),
    )(duc3, proj3, proj3, w_dw)


MASKED = 1e30
ATT_ROWS = 512
ATT_UNROLL = 8
ATT_FWD_UNROLL = 8


def _distance_mats(dil, seg_len):
    kw = min(2 * Q_BLOCK, seg_len)
    offsets = (0, -RADIUS, -2 * RADIUS) if kw == 2 * Q_BLOCK else (0,)
    a = np.arange(Q_BLOCK)[:, None]
    b = np.arange(kw)[None, :]
    mats = []
    for off in offsets:
        rel = np.abs(b + off - a)
        mats.append(np.where(rel <= RADIUS, dil * rel, MASKED))
    return jnp.asarray(np.stack(mats).astype(np.float32))


def _alibi_rows():
    s = np.zeros((4, 8, LANES), np.float32)
    for hp in range(4):
        for hl in range(2):
            s[hp, hl, :] = 2.0 ** (-(2 * hp + hl + 1))
    return jnp.asarray(s)


def _window(n, seg_len):
    i0 = pl.multiple_of(n * Q_BLOCK, Q_BLOCK)
    if seg_len <= Q_BLOCK:
        return i0, i0, 0
    per_seg = seg_len // Q_BLOCK
    j = n % per_seg
    seg0 = (n // per_seg) * seg_len
    ks_local = jnp.clip(j * Q_BLOCK - RADIUS, 0, seg_len - 2 * Q_BLOCK)
    ks = pl.multiple_of(seg0 + ks_local, RADIUS)
    var = jnp.where(j == 0, 0, jnp.where(j == per_seg - 1, 2, 1))
    return i0, ks, var


def _first_head(rows):
    return lax.broadcasted_iota(jnp.int32, (rows, LANES), 1) < HEAD_DIM


def _same_head():
    head = np.arange(LANES) // HEAD_DIM
    return jnp.asarray((head[:, None] == head[None, :]).astype(np.float32)).astype(BF16)


def _head_sum(x, same_ref):
    hi = x.astype(BF16)
    lo = (x - hi.astype(F32)).astype(BF16)
    return (jnp.dot(hi, same_ref[...], preferred_element_type=F32)
            + jnp.dot(lo, same_ref[...], preferred_element_type=F32))


def _head_mean(x, same_ref):
    return _head_sum(x, same_ref) * (1.0 / HEAD_DIM)


def _per_head(x, first):
    swapped = pltpu.roll(x, HEAD_DIM, 1)
    return jnp.where(first, x, swapped), jnp.where(first, swapped, x)


STRIDE = 4


def _gather_segments(src, dil, seq, tmp, put):
    if dil == 1:
        put(0, seq, src[pl.ds(0, seq), :])
    elif dil == STRIDE:
        seg = seq // dil
        for r in range(dil):
            put(r * seg, seg, src[pl.ds(r, seg, stride=dil), :])
    else:
        part, seg = seq // STRIDE, seq // dil
        for b in range(STRIDE):
            tmp[pl.ds(b * part, part), :] = src[pl.ds(b, part, stride=STRIDE), :]
        for b in range(STRIDE):
            for a in range(dil // STRIDE):
                put(b * part + a * seg, seg, tmp[pl.ds(b * part + a, seg, stride=dil // STRIDE), :])


def _scatter_segments(dst, get, dil, seq, tmp, accumulate):
    def write(rows, val):
        if accumulate:
            dst[rows, :] += val
        else:
            dst[rows, :] = val

    if dil == 1:
        write(pl.ds(0, seq), get(0, seq))
    elif dil == STRIDE:
        seg = seq // dil
        for r in range(dil):
            write(pl.ds(r, seg, stride=dil), get(r * seg, seg))
    else:
        part, seg = seq // STRIDE, seq // dil
        for b in range(STRIDE):
            for a in range(dil // STRIDE):
                tmp[pl.ds(b * part + a, seg, stride=dil // STRIDE), :] = get(b * part + a * seg, seg)
        for b in range(STRIDE):
            write(pl.ds(b, part, stride=STRIDE), tmp[pl.ds(b * part, part), :])


def _permute_rows(dst, src, dil, seq, tmp):
    def put(start, size, val):
        dst[pl.ds(start, size), :] = val.astype(dst.dtype)

    _gather_segments(src, dil, seq, tmp, put)


def _permute_rows_by_head(dst, src, dil, seq, tmp):
    def put(start, size, val):
        first = _first_head(size)
        dst[0, pl.ds(start, size), :] = jnp.where(first, val, 0.0).astype(dst.dtype)
        dst[1, pl.ds(start, size), :] = jnp.where(first, 0.0, val).astype(dst.dtype)

    _gather_segments(src, dil, seq, tmp, put)


def _qk_normalise(q_ref, g2_ref, same_ref, dst, seq, scale):
    def chunk(ci, carry):
        rows = pl.ds(pl.multiple_of(ci * ATT_ROWS, ATT_ROWS), ATT_ROWS)
        qv = q_ref[rows, :]
        r = lax.rsqrt(_head_mean(qv * qv, same_ref) + EPS)
        dst[rows, :] = qv * r * (g2_ref[...] * scale)
        return carry

    lax.fori_loop(0, seq // ATT_ROWS, chunk, 0)


def _attn_fwd(proj3, g_q2, g_k2, ride):
    _, n_seq, seq, _ = proj3.shape
    dms = [_distance_mats(d, seq // d) for d in DILATIONS]
    same = _same_head()
    col0 = 2 * D_CONV // LANES
    n_hp = D_ATT // LANES

    n_ride = len(ride)
    assert not any(scatter for _, scatter in ride), "the forward's ride is an all-gather"

    def body(*refs):
        q_ref, k_ref, v_ref, gq_ref, gk_ref, sl_ref, dm1, dm4, dm16, same_ref = refs[:10]
        ride_in = refs[10:10 + n_ride]
        y_ref, lse_ref = refs[10 + n_ride:12 + n_ride]
        ride_out = refs[12 + n_ride:12 + 2 * n_ride]
        (qf, kf, qp, kp, vp, oml_p, o1, o4, o16, m1, m4, m16, l1, l4, l16,
         tmp) = refs[12 + 2 * n_ride:28 + 2 * n_ride]
        o_nat, m_nat, l_nat = (o1, o4, o16), (m1, m4, m16), (l1, l4, l16)
        ride_args = (ride_in, ride_out) + tuple(refs[28 + 2 * n_ride:])
        step = pl.program_id(0) * n_hp + pl.program_id(1)
        n_steps = n_seq * n_hp

        if n_ride:
            for phase, at in enumerate((0, (3 * n_steps) // 4)):
                @pl.when(step == at)
                def _(phase=phase):
                    _gather_by_chip_phase(phase, *ride_args)

        dm_refs = (dm1, dm4, dm16)
        _qk_normalise(q_ref, gq_ref, same_ref, qf, seq, HEAD_DIM ** -0.5)
        _qk_normalise(k_ref, gk_ref, same_ref, kf, seq, 1.0)
        slopes = (sl_ref[0:1, 0:1], sl_ref[1:2, 0:1])
        for pi, dil in enumerate(DILATIONS):
            seg = seq // dil
            kw = min(2 * Q_BLOCK, seg)
            _permute_rows_by_head(qp, qf, dil, seq, tmp)
            _permute_rows(kp, kf, dil, seq, tmp)
            _permute_rows(vp, v_ref, dil, seq, tmp)

            def blk(it, carry, seg=seg, kw=kw, pi=pi, dst=oml_p):
                first = _first_head(Q_BLOCK)
                chains = [(sub, h) for sub in range(ATT_FWD_UNROLL) for h in range(2)]
                win = [_window(it * ATT_FWD_UNROLL + sub, seg) for sub in range(ATT_FWD_UNROLL)]
                s = {}
                for sub, h in chains:
                    i0, ks, var = win[sub]
                    s[sub, h] = lax.dot_general(qp[h, pl.ds(i0, Q_BLOCK), :], kp[pl.ds(ks, kw), :], NT,
                                                preferred_element_type=F32) - slopes[h] * dm_refs[pi][var]
                m, l, p = {}, {}, {}
                for c in chains:
                    m[c] = jnp.max(s[c], axis=1, keepdims=True)
                    e = jnp.exp(s[c] - m[c])
                    l[c] = jnp.sum(e, axis=1, keepdims=True)
                    p[c] = e.astype(BF16)
                o = {}
                for sub, h in chains:
                    o[sub, h] = jnp.dot(p[sub, h], vp[pl.ds(win[sub][1], kw), :], preferred_element_type=F32)
                packed = [jnp.concatenate([jnp.where(first, t[sub, 0], t[sub, 1]) for t in (o, m, l)], axis=1)
                          for sub in range(ATT_FWD_UNROLL)]
                span = ATT_FWD_UNROLL * Q_BLOCK
                dst[pl.ds(pl.multiple_of(it * span, span), span), :] = jnp.concatenate(packed, axis=0)
                return carry

            lax.fori_loop(0, seq // (Q_BLOCK * ATT_FWD_UNROLL), blk, 0)
            for n, nat in enumerate((o_nat[pi], m_nat[pi], l_nat[pi])):
                _scatter_segments(nat, lambda start, size, n=n: oml_p[pl.ds(start, size), pl.ds(n * LANES, LANES)],
                                  dil, seq, tmp, accumulate=False)

        def merge(ci, carry):
            rows = pl.ds(pl.multiple_of(ci * ATT_ROWS, ATT_ROWS), ATT_ROWS)
            ms = [m_nat[pi][rows, :] for pi in range(3)]
            m_all = jnp.maximum(jnp.maximum(ms[0], ms[1]), ms[2])
            es = [jnp.exp(m - m_all) for m in ms]
            l_all = sum(l_nat[pi][rows, :] * es[pi] for pi in range(3))
            inv = 1.0 / l_all
            o = sum(o_nat[pi][rows, :] * (es[pi] * inv) for pi in range(3))
            y_ref[rows, :] = o.astype(BF16)
            lse_ref[rows, :] = m_all + jnp.log(l_all)
            return carry

        lax.fori_loop(0, seq // ATT_ROWS, merge, 0)

        if n_ride:
            @pl.when(step == n_steps - 1)
            def _():
                _gather_by_chip_phase(2, *ride_args)

    def col(off):
        return pl.BlockSpec((None, None, seq, LANES), lambda b, hp: (col0 + off * n_hp + hp, b, 0, 0))

    def whole(arr):
        return pl.BlockSpec(arr.shape, lambda b, hp: (0,) * arr.ndim)

    rows_f32 = pltpu.VMEM((seq, LANES), F32)
    rows_bf16 = pltpu.VMEM((seq, LANES), BF16)
    return _call(
        body, name="attn_fwd", grid=(n_seq, n_hp),
        in_specs=[col(0), col(1), col(2), whole(g_q2), whole(g_k2),
                  pl.BlockSpec((None, 8, LANES), lambda b, hp: (hp, 0, 0)),
                  whole(dms[0]), whole(dms[1]), whole(dms[2]), whole(same)] + [ANY_SPEC] * n_ride,
        out_specs=[pl.BlockSpec((None, seq, LANES), lambda b, hp: (b, 0, hp)),
                   pl.BlockSpec((None, seq, LANES), lambda b, hp: (b, 0, hp))] + [ANY_SPEC] * n_ride,
        out_shape=[_sds((n_seq, seq, D_ATT), BF16), _sds((n_seq, seq, D_ATT), F32)] + _exchange_shapes(ride),
        scratch_shapes=[rows_f32, rows_f32, pltpu.VMEM((2, seq, LANES), BF16), rows_bf16, rows_bf16]
        + [pltpu.VMEM((seq, 3 * LANES), F32)] + [rows_f32] * 10 + _exchange_sems(n_ride),
        compiler_params=_params(("arbitrary", "arbitrary")),
    )(proj3, proj3, proj3, g_q2, g_k2, _alibi_rows(), *dms, same, *[a for a, _ in ride])


def _attn_bwd(proj3, do3, y_att3, lse3, g_q2, g_k2, ride):
    _, n_seq, seq, _ = proj3.shape
    dms = [_distance_mats(d, seq // d) for d in DILATIONS]
    same = _same_head()
    col0 = 2 * D_CONV // LANES
    n_hp = D_ATT // LANES

    n_ride = len(ride)
    ride_scatter = [s for _, s in ride]

    def body(*refs):
        (q_ref, k_ref, v_ref, do_ref, o_ref, lse_ref, gq_ref, gk_ref, sl_ref, dm1, dm4, dm16,
         same_ref) = refs[:13]
        ride_in = refs[13:13 + n_ride]
        dq_ref, dk_ref, dv_ref, dg_ref = refs[13 + n_ride:17 + n_ride]
        ride_out = refs[17 + n_ride:17 + 2 * n_ride]
        (qf, kf, qp, dop, kp, vp, sn, sp, dqp, dkp, dvp, dqn, dkn, dvn,
         tmp) = refs[17 + 2 * n_ride:32 + 2 * n_ride]
        ride_args = (ride_scatter, ride_in, ride_out) + tuple(refs[32 + 2 * n_ride:])
        dm_refs = (dm1, dm4, dm16)
        step = pl.program_id(0) * n_hp + pl.program_id(1)

        @pl.when(step == 0)
        def _():
            _exchange_start(*ride_args)
            dg_ref[...] = jnp.zeros_like(dg_ref)

        _qk_normalise(q_ref, gq_ref, same_ref, qf, seq, HEAD_DIM ** -0.5)
        _qk_normalise(k_ref, gk_ref, same_ref, kf, seq, 1.0)

        def stats(ci, carry):
            rows = pl.ds(pl.multiple_of(ci * ATT_ROWS, ATT_ROWS), ATT_ROWS)
            first = _first_head(ATT_ROWS)
            sn[0, rows, :], sn[1, rows, :] = _per_head(lse_ref[rows, :], first)
            prod = do_ref[rows, :] * o_ref[rows, :].astype(F32)
            sn[2, rows, :], sn[3, rows, :] = _per_head(_head_sum(prod, same_ref), first)
            return carry

        lax.fori_loop(0, seq // ATT_ROWS, stats, 0)
        slopes = (sl_ref[0:1, 0:1], sl_ref[1:2, 0:1])
        half = seq // (Q_BLOCK * ATT_UNROLL)
        region = seq // ATT_UNROLL

        for pi, dil in enumerate(DILATIONS):
            seg = seq // dil
            kw = min(2 * Q_BLOCK, seg)
            _permute_rows_by_head(qp, qf, dil, seq, tmp)
            _permute_rows_by_head(dop, do_ref, dil, seq, tmp)
            _permute_rows(kp, kf, dil, seq, tmp)
            _permute_rows(vp, v_ref, dil, seq, tmp)
            if dil == 1:
                st = sn
            else:
                st = sp
                for n in range(4):
                    _permute_rows(sp.at[n], sn.at[n], dil, seq, tmp)
            def touched(sub, seg=seg):
                lo, hi = sub * region, (sub + 1) * region
                if seg < region:
                    return lo, hi
                seg0 = lo // seg * seg
                return max(lo - RADIUS, seg0), min(hi + RADIUS, seg0 + seg)

            def summed(acc, start, size, touched=touched):
                pieces = []
                for c0 in range(start, start + size, RADIUS):
                    owners = [s for s in range(ATT_UNROLL) if touched(s)[0] <= c0 and c0 + RADIUS <= touched(s)[1]]
                    if pieces and pieces[-1][2] == owners:
                        pieces[-1][1] += RADIUS
                    else:
                        pieces.append([c0, RADIUS, owners])
                vals = [sum(acc[o, pl.ds(c0, n), :] for o in owners) for c0, n, owners in pieces]
                return vals[0] if len(vals) == 1 else jnp.concatenate(vals, axis=0)

            for sub in range(ATT_UNROLL):
                lo, hi = touched(sub)
                dkp[sub, pl.ds(lo, hi - lo), :] = jnp.zeros((hi - lo, LANES), F32)
                dvp[sub, pl.ds(lo, hi - lo), :] = jnp.zeros((hi - lo, LANES), F32)

            def blk(it, carry, seg=seg, kw=kw, pi=pi, st=st):
                first = _first_head(Q_BLOCK)
                chains = [(sub, h) for sub in range(ATT_UNROLL) for h in range(2)]
                win = [_window(it + sub * half, seg) for sub in range(ATT_UNROLL)]
                qrows = [pl.ds(w[0], Q_BLOCK) for w in win]
                krows = [pl.ds(w[1], kw) for w in win]

                def over_keys(n, sub):
                    t = st[n, qrows[sub], :]
                    return t if kw == LANES else jnp.concatenate([t] * (kw // LANES), axis=1)

                s, dp = {}, {}
                for sub, h in chains:
                    s[sub, h] = lax.dot_general(qp[h, qrows[sub], :], kp[krows[sub], :], NT,
                                                preferred_element_type=F32) - slopes[h] * dm_refs[pi][win[sub][2]]
                    dp[sub, h] = lax.dot_general(dop[h, qrows[sub], :], vp[krows[sub], :], NT,
                                                 preferred_element_type=F32)
                p, ds = {}, {}
                for sub, h in chains:
                    e = jnp.exp(s[sub, h] - over_keys(h, sub))
                    ds[sub, h] = (e * (dp[sub, h] - over_keys(2 + h, sub))).astype(BF16)
                    p[sub, h] = e.astype(BF16)
                dq, dk, dv = {}, {}, {}
                for sub, h in chains:
                    dq[sub, h] = jnp.dot(ds[sub, h], kp[krows[sub], :], preferred_element_type=F32)
                    dk[sub, h] = lax.dot_general(ds[sub, h], qp[h, qrows[sub], :], TN, preferred_element_type=F32)
                    dv[sub, h] = lax.dot_general(p[sub, h], dop[h, qrows[sub], :], TN, preferred_element_type=F32)
                for sub in range(ATT_UNROLL):
                    dqp[qrows[sub], :] = jnp.where(first, dq[sub, 0], dq[sub, 1])
                    dkp[sub, krows[sub], :] += dk[sub, 0] + dk[sub, 1]
                    dvp[sub, krows[sub], :] += dv[sub, 0] + dv[sub, 1]
                return carry

            lax.fori_loop(0, half, blk, 0)
            first_pattern = pi == 0
            if first_pattern:
                for r0 in range(0, seq, region):
                    rows = pl.ds(r0, region)
                    dqn[rows, :] = dqp[rows, :]
                    dkn[rows, :] = summed(dkp, r0, region)
                    dvn[rows, :] = summed(dvp, r0, region)
            else:
                _scatter_segments(dqn, lambda start, size: dqp[pl.ds(start, size), :], dil, seq, tmp, accumulate=True)
                for nat, acc in ((dkn, dkp), (dvn, dvp)):
                    _scatter_segments(nat, lambda start, size, acc=acc: summed(acc, start, size),
                                      dil, seq, tmp, accumulate=True)

        def finish(ci, carry):
            rows = pl.ds(pl.multiple_of(ci * ATT_ROWS, ATT_ROWS), ATT_ROWS)
            for src_ref, g_ref, dn, dst_ref, scale, row in (
                    (q_ref, gq_ref, dqn, dq_ref, HEAD_DIM ** -0.5, 0), (k_ref, gk_ref, dkn, dk_ref, 1.0, 1)):
                xv = src_ref[rows, :]
                r = lax.rsqrt(_head_mean(xv * xv, same_ref) + EPS)
                xhat = xv * r
                d = dn[rows, :] * scale
                dg_ref[row:row + 1, :] += jnp.sum(d * xhat, axis=0, keepdims=True)
                dxh = d * g_ref[...]
                dst_ref[rows, :] = (r * (dxh - xhat * _head_mean(dxh * xhat, same_ref))).astype(BF16)
            dv_ref[rows, :] = dvn[rows, :].astype(BF16)
            return carry

        lax.fori_loop(0, seq // ATT_ROWS, finish, 0)

        @pl.when(step == n_seq * n_hp - 1)
        def _():
            _exchange_wait(*ride_args)

    def col(off):
        return pl.BlockSpec((None, None, seq, LANES), lambda b, hp: (col0 + off * n_hp + hp, b, 0, 0))

    def whole(arr):
        return pl.BlockSpec(arr.shape, lambda b, hp: (0,) * arr.ndim)

    att = pl.BlockSpec((None, seq, LANES), lambda b, hp: (b, 0, hp))
    rows_f32 = pltpu.VMEM((seq, LANES), F32)
    rows_bf16 = pltpu.VMEM((seq, LANES), BF16)
    by_head_bf16 = pltpu.VMEM((2, seq, LANES), BF16)
    per_sub_f32 = pltpu.VMEM((ATT_UNROLL, seq, LANES), F32)
    stats_f32 = pltpu.VMEM((4, seq, LANES), F32)
    return _call(
        body, name="attn_bwd", grid=(n_seq, n_hp),
        in_specs=[col(0), col(1), col(2), att, att, att, whole(g_q2), whole(g_k2),
                  pl.BlockSpec((None, 8, LANES), lambda b, hp: (hp, 0, 0)),
                  whole(dms[0]), whole(dms[1]), whole(dms[2]), whole(same)] + [ANY_SPEC] * n_ride,
        out_specs=[att, att, att, pl.BlockSpec((8, LANES), lambda b, hp: (0, 0))] + [ANY_SPEC] * n_ride,
        out_shape=[_sds((n_seq, seq, D_ATT), BF16)] * 3 + [_sds((8, LANES), F32)] + _exchange_shapes(ride),
        scratch_shapes=[rows_f32, rows_f32, by_head_bf16, by_head_bf16, rows_bf16, rows_bf16, stats_f32, stats_f32,
                        rows_f32, per_sub_f32, per_sub_f32, rows_f32, rows_f32, rows_f32, rows_f32]
        + _exchange_sems(n_ride),
        compiler_params=_params(("arbitrary", "arbitrary")),
    )(proj3, proj3, proj3, do3, y_att3, lse3, g_q2, g_k2, _alibi_rows(), *dms, same, *[a for a, _ in ride])


def _mix_out(uc2, y_att2, x2, mod8, g_ln, b_ln, g_ffn, w_out, seq, tm=512):
    tokens = x2.shape[0]
    per_seq = seq // tm

    def body(uc_ref, ya_ref, x_ref, m_ref, gl_ref, bl_ref, gf_ref, w_ref, yc_ref, mix_ref, x1_ref, h2_ref):
        uc = uc_ref[...]
        mu = jnp.mean(uc, axis=-1, keepdims=True)
        cen = uc - mu
        rs = lax.rsqrt(jnp.mean(cen * cen, axis=-1, keepdims=True) + EPS)
        z = cen * rs * gl_ref[...] + bl_ref[...]
        yc = (z * _sig(z)).astype(BF16)
        yc_ref[...] = yc
        mix = (jnp.dot(yc, w_ref[pl.ds(0, D_CONV), :], preferred_element_type=F32)
               + jnp.dot(ya_ref[...], w_ref[pl.ds(D_CONV, D_ATT), :], preferred_element_type=F32))
        mix_ref[...] = mix
        x1 = x_ref[...] + m_ref[2:3, :] * mix
        x1_ref[...] = x1
        r = lax.rsqrt(jnp.mean(x1 * x1, axis=-1, keepdims=True) + EPS)
        h2_ref[...] = ((x1 * r * gf_ref[...]) * (1.0 + m_ref[4:5, :]) + m_ref[3:4, :]).astype(BF16)

    def rows(width):
        return pl.BlockSpec((tm, width), lambda i: (i, 0))

    def vec(width):
        return pl.BlockSpec((1, width), lambda i: (0, 0))

    return _call(
        body, name="mix_out", grid=(tokens // tm,),
        in_specs=[rows(D_CONV), rows(D_ATT), rows(D_MODEL),
                  pl.BlockSpec((None, 8, D_MODEL), lambda i: (i // per_seq, 0, 0)),
                  vec(D_CONV), vec(D_CONV), vec(D_MODEL),
                  pl.BlockSpec((D_MODEL, D_MODEL), lambda i: (0, 0))],
        out_specs=[rows(D_CONV), rows(D_MODEL), rows(D_MODEL), rows(D_MODEL)],
        out_shape=[_sds((tokens, D_CONV), BF16), _sds((tokens, D_MODEL), F32),
                   _sds((tokens, D_MODEL), F32), _sds((tokens, D_MODEL), BF16)],
        compiler_params=_params(("parallel",)),
    )(uc2, y_att2, x2, mod8, g_ln, b_ln, g_ffn, w_out)


def _mix_out_bwd(dmix, uc2, g_ln, b_ln, w_out, tm=512):
    tokens = dmix.shape[0]

    def body(dm_ref, uc_ref, gl_ref, bl_ref, w_ref, duc_ref, do_ref, dgb_ref):
        @pl.when(pl.program_id(0) == 0)
        def _():
            dgb_ref[...] = jnp.zeros_like(dgb_ref)

        dmv = dm_ref[...]
        dyc = lax.dot_general(dmv, w_ref[pl.ds(0, D_CONV), :], NT, preferred_element_type=F32)
        do_ref[...] = lax.dot_general(dmv, w_ref[pl.ds(D_CONV, D_ATT), :], NT, preferred_element_type=F32)
        uc = uc_ref[...]
        mu = jnp.mean(uc, axis=-1, keepdims=True)
        cen = uc - mu
        rs = lax.rsqrt(jnp.mean(cen * cen, axis=-1, keepdims=True) + EPS)
        xh = cen * rs
        z = xh * gl_ref[...] + bl_ref[...]
        sg = _sig(z)
        dz = dyc * (sg * (1.0 + z * (1.0 - sg)))
        dgb_ref[0:1, :] += jnp.sum(dz * xh, axis=0, keepdims=True)
        dgb_ref[1:2, :] += jnp.sum(dz, axis=0, keepdims=True)
        dxh = dz * gl_ref[...]
        duc_ref[...] = rs * (dxh - jnp.mean(dxh, axis=-1, keepdims=True)
                             - xh * jnp.mean(dxh * xh, axis=-1, keepdims=True))

    return _call(
        body, name="mix_out_bwd", grid=(tokens // tm,),
        in_specs=[pl.BlockSpec((tm, D_MODEL), lambda i: (i, 0)),
                  pl.BlockSpec((tm, D_CONV), lambda i: (i, 0)),
                  pl.BlockSpec((1, D_CONV), lambda i: (0, 0)),
                  pl.BlockSpec((1, D_CONV), lambda i: (0, 0)),
                  pl.BlockSpec((D_MODEL, D_MODEL), lambda i: (0, 0))],
        out_specs=[pl.BlockSpec((tm, D_CONV), lambda i: (i, 0)),
                   pl.BlockSpec((tm, D_ATT), lambda i: (i, 0)),
                   pl.BlockSpec((8, D_CONV), lambda i: (0, 0))],
        out_shape=[_sds((tokens, D_CONV), F32), _sds((tokens, D_ATT), F32), _sds((8, D_CONV), F32)],
        compiler_params=_params(("arbitrary",)),
    )(dmix, uc2, g_ln, b_ln, w_out)


FF_TILE = 256
FF_ROWS = 256


def _ffn_fwd(h2, w_gate, w_up, w_down, x1, target, mod8, seq, tm=1024):
    tokens = h2.shape[0]
    per_seq = seq // tm
    n_seq = tokens // seq
    last = D_FF // FF_TILE - 1

    def body(h_ref, wg_ref, wu_ref, wd_ref, x1_ref, t_ref, m_ref, gate_ref, up_ref, dy_ref, df_ref, sq_ref, dgf_ref,
             f_ref):
        i, j = pl.program_id(0), pl.program_id(1)

        @pl.when(j == 0)
        def _():
            f_ref[...] = jnp.zeros_like(f_ref)

        @pl.when((j == 0) & (i == 0))
        def _():
            sq_ref[...] = jnp.zeros_like(sq_ref)

        @pl.when((j == 0) & (i % per_seq == 0))
        def _():
            dgf_ref[...] = jnp.zeros_like(dgf_ref)

        def gate_up(r):
            hv = h_ref[pl.ds(r * FF_ROWS, FF_ROWS), :]
            return (jnp.dot(hv, wg_ref[...], preferred_element_type=F32),
                    jnp.dot(hv, wu_ref[...], preferred_element_type=F32))

        ahead = gate_up(0)
        for r in range(tm // FF_ROWS):
            gate, up = ahead
            if r + 1 < tm // FF_ROWS:
                ahead = gate_up(r + 1)
            rows = pl.ds(r * FF_ROWS, FF_ROWS)
            gate_ref[rows, :] = gate.astype(BF16)
            up_ref[rows, :] = up.astype(BF16)
            act = (gate * _sig(gate) * up).astype(BF16)
            f_ref[rows, :] += jnp.dot(act, wd_ref[...], preferred_element_type=F32)

        @pl.when(j == last)
        def _():
            gate_f = m_ref[5:6, :]
            for r in range(tm // FF_ROWS):
                rows = pl.ds(r * FF_ROWS, FF_ROWS)
                fv = f_ref[rows, :]
                diff = x1_ref[rows, :] + gate_f * fv - t_ref[rows, :]
                sq_ref[0:1, :] += jnp.sum(diff * diff, axis=0, keepdims=True)
                dy = diff * (1.0 / D_MODEL)
                dy_ref[rows, :] = dy
                df_ref[rows, :] = (gate_f * dy).astype(BF16)
                dgf_ref[0:1, :] += jnp.sum(dy * fv, axis=0, keepdims=True)

    rows_spec = pl.BlockSpec((tm, D_MODEL), lambda i, j: (i, 0))
    per = pl.BlockSpec((None, 8, D_MODEL), lambda i, j: (i // per_seq, 0, 0))
    tile = pl.BlockSpec((tm, FF_TILE), lambda i, j: (i, j))
    return _call(
        body, name="ffn_fwd", grid=(tokens // tm, D_FF // FF_TILE),
        in_specs=[rows_spec,
                  pl.BlockSpec((D_MODEL, FF_TILE), lambda i, j: (0, j)),
                  pl.BlockSpec((D_MODEL, FF_TILE), lambda i, j: (0, j)),
                  pl.BlockSpec((FF_TILE, D_MODEL), lambda i, j: (j, 0)),
                  rows_spec, rows_spec, per],
        out_specs=[tile, tile, rows_spec, rows_spec, pl.BlockSpec((8, D_MODEL), lambda i, j: (0, 0)), per],
        out_shape=[_sds((tokens, D_FF), BF16), _sds((tokens, D_FF), BF16), _sds((tokens, D_MODEL), F32),
                   _sds((tokens, D_MODEL), BF16), _sds((8, D_MODEL), F32), _sds((n_seq, 8, D_MODEL), F32)],
        scratch_shapes=[pltpu.VMEM((tm, D_MODEL), F32)],
        compiler_params=_params(("arbitrary", "arbitrary")),
    )(h2, w_gate, w_up, w_down, x1, target, mod8)


def _ffn_bwd(df, gate, up, w_gate, w_up, w_down, x1, dy, mix, mod8, g_ffn, seq, tm=1024):
    tokens = df.shape[0]
    per_seq = seq // tm
    n_seq = tokens // seq
    last = D_FF // FF_TILE - 1

    def body(df_ref, gate_ref, up_ref, wg_ref, wu_ref, wd_ref, m_ref, g_ref, x1_hbm, dy_hbm, mix_hbm,
             dgate_ref, dup_ref, act_ref, dx1_ref, dmix_ref, dg_ref, dm_ref, dh_ref, late, late_sems):
        i, j = pl.program_id(0), pl.program_id(1)
        my_rows = pl.ds(pl.multiple_of(i * tm, tm), tm)
        fetches = [pltpu.make_async_copy(src.at[my_rows, :], late.at[n], late_sems.at[n])
                   for n, src in enumerate((x1_hbm, dy_hbm, mix_hbm))]

        @pl.when(j == 0)
        def _():
            dh_ref[...] = jnp.zeros_like(dh_ref)
            for cp in fetches:
                cp.start()

        @pl.when((j == 0) & (i == 0))
        def _():
            dg_ref[...] = jnp.zeros_like(dg_ref)

        @pl.when((j == 0) & (i % per_seq == 0))
        def _():
            dm_ref[...] = jnp.zeros_like(dm_ref)

        def d_act(r):
            return lax.dot_general(df_ref[pl.ds(r * FF_ROWS, FF_ROWS), :], wd_ref[...], NT,
                                   preferred_element_type=F32)

        ahead = d_act(0)
        for r in range(tm // FF_ROWS):
            dact = ahead
            if r + 1 < tm // FF_ROWS:
                ahead = d_act(r + 1)
            rows = pl.ds(r * FF_ROWS, FF_ROWS)
            gate = gate_ref[rows, :].astype(F32)
            up = up_ref[rows, :].astype(F32)
            sg = _sig(gate)
            silu = gate * sg
            act_ref[rows, :] = (silu * up).astype(BF16)
            dup = (dact * silu).astype(BF16)
            dgate = (dact * up * (sg * (1.0 + gate * (1.0 - sg)))).astype(BF16)
            dup_ref[rows, :] = dup
            dgate_ref[rows, :] = dgate
            dh_ref[rows, :] += (lax.dot_general(dgate, wg_ref[...], NT, preferred_element_type=F32)
                                + lax.dot_general(dup, wu_ref[...], NT, preferred_element_type=F32))

        @pl.when(j == last)
        def _():
            for cp in fetches:
                cp.wait()
            g = g_ref[...]
            for r in range(tm // FF_ROWS):
                rows = pl.ds(r * FF_ROWS, FF_ROWS)
                dh = dh_ref[rows, :]
                x1v = late[0, rows, :]
                rs = lax.rsqrt(jnp.mean(x1v * x1v, axis=-1, keepdims=True) + EPS)
                xhat = x1v * rs
                dm_ref[0:1, :] += jnp.sum(dh, axis=0, keepdims=True)
                dm_ref[1:2, :] += jnp.sum(dh * (xhat * g), axis=0, keepdims=True)
                dn = dh * (1.0 + m_ref[4:5, :])
                dg_ref[0:1, :] += jnp.sum(dn * xhat, axis=0, keepdims=True)
                dxh = dn * g
                dx1 = late[1, rows, :] + rs * (dxh - xhat * jnp.mean(dxh * xhat, axis=-1, keepdims=True))
                dx1_ref[rows, :] = dx1
                dm_ref[2:3, :] += jnp.sum(dx1 * late[2, rows, :], axis=0, keepdims=True)
                dmix_ref[rows, :] = (m_ref[2:3, :] * dx1).astype(BF16)

    tile = pl.BlockSpec((tm, FF_TILE), lambda i, j: (i, j))
    rows_spec = pl.BlockSpec((tm, D_MODEL), lambda i, j: (i, 0))
    per = pl.BlockSpec((None, 8, D_MODEL), lambda i, j: (i // per_seq, 0, 0))
    return _call(
        body, name="ffn_bwd", grid=(tokens // tm, D_FF // FF_TILE),
        in_specs=[rows_spec, tile, tile,
                  pl.BlockSpec((D_MODEL, FF_TILE), lambda i, j: (0, j)),
                  pl.BlockSpec((D_MODEL, FF_TILE), lambda i, j: (0, j)),
                  pl.BlockSpec((FF_TILE, D_MODEL), lambda i, j: (j, 0)),
                  per, pl.BlockSpec((1, D_MODEL), lambda i, j: (0, 0)), ANY_SPEC, ANY_SPEC, ANY_SPEC],
        out_specs=[tile, tile, tile, rows_spec, rows_spec, pl.BlockSpec((8, D_MODEL), lambda i, j: (0, 0)), per],
        out_shape=[_sds((tokens, D_FF), BF16)] * 3 + [_sds((tokens, D_MODEL), F32), _sds((tokens, D_MODEL), BF16),
                                                    _sds((8, D_MODEL), F32), _sds((n_seq, 8, D_MODEL), F32)],
        scratch_shapes=[pltpu.VMEM((tm, D_MODEL), F32), pltpu.VMEM((3, tm, D_MODEL), F32),
                        pltpu.SemaphoreType.DMA((3,))],
        compiler_params=_params(("arbitrary", "arbitrary")),
    )(df, gate, up, w_gate, w_up, w_down, mod8, g_ffn, x1, dy, mix)


def _mix_in_bwd(d_a, d_g, d_q, d_k, d_v, w_in, x2, dx1, mod8, g_mix, seq, ride, tm=512):
    tokens = x2.shape[0]
    per_seq = seq // tm
    n_seq = tokens // seq
    parts = (d_a, d_g, d_q, d_k, d_v)
    width = D_CONV
    n_ride = len(ride)
    ride_scatter = [s for _, s in ride]

    def body(*refs):
        da_ref, dg_ref, dq_ref, dk_ref, dv_ref, w_ref, x_ref, dx1_ref, m_ref, g_ref = refs[:10]
        ride_in = refs[10:10 + n_ride]
        gx_ref, dgm_ref, dm_ref = refs[10 + n_ride:13 + n_ride]
        ride_args = (ride_scatter, ride_in, refs[13 + n_ride:13 + 2 * n_ride]) + tuple(refs[13 + 2 * n_ride:])
        i = pl.program_id(0)

        @pl.when(i == 0)
        def _():
            _exchange_start(*ride_args)
            dgm_ref[...] = jnp.zeros_like(dgm_ref)

        @pl.when(i % per_seq == 0)
        def _():
            dm_ref[...] = jnp.zeros_like(dm_ref)

        dh = jnp.zeros((tm, D_MODEL), F32)
        for n, ref in enumerate((da_ref, dg_ref, dq_ref, dk_ref, dv_ref)):
            dh = dh + lax.dot_general(ref[...], w_ref[:, pl.ds(n * width, width)], NT, preferred_element_type=F32)
        xv = x_ref[...]
        r = lax.rsqrt(jnp.mean(xv * xv, axis=-1, keepdims=True) + EPS)
        xhat = xv * r
        g = g_ref[...]
        dm_ref[0:1, :] += jnp.sum(dh, axis=0, keepdims=True)
        dm_ref[1:2, :] += jnp.sum(dh * (xhat * g), axis=0, keepdims=True)
        dn = dh * (1.0 + m_ref[1:2, :])
        dgm_ref[0:1, :] += jnp.sum(dn * xhat, axis=0, keepdims=True)
        dxh = dn * g
        gx_ref[...] = dx1_ref[...] + r * (dxh - xhat * jnp.mean(dxh * xhat, axis=-1, keepdims=True))

        @pl.when(i == tokens // tm - 1)
        def _():
            _exchange_wait(*ride_args)

    rows = pl.BlockSpec((tm, D_MODEL), lambda i: (i, 0))
    half = pl.BlockSpec((tm, width), lambda i: (i, 0))
    per = pl.BlockSpec((None, 8, D_MODEL), lambda i: (i // per_seq, 0, 0))
    return _call(
        body, name="mix_in_bwd", grid=(tokens // tm,),
        in_specs=[half] * 5 + [pl.BlockSpec((D_MODEL, D_IN), lambda i: (0, 0)), rows, rows, per,
                               pl.BlockSpec((1, D_MODEL), lambda i: (0, 0))] + [ANY_SPEC] * n_ride,
        out_specs=[rows, pl.BlockSpec((8, D_MODEL), lambda i: (0, 0)), per] + [ANY_SPEC] * n_ride,
        out_shape=[_sds((tokens, D_MODEL), F32), _sds((8, D_MODEL), F32), _sds((n_seq, 8, D_MODEL), F32)]
        + _exchange_shapes(ride),
        scratch_shapes=_exchange_sems(n_ride),
        compiler_params=_params(("arbitrary",)),
    )(*parts, w_in, x2, dx1, mod8, g_mix, *[a for a, _ in ride])


def _grad_matmul_parts(a_parts, b_parts, name, tk=1024):
    tokens = a_parts[0].shape[0]
    na, nb = len(a_parts), len(b_parts)
    ma, nbw = a_parts[0].shape[1], b_parts[0].shape[1]

    n_k = tokens // tk

    def body(*refs):
        a_refs, b_refs, o_ref, acc = refs[:na], refs[na:na + nb], refs[na + nb], refs[na + nb + 1]

        @pl.when(pl.program_id(0) == 0)
        def _():
            acc[...] = jnp.zeros_like(acc)

        for i in range(na):
            for j in range(nb):
                acc[pl.ds(i * ma, ma), pl.ds(j * nbw, nbw)] += lax.dot_general(
                    a_refs[i][...], b_refs[j][...], TN, preferred_element_type=F32)

        @pl.when(pl.program_id(0) == n_k - 1)
        def _():
            o_ref[...] = acc[...].astype(o_ref.dtype)

    return _call(
        body, name=name, grid=(n_k,),
        in_specs=[pl.BlockSpec((tk, ma), lambda k: (k, 0))] * na + [pl.BlockSpec((tk, nbw), lambda k: (k, 0))] * nb,
        out_specs=pl.BlockSpec((na * ma, nb * nbw), lambda k: (0, 0)),
        out_shape=_sds((na * ma, nb * nbw), BF16),
        scratch_shapes=[pltpu.VMEM((na * ma, nb * nbw), F32)],
        compiler_params=_params(("arbitrary",)),
    )(*a_parts, *b_parts)


def _grad_matmul(a, b, name, tmo, tno, tk=1024):
    tokens, m = a.shape
    n = b.shape[1]
    n_k = tokens // tk

    def body(a_ref, b_ref, o_ref, acc):
        @pl.when(pl.program_id(2) == 0)
        def _():
            acc[...] = jnp.zeros_like(acc)

        acc[...] += lax.dot_general(a_ref[...], b_ref[...], TN, preferred_element_type=F32)

        @pl.when(pl.program_id(2) == n_k - 1)
        def _():
            o_ref[...] = acc[...].astype(o_ref.dtype)

    return _call(
        body, name=name, grid=(m // tmo, n // tno, n_k),
        in_specs=[pl.BlockSpec((tk, tmo), lambda i, j, k: (k, i)),
                  pl.BlockSpec((tk, tno), lambda i, j, k: (k, j))],
        out_specs=pl.BlockSpec((tmo, tno), lambda i, j, k: (i, j)),
        out_shape=_sds((m, n), BF16),
        scratch_shapes=[pltpu.VMEM((tmo, tno), F32)],
        compiler_params=_params(("parallel", "parallel", "arbitrary")),
    )(a, b)


def _adamw(w, m, v, g, name, n_parts=0, tr=256):
    rows, cols = w.shape
    tr = min(tr, rows)
    c1 = 1.0 - ADAM_B1 ** ADAM_STEP
    c2 = 1.0 - ADAM_B2 ** ADAM_STEP

    def body(w_ref, m_ref, v_ref, g_ref, go_ref, d_ref, mo_ref, vo_ref):
        if n_parts:
            gv = g_ref[0].astype(F32)
            for p in range(1, n_parts):
                gv = gv + g_ref[p].astype(F32)
        else:
            gv = g_ref[...]
        go_ref[...] = gv
        mn = ADAM_B1 * m_ref[...] + (1.0 - ADAM_B1) * gv
        vn = ADAM_B2 * v_ref[...] + (1.0 - ADAM_B2) * (gv * gv)
        mo_ref[...] = mn
        vo_ref[...] = vn
        d_ref[...] = -ADAM_LR * ((mn / c1) / (jnp.sqrt(vn / c2) + ADAM_EPS) + ADAM_WD * w_ref[...])

    blk = pl.BlockSpec((tr, cols), lambda i: (i, 0))
    g_spec = pl.BlockSpec((n_parts, tr, cols), lambda i: (0, i, 0)) if n_parts else blk
    return _call(
        body, name=name, grid=(rows // tr,),
        in_specs=[blk, blk, blk, g_spec], out_specs=[blk] * 4,
        out_shape=[_sds((rows, cols), F32)] * 4,
        compiler_params=_params(("parallel",)),
    )(w, m, v, g)


def _cols_to_full(blocks):
    n, r, c = blocks.shape
    return jnp.transpose(blocks, (1, 0, 2)).reshape(r, n * c)


def _full_to_cols(full, n=N_DEV):
    r, c = full.shape
    return jnp.transpose(full.reshape(r, n, c // n), (1, 0, 2))


def _pad_lanes(v, width):
    return jnp.pad(v, ((0, 0), (0, width - v.shape[1])))


def kernel(x, c, w_ada, b_ada, g_mix, w_in, w_dw, b_dw, g_conv_ln, b_conv_ln, g_q, g_k, w_out, g_ffn, w_gate, w_up, w_down, loss_target, m_w_ada, m_b_ada, m_g_mix, m_w_in, m_w_dw, m_b_dw, m_g_conv_ln, m_b_conv_ln, m_g_q, m_g_k, m_w_out, m_g_ffn, m_w_gate, m_w_up, m_w_down, v_w_ada, v_b_ada, v_g_mix, v_w_in, v_w_dw, v_b_dw, v_g_conv_ln, v_b_conv_ln, v_g_q, v_g_k, v_w_out, v_g_ffn, v_w_gate, v_w_up, v_w_down):
    n_seq, seq, _ = x.shape
    tokens = n_seq * seq
    me = 4 * lax.axis_index("x") + 2 * lax.axis_index("y") + lax.axis_index("c")
    ada_cols = w_ada.shape[2]
    dw_cols = w_dw.shape[2]

    (c_g, w_in_g, w_dw_g) = _gather_by_chip([c, w_in[0].astype(BF16), w_dw[0]], "gather_weights")
    c_all = c_g.reshape(N_DEV * n_seq, D_MODEL)
    w_in_f = _cols_to_full(w_in_g)
    w_dw_f = _cols_to_full(w_dw_g)

    b_cols = lax.dynamic_slice(b_ada, (0, me * ada_cols), (1, ada_cols))
    mod_cols = _ada_fwd(c_all, w_ada[0], b_cols)
    (mod_g,) = _exchange([(mod_cols, False)], "gather_mod")
    mod_mine = lax.dynamic_slice(mod_g, (0, me * n_seq, 0), (N_DEV, n_seq, ada_cols))
    mod = jnp.transpose(mod_mine, (1, 0, 2)).reshape(n_seq, N_MOD, D_MODEL)
    mod8 = jnp.pad(mod, ((0, 0), (0, 8 - N_MOD), (0, 0)))

    x2 = x.reshape(tokens, D_MODEL)
    h1, proj = _mix_in(x2, mod8, g_mix, w_in_f, seq)
    proj3 = proj.reshape(D_IN // LANES, n_seq, seq, LANES)
    uc3 = _conv_fwd(proj3, w_dw_f, b_dw)
    g_q2, g_k2 = jnp.tile(g_q, (1, 2)), jnp.tile(g_k, (1, 2))
    y_att3, lse3, w_out_g, w_gate_g, w_up_g, w_down_g = _attn_fwd(
        proj3, g_q2, g_k2,
        [(w_out[0].astype(BF16), False), (w_gate[0].astype(BF16), False), (w_up[0].astype(BF16), False),
         (w_down[0].astype(BF16), False)])
    w_out_f = w_out_g.reshape(D_MODEL, D_MODEL)
    w_gate_f = _cols_to_full(w_gate_g)
    w_up_f = _cols_to_full(w_up_g)
    w_down_f = w_down_g.reshape(D_FF, D_MODEL)
    uc2 = uc3.reshape(tokens, D_CONV)
    y_att2 = y_att3.reshape(tokens, D_ATT)
    y_conv, mix, x1, h2 = _mix_out(uc2, y_att2, x2, mod8, g_conv_ln, b_conv_ln, g_ffn, w_out_f, seq)
    gate, up, dy, df, sq, dgate_f = _ffn_fwd(
        h2, w_gate_f, w_up_f, w_down_f, x1, loss_target.reshape(tokens, D_MODEL), mod8, seq)

    dgate, dup, act, dx1, dmix, dg_ffn, dmod_f = _ffn_bwd(
        df, gate, up, w_gate_f, w_up_f, w_down_f, x1, dy, mix, mod8, g_ffn, seq)
    duc2, do2, dgb_ln = _mix_out_bwd(dmix, uc2, g_conv_ln, b_conv_ln, w_out_f)
    d_a3, d_g3, dw_dw_p, db_dw_p = _conv_bwd(duc2.reshape(n_seq, seq, D_CONV), proj3, w_dw_f)
    gw_gate = _grad_matmul(h2, dgate, "grad_w_gate", D_MODEL, D_FF // 2)
    gw_up = _grad_matmul(h2, dup, "grad_w_up", D_MODEL, D_FF // 2)
    gw_down = _grad_matmul(act, df, "grad_w_down", D_FF // 2, D_MODEL)
    gw_out = _grad_matmul_parts([y_conv, y_att2], [dmix], "grad_w_out")
    d_q3, d_k3, d_v3, dg_qk, p_gate, p_up, p_down, p_out = _attn_bwd(
        proj3, do2.reshape(n_seq, seq, D_ATT), y_att3, lse3, g_q2, g_k2,
        [(_full_to_cols(gw_gate).astype(BF16), True), (_full_to_cols(gw_up).astype(BF16), True),
         (gw_down.astype(BF16).reshape(N_DEV, D_FF // N_DEV, D_MODEL), True),
         (gw_out.astype(BF16).reshape(N_DEV, D_MODEL // N_DEV, D_MODEL), True)])
    flat = lambda t: t.reshape(tokens, t.shape[-1])
    d_a, d_g, d_q, d_k, d_v = flat(d_a3), flat(d_g3), flat(d_q3), flat(d_k3), flat(d_v3)
    gw_in = _grad_matmul_parts([h1], [d_a, d_g, d_q, d_k, d_v], "grad_w_in")
    grad_x2, dg_mix, dmod_m, p_in = _mix_in_bwd(
        d_a, d_g, d_q, d_k, d_v, w_in_f, x2, dx1, mod8, g_mix, seq, [(_full_to_cols(gw_in).astype(BF16), True)])

    dmod = jnp.concatenate([dmod_m[:, 0], dmod_m[:, 1], dmod_f[:, 2], dmod_f[:, 0], dmod_f[:, 1], dgate_f[:, 0]], axis=1)
    dg_q = dg_qk[0:1, 0:HEAD_DIM] + dg_qk[0:1, HEAD_DIM:]
    dg_k = dg_qk[1:2, 0:HEAD_DIM] + dg_qk[1:2, HEAD_DIM:]
    loss_part = (0.5 / D_MODEL) * jnp.sum(sq[0:1, :], axis=1, keepdims=True)
    small = jnp.concatenate(
        [dg_mix[0:1], dg_ffn[0:1], db_dw_p[0:1], dgb_ln[0:1], dgb_ln[1:2],
         _pad_lanes(dg_q, LANES), _pad_lanes(dg_k, LANES), _pad_lanes(loss_part, LANES)], axis=1)
    n_small = small.shape[1] - LANES

    (dmod_g, small_g, dw_g) = _exchange([(dmod, False), (small, False), (dw_dw_p, False)], "gather_small_grads")

    dmod_all = dmod_g.reshape(N_DEV * n_seq, N_MOD * D_MODEL)
    dmod_cols = lax.dynamic_slice(dmod_all, (0, me * ada_cols), (N_DEV * n_seq, ada_cols))
    gw_ada, gb_ada = _ada_bwd(c_all, dmod_cols, dmod_all)

    res = {}
    res["w_ada"] = _adamw(w_ada[0], m_w_ada[0], v_w_ada[0], gw_ada, "adamw_w_ada")
    res["b_ada"] = _adamw(b_ada, m_b_ada, v_b_ada, gb_ada, "adamw_b_ada")
    res["w_in"] = _adamw(w_in[0], m_w_in[0], v_w_in[0], p_in, "adamw_w_in", N_DEV)
    res["w_out"] = _adamw(w_out[0], m_w_out[0], v_w_out[0], p_out, "adamw_w_out", N_DEV)
    res["w_gate"] = _adamw(w_gate[0], m_w_gate[0], v_w_gate[0], p_gate, "adamw_w_gate", N_DEV)
    res["w_up"] = _adamw(w_up[0], m_w_up[0], v_w_up[0], p_up, "adamw_w_up", N_DEV)
    res["w_down"] = _adamw(w_down[0], m_w_down[0], v_w_down[0], p_down, "adamw_w_down", N_DEV, tr=176)
    dw_mine = lax.dynamic_slice(dw_g, (0, 0, me * dw_cols), (N_DEV, CONV_WIDTH, dw_cols))
    res["w_dw"] = _adamw(w_dw[0], m_w_dw[0], v_w_dw[0], dw_mine, "adamw_w_dw", N_DEV)

    small_names = ["g_mix", "g_ffn", "b_dw", "g_conv_ln", "b_conv_ln", "g_q", "g_k"]
    small_w = {"g_mix": (g_mix, m_g_mix, v_g_mix), "g_ffn": (g_ffn, m_g_ffn, v_g_ffn), "b_dw": (b_dw, m_b_dw, v_b_dw),
               "g_conv_ln": (g_conv_ln, m_g_conv_ln, v_g_conv_ln), "b_conv_ln": (b_conv_ln, m_b_conv_ln, v_b_conv_ln),
               "g_q": (g_q, m_g_q, v_g_q), "g_k": (g_k, m_g_k, v_g_k)}
    widths = [max(small_w[n][0].shape[1], LANES) for n in small_names]
    packed = [jnp.concatenate([_pad_lanes(small_w[n][i], wd) for n, wd in zip(small_names, widths)], axis=1) for i in range(3)]
    outs = _adamw(packed[0], packed[1], packed[2], small_g[:, :, :n_small], "adamw_small", N_DEV)
    off = 0
    for n, wd in zip(small_names, widths):
        real = small_w[n][0].shape[1]
        res[n] = tuple(o[:, off:off + real] for o in outs)
        off += wd
    loss = jnp.sum(small_g[:, 0, n_small])

    order = ["w_ada", "b_ada", "g_mix", "w_in", "w_dw", "b_dw", "g_conv_ln", "b_conv_ln", "g_q", "g_k",
             "w_out", "g_ffn", "w_gate", "w_up", "w_down"]
    lead = {"w_ada", "w_in", "w_dw", "w_out", "w_gate", "w_up", "w_down"}
    grads, deltas, new_m, new_v = [], [], [], []
    for n in order:
        g, d, mn, vn = res[n]
        g, d, mn, vn = (t[None] if n in lead else t for t in (g, d, mn, vn))
        grads.append(g)
        deltas.append(d)
        new_m.append(mn)
        new_v.append(vn)
    return (loss, grad_x2.reshape(n_seq, seq, D_MODEL), *grads, *deltas, *new_m, *new_v)
```

```python
import numpy as np
import jax
import jax.numpy as jnp
from jax import lax
from jax.experimental import pallas as pl
from jax.experimental.pallas import tpu as pltpu

F32 = jnp.float32
BF16 = jnp.bfloat16

N_DEV = 8
D_MODEL = 1024
D_CONV = 512
D_ATT = 512
HEAD_DIM = 64
CONV_WIDTH = 31
D_IN = 2 * D_CONV + 3 * D_ATT
D_FF = 2816
N_MOD = 6
EPS = 1e-6
RADIUS = 64
DILATIONS = (1, 4, 16)
Q_BLOCK = 128
LANES = 128
VMEM_LIMIT = 56 * 1024 * 1024

ADAM_LR = 0.001
ADAM_B1 = 0.9
ADAM_B2 = 0.999
ADAM_EPS = 1e-08
ADAM_WD = 0.01
ADAM_STEP = 10

NT = (((1,), (1,)), ((), ()))
TN = (((0,), (0,)), ((), ()))


def _call(body, **kw):
    return pl.pallas_call(body, **kw)


def _params(sem=None, vmem=VMEM_LIMIT):
    return pltpu.CompilerParams(dimension_semantics=sem, vmem_limit_bytes=vmem)


def _sig(x):
    return 1.0 / (1.0 + jnp.exp(-x))


def _sds(shape, dtype):
    return jax.ShapeDtypeStruct(shape, dtype)


N_PEER = N_DEV - 1
ANY_SPEC = pl.BlockSpec(memory_space=pl.ANY)


def _exchange_copies(scatter, ins, outs, *sems):
    n = len(ins)
    if n == 0:
        return [], []
    send_sems, recv_sems, local_sems = sems
    x, y, c = lax.axis_index("x"), lax.axis_index("y"), lax.axis_index("c")
    me = 4 * x + 2 * y + c

    def src(a, slot):
        return ins[a].at[slot] if scatter[a] else ins[a]

    local = [pltpu.make_async_copy(src(a, me), outs[a].at[me], local_sems.at[a]) for a in range(n)]
    flights = []
    for k in range(1, N_DEV):
        px = 1 - x if k & 4 else x
        py = 1 - y if k & 2 else y
        pc = 1 - c if k & 1 else c
        pid = 4 * px + 2 * py + pc
        for a in range(n):
            i = a * N_PEER + k - 1
            send, recv = (pltpu.make_async_remote_copy(
                src_ref=src(a, pid), dst_ref=outs[a].at[slot],
                send_sem=send_sems.at[i], recv_sem=recv_sems.at[i],
                device_id=(px, py, pc), device_id_type=pl.DeviceIdType.MESH) for slot in (me, pid))
            flights.append((send, recv))
    return local, flights


def _exchange_start(*args):
    local, flights = _exchange_copies(*args)
    for cp in local:
        cp.start()
    for send, _ in flights:
        send.start()


def _exchange_wait(*args):
    local, flights = _exchange_copies(*args)
    for send, recv in flights:
        send.wait_send()
        recv.wait_recv()
    for cp in local:
        cp.wait()


def _exchange_shapes(items):
    return [_sds((N_DEV,) + tuple(arr.shape[1:] if scatter else arr.shape), arr.dtype) for arr, scatter in items]


def _exchange_sems(n):
    if n == 0:
        return []
    return [pltpu.SemaphoreType.DMA((n * N_PEER,)), pltpu.SemaphoreType.DMA((n * N_PEER,)),
            pltpu.SemaphoreType.DMA((n,))]


def _gather_by_chip_phase(phase, ins, outs, send_sems, recv_sems, local_sems):
    n = len(ins)
    per = N_PEER
    x, y, c = lax.axis_index("x"), lax.axis_index("y"), lax.axis_index("c")
    me, sibling = (x, y, c), (x, y, 1 - c)
    chips = [(1 - x, y), (x, 1 - y), (1 - x, 1 - y)]

    def slot(px, py, pc):
        return 4 * px + 2 * py + pc

    def copy(a, k, block, to, src=None):
        dst = outs[a].at[slot(*block)]
        return pltpu.make_async_remote_copy(
            src_ref=dst if src is None else src, dst_ref=dst,
            send_sem=send_sems.at[a * per + k], recv_sem=recv_sems.at[a * per + k],
            device_id=to, device_id_type=pl.DeviceIdType.MESH)

    local = [pltpu.make_async_copy(ins[a], outs[a].at[slot(*me)], local_sems.at[a]) for a in range(n)]
    first = []
    for a in range(n):
        first.append(copy(a, 0, me, sibling, src=ins[a]))
        first += [copy(a, 1 + j, me, (*chip, c), src=ins[a]) for j, chip in enumerate(chips)]
    passed = [copy(a, 4 + j, (*chip, c), sibling) for j, chip in enumerate(chips) for a in range(n)]
    if phase == 0:
        for cp in local + first:
            cp.start()
    elif phase == 1:
        for j, chip in enumerate(chips):
            for a in range(n):
                copy(a, 1 + j, (*chip, c), me).wait_recv()
        for cp in passed:
            cp.start()
    else:
        for a in range(n):
            copy(a, 0, sibling, me).wait_recv()
            for j, chip in enumerate(chips):
                copy(a, 4 + j, (*chip, 1 - c), me).wait_recv()
        for cp in first + passed:
            cp.wait_send()
        for cp in local:
            cp.wait()


def _gather_by_chip(arrays, name):
    n = len(arrays)

    def body(*refs):
        for phase in range(3):
            _gather_by_chip_phase(phase, refs[:n], refs[n:2 * n], *refs[2 * n:])

    return _call(
        body, name=name, out_shape=_exchange_shapes([(arr, False) for arr in arrays]),
        in_specs=[ANY_SPEC] * n, out_specs=[ANY_SPEC] * n, scratch_shapes=_exchange_sems(n),
    )(*arrays)


def _exchange(items, name):
    n = len(items)
    scatter = [s for _, s in items]

    def body(*refs):
        args = (scatter, refs[:n], refs[n:2 * n]) + tuple(refs[2 * n:])
        _exchange_start(*args)
        _exchange_wait(*args)

    return _call(
        body, name=name, out_shape=_exchange_shapes(items),
        in_specs=[ANY_SPEC] * n, out_specs=[ANY_SPEC] * n, scratch_shapes=_exchange_sems(n),
    )(*[a for a, _ in items])


def _ada_fwd(c_all, w_ada, b_cols):
    def body(c_ref, w_ref, b_ref, o_ref):
        cv = c_ref[...]
        sc = (cv * _sig(cv)).astype(BF16)
        o_ref[...] = jnp.dot(sc, w_ref[...].astype(BF16), preferred_element_type=F32) + b_ref[...]

    return _call(body, name="ada_fwd", out_shape=_sds((c_all.shape[0], w_ada.shape[1]), F32),
                 compiler_params=_params())(c_all, w_ada, b_cols)


def _ada_bwd(c_all, dmod_cols, dmod_all):
    def body(c_ref, dc_ref, da_ref, gw_ref, gb_ref):
        cv = c_ref[...]
        sc = (cv * _sig(cv)).astype(BF16)
        gw_ref[...] = lax.dot_general(sc, dc_ref[...].astype(BF16), TN, preferred_element_type=F32)
        gb_ref[...] = jnp.sum(da_ref[...], axis=0, keepdims=True)

    return _call(body, name="ada_bwd",
                 out_shape=[_sds((c_all.shape[1], dmod_cols.shape[1]), F32), _sds((1, dmod_all.shape[1]), F32)],
                 compiler_params=_params())(c_all, dmod_cols, dmod_all)


MIX_ROWS = 128


def _mix_in(x2, mod8, g_mix, w_in, seq, tm=512):
    tokens = x2.shape[0]
    per_seq = seq // tm

    def body(x_ref, m_ref, g_ref, w_ref, h_ref, p_ref):
        def normed(c):
            rows = pl.ds(c * MIX_ROWS, MIX_ROWS)
            xv = x_ref[rows, :]
            r = lax.rsqrt(jnp.mean(xv * xv, axis=-1, keepdims=True) + EPS)
            hb = ((xv * r * g_ref[...]) * (1.0 + m_ref[1:2, :]) + m_ref[0:1, :]).astype(BF16)
            h_ref[rows, :] = hb
            return hb

        ahead = normed(0)
        for c in range(tm // MIX_ROWS):
            hb = ahead
            if c + 1 < tm // MIX_ROWS:
                ahead = normed(c + 1)
            p = jnp.dot(hb, w_ref[...], preferred_element_type=F32)
            for cb in range(D_IN // LANES):
                p_ref[cb, pl.ds(c * MIX_ROWS, MIX_ROWS), :] = p[:, cb * LANES:(cb + 1) * LANES]

    return _call(
        body, name="mix_in", grid=(tokens // tm,),
        in_specs=[pl.BlockSpec((tm, D_MODEL), lambda i: (i, 0)),
                  pl.BlockSpec((None, 8, D_MODEL), lambda i: (i // per_seq, 0, 0)),
                  pl.BlockSpec((1, D_MODEL), lambda i: (0, 0)),
                  pl.BlockSpec((D_MODEL, D_IN), lambda i: (0, 0))],
        out_specs=[pl.BlockSpec((tm, D_MODEL), lambda i: (i, 0)),
                   pl.BlockSpec((D_IN // LANES, tm, LANES), lambda i: (0, i, 0))],
        out_shape=[_sds((tokens, D_MODEL), BF16), _sds((D_IN // LANES, tokens, LANES), F32)],
        compiler_params=_params(("parallel",)),
    )(x2, mod8, g_mix, w_in)


CONV_ROWS = 64
CONV_DW_ROWS = 32
CONV_DW_UNROLL = 4
CONV_HALO = 16


def _fill_shifted(xp, sh, seq):
    for b in range(8):
        sh[b, pl.ds(0, seq + 24), :] = xp[pl.ds(b, seq + 24), :]


def _conv_fwd(proj3, w_dw, b_dw):
    _, n_seq, seq, _ = proj3.shape
    n_cb = D_CONV // LANES

    def body(a_ref, g_ref, w_ref, b_ref, uc_ref, xp, sh):
        zeros = jnp.zeros((CONV_HALO, LANES), F32)
        xp[pl.ds(0, CONV_HALO), :] = zeros
        xp[pl.ds(CONV_HALO + seq, CONV_HALO), :] = zeros
        xp[pl.ds(CONV_HALO, seq), :] = a_ref[...] * _sig(g_ref[...])
        _fill_shifted(xp, sh, seq)

        def blk(i, carry):
            t0 = pl.multiple_of(i * CONV_ROWS, CONV_ROWS)
            acc = jnp.zeros((CONV_ROWS, LANES), F32)
            for j in range(CONV_WIDTH):
                jj = j + 1
                acc = acc + sh[jj % 8, pl.ds(t0 + 8 * (jj // 8), CONV_ROWS), :] * w_ref[j:j + 1, :]
            uc_ref[pl.ds(t0, CONV_ROWS), :] = acc + b_ref[...]
            return carry

        lax.fori_loop(0, seq // CONV_ROWS, blk, 0)

    return _call(
        body, name="conv_fwd", grid=(n_seq, n_cb),
        in_specs=[pl.BlockSpec((None, None, seq, LANES), lambda b, cb: (cb, b, 0, 0)),
                  pl.BlockSpec((None, None, seq, LANES), lambda b, cb: (n_cb + cb, b, 0, 0)),
                  pl.BlockSpec((CONV_WIDTH, LANES), lambda b, cb: (0, cb)),
                  pl.BlockSpec((1, LANES), lambda b, cb: (0, cb))],
        out_specs=pl.BlockSpec((None, seq, LANES), lambda b, cb: (b, 0, cb)),
        out_shape=_sds((n_seq, seq, D_CONV), F32),
        scratch_shapes=[pltpu.VMEM((seq + 2 * CONV_HALO, LANES), F32),
                        pltpu.VMEM((8, seq + 2 * CONV_HALO, LANES), F32)],
        compiler_params=_params(("parallel", "parallel")),
    )(proj3, proj3, w_dw, b_dw)


def _conv_bwd(duc3, proj3, w_dw):
    _, n_seq, seq, _ = proj3.shape
    n_cb = D_CONV // LANES

    def body(duc_ref, a_ref, g_ref, w_ref, da_ref, dg_ref, dw_ref, db_ref, xp, sh):
        @pl.when(pl.program_id(1) == 0)
        def _():
            dw_ref[...] = jnp.zeros_like(dw_ref)
            db_ref[...] = jnp.zeros_like(db_ref)

        zeros = jnp.zeros((CONV_HALO, LANES), F32)
        xp[pl.ds(0, CONV_HALO), :] = zeros
        xp[pl.ds(CONV_HALO + seq, CONV_HALO), :] = zeros
        xp[pl.ds(CONV_HALO, seq), :] = a_ref[...] * _sig(g_ref[...])
        _fill_shifted(xp, sh, seq)
        for j0 in range(0, CONV_WIDTH, 8):
            taps = range(j0, min(j0 + 8, CONV_WIDTH))

            def wblk(i, accs, taps=taps):
                for u in range(CONV_DW_UNROLL):
                    t0 = pl.multiple_of((i * CONV_DW_UNROLL + u) * CONV_DW_ROWS, CONV_DW_ROWS)
                    d = duc_ref[pl.ds(t0, CONV_DW_ROWS), :]
                    accs = tuple(acc + d * sh[(j + 1) % 8, pl.ds(t0 + 8 * ((j + 1) // 8), CONV_DW_ROWS), :]
                                 for acc, j in zip(accs, taps))
                return accs

            accs = lax.fori_loop(0, seq // (CONV_DW_ROWS * CONV_DW_UNROLL), wblk,
                                 tuple(jnp.zeros((CONV_DW_ROWS, LANES), F32) for _ in taps))
            for acc, j in zip(accs, taps):
                dw_ref[j:j + 1, :] += jnp.sum(acc, axis=0, keepdims=True)
        db_ref[0:1, :] += jnp.sum(duc_ref[...], axis=0, keepdims=True)
        xp[pl.ds(CONV_HALO, seq), :] = duc_ref[...]
        _fill_shifted(xp, sh, seq)

        def ublk(i, carry):
            t0 = pl.multiple_of(i * CONV_ROWS, CONV_ROWS)
            acc = jnp.zeros((CONV_ROWS, LANES), F32)
            for j in range(CONV_WIDTH):
                jj = CONV_WIDTH - j
                acc = acc + sh[jj % 8, pl.ds(t0 + 8 * (jj // 8), CONV_ROWS), :] * w_ref[j:j + 1, :]
            av = a_ref[pl.ds(t0, CONV_ROWS), :]
            sg = _sig(g_ref[pl.ds(t0, CONV_ROWS), :])
            da_ref[pl.ds(t0, CONV_ROWS), :] = (acc * sg).astype(BF16)
            dg_ref[pl.ds(t0, CONV_ROWS), :] = (acc * av * sg * (1.0 - sg)).astype(BF16)
            return carry

        lax.fori_loop(0, seq // CONV_ROWS, ublk, 0)

    return _call(
        body, name="conv_bwd", grid=(n_cb, n_seq),
        in_specs=[pl.BlockSpec((None, seq, LANES), lambda cb, b: (b, 0, cb)),
                  pl.BlockSpec((None, None, seq, LANES), lambda cb, b: (cb, b, 0, 0)),
                  pl.BlockSpec((None, None, seq, LANES), lambda cb, b: (n_cb + cb, b, 0, 0)),
                  pl.BlockSpec((CONV_WIDTH, LANES), lambda cb, b: (0, cb))],
        out_specs=[pl.BlockSpec((None, seq, LANES), lambda cb, b: (b, 0, cb)),
                   pl.BlockSpec((None, seq, LANES), lambda cb, b: (b, 0, cb)),
                   pl.BlockSpec((32, LANES), lambda cb, b: (0, cb)),
                   pl.BlockSpec((8, LANES), lambda cb, b: (0, cb))],
        out_shape=[_sds((n_seq, seq, D_CONV), BF16), _sds((n_seq, seq, D_CONV), BF16),
                   _sds((32, D_CONV), F32), _sds((8, D_CONV), F32)],
        scratch_shapes=[pltpu.VMEM((seq + 2 * CONV_HALO, LANES), F32),
                        pltpu.VMEM((8, seq + 2 * CONV_HALO, LANES), F32)],
        compiler_params=_params(("parallel", "arbitrary")),
    )(duc3, proj3, proj3, w_dw)


MASKED = 1e30
ATT_ROWS = 512
ATT_UNROLL = 8
ATT_FWD_UNROLL = 8


def _distance_mats(dil, seg_len):
    kw = min(2 * Q_BLOCK, seg_len)
    offsets = (0, -RADIUS, -2 * RADIUS) if kw == 2 * Q_BLOCK else (0,)
    a = np.arange(Q_BLOCK)[:, None]
    b = np.arange(kw)[None, :]
    mats = []
    for off in offsets:
        rel = np.abs(b + off - a)
        mats.append(np.where(rel <= RADIUS, dil * rel, MASKED))
    return jnp.asarray(np.stack(mats).astype(np.float32))


def _alibi_rows():
    s = np.zeros((4, 8, LANES), np.float32)
    for hp in range(4):
        for hl in range(2):
            s[hp, hl, :] = 2.0 ** (-(2 * hp + hl + 1))
    return jnp.asarray(s)


def _window(n, seg_len):
    i0 = pl.multiple_of(n * Q_BLOCK, Q_BLOCK)
    if seg_len <= Q_BLOCK:
        return i0, i0, 0
    per_seg = seg_len // Q_BLOCK
    j = n % per_seg
    seg0 = (n // per_seg) * seg_len
    ks_local = jnp.clip(j * Q_BLOCK - RADIUS, 0, seg_len - 2 * Q_BLOCK)
    ks = pl.multiple_of(seg0 + ks_local, RADIUS)
    var = jnp.where(j == 0, 0, jnp.where(j == per_seg - 1, 2, 1))
    return i0, ks, var


def _first_head(rows):
    return lax.broadcasted_iota(jnp.int32, (rows, LANES), 1) < HEAD_DIM


def _same_head():
    head = np.arange(LANES) // HEAD_DIM
    return jnp.asarray((head[:, None] == head[None, :]).astype(np.float32)).astype(BF16)


def _head_sum(x, same_ref):
    hi = x.astype(BF16)
    lo = (x - hi.astype(F32)).astype(BF16)
    return (jnp.dot(hi, same_ref[...], preferred_element_type=F32)
            + jnp.dot(lo, same_ref[...], preferred_element_type=F32))


def _head_mean(x, same_ref):
    return _head_sum(x, same_ref) * (1.0 / HEAD_DIM)


def _per_head(x, first):
    swapped = pltpu.roll(x, HEAD_DIM, 1)
    return jnp.where(first, x, swapped), jnp.where(first, swapped, x)


STRIDE = 4


def _gather_segments(src, dil, seq, tmp, put):
    if dil == 1:
        put(0, seq, src[pl.ds(0, seq), :])
    elif dil == STRIDE:
        seg = seq // dil
        for r in range(dil):
            put(r * seg, seg, src[pl.ds(r, seg, stride=dil), :])
    else:
        part, seg = seq // STRIDE, seq // dil
        for b in range(STRIDE):
            tmp[pl.ds(b * part, part), :] = src[pl.ds(b, part, stride=STRIDE), :]
        for b in range(STRIDE):
            for a in range(dil // STRIDE):
                put(b * part + a * seg, seg, tmp[pl.ds(b * part + a, seg, stride=dil // STRIDE), :])


def _scatter_segments(dst, get, dil, seq, tmp, accumulate):
    def write(rows, val):
        if accumulate:
            dst[rows, :] += val
        else:
            dst[rows, :] = val

    if dil == 1:
        write(pl.ds(0, seq), get(0, seq))
    elif dil == STRIDE:
        seg = seq // dil
        for r in range(dil):
            write(pl.ds(r, seg, stride=dil), get(r * seg, seg))
    else:
        part, seg = seq // STRIDE, seq // dil
        for b in range(STRIDE):
            for a in range(dil // STRIDE):
                tmp[pl.ds(b * part + a, seg, stride=dil // STRIDE), :] = get(b * part + a * seg, seg)
        for b in range(STRIDE):
            write(pl.ds(b, part, stride=STRIDE), tmp[pl.ds(b * part, part), :])


def _permute_rows(dst, src, dil, seq, tmp):
    def put(start, size, val):
        dst[pl.ds(start, size), :] = val.astype(dst.dtype)

    _gather_segments(src, dil, seq, tmp, put)


def _permute_rows_by_head(dst, src, dil, seq, tmp):
    def put(start, size, val):
        first = _first_head(size)
        dst[0, pl.ds(start, size), :] = jnp.where(first, val, 0.0).astype(dst.dtype)
        dst[1, pl.ds(start, size), :] = jnp.where(first, 0.0, val).astype(dst.dtype)

    _gather_segments(src, dil, seq, tmp, put)


def _qk_normalise(q_ref, g2_ref, same_ref, dst, seq, scale):
    def chunk(ci, carry):
        rows = pl.ds(pl.multiple_of(ci * ATT_ROWS, ATT_ROWS), ATT_ROWS)
        qv = q_ref[rows, :]
        r = lax.rsqrt(_head_mean(qv * qv, same_ref) + EPS)
        dst[rows, :] = qv * r * (g2_ref[...] * scale)
        return carry

    lax.fori_loop(0, seq // ATT_ROWS, chunk, 0)


def _attn_fwd(proj3, g_q2, g_k2, ride):
    _, n_seq, seq, _ = proj3.shape
    dms = [_distance_mats(d, seq // d) for d in DILATIONS]
    same = _same_head()
    col0 = 2 * D_CONV // LANES
    n_hp = D_ATT // LANES

    n_ride = len(ride)
    assert not any(scatter for _, scatter in ride), "the forward's ride is an all-gather"

    def body(*refs):
        q_ref, k_ref, v_ref, gq_ref, gk_ref, sl_ref, dm1, dm4, dm16, same_ref = refs[:10]
        ride_in = refs[10:10 + n_ride]
        y_ref, lse_ref = refs[10 + n_ride:12 + n_ride]
        ride_out = refs[12 + n_ride:12 + 2 * n_ride]
        (qf, kf, qp, kp, vp, oml_p, o1, o4, o16, m1, m4, m16, l1, l4, l16,
         tmp) = refs[12 + 2 * n_ride:28 + 2 * n_ride]
        o_nat, m_nat, l_nat = (o1, o4, o16), (m1, m4, m16), (l1, l4, l16)
        ride_args = (ride_in, ride_out) + tuple(refs[28 + 2 * n_ride:])
        step = pl.program_id(0) * n_hp + pl.program_id(1)
        n_steps = n_seq * n_hp

        if n_ride:
            for phase, at in enumerate((0, (3 * n_steps) // 4)):
                @pl.when(step == at)
                def _(phase=phase):
                    _gather_by_chip_phase(phase, *ride_args)

        dm_refs = (dm1, dm4, dm16)
        _qk_normalise(q_ref, gq_ref, same_ref, qf, seq, HEAD_DIM ** -0.5)
        _qk_normalise(k_ref, gk_ref, same_ref, kf, seq, 1.0)
        slopes = (sl_ref[0:1, 0:1], sl_ref[1:2, 0:1])
        for pi, dil in enumerate(DILATIONS):
            seg = seq // dil
            kw = min(2 * Q_BLOCK, seg)
            _permute_rows_by_head(qp, qf, dil, seq, tmp)
            _permute_rows(kp, kf, dil, seq, tmp)
            _permute_rows(vp, v_ref, dil, seq, tmp)

            def blk(it, carry, seg=seg, kw=kw, pi=pi, dst=oml_p):
                first = _first_head(Q_BLOCK)
                chains = [(sub, h) for sub in range(ATT_FWD_UNROLL) for h in range(2)]
                win = [_window(it * ATT_FWD_UNROLL + sub, seg) for sub in range(ATT_FWD_UNROLL)]
                s = {}
                for sub, h in chains:
                    i0, ks, var = win[sub]
                    s[sub, h] = lax.dot_general(qp[h, pl.ds(i0, Q_BLOCK), :], kp[pl.ds(ks, kw), :], NT,
                                                preferred_element_type=F32) - slopes[h] * dm_refs[pi][var]
                m, l, p = {}, {}, {}
                for c in chains:
                    m[c] = jnp.max(s[c], axis=1, keepdims=True)
                    e = jnp.exp(s[c] - m[c])
                    l[c] = jnp.sum(e, axis=1, keepdims=True)
                    p[c] = e.astype(BF16)
                o = {}
                for sub, h in chains:
                    o[sub, h] = jnp.dot(p[sub, h], vp[pl.ds(win[sub][1], kw), :], preferred_element_type=F32)
                packed = [jnp.concatenate([jnp.where(first, t[sub, 0], t[sub, 1]) for t in (o, m, l)], axis=1)
                          for sub in range(ATT_FWD_UNROLL)]
                span = ATT_FWD_UNROLL * Q_BLOCK
                dst[pl.ds(pl.multiple_of(it * span, span), span), :] = jnp.concatenate(packed, axis=0)
                return carry

            lax.fori_loop(0, seq // (Q_BLOCK * ATT_FWD_UNROLL), blk, 0)
            for n, nat in enumerate((o_nat[pi], m_nat[pi], l_nat[pi])):
                _scatter_segments(nat, lambda start, size, n=n: oml_p[pl.ds(start, size), pl.ds(n * LANES, LANES)],
                                  dil, seq, tmp, accumulate=False)

        def merge(ci, carry):
            rows = pl.ds(pl.multiple_of(ci * ATT_ROWS, ATT_ROWS), ATT_ROWS)
            ms = [m_nat[pi][rows, :] for pi in range(3)]
            m_all = jnp.maximum(jnp.maximum(ms[0], ms[1]), ms[2])
            es = [jnp.exp(m - m_all) for m in ms]
            l_all = sum(l_nat[pi][rows, :] * es[pi] for pi in range(3))
            inv = 1.0 / l_all
            o = sum(o_nat[pi][rows, :] * (es[pi] * inv) for pi in range(3))
            y_ref[rows, :] = o.astype(BF16)
            lse_ref[rows, :] = m_all + jnp.log(l_all)
            return carry

        lax.fori_loop(0, seq // ATT_ROWS, merge, 0)

        if n_ride:
            @pl.when(step == n_steps - 1)
            def _():
                _gather_by_chip_phase(2, *ride_args)

    def col(off):
        return pl.BlockSpec((None, None, seq, LANES), lambda b, hp: (col0 + off * n_hp + hp, b, 0, 0))

    def whole(arr):
        return pl.BlockSpec(arr.shape, lambda b, hp: (0,) * arr.ndim)

    rows_f32 = pltpu.VMEM((seq, LANES), F32)
    rows_bf16 = pltpu.VMEM((seq, LANES), BF16)
    return _call(
        body, name="attn_fwd", grid=(n_seq, n_hp),
        in_specs=[col(0), col(1), col(2), whole(g_q2), whole(g_k2),
                  pl.BlockSpec((None, 8, LANES), lambda b, hp: (hp, 0, 0)),
                  whole(dms[0]), whole(dms[1]), whole(dms[2]), whole(same)] + [ANY_SPEC] * n_ride,
        out_specs=[pl.BlockSpec((None, seq, LANES), lambda b, hp: (b, 0, hp)),
                   pl.BlockSpec((None, seq, LANES), lambda b, hp: (b, 0, hp))] + [ANY_SPEC] * n_ride,
        out_shape=[_sds((n_seq, seq, D_ATT), BF16), _sds((n_seq, seq, D_ATT), F32)] + _exchange_shapes(ride),
        scratch_shapes=[rows_f32, rows_f32, pltpu.VMEM((2, seq, LANES), BF16), rows_bf16, rows_bf16]
        + [pltpu.VMEM((seq, 3 * LANES), F32)] + [rows_f32] * 10 + _exchange_sems(n_ride),
        compiler_params=_params(("arbitrary", "arbitrary")),
    )(proj3, proj3, proj3, g_q2, g_k2, _alibi_rows(), *dms, same, *[a for a, _ in ride])


def _attn_bwd(proj3, do3, y_att3, lse3, g_q2, g_k2, ride):
    _, n_seq, seq, _ = proj3.shape
    dms = [_distance_mats(d, seq // d) for d in DILATIONS]
    same = _same_head()
    col0 = 2 * D_CONV // LANES
    n_hp = D_ATT // LANES

    n_ride = len(ride)
    ride_scatter = [s for _, s in ride]

    def body(*refs):
        (q_ref, k_ref, v_ref, do_ref, o_ref, lse_ref, gq_ref, gk_ref, sl_ref, dm1, dm4, dm16,
         same_ref) = refs[:13]
        ride_in = refs[13:13 + n_ride]
        dq_ref, dk_ref, dv_ref, dg_ref = refs[13 + n_ride:17 + n_ride]
        ride_out = refs[17 + n_ride:17 + 2 * n_ride]
        (qf, kf, qp, dop, kp, vp, sn, sp, dqp, dkp, dvp, dqn, dkn, dvn,
         tmp) = refs[17 + 2 * n_ride:32 + 2 * n_ride]
        ride_args = (ride_scatter, ride_in, ride_out) + tuple(refs[32 + 2 * n_ride:])
        dm_refs = (dm1, dm4, dm16)
        step = pl.program_id(0) * n_hp + pl.program_id(1)

        @pl.when(step == 0)
        def _():
            _exchange_start(*ride_args)
            dg_ref[...] = jnp.zeros_like(dg_ref)

        _qk_normalise(q_ref, gq_ref, same_ref, qf, seq, HEAD_DIM ** -0.5)
        _qk_normalise(k_ref, gk_ref, same_ref, kf, seq, 1.0)

        def stats(ci, carry):
            rows = pl.ds(pl.multiple_of(ci * ATT_ROWS, ATT_ROWS), ATT_ROWS)
            first = _first_head(ATT_ROWS)
            sn[0, rows, :], sn[1, rows, :] = _per_head(lse_ref[rows, :], first)
            prod = do_ref[rows, :] * o_ref[rows, :].astype(F32)
            sn[2, rows, :], sn[3, rows, :] = _per_head(_head_sum(prod, same_ref), first)
            return carry

        lax.fori_loop(0, seq // ATT_ROWS, stats, 0)
        slopes = (sl_ref[0:1, 0:1], sl_ref[1:2, 0:1])
        half = seq // (Q_BLOCK * ATT_UNROLL)
        region = seq // ATT_UNROLL

        for pi, dil in enumerate(DILATIONS):
            seg = seq // dil
            kw = min(2 * Q_BLOCK, seg)
            _permute_rows_by_head(qp, qf, dil, seq, tmp)
            _permute_rows_by_head(dop, do_ref, dil, seq, tmp)
            _permute_rows(kp, kf, dil, seq, tmp)
            _permute_rows(vp, v_ref, dil, seq, tmp)
            if dil == 1:
                st = sn
            else:
                st = sp
                for n in range(4):
                    _permute_rows(sp.at[n], sn.at[n], dil, seq, tmp)
            def touched(sub, seg=seg):
                lo, hi = sub * region, (sub + 1) * region
                if seg < region:
                    return lo, hi
                seg0 = lo // seg * seg
                return max(lo - RADIUS, seg0), min(hi + RADIUS, seg0 + seg)

            def summed(acc, start, size, touched=touched):
                pieces = []
                for c0 in range(start, start + size, RADIUS):
                    owners = [s for s in range(ATT_UNROLL) if touched(s)[0] <= c0 and c0 + RADIUS <= touched(s)[1]]
                    if pieces and pieces[-1][2] == owners:
                        pieces[-1][1] += RADIUS
                    else:
                        pieces.append([c0, RADIUS, owners])
                vals = [sum(acc[o, pl.ds(c0, n), :] for o in owners) for c0, n, owners in pieces]
                return vals[0] if len(vals) == 1 else jnp.concatenate(vals, axis=0)

            for sub in range(ATT_UNROLL):
                lo, hi = touched(sub)
                dkp[sub, pl.ds(lo, hi - lo), :] = jnp.zeros((hi - lo, LANES), F32)
                dvp[sub, pl.ds(lo, hi - lo), :] = jnp.zeros((hi - lo, LANES), F32)

            def blk(it, carry, seg=seg, kw=kw, pi=pi, st=st):
                first = _first_head(Q_BLOCK)
                chains = [(sub, h) for sub in range(ATT_UNROLL) for h in range(2)]
                win = [_window(it + sub * half, seg) for sub in range(ATT_UNROLL)]
                qrows = [pl.ds(w[0], Q_BLOCK) for w in win]
                krows = [pl.ds(w[1], kw) for w in win]

                def over_keys(n, sub):
                    t = st[n, qrows[sub], :]
                    return t if kw == LANES else jnp.concatenate([t] * (kw // LANES), axis=1)

                s, dp = {}, {}
                for sub, h in chains:
                    s[sub, h] = lax.dot_general(qp[h, qrows[sub], :], kp[krows[sub], :], NT,
                                                preferred_element_type=F32) - slopes[h] * dm_refs[pi][win[sub][2]]
                    dp[sub, h] = lax.dot_general(dop[h, qrows[sub], :], vp[krows[sub], :], NT,
                                                 preferred_element_type=F32)
                p, ds = {}, {}
                for sub, h in chains:
                    e = jnp.exp(s[sub, h] - over_keys(h, sub))
                    ds[sub, h] = (e * (dp[sub, h] - over_keys(2 + h, sub))).astype(BF16)
                    p[sub, h] = e.astype(BF16)
                dq, dk, dv = {}, {}, {}
                for sub, h in chains:
                    dq[sub, h] = jnp.dot(ds[sub, h], kp[krows[sub], :], preferred_element_type=F32)
                    dk[sub, h] = lax.dot_general(ds[sub, h], qp[h, qrows[sub], :], TN, preferred_element_type=F32)
                    dv[sub, h] = lax.dot_general(p[sub, h], dop[h, qrows[sub], :], TN, preferred_element_type=F32)
                for sub in range(ATT_UNROLL):
                    dqp[qrows[sub], :] = jnp.where(first, dq[sub, 0], dq[sub, 1])
                    dkp[sub, krows[sub], :] += dk[sub, 0] + dk[sub, 1]
                    dvp[sub, krows[sub], :] += dv[sub, 0] + dv[sub, 1]
                return carry

            lax.fori_loop(0, half, blk, 0)
            first_pattern = pi == 0
            if first_pattern:
                for r0 in range(0, seq, region):
                    rows = pl.ds(r0, region)
                    dqn[rows, :] = dqp[rows, :]
                    dkn[rows, :] = summed(dkp, r0, region)
                    dvn[rows, :] = summed(dvp, r0, region)
            else:
                _scatter_segments(dqn, lambda start, size: dqp[pl.ds(start, size), :], dil, seq, tmp, accumulate=True)
                for nat, acc in ((dkn, dkp), (dvn, dvp)):
                    _scatter_segments(nat, lambda start, size, acc=acc: summed(acc, start, size),
                                      dil, seq, tmp, accumulate=True)

        def finish(ci, carry):
            rows = pl.ds(pl.multiple_of(ci * ATT_ROWS, ATT_ROWS), ATT_ROWS)
            for src_ref, g_ref, dn, dst_ref, scale, row in (
                    (q_ref, gq_ref, dqn, dq_ref, HEAD_DIM ** -0.5, 0), (k_ref, gk_ref, dkn, dk_ref, 1.0, 1)):
                xv = src_ref[rows, :]
                r = lax.rsqrt(_head_mean(xv * xv, same_ref) + EPS)
                xhat = xv * r
                d = dn[rows, :] * scale
                dg_ref[row:row + 1, :] += jnp.sum(d * xhat, axis=0, keepdims=True)
                dxh = d * g_ref[...]
                dst_ref[rows, :] = (r * (dxh - xhat * _head_mean(dxh * xhat, same_ref))).astype(BF16)
            dv_ref[rows, :] = dvn[rows, :].astype(BF16)
            return carry

        lax.fori_loop(0, seq // ATT_ROWS, finish, 0)

        @pl.when(step == n_seq * n_hp - 1)
        def _():
            _exchange_wait(*ride_args)

    def col(off):
        return pl.BlockSpec((None, None, seq, LANES), lambda b, hp: (col0 + off * n_hp + hp, b, 0, 0))

    def whole(arr):
        return pl.BlockSpec(arr.shape, lambda b, hp: (0,) * arr.ndim)

    att = pl.BlockSpec((None, seq, LANES), lambda b, hp: (b, 0, hp))
    rows_f32 = pltpu.VMEM((seq, LANES), F32)
    rows_bf16 = pltpu.VMEM((seq, LANES), BF16)
    by_head_bf16 = pltpu.VMEM((2, seq, LANES), BF16)
    per_sub_f32 = pltpu.VMEM((ATT_UNROLL, seq, LANES), F32)
    stats_f32 = pltpu.VMEM((4, seq, LANES), F32)
    return _call(
        body, name="attn_bwd", grid=(n_seq, n_hp),
        in_specs=[col(0), col(1), col(2), att, att, att, whole(g_q2), whole(g_k2),
                  pl.BlockSpec((None, 8, LANES), lambda b, hp: (hp, 0, 0)),
                  whole(dms[0]), whole(dms[1]), whole(dms[2]), whole(same)] + [ANY_SPEC] * n_ride,
        out_specs=[att, att, att, pl.BlockSpec((8, LANES), lambda b, hp: (0, 0))] + [ANY_SPEC] * n_ride,
        out_shape=[_sds((n_seq, seq, D_ATT), BF16)] * 3 + [_sds((8, LANES), F32)] + _exchange_shapes(ride),
        scratch_shapes=[rows_f32, rows_f32, by_head_bf16, by_head_bf16, rows_bf16, rows_bf16, stats_f32, stats_f32,
                        rows_f32, per_sub_f32, per_sub_f32, rows_f32, rows_f32, rows_f32, rows_f32]
        + _exchange_sems(n_ride),
        compiler_params=_params(("arbitrary", "arbitrary")),
    )(proj3, proj3, proj3, do3, y_att3, lse3, g_q2, g_k2, _alibi_rows(), *dms, same, *[a for a, _ in ride])


def _mix_out(uc2, y_att2, x2, mod8, g_ln, b_ln, g_ffn, w_out, seq, tm=512):
    tokens = x2.shape[0]
    per_seq = seq // tm

    def body(uc_ref, ya_ref, x_ref, m_ref, gl_ref, bl_ref, gf_ref, w_ref, yc_ref, mix_ref, x1_ref, h2_ref):
        uc = uc_ref[...]
        mu = jnp.mean(uc, axis=-1, keepdims=True)
        cen = uc - mu
        rs = lax.rsqrt(jnp.mean(cen * cen, axis=-1, keepdims=True) + EPS)
        z = cen * rs * gl_ref[...] + bl_ref[...]
        yc = (z * _sig(z)).astype(BF16)
        yc_ref[...] = yc
        mix = (jnp.dot(yc, w_ref[pl.ds(0, D_CONV), :], preferred_element_type=F32)
               + jnp.dot(ya_ref[...], w_ref[pl.ds(D_CONV, D_ATT), :], preferred_element_type=F32))
        mix_ref[...] = mix
        x1 = x_ref[...] + m_ref[2:3, :] * mix
        x1_ref[...] = x1
        r = lax.rsqrt(jnp.mean(x1 * x1, axis=-1, keepdims=True) + EPS)
        h2_ref[...] = ((x1 * r * gf_ref[...]) * (1.0 + m_ref[4:5, :]) + m_ref[3:4, :]).astype(BF16)

    def rows(width):
        return pl.BlockSpec((tm, width), lambda i: (i, 0))

    def vec(width):
        return pl.BlockSpec((1, width), lambda i: (0, 0))

    return _call(
        body, name="mix_out", grid=(tokens // tm,),
        in_specs=[rows(D_CONV), rows(D_ATT), rows(D_MODEL),
                  pl.BlockSpec((None, 8, D_MODEL), lambda i: (i // per_seq, 0, 0)),
                  vec(D_CONV), vec(D_CONV), vec(D_MODEL),
                  pl.BlockSpec((D_MODEL, D_MODEL), lambda i: (0, 0))],
        out_specs=[rows(D_CONV), rows(D_MODEL), rows(D_MODEL), rows(D_MODEL)],
        out_shape=[_sds((tokens, D_CONV), BF16), _sds((tokens, D_MODEL), F32),
                   _sds((tokens, D_MODEL), F32), _sds((tokens, D_MODEL), BF16)],
        compiler_params=_params(("parallel",)),
    )(uc2, y_att2, x2, mod8, g_ln, b_ln, g_ffn, w_out)


def _mix_out_bwd(dmix, uc2, g_ln, b_ln, w_out, tm=512):
    tokens = dmix.shape[0]

    def body(dm_ref, uc_ref, gl_ref, bl_ref, w_ref, duc_ref, do_ref, dgb_ref):
        @pl.when(pl.program_id(0) == 0)
        def _():
            dgb_ref[...] = jnp.zeros_like(dgb_ref)

        dmv = dm_ref[...]
        dyc = lax.dot_general(dmv, w_ref[pl.ds(0, D_CONV), :], NT, preferred_element_type=F32)
        do_ref[...] = lax.dot_general(dmv, w_ref[pl.ds(D_CONV, D_ATT), :], NT, preferred_element_type=F32)
        uc = uc_ref[...]
        mu = jnp.mean(uc, axis=-1, keepdims=True)
        cen = uc - mu
        rs = lax.rsqrt(jnp.mean(cen * cen, axis=-1, keepdims=True) + EPS)
        xh = cen * rs
        z = xh * gl_ref[...] + bl_ref[...]
        sg = _sig(z)
        dz = dyc * (sg * (1.0 + z * (1.0 - sg)))
        dgb_ref[0:1, :] += jnp.sum(dz * xh, axis=0, keepdims=True)
        dgb_ref[1:2, :] += jnp.sum(dz, axis=0, keepdims=True)
        dxh = dz * gl_ref[...]
        duc_ref[...] = rs * (dxh - jnp.mean(dxh, axis=-1, keepdims=True)
                             - xh * jnp.mean(dxh * xh, axis=-1, keepdims=True))

    return _call(
        body, name="mix_out_bwd", grid=(tokens // tm,),
        in_specs=[pl.BlockSpec((tm, D_MODEL), lambda i: (i, 0)),
                  pl.BlockSpec((tm, D_CONV), lambda i: (i, 0)),
                  pl.BlockSpec((1, D_CONV), lambda i: (0, 0)),
                  pl.BlockSpec((1, D_CONV), lambda i: (0, 0)),
                  pl.BlockSpec((D_MODEL, D_MODEL), lambda i: (0, 0))],
        out_specs=[pl.BlockSpec((tm, D_CONV), lambda i: (i, 0)),
                   pl.BlockSpec((tm, D_ATT), lambda i: (i, 0)),
                   pl.BlockSpec((8, D_CONV), lambda i: (0, 0))],
        out_shape=[_sds((tokens, D_CONV), F32), _sds((tokens, D_ATT), F32), _sds((8, D_CONV), F32)],
        compiler_params=_params(("arbitrary",)),
    )(dmix, uc2, g_ln, b_ln, w_out)


FF_TILE = 256
FF_TILES = D_FF // FF_TILE
FF_ROWS = 256


def _ffn_fwd(h2, w_gate, w_up, w_down, x1, target, mod8, seq, tm=1024):
    tokens = h2.shape[0]
    per_seq = seq // tm
    n_seq = tokens // seq
    last = D_FF // FF_TILE - 1

    def body(h_ref, wg_ref, wu_ref, wd_ref, x1_ref, t_ref, m_ref, gate_ref, up_ref, dy_ref, df_ref, sq_ref, dgf_ref,
             f_ref):
        i, j = pl.program_id(0), pl.program_id(1)

        @pl.when(j == 0)
        def _():
            f_ref[...] = jnp.zeros_like(f_ref)

        @pl.when((j == 0) & (i == 0))
        def _():
            sq_ref[...] = jnp.zeros_like(sq_ref)

        @pl.when((j == 0) & (i % per_seq == 0))
        def _():
            dgf_ref[...] = jnp.zeros_like(dgf_ref)

        def gate_up(r):
            hv = h_ref[pl.ds(r * FF_ROWS, FF_ROWS), :]
            return (jnp.dot(hv, wg_ref[...], preferred_element_type=F32),
                    jnp.dot(hv, wu_ref[...], preferred_element_type=F32))

        ahead = gate_up(0)
        for r in range(tm // FF_ROWS):
            gate, up = ahead
            if r + 1 < tm // FF_ROWS:
                ahead = gate_up(r + 1)
            rows = pl.ds(r * FF_ROWS, FF_ROWS)
            gate_ref[rows, :] = gate.astype(BF16)
            up_ref[rows, :] = up.astype(BF16)
            act = (gate * _sig(gate) * up).astype(BF16)
            f_ref[rows, :] += jnp.dot(act, wd_ref[...], preferred_element_type=F32)

        @pl.when(j == last)
        def _():
            gate_f = m_ref[5:6, :]
            for r in range(tm // FF_ROWS):
                rows = pl.ds(r * FF_ROWS, FF_ROWS)
                fv = f_ref[rows, :]
                diff = x1_ref[rows, :] + gate_f * fv - t_ref[rows, :]
                sq_ref[0:1, :] += jnp.sum(diff * diff, axis=0, keepdims=True)
                dy = diff * (1.0 / D_MODEL)
                dy_ref[rows, :] = dy
                df_ref[rows, :] = (gate_f * dy).astype(BF16)
                dgf_ref[0:1, :] += jnp.sum(dy * fv, axis=0, keepdims=True)

    rows_spec = pl.BlockSpec((tm, D_MODEL), lambda i, j: (i, 0))
    per = pl.BlockSpec((None, 8, D_MODEL), lambda i, j: (i // per_seq, 0, 0))
    tile = pl.BlockSpec((None, tm, FF_TILE), lambda i, j: (j, i, 0))
    w_tile = pl.BlockSpec((None, D_MODEL, FF_TILE), lambda i, j: (j, 0, 0))
    return _call(
        body, name="ffn_fwd", grid=(tokens // tm, D_FF // FF_TILE),
        in_specs=[rows_spec, w_tile, w_tile,
                  pl.BlockSpec((FF_TILE, D_MODEL), lambda i, j: (j, 0)),
                  rows_spec, rows_spec, per],
        out_specs=[tile, tile, rows_spec, rows_spec, pl.BlockSpec((8, D_MODEL), lambda i, j: (0, 0)), per],
        out_shape=[_sds((FF_TILES, tokens, FF_TILE), BF16), _sds((FF_TILES, tokens, FF_TILE), BF16),
                   _sds((tokens, D_MODEL), F32),
                   _sds((tokens, D_MODEL), BF16), _sds((8, D_MODEL), F32), _sds((n_seq, 8, D_MODEL), F32)],
        scratch_shapes=[pltpu.VMEM((tm, D_MODEL), F32)],
        compiler_params=_params(("arbitrary", "arbitrary")),
    )(h2, w_gate, w_up, w_down, x1, target, mod8)


def _ffn_bwd(df, gate, up, w_gate, w_up, w_down, x1, dy, mix, mod8, g_ffn, seq, tm=1024):
    tokens = df.shape[0]
    per_seq = seq // tm
    n_seq = tokens // seq
    last = D_FF // FF_TILE - 1

    def body(df_ref, gate_ref, up_ref, wg_ref, wu_ref, wd_ref, m_ref, g_ref, x1_hbm, dy_hbm, mix_hbm,
             dgate_ref, dup_ref, act_ref, dx1_ref, dmix_ref, dg_ref, dm_ref, dh_ref, late, late_sems):
        i, j = pl.program_id(0), pl.program_id(1)
        my_rows = pl.ds(pl.multiple_of(i * tm, tm), tm)
        fetches = [pltpu.make_async_copy(src.at[my_rows, :], late.at[n], late_sems.at[n])
                   for n, src in enumerate((x1_hbm, dy_hbm, mix_hbm))]

        @pl.when(j == 0)
        def _():
            dh_ref[...] = jnp.zeros_like(dh_ref)
            for cp in fetches:
                cp.start()

        @pl.when((j == 0) & (i == 0))
        def _():
            dg_ref[...] = jnp.zeros_like(dg_ref)

        @pl.when((j == 0) & (i % per_seq == 0))
        def _():
            dm_ref[...] = jnp.zeros_like(dm_ref)

        def d_act(r):
            return lax.dot_general(df_ref[pl.ds(r * FF_ROWS, FF_ROWS), :], wd_ref[...], NT,
                                   preferred_element_type=F32)

        ahead = d_act(0)
        for r in range(tm // FF_ROWS):
            dact = ahead
            if r + 1 < tm // FF_ROWS:
                ahead = d_act(r + 1)
            rows = pl.ds(r * FF_ROWS, FF_ROWS)
            gate = gate_ref[rows, :].astype(F32)
            up = up_ref[rows, :].astype(F32)
            sg = _sig(gate)
            silu = gate * sg
            act_ref[rows, :] = (silu * up).astype(BF16)
            dup = (dact * silu).astype(BF16)
            dgate = (dact * up * (sg * (1.0 + gate * (1.0 - sg)))).astype(BF16)
            dup_ref[rows, :] = dup
            dgate_ref[rows, :] = dgate
            dh_ref[rows, :] += (lax.dot_general(dgate, wg_ref[...], NT, preferred_element_type=F32)
                                + lax.dot_general(dup, wu_ref[...], NT, preferred_element_type=F32))

        @pl.when(j == last)
        def _():
            for cp in fetches:
                cp.wait()
            g = g_ref[...]
            for r in range(tm // FF_ROWS):
                rows = pl.ds(r * FF_ROWS, FF_ROWS)
                dh = dh_ref[rows, :]
                x1v = late[0, rows, :]
                rs = lax.rsqrt(jnp.mean(x1v * x1v, axis=-1, keepdims=True) + EPS)
                xhat = x1v * rs
                dm_ref[0:1, :] += jnp.sum(dh, axis=0, keepdims=True)
                dm_ref[1:2, :] += jnp.sum(dh * (xhat * g), axis=0, keepdims=True)
                dn = dh * (1.0 + m_ref[4:5, :])
                dg_ref[0:1, :] += jnp.sum(dn * xhat, axis=0, keepdims=True)
                dxh = dn * g
                dx1 = late[1, rows, :] + rs * (dxh - xhat * jnp.mean(dxh * xhat, axis=-1, keepdims=True))
                dx1_ref[rows, :] = dx1
                dm_ref[2:3, :] += jnp.sum(dx1 * late[2, rows, :], axis=0, keepdims=True)
                dmix_ref[rows, :] = (m_ref[2:3, :] * dx1).astype(BF16)

    tile = pl.BlockSpec((None, tm, FF_TILE), lambda i, j: (j, i, 0))
    w_tile = pl.BlockSpec((None, D_MODEL, FF_TILE), lambda i, j: (j, 0, 0))
    rows_spec = pl.BlockSpec((tm, D_MODEL), lambda i, j: (i, 0))
    per = pl.BlockSpec((None, 8, D_MODEL), lambda i, j: (i // per_seq, 0, 0))
    return _call(
        body, name="ffn_bwd", grid=(tokens // tm, D_FF // FF_TILE),
        in_specs=[rows_spec, tile, tile, w_tile, w_tile,
                  pl.BlockSpec((FF_TILE, D_MODEL), lambda i, j: (j, 0)),
                  per, pl.BlockSpec((1, D_MODEL), lambda i, j: (0, 0)), ANY_SPEC, ANY_SPEC, ANY_SPEC],
        out_specs=[tile, tile, tile, rows_spec, rows_spec, pl.BlockSpec((8, D_MODEL), lambda i, j: (0, 0)), per],
        out_shape=[_sds((FF_TILES, tokens, FF_TILE), BF16)] * 3
        + [_sds((tokens, D_MODEL), F32), _sds((tokens, D_MODEL), BF16),
           _sds((8, D_MODEL), F32), _sds((n_seq, 8, D_MODEL), F32)],
        scratch_shapes=[pltpu.VMEM((tm, D_MODEL), F32), pltpu.VMEM((3, tm, D_MODEL), F32),
                        pltpu.SemaphoreType.DMA((3,))],
        compiler_params=_params(("arbitrary", "arbitrary")),
    )(df, gate, up, w_gate, w_up, w_down, mod8, g_ffn, x1, dy, mix)


def _mix_in_bwd(d_a, d_g, d_q, d_k, d_v, w_in, x2, dx1, mod8, g_mix, seq, ride, tm=512):
    tokens = x2.shape[0]
    per_seq = seq // tm
    n_seq = tokens // seq
    parts = (d_a, d_g, d_q, d_k, d_v)
    width = D_CONV
    n_ride = len(ride)
    ride_scatter = [s for _, s in ride]

    def body(*refs):
        da_ref, dg_ref, dq_ref, dk_ref, dv_ref, w_ref, x_ref, dx1_ref, m_ref, g_ref = refs[:10]
        ride_in = refs[10:10 + n_ride]
        gx_ref, dgm_ref, dm_ref = refs[10 + n_ride:13 + n_ride]
        ride_args = (ride_scatter, ride_in, refs[13 + n_ride:13 + 2 * n_ride]) + tuple(refs[13 + 2 * n_ride:])
        i = pl.program_id(0)

        @pl.when(i == 0)
        def _():
            _exchange_start(*ride_args)
            dgm_ref[...] = jnp.zeros_like(dgm_ref)

        @pl.when(i % per_seq == 0)
        def _():
            dm_ref[...] = jnp.zeros_like(dm_ref)

        dh = jnp.zeros((tm, D_MODEL), F32)
        for n, ref in enumerate((da_ref, dg_ref, dq_ref, dk_ref, dv_ref)):
            dh = dh + lax.dot_general(ref[...], w_ref[:, pl.ds(n * width, width)], NT, preferred_element_type=F32)
        xv = x_ref[...]
        r = lax.rsqrt(jnp.mean(xv * xv, axis=-1, keepdims=True) + EPS)
        xhat = xv * r
        g = g_ref[...]
        dm_ref[0:1, :] += jnp.sum(dh, axis=0, keepdims=True)
        dm_ref[1:2, :] += jnp.sum(dh * (xhat * g), axis=0, keepdims=True)
        dn = dh * (1.0 + m_ref[1:2, :])
        dgm_ref[0:1, :] += jnp.sum(dn * xhat, axis=0, keepdims=True)
        dxh = dn * g
        gx_ref[...] = dx1_ref[...] + r * (dxh - xhat * jnp.mean(dxh * xhat, axis=-1, keepdims=True))

        @pl.when(i == tokens // tm - 1)
        def _():
            _exchange_wait(*ride_args)

    rows = pl.BlockSpec((tm, D_MODEL), lambda i: (i, 0))
    half = pl.BlockSpec((tm, width), lambda i: (i, 0))
    per = pl.BlockSpec((None, 8, D_MODEL), lambda i: (i // per_seq, 0, 0))
    return _call(
        body, name="mix_in_bwd", grid=(tokens // tm,),
        in_specs=[half] * 5 + [pl.BlockSpec((D_MODEL, D_IN), lambda i: (0, 0)), rows, rows, per,
                               pl.BlockSpec((1, D_MODEL), lambda i: (0, 0))] + [ANY_SPEC] * n_ride,
        out_specs=[rows, pl.BlockSpec((8, D_MODEL), lambda i: (0, 0)), per] + [ANY_SPEC] * n_ride,
        out_shape=[_sds((tokens, D_MODEL), F32), _sds((8, D_MODEL), F32), _sds((n_seq, 8, D_MODEL), F32)]
        + _exchange_shapes(ride),
        scratch_shapes=_exchange_sems(n_ride),
        compiler_params=_params(("arbitrary",)),
    )(*parts, w_in, x2, dx1, mod8, g_mix, *[a for a, _ in ride])


def _grad_matmul_parts(a_parts, b_parts, name, tk=1024):
    tokens = a_parts[0].shape[0]
    na, nb = len(a_parts), len(b_parts)
    ma, nbw = a_parts[0].shape[1], b_parts[0].shape[1]

    n_k = tokens // tk

    def body(*refs):
        a_refs, b_refs, o_ref, acc = refs[:na], refs[na:na + nb], refs[na + nb], refs[na + nb + 1]

        @pl.when(pl.program_id(0) == 0)
        def _():
            acc[...] = jnp.zeros_like(acc)

        for i in range(na):
            for j in range(nb):
                acc[pl.ds(i * ma, ma), pl.ds(j * nbw, nbw)] += lax.dot_general(
                    a_refs[i][...], b_refs[j][...], TN, preferred_element_type=F32)

        @pl.when(pl.program_id(0) == n_k - 1)
        def _():
            o_ref[...] = acc[...].astype(o_ref.dtype)

    return _call(
        body, name=name, grid=(n_k,),
        in_specs=[pl.BlockSpec((tk, ma), lambda k: (k, 0))] * na + [pl.BlockSpec((tk, nbw), lambda k: (k, 0))] * nb,
        out_specs=pl.BlockSpec((na * ma, nb * nbw), lambda k: (0, 0)),
        out_shape=_sds((na * ma, nb * nbw), BF16),
        scratch_shapes=[pltpu.VMEM((na * ma, nb * nbw), F32)],
        compiler_params=_params(("arbitrary",)),
    )(*a_parts, *b_parts)


def _grad_matmul_tiles(a, b, name, tk=1024):
    tiled_b = b.ndim == 3
    tiles, tokens, width = b.shape if tiled_b else a.shape
    other = a.shape[1] if tiled_b else b.shape[1]
    out_tile = (other, width) if tiled_b else (width, other)
    n_k = tokens // tk

    def body(a_ref, b_ref, o_ref, acc):
        @pl.when(pl.program_id(0) == 0)
        def _():
            acc[...] = jnp.zeros_like(acc)

        for t in range(tiles):
            lhs = a_ref[...] if tiled_b else a_ref[t]
            rhs = b_ref[t] if tiled_b else b_ref[...]
            acc[t] += lax.dot_general(lhs, rhs, TN, preferred_element_type=F32)

        @pl.when(pl.program_id(0) == n_k - 1)
        def _():
            o_ref[...] = acc[...].astype(o_ref.dtype)

    flat = pl.BlockSpec((tk, other), lambda k: (k, 0))
    tiled = pl.BlockSpec((tiles, tk, width), lambda k: (0, k, 0))
    return _call(
        body, name=name, grid=(n_k,),
        in_specs=[flat, tiled] if tiled_b else [tiled, flat],
        out_specs=pl.BlockSpec((tiles,) + out_tile, lambda k: (0, 0, 0)),
        out_shape=_sds((tiles,) + out_tile, BF16),
        scratch_shapes=[pltpu.VMEM((tiles,) + out_tile, F32)],
        compiler_params=_params(("arbitrary",)),
    )(a, b)


def _adamw(w, m, v, g, name, n_parts=0, tr=256):
    rows, cols = w.shape
    tr = min(tr, rows)
    c1 = 1.0 - ADAM_B1 ** ADAM_STEP
    c2 = 1.0 - ADAM_B2 ** ADAM_STEP

    def body(w_ref, m_ref, v_ref, g_ref, go_ref, d_ref, mo_ref, vo_ref):
        if n_parts:
            gv = g_ref[0].astype(F32)
            for p in range(1, n_parts):
                gv = gv + g_ref[p].astype(F32)
        else:
            gv = g_ref[...]
        go_ref[...] = gv
        mn = ADAM_B1 * m_ref[...] + (1.0 - ADAM_B1) * gv
        vn = ADAM_B2 * v_ref[...] + (1.0 - ADAM_B2) * (gv * gv)
        mo_ref[...] = mn
        vo_ref[...] = vn
        d_ref[...] = -ADAM_LR * ((mn / c1) / (jnp.sqrt(vn / c2) + ADAM_EPS) + ADAM_WD * w_ref[...])

    blk = pl.BlockSpec((tr, cols), lambda i: (i, 0))
    g_spec = pl.BlockSpec((n_parts, tr, cols), lambda i: (0, i, 0)) if n_parts else blk
    return _call(
        body, name=name, grid=(rows // tr,),
        in_specs=[blk, blk, blk, g_spec], out_specs=[blk] * 4,
        out_shape=[_sds((rows, cols), F32)] * 4,
        compiler_params=_params(("parallel",)),
    )(w, m, v, g)


def _cols_to_full(blocks):
    n, r, c = blocks.shape
    return jnp.transpose(blocks, (1, 0, 2)).reshape(r, n * c)


def _full_to_cols(full, n=N_DEV):
    r, c = full.shape
    return jnp.transpose(full.reshape(r, n, c // n), (1, 0, 2))


def _cols_to_tiles(blocks, width):
    full = _cols_to_full(blocks)
    r, c = full.shape
    return jnp.transpose(full.reshape(r, c // width, width), (1, 0, 2))


def _tiles_to_cols(tiles, n=N_DEV):
    t, r, w = tiles.shape
    return _full_to_cols(jnp.transpose(tiles, (1, 0, 2)).reshape(r, t * w), n)


def _pad_lanes(v, width):
    return jnp.pad(v, ((0, 0), (0, width - v.shape[1])))


def kernel(x, c, w_ada, b_ada, g_mix, w_in, w_dw, b_dw, g_conv_ln, b_conv_ln, g_q, g_k, w_out, g_ffn, w_gate, w_up, w_down, loss_target, m_w_ada, m_b_ada, m_g_mix, m_w_in, m_w_dw, m_b_dw, m_g_conv_ln, m_b_conv_ln, m_g_q, m_g_k, m_w_out, m_g_ffn, m_w_gate, m_w_up, m_w_down, v_w_ada, v_b_ada, v_g_mix, v_w_in, v_w_dw, v_b_dw, v_g_conv_ln, v_b_conv_ln, v_g_q, v_g_k, v_w_out, v_g_ffn, v_w_gate, v_w_up, v_w_down):
    n_seq, seq, _ = x.shape
    tokens = n_seq * seq
    me = 4 * lax.axis_index("x") + 2 * lax.axis_index("y") + lax.axis_index("c")
    ada_cols = w_ada.shape[2]
    dw_cols = w_dw.shape[2]

    (c_g, w_in_g, w_dw_g) = _gather_by_chip([c, w_in[0].astype(BF16), w_dw[0]], "gather_weights")
    c_all = c_g.reshape(N_DEV * n_seq, D_MODEL)
    w_in_f = _cols_to_full(w_in_g)
    w_dw_f = _cols_to_full(w_dw_g)

    b_cols = lax.dynamic_slice(b_ada, (0, me * ada_cols), (1, ada_cols))
    mod_cols = _ada_fwd(c_all, w_ada[0], b_cols)
    (mod_g,) = _exchange([(mod_cols, False)], "gather_mod")
    mod_mine = lax.dynamic_slice(mod_g, (0, me * n_seq, 0), (N_DEV, n_seq, ada_cols))
    mod = jnp.transpose(mod_mine, (1, 0, 2)).reshape(n_seq, N_MOD, D_MODEL)
    mod8 = jnp.pad(mod, ((0, 0), (0, 8 - N_MOD), (0, 0)))

    x2 = x.reshape(tokens, D_MODEL)
    h1, proj = _mix_in(x2, mod8, g_mix, w_in_f, seq)
    proj3 = proj.reshape(D_IN // LANES, n_seq, seq, LANES)
    uc3 = _conv_fwd(proj3, w_dw_f, b_dw)
    g_q2, g_k2 = jnp.tile(g_q, (1, 2)), jnp.tile(g_k, (1, 2))
    y_att3, lse3, w_out_g, w_gate_g, w_up_g, w_down_g = _attn_fwd(
        proj3, g_q2, g_k2,
        [(w_out[0].astype(BF16), False), (w_gate[0].astype(BF16), False), (w_up[0].astype(BF16), False),
         (w_down[0].astype(BF16), False)])
    w_out_f = w_out_g.reshape(D_MODEL, D_MODEL)
    w_gate_f = _cols_to_tiles(w_gate_g, FF_TILE)
    w_up_f = _cols_to_tiles(w_up_g, FF_TILE)
    w_down_f = w_down_g.reshape(D_FF, D_MODEL)
    uc2 = uc3.reshape(tokens, D_CONV)
    y_att2 = y_att3.reshape(tokens, D_ATT)
    y_conv, mix, x1, h2 = _mix_out(uc2, y_att2, x2, mod8, g_conv_ln, b_conv_ln, g_ffn, w_out_f, seq)
    gate, up, dy, df, sq, dgate_f = _ffn_fwd(
        h2, w_gate_f, w_up_f, w_down_f, x1, loss_target.reshape(tokens, D_MODEL), mod8, seq)

    dgate, dup, act, dx1, dmix, dg_ffn, dmod_f = _ffn_bwd(
        df, gate, up, w_gate_f, w_up_f, w_down_f, x1, dy, mix, mod8, g_ffn, seq)
    duc2, do2, dgb_ln = _mix_out_bwd(dmix, uc2, g_conv_ln, b_conv_ln, w_out_f)
    d_a3, d_g3, dw_dw_p, db_dw_p = _conv_bwd(duc2.reshape(n_seq, seq, D_CONV), proj3, w_dw_f)
    gw_gate = _grad_matmul_tiles(h2, dgate, "grad_w_gate")
    gw_up = _grad_matmul_tiles(h2, dup, "grad_w_up")
    gw_down = _grad_matmul_tiles(act, df, "grad_w_down")
    gw_out = _grad_matmul_parts([y_conv, y_att2], [dmix], "grad_w_out")
    d_q3, d_k3, d_v3, dg_qk, p_gate, p_up, p_down, p_out = _attn_bwd(
        proj3, do2.reshape(n_seq, seq, D_ATT), y_att3, lse3, g_q2, g_k2,
        [(_tiles_to_cols(gw_gate), True), (_tiles_to_cols(gw_up), True),
         (gw_down.reshape(N_DEV, D_FF // N_DEV, D_MODEL), True),
         (gw_out.astype(BF16).reshape(N_DEV, D_MODEL // N_DEV, D_MODEL), True)])
    flat = lambda t: t.reshape(tokens, t.shape[-1])
    d_a, d_g, d_q, d_k, d_v = flat(d_a3), flat(d_g3), flat(d_q3), flat(d_k3), flat(d_v3)
    gw_in = _grad_matmul_parts([h1], [d_a, d_g, d_q, d_k, d_v], "grad_w_in")
    grad_x2, dg_mix, dmod_m, p_in = _mix_in_bwd(
        d_a, d_g, d_q, d_k, d_v, w_in_f, x2, dx1, mod8, g_mix, seq, [(_full_to_cols(gw_in).astype(BF16), True)])

    dmod = jnp.concatenate([dmod_m[:, 0], dmod_m[:, 1], dmod_f[:, 2], dmod_f[:, 0], dmod_f[:, 1], dgate_f[:, 0]], axis=1)
    dg_q = dg_qk[0:1, 0:HEAD_DIM] + dg_qk[0:1, HEAD_DIM:]
    dg_k = dg_qk[1:2, 0:HEAD_DIM] + dg_qk[1:2, HEAD_DIM:]
    loss_part = (0.5 / D_MODEL) * jnp.sum(sq[0:1, :], axis=1, keepdims=True)
    small = jnp.concatenate(
        [dg_mix[0:1], dg_ffn[0:1], db_dw_p[0:1], dgb_ln[0:1], dgb_ln[1:2],
         _pad_lanes(dg_q, LANES), _pad_lanes(dg_k, LANES), _pad_lanes(loss_part, LANES)], axis=1)
    n_small = small.shape[1] - LANES

    (dmod_g, small_g, dw_g) = _exchange([(dmod, False), (small, False), (dw_dw_p, False)], "gather_small_grads")

    dmod_all = dmod_g.reshape(N_DEV * n_seq, N_MOD * D_MODEL)
    dmod_cols = lax.dynamic_slice(dmod_all, (0, me * ada_cols), (N_DEV * n_seq, ada_cols))
    gw_ada, gb_ada = _ada_bwd(c_all, dmod_cols, dmod_all)

    res = {}
    res["w_ada"] = _adamw(w_ada[0], m_w_ada[0], v_w_ada[0], gw_ada, "adamw_w_ada")
    res["b_ada"] = _adamw(b_ada, m_b_ada, v_b_ada, gb_ada, "adamw_b_ada")
    res["w_in"] = _adamw(w_in[0], m_w_in[0], v_w_in[0], p_in, "adamw_w_in", N_DEV)
    res["w_out"] = _adamw(w_out[0], m_w_out[0], v_w_out[0], p_out, "adamw_w_out", N_DEV)
    res["w_gate"] = _adamw(w_gate[0], m_w_gate[0], v_w_gate[0], p_gate, "adamw_w_gate", N_DEV)
    res["w_up"] = _adamw(w_up[0], m_w_up[0], v_w_up[0], p_up, "adamw_w_up", N_DEV)
    res["w_down"] = _adamw(w_down[0], m_w_down[0], v_w_down[0], p_down, "adamw_w_down", N_DEV, tr=176)
    dw_mine = lax.dynamic_slice(dw_g, (0, 0, me * dw_cols), (N_DEV, CONV_WIDTH, dw_cols))
    res["w_dw"] = _adamw(w_dw[0], m_w_dw[0], v_w_dw[0], dw_mine, "adamw_w_dw", N_DEV)

    small_names = ["g_mix", "g_ffn", "b_dw", "g_conv_ln", "b_conv_ln", "g_q", "g_k"]
    small_w = {"g_mix": (g_mix, m_g_mix, v_g_mix), "g_ffn": (g_ffn, m_g_ffn, v_g_ffn), "b_dw": (b_dw, m_b_dw, v_b_dw),
               "g_conv_ln": (g_conv_ln, m_g_conv_ln, v_g_conv_ln), "b_conv_ln": (b_conv_ln, m_b_conv_ln, v_b_conv_ln),
               "g_q": (g_q, m_g_q, v_g_q), "g_k": (g_k, m_g_k, v_g_k)}
    widths = [max(small_w[n][0].shape[1], LANES) for n in small_names]
    packed = [jnp.concatenate([_pad_lanes(small_w[n][i], wd) for n, wd in zip(small_names, widths)], axis=1) for i in range(3)]
    outs = _adamw(packed[0], packed[1], packed[2], small_g[:, :, :n_small], "adamw_small", N_DEV)
    off = 0
    for n, wd in zip(small_names, widths):
        real = small_w[n][0].shape[1]
        res[n] = tuple(o[:, off:off + real] for o in outs)
        off += wd
    loss = jnp.sum(small_g[:, 0, n_small])

    order = ["w_ada", "b_ada", "g_mix", "w_in", "w_dw", "b_dw", "g_conv_ln", "b_conv_ln", "g_q", "g_k",
             "w_out", "g_ffn", "w_gate", "w_up", "w_down"]
    lead = {"w_ada", "w_in", "w_dw", "w_out", "w_gate", "w_up", "w_down"}
    grads, deltas, new_m, new_v = [], [], [], []
    for n in order:
        g, d, mn, vn = res[n]
        g, d, mn, vn = (t[None] if n in lead else t for t in (g, d, mn, vn))
        grads.append(g)
        deltas.append(d)
        new_m.append(mn)
        new_v.append(vn)
    return (loss, grad_x2.reshape(n_seq, seq, D_MODEL), *grads, *deltas, *new_m, *new_v)
```

```python
import numpy as np
import jax
import jax.numpy as jnp
from jax import lax
from jax.experimental import pallas as pl
from jax.experimental.pallas import tpu as pltpu

F32 = jnp.float32
BF16 = jnp.bfloat16

N_DEV = 8
D_MODEL = 1024
D_CONV = 512
D_ATT = 512
HEAD_DIM = 64
CONV_WIDTH = 31
D_IN = 2 * D_CONV + 3 * D_ATT
D_FF = 2816
N_MOD = 6
EPS = 1e-6
RADIUS = 64
DILATIONS = (1, 4, 16)
Q_BLOCK = 128
LANES = 128
VMEM_LIMIT = 56 * 1024 * 1024

ADAM_LR = 0.001
ADAM_B1 = 0.9
ADAM_B2 = 0.999
ADAM_EPS = 1e-08
ADAM_WD = 0.01
ADAM_STEP = 10

NT = (((1,), (1,)), ((), ()))
TN = (((0,), (0,)), ((), ()))


def _call(body, **kw):
    return pl.pallas_call(body, **kw)


def _params(sem=None, vmem=VMEM_LIMIT):
    return pltpu.CompilerParams(dimension_semantics=sem, vmem_limit_bytes=vmem)


def _sig(x):
    return 1.0 / (1.0 + jnp.exp(-x))


def _sds(shape, dtype):
    return jax.ShapeDtypeStruct(shape, dtype)


N_PEER = N_DEV - 1
ANY_SPEC = pl.BlockSpec(memory_space=pl.ANY)


def _exchange_copies(scatter, ins, outs, *sems):
    n = len(ins)
    if n == 0:
        return [], []
    send_sems, recv_sems, local_sems = sems
    x, y, c = lax.axis_index("x"), lax.axis_index("y"), lax.axis_index("c")
    me = 4 * x + 2 * y + c

    def src(a, slot):
        return ins[a].at[slot] if scatter[a] else ins[a]

    local = [pltpu.make_async_copy(src(a, me), outs[a].at[me], local_sems.at[a]) for a in range(n)]
    flights = []
    for k in range(1, N_DEV):
        px = 1 - x if k & 4 else x
        py = 1 - y if k & 2 else y
        pc = 1 - c if k & 1 else c
        pid = 4 * px + 2 * py + pc
        for a in range(n):
            i = a * N_PEER + k - 1
            send, recv = (pltpu.make_async_remote_copy(
                src_ref=src(a, pid), dst_ref=outs[a].at[slot],
                send_sem=send_sems.at[i], recv_sem=recv_sems.at[i],
                device_id=(px, py, pc), device_id_type=pl.DeviceIdType.MESH) for slot in (me, pid))
            flights.append((send, recv))
    return local, flights


def _exchange_start(*args):
    local, flights = _exchange_copies(*args)
    for cp in local:
        cp.start()
    for send, _ in flights:
        send.start()


def _exchange_wait(*args):
    local, flights = _exchange_copies(*args)
    for send, recv in flights:
        send.wait_send()
        recv.wait_recv()
    for cp in local:
        cp.wait()


def _exchange_shapes(items):
    return [_sds((N_DEV,) + tuple(arr.shape[1:] if scatter else arr.shape), arr.dtype) for arr, scatter in items]


def _exchange_sems(n):
    if n == 0:
        return []
    return [pltpu.SemaphoreType.DMA((n * N_PEER,)), pltpu.SemaphoreType.DMA((n * N_PEER,)),
            pltpu.SemaphoreType.DMA((n,))]


def _gather_by_chip_phase(phase, ins, outs, send_sems, recv_sems, local_sems):
    n = len(ins)
    per = N_PEER
    x, y, c = lax.axis_index("x"), lax.axis_index("y"), lax.axis_index("c")
    me, sibling = (x, y, c), (x, y, 1 - c)
    chips = [(1 - x, y), (x, 1 - y), (1 - x, 1 - y)]

    def slot(px, py, pc):
        return 4 * px + 2 * py + pc

    def copy(a, k, block, to, src=None):
        dst = outs[a].at[slot(*block)]
        return pltpu.make_async_remote_copy(
            src_ref=dst if src is None else src, dst_ref=dst,
            send_sem=send_sems.at[a * per + k], recv_sem=recv_sems.at[a * per + k],
            device_id=to, device_id_type=pl.DeviceIdType.MESH)

    local = [pltpu.make_async_copy(ins[a], outs[a].at[slot(*me)], local_sems.at[a]) for a in range(n)]
    first = []
    for a in range(n):
        first.append(copy(a, 0, me, sibling, src=ins[a]))
        first += [copy(a, 1 + j, me, (*chip, c), src=ins[a]) for j, chip in enumerate(chips)]
    passed = [copy(a, 4 + j, (*chip, c), sibling) for j, chip in enumerate(chips) for a in range(n)]
    if phase == 0:
        for cp in local + first:
            cp.start()
    elif phase == 1:
        for j, chip in enumerate(chips):
            for a in range(n):
                copy(a, 1 + j, (*chip, c), me).wait_recv()
        for cp in passed:
            cp.start()
    else:
        for a in range(n):
            copy(a, 0, sibling, me).wait_recv()
            for j, chip in enumerate(chips):
                copy(a, 4 + j, (*chip, 1 - c), me).wait_recv()
        for cp in first + passed:
            cp.wait_send()
        for cp in local:
            cp.wait()


def _gather_by_chip(arrays, name):
    n = len(arrays)

    def body(*refs):
        for phase in range(3):
            _gather_by_chip_phase(phase, refs[:n], refs[n:2 * n], *refs[2 * n:])

    return _call(
        body, name=name, out_shape=_exchange_shapes([(arr, False) for arr in arrays]),
        in_specs=[ANY_SPEC] * n, out_specs=[ANY_SPEC] * n, scratch_shapes=_exchange_sems(n),
    )(*arrays)


def _exchange(items, name):
    n = len(items)
    scatter = [s for _, s in items]

    def body(*refs):
        args = (scatter, refs[:n], refs[n:2 * n]) + tuple(refs[2 * n:])
        _exchange_start(*args)
        _exchange_wait(*args)

    return _call(
        body, name=name, out_shape=_exchange_shapes(items),
        in_specs=[ANY_SPEC] * n, out_specs=[ANY_SPEC] * n, scratch_shapes=_exchange_sems(n),
    )(*[a for a, _ in items])


def _ada_fwd(c_all, w_ada, b_cols):
    def body(c_ref, w_ref, b_ref, o_ref):
        cv = c_ref[...]
        sc = (cv * _sig(cv)).astype(BF16)
        o_ref[...] = jnp.dot(sc, w_ref[...].astype(BF16), preferred_element_type=F32) + b_ref[...]

    return _call(body, name="ada_fwd", out_shape=_sds((c_all.shape[0], w_ada.shape[1]), F32),
                 compiler_params=_params())(c_all, w_ada, b_cols)


def _ada_bwd(c_all, dmod_cols, dmod_all):
    def body(c_ref, dc_ref, da_ref, gw_ref, gb_ref):
        cv = c_ref[...]
        sc = (cv * _sig(cv)).astype(BF16)
        gw_ref[...] = lax.dot_general(sc, dc_ref[...].astype(BF16), TN, preferred_element_type=F32)
        gb_ref[...] = jnp.sum(da_ref[...], axis=0, keepdims=True)

    return _call(body, name="ada_bwd",
                 out_shape=[_sds((c_all.shape[1], dmod_cols.shape[1]), F32), _sds((1, dmod_all.shape[1]), F32)],
                 compiler_params=_params())(c_all, dmod_cols, dmod_all)


MIX_ROWS = 128


def _mix_in(x2, mod8, g_mix, w_in, seq, tm=512):
    tokens = x2.shape[0]
    per_seq = seq // tm

    def body(x_ref, m_ref, g_ref, w_ref, h_ref, p_ref):
        def normed(c):
            rows = pl.ds(c * MIX_ROWS, MIX_ROWS)
            xv = x_ref[rows, :]
            r = lax.rsqrt(jnp.mean(xv * xv, axis=-1, keepdims=True) + EPS)
            hb = ((xv * r * g_ref[...]) * (1.0 + m_ref[1:2, :]) + m_ref[0:1, :]).astype(BF16)
            h_ref[rows, :] = hb
            return hb

        ahead = normed(0)
        for c in range(tm // MIX_ROWS):
            hb = ahead
            if c + 1 < tm // MIX_ROWS:
                ahead = normed(c + 1)
            p = lax.dot_general(hb, w_ref[...], NT, preferred_element_type=F32)
            for cb in range(D_IN // LANES):
                p_ref[cb, pl.ds(c * MIX_ROWS, MIX_ROWS), :] = p[:, cb * LANES:(cb + 1) * LANES]

    return _call(
        body, name="mix_in", grid=(tokens // tm,),
        in_specs=[pl.BlockSpec((tm, D_MODEL), lambda i: (i, 0)),
                  pl.BlockSpec((None, 8, D_MODEL), lambda i: (i // per_seq, 0, 0)),
                  pl.BlockSpec((1, D_MODEL), lambda i: (0, 0)),
                  pl.BlockSpec((D_IN, D_MODEL), lambda i: (0, 0))],
        out_specs=[pl.BlockSpec((tm, D_MODEL), lambda i: (i, 0)),
                   pl.BlockSpec((D_IN // LANES, tm, LANES), lambda i: (0, i, 0))],
        out_shape=[_sds((tokens, D_MODEL), BF16), _sds((D_IN // LANES, tokens, LANES), F32)],
        compiler_params=_params(("parallel",)),
    )(x2, mod8, g_mix, w_in)


CONV_ROWS = 64
CONV_DW_ROWS = 32
CONV_DW_UNROLL = 4
CONV_HALO = 16


def _fill_shifted(xp, sh, seq):
    for b in range(8):
        sh[b, pl.ds(0, seq + 24), :] = xp[pl.ds(b, seq + 24), :]


def _conv_fwd(proj3, w_dw, b_dw):
    _, n_seq, seq, _ = proj3.shape
    n_cb = D_CONV // LANES

    def body(a_ref, g_ref, w_ref, b_ref, uc_ref, xp, sh):
        zeros = jnp.zeros((CONV_HALO, LANES), F32)
        xp[pl.ds(0, CONV_HALO), :] = zeros
        xp[pl.ds(CONV_HALO + seq, CONV_HALO), :] = zeros
        xp[pl.ds(CONV_HALO, seq), :] = a_ref[...] * _sig(g_ref[...])
        _fill_shifted(xp, sh, seq)

        def blk(i, carry):
            t0 = pl.multiple_of(i * CONV_ROWS, CONV_ROWS)
            acc = jnp.zeros((CONV_ROWS, LANES), F32)
            for j in range(CONV_WIDTH):
                jj = j + 1
                acc = acc + sh[jj % 8, pl.ds(t0 + 8 * (jj // 8), CONV_ROWS), :] * w_ref[j:j + 1, :]
            uc_ref[pl.ds(t0, CONV_ROWS), :] = acc + b_ref[...]
            return carry

        lax.fori_loop(0, seq // CONV_ROWS, blk, 0)

    return _call(
        body, name="conv_fwd", grid=(n_seq, n_cb),
        in_specs=[pl.BlockSpec((None, None, seq, LANES), lambda b, cb: (cb, b, 0, 0)),
                  pl.BlockSpec((None, None, seq, LANES), lambda b, cb: (n_cb + cb, b, 0, 0)),
                  pl.BlockSpec((CONV_WIDTH, LANES), lambda b, cb: (0, cb)),
                  pl.BlockSpec((1, LANES), lambda b, cb: (0, cb))],
        out_specs=pl.BlockSpec((None, seq, LANES), lambda b, cb: (b, 0, cb)),
        out_shape=_sds((n_seq, seq, D_CONV), F32),
        scratch_shapes=[pltpu.VMEM((seq + 2 * CONV_HALO, LANES), F32),
                        pltpu.VMEM((8, seq + 2 * CONV_HALO, LANES), F32)],
        compiler_params=_params(("parallel", "parallel")),
    )(proj3, proj3, w_dw, b_dw)


def _conv_bwd(duc3, proj3, w_dw):
    _, n_seq, seq, _ = proj3.shape
    n_cb = D_CONV // LANES

    def body(duc_ref, a_ref, g_ref, w_ref, da_ref, dg_ref, dw_ref, db_ref, xp, sh):
        @pl.when(pl.program_id(1) == 0)
        def _():
            dw_ref[...] = jnp.zeros_like(dw_ref)
            db_ref[...] = jnp.zeros_like(db_ref)

        zeros = jnp.zeros((CONV_HALO, LANES), F32)
        xp[pl.ds(0, CONV_HALO), :] = zeros
        xp[pl.ds(CONV_HALO + seq, CONV_HALO), :] = zeros
        xp[pl.ds(CONV_HALO, seq), :] = a_ref[...] * _sig(g_ref[...])
        _fill_shifted(xp, sh, seq)
        for j0 in range(0, CONV_WIDTH, 8):
            taps = range(j0, min(j0 + 8, CONV_WIDTH))

            def wblk(i, accs, taps=taps):
                for u in range(CONV_DW_UNROLL):
                    t0 = pl.multiple_of((i * CONV_DW_UNROLL + u) * CONV_DW_ROWS, CONV_DW_ROWS)
                    d = duc_ref[pl.ds(t0, CONV_DW_ROWS), :]
                    accs = tuple(acc + d * sh[(j + 1) % 8, pl.ds(t0 + 8 * ((j + 1) // 8), CONV_DW_ROWS), :]
                                 for acc, j in zip(accs, taps))
                return accs

            accs = lax.fori_loop(0, seq // (CONV_DW_ROWS * CONV_DW_UNROLL), wblk,
                                 tuple(jnp.zeros((CONV_DW_ROWS, LANES), F32) for _ in taps))
            for acc, j in zip(accs, taps):
                dw_ref[j:j + 1, :] += jnp.sum(acc, axis=0, keepdims=True)
        db_ref[0:1, :] += jnp.sum(duc_ref[...], axis=0, keepdims=True)
        xp[pl.ds(CONV_HALO, seq), :] = duc_ref[...]
        _fill_shifted(xp, sh, seq)

        def ublk(i, carry):
            t0 = pl.multiple_of(i * CONV_ROWS, CONV_ROWS)
            acc = jnp.zeros((CONV_ROWS, LANES), F32)
            for j in range(CONV_WIDTH):
                jj = CONV_WIDTH - j
                acc = acc + sh[jj % 8, pl.ds(t0 + 8 * (jj // 8), CONV_ROWS), :] * w_ref[j:j + 1, :]
            av = a_ref[pl.ds(t0, CONV_ROWS), :]
            sg = _sig(g_ref[pl.ds(t0, CONV_ROWS), :])
            da_ref[pl.ds(t0, CONV_ROWS), :] = (acc * sg).astype(BF16)
            dg_ref[pl.ds(t0, CONV_ROWS), :] = (acc * av * sg * (1.0 - sg)).astype(BF16)
            return carry

        lax.fori_loop(0, seq // CONV_ROWS, ublk, 0)

    return _call(
        body, name="conv_bwd", grid=(n_cb, n_seq),
        in_specs=[pl.BlockSpec((None, seq, LANES), lambda cb, b: (b, 0, cb)),
                  pl.BlockSpec((None, None, seq, LANES), lambda cb, b: (cb, b, 0, 0)),
                  pl.BlockSpec((None, None, seq, LANES), lambda cb, b: (n_cb + cb, b, 0, 0)),
                  pl.BlockSpec((CONV_WIDTH, LANES), lambda cb, b: (0, cb))],
        out_specs=[pl.BlockSpec((None, seq, LANES), lambda cb, b: (b, 0, cb)),
                   pl.BlockSpec((None, seq, LANES), lambda cb, b: (b, 0, cb)),
                   pl.BlockSpec((32, LANES), lambda cb, b: (0, cb)),
                   pl.BlockSpec((8, LANES), lambda cb, b: (0, cb))],
        out_shape=[_sds((n_seq, seq, D_CONV), BF16), _sds((n_seq, seq, D_CONV), BF16),
                   _sds((32, D_CONV), F32), _sds((8, D_CONV), F32)],
        scratch_shapes=[pltpu.VMEM((seq + 2 * CONV_HALO, LANES), F32),
                        pltpu.VMEM((8, seq + 2 * CONV_HALO, LANES), F32)],
        compiler_params=_params(("parallel", "arbitrary")),
    )(duc3, proj3, proj3, w_dw)


MASKED = 1e30
ATT_ROWS = 512
ATT_UNROLL = 8
ATT_FWD_UNROLL = 8


def _distance_mats(dil, seg_len):
    kw = min(2 * Q_BLOCK, seg_len)
    offsets = (0, -RADIUS, -2 * RADIUS) if kw == 2 * Q_BLOCK else (0,)
    a = np.arange(Q_BLOCK)[:, None]
    b = np.arange(kw)[None, :]
    mats = []
    for off in offsets:
        rel = np.abs(b + off - a)
        mats.append(np.where(rel <= RADIUS, dil * rel, MASKED))
    return jnp.asarray(np.stack(mats).astype(np.float32))


def _alibi_rows():
    s = np.zeros((4, 8, LANES), np.float32)
    for hp in range(4):
        for hl in range(2):
            s[hp, hl, :] = 2.0 ** (-(2 * hp + hl + 1))
    return jnp.asarray(s)


def _window(n, seg_len):
    i0 = pl.multiple_of(n * Q_BLOCK, Q_BLOCK)
    if seg_len <= Q_BLOCK:
        return i0, i0, 0
    per_seg = seg_len // Q_BLOCK
    j = n % per_seg
    seg0 = (n // per_seg) * seg_len
    ks_local = jnp.clip(j * Q_BLOCK - RADIUS, 0, seg_len - 2 * Q_BLOCK)
    ks = pl.multiple_of(seg0 + ks_local, RADIUS)
    var = jnp.where(j == 0, 0, jnp.where(j == per_seg - 1, 2, 1))
    return i0, ks, var


def _first_head(rows):
    return lax.broadcasted_iota(jnp.int32, (rows, LANES), 1) < HEAD_DIM


def _same_head():
    head = np.arange(LANES) // HEAD_DIM
    return jnp.asarray((head[:, None] == head[None, :]).astype(np.float32)).astype(BF16)


def _head_sum(x, same_ref):
    hi = x.astype(BF16)
    lo = (x - hi.astype(F32)).astype(BF16)
    return (jnp.dot(hi, same_ref[...], preferred_element_type=F32)
            + jnp.dot(lo, same_ref[...], preferred_element_type=F32))


def _head_mean(x, same_ref):
    return _head_sum(x, same_ref) * (1.0 / HEAD_DIM)


def _per_head(x, first):
    swapped = pltpu.roll(x, HEAD_DIM, 1)
    return jnp.where(first, x, swapped), jnp.where(first, swapped, x)


STRIDE = 4


def _gather_segments(src, dil, seq, tmp, put):
    if dil == 1:
        put(0, seq, src[pl.ds(0, seq), :])
    elif dil == STRIDE:
        seg = seq // dil
        for r in range(dil):
            put(r * seg, seg, src[pl.ds(r, seg, stride=dil), :])
    else:
        part, seg = seq // STRIDE, seq // dil
        for b in range(STRIDE):
            tmp[pl.ds(b * part, part), :] = src[pl.ds(b, part, stride=STRIDE), :]
        for b in range(STRIDE):
            for a in range(dil // STRIDE):
                put(b * part + a * seg, seg, tmp[pl.ds(b * part + a, seg, stride=dil // STRIDE), :])


def _scatter_segments(dst, get, dil, seq, tmp, accumulate):
    def write(rows, val):
        if accumulate:
            dst[rows, :] += val
        else:
            dst[rows, :] = val

    if dil == 1:
        write(pl.ds(0, seq), get(0, seq))
    elif dil == STRIDE:
        seg = seq // dil
        for r in range(dil):
            write(pl.ds(r, seg, stride=dil), get(r * seg, seg))
    else:
        part, seg = seq // STRIDE, seq // dil
        for b in range(STRIDE):
            for a in range(dil // STRIDE):
                tmp[pl.ds(b * part + a, seg, stride=dil // STRIDE), :] = get(b * part + a * seg, seg)
        for b in range(STRIDE):
            write(pl.ds(b, part, stride=STRIDE), tmp[pl.ds(b * part, part), :])


def _permute_rows(dst, src, dil, seq, tmp):
    def put(start, size, val):
        dst[pl.ds(start, size), :] = val.astype(dst.dtype)

    _gather_segments(src, dil, seq, tmp, put)


def _permute_rows_by_head(dst, src, dil, seq, tmp):
    def put(start, size, val):
        first = _first_head(size)
        dst[0, pl.ds(start, size), :] = jnp.where(first, val, 0.0).astype(dst.dtype)
        dst[1, pl.ds(start, size), :] = jnp.where(first, 0.0, val).astype(dst.dtype)

    _gather_segments(src, dil, seq, tmp, put)


def _qk_normalise(q_ref, g2_ref, same_ref, dst, seq, scale):
    def chunk(ci, carry):
        rows = pl.ds(pl.multiple_of(ci * ATT_ROWS, ATT_ROWS), ATT_ROWS)
        qv = q_ref[rows, :]
        r = lax.rsqrt(_head_mean(qv * qv, same_ref) + EPS)
        dst[rows, :] = qv * r * (g2_ref[...] * scale)
        return carry

    lax.fori_loop(0, seq // ATT_ROWS, chunk, 0)


def _attn_fwd(proj3, g_q2, g_k2, ride):
    _, n_seq, seq, _ = proj3.shape
    dms = [_distance_mats(d, seq // d) for d in DILATIONS]
    same = _same_head()
    col0 = 2 * D_CONV // LANES
    n_hp = D_ATT // LANES

    n_ride = len(ride)
    assert not any(scatter for _, scatter in ride), "the forward's ride is an all-gather"

    def body(*refs):
        q_ref, k_ref, v_ref, gq_ref, gk_ref, sl_ref, dm1, dm4, dm16, same_ref = refs[:10]
        ride_in = refs[10:10 + n_ride]
        y_ref, lse_ref = refs[10 + n_ride:12 + n_ride]
        ride_out = refs[12 + n_ride:12 + 2 * n_ride]
        (qf, kf, qp, kp, vp, oml_p, o1, o4, o16, m1, m4, m16, l1, l4, l16,
         tmp) = refs[12 + 2 * n_ride:28 + 2 * n_ride]
        o_nat, m_nat, l_nat = (o1, o4, o16), (m1, m4, m16), (l1, l4, l16)
        ride_args = (ride_in, ride_out) + tuple(refs[28 + 2 * n_ride:])
        step = pl.program_id(0) * n_hp + pl.program_id(1)
        n_steps = n_seq * n_hp

        if n_ride:
            for phase, at in enumerate((0, (3 * n_steps) // 4)):
                @pl.when(step == at)
                def _(phase=phase):
                    _gather_by_chip_phase(phase, *ride_args)

        dm_refs = (dm1, dm4, dm16)
        _qk_normalise(q_ref, gq_ref, same_ref, qf, seq, HEAD_DIM ** -0.5)
        _qk_normalise(k_ref, gk_ref, same_ref, kf, seq, 1.0)
        slopes = (sl_ref[0:1, 0:1], sl_ref[1:2, 0:1])
        for pi, dil in enumerate(DILATIONS):
            seg = seq // dil
            kw = min(2 * Q_BLOCK, seg)
            _permute_rows_by_head(qp, qf, dil, seq, tmp)
            _permute_rows(kp, kf, dil, seq, tmp)
            _permute_rows(vp, v_ref, dil, seq, tmp)

            def blk(it, carry, seg=seg, kw=kw, pi=pi, dst=oml_p):
                first = _first_head(Q_BLOCK)
                chains = [(sub, h) for sub in range(ATT_FWD_UNROLL) for h in range(2)]
                win = [_window(it * ATT_FWD_UNROLL + sub, seg) for sub in range(ATT_FWD_UNROLL)]
                s = {}
                for sub, h in chains:
                    i0, ks, var = win[sub]
                    s[sub, h] = lax.dot_general(qp[h, pl.ds(i0, Q_BLOCK), :], kp[pl.ds(ks, kw), :], NT,
                                                preferred_element_type=F32) - slopes[h] * dm_refs[pi][var]
                m, l, p = {}, {}, {}
                for c in chains:
                    m[c] = jnp.max(s[c], axis=1, keepdims=True)
                    e = jnp.exp(s[c] - m[c])
                    l[c] = jnp.sum(e, axis=1, keepdims=True)
                    p[c] = e.astype(BF16)
                o = {}
                for sub, h in chains:
                    o[sub, h] = jnp.dot(p[sub, h], vp[pl.ds(win[sub][1], kw), :], preferred_element_type=F32)
                packed = [jnp.concatenate([jnp.where(first, t[sub, 0], t[sub, 1]) for t in (o, m, l)], axis=1)
                          for sub in range(ATT_FWD_UNROLL)]
                span = ATT_FWD_UNROLL * Q_BLOCK
                dst[pl.ds(pl.multiple_of(it * span, span), span), :] = jnp.concatenate(packed, axis=0)
                return carry

            lax.fori_loop(0, seq // (Q_BLOCK * ATT_FWD_UNROLL), blk, 0)
            for n, nat in enumerate((o_nat[pi], m_nat[pi], l_nat[pi])):
                _scatter_segments(nat, lambda start, size, n=n: oml_p[pl.ds(start, size), pl.ds(n * LANES, LANES)],
                                  dil, seq, tmp, accumulate=False)

        def merge(ci, carry):
            rows = pl.ds(pl.multiple_of(ci * ATT_ROWS, ATT_ROWS), ATT_ROWS)
            ms = [m_nat[pi][rows, :] for pi in range(3)]
            m_all = jnp.maximum(jnp.maximum(ms[0], ms[1]), ms[2])
            es = [jnp.exp(m - m_all) for m in ms]
            l_all = sum(l_nat[pi][rows, :] * es[pi] for pi in range(3))
            inv = 1.0 / l_all
            o = sum(o_nat[pi][rows, :] * (es[pi] * inv) for pi in range(3))
            y_ref[rows, :] = o.astype(BF16)
            lse_ref[rows, :] = m_all + jnp.log(l_all)
            return carry

        lax.fori_loop(0, seq // ATT_ROWS, merge, 0)

        if n_ride:
            @pl.when(step == n_steps - 1)
            def _():
                _gather_by_chip_phase(2, *ride_args)

    def col(off):
        return pl.BlockSpec((None, None, seq, LANES), lambda b, hp: (col0 + off * n_hp + hp, b, 0, 0))

    def whole(arr):
        return pl.BlockSpec(arr.shape, lambda b, hp: (0,) * arr.ndim)

    rows_f32 = pltpu.VMEM((seq, LANES), F32)
    rows_bf16 = pltpu.VMEM((seq, LANES), BF16)
    return _call(
        body, name="attn_fwd", grid=(n_seq, n_hp),
        in_specs=[col(0), col(1), col(2), whole(g_q2), whole(g_k2),
                  pl.BlockSpec((None, 8, LANES), lambda b, hp: (hp, 0, 0)),
                  whole(dms[0]), whole(dms[1]), whole(dms[2]), whole(same)] + [ANY_SPEC] * n_ride,
        out_specs=[pl.BlockSpec((None, seq, LANES), lambda b, hp: (b, 0, hp)),
                   pl.BlockSpec((None, seq, LANES), lambda b, hp: (b, 0, hp))] + [ANY_SPEC] * n_ride,
        out_shape=[_sds((n_seq, seq, D_ATT), BF16), _sds((n_seq, seq, D_ATT), F32)] + _exchange_shapes(ride),
        scratch_shapes=[rows_f32, rows_f32, pltpu.VMEM((2, seq, LANES), BF16), rows_bf16, rows_bf16]
        + [pltpu.VMEM((seq, 3 * LANES), F32)] + [rows_f32] * 10 + _exchange_sems(n_ride),
        compiler_params=_params(("arbitrary", "arbitrary")),
    )(proj3, proj3, proj3, g_q2, g_k2, _alibi_rows(), *dms, same, *[a for a, _ in ride])


def _attn_bwd(proj3, do3, y_att3, lse3, g_q2, g_k2, ride):
    _, n_seq, seq, _ = proj3.shape
    dms = [_distance_mats(d, seq // d) for d in DILATIONS]
    same = _same_head()
    col0 = 2 * D_CONV // LANES
    n_hp = D_ATT // LANES

    n_ride = len(ride)
    ride_scatter = [s for _, s in ride]

    def body(*refs):
        (q_ref, k_ref, v_ref, do_ref, o_ref, lse_ref, gq_ref, gk_ref, sl_ref, dm1, dm4, dm16,
         same_ref) = refs[:13]
        ride_in = refs[13:13 + n_ride]
        dq_ref, dk_ref, dv_ref, dg_ref = refs[13 + n_ride:17 + n_ride]
        ride_out = refs[17 + n_ride:17 + 2 * n_ride]
        (qf, kf, qp, dop, kp, vp, sn, sp, dqp, dkp, dvp, dqn, dkn, dvn,
         tmp) = refs[17 + 2 * n_ride:32 + 2 * n_ride]
        ride_args = (ride_scatter, ride_in, ride_out) + tuple(refs[32 + 2 * n_ride:])
        dm_refs = (dm1, dm4, dm16)
        step = pl.program_id(0) * n_hp + pl.program_id(1)

        @pl.when(step == 0)
        def _():
            _exchange_start(*ride_args)
            dg_ref[...] = jnp.zeros_like(dg_ref)

        _qk_normalise(q_ref, gq_ref, same_ref, qf, seq, HEAD_DIM ** -0.5)
        _qk_normalise(k_ref, gk_ref, same_ref, kf, seq, 1.0)

        def stats(ci, carry):
            rows = pl.ds(pl.multiple_of(ci * ATT_ROWS, ATT_ROWS), ATT_ROWS)
            first = _first_head(ATT_ROWS)
            sn[0, rows, :], sn[1, rows, :] = _per_head(lse_ref[rows, :], first)
            prod = do_ref[rows, :] * o_ref[rows, :].astype(F32)
            sn[2, rows, :], sn[3, rows, :] = _per_head(_head_sum(prod, same_ref), first)
            return carry

        lax.fori_loop(0, seq // ATT_ROWS, stats, 0)
        slopes = (sl_ref[0:1, 0:1], sl_ref[1:2, 0:1])
        half = seq // (Q_BLOCK * ATT_UNROLL)
        region = seq // ATT_UNROLL

        for pi, dil in enumerate(DILATIONS):
            seg = seq // dil
            kw = min(2 * Q_BLOCK, seg)
            _permute_rows_by_head(qp, qf, dil, seq, tmp)
            _permute_rows_by_head(dop, do_ref, dil, seq, tmp)
            _permute_rows(kp, kf, dil, seq, tmp)
            _permute_rows(vp, v_ref, dil, seq, tmp)
            if dil == 1:
                st = sn
            else:
                st = sp
                for n in range(4):
                    _permute_rows(sp.at[n], sn.at[n], dil, seq, tmp)
            def touched(sub, seg=seg):
                lo, hi = sub * region, (sub + 1) * region
                if seg < region:
                    return lo, hi
                seg0 = lo // seg * seg
                return max(lo - RADIUS, seg0), min(hi + RADIUS, seg0 + seg)

            def summed(acc, start, size, touched=touched):
                pieces = []
                for c0 in range(start, start + size, RADIUS):
                    owners = [s for s in range(ATT_UNROLL) if touched(s)[0] <= c0 and c0 + RADIUS <= touched(s)[1]]
                    if pieces and pieces[-1][2] == owners:
                        pieces[-1][1] += RADIUS
                    else:
                        pieces.append([c0, RADIUS, owners])
                vals = [sum(acc[o, pl.ds(c0, n), :] for o in owners) for c0, n, owners in pieces]
                return vals[0] if len(vals) == 1 else jnp.concatenate(vals, axis=0)

            for sub in range(ATT_UNROLL):
                lo, hi = touched(sub)
                dkp[sub, pl.ds(lo, hi - lo), :] = jnp.zeros((hi - lo, LANES), F32)
                dvp[sub, pl.ds(lo, hi - lo), :] = jnp.zeros((hi - lo, LANES), F32)

            def blk(it, carry, seg=seg, kw=kw, pi=pi, st=st):
                first = _first_head(Q_BLOCK)
                chains = [(sub, h) for sub in range(ATT_UNROLL) for h in range(2)]
                win = [_window(it + sub * half, seg) for sub in range(ATT_UNROLL)]
                qrows = [pl.ds(w[0], Q_BLOCK) for w in win]
                krows = [pl.ds(w[1], kw) for w in win]

                def over_keys(n, sub):
                    t = st[n, qrows[sub], :]
                    return t if kw == LANES else jnp.concatenate([t] * (kw // LANES), axis=1)

                s, dp = {}, {}
                for sub, h in chains:
                    s[sub, h] = lax.dot_general(qp[h, qrows[sub], :], kp[krows[sub], :], NT,
                                                preferred_element_type=F32) - slopes[h] * dm_refs[pi][win[sub][2]]
                    dp[sub, h] = lax.dot_general(dop[h, qrows[sub], :], vp[krows[sub], :], NT,
                                                 preferred_element_type=F32)
                p, ds = {}, {}
                for sub, h in chains:
                    e = jnp.exp(s[sub, h] - over_keys(h, sub))
                    ds[sub, h] = (e * (dp[sub, h] - over_keys(2 + h, sub))).astype(BF16)
                    p[sub, h] = e.astype(BF16)
                dq, dk, dv = {}, {}, {}
                for sub, h in chains:
                    dq[sub, h] = jnp.dot(ds[sub, h], kp[krows[sub], :], preferred_element_type=F32)
                    dk[sub, h] = lax.dot_general(ds[sub, h], qp[h, qrows[sub], :], TN, preferred_element_type=F32)
                    dv[sub, h] = lax.dot_general(p[sub, h], dop[h, qrows[sub], :], TN, preferred_element_type=F32)
                for sub in range(ATT_UNROLL):
                    dqp[qrows[sub], :] = jnp.where(first, dq[sub, 0], dq[sub, 1])
                    dkp[sub, krows[sub], :] += dk[sub, 0] + dk[sub, 1]
                    dvp[sub, krows[sub], :] += dv[sub, 0] + dv[sub, 1]
                return carry

            lax.fori_loop(0, half, blk, 0)
            first_pattern = pi == 0
            if first_pattern:
                for r0 in range(0, seq, region):
                    rows = pl.ds(r0, region)
                    dqn[rows, :] = dqp[rows, :]
                    dkn[rows, :] = summed(dkp, r0, region)
                    dvn[rows, :] = summed(dvp, r0, region)
            else:
                _scatter_segments(dqn, lambda start, size: dqp[pl.ds(start, size), :], dil, seq, tmp, accumulate=True)
                for nat, acc in ((dkn, dkp), (dvn, dvp)):
                    _scatter_segments(nat, lambda start, size, acc=acc: summed(acc, start, size),
                                      dil, seq, tmp, accumulate=True)

        def finish(ci, carry):
            rows = pl.ds(pl.multiple_of(ci * ATT_ROWS, ATT_ROWS), ATT_ROWS)
            for src_ref, g_ref, dn, dst_ref, scale, row in (
                    (q_ref, gq_ref, dqn, dq_ref, HEAD_DIM ** -0.5, 0), (k_ref, gk_ref, dkn, dk_ref, 1.0, 1)):
                xv = src_ref[rows, :]
                r = lax.rsqrt(_head_mean(xv * xv, same_ref) + EPS)
                xhat = xv * r
                d = dn[rows, :] * scale
                dg_ref[row:row + 1, :] += jnp.sum(d * xhat, axis=0, keepdims=True)
                dxh = d * g_ref[...]
                dst_ref[rows, :] = (r * (dxh - xhat * _head_mean(dxh * xhat, same_ref))).astype(BF16)
            dv_ref[rows, :] = dvn[rows, :].astype(BF16)
            return carry

        lax.fori_loop(0, seq // ATT_ROWS, finish, 0)

        @pl.when(step == n_seq * n_hp - 1)
        def _():
            _exchange_wait(*ride_args)

    def col(off):
        return pl.BlockSpec((None, None, seq, LANES), lambda b, hp: (col0 + off * n_hp + hp, b, 0, 0))

    def whole(arr):
        return pl.BlockSpec(arr.shape, lambda b, hp: (0,) * arr.ndim)

    att = pl.BlockSpec((None, seq, LANES), lambda b, hp: (b, 0, hp))
    rows_f32 = pltpu.VMEM((seq, LANES), F32)
    rows_bf16 = pltpu.VMEM((seq, LANES), BF16)
    by_head_bf16 = pltpu.VMEM((2, seq, LANES), BF16)
    per_sub_f32 = pltpu.VMEM((ATT_UNROLL, seq, LANES), F32)
    stats_f32 = pltpu.VMEM((4, seq, LANES), F32)
    return _call(
        body, name="attn_bwd", grid=(n_seq, n_hp),
        in_specs=[col(0), col(1), col(2), att, att, att, whole(g_q2), whole(g_k2),
                  pl.BlockSpec((None, 8, LANES), lambda b, hp: (hp, 0, 0)),
                  whole(dms[0]), whole(dms[1]), whole(dms[2]), whole(same)] + [ANY_SPEC] * n_ride,
        out_specs=[att, att, att, pl.BlockSpec((8, LANES), lambda b, hp: (0, 0))] + [ANY_SPEC] * n_ride,
        out_shape=[_sds((n_seq, seq, D_ATT), BF16)] * 3 + [_sds((8, LANES), F32)] + _exchange_shapes(ride),
        scratch_shapes=[rows_f32, rows_f32, by_head_bf16, by_head_bf16, rows_bf16, rows_bf16, stats_f32, stats_f32,
                        rows_f32, per_sub_f32, per_sub_f32, rows_f32, rows_f32, rows_f32, rows_f32]
        + _exchange_sems(n_ride),
        compiler_params=_params(("arbitrary", "arbitrary")),
    )(proj3, proj3, proj3, do3, y_att3, lse3, g_q2, g_k2, _alibi_rows(), *dms, same, *[a for a, _ in ride])


def _mix_out(uc2, y_att2, x2, mod8, g_ln, b_ln, g_ffn, w_out, seq, tm=512):
    tokens = x2.shape[0]
    per_seq = seq // tm

    def body(uc_ref, ya_ref, x_ref, m_ref, gl_ref, bl_ref, gf_ref, w_ref, yc_ref, mix_ref, x1_ref, h2_ref):
        uc = uc_ref[...]
        mu = jnp.mean(uc, axis=-1, keepdims=True)
        cen = uc - mu
        rs = lax.rsqrt(jnp.mean(cen * cen, axis=-1, keepdims=True) + EPS)
        z = cen * rs * gl_ref[...] + bl_ref[...]
        yc = (z * _sig(z)).astype(BF16)
        yc_ref[...] = yc
        mix = (jnp.dot(yc, w_ref[pl.ds(0, D_CONV), :], preferred_element_type=F32)
               + jnp.dot(ya_ref[...], w_ref[pl.ds(D_CONV, D_ATT), :], preferred_element_type=F32))
        mix_ref[...] = mix
        x1 = x_ref[...] + m_ref[2:3, :] * mix
        x1_ref[...] = x1
        r = lax.rsqrt(jnp.mean(x1 * x1, axis=-1, keepdims=True) + EPS)
        h2_ref[...] = ((x1 * r * gf_ref[...]) * (1.0 + m_ref[4:5, :]) + m_ref[3:4, :]).astype(BF16)

    def rows(width):
        return pl.BlockSpec((tm, width), lambda i: (i, 0))

    def vec(width):
        return pl.BlockSpec((1, width), lambda i: (0, 0))

    return _call(
        body, name="mix_out", grid=(tokens // tm,),
        in_specs=[rows(D_CONV), rows(D_ATT), rows(D_MODEL),
                  pl.BlockSpec((None, 8, D_MODEL), lambda i: (i // per_seq, 0, 0)),
                  vec(D_CONV), vec(D_CONV), vec(D_MODEL),
                  pl.BlockSpec((D_MODEL, D_MODEL), lambda i: (0, 0))],
        out_specs=[rows(D_CONV), rows(D_MODEL), rows(D_MODEL), rows(D_MODEL)],
        out_shape=[_sds((tokens, D_CONV), BF16), _sds((tokens, D_MODEL), F32),
                   _sds((tokens, D_MODEL), F32), _sds((tokens, D_MODEL), BF16)],
        compiler_params=_params(("parallel",)),
    )(uc2, y_att2, x2, mod8, g_ln, b_ln, g_ffn, w_out)


def _mix_out_bwd(dmix, uc2, g_ln, b_ln, w_out, tm=512):
    tokens = dmix.shape[0]

    def body(dm_ref, uc_ref, gl_ref, bl_ref, w_ref, duc_ref, do_ref, dgb_ref):
        @pl.when(pl.program_id(0) == 0)
        def _():
            dgb_ref[...] = jnp.zeros_like(dgb_ref)

        dmv = dm_ref[...]
        dyc = lax.dot_general(dmv, w_ref[pl.ds(0, D_CONV), :], NT, preferred_element_type=F32)
        do_ref[...] = lax.dot_general(dmv, w_ref[pl.ds(D_CONV, D_ATT), :], NT, preferred_element_type=F32)
        uc = uc_ref[...]
        mu = jnp.mean(uc, axis=-1, keepdims=True)
        cen = uc - mu
        rs = lax.rsqrt(jnp.mean(cen * cen, axis=-1, keepdims=True) + EPS)
        xh = cen * rs
        z = xh * gl_ref[...] + bl_ref[...]
        sg = _sig(z)
        dz = dyc * (sg * (1.0 + z * (1.0 - sg)))
        dgb_ref[0:1, :] += jnp.sum(dz * xh, axis=0, keepdims=True)
        dgb_ref[1:2, :] += jnp.sum(dz, axis=0, keepdims=True)
        dxh = dz * gl_ref[...]
        duc_ref[...] = rs * (dxh - jnp.mean(dxh, axis=-1, keepdims=True)
                             - xh * jnp.mean(dxh * xh, axis=-1, keepdims=True))

    return _call(
        body, name="mix_out_bwd", grid=(tokens // tm,),
        in_specs=[pl.BlockSpec((tm, D_MODEL), lambda i: (i, 0)),
                  pl.BlockSpec((tm, D_CONV), lambda i: (i, 0)),
                  pl.BlockSpec((1, D_CONV), lambda i: (0, 0)),
                  pl.BlockSpec((1, D_CONV), lambda i: (0, 0)),
                  pl.BlockSpec((D_MODEL, D_MODEL), lambda i: (0, 0))],
        out_specs=[pl.BlockSpec((tm, D_CONV), lambda i: (i, 0)),
                   pl.BlockSpec((tm, D_ATT), lambda i: (i, 0)),
                   pl.BlockSpec((8, D_CONV), lambda i: (0, 0))],
        out_shape=[_sds((tokens, D_CONV), F32), _sds((tokens, D_ATT), F32), _sds((8, D_CONV), F32)],
        compiler_params=_params(("arbitrary",)),
    )(dmix, uc2, g_ln, b_ln, w_out)


FF_TILE = 256
FF_TILES = D_FF // FF_TILE
FF_ROWS = 256


def _ffn_fwd(h2, w_gate, w_up, w_down, x1, target, mod8, seq, tm=1024):
    tokens = h2.shape[0]
    per_seq = seq // tm
    n_seq = tokens // seq
    last = D_FF // FF_TILE - 1

    def body(h_ref, wg_ref, wu_ref, wd_ref, x1_ref, t_ref, m_ref, gate_ref, up_ref, dy_ref, df_ref, sq_ref, dgf_ref,
             f_ref):
        i, j = pl.program_id(0), pl.program_id(1)

        @pl.when(j == 0)
        def _():
            f_ref[...] = jnp.zeros_like(f_ref)

        @pl.when((j == 0) & (i == 0))
        def _():
            sq_ref[...] = jnp.zeros_like(sq_ref)

        @pl.when((j == 0) & (i % per_seq == 0))
        def _():
            dgf_ref[...] = jnp.zeros_like(dgf_ref)

        def gate_up(r):
            hv = h_ref[pl.ds(r * FF_ROWS, FF_ROWS), :]
            return (lax.dot_general(hv, wg_ref[...], NT, preferred_element_type=F32),
                    lax.dot_general(hv, wu_ref[...], NT, preferred_element_type=F32))

        ahead = gate_up(0)
        for r in range(tm // FF_ROWS):
            gate, up = ahead
            if r + 1 < tm // FF_ROWS:
                ahead = gate_up(r + 1)
            rows = pl.ds(r * FF_ROWS, FF_ROWS)
            gate_ref[rows, :] = gate.astype(BF16)
            up_ref[rows, :] = up.astype(BF16)
            act = (gate * _sig(gate) * up).astype(BF16)
            f_ref[rows, :] += jnp.dot(act, wd_ref[...], preferred_element_type=F32)

        @pl.when(j == last)
        def _():
            gate_f = m_ref[5:6, :]
            for r in range(tm // FF_ROWS):
                rows = pl.ds(r * FF_ROWS, FF_ROWS)
                fv = f_ref[rows, :]
                diff = x1_ref[rows, :] + gate_f * fv - t_ref[rows, :]
                sq_ref[0:1, :] += jnp.sum(diff * diff, axis=0, keepdims=True)
                dy = diff * (1.0 / D_MODEL)
                dy_ref[rows, :] = dy
                df_ref[rows, :] = (gate_f * dy).astype(BF16)
                dgf_ref[0:1, :] += jnp.sum(dy * fv, axis=0, keepdims=True)

    rows_spec = pl.BlockSpec((tm, D_MODEL), lambda i, j: (i, 0))
    per = pl.BlockSpec((None, 8, D_MODEL), lambda i, j: (i // per_seq, 0, 0))
    tile = pl.BlockSpec((None, tm, FF_TILE), lambda i, j: (j, i, 0))
    w_tile = pl.BlockSpec((FF_TILE, D_MODEL), lambda i, j: (j, 0))
    return _call(
        body, name="ffn_fwd", grid=(tokens // tm, D_FF // FF_TILE),
        in_specs=[rows_spec, w_tile, w_tile,
                  pl.BlockSpec((FF_TILE, D_MODEL), lambda i, j: (j, 0)),
                  rows_spec, rows_spec, per],
        out_specs=[tile, tile, rows_spec, rows_spec, pl.BlockSpec((8, D_MODEL), lambda i, j: (0, 0)), per],
        out_shape=[_sds((FF_TILES, tokens, FF_TILE), BF16), _sds((FF_TILES, tokens, FF_TILE), BF16),
                   _sds((tokens, D_MODEL), F32),
                   _sds((tokens, D_MODEL), BF16), _sds((8, D_MODEL), F32), _sds((n_seq, 8, D_MODEL), F32)],
        scratch_shapes=[pltpu.VMEM((tm, D_MODEL), F32)],
        compiler_params=_params(("arbitrary", "arbitrary")),
    )(h2, w_gate, w_up, w_down, x1, target, mod8)


def _ffn_bwd(df, gate, up, w_gate, w_up, w_down, x1, dy, mix, mod8, g_ffn, seq, tm=1024):
    tokens = df.shape[0]
    per_seq = seq // tm
    n_seq = tokens // seq
    last = D_FF // FF_TILE - 1

    def body(df_ref, gate_ref, up_ref, wg_ref, wu_ref, wd_ref, m_ref, g_ref, x1_hbm, dy_hbm, mix_hbm,
             dgate_ref, dup_ref, act_ref, dx1_ref, dmix_ref, dg_ref, dm_ref, dh_ref, late, late_sems):
        i, j = pl.program_id(0), pl.program_id(1)
        my_rows = pl.ds(pl.multiple_of(i * tm, tm), tm)
        fetches = [pltpu.make_async_copy(src.at[my_rows, :], late.at[n], late_sems.at[n])
                   for n, src in enumerate((x1_hbm, dy_hbm, mix_hbm))]

        @pl.when(j == 0)
        def _():
            dh_ref[...] = jnp.zeros_like(dh_ref)
            for cp in fetches:
                cp.start()

        @pl.when((j == 0) & (i == 0))
        def _():
            dg_ref[...] = jnp.zeros_like(dg_ref)

        @pl.when((j == 0) & (i % per_seq == 0))
        def _():
            dm_ref[...] = jnp.zeros_like(dm_ref)

        def d_act(r):
            return lax.dot_general(df_ref[pl.ds(r * FF_ROWS, FF_ROWS), :], wd_ref[...], NT,
                                   preferred_element_type=F32)

        ahead = d_act(0)
        for r in range(tm // FF_ROWS):
            dact = ahead
            if r + 1 < tm // FF_ROWS:
                ahead = d_act(r + 1)
            rows = pl.ds(r * FF_ROWS, FF_ROWS)
            gate = gate_ref[rows, :].astype(F32)
            up = up_ref[rows, :].astype(F32)
            sg = _sig(gate)
            silu = gate * sg
            act_ref[rows, :] = (silu * up).astype(BF16)
            dup = (dact * silu).astype(BF16)
            dgate = (dact * up * (sg * (1.0 + gate * (1.0 - sg)))).astype(BF16)
            dup_ref[rows, :] = dup
            dgate_ref[rows, :] = dgate
            dh_ref[rows, :] += (jnp.dot(dgate, wg_ref[...], preferred_element_type=F32)
                                + jnp.dot(dup, wu_ref[...], preferred_element_type=F32))

        @pl.when(j == last)
        def _():
            for cp in fetches:
                cp.wait()
            g = g_ref[...]
            for r in range(tm // FF_ROWS):
                rows = pl.ds(r * FF_ROWS, FF_ROWS)
                dh = dh_ref[rows, :]
                x1v = late[0, rows, :]
                rs = lax.rsqrt(jnp.mean(x1v * x1v, axis=-1, keepdims=True) + EPS)
                xhat = x1v * rs
                dm_ref[0:1, :] += jnp.sum(dh, axis=0, keepdims=True)
                dm_ref[1:2, :] += jnp.sum(dh * (xhat * g), axis=0, keepdims=True)
                dn = dh * (1.0 + m_ref[4:5, :])
                dg_ref[0:1, :] += jnp.sum(dn * xhat, axis=0, keepdims=True)
                dxh = dn * g
                dx1 = late[1, rows, :] + rs * (dxh - xhat * jnp.mean(dxh * xhat, axis=-1, keepdims=True))
                dx1_ref[rows, :] = dx1
                dm_ref[2:3, :] += jnp.sum(dx1 * late[2, rows, :], axis=0, keepdims=True)
                dmix_ref[rows, :] = (m_ref[2:3, :] * dx1).astype(BF16)

    tile = pl.BlockSpec((None, tm, FF_TILE), lambda i, j: (j, i, 0))
    w_tile = pl.BlockSpec((FF_TILE, D_MODEL), lambda i, j: (j, 0))
    rows_spec = pl.BlockSpec((tm, D_MODEL), lambda i, j: (i, 0))
    per = pl.BlockSpec((None, 8, D_MODEL), lambda i, j: (i // per_seq, 0, 0))
    return _call(
        body, name="ffn_bwd", grid=(tokens // tm, D_FF // FF_TILE),
        in_specs=[rows_spec, tile, tile, w_tile, w_tile,
                  pl.BlockSpec((FF_TILE, D_MODEL), lambda i, j: (j, 0)),
                  per, pl.BlockSpec((1, D_MODEL), lambda i, j: (0, 0)), ANY_SPEC, ANY_SPEC, ANY_SPEC],
        out_specs=[tile, tile, tile, rows_spec, rows_spec, pl.BlockSpec((8, D_MODEL), lambda i, j: (0, 0)), per],
        out_shape=[_sds((FF_TILES, tokens, FF_TILE), BF16)] * 3
        + [_sds((tokens, D_MODEL), F32), _sds((tokens, D_MODEL), BF16),
           _sds((8, D_MODEL), F32), _sds((n_seq, 8, D_MODEL), F32)],
        scratch_shapes=[pltpu.VMEM((tm, D_MODEL), F32), pltpu.VMEM((3, tm, D_MODEL), F32),
                        pltpu.SemaphoreType.DMA((3,))],
        compiler_params=_params(("arbitrary", "arbitrary")),
    )(df, gate, up, w_gate, w_up, w_down, mod8, g_ffn, x1, dy, mix)


def _mix_in_bwd(d_a, d_g, d_q, d_k, d_v, w_in, x2, dx1, mod8, g_mix, seq, ride, tm=512):
    tokens = x2.shape[0]
    per_seq = seq // tm
    n_seq = tokens // seq
    parts = (d_a, d_g, d_q, d_k, d_v)
    width = D_CONV
    n_ride = len(ride)
    ride_scatter = [s for _, s in ride]

    def body(*refs):
        da_ref, dg_ref, dq_ref, dk_ref, dv_ref, w_ref, x_ref, dx1_ref, m_ref, g_ref = refs[:10]
        ride_in = refs[10:10 + n_ride]
        gx_ref, dgm_ref, dm_ref = refs[10 + n_ride:13 + n_ride]
        ride_args = (ride_scatter, ride_in, refs[13 + n_ride:13 + 2 * n_ride]) + tuple(refs[13 + 2 * n_ride:])
        i = pl.program_id(0)

        @pl.when(i == 0)
        def _():
            _exchange_start(*ride_args)
            dgm_ref[...] = jnp.zeros_like(dgm_ref)

        @pl.when(i % per_seq == 0)
        def _():
            dm_ref[...] = jnp.zeros_like(dm_ref)

        dh = jnp.zeros((tm, D_MODEL), F32)
        for n, ref in enumerate((da_ref, dg_ref, dq_ref, dk_ref, dv_ref)):
            dh = dh + jnp.dot(ref[...], w_ref[pl.ds(n * width, width), :], preferred_element_type=F32)
        xv = x_ref[...]
        r = lax.rsqrt(jnp.mean(xv * xv, axis=-1, keepdims=True) + EPS)
        xhat = xv * r
        g = g_ref[...]
        dm_ref[0:1, :] += jnp.sum(dh, axis=0, keepdims=True)
        dm_ref[1:2, :] += jnp.sum(dh * (xhat * g), axis=0, keepdims=True)
        dn = dh * (1.0 + m_ref[1:2, :])
        dgm_ref[0:1, :] += jnp.sum(dn * xhat, axis=0, keepdims=True)
        dxh = dn * g
        gx_ref[...] = dx1_ref[...] + r * (dxh - xhat * jnp.mean(dxh * xhat, axis=-1, keepdims=True))

        @pl.when(i == tokens // tm - 1)
        def _():
            _exchange_wait(*ride_args)

    rows = pl.BlockSpec((tm, D_MODEL), lambda i: (i, 0))
    half = pl.BlockSpec((tm, width), lambda i: (i, 0))
    per = pl.BlockSpec((None, 8, D_MODEL), lambda i: (i // per_seq, 0, 0))
    return _call(
        body, name="mix_in_bwd", grid=(tokens // tm,),
        in_specs=[half] * 5 + [pl.BlockSpec((D_IN, D_MODEL), lambda i: (0, 0)), rows, rows, per,
                               pl.BlockSpec((1, D_MODEL), lambda i: (0, 0))] + [ANY_SPEC] * n_ride,
        out_specs=[rows, pl.BlockSpec((8, D_MODEL), lambda i: (0, 0)), per] + [ANY_SPEC] * n_ride,
        out_shape=[_sds((tokens, D_MODEL), F32), _sds((8, D_MODEL), F32), _sds((n_seq, 8, D_MODEL), F32)]
        + _exchange_shapes(ride),
        scratch_shapes=_exchange_sems(n_ride),
        compiler_params=_params(("arbitrary",)),
    )(*parts, w_in, x2, dx1, mod8, g_mix, *[a for a, _ in ride])


def _grad_matmul_parts(a_parts, b_parts, name, tk=1024):
    tokens = a_parts[0].shape[0]
    na, nb = len(a_parts), len(b_parts)
    ma, nbw = a_parts[0].shape[1], b_parts[0].shape[1]

    n_k = tokens // tk

    def body(*refs):
        a_refs, b_refs, o_ref, acc = refs[:na], refs[na:na + nb], refs[na + nb], refs[na + nb + 1]

        @pl.when(pl.program_id(0) == 0)
        def _():
            acc[...] = jnp.zeros_like(acc)

        for i in range(na):
            for j in range(nb):
                acc[pl.ds(i * ma, ma), pl.ds(j * nbw, nbw)] += lax.dot_general(
                    a_refs[i][...], b_refs[j][...], TN, preferred_element_type=F32)

        @pl.when(pl.program_id(0) == n_k - 1)
        def _():
            o_ref[...] = acc[...].astype(o_ref.dtype)

    return _call(
        body, name=name, grid=(n_k,),
        in_specs=[pl.BlockSpec((tk, ma), lambda k: (k, 0))] * na + [pl.BlockSpec((tk, nbw), lambda k: (k, 0))] * nb,
        out_specs=pl.BlockSpec((na * ma, nb * nbw), lambda k: (0, 0)),
        out_shape=_sds((na * ma, nb * nbw), BF16),
        scratch_shapes=[pltpu.VMEM((na * ma, nb * nbw), F32)],
        compiler_params=_params(("arbitrary",)),
    )(*a_parts, *b_parts)


def _grad_matmul_tiles(a, b, name, tk=1024):
    tiled_b = b.ndim == 3
    tiles, tokens, width = b.shape if tiled_b else a.shape
    other = a.shape[1] if tiled_b else b.shape[1]
    out_tile = (other, width) if tiled_b else (width, other)
    n_k = tokens // tk

    def body(a_ref, b_ref, o_ref, acc):
        @pl.when(pl.program_id(0) == 0)
        def _():
            acc[...] = jnp.zeros_like(acc)

        for t in range(tiles):
            lhs = a_ref[...] if tiled_b else a_ref[t]
            rhs = b_ref[t] if tiled_b else b_ref[...]
            acc[t] += lax.dot_general(lhs, rhs, TN, preferred_element_type=F32)

        @pl.when(pl.program_id(0) == n_k - 1)
        def _():
            o_ref[...] = acc[...].astype(o_ref.dtype)

    flat = pl.BlockSpec((tk, other), lambda k: (k, 0))
    tiled = pl.BlockSpec((tiles, tk, width), lambda k: (0, k, 0))
    return _call(
        body, name=name, grid=(n_k,),
        in_specs=[flat, tiled] if tiled_b else [tiled, flat],
        out_specs=pl.BlockSpec((tiles,) + out_tile, lambda k: (0, 0, 0)),
        out_shape=_sds((tiles,) + out_tile, BF16),
        scratch_shapes=[pltpu.VMEM((tiles,) + out_tile, F32)],
        compiler_params=_params(("arbitrary",)),
    )(a, b)


def _adamw(w, m, v, g, name, n_parts=0, tr=256):
    rows, cols = w.shape
    tr = min(tr, rows)
    c1 = 1.0 - ADAM_B1 ** ADAM_STEP
    c2 = 1.0 - ADAM_B2 ** ADAM_STEP

    def body(w_ref, m_ref, v_ref, g_ref, go_ref, d_ref, mo_ref, vo_ref):
        if n_parts:
            gv = g_ref[0].astype(F32)
            for p in range(1, n_parts):
                gv = gv + g_ref[p].astype(F32)
        else:
            gv = g_ref[...]
        go_ref[...] = gv
        mn = ADAM_B1 * m_ref[...] + (1.0 - ADAM_B1) * gv
        vn = ADAM_B2 * v_ref[...] + (1.0 - ADAM_B2) * (gv * gv)
        mo_ref[...] = mn
        vo_ref[...] = vn
        d_ref[...] = -ADAM_LR * ((mn / c1) / (jnp.sqrt(vn / c2) + ADAM_EPS) + ADAM_WD * w_ref[...])

    blk = pl.BlockSpec((tr, cols), lambda i: (i, 0))
    g_spec = pl.BlockSpec((n_parts, tr, cols), lambda i: (0, i, 0)) if n_parts else blk
    return _call(
        body, name=name, grid=(rows // tr,),
        in_specs=[blk, blk, blk, g_spec], out_specs=[blk] * 4,
        out_shape=[_sds((rows, cols), F32)] * 4,
        compiler_params=_params(("parallel",)),
    )(w, m, v, g)


def _cols_to_full(blocks):
    n, r, c = blocks.shape
    return jnp.transpose(blocks, (1, 0, 2)).reshape(r, n * c)


def _pad_lanes(v, width):
    return jnp.pad(v, ((0, 0), (0, width - v.shape[1])))


def kernel(x, c, w_ada, b_ada, g_mix, w_in, w_dw, b_dw, g_conv_ln, b_conv_ln, g_q, g_k, w_out, g_ffn, w_gate, w_up, w_down, loss_target, m_w_ada, m_b_ada, m_g_mix, m_w_in, m_w_dw, m_b_dw, m_g_conv_ln, m_b_conv_ln, m_g_q, m_g_k, m_w_out, m_g_ffn, m_w_gate, m_w_up, m_w_down, v_w_ada, v_b_ada, v_g_mix, v_w_in, v_w_dw, v_b_dw, v_g_conv_ln, v_b_conv_ln, v_g_q, v_g_k, v_w_out, v_g_ffn, v_w_gate, v_w_up, v_w_down):
    n_seq, seq, _ = x.shape
    tokens = n_seq * seq
    me = 4 * lax.axis_index("x") + 2 * lax.axis_index("y") + lax.axis_index("c")
    ada_cols = w_ada.shape[2]
    dw_cols = w_dw.shape[2]

    def transposed(w):
        return jnp.transpose(w[0])

    (c_g, w_in_g, w_dw_g) = _gather_by_chip([c, transposed(w_in).astype(BF16), w_dw[0]], "gather_weights")
    c_all = c_g.reshape(N_DEV * n_seq, D_MODEL)
    w_in_t = w_in_g.reshape(D_IN, D_MODEL)
    w_dw_f = _cols_to_full(w_dw_g)

    b_cols = lax.dynamic_slice(b_ada, (0, me * ada_cols), (1, ada_cols))
    mod_cols = _ada_fwd(c_all, w_ada[0], b_cols)
    (mod_g,) = _exchange([(mod_cols, False)], "gather_mod")
    mod_mine = lax.dynamic_slice(mod_g, (0, me * n_seq, 0), (N_DEV, n_seq, ada_cols))
    mod = jnp.transpose(mod_mine, (1, 0, 2)).reshape(n_seq, N_MOD, D_MODEL)
    mod8 = jnp.pad(mod, ((0, 0), (0, 8 - N_MOD), (0, 0)))

    x2 = x.reshape(tokens, D_MODEL)
    h1, proj = _mix_in(x2, mod8, g_mix, w_in_t, seq)
    proj3 = proj.reshape(D_IN // LANES, n_seq, seq, LANES)
    uc3 = _conv_fwd(proj3, w_dw_f, b_dw)
    g_q2, g_k2 = jnp.tile(g_q, (1, 2)), jnp.tile(g_k, (1, 2))
    y_att3, lse3, w_out_g, w_gate_g, w_up_g, w_down_g = _attn_fwd(
        proj3, g_q2, g_k2,
        [(w_out[0].astype(BF16), False), (transposed(w_gate).astype(BF16), False),
         (transposed(w_up).astype(BF16), False), (w_down[0].astype(BF16), False)])
    w_out_f = w_out_g.reshape(D_MODEL, D_MODEL)
    w_gate_f = w_gate_g.reshape(D_FF, D_MODEL)
    w_up_f = w_up_g.reshape(D_FF, D_MODEL)
    w_down_f = w_down_g.reshape(D_FF, D_MODEL)
    uc2 = uc3.reshape(tokens, D_CONV)
    y_att2 = y_att3.reshape(tokens, D_ATT)
    y_conv, mix, x1, h2 = _mix_out(uc2, y_att2, x2, mod8, g_conv_ln, b_conv_ln, g_ffn, w_out_f, seq)
    gate, up, dy, df, sq, dgate_f = _ffn_fwd(
        h2, w_gate_f, w_up_f, w_down_f, x1, loss_target.reshape(tokens, D_MODEL), mod8, seq)

    dgate, dup, act, dx1, dmix, dg_ffn, dmod_f = _ffn_bwd(
        df, gate, up, w_gate_f, w_up_f, w_down_f, x1, dy, mix, mod8, g_ffn, seq)
    duc2, do2, dgb_ln = _mix_out_bwd(dmix, uc2, g_conv_ln, b_conv_ln, w_out_f)
    d_a3, d_g3, dw_dw_p, db_dw_p = _conv_bwd(duc2.reshape(n_seq, seq, D_CONV), proj3, w_dw_f)
    gw_gate = _grad_matmul_tiles(dgate, h2, "grad_w_gate")
    gw_up = _grad_matmul_tiles(dup, h2, "grad_w_up")
    gw_down = _grad_matmul_tiles(act, df, "grad_w_down")
    gw_out = _grad_matmul_parts([y_conv, y_att2], [dmix], "grad_w_out")
    d_q3, d_k3, d_v3, dg_qk, p_gate, p_up, p_down, p_out = _attn_bwd(
        proj3, do2.reshape(n_seq, seq, D_ATT), y_att3, lse3, g_q2, g_k2,
        [(gw_gate.reshape(N_DEV, D_FF // N_DEV, D_MODEL), True), (gw_up.reshape(N_DEV, D_FF // N_DEV, D_MODEL), True),
         (gw_down.reshape(N_DEV, D_FF // N_DEV, D_MODEL), True),
         (gw_out.reshape(N_DEV, D_MODEL // N_DEV, D_MODEL), True)])
    flat = lambda t: t.reshape(tokens, t.shape[-1])
    d_a, d_g, d_q, d_k, d_v = flat(d_a3), flat(d_g3), flat(d_q3), flat(d_k3), flat(d_v3)
    gw_in = _grad_matmul_parts([d_a, d_g, d_q, d_k, d_v], [h1], "grad_w_in")
    grad_x2, dg_mix, dmod_m, p_in = _mix_in_bwd(
        d_a, d_g, d_q, d_k, d_v, w_in_t, x2, dx1, mod8, g_mix, seq,
        [(gw_in.reshape(N_DEV, D_IN // N_DEV, D_MODEL), True)])

    dmod = jnp.concatenate([dmod_m[:, 0], dmod_m[:, 1], dmod_f[:, 2], dmod_f[:, 0], dmod_f[:, 1], dgate_f[:, 0]], axis=1)
    dg_q = dg_qk[0:1, 0:HEAD_DIM] + dg_qk[0:1, HEAD_DIM:]
    dg_k = dg_qk[1:2, 0:HEAD_DIM] + dg_qk[1:2, HEAD_DIM:]
    loss_part = (0.5 / D_MODEL) * jnp.sum(sq[0:1, :], axis=1, keepdims=True)
    small = jnp.concatenate(
        [dg_mix[0:1], dg_ffn[0:1], db_dw_p[0:1], dgb_ln[0:1], dgb_ln[1:2],
         _pad_lanes(dg_q, LANES), _pad_lanes(dg_k, LANES), _pad_lanes(loss_part, LANES)], axis=1)
    n_small = small.shape[1] - LANES

    (dmod_g, small_g, dw_g) = _exchange([(dmod, False), (small, False), (dw_dw_p, False)], "gather_small_grads")

    dmod_all = dmod_g.reshape(N_DEV * n_seq, N_MOD * D_MODEL)
    dmod_cols = lax.dynamic_slice(dmod_all, (0, me * ada_cols), (N_DEV * n_seq, ada_cols))
    gw_ada, gb_ada = _ada_bwd(c_all, dmod_cols, dmod_all)

    res = {}
    res["w_ada"] = _adamw(w_ada[0], m_w_ada[0], v_w_ada[0], gw_ada, "adamw_w_ada")
    res["b_ada"] = _adamw(b_ada, m_b_ada, v_b_ada, gb_ada, "adamw_b_ada")
    def adamw_transposed(w, m, v, parts, name, tr):
        outs = _adamw(transposed(w), transposed(m), transposed(v), parts, name, N_DEV, tr=tr)
        return tuple(jnp.transpose(o) for o in outs)

    res["w_in"] = adamw_transposed(w_in, m_w_in, v_w_in, p_in, "adamw_w_in", 160)
    res["w_out"] = _adamw(w_out[0], m_w_out[0], v_w_out[0], p_out, "adamw_w_out", N_DEV)
    res["w_gate"] = adamw_transposed(w_gate, m_w_gate, v_w_gate, p_gate, "adamw_w_gate", 176)
    res["w_up"] = adamw_transposed(w_up, m_w_up, v_w_up, p_up, "adamw_w_up", 176)
    res["w_down"] = _adamw(w_down[0], m_w_down[0], v_w_down[0], p_down, "adamw_w_down", N_DEV, tr=176)
    dw_mine = lax.dynamic_slice(dw_g, (0, 0, me * dw_cols), (N_DEV, CONV_WIDTH, dw_cols))
    res["w_dw"] = _adamw(w_dw[0], m_w_dw[0], v_w_dw[0], dw_mine, "adamw_w_dw", N_DEV)

    small_names = ["g_mix", "g_ffn", "b_dw", "g_conv_ln", "b_conv_ln", "g_q", "g_k"]
    small_w = {"g_mix": (g_mix, m_g_mix, v_g_mix), "g_ffn": (g_ffn, m_g_ffn, v_g_ffn), "b_dw": (b_dw, m_b_dw, v_b_dw),
               "g_conv_ln": (g_conv_ln, m_g_conv_ln, v_g_conv_ln), "b_conv_ln": (b_conv_ln, m_b_conv_ln, v_b_conv_ln),
               "g_q": (g_q, m_g_q, v_g_q), "g_k": (g_k, m_g_k, v_g_k)}
    widths = [max(small_w[n][0].shape[1], LANES) for n in small_names]
    packed = [jnp.concatenate([_pad_lanes(small_w[n][i], wd) for n, wd in zip(small_names, widths)], axis=1) for i in range(3)]
    outs = _adamw(packed[0], packed[1], packed[2], small_g[:, :, :n_small], "adamw_small", N_DEV)
    off = 0
    for n, wd in zip(small_names, widths):
        real = small_w[n][0].shape[1]
        res[n] = tuple(o[:, off:off + real] for o in outs)
        off += wd
    loss = jnp.sum(small_g[:, 0, n_small])

    order = ["w_ada", "b_ada", "g_mix", "w_in", "w_dw", "b_dw", "g_conv_ln", "b_conv_ln", "g_q", "g_k",
             "w_out", "g_ffn", "w_gate", "w_up", "w_down"]
    lead = {"w_ada", "w_in", "w_dw", "w_out", "w_gate", "w_up", "w_down"}
    grads, deltas, new_m, new_v = [], [], [], []
    for n in order:
        g, d, mn, vn = res[n]
        g, d, mn, vn = (t[None] if n in lead else t for t in (g, d, mn, vn))
        grads.append(g)
        deltas.append(d)
        new_m.append(mn)
        new_v.append(vn)
    return (loss, grad_x2.reshape(n_seq, seq, D_MODEL), *grads, *deltas, *new_m, *new_v)
```

```python
import numpy as np
import jax
import jax.numpy as jnp
from jax import lax
from jax.experimental import pallas as pl
from jax.experimental.pallas import tpu as pltpu

F32 = jnp.float32
BF16 = jnp.bfloat16

N_DEV = 8
D_MODEL = 1024
D_CONV = 512
D_ATT = 512
HEAD_DIM = 64
CONV_WIDTH = 31
D_IN = 2 * D_CONV + 3 * D_ATT
D_FF = 2816
N_MOD = 6
EPS = 1e-6
RADIUS = 64
DILATIONS = (1, 4, 16)
Q_BLOCK = 128
LANES = 128
VMEM_LIMIT = 56 * 1024 * 1024

ADAM_LR = 0.001
ADAM_B1 = 0.9
ADAM_B2 = 0.999
ADAM_EPS = 1e-08
ADAM_WD = 0.01
ADAM_STEP = 10

NT = (((1,), (1,)), ((), ()))
TN = (((0,), (0,)), ((), ()))


def _call(body, **kw):
    return pl.pallas_call(body, **kw)


def _params(sem=None, vmem=VMEM_LIMIT):
    return pltpu.CompilerParams(dimension_semantics=sem, vmem_limit_bytes=vmem)


def _sig(x):
    return 1.0 / (1.0 + jnp.exp(-x))


def _sds(shape, dtype):
    return jax.ShapeDtypeStruct(shape, dtype)


N_PEER = N_DEV - 1
ANY_SPEC = pl.BlockSpec(memory_space=pl.ANY)


def _exchange_copies(scatter, ins, outs, *sems):
    n = len(ins)
    if n == 0:
        return [], []
    send_sems, recv_sems, local_sems = sems
    x, y, c = lax.axis_index("x"), lax.axis_index("y"), lax.axis_index("c")
    me = 4 * x + 2 * y + c

    def src(a, slot):
        return ins[a].at[slot] if scatter[a] else ins[a]

    local = [pltpu.make_async_copy(src(a, me), outs[a].at[me], local_sems.at[a]) for a in range(n)]
    flights = []
    for k in range(1, N_DEV):
        px = 1 - x if k & 4 else x
        py = 1 - y if k & 2 else y
        pc = 1 - c if k & 1 else c
        pid = 4 * px + 2 * py + pc
        for a in range(n):
            i = a * N_PEER + k - 1
            send, recv = (pltpu.make_async_remote_copy(
                src_ref=src(a, pid), dst_ref=outs[a].at[slot],
                send_sem=send_sems.at[i], recv_sem=recv_sems.at[i],
                device_id=(px, py, pc), device_id_type=pl.DeviceIdType.MESH) for slot in (me, pid))
            flights.append((send, recv))
    return local, flights


def _exchange_start(*args):
    local, flights = _exchange_copies(*args)
    for cp in local:
        cp.start()
    for send, _ in flights:
        send.start()


def _exchange_wait(*args):
    local, flights = _exchange_copies(*args)
    for send, recv in flights:
        send.wait_send()
        recv.wait_recv()
    for cp in local:
        cp.wait()


def _exchange_shapes(items):
    return [_sds((N_DEV,) + tuple(arr.shape[1:] if scatter else arr.shape), arr.dtype) for arr, scatter in items]


def _exchange_sems(n):
    if n == 0:
        return []
    return [pltpu.SemaphoreType.DMA((n * N_PEER,)), pltpu.SemaphoreType.DMA((n * N_PEER,)),
            pltpu.SemaphoreType.DMA((n,))]


def _gather_by_chip_phase(phase, ins, outs, send_sems, recv_sems, local_sems):
    n = len(ins)
    per = N_PEER
    x, y, c = lax.axis_index("x"), lax.axis_index("y"), lax.axis_index("c")
    me, sibling = (x, y, c), (x, y, 1 - c)
    chips = [(1 - x, y), (x, 1 - y), (1 - x, 1 - y)]

    def slot(px, py, pc):
        return 4 * px + 2 * py + pc

    def copy(a, k, block, to, src=None):
        dst = outs[a].at[slot(*block)]
        return pltpu.make_async_remote_copy(
            src_ref=dst if src is None else src, dst_ref=dst,
            send_sem=send_sems.at[a * per + k], recv_sem=recv_sems.at[a * per + k],
            device_id=to, device_id_type=pl.DeviceIdType.MESH)

    local = [pltpu.make_async_copy(ins[a], outs[a].at[slot(*me)], local_sems.at[a]) for a in range(n)]
    first = []
    for a in range(n):
        first.append(copy(a, 0, me, sibling, src=ins[a]))
        first += [copy(a, 1 + j, me, (*chip, c), src=ins[a]) for j, chip in enumerate(chips)]
    passed = [copy(a, 4 + j, (*chip, c), sibling) for j, chip in enumerate(chips) for a in range(n)]
    if phase == 0:
        for cp in local + first:
            cp.start()
    elif phase == 1:
        for j, chip in enumerate(chips):
            for a in range(n):
                copy(a, 1 + j, (*chip, c), me).wait_recv()
        for cp in passed:
            cp.start()
    else:
        for a in range(n):
            copy(a, 0, sibling, me).wait_recv()
            for j, chip in enumerate(chips):
                copy(a, 4 + j, (*chip, 1 - c), me).wait_recv()
        for cp in first + passed:
            cp.wait_send()
        for cp in local:
            cp.wait()


def _gather_by_chip(arrays, name):
    n = len(arrays)

    def body(*refs):
        for phase in range(3):
            _gather_by_chip_phase(phase, refs[:n], refs[n:2 * n], *refs[2 * n:])

    return _call(
        body, name=name, out_shape=_exchange_shapes([(arr, False) for arr in arrays]),
        in_specs=[ANY_SPEC] * n, out_specs=[ANY_SPEC] * n, scratch_shapes=_exchange_sems(n),
    )(*arrays)


def _exchange(items, name):
    n = len(items)
    scatter = [s for _, s in items]

    def body(*refs):
        args = (scatter, refs[:n], refs[n:2 * n]) + tuple(refs[2 * n:])
        _exchange_start(*args)
        _exchange_wait(*args)

    return _call(
        body, name=name, out_shape=_exchange_shapes(items),
        in_specs=[ANY_SPEC] * n, out_specs=[ANY_SPEC] * n, scratch_shapes=_exchange_sems(n),
    )(*[a for a, _ in items])


def _ada_fwd(c_all, w_ada, b_cols):
    def body(c_ref, w_ref, b_ref, o_ref):
        cv = c_ref[...]
        sc = (cv * _sig(cv)).astype(BF16)
        o_ref[...] = jnp.dot(sc, w_ref[...].astype(BF16), preferred_element_type=F32) + b_ref[...]

    return _call(body, name="ada_fwd", out_shape=_sds((c_all.shape[0], w_ada.shape[1]), F32),
                 compiler_params=_params())(c_all, w_ada, b_cols)


def _ada_bwd(c_all, dmod_cols, dmod_all):
    def body(c_ref, dc_ref, da_ref, gw_ref, gb_ref):
        cv = c_ref[...]
        sc = (cv * _sig(cv)).astype(BF16)
        gw_ref[...] = lax.dot_general(sc, dc_ref[...].astype(BF16), TN, preferred_element_type=F32)
        gb_ref[...] = jnp.sum(da_ref[...], axis=0, keepdims=True)

    return _call(body, name="ada_bwd",
                 out_shape=[_sds((c_all.shape[1], dmod_cols.shape[1]), F32), _sds((1, dmod_all.shape[1]), F32)],
                 compiler_params=_params())(c_all, dmod_cols, dmod_all)


MIX_ROWS = 128


def _mix_in(x2, mod8, g_mix, w_in, seq, tm=512):
    tokens = x2.shape[0]
    per_seq = seq // tm

    def body(x_ref, m_ref, g_ref, wt_ref, h_ref, p_ref, w_ref):
        @pl.when(pl.program_id(0) == 0)
        def _():
            w_ref[...] = wt_ref[...].T

        def normed(c):
            rows = pl.ds(c * MIX_ROWS, MIX_ROWS)
            xv = x_ref[rows, :]
            r = lax.rsqrt(jnp.mean(xv * xv, axis=-1, keepdims=True) + EPS)
            hb = ((xv * r * g_ref[...]) * (1.0 + m_ref[1:2, :]) + m_ref[0:1, :]).astype(BF16)
            h_ref[rows, :] = hb
            return hb

        ahead = normed(0)
        for c in range(tm // MIX_ROWS):
            hb = ahead
            if c + 1 < tm // MIX_ROWS:
                ahead = normed(c + 1)
            p = jnp.dot(hb, w_ref[...], preferred_element_type=F32)
            for cb in range(D_IN // LANES):
                p_ref[cb, pl.ds(c * MIX_ROWS, MIX_ROWS), :] = p[:, cb * LANES:(cb + 1) * LANES]

    return _call(
        body, name="mix_in", grid=(tokens // tm,),
        in_specs=[pl.BlockSpec((tm, D_MODEL), lambda i: (i, 0)),
                  pl.BlockSpec((None, 8, D_MODEL), lambda i: (i // per_seq, 0, 0)),
                  pl.BlockSpec((1, D_MODEL), lambda i: (0, 0)),
                  pl.BlockSpec((D_IN, D_MODEL), lambda i: (0, 0))],
        out_specs=[pl.BlockSpec((tm, D_MODEL), lambda i: (i, 0)),
                   pl.BlockSpec((D_IN // LANES, tm, LANES), lambda i: (0, i, 0))],
        out_shape=[_sds((tokens, D_MODEL), BF16), _sds((D_IN // LANES, tokens, LANES), F32)],
        scratch_shapes=[pltpu.VMEM((D_MODEL, D_IN), BF16)],
        compiler_params=_params(("arbitrary",)),
    )(x2, mod8, g_mix, w_in)


CONV_ROWS = 64
CONV_DW_ROWS = 32
CONV_DW_UNROLL = 4
CONV_HALO = 16


def _fill_shifted(xp, sh, seq):
    for b in range(8):
        sh[b, pl.ds(0, seq + 24), :] = xp[pl.ds(b, seq + 24), :]


def _conv_fwd(proj3, w_dw, b_dw):
    _, n_seq, seq, _ = proj3.shape
    n_cb = D_CONV // LANES

    def body(a_ref, g_ref, w_ref, b_ref, uc_ref, xp, sh):
        zeros = jnp.zeros((CONV_HALO, LANES), F32)
        xp[pl.ds(0, CONV_HALO), :] = zeros
        xp[pl.ds(CONV_HALO + seq, CONV_HALO), :] = zeros
        xp[pl.ds(CONV_HALO, seq), :] = a_ref[...] * _sig(g_ref[...])
        _fill_shifted(xp, sh, seq)

        def blk(i, carry):
            t0 = pl.multiple_of(i * CONV_ROWS, CONV_ROWS)
            acc = jnp.zeros((CONV_ROWS, LANES), F32)
            for j in range(CONV_WIDTH):
                jj = j + 1
                acc = acc + sh[jj % 8, pl.ds(t0 + 8 * (jj // 8), CONV_ROWS), :] * w_ref[j:j + 1, :]
            uc_ref[pl.ds(t0, CONV_ROWS), :] = acc + b_ref[...]
            return carry

        lax.fori_loop(0, seq // CONV_ROWS, blk, 0)

    return _call(
        body, name="conv_fwd", grid=(n_seq, n_cb),
        in_specs=[pl.BlockSpec((None, None, seq, LANES), lambda b, cb: (cb, b, 0, 0)),
                  pl.BlockSpec((None, None, seq, LANES), lambda b, cb: (n_cb + cb, b, 0, 0)),
                  pl.BlockSpec((CONV_WIDTH, LANES), lambda b, cb: (0, cb)),
                  pl.BlockSpec((1, LANES), lambda b, cb: (0, cb))],
        out_specs=pl.BlockSpec((None, seq, LANES), lambda b, cb: (b, 0, cb)),
        out_shape=_sds((n_seq, seq, D_CONV), F32),
        scratch_shapes=[pltpu.VMEM((seq + 2 * CONV_HALO, LANES), F32),
                        pltpu.VMEM((8, seq + 2 * CONV_HALO, LANES), F32)],
        compiler_params=_params(("parallel", "parallel")),
    )(proj3, proj3, w_dw, b_dw)


def _conv_bwd(duc3, proj3, w_dw):
    _, n_seq, seq, _ = proj3.shape
    n_cb = D_CONV // LANES

    def body(duc_ref, a_ref, g_ref, w_ref, da_ref, dg_ref, dw_ref, db_ref, xp, sh):
        @pl.when(pl.program_id(1) == 0)
        def _():
            dw_ref[...] = jnp.zeros_like(dw_ref)
            db_ref[...] = jnp.zeros_like(db_ref)

        zeros = jnp.zeros((CONV_HALO, LANES), F32)
        xp[pl.ds(0, CONV_HALO), :] = zeros
        xp[pl.ds(CONV_HALO + seq, CONV_HALO), :] = zeros
        xp[pl.ds(CONV_HALO, seq), :] = a_ref[...] * _sig(g_ref[...])
        _fill_shifted(xp, sh, seq)
        for j0 in range(0, CONV_WIDTH, 8):
            taps = range(j0, min(j0 + 8, CONV_WIDTH))

            def wblk(i, accs, taps=taps):
                for u in range(CONV_DW_UNROLL):
                    t0 = pl.multiple_of((i * CONV_DW_UNROLL + u) * CONV_DW_ROWS, CONV_DW_ROWS)
                    d = duc_ref[pl.ds(t0, CONV_DW_ROWS), :]
                    accs = tuple(acc + d * sh[(j + 1) % 8, pl.ds(t0 + 8 * ((j + 1) // 8), CONV_DW_ROWS), :]
                                 for acc, j in zip(accs, taps))
                return accs

            accs = lax.fori_loop(0, seq // (CONV_DW_ROWS * CONV_DW_UNROLL), wblk,
                                 tuple(jnp.zeros((CONV_DW_ROWS, LANES), F32) for _ in taps))
            for acc, j in zip(accs, taps):
                dw_ref[j:j + 1, :] += jnp.sum(acc, axis=0, keepdims=True)
        db_ref[0:1, :] += jnp.sum(duc_ref[...], axis=0, keepdims=True)
        xp[pl.ds(CONV_HALO, seq), :] = duc_ref[...]
        _fill_shifted(xp, sh, seq)

        def ublk(i, carry):
            t0 = pl.multiple_of(i * CONV_ROWS, CONV_ROWS)
            acc = jnp.zeros((CONV_ROWS, LANES), F32)
            for j in range(CONV_WIDTH):
                jj = CONV_WIDTH - j
                acc = acc + sh[jj % 8, pl.ds(t0 + 8 * (jj // 8), CONV_ROWS), :] * w_ref[j:j + 1, :]
            av = a_ref[pl.ds(t0, CONV_ROWS), :]
            sg = _sig(g_ref[pl.ds(t0, CONV_ROWS), :])
            da_ref[pl.ds(t0, CONV_ROWS), :] = (acc * sg).astype(BF16)
            dg_ref[pl.ds(t0, CONV_ROWS), :] = (acc * av * sg * (1.0 - sg)).astype(BF16)
            return carry

        lax.fori_loop(0, seq // CONV_ROWS, ublk, 0)

    return _call(
        body, name="conv_bwd", grid=(n_cb, n_seq),
        in_specs=[pl.BlockSpec((None, seq, LANES), lambda cb, b: (b, 0, cb)),
                  pl.BlockSpec((None, None, seq, LANES), lambda cb, b: (cb, b, 0, 0)),
                  pl.BlockSpec((None, None, seq, LANES), lambda cb, b: (n_cb + cb, b, 0, 0)),
                  pl.BlockSpec((CONV_WIDTH, LANES), lambda cb, b: (0, cb))],
        out_specs=[pl.BlockSpec((None, seq, LANES), lambda cb, b: (b, 0, cb)),
                   pl.BlockSpec((None, seq, LANES), lambda cb, b: (b, 0, cb)),
                   pl.BlockSpec((32, LANES), lambda cb, b: (0, cb)),
                   pl.BlockSpec((8, LANES), lambda cb, b: (0, cb))],
        out_shape=[_sds((n_seq, seq, D_CONV), BF16), _sds((n_seq, seq, D_CONV), BF16),
                   _sds((32, D_CONV), F32), _sds((8, D_CONV), F32)],
        scratch_shapes=[pltpu.VMEM((seq + 2 * CONV_HALO, LANES), F32),
                        pltpu.VMEM((8, seq + 2 * CONV_HALO, LANES), F32)],
        compiler_params=_params(("parallel", "arbitrary")),
    )(duc3, proj3, proj3, w_dw)


MASKED = 1e30
ATT_ROWS = 512
ATT_UNROLL = 8
ATT_FWD_UNROLL = 8


def _distance_mats(dil, seg_len):
    kw = min(2 * Q_BLOCK, seg_len)
    offsets = (0, -RADIUS, -2 * RADIUS) if kw == 2 * Q_BLOCK else (0,)
    a = np.arange(Q_BLOCK)[:, None]
    b = np.arange(kw)[None, :]
    mats = []
    for off in offsets:
        rel = np.abs(b + off - a)
        mats.append(np.where(rel <= RADIUS, dil * rel, MASKED))
    return jnp.asarray(np.stack(mats).astype(np.float32))


def _alibi_rows():
    s = np.zeros((4, 8, LANES), np.float32)
    for hp in range(4):
        for hl in range(2):
            s[hp, hl, :] = 2.0 ** (-(2 * hp + hl + 1))
    return jnp.asarray(s)


def _window(n, seg_len):
    i0 = pl.multiple_of(n * Q_BLOCK, Q_BLOCK)
    if seg_len <= Q_BLOCK:
        return i0, i0, 0
    per_seg = seg_len // Q_BLOCK
    j = n % per_seg
    seg0 = (n // per_seg) * seg_len
    ks_local = jnp.clip(j * Q_BLOCK - RADIUS, 0, seg_len - 2 * Q_BLOCK)
    ks = pl.multiple_of(seg0 + ks_local, RADIUS)
    var = jnp.where(j == 0, 0, jnp.where(j == per_seg - 1, 2, 1))
    return i0, ks, var


def _first_head(rows):
    return lax.broadcasted_iota(jnp.int32, (rows, LANES), 1) < HEAD_DIM


def _same_head():
    head = np.arange(LANES) // HEAD_DIM
    return jnp.asarray((head[:, None] == head[None, :]).astype(np.float32)).astype(BF16)


def _head_sum(x, same_ref):
    hi = x.astype(BF16)
    lo = (x - hi.astype(F32)).astype(BF16)
    return (jnp.dot(hi, same_ref[...], preferred_element_type=F32)
            + jnp.dot(lo, same_ref[...], preferred_element_type=F32))


def _head_mean(x, same_ref):
    return _head_sum(x, same_ref) * (1.0 / HEAD_DIM)


def _per_head(x, first):
    swapped = pltpu.roll(x, HEAD_DIM, 1)
    return jnp.where(first, x, swapped), jnp.where(first, swapped, x)


STRIDE = 4


def _gather_segments(src, dil, seq, tmp, put):
    if dil == 1:
        put(0, seq, src[pl.ds(0, seq), :])
    elif dil == STRIDE:
        seg = seq // dil
        for r in range(dil):
            put(r * seg, seg, src[pl.ds(r, seg, stride=dil), :])
    else:
        part, seg = seq // STRIDE, seq // dil
        for b in range(STRIDE):
            tmp[pl.ds(b * part, part), :] = src[pl.ds(b, part, stride=STRIDE), :]
        for b in range(STRIDE):
            for a in range(dil // STRIDE):
                put(b * part + a * seg, seg, tmp[pl.ds(b * part + a, seg, stride=dil // STRIDE), :])


def _scatter_segments(dst, get, dil, seq, tmp, accumulate):
    def write(rows, val):
        if accumulate:
            dst[rows, :] += val
        else:
            dst[rows, :] = val

    if dil == 1:
        write(pl.ds(0, seq), get(0, seq))
    elif dil == STRIDE:
        seg = seq // dil
        for r in range(dil):
            write(pl.ds(r, seg, stride=dil), get(r * seg, seg))
    else:
        part, seg = seq // STRIDE, seq // dil
        for b in range(STRIDE):
            for a in range(dil // STRIDE):
                tmp[pl.ds(b * part + a, seg, stride=dil // STRIDE), :] = get(b * part + a * seg, seg)
        for b in range(STRIDE):
            write(pl.ds(b, part, stride=STRIDE), tmp[pl.ds(b * part, part), :])


def _permute_rows(dst, src, dil, seq, tmp):
    def put(start, size, val):
        dst[pl.ds(start, size), :] = val.astype(dst.dtype)

    _gather_segments(src, dil, seq, tmp, put)


def _permute_rows_by_head(dst, src, dil, seq, tmp):
    def put(start, size, val):
        first = _first_head(size)
        dst[0, pl.ds(start, size), :] = jnp.where(first, val, 0.0).astype(dst.dtype)
        dst[1, pl.ds(start, size), :] = jnp.where(first, 0.0, val).astype(dst.dtype)

    _gather_segments(src, dil, seq, tmp, put)


def _qk_normalise(q_ref, g2_ref, same_ref, dst, seq, scale):
    def chunk(ci, carry):
        rows = pl.ds(pl.multiple_of(ci * ATT_ROWS, ATT_ROWS), ATT_ROWS)
        qv = q_ref[rows, :]
        r = lax.rsqrt(_head_mean(qv * qv, same_ref) + EPS)
        dst[rows, :] = qv * r * (g2_ref[...] * scale)
        return carry

    lax.fori_loop(0, seq // ATT_ROWS, chunk, 0)


def _attn_fwd(proj3, g_q2, g_k2, ride):
    _, n_seq, seq, _ = proj3.shape
    dms = [_distance_mats(d, seq // d) for d in DILATIONS]
    same = _same_head()
    col0 = 2 * D_CONV // LANES
    n_hp = D_ATT // LANES

    n_ride = len(ride)
    assert not any(scatter for _, scatter in ride), "the forward's ride is an all-gather"

    def body(*refs):
        q_ref, k_ref, v_ref, gq_ref, gk_ref, sl_ref, dm1, dm4, dm16, same_ref = refs[:10]
        ride_in = refs[10:10 + n_ride]
        y_ref, lse_ref = refs[10 + n_ride:12 + n_ride]
        ride_out = refs[12 + n_ride:12 + 2 * n_ride]
        (qf, kf, qp, kp, vp, oml_p, o1, o4, o16, m1, m4, m16, l1, l4, l16,
         tmp) = refs[12 + 2 * n_ride:28 + 2 * n_ride]
        o_nat, m_nat, l_nat = (o1, o4, o16), (m1, m4, m16), (l1, l4, l16)
        ride_args = (ride_in, ride_out) + tuple(refs[28 + 2 * n_ride:])
        step = pl.program_id(0) * n_hp + pl.program_id(1)
        n_steps = n_seq * n_hp

        if n_ride:
            for phase, at in enumerate((0, (3 * n_steps) // 4)):
                @pl.when(step == at)
                def _(phase=phase):
                    _gather_by_chip_phase(phase, *ride_args)

        dm_refs = (dm1, dm4, dm16)
        _qk_normalise(q_ref, gq_ref, same_ref, qf, seq, HEAD_DIM ** -0.5)
        _qk_normalise(k_ref, gk_ref, same_ref, kf, seq, 1.0)
        slopes = (sl_ref[0:1, 0:1], sl_ref[1:2, 0:1])
        for pi, dil in enumerate(DILATIONS):
            seg = seq // dil
            kw = min(2 * Q_BLOCK, seg)
            _permute_rows_by_head(qp, qf, dil, seq, tmp)
            _permute_rows(kp, kf, dil, seq, tmp)
            _permute_rows(vp, v_ref, dil, seq, tmp)

            def blk(it, carry, seg=seg, kw=kw, pi=pi, dst=oml_p):
                first = _first_head(Q_BLOCK)
                chains = [(sub, h) for sub in range(ATT_FWD_UNROLL) for h in range(2)]
                win = [_window(it * ATT_FWD_UNROLL + sub, seg) for sub in range(ATT_FWD_UNROLL)]
                s = {}
                for sub, h in chains:
                    i0, ks, var = win[sub]
                    s[sub, h] = lax.dot_general(qp[h, pl.ds(i0, Q_BLOCK), :], kp[pl.ds(ks, kw), :], NT,
                                                preferred_element_type=F32) - slopes[h] * dm_refs[pi][var]
                m, l, p = {}, {}, {}
                for c in chains:
                    m[c] = jnp.max(s[c], axis=1, keepdims=True)
                    e = jnp.exp(s[c] - m[c])
                    l[c] = jnp.sum(e, axis=1, keepdims=True)
                    p[c] = e.astype(BF16)
                o = {}
                for sub, h in chains:
                    o[sub, h] = jnp.dot(p[sub, h], vp[pl.ds(win[sub][1], kw), :], preferred_element_type=F32)
                packed = [jnp.concatenate([jnp.where(first, t[sub, 0], t[sub, 1]) for t in (o, m, l)], axis=1)
                          for sub in range(ATT_FWD_UNROLL)]
                span = ATT_FWD_UNROLL * Q_BLOCK
                dst[pl.ds(pl.multiple_of(it * span, span), span), :] = jnp.concatenate(packed, axis=0)
                return carry

            lax.fori_loop(0, seq // (Q_BLOCK * ATT_FWD_UNROLL), blk, 0)
            for n, nat in enumerate((o_nat[pi], m_nat[pi], l_nat[pi])):
                _scatter_segments(nat, lambda start, size, n=n: oml_p[pl.ds(start, size), pl.ds(n * LANES, LANES)],
                                  dil, seq, tmp, accumulate=False)

        def merge(ci, carry):
            rows = pl.ds(pl.multiple_of(ci * ATT_ROWS, ATT_ROWS), ATT_ROWS)
            ms = [m_nat[pi][rows, :] for pi in range(3)]
            m_all = jnp.maximum(jnp.maximum(ms[0], ms[1]), ms[2])
            es = [jnp.exp(m - m_all) for m in ms]
            l_all = sum(l_nat[pi][rows, :] * es[pi] for pi in range(3))
            inv = 1.0 / l_all
            o = sum(o_nat[pi][rows, :] * (es[pi] * inv) for pi in range(3))
            y_ref[rows, :] = o.astype(BF16)
            lse_ref[rows, :] = m_all + jnp.log(l_all)
            return carry

        lax.fori_loop(0, seq // ATT_ROWS, merge, 0)

        if n_ride:
            @pl.when(step == n_steps - 1)
            def _():
                _gather_by_chip_phase(2, *ride_args)

    def col(off):
        return pl.BlockSpec((None, None, seq, LANES), lambda b, hp: (col0 + off * n_hp + hp, b, 0, 0))

    def whole(arr):
        return pl.BlockSpec(arr.shape, lambda b, hp: (0,) * arr.ndim)

    rows_f32 = pltpu.VMEM((seq, LANES), F32)
    rows_bf16 = pltpu.VMEM((seq, LANES), BF16)
    return _call(
        body, name="attn_fwd", grid=(n_seq, n_hp),
        in_specs=[col(0), col(1), col(2), whole(g_q2), whole(g_k2),
                  pl.BlockSpec((None, 8, LANES), lambda b, hp: (hp, 0, 0)),
                  whole(dms[0]), whole(dms[1]), whole(dms[2]), whole(same)] + [ANY_SPEC] * n_ride,
        out_specs=[pl.BlockSpec((None, seq, LANES), lambda b, hp: (b, 0, hp)),
                   pl.BlockSpec((None, seq, LANES), lambda b, hp: (b, 0, hp))] + [ANY_SPEC] * n_ride,
        out_shape=[_sds((n_seq, seq, D_ATT), BF16), _sds((n_seq, seq, D_ATT), F32)] + _exchange_shapes(ride),
        scratch_shapes=[rows_f32, rows_f32, pltpu.VMEM((2, seq, LANES), BF16), rows_bf16, rows_bf16]
        + [pltpu.VMEM((seq, 3 * LANES), F32)] + [rows_f32] * 10 + _exchange_sems(n_ride),
        compiler_params=_params(("arbitrary", "arbitrary")),
    )(proj3, proj3, proj3, g_q2, g_k2, _alibi_rows(), *dms, same, *[a for a, _ in ride])


def _attn_bwd(proj3, do3, y_att3, lse3, g_q2, g_k2, ride):
    _, n_seq, seq, _ = proj3.shape
    dms = [_distance_mats(d, seq // d) for d in DILATIONS]
    same = _same_head()
    col0 = 2 * D_CONV // LANES
    n_hp = D_ATT // LANES

    n_ride = len(ride)
    ride_scatter = [s for _, s in ride]

    def body(*refs):
        (q_ref, k_ref, v_ref, do_ref, o_ref, lse_ref, gq_ref, gk_ref, sl_ref, dm1, dm4, dm16,
         same_ref) = refs[:13]
        ride_in = refs[13:13 + n_ride]
        dq_ref, dk_ref, dv_ref, dg_ref = refs[13 + n_ride:17 + n_ride]
        ride_out = refs[17 + n_ride:17 + 2 * n_ride]
        (qf, kf, qp, dop, kp, vp, sn, sp, dqp, dkp, dvp, dqn, dkn, dvn,
         tmp) = refs[17 + 2 * n_ride:32 + 2 * n_ride]
        ride_args = (ride_scatter, ride_in, ride_out) + tuple(refs[32 + 2 * n_ride:])
        dm_refs = (dm1, dm4, dm16)
        step = pl.program_id(0) * n_hp + pl.program_id(1)

        @pl.when(step == 0)
        def _():
            _exchange_start(*ride_args)
            dg_ref[...] = jnp.zeros_like(dg_ref)

        _qk_normalise(q_ref, gq_ref, same_ref, qf, seq, HEAD_DIM ** -0.5)
        _qk_normalise(k_ref, gk_ref, same_ref, kf, seq, 1.0)

        def stats(ci, carry):
            rows = pl.ds(pl.multiple_of(ci * ATT_ROWS, ATT_ROWS), ATT_ROWS)
            first = _first_head(ATT_ROWS)
            sn[0, rows, :], sn[1, rows, :] = _per_head(lse_ref[rows, :], first)
            prod = do_ref[rows, :] * o_ref[rows, :].astype(F32)
            sn[2, rows, :], sn[3, rows, :] = _per_head(_head_sum(prod, same_ref), first)
            return carry

        lax.fori_loop(0, seq // ATT_ROWS, stats, 0)
        slopes = (sl_ref[0:1, 0:1], sl_ref[1:2, 0:1])
        half = seq // (Q_BLOCK * ATT_UNROLL)
        region = seq // ATT_UNROLL

        for pi, dil in enumerate(DILATIONS):
            seg = seq // dil
            kw = min(2 * Q_BLOCK, seg)
            _permute_rows_by_head(qp, qf, dil, seq, tmp)
            _permute_rows_by_head(dop, do_ref, dil, seq, tmp)
            _permute_rows(kp, kf, dil, seq, tmp)
            _permute_rows(vp, v_ref, dil, seq, tmp)
            if dil == 1:
                st = sn
            else:
                st = sp
                for n in range(4):
                    _permute_rows(sp.at[n], sn.at[n], dil, seq, tmp)
            def touched(sub, seg=seg):
                lo, hi = sub * region, (sub + 1) * region
                if seg < region:
                    return lo, hi
                seg0 = lo // seg * seg
                return max(lo - RADIUS, seg0), min(hi + RADIUS, seg0 + seg)

            def summed(acc, start, size, touched=touched):
                pieces = []
                for c0 in range(start, start + size, RADIUS):
                    owners = [s for s in range(ATT_UNROLL) if touched(s)[0] <= c0 and c0 + RADIUS <= touched(s)[1]]
                    if pieces and pieces[-1][2] == owners:
                        pieces[-1][1] += RADIUS
                    else:
                        pieces.append([c0, RADIUS, owners])
                vals = [sum(acc[o, pl.ds(c0, n), :] for o in owners) for c0, n, owners in pieces]
                return vals[0] if len(vals) == 1 else jnp.concatenate(vals, axis=0)

            for sub in range(ATT_UNROLL):
                lo, hi = touched(sub)
                dkp[sub, pl.ds(lo, hi - lo), :] = jnp.zeros((hi - lo, LANES), F32)
                dvp[sub, pl.ds(lo, hi - lo), :] = jnp.zeros((hi - lo, LANES), F32)

            def blk(it, carry, seg=seg, kw=kw, pi=pi, st=st):
                first = _first_head(Q_BLOCK)
                chains = [(sub, h) for sub in range(ATT_UNROLL) for h in range(2)]
                win = [_window(it + sub * half, seg) for sub in range(ATT_UNROLL)]
                qrows = [pl.ds(w[0], Q_BLOCK) for w in win]
                krows = [pl.ds(w[1], kw) for w in win]

                def over_keys(n, sub):
                    t = st[n, qrows[sub], :]
                    return t if kw == LANES else jnp.concatenate([t] * (kw // LANES), axis=1)

                s, dp = {}, {}
                for sub, h in chains:
                    s[sub, h] = lax.dot_general(qp[h, qrows[sub], :], kp[krows[sub], :], NT,
                                                preferred_element_type=F32) - slopes[h] * dm_refs[pi][win[sub][2]]
                    dp[sub, h] = lax.dot_general(dop[h, qrows[sub], :], vp[krows[sub], :], NT,
                                                 preferred_element_type=F32)
                p, ds = {}, {}
                for sub, h in chains:
                    e = jnp.exp(s[sub, h] - over_keys(h, sub))
                    ds[sub, h] = (e * (dp[sub, h] - over_keys(2 + h, sub))).astype(BF16)
                    p[sub, h] = e.astype(BF16)
                dq, dk, dv = {}, {}, {}
                for sub, h in chains:
                    dq[sub, h] = jnp.dot(ds[sub, h], kp[krows[sub], :], preferred_element_type=F32)
                    dk[sub, h] = lax.dot_general(ds[sub, h], qp[h, qrows[sub], :], TN, preferred_element_type=F32)
                    dv[sub, h] = lax.dot_general(p[sub, h], dop[h, qrows[sub], :], TN, preferred_element_type=F32)
                for sub in range(ATT_UNROLL):
                    dqp[qrows[sub], :] = jnp.where(first, dq[sub, 0], dq[sub, 1])
                    dkp[sub, krows[sub], :] += dk[sub, 0] + dk[sub, 1]
                    dvp[sub, krows[sub], :] += dv[sub, 0] + dv[sub, 1]
                return carry

            lax.fori_loop(0, half, blk, 0)
            first_pattern = pi == 0
            if first_pattern:
                for r0 in range(0, seq, region):
                    rows = pl.ds(r0, region)
                    dqn[rows, :] = dqp[rows, :]
                    dkn[rows, :] = summed(dkp, r0, region)
                    dvn[rows, :] = summed(dvp, r0, region)
            else:
                _scatter_segments(dqn, lambda start, size: dqp[pl.ds(start, size), :], dil, seq, tmp, accumulate=True)
                for nat, acc in ((dkn, dkp), (dvn, dvp)):
                    _scatter_segments(nat, lambda start, size, acc=acc: summed(acc, start, size),
                                      dil, seq, tmp, accumulate=True)

        def finish(ci, carry):
            rows = pl.ds(pl.multiple_of(ci * ATT_ROWS, ATT_ROWS), ATT_ROWS)
            for src_ref, g_ref, dn, dst_ref, scale, row in (
                    (q_ref, gq_ref, dqn, dq_ref, HEAD_DIM ** -0.5, 0), (k_ref, gk_ref, dkn, dk_ref, 1.0, 1)):
                xv = src_ref[rows, :]
                r = lax.rsqrt(_head_mean(xv * xv, same_ref) + EPS)
                xhat = xv * r
                d = dn[rows, :] * scale
                dg_ref[row:row + 1, :] += jnp.sum(d * xhat, axis=0, keepdims=True)
                dxh = d * g_ref[...]
                dst_ref[rows, :] = (r * (dxh - xhat * _head_mean(dxh * xhat, same_ref))).astype(BF16)
            dv_ref[rows, :] = dvn[rows, :].astype(BF16)
            return carry

        lax.fori_loop(0, seq // ATT_ROWS, finish, 0)

        @pl.when(step == n_seq * n_hp - 1)
        def _():
            _exchange_wait(*ride_args)

    def col(off):
        return pl.BlockSpec((None, None, seq, LANES), lambda b, hp: (col0 + off * n_hp + hp, b, 0, 0))

    def whole(arr):
        return pl.BlockSpec(arr.shape, lambda b, hp: (0,) * arr.ndim)

    att = pl.BlockSpec((None, seq, LANES), lambda b, hp: (b, 0, hp))
    rows_f32 = pltpu.VMEM((seq, LANES), F32)
    rows_bf16 = pltpu.VMEM((seq, LANES), BF16)
    by_head_bf16 = pltpu.VMEM((2, seq, LANES), BF16)
    per_sub_f32 = pltpu.VMEM((ATT_UNROLL, seq, LANES), F32)
    stats_f32 = pltpu.VMEM((4, seq, LANES), F32)
    return _call(
        body, name="attn_bwd", grid=(n_seq, n_hp),
        in_specs=[col(0), col(1), col(2), att, att, att, whole(g_q2), whole(g_k2),
                  pl.BlockSpec((None, 8, LANES), lambda b, hp: (hp, 0, 0)),
                  whole(dms[0]), whole(dms[1]), whole(dms[2]), whole(same)] + [ANY_SPEC] * n_ride,
        out_specs=[att, att, att, pl.BlockSpec((8, LANES), lambda b, hp: (0, 0))] + [ANY_SPEC] * n_ride,
        out_shape=[_sds((n_seq, seq, D_ATT), BF16)] * 3 + [_sds((8, LANES), F32)] + _exchange_shapes(ride),
        scratch_shapes=[rows_f32, rows_f32, by_head_bf16, by_head_bf16, rows_bf16, rows_bf16, stats_f32, stats_f32,
                        rows_f32, per_sub_f32, per_sub_f32, rows_f32, rows_f32, rows_f32, rows_f32]
        + _exchange_sems(n_ride),
        compiler_params=_params(("arbitrary", "arbitrary")),
    )(proj3, proj3, proj3, do3, y_att3, lse3, g_q2, g_k2, _alibi_rows(), *dms, same, *[a for a, _ in ride])


def _mix_out(uc2, y_att2, x2, mod8, g_ln, b_ln, g_ffn, w_out, seq, tm=512):
    tokens = x2.shape[0]
    per_seq = seq // tm

    def body(uc_ref, ya_ref, x_ref, m_ref, gl_ref, bl_ref, gf_ref, w_ref, yc_ref, mix_ref, x1_ref, h2_ref):
        uc = uc_ref[...]
        mu = jnp.mean(uc, axis=-1, keepdims=True)
        cen = uc - mu
        rs = lax.rsqrt(jnp.mean(cen * cen, axis=-1, keepdims=True) + EPS)
        z = cen * rs * gl_ref[...] + bl_ref[...]
        yc = (z * _sig(z)).astype(BF16)
        yc_ref[...] = yc
        mix = (jnp.dot(yc, w_ref[pl.ds(0, D_CONV), :], preferred_element_type=F32)
               + jnp.dot(ya_ref[...], w_ref[pl.ds(D_CONV, D_ATT), :], preferred_element_type=F32))
        mix_ref[...] = mix
        x1 = x_ref[...] + m_ref[2:3, :] * mix
        x1_ref[...] = x1
        r = lax.rsqrt(jnp.mean(x1 * x1, axis=-1, keepdims=True) + EPS)
        h2_ref[...] = ((x1 * r * gf_ref[...]) * (1.0 + m_ref[4:5, :]) + m_ref[3:4, :]).astype(BF16)

    def rows(width):
        return pl.BlockSpec((tm, width), lambda i: (i, 0))

    def vec(width):
        return pl.BlockSpec((1, width), lambda i: (0, 0))

    return _call(
        body, name="mix_out", grid=(tokens // tm,),
        in_specs=[rows(D_CONV), rows(D_ATT), rows(D_MODEL),
                  pl.BlockSpec((None, 8, D_MODEL), lambda i: (i // per_seq, 0, 0)),
                  vec(D_CONV), vec(D_CONV), vec(D_MODEL),
                  pl.BlockSpec((D_MODEL, D_MODEL), lambda i: (0, 0))],
        out_specs=[rows(D_CONV), rows(D_MODEL), rows(D_MODEL), rows(D_MODEL)],
        out_shape=[_sds((tokens, D_CONV), BF16), _sds((tokens, D_MODEL), F32),
                   _sds((tokens, D_MODEL), F32), _sds((tokens, D_MODEL), BF16)],
        compiler_params=_params(("parallel",)),
    )(uc2, y_att2, x2, mod8, g_ln, b_ln, g_ffn, w_out)


def _mix_out_bwd(dmix, uc2, g_ln, b_ln, w_out, tm=512):
    tokens = dmix.shape[0]

    def body(dm_ref, uc_ref, gl_ref, bl_ref, w_ref, duc_ref, do_ref, dgb_ref):
        @pl.when(pl.program_id(0) == 0)
        def _():
            dgb_ref[...] = jnp.zeros_like(dgb_ref)

        dmv = dm_ref[...]
        dyc = lax.dot_general(dmv, w_ref[pl.ds(0, D_CONV), :], NT, preferred_element_type=F32)
        do_ref[...] = lax.dot_general(dmv, w_ref[pl.ds(D_CONV, D_ATT), :], NT, preferred_element_type=F32)
        uc = uc_ref[...]
        mu = jnp.mean(uc, axis=-1, keepdims=True)
        cen = uc - mu
        rs = lax.rsqrt(jnp.mean(cen * cen, axis=-1, keepdims=True) + EPS)
        xh = cen * rs
        z = xh * gl_ref[...] + bl_ref[...]
        sg = _sig(z)
        dz = dyc * (sg * (1.0 + z * (1.0 - sg)))
        dgb_ref[0:1, :] += jnp.sum(dz * xh, axis=0, keepdims=True)
        dgb_ref[1:2, :] += jnp.sum(dz, axis=0, keepdims=True)
        dxh = dz * gl_ref[...]
        duc_ref[...] = rs * (dxh - jnp.mean(dxh, axis=-1, keepdims=True)
                             - xh * jnp.mean(dxh * xh, axis=-1, keepdims=True))

    return _call(
        body, name="mix_out_bwd", grid=(tokens // tm,),
        in_specs=[pl.BlockSpec((tm, D_MODEL), lambda i: (i, 0)),
                  pl.BlockSpec((tm, D_CONV), lambda i: (i, 0)),
                  pl.BlockSpec((1, D_CONV), lambda i: (0, 0)),
                  pl.BlockSpec((1, D_CONV), lambda i: (0, 0)),
                  pl.BlockSpec((D_MODEL, D_MODEL), lambda i: (0, 0))],
        out_specs=[pl.BlockSpec((tm, D_CONV), lambda i: (i, 0)),
                   pl.BlockSpec((tm, D_ATT), lambda i: (i, 0)),
                   pl.BlockSpec((8, D_CONV), lambda i: (0, 0))],
        out_shape=[_sds((tokens, D_CONV), F32), _sds((tokens, D_ATT), F32), _sds((8, D_CONV), F32)],
        compiler_params=_params(("arbitrary",)),
    )(dmix, uc2, g_ln, b_ln, w_out)


FF_TILE = 256
FF_TILES = D_FF // FF_TILE
FF_ROWS = 256


def _ffn_fwd(h2, w_gate, w_up, w_down, x1, target, mod8, seq, tm=1024):
    tokens = h2.shape[0]
    per_seq = seq // tm
    n_seq = tokens // seq
    last = D_FF // FF_TILE - 1

    def body(h_ref, wg_ref, wu_ref, wd_ref, x1_ref, t_ref, m_ref, gate_ref, up_ref, dy_ref, df_ref, sq_ref, dgf_ref,
             f_ref):
        i, j = pl.program_id(0), pl.program_id(1)

        @pl.when(j == 0)
        def _():
            f_ref[...] = jnp.zeros_like(f_ref)

        @pl.when((j == 0) & (i == 0))
        def _():
            sq_ref[...] = jnp.zeros_like(sq_ref)

        @pl.when((j == 0) & (i % per_seq == 0))
        def _():
            dgf_ref[...] = jnp.zeros_like(dgf_ref)

        def gate_up(r):
            hv = h_ref[pl.ds(r * FF_ROWS, FF_ROWS), :]
            return (lax.dot_general(hv, wg_ref[...], NT, preferred_element_type=F32),
                    lax.dot_general(hv, wu_ref[...], NT, preferred_element_type=F32))

        ahead = gate_up(0)
        for r in range(tm // FF_ROWS):
            gate, up = ahead
            if r + 1 < tm // FF_ROWS:
                ahead = gate_up(r + 1)
            rows = pl.ds(r * FF_ROWS, FF_ROWS)
            gate_ref[rows, :] = gate.astype(BF16)
            up_ref[rows, :] = up.astype(BF16)
            act = (gate * _sig(gate) * up).astype(BF16)
            f_ref[rows, :] += jnp.dot(act, wd_ref[...], preferred_element_type=F32)

        @pl.when(j == last)
        def _():
            gate_f = m_ref[5:6, :]
            for r in range(tm // FF_ROWS):
                rows = pl.ds(r * FF_ROWS, FF_ROWS)
                fv = f_ref[rows, :]
                diff = x1_ref[rows, :] + gate_f * fv - t_ref[rows, :]
                sq_ref[0:1, :] += jnp.sum(diff * diff, axis=0, keepdims=True)
                dy = diff * (1.0 / D_MODEL)
                dy_ref[rows, :] = dy
                df_ref[rows, :] = (gate_f * dy).astype(BF16)
                dgf_ref[0:1, :] += jnp.sum(dy * fv, axis=0, keepdims=True)

    rows_spec = pl.BlockSpec((tm, D_MODEL), lambda i, j: (i, 0))
    per = pl.BlockSpec((None, 8, D_MODEL), lambda i, j: (i // per_seq, 0, 0))
    tile = pl.BlockSpec((None, tm, FF_TILE), lambda i, j: (j, i, 0))
    w_tile = pl.BlockSpec((FF_TILE, D_MODEL), lambda i, j: (j, 0))
    return _call(
        body, name="ffn_fwd", grid=(tokens // tm, D_FF // FF_TILE),
        in_specs=[rows_spec, w_tile, w_tile,
                  pl.BlockSpec((FF_TILE, D_MODEL), lambda i, j: (j, 0)),
                  rows_spec, rows_spec, per],
        out_specs=[tile, tile, rows_spec, rows_spec, pl.BlockSpec((8, D_MODEL), lambda i, j: (0, 0)), per],
        out_shape=[_sds((FF_TILES, tokens, FF_TILE), BF16), _sds((FF_TILES, tokens, FF_TILE), BF16),
                   _sds((tokens, D_MODEL), F32),
                   _sds((tokens, D_MODEL), BF16), _sds((8, D_MODEL), F32), _sds((n_seq, 8, D_MODEL), F32)],
        scratch_shapes=[pltpu.VMEM((tm, D_MODEL), F32)],
        compiler_params=_params(("arbitrary", "arbitrary")),
    )(h2, w_gate, w_up, w_down, x1, target, mod8)


def _ffn_bwd(df, gate, up, w_gate, w_up, w_down, x1, dy, mix, mod8, g_ffn, seq, tm=1024):
    tokens = df.shape[0]
    per_seq = seq // tm
    n_seq = tokens // seq
    last = D_FF // FF_TILE - 1

    def body(df_ref, gate_ref, up_ref, wg_ref, wu_ref, wd_ref, m_ref, g_ref, x1_hbm, dy_hbm, mix_hbm,
             dgate_ref, dup_ref, act_ref, dx1_ref, dmix_ref, dg_ref, dm_ref, dh_ref, late, late_sems):
        i, j = pl.program_id(0), pl.program_id(1)
        my_rows = pl.ds(pl.multiple_of(i * tm, tm), tm)
        fetches = [pltpu.make_async_copy(src.at[my_rows, :], late.at[n], late_sems.at[n])
                   for n, src in enumerate((x1_hbm, dy_hbm, mix_hbm))]

        @pl.when(j == 0)
        def _():
            dh_ref[...] = jnp.zeros_like(dh_ref)
            for cp in fetches:
                cp.start()

        @pl.when((j == 0) & (i == 0))
        def _():
            dg_ref[...] = jnp.zeros_like(dg_ref)

        @pl.when((j == 0) & (i % per_seq == 0))
        def _():
            dm_ref[...] = jnp.zeros_like(dm_ref)

        def d_act(r):
            return lax.dot_general(df_ref[pl.ds(r * FF_ROWS, FF_ROWS), :], wd_ref[...], NT,
                                   preferred_element_type=F32)

        ahead = d_act(0)
        for r in range(tm // FF_ROWS):
            dact = ahead
            if r + 1 < tm // FF_ROWS:
                ahead = d_act(r + 1)
            rows = pl.ds(r * FF_ROWS, FF_ROWS)
            gate = gate_ref[rows, :].astype(F32)
            up = up_ref[rows, :].astype(F32)
            sg = _sig(gate)
            silu = gate * sg
            act_ref[rows, :] = (silu * up).astype(BF16)
            dup = (dact * silu).astype(BF16)
            dgate = (dact * up * (sg * (1.0 + gate * (1.0 - sg)))).astype(BF16)
            dup_ref[rows, :] = dup
            dgate_ref[rows, :] = dgate
            dh_ref[rows, :] += (jnp.dot(dgate, wg_ref[...], preferred_element_type=F32)
                                + jnp.dot(dup, wu_ref[...], preferred_element_type=F32))

        @pl.when(j == last)
        def _():
            for cp in fetches:
                cp.wait()
            g = g_ref[...]
            for r in range(tm // FF_ROWS):
                rows = pl.ds(r * FF_ROWS, FF_ROWS)
                dh = dh_ref[rows, :]
                x1v = late[0, rows, :]
                rs = lax.rsqrt(jnp.mean(x1v * x1v, axis=-1, keepdims=True) + EPS)
                xhat = x1v * rs
                dm_ref[0:1, :] += jnp.sum(dh, axis=0, keepdims=True)
                dm_ref[1:2, :] += jnp.sum(dh * (xhat * g), axis=0, keepdims=True)
                dn = dh * (1.0 + m_ref[4:5, :])
                dg_ref[0:1, :] += jnp.sum(dn * xhat, axis=0, keepdims=True)
                dxh = dn * g
                dx1 = late[1, rows, :] + rs * (dxh - xhat * jnp.mean(dxh * xhat, axis=-1, keepdims=True))
                dx1_ref[rows, :] = dx1
                dm_ref[2:3, :] += jnp.sum(dx1 * late[2, rows, :], axis=0, keepdims=True)
                dmix_ref[rows, :] = (m_ref[2:3, :] * dx1).astype(BF16)

    tile = pl.BlockSpec((None, tm, FF_TILE), lambda i, j: (j, i, 0))
    w_tile = pl.BlockSpec((FF_TILE, D_MODEL), lambda i, j: (j, 0))
    rows_spec = pl.BlockSpec((tm, D_MODEL), lambda i, j: (i, 0))
    per = pl.BlockSpec((None, 8, D_MODEL), lambda i, j: (i // per_seq, 0, 0))
    return _call(
        body, name="ffn_bwd", grid=(tokens // tm, D_FF // FF_TILE),
        in_specs=[rows_spec, tile, tile, w_tile, w_tile,
                  pl.BlockSpec((FF_TILE, D_MODEL), lambda i, j: (j, 0)),
                  per, pl.BlockSpec((1, D_MODEL), lambda i, j: (0, 0)), ANY_SPEC, ANY_SPEC, ANY_SPEC],
        out_specs=[tile, tile, tile, rows_spec, rows_spec, pl.BlockSpec((8, D_MODEL), lambda i, j: (0, 0)), per],
        out_shape=[_sds((FF_TILES, tokens, FF_TILE), BF16)] * 3
        + [_sds((tokens, D_MODEL), F32), _sds((tokens, D_MODEL), BF16),
           _sds((8, D_MODEL), F32), _sds((n_seq, 8, D_MODEL), F32)],
        scratch_shapes=[pltpu.VMEM((tm, D_MODEL), F32), pltpu.VMEM((3, tm, D_MODEL), F32),
                        pltpu.SemaphoreType.DMA((3,))],
        compiler_params=_params(("arbitrary", "arbitrary")),
    )(df, gate, up, w_gate, w_up, w_down, mod8, g_ffn, x1, dy, mix)


def _mix_in_bwd(d_a, d_g, d_q, d_k, d_v, w_in, x2, dx1, mod8, g_mix, seq, ride, tm=512):
    tokens = x2.shape[0]
    per_seq = seq // tm
    n_seq = tokens // seq
    parts = (d_a, d_g, d_q, d_k, d_v)
    width = D_CONV
    n_ride = len(ride)
    ride_scatter = [s for _, s in ride]

    def body(*refs):
        da_ref, dg_ref, dq_ref, dk_ref, dv_ref, w_ref, x_ref, dx1_ref, m_ref, g_ref = refs[:10]
        ride_in = refs[10:10 + n_ride]
        gx_ref, dgm_ref, dm_ref = refs[10 + n_ride:13 + n_ride]
        ride_args = (ride_scatter, ride_in, refs[13 + n_ride:13 + 2 * n_ride]) + tuple(refs[13 + 2 * n_ride:])
        i = pl.program_id(0)

        @pl.when(i == 0)
        def _():
            _exchange_start(*ride_args)
            dgm_ref[...] = jnp.zeros_like(dgm_ref)

        @pl.when(i % per_seq == 0)
        def _():
            dm_ref[...] = jnp.zeros_like(dm_ref)

        dh = jnp.zeros((tm, D_MODEL), F32)
        for n, ref in enumerate((da_ref, dg_ref, dq_ref, dk_ref, dv_ref)):
            dh = dh + jnp.dot(ref[...], w_ref[pl.ds(n * width, width), :], preferred_element_type=F32)
        xv = x_ref[...]
        r = lax.rsqrt(jnp.mean(xv * xv, axis=-1, keepdims=True) + EPS)
        xhat = xv * r
        g = g_ref[...]
        dm_ref[0:1, :] += jnp.sum(dh, axis=0, keepdims=True)
        dm_ref[1:2, :] += jnp.sum(dh * (xhat * g), axis=0, keepdims=True)
        dn = dh * (1.0 + m_ref[1:2, :])
        dgm_ref[0:1, :] += jnp.sum(dn * xhat, axis=0, keepdims=True)
        dxh = dn * g
        gx_ref[...] = dx1_ref[...] + r * (dxh - xhat * jnp.mean(dxh * xhat, axis=-1, keepdims=True))

        @pl.when(i == tokens // tm - 1)
        def _():
            _exchange_wait(*ride_args)

    rows = pl.BlockSpec((tm, D_MODEL), lambda i: (i, 0))
    half = pl.BlockSpec((tm, width), lambda i: (i, 0))
    per = pl.BlockSpec((None, 8, D_MODEL), lambda i: (i // per_seq, 0, 0))
    return _call(
        body, name="mix_in_bwd", grid=(tokens // tm,),
        in_specs=[half] * 5 + [pl.BlockSpec((D_IN, D_MODEL), lambda i: (0, 0)), rows, rows, per,
                               pl.BlockSpec((1, D_MODEL), lambda i: (0, 0))] + [ANY_SPEC] * n_ride,
        out_specs=[rows, pl.BlockSpec((8, D_MODEL), lambda i: (0, 0)), per] + [ANY_SPEC] * n_ride,
        out_shape=[_sds((tokens, D_MODEL), F32), _sds((8, D_MODEL), F32), _sds((n_seq, 8, D_MODEL), F32)]
        + _exchange_shapes(ride),
        scratch_shapes=_exchange_sems(n_ride),
        compiler_params=_params(("arbitrary",)),
    )(*parts, w_in, x2, dx1, mod8, g_mix, *[a for a, _ in ride])


def _grad_matmul_parts(a_parts, b_parts, name, tk=1024):
    tokens = a_parts[0].shape[0]
    na, nb = len(a_parts), len(b_parts)
    ma, nbw = a_parts[0].shape[1], b_parts[0].shape[1]

    n_k = tokens // tk

    def body(*refs):
        a_refs, b_refs, o_ref, acc = refs[:na], refs[na:na + nb], refs[na + nb], refs[na + nb + 1]

        @pl.when(pl.program_id(0) == 0)
        def _():
            acc[...] = jnp.zeros_like(acc)

        for i in range(na):
            for j in range(nb):
                acc[pl.ds(i * ma, ma), pl.ds(j * nbw, nbw)] += lax.dot_general(
                    a_refs[i][...], b_refs[j][...], TN, preferred_element_type=F32)

        @pl.when(pl.program_id(0) == n_k - 1)
        def _():
            o_ref[...] = acc[...].astype(o_ref.dtype)

    return _call(
        body, name=name, grid=(n_k,),
        in_specs=[pl.BlockSpec((tk, ma), lambda k: (k, 0))] * na + [pl.BlockSpec((tk, nbw), lambda k: (k, 0))] * nb,
        out_specs=pl.BlockSpec((na * ma, nb * nbw), lambda k: (0, 0)),
        out_shape=_sds((na * ma, nb * nbw), BF16),
        scratch_shapes=[pltpu.VMEM((na * ma, nb * nbw), F32)],
        compiler_params=_params(("arbitrary",)),
    )(*a_parts, *b_parts)


def _grad_matmul_tiles(a, b, name, tk=1024):
    tiled_b = b.ndim == 3
    tiles, tokens, width = b.shape if tiled_b else a.shape
    other = a.shape[1] if tiled_b else b.shape[1]
    out_tile = (other, width) if tiled_b else (width, other)
    n_k = tokens // tk

    def body(a_ref, b_ref, o_ref, acc):
        @pl.when(pl.program_id(0) == 0)
        def _():
            acc[...] = jnp.zeros_like(acc)

        for t in range(tiles):
            lhs = a_ref[...] if tiled_b else a_ref[t]
            rhs = b_ref[t] if tiled_b else b_ref[...]
            acc[t] += lax.dot_general(lhs, rhs, TN, preferred_element_type=F32)

        @pl.when(pl.program_id(0) == n_k - 1)
        def _():
            o_ref[...] = acc[...].astype(o_ref.dtype)

    flat = pl.BlockSpec((tk, other), lambda k: (k, 0))
    tiled = pl.BlockSpec((tiles, tk, width), lambda k: (0, k, 0))
    return _call(
        body, name=name, grid=(n_k,),
        in_specs=[flat, tiled] if tiled_b else [tiled, flat],
        out_specs=pl.BlockSpec((tiles,) + out_tile, lambda k: (0, 0, 0)),
        out_shape=_sds((tiles,) + out_tile, BF16),
        scratch_shapes=[pltpu.VMEM((tiles,) + out_tile, F32)],
        compiler_params=_params(("arbitrary",)),
    )(a, b)


def _adamw(w, m, v, g, name, n_parts=0, tr=256):
    rows, cols = w.shape
    tr = min(tr, rows)
    c1 = 1.0 - ADAM_B1 ** ADAM_STEP
    c2 = 1.0 - ADAM_B2 ** ADAM_STEP

    def body(w_ref, m_ref, v_ref, g_ref, go_ref, d_ref, mo_ref, vo_ref):
        if n_parts:
            gv = g_ref[0].astype(F32)
            for p in range(1, n_parts):
                gv = gv + g_ref[p].astype(F32)
        else:
            gv = g_ref[...]
        go_ref[...] = gv
        mn = ADAM_B1 * m_ref[...] + (1.0 - ADAM_B1) * gv
        vn = ADAM_B2 * v_ref[...] + (1.0 - ADAM_B2) * (gv * gv)
        mo_ref[...] = mn
        vo_ref[...] = vn
        d_ref[...] = -ADAM_LR * ((mn / c1) / (jnp.sqrt(vn / c2) + ADAM_EPS) + ADAM_WD * w_ref[...])

    blk = pl.BlockSpec((tr, cols), lambda i: (i, 0))
    g_spec = pl.BlockSpec((n_parts, tr, cols), lambda i: (0, i, 0)) if n_parts else blk
    return _call(
        body, name=name, grid=(rows // tr,),
        in_specs=[blk, blk, blk, g_spec], out_specs=[blk] * 4,
        out_shape=[_sds((rows, cols), F32)] * 4,
        compiler_params=_params(("parallel",)),
    )(w, m, v, g)


def _cols_to_full(blocks):
    n, r, c = blocks.shape
    return jnp.transpose(blocks, (1, 0, 2)).reshape(r, n * c)


def _pad_lanes(v, width):
    return jnp.pad(v, ((0, 0), (0, width - v.shape[1])))


def kernel(x, c, w_ada, b_ada, g_mix, w_in, w_dw, b_dw, g_conv_ln, b_conv_ln, g_q, g_k, w_out, g_ffn, w_gate, w_up, w_down, loss_target, m_w_ada, m_b_ada, m_g_mix, m_w_in, m_w_dw, m_b_dw, m_g_conv_ln, m_b_conv_ln, m_g_q, m_g_k, m_w_out, m_g_ffn, m_w_gate, m_w_up, m_w_down, v_w_ada, v_b_ada, v_g_mix, v_w_in, v_w_dw, v_b_dw, v_g_conv_ln, v_b_conv_ln, v_g_q, v_g_k, v_w_out, v_g_ffn, v_w_gate, v_w_up, v_w_down):
    n_seq, seq, _ = x.shape
    tokens = n_seq * seq
    me = 4 * lax.axis_index("x") + 2 * lax.axis_index("y") + lax.axis_index("c")
    ada_cols = w_ada.shape[2]
    dw_cols = w_dw.shape[2]

    def transposed(w):
        return jnp.transpose(w[0])

    (c_g, w_in_g, w_dw_g) = _gather_by_chip([c, transposed(w_in).astype(BF16), w_dw[0]], "gather_weights")
    c_all = c_g.reshape(N_DEV * n_seq, D_MODEL)
    w_in_t = w_in_g.reshape(D_IN, D_MODEL)
    w_dw_f = _cols_to_full(w_dw_g)

    b_cols = lax.dynamic_slice(b_ada, (0, me * ada_cols), (1, ada_cols))
    mod_cols = _ada_fwd(c_all, w_ada[0], b_cols)
    (mod_g,) = _exchange([(mod_cols, False)], "gather_mod")
    mod_mine = lax.dynamic_slice(mod_g, (0, me * n_seq, 0), (N_DEV, n_seq, ada_cols))
    mod = jnp.transpose(mod_mine, (1, 0, 2)).reshape(n_seq, N_MOD, D_MODEL)
    mod8 = jnp.pad(mod, ((0, 0), (0, 8 - N_MOD), (0, 0)))

    x2 = x.reshape(tokens, D_MODEL)
    h1, proj = _mix_in(x2, mod8, g_mix, w_in_t, seq)
    proj3 = proj.reshape(D_IN // LANES, n_seq, seq, LANES)
    uc3 = _conv_fwd(proj3, w_dw_f, b_dw)
    g_q2, g_k2 = jnp.tile(g_q, (1, 2)), jnp.tile(g_k, (1, 2))
    y_att3, lse3, w_out_g, w_gate_g, w_up_g, w_down_g = _attn_fwd(
        proj3, g_q2, g_k2,
        [(w_out[0].astype(BF16), False), (transposed(w_gate).astype(BF16), False),
         (transposed(w_up).astype(BF16), False), (w_down[0].astype(BF16), False)])
    w_out_f = w_out_g.reshape(D_MODEL, D_MODEL)
    w_gate_f = w_gate_g.reshape(D_FF, D_MODEL)
    w_up_f = w_up_g.reshape(D_FF, D_MODEL)
    w_down_f = w_down_g.reshape(D_FF, D_MODEL)
    uc2 = uc3.reshape(tokens, D_CONV)
    y_att2 = y_att3.reshape(tokens, D_ATT)
    y_conv, mix, x1, h2 = _mix_out(uc2, y_att2, x2, mod8, g_conv_ln, b_conv_ln, g_ffn, w_out_f, seq)
    gate, up, dy, df, sq, dgate_f = _ffn_fwd(
        h2, w_gate_f, w_up_f, w_down_f, x1, loss_target.reshape(tokens, D_MODEL), mod8, seq)

    dgate, dup, act, dx1, dmix, dg_ffn, dmod_f = _ffn_bwd(
        df, gate, up, w_gate_f, w_up_f, w_down_f, x1, dy, mix, mod8, g_ffn, seq)
    duc2, do2, dgb_ln = _mix_out_bwd(dmix, uc2, g_conv_ln, b_conv_ln, w_out_f)
    d_a3, d_g3, dw_dw_p, db_dw_p = _conv_bwd(duc2.reshape(n_seq, seq, D_CONV), proj3, w_dw_f)
    gw_gate = _grad_matmul_tiles(dgate, h2, "grad_w_gate")
    gw_up = _grad_matmul_tiles(dup, h2, "grad_w_up")
    gw_down = _grad_matmul_tiles(act, df, "grad_w_down")
    gw_out = _grad_matmul_parts([y_conv, y_att2], [dmix], "grad_w_out")
    d_q3, d_k3, d_v3, dg_qk, p_gate, p_up, p_down, p_out = _attn_bwd(
        proj3, do2.reshape(n_seq, seq, D_ATT), y_att3, lse3, g_q2, g_k2,
        [(gw_gate.reshape(N_DEV, D_FF // N_DEV, D_MODEL), True), (gw_up.reshape(N_DEV, D_FF // N_DEV, D_MODEL), True),
         (gw_down.reshape(N_DEV, D_FF // N_DEV, D_MODEL), True),
         (gw_out.reshape(N_DEV, D_MODEL // N_DEV, D_MODEL), True)])
    flat = lambda t: t.reshape(tokens, t.shape[-1])
    d_a, d_g, d_q, d_k, d_v = flat(d_a3), flat(d_g3), flat(d_q3), flat(d_k3), flat(d_v3)
    gw_in = _grad_matmul_parts([d_a, d_g, d_q, d_k, d_v], [h1], "grad_w_in")
    grad_x2, dg_mix, dmod_m, p_in = _mix_in_bwd(
        d_a, d_g, d_q, d_k, d_v, w_in_t, x2, dx1, mod8, g_mix, seq,
        [(gw_in.reshape(N_DEV, D_IN // N_DEV, D_MODEL), True)])

    dmod = jnp.concatenate([dmod_m[:, 0], dmod_m[:, 1], dmod_f[:, 2], dmod_f[:, 0], dmod_f[:, 1], dgate_f[:, 0]], axis=1)
    dg_q = dg_qk[0:1, 0:HEAD_DIM] + dg_qk[0:1, HEAD_DIM:]
    dg_k = dg_qk[1:2, 0:HEAD_DIM] + dg_qk[1:2, HEAD_DIM:]
    loss_part = (0.5 / D_MODEL) * jnp.sum(sq[0:1, :], axis=1, keepdims=True)
    small = jnp.concatenate(
        [dg_mix[0:1], dg_ffn[0:1], db_dw_p[0:1], dgb_ln[0:1], dgb_ln[1:2],
         _pad_lanes(dg_q, LANES), _pad_lanes(dg_k, LANES), _pad_lanes(loss_part, LANES)], axis=1)
    n_small = small.shape[1] - LANES

    (dmod_g, small_g, dw_g) = _exchange([(dmod, False), (small, False), (dw_dw_p, False)], "gather_small_grads")

    dmod_all = dmod_g.reshape(N_DEV * n_seq, N_MOD * D_MODEL)
    dmod_cols = lax.dynamic_slice(dmod_all, (0, me * ada_cols), (N_DEV * n_seq, ada_cols))
    gw_ada, gb_ada = _ada_bwd(c_all, dmod_cols, dmod_all)

    res = {}
    res["w_ada"] = _adamw(w_ada[0], m_w_ada[0], v_w_ada[0], gw_ada, "adamw_w_ada")
    res["b_ada"] = _adamw(b_ada, m_b_ada, v_b_ada, gb_ada, "adamw_b_ada")
    def adamw_transposed(w, m, v, parts, name, tr):
        outs = _adamw(transposed(w), transposed(m), transposed(v), parts, name, N_DEV, tr=tr)
        return tuple(jnp.transpose(o) for o in outs)

    res["w_in"] = adamw_transposed(w_in, m_w_in, v_w_in, p_in, "adamw_w_in", 160)
    res["w_out"] = _adamw(w_out[0], m_w_out[0], v_w_out[0], p_out, "adamw_w_out", N_DEV)
    res["w_gate"] = adamw_transposed(w_gate, m_w_gate, v_w_gate, p_gate, "adamw_w_gate", 176)
    res["w_up"] = adamw_transposed(w_up, m_w_up, v_w_up, p_up, "adamw_w_up", 176)
    res["w_down"] = _adamw(w_down[0], m_w_down[0], v_w_down[0], p_down, "adamw_w_down", N_DEV, tr=176)
    dw_mine = lax.dynamic_slice(dw_g, (0, 0, me * dw_cols), (N_DEV, CONV_WIDTH, dw_cols))
    res["w_dw"] = _adamw(w_dw[0], m_w_dw[0], v_w_dw[0], dw_mine, "adamw_w_dw", N_DEV)

    small_names = ["g_mix", "g_ffn", "b_dw", "g_conv_ln", "b_conv_ln", "g_q", "g_k"]
    small_w = {"g_mix": (g_mix, m_g_mix, v_g_mix), "g_ffn": (g_ffn, m_g_ffn, v_g_ffn), "b_dw": (b_dw, m_b_dw, v_b_dw),
               "g_conv_ln": (g_conv_ln, m_g_conv_ln, v_g_conv_ln), "b_conv_ln": (b_conv_ln, m_b_conv_ln, v_b_conv_ln),
               "g_q": (g_q, m_g_q, v_g_q), "g_k": (g_k, m_g_k, v_g_k)}
    widths = [max(small_w[n][0].shape[1], LANES) for n in small_names]
    packed = [jnp.concatenate([_pad_lanes(small_w[n][i], wd) for n, wd in zip(small_names, widths)], axis=1) for i in range(3)]
    outs = _adamw(packed[0], packed[1], packed[2], small_g[:, :, :n_small], "adamw_small", N_DEV)
    off = 0
    for n, wd in zip(small_names, widths):
        real = small_w[n][0].shape[1]
        res[n] = tuple(o[:, off:off + real] for o in outs)
        off += wd
    loss = jnp.sum(small_g[:, 0, n_small])

    order = ["w_ada", "b_ada", "g_mix", "w_in", "w_dw", "b_dw", "g_conv_ln", "b_conv_ln", "g_q", "g_k",
             "w_out", "g_ffn", "w_gate", "w_up", "w_down"]
    lead = {"w_ada", "w_in", "w_dw", "w_out", "w_gate", "w_up", "w_down"}
    grads, deltas, new_m, new_v = [], [], [], []
    for n in order:
        g, d, mn, vn = res[n]
        g, d, mn, vn = (t[None] if n in lead else t for t in (g, d, mn, vn))
        grads.append(g)
        deltas.append(d)
        new_m.append(mn)
        new_v.append(vn)
    return (loss, grad_x2.reshape(n_seq, seq, D_MODEL), *grads, *deltas, *new_m, *new_v)
```

```python
import numpy as np
import jax
import jax.numpy as jnp
from jax import lax
from jax.experimental import pallas as pl
from jax.experimental.pallas import tpu as pltpu

F32 = jnp.float32
BF16 = jnp.bfloat16

N_DEV = 8
D_MODEL = 1024
D_CONV = 512
D_ATT = 512
HEAD_DIM = 64
CONV_WIDTH = 31
D_IN = 2 * D_CONV + 3 * D_ATT
D_FF = 2816
N_MOD = 6
EPS = 1e-6
RADIUS = 64
DILATIONS = (1, 4, 16)
Q_BLOCK = 128
LANES = 128
VMEM_LIMIT = 56 * 1024 * 1024

ADAM_LR = 0.001
ADAM_B1 = 0.9
ADAM_B2 = 0.999
ADAM_EPS = 1e-08
ADAM_WD = 0.01
ADAM_STEP = 10

NT = (((1,), (1,)), ((), ()))
TN = (((0,), (0,)), ((), ()))


def _call(body, **kw):
    return pl.pallas_call(body, **kw)


def _params(sem=None, vmem=VMEM_LIMIT):
    return pltpu.CompilerParams(dimension_semantics=sem, vmem_limit_bytes=vmem)


def _sig(x):
    return 1.0 / (1.0 + jnp.exp(-x))


def _sds(shape, dtype):
    return jax.ShapeDtypeStruct(shape, dtype)


N_PEER = N_DEV - 1
ANY_SPEC = pl.BlockSpec(memory_space=pl.ANY)


def _exchange_copies(scatter, ins, outs, *sems):
    n = len(ins)
    if n == 0:
        return [], []
    send_sems, recv_sems, local_sems = sems
    x, y, c = lax.axis_index("x"), lax.axis_index("y"), lax.axis_index("c")
    me = 4 * x + 2 * y + c

    def src(a, slot):
        return ins[a].at[slot] if scatter[a] else ins[a]

    local = [pltpu.make_async_copy(src(a, me), outs[a].at[me], local_sems.at[a]) for a in range(n)]
    flights = []
    for k in range(1, N_DEV):
        px = 1 - x if k & 4 else x
        py = 1 - y if k & 2 else y
        pc = 1 - c if k & 1 else c
        pid = 4 * px + 2 * py + pc
        for a in range(n):
            i = a * N_PEER + k - 1
            send, recv = (pltpu.make_async_remote_copy(
                src_ref=src(a, pid), dst_ref=outs[a].at[slot],
                send_sem=send_sems.at[i], recv_sem=recv_sems.at[i],
                device_id=(px, py, pc), device_id_type=pl.DeviceIdType.MESH) for slot in (me, pid))
            flights.append((send, recv))
    return local, flights


def _exchange_start(*args):
    local, flights = _exchange_copies(*args)
    for cp in local:
        cp.start()
    for send, _ in flights:
        send.start()


def _exchange_wait(*args):
    local, flights = _exchange_copies(*args)
    for send, recv in flights:
        send.wait_send()
        recv.wait_recv()
    for cp in local:
        cp.wait()


def _exchange_shapes(items):
    return [_sds((N_DEV,) + tuple(arr.shape[1:] if scatter else arr.shape), arr.dtype) for arr, scatter in items]


def _exchange_sems(n):
    if n == 0:
        return []
    return [pltpu.SemaphoreType.DMA((n * N_PEER,)), pltpu.SemaphoreType.DMA((n * N_PEER,)),
            pltpu.SemaphoreType.DMA((n,))]


def _gather_by_chip_phase(phase, ins, outs, send_sems, recv_sems, local_sems):
    n = len(ins)
    per = N_PEER
    x, y, c = lax.axis_index("x"), lax.axis_index("y"), lax.axis_index("c")
    me, sibling = (x, y, c), (x, y, 1 - c)
    chips = [(1 - x, y), (x, 1 - y), (1 - x, 1 - y)]

    def slot(px, py, pc):
        return 4 * px + 2 * py + pc

    def copy(a, k, block, to, src=None):
        dst = outs[a].at[slot(*block)]
        return pltpu.make_async_remote_copy(
            src_ref=dst if src is None else src, dst_ref=dst,
            send_sem=send_sems.at[a * per + k], recv_sem=recv_sems.at[a * per + k],
            device_id=to, device_id_type=pl.DeviceIdType.MESH)

    local = [pltpu.make_async_copy(ins[a], outs[a].at[slot(*me)], local_sems.at[a]) for a in range(n)]
    first = []
    for a in range(n):
        first.append(copy(a, 0, me, sibling, src=ins[a]))
        first += [copy(a, 1 + j, me, (*chip, c), src=ins[a]) for j, chip in enumerate(chips)]
    passed = [copy(a, 4 + j, (*chip, c), sibling) for j, chip in enumerate(chips) for a in range(n)]
    if phase == 0:
        for cp in local + first:
            cp.start()
    elif phase == 1:
        for j, chip in enumerate(chips):
            for a in range(n):
                copy(a, 1 + j, (*chip, c), me).wait_recv()
        for cp in passed:
            cp.start()
    else:
        for a in range(n):
            copy(a, 0, sibling, me).wait_recv()
            for j, chip in enumerate(chips):
                copy(a, 4 + j, (*chip, 1 - c), me).wait_recv()
        for cp in first + passed:
            cp.wait_send()
        for cp in local:
            cp.wait()


def _gather_by_chip(arrays, name):
    n = len(arrays)

    def body(*refs):
        for phase in range(3):
            _gather_by_chip_phase(phase, refs[:n], refs[n:2 * n], *refs[2 * n:])

    return _call(
        body, name=name, out_shape=_exchange_shapes([(arr, False) for arr in arrays]),
        in_specs=[ANY_SPEC] * n, out_specs=[ANY_SPEC] * n, scratch_shapes=_exchange_sems(n),
    )(*arrays)


def _exchange(items, name):
    n = len(items)
    scatter = [s for _, s in items]

    def body(*refs):
        args = (scatter, refs[:n], refs[n:2 * n]) + tuple(refs[2 * n:])
        _exchange_start(*args)
        _exchange_wait(*args)

    return _call(
        body, name=name, out_shape=_exchange_shapes(items),
        in_specs=[ANY_SPEC] * n, out_specs=[ANY_SPEC] * n, scratch_shapes=_exchange_sems(n),
    )(*[a for a, _ in items])


def _ada_fwd(c_all, w_ada, b_cols):
    def body(c_ref, w_ref, b_ref, o_ref):
        cv = c_ref[...]
        sc = (cv * _sig(cv)).astype(BF16)
        o_ref[...] = jnp.dot(sc, w_ref[...].astype(BF16), preferred_element_type=F32) + b_ref[...]

    return _call(body, name="ada_fwd", out_shape=_sds((c_all.shape[0], w_ada.shape[1]), F32),
                 compiler_params=_params())(c_all, w_ada, b_cols)


def _ada_bwd(c_all, dmod_cols, dmod_all):
    def body(c_ref, dc_ref, da_ref, gw_ref, gb_ref):
        cv = c_ref[...]
        sc = (cv * _sig(cv)).astype(BF16)
        gw_ref[...] = lax.dot_general(sc, dc_ref[...].astype(BF16), TN, preferred_element_type=F32)
        gb_ref[...] = jnp.sum(da_ref[...], axis=0, keepdims=True)

    return _call(body, name="ada_bwd",
                 out_shape=[_sds((c_all.shape[1], dmod_cols.shape[1]), F32), _sds((1, dmod_all.shape[1]), F32)],
                 compiler_params=_params())(c_all, dmod_cols, dmod_all)


MIX_ROWS = 128


def _mix_in(x2, mod8, g_mix, w_in, seq, tm=512):
    tokens = x2.shape[0]
    per_seq = seq // tm

    def body(x_ref, m_ref, g_ref, wt_ref, h_ref, p_ref, w_ref):
        @pl.when(pl.program_id(0) == 0)
        def _():
            w_ref[...] = wt_ref[...].T

        def normed(c):
            rows = pl.ds(c * MIX_ROWS, MIX_ROWS)
            xv = x_ref[rows, :]
            r = lax.rsqrt(jnp.mean(xv * xv, axis=-1, keepdims=True) + EPS)
            hb = ((xv * r * g_ref[...]) * (1.0 + m_ref[1:2, :]) + m_ref[0:1, :]).astype(BF16)
            h_ref[rows, :] = hb
            return hb

        ahead = normed(0)
        for c in range(tm // MIX_ROWS):
            hb = ahead
            if c + 1 < tm // MIX_ROWS:
                ahead = normed(c + 1)
            p = jnp.dot(hb, w_ref[...], preferred_element_type=F32)
            for cb in range(D_IN // LANES):
                p_ref[cb, pl.ds(c * MIX_ROWS, MIX_ROWS), :] = p[:, cb * LANES:(cb + 1) * LANES]

    return _call(
        body, name="mix_in", grid=(tokens // tm,),
        in_specs=[pl.BlockSpec((tm, D_MODEL), lambda i: (i, 0)),
                  pl.BlockSpec((None, 8, D_MODEL), lambda i: (i // per_seq, 0, 0)),
                  pl.BlockSpec((1, D_MODEL), lambda i: (0, 0)),
                  pl.BlockSpec((D_IN, D_MODEL), lambda i: (0, 0))],
        out_specs=[pl.BlockSpec((tm, D_MODEL), lambda i: (i, 0)),
                   pl.BlockSpec((D_IN // LANES, tm, LANES), lambda i: (0, i, 0))],
        out_shape=[_sds((tokens, D_MODEL), BF16), _sds((D_IN // LANES, tokens, LANES), F32)],
        scratch_shapes=[pltpu.VMEM((D_MODEL, D_IN), BF16)],
        compiler_params=_params(("arbitrary",)),
    )(x2, mod8, g_mix, w_in)


CONV_ROWS = 64
CONV_DW_ROWS = 32
CONV_DW_UNROLL = 4
CONV_HALO = 16


def _fill_shifted(xp, sh, seq):
    for b in range(8):
        sh[b, pl.ds(0, seq + 24), :] = xp[pl.ds(b, seq + 24), :]


def _conv_fwd(proj3, w_dw, b_dw):
    _, n_seq, seq, _ = proj3.shape
    n_cb = D_CONV // LANES

    def body(a_ref, g_ref, w_ref, b_ref, uc_ref, xp, sh):
        zeros = jnp.zeros((CONV_HALO, LANES), F32)
        xp[pl.ds(0, CONV_HALO), :] = zeros
        xp[pl.ds(CONV_HALO + seq, CONV_HALO), :] = zeros
        xp[pl.ds(CONV_HALO, seq), :] = a_ref[...] * _sig(g_ref[...])
        _fill_shifted(xp, sh, seq)

        def blk(i, carry):
            t0 = pl.multiple_of(i * CONV_ROWS, CONV_ROWS)
            acc = jnp.zeros((CONV_ROWS, LANES), F32)
            for j in range(CONV_WIDTH):
                jj = j + 1
                acc = acc + sh[jj % 8, pl.ds(t0 + 8 * (jj // 8), CONV_ROWS), :] * w_ref[j:j + 1, :]
            uc_ref[pl.ds(t0, CONV_ROWS), :] = acc + b_ref[...]
            return carry

        lax.fori_loop(0, seq // CONV_ROWS, blk, 0)

    return _call(
        body, name="conv_fwd", grid=(n_seq, n_cb),
        in_specs=[pl.BlockSpec((None, None, seq, LANES), lambda b, cb: (cb, b, 0, 0)),
                  pl.BlockSpec((None, None, seq, LANES), lambda b, cb: (n_cb + cb, b, 0, 0)),
                  pl.BlockSpec((CONV_WIDTH, LANES), lambda b, cb: (0, cb)),
                  pl.BlockSpec((1, LANES), lambda b, cb: (0, cb))],
        out_specs=pl.BlockSpec((None, seq, LANES), lambda b, cb: (b, 0, cb)),
        out_shape=_sds((n_seq, seq, D_CONV), F32),
        scratch_shapes=[pltpu.VMEM((seq + 2 * CONV_HALO, LANES), F32),
                        pltpu.VMEM((8, seq + 2 * CONV_HALO, LANES), F32)],
        compiler_params=_params(("parallel", "parallel")),
    )(proj3, proj3, w_dw, b_dw)


def _conv_bwd(duc3, proj3, w_dw):
    _, n_seq, seq, _ = proj3.shape
    n_cb = D_CONV // LANES

    def body(duc_ref, a_ref, g_ref, w_ref, da_ref, dg_ref, dw_ref, db_ref, xp, sh):
        @pl.when(pl.program_id(1) == 0)
        def _():
            dw_ref[...] = jnp.zeros_like(dw_ref)
            db_ref[...] = jnp.zeros_like(db_ref)

        zeros = jnp.zeros((CONV_HALO, LANES), F32)
        xp[pl.ds(0, CONV_HALO), :] = zeros
        xp[pl.ds(CONV_HALO + seq, CONV_HALO), :] = zeros
        xp[pl.ds(CONV_HALO, seq), :] = a_ref[...] * _sig(g_ref[...])
        _fill_shifted(xp, sh, seq)
        for j0 in range(0, CONV_WIDTH, 8):
            taps = range(j0, min(j0 + 8, CONV_WIDTH))

            def wblk(i, accs, taps=taps):
                for u in range(CONV_DW_UNROLL):
                    t0 = pl.multiple_of((i * CONV_DW_UNROLL + u) * CONV_DW_ROWS, CONV_DW_ROWS)
                    d = duc_ref[pl.ds(t0, CONV_DW_ROWS), :]
                    accs = tuple(acc + d * sh[(j + 1) % 8, pl.ds(t0 + 8 * ((j + 1) // 8), CONV_DW_ROWS), :]
                                 for acc, j in zip(accs, taps))
                return accs

            accs = lax.fori_loop(0, seq // (CONV_DW_ROWS * CONV_DW_UNROLL), wblk,
                                 tuple(jnp.zeros((CONV_DW_ROWS, LANES), F32) for _ in taps))
            for acc, j in zip(accs, taps):
                dw_ref[j:j + 1, :] += jnp.sum(acc, axis=0, keepdims=True)
        db_ref[0:1, :] += jnp.sum(duc_ref[...], axis=0, keepdims=True)
        xp[pl.ds(CONV_HALO, seq), :] = duc_ref[...]
        _fill_shifted(xp, sh, seq)

        def ublk(i, carry):
            t0 = pl.multiple_of(i * CONV_ROWS, CONV_ROWS)
            acc = jnp.zeros((CONV_ROWS, LANES), F32)
            for j in range(CONV_WIDTH):
                jj = CONV_WIDTH - j
                acc = acc + sh[jj % 8, pl.ds(t0 + 8 * (jj // 8), CONV_ROWS), :] * w_ref[j:j + 1, :]
            av = a_ref[pl.ds(t0, CONV_ROWS), :]
            sg = _sig(g_ref[pl.ds(t0, CONV_ROWS), :])
            da_ref[pl.ds(t0, CONV_ROWS), :] = (acc * sg).astype(BF16)
            dg_ref[pl.ds(t0, CONV_ROWS), :] = (acc * av * sg * (1.0 - sg)).astype(BF16)
            return carry

        lax.fori_loop(0, seq // CONV_ROWS, ublk, 0)

    return _call(
        body, name="conv_bwd", grid=(n_cb, n_seq),
        in_specs=[pl.BlockSpec((None, seq, LANES), lambda cb, b: (b, 0, cb)),
                  pl.BlockSpec((None, None, seq, LANES), lambda cb, b: (cb, b, 0, 0)),
                  pl.BlockSpec((None, None, seq, LANES), lambda cb, b: (n_cb + cb, b, 0, 0)),
                  pl.BlockSpec((CONV_WIDTH, LANES), lambda cb, b: (0, cb))],
        out_specs=[pl.BlockSpec((None, seq, LANES), lambda cb, b: (b, 0, cb)),
                   pl.BlockSpec((None, seq, LANES), lambda cb, b: (b, 0, cb)),
                   pl.BlockSpec((32, LANES), lambda cb, b: (0, cb)),
                   pl.BlockSpec((8, LANES), lambda cb, b: (0, cb))],
        out_shape=[_sds((n_seq, seq, D_CONV), BF16), _sds((n_seq, seq, D_CONV), BF16),
                   _sds((32, D_CONV), F32), _sds((8, D_CONV), F32)],
        scratch_shapes=[pltpu.VMEM((seq + 2 * CONV_HALO, LANES), F32),
                        pltpu.VMEM((8, seq + 2 * CONV_HALO, LANES), F32)],
        compiler_params=_params(("parallel", "arbitrary")),
    )(duc3, proj3, proj3, w_dw)


MASKED = 1e30
ATT_ROWS = 512
ATT_UNROLL = 8
ATT_FWD_UNROLL = 8


def _distance_mats(dil, seg_len):
    kw = min(2 * Q_BLOCK, seg_len)
    offsets = (0, -RADIUS, -2 * RADIUS) if kw == 2 * Q_BLOCK else (0,)
    a = np.arange(Q_BLOCK)[:, None]
    b = np.arange(kw)[None, :]
    mats = []
    for off in offsets:
        rel = np.abs(b + off - a)
        mats.append(np.where(rel <= RADIUS, dil * rel, MASKED))
    return jnp.asarray(np.stack(mats).astype(np.float32))


def _alibi_rows():
    s = np.zeros((4, 8, LANES), np.float32)
    for hp in range(4):
        for hl in range(2):
            s[hp, hl, :] = 2.0 ** (-(2 * hp + hl + 1))
    return jnp.asarray(s)


def _window(n, seg_len):
    i0 = pl.multiple_of(n * Q_BLOCK, Q_BLOCK)
    if seg_len <= Q_BLOCK:
        return i0, i0, 0
    per_seg = seg_len // Q_BLOCK
    j = n % per_seg
    seg0 = (n // per_seg) * seg_len
    ks_local = jnp.clip(j * Q_BLOCK - RADIUS, 0, seg_len - 2 * Q_BLOCK)
    ks = pl.multiple_of(seg0 + ks_local, RADIUS)
    var = jnp.where(j == 0, 0, jnp.where(j == per_seg - 1, 2, 1))
    return i0, ks, var


def _first_head(rows):
    return lax.broadcasted_iota(jnp.int32, (rows, LANES), 1) < HEAD_DIM


def _same_head():
    head = np.arange(LANES) // HEAD_DIM
    return jnp.asarray((head[:, None] == head[None, :]).astype(np.float32)).astype(BF16)


def _head_sum(x, same_ref):
    hi = x.astype(BF16)
    lo = (x - hi.astype(F32)).astype(BF16)
    return (jnp.dot(hi, same_ref[...], preferred_element_type=F32)
            + jnp.dot(lo, same_ref[...], preferred_element_type=F32))


def _head_mean(x, same_ref):
    return _head_sum(x, same_ref) * (1.0 / HEAD_DIM)


def _per_head(x, first):
    swapped = pltpu.roll(x, HEAD_DIM, 1)
    return jnp.where(first, x, swapped), jnp.where(first, swapped, x)


STRIDE = 4


def _gather_segments(src, dil, seq, tmp, put):
    if dil == 1:
        put(0, seq, src[pl.ds(0, seq), :])
    elif dil == STRIDE:
        seg = seq // dil
        for r in range(dil):
            put(r * seg, seg, src[pl.ds(r, seg, stride=dil), :])
    else:
        part, seg = seq // STRIDE, seq // dil
        for b in range(STRIDE):
            tmp[pl.ds(b * part, part), :] = src[pl.ds(b, part, stride=STRIDE), :]
        for b in range(STRIDE):
            for a in range(dil // STRIDE):
                put(b * part + a * seg, seg, tmp[pl.ds(b * part + a, seg, stride=dil // STRIDE), :])


def _scatter_segments(dst, get, dil, seq, tmp, accumulate):
    def write(rows, val):
        if accumulate:
            dst[rows, :] += val
        else:
            dst[rows, :] = val

    if dil == 1:
        write(pl.ds(0, seq), get(0, seq))
    elif dil == STRIDE:
        seg = seq // dil
        for r in range(dil):
            write(pl.ds(r, seg, stride=dil), get(r * seg, seg))
    else:
        part, seg = seq // STRIDE, seq // dil
        for b in range(STRIDE):
            for a in range(dil // STRIDE):
                tmp[pl.ds(b * part + a, seg, stride=dil // STRIDE), :] = get(b * part + a * seg, seg)
        for b in range(STRIDE):
            write(pl.ds(b, part, stride=STRIDE), tmp[pl.ds(b * part, part), :])


def _permute_rows(dst, src, dil, seq, tmp):
    def put(start, size, val):
        dst[pl.ds(start, size), :] = val.astype(dst.dtype)

    _gather_segments(src, dil, seq, tmp, put)


def _permute_rows_by_head(dst, src, dil, seq, tmp):
    def put(start, size, val):
        first = _first_head(size)
        dst[0, pl.ds(start, size), :] = jnp.where(first, val, 0.0).astype(dst.dtype)
        dst[1, pl.ds(start, size), :] = jnp.where(first, 0.0, val).astype(dst.dtype)

    _gather_segments(src, dil, seq, tmp, put)


def _qk_normalise(q_ref, g2_ref, same_ref, dst, seq, scale):
    def chunk(ci, carry):
        rows = pl.ds(pl.multiple_of(ci * ATT_ROWS, ATT_ROWS), ATT_ROWS)
        qv = q_ref[rows, :]
        r = lax.rsqrt(_head_mean(qv * qv, same_ref) + EPS)
        dst[rows, :] = qv * r * (g2_ref[...] * scale)
        return carry

    lax.fori_loop(0, seq // ATT_ROWS, chunk, 0)


def _attn_fwd(proj3, g_q2, g_k2, ride):
    _, n_seq, seq, _ = proj3.shape
    dms = [_distance_mats(d, seq // d) for d in DILATIONS]
    same = _same_head()
    col0 = 2 * D_CONV // LANES
    n_hp = D_ATT // LANES

    n_ride = len(ride)
    assert not any(scatter for _, scatter in ride), "the forward's ride is an all-gather"

    def body(*refs):
        q_ref, k_ref, v_ref, gq_ref, gk_ref, sl_ref, dm1, dm4, dm16, same_ref = refs[:10]
        ride_in = refs[10:10 + n_ride]
        y_ref, lse_ref = refs[10 + n_ride:12 + n_ride]
        ride_out = refs[12 + n_ride:12 + 2 * n_ride]
        (qf, kf, qp, kp, vp, oml_p, o1, o4, o16, m1, m4, m16, l1, l4, l16,
         tmp) = refs[12 + 2 * n_ride:28 + 2 * n_ride]
        o_nat, m_nat, l_nat = (o1, o4, o16), (m1, m4, m16), (l1, l4, l16)
        ride_args = (ride_in, ride_out) + tuple(refs[28 + 2 * n_ride:])
        step = pl.program_id(0) * n_hp + pl.program_id(1)
        n_steps = n_seq * n_hp

        if n_ride:
            for phase, at in enumerate((0, (3 * n_steps) // 4)):
                @pl.when(step == at)
                def _(phase=phase):
                    _gather_by_chip_phase(phase, *ride_args)

        dm_refs = (dm1, dm4, dm16)
        _qk_normalise(q_ref, gq_ref, same_ref, qf, seq, HEAD_DIM ** -0.5)
        _qk_normalise(k_ref, gk_ref, same_ref, kf, seq, 1.0)
        slopes = (sl_ref[0:1, 0:1], sl_ref[1:2, 0:1])
        for pi, dil in enumerate(DILATIONS):
            seg = seq // dil
            kw = min(2 * Q_BLOCK, seg)
            _permute_rows_by_head(qp, qf, dil, seq, tmp)
            _permute_rows(kp, kf, dil, seq, tmp)
            _permute_rows(vp, v_ref, dil, seq, tmp)

            def blk(it, carry, seg=seg, kw=kw, pi=pi, dst=oml_p):
                first = _first_head(Q_BLOCK)
                chains = [(sub, h) for sub in range(ATT_FWD_UNROLL) for h in range(2)]
                win = [_window(it * ATT_FWD_UNROLL + sub, seg) for sub in range(ATT_FWD_UNROLL)]
                s = {}
                for sub, h in chains:
                    i0, ks, var = win[sub]
                    s[sub, h] = lax.dot_general(qp[h, pl.ds(i0, Q_BLOCK), :], kp[pl.ds(ks, kw), :], NT,
                                                preferred_element_type=F32) - slopes[h] * dm_refs[pi][var]
                m, l, p = {}, {}, {}
                for c in chains:
                    m[c] = jnp.max(s[c], axis=1, keepdims=True)
                    e = jnp.exp(s[c] - m[c])
                    l[c] = jnp.sum(e, axis=1, keepdims=True)
                    p[c] = e.astype(BF16)
                o = {}
                for sub, h in chains:
                    o[sub, h] = jnp.dot(p[sub, h], vp[pl.ds(win[sub][1], kw), :], preferred_element_type=F32)
                packed = [jnp.concatenate([jnp.where(first, t[sub, 0], t[sub, 1]) for t in (o, m, l)], axis=1)
                          for sub in range(ATT_FWD_UNROLL)]
                span = ATT_FWD_UNROLL * Q_BLOCK
                dst[pl.ds(pl.multiple_of(it * span, span), span), :] = jnp.concatenate(packed, axis=0)
                return carry

            lax.fori_loop(0, seq // (Q_BLOCK * ATT_FWD_UNROLL), blk, 0)
            for n, nat in enumerate((o_nat[pi], m_nat[pi], l_nat[pi])):
                _scatter_segments(nat, lambda start, size, n=n: oml_p[pl.ds(start, size), pl.ds(n * LANES, LANES)],
                                  dil, seq, tmp, accumulate=False)

        def merge(ci, carry):
            rows = pl.ds(pl.multiple_of(ci * ATT_ROWS, ATT_ROWS), ATT_ROWS)
            ms = [m_nat[pi][rows, :] for pi in range(3)]
            m_all = jnp.maximum(jnp.maximum(ms[0], ms[1]), ms[2])
            es = [jnp.exp(m - m_all) for m in ms]
            l_all = sum(l_nat[pi][rows, :] * es[pi] for pi in range(3))
            inv = 1.0 / l_all
            o = sum(o_nat[pi][rows, :] * (es[pi] * inv) for pi in range(3))
            y_ref[rows, :] = o.astype(BF16)
            lse_ref[rows, :] = m_all + jnp.log(l_all)
            return carry

        lax.fori_loop(0, seq // ATT_ROWS, merge, 0)

        if n_ride:
            @pl.when(step == n_steps - 1)
            def _():
                _gather_by_chip_phase(2, *ride_args)

    def col(off):
        return pl.BlockSpec((None, None, seq, LANES), lambda b, hp: (col0 + off * n_hp + hp, b, 0, 0))

    def whole(arr):
        return pl.BlockSpec(arr.shape, lambda b, hp: (0,) * arr.ndim)

    rows_f32 = pltpu.VMEM((seq, LANES), F32)
    rows_bf16 = pltpu.VMEM((seq, LANES), BF16)
    return _call(
        body, name="attn_fwd", grid=(n_seq, n_hp),
        in_specs=[col(0), col(1), col(2), whole(g_q2), whole(g_k2),
                  pl.BlockSpec((None, 8, LANES), lambda b, hp: (hp, 0, 0)),
                  whole(dms[0]), whole(dms[1]), whole(dms[2]), whole(same)] + [ANY_SPEC] * n_ride,
        out_specs=[pl.BlockSpec((None, seq, LANES), lambda b, hp: (b, 0, hp)),
                   pl.BlockSpec((None, seq, LANES), lambda b, hp: (b, 0, hp))] + [ANY_SPEC] * n_ride,
        out_shape=[_sds((n_seq, seq, D_ATT), BF16), _sds((n_seq, seq, D_ATT), F32)] + _exchange_shapes(ride),
        scratch_shapes=[rows_f32, rows_f32, pltpu.VMEM((2, seq, LANES), BF16), rows_bf16, rows_bf16]
        + [pltpu.VMEM((seq, 3 * LANES), F32)] + [rows_f32] * 10 + _exchange_sems(n_ride),
        compiler_params=_params(("arbitrary", "arbitrary")),
    )(proj3, proj3, proj3, g_q2, g_k2, _alibi_rows(), *dms, same, *[a for a, _ in ride])


def _attn_bwd(proj3, do3, y_att3, lse3, g_q2, g_k2, ride):
    _, n_seq, seq, _ = proj3.shape
    dms = [_distance_mats(d, seq // d) for d in DILATIONS]
    same = _same_head()
    col0 = 2 * D_CONV // LANES
    n_hp = D_ATT // LANES

    n_ride = len(ride)
    ride_scatter = [s for _, s in ride]

    def body(*refs):
        (q_ref, k_ref, v_ref, do_ref, o_ref, lse_ref, gq_ref, gk_ref, sl_ref, dm1, dm4, dm16,
         same_ref) = refs[:13]
        ride_in = refs[13:13 + n_ride]
        dq_ref, dk_ref, dv_ref, dg_ref = refs[13 + n_ride:17 + n_ride]
        ride_out = refs[17 + n_ride:17 + 2 * n_ride]
        (qf, kf, qp, dop, kp, vp, sn, sp, dqp, dkp, dvp, dqn, dkn, dvn,
         tmp) = refs[17 + 2 * n_ride:32 + 2 * n_ride]
        ride_args = (ride_scatter, ride_in, ride_out) + tuple(refs[32 + 2 * n_ride:])
        dm_refs = (dm1, dm4, dm16)
        step = pl.program_id(0) * n_hp + pl.program_id(1)

        @pl.when(step == 0)
        def _():
            _exchange_start(*ride_args)
            dg_ref[...] = jnp.zeros_like(dg_ref)

        _qk_normalise(q_ref, gq_ref, same_ref, qf, seq, HEAD_DIM ** -0.5)
        _qk_normalise(k_ref, gk_ref, same_ref, kf, seq, 1.0)

        def stats(ci, carry):
            rows = pl.ds(pl.multiple_of(ci * ATT_ROWS, ATT_ROWS), ATT_ROWS)
            first = _first_head(ATT_ROWS)
            sn[0, rows, :], sn[1, rows, :] = _per_head(lse_ref[rows, :], first)
            prod = do_ref[rows, :] * o_ref[rows, :].astype(F32)
            sn[2, rows, :], sn[3, rows, :] = _per_head(_head_sum(prod, same_ref), first)
            return carry

        lax.fori_loop(0, seq // ATT_ROWS, stats, 0)
        slopes = (sl_ref[0:1, 0:1], sl_ref[1:2, 0:1])
        half = seq // (Q_BLOCK * ATT_UNROLL)
        region = seq // ATT_UNROLL

        for pi, dil in enumerate(DILATIONS):
            seg = seq // dil
            kw = min(2 * Q_BLOCK, seg)
            _permute_rows_by_head(qp, qf, dil, seq, tmp)
            _permute_rows_by_head(dop, do_ref, dil, seq, tmp)
            _permute_rows(kp, kf, dil, seq, tmp)
            _permute_rows(vp, v_ref, dil, seq, tmp)
            if dil == 1:
                st = sn
            else:
                st = sp
                for n in range(4):
                    _permute_rows(sp.at[n], sn.at[n], dil, seq, tmp)
            def touched(sub, seg=seg):
                lo, hi = sub * region, (sub + 1) * region
                if seg < region:
                    return lo, hi
                seg0 = lo // seg * seg
                return max(lo - RADIUS, seg0), min(hi + RADIUS, seg0 + seg)

            def summed(acc, start, size, touched=touched):
                pieces = []
                for c0 in range(start, start + size, RADIUS):
                    owners = [s for s in range(ATT_UNROLL) if touched(s)[0] <= c0 and c0 + RADIUS <= touched(s)[1]]
                    if pieces and pieces[-1][2] == owners:
                        pieces[-1][1] += RADIUS
                    else:
                        pieces.append([c0, RADIUS, owners])
                vals = [sum(acc[o, pl.ds(c0, n), :] for o in owners) for c0, n, owners in pieces]
                return vals[0] if len(vals) == 1 else jnp.concatenate(vals, axis=0)

            for sub in range(ATT_UNROLL):
                lo, hi = touched(sub)
                dkp[sub, pl.ds(lo, hi - lo), :] = jnp.zeros((hi - lo, LANES), F32)
                dvp[sub, pl.ds(lo, hi - lo), :] = jnp.zeros((hi - lo, LANES), F32)

            def blk(it, carry, seg=seg, kw=kw, pi=pi, st=st):
                first = _first_head(Q_BLOCK)
                chains = [(sub, h) for sub in range(ATT_UNROLL) for h in range(2)]
                win = [_window(it + sub * half, seg) for sub in range(ATT_UNROLL)]
                qrows = [pl.ds(w[0], Q_BLOCK) for w in win]
                krows = [pl.ds(w[1], kw) for w in win]

                def over_keys(n, sub):
                    t = st[n, qrows[sub], :]
                    return t if kw == LANES else jnp.concatenate([t] * (kw // LANES), axis=1)

                s, dp = {}, {}
                for sub, h in chains:
                    s[sub, h] = lax.dot_general(qp[h, qrows[sub], :], kp[krows[sub], :], NT,
                                                preferred_element_type=F32) - slopes[h] * dm_refs[pi][win[sub][2]]
                    dp[sub, h] = lax.dot_general(dop[h, qrows[sub], :], vp[krows[sub], :], NT,
                                                 preferred_element_type=F32)
                p, ds = {}, {}
                for sub, h in chains:
                    e = jnp.exp(s[sub, h] - over_keys(h, sub))
                    ds[sub, h] = (e * (dp[sub, h] - over_keys(2 + h, sub))).astype(BF16)
                    p[sub, h] = e.astype(BF16)
                dq, dk, dv = {}, {}, {}
                for sub, h in chains:
                    dq[sub, h] = jnp.dot(ds[sub, h], kp[krows[sub], :], preferred_element_type=F32)
                    dk[sub, h] = lax.dot_general(ds[sub, h], qp[h, qrows[sub], :], TN, preferred_element_type=F32)
                    dv[sub, h] = lax.dot_general(p[sub, h], dop[h, qrows[sub], :], TN, preferred_element_type=F32)
                for sub in range(ATT_UNROLL):
                    dqp[qrows[sub], :] = jnp.where(first, dq[sub, 0], dq[sub, 1])
                    dkp[sub, krows[sub], :] += dk[sub, 0] + dk[sub, 1]
                    dvp[sub, krows[sub], :] += dv[sub, 0] + dv[sub, 1]
                return carry

            lax.fori_loop(0, half, blk, 0)
            first_pattern = pi == 0
            if first_pattern:
                for r0 in range(0, seq, region):
                    rows = pl.ds(r0, region)
                    dqn[rows, :] = dqp[rows, :]
                    dkn[rows, :] = summed(dkp, r0, region)
                    dvn[rows, :] = summed(dvp, r0, region)
            else:
                _scatter_segments(dqn, lambda start, size: dqp[pl.ds(start, size), :], dil, seq, tmp, accumulate=True)
                for nat, acc in ((dkn, dkp), (dvn, dvp)):
                    _scatter_segments(nat, lambda start, size, acc=acc: summed(acc, start, size),
                                      dil, seq, tmp, accumulate=True)

        def finish(ci, carry):
            rows = pl.ds(pl.multiple_of(ci * ATT_ROWS, ATT_ROWS), ATT_ROWS)
            for src_ref, g_ref, dn, dst_ref, scale, row in (
                    (q_ref, gq_ref, dqn, dq_ref, HEAD_DIM ** -0.5, 0), (k_ref, gk_ref, dkn, dk_ref, 1.0, 1)):
                xv = src_ref[rows, :]
                r = lax.rsqrt(_head_mean(xv * xv, same_ref) + EPS)
                xhat = xv * r
                d = dn[rows, :] * scale
                dg_ref[row:row + 1, :] += jnp.sum(d * xhat, axis=0, keepdims=True)
                dxh = d * g_ref[...]
                dst_ref[rows, :] = (r * (dxh - xhat * _head_mean(dxh * xhat, same_ref))).astype(BF16)
            dv_ref[rows, :] = dvn[rows, :].astype(BF16)
            return carry

        lax.fori_loop(0, seq // ATT_ROWS, finish, 0)

        @pl.when(step == n_seq * n_hp - 1)
        def _():
            _exchange_wait(*ride_args)

    def col(off):
        return pl.BlockSpec((None, None, seq, LANES), lambda b, hp: (col0 + off * n_hp + hp, b, 0, 0))

    def whole(arr):
        return pl.BlockSpec(arr.shape, lambda b, hp: (0,) * arr.ndim)

    att = pl.BlockSpec((None, seq, LANES), lambda b, hp: (b, 0, hp))
    rows_f32 = pltpu.VMEM((seq, LANES), F32)
    rows_bf16 = pltpu.VMEM((seq, LANES), BF16)
    by_head_bf16 = pltpu.VMEM((2, seq, LANES), BF16)
    per_sub_f32 = pltpu.VMEM((ATT_UNROLL, seq, LANES), F32)
    stats_f32 = pltpu.VMEM((4, seq, LANES), F32)
    return _call(
        body, name="attn_bwd", grid=(n_seq, n_hp),
        in_specs=[col(0), col(1), col(2), att, att, att, whole(g_q2), whole(g_k2),
                  pl.BlockSpec((None, 8, LANES), lambda b, hp: (hp, 0, 0)),
                  whole(dms[0]), whole(dms[1]), whole(dms[2]), whole(same)] + [ANY_SPEC] * n_ride,
        out_specs=[att, att, att, pl.BlockSpec((8, LANES), lambda b, hp: (0, 0))] + [ANY_SPEC] * n_ride,
        out_shape=[_sds((n_seq, seq, D_ATT), BF16)] * 3 + [_sds((8, LANES), F32)] + _exchange_shapes(ride),
        scratch_shapes=[rows_f32, rows_f32, by_head_bf16, by_head_bf16, rows_bf16, rows_bf16, stats_f32, stats_f32,
                        rows_f32, per_sub_f32, per_sub_f32, rows_f32, rows_f32, rows_f32, rows_f32]
        + _exchange_sems(n_ride),
        compiler_params=_params(("arbitrary", "arbitrary")),
    )(proj3, proj3, proj3, do3, y_att3, lse3, g_q2, g_k2, _alibi_rows(), *dms, same, *[a for a, _ in ride])


def _mix_out(uc2, y_att2, x2, mod8, g_ln, b_ln, g_ffn, w_out, seq, tm=512):
    tokens = x2.shape[0]
    per_seq = seq // tm

    def body(uc_ref, ya_ref, x_ref, m_ref, gl_ref, bl_ref, gf_ref, w_ref, yc_ref, mix_ref, x1_ref, h2_ref):
        uc = uc_ref[...]
        mu = jnp.mean(uc, axis=-1, keepdims=True)
        cen = uc - mu
        rs = lax.rsqrt(jnp.mean(cen * cen, axis=-1, keepdims=True) + EPS)
        z = cen * rs * gl_ref[...] + bl_ref[...]
        yc = (z * _sig(z)).astype(BF16)
        yc_ref[...] = yc
        mix = (jnp.dot(yc, w_ref[pl.ds(0, D_CONV), :], preferred_element_type=F32)
               + jnp.dot(ya_ref[...], w_ref[pl.ds(D_CONV, D_ATT), :], preferred_element_type=F32))
        mix_ref[...] = mix
        x1 = x_ref[...] + m_ref[2:3, :] * mix
        x1_ref[...] = x1
        r = lax.rsqrt(jnp.mean(x1 * x1, axis=-1, keepdims=True) + EPS)
        h2_ref[...] = ((x1 * r * gf_ref[...]) * (1.0 + m_ref[4:5, :]) + m_ref[3:4, :]).astype(BF16)

    def rows(width):
        return pl.BlockSpec((tm, width), lambda i: (i, 0))

    def vec(width):
        return pl.BlockSpec((1, width), lambda i: (0, 0))

    return _call(
        body, name="mix_out", grid=(tokens // tm,),
        in_specs=[rows(D_CONV), rows(D_ATT), rows(D_MODEL),
                  pl.BlockSpec((None, 8, D_MODEL), lambda i: (i // per_seq, 0, 0)),
                  vec(D_CONV), vec(D_CONV), vec(D_MODEL),
                  pl.BlockSpec((D_MODEL, D_MODEL), lambda i: (0, 0))],
        out_specs=[rows(D_CONV), rows(D_MODEL), rows(D_MODEL), rows(D_MODEL)],
        out_shape=[_sds((tokens, D_CONV), BF16), _sds((tokens, D_MODEL), F32),
                   _sds((tokens, D_MODEL), F32), _sds((tokens, D_MODEL), BF16)],
        compiler_params=_params(("parallel",)),
    )(uc2, y_att2, x2, mod8, g_ln, b_ln, g_ffn, w_out)


def _mix_out_bwd(dmix, uc2, g_ln, b_ln, w_out, tm=512):
    tokens = dmix.shape[0]

    def body(dm_ref, uc_ref, gl_ref, bl_ref, w_ref, duc_ref, do_ref, dgb_ref):
        @pl.when(pl.program_id(0) == 0)
        def _():
            dgb_ref[...] = jnp.zeros_like(dgb_ref)

        dmv = dm_ref[...]
        dyc = lax.dot_general(dmv, w_ref[pl.ds(0, D_CONV), :], NT, preferred_element_type=F32)
        do_ref[...] = lax.dot_general(dmv, w_ref[pl.ds(D_CONV, D_ATT), :], NT, preferred_element_type=F32)
        uc = uc_ref[...]
        mu = jnp.mean(uc, axis=-1, keepdims=True)
        cen = uc - mu
        rs = lax.rsqrt(jnp.mean(cen * cen, axis=-1, keepdims=True) + EPS)
        xh = cen * rs
        z = xh * gl_ref[...] + bl_ref[...]
        sg = _sig(z)
        dz = dyc * (sg * (1.0 + z * (1.0 - sg)))
        dgb_ref[0:1, :] += jnp.sum(dz * xh, axis=0, keepdims=True)
        dgb_ref[1:2, :] += jnp.sum(dz, axis=0, keepdims=True)
        dxh = dz * gl_ref[...]
        duc_ref[...] = rs * (dxh - jnp.mean(dxh, axis=-1, keepdims=True)
                             - xh * jnp.mean(dxh * xh, axis=-1, keepdims=True))

    return _call(
        body, name="mix_out_bwd", grid=(tokens // tm,),
        in_specs=[pl.BlockSpec((tm, D_MODEL), lambda i: (i, 0)),
                  pl.BlockSpec((tm, D_CONV), lambda i: (i, 0)),
                  pl.BlockSpec((1, D_CONV), lambda i: (0, 0)),
                  pl.BlockSpec((1, D_CONV), lambda i: (0, 0)),
                  pl.BlockSpec((D_MODEL, D_MODEL), lambda i: (0, 0))],
        out_specs=[pl.BlockSpec((tm, D_CONV), lambda i: (i, 0)),
                   pl.BlockSpec((tm, D_ATT), lambda i: (i, 0)),
                   pl.BlockSpec((8, D_CONV), lambda i: (0, 0))],
        out_shape=[_sds((tokens, D_CONV), F32), _sds((tokens, D_ATT), F32), _sds((8, D_CONV), F32)],
        compiler_params=_params(("arbitrary",)),
    )(dmix, uc2, g_ln, b_ln, w_out)


FF_TILE = 256
FF_TILES = D_FF // FF_TILE
FF_ROWS = 256


def _ffn_fwd(h2, w_gate, w_up, w_down, x1, target, mod8, seq, tm=1024):
    tokens = h2.shape[0]
    per_seq = seq // tm
    n_seq = tokens // seq
    last = D_FF // FF_TILE - 1

    def body(h_ref, wg_ref, wu_ref, wd_ref, x1_ref, t_ref, m_ref, gate_ref, up_ref, dy_ref, df_ref, sq_ref, dgf_ref,
             f_ref):
        i, j = pl.program_id(0), pl.program_id(1)

        @pl.when(j == 0)
        def _():
            f_ref[...] = jnp.zeros_like(f_ref)

        @pl.when((j == 0) & (i == 0))
        def _():
            sq_ref[...] = jnp.zeros_like(sq_ref)

        @pl.when((j == 0) & (i % per_seq == 0))
        def _():
            dgf_ref[...] = jnp.zeros_like(dgf_ref)

        def gate_up(r):
            hv = h_ref[pl.ds(r * FF_ROWS, FF_ROWS), :]
            return (lax.dot_general(hv, wg_ref[...], NT, preferred_element_type=F32),
                    lax.dot_general(hv, wu_ref[...], NT, preferred_element_type=F32))

        def loss_head(rows, fv):
            gate_f = m_ref[5:6, :]
            diff = x1_ref[rows, :] + gate_f * fv - t_ref[rows, :]
            sq_ref[0:1, :] += jnp.sum(diff * diff, axis=0, keepdims=True)
            dy = diff * (1.0 / D_MODEL)
            dy_ref[rows, :] = dy
            df_ref[rows, :] = (gate_f * dy).astype(BF16)
            dgf_ref[0:1, :] += jnp.sum(dy * fv, axis=0, keepdims=True)

        def tile(finish):
            ahead = gate_up(0)
            for r in range(tm // FF_ROWS):
                gate, up = ahead
                if r + 1 < tm // FF_ROWS:
                    ahead = gate_up(r + 1)
                rows = pl.ds(r * FF_ROWS, FF_ROWS)
                gate_ref[rows, :] = gate.astype(BF16)
                up_ref[rows, :] = up.astype(BF16)
                act = (gate * _sig(gate) * up).astype(BF16)
                fv = f_ref[rows, :] + jnp.dot(act, wd_ref[...], preferred_element_type=F32)
                if finish:
                    loss_head(rows, fv)
                else:
                    f_ref[rows, :] = fv

        @pl.when(j < last)
        def _():
            tile(False)

        @pl.when(j == last)
        def _():
            tile(True)

    rows_spec = pl.BlockSpec((tm, D_MODEL), lambda i, j: (i, 0))
    per = pl.BlockSpec((None, 8, D_MODEL), lambda i, j: (i // per_seq, 0, 0))
    tile = pl.BlockSpec((None, tm, FF_TILE), lambda i, j: (j, i, 0))
    w_tile = pl.BlockSpec((FF_TILE, D_MODEL), lambda i, j: (j, 0))
    return _call(
        body, name="ffn_fwd", grid=(tokens // tm, D_FF // FF_TILE),
        in_specs=[rows_spec, w_tile, w_tile,
                  pl.BlockSpec((FF_TILE, D_MODEL), lambda i, j: (j, 0)),
                  rows_spec, rows_spec, per],
        out_specs=[tile, tile, rows_spec, rows_spec, pl.BlockSpec((8, D_MODEL), lambda i, j: (0, 0)), per],
        out_shape=[_sds((FF_TILES, tokens, FF_TILE), BF16), _sds((FF_TILES, tokens, FF_TILE), BF16),
                   _sds((tokens, D_MODEL), F32),
                   _sds((tokens, D_MODEL), BF16), _sds((8, D_MODEL), F32), _sds((n_seq, 8, D_MODEL), F32)],
        scratch_shapes=[pltpu.VMEM((tm, D_MODEL), F32)],
        compiler_params=_params(("arbitrary", "arbitrary")),
    )(h2, w_gate, w_up, w_down, x1, target, mod8)


def _ffn_bwd(df, gate, up, w_gate, w_up, w_down, x1, dy, mix, mod8, g_ffn, seq, tm=1024):
    tokens = df.shape[0]
    per_seq = seq // tm
    n_seq = tokens // seq
    last = D_FF // FF_TILE - 1

    def body(df_ref, gate_ref, up_ref, wg_ref, wu_ref, wd_ref, m_ref, g_ref, x1_hbm, dy_hbm, mix_hbm,
             dgate_ref, dup_ref, act_ref, dx1_ref, dmix_ref, dg_ref, dm_ref, dh_ref, late, late_sems):
        i, j = pl.program_id(0), pl.program_id(1)
        my_rows = pl.ds(pl.multiple_of(i * tm, tm), tm)
        fetches = [pltpu.make_async_copy(src.at[my_rows, :], late.at[n], late_sems.at[n])
                   for n, src in enumerate((x1_hbm, dy_hbm, mix_hbm))]

        @pl.when(j == 0)
        def _():
            dh_ref[...] = jnp.zeros_like(dh_ref)
            for cp in fetches:
                cp.start()

        @pl.when((j == 0) & (i == 0))
        def _():
            dg_ref[...] = jnp.zeros_like(dg_ref)

        @pl.when((j == 0) & (i % per_seq == 0))
        def _():
            dm_ref[...] = jnp.zeros_like(dm_ref)

        def d_act(r):
            return lax.dot_general(df_ref[pl.ds(r * FF_ROWS, FF_ROWS), :], wd_ref[...], NT,
                                   preferred_element_type=F32)

        def norm_bwd(rows, dh):
            g = g_ref[...]
            x1v = late[0, rows, :]
            rs = lax.rsqrt(jnp.mean(x1v * x1v, axis=-1, keepdims=True) + EPS)
            xhat = x1v * rs
            dm_ref[0:1, :] += jnp.sum(dh, axis=0, keepdims=True)
            dm_ref[1:2, :] += jnp.sum(dh * (xhat * g), axis=0, keepdims=True)
            dn = dh * (1.0 + m_ref[4:5, :])
            dg_ref[0:1, :] += jnp.sum(dn * xhat, axis=0, keepdims=True)
            dxh = dn * g
            dx1 = late[1, rows, :] + rs * (dxh - xhat * jnp.mean(dxh * xhat, axis=-1, keepdims=True))
            dx1_ref[rows, :] = dx1
            dm_ref[2:3, :] += jnp.sum(dx1 * late[2, rows, :], axis=0, keepdims=True)
            dmix_ref[rows, :] = (m_ref[2:3, :] * dx1).astype(BF16)

        def tile(finish):
            ahead = d_act(0)
            for r in range(tm // FF_ROWS):
                dact = ahead
                if r + 1 < tm // FF_ROWS:
                    ahead = d_act(r + 1)
                rows = pl.ds(r * FF_ROWS, FF_ROWS)
                gate = gate_ref[rows, :].astype(F32)
                up = up_ref[rows, :].astype(F32)
                sg = _sig(gate)
                silu = gate * sg
                act_ref[rows, :] = (silu * up).astype(BF16)
                dup = (dact * silu).astype(BF16)
                dgate = (dact * up * (sg * (1.0 + gate * (1.0 - sg)))).astype(BF16)
                dup_ref[rows, :] = dup
                dgate_ref[rows, :] = dgate
                dh = dh_ref[rows, :] + (jnp.dot(dgate, wg_ref[...], preferred_element_type=F32)
                                        + jnp.dot(dup, wu_ref[...], preferred_element_type=F32))
                if finish:
                    norm_bwd(rows, dh)
                else:
                    dh_ref[rows, :] = dh

        @pl.when(j < last)
        def _():
            tile(False)

        @pl.when(j == last)
        def _():
            for cp in fetches:
                cp.wait()
            tile(True)

    tile = pl.BlockSpec((None, tm, FF_TILE), lambda i, j: (j, i, 0))
    w_tile = pl.BlockSpec((FF_TILE, D_MODEL), lambda i, j: (j, 0))
    rows_spec = pl.BlockSpec((tm, D_MODEL), lambda i, j: (i, 0))
    per = pl.BlockSpec((None, 8, D_MODEL), lambda i, j: (i // per_seq, 0, 0))
    return _call(
        body, name="ffn_bwd", grid=(tokens // tm, D_FF // FF_TILE),
        in_specs=[rows_spec, tile, tile, w_tile, w_tile,
                  pl.BlockSpec((FF_TILE, D_MODEL), lambda i, j: (j, 0)),
                  per, pl.BlockSpec((1, D_MODEL), lambda i, j: (0, 0)), ANY_SPEC, ANY_SPEC, ANY_SPEC],
        out_specs=[tile, tile, tile, rows_spec, rows_spec, pl.BlockSpec((8, D_MODEL), lambda i, j: (0, 0)), per],
        out_shape=[_sds((FF_TILES, tokens, FF_TILE), BF16)] * 3
        + [_sds((tokens, D_MODEL), F32), _sds((tokens, D_MODEL), BF16),
           _sds((8, D_MODEL), F32), _sds((n_seq, 8, D_MODEL), F32)],
        scratch_shapes=[pltpu.VMEM((tm, D_MODEL), F32), pltpu.VMEM((3, tm, D_MODEL), F32),
                        pltpu.SemaphoreType.DMA((3,))],
        compiler_params=_params(("arbitrary", "arbitrary")),
    )(df, gate, up, w_gate, w_up, w_down, mod8, g_ffn, x1, dy, mix)


def _mix_in_bwd(d_a, d_g, d_q, d_k, d_v, w_in, x2, dx1, mod8, g_mix, seq, ride, tm=512):
    tokens = x2.shape[0]
    per_seq = seq // tm
    n_seq = tokens // seq
    parts = (d_a, d_g, d_q, d_k, d_v)
    width = D_CONV
    n_ride = len(ride)
    ride_scatter = [s for _, s in ride]

    def body(*refs):
        da_ref, dg_ref, dq_ref, dk_ref, dv_ref, w_ref, x_ref, dx1_ref, m_ref, g_ref = refs[:10]
        ride_in = refs[10:10 + n_ride]
        gx_ref, dgm_ref, dm_ref = refs[10 + n_ride:13 + n_ride]
        ride_args = (ride_scatter, ride_in, refs[13 + n_ride:13 + 2 * n_ride]) + tuple(refs[13 + 2 * n_ride:])
        i = pl.program_id(0)

        @pl.when(i == 0)
        def _():
            _exchange_start(*ride_args)
            dgm_ref[...] = jnp.zeros_like(dgm_ref)

        @pl.when(i % per_seq == 0)
        def _():
            dm_ref[...] = jnp.zeros_like(dm_ref)

        dh = jnp.zeros((tm, D_MODEL), F32)
        for n, ref in enumerate((da_ref, dg_ref, dq_ref, dk_ref, dv_ref)):
            dh = dh + jnp.dot(ref[...], w_ref[pl.ds(n * width, width), :], preferred_element_type=F32)
        xv = x_ref[...]
        r = lax.rsqrt(jnp.mean(xv * xv, axis=-1, keepdims=True) + EPS)
        xhat = xv * r
        g = g_ref[...]
        dm_ref[0:1, :] += jnp.sum(dh, axis=0, keepdims=True)
        dm_ref[1:2, :] += jnp.sum(dh * (xhat * g), axis=0, keepdims=True)
        dn = dh * (1.0 + m_ref[1:2, :])
        dgm_ref[0:1, :] += jnp.sum(dn * xhat, axis=0, keepdims=True)
        dxh = dn * g
        gx_ref[...] = dx1_ref[...] + r * (dxh - xhat * jnp.mean(dxh * xhat, axis=-1, keepdims=True))

        @pl.when(i == tokens // tm - 1)
        def _():
            _exchange_wait(*ride_args)

    rows = pl.BlockSpec((tm, D_MODEL), lambda i: (i, 0))
    half = pl.BlockSpec((tm, width), lambda i: (i, 0))
    per = pl.BlockSpec((None, 8, D_MODEL), lambda i: (i // per_seq, 0, 0))
    return _call(
        body, name="mix_in_bwd", grid=(tokens // tm,),
        in_specs=[half] * 5 + [pl.BlockSpec((D_IN, D_MODEL), lambda i: (0, 0)), rows, rows, per,
                               pl.BlockSpec((1, D_MODEL), lambda i: (0, 0))] + [ANY_SPEC] * n_ride,
        out_specs=[rows, pl.BlockSpec((8, D_MODEL), lambda i: (0, 0)), per] + [ANY_SPEC] * n_ride,
        out_shape=[_sds((tokens, D_MODEL), F32), _sds((8, D_MODEL), F32), _sds((n_seq, 8, D_MODEL), F32)]
        + _exchange_shapes(ride),
        scratch_shapes=_exchange_sems(n_ride),
        compiler_params=_params(("arbitrary",)),
    )(*parts, w_in, x2, dx1, mod8, g_mix, *[a for a, _ in ride])


def _grad_matmul_parts(a_parts, b_parts, name, tk=1024):
    tokens = a_parts[0].shape[0]
    na, nb = len(a_parts), len(b_parts)
    ma, nbw = a_parts[0].shape[1], b_parts[0].shape[1]

    n_k = tokens // tk

    def body(*refs):
        a_refs, b_refs, o_ref, acc = refs[:na], refs[na:na + nb], refs[na + nb], refs[na + nb + 1]

        @pl.when(pl.program_id(0) == 0)
        def _():
            acc[...] = jnp.zeros_like(acc)

        for i in range(na):
            for j in range(nb):
                acc[pl.ds(i * ma, ma), pl.ds(j * nbw, nbw)] += lax.dot_general(
                    a_refs[i][...], b_refs[j][...], TN, preferred_element_type=F32)

        @pl.when(pl.program_id(0) == n_k - 1)
        def _():
            o_ref[...] = acc[...].astype(o_ref.dtype)

    return _call(
        body, name=name, grid=(n_k,),
        in_specs=[pl.BlockSpec((tk, ma), lambda k: (k, 0))] * na + [pl.BlockSpec((tk, nbw), lambda k: (k, 0))] * nb,
        out_specs=pl.BlockSpec((na * ma, nb * nbw), lambda k: (0, 0)),
        out_shape=_sds((na * ma, nb * nbw), BF16),
        scratch_shapes=[pltpu.VMEM((na * ma, nb * nbw), F32)],
        compiler_params=_params(("arbitrary",)),
    )(*a_parts, *b_parts)


def _grad_matmul_tiles(a, b, name, tk=1024):
    tiled_b = b.ndim == 3
    tiles, tokens, width = b.shape if tiled_b else a.shape
    other = a.shape[1] if tiled_b else b.shape[1]
    out_tile = (other, width) if tiled_b else (width, other)
    n_k = tokens // tk

    def body(a_ref, b_ref, o_ref, acc):
        @pl.when(pl.program_id(0) == 0)
        def _():
            acc[...] = jnp.zeros_like(acc)

        for t in range(tiles):
            lhs = a_ref[...] if tiled_b else a_ref[t]
            rhs = b_ref[t] if tiled_b else b_ref[...]
            acc[t] += lax.dot_general(lhs, rhs, TN, preferred_element_type=F32)

        @pl.when(pl.program_id(0) == n_k - 1)
        def _():
            o_ref[...] = acc[...].astype(o_ref.dtype)

    flat = pl.BlockSpec((tk, other), lambda k: (k, 0))
    tiled = pl.BlockSpec((tiles, tk, width), lambda k: (0, k, 0))
    return _call(
        body, name=name, grid=(n_k,),
        in_specs=[flat, tiled] if tiled_b else [tiled, flat],
        out_specs=pl.BlockSpec((tiles,) + out_tile, lambda k: (0, 0, 0)),
        out_shape=_sds((tiles,) + out_tile, BF16),
        scratch_shapes=[pltpu.VMEM((tiles,) + out_tile, F32)],
        compiler_params=_params(("arbitrary",)),
    )(a, b)


def _adamw(w, m, v, g, name, n_parts=0, tr=256):
    rows, cols = w.shape
    tr = min(tr, rows)
    c1 = 1.0 - ADAM_B1 ** ADAM_STEP
    c2 = 1.0 - ADAM_B2 ** ADAM_STEP

    def body(w_ref, m_ref, v_ref, g_ref, go_ref, d_ref, mo_ref, vo_ref):
        if n_parts:
            gv = g_ref[0].astype(F32)
            for p in range(1, n_parts):
                gv = gv + g_ref[p].astype(F32)
        else:
            gv = g_ref[...]
        go_ref[...] = gv
        mn = ADAM_B1 * m_ref[...] + (1.0 - ADAM_B1) * gv
        vn = ADAM_B2 * v_ref[...] + (1.0 - ADAM_B2) * (gv * gv)
        mo_ref[...] = mn
        vo_ref[...] = vn
        d_ref[...] = -ADAM_LR * ((mn / c1) / (jnp.sqrt(vn / c2) + ADAM_EPS) + ADAM_WD * w_ref[...])

    blk = pl.BlockSpec((tr, cols), lambda i: (i, 0))
    g_spec = pl.BlockSpec((n_parts, tr, cols), lambda i: (0, i, 0)) if n_parts else blk
    return _call(
        body, name=name, grid=(rows // tr,),
        in_specs=[blk, blk, blk, g_spec], out_specs=[blk] * 4,
        out_shape=[_sds((rows, cols), F32)] * 4,
        compiler_params=_params(("parallel",)),
    )(w, m, v, g)


def _cols_to_full(blocks):
    n, r, c = blocks.shape
    return jnp.transpose(blocks, (1, 0, 2)).reshape(r, n * c)


def _pad_lanes(v, width):
    return jnp.pad(v, ((0, 0), (0, width - v.shape[1])))


def kernel(x, c, w_ada, b_ada, g_mix, w_in, w_dw, b_dw, g_conv_ln, b_conv_ln, g_q, g_k, w_out, g_ffn, w_gate, w_up, w_down, loss_target, m_w_ada, m_b_ada, m_g_mix, m_w_in, m_w_dw, m_b_dw, m_g_conv_ln, m_b_conv_ln, m_g_q, m_g_k, m_w_out, m_g_ffn, m_w_gate, m_w_up, m_w_down, v_w_ada, v_b_ada, v_g_mix, v_w_in, v_w_dw, v_b_dw, v_g_conv_ln, v_b_conv_ln, v_g_q, v_g_k, v_w_out, v_g_ffn, v_w_gate, v_w_up, v_w_down):
    n_seq, seq, _ = x.shape
    tokens = n_seq * seq
    me = 4 * lax.axis_index("x") + 2 * lax.axis_index("y") + lax.axis_index("c")
    ada_cols = w_ada.shape[2]
    dw_cols = w_dw.shape[2]

    def transposed(w):
        return jnp.transpose(w[0])

    (c_g, w_in_g, w_dw_g) = _gather_by_chip([c, transposed(w_in).astype(BF16), w_dw[0]], "gather_weights")
    c_all = c_g.reshape(N_DEV * n_seq, D_MODEL)
    w_in_t = w_in_g.reshape(D_IN, D_MODEL)
    w_dw_f = _cols_to_full(w_dw_g)

    b_cols = lax.dynamic_slice(b_ada, (0, me * ada_cols), (1, ada_cols))
    mod_cols = _ada_fwd(c_all, w_ada[0], b_cols)
    (mod_g,) = _exchange([(mod_cols, False)], "gather_mod")
    mod_mine = lax.dynamic_slice(mod_g, (0, me * n_seq, 0), (N_DEV, n_seq, ada_cols))
    mod = jnp.transpose(mod_mine, (1, 0, 2)).reshape(n_seq, N_MOD, D_MODEL)
    mod8 = jnp.pad(mod, ((0, 0), (0, 8 - N_MOD), (0, 0)))

    x2 = x.reshape(tokens, D_MODEL)
    h1, proj = _mix_in(x2, mod8, g_mix, w_in_t, seq)
    proj3 = proj.reshape(D_IN // LANES, n_seq, seq, LANES)
    uc3 = _conv_fwd(proj3, w_dw_f, b_dw)
    g_q2, g_k2 = jnp.tile(g_q, (1, 2)), jnp.tile(g_k, (1, 2))
    y_att3, lse3, w_out_g, w_gate_g, w_up_g, w_down_g = _attn_fwd(
        proj3, g_q2, g_k2,
        [(w_out[0].astype(BF16), False), (transposed(w_gate).astype(BF16), False),
         (transposed(w_up).astype(BF16), False), (w_down[0].astype(BF16), False)])
    w_out_f = w_out_g.reshape(D_MODEL, D_MODEL)
    w_gate_f = w_gate_g.reshape(D_FF, D_MODEL)
    w_up_f = w_up_g.reshape(D_FF, D_MODEL)
    w_down_f = w_down_g.reshape(D_FF, D_MODEL)
    uc2 = uc3.reshape(tokens, D_CONV)
    y_att2 = y_att3.reshape(tokens, D_ATT)
    y_conv, mix, x1, h2 = _mix_out(uc2, y_att2, x2, mod8, g_conv_ln, b_conv_ln, g_ffn, w_out_f, seq)
    gate, up, dy, df, sq, dgate_f = _ffn_fwd(
        h2, w_gate_f, w_up_f, w_down_f, x1, loss_target.reshape(tokens, D_MODEL), mod8, seq)

    dgate, dup, act, dx1, dmix, dg_ffn, dmod_f = _ffn_bwd(
        df, gate, up, w_gate_f, w_up_f, w_down_f, x1, dy, mix, mod8, g_ffn, seq)
    duc2, do2, dgb_ln = _mix_out_bwd(dmix, uc2, g_conv_ln, b_conv_ln, w_out_f)
    d_a3, d_g3, dw_dw_p, db_dw_p = _conv_bwd(duc2.reshape(n_seq, seq, D_CONV), proj3, w_dw_f)
    gw_gate = _grad_matmul_tiles(dgate, h2, "grad_w_gate")
    gw_up = _grad_matmul_tiles(dup, h2, "grad_w_up")
    gw_down = _grad_matmul_tiles(act, df, "grad_w_down")
    gw_out = _grad_matmul_parts([y_conv, y_att2], [dmix], "grad_w_out")
    d_q3, d_k3, d_v3, dg_qk, p_gate, p_up, p_down, p_out = _attn_bwd(
        proj3, do2.reshape(n_seq, seq, D_ATT), y_att3, lse3, g_q2, g_k2,
        [(gw_gate.reshape(N_DEV, D_FF // N_DEV, D_MODEL), True), (gw_up.reshape(N_DEV, D_FF // N_DEV, D_MODEL), True),
         (gw_down.reshape(N_DEV, D_FF // N_DEV, D_MODEL), True),
         (gw_out.reshape(N_DEV, D_MODEL // N_DEV, D_MODEL), True)])
    flat = lambda t: t.reshape(tokens, t.shape[-1])
    d_a, d_g, d_q, d_k, d_v = flat(d_a3), flat(d_g3), flat(d_q3), flat(d_k3), flat(d_v3)
    gw_in = _grad_matmul_parts([d_a, d_g, d_q, d_k, d_v], [h1], "grad_w_in")
    grad_x2, dg_mix, dmod_m, p_in = _mix_in_bwd(
        d_a, d_g, d_q, d_k, d_v, w_in_t, x2, dx1, mod8, g_mix, seq,
        [(gw_in.reshape(N_DEV, D_IN // N_DEV, D_MODEL), True)])

    dmod = jnp.concatenate([dmod_m[:, 0], dmod_m[:, 1], dmod_f[:, 2], dmod_f[:, 0], dmod_f[:, 1], dgate_f[:, 0]], axis=1)
    dg_q = dg_qk[0:1, 0:HEAD_DIM] + dg_qk[0:1, HEAD_DIM:]
    dg_k = dg_qk[1:2, 0:HEAD_DIM] + dg_qk[1:2, HEAD_DIM:]
    loss_part = (0.5 / D_MODEL) * jnp.sum(sq[0:1, :], axis=1, keepdims=True)
    small = jnp.concatenate(
        [dg_mix[0:1], dg_ffn[0:1], db_dw_p[0:1], dgb_ln[0:1], dgb_ln[1:2],
         _pad_lanes(dg_q, LANES), _pad_lanes(dg_k, LANES), _pad_lanes(loss_part, LANES)], axis=1)
    n_small = small.shape[1] - LANES

    (dmod_g, small_g, dw_g) = _exchange([(dmod, False), (small, False), (dw_dw_p, False)], "gather_small_grads")

    dmod_all = dmod_g.reshape(N_DEV * n_seq, N_MOD * D_MODEL)
    dmod_cols = lax.dynamic_slice(dmod_all, (0, me * ada_cols), (N_DEV * n_seq, ada_cols))
    gw_ada, gb_ada = _ada_bwd(c_all, dmod_cols, dmod_all)

    res = {}
    res["w_ada"] = _adamw(w_ada[0], m_w_ada[0], v_w_ada[0], gw_ada, "adamw_w_ada")
    res["b_ada"] = _adamw(b_ada, m_b_ada, v_b_ada, gb_ada, "adamw_b_ada")
    def adamw_transposed(w, m, v, parts, name, tr):
        outs = _adamw(transposed(w), transposed(m), transposed(v), parts, name, N_DEV, tr=tr)
        return tuple(jnp.transpose(o) for o in outs)

    res["w_in"] = adamw_transposed(w_in, m_w_in, v_w_in, p_in, "adamw_w_in", 160)
    res["w_out"] = _adamw(w_out[0], m_w_out[0], v_w_out[0], p_out, "adamw_w_out", N_DEV)
    res["w_gate"] = adamw_transposed(w_gate, m_w_gate, v_w_gate, p_gate, "adamw_w_gate", 176)
    res["w_up"] = adamw_transposed(w_up, m_w_up, v_w_up, p_up, "adamw_w_up", 176)
    res["w_down"] = _adamw(w_down[0], m_w_down[0], v_w_down[0], p_down, "adamw_w_down", N_DEV, tr=176)
    dw_mine = lax.dynamic_slice(dw_g, (0, 0, me * dw_cols), (N_DEV, CONV_WIDTH, dw_cols))
    res["w_dw"] = _adamw(w_dw[0], m_w_dw[0], v_w_dw[0], dw_mine, "adamw_w_dw", N_DEV)

    small_names = ["g_mix", "g_ffn", "b_dw", "g_conv_ln", "b_conv_ln", "g_q", "g_k"]
    small_w = {"g_mix": (g_mix, m_g_mix, v_g_mix), "g_ffn": (g_ffn, m_g_ffn, v_g_ffn), "b_dw": (b_dw, m_b_dw, v_b_dw),
               "g_conv_ln": (g_conv_ln, m_g_conv_ln, v_g_conv_ln), "b_conv_ln": (b_conv_ln, m_b_conv_ln, v_b_conv_ln),
               "g_q": (g_q, m_g_q, v_g_q), "g_k": (g_k, m_g_k, v_g_k)}
    widths = [max(small_w[n][0].shape[1], LANES) for n in small_names]
    packed = [jnp.concatenate([_pad_lanes(small_w[n][i], wd) for n, wd in zip(small_names, widths)], axis=1) for i in range(3)]
    outs = _adamw(packed[0], packed[1], packed[2], small_g[:, :, :n_small], "adamw_small", N_DEV)
    off = 0
    for n, wd in zip(small_names, widths):
        real = small_w[n][0].shape[1]
        res[n] = tuple(o[:, off:off + real] for o in outs)
        off += wd
    loss = jnp.sum(small_g[:, 0, n_small])

    order = ["w_ada", "b_ada", "g_mix", "w_in", "w_dw", "b_dw", "g_conv_ln", "b_conv_ln", "g_q", "g_k",
             "w_out", "g_ffn", "w_gate", "w_up", "w_down"]
    lead = {"w_ada", "w_in", "w_dw", "w_out", "w_gate", "w_up", "w_down"}
    grads, deltas, new_m, new_v = [], [], [], []
    for n in order:
        g, d, mn, vn = res[n]
        g, d, mn, vn = (t[None] if n in lead else t for t in (g, d, mn, vn))
        grads.append(g)
        deltas.append(d)
        new_m.append(mn)
        new_v.append(vn)
    return (loss, grad_x2.reshape(n_seq, seq, D_MODEL), *grads, *deltas, *new_m, *new_v)
```

```python
import numpy as np
import jax
import jax.numpy as jnp
from jax import lax
from jax.experimental import pallas as pl
from jax.experimental.pallas import tpu as pltpu

F32 = jnp.float32
BF16 = jnp.bfloat16

N_DEV = 8
D_MODEL = 1024
D_CONV = 512
D_ATT = 512
HEAD_DIM = 64
CONV_WIDTH = 31
D_IN = 2 * D_CONV + 3 * D_ATT
D_FF = 2816
N_MOD = 6
EPS = 1e-6
RADIUS = 64
DILATIONS = (1, 4, 16)
Q_BLOCK = 128
LANES = 128
VMEM_LIMIT = 56 * 1024 * 1024

ADAM_LR = 0.001
ADAM_B1 = 0.9
ADAM_B2 = 0.999
ADAM_EPS = 1e-08
ADAM_WD = 0.01
ADAM_STEP = 10

NT = (((1,), (1,)), ((), ()))
TN = (((0,), (0,)), ((), ()))


def _call(body, **kw):
    return pl.pallas_call(body, **kw)


def _params(sem=None, vmem=VMEM_LIMIT):
    return pltpu.CompilerParams(dimension_semantics=sem, vmem_limit_bytes=vmem)


def _sig(x):
    return 1.0 / (1.0 + jnp.exp(-x))


def _sds(shape, dtype):
    return jax.ShapeDtypeStruct(shape, dtype)


N_PEER = N_DEV - 1
ANY_SPEC = pl.BlockSpec(memory_space=pl.ANY)


def _exchange_copies(scatter, ins, outs, *sems):
    n = len(ins)
    if n == 0:
        return [], []
    send_sems, recv_sems, local_sems = sems
    x, y, c = lax.axis_index("x"), lax.axis_index("y"), lax.axis_index("c")
    me = 4 * x + 2 * y + c

    def src(a, slot):
        return ins[a].at[slot] if scatter[a] else ins[a]

    local = [pltpu.make_async_copy(src(a, me), outs[a].at[me], local_sems.at[a]) for a in range(n)]
    flights = []
    for k in range(1, N_DEV):
        px = 1 - x if k & 4 else x
        py = 1 - y if k & 2 else y
        pc = 1 - c if k & 1 else c
        pid = 4 * px + 2 * py + pc
        for a in range(n):
            i = a * N_PEER + k - 1
            send, recv = (pltpu.make_async_remote_copy(
                src_ref=src(a, pid), dst_ref=outs[a].at[slot],
                send_sem=send_sems.at[i], recv_sem=recv_sems.at[i],
                device_id=(px, py, pc), device_id_type=pl.DeviceIdType.MESH) for slot in (me, pid))
            flights.append((send, recv))
    return local, flights


def _exchange_start(*args):
    local, flights = _exchange_copies(*args)
    for cp in local:
        cp.start()
    for send, _ in flights:
        send.start()


def _exchange_wait(*args):
    local, flights = _exchange_copies(*args)
    for send, recv in flights:
        send.wait_send()
        recv.wait_recv()
    for cp in local:
        cp.wait()


def _exchange_shapes(items):
    return [_sds((N_DEV,) + tuple(arr.shape[1:] if scatter else arr.shape), arr.dtype) for arr, scatter in items]


def _exchange_sems(n):
    if n == 0:
        return []
    return [pltpu.SemaphoreType.DMA((n * N_PEER,)), pltpu.SemaphoreType.DMA((n * N_PEER,)),
            pltpu.SemaphoreType.DMA((n,))]


def _gather_by_chip_phase(phase, ins, outs, send_sems, recv_sems, local_sems):
    n = len(ins)
    per = N_PEER
    x, y, c = lax.axis_index("x"), lax.axis_index("y"), lax.axis_index("c")
    me, sibling = (x, y, c), (x, y, 1 - c)
    chips = [(1 - x, y), (x, 1 - y), (1 - x, 1 - y)]

    def slot(px, py, pc):
        return 4 * px + 2 * py + pc

    def copy(a, k, block, to, src=None):
        dst = outs[a].at[slot(*block)]
        return pltpu.make_async_remote_copy(
            src_ref=dst if src is None else src, dst_ref=dst,
            send_sem=send_sems.at[a * per + k], recv_sem=recv_sems.at[a * per + k],
            device_id=to, device_id_type=pl.DeviceIdType.MESH)

    local = [pltpu.make_async_copy(ins[a], outs[a].at[slot(*me)], local_sems.at[a]) for a in range(n)]
    first = []
    for a in range(n):
        first.append(copy(a, 0, me, sibling, src=ins[a]))
        first += [copy(a, 1 + j, me, (*chip, c), src=ins[a]) for j, chip in enumerate(chips)]
    passed = [copy(a, 4 + j, (*chip, c), sibling) for j, chip in enumerate(chips) for a in range(n)]
    if phase == 0:
        for cp in local + first:
            cp.start()
    elif phase == 1:
        for j, chip in enumerate(chips):
            for a in range(n):
                copy(a, 1 + j, (*chip, c), me).wait_recv()
        for cp in passed:
            cp.start()
    else:
        for a in range(n):
            copy(a, 0, sibling, me).wait_recv()
            for j, chip in enumerate(chips):
                copy(a, 4 + j, (*chip, 1 - c), me).wait_recv()
        for cp in first + passed:
            cp.wait_send()
        for cp in local:
            cp.wait()


def _gather_by_chip(arrays, name):
    n = len(arrays)

    def body(*refs):
        for phase in range(3):
            _gather_by_chip_phase(phase, refs[:n], refs[n:2 * n], *refs[2 * n:])

    return _call(
        body, name=name, out_shape=_exchange_shapes([(arr, False) for arr in arrays]),
        in_specs=[ANY_SPEC] * n, out_specs=[ANY_SPEC] * n, scratch_shapes=_exchange_sems(n),
    )(*arrays)


def _exchange(items, name):
    n = len(items)
    scatter = [s for _, s in items]

    def body(*refs):
        args = (scatter, refs[:n], refs[n:2 * n]) + tuple(refs[2 * n:])
        _exchange_start(*args)
        _exchange_wait(*args)

    return _call(
        body, name=name, out_shape=_exchange_shapes(items),
        in_specs=[ANY_SPEC] * n, out_specs=[ANY_SPEC] * n, scratch_shapes=_exchange_sems(n),
    )(*[a for a, _ in items])


def _ada_fwd(c_all, w_ada, b_cols):
    def body(c_ref, w_ref, b_ref, o_ref):
        cv = c_ref[...]
        sc = (cv * _sig(cv)).astype(BF16)
        o_ref[...] = jnp.dot(sc, w_ref[...].astype(BF16), preferred_element_type=F32) + b_ref[...]

    return _call(body, name="ada_fwd", out_shape=_sds((c_all.shape[0], w_ada.shape[1]), F32),
                 compiler_params=_params())(c_all, w_ada, b_cols)


def _ada_bwd(c_all, dmod_cols, dmod_all):
    def body(c_ref, dc_ref, da_ref, gw_ref, gb_ref):
        cv = c_ref[...]
        sc = (cv * _sig(cv)).astype(BF16)
        gw_ref[...] = lax.dot_general(sc, dc_ref[...].astype(BF16), TN, preferred_element_type=F32)
        gb_ref[...] = jnp.sum(da_ref[...], axis=0, keepdims=True)

    return _call(body, name="ada_bwd",
                 out_shape=[_sds((c_all.shape[1], dmod_cols.shape[1]), F32), _sds((1, dmod_all.shape[1]), F32)],
                 compiler_params=_params())(c_all, dmod_cols, dmod_all)


MIX_ROWS = 128


def _mix_in(x2, mod8, g_mix, w_in, seq, tm=512):
    tokens = x2.shape[0]
    per_seq = seq // tm

    def body(x_ref, m_ref, g_ref, wt_ref, h_ref, p_ref, w_ref):
        @pl.when(pl.program_id(0) == 0)
        def _():
            w_ref[...] = wt_ref[...].T

        def normed(c):
            rows = pl.ds(c * MIX_ROWS, MIX_ROWS)
            xv = x_ref[rows, :]
            r = lax.rsqrt(jnp.mean(xv * xv, axis=-1, keepdims=True) + EPS)
            hb = ((xv * r * g_ref[...]) * (1.0 + m_ref[1:2, :]) + m_ref[0:1, :]).astype(BF16)
            h_ref[rows, :] = hb
            return hb

        ahead = normed(0)
        for c in range(tm // MIX_ROWS):
            hb = ahead
            if c + 1 < tm // MIX_ROWS:
                ahead = normed(c + 1)
            p = jnp.dot(hb, w_ref[...], preferred_element_type=F32)
            for cb in range(D_IN // LANES):
                p_ref[cb, pl.ds(c * MIX_ROWS, MIX_ROWS), :] = p[:, cb * LANES:(cb + 1) * LANES]

    return _call(
        body, name="mix_in", grid=(tokens // tm,),
        in_specs=[pl.BlockSpec((tm, D_MODEL), lambda i: (i, 0)),
                  pl.BlockSpec((None, 8, D_MODEL), lambda i: (i // per_seq, 0, 0)),
                  pl.BlockSpec((1, D_MODEL), lambda i: (0, 0)),
                  pl.BlockSpec((D_IN, D_MODEL), lambda i: (0, 0))],
        out_specs=[pl.BlockSpec((tm, D_MODEL), lambda i: (i, 0)),
                   pl.BlockSpec((D_IN // LANES, tm, LANES), lambda i: (0, i, 0))],
        out_shape=[_sds((tokens, D_MODEL), BF16), _sds((D_IN // LANES, tokens, LANES), F32)],
        scratch_shapes=[pltpu.VMEM((D_MODEL, D_IN), BF16)],
        compiler_params=_params(("arbitrary",)),
    )(x2, mod8, g_mix, w_in)


CONV_ROWS = 64
CONV_DW_ROWS = 32
CONV_DW_UNROLL = 4
CONV_HALO = 16


def _fill_shifted(xp, sh, seq):
    for b in range(8):
        sh[b, pl.ds(0, seq + 24), :] = xp[pl.ds(b, seq + 24), :]


def _conv_fwd(proj3, w_dw, b_dw):
    _, n_seq, seq, _ = proj3.shape
    n_cb = D_CONV // LANES

    def body(a_ref, g_ref, w_ref, b_ref, uc_ref, xp, sh):
        zeros = jnp.zeros((CONV_HALO, LANES), F32)
        xp[pl.ds(0, CONV_HALO), :] = zeros
        xp[pl.ds(CONV_HALO + seq, CONV_HALO), :] = zeros
        xp[pl.ds(CONV_HALO, seq), :] = a_ref[...] * _sig(g_ref[...])
        _fill_shifted(xp, sh, seq)

        def blk(i, carry):
            t0 = pl.multiple_of(i * CONV_ROWS, CONV_ROWS)
            acc = jnp.zeros((CONV_ROWS, LANES), F32)
            for j in range(CONV_WIDTH):
                jj = j + 1
                acc = acc + sh[jj % 8, pl.ds(t0 + 8 * (jj // 8), CONV_ROWS), :] * w_ref[j:j + 1, :]
            uc_ref[pl.ds(t0, CONV_ROWS), :] = acc + b_ref[...]
            return carry

        lax.fori_loop(0, seq // CONV_ROWS, blk, 0)

    return _call(
        body, name="conv_fwd", grid=(n_seq, n_cb),
        in_specs=[pl.BlockSpec((None, None, seq, LANES), lambda b, cb: (cb, b, 0, 0)),
                  pl.BlockSpec((None, None, seq, LANES), lambda b, cb: (n_cb + cb, b, 0, 0)),
                  pl.BlockSpec((CONV_WIDTH, LANES), lambda b, cb: (0, cb)),
                  pl.BlockSpec((1, LANES), lambda b, cb: (0, cb))],
        out_specs=pl.BlockSpec((None, seq, LANES), lambda b, cb: (b, 0, cb)),
        out_shape=_sds((n_seq, seq, D_CONV), F32),
        scratch_shapes=[pltpu.VMEM((seq + 2 * CONV_HALO, LANES), F32),
                        pltpu.VMEM((8, seq + 2 * CONV_HALO, LANES), F32)],
        compiler_params=_params(("parallel", "parallel")),
    )(proj3, proj3, w_dw, b_dw)


def _conv_bwd(duc3, proj3, w_dw):
    _, n_seq, seq, _ = proj3.shape
    n_cb = D_CONV // LANES

    def body(duc_ref, a_ref, g_ref, w_ref, da_ref, dg_ref, dw_ref, db_ref, xp, sh):
        @pl.when(pl.program_id(1) == 0)
        def _():
            dw_ref[...] = jnp.zeros_like(dw_ref)
            db_ref[...] = jnp.zeros_like(db_ref)

        zeros = jnp.zeros((CONV_HALO, LANES), F32)
        xp[pl.ds(0, CONV_HALO), :] = zeros
        xp[pl.ds(CONV_HALO + seq, CONV_HALO), :] = zeros
        xp[pl.ds(CONV_HALO, seq), :] = a_ref[...] * _sig(g_ref[...])
        _fill_shifted(xp, sh, seq)
        for j0 in range(0, CONV_WIDTH, 8):
            taps = range(j0, min(j0 + 8, CONV_WIDTH))

            def wblk(i, accs, taps=taps):
                for u in range(CONV_DW_UNROLL):
                    t0 = pl.multiple_of((i * CONV_DW_UNROLL + u) * CONV_DW_ROWS, CONV_DW_ROWS)
                    d = duc_ref[pl.ds(t0, CONV_DW_ROWS), :]
                    accs = tuple(acc + d * sh[(j + 1) % 8, pl.ds(t0 + 8 * ((j + 1) // 8), CONV_DW_ROWS), :]
                                 for acc, j in zip(accs, taps))
                return accs

            accs = lax.fori_loop(0, seq // (CONV_DW_ROWS * CONV_DW_UNROLL), wblk,
                                 tuple(jnp.zeros((CONV_DW_ROWS, LANES), F32) for _ in taps))
            for acc, j in zip(accs, taps):
                dw_ref[j:j + 1, :] += jnp.sum(acc, axis=0, keepdims=True)
        db_ref[0:1, :] += jnp.sum(duc_ref[...], axis=0, keepdims=True)
        xp[pl.ds(CONV_HALO, seq), :] = duc_ref[...]
        _fill_shifted(xp, sh, seq)

        def ublk(i, carry):
            t0 = pl.multiple_of(i * CONV_ROWS, CONV_ROWS)
            acc = jnp.zeros((CONV_ROWS, LANES), F32)
            for j in range(CONV_WIDTH):
                jj = CONV_WIDTH - j
                acc = acc + sh[jj % 8, pl.ds(t0 + 8 * (jj // 8), CONV_ROWS), :] * w_ref[j:j + 1, :]
            av = a_ref[pl.ds(t0, CONV_ROWS), :]
            sg = _sig(g_ref[pl.ds(t0, CONV_ROWS), :])
            da_ref[pl.ds(t0, CONV_ROWS), :] = (acc * sg).astype(BF16)
            dg_ref[pl.ds(t0, CONV_ROWS), :] = (acc * av * sg * (1.0 - sg)).astype(BF16)
            return carry

        lax.fori_loop(0, seq // CONV_ROWS, ublk, 0)

    return _call(
        body, name="conv_bwd", grid=(n_cb, n_seq),
        in_specs=[pl.BlockSpec((None, seq, LANES), lambda cb, b: (b, 0, cb)),
                  pl.BlockSpec((None, None, seq, LANES), lambda cb, b: (cb, b, 0, 0)),
                  pl.BlockSpec((None, None, seq, LANES), lambda cb, b: (n_cb + cb, b, 0, 0)),
                  pl.BlockSpec((CONV_WIDTH, LANES), lambda cb, b: (0, cb))],
        out_specs=[pl.BlockSpec((None, seq, LANES), lambda cb, b: (b, 0, cb)),
                   pl.BlockSpec((None, seq, LANES), lambda cb, b: (b, 0, cb)),
                   pl.BlockSpec((32, LANES), lambda cb, b: (0, cb)),
                   pl.BlockSpec((8, LANES), lambda cb, b: (0, cb))],
        out_shape=[_sds((n_seq, seq, D_CONV), BF16), _sds((n_seq, seq, D_CONV), BF16),
                   _sds((32, D_CONV), F32), _sds((8, D_CONV), F32)],
        scratch_shapes=[pltpu.VMEM((seq + 2 * CONV_HALO, LANES), F32),
                        pltpu.VMEM((8, seq + 2 * CONV_HALO, LANES), F32)],
        compiler_params=_params(("parallel", "arbitrary")),
    )(duc3, proj3, proj3, w_dw)


MASKED = 1e30
ATT_ROWS = 512
ATT_UNROLL = 8
ATT_FWD_UNROLL = 8


def _distance_mats(dil, seg_len):
    kw = min(2 * Q_BLOCK, seg_len)
    offsets = (0, -RADIUS, -2 * RADIUS) if kw == 2 * Q_BLOCK else (0,)
    a = np.arange(Q_BLOCK)[:, None]
    b = np.arange(kw)[None, :]
    mats = []
    for off in offsets:
        rel = np.abs(b + off - a)
        mats.append(np.where(rel <= RADIUS, dil * rel, MASKED))
    return jnp.asarray(np.stack(mats).astype(np.float32))


def _alibi_rows():
    s = np.zeros((4, 8, LANES), np.float32)
    for hp in range(4):
        for hl in range(2):
            s[hp, hl, :] = 2.0 ** (-(2 * hp + hl + 1))
    return jnp.asarray(s)


def _window(n, seg_len):
    i0 = pl.multiple_of(n * Q_BLOCK, Q_BLOCK)
    if seg_len <= Q_BLOCK:
        return i0, i0, 0
    per_seg = seg_len // Q_BLOCK
    j = n % per_seg
    seg0 = (n // per_seg) * seg_len
    ks_local = jnp.clip(j * Q_BLOCK - RADIUS, 0, seg_len - 2 * Q_BLOCK)
    ks = pl.multiple_of(seg0 + ks_local, RADIUS)
    var = jnp.where(j == 0, 0, jnp.where(j == per_seg - 1, 2, 1))
    return i0, ks, var


def _first_head(rows):
    return lax.broadcasted_iota(jnp.int32, (rows, LANES), 1) < HEAD_DIM


def _same_head():
    head = np.arange(LANES) // HEAD_DIM
    return jnp.asarray((head[:, None] == head[None, :]).astype(np.float32)).astype(BF16)


def _head_sum(x, same_ref):
    hi = x.astype(BF16)
    lo = (x - hi.astype(F32)).astype(BF16)
    return (jnp.dot(hi, same_ref[...], preferred_element_type=F32)
            + jnp.dot(lo, same_ref[...], preferred_element_type=F32))


def _head_mean(x, same_ref):
    return _head_sum(x, same_ref) * (1.0 / HEAD_DIM)


def _per_head(x, first):
    swapped = pltpu.roll(x, HEAD_DIM, 1)
    return jnp.where(first, x, swapped), jnp.where(first, swapped, x)


STRIDE = 4


def _gather_segments(src, dil, seq, tmp, put):
    if dil == 1:
        put(0, seq, src[pl.ds(0, seq), :])
    elif dil == STRIDE:
        seg = seq // dil
        for r in range(dil):
            put(r * seg, seg, src[pl.ds(r, seg, stride=dil), :])
    else:
        part, seg = seq // STRIDE, seq // dil
        for b in range(STRIDE):
            tmp[pl.ds(b * part, part), :] = src[pl.ds(b, part, stride=STRIDE), :]
        for b in range(STRIDE):
            for a in range(dil // STRIDE):
                put(b * part + a * seg, seg, tmp[pl.ds(b * part + a, seg, stride=dil // STRIDE), :])


def _scatter_segments(dst, get, dil, seq, tmp, accumulate):
    def write(rows, val):
        if accumulate:
            dst[rows, :] += val
        else:
            dst[rows, :] = val

    if dil == 1:
        write(pl.ds(0, seq), get(0, seq))
    elif dil == STRIDE:
        seg = seq // dil
        for r in range(dil):
            write(pl.ds(r, seg, stride=dil), get(r * seg, seg))
    else:
        part, seg = seq // STRIDE, seq // dil
        for b in range(STRIDE):
            for a in range(dil // STRIDE):
                tmp[pl.ds(b * part + a, seg, stride=dil // STRIDE), :] = get(b * part + a * seg, seg)
        for b in range(STRIDE):
            write(pl.ds(b, part, stride=STRIDE), tmp[pl.ds(b * part, part), :])


def _permute_rows(dst, src, dil, seq, tmp):
    def put(start, size, val):
        dst[pl.ds(start, size), :] = val.astype(dst.dtype)

    _gather_segments(src, dil, seq, tmp, put)


def _permute_rows_by_head(dst, src, dil, seq, tmp):
    def put(start, size, val):
        first = _first_head(size)
        dst[0, pl.ds(start, size), :] = jnp.where(first, val, 0.0).astype(dst.dtype)
        dst[1, pl.ds(start, size), :] = jnp.where(first, 0.0, val).astype(dst.dtype)

    _gather_segments(src, dil, seq, tmp, put)


def _qk_normalise(q_ref, g2_ref, same_ref, dst, seq, scale):
    def chunk(ci, carry):
        rows = pl.ds(pl.multiple_of(ci * ATT_ROWS, ATT_ROWS), ATT_ROWS)
        qv = q_ref[rows, :]
        r = lax.rsqrt(_head_mean(qv * qv, same_ref) + EPS)
        dst[rows, :] = qv * r * (g2_ref[...] * scale)
        return carry

    lax.fori_loop(0, seq // ATT_ROWS, chunk, 0)


def _attn_fwd(proj3, g_q2, g_k2, ride):
    _, n_seq, seq, _ = proj3.shape
    dms = [_distance_mats(d, seq // d) for d in DILATIONS]
    same = _same_head()
    col0 = 2 * D_CONV // LANES
    n_hp = D_ATT // LANES

    n_ride = len(ride)
    assert not any(scatter for _, scatter in ride), "the forward's ride is an all-gather"

    def body(*refs):
        q_ref, k_ref, v_ref, gq_ref, gk_ref, sl_ref, dm1, dm4, dm16, same_ref = refs[:10]
        ride_in = refs[10:10 + n_ride]
        y_ref, lse_ref = refs[10 + n_ride:12 + n_ride]
        ride_out = refs[12 + n_ride:12 + 2 * n_ride]
        (qf, kf, qp, kp, vp, oml_p, o1, o4, o16, m1, m4, m16, l1, l4, l16,
         tmp) = refs[12 + 2 * n_ride:28 + 2 * n_ride]
        o_nat, m_nat, l_nat = (o1, o4, o16), (m1, m4, m16), (l1, l4, l16)
        ride_args = (ride_in, ride_out) + tuple(refs[28 + 2 * n_ride:])
        step = pl.program_id(0) * n_hp + pl.program_id(1)
        n_steps = n_seq * n_hp

        if n_ride:
            for phase, at in enumerate((0, (3 * n_steps) // 4)):
                @pl.when(step == at)
                def _(phase=phase):
                    _gather_by_chip_phase(phase, *ride_args)

        dm_refs = (dm1, dm4, dm16)
        _qk_normalise(q_ref, gq_ref, same_ref, qf, seq, HEAD_DIM ** -0.5)
        _qk_normalise(k_ref, gk_ref, same_ref, kf, seq, 1.0)
        slopes = (sl_ref[0:1, 0:1], sl_ref[1:2, 0:1])
        for pi, dil in enumerate(DILATIONS):
            seg = seq // dil
            kw = min(2 * Q_BLOCK, seg)
            _permute_rows_by_head(qp, qf, dil, seq, tmp)
            _permute_rows(kp, kf, dil, seq, tmp)
            _permute_rows(vp, v_ref, dil, seq, tmp)

            def blk(it, carry, seg=seg, kw=kw, pi=pi, dst=oml_p):
                first = _first_head(Q_BLOCK)
                chains = [(sub, h) for sub in range(ATT_FWD_UNROLL) for h in range(2)]
                win = [_window(it * ATT_FWD_UNROLL + sub, seg) for sub in range(ATT_FWD_UNROLL)]
                s = {}
                for sub, h in chains:
                    i0, ks, var = win[sub]
                    s[sub, h] = lax.dot_general(qp[h, pl.ds(i0, Q_BLOCK), :], kp[pl.ds(ks, kw), :], NT,
                                                preferred_element_type=F32) - slopes[h] * dm_refs[pi][var]
                m, l, p = {}, {}, {}
                for c in chains:
                    m[c] = jnp.max(s[c], axis=1, keepdims=True)
                    e = jnp.exp(s[c] - m[c])
                    l[c] = jnp.sum(e, axis=1, keepdims=True)
                    p[c] = e.astype(BF16)
                o = {}
                for sub, h in chains:
                    o[sub, h] = jnp.dot(p[sub, h], vp[pl.ds(win[sub][1], kw), :], preferred_element_type=F32)
                packed = [jnp.concatenate([jnp.where(first, t[sub, 0], t[sub, 1]) for t in (o, m, l)], axis=1)
                          for sub in range(ATT_FWD_UNROLL)]
                span = ATT_FWD_UNROLL * Q_BLOCK
                dst[pl.ds(pl.multiple_of(it * span, span), span), :] = jnp.concatenate(packed, axis=0)
                return carry

            lax.fori_loop(0, seq // (Q_BLOCK * ATT_FWD_UNROLL), blk, 0)
            for n, nat in enumerate((o_nat[pi], m_nat[pi], l_nat[pi])):
                _scatter_segments(nat, lambda start, size, n=n: oml_p[pl.ds(start, size), pl.ds(n * LANES, LANES)],
                                  dil, seq, tmp, accumulate=False)

        def merge(ci, carry):
            rows = pl.ds(pl.multiple_of(ci * ATT_ROWS, ATT_ROWS), ATT_ROWS)
            ms = [m_nat[pi][rows, :] for pi in range(3)]
            m_all = jnp.maximum(jnp.maximum(ms[0], ms[1]), ms[2])
            es = [jnp.exp(m - m_all) for m in ms]
            l_all = sum(l_nat[pi][rows, :] * es[pi] for pi in range(3))
            inv = 1.0 / l_all
            o = sum(o_nat[pi][rows, :] * (es[pi] * inv) for pi in range(3))
            y_ref[rows, :] = o.astype(BF16)
            lse_ref[rows, :] = m_all + jnp.log(l_all)
            return carry

        lax.fori_loop(0, seq // ATT_ROWS, merge, 0)

        if n_ride:
            @pl.when(step == n_steps - 1)
            def _():
                _gather_by_chip_phase(2, *ride_args)

    def col(off):
        return pl.BlockSpec((None, None, seq, LANES), lambda b, hp: (col0 + off * n_hp + hp, b, 0, 0))

    def whole(arr):
        return pl.BlockSpec(arr.shape, lambda b, hp: (0,) * arr.ndim)

    rows_f32 = pltpu.VMEM((seq, LANES), F32)
    rows_bf16 = pltpu.VMEM((seq, LANES), BF16)
    return _call(
        body, name="attn_fwd", grid=(n_seq, n_hp),
        in_specs=[col(0), col(1), col(2), whole(g_q2), whole(g_k2),
                  pl.BlockSpec((None, 8, LANES), lambda b, hp: (hp, 0, 0)),
                  whole(dms[0]), whole(dms[1]), whole(dms[2]), whole(same)] + [ANY_SPEC] * n_ride,
        out_specs=[pl.BlockSpec((None, seq, LANES), lambda b, hp: (b, 0, hp)),
                   pl.BlockSpec((None, seq, LANES), lambda b, hp: (b, 0, hp))] + [ANY_SPEC] * n_ride,
        out_shape=[_sds((n_seq, seq, D_ATT), BF16), _sds((n_seq, seq, D_ATT), F32)] + _exchange_shapes(ride),
        scratch_shapes=[rows_f32, rows_f32, pltpu.VMEM((2, seq, LANES), BF16), rows_bf16, rows_bf16]
        + [pltpu.VMEM((seq, 3 * LANES), F32)] + [rows_f32] * 10 + _exchange_sems(n_ride),
        compiler_params=_params(("arbitrary", "arbitrary")),
    )(proj3, proj3, proj3, g_q2, g_k2, _alibi_rows(), *dms, same, *[a for a, _ in ride])


def _attn_bwd(proj3, do3, y_att3, lse3, g_q2, g_k2, ride):
    _, n_seq, seq, _ = proj3.shape
    dms = [_distance_mats(d, seq // d) for d in DILATIONS]
    same = _same_head()
    col0 = 2 * D_CONV // LANES
    n_hp = D_ATT // LANES

    n_ride = len(ride)
    ride_scatter = [s for _, s in ride]

    def body(*refs):
        (q_ref, k_ref, v_ref, do_ref, o_ref, lse_ref, gq_ref, gk_ref, sl_ref, dm1, dm4, dm16,
         same_ref) = refs[:13]
        ride_in = refs[13:13 + n_ride]
        dq_ref, dk_ref, dv_ref, dg_ref = refs[13 + n_ride:17 + n_ride]
        ride_out = refs[17 + n_ride:17 + 2 * n_ride]
        (qf, kf, qp, dop, kp, vp, sn, sp, dqp, dkp, dvp, dqn, dkn, dvn,
         tmp) = refs[17 + 2 * n_ride:32 + 2 * n_ride]
        ride_args = (ride_scatter, ride_in, ride_out) + tuple(refs[32 + 2 * n_ride:])
        dm_refs = (dm1, dm4, dm16)
        step = pl.program_id(0) * n_hp + pl.program_id(1)

        @pl.when(step == 0)
        def _():
            _exchange_start(*ride_args)
            dg_ref[...] = jnp.zeros_like(dg_ref)

        _qk_normalise(q_ref, gq_ref, same_ref, qf, seq, HEAD_DIM ** -0.5)
        _qk_normalise(k_ref, gk_ref, same_ref, kf, seq, 1.0)

        def stats(ci, carry):
            rows = pl.ds(pl.multiple_of(ci * ATT_ROWS, ATT_ROWS), ATT_ROWS)
            first = _first_head(ATT_ROWS)
            sn[0, rows, :], sn[1, rows, :] = _per_head(lse_ref[rows, :], first)
            prod = do_ref[rows, :] * o_ref[rows, :].astype(F32)
            sn[2, rows, :], sn[3, rows, :] = _per_head(_head_sum(prod, same_ref), first)
            return carry

        lax.fori_loop(0, seq // ATT_ROWS, stats, 0)
        slopes = (sl_ref[0:1, 0:1], sl_ref[1:2, 0:1])
        half = seq // (Q_BLOCK * ATT_UNROLL)
        region = seq // ATT_UNROLL

        for pi, dil in enumerate(DILATIONS):
            seg = seq // dil
            kw = min(2 * Q_BLOCK, seg)
            _permute_rows_by_head(qp, qf, dil, seq, tmp)
            _permute_rows_by_head(dop, do_ref, dil, seq, tmp)
            _permute_rows(kp, kf, dil, seq, tmp)
            _permute_rows(vp, v_ref, dil, seq, tmp)
            if dil == 1:
                st = sn
            else:
                st = sp
                for n in range(4):
                    _permute_rows(sp.at[n], sn.at[n], dil, seq, tmp)
            def touched(sub, seg=seg):
                lo, hi = sub * region, (sub + 1) * region
                if seg < region:
                    return lo, hi
                seg0 = lo // seg * seg
                return max(lo - RADIUS, seg0), min(hi + RADIUS, seg0 + seg)

            def summed(acc, start, size, touched=touched):
                pieces = []
                for c0 in range(start, start + size, RADIUS):
                    owners = [s for s in range(ATT_UNROLL) if touched(s)[0] <= c0 and c0 + RADIUS <= touched(s)[1]]
                    if pieces and pieces[-1][2] == owners:
                        pieces[-1][1] += RADIUS
                    else:
                        pieces.append([c0, RADIUS, owners])
                vals = [sum(acc[o, pl.ds(c0, n), :] for o in owners) for c0, n, owners in pieces]
                return vals[0] if len(vals) == 1 else jnp.concatenate(vals, axis=0)

            for sub in range(ATT_UNROLL):
                lo, hi = touched(sub)
                dkp[sub, pl.ds(lo, hi - lo), :] = jnp.zeros((hi - lo, LANES), F32)
                dvp[sub, pl.ds(lo, hi - lo), :] = jnp.zeros((hi - lo, LANES), F32)

            def blk(it, carry, seg=seg, kw=kw, pi=pi, st=st):
                first = _first_head(Q_BLOCK)
                chains = [(sub, h) for sub in range(ATT_UNROLL) for h in range(2)]
                win = [_window(it + sub * half, seg) for sub in range(ATT_UNROLL)]
                qrows = [pl.ds(w[0], Q_BLOCK) for w in win]
                krows = [pl.ds(w[1], kw) for w in win]

                def over_keys(n, sub):
                    t = st[n, qrows[sub], :]
                    return t if kw == LANES else jnp.concatenate([t] * (kw // LANES), axis=1)

                s, dp = {}, {}
                for sub, h in chains:
                    s[sub, h] = lax.dot_general(qp[h, qrows[sub], :], kp[krows[sub], :], NT,
                                                preferred_element_type=F32) - slopes[h] * dm_refs[pi][win[sub][2]]
                    dp[sub, h] = lax.dot_general(dop[h, qrows[sub], :], vp[krows[sub], :], NT,
                                                 preferred_element_type=F32)
                p, ds = {}, {}
                for sub, h in chains:
                    e = jnp.exp(s[sub, h] - over_keys(h, sub))
                    ds[sub, h] = (e * (dp[sub, h] - over_keys(2 + h, sub))).astype(BF16)
                    p[sub, h] = e.astype(BF16)
                dq, dk, dv = {}, {}, {}
                for sub, h in chains:
                    dq[sub, h] = jnp.dot(ds[sub, h], kp[krows[sub], :], preferred_element_type=F32)
                    dk[sub, h] = lax.dot_general(ds[sub, h], qp[h, qrows[sub], :], TN, preferred_element_type=F32)
                    dv[sub, h] = lax.dot_general(p[sub, h], dop[h, qrows[sub], :], TN, preferred_element_type=F32)
                for sub in range(ATT_UNROLL):
                    dqp[qrows[sub], :] = jnp.where(first, dq[sub, 0], dq[sub, 1])
                    dkp[sub, krows[sub], :] += dk[sub, 0] + dk[sub, 1]
                    dvp[sub, krows[sub], :] += dv[sub, 0] + dv[sub, 1]
                return carry

            lax.fori_loop(0, half, blk, 0)
            first_pattern = pi == 0
            if first_pattern:
                for r0 in range(0, seq, region):
                    rows = pl.ds(r0, region)
                    dqn[rows, :] = dqp[rows, :]
                    dkn[rows, :] = summed(dkp, r0, region)
                    dvn[rows, :] = summed(dvp, r0, region)
            else:
                _scatter_segments(dqn, lambda start, size: dqp[pl.ds(start, size), :], dil, seq, tmp, accumulate=True)
                for nat, acc in ((dkn, dkp), (dvn, dvp)):
                    _scatter_segments(nat, lambda start, size, acc=acc: summed(acc, start, size),
                                      dil, seq, tmp, accumulate=True)

        def finish(ci, carry):
            rows = pl.ds(pl.multiple_of(ci * ATT_ROWS, ATT_ROWS), ATT_ROWS)
            for src_ref, g_ref, dn, dst_ref, scale, row in (
                    (q_ref, gq_ref, dqn, dq_ref, HEAD_DIM ** -0.5, 0), (k_ref, gk_ref, dkn, dk_ref, 1.0, 1)):
                xv = src_ref[rows, :]
                r = lax.rsqrt(_head_mean(xv * xv, same_ref) + EPS)
                xhat = xv * r
                d = dn[rows, :] * scale
                dg_ref[row:row + 1, :] += jnp.sum(d * xhat, axis=0, keepdims=True)
                dxh = d * g_ref[...]
                dst_ref[rows, :] = (r * (dxh - xhat * _head_mean(dxh * xhat, same_ref))).astype(BF16)
            dv_ref[rows, :] = dvn[rows, :].astype(BF16)
            return carry

        lax.fori_loop(0, seq // ATT_ROWS, finish, 0)

        @pl.when(step == n_seq * n_hp - 1)
        def _():
            _exchange_wait(*ride_args)

    def col(off):
        return pl.BlockSpec((None, None, seq, LANES), lambda b, hp: (col0 + off * n_hp + hp, b, 0, 0))

    def whole(arr):
        return pl.BlockSpec(arr.shape, lambda b, hp: (0,) * arr.ndim)

    att = pl.BlockSpec((None, seq, LANES), lambda b, hp: (b, 0, hp))
    rows_f32 = pltpu.VMEM((seq, LANES), F32)
    rows_bf16 = pltpu.VMEM((seq, LANES), BF16)
    by_head_bf16 = pltpu.VMEM((2, seq, LANES), BF16)
    per_sub_f32 = pltpu.VMEM((ATT_UNROLL, seq, LANES), F32)
    stats_f32 = pltpu.VMEM((4, seq, LANES), F32)
    return _call(
        body, name="attn_bwd", grid=(n_seq, n_hp),
        in_specs=[col(0), col(1), col(2), att, att, att, whole(g_q2), whole(g_k2),
                  pl.BlockSpec((None, 8, LANES), lambda b, hp: (hp, 0, 0)),
                  whole(dms[0]), whole(dms[1]), whole(dms[2]), whole(same)] + [ANY_SPEC] * n_ride,
        out_specs=[att, att, att, pl.BlockSpec((8, LANES), lambda b, hp: (0, 0))] + [ANY_SPEC] * n_ride,
        out_shape=[_sds((n_seq, seq, D_ATT), BF16)] * 3 + [_sds((8, LANES), F32)] + _exchange_shapes(ride),
        scratch_shapes=[rows_f32, rows_f32, by_head_bf16, by_head_bf16, rows_bf16, rows_bf16, stats_f32, stats_f32,
                        rows_f32, per_sub_f32, per_sub_f32, rows_f32, rows_f32, rows_f32, rows_f32]
        + _exchange_sems(n_ride),
        compiler_params=_params(("arbitrary", "arbitrary")),
    )(proj3, proj3, proj3, do3, y_att3, lse3, g_q2, g_k2, _alibi_rows(), *dms, same, *[a for a, _ in ride])


def _mix_out(uc2, y_att2, x2, mod8, g_ln, b_ln, g_ffn, w_out, seq, tm=512):
    tokens = x2.shape[0]
    per_seq = seq // tm

    def body(uc_ref, ya_ref, x_ref, m_ref, gl_ref, bl_ref, gf_ref, w_ref, yc_ref, mix_ref, x1_ref, h2_ref):
        uc = uc_ref[...]
        mu = jnp.mean(uc, axis=-1, keepdims=True)
        cen = uc - mu
        rs = lax.rsqrt(jnp.mean(cen * cen, axis=-1, keepdims=True) + EPS)
        z = cen * rs * gl_ref[...] + bl_ref[...]
        yc = (z * _sig(z)).astype(BF16)
        yc_ref[...] = yc
        mix = (jnp.dot(yc, w_ref[pl.ds(0, D_CONV), :], preferred_element_type=F32)
               + jnp.dot(ya_ref[...], w_ref[pl.ds(D_CONV, D_ATT), :], preferred_element_type=F32))
        mix_ref[...] = mix
        x1 = x_ref[...] + m_ref[2:3, :] * mix
        x1_ref[...] = x1
        r = lax.rsqrt(jnp.mean(x1 * x1, axis=-1, keepdims=True) + EPS)
        h2_ref[...] = ((x1 * r * gf_ref[...]) * (1.0 + m_ref[4:5, :]) + m_ref[3:4, :]).astype(BF16)

    def rows(width):
        return pl.BlockSpec((tm, width), lambda i: (i, 0))

    def vec(width):
        return pl.BlockSpec((1, width), lambda i: (0, 0))

    return _call(
        body, name="mix_out", grid=(tokens // tm,),
        in_specs=[rows(D_CONV), rows(D_ATT), rows(D_MODEL),
                  pl.BlockSpec((None, 8, D_MODEL), lambda i: (i // per_seq, 0, 0)),
                  vec(D_CONV), vec(D_CONV), vec(D_MODEL),
                  pl.BlockSpec((D_MODEL, D_MODEL), lambda i: (0, 0))],
        out_specs=[rows(D_CONV), rows(D_MODEL), rows(D_MODEL), rows(D_MODEL)],
        out_shape=[_sds((tokens, D_CONV), BF16), _sds((tokens, D_MODEL), F32),
                   _sds((tokens, D_MODEL), F32), _sds((tokens, D_MODEL), BF16)],
        compiler_params=_params(("parallel",)),
    )(uc2, y_att2, x2, mod8, g_ln, b_ln, g_ffn, w_out)


def _mix_out_bwd(dmix, uc2, g_ln, b_ln, w_out, tm=512):
    tokens = dmix.shape[0]

    def body(dm_ref, uc_ref, gl_ref, bl_ref, w_ref, duc_ref, do_ref, dgb_ref):
        @pl.when(pl.program_id(0) == 0)
        def _():
            dgb_ref[...] = jnp.zeros_like(dgb_ref)

        dmv = dm_ref[...]
        dyc = lax.dot_general(dmv, w_ref[pl.ds(0, D_CONV), :], NT, preferred_element_type=F32)
        do_ref[...] = lax.dot_general(dmv, w_ref[pl.ds(D_CONV, D_ATT), :], NT, preferred_element_type=F32)
        uc = uc_ref[...]
        mu = jnp.mean(uc, axis=-1, keepdims=True)
        cen = uc - mu
        rs = lax.rsqrt(jnp.mean(cen * cen, axis=-1, keepdims=True) + EPS)
        xh = cen * rs
        z = xh * gl_ref[...] + bl_ref[...]
        sg = _sig(z)
        dz = dyc * (sg * (1.0 + z * (1.0 - sg)))
        dgb_ref[0:1, :] += jnp.sum(dz * xh, axis=0, keepdims=True)
        dgb_ref[1:2, :] += jnp.sum(dz, axis=0, keepdims=True)
        dxh = dz * gl_ref[...]
        duc_ref[...] = rs * (dxh - jnp.mean(dxh, axis=-1, keepdims=True)
                             - xh * jnp.mean(dxh * xh, axis=-1, keepdims=True))

    return _call(
        body, name="mix_out_bwd", grid=(tokens // tm,),
        in_specs=[pl.BlockSpec((tm, D_MODEL), lambda i: (i, 0)),
                  pl.BlockSpec((tm, D_CONV), lambda i: (i, 0)),
                  pl.BlockSpec((1, D_CONV), lambda i: (0, 0)),
                  pl.BlockSpec((1, D_CONV), lambda i: (0, 0)),
                  pl.BlockSpec((D_MODEL, D_MODEL), lambda i: (0, 0))],
        out_specs=[pl.BlockSpec((tm, D_CONV), lambda i: (i, 0)),
                   pl.BlockSpec((tm, D_ATT), lambda i: (i, 0)),
                   pl.BlockSpec((8, D_CONV), lambda i: (0, 0))],
        out_shape=[_sds((tokens, D_CONV), F32), _sds((tokens, D_ATT), F32), _sds((8, D_CONV), F32)],
        compiler_params=_params(("arbitrary",)),
    )(dmix, uc2, g_ln, b_ln, w_out)


FF_TILE = 256
FF_TILES = D_FF // FF_TILE
FF_ROWS = 256


def _chunk_stream(hbm_refs, bufs, sems, row0, to_hbm):
    def copies(r):
        rows = pl.ds(pl.multiple_of(row0 + r * FF_ROWS, FF_ROWS), FF_ROWS)
        out = []
        for n, (hbm, buf) in enumerate(zip(hbm_refs, bufs)):
            ends = (buf.at[r % 2], hbm.at[rows, :]) if to_hbm else (hbm.at[rows, :], buf.at[r % 2])
            out.append(pltpu.make_async_copy(*ends, sems.at[(r % 2) * len(hbm_refs) + n]))
        return out

    return copies


def _ffn_fwd(h2, w_gate, w_up, w_down, x1, target, mod8, seq, tm=2048):
    tokens = h2.shape[0]
    per_seq = seq // tm
    n_seq = tokens // seq
    last = D_FF // FF_TILE - 1
    n_chunks = tm // FF_ROWS

    def body(h_ref, wg_ref, wu_ref, wd_ref, m_ref, x1_hbm, t_hbm, gate_ref, up_ref, dy_hbm, df_hbm, sq_ref, dgf_ref,
             f_ref, x1_buf, t_buf, dy_buf, df_buf, in_sems, out_sems):
        i, j = pl.program_id(0), pl.program_id(1)
        loads = _chunk_stream((x1_hbm, t_hbm), (x1_buf, t_buf), in_sems, i * tm, to_hbm=False)
        stores = _chunk_stream((dy_hbm, df_hbm), (dy_buf, df_buf), out_sems, i * tm, to_hbm=True)

        @pl.when(j == 0)
        def _():
            f_ref[...] = jnp.zeros_like(f_ref)

        @pl.when((j == 0) & (i == 0))
        def _():
            sq_ref[...] = jnp.zeros_like(sq_ref)

        @pl.when((j == 0) & (i % per_seq == 0))
        def _():
            dgf_ref[...] = jnp.zeros_like(dgf_ref)

        def gate_up(r):
            hv = h_ref[pl.ds(r * FF_ROWS, FF_ROWS), :]
            return (lax.dot_general(hv, wg_ref[...], NT, preferred_element_type=F32),
                    lax.dot_general(hv, wu_ref[...], NT, preferred_element_type=F32))

        def loss_head(r, fv):
            slot = r % 2
            gate_f = m_ref[5:6, :]
            diff = x1_buf[slot] + gate_f * fv - t_buf[slot]
            sq_ref[0:1, :] += jnp.sum(diff * diff, axis=0, keepdims=True)
            dy = diff * (1.0 / D_MODEL)
            dy_buf[slot] = dy
            df_buf[slot] = (gate_f * dy).astype(BF16)
            dgf_ref[0:1, :] += jnp.sum(dy * fv, axis=0, keepdims=True)

        def tile(finish):
            if finish:
                for r in range(2):
                    for cp in loads(r):
                        cp.start()
            ahead = gate_up(0)
            for r in range(n_chunks):
                gate, up = ahead
                if r + 1 < n_chunks:
                    ahead = gate_up(r + 1)
                rows = pl.ds(r * FF_ROWS, FF_ROWS)
                gate_ref[rows, :] = gate.astype(BF16)
                up_ref[rows, :] = up.astype(BF16)
                act = (gate * _sig(gate) * up).astype(BF16)
                fv = f_ref[rows, :] + jnp.dot(act, wd_ref[...], preferred_element_type=F32)
                if not finish:
                    f_ref[rows, :] = fv
                    continue
                for cp in loads(r):
                    cp.wait()
                if r >= 2:
                    for cp in stores(r - 2):
                        cp.wait()
                loss_head(r, fv)
                for cp in stores(r):
                    cp.start()
                if r + 2 < n_chunks:
                    for cp in loads(r + 2):
                        cp.start()
            if finish:
                for r in (n_chunks - 2, n_chunks - 1):
                    for cp in stores(r):
                        cp.wait()

        @pl.when(j < last)
        def _():
            tile(False)

        @pl.when(j == last)
        def _():
            tile(True)

    rows_spec = pl.BlockSpec((tm, D_MODEL), lambda i, j: (i, 0))
    per = pl.BlockSpec((None, 8, D_MODEL), lambda i, j: (i // per_seq, 0, 0))
    tile = pl.BlockSpec((None, tm, FF_TILE), lambda i, j: (j, i, 0))
    w_tile = pl.BlockSpec((FF_TILE, D_MODEL), lambda i, j: (j, 0))
    chunk_f32 = pltpu.VMEM((2, FF_ROWS, D_MODEL), F32)
    return _call(
        body, name="ffn_fwd", grid=(tokens // tm, D_FF // FF_TILE),
        in_specs=[rows_spec, w_tile, w_tile,
                  pl.BlockSpec((FF_TILE, D_MODEL), lambda i, j: (j, 0)),
                  per, ANY_SPEC, ANY_SPEC],
        out_specs=[tile, tile, ANY_SPEC, ANY_SPEC, pl.BlockSpec((8, D_MODEL), lambda i, j: (0, 0)), per],
        out_shape=[_sds((FF_TILES, tokens, FF_TILE), BF16), _sds((FF_TILES, tokens, FF_TILE), BF16),
                   _sds((tokens, D_MODEL), F32),
                   _sds((tokens, D_MODEL), BF16), _sds((8, D_MODEL), F32), _sds((n_seq, 8, D_MODEL), F32)],
        scratch_shapes=[pltpu.VMEM((tm, D_MODEL), F32), chunk_f32, chunk_f32, chunk_f32,
                        pltpu.VMEM((2, FF_ROWS, D_MODEL), BF16),
                        pltpu.SemaphoreType.DMA((4,)), pltpu.SemaphoreType.DMA((4,))],
        compiler_params=_params(("arbitrary", "arbitrary")),
    )(h2, w_gate, w_up, w_down, mod8, x1, target)


def _ffn_bwd(df, gate, up, w_gate, w_up, w_down, x1, dy, mix, mod8, g_ffn, seq, tm=2048):
    tokens = df.shape[0]
    per_seq = seq // tm
    n_seq = tokens // seq
    last = D_FF // FF_TILE - 1
    n_chunks = tm // FF_ROWS

    def body(df_ref, gate_ref, up_ref, wg_ref, wu_ref, wd_ref, m_ref, g_ref, x1_hbm, dy_hbm, mix_hbm,
             dgate_ref, dup_ref, act_ref, dx1_hbm, dmix_hbm, dg_ref, dm_ref,
             dh_ref, x1_buf, dy_buf, mix_buf, dx1_buf, dmix_buf, in_sems, out_sems):
        i, j = pl.program_id(0), pl.program_id(1)
        loads = _chunk_stream((x1_hbm, dy_hbm, mix_hbm), (x1_buf, dy_buf, mix_buf), in_sems, i * tm, to_hbm=False)
        stores = _chunk_stream((dx1_hbm, dmix_hbm), (dx1_buf, dmix_buf), out_sems, i * tm, to_hbm=True)

        @pl.when(j == 0)
        def _():
            dh_ref[...] = jnp.zeros_like(dh_ref)

        @pl.when((j == 0) & (i == 0))
        def _():
            dg_ref[...] = jnp.zeros_like(dg_ref)

        @pl.when((j == 0) & (i % per_seq == 0))
        def _():
            dm_ref[...] = jnp.zeros_like(dm_ref)

        def d_act(r):
            return lax.dot_general(df_ref[pl.ds(r * FF_ROWS, FF_ROWS), :], wd_ref[...], NT,
                                   preferred_element_type=F32)

        def norm_bwd(r, dh):
            slot = r % 2
            g = g_ref[...]
            x1v = x1_buf[slot]
            rs = lax.rsqrt(jnp.mean(x1v * x1v, axis=-1, keepdims=True) + EPS)
            xhat = x1v * rs
            dm_ref[0:1, :] += jnp.sum(dh, axis=0, keepdims=True)
            dm_ref[1:2, :] += jnp.sum(dh * (xhat * g), axis=0, keepdims=True)
            dn = dh * (1.0 + m_ref[4:5, :])
            dg_ref[0:1, :] += jnp.sum(dn * xhat, axis=0, keepdims=True)
            dxh = dn * g
            dx1 = dy_buf[slot] + rs * (dxh - xhat * jnp.mean(dxh * xhat, axis=-1, keepdims=True))
            dx1_buf[slot] = dx1
            dm_ref[2:3, :] += jnp.sum(dx1 * mix_buf[slot], axis=0, keepdims=True)
            dmix_buf[slot] = (m_ref[2:3, :] * dx1).astype(BF16)

        def tile(finish):
            if finish:
                for r in range(2):
                    for cp in loads(r):
                        cp.start()
            ahead = d_act(0)
            for r in range(n_chunks):
                dact = ahead
                if r + 1 < n_chunks:
                    ahead = d_act(r + 1)
                rows = pl.ds(r * FF_ROWS, FF_ROWS)
                gate = gate_ref[rows, :].astype(F32)
                up = up_ref[rows, :].astype(F32)
                sg = _sig(gate)
                silu = gate * sg
                act_ref[rows, :] = (silu * up).astype(BF16)
                dup = (dact * silu).astype(BF16)
                dgate = (dact * up * (sg * (1.0 + gate * (1.0 - sg)))).astype(BF16)
                dup_ref[rows, :] = dup
                dgate_ref[rows, :] = dgate
                dh = dh_ref[rows, :] + (jnp.dot(dgate, wg_ref[...], preferred_element_type=F32)
                                        + jnp.dot(dup, wu_ref[...], preferred_element_type=F32))
                if not finish:
                    dh_ref[rows, :] = dh
                    continue
                for cp in loads(r):
                    cp.wait()
                if r >= 2:
                    for cp in stores(r - 2):
                        cp.wait()
                norm_bwd(r, dh)
                for cp in stores(r):
                    cp.start()
                if r + 2 < n_chunks:
                    for cp in loads(r + 2):
                        cp.start()
            if finish:
                for r in (n_chunks - 2, n_chunks - 1):
                    for cp in stores(r):
                        cp.wait()

        @pl.when(j < last)
        def _():
            tile(False)

        @pl.when(j == last)
        def _():
            tile(True)

    tile = pl.BlockSpec((None, tm, FF_TILE), lambda i, j: (j, i, 0))
    w_tile = pl.BlockSpec((FF_TILE, D_MODEL), lambda i, j: (j, 0))
    rows_spec = pl.BlockSpec((tm, D_MODEL), lambda i, j: (i, 0))
    per = pl.BlockSpec((None, 8, D_MODEL), lambda i, j: (i // per_seq, 0, 0))
    chunk_f32 = pltpu.VMEM((2, FF_ROWS, D_MODEL), F32)
    return _call(
        body, name="ffn_bwd", grid=(tokens // tm, D_FF // FF_TILE),
        in_specs=[rows_spec, tile, tile, w_tile, w_tile,
                  pl.BlockSpec((FF_TILE, D_MODEL), lambda i, j: (j, 0)),
                  per, pl.BlockSpec((1, D_MODEL), lambda i, j: (0, 0)), ANY_SPEC, ANY_SPEC, ANY_SPEC],
        out_specs=[tile, tile, tile, ANY_SPEC, ANY_SPEC, pl.BlockSpec((8, D_MODEL), lambda i, j: (0, 0)), per],
        out_shape=[_sds((FF_TILES, tokens, FF_TILE), BF16)] * 3
        + [_sds((tokens, D_MODEL), F32), _sds((tokens, D_MODEL), BF16),
           _sds((8, D_MODEL), F32), _sds((n_seq, 8, D_MODEL), F32)],
        scratch_shapes=[pltpu.VMEM((tm, D_MODEL), F32), chunk_f32, chunk_f32, chunk_f32, chunk_f32,
                        pltpu.VMEM((2, FF_ROWS, D_MODEL), BF16),
                        pltpu.SemaphoreType.DMA((6,)), pltpu.SemaphoreType.DMA((4,))],
        compiler_params=_params(("arbitrary", "arbitrary")),
    )(df, gate, up, w_gate, w_up, w_down, mod8, g_ffn, x1, dy, mix)


def _mix_in_bwd(d_a, d_g, d_q, d_k, d_v, w_in, x2, dx1, mod8, g_mix, seq, ride, tm=512):
    tokens = x2.shape[0]
    per_seq = seq // tm
    n_seq = tokens // seq
    parts = (d_a, d_g, d_q, d_k, d_v)
    width = D_CONV
    n_ride = len(ride)
    ride_scatter = [s for _, s in ride]

    def body(*refs):
        da_ref, dg_ref, dq_ref, dk_ref, dv_ref, w_ref, x_ref, dx1_ref, m_ref, g_ref = refs[:10]
        ride_in = refs[10:10 + n_ride]
        gx_ref, dgm_ref, dm_ref = refs[10 + n_ride:13 + n_ride]
        ride_args = (ride_scatter, ride_in, refs[13 + n_ride:13 + 2 * n_ride]) + tuple(refs[13 + 2 * n_ride:])
        i = pl.program_id(0)

        @pl.when(i == 0)
        def _():
            _exchange_start(*ride_args)
            dgm_ref[...] = jnp.zeros_like(dgm_ref)

        @pl.when(i % per_seq == 0)
        def _():
            dm_ref[...] = jnp.zeros_like(dm_ref)

        dh = jnp.zeros((tm, D_MODEL), F32)
        for n, ref in enumerate((da_ref, dg_ref, dq_ref, dk_ref, dv_ref)):
            dh = dh + jnp.dot(ref[...], w_ref[pl.ds(n * width, width), :], preferred_element_type=F32)
        xv = x_ref[...]
        r = lax.rsqrt(jnp.mean(xv * xv, axis=-1, keepdims=True) + EPS)
        xhat = xv * r
        g = g_ref[...]
        dm_ref[0:1, :] += jnp.sum(dh, axis=0, keepdims=True)
        dm_ref[1:2, :] += jnp.sum(dh * (xhat * g), axis=0, keepdims=True)
        dn = dh * (1.0 + m_ref[1:2, :])
        dgm_ref[0:1, :] += jnp.sum(dn * xhat, axis=0, keepdims=True)
        dxh = dn * g
        gx_ref[...] = dx1_ref[...] + r * (dxh - xhat * jnp.mean(dxh * xhat, axis=-1, keepdims=True))

        @pl.when(i == tokens // tm - 1)
        def _():
            _exchange_wait(*ride_args)

    rows = pl.BlockSpec((tm, D_MODEL), lambda i: (i, 0))
    half = pl.BlockSpec((tm, width), lambda i: (i, 0))
    per = pl.BlockSpec((None, 8, D_MODEL), lambda i: (i // per_seq, 0, 0))
    return _call(
        body, name="mix_in_bwd", grid=(tokens // tm,),
        in_specs=[half] * 5 + [pl.BlockSpec((D_IN, D_MODEL), lambda i: (0, 0)), rows, rows, per,
                               pl.BlockSpec((1, D_MODEL), lambda i: (0, 0))] + [ANY_SPEC] * n_ride,
        out_specs=[rows, pl.BlockSpec((8, D_MODEL), lambda i: (0, 0)), per] + [ANY_SPEC] * n_ride,
        out_shape=[_sds((tokens, D_MODEL), F32), _sds((8, D_MODEL), F32), _sds((n_seq, 8, D_MODEL), F32)]
        + _exchange_shapes(ride),
        scratch_shapes=_exchange_sems(n_ride),
        compiler_params=_params(("arbitrary",)),
    )(*parts, w_in, x2, dx1, mod8, g_mix, *[a for a, _ in ride])


def _grad_matmul_parts(a_parts, b_parts, name, tk=1024):
    tokens = a_parts[0].shape[0]
    na, nb = len(a_parts), len(b_parts)
    ma, nbw = a_parts[0].shape[1], b_parts[0].shape[1]

    n_k = tokens // tk

    def body(*refs):
        a_refs, b_refs, o_ref, acc = refs[:na], refs[na:na + nb], refs[na + nb], refs[na + nb + 1]

        @pl.when(pl.program_id(0) == 0)
        def _():
            acc[...] = jnp.zeros_like(acc)

        for i in range(na):
            for j in range(nb):
                acc[pl.ds(i * ma, ma), pl.ds(j * nbw, nbw)] += lax.dot_general(
                    a_refs[i][...], b_refs[j][...], TN, preferred_element_type=F32)

        @pl.when(pl.program_id(0) == n_k - 1)
        def _():
            o_ref[...] = acc[...].astype(o_ref.dtype)

    return _call(
        body, name=name, grid=(n_k,),
        in_specs=[pl.BlockSpec((tk, ma), lambda k: (k, 0))] * na + [pl.BlockSpec((tk, nbw), lambda k: (k, 0))] * nb,
        out_specs=pl.BlockSpec((na * ma, nb * nbw), lambda k: (0, 0)),
        out_shape=_sds((na * ma, nb * nbw), BF16),
        scratch_shapes=[pltpu.VMEM((na * ma, nb * nbw), F32)],
        compiler_params=_params(("arbitrary",)),
    )(*a_parts, *b_parts)


def _grad_matmul_tiles(a, b, name, tk=1024):
    tiled_b = b.ndim == 3
    tiles, tokens, width = b.shape if tiled_b else a.shape
    other = a.shape[1] if tiled_b else b.shape[1]
    out_tile = (other, width) if tiled_b else (width, other)
    n_k = tokens // tk

    def body(a_ref, b_ref, o_ref, acc):
        @pl.when(pl.program_id(0) == 0)
        def _():
            acc[...] = jnp.zeros_like(acc)

        for t in range(tiles):
            lhs = a_ref[...] if tiled_b else a_ref[t]
            rhs = b_ref[t] if tiled_b else b_ref[...]
            acc[t] += lax.dot_general(lhs, rhs, TN, preferred_element_type=F32)

        @pl.when(pl.program_id(0) == n_k - 1)
        def _():
            o_ref[...] = acc[...].astype(o_ref.dtype)

    flat = pl.BlockSpec((tk, other), lambda k: (k, 0))
    tiled = pl.BlockSpec((tiles, tk, width), lambda k: (0, k, 0))
    return _call(
        body, name=name, grid=(n_k,),
        in_specs=[flat, tiled] if tiled_b else [tiled, flat],
        out_specs=pl.BlockSpec((tiles,) + out_tile, lambda k: (0, 0, 0)),
        out_shape=_sds((tiles,) + out_tile, BF16),
        scratch_shapes=[pltpu.VMEM((tiles,) + out_tile, F32)],
        compiler_params=_params(("arbitrary",)),
    )(a, b)


def _adamw(w, m, v, g, name, n_parts=0, tr=256):
    rows, cols = w.shape
    tr = min(tr, rows)
    c1 = 1.0 - ADAM_B1 ** ADAM_STEP
    c2 = 1.0 - ADAM_B2 ** ADAM_STEP

    def body(w_ref, m_ref, v_ref, g_ref, go_ref, d_ref, mo_ref, vo_ref):
        if n_parts:
            gv = g_ref[0].astype(F32)
            for p in range(1, n_parts):
                gv = gv + g_ref[p].astype(F32)
        else:
            gv = g_ref[...]
        go_ref[...] = gv
        mn = ADAM_B1 * m_ref[...] + (1.0 - ADAM_B1) * gv
        vn = ADAM_B2 * v_ref[...] + (1.0 - ADAM_B2) * (gv * gv)
        mo_ref[...] = mn
        vo_ref[...] = vn
        d_ref[...] = -ADAM_LR * ((mn / c1) / (jnp.sqrt(vn / c2) + ADAM_EPS) + ADAM_WD * w_ref[...])

    blk = pl.BlockSpec((tr, cols), lambda i: (i, 0))
    g_spec = pl.BlockSpec((n_parts, tr, cols), lambda i: (0, i, 0)) if n_parts else blk
    return _call(
        body, name=name, grid=(rows // tr,),
        in_specs=[blk, blk, blk, g_spec], out_specs=[blk] * 4,
        out_shape=[_sds((rows, cols), F32)] * 4,
        compiler_params=_params(("parallel",)),
    )(w, m, v, g)


def _cols_to_full(blocks):
    n, r, c = blocks.shape
    return jnp.transpose(blocks, (1, 0, 2)).reshape(r, n * c)


def _pad_lanes(v, width):
    return jnp.pad(v, ((0, 0), (0, width - v.shape[1])))


def kernel(x, c, w_ada, b_ada, g_mix, w_in, w_dw, b_dw, g_conv_ln, b_conv_ln, g_q, g_k, w_out, g_ffn, w_gate, w_up, w_down, loss_target, m_w_ada, m_b_ada, m_g_mix, m_w_in, m_w_dw, m_b_dw, m_g_conv_ln, m_b_conv_ln, m_g_q, m_g_k, m_w_out, m_g_ffn, m_w_gate, m_w_up, m_w_down, v_w_ada, v_b_ada, v_g_mix, v_w_in, v_w_dw, v_b_dw, v_g_conv_ln, v_b_conv_ln, v_g_q, v_g_k, v_w_out, v_g_ffn, v_w_gate, v_w_up, v_w_down):
    n_seq, seq, _ = x.shape
    tokens = n_seq * seq
    me = 4 * lax.axis_index("x") + 2 * lax.axis_index("y") + lax.axis_index("c")
    ada_cols = w_ada.shape[2]
    dw_cols = w_dw.shape[2]

    def transposed(w):
        return jnp.transpose(w[0])

    (c_g, w_in_g, w_dw_g) = _gather_by_chip([c, transposed(w_in).astype(BF16), w_dw[0]], "gather_weights")
    c_all = c_g.reshape(N_DEV * n_seq, D_MODEL)
    w_in_t = w_in_g.reshape(D_IN, D_MODEL)
    w_dw_f = _cols_to_full(w_dw_g)

    b_cols = lax.dynamic_slice(b_ada, (0, me * ada_cols), (1, ada_cols))
    mod_cols = _ada_fwd(c_all, w_ada[0], b_cols)
    (mod_g,) = _exchange([(mod_cols, False)], "gather_mod")
    mod_mine = lax.dynamic_slice(mod_g, (0, me * n_seq, 0), (N_DEV, n_seq, ada_cols))
    mod = jnp.transpose(mod_mine, (1, 0, 2)).reshape(n_seq, N_MOD, D_MODEL)
    mod8 = jnp.pad(mod, ((0, 0), (0, 8 - N_MOD), (0, 0)))

    x2 = x.reshape(tokens, D_MODEL)
    h1, proj = _mix_in(x2, mod8, g_mix, w_in_t, seq)
    proj3 = proj.reshape(D_IN // LANES, n_seq, seq, LANES)
    uc3 = _conv_fwd(proj3, w_dw_f, b_dw)
    g_q2, g_k2 = jnp.tile(g_q, (1, 2)), jnp.tile(g_k, (1, 2))
    y_att3, lse3, w_out_g, w_gate_g, w_up_g, w_down_g = _attn_fwd(
        proj3, g_q2, g_k2,
        [(w_out[0].astype(BF16), False), (transposed(w_gate).astype(BF16), False),
         (transposed(w_up).astype(BF16), False), (w_down[0].astype(BF16), False)])
    w_out_f = w_out_g.reshape(D_MODEL, D_MODEL)
    w_gate_f = w_gate_g.reshape(D_FF, D_MODEL)
    w_up_f = w_up_g.reshape(D_FF, D_MODEL)
    w_down_f = w_down_g.reshape(D_FF, D_MODEL)
    uc2 = uc3.reshape(tokens, D_CONV)
    y_att2 = y_att3.reshape(tokens, D_ATT)
    y_conv, mix, x1, h2 = _mix_out(uc2, y_att2, x2, mod8, g_conv_ln, b_conv_ln, g_ffn, w_out_f, seq)
    gate, up, dy, df, sq, dgate_f = _ffn_fwd(
        h2, w_gate_f, w_up_f, w_down_f, x1, loss_target.reshape(tokens, D_MODEL), mod8, seq)

    dgate, dup, act, dx1, dmix, dg_ffn, dmod_f = _ffn_bwd(
        df, gate, up, w_gate_f, w_up_f, w_down_f, x1, dy, mix, mod8, g_ffn, seq)
    duc2, do2, dgb_ln = _mix_out_bwd(dmix, uc2, g_conv_ln, b_conv_ln, w_out_f)
    d_a3, d_g3, dw_dw_p, db_dw_p = _conv_bwd(duc2.reshape(n_seq, seq, D_CONV), proj3, w_dw_f)
    gw_gate = _grad_matmul_tiles(dgate, h2, "grad_w_gate")
    gw_up = _grad_matmul_tiles(dup, h2, "grad_w_up")
    gw_down = _grad_matmul_tiles(act, df, "grad_w_down")
    gw_out = _grad_matmul_parts([y_conv, y_att2], [dmix], "grad_w_out")
    d_q3, d_k3, d_v3, dg_qk, p_gate, p_up, p_down, p_out = _attn_bwd(
        proj3, do2.reshape(n_seq, seq, D_ATT), y_att3, lse3, g_q2, g_k2,
        [(gw_gate.reshape(N_DEV, D_FF // N_DEV, D_MODEL), True), (gw_up.reshape(N_DEV, D_FF // N_DEV, D_MODEL), True),
         (gw_down.reshape(N_DEV, D_FF // N_DEV, D_MODEL), True),
         (gw_out.reshape(N_DEV, D_MODEL // N_DEV, D_MODEL), True)])
    flat = lambda t: t.reshape(tokens, t.shape[-1])
    d_a, d_g, d_q, d_k, d_v = flat(d_a3), flat(d_g3), flat(d_q3), flat(d_k3), flat(d_v3)
    gw_in = _grad_matmul_parts([d_a, d_g, d_q, d_k, d_v], [h1], "grad_w_in")
    grad_x2, dg_mix, dmod_m, p_in = _mix_in_bwd(
        d_a, d_g, d_q, d_k, d_v, w_in_t, x2, dx1, mod8, g_mix, seq,
        [(gw_in.reshape(N_DEV, D_IN // N_DEV, D_MODEL), True)])

    dmod = jnp.concatenate([dmod_m[:, 0], dmod_m[:, 1], dmod_f[:, 2], dmod_f[:, 0], dmod_f[:, 1], dgate_f[:, 0]], axis=1)
    dg_q = dg_qk[0:1, 0:HEAD_DIM] + dg_qk[0:1, HEAD_DIM:]
    dg_k = dg_qk[1:2, 0:HEAD_DIM] + dg_qk[1:2, HEAD_DIM:]
    loss_part = (0.5 / D_MODEL) * jnp.sum(sq[0:1, :], axis=1, keepdims=True)
    small = jnp.concatenate(
        [dg_mix[0:1], dg_ffn[0:1], db_dw_p[0:1], dgb_ln[0:1], dgb_ln[1:2],
         _pad_lanes(dg_q, LANES), _pad_lanes(dg_k, LANES), _pad_lanes(loss_part, LANES)], axis=1)
    n_small = small.shape[1] - LANES

    (dmod_g, small_g, dw_g) = _exchange([(dmod, False), (small, False), (dw_dw_p, False)], "gather_small_grads")

    dmod_all = dmod_g.reshape(N_DEV * n_seq, N_MOD * D_MODEL)
    dmod_cols = lax.dynamic_slice(dmod_all, (0, me * ada_cols), (N_DEV * n_seq, ada_cols))
    gw_ada, gb_ada = _ada_bwd(c_all, dmod_cols, dmod_all)

    res = {}
    res["w_ada"] = _adamw(w_ada[0], m_w_ada[0], v_w_ada[0], gw_ada, "adamw_w_ada")
    res["b_ada"] = _adamw(b_ada, m_b_ada, v_b_ada, gb_ada, "adamw_b_ada")
    def adamw_transposed(w, m, v, parts, name, tr):
        outs = _adamw(transposed(w), transposed(m), transposed(v), parts, name, N_DEV, tr=tr)
        return tuple(jnp.transpose(o) for o in outs)

    res["w_in"] = adamw_transposed(w_in, m_w_in, v_w_in, p_in, "adamw_w_in", 160)
    res["w_out"] = _adamw(w_out[0], m_w_out[0], v_w_out[0], p_out, "adamw_w_out", N_DEV)
    res["w_gate"] = adamw_transposed(w_gate, m_w_gate, v_w_gate, p_gate, "adamw_w_gate", 176)
    res["w_up"] = adamw_transposed(w_up, m_w_up, v_w_up, p_up, "adamw_w_up", 176)
    res["w_down"] = _adamw(w_down[0], m_w_down[0], v_w_down[0], p_down, "adamw_w_down", N_DEV, tr=176)
    dw_mine = lax.dynamic_slice(dw_g, (0, 0, me * dw_cols), (N_DEV, CONV_WIDTH, dw_cols))
    res["w_dw"] = _adamw(w_dw[0], m_w_dw[0], v_w_dw[0], dw_mine, "adamw_w_dw", N_DEV)

    small_names = ["g_mix", "g_ffn", "b_dw", "g_conv_ln", "b_conv_ln", "g_q", "g_k"]
    small_w = {"g_mix": (g_mix, m_g_mix, v_g_mix), "g_ffn": (g_ffn, m_g_ffn, v_g_ffn), "b_dw": (b_dw, m_b_dw, v_b_dw),
               "g_conv_ln": (g_conv_ln, m_g_conv_ln, v_g_conv_ln), "b_conv_ln": (b_conv_ln, m_b_conv_ln, v_b_conv_ln),
               "g_q": (g_q, m_g_q, v_g_q), "g_k": (g_k, m_g_k, v_g_k)}
    widths = [max(small_w[n][0].shape[1], LANES) for n in small_names]
    packed = [jnp.concatenate([_pad_lanes(small_w[n][i], wd) for n, wd in zip(small_names, widths)], axis=1) for i in range(3)]
    outs = _adamw(packed[0], packed[1], packed[2], small_g[:, :, :n_small], "adamw_small", N_DEV)
    off = 0
    for n, wd in zip(small_names, widths):
        real = small_w[n][0].shape[1]
        res[n] = tuple(o[:, off:off + real] for o in outs)
        off += wd
    loss = jnp.sum(small_g[:, 0, n_small])

    order = ["w_ada", "b_ada", "g_mix", "w_in", "w_dw", "b_dw", "g_conv_ln", "b_conv_ln", "g_q", "g_k",
             "w_out", "g_ffn", "w_gate", "w_up", "w_down"]
    lead = {"w_ada", "w_in", "w_dw", "w_out", "w_gate", "w_up", "w_down"}
    grads, deltas, new_m, new_v = [], [], [], []
    for n in order:
        g, d, mn, vn = res[n]
        g, d, mn, vn = (t[None] if n in lead else t for t in (g, d, mn, vn))
        grads.append(g)
        deltas.append(d)
        new_m.append(mn)
        new_v.append(vn)
    return (loss, grad_x2.reshape(n_seq, seq, D_MODEL), *grads, *deltas, *new_m, *new_v)
```

```python
import numpy as np
import jax
import jax.numpy as jnp
from jax import lax
from jax.experimental import pallas as pl
from jax.experimental.pallas import tpu as pltpu

F32 = jnp.float32
BF16 = jnp.bfloat16

N_DEV = 8
D_MODEL = 1024
D_CONV = 512
D_ATT = 512
HEAD_DIM = 64
CONV_WIDTH = 31
D_IN = 2 * D_CONV + 3 * D_ATT
D_FF = 2816
N_MOD = 6
EPS = 1e-6
RADIUS = 64
DILATIONS = (1, 4, 16)
Q_BLOCK = 128
LANES = 128
VMEM_LIMIT = 56 * 1024 * 1024

ADAM_LR = 0.001
ADAM_B1 = 0.9
ADAM_B2 = 0.999
ADAM_EPS = 1e-08
ADAM_WD = 0.01
ADAM_STEP = 10

NT = (((1,), (1,)), ((), ()))
TN = (((0,), (0,)), ((), ()))


def _call(body, **kw):
    return pl.pallas_call(body, **kw)


def _params(sem=None, vmem=VMEM_LIMIT):
    return pltpu.CompilerParams(dimension_semantics=sem, vmem_limit_bytes=vmem)


def _sig(x):
    return 1.0 / (1.0 + jnp.exp(-x))


def _sds(shape, dtype):
    return jax.ShapeDtypeStruct(shape, dtype)


N_PEER = N_DEV - 1
ANY_SPEC = pl.BlockSpec(memory_space=pl.ANY)


def _exchange_copies(scatter, ins, outs, *sems):
    n = len(ins)
    if n == 0:
        return [], []
    send_sems, recv_sems, local_sems = sems
    x, y, c = lax.axis_index("x"), lax.axis_index("y"), lax.axis_index("c")
    me = 4 * x + 2 * y + c

    def src(a, slot):
        return ins[a].at[slot] if scatter[a] else ins[a]

    local = [pltpu.make_async_copy(src(a, me), outs[a].at[me], local_sems.at[a]) for a in range(n)]
    flights = []
    for k in range(1, N_DEV):
        px = 1 - x if k & 4 else x
        py = 1 - y if k & 2 else y
        pc = 1 - c if k & 1 else c
        pid = 4 * px + 2 * py + pc
        for a in range(n):
            i = a * N_PEER + k - 1
            send, recv = (pltpu.make_async_remote_copy(
                src_ref=src(a, pid), dst_ref=outs[a].at[slot],
                send_sem=send_sems.at[i], recv_sem=recv_sems.at[i],
                device_id=(px, py, pc), device_id_type=pl.DeviceIdType.MESH) for slot in (me, pid))
            flights.append((send, recv))
    return local, flights


def _exchange_start(*args):
    local, flights = _exchange_copies(*args)
    for cp in local:
        cp.start()
    for send, _ in flights:
        send.start()


def _exchange_wait(*args):
    local, flights = _exchange_copies(*args)
    for send, recv in flights:
        send.wait_send()
        recv.wait_recv()
    for cp in local:
        cp.wait()


def _exchange_shapes(items):
    return [_sds((N_DEV,) + tuple(arr.shape[1:] if scatter else arr.shape), arr.dtype) for arr, scatter in items]


def _exchange_sems(n):
    if n == 0:
        return []
    return [pltpu.SemaphoreType.DMA((n * N_PEER,)), pltpu.SemaphoreType.DMA((n * N_PEER,)),
            pltpu.SemaphoreType.DMA((n,))]


def _gather_by_chip_phase(phase, ins, outs, send_sems, recv_sems, local_sems):
    n = len(ins)
    per = N_PEER
    x, y, c = lax.axis_index("x"), lax.axis_index("y"), lax.axis_index("c")
    me, sibling = (x, y, c), (x, y, 1 - c)
    chips = [(1 - x, y), (x, 1 - y), (1 - x, 1 - y)]

    def slot(px, py, pc):
        return 4 * px + 2 * py + pc

    def copy(a, k, block, to, src=None):
        dst = outs[a].at[slot(*block)]
        return pltpu.make_async_remote_copy(
            src_ref=dst if src is None else src, dst_ref=dst,
            send_sem=send_sems.at[a * per + k], recv_sem=recv_sems.at[a * per + k],
            device_id=to, device_id_type=pl.DeviceIdType.MESH)

    local = [pltpu.make_async_copy(ins[a], outs[a].at[slot(*me)], local_sems.at[a]) for a in range(n)]
    first = []
    for a in range(n):
        first.append(copy(a, 0, me, sibling, src=ins[a]))
        first += [copy(a, 1 + j, me, (*chip, c), src=ins[a]) for j, chip in enumerate(chips)]
    passed = [copy(a, 4 + j, (*chip, c), sibling) for j, chip in enumerate(chips) for a in range(n)]
    if phase == 0:
        for cp in local + first:
            cp.start()
    elif phase == 1:
        for j, chip in enumerate(chips):
            for a in range(n):
                copy(a, 1 + j, (*chip, c), me).wait_recv()
        for cp in passed:
            cp.start()
    else:
        for a in range(n):
            copy(a, 0, sibling, me).wait_recv()
            for j, chip in enumerate(chips):
                copy(a, 4 + j, (*chip, 1 - c), me).wait_recv()
        for cp in first + passed:
            cp.wait_send()
        for cp in local:
            cp.wait()


def _gather_by_chip(arrays, name):
    n = len(arrays)

    def body(*refs):
        for phase in range(3):
            _gather_by_chip_phase(phase, refs[:n], refs[n:2 * n], *refs[2 * n:])

    return _call(
        body, name=name, out_shape=_exchange_shapes([(arr, False) for arr in arrays]),
        in_specs=[ANY_SPEC] * n, out_specs=[ANY_SPEC] * n, scratch_shapes=_exchange_sems(n),
    )(*arrays)


def _exchange(items, name):
    n = len(items)
    scatter = [s for _, s in items]

    def body(*refs):
        args = (scatter, refs[:n], refs[n:2 * n]) + tuple(refs[2 * n:])
        _exchange_start(*args)
        _exchange_wait(*args)

    return _call(
        body, name=name, out_shape=_exchange_shapes(items),
        in_specs=[ANY_SPEC] * n, out_specs=[ANY_SPEC] * n, scratch_shapes=_exchange_sems(n),
    )(*[a for a, _ in items])


def _ada_fwd(c_all, w_ada, b_cols):
    def body(c_ref, w_ref, b_ref, o_ref):
        cv = c_ref[...]
        sc = (cv * _sig(cv)).astype(BF16)
        o_ref[...] = jnp.dot(sc, w_ref[...].astype(BF16), preferred_element_type=F32) + b_ref[...]

    return _call(body, name="ada_fwd", out_shape=_sds((c_all.shape[0], w_ada.shape[1]), F32),
                 compiler_params=_params())(c_all, w_ada, b_cols)


def _ada_bwd(c_all, dmod_cols, dmod_all):
    def body(c_ref, dc_ref, da_ref, gw_ref, gb_ref):
        cv = c_ref[...]
        sc = (cv * _sig(cv)).astype(BF16)
        gw_ref[...] = lax.dot_general(sc, dc_ref[...].astype(BF16), TN, preferred_element_type=F32)
        gb_ref[...] = jnp.sum(da_ref[...], axis=0, keepdims=True)

    return _call(body, name="ada_bwd",
                 out_shape=[_sds((c_all.shape[1], dmod_cols.shape[1]), F32), _sds((1, dmod_all.shape[1]), F32)],
                 compiler_params=_params())(c_all, dmod_cols, dmod_all)


MIX_ROWS = 128


def _mix_in(x2, mod8, g_mix, w_in, seq, tm=512):
    tokens = x2.shape[0]
    per_seq = seq // tm

    def body(x_ref, m_ref, g_ref, wt_ref, h_ref, p_ref, w_ref):
        @pl.when(pl.program_id(0) == 0)
        def _():
            w_ref[...] = wt_ref[...].T

        def normed(c):
            rows = pl.ds(c * MIX_ROWS, MIX_ROWS)
            xv = x_ref[rows, :]
            r = lax.rsqrt(jnp.mean(xv * xv, axis=-1, keepdims=True) + EPS)
            hb = ((xv * r * g_ref[...]) * (1.0 + m_ref[1:2, :]) + m_ref[0:1, :]).astype(BF16)
            h_ref[rows, :] = hb
            return hb

        ahead = normed(0)
        for c in range(tm // MIX_ROWS):
            hb = ahead
            if c + 1 < tm // MIX_ROWS:
                ahead = normed(c + 1)
            p = jnp.dot(hb, w_ref[...], preferred_element_type=F32)
            for cb in range(D_IN // LANES):
                p_ref[cb, pl.ds(c * MIX_ROWS, MIX_ROWS), :] = p[:, cb * LANES:(cb + 1) * LANES]

    return _call(
        body, name="mix_in", grid=(tokens // tm,),
        in_specs=[pl.BlockSpec((tm, D_MODEL), lambda i: (i, 0)),
                  pl.BlockSpec((None, 8, D_MODEL), lambda i: (i // per_seq, 0, 0)),
                  pl.BlockSpec((1, D_MODEL), lambda i: (0, 0)),
                  pl.BlockSpec((D_IN, D_MODEL), lambda i: (0, 0))],
        out_specs=[pl.BlockSpec((tm, D_MODEL), lambda i: (i, 0)),
                   pl.BlockSpec((D_IN // LANES, tm, LANES), lambda i: (0, i, 0))],
        out_shape=[_sds((tokens, D_MODEL), BF16), _sds((D_IN // LANES, tokens, LANES), F32)],
        scratch_shapes=[pltpu.VMEM((D_MODEL, D_IN), BF16)],
        compiler_params=_params(("arbitrary",)),
    )(x2, mod8, g_mix, w_in)


CONV_ROWS = 64
CONV_DW_ROWS = 32
CONV_DW_UNROLL = 4
CONV_HALO = 16


def _fill_shifted(xp, sh, seq):
    for b in range(8):
        sh[b, pl.ds(0, seq + 24), :] = xp[pl.ds(b, seq + 24), :]


def _conv_fwd(proj3, w_dw, b_dw):
    _, n_seq, seq, _ = proj3.shape
    n_cb = D_CONV // LANES

    def body(a_ref, g_ref, w_ref, b_ref, uc_ref, xp, sh):
        zeros = jnp.zeros((CONV_HALO, LANES), F32)
        xp[pl.ds(0, CONV_HALO), :] = zeros
        xp[pl.ds(CONV_HALO + seq, CONV_HALO), :] = zeros
        xp[pl.ds(CONV_HALO, seq), :] = a_ref[...] * _sig(g_ref[...])
        _fill_shifted(xp, sh, seq)

        def blk(i, carry):
            t0 = pl.multiple_of(i * CONV_ROWS, CONV_ROWS)
            acc = jnp.zeros((CONV_ROWS, LANES), F32)
            for j in range(CONV_WIDTH):
                jj = j + 1
                acc = acc + sh[jj % 8, pl.ds(t0 + 8 * (jj // 8), CONV_ROWS), :] * w_ref[j:j + 1, :]
            uc_ref[pl.ds(t0, CONV_ROWS), :] = acc + b_ref[...]
            return carry

        lax.fori_loop(0, seq // CONV_ROWS, blk, 0)

    return _call(
        body, name="conv_fwd", grid=(n_seq, n_cb),
        in_specs=[pl.BlockSpec((None, None, seq, LANES), lambda b, cb: (cb, b, 0, 0)),
                  pl.BlockSpec((None, None, seq, LANES), lambda b, cb: (n_cb + cb, b, 0, 0)),
                  pl.BlockSpec((CONV_WIDTH, LANES), lambda b, cb: (0, cb)),
                  pl.BlockSpec((1, LANES), lambda b, cb: (0, cb))],
        out_specs=pl.BlockSpec((None, seq, LANES), lambda b, cb: (b, 0, cb)),
        out_shape=_sds((n_seq, seq, D_CONV), F32),
        scratch_shapes=[pltpu.VMEM((seq + 2 * CONV_HALO, LANES), F32),
                        pltpu.VMEM((8, seq + 2 * CONV_HALO, LANES), F32)],
        compiler_params=_params(("parallel", "parallel")),
    )(proj3, proj3, w_dw, b_dw)


def _conv_bwd(duc3, proj3, w_dw):
    _, n_seq, seq, _ = proj3.shape
    n_cb = D_CONV // LANES

    def body(duc_ref, a_ref, g_ref, w_ref, da_ref, dg_ref, dw_ref, db_ref, xp, sh):
        @pl.when(pl.program_id(1) == 0)
        def _():
            dw_ref[...] = jnp.zeros_like(dw_ref)
            db_ref[...] = jnp.zeros_like(db_ref)

        zeros = jnp.zeros((CONV_HALO, LANES), F32)
        xp[pl.ds(0, CONV_HALO), :] = zeros
        xp[pl.ds(CONV_HALO + seq, CONV_HALO), :] = zeros
        xp[pl.ds(CONV_HALO, seq), :] = a_ref[...] * _sig(g_ref[...])
        _fill_shifted(xp, sh, seq)
        for j0 in range(0, CONV_WIDTH, 8):
            taps = range(j0, min(j0 + 8, CONV_WIDTH))

            def wblk(i, accs, taps=taps):
                for u in range(CONV_DW_UNROLL):
                    t0 = pl.multiple_of((i * CONV_DW_UNROLL + u) * CONV_DW_ROWS, CONV_DW_ROWS)
                    d = duc_ref[pl.ds(t0, CONV_DW_ROWS), :]
                    accs = tuple(acc + d * sh[(j + 1) % 8, pl.ds(t0 + 8 * ((j + 1) // 8), CONV_DW_ROWS), :]
                                 for acc, j in zip(accs, taps))
                return accs

            accs = lax.fori_loop(0, seq // (CONV_DW_ROWS * CONV_DW_UNROLL), wblk,
                                 tuple(jnp.zeros((CONV_DW_ROWS, LANES), F32) for _ in taps))
            for acc, j in zip(accs, taps):
                dw_ref[j:j + 1, :] += jnp.sum(acc, axis=0, keepdims=True)
        db_ref[0:1, :] += jnp.sum(duc_ref[...], axis=0, keepdims=True)
        xp[pl.ds(CONV_HALO, seq), :] = duc_ref[...]
        _fill_shifted(xp, sh, seq)

        def ublk(i, carry):
            t0 = pl.multiple_of(i * CONV_ROWS, CONV_ROWS)
            acc = jnp.zeros((CONV_ROWS, LANES), F32)
            for j in range(CONV_WIDTH):
                jj = CONV_WIDTH - j
                acc = acc + sh[jj % 8, pl.ds(t0 + 8 * (jj // 8), CONV_ROWS), :] * w_ref[j:j + 1, :]
            av = a_ref[pl.ds(t0, CONV_ROWS), :]
            sg = _sig(g_ref[pl.ds(t0, CONV_ROWS), :])
            da_ref[pl.ds(t0, CONV_ROWS), :] = (acc * sg).astype(BF16)
            dg_ref[pl.ds(t0, CONV_ROWS), :] = (acc * av * sg * (1.0 - sg)).astype(BF16)
            return carry

        lax.fori_loop(0, seq // CONV_ROWS, ublk, 0)

    return _call(
        body, name="conv_bwd", grid=(n_cb, n_seq),
        in_specs=[pl.BlockSpec((None, seq, LANES), lambda cb, b: (b, 0, cb)),
                  pl.BlockSpec((None, None, seq, LANES), lambda cb, b: (cb, b, 0, 0)),
                  pl.BlockSpec((None, None, seq, LANES), lambda cb, b: (n_cb + cb, b, 0, 0)),
                  pl.BlockSpec((CONV_WIDTH, LANES), lambda cb, b: (0, cb))],
        out_specs=[pl.BlockSpec((None, seq, LANES), lambda cb, b: (b, 0, cb)),
                   pl.BlockSpec((None, seq, LANES), lambda cb, b: (b, 0, cb)),
                   pl.BlockSpec((32, LANES), lambda cb, b: (0, cb)),
                   pl.BlockSpec((8, LANES), lambda cb, b: (0, cb))],
        out_shape=[_sds((n_seq, seq, D_CONV), BF16), _sds((n_seq, seq, D_CONV), BF16),
                   _sds((32, D_CONV), F32), _sds((8, D_CONV), F32)],
        scratch_shapes=[pltpu.VMEM((seq + 2 * CONV_HALO, LANES), F32),
                        pltpu.VMEM((8, seq + 2 * CONV_HALO, LANES), F32)],
        compiler_params=_params(("parallel", "arbitrary")),
    )(duc3, proj3, proj3, w_dw)


MASKED = 1e30
ATT_ROWS = 512
ATT_UNROLL = 8
ATT_FWD_UNROLL = 8


def _distance_mats(dil, seg_len):
    kw = min(2 * Q_BLOCK, seg_len)
    offsets = (0, -RADIUS, -2 * RADIUS) if kw == 2 * Q_BLOCK else (0,)
    a = np.arange(Q_BLOCK)[:, None]
    b = np.arange(kw)[None, :]
    mats = []
    for off in offsets:
        rel = np.abs(b + off - a)
        mats.append(np.where(rel <= RADIUS, dil * rel, MASKED))
    return jnp.asarray(np.stack(mats).astype(np.float32))


def _alibi_rows():
    s = np.zeros((4, 8, LANES), np.float32)
    for hp in range(4):
        for hl in range(2):
            s[hp, hl, :] = 2.0 ** (-(2 * hp + hl + 1))
    return jnp.asarray(s)


def _window(n, seg_len):
    i0 = pl.multiple_of(n * Q_BLOCK, Q_BLOCK)
    if seg_len <= Q_BLOCK:
        return i0, i0, 0
    per_seg = seg_len // Q_BLOCK
    j = n % per_seg
    seg0 = (n // per_seg) * seg_len
    ks_local = jnp.clip(j * Q_BLOCK - RADIUS, 0, seg_len - 2 * Q_BLOCK)
    ks = pl.multiple_of(seg0 + ks_local, RADIUS)
    var = jnp.where(j == 0, 0, jnp.where(j == per_seg - 1, 2, 1))
    return i0, ks, var


def _first_head(rows):
    return lax.broadcasted_iota(jnp.int32, (rows, LANES), 1) < HEAD_DIM


def _same_head():
    head = np.arange(LANES) // HEAD_DIM
    return jnp.asarray((head[:, None] == head[None, :]).astype(np.float32)).astype(BF16)


def _head_sum(x, same_ref):
    hi = x.astype(BF16)
    lo = (x - hi.astype(F32)).astype(BF16)
    return (jnp.dot(hi, same_ref[...], preferred_element_type=F32)
            + jnp.dot(lo, same_ref[...], preferred_element_type=F32))


def _head_mean(x, same_ref):
    return _head_sum(x, same_ref) * (1.0 / HEAD_DIM)


def _per_head(x, first):
    swapped = pltpu.roll(x, HEAD_DIM, 1)
    return jnp.where(first, x, swapped), jnp.where(first, swapped, x)


STRIDE = 4


def _gather_segments(src, dil, seq, tmp, put):
    if dil == 1:
        put(0, seq, src[pl.ds(0, seq), :])
    elif dil == STRIDE:
        seg = seq // dil
        for r in range(dil):
            put(r * seg, seg, src[pl.ds(r, seg, stride=dil), :])
    else:
        part, seg = seq // STRIDE, seq // dil
        for b in range(STRIDE):
            tmp[pl.ds(b * part, part), :] = src[pl.ds(b, part, stride=STRIDE), :]
        for b in range(STRIDE):
            for a in range(dil // STRIDE):
                put(b * part + a * seg, seg, tmp[pl.ds(b * part + a, seg, stride=dil // STRIDE), :])


def _scatter_segments(dst, get, dil, seq, tmp, accumulate):
    def write(rows, val):
        if accumulate:
            dst[rows, :] += val
        else:
            dst[rows, :] = val

    if dil == 1:
        write(pl.ds(0, seq), get(0, seq))
    elif dil == STRIDE:
        seg = seq // dil
        for r in range(dil):
            write(pl.ds(r, seg, stride=dil), get(r * seg, seg))
    else:
        part, seg = seq // STRIDE, seq // dil
        for b in range(STRIDE):
            for a in range(dil // STRIDE):
                tmp[pl.ds(b * part + a, seg, stride=dil // STRIDE), :] = get(b * part + a * seg, seg)
        for b in range(STRIDE):
            write(pl.ds(b, part, stride=STRIDE), tmp[pl.ds(b * part, part), :])


def _permute_rows(dst, src, dil, seq, tmp):
    def put(start, size, val):
        dst[pl.ds(start, size), :] = val.astype(dst.dtype)

    _gather_segments(src, dil, seq, tmp, put)


def _permute_rows_by_head(dst, src, dil, seq, tmp):
    def put(start, size, val):
        first = _first_head(size)
        dst[0, pl.ds(start, size), :] = jnp.where(first, val, 0.0).astype(dst.dtype)
        dst[1, pl.ds(start, size), :] = jnp.where(first, 0.0, val).astype(dst.dtype)

    _gather_segments(src, dil, seq, tmp, put)


def _qk_normalise(q_ref, g2_ref, same_ref, dst, seq, scale):
    def chunk(ci, carry):
        rows = pl.ds(pl.multiple_of(ci * ATT_ROWS, ATT_ROWS), ATT_ROWS)
        qv = q_ref[rows, :]
        r = lax.rsqrt(_head_mean(qv * qv, same_ref) + EPS)
        dst[rows, :] = qv * r * (g2_ref[...] * scale)
        return carry

    lax.fori_loop(0, seq // ATT_ROWS, chunk, 0)


def _attn_fwd(proj3, g_q2, g_k2, ride):
    _, n_seq, seq, _ = proj3.shape
    dms = [_distance_mats(d, seq // d) for d in DILATIONS]
    same = _same_head()
    col0 = 2 * D_CONV // LANES
    n_hp = D_ATT // LANES

    n_ride = len(ride)
    assert not any(scatter for _, scatter in ride), "the forward's ride is an all-gather"

    def body(*refs):
        q_ref, k_ref, v_ref, gq_ref, gk_ref, sl_ref, dm1, dm4, dm16, same_ref = refs[:10]
        ride_in = refs[10:10 + n_ride]
        y_ref, lse_ref = refs[10 + n_ride:12 + n_ride]
        ride_out = refs[12 + n_ride:12 + 2 * n_ride]
        (qf, kf, qp, kp, vp, oml_p, o1, o4, o16, m1, m4, m16, l1, l4, l16,
         tmp) = refs[12 + 2 * n_ride:28 + 2 * n_ride]
        o_nat, m_nat, l_nat = (o1, o4, o16), (m1, m4, m16), (l1, l4, l16)
        ride_args = (ride_in, ride_out) + tuple(refs[28 + 2 * n_ride:])
        step = pl.program_id(0) * n_hp + pl.program_id(1)
        n_steps = n_seq * n_hp

        if n_ride:
            for phase, at in enumerate((0, (3 * n_steps) // 4)):
                @pl.when(step == at)
                def _(phase=phase):
                    _gather_by_chip_phase(phase, *ride_args)

        dm_refs = (dm1, dm4, dm16)
        _qk_normalise(q_ref, gq_ref, same_ref, qf, seq, HEAD_DIM ** -0.5)
        _qk_normalise(k_ref, gk_ref, same_ref, kf, seq, 1.0)
        slopes = (sl_ref[0:1, 0:1], sl_ref[1:2, 0:1])
        for pi, dil in enumerate(DILATIONS):
            seg = seq // dil
            kw = min(2 * Q_BLOCK, seg)
            _permute_rows_by_head(qp, qf, dil, seq, tmp)
            _permute_rows(kp, kf, dil, seq, tmp)
            _permute_rows(vp, v_ref, dil, seq, tmp)

            def blk(it, carry, seg=seg, kw=kw, pi=pi, dst=oml_p):
                first = _first_head(Q_BLOCK)
                chains = [(sub, h) for sub in range(ATT_FWD_UNROLL) for h in range(2)]
                win = [_window(it * ATT_FWD_UNROLL + sub, seg) for sub in range(ATT_FWD_UNROLL)]
                s = {}
                for sub, h in chains:
                    i0, ks, var = win[sub]
                    s[sub, h] = lax.dot_general(qp[h, pl.ds(i0, Q_BLOCK), :], kp[pl.ds(ks, kw), :], NT,
                                                preferred_element_type=F32) - slopes[h] * dm_refs[pi][var]
                m, l, p = {}, {}, {}
                for c in chains:
                    m[c] = jnp.max(s[c], axis=1, keepdims=True)
                    e = jnp.exp(s[c] - m[c])
                    l[c] = jnp.sum(e, axis=1, keepdims=True)
                    p[c] = e.astype(BF16)
                o = {}
                for sub, h in chains:
                    o[sub, h] = jnp.dot(p[sub, h], vp[pl.ds(win[sub][1], kw), :], preferred_element_type=F32)
                packed = [jnp.concatenate([jnp.where(first, t[sub, 0], t[sub, 1]) for t in (o, m, l)], axis=1)
                          for sub in range(ATT_FWD_UNROLL)]
                span = ATT_FWD_UNROLL * Q_BLOCK
                dst[pl.ds(pl.multiple_of(it * span, span), span), :] = jnp.concatenate(packed, axis=0)
                return carry

            lax.fori_loop(0, seq // (Q_BLOCK * ATT_FWD_UNROLL), blk, 0)
            for n, nat in enumerate((o_nat[pi], m_nat[pi], l_nat[pi])):
                _scatter_segments(nat, lambda start, size, n=n: oml_p[pl.ds(start, size), pl.ds(n * LANES, LANES)],
                                  dil, seq, tmp, accumulate=False)

        def merge(ci, carry):
            rows = pl.ds(pl.multiple_of(ci * ATT_ROWS, ATT_ROWS), ATT_ROWS)
            ms = [m_nat[pi][rows, :] for pi in range(3)]
            m_all = jnp.maximum(jnp.maximum(ms[0], ms[1]), ms[2])
            es = [jnp.exp(m - m_all) for m in ms]
            l_all = sum(l_nat[pi][rows, :] * es[pi] for pi in range(3))
            inv = 1.0 / l_all
            o = sum(o_nat[pi][rows, :] * (es[pi] * inv) for pi in range(3))
            y_ref[rows, :] = o.astype(BF16)
            lse_ref[rows, :] = m_all + jnp.log(l_all)
            return carry

        lax.fori_loop(0, seq // ATT_ROWS, merge, 0)

        if n_ride:
            @pl.when(step == n_steps - 1)
            def _():
                _gather_by_chip_phase(2, *ride_args)

    def col(off):
        return pl.BlockSpec((None, None, seq, LANES), lambda b, hp: (col0 + off * n_hp + hp, b, 0, 0))

    def whole(arr):
        return pl.BlockSpec(arr.shape, lambda b, hp: (0,) * arr.ndim)

    rows_f32 = pltpu.VMEM((seq, LANES), F32)
    rows_bf16 = pltpu.VMEM((seq, LANES), BF16)
    return _call(
        body, name="attn_fwd", grid=(n_seq, n_hp),
        in_specs=[col(0), col(1), col(2), whole(g_q2), whole(g_k2),
                  pl.BlockSpec((None, 8, LANES), lambda b, hp: (hp, 0, 0)),
                  whole(dms[0]), whole(dms[1]), whole(dms[2]), whole(same)] + [ANY_SPEC] * n_ride,
        out_specs=[pl.BlockSpec((None, seq, LANES), lambda b, hp: (b, 0, hp)),
                   pl.BlockSpec((None, seq, LANES), lambda b, hp: (b, 0, hp))] + [ANY_SPEC] * n_ride,
        out_shape=[_sds((n_seq, seq, D_ATT), BF16), _sds((n_seq, seq, D_ATT), F32)] + _exchange_shapes(ride),
        scratch_shapes=[rows_f32, rows_f32, pltpu.VMEM((2, seq, LANES), BF16), rows_bf16, rows_bf16]
        + [pltpu.VMEM((seq, 3 * LANES), F32)] + [rows_f32] * 10 + _exchange_sems(n_ride),
        compiler_params=_params(("arbitrary", "arbitrary")),
    )(proj3, proj3, proj3, g_q2, g_k2, _alibi_rows(), *dms, same, *[a for a, _ in ride])


def _attn_bwd(proj3, do3, y_att3, lse3, g_q2, g_k2, ride):
    _, n_seq, seq, _ = proj3.shape
    dms = [_distance_mats(d, seq // d) for d in DILATIONS]
    same = _same_head()
    col0 = 2 * D_CONV // LANES
    n_hp = D_ATT // LANES

    n_ride = len(ride)
    ride_scatter = [s for _, s in ride]

    def body(*refs):
        (q_ref, k_ref, v_ref, do_ref, o_ref, lse_ref, gq_ref, gk_ref, sl_ref, dm1, dm4, dm16,
         same_ref) = refs[:13]
        ride_in = refs[13:13 + n_ride]
        dq_ref, dk_ref, dv_ref, dg_ref = refs[13 + n_ride:17 + n_ride]
        ride_out = refs[17 + n_ride:17 + 2 * n_ride]
        (qf, kf, qp, dop, kp, vp, sn, sp, dqp, dkp, dvp, dqn, dkn, dvn,
         tmp) = refs[17 + 2 * n_ride:32 + 2 * n_ride]
        ride_args = (ride_scatter, ride_in, ride_out) + tuple(refs[32 + 2 * n_ride:])
        dm_refs = (dm1, dm4, dm16)
        step = pl.program_id(0) * n_hp + pl.program_id(1)

        @pl.when(step == 0)
        def _():
            _exchange_start(*ride_args)
            dg_ref[...] = jnp.zeros_like(dg_ref)

        _qk_normalise(q_ref, gq_ref, same_ref, qf, seq, HEAD_DIM ** -0.5)
        _qk_normalise(k_ref, gk_ref, same_ref, kf, seq, 1.0)

        def stats(ci, carry):
            rows = pl.ds(pl.multiple_of(ci * ATT_ROWS, ATT_ROWS), ATT_ROWS)
            first = _first_head(ATT_ROWS)
            sn[0, rows, :], sn[1, rows, :] = _per_head(lse_ref[rows, :], first)
            prod = do_ref[rows, :] * o_ref[rows, :].astype(F32)
            sn[2, rows, :], sn[3, rows, :] = _per_head(_head_sum(prod, same_ref), first)
            return carry

        lax.fori_loop(0, seq // ATT_ROWS, stats, 0)
        slopes = (sl_ref[0:1, 0:1], sl_ref[1:2, 0:1])
        half = seq // (Q_BLOCK * ATT_UNROLL)
        region = seq // ATT_UNROLL

        for pi, dil in enumerate(DILATIONS):
            seg = seq // dil
            kw = min(2 * Q_BLOCK, seg)
            _permute_rows_by_head(qp, qf, dil, seq, tmp)
            _permute_rows_by_head(dop, do_ref, dil, seq, tmp)
            _permute_rows(kp, kf, dil, seq, tmp)
            _permute_rows(vp, v_ref, dil, seq, tmp)
            if dil == 1:
                st = sn
            else:
                st = sp
                for n in range(4):
                    _permute_rows(sp.at[n], sn.at[n], dil, seq, tmp)
            def touched(sub, seg=seg):
                lo, hi = sub * region, (sub + 1) * region
                if seg < region:
                    return lo, hi
                seg0 = lo // seg * seg
                return max(lo - RADIUS, seg0), min(hi + RADIUS, seg0 + seg)

            def summed(acc, start, size, touched=touched):
                pieces = []
                for c0 in range(start, start + size, RADIUS):
                    owners = [s for s in range(ATT_UNROLL) if touched(s)[0] <= c0 and c0 + RADIUS <= touched(s)[1]]
                    if pieces and pieces[-1][2] == owners:
                        pieces[-1][1] += RADIUS
                    else:
                        pieces.append([c0, RADIUS, owners])
                vals = [sum(acc[o, pl.ds(c0, n), :] for o in owners) for c0, n, owners in pieces]
                return vals[0] if len(vals) == 1 else jnp.concatenate(vals, axis=0)

            for sub in range(ATT_UNROLL):
                lo, hi = touched(sub)
                dkp[sub, pl.ds(lo, hi - lo), :] = jnp.zeros((hi - lo, LANES), F32)
                dvp[sub, pl.ds(lo, hi - lo), :] = jnp.zeros((hi - lo, LANES), F32)

            def blk(it, carry, seg=seg, kw=kw, pi=pi, st=st):
                first = _first_head(Q_BLOCK)
                chains = [(sub, h) for sub in range(ATT_UNROLL) for h in range(2)]
                win = [_window(it + sub * half, seg) for sub in range(ATT_UNROLL)]
                qrows = [pl.ds(w[0], Q_BLOCK) for w in win]
                krows = [pl.ds(w[1], kw) for w in win]

                def over_keys(n, sub):
                    t = st[n, qrows[sub], :]
                    return t if kw == LANES else jnp.concatenate([t] * (kw // LANES), axis=1)

                s, dp = {}, {}
                for sub, h in chains:
                    s[sub, h] = lax.dot_general(qp[h, qrows[sub], :], kp[krows[sub], :], NT,
                                                preferred_element_type=F32) - slopes[h] * dm_refs[pi][win[sub][2]]
                    dp[sub, h] = lax.dot_general(dop[h, qrows[sub], :], vp[krows[sub], :], NT,
                                                 preferred_element_type=F32)
                p, ds = {}, {}
                for sub, h in chains:
                    e = jnp.exp(s[sub, h] - over_keys(h, sub))
                    ds[sub, h] = (e * (dp[sub, h] - over_keys(2 + h, sub))).astype(BF16)
                    p[sub, h] = e.astype(BF16)
                dq, dk, dv = {}, {}, {}
                for sub, h in chains:
                    dq[sub, h] = jnp.dot(ds[sub, h], kp[krows[sub], :], preferred_element_type=F32)
                    dk[sub, h] = lax.dot_general(ds[sub, h], qp[h, qrows[sub], :], TN, preferred_element_type=F32)
                    dv[sub, h] = lax.dot_general(p[sub, h], dop[h, qrows[sub], :], TN, preferred_element_type=F32)
                for sub in range(ATT_UNROLL):
                    dqp[qrows[sub], :] = jnp.where(first, dq[sub, 0], dq[sub, 1])
                    dkp[sub, krows[sub], :] += dk[sub, 0] + dk[sub, 1]
                    dvp[sub, krows[sub], :] += dv[sub, 0] + dv[sub, 1]
                return carry

            lax.fori_loop(0, half, blk, 0)
            first_pattern = pi == 0
            if first_pattern:
                for r0 in range(0, seq, region):
                    rows = pl.ds(r0, region)
                    dqn[rows, :] = dqp[rows, :]
                    dkn[rows, :] = summed(dkp, r0, region)
                    dvn[rows, :] = summed(dvp, r0, region)
            else:
                _scatter_segments(dqn, lambda start, size: dqp[pl.ds(start, size), :], dil, seq, tmp, accumulate=True)
                for nat, acc in ((dkn, dkp), (dvn, dvp)):
                    _scatter_segments(nat, lambda start, size, acc=acc: summed(acc, start, size),
                                      dil, seq, tmp, accumulate=True)

        def finish(ci, carry):
            rows = pl.ds(pl.multiple_of(ci * ATT_ROWS, ATT_ROWS), ATT_ROWS)
            for src_ref, g_ref, dn, dst_ref, scale, row in (
                    (q_ref, gq_ref, dqn, dq_ref, HEAD_DIM ** -0.5, 0), (k_ref, gk_ref, dkn, dk_ref, 1.0, 1)):
                xv = src_ref[rows, :]
                r = lax.rsqrt(_head_mean(xv * xv, same_ref) + EPS)
                xhat = xv * r
                d = dn[rows, :] * scale
                dg_ref[row:row + 1, :] += jnp.sum(d * xhat, axis=0, keepdims=True)
                dxh = d * g_ref[...]
                dst_ref[rows, :] = (r * (dxh - xhat * _head_mean(dxh * xhat, same_ref))).astype(BF16)
            dv_ref[rows, :] = dvn[rows, :].astype(BF16)
            return carry

        lax.fori_loop(0, seq // ATT_ROWS, finish, 0)

        @pl.when(step == n_seq * n_hp - 1)
        def _():
            _exchange_wait(*ride_args)

    def col(off):
        return pl.BlockSpec((None, None, seq, LANES), lambda b, hp: (col0 + off * n_hp + hp, b, 0, 0))

    def whole(arr):
        return pl.BlockSpec(arr.shape, lambda b, hp: (0,) * arr.ndim)

    att = pl.BlockSpec((None, seq, LANES), lambda b, hp: (b, 0, hp))
    rows_f32 = pltpu.VMEM((seq, LANES), F32)
    rows_bf16 = pltpu.VMEM((seq, LANES), BF16)
    by_head_bf16 = pltpu.VMEM((2, seq, LANES), BF16)
    per_sub_f32 = pltpu.VMEM((ATT_UNROLL, seq, LANES), F32)
    stats_f32 = pltpu.VMEM((4, seq, LANES), F32)
    return _call(
        body, name="attn_bwd", grid=(n_seq, n_hp),
        in_specs=[col(0), col(1), col(2), att, att, att, whole(g_q2), whole(g_k2),
                  pl.BlockSpec((None, 8, LANES), lambda b, hp: (hp, 0, 0)),
                  whole(dms[0]), whole(dms[1]), whole(dms[2]), whole(same)] + [ANY_SPEC] * n_ride,
        out_specs=[att, att, att, pl.BlockSpec((8, LANES), lambda b, hp: (0, 0))] + [ANY_SPEC] * n_ride,
        out_shape=[_sds((n_seq, seq, D_ATT), BF16)] * 3 + [_sds((8, LANES), F32)] + _exchange_shapes(ride),
        scratch_shapes=[rows_f32, rows_f32, by_head_bf16, by_head_bf16, rows_bf16, rows_bf16, stats_f32, stats_f32,
                        rows_f32, per_sub_f32, per_sub_f32, rows_f32, rows_f32, rows_f32, rows_f32]
        + _exchange_sems(n_ride),
        compiler_params=_params(("arbitrary", "arbitrary")),
    )(proj3, proj3, proj3, do3, y_att3, lse3, g_q2, g_k2, _alibi_rows(), *dms, same, *[a for a, _ in ride])


def _mix_out(uc2, y_att2, x2, mod8, g_ln, b_ln, g_ffn, w_out, seq, tm=512):
    tokens = x2.shape[0]
    per_seq = seq // tm

    def body(uc_ref, ya_ref, x_ref, m_ref, gl_ref, bl_ref, gf_ref, w_ref, yc_ref, mix_ref, x1_ref, h2_ref):
        uc = uc_ref[...]
        mu = jnp.mean(uc, axis=-1, keepdims=True)
        cen = uc - mu
        rs = lax.rsqrt(jnp.mean(cen * cen, axis=-1, keepdims=True) + EPS)
        z = cen * rs * gl_ref[...] + bl_ref[...]
        yc = (z * _sig(z)).astype(BF16)
        yc_ref[...] = yc
        mix = (jnp.dot(yc, w_ref[pl.ds(0, D_CONV), :], preferred_element_type=F32)
               + jnp.dot(ya_ref[...], w_ref[pl.ds(D_CONV, D_ATT), :], preferred_element_type=F32))
        mix_ref[...] = mix.astype(BF16)
        x1 = x_ref[...] + m_ref[2:3, :] * mix
        x1_ref[...] = x1
        r = lax.rsqrt(jnp.mean(x1 * x1, axis=-1, keepdims=True) + EPS)
        h2_ref[...] = ((x1 * r * gf_ref[...]) * (1.0 + m_ref[4:5, :]) + m_ref[3:4, :]).astype(BF16)

    def rows(width):
        return pl.BlockSpec((tm, width), lambda i: (i, 0))

    def vec(width):
        return pl.BlockSpec((1, width), lambda i: (0, 0))

    return _call(
        body, name="mix_out", grid=(tokens // tm,),
        in_specs=[rows(D_CONV), rows(D_ATT), rows(D_MODEL),
                  pl.BlockSpec((None, 8, D_MODEL), lambda i: (i // per_seq, 0, 0)),
                  vec(D_CONV), vec(D_CONV), vec(D_MODEL),
                  pl.BlockSpec((D_MODEL, D_MODEL), lambda i: (0, 0))],
        out_specs=[rows(D_CONV), rows(D_MODEL), rows(D_MODEL), rows(D_MODEL)],
        out_shape=[_sds((tokens, D_CONV), BF16), _sds((tokens, D_MODEL), BF16),
                   _sds((tokens, D_MODEL), F32), _sds((tokens, D_MODEL), BF16)],
        compiler_params=_params(("parallel",)),
    )(uc2, y_att2, x2, mod8, g_ln, b_ln, g_ffn, w_out)


def _mix_out_bwd(dmix, uc2, g_ln, b_ln, w_out, tm=512):
    tokens = dmix.shape[0]

    def body(dm_ref, uc_ref, gl_ref, bl_ref, w_ref, duc_ref, do_ref, dgb_ref):
        @pl.when(pl.program_id(0) == 0)
        def _():
            dgb_ref[...] = jnp.zeros_like(dgb_ref)

        dmv = dm_ref[...]
        dyc = lax.dot_general(dmv, w_ref[pl.ds(0, D_CONV), :], NT, preferred_element_type=F32)
        do_ref[...] = lax.dot_general(dmv, w_ref[pl.ds(D_CONV, D_ATT), :], NT, preferred_element_type=F32)
        uc = uc_ref[...]
        mu = jnp.mean(uc, axis=-1, keepdims=True)
        cen = uc - mu
        rs = lax.rsqrt(jnp.mean(cen * cen, axis=-1, keepdims=True) + EPS)
        xh = cen * rs
        z = xh * gl_ref[...] + bl_ref[...]
        sg = _sig(z)
        dz = dyc * (sg * (1.0 + z * (1.0 - sg)))
        dgb_ref[0:1, :] += jnp.sum(dz * xh, axis=0, keepdims=True)
        dgb_ref[1:2, :] += jnp.sum(dz, axis=0, keepdims=True)
        dxh = dz * gl_ref[...]
        duc_ref[...] = rs * (dxh - jnp.mean(dxh, axis=-1, keepdims=True)
                             - xh * jnp.mean(dxh * xh, axis=-1, keepdims=True))

    return _call(
        body, name="mix_out_bwd", grid=(tokens // tm,),
        in_specs=[pl.BlockSpec((tm, D_MODEL), lambda i: (i, 0)),
                  pl.BlockSpec((tm, D_CONV), lambda i: (i, 0)),
                  pl.BlockSpec((1, D_CONV), lambda i: (0, 0)),
                  pl.BlockSpec((1, D_CONV), lambda i: (0, 0)),
                  pl.BlockSpec((D_MODEL, D_MODEL), lambda i: (0, 0))],
        out_specs=[pl.BlockSpec((tm, D_CONV), lambda i: (i, 0)),
                   pl.BlockSpec((tm, D_ATT), lambda i: (i, 0)),
                   pl.BlockSpec((8, D_CONV), lambda i: (0, 0))],
        out_shape=[_sds((tokens, D_CONV), F32), _sds((tokens, D_ATT), F32), _sds((8, D_CONV), F32)],
        compiler_params=_params(("arbitrary",)),
    )(dmix, uc2, g_ln, b_ln, w_out)


FF_TILE = 256
FF_TILES = D_FF // FF_TILE
FF_ROWS = 256


def _chunk_stream(hbm_refs, bufs, sems, row0, to_hbm):
    def copies(r):
        rows = pl.ds(pl.multiple_of(row0 + r * FF_ROWS, FF_ROWS), FF_ROWS)
        out = []
        for n, (hbm, buf) in enumerate(zip(hbm_refs, bufs)):
            ends = (buf.at[r % 2], hbm.at[rows, :]) if to_hbm else (hbm.at[rows, :], buf.at[r % 2])
            out.append(pltpu.make_async_copy(*ends, sems.at[(r % 2) * len(hbm_refs) + n]))
        return out

    return copies


def _ffn_fwd(h2, w_gate, w_up, w_down, x1, target, mod8, seq, tm=2048):
    tokens = h2.shape[0]
    per_seq = seq // tm
    n_seq = tokens // seq
    last = D_FF // FF_TILE - 1
    n_chunks = tm // FF_ROWS

    def body(h_ref, wg_ref, wu_ref, wd_ref, m_ref, x1_hbm, t_hbm, gate_ref, up_ref, dy_hbm, df_hbm, sq_ref, dgf_ref,
             f_ref, x1_buf, t_buf, dy_buf, df_buf, in_sems, out_sems):
        i, j = pl.program_id(0), pl.program_id(1)
        loads = _chunk_stream((x1_hbm, t_hbm), (x1_buf, t_buf), in_sems, i * tm, to_hbm=False)
        stores = _chunk_stream((dy_hbm, df_hbm), (dy_buf, df_buf), out_sems, i * tm, to_hbm=True)

        @pl.when(j == 0)
        def _():
            f_ref[...] = jnp.zeros_like(f_ref)

        @pl.when((j == 0) & (i == 0))
        def _():
            sq_ref[...] = jnp.zeros_like(sq_ref)

        @pl.when((j == 0) & (i % per_seq == 0))
        def _():
            dgf_ref[...] = jnp.zeros_like(dgf_ref)

        def gate_up(r):
            hv = h_ref[pl.ds(r * FF_ROWS, FF_ROWS), :]
            return (lax.dot_general(hv, wg_ref[...], NT, preferred_element_type=F32),
                    lax.dot_general(hv, wu_ref[...], NT, preferred_element_type=F32))

        def loss_head(r, fv):
            slot = r % 2
            gate_f = m_ref[5:6, :]
            diff = x1_buf[slot] + gate_f * fv - t_buf[slot]
            sq_ref[0:1, :] += jnp.sum(diff * diff, axis=0, keepdims=True)
            dy = diff * (1.0 / D_MODEL)
            dy_buf[slot] = dy
            df_buf[slot] = (gate_f * dy).astype(BF16)
            dgf_ref[0:1, :] += jnp.sum(dy * fv, axis=0, keepdims=True)

        def tile(finish):
            if finish:
                for r in range(2):
                    for cp in loads(r):
                        cp.start()
            ahead = gate_up(0)
            for r in range(n_chunks):
                gate, up = ahead
                if r + 1 < n_chunks:
                    ahead = gate_up(r + 1)
                rows = pl.ds(r * FF_ROWS, FF_ROWS)
                gate_ref[rows, :] = gate.astype(BF16)
                up_ref[rows, :] = up.astype(BF16)
                act = (gate * _sig(gate) * up).astype(BF16)
                fv = f_ref[rows, :] + jnp.dot(act, wd_ref[...], preferred_element_type=F32)
                if not finish:
                    f_ref[rows, :] = fv
                    continue
                for cp in loads(r):
                    cp.wait()
                if r >= 2:
                    for cp in stores(r - 2):
                        cp.wait()
                loss_head(r, fv)
                for cp in stores(r):
                    cp.start()
                if r + 2 < n_chunks:
                    for cp in loads(r + 2):
                        cp.start()
            if finish:
                for r in (n_chunks - 2, n_chunks - 1):
                    for cp in stores(r):
                        cp.wait()

        @pl.when(j < last)
        def _():
            tile(False)

        @pl.when(j == last)
        def _():
            tile(True)

    rows_spec = pl.BlockSpec((tm, D_MODEL), lambda i, j: (i, 0))
    per = pl.BlockSpec((None, 8, D_MODEL), lambda i, j: (i // per_seq, 0, 0))
    tile = pl.BlockSpec((None, tm, FF_TILE), lambda i, j: (j, i, 0))
    w_tile = pl.BlockSpec((FF_TILE, D_MODEL), lambda i, j: (j, 0))
    chunk_f32 = pltpu.VMEM((2, FF_ROWS, D_MODEL), F32)
    return _call(
        body, name="ffn_fwd", grid=(tokens // tm, D_FF // FF_TILE),
        in_specs=[rows_spec, w_tile, w_tile,
                  pl.BlockSpec((FF_TILE, D_MODEL), lambda i, j: (j, 0)),
                  per, ANY_SPEC, ANY_SPEC],
        out_specs=[tile, tile, ANY_SPEC, ANY_SPEC, pl.BlockSpec((8, D_MODEL), lambda i, j: (0, 0)), per],
        out_shape=[_sds((FF_TILES, tokens, FF_TILE), BF16), _sds((FF_TILES, tokens, FF_TILE), BF16),
                   _sds((tokens, D_MODEL), F32),
                   _sds((tokens, D_MODEL), BF16), _sds((8, D_MODEL), F32), _sds((n_seq, 8, D_MODEL), F32)],
        scratch_shapes=[pltpu.VMEM((tm, D_MODEL), F32), chunk_f32, chunk_f32, chunk_f32,
                        pltpu.VMEM((2, FF_ROWS, D_MODEL), BF16),
                        pltpu.SemaphoreType.DMA((4,)), pltpu.SemaphoreType.DMA((4,))],
        compiler_params=_params(("arbitrary", "arbitrary")),
    )(h2, w_gate, w_up, w_down, mod8, x1, target)


def _ffn_bwd(df, gate, up, w_gate, w_up, w_down, x1, dy, mix, mod8, g_ffn, seq, tm=2048):
    tokens = df.shape[0]
    per_seq = seq // tm
    n_seq = tokens // seq
    last = D_FF // FF_TILE - 1
    n_chunks = tm // FF_ROWS

    def body(df_ref, gate_ref, up_ref, wg_ref, wu_ref, wd_ref, m_ref, g_ref, x1_hbm, dy_hbm, mix_hbm,
             dgate_ref, dup_ref, act_ref, dx1_hbm, dmix_hbm, dg_ref, dm_ref,
             dh_ref, x1_buf, dy_buf, mix_buf, dx1_buf, dmix_buf, in_sems, out_sems):
        i, j = pl.program_id(0), pl.program_id(1)
        loads = _chunk_stream((x1_hbm, dy_hbm, mix_hbm), (x1_buf, dy_buf, mix_buf), in_sems, i * tm, to_hbm=False)
        stores = _chunk_stream((dx1_hbm, dmix_hbm), (dx1_buf, dmix_buf), out_sems, i * tm, to_hbm=True)

        @pl.when(j == 0)
        def _():
            dh_ref[...] = jnp.zeros_like(dh_ref)

        @pl.when((j == 0) & (i == 0))
        def _():
            dg_ref[...] = jnp.zeros_like(dg_ref)

        @pl.when((j == 0) & (i % per_seq == 0))
        def _():
            dm_ref[...] = jnp.zeros_like(dm_ref)

        def d_act(r):
            return lax.dot_general(df_ref[pl.ds(r * FF_ROWS, FF_ROWS), :], wd_ref[...], NT,
                                   preferred_element_type=F32)

        def norm_bwd(r, dh):
            slot = r % 2
            g = g_ref[...]
            x1v = x1_buf[slot]
            rs = lax.rsqrt(jnp.mean(x1v * x1v, axis=-1, keepdims=True) + EPS)
            xhat = x1v * rs
            dm_ref[0:1, :] += jnp.sum(dh, axis=0, keepdims=True)
            dm_ref[1:2, :] += jnp.sum(dh * (xhat * g), axis=0, keepdims=True)
            dn = dh * (1.0 + m_ref[4:5, :])
            dg_ref[0:1, :] += jnp.sum(dn * xhat, axis=0, keepdims=True)
            dxh = dn * g
            dx1 = dy_buf[slot] + rs * (dxh - xhat * jnp.mean(dxh * xhat, axis=-1, keepdims=True))
            dx1_buf[slot] = dx1
            dm_ref[2:3, :] += jnp.sum(dx1 * mix_buf[slot].astype(F32), axis=0, keepdims=True)
            dmix_buf[slot] = (m_ref[2:3, :] * dx1).astype(BF16)

        def tile(finish):
            if finish:
                for r in range(2):
                    for cp in loads(r):
                        cp.start()
            ahead = d_act(0)
            for r in range(n_chunks):
                dact = ahead
                if r + 1 < n_chunks:
                    ahead = d_act(r + 1)
                rows = pl.ds(r * FF_ROWS, FF_ROWS)
                gate = gate_ref[rows, :].astype(F32)
                up = up_ref[rows, :].astype(F32)
                sg = _sig(gate)
                silu = gate * sg
                act_ref[rows, :] = (silu * up).astype(BF16)
                dup = (dact * silu).astype(BF16)
                dgate = (dact * up * (sg * (1.0 + gate * (1.0 - sg)))).astype(BF16)
                dup_ref[rows, :] = dup
                dgate_ref[rows, :] = dgate
                dh = dh_ref[rows, :] + (jnp.dot(dgate, wg_ref[...], preferred_element_type=F32)
                                        + jnp.dot(dup, wu_ref[...], preferred_element_type=F32))
                if not finish:
                    dh_ref[rows, :] = dh
                    continue
                for cp in loads(r):
                    cp.wait()
                if r >= 2:
                    for cp in stores(r - 2):
                        cp.wait()
                norm_bwd(r, dh)
                for cp in stores(r):
                    cp.start()
                if r + 2 < n_chunks:
                    for cp in loads(r + 2):
                        cp.start()
            if finish:
                for r in (n_chunks - 2, n_chunks - 1):
                    for cp in stores(r):
                        cp.wait()

        @pl.when(j < last)
        def _():
            tile(False)

        @pl.when(j == last)
        def _():
            tile(True)

    tile = pl.BlockSpec((None, tm, FF_TILE), lambda i, j: (j, i, 0))
    w_tile = pl.BlockSpec((FF_TILE, D_MODEL), lambda i, j: (j, 0))
    rows_spec = pl.BlockSpec((tm, D_MODEL), lambda i, j: (i, 0))
    per = pl.BlockSpec((None, 8, D_MODEL), lambda i, j: (i // per_seq, 0, 0))
    chunk_f32 = pltpu.VMEM((2, FF_ROWS, D_MODEL), F32)
    chunk_bf16 = pltpu.VMEM((2, FF_ROWS, D_MODEL), BF16)
    return _call(
        body, name="ffn_bwd", grid=(tokens // tm, D_FF // FF_TILE),
        in_specs=[rows_spec, tile, tile, w_tile, w_tile,
                  pl.BlockSpec((FF_TILE, D_MODEL), lambda i, j: (j, 0)),
                  per, pl.BlockSpec((1, D_MODEL), lambda i, j: (0, 0)), ANY_SPEC, ANY_SPEC, ANY_SPEC],
        out_specs=[tile, tile, tile, ANY_SPEC, ANY_SPEC, pl.BlockSpec((8, D_MODEL), lambda i, j: (0, 0)), per],
        out_shape=[_sds((FF_TILES, tokens, FF_TILE), BF16)] * 3
        + [_sds((tokens, D_MODEL), F32), _sds((tokens, D_MODEL), BF16),
           _sds((8, D_MODEL), F32), _sds((n_seq, 8, D_MODEL), F32)],
        scratch_shapes=[pltpu.VMEM((tm, D_MODEL), F32), chunk_f32, chunk_f32, chunk_bf16, chunk_f32, chunk_bf16,
                        pltpu.SemaphoreType.DMA((6,)), pltpu.SemaphoreType.DMA((4,))],
        compiler_params=_params(("arbitrary", "arbitrary")),
    )(df, gate, up, w_gate, w_up, w_down, mod8, g_ffn, x1, dy, mix)


def _mix_in_bwd(d_a, d_g, d_q, d_k, d_v, w_in, x2, dx1, mod8, g_mix, seq, ride, tm=512):
    tokens = x2.shape[0]
    per_seq = seq // tm
    n_seq = tokens // seq
    parts = (d_a, d_g, d_q, d_k, d_v)
    width = D_CONV
    n_ride = len(ride)
    ride_scatter = [s for _, s in ride]

    def body(*refs):
        da_ref, dg_ref, dq_ref, dk_ref, dv_ref, w_ref, x_ref, dx1_ref, m_ref, g_ref = refs[:10]
        ride_in = refs[10:10 + n_ride]
        gx_ref, dgm_ref, dm_ref = refs[10 + n_ride:13 + n_ride]
        ride_args = (ride_scatter, ride_in, refs[13 + n_ride:13 + 2 * n_ride]) + tuple(refs[13 + 2 * n_ride:])
        i = pl.program_id(0)

        @pl.when(i == 0)
        def _():
            _exchange_start(*ride_args)
            dgm_ref[...] = jnp.zeros_like(dgm_ref)

        @pl.when(i % per_seq == 0)
        def _():
            dm_ref[...] = jnp.zeros_like(dm_ref)

        dh = jnp.zeros((tm, D_MODEL), F32)
        for n, ref in enumerate((da_ref, dg_ref, dq_ref, dk_ref, dv_ref)):
            dh = dh + jnp.dot(ref[...], w_ref[pl.ds(n * width, width), :], preferred_element_type=F32)
        xv = x_ref[...]
        r = lax.rsqrt(jnp.mean(xv * xv, axis=-1, keepdims=True) + EPS)
        xhat = xv * r
        g = g_ref[...]
        dm_ref[0:1, :] += jnp.sum(dh, axis=0, keepdims=True)
        dm_ref[1:2, :] += jnp.sum(dh * (xhat * g), axis=0, keepdims=True)
        dn = dh * (1.0 + m_ref[1:2, :])
        dgm_ref[0:1, :] += jnp.sum(dn * xhat, axis=0, keepdims=True)
        dxh = dn * g
        gx_ref[...] = dx1_ref[...] + r * (dxh - xhat * jnp.mean(dxh * xhat, axis=-1, keepdims=True))

        @pl.when(i == tokens // tm - 1)
        def _():
            _exchange_wait(*ride_args)

    rows = pl.BlockSpec((tm, D_MODEL), lambda i: (i, 0))
    half = pl.BlockSpec((tm, width), lambda i: (i, 0))
    per = pl.BlockSpec((None, 8, D_MODEL), lambda i: (i // per_seq, 0, 0))
    return _call(
        body, name="mix_in_bwd", grid=(tokens // tm,),
        in_specs=[half] * 5 + [pl.BlockSpec((D_IN, D_MODEL), lambda i: (0, 0)), rows, rows, per,
                               pl.BlockSpec((1, D_MODEL), lambda i: (0, 0))] + [ANY_SPEC] * n_ride,
        out_specs=[rows, pl.BlockSpec((8, D_MODEL), lambda i: (0, 0)), per] + [ANY_SPEC] * n_ride,
        out_shape=[_sds((tokens, D_MODEL), F32), _sds((8, D_MODEL), F32), _sds((n_seq, 8, D_MODEL), F32)]
        + _exchange_shapes(ride),
        scratch_shapes=_exchange_sems(n_ride),
        compiler_params=_params(("arbitrary",)),
    )(*parts, w_in, x2, dx1, mod8, g_mix, *[a for a, _ in ride])


def _grad_matmul_parts(a_parts, b_parts, name, tk=1024):
    tokens = a_parts[0].shape[0]
    na, nb = len(a_parts), len(b_parts)
    ma, nbw = a_parts[0].shape[1], b_parts[0].shape[1]

    n_k = tokens // tk

    def body(*refs):
        a_refs, b_refs, o_ref, acc = refs[:na], refs[na:na + nb], refs[na + nb], refs[na + nb + 1]

        @pl.when(pl.program_id(0) == 0)
        def _():
            acc[...] = jnp.zeros_like(acc)

        for i in range(na):
            for j in range(nb):
                acc[pl.ds(i * ma, ma), pl.ds(j * nbw, nbw)] += lax.dot_general(
                    a_refs[i][...], b_refs[j][...], TN, preferred_element_type=F32)

        @pl.when(pl.program_id(0) == n_k - 1)
        def _():
            o_ref[...] = acc[...].astype(o_ref.dtype)

    return _call(
        body, name=name, grid=(n_k,),
        in_specs=[pl.BlockSpec((tk, ma), lambda k: (k, 0))] * na + [pl.BlockSpec((tk, nbw), lambda k: (k, 0))] * nb,
        out_specs=pl.BlockSpec((na * ma, nb * nbw), lambda k: (0, 0)),
        out_shape=_sds((na * ma, nb * nbw), BF16),
        scratch_shapes=[pltpu.VMEM((na * ma, nb * nbw), F32)],
        compiler_params=_params(("arbitrary",)),
    )(*a_parts, *b_parts)


def _grad_matmul_tiles(a, b, name, tk=1024):
    tiled_b = b.ndim == 3
    tiles, tokens, width = b.shape if tiled_b else a.shape
    other = a.shape[1] if tiled_b else b.shape[1]
    out_tile = (other, width) if tiled_b else (width, other)
    n_k = tokens // tk

    def body(a_ref, b_ref, o_ref, acc):
        @pl.when(pl.program_id(0) == 0)
        def _():
            acc[...] = jnp.zeros_like(acc)

        for t in range(tiles):
            lhs = a_ref[...] if tiled_b else a_ref[t]
            rhs = b_ref[t] if tiled_b else b_ref[...]
            acc[t] += lax.dot_general(lhs, rhs, TN, preferred_element_type=F32)

        @pl.when(pl.program_id(0) == n_k - 1)
        def _():
            o_ref[...] = acc[...].astype(o_ref.dtype)

    flat = pl.BlockSpec((tk, other), lambda k: (k, 0))
    tiled = pl.BlockSpec((tiles, tk, width), lambda k: (0, k, 0))
    return _call(
        body, name=name, grid=(n_k,),
        in_specs=[flat, tiled] if tiled_b else [tiled, flat],
        out_specs=pl.BlockSpec((tiles,) + out_tile, lambda k: (0, 0, 0)),
        out_shape=_sds((tiles,) + out_tile, BF16),
        scratch_shapes=[pltpu.VMEM((tiles,) + out_tile, F32)],
        compiler_params=_params(("arbitrary",)),
    )(a, b)


def _adamw(w, m, v, g, name, n_parts=0, tr=256):
    rows, cols = w.shape
    tr = min(tr, rows)
    c1 = 1.0 - ADAM_B1 ** ADAM_STEP
    c2 = 1.0 - ADAM_B2 ** ADAM_STEP

    def body(w_ref, m_ref, v_ref, g_ref, go_ref, d_ref, mo_ref, vo_ref):
        if n_parts:
            gv = g_ref[0].astype(F32)
            for p in range(1, n_parts):
                gv = gv + g_ref[p].astype(F32)
        else:
            gv = g_ref[...]
        go_ref[...] = gv
        mn = ADAM_B1 * m_ref[...] + (1.0 - ADAM_B1) * gv
        vn = ADAM_B2 * v_ref[...] + (1.0 - ADAM_B2) * (gv * gv)
        mo_ref[...] = mn
        vo_ref[...] = vn
        d_ref[...] = -ADAM_LR * ((mn / c1) / (jnp.sqrt(vn / c2) + ADAM_EPS) + ADAM_WD * w_ref[...])

    blk = pl.BlockSpec((tr, cols), lambda i: (i, 0))
    g_spec = pl.BlockSpec((n_parts, tr, cols), lambda i: (0, i, 0)) if n_parts else blk
    return _call(
        body, name=name, grid=(rows // tr,),
        in_specs=[blk, blk, blk, g_spec], out_specs=[blk] * 4,
        out_shape=[_sds((rows, cols), F32)] * 4,
        compiler_params=_params(("parallel",)),
    )(w, m, v, g)


def _cols_to_full(blocks):
    n, r, c = blocks.shape
    return jnp.transpose(blocks, (1, 0, 2)).reshape(r, n * c)


def _pad_lanes(v, width):
    return jnp.pad(v, ((0, 0), (0, width - v.shape[1])))


def kernel(x, c, w_ada, b_ada, g_mix, w_in, w_dw, b_dw, g_conv_ln, b_conv_ln, g_q, g_k, w_out, g_ffn, w_gate, w_up, w_down, loss_target, m_w_ada, m_b_ada, m_g_mix, m_w_in, m_w_dw, m_b_dw, m_g_conv_ln, m_b_conv_ln, m_g_q, m_g_k, m_w_out, m_g_ffn, m_w_gate, m_w_up, m_w_down, v_w_ada, v_b_ada, v_g_mix, v_w_in, v_w_dw, v_b_dw, v_g_conv_ln, v_b_conv_ln, v_g_q, v_g_k, v_w_out, v_g_ffn, v_w_gate, v_w_up, v_w_down):
    n_seq, seq, _ = x.shape
    tokens = n_seq * seq
    me = 4 * lax.axis_index("x") + 2 * lax.axis_index("y") + lax.axis_index("c")
    ada_cols = w_ada.shape[2]
    dw_cols = w_dw.shape[2]

    def transposed(w):
        return jnp.transpose(w[0])

    (c_g, w_in_g, w_dw_g) = _gather_by_chip([c, transposed(w_in).astype(BF16), w_dw[0]], "gather_weights")
    c_all = c_g.reshape(N_DEV * n_seq, D_MODEL)
    w_in_t = w_in_g.reshape(D_IN, D_MODEL)
    w_dw_f = _cols_to_full(w_dw_g)

    b_cols = lax.dynamic_slice(b_ada, (0, me * ada_cols), (1, ada_cols))
    mod_cols = _ada_fwd(c_all, w_ada[0], b_cols)
    (mod_g,) = _exchange([(mod_cols, False)], "gather_mod")
    mod_mine = lax.dynamic_slice(mod_g, (0, me * n_seq, 0), (N_DEV, n_seq, ada_cols))
    mod = jnp.transpose(mod_mine, (1, 0, 2)).reshape(n_seq, N_MOD, D_MODEL)
    mod8 = jnp.pad(mod, ((0, 0), (0, 8 - N_MOD), (0, 0)))

    x2 = x.reshape(tokens, D_MODEL)
    h1, proj = _mix_in(x2, mod8, g_mix, w_in_t, seq)
    proj3 = proj.reshape(D_IN // LANES, n_seq, seq, LANES)
    uc3 = _conv_fwd(proj3, w_dw_f, b_dw)
    g_q2, g_k2 = jnp.tile(g_q, (1, 2)), jnp.tile(g_k, (1, 2))
    y_att3, lse3, w_out_g, w_gate_g, w_up_g, w_down_g = _attn_fwd(
        proj3, g_q2, g_k2,
        [(w_out[0].astype(BF16), False), (transposed(w_gate).astype(BF16), False),
         (transposed(w_up).astype(BF16), False), (w_down[0].astype(BF16), False)])
    w_out_f = w_out_g.reshape(D_MODEL, D_MODEL)
    w_gate_f = w_gate_g.reshape(D_FF, D_MODEL)
    w_up_f = w_up_g.reshape(D_FF, D_MODEL)
    w_down_f = w_down_g.reshape(D_FF, D_MODEL)
    uc2 = uc3.reshape(tokens, D_CONV)
    y_att2 = y_att3.reshape(tokens, D_ATT)
    y_conv, mix, x1, h2 = _mix_out(uc2, y_att2, x2, mod8, g_conv_ln, b_conv_ln, g_ffn, w_out_f, seq)
    gate, up, dy, df, sq, dgate_f = _ffn_fwd(
        h2, w_gate_f, w_up_f, w_down_f, x1, loss_target.reshape(tokens, D_MODEL), mod8, seq)

    dgate, dup, act, dx1, dmix, dg_ffn, dmod_f = _ffn_bwd(
        df, gate, up, w_gate_f, w_up_f, w_down_f, x1, dy, mix, mod8, g_ffn, seq)
    duc2, do2, dgb_ln = _mix_out_bwd(dmix, uc2, g_conv_ln, b_conv_ln, w_out_f)
    d_a3, d_g3, dw_dw_p, db_dw_p = _conv_bwd(duc2.reshape(n_seq, seq, D_CONV), proj3, w_dw_f)
    gw_gate = _grad_matmul_tiles(dgate, h2, "grad_w_gate")
    gw_up = _grad_matmul_tiles(dup, h2, "grad_w_up")
    gw_down = _grad_matmul_tiles(act, df, "grad_w_down")
    gw_out = _grad_matmul_parts([y_conv, y_att2], [dmix], "grad_w_out")
    d_q3, d_k3, d_v3, dg_qk, p_gate, p_up, p_down, p_out = _attn_bwd(
        proj3, do2.reshape(n_seq, seq, D_ATT), y_att3, lse3, g_q2, g_k2,
        [(gw_gate.reshape(N_DEV, D_FF // N_DEV, D_MODEL), True), (gw_up.reshape(N_DEV, D_FF // N_DEV, D_MODEL), True),
         (gw_down.reshape(N_DEV, D_FF // N_DEV, D_MODEL), True),
         (gw_out.reshape(N_DEV, D_MODEL // N_DEV, D_MODEL), True)])
    flat = lambda t: t.reshape(tokens, t.shape[-1])
    d_a, d_g, d_q, d_k, d_v = flat(d_a3), flat(d_g3), flat(d_q3), flat(d_k3), flat(d_v3)
    gw_in = _grad_matmul_parts([d_a, d_g, d_q, d_k, d_v], [h1], "grad_w_in")
    grad_x2, dg_mix, dmod_m, p_in = _mix_in_bwd(
        d_a, d_g, d_q, d_k, d_v, w_in_t, x2, dx1, mod8, g_mix, seq,
        [(gw_in.reshape(N_DEV, D_IN // N_DEV, D_MODEL), True)])

    dmod = jnp.concatenate([dmod_m[:, 0], dmod_m[:, 1], dmod_f[:, 2], dmod_f[:, 0], dmod_f[:, 1], dgate_f[:, 0]], axis=1)
    dg_q = dg_qk[0:1, 0:HEAD_DIM] + dg_qk[0:1, HEAD_DIM:]
    dg_k = dg_qk[1:2, 0:HEAD_DIM] + dg_qk[1:2, HEAD_DIM:]
    loss_part = (0.5 / D_MODEL) * jnp.sum(sq[0:1, :], axis=1, keepdims=True)
    small = jnp.concatenate(
        [dg_mix[0:1], dg_ffn[0:1], db_dw_p[0:1], dgb_ln[0:1], dgb_ln[1:2],
         _pad_lanes(dg_q, LANES), _pad_lanes(dg_k, LANES), _pad_lanes(loss_part, LANES)], axis=1)
    n_small = small.shape[1] - LANES

    (dmod_g, small_g, dw_g) = _exchange([(dmod, False), (small, False), (dw_dw_p, False)], "gather_small_grads")

    dmod_all = dmod_g.reshape(N_DEV * n_seq, N_MOD * D_MODEL)
    dmod_cols = lax.dynamic_slice(dmod_all, (0, me * ada_cols), (N_DEV * n_seq, ada_cols))
    gw_ada, gb_ada = _ada_bwd(c_all, dmod_cols, dmod_all)

    res = {}
    res["w_ada"] = _adamw(w_ada[0], m_w_ada[0], v_w_ada[0], gw_ada, "adamw_w_ada")
    res["b_ada"] = _adamw(b_ada, m_b_ada, v_b_ada, gb_ada, "adamw_b_ada")
    def adamw_transposed(w, m, v, parts, name, tr):
        outs = _adamw(transposed(w), transposed(m), transposed(v), parts, name, N_DEV, tr=tr)
        return tuple(jnp.transpose(o) for o in outs)

    res["w_in"] = adamw_transposed(w_in, m_w_in, v_w_in, p_in, "adamw_w_in", 160)
    res["w_out"] = _adamw(w_out[0], m_w_out[0], v_w_out[0], p_out, "adamw_w_out", N_DEV)
    res["w_gate"] = adamw_transposed(w_gate, m_w_gate, v_w_gate, p_gate, "adamw_w_gate", 176)
    res["w_up"] = adamw_transposed(w_up, m_w_up, v_w_up, p_up, "adamw_w_up", 176)
    res["w_down"] = _adamw(w_down[0], m_w_down[0], v_w_down[0], p_down, "adamw_w_down", N_DEV, tr=176)
    dw_mine = lax.dynamic_slice(dw_g, (0, 0, me * dw_cols), (N_DEV, CONV_WIDTH, dw_cols))
    res["w_dw"] = _adamw(w_dw[0], m_w_dw[0], v_w_dw[0], dw_mine, "adamw_w_dw", N_DEV)

    small_names = ["g_mix", "g_ffn", "b_dw", "g_conv_ln", "b_conv_ln", "g_q", "g_k"]
    small_w = {"g_mix": (g_mix, m_g_mix, v_g_mix), "g_ffn": (g_ffn, m_g_ffn, v_g_ffn), "b_dw": (b_dw, m_b_dw, v_b_dw),
               "g_conv_ln": (g_conv_ln, m_g_conv_ln, v_g_conv_ln), "b_conv_ln": (b_conv_ln, m_b_conv_ln, v_b_conv_ln),
               "g_q": (g_q, m_g_q, v_g_q), "g_k": (g_k, m_g_k, v_g_k)}
    widths = [max(small_w[n][0].shape[1], LANES) for n in small_names]
    packed = [jnp.concatenate([_pad_lanes(small_w[n][i], wd) for n, wd in zip(small_names, widths)], axis=1) for i in range(3)]
    outs = _adamw(packed[0], packed[1], packed[2], small_g[:, :, :n_small], "adamw_small", N_DEV)
    off = 0
    for n, wd in zip(small_names, widths):
        real = small_w[n][0].shape[1]
        res[n] = tuple(o[:, off:off + real] for o in outs)
        off += wd
    loss = jnp.sum(small_g[:, 0, n_small])

    order = ["w_ada", "b_ada", "g_mix", "w_in", "w_dw", "b_dw", "g_conv_ln", "b_conv_ln", "g_q", "g_k",
             "w_out", "g_ffn", "w_gate", "w_up", "w_down"]
    lead = {"w_ada", "w_in", "w_dw", "w_out", "w_gate", "w_up", "w_down"}
    grads, deltas, new_m, new_v = [], [], [], []
    for n in order:
        g, d, mn, vn = res[n]
        g, d, mn, vn = (t[None] if n in lead else t for t in (g, d, mn, vn))
        grads.append(g)
        deltas.append(d)
        new_m.append(mn)
        new_v.append(vn)
    return (loss, grad_x2.reshape(n_seq, seq, D_MODEL), *grads, *deltas, *new_m, *new_v)
```

```python
import numpy as np
import jax
import jax.numpy as jnp
from jax import lax
from jax.experimental import pallas as pl
from jax.experimental.pallas import tpu as pltpu

F32 = jnp.float32
BF16 = jnp.bfloat16

N_DEV = 8
D_MODEL = 1024
D_CONV = 512
D_ATT = 512
HEAD_DIM = 64
CONV_WIDTH = 31
D_IN = 2 * D_CONV + 3 * D_ATT
D_FF = 2816
N_MOD = 6
EPS = 1e-6
RADIUS = 64
DILATIONS = (1, 4, 16)
Q_BLOCK = 128
LANES = 128
VMEM_LIMIT = 56 * 1024 * 1024

ADAM_LR = 0.001
ADAM_B1 = 0.9
ADAM_B2 = 0.999
ADAM_EPS = 1e-08
ADAM_WD = 0.01
ADAM_STEP = 10

NT = (((1,), (1,)), ((), ()))
TN = (((0,), (0,)), ((), ()))


def _call(body, **kw):
    return pl.pallas_call(body, **kw)


def _params(sem=None, vmem=VMEM_LIMIT):
    return pltpu.CompilerParams(dimension_semantics=sem, vmem_limit_bytes=vmem)


def _sig(x):
    return 1.0 / (1.0 + jnp.exp(-x))


def _sds(shape, dtype):
    return jax.ShapeDtypeStruct(shape, dtype)


N_PEER = N_DEV - 1
ANY_SPEC = pl.BlockSpec(memory_space=pl.ANY)


def _exchange_copies(scatter, ins, outs, *sems):
    n = len(ins)
    if n == 0:
        return [], []
    send_sems, recv_sems, local_sems = sems
    x, y, c = lax.axis_index("x"), lax.axis_index("y"), lax.axis_index("c")
    me = 4 * x + 2 * y + c

    def src(a, slot):
        return ins[a].at[slot] if scatter[a] else ins[a]

    local = [pltpu.make_async_copy(src(a, me), outs[a].at[me], local_sems.at[a]) for a in range(n)]
    flights = []
    for k in range(1, N_DEV):
        px = 1 - x if k & 4 else x
        py = 1 - y if k & 2 else y
        pc = 1 - c if k & 1 else c
        pid = 4 * px + 2 * py + pc
        for a in range(n):
            i = a * N_PEER + k - 1
            send, recv = (pltpu.make_async_remote_copy(
                src_ref=src(a, pid), dst_ref=outs[a].at[slot],
                send_sem=send_sems.at[i], recv_sem=recv_sems.at[i],
                device_id=(px, py, pc), device_id_type=pl.DeviceIdType.MESH) for slot in (me, pid))
            flights.append((send, recv))
    return local, flights


def _exchange_start(*args):
    local, flights = _exchange_copies(*args)
    for cp in local:
        cp.start()
    for send, _ in flights:
        send.start()


def _exchange_wait(*args):
    local, flights = _exchange_copies(*args)
    for send, recv in flights:
        send.wait_send()
        recv.wait_recv()
    for cp in local:
        cp.wait()


def _exchange_shapes(items):
    return [_sds((N_DEV,) + tuple(arr.shape[1:] if scatter else arr.shape), arr.dtype) for arr, scatter in items]


def _exchange_sems(n):
    if n == 0:
        return []
    return [pltpu.SemaphoreType.DMA((n * N_PEER,)), pltpu.SemaphoreType.DMA((n * N_PEER,)),
            pltpu.SemaphoreType.DMA((n,))]


def _gather_by_chip_phase(phase, ins, outs, send_sems, recv_sems, local_sems):
    n = len(ins)
    per = N_PEER
    x, y, c = lax.axis_index("x"), lax.axis_index("y"), lax.axis_index("c")
    me, sibling = (x, y, c), (x, y, 1 - c)
    chips = [(1 - x, y), (x, 1 - y), (1 - x, 1 - y)]

    def slot(px, py, pc):
        return 4 * px + 2 * py + pc

    def copy(a, k, block, to, src=None):
        dst = outs[a].at[slot(*block)]
        return pltpu.make_async_remote_copy(
            src_ref=dst if src is None else src, dst_ref=dst,
            send_sem=send_sems.at[a * per + k], recv_sem=recv_sems.at[a * per + k],
            device_id=to, device_id_type=pl.DeviceIdType.MESH)

    local = [pltpu.make_async_copy(ins[a], outs[a].at[slot(*me)], local_sems.at[a]) for a in range(n)]
    first = []
    for a in range(n):
        first.append(copy(a, 0, me, sibling, src=ins[a]))
        first += [copy(a, 1 + j, me, (*chip, c), src=ins[a]) for j, chip in enumerate(chips)]
    passed = [copy(a, 4 + j, (*chip, c), sibling) for j, chip in enumerate(chips) for a in range(n)]
    if phase == 0:
        for cp in local + first:
            cp.start()
    elif phase == 1:
        for j, chip in enumerate(chips):
            for a in range(n):
                copy(a, 1 + j, (*chip, c), me).wait_recv()
        for cp in passed:
            cp.start()
    else:
        for a in range(n):
            copy(a, 0, sibling, me).wait_recv()
            for j, chip in enumerate(chips):
                copy(a, 4 + j, (*chip, 1 - c), me).wait_recv()
        for cp in first + passed:
            cp.wait_send()
        for cp in local:
            cp.wait()


def _gather_by_chip(arrays, name):
    n = len(arrays)

    def body(*refs):
        for phase in range(3):
            _gather_by_chip_phase(phase, refs[:n], refs[n:2 * n], *refs[2 * n:])

    return _call(
        body, name=name, out_shape=_exchange_shapes([(arr, False) for arr in arrays]),
        in_specs=[ANY_SPEC] * n, out_specs=[ANY_SPEC] * n, scratch_shapes=_exchange_sems(n),
    )(*arrays)


def _exchange(items, name):
    n = len(items)
    scatter = [s for _, s in items]

    def body(*refs):
        args = (scatter, refs[:n], refs[n:2 * n]) + tuple(refs[2 * n:])
        _exchange_start(*args)
        _exchange_wait(*args)

    return _call(
        body, name=name, out_shape=_exchange_shapes(items),
        in_specs=[ANY_SPEC] * n, out_specs=[ANY_SPEC] * n, scratch_shapes=_exchange_sems(n),
    )(*[a for a, _ in items])


def _ada_fwd(c_all, w_ada, b_cols):
    def body(c_ref, w_ref, b_ref, o_ref):
        cv = c_ref[...]
        sc = (cv * _sig(cv)).astype(BF16)
        o_ref[...] = jnp.dot(sc, w_ref[...].astype(BF16), preferred_element_type=F32) + b_ref[...]

    return _call(body, name="ada_fwd", out_shape=_sds((c_all.shape[0], w_ada.shape[1]), F32),
                 compiler_params=_params())(c_all, w_ada, b_cols)


def _ada_bwd(c_all, dmod_cols, dmod_all):
    def body(c_ref, dc_ref, da_ref, gw_ref, gb_ref):
        cv = c_ref[...]
        sc = (cv * _sig(cv)).astype(BF16)
        gw_ref[...] = lax.dot_general(sc, dc_ref[...].astype(BF16), TN, preferred_element_type=F32)
        gb_ref[...] = jnp.sum(da_ref[...], axis=0, keepdims=True)

    return _call(body, name="ada_bwd",
                 out_shape=[_sds((c_all.shape[1], dmod_cols.shape[1]), F32), _sds((1, dmod_all.shape[1]), F32)],
                 compiler_params=_params())(c_all, dmod_cols, dmod_all)


MIX_ROWS = 128


def _mix_in(x2, mod8, g_mix, w_in, seq, tm=512):
    tokens = x2.shape[0]
    per_seq = seq // tm

    def body(x_ref, m_ref, g_ref, wt_ref, h_ref, p_ref, w_ref):
        @pl.when(pl.program_id(0) == 0)
        def _():
            w_ref[...] = wt_ref[...].T

        def normed(c):
            rows = pl.ds(c * MIX_ROWS, MIX_ROWS)
            xv = x_ref[rows, :]
            r = lax.rsqrt(jnp.mean(xv * xv, axis=-1, keepdims=True) + EPS)
            hb = ((xv * r * g_ref[...]) * (1.0 + m_ref[1:2, :]) + m_ref[0:1, :]).astype(BF16)
            h_ref[rows, :] = hb
            return hb

        ahead = normed(0)
        for c in range(tm // MIX_ROWS):
            hb = ahead
            if c + 1 < tm // MIX_ROWS:
                ahead = normed(c + 1)
            p = jnp.dot(hb, w_ref[...], preferred_element_type=F32)
            for cb in range(D_IN // LANES):
                p_ref[cb, pl.ds(c * MIX_ROWS, MIX_ROWS), :] = p[:, cb * LANES:(cb + 1) * LANES]

    return _call(
        body, name="mix_in", grid=(tokens // tm,),
        in_specs=[pl.BlockSpec((tm, D_MODEL), lambda i: (i, 0)),
                  pl.BlockSpec((None, 8, D_MODEL), lambda i: (i // per_seq, 0, 0)),
                  pl.BlockSpec((1, D_MODEL), lambda i: (0, 0)),
                  pl.BlockSpec((D_IN, D_MODEL), lambda i: (0, 0))],
        out_specs=[pl.BlockSpec((tm, D_MODEL), lambda i: (i, 0)),
                   pl.BlockSpec((D_IN // LANES, tm, LANES), lambda i: (0, i, 0))],
        out_shape=[_sds((tokens, D_MODEL), BF16), _sds((D_IN // LANES, tokens, LANES), F32)],
        scratch_shapes=[pltpu.VMEM((D_MODEL, D_IN), BF16)],
        compiler_params=_params(("arbitrary",)),
    )(x2, mod8, g_mix, w_in)


CONV_ROWS = 64
CONV_DW_ROWS = 32
CONV_DW_UNROLL = 4
CONV_HALO = 16


def _fill_shifted(xp, sh, seq):
    for b in range(8):
        sh[b, pl.ds(0, seq + 24), :] = xp[pl.ds(b, seq + 24), :]


def _conv_fwd(proj3, w_dw, b_dw):
    _, n_seq, seq, _ = proj3.shape
    n_cb = D_CONV // LANES

    def body(a_ref, g_ref, w_ref, b_ref, uc_ref, xp, sh):
        zeros = jnp.zeros((CONV_HALO, LANES), F32)
        xp[pl.ds(0, CONV_HALO), :] = zeros
        xp[pl.ds(CONV_HALO + seq, CONV_HALO), :] = zeros
        xp[pl.ds(CONV_HALO, seq), :] = a_ref[...] * _sig(g_ref[...])
        _fill_shifted(xp, sh, seq)

        def blk(i, carry):
            t0 = pl.multiple_of(i * CONV_ROWS, CONV_ROWS)
            acc = jnp.zeros((CONV_ROWS, LANES), F32)
            for j in range(CONV_WIDTH):
                jj = j + 1
                acc = acc + sh[jj % 8, pl.ds(t0 + 8 * (jj // 8), CONV_ROWS), :] * w_ref[j:j + 1, :]
            uc_ref[pl.ds(t0, CONV_ROWS), :] = acc + b_ref[...]
            return carry

        lax.fori_loop(0, seq // CONV_ROWS, blk, 0)

    return _call(
        body, name="conv_fwd", grid=(n_seq, n_cb),
        in_specs=[pl.BlockSpec((None, None, seq, LANES), lambda b, cb: (cb, b, 0, 0)),
                  pl.BlockSpec((None, None, seq, LANES), lambda b, cb: (n_cb + cb, b, 0, 0)),
                  pl.BlockSpec((CONV_WIDTH, LANES), lambda b, cb: (0, cb)),
                  pl.BlockSpec((1, LANES), lambda b, cb: (0, cb))],
        out_specs=pl.BlockSpec((None, seq, LANES), lambda b, cb: (b, 0, cb)),
        out_shape=_sds((n_seq, seq, D_CONV), F32),
        scratch_shapes=[pltpu.VMEM((seq + 2 * CONV_HALO, LANES), F32),
                        pltpu.VMEM((8, seq + 2 * CONV_HALO, LANES), F32)],
        compiler_params=_params(("parallel", "parallel")),
    )(proj3, proj3, w_dw, b_dw)


def _conv_bwd(duc3, proj3, w_dw):
    _, n_seq, seq, _ = proj3.shape
    n_cb = D_CONV // LANES

    def body(duc_ref, a_ref, g_ref, w_ref, da_ref, dg_ref, dw_ref, db_ref, xp, sh):
        @pl.when(pl.program_id(1) == 0)
        def _():
            dw_ref[...] = jnp.zeros_like(dw_ref)
            db_ref[...] = jnp.zeros_like(db_ref)

        zeros = jnp.zeros((CONV_HALO, LANES), F32)
        xp[pl.ds(0, CONV_HALO), :] = zeros
        xp[pl.ds(CONV_HALO + seq, CONV_HALO), :] = zeros
        xp[pl.ds(CONV_HALO, seq), :] = a_ref[...] * _sig(g_ref[...])
        _fill_shifted(xp, sh, seq)
        for j0 in range(0, CONV_WIDTH, 8):
            taps = range(j0, min(j0 + 8, CONV_WIDTH))

            def wblk(i, accs, taps=taps):
                for u in range(CONV_DW_UNROLL):
                    t0 = pl.multiple_of((i * CONV_DW_UNROLL + u) * CONV_DW_ROWS, CONV_DW_ROWS)
                    d = duc_ref[pl.ds(t0, CONV_DW_ROWS), :]
                    accs = tuple(acc + d * sh[(j + 1) % 8, pl.ds(t0 + 8 * ((j + 1) // 8), CONV_DW_ROWS), :]
                                 for acc, j in zip(accs, taps))
                return accs

            accs = lax.fori_loop(0, seq // (CONV_DW_ROWS * CONV_DW_UNROLL), wblk,
                                 tuple(jnp.zeros((CONV_DW_ROWS, LANES), F32) for _ in taps))
            for acc, j in zip(accs, taps):
                dw_ref[j:j + 1, :] += jnp.sum(acc, axis=0, keepdims=True)
        db_ref[0:1, :] += jnp.sum(duc_ref[...], axis=0, keepdims=True)
        xp[pl.ds(CONV_HALO, seq), :] = duc_ref[...]
        _fill_shifted(xp, sh, seq)

        def ublk(i, carry):
            t0 = pl.multiple_of(i * CONV_ROWS, CONV_ROWS)
            acc = jnp.zeros((CONV_ROWS, LANES), F32)
            for j in range(CONV_WIDTH):
                jj = CONV_WIDTH - j
                acc = acc + sh[jj % 8, pl.ds(t0 + 8 * (jj // 8), CONV_ROWS), :] * w_ref[j:j + 1, :]
            av = a_ref[pl.ds(t0, CONV_ROWS), :]
            sg = _sig(g_ref[pl.ds(t0, CONV_ROWS), :])
            da_ref[pl.ds(t0, CONV_ROWS), :] = (acc * sg).astype(BF16)
            dg_ref[pl.ds(t0, CONV_ROWS), :] = (acc * av * sg * (1.0 - sg)).astype(BF16)
            return carry

        lax.fori_loop(0, seq // CONV_ROWS, ublk, 0)

    return _call(
        body, name="conv_bwd", grid=(n_cb, n_seq),
        in_specs=[pl.BlockSpec((None, seq, LANES), lambda cb, b: (b, 0, cb)),
                  pl.BlockSpec((None, None, seq, LANES), lambda cb, b: (cb, b, 0, 0)),
                  pl.BlockSpec((None, None, seq, LANES), lambda cb, b: (n_cb + cb, b, 0, 0)),
                  pl.BlockSpec((CONV_WIDTH, LANES), lambda cb, b: (0, cb))],
        out_specs=[pl.BlockSpec((None, seq, LANES), lambda cb, b: (b, 0, cb)),
                   pl.BlockSpec((None, seq, LANES), lambda cb, b: (b, 0, cb)),
                   pl.BlockSpec((32, LANES), lambda cb, b: (0, cb)),
                   pl.BlockSpec((8, LANES), lambda cb, b: (0, cb))],
        out_shape=[_sds((n_seq, seq, D_CONV), BF16), _sds((n_seq, seq, D_CONV), BF16),
                   _sds((32, D_CONV), F32), _sds((8, D_CONV), F32)],
        scratch_shapes=[pltpu.VMEM((seq + 2 * CONV_HALO, LANES), F32),
                        pltpu.VMEM((8, seq + 2 * CONV_HALO, LANES), F32)],
        compiler_params=_params(("parallel", "arbitrary")),
    )(duc3, proj3, proj3, w_dw)


MASKED = 1e30
ATT_ROWS = 512
ATT_UNROLL = 8
ATT_FWD_UNROLL = 8


def _distance_mats(dil, seg_len):
    kw = min(2 * Q_BLOCK, seg_len)
    offsets = (0, -RADIUS, -2 * RADIUS) if kw == 2 * Q_BLOCK else (0,)
    a = np.arange(Q_BLOCK)[:, None]
    b = np.arange(kw)[None, :]
    mats = []
    for off in offsets:
        rel = np.abs(b + off - a)
        mats.append(np.where(rel <= RADIUS, dil * rel, MASKED))
    return jnp.asarray(np.stack(mats).astype(np.float32))


def _alibi_rows():
    s = np.zeros((4, 8, LANES), np.float32)
    for hp in range(4):
        for hl in range(2):
            s[hp, hl, :] = 2.0 ** (-(2 * hp + hl + 1))
    return jnp.asarray(s)


def _window(n, seg_len):
    i0 = pl.multiple_of(n * Q_BLOCK, Q_BLOCK)
    if seg_len <= Q_BLOCK:
        return i0, i0, 0
    per_seg = seg_len // Q_BLOCK
    j = n % per_seg
    seg0 = (n // per_seg) * seg_len
    ks_local = jnp.clip(j * Q_BLOCK - RADIUS, 0, seg_len - 2 * Q_BLOCK)
    ks = pl.multiple_of(seg0 + ks_local, RADIUS)
    var = jnp.where(j == 0, 0, jnp.where(j == per_seg - 1, 2, 1))
    return i0, ks, var


def _first_head(rows):
    return lax.broadcasted_iota(jnp.int32, (rows, LANES), 1) < HEAD_DIM


def _same_head():
    head = np.arange(LANES) // HEAD_DIM
    return jnp.asarray((head[:, None] == head[None, :]).astype(np.float32)).astype(BF16)


def _head_sum(x, same_ref):
    hi = x.astype(BF16)
    lo = (x - hi.astype(F32)).astype(BF16)
    return (jnp.dot(hi, same_ref[...], preferred_element_type=F32)
            + jnp.dot(lo, same_ref[...], preferred_element_type=F32))


def _head_mean(x, same_ref):
    return _head_sum(x, same_ref) * (1.0 / HEAD_DIM)


def _per_head(x, first):
    swapped = pltpu.roll(x, HEAD_DIM, 1)
    return jnp.where(first, x, swapped), jnp.where(first, swapped, x)


STRIDE = 4


def _gather_segments(src, dil, seq, tmp, put):
    if dil == 1:
        put(0, seq, src[pl.ds(0, seq), :])
    elif dil == STRIDE:
        seg = seq // dil
        for r in range(dil):
            put(r * seg, seg, src[pl.ds(r, seg, stride=dil), :])
    else:
        part, seg = seq // STRIDE, seq // dil
        for b in range(STRIDE):
            tmp[pl.ds(b * part, part), :] = src[pl.ds(b, part, stride=STRIDE), :]
        for b in range(STRIDE):
            for a in range(dil // STRIDE):
                put(b * part + a * seg, seg, tmp[pl.ds(b * part + a, seg, stride=dil // STRIDE), :])


def _scatter_segments(dst, get, dil, seq, tmp, accumulate):
    def write(rows, val):
        if accumulate:
            dst[rows, :] += val
        else:
            dst[rows, :] = val

    if dil == 1:
        write(pl.ds(0, seq), get(0, seq))
    elif dil == STRIDE:
        seg = seq // dil
        for r in range(dil):
            write(pl.ds(r, seg, stride=dil), get(r * seg, seg))
    else:
        part, seg = seq // STRIDE, seq // dil
        for b in range(STRIDE):
            for a in range(dil // STRIDE):
                tmp[pl.ds(b * part + a, seg, stride=dil // STRIDE), :] = get(b * part + a * seg, seg)
        for b in range(STRIDE):
            write(pl.ds(b, part, stride=STRIDE), tmp[pl.ds(b * part, part), :])


def _permute_rows(dst, src, dil, seq, tmp):
    def put(start, size, val):
        dst[pl.ds(start, size), :] = val.astype(dst.dtype)

    _gather_segments(src, dil, seq, tmp, put)


def _permute_rows_by_head(dst, src, dil, seq, tmp):
    def put(start, size, val):
        first = _first_head(size)
        dst[0, pl.ds(start, size), :] = jnp.where(first, val, 0.0).astype(dst.dtype)
        dst[1, pl.ds(start, size), :] = jnp.where(first, 0.0, val).astype(dst.dtype)

    _gather_segments(src, dil, seq, tmp, put)


def _qk_normalise(q_ref, g2_ref, same_ref, dst, seq, scale):
    def chunk(ci, carry):
        rows = pl.ds(pl.multiple_of(ci * ATT_ROWS, ATT_ROWS), ATT_ROWS)
        qv = q_ref[rows, :]
        r = lax.rsqrt(_head_mean(qv * qv, same_ref) + EPS)
        dst[rows, :] = qv * r * (g2_ref[...] * scale)
        return carry

    lax.fori_loop(0, seq // ATT_ROWS, chunk, 0)


def _attn_fwd(proj3, g_q2, g_k2, ride):
    _, n_seq, seq, _ = proj3.shape
    dms = [_distance_mats(d, seq // d) for d in DILATIONS]
    same = _same_head()
    col0 = 2 * D_CONV // LANES
    n_hp = D_ATT // LANES

    n_ride = len(ride)
    assert not any(scatter for _, scatter in ride), "the forward's ride is an all-gather"

    def body(*refs):
        q_ref, k_ref, v_ref, gq_ref, gk_ref, sl_ref, dm1, dm4, dm16, same_ref = refs[:10]
        ride_in = refs[10:10 + n_ride]
        y_ref, lse_ref = refs[10 + n_ride:12 + n_ride]
        ride_out = refs[12 + n_ride:12 + 2 * n_ride]
        (qf, kf, qp, kp, vp, oml_p, o1, o4, o16, m1, m4, m16, l1, l4, l16,
         tmp) = refs[12 + 2 * n_ride:28 + 2 * n_ride]
        o_nat, m_nat, l_nat = (o1, o4, o16), (m1, m4, m16), (l1, l4, l16)
        ride_args = (ride_in, ride_out) + tuple(refs[28 + 2 * n_ride:])
        step = pl.program_id(0) * n_hp + pl.program_id(1)
        n_steps = n_seq * n_hp

        if n_ride:
            for phase, at in enumerate((0, (3 * n_steps) // 4)):
                @pl.when(step == at)
                def _(phase=phase):
                    _gather_by_chip_phase(phase, *ride_args)

        dm_refs = (dm1, dm4, dm16)
        _qk_normalise(q_ref, gq_ref, same_ref, qf, seq, HEAD_DIM ** -0.5)
        _qk_normalise(k_ref, gk_ref, same_ref, kf, seq, 1.0)
        slopes = (sl_ref[0:1, 0:1], sl_ref[1:2, 0:1])
        for pi, dil in enumerate(DILATIONS):
            seg = seq // dil
            kw = min(2 * Q_BLOCK, seg)
            _permute_rows_by_head(qp, qf, dil, seq, tmp)
            _permute_rows(kp, kf, dil, seq, tmp)
            _permute_rows(vp, v_ref, dil, seq, tmp)

            def blk(it, carry, seg=seg, kw=kw, pi=pi, dst=oml_p):
                first = _first_head(Q_BLOCK)
                chains = [(sub, h) for sub in range(ATT_FWD_UNROLL) for h in range(2)]
                win = [_window(it * ATT_FWD_UNROLL + sub, seg) for sub in range(ATT_FWD_UNROLL)]
                s = {}
                for sub, h in chains:
                    i0, ks, var = win[sub]
                    s[sub, h] = lax.dot_general(qp[h, pl.ds(i0, Q_BLOCK), :], kp[pl.ds(ks, kw), :], NT,
                                                preferred_element_type=F32) - slopes[h] * dm_refs[pi][var]
                m, l, p = {}, {}, {}
                for c in chains:
                    m[c] = jnp.max(s[c], axis=1, keepdims=True)
                    e = jnp.exp(s[c] - m[c])
                    l[c] = jnp.sum(e, axis=1, keepdims=True)
                    p[c] = e.astype(BF16)
                o = {}
                for sub, h in chains:
                    o[sub, h] = jnp.dot(p[sub, h], vp[pl.ds(win[sub][1], kw), :], preferred_element_type=F32)
                packed = [jnp.concatenate([jnp.where(first, t[sub, 0], t[sub, 1]) for t in (o, m, l)], axis=1)
                          for sub in range(ATT_FWD_UNROLL)]
                span = ATT_FWD_UNROLL * Q_BLOCK
                dst[pl.ds(pl.multiple_of(it * span, span), span), :] = jnp.concatenate(packed, axis=0)
                return carry

            lax.fori_loop(0, seq // (Q_BLOCK * ATT_FWD_UNROLL), blk, 0)
            for n, nat in enumerate((o_nat[pi], m_nat[pi], l_nat[pi])):
                _scatter_segments(nat, lambda start, size, n=n: oml_p[pl.ds(start, size), pl.ds(n * LANES, LANES)],
                                  dil, seq, tmp, accumulate=False)

        def merge(ci, carry):
            rows = pl.ds(pl.multiple_of(ci * ATT_ROWS, ATT_ROWS), ATT_ROWS)
            ms = [m_nat[pi][rows, :] for pi in range(3)]
            m_all = jnp.maximum(jnp.maximum(ms[0], ms[1]), ms[2])
            es = [jnp.exp(m - m_all) for m in ms]
            l_all = sum(l_nat[pi][rows, :] * es[pi] for pi in range(3))
            inv = 1.0 / l_all
            o = sum(o_nat[pi][rows, :] * (es[pi] * inv) for pi in range(3))
            y_ref[rows, :] = o.astype(BF16)
            lse_ref[rows, :] = m_all + jnp.log(l_all)
            return carry

        lax.fori_loop(0, seq // ATT_ROWS, merge, 0)

        if n_ride:
            @pl.when(step == n_steps - 1)
            def _():
                _gather_by_chip_phase(2, *ride_args)

    def col(off):
        return pl.BlockSpec((None, None, seq, LANES), lambda b, hp: (col0 + off * n_hp + hp, b, 0, 0))

    def whole(arr):
        return pl.BlockSpec(arr.shape, lambda b, hp: (0,) * arr.ndim)

    rows_f32 = pltpu.VMEM((seq, LANES), F32)
    rows_bf16 = pltpu.VMEM((seq, LANES), BF16)
    return _call(
        body, name="attn_fwd", grid=(n_seq, n_hp),
        in_specs=[col(0), col(1), col(2), whole(g_q2), whole(g_k2),
                  pl.BlockSpec((None, 8, LANES), lambda b, hp: (hp, 0, 0)),
                  whole(dms[0]), whole(dms[1]), whole(dms[2]), whole(same)] + [ANY_SPEC] * n_ride,
        out_specs=[pl.BlockSpec((None, seq, LANES), lambda b, hp: (b, 0, hp)),
                   pl.BlockSpec((None, seq, LANES), lambda b, hp: (b, 0, hp))] + [ANY_SPEC] * n_ride,
        out_shape=[_sds((n_seq, seq, D_ATT), BF16), _sds((n_seq, seq, D_ATT), F32)] + _exchange_shapes(ride),
        scratch_shapes=[rows_f32, rows_f32, pltpu.VMEM((2, seq, LANES), BF16), rows_bf16, rows_bf16]
        + [pltpu.VMEM((seq, 3 * LANES), F32)] + [rows_f32] * 10 + _exchange_sems(n_ride),
        compiler_params=_params(("arbitrary", "arbitrary")),
    )(proj3, proj3, proj3, g_q2, g_k2, _alibi_rows(), *dms, same, *[a for a, _ in ride])


def _attn_bwd(proj3, do3, y_att3, lse3, g_q2, g_k2, ride):
    _, n_seq, seq, _ = proj3.shape
    dms = [_distance_mats(d, seq // d) for d in DILATIONS]
    same = _same_head()
    col0 = 2 * D_CONV // LANES
    n_hp = D_ATT // LANES

    n_ride = len(ride)
    ride_scatter = [s for _, s in ride]

    def body(*refs):
        (q_ref, k_ref, v_ref, do_ref, o_ref, lse_ref, gq_ref, gk_ref, sl_ref, dm1, dm4, dm16,
         same_ref) = refs[:13]
        ride_in = refs[13:13 + n_ride]
        dq_ref, dk_ref, dv_ref, dg_ref = refs[13 + n_ride:17 + n_ride]
        ride_out = refs[17 + n_ride:17 + 2 * n_ride]
        (qf, kf, qp, dop, kp, vp, sn, sp, dqp, dkp, dvp, dqn, dkn, dvn,
         tmp) = refs[17 + 2 * n_ride:32 + 2 * n_ride]
        ride_args = (ride_scatter, ride_in, ride_out) + tuple(refs[32 + 2 * n_ride:])
        dm_refs = (dm1, dm4, dm16)
        step = pl.program_id(0) * n_hp + pl.program_id(1)

        @pl.when(step == 0)
        def _():
            _exchange_start(*ride_args)
            dg_ref[...] = jnp.zeros_like(dg_ref)

        _qk_normalise(q_ref, gq_ref, same_ref, qf, seq, HEAD_DIM ** -0.5)
        _qk_normalise(k_ref, gk_ref, same_ref, kf, seq, 1.0)

        def stats(ci, carry):
            rows = pl.ds(pl.multiple_of(ci * ATT_ROWS, ATT_ROWS), ATT_ROWS)
            first = _first_head(ATT_ROWS)
            sn[0, rows, :], sn[1, rows, :] = _per_head(lse_ref[rows, :], first)
            prod = do_ref[rows, :] * o_ref[rows, :].astype(F32)
            sn[2, rows, :], sn[3, rows, :] = _per_head(_head_sum(prod, same_ref), first)
            return carry

        lax.fori_loop(0, seq // ATT_ROWS, stats, 0)
        slopes = (sl_ref[0:1, 0:1], sl_ref[1:2, 0:1])
        half = seq // (Q_BLOCK * ATT_UNROLL)
        region = seq // ATT_UNROLL

        for pi, dil in enumerate(DILATIONS):
            seg = seq // dil
            kw = min(2 * Q_BLOCK, seg)
            _permute_rows_by_head(qp, qf, dil, seq, tmp)
            _permute_rows_by_head(dop, do_ref, dil, seq, tmp)
            _permute_rows(kp, kf, dil, seq, tmp)
            _permute_rows(vp, v_ref, dil, seq, tmp)
            if dil == 1:
                st = sn
            else:
                st = sp
                for n in range(4):
                    _permute_rows(sp.at[n], sn.at[n], dil, seq, tmp)
            def touched(sub, seg=seg):
                lo, hi = sub * region, (sub + 1) * region
                if seg < region:
                    return lo, hi
                seg0 = lo // seg * seg
                return max(lo - RADIUS, seg0), min(hi + RADIUS, seg0 + seg)

            def summed(acc, start, size, touched=touched):
                pieces = []
                for c0 in range(start, start + size, RADIUS):
                    owners = [s for s in range(ATT_UNROLL) if touched(s)[0] <= c0 and c0 + RADIUS <= touched(s)[1]]
                    if pieces and pieces[-1][2] == owners:
                        pieces[-1][1] += RADIUS
                    else:
                        pieces.append([c0, RADIUS, owners])
                vals = [sum(acc[o, pl.ds(c0, n), :] for o in owners) for c0, n, owners in pieces]
                return vals[0] if len(vals) == 1 else jnp.concatenate(vals, axis=0)

            for sub in range(ATT_UNROLL):
                lo, hi = touched(sub)
                dkp[sub, pl.ds(lo, hi - lo), :] = jnp.zeros((hi - lo, LANES), F32)
                dvp[sub, pl.ds(lo, hi - lo), :] = jnp.zeros((hi - lo, LANES), F32)

            def blk(it, carry, seg=seg, kw=kw, pi=pi, st=st):
                first = _first_head(Q_BLOCK)
                chains = [(sub, h) for sub in range(ATT_UNROLL) for h in range(2)]
                win = [_window(it + sub * half, seg) for sub in range(ATT_UNROLL)]
                qrows = [pl.ds(w[0], Q_BLOCK) for w in win]
                krows = [pl.ds(w[1], kw) for w in win]

                def over_keys(n, sub):
                    t = st[n, qrows[sub], :]
                    return t if kw == LANES else jnp.concatenate([t] * (kw // LANES), axis=1)

                s, dp = {}, {}
                for sub, h in chains:
                    s[sub, h] = lax.dot_general(qp[h, qrows[sub], :], kp[krows[sub], :], NT,
                                                preferred_element_type=F32) - slopes[h] * dm_refs[pi][win[sub][2]]
                    dp[sub, h] = lax.dot_general(dop[h, qrows[sub], :], vp[krows[sub], :], NT,
                                                 preferred_element_type=F32)
                p, ds = {}, {}
                for sub, h in chains:
                    e = jnp.exp(s[sub, h] - over_keys(h, sub))
                    ds[sub, h] = (e * (dp[sub, h] - over_keys(2 + h, sub))).astype(BF16)
                    p[sub, h] = e.astype(BF16)
                dq, dk, dv = {}, {}, {}
                for sub, h in chains:
                    dq[sub, h] = jnp.dot(ds[sub, h], kp[krows[sub], :], preferred_element_type=F32)
                    dk[sub, h] = lax.dot_general(ds[sub, h], qp[h, qrows[sub], :], TN, preferred_element_type=F32)
                    dv[sub, h] = lax.dot_general(p[sub, h], dop[h, qrows[sub], :], TN, preferred_element_type=F32)
                for sub in range(ATT_UNROLL):
                    dqp[qrows[sub], :] = jnp.where(first, dq[sub, 0], dq[sub, 1])
                    dkp[sub, krows[sub], :] += dk[sub, 0] + dk[sub, 1]
                    dvp[sub, krows[sub], :] += dv[sub, 0] + dv[sub, 1]
                return carry

            lax.fori_loop(0, half, blk, 0)
            first_pattern = pi == 0
            if first_pattern:
                for r0 in range(0, seq, region):
                    rows = pl.ds(r0, region)
                    dqn[rows, :] = dqp[rows, :]
                    dkn[rows, :] = summed(dkp, r0, region)
                    dvn[rows, :] = summed(dvp, r0, region)
            else:
                _scatter_segments(dqn, lambda start, size: dqp[pl.ds(start, size), :], dil, seq, tmp, accumulate=True)
                for nat, acc in ((dkn, dkp), (dvn, dvp)):
                    _scatter_segments(nat, lambda start, size, acc=acc: summed(acc, start, size),
                                      dil, seq, tmp, accumulate=True)

        def finish(ci, carry):
            rows = pl.ds(pl.multiple_of(ci * ATT_ROWS, ATT_ROWS), ATT_ROWS)
            for src_ref, g_ref, dn, dst_ref, scale, row in (
                    (q_ref, gq_ref, dqn, dq_ref, HEAD_DIM ** -0.5, 0), (k_ref, gk_ref, dkn, dk_ref, 1.0, 1)):
                xv = src_ref[rows, :]
                r = lax.rsqrt(_head_mean(xv * xv, same_ref) + EPS)
                xhat = xv * r
                d = dn[rows, :] * scale
                dg_ref[row:row + 1, :] += jnp.sum(d * xhat, axis=0, keepdims=True)
                dxh = d * g_ref[...]
                dst_ref[rows, :] = (r * (dxh - xhat * _head_mean(dxh * xhat, same_ref))).astype(BF16)
            dv_ref[rows, :] = dvn[rows, :].astype(BF16)
            return carry

        lax.fori_loop(0, seq // ATT_ROWS, finish, 0)

        @pl.when(step == n_seq * n_hp - 1)
        def _():
            _exchange_wait(*ride_args)

    def col(off):
        return pl.BlockSpec((None, None, seq, LANES), lambda b, hp: (col0 + off * n_hp + hp, b, 0, 0))

    def whole(arr):
        return pl.BlockSpec(arr.shape, lambda b, hp: (0,) * arr.ndim)

    att = pl.BlockSpec((None, seq, LANES), lambda b, hp: (b, 0, hp))
    rows_f32 = pltpu.VMEM((seq, LANES), F32)
    rows_bf16 = pltpu.VMEM((seq, LANES), BF16)
    by_head_bf16 = pltpu.VMEM((2, seq, LANES), BF16)
    per_sub_f32 = pltpu.VMEM((ATT_UNROLL, seq, LANES), F32)
    stats_f32 = pltpu.VMEM((4, seq, LANES), F32)
    return _call(
        body, name="attn_bwd", grid=(n_seq, n_hp),
        in_specs=[col(0), col(1), col(2), att, att, att, whole(g_q2), whole(g_k2),
                  pl.BlockSpec((None, 8, LANES), lambda b, hp: (hp, 0, 0)),
                  whole(dms[0]), whole(dms[1]), whole(dms[2]), whole(same)] + [ANY_SPEC] * n_ride,
        out_specs=[att, att, att, pl.BlockSpec((8, LANES), lambda b, hp: (0, 0))] + [ANY_SPEC] * n_ride,
        out_shape=[_sds((n_seq, seq, D_ATT), BF16)] * 3 + [_sds((8, LANES), F32)] + _exchange_shapes(ride),
        scratch_shapes=[rows_f32, rows_f32, by_head_bf16, by_head_bf16, rows_bf16, rows_bf16, stats_f32, stats_f32,
                        rows_f32, per_sub_f32, per_sub_f32, rows_f32, rows_f32, rows_f32, rows_f32]
        + _exchange_sems(n_ride),
        compiler_params=_params(("arbitrary", "arbitrary")),
    )(proj3, proj3, proj3, do3, y_att3, lse3, g_q2, g_k2, _alibi_rows(), *dms, same, *[a for a, _ in ride])


def _mix_out(uc2, y_att2, x2, mod8, g_ln, b_ln, g_ffn, w_out, seq, tm=512):
    tokens = x2.shape[0]
    per_seq = seq // tm

    def body(uc_ref, ya_ref, x_ref, m_ref, gl_ref, bl_ref, gf_ref, w_ref, yc_ref, mix_ref, x1_ref, h2_ref):
        uc = uc_ref[...]
        mu = jnp.mean(uc, axis=-1, keepdims=True)
        cen = uc - mu
        rs = lax.rsqrt(jnp.mean(cen * cen, axis=-1, keepdims=True) + EPS)
        z = cen * rs * gl_ref[...] + bl_ref[...]
        yc = (z * _sig(z)).astype(BF16)
        yc_ref[...] = yc
        mix = (jnp.dot(yc, w_ref[pl.ds(0, D_CONV), :], preferred_element_type=F32)
               + jnp.dot(ya_ref[...], w_ref[pl.ds(D_CONV, D_ATT), :], preferred_element_type=F32))
        mix_ref[...] = mix.astype(BF16)
        x1 = x_ref[...] + m_ref[2:3, :] * mix
        x1_ref[...] = x1
        r = lax.rsqrt(jnp.mean(x1 * x1, axis=-1, keepdims=True) + EPS)
        h2_ref[...] = ((x1 * r * gf_ref[...]) * (1.0 + m_ref[4:5, :]) + m_ref[3:4, :]).astype(BF16)

    def rows(width):
        return pl.BlockSpec((tm, width), lambda i: (i, 0))

    def vec(width):
        return pl.BlockSpec((1, width), lambda i: (0, 0))

    return _call(
        body, name="mix_out", grid=(tokens // tm,),
        in_specs=[rows(D_CONV), rows(D_ATT), rows(D_MODEL),
                  pl.BlockSpec((None, 8, D_MODEL), lambda i: (i // per_seq, 0, 0)),
                  vec(D_CONV), vec(D_CONV), vec(D_MODEL),
                  pl.BlockSpec((D_MODEL, D_MODEL), lambda i: (0, 0))],
        out_specs=[rows(D_CONV), rows(D_MODEL), rows(D_MODEL), rows(D_MODEL)],
        out_shape=[_sds((tokens, D_CONV), BF16), _sds((tokens, D_MODEL), BF16),
                   _sds((tokens, D_MODEL), F32), _sds((tokens, D_MODEL), BF16)],
        compiler_params=_params(("parallel",)),
    )(uc2, y_att2, x2, mod8, g_ln, b_ln, g_ffn, w_out)


def _mix_out_bwd(dmix, uc2, g_ln, b_ln, w_out, tm=512):
    tokens = dmix.shape[0]

    def body(dm_ref, uc_ref, gl_ref, bl_ref, w_ref, duc_ref, do_ref, dgb_ref):
        @pl.when(pl.program_id(0) == 0)
        def _():
            dgb_ref[...] = jnp.zeros_like(dgb_ref)

        dmv = dm_ref[...]
        dyc = lax.dot_general(dmv, w_ref[pl.ds(0, D_CONV), :], NT, preferred_element_type=F32)
        do_ref[...] = lax.dot_general(dmv, w_ref[pl.ds(D_CONV, D_ATT), :], NT, preferred_element_type=F32)
        uc = uc_ref[...]
        mu = jnp.mean(uc, axis=-1, keepdims=True)
        cen = uc - mu
        rs = lax.rsqrt(jnp.mean(cen * cen, axis=-1, keepdims=True) + EPS)
        xh = cen * rs
        z = xh * gl_ref[...] + bl_ref[...]
        sg = _sig(z)
        dz = dyc * (sg * (1.0 + z * (1.0 - sg)))
        dgb_ref[0:1, :] += jnp.sum(dz * xh, axis=0, keepdims=True)
        dgb_ref[1:2, :] += jnp.sum(dz, axis=0, keepdims=True)
        dxh = dz * gl_ref[...]
        duc_ref[...] = rs * (dxh - jnp.mean(dxh, axis=-1, keepdims=True)
                             - xh * jnp.mean(dxh * xh, axis=-1, keepdims=True))

    return _call(
        body, name="mix_out_bwd", grid=(tokens // tm,),
        in_specs=[pl.BlockSpec((tm, D_MODEL), lambda i: (i, 0)),
                  pl.BlockSpec((tm, D_CONV), lambda i: (i, 0)),
                  pl.BlockSpec((1, D_CONV), lambda i: (0, 0)),
                  pl.BlockSpec((1, D_CONV), lambda i: (0, 0)),
                  pl.BlockSpec((D_MODEL, D_MODEL), lambda i: (0, 0))],
        out_specs=[pl.BlockSpec((tm, D_CONV), lambda i: (i, 0)),
                   pl.BlockSpec((tm, D_ATT), lambda i: (i, 0)),
                   pl.BlockSpec((8, D_CONV), lambda i: (0, 0))],
        out_shape=[_sds((tokens, D_CONV), F32), _sds((tokens, D_ATT), F32), _sds((8, D_CONV), F32)],
        compiler_params=_params(("arbitrary",)),
    )(dmix, uc2, g_ln, b_ln, w_out)


FF_TILE = 256
FF_TILES = D_FF // FF_TILE


def _load_once(hbm_refs, vmem_refs, sems):
    @pl.when(pl.program_id(0) == 0)
    def _():
        copies = [pltpu.make_async_copy(src, dst, sems.at[n]) for n, (src, dst) in enumerate(zip(hbm_refs, vmem_refs))]
        for cp in copies:
            cp.start()
        for cp in copies:
            cp.wait()


def _ffn_fwd(h2, w_gate_t, w_up_t, w_down, x1, target, mod8, seq, tm=256):
    tokens = h2.shape[0]
    per_seq = seq // tm
    n_seq = tokens // seq

    def body(h_ref, m_ref, x1_ref, t_ref, wg_hbm, wu_hbm, wd_hbm, gate_ref, up_ref, dy_ref, df_ref, sq_ref, dgf_ref,
             wg, wu, wd, w_sems):
        i = pl.program_id(0)
        _load_once((wg_hbm, wu_hbm, wd_hbm), (wg, wu, wd), w_sems)

        @pl.when(i == 0)
        def _():
            sq_ref[...] = jnp.zeros_like(sq_ref)

        @pl.when(i % per_seq == 0)
        def _():
            dgf_ref[...] = jnp.zeros_like(dgf_ref)

        hv = h_ref[...]

        def gate_up(t):
            rows = pl.ds(t * FF_TILE, FF_TILE)
            return (lax.dot_general(hv, wg[rows, :], NT, preferred_element_type=F32),
                    lax.dot_general(hv, wu[rows, :], NT, preferred_element_type=F32))

        fv = jnp.zeros((tm, D_MODEL), F32)
        ahead = gate_up(0)
        for t in range(FF_TILES):
            gate, up = ahead
            if t + 1 < FF_TILES:
                ahead = gate_up(t + 1)
            gate_ref[t] = gate.astype(BF16)
            up_ref[t] = up.astype(BF16)
            act = (gate * _sig(gate) * up).astype(BF16)
            fv = fv + jnp.dot(act, wd[pl.ds(t * FF_TILE, FF_TILE), :], preferred_element_type=F32)
        gate_f = m_ref[5:6, :]
        diff = x1_ref[...] + gate_f * fv - t_ref[...]
        sq_ref[0:1, :] += jnp.sum(diff * diff, axis=0, keepdims=True)
        dy = diff * (1.0 / D_MODEL)
        dy_ref[...] = dy
        df_ref[...] = (gate_f * dy).astype(BF16)
        dgf_ref[0:1, :] += jnp.sum(dy * fv, axis=0, keepdims=True)

    rows_spec = pl.BlockSpec((tm, D_MODEL), lambda i: (i, 0))
    per = pl.BlockSpec((None, 8, D_MODEL), lambda i: (i // per_seq, 0, 0))
    tiles = pl.BlockSpec((FF_TILES, tm, FF_TILE), lambda i: (0, i, 0))
    weight = pltpu.VMEM((D_FF, D_MODEL), BF16)
    return _call(
        body, name="ffn_fwd", grid=(tokens // tm,),
        in_specs=[rows_spec, per, rows_spec, rows_spec, ANY_SPEC, ANY_SPEC, ANY_SPEC],
        out_specs=[tiles, tiles, rows_spec, rows_spec, pl.BlockSpec((8, D_MODEL), lambda i: (0, 0)), per],
        out_shape=[_sds((FF_TILES, tokens, FF_TILE), BF16), _sds((FF_TILES, tokens, FF_TILE), BF16),
                   _sds((tokens, D_MODEL), F32), _sds((tokens, D_MODEL), BF16),
                   _sds((8, D_MODEL), F32), _sds((n_seq, 8, D_MODEL), F32)],
        scratch_shapes=[weight, weight, weight, pltpu.SemaphoreType.DMA((3,))],
        compiler_params=_params(("arbitrary",)),
    )(h2, mod8, x1, target, w_gate_t, w_up_t, w_down)


def _ffn_bwd(df, gate, up, w_gate_t, w_up_t, w_down, x1, dy, mix, mod8, g_ffn, seq, tm=256):
    tokens = df.shape[0]
    per_seq = seq // tm
    n_seq = tokens // seq

    def body(df_ref, gate_ref, up_ref, m_ref, g_ref, x1_ref, dy_ref, mix_ref, wg_hbm, wu_hbm, wd_hbm,
             dgate_ref, dup_ref, act_ref, dx1_ref, dmix_ref, dg_ref, dm_ref, wg, wu, wd, w_sems):
        i = pl.program_id(0)
        _load_once((wg_hbm, wu_hbm, wd_hbm), (wg, wu, wd), w_sems)

        @pl.when(i == 0)
        def _():
            dg_ref[...] = jnp.zeros_like(dg_ref)

        @pl.when(i % per_seq == 0)
        def _():
            dm_ref[...] = jnp.zeros_like(dm_ref)

        dfv = df_ref[...]

        def d_act(t):
            return lax.dot_general(dfv, wd[pl.ds(t * FF_TILE, FF_TILE), :], NT, preferred_element_type=F32)

        dh = jnp.zeros((tm, D_MODEL), F32)
        ahead = d_act(0)
        for t in range(FF_TILES):
            dact = ahead
            if t + 1 < FF_TILES:
                ahead = d_act(t + 1)
            rows = pl.ds(t * FF_TILE, FF_TILE)
            gv = gate_ref[t].astype(F32)
            uv = up_ref[t].astype(F32)
            sg = _sig(gv)
            silu = gv * sg
            act_ref[t] = (silu * uv).astype(BF16)
            dup = (dact * silu).astype(BF16)
            dgate = (dact * uv * (sg * (1.0 + gv * (1.0 - sg)))).astype(BF16)
            dup_ref[t] = dup
            dgate_ref[t] = dgate
            dh = dh + (jnp.dot(dgate, wg[rows, :], preferred_element_type=F32)
                       + jnp.dot(dup, wu[rows, :], preferred_element_type=F32))
        g = g_ref[...]
        x1v = x1_ref[...]
        rs = lax.rsqrt(jnp.mean(x1v * x1v, axis=-1, keepdims=True) + EPS)
        xhat = x1v * rs
        dm_ref[0:1, :] += jnp.sum(dh, axis=0, keepdims=True)
        dm_ref[1:2, :] += jnp.sum(dh * (xhat * g), axis=0, keepdims=True)
        dn = dh * (1.0 + m_ref[4:5, :])
        dg_ref[0:1, :] += jnp.sum(dn * xhat, axis=0, keepdims=True)
        dxh = dn * g
        dx1 = dy_ref[...] + rs * (dxh - xhat * jnp.mean(dxh * xhat, axis=-1, keepdims=True))
        dx1_ref[...] = dx1
        dm_ref[2:3, :] += jnp.sum(dx1 * mix_ref[...].astype(F32), axis=0, keepdims=True)
        dmix_ref[...] = (m_ref[2:3, :] * dx1).astype(BF16)

    rows_spec = pl.BlockSpec((tm, D_MODEL), lambda i: (i, 0))
    per = pl.BlockSpec((None, 8, D_MODEL), lambda i: (i // per_seq, 0, 0))
    tiles = pl.BlockSpec((FF_TILES, tm, FF_TILE), lambda i: (0, i, 0))
    weight = pltpu.VMEM((D_FF, D_MODEL), BF16)
    return _call(
        body, name="ffn_bwd", grid=(tokens // tm,),
        in_specs=[rows_spec, tiles, tiles, per, pl.BlockSpec((1, D_MODEL), lambda i: (0, 0)),
                  rows_spec, rows_spec, rows_spec, ANY_SPEC, ANY_SPEC, ANY_SPEC],
        out_specs=[tiles, tiles, tiles, rows_spec, rows_spec, pl.BlockSpec((8, D_MODEL), lambda i: (0, 0)), per],
        out_shape=[_sds((FF_TILES, tokens, FF_TILE), BF16)] * 3
        + [_sds((tokens, D_MODEL), F32), _sds((tokens, D_MODEL), BF16),
           _sds((8, D_MODEL), F32), _sds((n_seq, 8, D_MODEL), F32)],
        scratch_shapes=[weight, weight, weight, pltpu.SemaphoreType.DMA((3,))],
        compiler_params=_params(("arbitrary",)),
    )(df, gate, up, mod8, g_ffn, x1, dy, mix, w_gate_t, w_up_t, w_down)


def _mix_in_bwd(d_a, d_g, d_q, d_k, d_v, w_in, x2, dx1, mod8, g_mix, seq, ride, tm=512):
    tokens = x2.shape[0]
    per_seq = seq // tm
    n_seq = tokens // seq
    parts = (d_a, d_g, d_q, d_k, d_v)
    width = D_CONV
    n_ride = len(ride)
    ride_scatter = [s for _, s in ride]

    def body(*refs):
        da_ref, dg_ref, dq_ref, dk_ref, dv_ref, w_ref, x_ref, dx1_ref, m_ref, g_ref = refs[:10]
        ride_in = refs[10:10 + n_ride]
        gx_ref, dgm_ref, dm_ref = refs[10 + n_ride:13 + n_ride]
        ride_args = (ride_scatter, ride_in, refs[13 + n_ride:13 + 2 * n_ride]) + tuple(refs[13 + 2 * n_ride:])
        i = pl.program_id(0)

        @pl.when(i == 0)
        def _():
            _exchange_start(*ride_args)
            dgm_ref[...] = jnp.zeros_like(dgm_ref)

        @pl.when(i % per_seq == 0)
        def _():
            dm_ref[...] = jnp.zeros_like(dm_ref)

        dh = jnp.zeros((tm, D_MODEL), F32)
        for n, ref in enumerate((da_ref, dg_ref, dq_ref, dk_ref, dv_ref)):
            dh = dh + jnp.dot(ref[...], w_ref[pl.ds(n * width, width), :], preferred_element_type=F32)
        xv = x_ref[...]
        r = lax.rsqrt(jnp.mean(xv * xv, axis=-1, keepdims=True) + EPS)
        xhat = xv * r
        g = g_ref[...]
        dm_ref[0:1, :] += jnp.sum(dh, axis=0, keepdims=True)
        dm_ref[1:2, :] += jnp.sum(dh * (xhat * g), axis=0, keepdims=True)
        dn = dh * (1.0 + m_ref[1:2, :])
        dgm_ref[0:1, :] += jnp.sum(dn * xhat, axis=0, keepdims=True)
        dxh = dn * g
        gx_ref[...] = dx1_ref[...] + r * (dxh - xhat * jnp.mean(dxh * xhat, axis=-1, keepdims=True))

        @pl.when(i == tokens // tm - 1)
        def _():
            _exchange_wait(*ride_args)

    rows = pl.BlockSpec((tm, D_MODEL), lambda i: (i, 0))
    half = pl.BlockSpec((tm, width), lambda i: (i, 0))
    per = pl.BlockSpec((None, 8, D_MODEL), lambda i: (i // per_seq, 0, 0))
    return _call(
        body, name="mix_in_bwd", grid=(tokens // tm,),
        in_specs=[half] * 5 + [pl.BlockSpec((D_IN, D_MODEL), lambda i: (0, 0)), rows, rows, per,
                               pl.BlockSpec((1, D_MODEL), lambda i: (0, 0))] + [ANY_SPEC] * n_ride,
        out_specs=[rows, pl.BlockSpec((8, D_MODEL), lambda i: (0, 0)), per] + [ANY_SPEC] * n_ride,
        out_shape=[_sds((tokens, D_MODEL), F32), _sds((8, D_MODEL), F32), _sds((n_seq, 8, D_MODEL), F32)]
        + _exchange_shapes(ride),
        scratch_shapes=_exchange_sems(n_ride),
        compiler_params=_params(("arbitrary",)),
    )(*parts, w_in, x2, dx1, mod8, g_mix, *[a for a, _ in ride])


def _grad_matmul_parts(a_parts, b_parts, name, tk=1024):
    tokens = a_parts[0].shape[0]
    na, nb = len(a_parts), len(b_parts)
    ma, nbw = a_parts[0].shape[1], b_parts[0].shape[1]

    n_k = tokens // tk

    def body(*refs):
        a_refs, b_refs, o_ref, acc = refs[:na], refs[na:na + nb], refs[na + nb], refs[na + nb + 1]

        @pl.when(pl.program_id(0) == 0)
        def _():
            acc[...] = jnp.zeros_like(acc)

        for i in range(na):
            for j in range(nb):
                acc[pl.ds(i * ma, ma), pl.ds(j * nbw, nbw)] += lax.dot_general(
                    a_refs[i][...], b_refs[j][...], TN, preferred_element_type=F32)

        @pl.when(pl.program_id(0) == n_k - 1)
        def _():
            o_ref[...] = acc[...].astype(o_ref.dtype)

    return _call(
        body, name=name, grid=(n_k,),
        in_specs=[pl.BlockSpec((tk, ma), lambda k: (k, 0))] * na + [pl.BlockSpec((tk, nbw), lambda k: (k, 0))] * nb,
        out_specs=pl.BlockSpec((na * ma, nb * nbw), lambda k: (0, 0)),
        out_shape=_sds((na * ma, nb * nbw), BF16),
        scratch_shapes=[pltpu.VMEM((na * ma, nb * nbw), F32)],
        compiler_params=_params(("arbitrary",)),
    )(*a_parts, *b_parts)


def _grad_matmul_tiles(a, b, name, tk=1024):
    tiled_b = b.ndim == 3
    tiles, tokens, width = b.shape if tiled_b else a.shape
    other = a.shape[1] if tiled_b else b.shape[1]
    out_tile = (other, width) if tiled_b else (width, other)
    n_k = tokens // tk

    def body(a_ref, b_ref, o_ref, acc):
        @pl.when(pl.program_id(0) == 0)
        def _():
            acc[...] = jnp.zeros_like(acc)

        for t in range(tiles):
            lhs = a_ref[...] if tiled_b else a_ref[t]
            rhs = b_ref[t] if tiled_b else b_ref[...]
            acc[t] += lax.dot_general(lhs, rhs, TN, preferred_element_type=F32)

        @pl.when(pl.program_id(0) == n_k - 1)
        def _():
            o_ref[...] = acc[...].astype(o_ref.dtype)

    flat = pl.BlockSpec((tk, other), lambda k: (k, 0))
    tiled = pl.BlockSpec((tiles, tk, width), lambda k: (0, k, 0))
    return _call(
        body, name=name, grid=(n_k,),
        in_specs=[flat, tiled] if tiled_b else [tiled, flat],
        out_specs=pl.BlockSpec((tiles,) + out_tile, lambda k: (0, 0, 0)),
        out_shape=_sds((tiles,) + out_tile, BF16),
        scratch_shapes=[pltpu.VMEM((tiles,) + out_tile, F32)],
        compiler_params=_params(("arbitrary",)),
    )(a, b)


def _adamw(w, m, v, g, name, n_parts=0, tr=256):
    rows, cols = w.shape
    tr = min(tr, rows)
    c1 = 1.0 - ADAM_B1 ** ADAM_STEP
    c2 = 1.0 - ADAM_B2 ** ADAM_STEP

    def body(w_ref, m_ref, v_ref, g_ref, go_ref, d_ref, mo_ref, vo_ref):
        if n_parts:
            gv = g_ref[0].astype(F32)
            for p in range(1, n_parts):
                gv = gv + g_ref[p].astype(F32)
        else:
            gv = g_ref[...]
        go_ref[...] = gv
        mn = ADAM_B1 * m_ref[...] + (1.0 - ADAM_B1) * gv
        vn = ADAM_B2 * v_ref[...] + (1.0 - ADAM_B2) * (gv * gv)
        mo_ref[...] = mn
        vo_ref[...] = vn
        d_ref[...] = -ADAM_LR * ((mn / c1) / (jnp.sqrt(vn / c2) + ADAM_EPS) + ADAM_WD * w_ref[...])

    blk = pl.BlockSpec((tr, cols), lambda i: (i, 0))
    g_spec = pl.BlockSpec((n_parts, tr, cols), lambda i: (0, i, 0)) if n_parts else blk
    return _call(
        body, name=name, grid=(rows // tr,),
        in_specs=[blk, blk, blk, g_spec], out_specs=[blk] * 4,
        out_shape=[_sds((rows, cols), F32)] * 4,
        compiler_params=_params(("parallel",)),
    )(w, m, v, g)


def _cols_to_full(blocks):
    n, r, c = blocks.shape
    return jnp.transpose(blocks, (1, 0, 2)).reshape(r, n * c)


def _pad_lanes(v, width):
    return jnp.pad(v, ((0, 0), (0, width - v.shape[1])))


def kernel(x, c, w_ada, b_ada, g_mix, w_in, w_dw, b_dw, g_conv_ln, b_conv_ln, g_q, g_k, w_out, g_ffn, w_gate, w_up, w_down, loss_target, m_w_ada, m_b_ada, m_g_mix, m_w_in, m_w_dw, m_b_dw, m_g_conv_ln, m_b_conv_ln, m_g_q, m_g_k, m_w_out, m_g_ffn, m_w_gate, m_w_up, m_w_down, v_w_ada, v_b_ada, v_g_mix, v_w_in, v_w_dw, v_b_dw, v_g_conv_ln, v_b_conv_ln, v_g_q, v_g_k, v_w_out, v_g_ffn, v_w_gate, v_w_up, v_w_down):
    n_seq, seq, _ = x.shape
    tokens = n_seq * seq
    me = 4 * lax.axis_index("x") + 2 * lax.axis_index("y") + lax.axis_index("c")
    ada_cols = w_ada.shape[2]
    dw_cols = w_dw.shape[2]

    def transposed(w):
        return jnp.transpose(w[0])

    (c_g, w_in_g, w_dw_g) = _gather_by_chip([c, transposed(w_in).astype(BF16), w_dw[0]], "gather_weights")
    c_all = c_g.reshape(N_DEV * n_seq, D_MODEL)
    w_in_t = w_in_g.reshape(D_IN, D_MODEL)
    w_dw_f = _cols_to_full(w_dw_g)

    b_cols = lax.dynamic_slice(b_ada, (0, me * ada_cols), (1, ada_cols))
    mod_cols = _ada_fwd(c_all, w_ada[0], b_cols)
    (mod_g,) = _exchange([(mod_cols, False)], "gather_mod")
    mod_mine = lax.dynamic_slice(mod_g, (0, me * n_seq, 0), (N_DEV, n_seq, ada_cols))
    mod = jnp.transpose(mod_mine, (1, 0, 2)).reshape(n_seq, N_MOD, D_MODEL)
    mod8 = jnp.pad(mod, ((0, 0), (0, 8 - N_MOD), (0, 0)))

    x2 = x.reshape(tokens, D_MODEL)
    h1, proj = _mix_in(x2, mod8, g_mix, w_in_t, seq)
    proj3 = proj.reshape(D_IN // LANES, n_seq, seq, LANES)
    uc3 = _conv_fwd(proj3, w_dw_f, b_dw)
    g_q2, g_k2 = jnp.tile(g_q, (1, 2)), jnp.tile(g_k, (1, 2))
    y_att3, lse3, w_out_g, w_gate_g, w_up_g, w_down_g = _attn_fwd(
        proj3, g_q2, g_k2,
        [(w_out[0].astype(BF16), False), (transposed(w_gate).astype(BF16), False),
         (transposed(w_up).astype(BF16), False), (w_down[0].astype(BF16), False)])
    w_out_f = w_out_g.reshape(D_MODEL, D_MODEL)
    w_gate_f = w_gate_g.reshape(D_FF, D_MODEL)
    w_up_f = w_up_g.reshape(D_FF, D_MODEL)
    w_down_f = w_down_g.reshape(D_FF, D_MODEL)
    uc2 = uc3.reshape(tokens, D_CONV)
    y_att2 = y_att3.reshape(tokens, D_ATT)
    y_conv, mix, x1, h2 = _mix_out(uc2, y_att2, x2, mod8, g_conv_ln, b_conv_ln, g_ffn, w_out_f, seq)
    gate, up, dy, df, sq, dgate_f = _ffn_fwd(
        h2, w_gate_f, w_up_f, w_down_f, x1, loss_target.reshape(tokens, D_MODEL), mod8, seq)

    dgate, dup, act, dx1, dmix, dg_ffn, dmod_f = _ffn_bwd(
        df, gate, up, w_gate_f, w_up_f, w_down_f, x1, dy, mix, mod8, g_ffn, seq)
    duc2, do2, dgb_ln = _mix_out_bwd(dmix, uc2, g_conv_ln, b_conv_ln, w_out_f)
    d_a3, d_g3, dw_dw_p, db_dw_p = _conv_bwd(duc2.reshape(n_seq, seq, D_CONV), proj3, w_dw_f)
    gw_gate = _grad_matmul_tiles(dgate, h2, "grad_w_gate")
    gw_up = _grad_matmul_tiles(dup, h2, "grad_w_up")
    gw_down = _grad_matmul_tiles(act, df, "grad_w_down")
    gw_out = _grad_matmul_parts([y_conv, y_att2], [dmix], "grad_w_out")
    d_q3, d_k3, d_v3, dg_qk, p_gate, p_up, p_down, p_out = _attn_bwd(
        proj3, do2.reshape(n_seq, seq, D_ATT), y_att3, lse3, g_q2, g_k2,
        [(gw_gate.reshape(N_DEV, D_FF // N_DEV, D_MODEL), True), (gw_up.reshape(N_DEV, D_FF // N_DEV, D_MODEL), True),
         (gw_down.reshape(N_DEV, D_FF // N_DEV, D_MODEL), True),
         (gw_out.reshape(N_DEV, D_MODEL // N_DEV, D_MODEL), True)])
    flat = lambda t: t.reshape(tokens, t.shape[-1])
    d_a, d_g, d_q, d_k, d_v = flat(d_a3), flat(d_g3), flat(d_q3), flat(d_k3), flat(d_v3)
    gw_in = _grad_matmul_parts([d_a, d_g, d_q, d_k, d_v], [h1], "grad_w_in")
    grad_x2, dg_mix, dmod_m, p_in = _mix_in_bwd(
        d_a, d_g, d_q, d_k, d_v, w_in_t, x2, dx1, mod8, g_mix, seq,
        [(gw_in.reshape(N_DEV, D_IN // N_DEV, D_MODEL), True)])

    dmod = jnp.concatenate([dmod_m[:, 0], dmod_m[:, 1], dmod_f[:, 2], dmod_f[:, 0], dmod_f[:, 1], dgate_f[:, 0]], axis=1)
    dg_q = dg_qk[0:1, 0:HEAD_DIM] + dg_qk[0:1, HEAD_DIM:]
    dg_k = dg_qk[1:2, 0:HEAD_DIM] + dg_qk[1:2, HEAD_DIM:]
    loss_part = (0.5 / D_MODEL) * jnp.sum(sq[0:1, :], axis=1, keepdims=True)
    small = jnp.concatenate(
        [dg_mix[0:1], dg_ffn[0:1], db_dw_p[0:1], dgb_ln[0:1], dgb_ln[1:2],
         _pad_lanes(dg_q, LANES), _pad_lanes(dg_k, LANES), _pad_lanes(loss_part, LANES)], axis=1)
    n_small = small.shape[1] - LANES

    (dmod_g, small_g, dw_g) = _exchange([(dmod, False), (small, False), (dw_dw_p, False)], "gather_small_grads")

    dmod_all = dmod_g.reshape(N_DEV * n_seq, N_MOD * D_MODEL)
    dmod_cols = lax.dynamic_slice(dmod_all, (0, me * ada_cols), (N_DEV * n_seq, ada_cols))
    gw_ada, gb_ada = _ada_bwd(c_all, dmod_cols, dmod_all)

    res = {}
    res["w_ada"] = _adamw(w_ada[0], m_w_ada[0], v_w_ada[0], gw_ada, "adamw_w_ada")
    res["b_ada"] = _adamw(b_ada, m_b_ada, v_b_ada, gb_ada, "adamw_b_ada")
    def adamw_transposed(w, m, v, parts, name, tr):
        outs = _adamw(transposed(w), transposed(m), transposed(v), parts, name, N_DEV, tr=tr)
        return tuple(jnp.transpose(o) for o in outs)

    res["w_in"] = adamw_transposed(w_in, m_w_in, v_w_in, p_in, "adamw_w_in", 160)
    res["w_out"] = _adamw(w_out[0], m_w_out[0], v_w_out[0], p_out, "adamw_w_out", N_DEV)
    res["w_gate"] = adamw_transposed(w_gate, m_w_gate, v_w_gate, p_gate, "adamw_w_gate", 176)
    res["w_up"] = adamw_transposed(w_up, m_w_up, v_w_up, p_up, "adamw_w_up", 176)
    res["w_down"] = _adamw(w_down[0], m_w_down[0], v_w_down[0], p_down, "adamw_w_down", N_DEV, tr=176)
    dw_mine = lax.dynamic_slice(dw_g, (0, 0, me * dw_cols), (N_DEV, CONV_WIDTH, dw_cols))
    res["w_dw"] = _adamw(w_dw[0], m_w_dw[0], v_w_dw[0], dw_mine, "adamw_w_dw", N_DEV)

    small_names = ["g_mix", "g_ffn", "b_dw", "g_conv_ln", "b_conv_ln", "g_q", "g_k"]
    small_w = {"g_mix": (g_mix, m_g_mix, v_g_mix), "g_ffn": (g_ffn, m_g_ffn, v_g_ffn), "b_dw": (b_dw, m_b_dw, v_b_dw),
               "g_conv_ln": (g_conv_ln, m_g_conv_ln, v_g_conv_ln), "b_conv_ln": (b_conv_ln, m_b_conv_ln, v_b_conv_ln),
               "g_q": (g_q, m_g_q, v_g_q), "g_k": (g_k, m_g_k, v_g_k)}
    widths = [max(small_w[n][0].shape[1], LANES) for n in small_names]
    packed = [jnp.concatenate([_pad_lanes(small_w[n][i], wd) for n, wd in zip(small_names, widths)], axis=1) for i in range(3)]
    outs = _adamw(packed[0], packed[1], packed[2], small_g[:, :, :n_small], "adamw_small", N_DEV)
    off = 0
    for n, wd in zip(small_names, widths):
        real = small_w[n][0].shape[1]
        res[n] = tuple(o[:, off:off + real] for o in outs)
        off += wd
    loss = jnp.sum(small_g[:, 0, n_small])

    order = ["w_ada", "b_ada", "g_mix", "w_in", "w_dw", "b_dw", "g_conv_ln", "b_conv_ln", "g_q", "g_k",
             "w_out", "g_ffn", "w_gate", "w_up", "w_down"]
    lead = {"w_ada", "w_in", "w_dw", "w_out", "w_gate", "w_up", "w_down"}
    grads, deltas, new_m, new_v = [], [], [], []
    for n in order:
        g, d, mn, vn = res[n]
        g, d, mn, vn = (t[None] if n in lead else t for t in (g, d, mn, vn))
        grads.append(g)
        deltas.append(d)
        new_m.append(mn)
        new_v.append(vn)
    return (loss, grad_x2.reshape(n_seq, seq, D_MODEL), *grads, *deltas, *new_m, *new_v)
```

```python
import numpy as np
import jax
import jax.numpy as jnp
from jax import lax
from jax.experimental import pallas as pl
from jax.experimental.pallas import tpu as pltpu

F32 = jnp.float32
BF16 = jnp.bfloat16

N_DEV = 8
D_MODEL = 1024
D_CONV = 512
D_ATT = 512
HEAD_DIM = 64
CONV_WIDTH = 31
D_IN = 2 * D_CONV + 3 * D_ATT
D_FF = 2816
N_MOD = 6
EPS = 1e-6
RADIUS = 64
DILATIONS = (1, 4, 16)
Q_BLOCK = 128
LANES = 128
VMEM_LIMIT = 56 * 1024 * 1024

ADAM_LR = 0.001
ADAM_B1 = 0.9
ADAM_B2 = 0.999
ADAM_EPS = 1e-08
ADAM_WD = 0.01
ADAM_STEP = 10

NT = (((1,), (1,)), ((), ()))
TN = (((0,), (0,)), ((), ()))


def _call(body, **kw):
    return pl.pallas_call(body, **kw)


def _params(sem=None, vmem=VMEM_LIMIT):
    return pltpu.CompilerParams(dimension_semantics=sem, vmem_limit_bytes=vmem)


def _sig(x):
    return 1.0 / (1.0 + jnp.exp(-x))


def _sds(shape, dtype):
    return jax.ShapeDtypeStruct(shape, dtype)


N_PEER = N_DEV - 1
ANY_SPEC = pl.BlockSpec(memory_space=pl.ANY)


def _exchange_copies(scatter, ins, outs, *sems):
    n = len(ins)
    if n == 0:
        return [], []
    send_sems, recv_sems, local_sems = sems
    x, y, c = lax.axis_index("x"), lax.axis_index("y"), lax.axis_index("c")
    me = 4 * x + 2 * y + c

    def src(a, slot):
        return ins[a].at[slot] if scatter[a] else ins[a]

    local = [pltpu.make_async_copy(src(a, me), outs[a].at[me], local_sems.at[a]) for a in range(n)]
    flights = []
    for k in range(1, N_DEV):
        px = 1 - x if k & 4 else x
        py = 1 - y if k & 2 else y
        pc = 1 - c if k & 1 else c
        pid = 4 * px + 2 * py + pc
        for a in range(n):
            i = a * N_PEER + k - 1
            send, recv = (pltpu.make_async_remote_copy(
                src_ref=src(a, pid), dst_ref=outs[a].at[slot],
                send_sem=send_sems.at[i], recv_sem=recv_sems.at[i],
                device_id=(px, py, pc), device_id_type=pl.DeviceIdType.MESH) for slot in (me, pid))
            flights.append((send, recv))
    return local, flights


def _exchange_start(*args):
    local, flights = _exchange_copies(*args)
    for cp in local:
        cp.start()
    for send, _ in flights:
        send.start()


def _exchange_wait(*args):
    local, flights = _exchange_copies(*args)
    for send, recv in flights:
        send.wait_send()
        recv.wait_recv()
    for cp in local:
        cp.wait()


def _exchange_shapes(items):
    return [_sds((N_DEV,) + tuple(arr.shape[1:] if scatter else arr.shape), arr.dtype) for arr, scatter in items]


def _exchange_sems(n):
    if n == 0:
        return []
    return [pltpu.SemaphoreType.DMA((n * N_PEER,)), pltpu.SemaphoreType.DMA((n * N_PEER,)),
            pltpu.SemaphoreType.DMA((n,))]


def _gather_by_chip_phase(phase, ins, outs, send_sems, recv_sems, local_sems):
    n = len(ins)
    per = N_PEER
    x, y, c = lax.axis_index("x"), lax.axis_index("y"), lax.axis_index("c")
    me, sibling = (x, y, c), (x, y, 1 - c)
    chips = [(1 - x, y), (x, 1 - y), (1 - x, 1 - y)]

    def slot(px, py, pc):
        return 4 * px + 2 * py + pc

    def copy(a, k, block, to, src=None):
        dst = outs[a].at[slot(*block)]
        return pltpu.make_async_remote_copy(
            src_ref=dst if src is None else src, dst_ref=dst,
            send_sem=send_sems.at[a * per + k], recv_sem=recv_sems.at[a * per + k],
            device_id=to, device_id_type=pl.DeviceIdType.MESH)

    local = [pltpu.make_async_copy(ins[a], outs[a].at[slot(*me)], local_sems.at[a]) for a in range(n)]
    first = []
    for a in range(n):
        first.append(copy(a, 0, me, sibling, src=ins[a]))
        first += [copy(a, 1 + j, me, (*chip, c), src=ins[a]) for j, chip in enumerate(chips)]
    passed = [copy(a, 4 + j, (*chip, c), sibling) for j, chip in enumerate(chips) for a in range(n)]
    if phase == 0:
        for cp in local + first:
            cp.start()
    elif phase == 1:
        for j, chip in enumerate(chips):
            for a in range(n):
                copy(a, 1 + j, (*chip, c), me).wait_recv()
        for cp in passed:
            cp.start()
    else:
        for a in range(n):
            copy(a, 0, sibling, me).wait_recv()
            for j, chip in enumerate(chips):
                copy(a, 4 + j, (*chip, 1 - c), me).wait_recv()
        for cp in first + passed:
            cp.wait_send()
        for cp in local:
            cp.wait()


def _gather_by_chip(arrays, name):
    n = len(arrays)

    def body(*refs):
        for phase in range(3):
            _gather_by_chip_phase(phase, refs[:n], refs[n:2 * n], *refs[2 * n:])

    return _call(
        body, name=name, out_shape=_exchange_shapes([(arr, False) for arr in arrays]),
        in_specs=[ANY_SPEC] * n, out_specs=[ANY_SPEC] * n, scratch_shapes=_exchange_sems(n),
    )(*arrays)


def _exchange(items, name):
    n = len(items)
    scatter = [s for _, s in items]

    def body(*refs):
        args = (scatter, refs[:n], refs[n:2 * n]) + tuple(refs[2 * n:])
        _exchange_start(*args)
        _exchange_wait(*args)

    return _call(
        body, name=name, out_shape=_exchange_shapes(items),
        in_specs=[ANY_SPEC] * n, out_specs=[ANY_SPEC] * n, scratch_shapes=_exchange_sems(n),
    )(*[a for a, _ in items])


def _ada_fwd(c_all, w_ada, b_cols):
    def body(c_ref, w_ref, b_ref, o_ref):
        cv = c_ref[...]
        sc = (cv * _sig(cv)).astype(BF16)
        o_ref[...] = jnp.dot(sc, w_ref[...].astype(BF16), preferred_element_type=F32) + b_ref[...]

    return _call(body, name="ada_fwd", out_shape=_sds((c_all.shape[0], w_ada.shape[1]), F32),
                 compiler_params=_params())(c_all, w_ada, b_cols)


def _ada_bwd(c_all, dmod_cols, dmod_all):
    def body(c_ref, dc_ref, da_ref, gw_ref, gb_ref):
        cv = c_ref[...]
        sc = (cv * _sig(cv)).astype(BF16)
        gw_ref[...] = lax.dot_general(sc, dc_ref[...].astype(BF16), TN, preferred_element_type=F32)
        gb_ref[...] = jnp.sum(da_ref[...], axis=0, keepdims=True)

    return _call(body, name="ada_bwd",
                 out_shape=[_sds((c_all.shape[1], dmod_cols.shape[1]), F32), _sds((1, dmod_all.shape[1]), F32)],
                 compiler_params=_params())(c_all, dmod_cols, dmod_all)


MIX_ROWS = 128


def _mix_in(x2, mod8, g_mix, w_in, seq, tm=512):
    tokens = x2.shape[0]
    per_seq = seq // tm

    def body(x_ref, m_ref, g_ref, wt_ref, h_ref, p_ref, w_ref):
        @pl.when(pl.program_id(0) == 0)
        def _():
            w_ref[...] = wt_ref[...].T

        def normed(c):
            rows = pl.ds(c * MIX_ROWS, MIX_ROWS)
            xv = x_ref[rows, :]
            r = lax.rsqrt(jnp.mean(xv * xv, axis=-1, keepdims=True) + EPS)
            hb = ((xv * r * g_ref[...]) * (1.0 + m_ref[1:2, :]) + m_ref[0:1, :]).astype(BF16)
            h_ref[rows, :] = hb
            return hb

        ahead = normed(0)
        for c in range(tm // MIX_ROWS):
            hb = ahead
            if c + 1 < tm // MIX_ROWS:
                ahead = normed(c + 1)
            p = jnp.dot(hb, w_ref[...], preferred_element_type=F32)
            for cb in range(D_IN // LANES):
                p_ref[cb, pl.ds(c * MIX_ROWS, MIX_ROWS), :] = p[:, cb * LANES:(cb + 1) * LANES]

    return _call(
        body, name="mix_in", grid=(tokens // tm,),
        in_specs=[pl.BlockSpec((tm, D_MODEL), lambda i: (i, 0)),
                  pl.BlockSpec((None, 8, D_MODEL), lambda i: (i // per_seq, 0, 0)),
                  pl.BlockSpec((1, D_MODEL), lambda i: (0, 0)),
                  pl.BlockSpec((D_IN, D_MODEL), lambda i: (0, 0))],
        out_specs=[pl.BlockSpec((tm, D_MODEL), lambda i: (i, 0)),
                   pl.BlockSpec((D_IN // LANES, tm, LANES), lambda i: (0, i, 0))],
        out_shape=[_sds((tokens, D_MODEL), BF16), _sds((D_IN // LANES, tokens, LANES), F32)],
        scratch_shapes=[pltpu.VMEM((D_MODEL, D_IN), BF16)],
        compiler_params=_params(("arbitrary",)),
    )(x2, mod8, g_mix, w_in)


CONV_ROWS = 64
CONV_DW_ROWS = 32
CONV_DW_UNROLL = 4
CONV_HALO = 16


def _fill_shifted(xp, sh, seq):
    for b in range(8):
        sh[b, pl.ds(0, seq + 24), :] = xp[pl.ds(b, seq + 24), :]


def _conv_fwd(proj3, w_dw, b_dw):
    _, n_seq, seq, _ = proj3.shape
    n_cb = D_CONV // LANES

    def body(a_ref, g_ref, w_ref, b_ref, uc_ref, xp, sh):
        zeros = jnp.zeros((CONV_HALO, LANES), F32)
        xp[pl.ds(0, CONV_HALO), :] = zeros
        xp[pl.ds(CONV_HALO + seq, CONV_HALO), :] = zeros
        xp[pl.ds(CONV_HALO, seq), :] = a_ref[...] * _sig(g_ref[...])
        _fill_shifted(xp, sh, seq)

        def blk(i, carry):
            t0 = pl.multiple_of(i * CONV_ROWS, CONV_ROWS)
            acc = jnp.zeros((CONV_ROWS, LANES), F32)
            for j in range(CONV_WIDTH):
                jj = j + 1
                acc = acc + sh[jj % 8, pl.ds(t0 + 8 * (jj // 8), CONV_ROWS), :] * w_ref[j:j + 1, :]
            uc_ref[pl.ds(t0, CONV_ROWS), :] = acc + b_ref[...]
            return carry

        lax.fori_loop(0, seq // CONV_ROWS, blk, 0)

    return _call(
        body, name="conv_fwd", grid=(n_seq, n_cb),
        in_specs=[pl.BlockSpec((None, None, seq, LANES), lambda b, cb: (cb, b, 0, 0)),
                  pl.BlockSpec((None, None, seq, LANES), lambda b, cb: (n_cb + cb, b, 0, 0)),
                  pl.BlockSpec((CONV_WIDTH, LANES), lambda b, cb: (0, cb)),
                  pl.BlockSpec((1, LANES), lambda b, cb: (0, cb))],
        out_specs=pl.BlockSpec((None, seq, LANES), lambda b, cb: (b, 0, cb)),
        out_shape=_sds((n_seq, seq, D_CONV), F32),
        scratch_shapes=[pltpu.VMEM((seq + 2 * CONV_HALO, LANES), F32),
                        pltpu.VMEM((8, seq + 2 * CONV_HALO, LANES), F32)],
        compiler_params=_params(("parallel", "parallel")),
    )(proj3, proj3, w_dw, b_dw)


def _conv_bwd(duc3, proj3, w_dw):
    _, n_seq, seq, _ = proj3.shape
    n_cb = D_CONV // LANES

    def body(duc_ref, a_ref, g_ref, w_ref, da_ref, dg_ref, dw_ref, db_ref, xp, sh):
        @pl.when(pl.program_id(1) == 0)
        def _():
            dw_ref[...] = jnp.zeros_like(dw_ref)
            db_ref[...] = jnp.zeros_like(db_ref)

        zeros = jnp.zeros((CONV_HALO, LANES), F32)
        xp[pl.ds(0, CONV_HALO), :] = zeros
        xp[pl.ds(CONV_HALO + seq, CONV_HALO), :] = zeros
        xp[pl.ds(CONV_HALO, seq), :] = a_ref[...] * _sig(g_ref[...])
        _fill_shifted(xp, sh, seq)
        for j0 in range(0, CONV_WIDTH, 8):
            taps = range(j0, min(j0 + 8, CONV_WIDTH))

            def wblk(i, accs, taps=taps):
                for u in range(CONV_DW_UNROLL):
                    t0 = pl.multiple_of((i * CONV_DW_UNROLL + u) * CONV_DW_ROWS, CONV_DW_ROWS)
                    d = duc_ref[pl.ds(t0, CONV_DW_ROWS), :]
                    accs = tuple(acc + d * sh[(j + 1) % 8, pl.ds(t0 + 8 * ((j + 1) // 8), CONV_DW_ROWS), :]
                                 for acc, j in zip(accs, taps))
                return accs

            accs = lax.fori_loop(0, seq // (CONV_DW_ROWS * CONV_DW_UNROLL), wblk,
                                 tuple(jnp.zeros((CONV_DW_ROWS, LANES), F32) for _ in taps))
            for acc, j in zip(accs, taps):
                dw_ref[j:j + 1, :] += jnp.sum(acc, axis=0, keepdims=True)
        db_ref[0:1, :] += jnp.sum(duc_ref[...], axis=0, keepdims=True)
        xp[pl.ds(CONV_HALO, seq), :] = duc_ref[...]
        _fill_shifted(xp, sh, seq)

        def ublk(i, carry):
            t0 = pl.multiple_of(i * CONV_ROWS, CONV_ROWS)
            acc = jnp.zeros((CONV_ROWS, LANES), F32)
            for j in range(CONV_WIDTH):
                jj = CONV_WIDTH - j
                acc = acc + sh[jj % 8, pl.ds(t0 + 8 * (jj // 8), CONV_ROWS), :] * w_ref[j:j + 1, :]
            av = a_ref[pl.ds(t0, CONV_ROWS), :]
            sg = _sig(g_ref[pl.ds(t0, CONV_ROWS), :])
            da_ref[pl.ds(t0, CONV_ROWS), :] = (acc * sg).astype(BF16)
            dg_ref[pl.ds(t0, CONV_ROWS), :] = (acc * av * sg * (1.0 - sg)).astype(BF16)
            return carry

        lax.fori_loop(0, seq // CONV_ROWS, ublk, 0)

    return _call(
        body, name="conv_bwd", grid=(n_cb, n_seq),
        in_specs=[pl.BlockSpec((None, seq, LANES), lambda cb, b: (b, 0, cb)),
                  pl.BlockSpec((None, None, seq, LANES), lambda cb, b: (cb, b, 0, 0)),
                  pl.BlockSpec((None, None, seq, LANES), lambda cb, b: (n_cb + cb, b, 0, 0)),
                  pl.BlockSpec((CONV_WIDTH, LANES), lambda cb, b: (0, cb))],
        out_specs=[pl.BlockSpec((None, seq, LANES), lambda cb, b: (b, 0, cb)),
                   pl.BlockSpec((None, seq, LANES), lambda cb, b: (b, 0, cb)),
                   pl.BlockSpec((32, LANES), lambda cb, b: (0, cb)),
                   pl.BlockSpec((8, LANES), lambda cb, b: (0, cb))],
        out_shape=[_sds((n_seq, seq, D_CONV), BF16), _sds((n_seq, seq, D_CONV), BF16),
                   _sds((32, D_CONV), F32), _sds((8, D_CONV), F32)],
        scratch_shapes=[pltpu.VMEM((seq + 2 * CONV_HALO, LANES), F32),
                        pltpu.VMEM((8, seq + 2 * CONV_HALO, LANES), F32)],
        compiler_params=_params(("parallel", "arbitrary")),
    )(duc3, proj3, proj3, w_dw)


MASKED = 1e30
ATT_ROWS = 512
ATT_UNROLL = 8
ATT_FWD_UNROLL = 8


def _distance_mats(dil, seg_len):
    kw = min(2 * Q_BLOCK, seg_len)
    offsets = (0, -RADIUS, -2 * RADIUS) if kw == 2 * Q_BLOCK else (0,)
    a = np.arange(Q_BLOCK)[:, None]
    b = np.arange(kw)[None, :]
    mats = []
    for off in offsets:
        rel = np.abs(b + off - a)
        mats.append(np.where(rel <= RADIUS, dil * rel, MASKED))
    return jnp.asarray(np.stack(mats).astype(np.float32))


def _alibi_rows():
    s = np.zeros((4, 8, LANES), np.float32)
    for hp in range(4):
        for hl in range(2):
            s[hp, hl, :] = 2.0 ** (-(2 * hp + hl + 1))
    return jnp.asarray(s)


def _window(n, seg_len):
    i0 = pl.multiple_of(n * Q_BLOCK, Q_BLOCK)
    if seg_len <= Q_BLOCK:
        return i0, i0, 0
    per_seg = seg_len // Q_BLOCK
    j = n % per_seg
    seg0 = (n // per_seg) * seg_len
    ks_local = jnp.clip(j * Q_BLOCK - RADIUS, 0, seg_len - 2 * Q_BLOCK)
    ks = pl.multiple_of(seg0 + ks_local, RADIUS)
    var = jnp.where(j == 0, 0, jnp.where(j == per_seg - 1, 2, 1))
    return i0, ks, var


def _first_head(rows):
    return lax.broadcasted_iota(jnp.int32, (rows, LANES), 1) < HEAD_DIM


def _same_head():
    head = np.arange(LANES) // HEAD_DIM
    return jnp.asarray((head[:, None] == head[None, :]).astype(np.float32)).astype(BF16)


def _head_sum(x, same_ref):
    hi = x.astype(BF16)
    lo = (x - hi.astype(F32)).astype(BF16)
    return (jnp.dot(hi, same_ref[...], preferred_element_type=F32)
            + jnp.dot(lo, same_ref[...], preferred_element_type=F32))


def _head_mean(x, same_ref):
    return _head_sum(x, same_ref) * (1.0 / HEAD_DIM)


def _per_head(x, first):
    swapped = pltpu.roll(x, HEAD_DIM, 1)
    return jnp.where(first, x, swapped), jnp.where(first, swapped, x)


STRIDE = 4


def _gather_segments(src, dil, seq, tmp, put):
    if dil == 1:
        put(0, seq, src[pl.ds(0, seq), :])
    elif dil == STRIDE:
        seg = seq // dil
        for r in range(dil):
            put(r * seg, seg, src[pl.ds(r, seg, stride=dil), :])
    else:
        part, seg = seq // STRIDE, seq // dil
        for b in range(STRIDE):
            tmp[pl.ds(b * part, part), :] = src[pl.ds(b, part, stride=STRIDE), :]
        for b in range(STRIDE):
            for a in range(dil // STRIDE):
                put(b * part + a * seg, seg, tmp[pl.ds(b * part + a, seg, stride=dil // STRIDE), :])


def _scatter_segments(dst, get, dil, seq, tmp, accumulate):
    def write(rows, val):
        if accumulate:
            dst[rows, :] += val
        else:
            dst[rows, :] = val

    if dil == 1:
        write(pl.ds(0, seq), get(0, seq))
    elif dil == STRIDE:
        seg = seq // dil
        for r in range(dil):
            write(pl.ds(r, seg, stride=dil), get(r * seg, seg))
    else:
        part, seg = seq // STRIDE, seq // dil
        for b in range(STRIDE):
            for a in range(dil // STRIDE):
                tmp[pl.ds(b * part + a, seg, stride=dil // STRIDE), :] = get(b * part + a * seg, seg)
        for b in range(STRIDE):
            write(pl.ds(b, part, stride=STRIDE), tmp[pl.ds(b * part, part), :])


def _permute_rows(dst, src, dil, seq, tmp):
    def put(start, size, val):
        dst[pl.ds(start, size), :] = val.astype(dst.dtype)

    _gather_segments(src, dil, seq, tmp, put)


def _permute_rows_by_head(dst, src, dil, seq, tmp):
    def put(start, size, val):
        first = _first_head(size)
        dst[0, pl.ds(start, size), :] = jnp.where(first, val, 0.0).astype(dst.dtype)
        dst[1, pl.ds(start, size), :] = jnp.where(first, 0.0, val).astype(dst.dtype)

    _gather_segments(src, dil, seq, tmp, put)


def _qk_normalise(q_ref, g2_ref, same_ref, dst, seq, scale):
    def chunk(ci, carry):
        rows = pl.ds(pl.multiple_of(ci * ATT_ROWS, ATT_ROWS), ATT_ROWS)
        qv = q_ref[rows, :]
        r = lax.rsqrt(_head_mean(qv * qv, same_ref) + EPS)
        dst[rows, :] = qv * r * (g2_ref[...] * scale)
        return carry

    lax.fori_loop(0, seq // ATT_ROWS, chunk, 0)


def _attn_fwd(proj3, g_q2, g_k2, ride):
    _, n_seq, seq, _ = proj3.shape
    dms = [_distance_mats(d, seq // d) for d in DILATIONS]
    same = _same_head()
    col0 = 2 * D_CONV // LANES
    n_hp = D_ATT // LANES

    n_ride = len(ride)
    assert not any(scatter for _, scatter in ride), "the forward's ride is an all-gather"

    def body(*refs):
        q_ref, k_ref, v_ref, gq_ref, gk_ref, sl_ref, dm1, dm4, dm16, same_ref = refs[:10]
        ride_in = refs[10:10 + n_ride]
        y_ref, lse_ref = refs[10 + n_ride:12 + n_ride]
        ride_out = refs[12 + n_ride:12 + 2 * n_ride]
        (qf, kf, qp, kp, vp, oml_p, o1, o4, o16, m1, m4, m16, l1, l4, l16,
         tmp) = refs[12 + 2 * n_ride:28 + 2 * n_ride]
        o_nat, m_nat, l_nat = (o1, o4, o16), (m1, m4, m16), (l1, l4, l16)
        ride_args = (ride_in, ride_out) + tuple(refs[28 + 2 * n_ride:])
        step = pl.program_id(0) * n_hp + pl.program_id(1)
        n_steps = n_seq * n_hp

        if n_ride:
            for phase, at in enumerate((0, (3 * n_steps) // 4)):
                @pl.when(step == at)
                def _(phase=phase):
                    _gather_by_chip_phase(phase, *ride_args)

        dm_refs = (dm1, dm4, dm16)
        _qk_normalise(q_ref, gq_ref, same_ref, qf, seq, HEAD_DIM ** -0.5)
        _qk_normalise(k_ref, gk_ref, same_ref, kf, seq, 1.0)
        slopes = (sl_ref[0:1, 0:1], sl_ref[1:2, 0:1])
        for pi, dil in enumerate(DILATIONS):
            seg = seq // dil
            kw = min(2 * Q_BLOCK, seg)
            _permute_rows_by_head(qp, qf, dil, seq, tmp)
            _permute_rows(kp, kf, dil, seq, tmp)
            _permute_rows(vp, v_ref, dil, seq, tmp)

            def blk(it, carry, seg=seg, kw=kw, pi=pi, dst=oml_p):
                first = _first_head(Q_BLOCK)
                chains = [(sub, h) for sub in range(ATT_FWD_UNROLL) for h in range(2)]
                win = [_window(it * ATT_FWD_UNROLL + sub, seg) for sub in range(ATT_FWD_UNROLL)]
                s = {}
                for sub, h in chains:
                    i0, ks, var = win[sub]
                    s[sub, h] = lax.dot_general(qp[h, pl.ds(i0, Q_BLOCK), :], kp[pl.ds(ks, kw), :], NT,
                                                preferred_element_type=F32) - slopes[h] * dm_refs[pi][var]
                m, l, p = {}, {}, {}
                for c in chains:
                    m[c] = jnp.max(s[c], axis=1, keepdims=True)
                    e = jnp.exp(s[c] - m[c])
                    l[c] = jnp.sum(e, axis=1, keepdims=True)
                    p[c] = e.astype(BF16)
                o = {}
                for sub, h in chains:
                    o[sub, h] = jnp.dot(p[sub, h], vp[pl.ds(win[sub][1], kw), :], preferred_element_type=F32)
                packed = [jnp.concatenate([jnp.where(first, t[sub, 0], t[sub, 1]) for t in (o, m, l)], axis=1)
                          for sub in range(ATT_FWD_UNROLL)]
                span = ATT_FWD_UNROLL * Q_BLOCK
                dst[pl.ds(pl.multiple_of(it * span, span), span), :] = jnp.concatenate(packed, axis=0)
                return carry

            lax.fori_loop(0, seq // (Q_BLOCK * ATT_FWD_UNROLL), blk, 0)
            for n, nat in enumerate((o_nat[pi], m_nat[pi], l_nat[pi])):
                _scatter_segments(nat, lambda start, size, n=n: oml_p[pl.ds(start, size), pl.ds(n * LANES, LANES)],
                                  dil, seq, tmp, accumulate=False)

        def merge(ci, carry):
            rows = pl.ds(pl.multiple_of(ci * ATT_ROWS, ATT_ROWS), ATT_ROWS)
            ms = [m_nat[pi][rows, :] for pi in range(3)]
            m_all = jnp.maximum(jnp.maximum(ms[0], ms[1]), ms[2])
            es = [jnp.exp(m - m_all) for m in ms]
            l_all = sum(l_nat[pi][rows, :] * es[pi] for pi in range(3))
            inv = 1.0 / l_all
            o = sum(o_nat[pi][rows, :] * (es[pi] * inv) for pi in range(3))
            y_ref[rows, :] = o.astype(BF16)
            lse_ref[rows, :] = m_all + jnp.log(l_all)
            return carry

        lax.fori_loop(0, seq // ATT_ROWS, merge, 0)

        if n_ride:
            @pl.when(step == n_steps - 1)
            def _():
                _gather_by_chip_phase(2, *ride_args)

    def col(off):
        return pl.BlockSpec((None, None, seq, LANES), lambda b, hp: (col0 + off * n_hp + hp, b, 0, 0))

    def whole(arr):
        return pl.BlockSpec(arr.shape, lambda b, hp: (0,) * arr.ndim)

    rows_f32 = pltpu.VMEM((seq, LANES), F32)
    rows_bf16 = pltpu.VMEM((seq, LANES), BF16)
    return _call(
        body, name="attn_fwd", grid=(n_seq, n_hp),
        in_specs=[col(0), col(1), col(2), whole(g_q2), whole(g_k2),
                  pl.BlockSpec((None, 8, LANES), lambda b, hp: (hp, 0, 0)),
                  whole(dms[0]), whole(dms[1]), whole(dms[2]), whole(same)] + [ANY_SPEC] * n_ride,
        out_specs=[pl.BlockSpec((None, seq, LANES), lambda b, hp: (b, 0, hp)),
                   pl.BlockSpec((None, seq, LANES), lambda b, hp: (b, 0, hp))] + [ANY_SPEC] * n_ride,
        out_shape=[_sds((n_seq, seq, D_ATT), BF16), _sds((n_seq, seq, D_ATT), F32)] + _exchange_shapes(ride),
        scratch_shapes=[rows_f32, rows_f32, pltpu.VMEM((2, seq, LANES), BF16), rows_bf16, rows_bf16]
        + [pltpu.VMEM((seq, 3 * LANES), F32)] + [rows_f32] * 10 + _exchange_sems(n_ride),
        compiler_params=_params(("arbitrary", "arbitrary")),
    )(proj3, proj3, proj3, g_q2, g_k2, _alibi_rows(), *dms, same, *[a for a, _ in ride])


def _attn_bwd(proj3, do3, y_att3, lse3, g_q2, g_k2, ride):
    _, n_seq, seq, _ = proj3.shape
    dms = [_distance_mats(d, seq // d) for d in DILATIONS]
    same = _same_head()
    col0 = 2 * D_CONV // LANES
    n_hp = D_ATT // LANES

    n_ride = len(ride)
    ride_scatter = [s for _, s in ride]

    def body(*refs):
        (q_ref, k_ref, v_ref, do_ref, o_ref, lse_ref, gq_ref, gk_ref, sl_ref, dm1, dm4, dm16,
         same_ref) = refs[:13]
        ride_in = refs[13:13 + n_ride]
        dq_ref, dk_ref, dv_ref, dg_ref = refs[13 + n_ride:17 + n_ride]
        ride_out = refs[17 + n_ride:17 + 2 * n_ride]
        (qf, kf, qp, dop, kp, vp, sn, sp, dqp, dkp, dvp, dqn, dkn, dvn,
         tmp) = refs[17 + 2 * n_ride:32 + 2 * n_ride]
        ride_args = (ride_scatter, ride_in, ride_out) + tuple(refs[32 + 2 * n_ride:])
        dm_refs = (dm1, dm4, dm16)
        step = pl.program_id(0) * n_hp + pl.program_id(1)

        @pl.when(step == 0)
        def _():
            _exchange_start(*ride_args)
            dg_ref[...] = jnp.zeros_like(dg_ref)

        _qk_normalise(q_ref, gq_ref, same_ref, qf, seq, HEAD_DIM ** -0.5)
        _qk_normalise(k_ref, gk_ref, same_ref, kf, seq, 1.0)

        def stats(ci, carry):
            rows = pl.ds(pl.multiple_of(ci * ATT_ROWS, ATT_ROWS), ATT_ROWS)
            first = _first_head(ATT_ROWS)
            sn[0, rows, :], sn[1, rows, :] = _per_head(lse_ref[rows, :], first)
            prod = do_ref[rows, :] * o_ref[rows, :].astype(F32)
            sn[2, rows, :], sn[3, rows, :] = _per_head(_head_sum(prod, same_ref), first)
            return carry

        lax.fori_loop(0, seq // ATT_ROWS, stats, 0)
        slopes = (sl_ref[0:1, 0:1], sl_ref[1:2, 0:1])
        half = seq // (Q_BLOCK * ATT_UNROLL)
        region = seq // ATT_UNROLL

        for pi, dil in enumerate(DILATIONS):
            seg = seq // dil
            kw = min(2 * Q_BLOCK, seg)
            _permute_rows_by_head(qp, qf, dil, seq, tmp)
            _permute_rows_by_head(dop, do_ref, dil, seq, tmp)
            _permute_rows(kp, kf, dil, seq, tmp)
            _permute_rows(vp, v_ref, dil, seq, tmp)
            if dil == 1:
                st = sn
            else:
                st = sp
                for n in range(4):
                    _permute_rows(sp.at[n], sn.at[n], dil, seq, tmp)
            def touched(sub, seg=seg):
                lo, hi = sub * region, (sub + 1) * region
                if seg < region:
                    return lo, hi
                seg0 = lo // seg * seg
                return max(lo - RADIUS, seg0), min(hi + RADIUS, seg0 + seg)

            def summed(acc, start, size, touched=touched):
                pieces = []
                for c0 in range(start, start + size, RADIUS):
                    owners = [s for s in range(ATT_UNROLL) if touched(s)[0] <= c0 and c0 + RADIUS <= touched(s)[1]]
                    if pieces and pieces[-1][2] == owners:
                        pieces[-1][1] += RADIUS
                    else:
                        pieces.append([c0, RADIUS, owners])
                vals = [sum(acc[o, pl.ds(c0, n), :] for o in owners) for c0, n, owners in pieces]
                return vals[0] if len(vals) == 1 else jnp.concatenate(vals, axis=0)

            for sub in range(ATT_UNROLL):
                lo, hi = touched(sub)
                dkp[sub, pl.ds(lo, hi - lo), :] = jnp.zeros((hi - lo, LANES), F32)
                dvp[sub, pl.ds(lo, hi - lo), :] = jnp.zeros((hi - lo, LANES), F32)

            def blk(it, carry, seg=seg, kw=kw, pi=pi, st=st):
                first = _first_head(Q_BLOCK)
                chains = [(sub, h) for sub in range(ATT_UNROLL) for h in range(2)]
                win = [_window(it + sub * half, seg) for sub in range(ATT_UNROLL)]
                qrows = [pl.ds(w[0], Q_BLOCK) for w in win]
                krows = [pl.ds(w[1], kw) for w in win]

                def over_keys(n, sub):
                    t = st[n, qrows[sub], :]
                    return t if kw == LANES else jnp.concatenate([t] * (kw // LANES), axis=1)

                s, dp = {}, {}
                for sub, h in chains:
                    s[sub, h] = lax.dot_general(qp[h, qrows[sub], :], kp[krows[sub], :], NT,
                                                preferred_element_type=F32) - slopes[h] * dm_refs[pi][win[sub][2]]
                    dp[sub, h] = lax.dot_general(dop[h, qrows[sub], :], vp[krows[sub], :], NT,
                                                 preferred_element_type=F32)
                p, ds = {}, {}
                for sub, h in chains:
                    e = jnp.exp(s[sub, h] - over_keys(h, sub))
                    ds[sub, h] = (e * (dp[sub, h] - over_keys(2 + h, sub))).astype(BF16)
                    p[sub, h] = e.astype(BF16)
                dq, dk, dv = {}, {}, {}
                for sub, h in chains:
                    dq[sub, h] = jnp.dot(ds[sub, h], kp[krows[sub], :], preferred_element_type=F32)
                    dk[sub, h] = lax.dot_general(ds[sub, h], qp[h, qrows[sub], :], TN, preferred_element_type=F32)
                    dv[sub, h] = lax.dot_general(p[sub, h], dop[h, qrows[sub], :], TN, preferred_element_type=F32)
                for sub in range(ATT_UNROLL):
                    dqp[qrows[sub], :] = jnp.where(first, dq[sub, 0], dq[sub, 1])
                    dkp[sub, krows[sub], :] += dk[sub, 0] + dk[sub, 1]
                    dvp[sub, krows[sub], :] += dv[sub, 0] + dv[sub, 1]
                return carry

            lax.fori_loop(0, half, blk, 0)
            first_pattern = pi == 0
            if first_pattern:
                for r0 in range(0, seq, region):
                    rows = pl.ds(r0, region)
                    dqn[rows, :] = dqp[rows, :]
                    dkn[rows, :] = summed(dkp, r0, region)
                    dvn[rows, :] = summed(dvp, r0, region)
            else:
                _scatter_segments(dqn, lambda start, size: dqp[pl.ds(start, size), :], dil, seq, tmp, accumulate=True)
                for nat, acc in ((dkn, dkp), (dvn, dvp)):
                    _scatter_segments(nat, lambda start, size, acc=acc: summed(acc, start, size),
                                      dil, seq, tmp, accumulate=True)

        def finish(ci, carry):
            rows = pl.ds(pl.multiple_of(ci * ATT_ROWS, ATT_ROWS), ATT_ROWS)
            for src_ref, g_ref, dn, dst_ref, scale, row in (
                    (q_ref, gq_ref, dqn, dq_ref, HEAD_DIM ** -0.5, 0), (k_ref, gk_ref, dkn, dk_ref, 1.0, 1)):
                xv = src_ref[rows, :]
                r = lax.rsqrt(_head_mean(xv * xv, same_ref) + EPS)
                xhat = xv * r
                d = dn[rows, :] * scale
                dg_ref[row:row + 1, :] += jnp.sum(d * xhat, axis=0, keepdims=True)
                dxh = d * g_ref[...]
                dst_ref[rows, :] = (r * (dxh - xhat * _head_mean(dxh * xhat, same_ref))).astype(BF16)
            dv_ref[rows, :] = dvn[rows, :].astype(BF16)
            return carry

        lax.fori_loop(0, seq // ATT_ROWS, finish, 0)

        @pl.when(step == n_seq * n_hp - 1)
        def _():
            _exchange_wait(*ride_args)

    def col(off):
        return pl.BlockSpec((None, None, seq, LANES), lambda b, hp: (col0 + off * n_hp + hp, b, 0, 0))

    def whole(arr):
        return pl.BlockSpec(arr.shape, lambda b, hp: (0,) * arr.ndim)

    att = pl.BlockSpec((None, seq, LANES), lambda b, hp: (b, 0, hp))
    rows_f32 = pltpu.VMEM((seq, LANES), F32)
    rows_bf16 = pltpu.VMEM((seq, LANES), BF16)
    by_head_bf16 = pltpu.VMEM((2, seq, LANES), BF16)
    per_sub_f32 = pltpu.VMEM((ATT_UNROLL, seq, LANES), F32)
    stats_f32 = pltpu.VMEM((4, seq, LANES), F32)
    return _call(
        body, name="attn_bwd", grid=(n_seq, n_hp),
        in_specs=[col(0), col(1), col(2), att, att, att, whole(g_q2), whole(g_k2),
                  pl.BlockSpec((None, 8, LANES), lambda b, hp: (hp, 0, 0)),
                  whole(dms[0]), whole(dms[1]), whole(dms[2]), whole(same)] + [ANY_SPEC] * n_ride,
        out_specs=[att, att, att, pl.BlockSpec((8, LANES), lambda b, hp: (0, 0))] + [ANY_SPEC] * n_ride,
        out_shape=[_sds((n_seq, seq, D_ATT), BF16)] * 3 + [_sds((8, LANES), F32)] + _exchange_shapes(ride),
        scratch_shapes=[rows_f32, rows_f32, by_head_bf16, by_head_bf16, rows_bf16, rows_bf16, stats_f32, stats_f32,
                        rows_f32, per_sub_f32, per_sub_f32, rows_f32, rows_f32, rows_f32, rows_f32]
        + _exchange_sems(n_ride),
        compiler_params=_params(("arbitrary", "arbitrary")),
    )(proj3, proj3, proj3, do3, y_att3, lse3, g_q2, g_k2, _alibi_rows(), *dms, same, *[a for a, _ in ride])


def _mix_out(uc2, y_att2, x2, mod8, g_ln, b_ln, g_ffn, w_out, seq, tm=512):
    tokens = x2.shape[0]
    per_seq = seq // tm

    def body(uc_ref, ya_ref, x_ref, m_ref, gl_ref, bl_ref, gf_ref, w_ref, yc_ref, mix_ref, x1_ref, h2_ref):
        uc = uc_ref[...]
        mu = jnp.mean(uc, axis=-1, keepdims=True)
        cen = uc - mu
        rs = lax.rsqrt(jnp.mean(cen * cen, axis=-1, keepdims=True) + EPS)
        z = cen * rs * gl_ref[...] + bl_ref[...]
        yc = (z * _sig(z)).astype(BF16)
        yc_ref[...] = yc
        mix = (jnp.dot(yc, w_ref[pl.ds(0, D_CONV), :], preferred_element_type=F32)
               + jnp.dot(ya_ref[...], w_ref[pl.ds(D_CONV, D_ATT), :], preferred_element_type=F32))
        mix_ref[...] = mix.astype(BF16)
        x1 = x_ref[...] + m_ref[2:3, :] * mix
        x1_ref[...] = x1
        r = lax.rsqrt(jnp.mean(x1 * x1, axis=-1, keepdims=True) + EPS)
        h2_ref[...] = ((x1 * r * gf_ref[...]) * (1.0 + m_ref[4:5, :]) + m_ref[3:4, :]).astype(BF16)

    def rows(width):
        return pl.BlockSpec((tm, width), lambda i: (i, 0))

    def vec(width):
        return pl.BlockSpec((1, width), lambda i: (0, 0))

    return _call(
        body, name="mix_out", grid=(tokens // tm,),
        in_specs=[rows(D_CONV), rows(D_ATT), rows(D_MODEL),
                  pl.BlockSpec((None, 8, D_MODEL), lambda i: (i // per_seq, 0, 0)),
                  vec(D_CONV), vec(D_CONV), vec(D_MODEL),
                  pl.BlockSpec((D_MODEL, D_MODEL), lambda i: (0, 0))],
        out_specs=[rows(D_CONV), rows(D_MODEL), rows(D_MODEL), rows(D_MODEL)],
        out_shape=[_sds((tokens, D_CONV), BF16), _sds((tokens, D_MODEL), BF16),
                   _sds((tokens, D_MODEL), F32), _sds((tokens, D_MODEL), BF16)],
        compiler_params=_params(("parallel",)),
    )(uc2, y_att2, x2, mod8, g_ln, b_ln, g_ffn, w_out)


def _mix_out_bwd(dmix, uc2, g_ln, b_ln, w_out, tm=512):
    tokens = dmix.shape[0]

    def body(dm_ref, uc_ref, gl_ref, bl_ref, w_ref, duc_ref, do_ref, dgb_ref):
        @pl.when(pl.program_id(0) == 0)
        def _():
            dgb_ref[...] = jnp.zeros_like(dgb_ref)

        dmv = dm_ref[...]
        dyc = lax.dot_general(dmv, w_ref[pl.ds(0, D_CONV), :], NT, preferred_element_type=F32)
        do_ref[...] = lax.dot_general(dmv, w_ref[pl.ds(D_CONV, D_ATT), :], NT, preferred_element_type=F32)
        uc = uc_ref[...]
        mu = jnp.mean(uc, axis=-1, keepdims=True)
        cen = uc - mu
        rs = lax.rsqrt(jnp.mean(cen * cen, axis=-1, keepdims=True) + EPS)
        xh = cen * rs
        z = xh * gl_ref[...] + bl_ref[...]
        sg = _sig(z)
        dz = dyc * (sg * (1.0 + z * (1.0 - sg)))
        dgb_ref[0:1, :] += jnp.sum(dz * xh, axis=0, keepdims=True)
        dgb_ref[1:2, :] += jnp.sum(dz, axis=0, keepdims=True)
        dxh = dz * gl_ref[...]
        duc_ref[...] = rs * (dxh - jnp.mean(dxh, axis=-1, keepdims=True)
                             - xh * jnp.mean(dxh * xh, axis=-1, keepdims=True))

    return _call(
        body, name="mix_out_bwd", grid=(tokens // tm,),
        in_specs=[pl.BlockSpec((tm, D_MODEL), lambda i: (i, 0)),
                  pl.BlockSpec((tm, D_CONV), lambda i: (i, 0)),
                  pl.BlockSpec((1, D_CONV), lambda i: (0, 0)),
                  pl.BlockSpec((1, D_CONV), lambda i: (0, 0)),
                  pl.BlockSpec((D_MODEL, D_MODEL), lambda i: (0, 0))],
        out_specs=[pl.BlockSpec((tm, D_CONV), lambda i: (i, 0)),
                   pl.BlockSpec((tm, D_ATT), lambda i: (i, 0)),
                   pl.BlockSpec((8, D_CONV), lambda i: (0, 0))],
        out_shape=[_sds((tokens, D_CONV), F32), _sds((tokens, D_ATT), F32), _sds((8, D_CONV), F32)],
        compiler_params=_params(("arbitrary",)),
    )(dmix, uc2, g_ln, b_ln, w_out)


FF_TILE = 256
FF_TILES = D_FF // FF_TILE


def _load_once(hbm_refs, vmem_refs, sems):
    @pl.when(pl.program_id(0) == 0)
    def _():
        copies = [pltpu.make_async_copy(src, dst, sems.at[n]) for n, (src, dst) in enumerate(zip(hbm_refs, vmem_refs))]
        for cp in copies:
            cp.start()
        for cp in copies:
            cp.wait()


def _ffn_fwd(h2, w_gate_t, w_up_t, w_down, x1, target, mod8, seq, tm=512):
    tokens = h2.shape[0]
    per_seq = seq // tm
    n_seq = tokens // seq

    def body(h_ref, m_ref, x1_ref, t_ref, wg_hbm, wu_hbm, wd_hbm, gate_ref, up_ref, dy_ref, df_ref, sq_ref, dgf_ref,
             wg, wu, wd, w_sems):
        i = pl.program_id(0)
        _load_once((wg_hbm, wu_hbm, wd_hbm), (wg, wu, wd), w_sems)

        @pl.when(i == 0)
        def _():
            sq_ref[...] = jnp.zeros_like(sq_ref)

        @pl.when(i % per_seq == 0)
        def _():
            dgf_ref[...] = jnp.zeros_like(dgf_ref)

        hv = h_ref[...]

        def gate_up(t):
            rows = pl.ds(t * FF_TILE, FF_TILE)
            return (lax.dot_general(hv, wg[rows, :], NT, preferred_element_type=F32),
                    lax.dot_general(hv, wu[rows, :], NT, preferred_element_type=F32))

        fv = jnp.zeros((tm, D_MODEL), F32)
        ahead = gate_up(0)
        for t in range(FF_TILES):
            gate, up = ahead
            if t + 1 < FF_TILES:
                ahead = gate_up(t + 1)
            gate_ref[t] = gate.astype(BF16)
            up_ref[t] = up.astype(BF16)
            act = (gate * _sig(gate) * up).astype(BF16)
            fv = fv + jnp.dot(act, wd[pl.ds(t * FF_TILE, FF_TILE), :], preferred_element_type=F32)
        gate_f = m_ref[5:6, :]
        diff = x1_ref[...] + gate_f * fv - t_ref[...]
        sq_ref[0:1, :] += jnp.sum(diff * diff, axis=0, keepdims=True)
        dy = diff * (1.0 / D_MODEL)
        dy_ref[...] = dy
        df_ref[...] = (gate_f * dy).astype(BF16)
        dgf_ref[0:1, :] += jnp.sum(dy * fv, axis=0, keepdims=True)

    rows_spec = pl.BlockSpec((tm, D_MODEL), lambda i: (i, 0))
    per = pl.BlockSpec((None, 8, D_MODEL), lambda i: (i // per_seq, 0, 0))
    tiles = pl.BlockSpec((FF_TILES, tm, FF_TILE), lambda i: (0, i, 0))
    weight = pltpu.VMEM((D_FF, D_MODEL), BF16)
    return _call(
        body, name="ffn_fwd", grid=(tokens // tm,),
        in_specs=[rows_spec, per, rows_spec, rows_spec, ANY_SPEC, ANY_SPEC, ANY_SPEC],
        out_specs=[tiles, tiles, rows_spec, rows_spec, pl.BlockSpec((8, D_MODEL), lambda i: (0, 0)), per],
        out_shape=[_sds((FF_TILES, tokens, FF_TILE), BF16), _sds((FF_TILES, tokens, FF_TILE), BF16),
                   _sds((tokens, D_MODEL), F32), _sds((tokens, D_MODEL), BF16),
                   _sds((8, D_MODEL), F32), _sds((n_seq, 8, D_MODEL), F32)],
        scratch_shapes=[weight, weight, weight, pltpu.SemaphoreType.DMA((3,))],
        compiler_params=_params(("arbitrary",)),
    )(h2, mod8, x1, target, w_gate_t, w_up_t, w_down)


def _ffn_bwd(df, gate, up, w_gate_t, w_up_t, w_down, x1, dy, mix, mod8, g_ffn, seq, tm=256):
    tokens = df.shape[0]
    per_seq = seq // tm
    n_seq = tokens // seq

    def body(df_ref, gate_ref, up_ref, m_ref, g_ref, x1_ref, dy_ref, mix_ref, wg_hbm, wu_hbm, wd_hbm,
             dgate_ref, dup_ref, act_ref, dx1_ref, dmix_ref, dg_ref, dm_ref, wg, wu, wd, w_sems):
        i = pl.program_id(0)
        _load_once((wg_hbm, wu_hbm, wd_hbm), (wg, wu, wd), w_sems)

        @pl.when(i == 0)
        def _():
            dg_ref[...] = jnp.zeros_like(dg_ref)

        @pl.when(i % per_seq == 0)
        def _():
            dm_ref[...] = jnp.zeros_like(dm_ref)

        dfv = df_ref[...]

        def d_act(t):
            return lax.dot_general(dfv, wd[pl.ds(t * FF_TILE, FF_TILE), :], NT, preferred_element_type=F32)

        dh = jnp.zeros((tm, D_MODEL), F32)
        ahead = d_act(0)
        for t in range(FF_TILES):
            dact = ahead
            if t + 1 < FF_TILES:
                ahead = d_act(t + 1)
            rows = pl.ds(t * FF_TILE, FF_TILE)
            gv = gate_ref[t].astype(F32)
            uv = up_ref[t].astype(F32)
            sg = _sig(gv)
            silu = gv * sg
            act_ref[t] = (silu * uv).astype(BF16)
            dup = (dact * silu).astype(BF16)
            dgate = (dact * uv * (sg * (1.0 + gv * (1.0 - sg)))).astype(BF16)
            dup_ref[t] = dup
            dgate_ref[t] = dgate
            dh = dh + (jnp.dot(dgate, wg[rows, :], preferred_element_type=F32)
                       + jnp.dot(dup, wu[rows, :], preferred_element_type=F32))
        g = g_ref[...]
        x1v = x1_ref[...]
        rs = lax.rsqrt(jnp.mean(x1v * x1v, axis=-1, keepdims=True) + EPS)
        xhat = x1v * rs
        dm_ref[0:1, :] += jnp.sum(dh, axis=0, keepdims=True)
        dm_ref[1:2, :] += jnp.sum(dh * (xhat * g), axis=0, keepdims=True)
        dn = dh * (1.0 + m_ref[4:5, :])
        dg_ref[0:1, :] += jnp.sum(dn * xhat, axis=0, keepdims=True)
        dxh = dn * g
        dx1 = dy_ref[...] + rs * (dxh - xhat * jnp.mean(dxh * xhat, axis=-1, keepdims=True))
        dx1_ref[...] = dx1
        dm_ref[2:3, :] += jnp.sum(dx1 * mix_ref[...].astype(F32), axis=0, keepdims=True)
        dmix_ref[...] = (m_ref[2:3, :] * dx1).astype(BF16)

    rows_spec = pl.BlockSpec((tm, D_MODEL), lambda i: (i, 0))
    per = pl.BlockSpec((None, 8, D_MODEL), lambda i: (i // per_seq, 0, 0))
    tiles = pl.BlockSpec((FF_TILES, tm, FF_TILE), lambda i: (0, i, 0))
    weight = pltpu.VMEM((D_FF, D_MODEL), BF16)
    return _call(
        body, name="ffn_bwd", grid=(tokens // tm,),
        in_specs=[rows_spec, tiles, tiles, per, pl.BlockSpec((1, D_MODEL), lambda i: (0, 0)),
                  rows_spec, rows_spec, rows_spec, ANY_SPEC, ANY_SPEC, ANY_SPEC],
        out_specs=[tiles, tiles, tiles, rows_spec, rows_spec, pl.BlockSpec((8, D_MODEL), lambda i: (0, 0)), per],
        out_shape=[_sds((FF_TILES, tokens, FF_TILE), BF16)] * 3
        + [_sds((tokens, D_MODEL), F32), _sds((tokens, D_MODEL), BF16),
           _sds((8, D_MODEL), F32), _sds((n_seq, 8, D_MODEL), F32)],
        scratch_shapes=[weight, weight, weight, pltpu.SemaphoreType.DMA((3,))],
        compiler_params=_params(("arbitrary",)),
    )(df, gate, up, mod8, g_ffn, x1, dy, mix, w_gate_t, w_up_t, w_down)


def _mix_in_bwd(d_a, d_g, d_q, d_k, d_v, w_in, x2, dx1, mod8, g_mix, seq, ride, tm=512):
    tokens = x2.shape[0]
    per_seq = seq // tm
    n_seq = tokens // seq
    parts = (d_a, d_g, d_q, d_k, d_v)
    width = D_CONV
    n_ride = len(ride)
    ride_scatter = [s for _, s in ride]

    def body(*refs):
        da_ref, dg_ref, dq_ref, dk_ref, dv_ref, w_ref, x_ref, dx1_ref, m_ref, g_ref = refs[:10]
        ride_in = refs[10:10 + n_ride]
        gx_ref, dgm_ref, dm_ref = refs[10 + n_ride:13 + n_ride]
        ride_args = (ride_scatter, ride_in, refs[13 + n_ride:13 + 2 * n_ride]) + tuple(refs[13 + 2 * n_ride:])
        i = pl.program_id(0)

        @pl.when(i == 0)
        def _():
            _exchange_start(*ride_args)
            dgm_ref[...] = jnp.zeros_like(dgm_ref)

        @pl.when(i % per_seq == 0)
        def _():
            dm_ref[...] = jnp.zeros_like(dm_ref)

        dh = jnp.zeros((tm, D_MODEL), F32)
        for n, ref in enumerate((da_ref, dg_ref, dq_ref, dk_ref, dv_ref)):
            dh = dh + jnp.dot(ref[...], w_ref[pl.ds(n * width, width), :], preferred_element_type=F32)
        xv = x_ref[...]
        r = lax.rsqrt(jnp.mean(xv * xv, axis=-1, keepdims=True) + EPS)
        xhat = xv * r
        g = g_ref[...]
        dm_ref[0:1, :] += jnp.sum(dh, axis=0, keepdims=True)
        dm_ref[1:2, :] += jnp.sum(dh * (xhat * g), axis=0, keepdims=True)
        dn = dh * (1.0 + m_ref[1:2, :])
        dgm_ref[0:1, :] += jnp.sum(dn * xhat, axis=0, keepdims=True)
        dxh = dn * g
        gx_ref[...] = dx1_ref[...] + r * (dxh - xhat * jnp.mean(dxh * xhat, axis=-1, keepdims=True))

        @pl.when(i == tokens // tm - 1)
        def _():
            _exchange_wait(*ride_args)

    rows = pl.BlockSpec((tm, D_MODEL), lambda i: (i, 0))
    half = pl.BlockSpec((tm, width), lambda i: (i, 0))
    per = pl.BlockSpec((None, 8, D_MODEL), lambda i: (i // per_seq, 0, 0))
    return _call(
        body, name="mix_in_bwd", grid=(tokens // tm,),
        in_specs=[half] * 5 + [pl.BlockSpec((D_IN, D_MODEL), lambda i: (0, 0)), rows, rows, per,
                               pl.BlockSpec((1, D_MODEL), lambda i: (0, 0))] + [ANY_SPEC] * n_ride,
        out_specs=[rows, pl.BlockSpec((8, D_MODEL), lambda i: (0, 0)), per] + [ANY_SPEC] * n_ride,
        out_shape=[_sds((tokens, D_MODEL), F32), _sds((8, D_MODEL), F32), _sds((n_seq, 8, D_MODEL), F32)]
        + _exchange_shapes(ride),
        scratch_shapes=_exchange_sems(n_ride),
        compiler_params=_params(("arbitrary",)),
    )(*parts, w_in, x2, dx1, mod8, g_mix, *[a for a, _ in ride])


def _grad_matmul_parts(a_parts, b_parts, name, tk=1024):
    tokens = a_parts[0].shape[0]
    na, nb = len(a_parts), len(b_parts)
    ma, nbw = a_parts[0].shape[1], b_parts[0].shape[1]

    n_k = tokens // tk

    def body(*refs):
        a_refs, b_refs, o_ref, acc = refs[:na], refs[na:na + nb], refs[na + nb], refs[na + nb + 1]

        @pl.when(pl.program_id(0) == 0)
        def _():
            acc[...] = jnp.zeros_like(acc)

        for i in range(na):
            for j in range(nb):
                acc[pl.ds(i * ma, ma), pl.ds(j * nbw, nbw)] += lax.dot_general(
                    a_refs[i][...], b_refs[j][...], TN, preferred_element_type=F32)

        @pl.when(pl.program_id(0) == n_k - 1)
        def _():
            o_ref[...] = acc[...].astype(o_ref.dtype)

    return _call(
        body, name=name, grid=(n_k,),
        in_specs=[pl.BlockSpec((tk, ma), lambda k: (k, 0))] * na + [pl.BlockSpec((tk, nbw), lambda k: (k, 0))] * nb,
        out_specs=pl.BlockSpec((na * ma, nb * nbw), lambda k: (0, 0)),
        out_shape=_sds((na * ma, nb * nbw), BF16),
        scratch_shapes=[pltpu.VMEM((na * ma, nb * nbw), F32)],
        compiler_params=_params(("arbitrary",)),
    )(*a_parts, *b_parts)


def _grad_matmul_tiles(a, b, name, tk=1024):
    tiled_b = b.ndim == 3
    tiles, tokens, width = b.shape if tiled_b else a.shape
    other = a.shape[1] if tiled_b else b.shape[1]
    out_tile = (other, width) if tiled_b else (width, other)
    n_k = tokens // tk

    def body(a_ref, b_ref, o_ref, acc):
        @pl.when(pl.program_id(0) == 0)
        def _():
            acc[...] = jnp.zeros_like(acc)

        for t in range(tiles):
            lhs = a_ref[...] if tiled_b else a_ref[t]
            rhs = b_ref[t] if tiled_b else b_ref[...]
            acc[t] += lax.dot_general(lhs, rhs, TN, preferred_element_type=F32)

        @pl.when(pl.program_id(0) == n_k - 1)
        def _():
            o_ref[...] = acc[...].astype(o_ref.dtype)

    flat = pl.BlockSpec((tk, other), lambda k: (k, 0))
    tiled = pl.BlockSpec((tiles, tk, width), lambda k: (0, k, 0))
    return _call(
        body, name=name, grid=(n_k,),
        in_specs=[flat, tiled] if tiled_b else [tiled, flat],
        out_specs=pl.BlockSpec((tiles,) + out_tile, lambda k: (0, 0, 0)),
        out_shape=_sds((tiles,) + out_tile, BF16),
        scratch_shapes=[pltpu.VMEM((tiles,) + out_tile, F32)],
        compiler_params=_params(("arbitrary",)),
    )(a, b)


def _adamw(w, m, v, g, name, n_parts=0, tr=256):
    rows, cols = w.shape
    tr = min(tr, rows)
    c1 = 1.0 - ADAM_B1 ** ADAM_STEP
    c2 = 1.0 - ADAM_B2 ** ADAM_STEP

    def body(w_ref, m_ref, v_ref, g_ref, go_ref, d_ref, mo_ref, vo_ref):
        if n_parts:
            gv = g_ref[0].astype(F32)
            for p in range(1, n_parts):
                gv = gv + g_ref[p].astype(F32)
        else:
            gv = g_ref[...]
        go_ref[...] = gv
        mn = ADAM_B1 * m_ref[...] + (1.0 - ADAM_B1) * gv
        vn = ADAM_B2 * v_ref[...] + (1.0 - ADAM_B2) * (gv * gv)
        mo_ref[...] = mn
        vo_ref[...] = vn
        d_ref[...] = -ADAM_LR * ((mn / c1) / (jnp.sqrt(vn / c2) + ADAM_EPS) + ADAM_WD * w_ref[...])

    blk = pl.BlockSpec((tr, cols), lambda i: (i, 0))
    g_spec = pl.BlockSpec((n_parts, tr, cols), lambda i: (0, i, 0)) if n_parts else blk
    return _call(
        body, name=name, grid=(rows // tr,),
        in_specs=[blk, blk, blk, g_spec], out_specs=[blk] * 4,
        out_shape=[_sds((rows, cols), F32)] * 4,
        compiler_params=_params(("parallel",)),
    )(w, m, v, g)


def _cols_to_full(blocks):
    n, r, c = blocks.shape
    return jnp.transpose(blocks, (1, 0, 2)).reshape(r, n * c)


def _pad_lanes(v, width):
    return jnp.pad(v, ((0, 0), (0, width - v.shape[1])))


def kernel(x, c, w_ada, b_ada, g_mix, w_in, w_dw, b_dw, g_conv_ln, b_conv_ln, g_q, g_k, w_out, g_ffn, w_gate, w_up, w_down, loss_target, m_w_ada, m_b_ada, m_g_mix, m_w_in, m_w_dw, m_b_dw, m_g_conv_ln, m_b_conv_ln, m_g_q, m_g_k, m_w_out, m_g_ffn, m_w_gate, m_w_up, m_w_down, v_w_ada, v_b_ada, v_g_mix, v_w_in, v_w_dw, v_b_dw, v_g_conv_ln, v_b_conv_ln, v_g_q, v_g_k, v_w_out, v_g_ffn, v_w_gate, v_w_up, v_w_down):
    n_seq, seq, _ = x.shape
    tokens = n_seq * seq
    me = 4 * lax.axis_index("x") + 2 * lax.axis_index("y") + lax.axis_index("c")
    ada_cols = w_ada.shape[2]
    dw_cols = w_dw.shape[2]

    def transposed(w):
        return jnp.transpose(w[0])

    (c_g, w_in_g, w_dw_g) = _gather_by_chip([c, transposed(w_in).astype(BF16), w_dw[0]], "gather_weights")
    c_all = c_g.reshape(N_DEV * n_seq, D_MODEL)
    w_in_t = w_in_g.reshape(D_IN, D_MODEL)
    w_dw_f = _cols_to_full(w_dw_g)

    b_cols = lax.dynamic_slice(b_ada, (0, me * ada_cols), (1, ada_cols))
    mod_cols = _ada_fwd(c_all, w_ada[0], b_cols)
    (mod_g,) = _exchange([(mod_cols, False)], "gather_mod")
    mod_mine = lax.dynamic_slice(mod_g, (0, me * n_seq, 0), (N_DEV, n_seq, ada_cols))
    mod = jnp.transpose(mod_mine, (1, 0, 2)).reshape(n_seq, N_MOD, D_MODEL)
    mod8 = jnp.pad(mod, ((0, 0), (0, 8 - N_MOD), (0, 0)))

    x2 = x.reshape(tokens, D_MODEL)
    h1, proj = _mix_in(x2, mod8, g_mix, w_in_t, seq)
    proj3 = proj.reshape(D_IN // LANES, n_seq, seq, LANES)
    uc3 = _conv_fwd(proj3, w_dw_f, b_dw)
    g_q2, g_k2 = jnp.tile(g_q, (1, 2)), jnp.tile(g_k, (1, 2))
    y_att3, lse3, w_out_g, w_gate_g, w_up_g, w_down_g = _attn_fwd(
        proj3, g_q2, g_k2,
        [(w_out[0].astype(BF16), False), (transposed(w_gate).astype(BF16), False),
         (transposed(w_up).astype(BF16), False), (w_down[0].astype(BF16), False)])
    w_out_f = w_out_g.reshape(D_MODEL, D_MODEL)
    w_gate_f = w_gate_g.reshape(D_FF, D_MODEL)
    w_up_f = w_up_g.reshape(D_FF, D_MODEL)
    w_down_f = w_down_g.reshape(D_FF, D_MODEL)
    uc2 = uc3.reshape(tokens, D_CONV)
    y_att2 = y_att3.reshape(tokens, D_ATT)
    y_conv, mix, x1, h2 = _mix_out(uc2, y_att2, x2, mod8, g_conv_ln, b_conv_ln, g_ffn, w_out_f, seq)
    gate, up, dy, df, sq, dgate_f = _ffn_fwd(
        h2, w_gate_f, w_up_f, w_down_f, x1, loss_target.reshape(tokens, D_MODEL), mod8, seq)

    dgate, dup, act, dx1, dmix, dg_ffn, dmod_f = _ffn_bwd(
        df, gate, up, w_gate_f, w_up_f, w_down_f, x1, dy, mix, mod8, g_ffn, seq)
    duc2, do2, dgb_ln = _mix_out_bwd(dmix, uc2, g_conv_ln, b_conv_ln, w_out_f)
    d_a3, d_g3, dw_dw_p, db_dw_p = _conv_bwd(duc2.reshape(n_seq, seq, D_CONV), proj3, w_dw_f)
    gw_gate = _grad_matmul_tiles(dgate, h2, "grad_w_gate")
    gw_up = _grad_matmul_tiles(dup, h2, "grad_w_up")
    gw_down = _grad_matmul_tiles(act, df, "grad_w_down")
    gw_out = _grad_matmul_parts([y_conv, y_att2], [dmix], "grad_w_out")
    d_q3, d_k3, d_v3, dg_qk, p_gate, p_up, p_down, p_out = _attn_bwd(
        proj3, do2.reshape(n_seq, seq, D_ATT), y_att3, lse3, g_q2, g_k2,
        [(gw_gate.reshape(N_DEV, D_FF // N_DEV, D_MODEL), True), (gw_up.reshape(N_DEV, D_FF // N_DEV, D_MODEL), True),
         (gw_down.reshape(N_DEV, D_FF // N_DEV, D_MODEL), True),
         (gw_out.reshape(N_DEV, D_MODEL // N_DEV, D_MODEL), True)])
    flat = lambda t: t.reshape(tokens, t.shape[-1])
    d_a, d_g, d_q, d_k, d_v = flat(d_a3), flat(d_g3), flat(d_q3), flat(d_k3), flat(d_v3)
    gw_in = _grad_matmul_parts([d_a, d_g, d_q, d_k, d_v], [h1], "grad_w_in")
    grad_x2, dg_mix, dmod_m, p_in = _mix_in_bwd(
        d_a, d_g, d_q, d_k, d_v, w_in_t, x2, dx1, mod8, g_mix, seq,
        [(gw_in.reshape(N_DEV, D_IN // N_DEV, D_MODEL), True)])

    dmod = jnp.concatenate([dmod_m[:, 0], dmod_m[:, 1], dmod_f[:, 2], dmod_f[:, 0], dmod_f[:, 1], dgate_f[:, 0]], axis=1)
    dg_q = dg_qk[0:1, 0:HEAD_DIM] + dg_qk[0:1, HEAD_DIM:]
    dg_k = dg_qk[1:2, 0:HEAD_DIM] + dg_qk[1:2, HEAD_DIM:]
    loss_part = (0.5 / D_MODEL) * jnp.sum(sq[0:1, :], axis=1, keepdims=True)
    small = jnp.concatenate(
        [dg_mix[0:1], dg_ffn[0:1], db_dw_p[0:1], dgb_ln[0:1], dgb_ln[1:2],
         _pad_lanes(dg_q, LANES), _pad_lanes(dg_k, LANES), _pad_lanes(loss_part, LANES)], axis=1)
    n_small = small.shape[1] - LANES

    (dmod_g, small_g, dw_g) = _exchange([(dmod, False), (small, False), (dw_dw_p, False)], "gather_small_grads")

    dmod_all = dmod_g.reshape(N_DEV * n_seq, N_MOD * D_MODEL)
    dmod_cols = lax.dynamic_slice(dmod_all, (0, me * ada_cols), (N_DEV * n_seq, ada_cols))
    gw_ada, gb_ada = _ada_bwd(c_all, dmod_cols, dmod_all)

    res = {}
    res["w_ada"] = _adamw(w_ada[0], m_w_ada[0], v_w_ada[0], gw_ada, "adamw_w_ada")
    res["b_ada"] = _adamw(b_ada, m_b_ada, v_b_ada, gb_ada, "adamw_b_ada")
    def adamw_transposed(w, m, v, parts, name, tr):
        outs = _adamw(transposed(w), transposed(m), transposed(v), parts, name, N_DEV, tr=tr)
        return tuple(jnp.transpose(o) for o in outs)

    res["w_in"] = adamw_transposed(w_in, m_w_in, v_w_in, p_in, "adamw_w_in", 160)
    res["w_out"] = _adamw(w_out[0], m_w_out[0], v_w_out[0], p_out, "adamw_w_out", N_DEV)
    res["w_gate"] = adamw_transposed(w_gate, m_w_gate, v_w_gate, p_gate, "adamw_w_gate", 176)
    res["w_up"] = adamw_transposed(w_up, m_w_up, v_w_up, p_up, "adamw_w_up", 176)
    res["w_down"] = _adamw(w_down[0], m_w_down[0], v_w_down[0], p_down, "adamw_w_down", N_DEV, tr=176)
    dw_mine = lax.dynamic_slice(dw_g, (0, 0, me * dw_cols), (N_DEV, CONV_WIDTH, dw_cols))
    res["w_dw"] = _adamw(w_dw[0], m_w_dw[0], v_w_dw[0], dw_mine, "adamw_w_dw", N_DEV)

    small_names = ["g_mix", "g_ffn", "b_dw", "g_conv_ln", "b_conv_ln", "g_q", "g_k"]
    small_w = {"g_mix": (g_mix, m_g_mix, v_g_mix), "g_ffn": (g_ffn, m_g_ffn, v_g_ffn), "b_dw": (b_dw, m_b_dw, v_b_dw),
               "g_conv_ln": (g_conv_ln, m_g_conv_ln, v_g_conv_ln), "b_conv_ln": (b_conv_ln, m_b_conv_ln, v_b_conv_ln),
               "g_q": (g_q, m_g_q, v_g_q), "g_k": (g_k, m_g_k, v_g_k)}
    widths = [max(small_w[n][0].shape[1], LANES) for n in small_names]
    packed = [jnp.concatenate([_pad_lanes(small_w[n][i], wd) for n, wd in zip(small_names, widths)], axis=1) for i in range(3)]
    outs = _adamw(packed[0], packed[1], packed[2], small_g[:, :, :n_small], "adamw_small", N_DEV)
    off = 0
    for n, wd in zip(small_names, widths):
        real = small_w[n][0].shape[1]
        res[n] = tuple(o[:, off:off + real] for o in outs)
        off += wd
    loss = jnp.sum(small_g[:, 0, n_small])

    order = ["w_ada", "b_ada", "g_mix", "w_in", "w_dw", "b_dw", "g_conv_ln", "b_conv_ln", "g_q", "g_k",
             "w_out", "g_ffn", "w_gate", "w_up", "w_down"]
    lead = {"w_ada", "w_in", "w_dw", "w_out", "w_gate", "w_up", "w_down"}
    grads, deltas, new_m, new_v = [], [], [], []
    for n in order:
        g, d, mn, vn = res[n]
        g, d, mn, vn = (t[None] if n in lead else t for t in (g, d, mn, vn))
        grads.append(g)
        deltas.append(d)
        new_m.append(mn)
        new_v.append(vn)
    return (loss, grad_x2.reshape(n_seq, seq, D_MODEL), *grads, *deltas, *new_m, *new_v)
```

```python
import numpy as np
import jax
import jax.numpy as jnp
from jax import lax
from jax.experimental import pallas as pl
from jax.experimental.pallas import tpu as pltpu

F32 = jnp.float32
BF16 = jnp.bfloat16

N_DEV = 8
D_MODEL = 1024
D_CONV = 512
D_ATT = 512
HEAD_DIM = 64
CONV_WIDTH = 31
D_IN = 2 * D_CONV + 3 * D_ATT
D_FF = 2816
N_MOD = 6
EPS = 1e-6
RADIUS = 64
DILATIONS = (1, 4, 16)
Q_BLOCK = 128
LANES = 128
VMEM_LIMIT = 56 * 1024 * 1024

ADAM_LR = 0.001
ADAM_B1 = 0.9
ADAM_B2 = 0.999
ADAM_EPS = 1e-08
ADAM_WD = 0.01
ADAM_STEP = 10

NT = (((1,), (1,)), ((), ()))
TN = (((0,), (0,)), ((), ()))


def _call(body, **kw):
    return pl.pallas_call(body, **kw)


def _params(sem=None, vmem=VMEM_LIMIT):
    return pltpu.CompilerParams(dimension_semantics=sem, vmem_limit_bytes=vmem)


def _sig(x):
    return 1.0 / (1.0 + jnp.exp(-x))


def _sds(shape, dtype):
    return jax.ShapeDtypeStruct(shape, dtype)


N_PEER = N_DEV - 1
ANY_SPEC = pl.BlockSpec(memory_space=pl.ANY)


def _exchange_copies(scatter, ins, outs, *sems):
    n = len(ins)
    if n == 0:
        return [], []
    send_sems, recv_sems, local_sems = sems
    x, y, c = lax.axis_index("x"), lax.axis_index("y"), lax.axis_index("c")
    me = 4 * x + 2 * y + c

    def src(a, slot):
        return ins[a].at[slot] if scatter[a] else ins[a]

    local = [pltpu.make_async_copy(src(a, me), outs[a].at[me], local_sems.at[a]) for a in range(n)]
    flights = []
    for k in range(1, N_DEV):
        px = 1 - x if k & 4 else x
        py = 1 - y if k & 2 else y
        pc = 1 - c if k & 1 else c
        pid = 4 * px + 2 * py + pc
        for a in range(n):
            i = a * N_PEER + k - 1
            send, recv = (pltpu.make_async_remote_copy(
                src_ref=src(a, pid), dst_ref=outs[a].at[slot],
                send_sem=send_sems.at[i], recv_sem=recv_sems.at[i],
                device_id=(px, py, pc), device_id_type=pl.DeviceIdType.MESH) for slot in (me, pid))
            flights.append((send, recv))
    return local, flights


def _exchange_start(*args):
    local, flights = _exchange_copies(*args)
    for cp in local:
        cp.start()
    for send, _ in flights:
        send.start()


def _exchange_wait(*args):
    local, flights = _exchange_copies(*args)
    for send, recv in flights:
        send.wait_send()
        recv.wait_recv()
    for cp in local:
        cp.wait()


def _exchange_shapes(items):
    return [_sds((N_DEV,) + tuple(arr.shape[1:] if scatter else arr.shape), arr.dtype) for arr, scatter in items]


def _exchange_sems(n):
    if n == 0:
        return []
    return [pltpu.SemaphoreType.DMA((n * N_PEER,)), pltpu.SemaphoreType.DMA((n * N_PEER,)),
            pltpu.SemaphoreType.DMA((n,))]


def _gather_by_chip_phase(phase, ins, outs, send_sems, recv_sems, local_sems):
    n = len(ins)
    per = N_PEER
    x, y, c = lax.axis_index("x"), lax.axis_index("y"), lax.axis_index("c")
    me, sibling = (x, y, c), (x, y, 1 - c)
    chips = [(1 - x, y), (x, 1 - y), (1 - x, 1 - y)]

    def slot(px, py, pc):
        return 4 * px + 2 * py + pc

    def copy(a, k, block, to, src=None):
        dst = outs[a].at[slot(*block)]
        return pltpu.make_async_remote_copy(
            src_ref=dst if src is None else src, dst_ref=dst,
            send_sem=send_sems.at[a * per + k], recv_sem=recv_sems.at[a * per + k],
            device_id=to, device_id_type=pl.DeviceIdType.MESH)

    local = [pltpu.make_async_copy(ins[a], outs[a].at[slot(*me)], local_sems.at[a]) for a in range(n)]
    first = []
    for a in range(n):
        first.append(copy(a, 0, me, sibling, src=ins[a]))
        first += [copy(a, 1 + j, me, (*chip, c), src=ins[a]) for j, chip in enumerate(chips)]
    passed = [copy(a, 4 + j, (*chip, c), sibling) for j, chip in enumerate(chips) for a in range(n)]
    if phase == 0:
        for cp in local + first:
            cp.start()
    elif phase == 1:
        for j, chip in enumerate(chips):
            for a in range(n):
                copy(a, 1 + j, (*chip, c), me).wait_recv()
        for cp in passed:
            cp.start()
    else:
        for a in range(n):
            copy(a, 0, sibling, me).wait_recv()
            for j, chip in enumerate(chips):
                copy(a, 4 + j, (*chip, 1 - c), me).wait_recv()
        for cp in first + passed:
            cp.wait_send()
        for cp in local:
            cp.wait()


def _gather_by_chip(arrays, name):
    n = len(arrays)

    def body(*refs):
        for phase in range(3):
            _gather_by_chip_phase(phase, refs[:n], refs[n:2 * n], *refs[2 * n:])

    return _call(
        body, name=name, out_shape=_exchange_shapes([(arr, False) for arr in arrays]),
        in_specs=[ANY_SPEC] * n, out_specs=[ANY_SPEC] * n, scratch_shapes=_exchange_sems(n),
    )(*arrays)


def _exchange(items, name):
    n = len(items)
    scatter = [s for _, s in items]

    def body(*refs):
        args = (scatter, refs[:n], refs[n:2 * n]) + tuple(refs[2 * n:])
        _exchange_start(*args)
        _exchange_wait(*args)

    return _call(
        body, name=name, out_shape=_exchange_shapes(items),
        in_specs=[ANY_SPEC] * n, out_specs=[ANY_SPEC] * n, scratch_shapes=_exchange_sems(n),
    )(*[a for a, _ in items])


def _ada_fwd(c_all, w_ada, b_cols):
    def body(c_ref, w_ref, b_ref, o_ref):
        cv = c_ref[...]
        sc = (cv * _sig(cv)).astype(BF16)
        o_ref[...] = jnp.dot(sc, w_ref[...].astype(BF16), preferred_element_type=F32) + b_ref[...]

    return _call(body, name="ada_fwd", out_shape=_sds((c_all.shape[0], w_ada.shape[1]), F32),
                 compiler_params=_params())(c_all, w_ada, b_cols)


def _ada_bwd(c_all, dmod_cols, dmod_all):
    def body(c_ref, dc_ref, da_ref, gw_ref, gb_ref):
        cv = c_ref[...]
        sc = (cv * _sig(cv)).astype(BF16)
        gw_ref[...] = lax.dot_general(sc, dc_ref[...].astype(BF16), TN, preferred_element_type=F32)
        gb_ref[...] = jnp.sum(da_ref[...], axis=0, keepdims=True)

    return _call(body, name="ada_bwd",
                 out_shape=[_sds((c_all.shape[1], dmod_cols.shape[1]), F32), _sds((1, dmod_all.shape[1]), F32)],
                 compiler_params=_params())(c_all, dmod_cols, dmod_all)


MIX_ROWS = 128


def _mix_in(x2, mod8, g_mix, w_in, seq, tm=512):
    tokens = x2.shape[0]
    per_seq = seq // tm

    def body(x_ref, m_ref, g_ref, wt_ref, h_ref, p_ref, w_ref):
        @pl.when(pl.program_id(0) == 0)
        def _():
            w_ref[...] = wt_ref[...].T

        def normed(c):
            rows = pl.ds(c * MIX_ROWS, MIX_ROWS)
            xv = x_ref[rows, :]
            r = lax.rsqrt(jnp.mean(xv * xv, axis=-1, keepdims=True) + EPS)
            hb = ((xv * r * g_ref[...]) * (1.0 + m_ref[1:2, :]) + m_ref[0:1, :]).astype(BF16)
            h_ref[rows, :] = hb
            return hb

        ahead = normed(0)
        for c in range(tm // MIX_ROWS):
            hb = ahead
            if c + 1 < tm // MIX_ROWS:
                ahead = normed(c + 1)
            p = jnp.dot(hb, w_ref[...], preferred_element_type=F32)
            for cb in range(D_IN // LANES):
                p_ref[cb, pl.ds(c * MIX_ROWS, MIX_ROWS), :] = p[:, cb * LANES:(cb + 1) * LANES]

    return _call(
        body, name="mix_in", grid=(tokens // tm,),
        in_specs=[pl.BlockSpec((tm, D_MODEL), lambda i: (i, 0)),
                  pl.BlockSpec((None, 8, D_MODEL), lambda i: (i // per_seq, 0, 0)),
                  pl.BlockSpec((1, D_MODEL), lambda i: (0, 0)),
                  pl.BlockSpec((D_IN, D_MODEL), lambda i: (0, 0))],
        out_specs=[pl.BlockSpec((tm, D_MODEL), lambda i: (i, 0)),
                   pl.BlockSpec((D_IN // LANES, tm, LANES), lambda i: (0, i, 0))],
        out_shape=[_sds((tokens, D_MODEL), BF16), _sds((D_IN // LANES, tokens, LANES), F32)],
        scratch_shapes=[pltpu.VMEM((D_MODEL, D_IN), BF16)],
        compiler_params=_params(("arbitrary",)),
    )(x2, mod8, g_mix, w_in)


CONV_ROWS = 128
CONV_DW_ROWS = 32
CONV_DW_UNROLL = 4
CONV_HALO = 16


def _fill_shifted(xp, sh, seq):
    for b in range(8):
        sh[b, pl.ds(0, seq + 24), :] = xp[pl.ds(b, seq + 24), :]


def _conv_fwd(proj3, w_dw, b_dw):
    _, n_seq, seq, _ = proj3.shape
    n_cb = D_CONV // LANES

    def body(a_ref, g_ref, w_ref, b_ref, uc_ref, xp, sh):
        zeros = jnp.zeros((CONV_HALO, LANES), F32)
        xp[pl.ds(0, CONV_HALO), :] = zeros
        xp[pl.ds(CONV_HALO + seq, CONV_HALO), :] = zeros
        xp[pl.ds(CONV_HALO, seq), :] = a_ref[...] * _sig(g_ref[...])
        _fill_shifted(xp, sh, seq)

        def blk(i, carry):
            t0 = pl.multiple_of(i * CONV_ROWS, CONV_ROWS)
            acc = jnp.zeros((CONV_ROWS, LANES), F32)
            for j in range(CONV_WIDTH):
                jj = j + 1
                acc = acc + sh[jj % 8, pl.ds(t0 + 8 * (jj // 8), CONV_ROWS), :] * w_ref[j:j + 1, :]
            uc_ref[pl.ds(t0, CONV_ROWS), :] = acc + b_ref[...]
            return carry

        lax.fori_loop(0, seq // CONV_ROWS, blk, 0)

    return _call(
        body, name="conv_fwd", grid=(n_seq, n_cb),
        in_specs=[pl.BlockSpec((None, None, seq, LANES), lambda b, cb: (cb, b, 0, 0)),
                  pl.BlockSpec((None, None, seq, LANES), lambda b, cb: (n_cb + cb, b, 0, 0)),
                  pl.BlockSpec((CONV_WIDTH, LANES), lambda b, cb: (0, cb)),
                  pl.BlockSpec((1, LANES), lambda b, cb: (0, cb))],
        out_specs=pl.BlockSpec((None, seq, LANES), lambda b, cb: (b, 0, cb)),
        out_shape=_sds((n_seq, seq, D_CONV), F32),
        scratch_shapes=[pltpu.VMEM((seq + 2 * CONV_HALO, LANES), F32),
                        pltpu.VMEM((8, seq + 2 * CONV_HALO, LANES), F32)],
        compiler_params=_params(("parallel", "parallel")),
    )(proj3, proj3, w_dw, b_dw)


def _conv_bwd(duc3, proj3, w_dw):
    _, n_seq, seq, _ = proj3.shape
    n_cb = D_CONV // LANES

    def body(duc_ref, a_ref, g_ref, w_ref, da_ref, dg_ref, dw_ref, db_ref, xp, sh):
        @pl.when(pl.program_id(1) == 0)
        def _():
            dw_ref[...] = jnp.zeros_like(dw_ref)
            db_ref[...] = jnp.zeros_like(db_ref)

        zeros = jnp.zeros((CONV_HALO, LANES), F32)
        xp[pl.ds(0, CONV_HALO), :] = zeros
        xp[pl.ds(CONV_HALO + seq, CONV_HALO), :] = zeros
        xp[pl.ds(CONV_HALO, seq), :] = a_ref[...] * _sig(g_ref[...])
        _fill_shifted(xp, sh, seq)
        for j0 in range(0, CONV_WIDTH, 8):
            taps = range(j0, min(j0 + 8, CONV_WIDTH))

            def wblk(i, accs, taps=taps):
                for u in range(CONV_DW_UNROLL):
                    t0 = pl.multiple_of((i * CONV_DW_UNROLL + u) * CONV_DW_ROWS, CONV_DW_ROWS)
                    d = duc_ref[pl.ds(t0, CONV_DW_ROWS), :]
                    accs = tuple(acc + d * sh[(j + 1) % 8, pl.ds(t0 + 8 * ((j + 1) // 8), CONV_DW_ROWS), :]
                                 for acc, j in zip(accs, taps))
                return accs

            accs = lax.fori_loop(0, seq // (CONV_DW_ROWS * CONV_DW_UNROLL), wblk,
                                 tuple(jnp.zeros((CONV_DW_ROWS, LANES), F32) for _ in taps))
            for acc, j in zip(accs, taps):
                dw_ref[j:j + 1, :] += jnp.sum(acc, axis=0, keepdims=True)
        db_ref[0:1, :] += jnp.sum(duc_ref[...], axis=0, keepdims=True)
        xp[pl.ds(CONV_HALO, seq), :] = duc_ref[...]
        _fill_shifted(xp, sh, seq)

        def ublk(i, carry):
            t0 = pl.multiple_of(i * CONV_ROWS, CONV_ROWS)
            acc = jnp.zeros((CONV_ROWS, LANES), F32)
            for j in range(CONV_WIDTH):
                jj = CONV_WIDTH - j
                acc = acc + sh[jj % 8, pl.ds(t0 + 8 * (jj // 8), CONV_ROWS), :] * w_ref[j:j + 1, :]
            av = a_ref[pl.ds(t0, CONV_ROWS), :]
            sg = _sig(g_ref[pl.ds(t0, CONV_ROWS), :])
            da_ref[pl.ds(t0, CONV_ROWS), :] = (acc * sg).astype(BF16)
            dg_ref[pl.ds(t0, CONV_ROWS), :] = (acc * av * sg * (1.0 - sg)).astype(BF16)
            return carry

        lax.fori_loop(0, seq // CONV_ROWS, ublk, 0)

    return _call(
        body, name="conv_bwd", grid=(n_cb, n_seq),
        in_specs=[pl.BlockSpec((None, seq, LANES), lambda cb, b: (b, 0, cb)),
                  pl.BlockSpec((None, None, seq, LANES), lambda cb, b: (cb, b, 0, 0)),
                  pl.BlockSpec((None, None, seq, LANES), lambda cb, b: (n_cb + cb, b, 0, 0)),
                  pl.BlockSpec((CONV_WIDTH, LANES), lambda cb, b: (0, cb))],
        out_specs=[pl.BlockSpec((None, seq, LANES), lambda cb, b: (b, 0, cb)),
                   pl.BlockSpec((None, seq, LANES), lambda cb, b: (b, 0, cb)),
                   pl.BlockSpec((32, LANES), lambda cb, b: (0, cb)),
                   pl.BlockSpec((8, LANES), lambda cb, b: (0, cb))],
        out_shape=[_sds((n_seq, seq, D_CONV), BF16), _sds((n_seq, seq, D_CONV), BF16),
                   _sds((32, D_CONV), F32), _sds((8, D_CONV), F32)],
        scratch_shapes=[pltpu.VMEM((seq + 2 * CONV_HALO, LANES), F32),
                        pltpu.VMEM((8, seq + 2 * CONV_HALO, LANES), F32)],
        compiler_params=_params(("parallel", "arbitrary")),
    )(duc3, proj3, proj3, w_dw)


MASKED = 1e30
ATT_ROWS = 1024
ATT_BWD_ROWS = 2048
ATT_UNROLL = 8
ATT_FWD_UNROLL = 8


def _distance_mats(dil, seg_len):
    kw = min(2 * Q_BLOCK, seg_len)
    offsets = (0, -RADIUS, -2 * RADIUS) if kw == 2 * Q_BLOCK else (0,)
    a = np.arange(Q_BLOCK)[:, None]
    b = np.arange(kw)[None, :]
    mats = []
    for off in offsets:
        rel = np.abs(b + off - a)
        mats.append(np.where(rel <= RADIUS, dil * rel, MASKED))
    return jnp.asarray(np.stack(mats).astype(np.float32))


def _alibi_rows():
    s = np.zeros((4, 8, LANES), np.float32)
    for hp in range(4):
        for hl in range(2):
            s[hp, hl, :] = 2.0 ** (-(2 * hp + hl + 1))
    return jnp.asarray(s)


def _window(n, seg_len):
    i0 = pl.multiple_of(n * Q_BLOCK, Q_BLOCK)
    if seg_len <= Q_BLOCK:
        return i0, i0, 0
    per_seg = seg_len // Q_BLOCK
    j = n % per_seg
    seg0 = (n // per_seg) * seg_len
    ks_local = jnp.clip(j * Q_BLOCK - RADIUS, 0, seg_len - 2 * Q_BLOCK)
    ks = pl.multiple_of(seg0 + ks_local, RADIUS)
    var = jnp.where(j == 0, 0, jnp.where(j == per_seg - 1, 2, 1))
    return i0, ks, var


def _first_head(rows):
    return lax.broadcasted_iota(jnp.int32, (rows, LANES), 1) < HEAD_DIM


def _same_head():
    head = np.arange(LANES) // HEAD_DIM
    return jnp.asarray((head[:, None] == head[None, :]).astype(np.float32)).astype(BF16)


def _head_sum(x, same_ref):
    hi = x.astype(BF16)
    lo = (x - hi.astype(F32)).astype(BF16)
    return (jnp.dot(hi, same_ref[...], preferred_element_type=F32)
            + jnp.dot(lo, same_ref[...], preferred_element_type=F32))


def _head_mean(x, same_ref):
    return _head_sum(x, same_ref) * (1.0 / HEAD_DIM)


def _per_head(x, first):
    swapped = pltpu.roll(x, HEAD_DIM, 1)
    return jnp.where(first, x, swapped), jnp.where(first, swapped, x)


STRIDE = 4


def _gather_segments(src, dil, seq, tmp, put):
    if dil == 1:
        put(0, seq, src[pl.ds(0, seq), :])
    elif dil == STRIDE:
        seg = seq // dil
        for r in range(dil):
            put(r * seg, seg, src[pl.ds(r, seg, stride=dil), :])
    else:
        part, seg = seq // STRIDE, seq // dil
        for b in range(STRIDE):
            tmp[pl.ds(b * part, part), :] = src[pl.ds(b, part, stride=STRIDE), :]
        for b in range(STRIDE):
            for a in range(dil // STRIDE):
                put(b * part + a * seg, seg, tmp[pl.ds(b * part + a, seg, stride=dil // STRIDE), :])


def _scatter_segments(dst, get, dil, seq, tmp, accumulate):
    def write(rows, val):
        if accumulate:
            dst[rows, :] += val
        else:
            dst[rows, :] = val

    if dil == 1:
        write(pl.ds(0, seq), get(0, seq))
    elif dil == STRIDE:
        seg = seq // dil
        for r in range(dil):
            write(pl.ds(r, seg, stride=dil), get(r * seg, seg))
    else:
        part, seg = seq // STRIDE, seq // dil
        for b in range(STRIDE):
            for a in range(dil // STRIDE):
                tmp[pl.ds(b * part + a, seg, stride=dil // STRIDE), :] = get(b * part + a * seg, seg)
        for b in range(STRIDE):
            write(pl.ds(b, part, stride=STRIDE), tmp[pl.ds(b * part, part), :])


def _permute_rows(dst, src, dil, seq, tmp):
    def put(start, size, val):
        dst[pl.ds(start, size), :] = val.astype(dst.dtype)

    _gather_segments(src, dil, seq, tmp, put)


def _permute_rows_by_head(dst, src, dil, seq, tmp):
    def put(start, size, val):
        first = _first_head(size)
        dst[0, pl.ds(start, size), :] = jnp.where(first, val, 0.0).astype(dst.dtype)
        dst[1, pl.ds(start, size), :] = jnp.where(first, 0.0, val).astype(dst.dtype)

    _gather_segments(src, dil, seq, tmp, put)


def _qk_normalise(q_ref, g2_ref, same_ref, dst, seq, scale, step):
    def chunk(ci, carry):
        rows = pl.ds(pl.multiple_of(ci * step, step), step)
        qv = q_ref[rows, :]
        r = lax.rsqrt(_head_mean(qv * qv, same_ref) + EPS)
        dst[rows, :] = qv * r * (g2_ref[...] * scale)
        return carry

    lax.fori_loop(0, seq // step, chunk, 0)


def _attn_fwd(proj3, g_q2, g_k2, ride):
    _, n_seq, seq, _ = proj3.shape
    dms = [_distance_mats(d, seq // d) for d in DILATIONS]
    same = _same_head()
    col0 = 2 * D_CONV // LANES
    n_hp = D_ATT // LANES

    n_ride = len(ride)
    assert not any(scatter for _, scatter in ride), "the forward's ride is an all-gather"

    def body(*refs):
        q_ref, k_ref, v_ref, gq_ref, gk_ref, sl_ref, dm1, dm4, dm16, same_ref = refs[:10]
        ride_in = refs[10:10 + n_ride]
        y_ref, lse_ref = refs[10 + n_ride:12 + n_ride]
        ride_out = refs[12 + n_ride:12 + 2 * n_ride]
        (qf, kf, qp, kp, vp, oml_p, o1, o4, o16, m1, m4, m16, l1, l4, l16,
         tmp) = refs[12 + 2 * n_ride:28 + 2 * n_ride]
        o_nat, m_nat, l_nat = (o1, o4, o16), (m1, m4, m16), (l1, l4, l16)
        ride_args = (ride_in, ride_out) + tuple(refs[28 + 2 * n_ride:])
        step = pl.program_id(0) * n_hp + pl.program_id(1)
        n_steps = n_seq * n_hp

        if n_ride:
            for phase, at in enumerate((0, (3 * n_steps) // 4)):
                @pl.when(step == at)
                def _(phase=phase):
                    _gather_by_chip_phase(phase, *ride_args)

        dm_refs = (dm1, dm4, dm16)
        _qk_normalise(q_ref, gq_ref, same_ref, qf, seq, HEAD_DIM ** -0.5, ATT_ROWS)
        _qk_normalise(k_ref, gk_ref, same_ref, kf, seq, 1.0, ATT_ROWS)
        slopes = (sl_ref[0:1, 0:1], sl_ref[1:2, 0:1])
        for pi, dil in enumerate(DILATIONS):
            seg = seq // dil
            kw = min(2 * Q_BLOCK, seg)
            _permute_rows_by_head(qp, qf, dil, seq, tmp)
            _permute_rows(kp, kf, dil, seq, tmp)
            _permute_rows(vp, v_ref, dil, seq, tmp)

            def blk(it, carry, seg=seg, kw=kw, pi=pi, dst=oml_p):
                first = _first_head(Q_BLOCK)
                chains = [(sub, h) for sub in range(ATT_FWD_UNROLL) for h in range(2)]
                win = [_window(it * ATT_FWD_UNROLL + sub, seg) for sub in range(ATT_FWD_UNROLL)]
                s = {}
                for sub, h in chains:
                    i0, ks, var = win[sub]
                    s[sub, h] = lax.dot_general(qp[h, pl.ds(i0, Q_BLOCK), :], kp[pl.ds(ks, kw), :], NT,
                                                preferred_element_type=F32) - slopes[h] * dm_refs[pi][var]
                m, l, p = {}, {}, {}
                for c in chains:
                    m[c] = jnp.max(s[c], axis=1, keepdims=True)
                    e = jnp.exp(s[c] - m[c])
                    l[c] = jnp.sum(e, axis=1, keepdims=True)
                    p[c] = e.astype(BF16)
                o = {}
                for sub, h in chains:
                    o[sub, h] = jnp.dot(p[sub, h], vp[pl.ds(win[sub][1], kw), :], preferred_element_type=F32)
                packed = [jnp.concatenate([jnp.where(first, t[sub, 0], t[sub, 1]) for t in (o, m, l)], axis=1)
                          for sub in range(ATT_FWD_UNROLL)]
                span = ATT_FWD_UNROLL * Q_BLOCK
                dst[pl.ds(pl.multiple_of(it * span, span), span), :] = jnp.concatenate(packed, axis=0)
                return carry

            lax.fori_loop(0, seq // (Q_BLOCK * ATT_FWD_UNROLL), blk, 0)
            for n, nat in enumerate((o_nat[pi], m_nat[pi], l_nat[pi])):
                _scatter_segments(nat, lambda start, size, n=n: oml_p[pl.ds(start, size), pl.ds(n * LANES, LANES)],
                                  dil, seq, tmp, accumulate=False)

        def merge(ci, carry):
            rows = pl.ds(pl.multiple_of(ci * ATT_ROWS, ATT_ROWS), ATT_ROWS)
            ms = [m_nat[pi][rows, :] for pi in range(3)]
            m_all = jnp.maximum(jnp.maximum(ms[0], ms[1]), ms[2])
            es = [jnp.exp(m - m_all) for m in ms]
            l_all = sum(l_nat[pi][rows, :] * es[pi] for pi in range(3))
            inv = 1.0 / l_all
            o = sum(o_nat[pi][rows, :] * (es[pi] * inv) for pi in range(3))
            y_ref[rows, :] = o.astype(BF16)
            lse_ref[rows, :] = m_all + jnp.log(l_all)
            return carry

        lax.fori_loop(0, seq // ATT_ROWS, merge, 0)

        if n_ride:
            @pl.when(step == n_steps - 1)
            def _():
                _gather_by_chip_phase(2, *ride_args)

    def col(off):
        return pl.BlockSpec((None, None, seq, LANES), lambda b, hp: (col0 + off * n_hp + hp, b, 0, 0))

    def whole(arr):
        return pl.BlockSpec(arr.shape, lambda b, hp: (0,) * arr.ndim)

    rows_f32 = pltpu.VMEM((seq, LANES), F32)
    rows_bf16 = pltpu.VMEM((seq, LANES), BF16)
    return _call(
        body, name="attn_fwd", grid=(n_seq, n_hp),
        in_specs=[col(0), col(1), col(2), whole(g_q2), whole(g_k2),
                  pl.BlockSpec((None, 8, LANES), lambda b, hp: (hp, 0, 0)),
                  whole(dms[0]), whole(dms[1]), whole(dms[2]), whole(same)] + [ANY_SPEC] * n_ride,
        out_specs=[pl.BlockSpec((None, seq, LANES), lambda b, hp: (b, 0, hp)),
                   pl.BlockSpec((None, seq, LANES), lambda b, hp: (b, 0, hp))] + [ANY_SPEC] * n_ride,
        out_shape=[_sds((n_seq, seq, D_ATT), BF16), _sds((n_seq, seq, D_ATT), F32)] + _exchange_shapes(ride),
        scratch_shapes=[rows_f32, rows_f32, pltpu.VMEM((2, seq, LANES), BF16), rows_bf16, rows_bf16]
        + [pltpu.VMEM((seq, 3 * LANES), F32)] + [rows_f32] * 10 + _exchange_sems(n_ride),
        compiler_params=_params(("arbitrary", "arbitrary")),
    )(proj3, proj3, proj3, g_q2, g_k2, _alibi_rows(), *dms, same, *[a for a, _ in ride])


def _attn_bwd(proj3, do3, y_att3, lse3, g_q2, g_k2, ride):
    _, n_seq, seq, _ = proj3.shape
    dms = [_distance_mats(d, seq // d) for d in DILATIONS]
    same = _same_head()
    col0 = 2 * D_CONV // LANES
    n_hp = D_ATT // LANES

    n_ride = len(ride)
    ride_scatter = [s for _, s in ride]

    def body(*refs):
        (q_ref, k_ref, v_ref, do_ref, o_ref, lse_ref, gq_ref, gk_ref, sl_ref, dm1, dm4, dm16,
         same_ref) = refs[:13]
        ride_in = refs[13:13 + n_ride]
        dq_ref, dk_ref, dv_ref, dg_ref = refs[13 + n_ride:17 + n_ride]
        ride_out = refs[17 + n_ride:17 + 2 * n_ride]
        (qf, kf, qp, dop, kp, vp, sn, sp, dqp, dkp, dvp, dqn, dkn, dvn,
         tmp) = refs[17 + 2 * n_ride:32 + 2 * n_ride]
        ride_args = (ride_scatter, ride_in, ride_out) + tuple(refs[32 + 2 * n_ride:])
        dm_refs = (dm1, dm4, dm16)
        step = pl.program_id(0) * n_hp + pl.program_id(1)

        @pl.when(step == 0)
        def _():
            _exchange_start(*ride_args)
            dg_ref[...] = jnp.zeros_like(dg_ref)

        _qk_normalise(q_ref, gq_ref, same_ref, qf, seq, HEAD_DIM ** -0.5, ATT_BWD_ROWS)
        _qk_normalise(k_ref, gk_ref, same_ref, kf, seq, 1.0, ATT_BWD_ROWS)

        def stats(ci, carry):
            rows = pl.ds(pl.multiple_of(ci * ATT_BWD_ROWS, ATT_BWD_ROWS), ATT_BWD_ROWS)
            first = _first_head(ATT_BWD_ROWS)
            sn[0, rows, :], sn[1, rows, :] = _per_head(lse_ref[rows, :], first)
            prod = do_ref[rows, :] * o_ref[rows, :].astype(F32)
            sn[2, rows, :], sn[3, rows, :] = _per_head(_head_sum(prod, same_ref), first)
            return carry

        lax.fori_loop(0, seq // ATT_BWD_ROWS, stats, 0)
        slopes = (sl_ref[0:1, 0:1], sl_ref[1:2, 0:1])
        half = seq // (Q_BLOCK * ATT_UNROLL)
        region = seq // ATT_UNROLL

        for pi, dil in enumerate(DILATIONS):
            seg = seq // dil
            kw = min(2 * Q_BLOCK, seg)
            _permute_rows_by_head(qp, qf, dil, seq, tmp)
            _permute_rows_by_head(dop, do_ref, dil, seq, tmp)
            _permute_rows(kp, kf, dil, seq, tmp)
            _permute_rows(vp, v_ref, dil, seq, tmp)
            if dil == 1:
                st = sn
            else:
                st = sp
                for n in range(4):
                    _permute_rows(sp.at[n], sn.at[n], dil, seq, tmp)
            def touched(sub, seg=seg):
                lo, hi = sub * region, (sub + 1) * region
                if seg < region:
                    return lo, hi
                seg0 = lo // seg * seg
                return max(lo - RADIUS, seg0), min(hi + RADIUS, seg0 + seg)

            def summed(acc, start, size, touched=touched):
                pieces = []
                for c0 in range(start, start + size, RADIUS):
                    owners = [s for s in range(ATT_UNROLL) if touched(s)[0] <= c0 and c0 + RADIUS <= touched(s)[1]]
                    if pieces and pieces[-1][2] == owners:
                        pieces[-1][1] += RADIUS
                    else:
                        pieces.append([c0, RADIUS, owners])
                vals = [sum(acc[o, pl.ds(c0, n), :] for o in owners) for c0, n, owners in pieces]
                return vals[0] if len(vals) == 1 else jnp.concatenate(vals, axis=0)

            for sub in range(ATT_UNROLL):
                lo, hi = touched(sub)
                dkp[sub, pl.ds(lo, hi - lo), :] = jnp.zeros((hi - lo, LANES), F32)
                dvp[sub, pl.ds(lo, hi - lo), :] = jnp.zeros((hi - lo, LANES), F32)

            def blk(it, carry, seg=seg, kw=kw, pi=pi, st=st):
                first = _first_head(Q_BLOCK)
                chains = [(sub, h) for sub in range(ATT_UNROLL) for h in range(2)]
                win = [_window(it + sub * half, seg) for sub in range(ATT_UNROLL)]
                qrows = [pl.ds(w[0], Q_BLOCK) for w in win]
                krows = [pl.ds(w[1], kw) for w in win]

                def over_keys(n, sub):
                    t = st[n, qrows[sub], :]
                    return t if kw == LANES else jnp.concatenate([t] * (kw // LANES), axis=1)

                s, dp = {}, {}
                for sub, h in chains:
                    s[sub, h] = lax.dot_general(qp[h, qrows[sub], :], kp[krows[sub], :], NT,
                                                preferred_element_type=F32) - slopes[h] * dm_refs[pi][win[sub][2]]
                    dp[sub, h] = lax.dot_general(dop[h, qrows[sub], :], vp[krows[sub], :], NT,
                                                 preferred_element_type=F32)
                p, ds = {}, {}
                for sub, h in chains:
                    e = jnp.exp(s[sub, h] - over_keys(h, sub))
                    ds[sub, h] = (e * (dp[sub, h] - over_keys(2 + h, sub))).astype(BF16)
                    p[sub, h] = e.astype(BF16)
                dq, dk, dv = {}, {}, {}
                for sub, h in chains:
                    dq[sub, h] = jnp.dot(ds[sub, h], kp[krows[sub], :], preferred_element_type=F32)
                    dk[sub, h] = lax.dot_general(ds[sub, h], qp[h, qrows[sub], :], TN, preferred_element_type=F32)
                    dv[sub, h] = lax.dot_general(p[sub, h], dop[h, qrows[sub], :], TN, preferred_element_type=F32)
                for sub in range(ATT_UNROLL):
                    dqp[qrows[sub], :] = jnp.where(first, dq[sub, 0], dq[sub, 1])
                    dkp[sub, krows[sub], :] += dk[sub, 0] + dk[sub, 1]
                    dvp[sub, krows[sub], :] += dv[sub, 0] + dv[sub, 1]
                return carry

            lax.fori_loop(0, half, blk, 0)
            first_pattern = pi == 0
            if first_pattern:
                for r0 in range(0, seq, region):
                    rows = pl.ds(r0, region)
                    dqn[rows, :] = dqp[rows, :]
                    dkn[rows, :] = summed(dkp, r0, region)
                    dvn[rows, :] = summed(dvp, r0, region)
            else:
                _scatter_segments(dqn, lambda start, size: dqp[pl.ds(start, size), :], dil, seq, tmp, accumulate=True)
                for nat, acc in ((dkn, dkp), (dvn, dvp)):
                    _scatter_segments(nat, lambda start, size, acc=acc: summed(acc, start, size),
                                      dil, seq, tmp, accumulate=True)

        def finish(ci, carry):
            rows = pl.ds(pl.multiple_of(ci * ATT_BWD_ROWS, ATT_BWD_ROWS), ATT_BWD_ROWS)
            for src_ref, g_ref, dn, dst_ref, scale, row in (
                    (q_ref, gq_ref, dqn, dq_ref, HEAD_DIM ** -0.5, 0), (k_ref, gk_ref, dkn, dk_ref, 1.0, 1)):
                xv = src_ref[rows, :]
                r = lax.rsqrt(_head_mean(xv * xv, same_ref) + EPS)
                xhat = xv * r
                d = dn[rows, :] * scale
                dg_ref[row:row + 1, :] += jnp.sum(d * xhat, axis=0, keepdims=True)
                dxh = d * g_ref[...]
                dst_ref[rows, :] = (r * (dxh - xhat * _head_mean(dxh * xhat, same_ref))).astype(BF16)
            dv_ref[rows, :] = dvn[rows, :].astype(BF16)
            return carry

        lax.fori_loop(0, seq // ATT_BWD_ROWS, finish, 0)

        @pl.when(step == n_seq * n_hp - 1)
        def _():
            _exchange_wait(*ride_args)

    def col(off):
        return pl.BlockSpec((None, None, seq, LANES), lambda b, hp: (col0 + off * n_hp + hp, b, 0, 0))

    def whole(arr):
        return pl.BlockSpec(arr.shape, lambda b, hp: (0,) * arr.ndim)

    att = pl.BlockSpec((None, seq, LANES), lambda b, hp: (b, 0, hp))
    rows_f32 = pltpu.VMEM((seq, LANES), F32)
    rows_bf16 = pltpu.VMEM((seq, LANES), BF16)
    by_head_bf16 = pltpu.VMEM((2, seq, LANES), BF16)
    per_sub_f32 = pltpu.VMEM((ATT_UNROLL, seq, LANES), F32)
    stats_f32 = pltpu.VMEM((4, seq, LANES), F32)
    return _call(
        body, name="attn_bwd", grid=(n_seq, n_hp),
        in_specs=[col(0), col(1), col(2), att, att, att, whole(g_q2), whole(g_k2),
                  pl.BlockSpec((None, 8, LANES), lambda b, hp: (hp, 0, 0)),
                  whole(dms[0]), whole(dms[1]), whole(dms[2]), whole(same)] + [ANY_SPEC] * n_ride,
        out_specs=[att, att, att, pl.BlockSpec((8, LANES), lambda b, hp: (0, 0))] + [ANY_SPEC] * n_ride,
        out_shape=[_sds((n_seq, seq, D_ATT), BF16)] * 3 + [_sds((8, LANES), F32)] + _exchange_shapes(ride),
        scratch_shapes=[rows_f32, rows_f32, by_head_bf16, by_head_bf16, rows_bf16, rows_bf16, stats_f32, stats_f32,
                        rows_f32, per_sub_f32, per_sub_f32, rows_f32, rows_f32, rows_f32, rows_f32]
        + _exchange_sems(n_ride),
        compiler_params=_params(("arbitrary", "arbitrary")),
    )(proj3, proj3, proj3, do3, y_att3, lse3, g_q2, g_k2, _alibi_rows(), *dms, same, *[a for a, _ in ride])


def _mix_out(uc2, y_att2, x2, mod8, g_ln, b_ln, g_ffn, w_out, seq, tm=512):
    tokens = x2.shape[0]
    per_seq = seq // tm

    def body(uc_ref, ya_ref, x_ref, m_ref, gl_ref, bl_ref, gf_ref, w_ref, yc_ref, mix_ref, x1_ref, h2_ref):
        uc = uc_ref[...]
        mu = jnp.mean(uc, axis=-1, keepdims=True)
        cen = uc - mu
        rs = lax.rsqrt(jnp.mean(cen * cen, axis=-1, keepdims=True) + EPS)
        z = cen * rs * gl_ref[...] + bl_ref[...]
        yc = (z * _sig(z)).astype(BF16)
        yc_ref[...] = yc
        mix = (jnp.dot(yc, w_ref[pl.ds(0, D_CONV), :], preferred_element_type=F32)
               + jnp.dot(ya_ref[...], w_ref[pl.ds(D_CONV, D_ATT), :], preferred_element_type=F32))
        mix_ref[...] = mix.astype(BF16)
        x1 = x_ref[...] + m_ref[2:3, :] * mix
        x1_ref[...] = x1
        r = lax.rsqrt(jnp.mean(x1 * x1, axis=-1, keepdims=True) + EPS)
        h2_ref[...] = ((x1 * r * gf_ref[...]) * (1.0 + m_ref[4:5, :]) + m_ref[3:4, :]).astype(BF16)

    def rows(width):
        return pl.BlockSpec((tm, width), lambda i: (i, 0))

    def vec(width):
        return pl.BlockSpec((1, width), lambda i: (0, 0))

    return _call(
        body, name="mix_out", grid=(tokens // tm,),
        in_specs=[rows(D_CONV), rows(D_ATT), rows(D_MODEL),
                  pl.BlockSpec((None, 8, D_MODEL), lambda i: (i // per_seq, 0, 0)),
                  vec(D_CONV), vec(D_CONV), vec(D_MODEL),
                  pl.BlockSpec((D_MODEL, D_MODEL), lambda i: (0, 0))],
        out_specs=[rows(D_CONV), rows(D_MODEL), rows(D_MODEL), rows(D_MODEL)],
        out_shape=[_sds((tokens, D_CONV), BF16), _sds((tokens, D_MODEL), BF16),
                   _sds((tokens, D_MODEL), F32), _sds((tokens, D_MODEL), BF16)],
        compiler_params=_params(("parallel",)),
    )(uc2, y_att2, x2, mod8, g_ln, b_ln, g_ffn, w_out)


def _mix_out_bwd(dmix, uc2, g_ln, b_ln, w_out, tm=512):
    tokens = dmix.shape[0]

    def body(dm_ref, uc_ref, gl_ref, bl_ref, w_ref, duc_ref, do_ref, dgb_ref):
        @pl.when(pl.program_id(0) == 0)
        def _():
            dgb_ref[...] = jnp.zeros_like(dgb_ref)

        dmv = dm_ref[...]
        dyc = lax.dot_general(dmv, w_ref[pl.ds(0, D_CONV), :], NT, preferred_element_type=F32)
        do_ref[...] = lax.dot_general(dmv, w_ref[pl.ds(D_CONV, D_ATT), :], NT, preferred_element_type=F32)
        uc = uc_ref[...]
        mu = jnp.mean(uc, axis=-1, keepdims=True)
        cen = uc - mu
        rs = lax.rsqrt(jnp.mean(cen * cen, axis=-1, keepdims=True) + EPS)
        xh = cen * rs
        z = xh * gl_ref[...] + bl_ref[...]
        sg = _sig(z)
        dz = dyc * (sg * (1.0 + z * (1.0 - sg)))
        dgb_ref[0:1, :] += jnp.sum(dz * xh, axis=0, keepdims=True)
        dgb_ref[1:2, :] += jnp.sum(dz, axis=0, keepdims=True)
        dxh = dz * gl_ref[...]
        duc_ref[...] = rs * (dxh - jnp.mean(dxh, axis=-1, keepdims=True)
                             - xh * jnp.mean(dxh * xh, axis=-1, keepdims=True))

    return _call(
        body, name="mix_out_bwd", grid=(tokens // tm,),
        in_specs=[pl.BlockSpec((tm, D_MODEL), lambda i: (i, 0)),
                  pl.BlockSpec((tm, D_CONV), lambda i: (i, 0)),
                  pl.BlockSpec((1, D_CONV), lambda i: (0, 0)),
                  pl.BlockSpec((1, D_CONV), lambda i: (0, 0)),
                  pl.BlockSpec((D_MODEL, D_MODEL), lambda i: (0, 0))],
        out_specs=[pl.BlockSpec((tm, D_CONV), lambda i: (i, 0)),
                   pl.BlockSpec((tm, D_ATT), lambda i: (i, 0)),
                   pl.BlockSpec((8, D_CONV), lambda i: (0, 0))],
        out_shape=[_sds((tokens, D_CONV), F32), _sds((tokens, D_ATT), F32), _sds((8, D_CONV), F32)],
        compiler_params=_params(("arbitrary",)),
    )(dmix, uc2, g_ln, b_ln, w_out)


FF_TILE = 256
FF_TILES = D_FF // FF_TILE


def _load_once(hbm_refs, vmem_refs, sems):
    @pl.when(pl.program_id(0) == 0)
    def _():
        copies = [pltpu.make_async_copy(src, dst, sems.at[n]) for n, (src, dst) in enumerate(zip(hbm_refs, vmem_refs))]
        for cp in copies:
            cp.start()
        for cp in copies:
            cp.wait()


def _ffn_fwd(h2, w_gate_t, w_up_t, w_down, x1, target, mod8, seq, tm=512):
    tokens = h2.shape[0]
    per_seq = seq // tm
    n_seq = tokens // seq

    def body(h_ref, m_ref, x1_ref, t_ref, wg_hbm, wu_hbm, wd_hbm, gate_ref, up_ref, dy_ref, df_ref, sq_ref, dgf_ref,
             wg, wu, wd, w_sems):
        i = pl.program_id(0)
        _load_once((wg_hbm, wu_hbm, wd_hbm), (wg, wu, wd), w_sems)

        @pl.when(i == 0)
        def _():
            sq_ref[...] = jnp.zeros_like(sq_ref)

        @pl.when(i % per_seq == 0)
        def _():
            dgf_ref[...] = jnp.zeros_like(dgf_ref)

        hv = h_ref[...]

        def gate_up(t):
            rows = pl.ds(t * FF_TILE, FF_TILE)
            return (lax.dot_general(hv, wg[rows, :], NT, preferred_element_type=F32),
                    lax.dot_general(hv, wu[rows, :], NT, preferred_element_type=F32))

        fv = jnp.zeros((tm, D_MODEL), F32)
        ahead = gate_up(0)
        for t in range(FF_TILES):
            gate, up = ahead
            if t + 1 < FF_TILES:
                ahead = gate_up(t + 1)
            gate_ref[t] = gate.astype(BF16)
            up_ref[t] = up.astype(BF16)
            act = (gate * _sig(gate) * up).astype(BF16)
            fv = fv + jnp.dot(act, wd[pl.ds(t * FF_TILE, FF_TILE), :], preferred_element_type=F32)
        gate_f = m_ref[5:6, :]
        diff = x1_ref[...] + gate_f * fv - t_ref[...]
        sq_ref[0:1, :] += jnp.sum(diff * diff, axis=0, keepdims=True)
        dy = diff * (1.0 / D_MODEL)
        dy_ref[...] = dy
        df_ref[...] = (gate_f * dy).astype(BF16)
        dgf_ref[0:1, :] += jnp.sum(dy * fv, axis=0, keepdims=True)

    rows_spec = pl.BlockSpec((tm, D_MODEL), lambda i: (i, 0))
    per = pl.BlockSpec((None, 8, D_MODEL), lambda i: (i // per_seq, 0, 0))
    tiles = pl.BlockSpec((FF_TILES, tm, FF_TILE), lambda i: (0, i, 0))
    weight = pltpu.VMEM((D_FF, D_MODEL), BF16)
    return _call(
        body, name="ffn_fwd", grid=(tokens // tm,),
        in_specs=[rows_spec, per, rows_spec, rows_spec, ANY_SPEC, ANY_SPEC, ANY_SPEC],
        out_specs=[tiles, tiles, rows_spec, rows_spec, pl.BlockSpec((8, D_MODEL), lambda i: (0, 0)), per],
        out_shape=[_sds((FF_TILES, tokens, FF_TILE), BF16), _sds((FF_TILES, tokens, FF_TILE), BF16),
                   _sds((tokens, D_MODEL), F32), _sds((tokens, D_MODEL), BF16),
                   _sds((8, D_MODEL), F32), _sds((n_seq, 8, D_MODEL), F32)],
        scratch_shapes=[weight, weight, weight, pltpu.SemaphoreType.DMA((3,))],
        compiler_params=_params(("arbitrary",)),
    )(h2, mod8, x1, target, w_gate_t, w_up_t, w_down)


def _ffn_bwd(df, gate, up, w_gate_t, w_up_t, w_down, x1, dy, mix, mod8, g_ffn, seq, tm=256):
    tokens = df.shape[0]
    per_seq = seq // tm
    n_seq = tokens // seq

    def body(df_ref, gate_ref, up_ref, m_ref, g_ref, x1_ref, dy_ref, mix_ref, wg_hbm, wu_hbm, wd_hbm,
             dgate_ref, dup_ref, act_ref, dx1_ref, dmix_ref, dg_ref, dm_ref, wg, wu, wd, w_sems):
        i = pl.program_id(0)
        _load_once((wg_hbm, wu_hbm, wd_hbm), (wg, wu, wd), w_sems)

        @pl.when(i == 0)
        def _():
            dg_ref[...] = jnp.zeros_like(dg_ref)

        @pl.when(i % per_seq == 0)
        def _():
            dm_ref[...] = jnp.zeros_like(dm_ref)

        dfv = df_ref[...]

        def d_act(t):
            return lax.dot_general(dfv, wd[pl.ds(t * FF_TILE, FF_TILE), :], NT, preferred_element_type=F32)

        dh = jnp.zeros((tm, D_MODEL), F32)
        ahead = d_act(0)
        for t in range(FF_TILES):
            dact = ahead
            if t + 1 < FF_TILES:
                ahead = d_act(t + 1)
            rows = pl.ds(t * FF_TILE, FF_TILE)
            gv = gate_ref[t].astype(F32)
            uv = up_ref[t].astype(F32)
            sg = _sig(gv)
            silu = gv * sg
            act_ref[t] = (silu * uv).astype(BF16)
            dup = (dact * silu).astype(BF16)
            dgate = (dact * uv * (sg * (1.0 + gv * (1.0 - sg)))).astype(BF16)
            dup_ref[t] = dup
            dgate_ref[t] = dgate
            dh = dh + (jnp.dot(dgate, wg[rows, :], preferred_element_type=F32)
                       + jnp.dot(dup, wu[rows, :], preferred_element_type=F32))
        g = g_ref[...]
        x1v = x1_ref[...]
        rs = lax.rsqrt(jnp.mean(x1v * x1v, axis=-1, keepdims=True) + EPS)
        xhat = x1v * rs
        dm_ref[0:1, :] += jnp.sum(dh, axis=0, keepdims=True)
        dm_ref[1:2, :] += jnp.sum(dh * (xhat * g), axis=0, keepdims=True)
        dn = dh * (1.0 + m_ref[4:5, :])
        dg_ref[0:1, :] += jnp.sum(dn * xhat, axis=0, keepdims=True)
        dxh = dn * g
        dx1 = dy_ref[...] + rs * (dxh - xhat * jnp.mean(dxh * xhat, axis=-1, keepdims=True))
        dx1_ref[...] = dx1
        dm_ref[2:3, :] += jnp.sum(dx1 * mix_ref[...].astype(F32), axis=0, keepdims=True)
        dmix_ref[...] = (m_ref[2:3, :] * dx1).astype(BF16)

    rows_spec = pl.BlockSpec((tm, D_MODEL), lambda i: (i, 0))
    per = pl.BlockSpec((None, 8, D_MODEL), lambda i: (i // per_seq, 0, 0))
    tiles = pl.BlockSpec((FF_TILES, tm, FF_TILE), lambda i: (0, i, 0))
    weight = pltpu.VMEM((D_FF, D_MODEL), BF16)
    return _call(
        body, name="ffn_bwd", grid=(tokens // tm,),
        in_specs=[rows_spec, tiles, tiles, per, pl.BlockSpec((1, D_MODEL), lambda i: (0, 0)),
                  rows_spec, rows_spec, rows_spec, ANY_SPEC, ANY_SPEC, ANY_SPEC],
        out_specs=[tiles, tiles, tiles, rows_spec, rows_spec, pl.BlockSpec((8, D_MODEL), lambda i: (0, 0)), per],
        out_shape=[_sds((FF_TILES, tokens, FF_TILE), BF16)] * 3
        + [_sds((tokens, D_MODEL), F32), _sds((tokens, D_MODEL), BF16),
           _sds((8, D_MODEL), F32), _sds((n_seq, 8, D_MODEL), F32)],
        scratch_shapes=[weight, weight, weight, pltpu.SemaphoreType.DMA((3,))],
        compiler_params=_params(("arbitrary",)),
    )(df, gate, up, mod8, g_ffn, x1, dy, mix, w_gate_t, w_up_t, w_down)


def _mix_in_bwd(d_a, d_g, d_q, d_k, d_v, w_in, x2, dx1, mod8, g_mix, seq, ride, tm=512):
    tokens = x2.shape[0]
    per_seq = seq // tm
    n_seq = tokens // seq
    parts = (d_a, d_g, d_q, d_k, d_v)
    width = D_CONV
    n_ride = len(ride)
    ride_scatter = [s for _, s in ride]

    def body(*refs):
        da_ref, dg_ref, dq_ref, dk_ref, dv_ref, w_ref, x_ref, dx1_ref, m_ref, g_ref = refs[:10]
        ride_in = refs[10:10 + n_ride]
        gx_ref, dgm_ref, dm_ref = refs[10 + n_ride:13 + n_ride]
        ride_args = (ride_scatter, ride_in, refs[13 + n_ride:13 + 2 * n_ride]) + tuple(refs[13 + 2 * n_ride:])
        i = pl.program_id(0)

        @pl.when(i == 0)
        def _():
            _exchange_start(*ride_args)
            dgm_ref[...] = jnp.zeros_like(dgm_ref)

        @pl.when(i % per_seq == 0)
        def _():
            dm_ref[...] = jnp.zeros_like(dm_ref)

        dh = jnp.zeros((tm, D_MODEL), F32)
        for n, ref in enumerate((da_ref, dg_ref, dq_ref, dk_ref, dv_ref)):
            dh = dh + jnp.dot(ref[...], w_ref[pl.ds(n * width, width), :], preferred_element_type=F32)
        xv = x_ref[...]
        r = lax.rsqrt(jnp.mean(xv * xv, axis=-1, keepdims=True) + EPS)
        xhat = xv * r
        g = g_ref[...]
        dm_ref[0:1, :] += jnp.sum(dh, axis=0, keepdims=True)
        dm_ref[1:2, :] += jnp.sum(dh * (xhat * g), axis=0, keepdims=True)
        dn = dh * (1.0 + m_ref[1:2, :])
        dgm_ref[0:1, :] += jnp.sum(dn * xhat, axis=0, keepdims=True)
        dxh = dn * g
        gx_ref[...] = dx1_ref[...] + r * (dxh - xhat * jnp.mean(dxh * xhat, axis=-1, keepdims=True))

        @pl.when(i == tokens // tm - 1)
        def _():
            _exchange_wait(*ride_args)

    rows = pl.BlockSpec((tm, D_MODEL), lambda i: (i, 0))
    half = pl.BlockSpec((tm, width), lambda i: (i, 0))
    per = pl.BlockSpec((None, 8, D_MODEL), lambda i: (i // per_seq, 0, 0))
    return _call(
        body, name="mix_in_bwd", grid=(tokens // tm,),
        in_specs=[half] * 5 + [pl.BlockSpec((D_IN, D_MODEL), lambda i: (0, 0)), rows, rows, per,
                               pl.BlockSpec((1, D_MODEL), lambda i: (0, 0))] + [ANY_SPEC] * n_ride,
        out_specs=[rows, pl.BlockSpec((8, D_MODEL), lambda i: (0, 0)), per] + [ANY_SPEC] * n_ride,
        out_shape=[_sds((tokens, D_MODEL), F32), _sds((8, D_MODEL), F32), _sds((n_seq, 8, D_MODEL), F32)]
        + _exchange_shapes(ride),
        scratch_shapes=_exchange_sems(n_ride),
        compiler_params=_params(("arbitrary",)),
    )(*parts, w_in, x2, dx1, mod8, g_mix, *[a for a, _ in ride])


def _grad_matmul_parts(a_parts, b_parts, name, tk=1024):
    tokens = a_parts[0].shape[0]
    na, nb = len(a_parts), len(b_parts)
    ma, nbw = a_parts[0].shape[1], b_parts[0].shape[1]

    n_k = tokens // tk

    def body(*refs):
        a_refs, b_refs, o_ref, acc = refs[:na], refs[na:na + nb], refs[na + nb], refs[na + nb + 1]

        @pl.when(pl.program_id(0) == 0)
        def _():
            acc[...] = jnp.zeros_like(acc)

        for i in range(na):
            for j in range(nb):
                acc[pl.ds(i * ma, ma), pl.ds(j * nbw, nbw)] += lax.dot_general(
                    a_refs[i][...], b_refs[j][...], TN, preferred_element_type=F32)

        @pl.when(pl.program_id(0) == n_k - 1)
        def _():
            o_ref[...] = acc[...].astype(o_ref.dtype)

    return _call(
        body, name=name, grid=(n_k,),
        in_specs=[pl.BlockSpec((tk, ma), lambda k: (k, 0))] * na + [pl.BlockSpec((tk, nbw), lambda k: (k, 0))] * nb,
        out_specs=pl.BlockSpec((na * ma, nb * nbw), lambda k: (0, 0)),
        out_shape=_sds((na * ma, nb * nbw), BF16),
        scratch_shapes=[pltpu.VMEM((na * ma, nb * nbw), F32)],
        compiler_params=_params(("arbitrary",)),
    )(*a_parts, *b_parts)


def _grad_matmul_tiles(a, b, name, tk=1024):
    tiled_b = b.ndim == 3
    tiles, tokens, width = b.shape if tiled_b else a.shape
    other = a.shape[1] if tiled_b else b.shape[1]
    out_tile = (other, width) if tiled_b else (width, other)
    n_k = tokens // tk

    def body(a_ref, b_ref, o_ref, acc):
        @pl.when(pl.program_id(0) == 0)
        def _():
            acc[...] = jnp.zeros_like(acc)

        for t in range(tiles):
            lhs = a_ref[...] if tiled_b else a_ref[t]
            rhs = b_ref[t] if tiled_b else b_ref[...]
            acc[t] += lax.dot_general(lhs, rhs, TN, preferred_element_type=F32)

        @pl.when(pl.program_id(0) == n_k - 1)
        def _():
            o_ref[...] = acc[...].astype(o_ref.dtype)

    flat = pl.BlockSpec((tk, other), lambda k: (k, 0))
    tiled = pl.BlockSpec((tiles, tk, width), lambda k: (0, k, 0))
    return _call(
        body, name=name, grid=(n_k,),
        in_specs=[flat, tiled] if tiled_b else [tiled, flat],
        out_specs=pl.BlockSpec((tiles,) + out_tile, lambda k: (0, 0, 0)),
        out_shape=_sds((tiles,) + out_tile, BF16),
        scratch_shapes=[pltpu.VMEM((tiles,) + out_tile, F32)],
        compiler_params=_params(("arbitrary",)),
    )(a, b)


def _adamw(w, m, v, g, name, n_parts=0, tr=256):
    rows, cols = w.shape
    tr = min(tr, rows)
    c1 = 1.0 - ADAM_B1 ** ADAM_STEP
    c2 = 1.0 - ADAM_B2 ** ADAM_STEP

    def body(w_ref, m_ref, v_ref, g_ref, go_ref, d_ref, mo_ref, vo_ref):
        if n_parts:
            gv = g_ref[0].astype(F32)
            for p in range(1, n_parts):
                gv = gv + g_ref[p].astype(F32)
        else:
            gv = g_ref[...]
        go_ref[...] = gv
        mn = ADAM_B1 * m_ref[...] + (1.0 - ADAM_B1) * gv
        vn = ADAM_B2 * v_ref[...] + (1.0 - ADAM_B2) * (gv * gv)
        mo_ref[...] = mn
        vo_ref[...] = vn
        d_ref[...] = -ADAM_LR * ((mn / c1) / (jnp.sqrt(vn / c2) + ADAM_EPS) + ADAM_WD * w_ref[...])

    blk = pl.BlockSpec((tr, cols), lambda i: (i, 0))
    g_spec = pl.BlockSpec((n_parts, tr, cols), lambda i: (0, i, 0)) if n_parts else blk
    return _call(
        body, name=name, grid=(rows // tr,),
        in_specs=[blk, blk, blk, g_spec], out_specs=[blk] * 4,
        out_shape=[_sds((rows, cols), F32)] * 4,
        compiler_params=_params(("parallel",)),
    )(w, m, v, g)


def _cols_to_full(blocks):
    n, r, c = blocks.shape
    return jnp.transpose(blocks, (1, 0, 2)).reshape(r, n * c)


def _pad_lanes(v, width):
    return jnp.pad(v, ((0, 0), (0, width - v.shape[1])))


def kernel(x, c, w_ada, b_ada, g_mix, w_in, w_dw, b_dw, g_conv_ln, b_conv_ln, g_q, g_k, w_out, g_ffn, w_gate, w_up, w_down, loss_target, m_w_ada, m_b_ada, m_g_mix, m_w_in, m_w_dw, m_b_dw, m_g_conv_ln, m_b_conv_ln, m_g_q, m_g_k, m_w_out, m_g_ffn, m_w_gate, m_w_up, m_w_down, v_w_ada, v_b_ada, v_g_mix, v_w_in, v_w_dw, v_b_dw, v_g_conv_ln, v_b_conv_ln, v_g_q, v_g_k, v_w_out, v_g_ffn, v_w_gate, v_w_up, v_w_down):
    n_seq, seq, _ = x.shape
    tokens = n_seq * seq
    me = 4 * lax.axis_index("x") + 2 * lax.axis_index("y") + lax.axis_index("c")
    ada_cols = w_ada.shape[2]
    dw_cols = w_dw.shape[2]

    def transposed(w):
        return jnp.transpose(w[0])

    (c_g, w_in_g, w_dw_g) = _gather_by_chip([c, transposed(w_in).astype(BF16), w_dw[0]], "gather_weights")
    c_all = c_g.reshape(N_DEV * n_seq, D_MODEL)
    w_in_t = w_in_g.reshape(D_IN, D_MODEL)
    w_dw_f = _cols_to_full(w_dw_g)

    b_cols = lax.dynamic_slice(b_ada, (0, me * ada_cols), (1, ada_cols))
    mod_cols = _ada_fwd(c_all, w_ada[0], b_cols)
    (mod_g,) = _exchange([(mod_cols, False)], "gather_mod")
    mod_mine = lax.dynamic_slice(mod_g, (0, me * n_seq, 0), (N_DEV, n_seq, ada_cols))
    mod = jnp.transpose(mod_mine, (1, 0, 2)).reshape(n_seq, N_MOD, D_MODEL)
    mod8 = jnp.pad(mod, ((0, 0), (0, 8 - N_MOD), (0, 0)))

    x2 = x.reshape(tokens, D_MODEL)
    h1, proj = _mix_in(x2, mod8, g_mix, w_in_t, seq)
    proj3 = proj.reshape(D_IN // LANES, n_seq, seq, LANES)
    uc3 = _conv_fwd(proj3, w_dw_f, b_dw)
    g_q2, g_k2 = jnp.tile(g_q, (1, 2)), jnp.tile(g_k, (1, 2))
    y_att3, lse3, w_out_g, w_gate_g, w_up_g, w_down_g = _attn_fwd(
        proj3, g_q2, g_k2,
        [(w_out[0].astype(BF16), False), (transposed(w_gate).astype(BF16), False),
         (transposed(w_up).astype(BF16), False), (w_down[0].astype(BF16), False)])
    w_out_f = w_out_g.reshape(D_MODEL, D_MODEL)
    w_gate_f = w_gate_g.reshape(D_FF, D_MODEL)
    w_up_f = w_up_g.reshape(D_FF, D_MODEL)
    w_down_f = w_down_g.reshape(D_FF, D_MODEL)
    uc2 = uc3.reshape(tokens, D_CONV)
    y_att2 = y_att3.reshape(tokens, D_ATT)
    y_conv, mix, x1, h2 = _mix_out(uc2, y_att2, x2, mod8, g_conv_ln, b_conv_ln, g_ffn, w_out_f, seq)
    gate, up, dy, df, sq, dgate_f = _ffn_fwd(
        h2, w_gate_f, w_up_f, w_down_f, x1, loss_target.reshape(tokens, D_MODEL), mod8, seq)

    dgate, dup, act, dx1, dmix, dg_ffn, dmod_f = _ffn_bwd(
        df, gate, up, w_gate_f, w_up_f, w_down_f, x1, dy, mix, mod8, g_ffn, seq)
    duc2, do2, dgb_ln = _mix_out_bwd(dmix, uc2, g_conv_ln, b_conv_ln, w_out_f)
    d_a3, d_g3, dw_dw_p, db_dw_p = _conv_bwd(duc2.reshape(n_seq, seq, D_CONV), proj3, w_dw_f)
    gw_gate = _grad_matmul_tiles(dgate, h2, "grad_w_gate")
    gw_up = _grad_matmul_tiles(dup, h2, "grad_w_up")
    gw_down = _grad_matmul_tiles(act, df, "grad_w_down")
    gw_out = _grad_matmul_parts([y_conv, y_att2], [dmix], "grad_w_out")
    d_q3, d_k3, d_v3, dg_qk, p_gate, p_up, p_down, p_out = _attn_bwd(
        proj3, do2.reshape(n_seq, seq, D_ATT), y_att3, lse3, g_q2, g_k2,
        [(gw_gate.reshape(N_DEV, D_FF // N_DEV, D_MODEL), True), (gw_up.reshape(N_DEV, D_FF // N_DEV, D_MODEL), True),
         (gw_down.reshape(N_DEV, D_FF // N_DEV, D_MODEL), True),
         (gw_out.reshape(N_DEV, D_MODEL // N_DEV, D_MODEL), True)])
    flat = lambda t: t.reshape(tokens, t.shape[-1])
    d_a, d_g, d_q, d_k, d_v = flat(d_a3), flat(d_g3), flat(d_q3), flat(d_k3), flat(d_v3)
    gw_in = _grad_matmul_parts([d_a, d_g, d_q, d_k, d_v], [h1], "grad_w_in")
    grad_x2, dg_mix, dmod_m, p_in = _mix_in_bwd(
        d_a, d_g, d_q, d_k, d_v, w_in_t, x2, dx1, mod8, g_mix, seq,
        [(gw_in.reshape(N_DEV, D_IN // N_DEV, D_MODEL), True)])

    dmod = jnp.concatenate([dmod_m[:, 0], dmod_m[:, 1], dmod_f[:, 2], dmod_f[:, 0], dmod_f[:, 1], dgate_f[:, 0]], axis=1)
    dg_q = dg_qk[0:1, 0:HEAD_DIM] + dg_qk[0:1, HEAD_DIM:]
    dg_k = dg_qk[1:2, 0:HEAD_DIM] + dg_qk[1:2, HEAD_DIM:]
    loss_part = (0.5 / D_MODEL) * jnp.sum(sq[0:1, :], axis=1, keepdims=True)
    small = jnp.concatenate(
        [dg_mix[0:1], dg_ffn[0:1], db_dw_p[0:1], dgb_ln[0:1], dgb_ln[1:2],
         _pad_lanes(dg_q, LANES), _pad_lanes(dg_k, LANES), _pad_lanes(loss_part, LANES)], axis=1)
    n_small = small.shape[1] - LANES

    (dmod_g, small_g, dw_g) = _exchange([(dmod, False), (small, False), (dw_dw_p, False)], "gather_small_grads")

    dmod_all = dmod_g.reshape(N_DEV * n_seq, N_MOD * D_MODEL)
    dmod_cols = lax.dynamic_slice(dmod_all, (0, me * ada_cols), (N_DEV * n_seq, ada_cols))
    gw_ada, gb_ada = _ada_bwd(c_all, dmod_cols, dmod_all)

    res = {}
    res["w_ada"] = _adamw(w_ada[0], m_w_ada[0], v_w_ada[0], gw_ada, "adamw_w_ada")
    res["b_ada"] = _adamw(b_ada, m_b_ada, v_b_ada, gb_ada, "adamw_b_ada")
    def adamw_transposed(w, m, v, parts, name, tr):
        outs = _adamw(transposed(w), transposed(m), transposed(v), parts, name, N_DEV, tr=tr)
        return tuple(jnp.transpose(o) for o in outs)

    res["w_in"] = adamw_transposed(w_in, m_w_in, v_w_in, p_in, "adamw_w_in", 160)
    res["w_out"] = _adamw(w_out[0], m_w_out[0], v_w_out[0], p_out, "adamw_w_out", N_DEV)
    res["w_gate"] = adamw_transposed(w_gate, m_w_gate, v_w_gate, p_gate, "adamw_w_gate", 176)
    res["w_up"] = adamw_transposed(w_up, m_w_up, v_w_up, p_up, "adamw_w_up", 176)
    res["w_down"] = _adamw(w_down[0], m_w_down[0], v_w_down[0], p_down, "adamw_w_down", N_DEV, tr=176)
    dw_mine = lax.dynamic_slice(dw_g, (0, 0, me * dw_cols), (N_DEV, CONV_WIDTH, dw_cols))
    res["w_dw"] = _adamw(w_dw[0], m_w_dw[0], v_w_dw[0], dw_mine, "adamw_w_dw", N_DEV)

    small_names = ["g_mix", "g_ffn", "b_dw", "g_conv_ln", "b_conv_ln", "g_q", "g_k"]
    small_w = {"g_mix": (g_mix, m_g_mix, v_g_mix), "g_ffn": (g_ffn, m_g_ffn, v_g_ffn), "b_dw": (b_dw, m_b_dw, v_b_dw),
               "g_conv_ln": (g_conv_ln, m_g_conv_ln, v_g_conv_ln), "b_conv_ln": (b_conv_ln, m_b_conv_ln, v_b_conv_ln),
               "g_q": (g_q, m_g_q, v_g_q), "g_k": (g_k, m_g_k, v_g_k)}
    widths = [max(small_w[n][0].shape[1], LANES) for n in small_names]
    packed = [jnp.concatenate([_pad_lanes(small_w[n][i], wd) for n, wd in zip(small_names, widths)], axis=1) for i in range(3)]
    outs = _adamw(packed[0], packed[1], packed[2], small_g[:, :, :n_small], "adamw_small", N_DEV)
    off = 0
    for n, wd in zip(small_names, widths):
        real = small_w[n][0].shape[1]
        res[n] = tuple(o[:, off:off + real] for o in outs)
        off += wd
    loss = jnp.sum(small_g[:, 0, n_small])

    order = ["w_ada", "b_ada", "g_mix", "w_in", "w_dw", "b_dw", "g_conv_ln", "b_conv_ln", "g_q", "g_k",
             "w_out", "g_ffn", "w_gate", "w_up", "w_down"]
    lead = {"w_ada", "w_in", "w_dw", "w_out", "w_gate", "w_up", "w_down"}
    grads, deltas, new_m, new_v = [], [], [], []
    for n in order:
        g, d, mn, vn = res[n]
        g, d, mn, vn = (t[None] if n in lead else t for t in (g, d, mn, vn))
        grads.append(g)
        deltas.append(d)
        new_m.append(mn)
        new_v.append(vn)
    return (loss, grad_x2.reshape(n_seq, seq, D_MODEL), *grads, *deltas, *new_m, *new_v)
```

```python
import numpy as np
import jax
import jax.numpy as jnp
from jax import lax
from jax.experimental import pallas as pl
from jax.experimental.pallas import tpu as pltpu

F32 = jnp.float32
BF16 = jnp.bfloat16

N_DEV = 8
D_MODEL = 1024
D_CONV = 512
D_ATT = 512
HEAD_DIM = 64
CONV_WIDTH = 31
D_IN = 2 * D_CONV + 3 * D_ATT
D_FF = 2816
N_MOD = 6
EPS = 1e-6
RADIUS = 64
DILATIONS = (1, 4, 16)
Q_BLOCK = 128
LANES = 128
VMEM_LIMIT = 56 * 1024 * 1024

ADAM_LR = 0.001
ADAM_B1 = 0.9
ADAM_B2 = 0.999
ADAM_EPS = 1e-08
ADAM_WD = 0.01
ADAM_STEP = 10

NT = (((1,), (1,)), ((), ()))
TN = (((0,), (0,)), ((), ()))


def _call(body, **kw):
    return pl.pallas_call(body, **kw)


def _params(sem=None, vmem=VMEM_LIMIT):
    return pltpu.CompilerParams(dimension_semantics=sem, vmem_limit_bytes=vmem)


def _sig(x):
    return 1.0 / (1.0 + jnp.exp(-x))


def _sds(shape, dtype):
    return jax.ShapeDtypeStruct(shape, dtype)


N_PEER = N_DEV - 1
ANY_SPEC = pl.BlockSpec(memory_space=pl.ANY)


def _exchange_copies(scatter, ins, outs, *sems):
    n = len(ins)
    if n == 0:
        return [], []
    send_sems, recv_sems, local_sems = sems
    x, y, c = lax.axis_index("x"), lax.axis_index("y"), lax.axis_index("c")
    me = 4 * x + 2 * y + c

    def src(a, slot):
        return ins[a].at[slot] if scatter[a] else ins[a]

    local = [pltpu.make_async_copy(src(a, me), outs[a].at[me], local_sems.at[a]) for a in range(n)]
    flights = []
    for k in range(1, N_DEV):
        px = 1 - x if k & 4 else x
        py = 1 - y if k & 2 else y
        pc = 1 - c if k & 1 else c
        pid = 4 * px + 2 * py + pc
        for a in range(n):
            i = a * N_PEER + k - 1
            send, recv = (pltpu.make_async_remote_copy(
                src_ref=src(a, pid), dst_ref=outs[a].at[slot],
                send_sem=send_sems.at[i], recv_sem=recv_sems.at[i],
                device_id=(px, py, pc), device_id_type=pl.DeviceIdType.MESH) for slot in (me, pid))
            flights.append((send, recv))
    return local, flights


def _exchange_start(*args):
    local, flights = _exchange_copies(*args)
    for cp in local:
        cp.start()
    for send, _ in flights:
        send.start()


def _exchange_wait(*args):
    local, flights = _exchange_copies(*args)
    for send, recv in flights:
        send.wait_send()
        recv.wait_recv()
    for cp in local:
        cp.wait()


def _exchange_shapes(items):
    return [_sds((N_DEV,) + tuple(arr.shape[1:] if scatter else arr.shape), arr.dtype) for arr, scatter in items]


def _exchange_sems(n):
    if n == 0:
        return []
    return [pltpu.SemaphoreType.DMA((n * N_PEER,)), pltpu.SemaphoreType.DMA((n * N_PEER,)),
            pltpu.SemaphoreType.DMA((n,))]


def _gather_by_chip_phase(phase, ins, outs, send_sems, recv_sems, local_sems):
    n = len(ins)
    per = N_PEER
    x, y, c = lax.axis_index("x"), lax.axis_index("y"), lax.axis_index("c")
    me, sibling = (x, y, c), (x, y, 1 - c)
    chips = [(1 - x, y), (x, 1 - y), (1 - x, 1 - y)]

    def slot(px, py, pc):
        return 4 * px + 2 * py + pc

    def copy(a, k, block, to, src=None):
        dst = outs[a].at[slot(*block)]
        return pltpu.make_async_remote_copy(
            src_ref=dst if src is None else src, dst_ref=dst,
            send_sem=send_sems.at[a * per + k], recv_sem=recv_sems.at[a * per + k],
            device_id=to, device_id_type=pl.DeviceIdType.MESH)

    local = [pltpu.make_async_copy(ins[a], outs[a].at[slot(*me)], local_sems.at[a]) for a in range(n)]
    first = []
    for a in range(n):
        first.append(copy(a, 0, me, sibling, src=ins[a]))
        first += [copy(a, 1 + j, me, (*chip, c), src=ins[a]) for j, chip in enumerate(chips)]
    passed = [copy(a, 4 + j, (*chip, c), sibling) for j, chip in enumerate(chips) for a in range(n)]
    if phase == 0:
        for cp in local + first:
            cp.start()
    elif phase == 1:
        for j, chip in enumerate(chips):
            for a in range(n):
                copy(a, 1 + j, (*chip, c), me).wait_recv()
        for cp in passed:
            cp.start()
    else:
        for a in range(n):
            copy(a, 0, sibling, me).wait_recv()
            for j, chip in enumerate(chips):
                copy(a, 4 + j, (*chip, 1 - c), me).wait_recv()
        for cp in first + passed:
            cp.wait_send()
        for cp in local:
            cp.wait()


def _gather_by_chip(arrays, name):
    n = len(arrays)

    def body(*refs):
        for phase in range(3):
            _gather_by_chip_phase(phase, refs[:n], refs[n:2 * n], *refs[2 * n:])

    return _call(
        body, name=name, out_shape=_exchange_shapes([(arr, False) for arr in arrays]),
        in_specs=[ANY_SPEC] * n, out_specs=[ANY_SPEC] * n, scratch_shapes=_exchange_sems(n),
    )(*arrays)


def _exchange(items, name):
    n = len(items)
    scatter = [s for _, s in items]

    def body(*refs):
        args = (scatter, refs[:n], refs[n:2 * n]) + tuple(refs[2 * n:])
        _exchange_start(*args)
        _exchange_wait(*args)

    return _call(
        body, name=name, out_shape=_exchange_shapes(items),
        in_specs=[ANY_SPEC] * n, out_specs=[ANY_SPEC] * n, scratch_shapes=_exchange_sems(n),
    )(*[a for a, _ in items])


def _ada_fwd(c_all, w_ada, b_cols):
    def body(c_ref, w_ref, b_ref, o_ref):
        cv = c_ref[...]
        sc = (cv * _sig(cv)).astype(BF16)
        o_ref[...] = jnp.dot(sc, w_ref[...].astype(BF16), preferred_element_type=F32) + b_ref[...]

    return _call(body, name="ada_fwd", out_shape=_sds((c_all.shape[0], w_ada.shape[1]), F32),
                 compiler_params=_params())(c_all, w_ada, b_cols)


def _ada_bwd(c_all, dmod_cols, dmod_all):
    def body(c_ref, dc_ref, da_ref, gw_ref, gb_ref):
        cv = c_ref[...]
        sc = (cv * _sig(cv)).astype(BF16)
        gw_ref[...] = lax.dot_general(sc, dc_ref[...].astype(BF16), TN, preferred_element_type=F32)
        gb_ref[...] = jnp.sum(da_ref[...], axis=0, keepdims=True)

    return _call(body, name="ada_bwd",
                 out_shape=[_sds((c_all.shape[1], dmod_cols.shape[1]), F32), _sds((1, dmod_all.shape[1]), F32)],
                 compiler_params=_params())(c_all, dmod_cols, dmod_all)


MIX_ROWS = 128


def _mix_in(x2, mod8, g_mix, w_in, seq, tm=512):
    tokens = x2.shape[0]
    per_seq = seq // tm

    def body(x_ref, m_ref, g_ref, wt_ref, h_ref, p_ref, w_ref):
        @pl.when(pl.program_id(0) == 0)
        def _():
            w_ref[...] = wt_ref[...].T

        def normed(c):
            rows = pl.ds(c * MIX_ROWS, MIX_ROWS)
            xv = x_ref[rows, :]
            r = lax.rsqrt(jnp.mean(xv * xv, axis=-1, keepdims=True) + EPS)
            hb = ((xv * r * g_ref[...]) * (1.0 + m_ref[1:2, :]) + m_ref[0:1, :]).astype(BF16)
            h_ref[rows, :] = hb
            return hb

        ahead = normed(0)
        for c in range(tm // MIX_ROWS):
            hb = ahead
            if c + 1 < tm // MIX_ROWS:
                ahead = normed(c + 1)
            p = jnp.dot(hb, w_ref[...], preferred_element_type=F32)
            for cb in range(D_IN // LANES):
                p_ref[cb, pl.ds(c * MIX_ROWS, MIX_ROWS), :] = p[:, cb * LANES:(cb + 1) * LANES]

    return _call(
        body, name="mix_in", grid=(tokens // tm,),
        in_specs=[pl.BlockSpec((tm, D_MODEL), lambda i: (i, 0)),
                  pl.BlockSpec((None, 8, D_MODEL), lambda i: (i // per_seq, 0, 0)),
                  pl.BlockSpec((1, D_MODEL), lambda i: (0, 0)),
                  pl.BlockSpec((D_IN, D_MODEL), lambda i: (0, 0))],
        out_specs=[pl.BlockSpec((tm, D_MODEL), lambda i: (i, 0)),
                   pl.BlockSpec((D_IN // LANES, tm, LANES), lambda i: (0, i, 0))],
        out_shape=[_sds((tokens, D_MODEL), BF16), _sds((D_IN // LANES, tokens, LANES), F32)],
        scratch_shapes=[pltpu.VMEM((D_MODEL, D_IN), BF16)],
        compiler_params=_params(("arbitrary",)),
    )(x2, mod8, g_mix, w_in)


CONV_ROWS = 128
CONV_DW_ROWS = 32
CONV_DW_UNROLL = 8
CONV_HALO = 16


def _fill_shifted(xp, sh, seq):
    for b in range(8):
        sh[b, pl.ds(0, seq + 24), :] = xp[pl.ds(b, seq + 24), :]


def _conv_fwd(proj3, w_dw, b_dw):
    _, n_seq, seq, _ = proj3.shape
    n_cb = D_CONV // LANES

    def body(a_ref, g_ref, w_ref, b_ref, uc_ref, xp, sh):
        zeros = jnp.zeros((CONV_HALO, LANES), F32)
        xp[pl.ds(0, CONV_HALO), :] = zeros
        xp[pl.ds(CONV_HALO + seq, CONV_HALO), :] = zeros
        xp[pl.ds(CONV_HALO, seq), :] = a_ref[...] * _sig(g_ref[...])
        _fill_shifted(xp, sh, seq)

        def blk(i, carry):
            t0 = pl.multiple_of(i * CONV_ROWS, CONV_ROWS)
            acc = jnp.zeros((CONV_ROWS, LANES), F32)
            for j in range(CONV_WIDTH):
                jj = j + 1
                acc = acc + sh[jj % 8, pl.ds(t0 + 8 * (jj // 8), CONV_ROWS), :] * w_ref[j:j + 1, :]
            uc_ref[pl.ds(t0, CONV_ROWS), :] = acc + b_ref[...]
            return carry

        lax.fori_loop(0, seq // CONV_ROWS, blk, 0)

    return _call(
        body, name="conv_fwd", grid=(n_seq, n_cb),
        in_specs=[pl.BlockSpec((None, None, seq, LANES), lambda b, cb: (cb, b, 0, 0)),
                  pl.BlockSpec((None, None, seq, LANES), lambda b, cb: (n_cb + cb, b, 0, 0)),
                  pl.BlockSpec((CONV_WIDTH, LANES), lambda b, cb: (0, cb)),
                  pl.BlockSpec((1, LANES), lambda b, cb: (0, cb))],
        out_specs=pl.BlockSpec((None, seq, LANES), lambda b, cb: (b, 0, cb)),
        out_shape=_sds((n_seq, seq, D_CONV), F32),
        scratch_shapes=[pltpu.VMEM((seq + 2 * CONV_HALO, LANES), F32),
                        pltpu.VMEM((8, seq + 2 * CONV_HALO, LANES), F32)],
        compiler_params=_params(("parallel", "parallel")),
    )(proj3, proj3, w_dw, b_dw)


def _conv_bwd(duc3, proj3, w_dw):
    _, n_seq, seq, _ = proj3.shape
    n_cb = D_CONV // LANES

    def body(duc_ref, a_ref, g_ref, w_ref, da_ref, dg_ref, dw_ref, db_ref, xp, sh):
        @pl.when(pl.program_id(1) == 0)
        def _():
            dw_ref[...] = jnp.zeros_like(dw_ref)
            db_ref[...] = jnp.zeros_like(db_ref)

        zeros = jnp.zeros((CONV_HALO, LANES), F32)
        xp[pl.ds(0, CONV_HALO), :] = zeros
        xp[pl.ds(CONV_HALO + seq, CONV_HALO), :] = zeros
        xp[pl.ds(CONV_HALO, seq), :] = a_ref[...] * _sig(g_ref[...])
        _fill_shifted(xp, sh, seq)
        for j0 in range(0, CONV_WIDTH, 8):
            taps = range(j0, min(j0 + 8, CONV_WIDTH))

            def wblk(i, accs, taps=taps):
                for u in range(CONV_DW_UNROLL):
                    t0 = pl.multiple_of((i * CONV_DW_UNROLL + u) * CONV_DW_ROWS, CONV_DW_ROWS)
                    d = duc_ref[pl.ds(t0, CONV_DW_ROWS), :]
                    accs = tuple(acc + d * sh[(j + 1) % 8, pl.ds(t0 + 8 * ((j + 1) // 8), CONV_DW_ROWS), :]
                                 for acc, j in zip(accs, taps))
                return accs

            accs = lax.fori_loop(0, seq // (CONV_DW_ROWS * CONV_DW_UNROLL), wblk,
                                 tuple(jnp.zeros((CONV_DW_ROWS, LANES), F32) for _ in taps))
            for acc, j in zip(accs, taps):
                dw_ref[j:j + 1, :] += jnp.sum(acc, axis=0, keepdims=True)
        db_ref[0:1, :] += jnp.sum(duc_ref[...], axis=0, keepdims=True)
        xp[pl.ds(CONV_HALO, seq), :] = duc_ref[...]
        _fill_shifted(xp, sh, seq)

        def ublk(i, carry):
            t0 = pl.multiple_of(i * CONV_ROWS, CONV_ROWS)
            acc = jnp.zeros((CONV_ROWS, LANES), F32)
            for j in range(CONV_WIDTH):
                jj = CONV_WIDTH - j
                acc = acc + sh[jj % 8, pl.ds(t0 + 8 * (jj // 8), CONV_ROWS), :] * w_ref[j:j + 1, :]
            av = a_ref[pl.ds(t0, CONV_ROWS), :]
            sg = _sig(g_ref[pl.ds(t0, CONV_ROWS), :])
            da_ref[pl.ds(t0, CONV_ROWS), :] = (acc * sg).astype(BF16)
            dg_ref[pl.ds(t0, CONV_ROWS), :] = (acc * av * sg * (1.0 - sg)).astype(BF16)
            return carry

        lax.fori_loop(0, seq // CONV_ROWS, ublk, 0)

    return _call(
        body, name="conv_bwd", grid=(n_cb, n_seq),
        in_specs=[pl.BlockSpec((None, seq, LANES), lambda cb, b: (b, 0, cb)),
                  pl.BlockSpec((None, None, seq, LANES), lambda cb, b: (cb, b, 0, 0)),
                  pl.BlockSpec((None, None, seq, LANES), lambda cb, b: (n_cb + cb, b, 0, 0)),
                  pl.BlockSpec((CONV_WIDTH, LANES), lambda cb, b: (0, cb))],
        out_specs=[pl.BlockSpec((None, seq, LANES), lambda cb, b: (b, 0, cb)),
                   pl.BlockSpec((None, seq, LANES), lambda cb, b: (b, 0, cb)),
                   pl.BlockSpec((32, LANES), lambda cb, b: (0, cb)),
                   pl.BlockSpec((8, LANES), lambda cb, b: (0, cb))],
        out_shape=[_sds((n_seq, seq, D_CONV), BF16), _sds((n_seq, seq, D_CONV), BF16),
                   _sds((32, D_CONV), F32), _sds((8, D_CONV), F32)],
        scratch_shapes=[pltpu.VMEM((seq + 2 * CONV_HALO, LANES), F32),
                        pltpu.VMEM((8, seq + 2 * CONV_HALO, LANES), F32)],
        compiler_params=_params(("parallel", "arbitrary")),
    )(duc3, proj3, proj3, w_dw)


MASKED = 1e30
ATT_ROWS = 1024
ATT_BWD_ROWS = 2048
ATT_UNROLL = 8
ATT_FWD_UNROLL = 8


def _distance_mats(dil, seg_len):
    kw = min(2 * Q_BLOCK, seg_len)
    offsets = (0, -RADIUS, -2 * RADIUS) if kw == 2 * Q_BLOCK else (0,)
    a = np.arange(Q_BLOCK)[:, None]
    b = np.arange(kw)[None, :]
    mats = []
    for off in offsets:
        rel = np.abs(b + off - a)
        mats.append(np.where(rel <= RADIUS, dil * rel, MASKED))
    return jnp.asarray(np.stack(mats).astype(np.float32))


def _alibi_rows():
    s = np.zeros((4, 8, LANES), np.float32)
    for hp in range(4):
        for hl in range(2):
            s[hp, hl, :] = 2.0 ** (-(2 * hp + hl + 1))
    return jnp.asarray(s)


def _window(n, seg_len):
    i0 = pl.multiple_of(n * Q_BLOCK, Q_BLOCK)
    if seg_len <= Q_BLOCK:
        return i0, i0, 0
    per_seg = seg_len // Q_BLOCK
    j = n % per_seg
    seg0 = (n // per_seg) * seg_len
    ks_local = jnp.clip(j * Q_BLOCK - RADIUS, 0, seg_len - 2 * Q_BLOCK)
    ks = pl.multiple_of(seg0 + ks_local, RADIUS)
    var = jnp.where(j == 0, 0, jnp.where(j == per_seg - 1, 2, 1))
    return i0, ks, var


def _first_head(rows):
    return lax.broadcasted_iota(jnp.int32, (rows, LANES), 1) < HEAD_DIM


def _same_head():
    head = np.arange(LANES) // HEAD_DIM
    return jnp.asarray((head[:, None] == head[None, :]).astype(np.float32)).astype(BF16)


def _head_sum(x, same_ref):
    hi = x.astype(BF16)
    lo = (x - hi.astype(F32)).astype(BF16)
    return (jnp.dot(hi, same_ref[...], preferred_element_type=F32)
            + jnp.dot(lo, same_ref[...], preferred_element_type=F32))


def _head_mean(x, same_ref):
    return _head_sum(x, same_ref) * (1.0 / HEAD_DIM)


def _per_head(x, first):
    swapped = pltpu.roll(x, HEAD_DIM, 1)
    return jnp.where(first, x, swapped), jnp.where(first, swapped, x)


STRIDE = 4


def _gather_segments(src, dil, seq, tmp, put):
    if dil == 1:
        put(0, seq, src[pl.ds(0, seq), :])
    elif dil == STRIDE:
        seg = seq // dil
        for r in range(dil):
            put(r * seg, seg, src[pl.ds(r, seg, stride=dil), :])
    else:
        part, seg = seq // STRIDE, seq // dil
        for b in range(STRIDE):
            tmp[pl.ds(b * part, part), :] = src[pl.ds(b, part, stride=STRIDE), :]
        for b in range(STRIDE):
            for a in range(dil // STRIDE):
                put(b * part + a * seg, seg, tmp[pl.ds(b * part + a, seg, stride=dil // STRIDE), :])


def _scatter_segments(dst, get, dil, seq, tmp, accumulate):
    def write(rows, val):
        if accumulate:
            dst[rows, :] += val
        else:
            dst[rows, :] = val

    if dil == 1:
        write(pl.ds(0, seq), get(0, seq))
    elif dil == STRIDE:
        seg = seq // dil
        for r in range(dil):
            write(pl.ds(r, seg, stride=dil), get(r * seg, seg))
    else:
        part, seg = seq // STRIDE, seq // dil
        for b in range(STRIDE):
            for a in range(dil // STRIDE):
                tmp[pl.ds(b * part + a, seg, stride=dil // STRIDE), :] = get(b * part + a * seg, seg)
        for b in range(STRIDE):
            write(pl.ds(b, part, stride=STRIDE), tmp[pl.ds(b * part, part), :])


def _permute_rows(dst, src, dil, seq, tmp):
    def put(start, size, val):
        dst[pl.ds(start, size), :] = val.astype(dst.dtype)

    _gather_segments(src, dil, seq, tmp, put)


def _permute_rows_by_head(dst, src, dil, seq, tmp):
    def put(start, size, val):
        first = _first_head(size)
        dst[0, pl.ds(start, size), :] = jnp.where(first, val, 0.0).astype(dst.dtype)
        dst[1, pl.ds(start, size), :] = jnp.where(first, 0.0, val).astype(dst.dtype)

    _gather_segments(src, dil, seq, tmp, put)


def _qk_normalise(q_ref, g2_ref, same_ref, dst, seq, scale, step):
    def chunk(ci, carry):
        rows = pl.ds(pl.multiple_of(ci * step, step), step)
        qv = q_ref[rows, :]
        r = lax.rsqrt(_head_mean(qv * qv, same_ref) + EPS)
        dst[rows, :] = qv * r * (g2_ref[...] * scale)
        return carry

    lax.fori_loop(0, seq // step, chunk, 0)


def _attn_fwd(proj3, g_q2, g_k2, ride):
    _, n_seq, seq, _ = proj3.shape
    dms = [_distance_mats(d, seq // d) for d in DILATIONS]
    same = _same_head()
    col0 = 2 * D_CONV // LANES
    n_hp = D_ATT // LANES

    n_ride = len(ride)
    assert not any(scatter for _, scatter in ride), "the forward's ride is an all-gather"

    def body(*refs):
        q_ref, k_ref, v_ref, gq_ref, gk_ref, sl_ref, dm1, dm4, dm16, same_ref = refs[:10]
        ride_in = refs[10:10 + n_ride]
        y_ref, lse_ref = refs[10 + n_ride:12 + n_ride]
        ride_out = refs[12 + n_ride:12 + 2 * n_ride]
        (qf, kf, qp, kp, vp, oml_p, o1, o4, o16, m1, m4, m16, l1, l4, l16,
         tmp) = refs[12 + 2 * n_ride:28 + 2 * n_ride]
        o_nat, m_nat, l_nat = (o1, o4, o16), (m1, m4, m16), (l1, l4, l16)
        ride_args = (ride_in, ride_out) + tuple(refs[28 + 2 * n_ride:])
        step = pl.program_id(0) * n_hp + pl.program_id(1)
        n_steps = n_seq * n_hp

        if n_ride:
            for phase, at in enumerate((0, (3 * n_steps) // 4)):
                @pl.when(step == at)
                def _(phase=phase):
                    _gather_by_chip_phase(phase, *ride_args)

        dm_refs = (dm1, dm4, dm16)
        _qk_normalise(q_ref, gq_ref, same_ref, qf, seq, HEAD_DIM ** -0.5, ATT_ROWS)
        _qk_normalise(k_ref, gk_ref, same_ref, kf, seq, 1.0, ATT_ROWS)
        slopes = (sl_ref[0:1, 0:1], sl_ref[1:2, 0:1])
        for pi, dil in enumerate(DILATIONS):
            seg = seq // dil
            kw = min(2 * Q_BLOCK, seg)
            _permute_rows_by_head(qp, qf, dil, seq, tmp)
            _permute_rows(kp, kf, dil, seq, tmp)
            _permute_rows(vp, v_ref, dil, seq, tmp)

            def blk(it, carry, seg=seg, kw=kw, pi=pi, dst=oml_p):
                first = _first_head(Q_BLOCK)
                chains = [(sub, h) for sub in range(ATT_FWD_UNROLL) for h in range(2)]
                win = [_window(it * ATT_FWD_UNROLL + sub, seg) for sub in range(ATT_FWD_UNROLL)]
                s = {}
                for sub, h in chains:
                    i0, ks, var = win[sub]
                    s[sub, h] = lax.dot_general(qp[h, pl.ds(i0, Q_BLOCK), :], kp[pl.ds(ks, kw), :], NT,
                                                preferred_element_type=F32) - slopes[h] * dm_refs[pi][var]
                m, l, p = {}, {}, {}
                for c in chains:
                    m[c] = jnp.max(s[c], axis=1, keepdims=True)
                    e = jnp.exp(s[c] - m[c])
                    l[c] = jnp.sum(e, axis=1, keepdims=True)
                    p[c] = e.astype(BF16)
                o = {}
                for sub, h in chains:
                    o[sub, h] = jnp.dot(p[sub, h], vp[pl.ds(win[sub][1], kw), :], preferred_element_type=F32)
                packed = [jnp.concatenate([jnp.where(first, t[sub, 0], t[sub, 1]) for t in (o, m, l)], axis=1)
                          for sub in range(ATT_FWD_UNROLL)]
                span = ATT_FWD_UNROLL * Q_BLOCK
                dst[pl.ds(pl.multiple_of(it * span, span), span), :] = jnp.concatenate(packed, axis=0)
                return carry

            lax.fori_loop(0, seq // (Q_BLOCK * ATT_FWD_UNROLL), blk, 0)
            for n, nat in enumerate((o_nat[pi], m_nat[pi], l_nat[pi])):
                _scatter_segments(nat, lambda start, size, n=n: oml_p[pl.ds(start, size), pl.ds(n * LANES, LANES)],
                                  dil, seq, tmp, accumulate=False)

        def merge(ci, carry):
            rows = pl.ds(pl.multiple_of(ci * ATT_ROWS, ATT_ROWS), ATT_ROWS)
            ms = [m_nat[pi][rows, :] for pi in range(3)]
            m_all = jnp.maximum(jnp.maximum(ms[0], ms[1]), ms[2])
            es = [jnp.exp(m - m_all) for m in ms]
            l_all = sum(l_nat[pi][rows, :] * es[pi] for pi in range(3))
            inv = 1.0 / l_all
            o = sum(o_nat[pi][rows, :] * (es[pi] * inv) for pi in range(3))
            y_ref[rows, :] = o.astype(BF16)
            lse_ref[rows, :] = m_all + jnp.log(l_all)
            return carry

        lax.fori_loop(0, seq // ATT_ROWS, merge, 0)

        if n_ride:
            @pl.when(step == n_steps - 1)
            def _():
                _gather_by_chip_phase(2, *ride_args)

    def col(off):
        return pl.BlockSpec((None, None, seq, LANES), lambda b, hp: (col0 + off * n_hp + hp, b, 0, 0))

    def whole(arr):
        return pl.BlockSpec(arr.shape, lambda b, hp: (0,) * arr.ndim)

    rows_f32 = pltpu.VMEM((seq, LANES), F32)
    rows_bf16 = pltpu.VMEM((seq, LANES), BF16)
    return _call(
        body, name="attn_fwd", grid=(n_seq, n_hp),
        in_specs=[col(0), col(1), col(2), whole(g_q2), whole(g_k2),
                  pl.BlockSpec((None, 8, LANES), lambda b, hp: (hp, 0, 0)),
                  whole(dms[0]), whole(dms[1]), whole(dms[2]), whole(same)] + [ANY_SPEC] * n_ride,
        out_specs=[pl.BlockSpec((None, seq, LANES), lambda b, hp: (b, 0, hp)),
                   pl.BlockSpec((None, seq, LANES), lambda b, hp: (b, 0, hp))] + [ANY_SPEC] * n_ride,
        out_shape=[_sds((n_seq, seq, D_ATT), BF16), _sds((n_seq, seq, D_ATT), F32)] + _exchange_shapes(ride),
        scratch_shapes=[rows_f32, rows_f32, pltpu.VMEM((2, seq, LANES), BF16), rows_bf16, rows_bf16]
        + [pltpu.VMEM((seq, 3 * LANES), F32)] + [rows_f32] * 10 + _exchange_sems(n_ride),
        compiler_params=_params(("arbitrary", "arbitrary")),
    )(proj3, proj3, proj3, g_q2, g_k2, _alibi_rows(), *dms, same, *[a for a, _ in ride])


def _attn_bwd(proj3, do3, y_att3, lse3, g_q2, g_k2, ride):
    _, n_seq, seq, _ = proj3.shape
    dms = [_distance_mats(d, seq // d) for d in DILATIONS]
    same = _same_head()
    col0 = 2 * D_CONV // LANES
    n_hp = D_ATT // LANES

    n_ride = len(ride)
    ride_scatter = [s for _, s in ride]

    def body(*refs):
        (q_ref, k_ref, v_ref, do_ref, o_ref, lse_ref, gq_ref, gk_ref, sl_ref, dm1, dm4, dm16,
         same_ref) = refs[:13]
        ride_in = refs[13:13 + n_ride]
        dq_ref, dk_ref, dv_ref, dg_ref = refs[13 + n_ride:17 + n_ride]
        ride_out = refs[17 + n_ride:17 + 2 * n_ride]
        (qf, kf, qp, dop, kp, vp, sn, sp, dqp, dkp, dvp, dqn, dkn, dvn,
         tmp) = refs[17 + 2 * n_ride:32 + 2 * n_ride]
        ride_args = (ride_scatter, ride_in, ride_out) + tuple(refs[32 + 2 * n_ride:])
        dm_refs = (dm1, dm4, dm16)
        step = pl.program_id(0) * n_hp + pl.program_id(1)

        @pl.when(step == 0)
        def _():
            _exchange_start(*ride_args)
            dg_ref[...] = jnp.zeros_like(dg_ref)

        _qk_normalise(q_ref, gq_ref, same_ref, qf, seq, HEAD_DIM ** -0.5, ATT_BWD_ROWS)
        _qk_normalise(k_ref, gk_ref, same_ref, kf, seq, 1.0, ATT_BWD_ROWS)

        def stats(ci, carry):
            rows = pl.ds(pl.multiple_of(ci * ATT_BWD_ROWS, ATT_BWD_ROWS), ATT_BWD_ROWS)
            first = _first_head(ATT_BWD_ROWS)
            sn[0, rows, :], sn[1, rows, :] = _per_head(lse_ref[rows, :], first)
            prod = do_ref[rows, :] * o_ref[rows, :].astype(F32)
            sn[2, rows, :], sn[3, rows, :] = _per_head(_head_sum(prod, same_ref), first)
            return carry

        lax.fori_loop(0, seq // ATT_BWD_ROWS, stats, 0)
        slopes = (sl_ref[0:1, 0:1], sl_ref[1:2, 0:1])
        half = seq // (Q_BLOCK * ATT_UNROLL)
        region = seq // ATT_UNROLL

        for pi, dil in enumerate(DILATIONS):
            seg = seq // dil
            kw = min(2 * Q_BLOCK, seg)
            _permute_rows_by_head(qp, qf, dil, seq, tmp)
            _permute_rows_by_head(dop, do_ref, dil, seq, tmp)
            _permute_rows(kp, kf, dil, seq, tmp)
            _permute_rows(vp, v_ref, dil, seq, tmp)
            if dil == 1:
                st = sn
            else:
                st = sp
                for n in range(4):
                    _permute_rows(sp.at[n], sn.at[n], dil, seq, tmp)
            def touched(sub, seg=seg):
                lo, hi = sub * region, (sub + 1) * region
                if seg < region:
                    return lo, hi
                seg0 = lo // seg * seg
                return max(lo - RADIUS, seg0), min(hi + RADIUS, seg0 + seg)

            def summed(acc, start, size, touched=touched):
                pieces = []
                for c0 in range(start, start + size, RADIUS):
                    owners = [s for s in range(ATT_UNROLL) if touched(s)[0] <= c0 and c0 + RADIUS <= touched(s)[1]]
                    if pieces and pieces[-1][2] == owners:
                        pieces[-1][1] += RADIUS
                    else:
                        pieces.append([c0, RADIUS, owners])
                vals = [sum(acc[o, pl.ds(c0, n), :] for o in owners) for c0, n, owners in pieces]
                return vals[0] if len(vals) == 1 else jnp.concatenate(vals, axis=0)

            for sub in range(ATT_UNROLL):
                lo, hi = touched(sub)
                dkp[sub, pl.ds(lo, hi - lo), :] = jnp.zeros((hi - lo, LANES), F32)
                dvp[sub, pl.ds(lo, hi - lo), :] = jnp.zeros((hi - lo, LANES), F32)

            def blk(it, carry, seg=seg, kw=kw, pi=pi, st=st):
                first = _first_head(Q_BLOCK)
                chains = [(sub, h) for sub in range(ATT_UNROLL) for h in range(2)]
                win = [_window(it + sub * half, seg) for sub in range(ATT_UNROLL)]
                qrows = [pl.ds(w[0], Q_BLOCK) for w in win]
                krows = [pl.ds(w[1], kw) for w in win]

                def over_keys(n, sub):
                    t = st[n, qrows[sub], :]
                    return t if kw == LANES else jnp.concatenate([t] * (kw // LANES), axis=1)

                s, dp = {}, {}
                for sub, h in chains:
                    s[sub, h] = lax.dot_general(qp[h, qrows[sub], :], kp[krows[sub], :], NT,
                                                preferred_element_type=F32) - slopes[h] * dm_refs[pi][win[sub][2]]
                    dp[sub, h] = lax.dot_general(dop[h, qrows[sub], :], vp[krows[sub], :], NT,
                                                 preferred_element_type=F32)
                p, ds = {}, {}
                for sub, h in chains:
                    e = jnp.exp(s[sub, h] - over_keys(h, sub))
                    ds[sub, h] = (e * (dp[sub, h] - over_keys(2 + h, sub))).astype(BF16)
                    p[sub, h] = e.astype(BF16)
                dq, dk, dv = {}, {}, {}
                for sub, h in chains:
                    dq[sub, h] = jnp.dot(ds[sub, h], kp[krows[sub], :], preferred_element_type=F32)
                    dk[sub, h] = lax.dot_general(ds[sub, h], qp[h, qrows[sub], :], TN, preferred_element_type=F32)
                    dv[sub, h] = lax.dot_general(p[sub, h], dop[h, qrows[sub], :], TN, preferred_element_type=F32)
                for sub in range(ATT_UNROLL):
                    dqp[qrows[sub], :] = jnp.where(first, dq[sub, 0], dq[sub, 1])
                    dkp[sub, krows[sub], :] += dk[sub, 0] + dk[sub, 1]
                    dvp[sub, krows[sub], :] += dv[sub, 0] + dv[sub, 1]
                return carry

            lax.fori_loop(0, half, blk, 0)
            first_pattern = pi == 0
            if first_pattern:
                for r0 in range(0, seq, region):
                    rows = pl.ds(r0, region)
                    dqn[rows, :] = dqp[rows, :]
                    dkn[rows, :] = summed(dkp, r0, region)
                    dvn[rows, :] = summed(dvp, r0, region)
            else:
                _scatter_segments(dqn, lambda start, size: dqp[pl.ds(start, size), :], dil, seq, tmp, accumulate=True)
                for nat, acc in ((dkn, dkp), (dvn, dvp)):
                    _scatter_segments(nat, lambda start, size, acc=acc: summed(acc, start, size),
                                      dil, seq, tmp, accumulate=True)

        def finish(ci, carry):
            rows = pl.ds(pl.multiple_of(ci * ATT_BWD_ROWS, ATT_BWD_ROWS), ATT_BWD_ROWS)
            for src_ref, g_ref, dn, dst_ref, scale, row in (
                    (q_ref, gq_ref, dqn, dq_ref, HEAD_DIM ** -0.5, 0), (k_ref, gk_ref, dkn, dk_ref, 1.0, 1)):
                xv = src_ref[rows, :]
                r = lax.rsqrt(_head_mean(xv * xv, same_ref) + EPS)
                xhat = xv * r
                d = dn[rows, :] * scale
                dg_ref[row:row + 1, :] += jnp.sum(d * xhat, axis=0, keepdims=True)
                dxh = d * g_ref[...]
                dst_ref[rows, :] = (r * (dxh - xhat * _head_mean(dxh * xhat, same_ref))).astype(BF16)
            dv_ref[rows, :] = dvn[rows, :].astype(BF16)
            return carry

        lax.fori_loop(0, seq // ATT_BWD_ROWS, finish, 0)

        @pl.when(step == n_seq * n_hp - 1)
        def _():
            _exchange_wait(*ride_args)

    def col(off):
        return pl.BlockSpec((None, None, seq, LANES), lambda b, hp: (col0 + off * n_hp + hp, b, 0, 0))

    def whole(arr):
        return pl.BlockSpec(arr.shape, lambda b, hp: (0,) * arr.ndim)

    att = pl.BlockSpec((None, seq, LANES), lambda b, hp: (b, 0, hp))
    rows_f32 = pltpu.VMEM((seq, LANES), F32)
    rows_bf16 = pltpu.VMEM((seq, LANES), BF16)
    by_head_bf16 = pltpu.VMEM((2, seq, LANES), BF16)
    per_sub_f32 = pltpu.VMEM((ATT_UNROLL, seq, LANES), F32)
    stats_f32 = pltpu.VMEM((4, seq, LANES), F32)
    return _call(
        body, name="attn_bwd", grid=(n_seq, n_hp),
        in_specs=[col(0), col(1), col(2), att, att, att, whole(g_q2), whole(g_k2),
                  pl.BlockSpec((None, 8, LANES), lambda b, hp: (hp, 0, 0)),
                  whole(dms[0]), whole(dms[1]), whole(dms[2]), whole(same)] + [ANY_SPEC] * n_ride,
        out_specs=[att, att, att, pl.BlockSpec((8, LANES), lambda b, hp: (0, 0))] + [ANY_SPEC] * n_ride,
        out_shape=[_sds((n_seq, seq, D_ATT), BF16)] * 3 + [_sds((8, LANES), F32)] + _exchange_shapes(ride),
        scratch_shapes=[rows_f32, rows_f32, by_head_bf16, by_head_bf16, rows_bf16, rows_bf16, stats_f32, stats_f32,
                        rows_f32, per_sub_f32, per_sub_f32, rows_f32, rows_f32, rows_f32, rows_f32]
        + _exchange_sems(n_ride),
        compiler_params=_params(("arbitrary", "arbitrary")),
    )(proj3, proj3, proj3, do3, y_att3, lse3, g_q2, g_k2, _alibi_rows(), *dms, same, *[a for a, _ in ride])


def _mix_out(uc2, y_att2, x2, mod8, g_ln, b_ln, g_ffn, w_out, seq, tm=512):
    tokens = x2.shape[0]
    per_seq = seq // tm

    def body(uc_ref, ya_ref, x_ref, m_ref, gl_ref, bl_ref, gf_ref, w_ref, yc_ref, mix_ref, x1_ref, h2_ref):
        uc = uc_ref[...]
        mu = jnp.mean(uc, axis=-1, keepdims=True)
        cen = uc - mu
        rs = lax.rsqrt(jnp.mean(cen * cen, axis=-1, keepdims=True) + EPS)
        z = cen * rs * gl_ref[...] + bl_ref[...]
        yc = (z * _sig(z)).astype(BF16)
        yc_ref[...] = yc
        mix = (jnp.dot(yc, w_ref[pl.ds(0, D_CONV), :], preferred_element_type=F32)
               + jnp.dot(ya_ref[...], w_ref[pl.ds(D_CONV, D_ATT), :], preferred_element_type=F32))
        mix_ref[...] = mix.astype(BF16)
        x1 = x_ref[...] + m_ref[2:3, :] * mix
        x1_ref[...] = x1
        r = lax.rsqrt(jnp.mean(x1 * x1, axis=-1, keepdims=True) + EPS)
        h2_ref[...] = ((x1 * r * gf_ref[...]) * (1.0 + m_ref[4:5, :]) + m_ref[3:4, :]).astype(BF16)

    def rows(width):
        return pl.BlockSpec((tm, width), lambda i: (i, 0))

    def vec(width):
        return pl.BlockSpec((1, width), lambda i: (0, 0))

    return _call(
        body, name="mix_out", grid=(tokens // tm,),
        in_specs=[rows(D_CONV), rows(D_ATT), rows(D_MODEL),
                  pl.BlockSpec((None, 8, D_MODEL), lambda i: (i // per_seq, 0, 0)),
                  vec(D_CONV), vec(D_CONV), vec(D_MODEL),
                  pl.BlockSpec((D_MODEL, D_MODEL), lambda i: (0, 0))],
        out_specs=[rows(D_CONV), rows(D_MODEL), rows(D_MODEL), rows(D_MODEL)],
        out_shape=[_sds((tokens, D_CONV), BF16), _sds((tokens, D_MODEL), BF16),
                   _sds((tokens, D_MODEL), F32), _sds((tokens, D_MODEL), BF16)],
        compiler_params=_params(("parallel",)),
    )(uc2, y_att2, x2, mod8, g_ln, b_ln, g_ffn, w_out)


def _mix_out_bwd(dmix, uc2, g_ln, b_ln, w_out, tm=512):
    tokens = dmix.shape[0]

    def body(dm_ref, uc_ref, gl_ref, bl_ref, w_ref, duc_ref, do_ref, dgb_ref):
        @pl.when(pl.program_id(0) == 0)
        def _():
            dgb_ref[...] = jnp.zeros_like(dgb_ref)

        dmv = dm_ref[...]
        dyc = lax.dot_general(dmv, w_ref[pl.ds(0, D_CONV), :], NT, preferred_element_type=F32)
        do_ref[...] = lax.dot_general(dmv, w_ref[pl.ds(D_CONV, D_ATT), :], NT, preferred_element_type=F32)
        uc = uc_ref[...]
        mu = jnp.mean(uc, axis=-1, keepdims=True)
        cen = uc - mu
        rs = lax.rsqrt(jnp.mean(cen * cen, axis=-1, keepdims=True) + EPS)
        xh = cen * rs
        z = xh * gl_ref[...] + bl_ref[...]
        sg = _sig(z)
        dz = dyc * (sg * (1.0 + z * (1.0 - sg)))
        dgb_ref[0:1, :] += jnp.sum(dz * xh, axis=0, keepdims=True)
        dgb_ref[1:2, :] += jnp.sum(dz, axis=0, keepdims=True)
        dxh = dz * gl_ref[...]
        duc_ref[...] = rs * (dxh - jnp.mean(dxh, axis=-1, keepdims=True)
                             - xh * jnp.mean(dxh * xh, axis=-1, keepdims=True))

    return _call(
        body, name="mix_out_bwd", grid=(tokens // tm,),
        in_specs=[pl.BlockSpec((tm, D_MODEL), lambda i: (i, 0)),
                  pl.BlockSpec((tm, D_CONV), lambda i: (i, 0)),
                  pl.BlockSpec((1, D_CONV), lambda i: (0, 0)),
                  pl.BlockSpec((1, D_CONV), lambda i: (0, 0)),
                  pl.BlockSpec((D_MODEL, D_MODEL), lambda i: (0, 0))],
        out_specs=[pl.BlockSpec((tm, D_CONV), lambda i: (i, 0)),
                   pl.BlockSpec((tm, D_ATT), lambda i: (i, 0)),
                   pl.BlockSpec((8, D_CONV), lambda i: (0, 0))],
        out_shape=[_sds((tokens, D_CONV), F32), _sds((tokens, D_ATT), F32), _sds((8, D_CONV), F32)],
        compiler_params=_params(("arbitrary",)),
    )(dmix, uc2, g_ln, b_ln, w_out)


FF_TILE = 256
FF_TILES = D_FF // FF_TILE


def _load_once(hbm_refs, vmem_refs, sems):
    @pl.when(pl.program_id(0) == 0)
    def _():
        copies = [pltpu.make_async_copy(src, dst, sems.at[n]) for n, (src, dst) in enumerate(zip(hbm_refs, vmem_refs))]
        for cp in copies:
            cp.start()
        for cp in copies:
            cp.wait()


def _ffn_fwd(h2, w_gate_t, w_up_t, w_down, x1, target, mod8, seq, tm=512):
    tokens = h2.shape[0]
    per_seq = seq // tm
    n_seq = tokens // seq

    def body(h_ref, m_ref, x1_ref, t_ref, wg_hbm, wu_hbm, wd_hbm, gate_ref, up_ref, dy_ref, df_ref, sq_ref, dgf_ref,
             wg, wu, wd, w_sems):
        i = pl.program_id(0)
        _load_once((wg_hbm, wu_hbm, wd_hbm), (wg, wu, wd), w_sems)

        @pl.when(i == 0)
        def _():
            sq_ref[...] = jnp.zeros_like(sq_ref)

        @pl.when(i % per_seq == 0)
        def _():
            dgf_ref[...] = jnp.zeros_like(dgf_ref)

        hv = h_ref[...]

        def gate_up(t):
            rows = pl.ds(t * FF_TILE, FF_TILE)
            return (lax.dot_general(hv, wg[rows, :], NT, preferred_element_type=F32),
                    lax.dot_general(hv, wu[rows, :], NT, preferred_element_type=F32))

        fv = jnp.zeros((tm, D_MODEL), F32)
        ahead = gate_up(0)
        for t in range(FF_TILES):
            gate, up = ahead
            if t + 1 < FF_TILES:
                ahead = gate_up(t + 1)
            gate_ref[t] = gate.astype(BF16)
            up_ref[t] = up.astype(BF16)
            act = (gate * _sig(gate) * up).astype(BF16)
            fv = fv + jnp.dot(act, wd[pl.ds(t * FF_TILE, FF_TILE), :], preferred_element_type=F32)
        gate_f = m_ref[5:6, :]
        diff = x1_ref[...] + gate_f * fv - t_ref[...]
        sq_ref[0:1, :] += jnp.sum(diff * diff, axis=0, keepdims=True)
        dy = diff * (1.0 / D_MODEL)
        dy_ref[...] = dy
        df_ref[...] = (gate_f * dy).astype(BF16)
        dgf_ref[0:1, :] += jnp.sum(dy * fv, axis=0, keepdims=True)

    rows_spec = pl.BlockSpec((tm, D_MODEL), lambda i: (i, 0))
    per = pl.BlockSpec((None, 8, D_MODEL), lambda i: (i // per_seq, 0, 0))
    tiles = pl.BlockSpec((FF_TILES, tm, FF_TILE), lambda i: (0, i, 0))
    weight = pltpu.VMEM((D_FF, D_MODEL), BF16)
    return _call(
        body, name="ffn_fwd", grid=(tokens // tm,),
        in_specs=[rows_spec, per, rows_spec, rows_spec, ANY_SPEC, ANY_SPEC, ANY_SPEC],
        out_specs=[tiles, tiles, rows_spec, rows_spec, pl.BlockSpec((8, D_MODEL), lambda i: (0, 0)), per],
        out_shape=[_sds((FF_TILES, tokens, FF_TILE), BF16), _sds((FF_TILES, tokens, FF_TILE), BF16),
                   _sds((tokens, D_MODEL), F32), _sds((tokens, D_MODEL), BF16),
                   _sds((8, D_MODEL), F32), _sds((n_seq, 8, D_MODEL), F32)],
        scratch_shapes=[weight, weight, weight, pltpu.SemaphoreType.DMA((3,))],
        compiler_params=_params(("arbitrary",)),
    )(h2, mod8, x1, target, w_gate_t, w_up_t, w_down)


def _ffn_bwd(df, gate, up, w_gate_t, w_up_t, w_down, x1, dy, mix, mod8, g_ffn, seq, tm=256):
    tokens = df.shape[0]
    per_seq = seq // tm
    n_seq = tokens // seq

    def body(df_ref, gate_ref, up_ref, m_ref, g_ref, x1_ref, dy_ref, mix_ref, wg_hbm, wu_hbm, wd_hbm,
             dgate_ref, dup_ref, act_ref, dx1_ref, dmix_ref, dg_ref, dm_ref, wg, wu, wd, w_sems):
        i = pl.program_id(0)
        _load_once((wg_hbm, wu_hbm, wd_hbm), (wg, wu, wd), w_sems)

        @pl.when(i == 0)
        def _():
            dg_ref[...] = jnp.zeros_like(dg_ref)

        @pl.when(i % per_seq == 0)
        def _():
            dm_ref[...] = jnp.zeros_like(dm_ref)

        dfv = df_ref[...]

        def d_act(t):
            return lax.dot_general(dfv, wd[pl.ds(t * FF_TILE, FF_TILE), :], NT, preferred_element_type=F32)

        dh = jnp.zeros((tm, D_MODEL), F32)
        ahead = d_act(0)
        for t in range(FF_TILES):
            dact = ahead
            if t + 1 < FF_TILES:
                ahead = d_act(t + 1)
            rows = pl.ds(t * FF_TILE, FF_TILE)
            gv = gate_ref[t].astype(F32)
            uv = up_ref[t].astype(F32)
            sg = _sig(gv)
            silu = gv * sg
            act_ref[t] = (silu * uv).astype(BF16)
            dup = (dact * silu).astype(BF16)
            dgate = (dact * uv * (sg * (1.0 + gv * (1.0 - sg)))).astype(BF16)
            dup_ref[t] = dup
            dgate_ref[t] = dgate
            dh = dh + (jnp.dot(dgate, wg[rows, :], preferred_element_type=F32)
                       + jnp.dot(dup, wu[rows, :], preferred_element_type=F32))
        g = g_ref[...]
        x1v = x1_ref[...]
        rs = lax.rsqrt(jnp.mean(x1v * x1v, axis=-1, keepdims=True) + EPS)
        xhat = x1v * rs
        dm_ref[0:1, :] += jnp.sum(dh, axis=0, keepdims=True)
        dm_ref[1:2, :] += jnp.sum(dh * (xhat * g), axis=0, keepdims=True)
        dn = dh * (1.0 + m_ref[4:5, :])
        dg_ref[0:1, :] += jnp.sum(dn * xhat, axis=0, keepdims=True)
        dxh = dn * g
        dx1 = dy_ref[...] + rs * (dxh - xhat * jnp.mean(dxh * xhat, axis=-1, keepdims=True))
        dx1_ref[...] = dx1
        dm_ref[2:3, :] += jnp.sum(dx1 * mix_ref[...].astype(F32), axis=0, keepdims=True)
        dmix_ref[...] = (m_ref[2:3, :] * dx1).astype(BF16)

    rows_spec = pl.BlockSpec((tm, D_MODEL), lambda i: (i, 0))
    per = pl.BlockSpec((None, 8, D_MODEL), lambda i: (i // per_seq, 0, 0))
    tiles = pl.BlockSpec((FF_TILES, tm, FF_TILE), lambda i: (0, i, 0))
    weight = pltpu.VMEM((D_FF, D_MODEL), BF16)
    return _call(
        body, name="ffn_bwd", grid=(tokens // tm,),
        in_specs=[rows_spec, tiles, tiles, per, pl.BlockSpec((1, D_MODEL), lambda i: (0, 0)),
                  rows_spec, rows_spec, rows_spec, ANY_SPEC, ANY_SPEC, ANY_SPEC],
        out_specs=[tiles, tiles, tiles, rows_spec, rows_spec, pl.BlockSpec((8, D_MODEL), lambda i: (0, 0)), per],
        out_shape=[_sds((FF_TILES, tokens, FF_TILE), BF16)] * 3
        + [_sds((tokens, D_MODEL), F32), _sds((tokens, D_MODEL), BF16),
           _sds((8, D_MODEL), F32), _sds((n_seq, 8, D_MODEL), F32)],
        scratch_shapes=[weight, weight, weight, pltpu.SemaphoreType.DMA((3,))],
        compiler_params=_params(("arbitrary",)),
    )(df, gate, up, mod8, g_ffn, x1, dy, mix, w_gate_t, w_up_t, w_down)


def _mix_in_bwd(d_a, d_g, d_q, d_k, d_v, w_in, x2, dx1, mod8, g_mix, seq, ride, tm=512):
    tokens = x2.shape[0]
    per_seq = seq // tm
    n_seq = tokens // seq
    parts = (d_a, d_g, d_q, d_k, d_v)
    width = D_CONV
    n_ride = len(ride)
    ride_scatter = [s for _, s in ride]

    def body(*refs):
        da_ref, dg_ref, dq_ref, dk_ref, dv_ref, w_ref, x_ref, dx1_ref, m_ref, g_ref = refs[:10]
        ride_in = refs[10:10 + n_ride]
        gx_ref, dgm_ref, dm_ref = refs[10 + n_ride:13 + n_ride]
        ride_args = (ride_scatter, ride_in, refs[13 + n_ride:13 + 2 * n_ride]) + tuple(refs[13 + 2 * n_ride:])
        i = pl.program_id(0)

        @pl.when(i == 0)
        def _():
            _exchange_start(*ride_args)
            dgm_ref[...] = jnp.zeros_like(dgm_ref)

        @pl.when(i % per_seq == 0)
        def _():
            dm_ref[...] = jnp.zeros_like(dm_ref)

        dh = jnp.zeros((tm, D_MODEL), F32)
        for n, ref in enumerate((da_ref, dg_ref, dq_ref, dk_ref, dv_ref)):
            dh = dh + jnp.dot(ref[...], w_ref[pl.ds(n * width, width), :], preferred_element_type=F32)
        xv = x_ref[...]
        r = lax.rsqrt(jnp.mean(xv * xv, axis=-1, keepdims=True) + EPS)
        xhat = xv * r
        g = g_ref[...]
        dm_ref[0:1, :] += jnp.sum(dh, axis=0, keepdims=True)
        dm_ref[1:2, :] += jnp.sum(dh * (xhat * g), axis=0, keepdims=True)
        dn = dh * (1.0 + m_ref[1:2, :])
        dgm_ref[0:1, :] += jnp.sum(dn * xhat, axis=0, keepdims=True)
        dxh = dn * g
        gx_ref[...] = dx1_ref[...] + r * (dxh - xhat * jnp.mean(dxh * xhat, axis=-1, keepdims=True))

        @pl.when(i == tokens // tm - 1)
        def _():
            _exchange_wait(*ride_args)

    rows = pl.BlockSpec((tm, D_MODEL), lambda i: (i, 0))
    half = pl.BlockSpec((tm, width), lambda i: (i, 0))
    per = pl.BlockSpec((None, 8, D_MODEL), lambda i: (i // per_seq, 0, 0))
    return _call(
        body, name="mix_in_bwd", grid=(tokens // tm,),
        in_specs=[half] * 5 + [pl.BlockSpec((D_IN, D_MODEL), lambda i: (0, 0)), rows, rows, per,
                               pl.BlockSpec((1, D_MODEL), lambda i: (0, 0))] + [ANY_SPEC] * n_ride,
        out_specs=[rows, pl.BlockSpec((8, D_MODEL), lambda i: (0, 0)), per] + [ANY_SPEC] * n_ride,
        out_shape=[_sds((tokens, D_MODEL), F32), _sds((8, D_MODEL), F32), _sds((n_seq, 8, D_MODEL), F32)]
        + _exchange_shapes(ride),
        scratch_shapes=_exchange_sems(n_ride),
        compiler_params=_params(("arbitrary",)),
    )(*parts, w_in, x2, dx1, mod8, g_mix, *[a for a, _ in ride])


def _grad_matmul_parts(a_parts, b_parts, name, tk=1024):
    tokens = a_parts[0].shape[0]
    na, nb = len(a_parts), len(b_parts)
    ma, nbw = a_parts[0].shape[1], b_parts[0].shape[1]

    n_k = tokens // tk

    def body(*refs):
        a_refs, b_refs, o_ref, acc = refs[:na], refs[na:na + nb], refs[na + nb], refs[na + nb + 1]

        @pl.when(pl.program_id(0) == 0)
        def _():
            acc[...] = jnp.zeros_like(acc)

        for i in range(na):
            for j in range(nb):
                acc[pl.ds(i * ma, ma), pl.ds(j * nbw, nbw)] += lax.dot_general(
                    a_refs[i][...], b_refs[j][...], TN, preferred_element_type=F32)

        @pl.when(pl.program_id(0) == n_k - 1)
        def _():
            o_ref[...] = acc[...].astype(o_ref.dtype)

    return _call(
        body, name=name, grid=(n_k,),
        in_specs=[pl.BlockSpec((tk, ma), lambda k: (k, 0))] * na + [pl.BlockSpec((tk, nbw), lambda k: (k, 0))] * nb,
        out_specs=pl.BlockSpec((na * ma, nb * nbw), lambda k: (0, 0)),
        out_shape=_sds((na * ma, nb * nbw), BF16),
        scratch_shapes=[pltpu.VMEM((na * ma, nb * nbw), F32)],
        compiler_params=_params(("arbitrary",)),
    )(*a_parts, *b_parts)


def _grad_matmul_tiles(a, b, name, tk=1024):
    tiled_b = b.ndim == 3
    tiles, tokens, width = b.shape if tiled_b else a.shape
    other = a.shape[1] if tiled_b else b.shape[1]
    out_tile = (other, width) if tiled_b else (width, other)
    n_k = tokens // tk

    def body(a_ref, b_ref, o_ref, acc):
        @pl.when(pl.program_id(0) == 0)
        def _():
            acc[...] = jnp.zeros_like(acc)

        for t in range(tiles):
            lhs = a_ref[...] if tiled_b else a_ref[t]
            rhs = b_ref[t] if tiled_b else b_ref[...]
            acc[t] += lax.dot_general(lhs, rhs, TN, preferred_element_type=F32)

        @pl.when(pl.program_id(0) == n_k - 1)
        def _():
            o_ref[...] = acc[...].astype(o_ref.dtype)

    flat = pl.BlockSpec((tk, other), lambda k: (k, 0))
    tiled = pl.BlockSpec((tiles, tk, width), lambda k: (0, k, 0))
    return _call(
        body, name=name, grid=(n_k,),
        in_specs=[flat, tiled] if tiled_b else [tiled, flat],
        out_specs=pl.BlockSpec((tiles,) + out_tile, lambda k: (0, 0, 0)),
        out_shape=_sds((tiles,) + out_tile, BF16),
        scratch_shapes=[pltpu.VMEM((tiles,) + out_tile, F32)],
        compiler_params=_params(("arbitrary",)),
    )(a, b)


def _adamw(w, m, v, g, name, n_parts=0, tr=256):
    rows, cols = w.shape
    tr = min(tr, rows)
    c1 = 1.0 - ADAM_B1 ** ADAM_STEP
    c2 = 1.0 - ADAM_B2 ** ADAM_STEP

    def body(w_ref, m_ref, v_ref, g_ref, go_ref, d_ref, mo_ref, vo_ref):
        if n_parts:
            gv = g_ref[0].astype(F32)
            for p in range(1, n_parts):
                gv = gv + g_ref[p].astype(F32)
        else:
            gv = g_ref[...]
        go_ref[...] = gv
        mn = ADAM_B1 * m_ref[...] + (1.0 - ADAM_B1) * gv
        vn = ADAM_B2 * v_ref[...] + (1.0 - ADAM_B2) * (gv * gv)
        mo_ref[...] = mn
        vo_ref[...] = vn
        d_ref[...] = -ADAM_LR * ((mn / c1) / (jnp.sqrt(vn / c2) + ADAM_EPS) + ADAM_WD * w_ref[...])

    blk = pl.BlockSpec((tr, cols), lambda i: (i, 0))
    g_spec = pl.BlockSpec((n_parts, tr, cols), lambda i: (0, i, 0)) if n_parts else blk
    return _call(
        body, name=name, grid=(rows // tr,),
        in_specs=[blk, blk, blk, g_spec], out_specs=[blk] * 4,
        out_shape=[_sds((rows, cols), F32)] * 4,
        compiler_params=_params(("parallel",)),
    )(w, m, v, g)


def _cols_to_full(blocks):
    n, r, c = blocks.shape
    return jnp.transpose(blocks, (1, 0, 2)).reshape(r, n * c)


def _pad_lanes(v, width):
    return jnp.pad(v, ((0, 0), (0, width - v.shape[1])))


def kernel(x, c, w_ada, b_ada, g_mix, w_in, w_dw, b_dw, g_conv_ln, b_conv_ln, g_q, g_k, w_out, g_ffn, w_gate, w_up, w_down, loss_target, m_w_ada, m_b_ada, m_g_mix, m_w_in, m_w_dw, m_b_dw, m_g_conv_ln, m_b_conv_ln, m_g_q, m_g_k, m_w_out, m_g_ffn, m_w_gate, m_w_up, m_w_down, v_w_ada, v_b_ada, v_g_mix, v_w_in, v_w_dw, v_b_dw, v_g_conv_ln, v_b_conv_ln, v_g_q, v_g_k, v_w_out, v_g_ffn, v_w_gate, v_w_up, v_w_down):
    n_seq, seq, _ = x.shape
    tokens = n_seq * seq
    me = 4 * lax.axis_index("x") + 2 * lax.axis_index("y") + lax.axis_index("c")
    ada_cols = w_ada.shape[2]
    dw_cols = w_dw.shape[2]

    def transposed(w):
        return jnp.transpose(w[0])

    (c_g, w_in_g, w_dw_g) = _gather_by_chip([c, transposed(w_in).astype(BF16), w_dw[0]], "gather_weights")
    c_all = c_g.reshape(N_DEV * n_seq, D_MODEL)
    w_in_t = w_in_g.reshape(D_IN, D_MODEL)
    w_dw_f = _cols_to_full(w_dw_g)

    b_cols = lax.dynamic_slice(b_ada, (0, me * ada_cols), (1, ada_cols))
    mod_cols = _ada_fwd(c_all, w_ada[0], b_cols)
    (mod_g,) = _exchange([(mod_cols, False)], "gather_mod")
    mod_mine = lax.dynamic_slice(mod_g, (0, me * n_seq, 0), (N_DEV, n_seq, ada_cols))
    mod = jnp.transpose(mod_mine, (1, 0, 2)).reshape(n_seq, N_MOD, D_MODEL)
    mod8 = jnp.pad(mod, ((0, 0), (0, 8 - N_MOD), (0, 0)))

    x2 = x.reshape(tokens, D_MODEL)
    h1, proj = _mix_in(x2, mod8, g_mix, w_in_t, seq)
    proj3 = proj.reshape(D_IN // LANES, n_seq, seq, LANES)
    uc3 = _conv_fwd(proj3, w_dw_f, b_dw)
    g_q2, g_k2 = jnp.tile(g_q, (1, 2)), jnp.tile(g_k, (1, 2))
    y_att3, lse3, w_out_g, w_gate_g, w_up_g, w_down_g = _attn_fwd(
        proj3, g_q2, g_k2,
        [(w_out[0].astype(BF16), False), (transposed(w_gate).astype(BF16), False),
         (transposed(w_up).astype(BF16), False), (w_down[0].astype(BF16), False)])
    w_out_f = w_out_g.reshape(D_MODEL, D_MODEL)
    w_gate_f = w_gate_g.reshape(D_FF, D_MODEL)
    w_up_f = w_up_g.reshape(D_FF, D_MODEL)
    w_down_f = w_down_g.reshape(D_FF, D_MODEL)
    uc2 = uc3.reshape(tokens, D_CONV)
    y_att2 = y_att3.reshape(tokens, D_ATT)
    y_conv, mix, x1, h2 = _mix_out(uc2, y_att2, x2, mod8, g_conv_ln, b_conv_ln, g_ffn, w_out_f, seq)
    gate, up, dy, df, sq, dgate_f = _ffn_fwd(
        h2, w_gate_f, w_up_f, w_down_f, x1, loss_target.reshape(tokens, D_MODEL), mod8, seq)

    dgate, dup, act, dx1, dmix, dg_ffn, dmod_f = _ffn_bwd(
        df, gate, up, w_gate_f, w_up_f, w_down_f, x1, dy, mix, mod8, g_ffn, seq)
    duc2, do2, dgb_ln = _mix_out_bwd(dmix, uc2, g_conv_ln, b_conv_ln, w_out_f)
    d_a3, d_g3, dw_dw_p, db_dw_p = _conv_bwd(duc2.reshape(n_seq, seq, D_CONV), proj3, w_dw_f)
    gw_gate = _grad_matmul_tiles(dgate, h2, "grad_w_gate")
    gw_up = _grad_matmul_tiles(dup, h2, "grad_w_up")
    gw_down = _grad_matmul_tiles(act, df, "grad_w_down")
    gw_out = _grad_matmul_parts([y_conv, y_att2], [dmix], "grad_w_out")
    d_q3, d_k3, d_v3, dg_qk, p_gate, p_up, p_down, p_out = _attn_bwd(
        proj3, do2.reshape(n_seq, seq, D_ATT), y_att3, lse3, g_q2, g_k2,
        [(gw_gate.reshape(N_DEV, D_FF // N_DEV, D_MODEL), True), (gw_up.reshape(N_DEV, D_FF // N_DEV, D_MODEL), True),
         (gw_down.reshape(N_DEV, D_FF // N_DEV, D_MODEL), True),
         (gw_out.reshape(N_DEV, D_MODEL // N_DEV, D_MODEL), True)])
    flat = lambda t: t.reshape(tokens, t.shape[-1])
    d_a, d_g, d_q, d_k, d_v = flat(d_a3), flat(d_g3), flat(d_q3), flat(d_k3), flat(d_v3)
    gw_in = _grad_matmul_parts([d_a, d_g, d_q, d_k, d_v], [h1], "grad_w_in")
    grad_x2, dg_mix, dmod_m, p_in = _mix_in_bwd(
        d_a, d_g, d_q, d_k, d_v, w_in_t, x2, dx1, mod8, g_mix, seq,
        [(gw_in.reshape(N_DEV, D_IN // N_DEV, D_MODEL), True)])

    dmod = jnp.concatenate([dmod_m[:, 0], dmod_m[:, 1], dmod_f[:, 2], dmod_f[:, 0], dmod_f[:, 1], dgate_f[:, 0]], axis=1)
    dg_q = dg_qk[0:1, 0:HEAD_DIM] + dg_qk[0:1, HEAD_DIM:]
    dg_k = dg_qk[1:2, 0:HEAD_DIM] + dg_qk[1:2, HEAD_DIM:]
    loss_part = (0.5 / D_MODEL) * jnp.sum(sq[0:1, :], axis=1, keepdims=True)
    small = jnp.concatenate(
        [dg_mix[0:1], dg_ffn[0:1], db_dw_p[0:1], dgb_ln[0:1], dgb_ln[1:2],
         _pad_lanes(dg_q, LANES), _pad_lanes(dg_k, LANES), _pad_lanes(loss_part, LANES)], axis=1)
    n_small = small.shape[1] - LANES

    (dmod_g, small_g, dw_g) = _exchange([(dmod, False), (small, False), (dw_dw_p, False)], "gather_small_grads")

    dmod_all = dmod_g.reshape(N_DEV * n_seq, N_MOD * D_MODEL)
    dmod_cols = lax.dynamic_slice(dmod_all, (0, me * ada_cols), (N_DEV * n_seq, ada_cols))
    gw_ada, gb_ada = _ada_bwd(c_all, dmod_cols, dmod_all)

    res = {}
    res["w_ada"] = _adamw(w_ada[0], m_w_ada[0], v_w_ada[0], gw_ada, "adamw_w_ada")
    res["b_ada"] = _adamw(b_ada, m_b_ada, v_b_ada, gb_ada, "adamw_b_ada")
    def adamw_transposed(w, m, v, parts, name, tr):
        outs = _adamw(transposed(w), transposed(m), transposed(v), parts, name, N_DEV, tr=tr)
        return tuple(jnp.transpose(o) for o in outs)

    res["w_in"] = adamw_transposed(w_in, m_w_in, v_w_in, p_in, "adamw_w_in", 160)
    res["w_out"] = _adamw(w_out[0], m_w_out[0], v_w_out[0], p_out, "adamw_w_out", N_DEV)
    res["w_gate"] = adamw_transposed(w_gate, m_w_gate, v_w_gate, p_gate, "adamw_w_gate", 176)
    res["w_up"] = adamw_transposed(w_up, m_w_up, v_w_up, p_up, "adamw_w_up", 176)
    res["w_down"] = _adamw(w_down[0], m_w_down[0], v_w_down[0], p_down, "adamw_w_down", N_DEV, tr=176)
    dw_mine = lax.dynamic_slice(dw_g, (0, 0, me * dw_cols), (N_DEV, CONV_WIDTH, dw_cols))
    res["w_dw"] = _adamw(w_dw[0], m_w_dw[0], v_w_dw[0], dw_mine, "adamw_w_dw", N_DEV)

    small_names = ["g_mix", "g_ffn", "b_dw", "g_conv_ln", "b_conv_ln", "g_q", "g_k"]
    small_w = {"g_mix": (g_mix, m_g_mix, v_g_mix), "g_ffn": (g_ffn, m_g_ffn, v_g_ffn), "b_dw": (b_dw, m_b_dw, v_b_dw),
               "g_conv_ln": (g_conv_ln, m_g_conv_ln, v_g_conv_ln), "b_conv_ln": (b_conv_ln, m_b_conv_ln, v_b_conv_ln),
               "g_q": (g_q, m_g_q, v_g_q), "g_k": (g_k, m_g_k, v_g_k)}
    widths = [max(small_w[n][0].shape[1], LANES) for n in small_names]
    packed = [jnp.concatenate([_pad_lanes(small_w[n][i], wd) for n, wd in zip(small_names, widths)], axis=1) for i in range(3)]
    outs = _adamw(packed[0], packed[1], packed[2], small_g[:, :, :n_small], "adamw_small", N_DEV)
    off = 0
    for n, wd in zip(small_names, widths):
        real = small_w[n][0].shape[1]
        res[n] = tuple(o[:, off:off + real] for o in outs)
        off += wd
    loss = jnp.sum(small_g[:, 0, n_small])

    order = ["w_ada", "b_ada", "g_mix", "w_in", "w_dw", "b_dw", "g_conv_ln", "b_conv_ln", "g_q", "g_k",
             "w_out", "g_ffn", "w_gate", "w_up", "w_down"]
    lead = {"w_ada", "w_in", "w_dw", "w_out", "w_gate", "w_up", "w_down"}
    grads, deltas, new_m, new_v = [], [], [], []
    for n in order:
        g, d, mn, vn = res[n]
        g, d, mn, vn = (t[None] if n in lead else t for t in (g, d, mn, vn))
        grads.append(g)
        deltas.append(d)
        new_m.append(mn)
        new_v.append(vn)
    return (loss, grad_x2.reshape(n_seq, seq, D_MODEL), *grads, *deltas, *new_m, *new_v)
```

```python
import numpy as np
import jax
import jax.numpy as jnp
from jax import lax
from jax.experimental import pallas as pl
from jax.experimental.pallas import tpu as pltpu

F32 = jnp.float32
BF16 = jnp.bfloat16

N_DEV = 8
D_MODEL = 1024
D_CONV = 512
D_ATT = 512
HEAD_DIM = 64
CONV_WIDTH = 31
D_IN = 2 * D_CONV + 3 * D_ATT
D_FF = 2816
N_MOD = 6
EPS = 1e-6
RADIUS = 64
DILATIONS = (1, 4, 16)
Q_BLOCK = 128
LANES = 128
VMEM_LIMIT = 56 * 1024 * 1024

ADAM_LR = 0.001
ADAM_B1 = 0.9
ADAM_B2 = 0.999
ADAM_EPS = 1e-08
ADAM_WD = 0.01
ADAM_STEP = 10

NT = (((1,), (1,)), ((), ()))
TN = (((0,), (0,)), ((), ()))


def _call(body, **kw):
    return pl.pallas_call(body, **kw)


def _params(sem=None, vmem=VMEM_LIMIT):
    return pltpu.CompilerParams(dimension_semantics=sem, vmem_limit_bytes=vmem)


def _sig(x):
    return 1.0 / (1.0 + jnp.exp(-x))


def _sds(shape, dtype):
    return jax.ShapeDtypeStruct(shape, dtype)


N_PEER = N_DEV - 1
ANY_SPEC = pl.BlockSpec(memory_space=pl.ANY)


def _exchange_copies(scatter, ins, outs, *sems):
    n = len(ins)
    if n == 0:
        return [], []
    send_sems, recv_sems, local_sems = sems
    x, y, c = lax.axis_index("x"), lax.axis_index("y"), lax.axis_index("c")
    me = 4 * x + 2 * y + c

    def src(a, slot):
        return ins[a].at[slot] if scatter[a] else ins[a]

    local = [pltpu.make_async_copy(src(a, me), outs[a].at[me], local_sems.at[a]) for a in range(n)]
    flights = []
    for k in range(1, N_DEV):
        px = 1 - x if k & 4 else x
        py = 1 - y if k & 2 else y
        pc = 1 - c if k & 1 else c
        pid = 4 * px + 2 * py + pc
        for a in range(n):
            i = a * N_PEER + k - 1
            send, recv = (pltpu.make_async_remote_copy(
                src_ref=src(a, pid), dst_ref=outs[a].at[slot],
                send_sem=send_sems.at[i], recv_sem=recv_sems.at[i],
                device_id=(px, py, pc), device_id_type=pl.DeviceIdType.MESH) for slot in (me, pid))
            flights.append((send, recv))
    return local, flights


def _exchange_start(*args):
    local, flights = _exchange_copies(*args)
    for cp in local:
        cp.start()
    for send, _ in flights:
        send.start()


def _exchange_wait(*args):
    local, flights = _exchange_copies(*args)
    for send, recv in flights:
        send.wait_send()
        recv.wait_recv()
    for cp in local:
        cp.wait()


def _exchange_shapes(items):
    return [_sds((N_DEV,) + tuple(arr.shape[1:] if scatter else arr.shape), arr.dtype) for arr, scatter in items]


def _exchange_sems(n):
    if n == 0:
        return []
    return [pltpu.SemaphoreType.DMA((n * N_PEER,)), pltpu.SemaphoreType.DMA((n * N_PEER,)),
            pltpu.SemaphoreType.DMA((n,))]


def _gather_by_chip_phase(phase, ins, outs, send_sems, recv_sems, local_sems):
    n = len(ins)
    per = N_PEER
    x, y, c = lax.axis_index("x"), lax.axis_index("y"), lax.axis_index("c")
    me, sibling = (x, y, c), (x, y, 1 - c)
    chips = [(1 - x, y), (x, 1 - y), (1 - x, 1 - y)]

    def slot(px, py, pc):
        return 4 * px + 2 * py + pc

    def copy(a, k, block, to, src=None):
        dst = outs[a].at[slot(*block)]
        return pltpu.make_async_remote_copy(
            src_ref=dst if src is None else src, dst_ref=dst,
            send_sem=send_sems.at[a * per + k], recv_sem=recv_sems.at[a * per + k],
            device_id=to, device_id_type=pl.DeviceIdType.MESH)

    local = [pltpu.make_async_copy(ins[a], outs[a].at[slot(*me)], local_sems.at[a]) for a in range(n)]
    first = []
    for a in range(n):
        first.append(copy(a, 0, me, sibling, src=ins[a]))
        first += [copy(a, 1 + j, me, (*chip, c), src=ins[a]) for j, chip in enumerate(chips)]
    passed = [copy(a, 4 + j, (*chip, c), sibling) for j, chip in enumerate(chips) for a in range(n)]
    if phase == 0:
        for cp in local + first:
            cp.start()
    elif phase == 1:
        for j, chip in enumerate(chips):
            for a in range(n):
                copy(a, 1 + j, (*chip, c), me).wait_recv()
        for cp in passed:
            cp.start()
    else:
        for a in range(n):
            copy(a, 0, sibling, me).wait_recv()
            for j, chip in enumerate(chips):
                copy(a, 4 + j, (*chip, 1 - c), me).wait_recv()
        for cp in first + passed:
            cp.wait_send()
        for cp in local:
            cp.wait()


def _gather_by_chip(arrays, name):
    n = len(arrays)

    def body(*refs):
        for phase in range(3):
            _gather_by_chip_phase(phase, refs[:n], refs[n:2 * n], *refs[2 * n:])

    return _call(
        body, name=name, out_shape=_exchange_shapes([(arr, False) for arr in arrays]),
        in_specs=[ANY_SPEC] * n, out_specs=[ANY_SPEC] * n, scratch_shapes=_exchange_sems(n),
    )(*arrays)


def _exchange(items, name):
    n = len(items)
    scatter = [s for _, s in items]

    def body(*refs):
        args = (scatter, refs[:n], refs[n:2 * n]) + tuple(refs[2 * n:])
        _exchange_start(*args)
        _exchange_wait(*args)

    return _call(
        body, name=name, out_shape=_exchange_shapes(items),
        in_specs=[ANY_SPEC] * n, out_specs=[ANY_SPEC] * n, scratch_shapes=_exchange_sems(n),
    )(*[a for a, _ in items])


def _ada_fwd(c_all, w_ada, b_cols):
    def body(c_ref, w_ref, b_ref, o_ref):
        cv = c_ref[...]
        sc = (cv * _sig(cv)).astype(BF16)
        o_ref[...] = jnp.dot(sc, w_ref[...].astype(BF16), preferred_element_type=F32) + b_ref[...]

    return _call(body, name="ada_fwd", out_shape=_sds((c_all.shape[0], w_ada.shape[1]), F32),
                 compiler_params=_params())(c_all, w_ada, b_cols)


def _ada_bwd(c_all, dmod_cols, dmod_all):
    def body(c_ref, dc_ref, da_ref, gw_ref, gb_ref):
        cv = c_ref[...]
        sc = (cv * _sig(cv)).astype(BF16)
        gw_ref[...] = lax.dot_general(sc, dc_ref[...].astype(BF16), TN, preferred_element_type=F32)
        gb_ref[...] = jnp.sum(da_ref[...], axis=0, keepdims=True)

    return _call(body, name="ada_bwd",
                 out_shape=[_sds((c_all.shape[1], dmod_cols.shape[1]), F32), _sds((1, dmod_all.shape[1]), F32)],
                 compiler_params=_params())(c_all, dmod_cols, dmod_all)


MIX_ROWS = 128


def _mix_in(x2, mod8, g_mix, w_in, seq, tm=512):
    tokens = x2.shape[0]
    per_seq = seq // tm

    def body(x_ref, m_ref, g_ref, wt_ref, h_ref, p_ref, w_ref):
        @pl.when(pl.program_id(0) == 0)
        def _():
            w_ref[...] = wt_ref[...].T

        def normed(c):
            rows = pl.ds(c * MIX_ROWS, MIX_ROWS)
            xv = x_ref[rows, :]
            r = lax.rsqrt(jnp.mean(xv * xv, axis=-1, keepdims=True) + EPS)
            hb = ((xv * r * g_ref[...]) * (1.0 + m_ref[1:2, :]) + m_ref[0:1, :]).astype(BF16)
            h_ref[rows, :] = hb
            return hb

        ahead = normed(0)
        for c in range(tm // MIX_ROWS):
            hb = ahead
            if c + 1 < tm // MIX_ROWS:
                ahead = normed(c + 1)
            p = jnp.dot(hb, w_ref[...], preferred_element_type=F32)
            for cb in range(D_IN // LANES):
                p_ref[cb, pl.ds(c * MIX_ROWS, MIX_ROWS), :] = p[:, cb * LANES:(cb + 1) * LANES]

    return _call(
        body, name="mix_in", grid=(tokens // tm,),
        in_specs=[pl.BlockSpec((tm, D_MODEL), lambda i: (i, 0)),
                  pl.BlockSpec((None, 8, D_MODEL), lambda i: (i // per_seq, 0, 0)),
                  pl.BlockSpec((1, D_MODEL), lambda i: (0, 0)),
                  pl.BlockSpec((D_IN, D_MODEL), lambda i: (0, 0))],
        out_specs=[pl.BlockSpec((tm, D_MODEL), lambda i: (i, 0)),
                   pl.BlockSpec((D_IN // LANES, tm, LANES), lambda i: (0, i, 0))],
        out_shape=[_sds((tokens, D_MODEL), BF16), _sds((D_IN // LANES, tokens, LANES), F32)],
        scratch_shapes=[pltpu.VMEM((D_MODEL, D_IN), BF16)],
        compiler_params=_params(("arbitrary",)),
    )(x2, mod8, g_mix, w_in)


CONV_ROWS = 128
CONV_DW_ROWS = 32
CONV_DW_UNROLL = 8
CONV_HALO = 16


def _fill_shifted(xp, sh, seq):
    for b in range(8):
        sh[b, pl.ds(0, seq + 24), :] = xp[pl.ds(b, seq + 24), :]


def _conv_fwd(proj3, w_dw, b_dw):
    _, n_seq, seq, _ = proj3.shape
    n_cb = D_CONV // LANES

    def body(a_ref, g_ref, w_ref, b_ref, uc_ref, xp, sh):
        zeros = jnp.zeros((CONV_HALO, LANES), F32)
        xp[pl.ds(0, CONV_HALO), :] = zeros
        xp[pl.ds(CONV_HALO + seq, CONV_HALO), :] = zeros
        xp[pl.ds(CONV_HALO, seq), :] = a_ref[...] * _sig(g_ref[...])
        _fill_shifted(xp, sh, seq)

        def blk(i, carry):
            t0 = pl.multiple_of(i * CONV_ROWS, CONV_ROWS)
            acc = jnp.zeros((CONV_ROWS, LANES), F32)
            for j in range(CONV_WIDTH):
                jj = j + 1
                acc = acc + sh[jj % 8, pl.ds(t0 + 8 * (jj // 8), CONV_ROWS), :] * w_ref[j:j + 1, :]
            uc_ref[pl.ds(t0, CONV_ROWS), :] = acc + b_ref[...]
            return carry

        lax.fori_loop(0, seq // CONV_ROWS, blk, 0)

    return _call(
        body, name="conv_fwd", grid=(n_seq, n_cb),
        in_specs=[pl.BlockSpec((None, None, seq, LANES), lambda b, cb: (cb, b, 0, 0)),
                  pl.BlockSpec((None, None, seq, LANES), lambda b, cb: (n_cb + cb, b, 0, 0)),
                  pl.BlockSpec((CONV_WIDTH, LANES), lambda b, cb: (0, cb)),
                  pl.BlockSpec((1, LANES), lambda b, cb: (0, cb))],
        out_specs=pl.BlockSpec((None, seq, LANES), lambda b, cb: (b, 0, cb)),
        out_shape=_sds((n_seq, seq, D_CONV), F32),
        scratch_shapes=[pltpu.VMEM((seq + 2 * CONV_HALO, LANES), F32),
                        pltpu.VMEM((8, seq + 2 * CONV_HALO, LANES), F32)],
        compiler_params=_params(("parallel", "parallel")),
    )(proj3, proj3, w_dw, b_dw)


def _conv_bwd(duc3, proj3, w_dw):
    _, n_seq, seq, _ = proj3.shape
    n_cb = D_CONV // LANES

    def body(duc_ref, a_ref, g_ref, w_ref, da_ref, dg_ref, dw_ref, db_ref, xp, sh):
        @pl.when(pl.program_id(1) == 0)
        def _():
            dw_ref[...] = jnp.zeros_like(dw_ref)
            db_ref[...] = jnp.zeros_like(db_ref)

        zeros = jnp.zeros((CONV_HALO, LANES), F32)
        xp[pl.ds(0, CONV_HALO), :] = zeros
        xp[pl.ds(CONV_HALO + seq, CONV_HALO), :] = zeros
        xp[pl.ds(CONV_HALO, seq), :] = a_ref[...] * _sig(g_ref[...])
        _fill_shifted(xp, sh, seq)
        for j0 in range(0, CONV_WIDTH, 8):
            taps = range(j0, min(j0 + 8, CONV_WIDTH))

            def wblk(i, accs, taps=taps):
                for u in range(CONV_DW_UNROLL):
                    t0 = pl.multiple_of((i * CONV_DW_UNROLL + u) * CONV_DW_ROWS, CONV_DW_ROWS)
                    d = duc_ref[pl.ds(t0, CONV_DW_ROWS), :]
                    accs = tuple(acc + d * sh[(j + 1) % 8, pl.ds(t0 + 8 * ((j + 1) // 8), CONV_DW_ROWS), :]
                                 for acc, j in zip(accs, taps))
                return accs

            accs = lax.fori_loop(0, seq // (CONV_DW_ROWS * CONV_DW_UNROLL), wblk,
                                 tuple(jnp.zeros((CONV_DW_ROWS, LANES), F32) for _ in taps))
            for acc, j in zip(accs, taps):
                dw_ref[j:j + 1, :] += jnp.sum(acc, axis=0, keepdims=True)
        db_ref[0:1, :] += jnp.sum(duc_ref[...], axis=0, keepdims=True)
        xp[pl.ds(CONV_HALO, seq), :] = duc_ref[...]
        _fill_shifted(xp, sh, seq)

        def ublk(i, carry):
            t0 = pl.multiple_of(i * CONV_ROWS, CONV_ROWS)
            acc = jnp.zeros((CONV_ROWS, LANES), F32)
            for j in range(CONV_WIDTH):
                jj = CONV_WIDTH - j
                acc = acc + sh[jj % 8, pl.ds(t0 + 8 * (jj // 8), CONV_ROWS), :] * w_ref[j:j + 1, :]
            av = a_ref[pl.ds(t0, CONV_ROWS), :]
            sg = _sig(g_ref[pl.ds(t0, CONV_ROWS), :])
            da_ref[pl.ds(t0, CONV_ROWS), :] = (acc * sg).astype(BF16)
            dg_ref[pl.ds(t0, CONV_ROWS), :] = (acc * av * sg * (1.0 - sg)).astype(BF16)
            return carry

        lax.fori_loop(0, seq // CONV_ROWS, ublk, 0)

    return _call(
        body, name="conv_bwd", grid=(n_cb, n_seq),
        in_specs=[pl.BlockSpec((None, seq, LANES), lambda cb, b: (b, 0, cb)),
                  pl.BlockSpec((None, None, seq, LANES), lambda cb, b: (cb, b, 0, 0)),
                  pl.BlockSpec((None, None, seq, LANES), lambda cb, b: (n_cb + cb, b, 0, 0)),
                  pl.BlockSpec((CONV_WIDTH, LANES), lambda cb, b: (0, cb))],
        out_specs=[pl.BlockSpec((None, seq, LANES), lambda cb, b: (b, 0, cb)),
                   pl.BlockSpec((None, seq, LANES), lambda cb, b: (b, 0, cb)),
                   pl.BlockSpec((32, LANES), lambda cb, b: (0, cb)),
                   pl.BlockSpec((8, LANES), lambda cb, b: (0, cb))],
        out_shape=[_sds((n_seq, seq, D_CONV), BF16), _sds((n_seq, seq, D_CONV), BF16),
                   _sds((32, D_CONV), F32), _sds((8, D_CONV), F32)],
        scratch_shapes=[pltpu.VMEM((seq + 2 * CONV_HALO, LANES), F32),
                        pltpu.VMEM((8, seq + 2 * CONV_HALO, LANES), F32)],
        compiler_params=_params(("parallel", "arbitrary")),
    )(duc3, proj3, proj3, w_dw)


MASKED = 1e30
ATT_ROWS = 1024
ATT_BWD_ROWS = 2048
ATT_UNROLL = 8
ATT_FWD_UNROLL = 8


def _distance_mats(dil, seg_len):
    kw = min(2 * Q_BLOCK, seg_len)
    offsets = (0, -RADIUS, -2 * RADIUS) if kw == 2 * Q_BLOCK else (0,)
    a = np.arange(Q_BLOCK)[:, None]
    b = np.arange(kw)[None, :]
    mats = []
    for off in offsets:
        rel = np.abs(b + off - a)
        mats.append(np.where(rel <= RADIUS, dil * rel, MASKED))
    return jnp.asarray(np.stack(mats).astype(np.float32))


def _alibi_rows():
    s = np.zeros((4, 8, LANES), np.float32)
    for hp in range(4):
        for hl in range(2):
            s[hp, hl, :] = 2.0 ** (-(2 * hp + hl + 1))
    return jnp.asarray(s)


def _window(n, seg_len):
    i0 = pl.multiple_of(n * Q_BLOCK, Q_BLOCK)
    if seg_len <= Q_BLOCK:
        return i0, i0, 0
    per_seg = seg_len // Q_BLOCK
    j = n % per_seg
    seg0 = (n // per_seg) * seg_len
    ks_local = jnp.clip(j * Q_BLOCK - RADIUS, 0, seg_len - 2 * Q_BLOCK)
    ks = pl.multiple_of(seg0 + ks_local, RADIUS)
    var = jnp.where(j == 0, 0, jnp.where(j == per_seg - 1, 2, 1))
    return i0, ks, var


def _first_head(rows):
    return lax.broadcasted_iota(jnp.int32, (rows, LANES), 1) < HEAD_DIM


def _same_head():
    head = np.arange(LANES) // HEAD_DIM
    return jnp.asarray((head[:, None] == head[None, :]).astype(np.float32)).astype(BF16)


def _head_sum(x, same_ref):
    return jnp.dot(x.astype(BF16), same_ref[...], preferred_element_type=F32)


def _head_mean(x, same_ref):
    return _head_sum(x, same_ref) * (1.0 / HEAD_DIM)


def _per_head(x, first):
    swapped = pltpu.roll(x, HEAD_DIM, 1)
    return jnp.where(first, x, swapped), jnp.where(first, swapped, x)


STRIDE = 4


def _gather_segments(src, dil, seq, tmp, put):
    if dil == 1:
        put(0, seq, src[pl.ds(0, seq), :])
    elif dil == STRIDE:
        seg = seq // dil
        for r in range(dil):
            put(r * seg, seg, src[pl.ds(r, seg, stride=dil), :])
    else:
        part, seg = seq // STRIDE, seq // dil
        for b in range(STRIDE):
            tmp[pl.ds(b * part, part), :] = src[pl.ds(b, part, stride=STRIDE), :]
        for b in range(STRIDE):
            for a in range(dil // STRIDE):
                put(b * part + a * seg, seg, tmp[pl.ds(b * part + a, seg, stride=dil // STRIDE), :])


def _scatter_segments(dst, get, dil, seq, tmp, accumulate):
    def write(rows, val):
        if accumulate:
            dst[rows, :] += val
        else:
            dst[rows, :] = val

    if dil == 1:
        write(pl.ds(0, seq), get(0, seq))
    elif dil == STRIDE:
        seg = seq // dil
        for r in range(dil):
            write(pl.ds(r, seg, stride=dil), get(r * seg, seg))
    else:
        part, seg = seq // STRIDE, seq // dil
        for b in range(STRIDE):
            for a in range(dil // STRIDE):
                tmp[pl.ds(b * part + a, seg, stride=dil // STRIDE), :] = get(b * part + a * seg, seg)
        for b in range(STRIDE):
            write(pl.ds(b, part, stride=STRIDE), tmp[pl.ds(b * part, part), :])


def _permute_rows(dst, src, dil, seq, tmp):
    def put(start, size, val):
        dst[pl.ds(start, size), :] = val.astype(dst.dtype)

    _gather_segments(src, dil, seq, tmp, put)


def _permute_rows_by_head(dst, src, dil, seq, tmp):
    def put(start, size, val):
        first = _first_head(size)
        dst[0, pl.ds(start, size), :] = jnp.where(first, val, 0.0).astype(dst.dtype)
        dst[1, pl.ds(start, size), :] = jnp.where(first, 0.0, val).astype(dst.dtype)

    _gather_segments(src, dil, seq, tmp, put)


def _qk_normalise(q_ref, g2_ref, same_ref, dst, seq, scale, step):
    def chunk(ci, carry):
        rows = pl.ds(pl.multiple_of(ci * step, step), step)
        qv = q_ref[rows, :]
        r = lax.rsqrt(_head_mean(qv * qv, same_ref) + EPS)
        dst[rows, :] = qv * r * (g2_ref[...] * scale)
        return carry

    lax.fori_loop(0, seq // step, chunk, 0)


def _attn_fwd(proj3, g_q2, g_k2, ride):
    _, n_seq, seq, _ = proj3.shape
    dms = [_distance_mats(d, seq // d) for d in DILATIONS]
    same = _same_head()
    col0 = 2 * D_CONV // LANES
    n_hp = D_ATT // LANES

    n_ride = len(ride)
    assert not any(scatter for _, scatter in ride), "the forward's ride is an all-gather"

    def body(*refs):
        q_ref, k_ref, v_ref, gq_ref, gk_ref, sl_ref, dm1, dm4, dm16, same_ref = refs[:10]
        ride_in = refs[10:10 + n_ride]
        y_ref, lse_ref = refs[10 + n_ride:12 + n_ride]
        ride_out = refs[12 + n_ride:12 + 2 * n_ride]
        (qf, kf, qp, kp, vp, oml_p, o1, o4, o16, m1, m4, m16, l1, l4, l16,
         tmp) = refs[12 + 2 * n_ride:28 + 2 * n_ride]
        o_nat, m_nat, l_nat = (o1, o4, o16), (m1, m4, m16), (l1, l4, l16)
        ride_args = (ride_in, ride_out) + tuple(refs[28 + 2 * n_ride:])
        step = pl.program_id(0) * n_hp + pl.program_id(1)
        n_steps = n_seq * n_hp

        if n_ride:
            for phase, at in enumerate((0, (3 * n_steps) // 4)):
                @pl.when(step == at)
                def _(phase=phase):
                    _gather_by_chip_phase(phase, *ride_args)

        dm_refs = (dm1, dm4, dm16)
        _qk_normalise(q_ref, gq_ref, same_ref, qf, seq, HEAD_DIM ** -0.5, ATT_ROWS)
        _qk_normalise(k_ref, gk_ref, same_ref, kf, seq, 1.0, ATT_ROWS)
        slopes = (sl_ref[0:1, 0:1], sl_ref[1:2, 0:1])
        for pi, dil in enumerate(DILATIONS):
            seg = seq // dil
            kw = min(2 * Q_BLOCK, seg)
            _permute_rows_by_head(qp, qf, dil, seq, tmp)
            _permute_rows(kp, kf, dil, seq, tmp)
            _permute_rows(vp, v_ref, dil, seq, tmp)

            def blk(it, carry, seg=seg, kw=kw, pi=pi, dst=oml_p):
                first = _first_head(Q_BLOCK)
                chains = [(sub, h) for sub in range(ATT_FWD_UNROLL) for h in range(2)]
                win = [_window(it * ATT_FWD_UNROLL + sub, seg) for sub in range(ATT_FWD_UNROLL)]
                s = {}
                for sub, h in chains:
                    i0, ks, var = win[sub]
                    s[sub, h] = lax.dot_general(qp[h, pl.ds(i0, Q_BLOCK), :], kp[pl.ds(ks, kw), :], NT,
                                                preferred_element_type=F32) - slopes[h] * dm_refs[pi][var]
                m, l, p = {}, {}, {}
                for c in chains:
                    m[c] = jnp.max(s[c], axis=1, keepdims=True)
                    e = jnp.exp(s[c] - m[c])
                    l[c] = jnp.sum(e, axis=1, keepdims=True)
                    p[c] = e.astype(BF16)
                o = {}
                for sub, h in chains:
                    o[sub, h] = jnp.dot(p[sub, h], vp[pl.ds(win[sub][1], kw), :], preferred_element_type=F32)
                packed = [jnp.concatenate([jnp.where(first, t[sub, 0], t[sub, 1]) for t in (o, m, l)], axis=1)
                          for sub in range(ATT_FWD_UNROLL)]
                span = ATT_FWD_UNROLL * Q_BLOCK
                dst[pl.ds(pl.multiple_of(it * span, span), span), :] = jnp.concatenate(packed, axis=0)
                return carry

            lax.fori_loop(0, seq // (Q_BLOCK * ATT_FWD_UNROLL), blk, 0)
            for n, nat in enumerate((o_nat[pi], m_nat[pi], l_nat[pi])):
                _scatter_segments(nat, lambda start, size, n=n: oml_p[pl.ds(start, size), pl.ds(n * LANES, LANES)],
                                  dil, seq, tmp, accumulate=False)

        def merge(ci, carry):
            rows = pl.ds(pl.multiple_of(ci * ATT_ROWS, ATT_ROWS), ATT_ROWS)
            ms = [m_nat[pi][rows, :] for pi in range(3)]
            m_all = jnp.maximum(jnp.maximum(ms[0], ms[1]), ms[2])
            es = [jnp.exp(m - m_all) for m in ms]
            l_all = sum(l_nat[pi][rows, :] * es[pi] for pi in range(3))
            inv = 1.0 / l_all
            o = sum(o_nat[pi][rows, :] * (es[pi] * inv) for pi in range(3))
            y_ref[rows, :] = o.astype(BF16)
            lse_ref[rows, :] = m_all + jnp.log(l_all)
            return carry

        lax.fori_loop(0, seq // ATT_ROWS, merge, 0)

        if n_ride:
            @pl.when(step == n_steps - 1)
            def _():
                _gather_by_chip_phase(2, *ride_args)

    def col(off):
        return pl.BlockSpec((None, None, seq, LANES), lambda b, hp: (col0 + off * n_hp + hp, b, 0, 0))

    def whole(arr):
        return pl.BlockSpec(arr.shape, lambda b, hp: (0,) * arr.ndim)

    rows_f32 = pltpu.VMEM((seq, LANES), F32)
    rows_bf16 = pltpu.VMEM((seq, LANES), BF16)
    return _call(
        body, name="attn_fwd", grid=(n_seq, n_hp),
        in_specs=[col(0), col(1), col(2), whole(g_q2), whole(g_k2),
                  pl.BlockSpec((None, 8, LANES), lambda b, hp: (hp, 0, 0)),
                  whole(dms[0]), whole(dms[1]), whole(dms[2]), whole(same)] + [ANY_SPEC] * n_ride,
        out_specs=[pl.BlockSpec((None, seq, LANES), lambda b, hp: (b, 0, hp)),
                   pl.BlockSpec((None, seq, LANES), lambda b, hp: (b, 0, hp))] + [ANY_SPEC] * n_ride,
        out_shape=[_sds((n_seq, seq, D_ATT), BF16), _sds((n_seq, seq, D_ATT), F32)] + _exchange_shapes(ride),
        scratch_shapes=[rows_f32, rows_f32, pltpu.VMEM((2, seq, LANES), BF16), rows_bf16, rows_bf16]
        + [pltpu.VMEM((seq, 3 * LANES), F32)] + [rows_f32] * 10 + _exchange_sems(n_ride),
        compiler_params=_params(("arbitrary", "arbitrary")),
    )(proj3, proj3, proj3, g_q2, g_k2, _alibi_rows(), *dms, same, *[a for a, _ in ride])


def _attn_bwd(proj3, do3, y_att3, lse3, g_q2, g_k2, ride):
    _, n_seq, seq, _ = proj3.shape
    dms = [_distance_mats(d, seq // d) for d in DILATIONS]
    same = _same_head()
    col0 = 2 * D_CONV // LANES
    n_hp = D_ATT // LANES

    n_ride = len(ride)
    ride_scatter = [s for _, s in ride]

    def body(*refs):
        (q_ref, k_ref, v_ref, do_ref, o_ref, lse_ref, gq_ref, gk_ref, sl_ref, dm1, dm4, dm16,
         same_ref) = refs[:13]
        ride_in = refs[13:13 + n_ride]
        dq_ref, dk_ref, dv_ref, dg_ref = refs[13 + n_ride:17 + n_ride]
        ride_out = refs[17 + n_ride:17 + 2 * n_ride]
        (qf, kf, qp, dop, kp, vp, sn, sp, dqp, dkp, dvp, dqn, dkn, dvn,
         tmp) = refs[17 + 2 * n_ride:32 + 2 * n_ride]
        ride_args = (ride_scatter, ride_in, ride_out) + tuple(refs[32 + 2 * n_ride:])
        dm_refs = (dm1, dm4, dm16)
        step = pl.program_id(0) * n_hp + pl.program_id(1)

        @pl.when(step == 0)
        def _():
            _exchange_start(*ride_args)
            dg_ref[...] = jnp.zeros_like(dg_ref)

        _qk_normalise(q_ref, gq_ref, same_ref, qf, seq, HEAD_DIM ** -0.5, ATT_BWD_ROWS)
        _qk_normalise(k_ref, gk_ref, same_ref, kf, seq, 1.0, ATT_BWD_ROWS)

        def stats(ci, carry):
            rows = pl.ds(pl.multiple_of(ci * ATT_BWD_ROWS, ATT_BWD_ROWS), ATT_BWD_ROWS)
            first = _first_head(ATT_BWD_ROWS)
            sn[0, rows, :], sn[1, rows, :] = _per_head(lse_ref[rows, :], first)
            prod = do_ref[rows, :] * o_ref[rows, :].astype(F32)
            sn[2, rows, :], sn[3, rows, :] = _per_head(_head_sum(prod, same_ref), first)
            return carry

        lax.fori_loop(0, seq // ATT_BWD_ROWS, stats, 0)
        slopes = (sl_ref[0:1, 0:1], sl_ref[1:2, 0:1])
        half = seq // (Q_BLOCK * ATT_UNROLL)
        region = seq // ATT_UNROLL

        for pi, dil in enumerate(DILATIONS):
            seg = seq // dil
            kw = min(2 * Q_BLOCK, seg)
            _permute_rows_by_head(qp, qf, dil, seq, tmp)
            _permute_rows_by_head(dop, do_ref, dil, seq, tmp)
            _permute_rows(kp, kf, dil, seq, tmp)
            _permute_rows(vp, v_ref, dil, seq, tmp)
            if dil == 1:
                st = sn
            else:
                st = sp
                for n in range(4):
                    _permute_rows(sp.at[n], sn.at[n], dil, seq, tmp)
            def touched(sub, seg=seg):
                lo, hi = sub * region, (sub + 1) * region
                if seg < region:
                    return lo, hi
                seg0 = lo // seg * seg
                return max(lo - RADIUS, seg0), min(hi + RADIUS, seg0 + seg)

            def summed(acc, start, size, touched=touched):
                pieces = []
                for c0 in range(start, start + size, RADIUS):
                    owners = [s for s in range(ATT_UNROLL) if touched(s)[0] <= c0 and c0 + RADIUS <= touched(s)[1]]
                    if pieces and pieces[-1][2] == owners:
                        pieces[-1][1] += RADIUS
                    else:
                        pieces.append([c0, RADIUS, owners])
                vals = [sum(acc[o, pl.ds(c0, n), :] for o in owners) for c0, n, owners in pieces]
                return vals[0] if len(vals) == 1 else jnp.concatenate(vals, axis=0)

            for sub in range(ATT_UNROLL):
                lo, hi = touched(sub)
                dkp[sub, pl.ds(lo, hi - lo), :] = jnp.zeros((hi - lo, LANES), F32)
                dvp[sub, pl.ds(lo, hi - lo), :] = jnp.zeros((hi - lo, LANES), F32)

            def blk(it, carry, seg=seg, kw=kw, pi=pi, st=st):
                first = _first_head(Q_BLOCK)
                chains = [(sub, h) for sub in range(ATT_UNROLL) for h in range(2)]
                win = [_window(it + sub * half, seg) for sub in range(ATT_UNROLL)]
                qrows = [pl.ds(w[0], Q_BLOCK) for w in win]
                krows = [pl.ds(w[1], kw) for w in win]

                def over_keys(n, sub):
                    t = st[n, qrows[sub], :]
                    return t if kw == LANES else jnp.concatenate([t] * (kw // LANES), axis=1)

                s, dp = {}, {}
                for sub, h in chains:
                    s[sub, h] = lax.dot_general(qp[h, qrows[sub], :], kp[krows[sub], :], NT,
                                                preferred_element_type=F32) - slopes[h] * dm_refs[pi][win[sub][2]]
                    dp[sub, h] = lax.dot_general(dop[h, qrows[sub], :], vp[krows[sub], :], NT,
                                                 preferred_element_type=F32)
                p, ds = {}, {}
                for sub, h in chains:
                    e = jnp.exp(s[sub, h] - over_keys(h, sub))
                    ds[sub, h] = (e * (dp[sub, h] - over_keys(2 + h, sub))).astype(BF16)
                    p[sub, h] = e.astype(BF16)
                dq, dk, dv = {}, {}, {}
                for sub, h in chains:
                    dq[sub, h] = jnp.dot(ds[sub, h], kp[krows[sub], :], preferred_element_type=F32)
                    dk[sub, h] = lax.dot_general(ds[sub, h], qp[h, qrows[sub], :], TN, preferred_element_type=F32)
                    dv[sub, h] = lax.dot_general(p[sub, h], dop[h, qrows[sub], :], TN, preferred_element_type=F32)
                for sub in range(ATT_UNROLL):
                    dqp[qrows[sub], :] = jnp.where(first, dq[sub, 0], dq[sub, 1])
                    dkp[sub, krows[sub], :] += dk[sub, 0] + dk[sub, 1]
                    dvp[sub, krows[sub], :] += dv[sub, 0] + dv[sub, 1]
                return carry

            lax.fori_loop(0, half, blk, 0)
            first_pattern = pi == 0
            if first_pattern:
                for r0 in range(0, seq, region):
                    rows = pl.ds(r0, region)
                    dqn[rows, :] = dqp[rows, :]
                    dkn[rows, :] = summed(dkp, r0, region)
                    dvn[rows, :] = summed(dvp, r0, region)
            else:
                _scatter_segments(dqn, lambda start, size: dqp[pl.ds(start, size), :], dil, seq, tmp, accumulate=True)
                for nat, acc in ((dkn, dkp), (dvn, dvp)):
                    _scatter_segments(nat, lambda start, size, acc=acc: summed(acc, start, size),
                                      dil, seq, tmp, accumulate=True)

        def finish(ci, carry):
            rows = pl.ds(pl.multiple_of(ci * ATT_BWD_ROWS, ATT_BWD_ROWS), ATT_BWD_ROWS)
            for src_ref, g_ref, dn, dst_ref, scale, row in (
                    (q_ref, gq_ref, dqn, dq_ref, HEAD_DIM ** -0.5, 0), (k_ref, gk_ref, dkn, dk_ref, 1.0, 1)):
                xv = src_ref[rows, :]
                r = lax.rsqrt(_head_mean(xv * xv, same_ref) + EPS)
                xhat = xv * r
                d = dn[rows, :] * scale
                dg_ref[row:row + 1, :] += jnp.sum(d * xhat, axis=0, keepdims=True)
                dxh = d * g_ref[...]
                dst_ref[rows, :] = (r * (dxh - xhat * _head_mean(dxh * xhat, same_ref))).astype(BF16)
            dv_ref[rows, :] = dvn[rows, :].astype(BF16)
            return carry

        lax.fori_loop(0, seq // ATT_BWD_ROWS, finish, 0)

        @pl.when(step == n_seq * n_hp - 1)
        def _():
            _exchange_wait(*ride_args)

    def col(off):
        return pl.BlockSpec((None, None, seq, LANES), lambda b, hp: (col0 + off * n_hp + hp, b, 0, 0))

    def whole(arr):
        return pl.BlockSpec(arr.shape, lambda b, hp: (0,) * arr.ndim)

    att = pl.BlockSpec((None, seq, LANES), lambda b, hp: (b, 0, hp))
    rows_f32 = pltpu.VMEM((seq, LANES), F32)
    rows_bf16 = pltpu.VMEM((seq, LANES), BF16)
    by_head_bf16 = pltpu.VMEM((2, seq, LANES), BF16)
    per_sub_f32 = pltpu.VMEM((ATT_UNROLL, seq, LANES), F32)
    stats_f32 = pltpu.VMEM((4, seq, LANES), F32)
    return _call(
        body, name="attn_bwd", grid=(n_seq, n_hp),
        in_specs=[col(0), col(1), col(2), att, att, att, whole(g_q2), whole(g_k2),
                  pl.BlockSpec((None, 8, LANES), lambda b, hp: (hp, 0, 0)),
                  whole(dms[0]), whole(dms[1]), whole(dms[2]), whole(same)] + [ANY_SPEC] * n_ride,
        out_specs=[att, att, att, pl.BlockSpec((8, LANES), lambda b, hp: (0, 0))] + [ANY_SPEC] * n_ride,
        out_shape=[_sds((n_seq, seq, D_ATT), BF16)] * 3 + [_sds((8, LANES), F32)] + _exchange_shapes(ride),
        scratch_shapes=[rows_f32, rows_f32, by_head_bf16, by_head_bf16, rows_bf16, rows_bf16, stats_f32, stats_f32,
                        rows_f32, per_sub_f32, per_sub_f32, rows_f32, rows_f32, rows_f32, rows_f32]
        + _exchange_sems(n_ride),
        compiler_params=_params(("arbitrary", "arbitrary")),
    )(proj3, proj3, proj3, do3, y_att3, lse3, g_q2, g_k2, _alibi_rows(), *dms, same, *[a for a, _ in ride])


def _mix_out(uc2, y_att2, x2, mod8, g_ln, b_ln, g_ffn, w_out, seq, tm=512):
    tokens = x2.shape[0]
    per_seq = seq // tm

    def body(uc_ref, ya_ref, x_ref, m_ref, gl_ref, bl_ref, gf_ref, w_ref, yc_ref, mix_ref, x1_ref, h2_ref):
        uc = uc_ref[...]
        mu = jnp.mean(uc, axis=-1, keepdims=True)
        cen = uc - mu
        rs = lax.rsqrt(jnp.mean(cen * cen, axis=-1, keepdims=True) + EPS)
        z = cen * rs * gl_ref[...] + bl_ref[...]
        yc = (z * _sig(z)).astype(BF16)
        yc_ref[...] = yc
        mix = (jnp.dot(yc, w_ref[pl.ds(0, D_CONV), :], preferred_element_type=F32)
               + jnp.dot(ya_ref[...], w_ref[pl.ds(D_CONV, D_ATT), :], preferred_element_type=F32))
        mix_ref[...] = mix.astype(BF16)
        x1 = x_ref[...] + m_ref[2:3, :] * mix
        x1_ref[...] = x1
        r = lax.rsqrt(jnp.mean(x1 * x1, axis=-1, keepdims=True) + EPS)
        h2_ref[...] = ((x1 * r * gf_ref[...]) * (1.0 + m_ref[4:5, :]) + m_ref[3:4, :]).astype(BF16)

    def rows(width):
        return pl.BlockSpec((tm, width), lambda i: (i, 0))

    def vec(width):
        return pl.BlockSpec((1, width), lambda i: (0, 0))

    return _call(
        body, name="mix_out", grid=(tokens // tm,),
        in_specs=[rows(D_CONV), rows(D_ATT), rows(D_MODEL),
                  pl.BlockSpec((None, 8, D_MODEL), lambda i: (i // per_seq, 0, 0)),
                  vec(D_CONV), vec(D_CONV), vec(D_MODEL),
                  pl.BlockSpec((D_MODEL, D_MODEL), lambda i: (0, 0))],
        out_specs=[rows(D_CONV), rows(D_MODEL), rows(D_MODEL), rows(D_MODEL)],
        out_shape=[_sds((tokens, D_CONV), BF16), _sds((tokens, D_MODEL), BF16),
                   _sds((tokens, D_MODEL), F32), _sds((tokens, D_MODEL), BF16)],
        compiler_params=_params(("parallel",)),
    )(uc2, y_att2, x2, mod8, g_ln, b_ln, g_ffn, w_out)


def _mix_out_bwd(dmix, uc2, g_ln, b_ln, w_out, tm=512):
    tokens = dmix.shape[0]

    def body(dm_ref, uc_ref, gl_ref, bl_ref, w_ref, duc_ref, do_ref, dgb_ref):
        @pl.when(pl.program_id(0) == 0)
        def _():
            dgb_ref[...] = jnp.zeros_like(dgb_ref)

        dmv = dm_ref[...]
        dyc = lax.dot_general(dmv, w_ref[pl.ds(0, D_CONV), :], NT, preferred_element_type=F32)
        do_ref[...] = lax.dot_general(dmv, w_ref[pl.ds(D_CONV, D_ATT), :], NT, preferred_element_type=F32)
        uc = uc_ref[...]
        mu = jnp.mean(uc, axis=-1, keepdims=True)
        cen = uc - mu
        rs = lax.rsqrt(jnp.mean(cen * cen, axis=-1, keepdims=True) + EPS)
        xh = cen * rs
        z = xh * gl_ref[...] + bl_ref[...]
        sg = _sig(z)
        dz = dyc * (sg * (1.0 + z * (1.0 - sg)))
        dgb_ref[0:1, :] += jnp.sum(dz * xh, axis=0, keepdims=True)
        dgb_ref[1:2, :] += jnp.sum(dz, axis=0, keepdims=True)
        dxh = dz * gl_ref[...]
        duc_ref[...] = rs * (dxh - jnp.mean(dxh, axis=-1, keepdims=True)
                             - xh * jnp.mean(dxh * xh, axis=-1, keepdims=True))

    return _call(
        body, name="mix_out_bwd", grid=(tokens // tm,),
        in_specs=[pl.BlockSpec((tm, D_MODEL), lambda i: (i, 0)),
                  pl.BlockSpec((tm, D_CONV), lambda i: (i, 0)),
                  pl.BlockSpec((1, D_CONV), lambda i: (0, 0)),
                  pl.BlockSpec((1, D_CONV), lambda i: (0, 0)),
                  pl.BlockSpec((D_MODEL, D_MODEL), lambda i: (0, 0))],
        out_specs=[pl.BlockSpec((tm, D_CONV), lambda i: (i, 0)),
                   pl.BlockSpec((tm, D_ATT), lambda i: (i, 0)),
                   pl.BlockSpec((8, D_CONV), lambda i: (0, 0))],
        out_shape=[_sds((tokens, D_CONV), F32), _sds((tokens, D_ATT), F32), _sds((8, D_CONV), F32)],
        compiler_params=_params(("arbitrary",)),
    )(dmix, uc2, g_ln, b_ln, w_out)


FF_TILE = 256
FF_TILES = D_FF // FF_TILE


def _load_once(hbm_refs, vmem_refs, sems):
    @pl.when(pl.program_id(0) == 0)
    def _():
        copies = [pltpu.make_async_copy(src, dst, sems.at[n]) for n, (src, dst) in enumerate(zip(hbm_refs, vmem_refs))]
        for cp in copies:
            cp.start()
        for cp in copies:
            cp.wait()


def _ffn_fwd(h2, w_gate_t, w_up_t, w_down, x1, target, mod8, seq, tm=512):
    tokens = h2.shape[0]
    per_seq = seq // tm
    n_seq = tokens // seq

    def body(h_ref, m_ref, x1_ref, t_ref, wg_hbm, wu_hbm, wd_hbm, gate_ref, up_ref, dy_ref, df_ref, sq_ref, dgf_ref,
             wg, wu, wd, w_sems):
        i = pl.program_id(0)
        _load_once((wg_hbm, wu_hbm, wd_hbm), (wg, wu, wd), w_sems)

        @pl.when(i == 0)
        def _():
            sq_ref[...] = jnp.zeros_like(sq_ref)

        @pl.when(i % per_seq == 0)
        def _():
            dgf_ref[...] = jnp.zeros_like(dgf_ref)

        hv = h_ref[...]

        def gate_up(t):
            rows = pl.ds(t * FF_TILE, FF_TILE)
            return (lax.dot_general(hv, wg[rows, :], NT, preferred_element_type=F32),
                    lax.dot_general(hv, wu[rows, :], NT, preferred_element_type=F32))

        fv = jnp.zeros((tm, D_MODEL), F32)
        ahead = gate_up(0)
        for t in range(FF_TILES):
            gate, up = ahead
            if t + 1 < FF_TILES:
                ahead = gate_up(t + 1)
            gate_ref[t] = gate.astype(BF16)
            up_ref[t] = up.astype(BF16)
            act = (gate * _sig(gate) * up).astype(BF16)
            fv = fv + jnp.dot(act, wd[pl.ds(t * FF_TILE, FF_TILE), :], preferred_element_type=F32)
        gate_f = m_ref[5:6, :]
        diff = x1_ref[...] + gate_f * fv - t_ref[...]
        sq_ref[0:1, :] += jnp.sum(diff * diff, axis=0, keepdims=True)
        dy = diff * (1.0 / D_MODEL)
        dy_ref[...] = dy
        df_ref[...] = (gate_f * dy).astype(BF16)
        dgf_ref[0:1, :] += jnp.sum(dy * fv, axis=0, keepdims=True)

    rows_spec = pl.BlockSpec((tm, D_MODEL), lambda i: (i, 0))
    per = pl.BlockSpec((None, 8, D_MODEL), lambda i: (i // per_seq, 0, 0))
    tiles = pl.BlockSpec((FF_TILES, tm, FF_TILE), lambda i: (0, i, 0))
    weight = pltpu.VMEM((D_FF, D_MODEL), BF16)
    return _call(
        body, name="ffn_fwd", grid=(tokens // tm,),
        in_specs=[rows_spec, per, rows_spec, rows_spec, ANY_SPEC, ANY_SPEC, ANY_SPEC],
        out_specs=[tiles, tiles, rows_spec, rows_spec, pl.BlockSpec((8, D_MODEL), lambda i: (0, 0)), per],
        out_shape=[_sds((FF_TILES, tokens, FF_TILE), BF16), _sds((FF_TILES, tokens, FF_TILE), BF16),
                   _sds((tokens, D_MODEL), F32), _sds((tokens, D_MODEL), BF16),
                   _sds((8, D_MODEL), F32), _sds((n_seq, 8, D_MODEL), F32)],
        scratch_shapes=[weight, weight, weight, pltpu.SemaphoreType.DMA((3,))],
        compiler_params=_params(("arbitrary",)),
    )(h2, mod8, x1, target, w_gate_t, w_up_t, w_down)


def _ffn_bwd(df, gate, up, w_gate_t, w_up_t, w_down, x1, dy, mix, mod8, g_ffn, seq, tm=256):
    tokens = df.shape[0]
    per_seq = seq // tm
    n_seq = tokens // seq

    def body(df_ref, gate_ref, up_ref, m_ref, g_ref, x1_ref, dy_ref, mix_ref, wg_hbm, wu_hbm, wd_hbm,
             dgate_ref, dup_ref, act_ref, dx1_ref, dmix_ref, dg_ref, dm_ref, wg, wu, wd, w_sems):
        i = pl.program_id(0)
        _load_once((wg_hbm, wu_hbm, wd_hbm), (wg, wu, wd), w_sems)

        @pl.when(i == 0)
        def _():
            dg_ref[...] = jnp.zeros_like(dg_ref)

        @pl.when(i % per_seq == 0)
        def _():
            dm_ref[...] = jnp.zeros_like(dm_ref)

        dfv = df_ref[...]

        def d_act(t):
            return lax.dot_general(dfv, wd[pl.ds(t * FF_TILE, FF_TILE), :], NT, preferred_element_type=F32)

        dh = jnp.zeros((tm, D_MODEL), F32)
        ahead = d_act(0)
        for t in range(FF_TILES):
            dact = ahead
            if t + 1 < FF_TILES:
                ahead = d_act(t + 1)
            rows = pl.ds(t * FF_TILE, FF_TILE)
            gv = gate_ref[t].astype(F32)
            uv = up_ref[t].astype(F32)
            sg = _sig(gv)
            silu = gv * sg
            act_ref[t] = (silu * uv).astype(BF16)
            dup = (dact * silu).astype(BF16)
            dgate = (dact * uv * (sg * (1.0 + gv * (1.0 - sg)))).astype(BF16)
            dup_ref[t] = dup
            dgate_ref[t] = dgate
            dh = dh + (jnp.dot(dgate, wg[rows, :], preferred_element_type=F32)
                       + jnp.dot(dup, wu[rows, :], preferred_element_type=F32))
        g = g_ref[...]
        x1v = x1_ref[...]
        rs = lax.rsqrt(jnp.mean(x1v * x1v, axis=-1, keepdims=True) + EPS)
        xhat = x1v * rs
        dm_ref[0:1, :] += jnp.sum(dh, axis=0, keepdims=True)
        dm_ref[1:2, :] += jnp.sum(dh * (xhat * g), axis=0, keepdims=True)
        dn = dh * (1.0 + m_ref[4:5, :])
        dg_ref[0:1, :] += jnp.sum(dn * xhat, axis=0, keepdims=True)
        dxh = dn * g
        dx1 = dy_ref[...] + rs * (dxh - xhat * jnp.mean(dxh * xhat, axis=-1, keepdims=True))
        dx1_ref[...] = dx1
        dm_ref[2:3, :] += jnp.sum(dx1 * mix_ref[...].astype(F32), axis=0, keepdims=True)
        dmix_ref[...] = (m_ref[2:3, :] * dx1).astype(BF16)

    rows_spec = pl.BlockSpec((tm, D_MODEL), lambda i: (i, 0))
    per = pl.BlockSpec((None, 8, D_MODEL), lambda i: (i // per_seq, 0, 0))
    tiles = pl.BlockSpec((FF_TILES, tm, FF_TILE), lambda i: (0, i, 0))
    weight = pltpu.VMEM((D_FF, D_MODEL), BF16)
    return _call(
        body, name="ffn_bwd", grid=(tokens // tm,),
        in_specs=[rows_spec, tiles, tiles, per, pl.BlockSpec((1, D_MODEL), lambda i: (0, 0)),
                  rows_spec, rows_spec, rows_spec, ANY_SPEC, ANY_SPEC, ANY_SPEC],
        out_specs=[tiles, tiles, tiles, rows_spec, rows_spec, pl.BlockSpec((8, D_MODEL), lambda i: (0, 0)), per],
        out_shape=[_sds((FF_TILES, tokens, FF_TILE), BF16)] * 3
        + [_sds((tokens, D_MODEL), F32), _sds((tokens, D_MODEL), BF16),
           _sds((8, D_MODEL), F32), _sds((n_seq, 8, D_MODEL), F32)],
        scratch_shapes=[weight, weight, weight, pltpu.SemaphoreType.DMA((3,))],
        compiler_params=_params(("arbitrary",)),
    )(df, gate, up, mod8, g_ffn, x1, dy, mix, w_gate_t, w_up_t, w_down)


def _mix_in_bwd(d_a, d_g, d_q, d_k, d_v, w_in, x2, dx1, mod8, g_mix, seq, ride, tm=512):
    tokens = x2.shape[0]
    per_seq = seq // tm
    n_seq = tokens // seq
    parts = (d_a, d_g, d_q, d_k, d_v)
    width = D_CONV
    n_ride = len(ride)
    ride_scatter = [s for _, s in ride]

    def body(*refs):
        da_ref, dg_ref, dq_ref, dk_ref, dv_ref, w_ref, x_ref, dx1_ref, m_ref, g_ref = refs[:10]
        ride_in = refs[10:10 + n_ride]
        gx_ref, dgm_ref, dm_ref = refs[10 + n_ride:13 + n_ride]
        ride_args = (ride_scatter, ride_in, refs[13 + n_ride:13 + 2 * n_ride]) + tuple(refs[13 + 2 * n_ride:])
        i = pl.program_id(0)

        @pl.when(i == 0)
        def _():
            _exchange_start(*ride_args)
            dgm_ref[...] = jnp.zeros_like(dgm_ref)

        @pl.when(i % per_seq == 0)
        def _():
            dm_ref[...] = jnp.zeros_like(dm_ref)

        dh = jnp.zeros((tm, D_MODEL), F32)
        for n, ref in enumerate((da_ref, dg_ref, dq_ref, dk_ref, dv_ref)):
            dh = dh + jnp.dot(ref[...], w_ref[pl.ds(n * width, width), :], preferred_element_type=F32)
        xv = x_ref[...]
        r = lax.rsqrt(jnp.mean(xv * xv, axis=-1, keepdims=True) + EPS)
        xhat = xv * r
        g = g_ref[...]
        dm_ref[0:1, :] += jnp.sum(dh, axis=0, keepdims=True)
        dm_ref[1:2, :] += jnp.sum(dh * (xhat * g), axis=0, keepdims=True)
        dn = dh * (1.0 + m_ref[1:2, :])
        dgm_ref[0:1, :] += jnp.sum(dn * xhat, axis=0, keepdims=True)
        dxh = dn * g
        gx_ref[...] = dx1_ref[...] + r * (dxh - xhat * jnp.mean(dxh * xhat, axis=-1, keepdims=True))

        @pl.when(i == tokens // tm - 1)
        def _():
            _exchange_wait(*ride_args)

    rows = pl.BlockSpec((tm, D_MODEL), lambda i: (i, 0))
    half = pl.BlockSpec((tm, width), lambda i: (i, 0))
    per = pl.BlockSpec((None, 8, D_MODEL), lambda i: (i // per_seq, 0, 0))
    return _call(
        body, name="mix_in_bwd", grid=(tokens // tm,),
        in_specs=[half] * 5 + [pl.BlockSpec((D_IN, D_MODEL), lambda i: (0, 0)), rows, rows, per,
                               pl.BlockSpec((1, D_MODEL), lambda i: (0, 0))] + [ANY_SPEC] * n_ride,
        out_specs=[rows, pl.BlockSpec((8, D_MODEL), lambda i: (0, 0)), per] + [ANY_SPEC] * n_ride,
        out_shape=[_sds((tokens, D_MODEL), F32), _sds((8, D_MODEL), F32), _sds((n_seq, 8, D_MODEL), F32)]
        + _exchange_shapes(ride),
        scratch_shapes=_exchange_sems(n_ride),
        compiler_params=_params(("arbitrary",)),
    )(*parts, w_in, x2, dx1, mod8, g_mix, *[a for a, _ in ride])


def _grad_matmul_parts(a_parts, b_parts, name, tk=1024):
    tokens = a_parts[0].shape[0]
    na, nb = len(a_parts), len(b_parts)
    ma, nbw = a_parts[0].shape[1], b_parts[0].shape[1]

    n_k = tokens // tk

    def body(*refs):
        a_refs, b_refs, o_ref, acc = refs[:na], refs[na:na + nb], refs[na + nb], refs[na + nb + 1]

        @pl.when(pl.program_id(0) == 0)
        def _():
            acc[...] = jnp.zeros_like(acc)

        for i in range(na):
            for j in range(nb):
                acc[pl.ds(i * ma, ma), pl.ds(j * nbw, nbw)] += lax.dot_general(
                    a_refs[i][...], b_refs[j][...], TN, preferred_element_type=F32)

        @pl.when(pl.program_id(0) == n_k - 1)
        def _():
            o_ref[...] = acc[...].astype(o_ref.dtype)

    return _call(
        body, name=name, grid=(n_k,),
        in_specs=[pl.BlockSpec((tk, ma), lambda k: (k, 0))] * na + [pl.BlockSpec((tk, nbw), lambda k: (k, 0))] * nb,
        out_specs=pl.BlockSpec((na * ma, nb * nbw), lambda k: (0, 0)),
        out_shape=_sds((na * ma, nb * nbw), BF16),
        scratch_shapes=[pltpu.VMEM((na * ma, nb * nbw), F32)],
        compiler_params=_params(("arbitrary",)),
    )(*a_parts, *b_parts)


def _grad_matmul_tiles(a, b, name, tk=1024):
    tiled_b = b.ndim == 3
    tiles, tokens, width = b.shape if tiled_b else a.shape
    other = a.shape[1] if tiled_b else b.shape[1]
    out_tile = (other, width) if tiled_b else (width, other)
    n_k = tokens // tk

    def body(a_ref, b_ref, o_ref, acc):
        @pl.when(pl.program_id(0) == 0)
        def _():
            acc[...] = jnp.zeros_like(acc)

        for t in range(tiles):
            lhs = a_ref[...] if tiled_b else a_ref[t]
            rhs = b_ref[t] if tiled_b else b_ref[...]
            acc[t] += lax.dot_general(lhs, rhs, TN, preferred_element_type=F32)

        @pl.when(pl.program_id(0) == n_k - 1)
        def _():
            o_ref[...] = acc[...].astype(o_ref.dtype)

    flat = pl.BlockSpec((tk, other), lambda k: (k, 0))
    tiled = pl.BlockSpec((tiles, tk, width), lambda k: (0, k, 0))
    return _call(
        body, name=name, grid=(n_k,),
        in_specs=[flat, tiled] if tiled_b else [tiled, flat],
        out_specs=pl.BlockSpec((tiles,) + out_tile, lambda k: (0, 0, 0)),
        out_shape=_sds((tiles,) + out_tile, BF16),
        scratch_shapes=[pltpu.VMEM((tiles,) + out_tile, F32)],
        compiler_params=_params(("arbitrary",)),
    )(a, b)


def _adamw(w, m, v, g, name, n_parts=0, tr=256):
    rows, cols = w.shape
    tr = min(tr, rows)
    c1 = 1.0 - ADAM_B1 ** ADAM_STEP
    c2 = 1.0 - ADAM_B2 ** ADAM_STEP

    def body(w_ref, m_ref, v_ref, g_ref, go_ref, d_ref, mo_ref, vo_ref):
        if n_parts:
            gv = g_ref[0].astype(F32)
            for p in range(1, n_parts):
                gv = gv + g_ref[p].astype(F32)
        else:
            gv = g_ref[...]
        go_ref[...] = gv
        mn = ADAM_B1 * m_ref[...] + (1.0 - ADAM_B1) * gv
        vn = ADAM_B2 * v_ref[...] + (1.0 - ADAM_B2) * (gv * gv)
        mo_ref[...] = mn
        vo_ref[...] = vn
        d_ref[...] = -ADAM_LR * ((mn / c1) / (jnp.sqrt(vn / c2) + ADAM_EPS) + ADAM_WD * w_ref[...])

    blk = pl.BlockSpec((tr, cols), lambda i: (i, 0))
    g_spec = pl.BlockSpec((n_parts, tr, cols), lambda i: (0, i, 0)) if n_parts else blk
    return _call(
        body, name=name, grid=(rows // tr,),
        in_specs=[blk, blk, blk, g_spec], out_specs=[blk] * 4,
        out_shape=[_sds((rows, cols), F32)] * 4,
        compiler_params=_params(("parallel",)),
    )(w, m, v, g)


def _cols_to_full(blocks):
    n, r, c = blocks.shape
    return jnp.transpose(blocks, (1, 0, 2)).reshape(r, n * c)


def _pad_lanes(v, width):
    return jnp.pad(v, ((0, 0), (0, width - v.shape[1])))


def kernel(x, c, w_ada, b_ada, g_mix, w_in, w_dw, b_dw, g_conv_ln, b_conv_ln, g_q, g_k, w_out, g_ffn, w_gate, w_up, w_down, loss_target, m_w_ada, m_b_ada, m_g_mix, m_w_in, m_w_dw, m_b_dw, m_g_conv_ln, m_b_conv_ln, m_g_q, m_g_k, m_w_out, m_g_ffn, m_w_gate, m_w_up, m_w_down, v_w_ada, v_b_ada, v_g_mix, v_w_in, v_w_dw, v_b_dw, v_g_conv_ln, v_b_conv_ln, v_g_q, v_g_k, v_w_out, v_g_ffn, v_w_gate, v_w_up, v_w_down):
    n_seq, seq, _ = x.shape
    tokens = n_seq * seq
    me = 4 * lax.axis_index("x") + 2 * lax.axis_index("y") + lax.axis_index("c")
    ada_cols = w_ada.shape[2]
    dw_cols = w_dw.shape[2]

    def transposed(w):
        return jnp.transpose(w[0])

    (c_g, w_in_g, w_dw_g) = _gather_by_chip([c, transposed(w_in).astype(BF16), w_dw[0]], "gather_weights")
    c_all = c_g.reshape(N_DEV * n_seq, D_MODEL)
    w_in_t = w_in_g.reshape(D_IN, D_MODEL)
    w_dw_f = _cols_to_full(w_dw_g)

    b_cols = lax.dynamic_slice(b_ada, (0, me * ada_cols), (1, ada_cols))
    mod_cols = _ada_fwd(c_all, w_ada[0], b_cols)
    (mod_g,) = _exchange([(mod_cols, False)], "gather_mod")
    mod_mine = lax.dynamic_slice(mod_g, (0, me * n_seq, 0), (N_DEV, n_seq, ada_cols))
    mod = jnp.transpose(mod_mine, (1, 0, 2)).reshape(n_seq, N_MOD, D_MODEL)
    mod8 = jnp.pad(mod, ((0, 0), (0, 8 - N_MOD), (0, 0)))

    x2 = x.reshape(tokens, D_MODEL)
    h1, proj = _mix_in(x2, mod8, g_mix, w_in_t, seq)
    proj3 = proj.reshape(D_IN // LANES, n_seq, seq, LANES)
    uc3 = _conv_fwd(proj3, w_dw_f, b_dw)
    g_q2, g_k2 = jnp.tile(g_q, (1, 2)), jnp.tile(g_k, (1, 2))
    y_att3, lse3, w_out_g, w_gate_g, w_up_g, w_down_g = _attn_fwd(
        proj3, g_q2, g_k2,
        [(w_out[0].astype(BF16), False), (transposed(w_gate).astype(BF16), False),
         (transposed(w_up).astype(BF16), False), (w_down[0].astype(BF16), False)])
    w_out_f = w_out_g.reshape(D_MODEL, D_MODEL)
    w_gate_f = w_gate_g.reshape(D_FF, D_MODEL)
    w_up_f = w_up_g.reshape(D_FF, D_MODEL)
    w_down_f = w_down_g.reshape(D_FF, D_MODEL)
    uc2 = uc3.reshape(tokens, D_CONV)
    y_att2 = y_att3.reshape(tokens, D_ATT)
    y_conv, mix, x1, h2 = _mix_out(uc2, y_att2, x2, mod8, g_conv_ln, b_conv_ln, g_ffn, w_out_f, seq)
    gate, up, dy, df, sq, dgate_f = _ffn_fwd(
        h2, w_gate_f, w_up_f, w_down_f, x1, loss_target.reshape(tokens, D_MODEL), mod8, seq)

    dgate, dup, act, dx1, dmix, dg_ffn, dmod_f = _ffn_bwd(
        df, gate, up, w_gate_f, w_up_f, w_down_f, x1, dy, mix, mod8, g_ffn, seq)
    duc2, do2, dgb_ln = _mix_out_bwd(dmix, uc2, g_conv_ln, b_conv_ln, w_out_f)
    d_a3, d_g3, dw_dw_p, db_dw_p = _conv_bwd(duc2.reshape(n_seq, seq, D_CONV), proj3, w_dw_f)
    gw_gate = _grad_matmul_tiles(dgate, h2, "grad_w_gate")
    gw_up = _grad_matmul_tiles(dup, h2, "grad_w_up")
    gw_down = _grad_matmul_tiles(act, df, "grad_w_down")
    gw_out = _grad_matmul_parts([y_conv, y_att2], [dmix], "grad_w_out")
    d_q3, d_k3, d_v3, dg_qk, p_gate, p_up, p_down, p_out = _attn_bwd(
        proj3, do2.reshape(n_seq, seq, D_ATT), y_att3, lse3, g_q2, g_k2,
        [(gw_gate.reshape(N_DEV, D_FF // N_DEV, D_MODEL), True), (gw_up.reshape(N_DEV, D_FF // N_DEV, D_MODEL), True),
         (gw_down.reshape(N_DEV, D_FF // N_DEV, D_MODEL), True),
         (gw_out.reshape(N_DEV, D_MODEL // N_DEV, D_MODEL), True)])
    flat = lambda t: t.reshape(tokens, t.shape[-1])
    d_a, d_g, d_q, d_k, d_v = flat(d_a3), flat(d_g3), flat(d_q3), flat(d_k3), flat(d_v3)
    gw_in = _grad_matmul_parts([d_a, d_g, d_q, d_k, d_v], [h1], "grad_w_in")
    grad_x2, dg_mix, dmod_m, p_in = _mix_in_bwd(
        d_a, d_g, d_q, d_k, d_v, w_in_t, x2, dx1, mod8, g_mix, seq,
        [(gw_in.reshape(N_DEV, D_IN // N_DEV, D_MODEL), True)])

    dmod = jnp.concatenate([dmod_m[:, 0], dmod_m[:, 1], dmod_f[:, 2], dmod_f[:, 0], dmod_f[:, 1], dgate_f[:, 0]], axis=1)
    dg_q = dg_qk[0:1, 0:HEAD_DIM] + dg_qk[0:1, HEAD_DIM:]
    dg_k = dg_qk[1:2, 0:HEAD_DIM] + dg_qk[1:2, HEAD_DIM:]
    loss_part = (0.5 / D_MODEL) * jnp.sum(sq[0:1, :], axis=1, keepdims=True)
    small = jnp.concatenate(
        [dg_mix[0:1], dg_ffn[0:1], db_dw_p[0:1], dgb_ln[0:1], dgb_ln[1:2],
         _pad_lanes(dg_q, LANES), _pad_lanes(dg_k, LANES), _pad_lanes(loss_part, LANES)], axis=1)
    n_small = small.shape[1] - LANES

    (dmod_g, small_g, dw_g) = _exchange([(dmod, False), (small, False), (dw_dw_p, False)], "gather_small_grads")

    dmod_all = dmod_g.reshape(N_DEV * n_seq, N_MOD * D_MODEL)
    dmod_cols = lax.dynamic_slice(dmod_all, (0, me * ada_cols), (N_DEV * n_seq, ada_cols))
    gw_ada, gb_ada = _ada_bwd(c_all, dmod_cols, dmod_all)

    res = {}
    res["w_ada"] = _adamw(w_ada[0], m_w_ada[0], v_w_ada[0], gw_ada, "adamw_w_ada")
    res["b_ada"] = _adamw(b_ada, m_b_ada, v_b_ada, gb_ada, "adamw_b_ada")
    def adamw_transposed(w, m, v, parts, name, tr):
        outs = _adamw(transposed(w), transposed(m), transposed(v), parts, name, N_DEV, tr=tr)
        return tuple(jnp.transpose(o) for o in outs)

    res["w_in"] = adamw_transposed(w_in, m_w_in, v_w_in, p_in, "adamw_w_in", 160)
    res["w_out"] = _adamw(w_out[0], m_w_out[0], v_w_out[0], p_out, "adamw_w_out", N_DEV)
    res["w_gate"] = adamw_transposed(w_gate, m_w_gate, v_w_gate, p_gate, "adamw_w_gate", 176)
    res["w_up"] = adamw_transposed(w_up, m_w_up, v_w_up, p_up, "adamw_w_up", 176)
    res["w_down"] = _adamw(w_down[0], m_w_down[0], v_w_down[0], p_down, "adamw_w_down", N_DEV, tr=176)
    dw_mine = lax.dynamic_slice(dw_g, (0, 0, me * dw_cols), (N_DEV, CONV_WIDTH, dw_cols))
    res["w_dw"] = _adamw(w_dw[0], m_w_dw[0], v_w_dw[0], dw_mine, "adamw_w_dw", N_DEV)

    small_names = ["g_mix", "g_ffn", "b_dw", "g_conv_ln", "b_conv_ln", "g_q", "g_k"]
    small_w = {"g_mix": (g_mix, m_g_mix, v_g_mix), "g_ffn": (g_ffn, m_g_ffn, v_g_ffn), "b_dw": (b_dw, m_b_dw, v_b_dw),
               "g_conv_ln": (g_conv_ln, m_g_conv_ln, v_g_conv_ln), "b_conv_ln": (b_conv_ln, m_b_conv_ln, v_b_conv_ln),
               "g_q": (g_q, m_g_q, v_g_q), "g_k": (g_k, m_g_k, v_g_k)}
    widths = [max(small_w[n][0].shape[1], LANES) for n in small_names]
    packed = [jnp.concatenate([_pad_lanes(small_w[n][i], wd) for n, wd in zip(small_names, widths)], axis=1) for i in range(3)]
    outs = _adamw(packed[0], packed[1], packed[2], small_g[:, :, :n_small], "adamw_small", N_DEV)
    off = 0
    for n, wd in zip(small_names, widths):
        real = small_w[n][0].shape[1]
        res[n] = tuple(o[:, off:off + real] for o in outs)
        off += wd
    loss = jnp.sum(small_g[:, 0, n_small])

    order = ["w_ada", "b_ada", "g_mix", "w_in", "w_dw", "b_dw", "g_conv_ln", "b_conv_ln", "g_q", "g_k",
             "w_out", "g_ffn", "w_gate", "w_up", "w_down"]
    lead = {"w_ada", "w_in", "w_dw", "w_out", "w_gate", "w_up", "w_down"}
    grads, deltas, new_m, new_v = [], [], [], []
    for n in order:
        g, d, mn, vn = res[n]
        g, d, mn, vn = (t[None] if n in lead else t for t in (g, d, mn, vn))
        grads.append(g)
        deltas.append(d)
        new_m.append(mn)
        new_v.append(vn)
    return (loss, grad_x2.reshape(n_seq, seq, D_MODEL), *grads, *deltas, *new_m, *new_v)
```

```python
import numpy as np
import jax
import jax.numpy as jnp
from jax import lax
from jax.experimental import pallas as pl
from jax.experimental.pallas import tpu as pltpu

F32 = jnp.float32
BF16 = jnp.bfloat16

N_DEV = 8
D_MODEL = 1024
D_CONV = 512
D_ATT = 512
HEAD_DIM = 64
CONV_WIDTH = 31
D_IN = 2 * D_CONV + 3 * D_ATT
D_FF = 2816
N_MOD = 6
EPS = 1e-6
RADIUS = 64
DILATIONS = (1, 4, 16)
Q_BLOCK = 128
LANES = 128
VMEM_LIMIT = 56 * 1024 * 1024

ADAM_LR = 0.001
ADAM_B1 = 0.9
ADAM_B2 = 0.999
ADAM_EPS = 1e-08
ADAM_WD = 0.01
ADAM_STEP = 10

NT = (((1,), (1,)), ((), ()))
TN = (((0,), (0,)), ((), ()))


def _call(body, **kw):
    return pl.pallas_call(body, **kw)


def _params(sem=None, vmem=VMEM_LIMIT):
    return pltpu.CompilerParams(dimension_semantics=sem, vmem_limit_bytes=vmem)


def _sig(x):
    return 1.0 / (1.0 + jnp.exp(-x))


def _sds(shape, dtype):
    return jax.ShapeDtypeStruct(shape, dtype)


N_PEER = N_DEV - 1
ANY_SPEC = pl.BlockSpec(memory_space=pl.ANY)


def _exchange_copies(scatter, ins, outs, *sems):
    n = len(ins)
    if n == 0:
        return [], []
    send_sems, recv_sems, local_sems = sems
    x, y, c = lax.axis_index("x"), lax.axis_index("y"), lax.axis_index("c")
    me = 4 * x + 2 * y + c

    def src(a, slot):
        return ins[a].at[slot] if scatter[a] else ins[a]

    local = [pltpu.make_async_copy(src(a, me), outs[a].at[me], local_sems.at[a]) for a in range(n)]
    flights = []
    for k in range(1, N_DEV):
        px = 1 - x if k & 4 else x
        py = 1 - y if k & 2 else y
        pc = 1 - c if k & 1 else c
        pid = 4 * px + 2 * py + pc
        for a in range(n):
            i = a * N_PEER + k - 1
            send, recv = (pltpu.make_async_remote_copy(
                src_ref=src(a, pid), dst_ref=outs[a].at[slot],
                send_sem=send_sems.at[i], recv_sem=recv_sems.at[i],
                device_id=(px, py, pc), device_id_type=pl.DeviceIdType.MESH) for slot in (me, pid))
            flights.append((send, recv))
    return local, flights


def _exchange_start(*args):
    local, flights = _exchange_copies(*args)
    for cp in local:
        cp.start()
    for send, _ in flights:
        send.start()


def _exchange_wait(*args):
    local, flights = _exchange_copies(*args)
    for send, recv in flights:
        send.wait_send()
        recv.wait_recv()
    for cp in local:
        cp.wait()


def _exchange_shapes(items):
    return [_sds((N_DEV,) + tuple(arr.shape[1:] if scatter else arr.shape), arr.dtype) for arr, scatter in items]


def _exchange_sems(n):
    if n == 0:
        return []
    return [pltpu.SemaphoreType.DMA((n * N_PEER,)), pltpu.SemaphoreType.DMA((n * N_PEER,)),
            pltpu.SemaphoreType.DMA((n,))]


def _gather_by_chip_phase(phase, ins, outs, send_sems, recv_sems, local_sems):
    n = len(ins)
    per = N_PEER
    x, y, c = lax.axis_index("x"), lax.axis_index("y"), lax.axis_index("c")
    me, sibling = (x, y, c), (x, y, 1 - c)
    chips = [(1 - x, y), (x, 1 - y), (1 - x, 1 - y)]

    def slot(px, py, pc):
        return 4 * px + 2 * py + pc

    def copy(a, k, block, to, src=None):
        dst = outs[a].at[slot(*block)]
        return pltpu.make_async_remote_copy(
            src_ref=dst if src is None else src, dst_ref=dst,
            send_sem=send_sems.at[a * per + k], recv_sem=recv_sems.at[a * per + k],
            device_id=to, device_id_type=pl.DeviceIdType.MESH)

    local = [pltpu.make_async_copy(ins[a], outs[a].at[slot(*me)], local_sems.at[a]) for a in range(n)]
    first = []
    for a in range(n):
        first.append(copy(a, 0, me, sibling, src=ins[a]))
        first += [copy(a, 1 + j, me, (*chip, c), src=ins[a]) for j, chip in enumerate(chips)]
    passed = [copy(a, 4 + j, (*chip, c), sibling) for j, chip in enumerate(chips) for a in range(n)]
    if phase == 0:
        for cp in local + first:
            cp.start()
    elif phase == 1:
        for j, chip in enumerate(chips):
            for a in range(n):
                copy(a, 1 + j, (*chip, c), me).wait_recv()
        for cp in passed:
            cp.start()
    else:
        for a in range(n):
            copy(a, 0, sibling, me).wait_recv()
            for j, chip in enumerate(chips):
                copy(a, 4 + j, (*chip, 1 - c), me).wait_recv()
        for cp in first + passed:
            cp.wait_send()
        for cp in local:
            cp.wait()


def _gather_by_chip(arrays, name):
    n = len(arrays)

    def body(*refs):
        for phase in range(3):
            _gather_by_chip_phase(phase, refs[:n], refs[n:2 * n], *refs[2 * n:])

    return _call(
        body, name=name, out_shape=_exchange_shapes([(arr, False) for arr in arrays]),
        in_specs=[ANY_SPEC] * n, out_specs=[ANY_SPEC] * n, scratch_shapes=_exchange_sems(n),
    )(*arrays)


def _exchange(items, name):
    n = len(items)
    scatter = [s for _, s in items]

    def body(*refs):
        args = (scatter, refs[:n], refs[n:2 * n]) + tuple(refs[2 * n:])
        _exchange_start(*args)
        _exchange_wait(*args)

    return _call(
        body, name=name, out_shape=_exchange_shapes(items),
        in_specs=[ANY_SPEC] * n, out_specs=[ANY_SPEC] * n, scratch_shapes=_exchange_sems(n),
    )(*[a for a, _ in items])


def _ada_fwd(c_all, w_ada, b_cols):
    def body(c_ref, w_ref, b_ref, o_ref):
        cv = c_ref[...]
        sc = (cv * _sig(cv)).astype(BF16)
        o_ref[...] = jnp.dot(sc, w_ref[...].astype(BF16), preferred_element_type=F32) + b_ref[...]

    return _call(body, name="ada_fwd", out_shape=_sds((c_all.shape[0], w_ada.shape[1]), F32),
                 compiler_params=_params())(c_all, w_ada, b_cols)


def _ada_bwd(c_all, dmod_cols, dmod_all):
    def body(c_ref, dc_ref, da_ref, gw_ref, gb_ref):
        cv = c_ref[...]
        sc = (cv * _sig(cv)).astype(BF16)
        gw_ref[...] = lax.dot_general(sc, dc_ref[...].astype(BF16), TN, preferred_element_type=F32)
        gb_ref[...] = jnp.sum(da_ref[...], axis=0, keepdims=True)

    return _call(body, name="ada_bwd",
                 out_shape=[_sds((c_all.shape[1], dmod_cols.shape[1]), F32), _sds((1, dmod_all.shape[1]), F32)],
                 compiler_params=_params())(c_all, dmod_cols, dmod_all)


MIX_ROWS = 128


def _mix_in(x2, mod8, g_mix, w_in, seq, tm=512):
    tokens = x2.shape[0]
    per_seq = seq // tm

    def body(x_ref, m_ref, g_ref, wt_ref, h_ref, p_ref, w_ref):
        @pl.when(pl.program_id(0) == 0)
        def _():
            w_ref[...] = wt_ref[...].T

        def normed(c):
            rows = pl.ds(c * MIX_ROWS, MIX_ROWS)
            xv = x_ref[rows, :]
            r = lax.rsqrt(jnp.mean(xv * xv, axis=-1, keepdims=True) + EPS)
            hb = ((xv * r * g_ref[...]) * (1.0 + m_ref[1:2, :]) + m_ref[0:1, :]).astype(BF16)
            h_ref[rows, :] = hb
            return hb

        ahead = normed(0)
        for c in range(tm // MIX_ROWS):
            hb = ahead
            if c + 1 < tm // MIX_ROWS:
                ahead = normed(c + 1)
            p = jnp.dot(hb, w_ref[...], preferred_element_type=F32)
            for cb in range(D_IN // LANES):
                p_ref[cb, pl.ds(c * MIX_ROWS, MIX_ROWS), :] = p[:, cb * LANES:(cb + 1) * LANES]

    return _call(
        body, name="mix_in", grid=(tokens // tm,),
        in_specs=[pl.BlockSpec((tm, D_MODEL), lambda i: (i, 0)),
                  pl.BlockSpec((None, 8, D_MODEL), lambda i: (i // per_seq, 0, 0)),
                  pl.BlockSpec((1, D_MODEL), lambda i: (0, 0)),
                  pl.BlockSpec((D_IN, D_MODEL), lambda i: (0, 0))],
        out_specs=[pl.BlockSpec((tm, D_MODEL), lambda i: (i, 0)),
                   pl.BlockSpec((D_IN // LANES, tm, LANES), lambda i: (0, i, 0))],
        out_shape=[_sds((tokens, D_MODEL), BF16), _sds((D_IN // LANES, tokens, LANES), F32)],
        scratch_shapes=[pltpu.VMEM((D_MODEL, D_IN), BF16)],
        compiler_params=_params(("arbitrary",)),
    )(x2, mod8, g_mix, w_in)


CONV_ROWS = 128
CONV_DW_ROWS = 32
CONV_DW_UNROLL = 8
CONV_HALO = 16


def _fill_shifted(xp, sh, seq):
    for b in range(8):
        sh[b, pl.ds(0, seq + 24), :] = xp[pl.ds(b, seq + 24), :]


def _conv_fwd(proj3, w_dw, b_dw):
    _, n_seq, seq, _ = proj3.shape
    n_cb = D_CONV // LANES

    def body(a_ref, g_ref, w_ref, b_ref, uc_ref, xp, sh):
        zeros = jnp.zeros((CONV_HALO, LANES), F32)
        xp[pl.ds(0, CONV_HALO), :] = zeros
        xp[pl.ds(CONV_HALO + seq, CONV_HALO), :] = zeros
        xp[pl.ds(CONV_HALO, seq), :] = a_ref[...] * _sig(g_ref[...])
        _fill_shifted(xp, sh, seq)

        def blk(i, carry):
            t0 = pl.multiple_of(i * CONV_ROWS, CONV_ROWS)
            acc = jnp.zeros((CONV_ROWS, LANES), F32)
            for j in range(CONV_WIDTH):
                jj = j + 1
                acc = acc + sh[jj % 8, pl.ds(t0 + 8 * (jj // 8), CONV_ROWS), :] * w_ref[j:j + 1, :]
            uc_ref[pl.ds(t0, CONV_ROWS), :] = acc + b_ref[...]
            return carry

        lax.fori_loop(0, seq // CONV_ROWS, blk, 0)

    return _call(
        body, name="conv_fwd", grid=(n_seq, n_cb),
        in_specs=[pl.BlockSpec((None, None, seq, LANES), lambda b, cb: (cb, b, 0, 0)),
                  pl.BlockSpec((None, None, seq, LANES), lambda b, cb: (n_cb + cb, b, 0, 0)),
                  pl.BlockSpec((CONV_WIDTH, LANES), lambda b, cb: (0, cb)),
                  pl.BlockSpec((1, LANES), lambda b, cb: (0, cb))],
        out_specs=pl.BlockSpec((None, seq, LANES), lambda b, cb: (b, 0, cb)),
        out_shape=_sds((n_seq, seq, D_CONV), F32),
        scratch_shapes=[pltpu.VMEM((seq + 2 * CONV_HALO, LANES), F32),
                        pltpu.VMEM((8, seq + 2 * CONV_HALO, LANES), F32)],
        compiler_params=_params(("parallel", "parallel")),
    )(proj3, proj3, w_dw, b_dw)


def _conv_bwd(duc3, proj3, w_dw):
    _, n_seq, seq, _ = proj3.shape
    n_cb = D_CONV // LANES

    def body(duc_ref, a_ref, g_ref, w_ref, da_ref, dg_ref, dw_ref, db_ref, xp, sh):
        @pl.when(pl.program_id(1) == 0)
        def _():
            dw_ref[...] = jnp.zeros_like(dw_ref)
            db_ref[...] = jnp.zeros_like(db_ref)

        zeros = jnp.zeros((CONV_HALO, LANES), F32)
        xp[pl.ds(0, CONV_HALO), :] = zeros
        xp[pl.ds(CONV_HALO + seq, CONV_HALO), :] = zeros
        xp[pl.ds(CONV_HALO, seq), :] = a_ref[...] * _sig(g_ref[...])
        _fill_shifted(xp, sh, seq)
        for j0 in range(0, CONV_WIDTH, 8):
            taps = range(j0, min(j0 + 8, CONV_WIDTH))

            def wblk(i, accs, taps=taps):
                for u in range(CONV_DW_UNROLL):
                    t0 = pl.multiple_of((i * CONV_DW_UNROLL + u) * CONV_DW_ROWS, CONV_DW_ROWS)
                    d = duc_ref[pl.ds(t0, CONV_DW_ROWS), :]
                    accs = tuple(acc + d * sh[(j + 1) % 8, pl.ds(t0 + 8 * ((j + 1) // 8), CONV_DW_ROWS), :]
                                 for acc, j in zip(accs, taps))
                return accs

            accs = lax.fori_loop(0, seq // (CONV_DW_ROWS * CONV_DW_UNROLL), wblk,
                                 tuple(jnp.zeros((CONV_DW_ROWS, LANES), F32) for _ in taps))
            for acc, j in zip(accs, taps):
                dw_ref[j:j + 1, :] += jnp.sum(acc, axis=0, keepdims=True)
        db_ref[0:1, :] += jnp.sum(duc_ref[...], axis=0, keepdims=True)
        xp[pl.ds(CONV_HALO, seq), :] = duc_ref[...]
        _fill_shifted(xp, sh, seq)

        def ublk(i, carry):
            t0 = pl.multiple_of(i * CONV_ROWS, CONV_ROWS)
            acc = jnp.zeros((CONV_ROWS, LANES), F32)
            for j in range(CONV_WIDTH):
                jj = CONV_WIDTH - j
                acc = acc + sh[jj % 8, pl.ds(t0 + 8 * (jj // 8), CONV_ROWS), :] * w_ref[j:j + 1, :]
            av = a_ref[pl.ds(t0, CONV_ROWS), :]
            sg = _sig(g_ref[pl.ds(t0, CONV_ROWS), :])
            da_ref[pl.ds(t0, CONV_ROWS), :] = (acc * sg).astype(BF16)
            dg_ref[pl.ds(t0, CONV_ROWS), :] = (acc * av * sg * (1.0 - sg)).astype(BF16)
            return carry

        lax.fori_loop(0, seq // CONV_ROWS, ublk, 0)

    return _call(
        body, name="conv_bwd", grid=(n_cb, n_seq),
        in_specs=[pl.BlockSpec((None, seq, LANES), lambda cb, b: (b, 0, cb)),
                  pl.BlockSpec((None, None, seq, LANES), lambda cb, b: (cb, b, 0, 0)),
                  pl.BlockSpec((None, None, seq, LANES), lambda cb, b: (n_cb + cb, b, 0, 0)),
                  pl.BlockSpec((CONV_WIDTH, LANES), lambda cb, b: (0, cb))],
        out_specs=[pl.BlockSpec((None, seq, LANES), lambda cb, b: (b, 0, cb)),
                   pl.BlockSpec((None, seq, LANES), lambda cb, b: (b, 0, cb)),
                   pl.BlockSpec((32, LANES), lambda cb, b: (0, cb)),
                   pl.BlockSpec((8, LANES), lambda cb, b: (0, cb))],
        out_shape=[_sds((n_seq, seq, D_CONV), BF16), _sds((n_seq, seq, D_CONV), BF16),
                   _sds((32, D_CONV), F32), _sds((8, D_CONV), F32)],
        scratch_shapes=[pltpu.VMEM((seq + 2 * CONV_HALO, LANES), F32),
                        pltpu.VMEM((8, seq + 2 * CONV_HALO, LANES), F32)],
        compiler_params=_params(("parallel", "arbitrary")),
    )(duc3, proj3, proj3, w_dw)


MASKED = 1e30
ATT_ROWS = 1024
ATT_BWD_ROWS = 2048
ATT_UNROLL = 8
ATT_FWD_UNROLL = 8


def _distance_mats(dil, seg_len):
    kw = min(2 * Q_BLOCK, seg_len)
    offsets = (0, -RADIUS, -2 * RADIUS) if kw == 2 * Q_BLOCK else (0,)
    a = np.arange(Q_BLOCK)[:, None]
    b = np.arange(kw)[None, :]
    mats = []
    for off in offsets:
        rel = np.abs(b + off - a)
        mats.append(np.where(rel <= RADIUS, dil * rel, MASKED))
    return jnp.asarray(np.stack(mats).astype(np.float32))


def _alibi_rows():
    s = np.zeros((4, 8, LANES), np.float32)
    for hp in range(4):
        for hl in range(2):
            s[hp, hl, :] = 2.0 ** (-(2 * hp + hl + 1))
    return jnp.asarray(s)


def _window(n, seg_len):
    i0 = pl.multiple_of(n * Q_BLOCK, Q_BLOCK)
    if seg_len <= Q_BLOCK:
        return i0, i0, 0
    per_seg = seg_len // Q_BLOCK
    j = n % per_seg
    seg0 = (n // per_seg) * seg_len
    ks_local = jnp.clip(j * Q_BLOCK - RADIUS, 0, seg_len - 2 * Q_BLOCK)
    ks = pl.multiple_of(seg0 + ks_local, RADIUS)
    var = jnp.where(j == 0, 0, jnp.where(j == per_seg - 1, 2, 1))
    return i0, ks, var


def _first_head(rows):
    return lax.broadcasted_iota(jnp.int32, (rows, LANES), 1) < HEAD_DIM


def _same_head():
    head = np.arange(LANES) // HEAD_DIM
    return jnp.asarray((head[:, None] == head[None, :]).astype(np.float32)).astype(BF16)


def _head_sum(x, same_ref):
    return jnp.dot(x.astype(BF16), same_ref[...], preferred_element_type=F32)


def _head_mean(x, same_ref):
    return _head_sum(x, same_ref) * (1.0 / HEAD_DIM)


def _per_head(x, first):
    swapped = pltpu.roll(x, HEAD_DIM, 1)
    return jnp.where(first, x, swapped), jnp.where(first, swapped, x)


STRIDE = 4


def _gather_segments(src, dil, seq, tmp, put, staged=False):
    if dil == 1:
        put(0, seq, src[pl.ds(0, seq), :])
    elif dil == STRIDE:
        seg = seq // dil
        for r in range(dil):
            put(r * seg, seg, src[pl.ds(r, seg, stride=dil), :])
    else:
        part, seg = seq // STRIDE, seq // dil
        if staged:
            tmp = src
        else:
            for b in range(STRIDE):
                tmp[pl.ds(b * part, part), :] = src[pl.ds(b, part, stride=STRIDE), :]
        for b in range(STRIDE):
            for a in range(dil // STRIDE):
                put(b * part + a * seg, seg, tmp[pl.ds(b * part + a, seg, stride=dil // STRIDE), :])


def _scatter_segments(dst, get, dil, seq, tmp, accumulate):
    def write(rows, val):
        if accumulate:
            dst[rows, :] += val
        else:
            dst[rows, :] = val

    if dil == 1:
        write(pl.ds(0, seq), get(0, seq))
    elif dil == STRIDE:
        seg = seq // dil
        for r in range(dil):
            write(pl.ds(r, seg, stride=dil), get(r * seg, seg))
    else:
        part, seg = seq // STRIDE, seq // dil
        for b in range(STRIDE):
            for a in range(dil // STRIDE):
                tmp[pl.ds(b * part + a, seg, stride=dil // STRIDE), :] = get(b * part + a * seg, seg)
        for b in range(STRIDE):
            write(pl.ds(b, part, stride=STRIDE), tmp[pl.ds(b * part, part), :])


def _permute_rows(dst, src, dil, seq, tmp, staged=False):
    def put(start, size, val):
        dst[pl.ds(start, size), :] = val.astype(dst.dtype)

    _gather_segments(src, dil, seq, tmp, put, staged)


def _permute_rows_by_head(dst, src, dil, seq, tmp):
    def put(start, size, val):
        first = _first_head(size)
        dst[0, pl.ds(start, size), :] = jnp.where(first, val, 0.0).astype(dst.dtype)
        dst[1, pl.ds(start, size), :] = jnp.where(first, 0.0, val).astype(dst.dtype)

    _gather_segments(src, dil, seq, tmp, put)


def _qk_normalise(q_ref, g2_ref, same_ref, dst, seq, scale, step):
    def chunk(ci, carry):
        rows = pl.ds(pl.multiple_of(ci * step, step), step)
        qv = q_ref[rows, :]
        r = lax.rsqrt(_head_mean(qv * qv, same_ref) + EPS)
        dst[rows, :] = qv * r * (g2_ref[...] * scale)
        return carry

    lax.fori_loop(0, seq // step, chunk, 0)


def _attn_fwd(proj3, g_q2, g_k2, ride):
    _, n_seq, seq, _ = proj3.shape
    dms = [_distance_mats(d, seq // d) for d in DILATIONS]
    same = _same_head()
    col0 = 2 * D_CONV // LANES
    n_hp = D_ATT // LANES

    n_ride = len(ride)
    assert not any(scatter for _, scatter in ride), "the forward's ride is an all-gather"

    def body(*refs):
        q_ref, k_ref, v_ref, gq_ref, gk_ref, sl_ref, dm1, dm4, dm16, same_ref = refs[:10]
        ride_in = refs[10:10 + n_ride]
        y_ref, lse_ref = refs[10 + n_ride:12 + n_ride]
        ride_out = refs[12 + n_ride:12 + 2 * n_ride]
        (qf, kf, qp, kp, vp, oml_p, o1, o4, o16, m1, m4, m16, l1, l4, l16,
         tmp) = refs[12 + 2 * n_ride:28 + 2 * n_ride]
        o_nat, m_nat, l_nat = (o1, o4, o16), (m1, m4, m16), (l1, l4, l16)
        ride_args = (ride_in, ride_out) + tuple(refs[28 + 2 * n_ride:])
        step = pl.program_id(0) * n_hp + pl.program_id(1)
        n_steps = n_seq * n_hp

        if n_ride:
            for phase, at in enumerate((0, (3 * n_steps) // 4)):
                @pl.when(step == at)
                def _(phase=phase):
                    _gather_by_chip_phase(phase, *ride_args)

        dm_refs = (dm1, dm4, dm16)
        _qk_normalise(q_ref, gq_ref, same_ref, qf, seq, HEAD_DIM ** -0.5, ATT_ROWS)
        _qk_normalise(k_ref, gk_ref, same_ref, kf, seq, 1.0, ATT_ROWS)
        slopes = (sl_ref[0:1, 0:1], sl_ref[1:2, 0:1])
        for pi, dil in enumerate(DILATIONS):
            seg = seq // dil
            kw = min(2 * Q_BLOCK, seg)
            _permute_rows_by_head(qp, qf, dil, seq, tmp)
            _permute_rows(kp, kf, dil, seq, tmp)
            _permute_rows(vp, v_ref, dil, seq, tmp)

            def blk(it, carry, seg=seg, kw=kw, pi=pi, dst=oml_p):
                first = _first_head(Q_BLOCK)
                chains = [(sub, h) for sub in range(ATT_FWD_UNROLL) for h in range(2)]
                win = [_window(it * ATT_FWD_UNROLL + sub, seg) for sub in range(ATT_FWD_UNROLL)]
                s = {}
                for sub, h in chains:
                    i0, ks, var = win[sub]
                    s[sub, h] = lax.dot_general(qp[h, pl.ds(i0, Q_BLOCK), :], kp[pl.ds(ks, kw), :], NT,
                                                preferred_element_type=F32) - slopes[h] * dm_refs[pi][var]
                m, l, p = {}, {}, {}
                for c in chains:
                    m[c] = jnp.max(s[c], axis=1, keepdims=True)
                    e = jnp.exp(s[c] - m[c])
                    l[c] = jnp.sum(e, axis=1, keepdims=True)
                    p[c] = e.astype(BF16)
                o = {}
                for sub, h in chains:
                    o[sub, h] = jnp.dot(p[sub, h], vp[pl.ds(win[sub][1], kw), :], preferred_element_type=F32)
                packed = [jnp.concatenate([jnp.where(first, t[sub, 0], t[sub, 1]) for t in (o, m, l)], axis=1)
                          for sub in range(ATT_FWD_UNROLL)]
                span = ATT_FWD_UNROLL * Q_BLOCK
                dst[pl.ds(pl.multiple_of(it * span, span), span), :] = jnp.concatenate(packed, axis=0)
                return carry

            lax.fori_loop(0, seq // (Q_BLOCK * ATT_FWD_UNROLL), blk, 0)
            for n, nat in enumerate((o_nat[pi], m_nat[pi], l_nat[pi])):
                _scatter_segments(nat, lambda start, size, n=n: oml_p[pl.ds(start, size), pl.ds(n * LANES, LANES)],
                                  dil, seq, tmp, accumulate=False)

        def merge(ci, carry):
            rows = pl.ds(pl.multiple_of(ci * ATT_ROWS, ATT_ROWS), ATT_ROWS)
            ms = [m_nat[pi][rows, :] for pi in range(3)]
            m_all = jnp.maximum(jnp.maximum(ms[0], ms[1]), ms[2])
            es = [jnp.exp(m - m_all) for m in ms]
            l_all = sum(l_nat[pi][rows, :] * es[pi] for pi in range(3))
            inv = 1.0 / l_all
            o = sum(o_nat[pi][rows, :] * (es[pi] * inv) for pi in range(3))
            y_ref[rows, :] = o.astype(BF16)
            lse_ref[rows, :] = m_all + jnp.log(l_all)
            return carry

        lax.fori_loop(0, seq // ATT_ROWS, merge, 0)

        if n_ride:
            @pl.when(step == n_steps - 1)
            def _():
                _gather_by_chip_phase(2, *ride_args)

    def col(off):
        return pl.BlockSpec((None, None, seq, LANES), lambda b, hp: (col0 + off * n_hp + hp, b, 0, 0))

    def whole(arr):
        return pl.BlockSpec(arr.shape, lambda b, hp: (0,) * arr.ndim)

    rows_f32 = pltpu.VMEM((seq, LANES), F32)
    rows_bf16 = pltpu.VMEM((seq, LANES), BF16)
    return _call(
        body, name="attn_fwd", grid=(n_seq, n_hp),
        in_specs=[col(0), col(1), col(2), whole(g_q2), whole(g_k2),
                  pl.BlockSpec((None, 8, LANES), lambda b, hp: (hp, 0, 0)),
                  whole(dms[0]), whole(dms[1]), whole(dms[2]), whole(same)] + [ANY_SPEC] * n_ride,
        out_specs=[pl.BlockSpec((None, seq, LANES), lambda b, hp: (b, 0, hp)),
                   pl.BlockSpec((None, seq, LANES), lambda b, hp: (b, 0, hp))] + [ANY_SPEC] * n_ride,
        out_shape=[_sds((n_seq, seq, D_ATT), BF16), _sds((n_seq, seq, D_ATT), F32)] + _exchange_shapes(ride),
        scratch_shapes=[rows_f32, rows_f32, pltpu.VMEM((2, seq, LANES), BF16), rows_bf16, rows_bf16]
        + [pltpu.VMEM((seq, 3 * LANES), F32)] + [rows_f32] * 10 + _exchange_sems(n_ride),
        compiler_params=_params(("arbitrary", "arbitrary")),
    )(proj3, proj3, proj3, g_q2, g_k2, _alibi_rows(), *dms, same, *[a for a, _ in ride])


def _attn_bwd(proj3, do3, y_att3, lse3, g_q2, g_k2, ride):
    _, n_seq, seq, _ = proj3.shape
    dms = [_distance_mats(d, seq // d) for d in DILATIONS]
    same = _same_head()
    col0 = 2 * D_CONV // LANES
    n_hp = D_ATT // LANES

    n_ride = len(ride)
    ride_scatter = [s for _, s in ride]

    def body(*refs):
        (q_ref, k_ref, v_ref, do_ref, o_ref, lse_ref, gq_ref, gk_ref, sl_ref, dm1, dm4, dm16,
         same_ref) = refs[:13]
        ride_in = refs[13:13 + n_ride]
        dq_ref, dk_ref, dv_ref, dg_ref = refs[13 + n_ride:17 + n_ride]
        ride_out = refs[17 + n_ride:17 + 2 * n_ride]
        (qf, kf, qp, dop, kp, vp, sn, sp, dqp, dkp, dvp, dqn, dkn, dvn,
         tmp) = refs[17 + 2 * n_ride:32 + 2 * n_ride]
        ride_args = (ride_scatter, ride_in, ride_out) + tuple(refs[32 + 2 * n_ride:])
        dm_refs = (dm1, dm4, dm16)
        step = pl.program_id(0) * n_hp + pl.program_id(1)

        @pl.when(step == 0)
        def _():
            _exchange_start(*ride_args)
            dg_ref[...] = jnp.zeros_like(dg_ref)

        _qk_normalise(q_ref, gq_ref, same_ref, qf, seq, HEAD_DIM ** -0.5, ATT_BWD_ROWS)
        _qk_normalise(k_ref, gk_ref, same_ref, kf, seq, 1.0, ATT_BWD_ROWS)

        def stats(ci, carry):
            rows = pl.ds(pl.multiple_of(ci * ATT_BWD_ROWS, ATT_BWD_ROWS), ATT_BWD_ROWS)
            first = _first_head(ATT_BWD_ROWS)
            sn[0, rows, :], sn[1, rows, :] = _per_head(lse_ref[rows, :], first)
            prod = do_ref[rows, :] * o_ref[rows, :].astype(F32)
            sn[2, rows, :], sn[3, rows, :] = _per_head(_head_sum(prod, same_ref), first)
            return carry

        lax.fori_loop(0, seq // ATT_BWD_ROWS, stats, 0)
        slopes = (sl_ref[0:1, 0:1], sl_ref[1:2, 0:1])
        half = seq // (Q_BLOCK * ATT_UNROLL)
        region = seq // ATT_UNROLL

        for pi, dil in enumerate(DILATIONS):
            seg = seq // dil
            kw = min(2 * Q_BLOCK, seg)
            _permute_rows_by_head(qp, qf, dil, seq, tmp)
            _permute_rows_by_head(dop, do_ref, dil, seq, tmp)
            _permute_rows(kp, kf, dil, seq, tmp)
            _permute_rows(vp, v_ref, dil, seq, tmp)
            if dil == 1:
                st = sn
            else:
                staged = dil > STRIDE
                assert not staged or DILATIONS[pi - 1] == STRIDE
                st, earlier = (sn, sp) if staged else (sp, sn)
                for n in range(4):
                    _permute_rows(st.at[n], earlier.at[n], dil, seq, tmp, staged)
            def touched(sub, seg=seg):
                lo, hi = sub * region, (sub + 1) * region
                if seg < region:
                    return lo, hi
                seg0 = lo // seg * seg
                return max(lo - RADIUS, seg0), min(hi + RADIUS, seg0 + seg)

            def summed(acc, start, size, touched=touched):
                pieces = []
                for c0 in range(start, start + size, RADIUS):
                    owners = [s for s in range(ATT_UNROLL) if touched(s)[0] <= c0 and c0 + RADIUS <= touched(s)[1]]
                    if pieces and pieces[-1][2] == owners:
                        pieces[-1][1] += RADIUS
                    else:
                        pieces.append([c0, RADIUS, owners])
                vals = [sum(acc[o, pl.ds(c0, n), :] for o in owners) for c0, n, owners in pieces]
                return vals[0] if len(vals) == 1 else jnp.concatenate(vals, axis=0)

            for sub in range(ATT_UNROLL):
                lo, hi = touched(sub)
                dkp[sub, pl.ds(lo, hi - lo), :] = jnp.zeros((hi - lo, LANES), F32)
                dvp[sub, pl.ds(lo, hi - lo), :] = jnp.zeros((hi - lo, LANES), F32)

            def blk(it, carry, seg=seg, kw=kw, pi=pi, st=st):
                first = _first_head(Q_BLOCK)
                chains = [(sub, h) for sub in range(ATT_UNROLL) for h in range(2)]
                win = [_window(it + sub * half, seg) for sub in range(ATT_UNROLL)]
                qrows = [pl.ds(w[0], Q_BLOCK) for w in win]
                krows = [pl.ds(w[1], kw) for w in win]

                def over_keys(n, sub):
                    t = st[n, qrows[sub], :]
                    return t if kw == LANES else jnp.concatenate([t] * (kw // LANES), axis=1)

                s, dp = {}, {}
                for sub, h in chains:
                    s[sub, h] = lax.dot_general(qp[h, qrows[sub], :], kp[krows[sub], :], NT,
                                                preferred_element_type=F32) - slopes[h] * dm_refs[pi][win[sub][2]]
                    dp[sub, h] = lax.dot_general(dop[h, qrows[sub], :], vp[krows[sub], :], NT,
                                                 preferred_element_type=F32)
                p, ds = {}, {}
                for sub, h in chains:
                    e = jnp.exp(s[sub, h] - over_keys(h, sub))
                    ds[sub, h] = (e * (dp[sub, h] - over_keys(2 + h, sub))).astype(BF16)
                    p[sub, h] = e.astype(BF16)
                dq, dk, dv = {}, {}, {}
                for sub, h in chains:
                    dq[sub, h] = jnp.dot(ds[sub, h], kp[krows[sub], :], preferred_element_type=F32)
                    dk[sub, h] = lax.dot_general(ds[sub, h], qp[h, qrows[sub], :], TN, preferred_element_type=F32)
                    dv[sub, h] = lax.dot_general(p[sub, h], dop[h, qrows[sub], :], TN, preferred_element_type=F32)
                dq_dst = dqn if pi == 0 else dqp
                for sub in range(ATT_UNROLL):
                    dq_dst[qrows[sub], :] = jnp.where(first, dq[sub, 0], dq[sub, 1])
                    dkp[sub, krows[sub], :] += dk[sub, 0] + dk[sub, 1]
                    dvp[sub, krows[sub], :] += dv[sub, 0] + dv[sub, 1]
                return carry

            lax.fori_loop(0, half, blk, 0)
            first_pattern = pi == 0
            if first_pattern:
                for r0 in range(0, seq, region):
                    rows = pl.ds(r0, region)
                    dkn[rows, :] = summed(dkp, r0, region)
                    dvn[rows, :] = summed(dvp, r0, region)
            else:
                _scatter_segments(dqn, lambda start, size: dqp[pl.ds(start, size), :], dil, seq, tmp, accumulate=True)
                for nat, acc in ((dkn, dkp), (dvn, dvp)):
                    _scatter_segments(nat, lambda start, size, acc=acc: summed(acc, start, size),
                                      dil, seq, tmp, accumulate=True)

        def finish(ci, carry):
            rows = pl.ds(pl.multiple_of(ci * ATT_BWD_ROWS, ATT_BWD_ROWS), ATT_BWD_ROWS)
            for src_ref, g_ref, dn, dst_ref, scale, row in (
                    (q_ref, gq_ref, dqn, dq_ref, HEAD_DIM ** -0.5, 0), (k_ref, gk_ref, dkn, dk_ref, 1.0, 1)):
                xv = src_ref[rows, :]
                r = lax.rsqrt(_head_mean(xv * xv, same_ref) + EPS)
                xhat = xv * r
                d = dn[rows, :] * scale
                dg_ref[row:row + 1, :] += jnp.sum(d * xhat, axis=0, keepdims=True)
                dxh = d * g_ref[...]
                dst_ref[rows, :] = (r * (dxh - xhat * _head_mean(dxh * xhat, same_ref))).astype(BF16)
            dv_ref[rows, :] = dvn[rows, :].astype(BF16)
            return carry

        lax.fori_loop(0, seq // ATT_BWD_ROWS, finish, 0)

        @pl.when(step == n_seq * n_hp - 1)
        def _():
            _exchange_wait(*ride_args)

    def col(off):
        return pl.BlockSpec((None, None, seq, LANES), lambda b, hp: (col0 + off * n_hp + hp, b, 0, 0))

    def whole(arr):
        return pl.BlockSpec(arr.shape, lambda b, hp: (0,) * arr.ndim)

    att = pl.BlockSpec((None, seq, LANES), lambda b, hp: (b, 0, hp))
    rows_f32 = pltpu.VMEM((seq, LANES), F32)
    rows_bf16 = pltpu.VMEM((seq, LANES), BF16)
    by_head_bf16 = pltpu.VMEM((2, seq, LANES), BF16)
    per_sub_f32 = pltpu.VMEM((ATT_UNROLL, seq, LANES), F32)
    stats_f32 = pltpu.VMEM((4, seq, LANES), F32)
    return _call(
        body, name="attn_bwd", grid=(n_seq, n_hp),
        in_specs=[col(0), col(1), col(2), att, att, att, whole(g_q2), whole(g_k2),
                  pl.BlockSpec((None, 8, LANES), lambda b, hp: (hp, 0, 0)),
                  whole(dms[0]), whole(dms[1]), whole(dms[2]), whole(same)] + [ANY_SPEC] * n_ride,
        out_specs=[att, att, att, pl.BlockSpec((8, LANES), lambda b, hp: (0, 0))] + [ANY_SPEC] * n_ride,
        out_shape=[_sds((n_seq, seq, D_ATT), BF16)] * 3 + [_sds((8, LANES), F32)] + _exchange_shapes(ride),
        scratch_shapes=[rows_f32, rows_f32, by_head_bf16, by_head_bf16, rows_bf16, rows_bf16, stats_f32, stats_f32,
                        rows_f32, per_sub_f32, per_sub_f32, rows_f32, rows_f32, rows_f32, rows_f32]
        + _exchange_sems(n_ride),
        compiler_params=_params(("arbitrary", "arbitrary")),
    )(proj3, proj3, proj3, do3, y_att3, lse3, g_q2, g_k2, _alibi_rows(), *dms, same, *[a for a, _ in ride])


def _mix_out(uc2, y_att2, x2, mod8, g_ln, b_ln, g_ffn, w_out, seq, tm=512):
    tokens = x2.shape[0]
    per_seq = seq // tm

    def body(uc_ref, ya_ref, x_ref, m_ref, gl_ref, bl_ref, gf_ref, w_ref, yc_ref, mix_ref, x1_ref, h2_ref):
        uc = uc_ref[...]
        mu = jnp.mean(uc, axis=-1, keepdims=True)
        cen = uc - mu
        rs = lax.rsqrt(jnp.mean(cen * cen, axis=-1, keepdims=True) + EPS)
        z = cen * rs * gl_ref[...] + bl_ref[...]
        yc = (z * _sig(z)).astype(BF16)
        yc_ref[...] = yc
        mix = (jnp.dot(yc, w_ref[pl.ds(0, D_CONV), :], preferred_element_type=F32)
               + jnp.dot(ya_ref[...], w_ref[pl.ds(D_CONV, D_ATT), :], preferred_element_type=F32))
        mix_ref[...] = mix.astype(BF16)
        x1 = x_ref[...] + m_ref[2:3, :] * mix
        x1_ref[...] = x1
        r = lax.rsqrt(jnp.mean(x1 * x1, axis=-1, keepdims=True) + EPS)
        h2_ref[...] = ((x1 * r * gf_ref[...]) * (1.0 + m_ref[4:5, :]) + m_ref[3:4, :]).astype(BF16)

    def rows(width):
        return pl.BlockSpec((tm, width), lambda i: (i, 0))

    def vec(width):
        return pl.BlockSpec((1, width), lambda i: (0, 0))

    return _call(
        body, name="mix_out", grid=(tokens // tm,),
        in_specs=[rows(D_CONV), rows(D_ATT), rows(D_MODEL),
                  pl.BlockSpec((None, 8, D_MODEL), lambda i: (i // per_seq, 0, 0)),
                  vec(D_CONV), vec(D_CONV), vec(D_MODEL),
                  pl.BlockSpec((D_MODEL, D_MODEL), lambda i: (0, 0))],
        out_specs=[rows(D_CONV), rows(D_MODEL), rows(D_MODEL), rows(D_MODEL)],
        out_shape=[_sds((tokens, D_CONV), BF16), _sds((tokens, D_MODEL), BF16),
                   _sds((tokens, D_MODEL), F32), _sds((tokens, D_MODEL), BF16)],
        compiler_params=_params(("parallel",)),
    )(uc2, y_att2, x2, mod8, g_ln, b_ln, g_ffn, w_out)


def _mix_out_bwd(dmix, uc2, g_ln, b_ln, w_out, tm=512):
    tokens = dmix.shape[0]

    def body(dm_ref, uc_ref, gl_ref, bl_ref, w_ref, duc_ref, do_ref, dgb_ref):
        @pl.when(pl.program_id(0) == 0)
        def _():
            dgb_ref[...] = jnp.zeros_like(dgb_ref)

        dmv = dm_ref[...]
        dyc = lax.dot_general(dmv, w_ref[pl.ds(0, D_CONV), :], NT, preferred_element_type=F32)
        do_ref[...] = lax.dot_general(dmv, w_ref[pl.ds(D_CONV, D_ATT), :], NT, preferred_element_type=F32)
        uc = uc_ref[...]
        mu = jnp.mean(uc, axis=-1, keepdims=True)
        cen = uc - mu
        rs = lax.rsqrt(jnp.mean(cen * cen, axis=-1, keepdims=True) + EPS)
        xh = cen * rs
        z = xh * gl_ref[...] + bl_ref[...]
        sg = _sig(z)
        dz = dyc * (sg * (1.0 + z * (1.0 - sg)))
        dgb_ref[0:1, :] += jnp.sum(dz * xh, axis=0, keepdims=True)
        dgb_ref[1:2, :] += jnp.sum(dz, axis=0, keepdims=True)
        dxh = dz * gl_ref[...]
        duc_ref[...] = rs * (dxh - jnp.mean(dxh, axis=-1, keepdims=True)
                             - xh * jnp.mean(dxh * xh, axis=-1, keepdims=True))

    return _call(
        body, name="mix_out_bwd", grid=(tokens // tm,),
        in_specs=[pl.BlockSpec((tm, D_MODEL), lambda i: (i, 0)),
                  pl.BlockSpec((tm, D_CONV), lambda i: (i, 0)),
                  pl.BlockSpec((1, D_CONV), lambda i: (0, 0)),
                  pl.BlockSpec((1, D_CONV), lambda i: (0, 0)),
                  pl.BlockSpec((D_MODEL, D_MODEL), lambda i: (0, 0))],
        out_specs=[pl.BlockSpec((tm, D_CONV), lambda i: (i, 0)),
                   pl.BlockSpec((tm, D_ATT), lambda i: (i, 0)),
                   pl.BlockSpec((8, D_CONV), lambda i: (0, 0))],
        out_shape=[_sds((tokens, D_CONV), F32), _sds((tokens, D_ATT), F32), _sds((8, D_CONV), F32)],
        compiler_params=_params(("arbitrary",)),
    )(dmix, uc2, g_ln, b_ln, w_out)


FF_TILE = 256
FF_TILES = D_FF // FF_TILE


def _load_once(hbm_refs, vmem_refs, sems):
    @pl.when(pl.program_id(0) == 0)
    def _():
        copies = [pltpu.make_async_copy(src, dst, sems.at[n]) for n, (src, dst) in enumerate(zip(hbm_refs, vmem_refs))]
        for cp in copies:
            cp.start()
        for cp in copies:
            cp.wait()


def _ffn_fwd(h2, w_gate_t, w_up_t, w_down, x1, target, mod8, seq, tm=512):
    tokens = h2.shape[0]
    per_seq = seq // tm
    n_seq = tokens // seq

    def body(h_ref, m_ref, x1_ref, t_ref, wg_hbm, wu_hbm, wd_hbm, gate_ref, up_ref, dy_ref, df_ref, sq_ref, dgf_ref,
             wg, wu, wd, w_sems):
        i = pl.program_id(0)
        _load_once((wg_hbm, wu_hbm, wd_hbm), (wg, wu, wd), w_sems)

        @pl.when(i == 0)
        def _():
            sq_ref[...] = jnp.zeros_like(sq_ref)

        @pl.when(i % per_seq == 0)
        def _():
            dgf_ref[...] = jnp.zeros_like(dgf_ref)

        hv = h_ref[...]

        def gate_up(t):
            rows = pl.ds(t * FF_TILE, FF_TILE)
            return (lax.dot_general(hv, wg[rows, :], NT, preferred_element_type=F32),
                    lax.dot_general(hv, wu[rows, :], NT, preferred_element_type=F32))

        fv = jnp.zeros((tm, D_MODEL), F32)
        ahead = gate_up(0)
        for t in range(FF_TILES):
            gate, up = ahead
            if t + 1 < FF_TILES:
                ahead = gate_up(t + 1)
            gate_ref[t] = gate.astype(BF16)
            up_ref[t] = up.astype(BF16)
            act = (gate * _sig(gate) * up).astype(BF16)
            fv = fv + jnp.dot(act, wd[pl.ds(t * FF_TILE, FF_TILE), :], preferred_element_type=F32)
        gate_f = m_ref[5:6, :]
        diff = x1_ref[...] + gate_f * fv - t_ref[...]
        sq_ref[0:1, :] += jnp.sum(diff * diff, axis=0, keepdims=True)
        dy = diff * (1.0 / D_MODEL)
        dy_ref[...] = dy
        df_ref[...] = (gate_f * dy).astype(BF16)
        dgf_ref[0:1, :] += jnp.sum(dy * fv, axis=0, keepdims=True)

    rows_spec = pl.BlockSpec((tm, D_MODEL), lambda i: (i, 0))
    per = pl.BlockSpec((None, 8, D_MODEL), lambda i: (i // per_seq, 0, 0))
    tiles = pl.BlockSpec((FF_TILES, tm, FF_TILE), lambda i: (0, i, 0))
    weight = pltpu.VMEM((D_FF, D_MODEL), BF16)
    return _call(
        body, name="ffn_fwd", grid=(tokens // tm,),
        in_specs=[rows_spec, per, rows_spec, rows_spec, ANY_SPEC, ANY_SPEC, ANY_SPEC],
        out_specs=[tiles, tiles, rows_spec, rows_spec, pl.BlockSpec((8, D_MODEL), lambda i: (0, 0)), per],
        out_shape=[_sds((FF_TILES, tokens, FF_TILE), BF16), _sds((FF_TILES, tokens, FF_TILE), BF16),
                   _sds((tokens, D_MODEL), F32), _sds((tokens, D_MODEL), BF16),
                   _sds((8, D_MODEL), F32), _sds((n_seq, 8, D_MODEL), F32)],
        scratch_shapes=[weight, weight, weight, pltpu.SemaphoreType.DMA((3,))],
        compiler_params=_params(("arbitrary",)),
    )(h2, mod8, x1, target, w_gate_t, w_up_t, w_down)


def _ffn_bwd(df, gate, up, w_gate_t, w_up_t, w_down, x1, dy, mix, mod8, g_ffn, seq, tm=256):
    tokens = df.shape[0]
    per_seq = seq // tm
    n_seq = tokens // seq

    def body(df_ref, gate_ref, up_ref, m_ref, g_ref, x1_ref, dy_ref, mix_ref, wg_hbm, wu_hbm, wd_hbm,
             dgate_ref, dup_ref, act_ref, dx1_ref, dmix_ref, dg_ref, dm_ref, wg, wu, wd, w_sems):
        i = pl.program_id(0)
        _load_once((wg_hbm, wu_hbm, wd_hbm), (wg, wu, wd), w_sems)

        @pl.when(i == 0)
        def _():
            dg_ref[...] = jnp.zeros_like(dg_ref)

        @pl.when(i % per_seq == 0)
        def _():
            dm_ref[...] = jnp.zeros_like(dm_ref)

        dfv = df_ref[...]

        def d_act(t):
            return lax.dot_general(dfv, wd[pl.ds(t * FF_TILE, FF_TILE), :], NT, preferred_element_type=F32)

        dh = jnp.zeros((tm, D_MODEL), F32)
        ahead = d_act(0)
        for t in range(FF_TILES):
            dact = ahead
            if t + 1 < FF_TILES:
                ahead = d_act(t + 1)
            rows = pl.ds(t * FF_TILE, FF_TILE)
            gv = gate_ref[t].astype(F32)
            uv = up_ref[t].astype(F32)
            sg = _sig(gv)
            silu = gv * sg
            act_ref[t] = (silu * uv).astype(BF16)
            dup = (dact * silu).astype(BF16)
            dgate = (dact * uv * (sg * (1.0 + gv * (1.0 - sg)))).astype(BF16)
            dup_ref[t] = dup
            dgate_ref[t] = dgate
            dh = dh + (jnp.dot(dgate, wg[rows, :], preferred_element_type=F32)
                       + jnp.dot(dup, wu[rows, :], preferred_element_type=F32))
        g = g_ref[...]
        x1v = x1_ref[...]
        rs = lax.rsqrt(jnp.mean(x1v * x1v, axis=-1, keepdims=True) + EPS)
        xhat = x1v * rs
        dm_ref[0:1, :] += jnp.sum(dh, axis=0, keepdims=True)
        dm_ref[1:2, :] += jnp.sum(dh * (xhat * g), axis=0, keepdims=True)
        dn = dh * (1.0 + m_ref[4:5, :])
        dg_ref[0:1, :] += jnp.sum(dn * xhat, axis=0, keepdims=True)
        dxh = dn * g
        dx1 = dy_ref[...] + rs * (dxh - xhat * jnp.mean(dxh * xhat, axis=-1, keepdims=True))
        dx1_ref[...] = dx1
        dm_ref[2:3, :] += jnp.sum(dx1 * mix_ref[...].astype(F32), axis=0, keepdims=True)
        dmix_ref[...] = (m_ref[2:3, :] * dx1).astype(BF16)

    rows_spec = pl.BlockSpec((tm, D_MODEL), lambda i: (i, 0))
    per = pl.BlockSpec((None, 8, D_MODEL), lambda i: (i // per_seq, 0, 0))
    tiles = pl.BlockSpec((FF_TILES, tm, FF_TILE), lambda i: (0, i, 0))
    weight = pltpu.VMEM((D_FF, D_MODEL), BF16)
    return _call(
        body, name="ffn_bwd", grid=(tokens // tm,),
        in_specs=[rows_spec, tiles, tiles, per, pl.BlockSpec((1, D_MODEL), lambda i: (0, 0)),
                  rows_spec, rows_spec, rows_spec, ANY_SPEC, ANY_SPEC, ANY_SPEC],
        out_specs=[tiles, tiles, tiles, rows_spec, rows_spec, pl.BlockSpec((8, D_MODEL), lambda i: (0, 0)), per],
        out_shape=[_sds((FF_TILES, tokens, FF_TILE), BF16)] * 3
        + [_sds((tokens, D_MODEL), F32), _sds((tokens, D_MODEL), BF16),
           _sds((8, D_MODEL), F32), _sds((n_seq, 8, D_MODEL), F32)],
        scratch_shapes=[weight, weight, weight, pltpu.SemaphoreType.DMA((3,))],
        compiler_params=_params(("arbitrary",)),
    )(df, gate, up, mod8, g_ffn, x1, dy, mix, w_gate_t, w_up_t, w_down)


def _mix_in_bwd(d_a, d_g, d_q, d_k, d_v, w_in, x2, dx1, mod8, g_mix, seq, ride, tm=512):
    tokens = x2.shape[0]
    per_seq = seq // tm
    n_seq = tokens // seq
    parts = (d_a, d_g, d_q, d_k, d_v)
    width = D_CONV
    n_ride = len(ride)
    ride_scatter = [s for _, s in ride]

    def body(*refs):
        da_ref, dg_ref, dq_ref, dk_ref, dv_ref, w_ref, x_ref, dx1_ref, m_ref, g_ref = refs[:10]
        ride_in = refs[10:10 + n_ride]
        gx_ref, dgm_ref, dm_ref = refs[10 + n_ride:13 + n_ride]
        ride_args = (ride_scatter, ride_in, refs[13 + n_ride:13 + 2 * n_ride]) + tuple(refs[13 + 2 * n_ride:])
        i = pl.program_id(0)

        @pl.when(i == 0)
        def _():
            _exchange_start(*ride_args)
            dgm_ref[...] = jnp.zeros_like(dgm_ref)

        @pl.when(i % per_seq == 0)
        def _():
            dm_ref[...] = jnp.zeros_like(dm_ref)

        dh = jnp.zeros((tm, D_MODEL), F32)
        for n, ref in enumerate((da_ref, dg_ref, dq_ref, dk_ref, dv_ref)):
            dh = dh + jnp.dot(ref[...], w_ref[pl.ds(n * width, width), :], preferred_element_type=F32)
        xv = x_ref[...]
        r = lax.rsqrt(jnp.mean(xv * xv, axis=-1, keepdims=True) + EPS)
        xhat = xv * r
        g = g_ref[...]
        dm_ref[0:1, :] += jnp.sum(dh, axis=0, keepdims=True)
        dm_ref[1:2, :] += jnp.sum(dh * (xhat * g), axis=0, keepdims=True)
        dn = dh * (1.0 + m_ref[1:2, :])
        dgm_ref[0:1, :] += jnp.sum(dn * xhat, axis=0, keepdims=True)
        dxh = dn * g
        gx_ref[...] = dx1_ref[...] + r * (dxh - xhat * jnp.mean(dxh * xhat, axis=-1, keepdims=True))

        @pl.when(i == tokens // tm - 1)
        def _():
            _exchange_wait(*ride_args)

    rows = pl.BlockSpec((tm, D_MODEL), lambda i: (i, 0))
    half = pl.BlockSpec((tm, width), lambda i: (i, 0))
    per = pl.BlockSpec((None, 8, D_MODEL), lambda i: (i // per_seq, 0, 0))
    return _call(
        body, name="mix_in_bwd", grid=(tokens // tm,),
        in_specs=[half] * 5 + [pl.BlockSpec((D_IN, D_MODEL), lambda i: (0, 0)), rows, rows, per,
                               pl.BlockSpec((1, D_MODEL), lambda i: (0, 0))] + [ANY_SPEC] * n_ride,
        out_specs=[rows, pl.BlockSpec((8, D_MODEL), lambda i: (0, 0)), per] + [ANY_SPEC] * n_ride,
        out_shape=[_sds((tokens, D_MODEL), F32), _sds((8, D_MODEL), F32), _sds((n_seq, 8, D_MODEL), F32)]
        + _exchange_shapes(ride),
        scratch_shapes=_exchange_sems(n_ride),
        compiler_params=_params(("arbitrary",)),
    )(*parts, w_in, x2, dx1, mod8, g_mix, *[a for a, _ in ride])


def _grad_matmul_parts(a_parts, b_parts, name, tk=1024):
    tokens = a_parts[0].shape[0]
    na, nb = len(a_parts), len(b_parts)
    ma, nbw = a_parts[0].shape[1], b_parts[0].shape[1]

    n_k = tokens // tk

    def body(*refs):
        a_refs, b_refs, o_ref, acc = refs[:na], refs[na:na + nb], refs[na + nb], refs[na + nb + 1]

        @pl.when(pl.program_id(0) == 0)
        def _():
            acc[...] = jnp.zeros_like(acc)

        for i in range(na):
            for j in range(nb):
                acc[pl.ds(i * ma, ma), pl.ds(j * nbw, nbw)] += lax.dot_general(
                    a_refs[i][...], b_refs[j][...], TN, preferred_element_type=F32)

        @pl.when(pl.program_id(0) == n_k - 1)
        def _():
            o_ref[...] = acc[...].astype(o_ref.dtype)

    return _call(
        body, name=name, grid=(n_k,),
        in_specs=[pl.BlockSpec((tk, ma), lambda k: (k, 0))] * na + [pl.BlockSpec((tk, nbw), lambda k: (k, 0))] * nb,
        out_specs=pl.BlockSpec((na * ma, nb * nbw), lambda k: (0, 0)),
        out_shape=_sds((na * ma, nb * nbw), BF16),
        scratch_shapes=[pltpu.VMEM((na * ma, nb * nbw), F32)],
        compiler_params=_params(("arbitrary",)),
    )(*a_parts, *b_parts)


def _grad_matmul_tiles(a, b, name, tk=1024):
    tiled_b = b.ndim == 3
    tiles, tokens, width = b.shape if tiled_b else a.shape
    other = a.shape[1] if tiled_b else b.shape[1]
    out_tile = (other, width) if tiled_b else (width, other)
    n_k = tokens // tk

    def body(a_ref, b_ref, o_ref, acc):
        @pl.when(pl.program_id(0) == 0)
        def _():
            acc[...] = jnp.zeros_like(acc)

        for t in range(tiles):
            lhs = a_ref[...] if tiled_b else a_ref[t]
            rhs = b_ref[t] if tiled_b else b_ref[...]
            acc[t] += lax.dot_general(lhs, rhs, TN, preferred_element_type=F32)

        @pl.when(pl.program_id(0) == n_k - 1)
        def _():
            o_ref[...] = acc[...].astype(o_ref.dtype)

    flat = pl.BlockSpec((tk, other), lambda k: (k, 0))
    tiled = pl.BlockSpec((tiles, tk, width), lambda k: (0, k, 0))
    return _call(
        body, name=name, grid=(n_k,),
        in_specs=[flat, tiled] if tiled_b else [tiled, flat],
        out_specs=pl.BlockSpec((tiles,) + out_tile, lambda k: (0, 0, 0)),
        out_shape=_sds((tiles,) + out_tile, BF16),
        scratch_shapes=[pltpu.VMEM((tiles,) + out_tile, F32)],
        compiler_params=_params(("arbitrary",)),
    )(a, b)


def _adamw(w, m, v, g, name, n_parts=0, tr=256):
    rows, cols = w.shape
    tr = min(tr, rows)
    c1 = 1.0 - ADAM_B1 ** ADAM_STEP
    c2 = 1.0 - ADAM_B2 ** ADAM_STEP

    def body(w_ref, m_ref, v_ref, g_ref, go_ref, d_ref, mo_ref, vo_ref):
        if n_parts:
            gv = g_ref[0].astype(F32)
            for p in range(1, n_parts):
                gv = gv + g_ref[p].astype(F32)
        else:
            gv = g_ref[...]
        go_ref[...] = gv
        mn = ADAM_B1 * m_ref[...] + (1.0 - ADAM_B1) * gv
        vn = ADAM_B2 * v_ref[...] + (1.0 - ADAM_B2) * (gv * gv)
        mo_ref[...] = mn
        vo_ref[...] = vn
        d_ref[...] = -ADAM_LR * ((mn / c1) / (jnp.sqrt(vn / c2) + ADAM_EPS) + ADAM_WD * w_ref[...])

    blk = pl.BlockSpec((tr, cols), lambda i: (i, 0))
    g_spec = pl.BlockSpec((n_parts, tr, cols), lambda i: (0, i, 0)) if n_parts else blk
    return _call(
        body, name=name, grid=(rows // tr,),
        in_specs=[blk, blk, blk, g_spec], out_specs=[blk] * 4,
        out_shape=[_sds((rows, cols), F32)] * 4,
        compiler_params=_params(("parallel",)),
    )(w, m, v, g)


def _cols_to_full(blocks):
    n, r, c = blocks.shape
    return jnp.transpose(blocks, (1, 0, 2)).reshape(r, n * c)


def _pad_lanes(v, width):
    return jnp.pad(v, ((0, 0), (0, width - v.shape[1])))


def kernel(x, c, w_ada, b_ada, g_mix, w_in, w_dw, b_dw, g_conv_ln, b_conv_ln, g_q, g_k, w_out, g_ffn, w_gate, w_up, w_down, loss_target, m_w_ada, m_b_ada, m_g_mix, m_w_in, m_w_dw, m_b_dw, m_g_conv_ln, m_b_conv_ln, m_g_q, m_g_k, m_w_out, m_g_ffn, m_w_gate, m_w_up, m_w_down, v_w_ada, v_b_ada, v_g_mix, v_w_in, v_w_dw, v_b_dw, v_g_conv_ln, v_b_conv_ln, v_g_q, v_g_k, v_w_out, v_g_ffn, v_w_gate, v_w_up, v_w_down):
    n_seq, seq, _ = x.shape
    tokens = n_seq * seq
    me = 4 * lax.axis_index("x") + 2 * lax.axis_index("y") + lax.axis_index("c")
    ada_cols = w_ada.shape[2]
    dw_cols = w_dw.shape[2]

    def transposed(w):
        return jnp.transpose(w[0])

    (c_g, w_in_g, w_dw_g) = _gather_by_chip([c, transposed(w_in).astype(BF16), w_dw[0]], "gather_weights")
    c_all = c_g.reshape(N_DEV * n_seq, D_MODEL)
    w_in_t = w_in_g.reshape(D_IN, D_MODEL)
    w_dw_f = _cols_to_full(w_dw_g)

    b_cols = lax.dynamic_slice(b_ada, (0, me * ada_cols), (1, ada_cols))
    mod_cols = _ada_fwd(c_all, w_ada[0], b_cols)
    (mod_g,) = _exchange([(mod_cols, False)], "gather_mod")
    mod_mine = lax.dynamic_slice(mod_g, (0, me * n_seq, 0), (N_DEV, n_seq, ada_cols))
    mod = jnp.transpose(mod_mine, (1, 0, 2)).reshape(n_seq, N_MOD, D_MODEL)
    mod8 = jnp.pad(mod, ((0, 0), (0, 8 - N_MOD), (0, 0)))

    x2 = x.reshape(tokens, D_MODEL)
    h1, proj = _mix_in(x2, mod8, g_mix, w_in_t, seq)
    proj3 = proj.reshape(D_IN // LANES, n_seq, seq, LANES)
    uc3 = _conv_fwd(proj3, w_dw_f, b_dw)
    g_q2, g_k2 = jnp.tile(g_q, (1, 2)), jnp.tile(g_k, (1, 2))
    y_att3, lse3, w_out_g, w_gate_g, w_up_g, w_down_g = _attn_fwd(
        proj3, g_q2, g_k2,
        [(w_out[0].astype(BF16), False), (transposed(w_gate).astype(BF16), False),
         (transposed(w_up).astype(BF16), False), (w_down[0].astype(BF16), False)])
    w_out_f = w_out_g.reshape(D_MODEL, D_MODEL)
    w_gate_f = w_gate_g.reshape(D_FF, D_MODEL)
    w_up_f = w_up_g.reshape(D_FF, D_MODEL)
    w_down_f = w_down_g.reshape(D_FF, D_MODEL)
    uc2 = uc3.reshape(tokens, D_CONV)
    y_att2 = y_att3.reshape(tokens, D_ATT)
    y_conv, mix, x1, h2 = _mix_out(uc2, y_att2, x2, mod8, g_conv_ln, b_conv_ln, g_ffn, w_out_f, seq)
    gate, up, dy, df, sq, dgate_f = _ffn_fwd(
        h2, w_gate_f, w_up_f, w_down_f, x1, loss_target.reshape(tokens, D_MODEL), mod8, seq)

    dgate, dup, act, dx1, dmix, dg_ffn, dmod_f = _ffn_bwd(
        df, gate, up, w_gate_f, w_up_f, w_down_f, x1, dy, mix, mod8, g_ffn, seq)
    duc2, do2, dgb_ln = _mix_out_bwd(dmix, uc2, g_conv_ln, b_conv_ln, w_out_f)
    d_a3, d_g3, dw_dw_p, db_dw_p = _conv_bwd(duc2.reshape(n_seq, seq, D_CONV), proj3, w_dw_f)
    gw_gate = _grad_matmul_tiles(dgate, h2, "grad_w_gate")
    gw_up = _grad_matmul_tiles(dup, h2, "grad_w_up")
    gw_down = _grad_matmul_tiles(act, df, "grad_w_down")
    gw_out = _grad_matmul_parts([y_conv, y_att2], [dmix], "grad_w_out")
    d_q3, d_k3, d_v3, dg_qk, p_gate, p_up, p_down, p_out = _attn_bwd(
        proj3, do2.reshape(n_seq, seq, D_ATT), y_att3, lse3, g_q2, g_k2,
        [(gw_gate.reshape(N_DEV, D_FF // N_DEV, D_MODEL), True), (gw_up.reshape(N_DEV, D_FF // N_DEV, D_MODEL), True),
         (gw_down.reshape(N_DEV, D_FF // N_DEV, D_MODEL), True),
         (gw_out.reshape(N_DEV, D_MODEL // N_DEV, D_MODEL), True)])
    flat = lambda t: t.reshape(tokens, t.shape[-1])
    d_a, d_g, d_q, d_k, d_v = flat(d_a3), flat(d_g3), flat(d_q3), flat(d_k3), flat(d_v3)
    gw_in = _grad_matmul_parts([d_a, d_g, d_q, d_k, d_v], [h1], "grad_w_in")
    grad_x2, dg_mix, dmod_m, p_in = _mix_in_bwd(
        d_a, d_g, d_q, d_k, d_v, w_in_t, x2, dx1, mod8, g_mix, seq,
        [(gw_in.reshape(N_DEV, D_IN // N_DEV, D_MODEL), True)])

    dmod = jnp.concatenate([dmod_m[:, 0], dmod_m[:, 1], dmod_f[:, 2], dmod_f[:, 0], dmod_f[:, 1], dgate_f[:, 0]], axis=1)
    dg_q = dg_qk[0:1, 0:HEAD_DIM] + dg_qk[0:1, HEAD_DIM:]
    dg_k = dg_qk[1:2, 0:HEAD_DIM] + dg_qk[1:2, HEAD_DIM:]
    loss_part = (0.5 / D_MODEL) * jnp.sum(sq[0:1, :], axis=1, keepdims=True)
    small = jnp.concatenate(
        [dg_mix[0:1], dg_ffn[0:1], db_dw_p[0:1], dgb_ln[0:1], dgb_ln[1:2],
         _pad_lanes(dg_q, LANES), _pad_lanes(dg_k, LANES), _pad_lanes(loss_part, LANES)], axis=1)
    n_small = small.shape[1] - LANES

    (dmod_g, small_g, dw_g) = _exchange([(dmod, False), (small, False), (dw_dw_p, False)], "gather_small_grads")

    dmod_all = dmod_g.reshape(N_DEV * n_seq, N_MOD * D_MODEL)
    dmod_cols = lax.dynamic_slice(dmod_all, (0, me * ada_cols), (N_DEV * n_seq, ada_cols))
    gw_ada, gb_ada = _ada_bwd(c_all, dmod_cols, dmod_all)

    res = {}
    res["w_ada"] = _adamw(w_ada[0], m_w_ada[0], v_w_ada[0], gw_ada, "adamw_w_ada")
    res["b_ada"] = _adamw(b_ada, m_b_ada, v_b_ada, gb_ada, "adamw_b_ada")
    def adamw_transposed(w, m, v, parts, name, tr):
        outs = _adamw(transposed(w), transposed(m), transposed(v), parts, name, N_DEV, tr=tr)
        return tuple(jnp.transpose(o) for o in outs)

    res["w_in"] = adamw_transposed(w_in, m_w_in, v_w_in, p_in, "adamw_w_in", 160)
    res["w_out"] = _adamw(w_out[0], m_w_out[0], v_w_out[0], p_out, "adamw_w_out", N_DEV)
    res["w_gate"] = adamw_transposed(w_gate, m_w_gate, v_w_gate, p_gate, "adamw_w_gate", 176)
    res["w_up"] = adamw_transposed(w_up, m_w_up, v_w_up, p_up, "adamw_w_up", 176)
    res["w_down"] = _adamw(w_down[0], m_w_down[0], v_w_down[0], p_down, "adamw_w_down", N_DEV, tr=176)
    dw_mine = lax.dynamic_slice(dw_g, (0, 0, me * dw_cols), (N_DEV, CONV_WIDTH, dw_cols))
    res["w_dw"] = _adamw(w_dw[0], m_w_dw[0], v_w_dw[0], dw_mine, "adamw_w_dw", N_DEV)

    small_names = ["g_mix", "g_ffn", "b_dw", "g_conv_ln", "b_conv_ln", "g_q", "g_k"]
    small_w = {"g_mix": (g_mix, m_g_mix, v_g_mix), "g_ffn": (g_ffn, m_g_ffn, v_g_ffn), "b_dw": (b_dw, m_b_dw, v_b_dw),
               "g_conv_ln": (g_conv_ln, m_g_conv_ln, v_g_conv_ln), "b_conv_ln": (b_conv_ln, m_b_conv_ln, v_b_conv_ln),
               "g_q": (g_q, m_g_q, v_g_q), "g_k": (g_k, m_g_k, v_g_k)}
    widths = [max(small_w[n][0].shape[1], LANES) for n in small_names]
    packed = [jnp.concatenate([_pad_lanes(small_w[n][i], wd) for n, wd in zip(small_names, widths)], axis=1) for i in range(3)]
    outs = _adamw(packed[0], packed[1], packed[2], small_g[:, :, :n_small], "adamw_small", N_DEV)
    off = 0
    for n, wd in zip(small_names, widths):
        real = small_w[n][0].shape[1]
        res[n] = tuple(o[:, off:off + real] for o in outs)
        off += wd
    loss = jnp.sum(small_g[:, 0, n_small])

    order = ["w_ada", "b_ada", "g_mix", "w_in", "w_dw", "b_dw", "g_conv_ln", "b_conv_ln", "g_q", "g_k",
             "w_out", "g_ffn", "w_gate", "w_up", "w_down"]
    lead = {"w_ada", "w_in", "w_dw", "w_out", "w_gate", "w_up", "w_down"}
    grads, deltas, new_m, new_v = [], [], [], []
    for n in order:
        g, d, mn, vn = res[n]
        g, d, mn, vn = (t[None] if n in lead else t for t in (g, d, mn, vn))
        grads.append(g)
        deltas.append(d)
        new_m.append(mn)
        new_v.append(vn)
    return (loss, grad_x2.reshape(n_seq, seq, D_MODEL), *grads, *deltas, *new_m, *new_v)
```

```python
import numpy as np
import jax
import jax.numpy as jnp
from jax import lax
from jax.experimental import pallas as pl
from jax.experimental.pallas import tpu as pltpu

F32 = jnp.float32
BF16 = jnp.bfloat16

N_DEV = 8
D_MODEL = 1024
D_CONV = 512
D_ATT = 512
HEAD_DIM = 64
CONV_WIDTH = 31
D_IN = 2 * D_CONV + 3 * D_ATT
D_FF = 2816
N_MOD = 6
EPS = 1e-6
RADIUS = 64
DILATIONS = (1, 4, 16)
Q_BLOCK = 128
LANES = 128
VMEM_LIMIT = 56 * 1024 * 1024

ADAM_LR = 0.001
ADAM_B1 = 0.9
ADAM_B2 = 0.999
ADAM_EPS = 1e-08
ADAM_WD = 0.01
ADAM_STEP = 10

NT = (((1,), (1,)), ((), ()))
TN = (((0,), (0,)), ((), ()))


def _call(body, **kw):
    return pl.pallas_call(body, **kw)


def _params(sem=None, vmem=VMEM_LIMIT):
    return pltpu.CompilerParams(dimension_semantics=sem, vmem_limit_bytes=vmem)


def _sig(x):
    return 1.0 / (1.0 + jnp.exp(-x))


def _sds(shape, dtype):
    return jax.ShapeDtypeStruct(shape, dtype)


N_PEER = N_DEV - 1
ANY_SPEC = pl.BlockSpec(memory_space=pl.ANY)


def _exchange_copies(scatter, ins, outs, *sems):
    n = len(ins)
    if n == 0:
        return [], []
    send_sems, recv_sems, local_sems = sems
    x, y, c = lax.axis_index("x"), lax.axis_index("y"), lax.axis_index("c")
    me = 4 * x + 2 * y + c

    def src(a, slot):
        return ins[a].at[slot] if scatter[a] else ins[a]

    local = [pltpu.make_async_copy(src(a, me), outs[a].at[me], local_sems.at[a]) for a in range(n)]
    flights = []
    for k in range(1, N_DEV):
        px = 1 - x if k & 4 else x
        py = 1 - y if k & 2 else y
        pc = 1 - c if k & 1 else c
        pid = 4 * px + 2 * py + pc
        for a in range(n):
            i = a * N_PEER + k - 1
            send, recv = (pltpu.make_async_remote_copy(
                src_ref=src(a, pid), dst_ref=outs[a].at[slot],
                send_sem=send_sems.at[i], recv_sem=recv_sems.at[i],
                device_id=(px, py, pc), device_id_type=pl.DeviceIdType.MESH) for slot in (me, pid))
            flights.append((send, recv))
    return local, flights


def _exchange_start(*args):
    local, flights = _exchange_copies(*args)
    for cp in local:
        cp.start()
    for send, _ in flights:
        send.start()


def _exchange_wait(*args):
    local, flights = _exchange_copies(*args)
    for send, recv in flights:
        send.wait_send()
        recv.wait_recv()
    for cp in local:
        cp.wait()


def _exchange_shapes(items):
    return [_sds((N_DEV,) + tuple(arr.shape[1:] if scatter else arr.shape), arr.dtype) for arr, scatter in items]


def _exchange_sems(n):
    if n == 0:
        return []
    return [pltpu.SemaphoreType.DMA((n * N_PEER,)), pltpu.SemaphoreType.DMA((n * N_PEER,)),
            pltpu.SemaphoreType.DMA((n,))]


def _gather_by_chip_phase(phase, ins, outs, send_sems, recv_sems, local_sems):
    n = len(ins)
    per = N_PEER
    x, y, c = lax.axis_index("x"), lax.axis_index("y"), lax.axis_index("c")
    me, sibling = (x, y, c), (x, y, 1 - c)
    chips = [(1 - x, y), (x, 1 - y), (1 - x, 1 - y)]

    def slot(px, py, pc):
        return 4 * px + 2 * py + pc

    def copy(a, k, block, to, src=None):
        dst = outs[a].at[slot(*block)]
        return pltpu.make_async_remote_copy(
            src_ref=dst if src is None else src, dst_ref=dst,
            send_sem=send_sems.at[a * per + k], recv_sem=recv_sems.at[a * per + k],
            device_id=to, device_id_type=pl.DeviceIdType.MESH)

    local = [pltpu.make_async_copy(ins[a], outs[a].at[slot(*me)], local_sems.at[a]) for a in range(n)]
    first = []
    for a in range(n):
        first.append(copy(a, 0, me, sibling, src=ins[a]))
        first += [copy(a, 1 + j, me, (*chip, c), src=ins[a]) for j, chip in enumerate(chips)]
    passed = [copy(a, 4 + j, (*chip, c), sibling) for j, chip in enumerate(chips) for a in range(n)]
    if phase == 0:
        for cp in local + first:
            cp.start()
    elif phase == 1:
        for j, chip in enumerate(chips):
            for a in range(n):
                copy(a, 1 + j, (*chip, c), me).wait_recv()
        for cp in passed:
            cp.start()
    else:
        for a in range(n):
            copy(a, 0, sibling, me).wait_recv()
            for j, chip in enumerate(chips):
                copy(a, 4 + j, (*chip, 1 - c), me).wait_recv()
        for cp in first + passed:
            cp.wait_send()
        for cp in local:
            cp.wait()


def _gather_by_chip(arrays, name):
    n = len(arrays)

    def body(*refs):
        for phase in range(3):
            _gather_by_chip_phase(phase, refs[:n], refs[n:2 * n], *refs[2 * n:])

    return _call(
        body, name=name, out_shape=_exchange_shapes([(arr, False) for arr in arrays]),
        in_specs=[ANY_SPEC] * n, out_specs=[ANY_SPEC] * n, scratch_shapes=_exchange_sems(n),
    )(*arrays)


def _exchange(items, name):
    n = len(items)
    scatter = [s for _, s in items]

    def body(*refs):
        args = (scatter, refs[:n], refs[n:2 * n]) + tuple(refs[2 * n:])
        _exchange_start(*args)
        _exchange_wait(*args)

    return _call(
        body, name=name, out_shape=_exchange_shapes(items),
        in_specs=[ANY_SPEC] * n, out_specs=[ANY_SPEC] * n, scratch_shapes=_exchange_sems(n),
    )(*[a for a, _ in items])


def _ada_fwd(c_all, w_ada, b_cols):
    def body(c_ref, w_ref, b_ref, o_ref):
        cv = c_ref[...]
        sc = (cv * _sig(cv)).astype(BF16)
        o_ref[...] = jnp.dot(sc, w_ref[...].astype(BF16), preferred_element_type=F32) + b_ref[...]

    return _call(body, name="ada_fwd", out_shape=_sds((c_all.shape[0], w_ada.shape[1]), F32),
                 compiler_params=_params())(c_all, w_ada, b_cols)


def _ada_bwd(c_all, dmod_cols, dmod_all):
    def body(c_ref, dc_ref, da_ref, gw_ref, gb_ref):
        cv = c_ref[...]
        sc = (cv * _sig(cv)).astype(BF16)
        gw_ref[...] = lax.dot_general(sc, dc_ref[...].astype(BF16), TN, preferred_element_type=F32)
        gb_ref[...] = jnp.sum(da_ref[...], axis=0, keepdims=True)

    return _call(body, name="ada_bwd",
                 out_shape=[_sds((c_all.shape[1], dmod_cols.shape[1]), F32), _sds((1, dmod_all.shape[1]), F32)],
                 compiler_params=_params())(c_all, dmod_cols, dmod_all)


MIX_ROWS = 128


def _mix_in(x2, mod8, g_mix, w_in, seq, tm=512):
    tokens = x2.shape[0]
    per_seq = seq // tm

    def body(x_ref, m_ref, g_ref, wt_ref, h_ref, p_ref, w_ref):
        @pl.when(pl.program_id(0) == 0)
        def _():
            w_ref[...] = wt_ref[...].T

        def normed(c):
            rows = pl.ds(c * MIX_ROWS, MIX_ROWS)
            xv = x_ref[rows, :]
            r = lax.rsqrt(jnp.mean(xv * xv, axis=-1, keepdims=True) + EPS)
            hb = ((xv * r * g_ref[...]) * (1.0 + m_ref[1:2, :]) + m_ref[0:1, :]).astype(BF16)
            h_ref[rows, :] = hb
            return hb

        ahead = normed(0)
        for c in range(tm // MIX_ROWS):
            hb = ahead
            if c + 1 < tm // MIX_ROWS:
                ahead = normed(c + 1)
            p = jnp.dot(hb, w_ref[...], preferred_element_type=F32)
            for cb in range(D_IN // LANES):
                p_ref[cb, pl.ds(c * MIX_ROWS, MIX_ROWS), :] = p[:, cb * LANES:(cb + 1) * LANES]

    return _call(
        body, name="mix_in", grid=(tokens // tm,),
        in_specs=[pl.BlockSpec((tm, D_MODEL), lambda i: (i, 0)),
                  pl.BlockSpec((None, 8, D_MODEL), lambda i: (i // per_seq, 0, 0)),
                  pl.BlockSpec((1, D_MODEL), lambda i: (0, 0)),
                  pl.BlockSpec((D_IN, D_MODEL), lambda i: (0, 0))],
        out_specs=[pl.BlockSpec((tm, D_MODEL), lambda i: (i, 0)),
                   pl.BlockSpec((D_IN // LANES, tm, LANES), lambda i: (0, i, 0))],
        out_shape=[_sds((tokens, D_MODEL), BF16), _sds((D_IN // LANES, tokens, LANES), F32)],
        scratch_shapes=[pltpu.VMEM((D_MODEL, D_IN), BF16)],
        compiler_params=_params(("arbitrary",)),
    )(x2, mod8, g_mix, w_in)


CONV_ROWS = 128
CONV_DW_ROWS = 32
CONV_DW_UNROLL = 8
CONV_HALO = 16


def _fill_shifted(xp, sh, seq):
    for b in range(8):
        sh[b, pl.ds(0, seq + 24), :] = xp[pl.ds(b, seq + 24), :]


def _conv_fwd(proj3, w_dw, b_dw):
    _, n_seq, seq, _ = proj3.shape
    n_cb = D_CONV // LANES

    def body(a_ref, g_ref, w_ref, b_ref, uc_ref, xp, sh):
        zeros = jnp.zeros((CONV_HALO, LANES), F32)
        xp[pl.ds(0, CONV_HALO), :] = zeros
        xp[pl.ds(CONV_HALO + seq, CONV_HALO), :] = zeros
        xp[pl.ds(CONV_HALO, seq), :] = a_ref[...] * _sig(g_ref[...])
        _fill_shifted(xp, sh, seq)

        def blk(i, carry):
            t0 = pl.multiple_of(i * CONV_ROWS, CONV_ROWS)
            acc = jnp.zeros((CONV_ROWS, LANES), F32)
            for j in range(CONV_WIDTH):
                jj = j + 1
                acc = acc + sh[jj % 8, pl.ds(t0 + 8 * (jj // 8), CONV_ROWS), :] * w_ref[j:j + 1, :]
            uc_ref[pl.ds(t0, CONV_ROWS), :] = acc + b_ref[...]
            return carry

        lax.fori_loop(0, seq // CONV_ROWS, blk, 0)

    return _call(
        body, name="conv_fwd", grid=(n_seq, n_cb),
        in_specs=[pl.BlockSpec((None, None, seq, LANES), lambda b, cb: (cb, b, 0, 0)),
                  pl.BlockSpec((None, None, seq, LANES), lambda b, cb: (n_cb + cb, b, 0, 0)),
                  pl.BlockSpec((CONV_WIDTH, LANES), lambda b, cb: (0, cb)),
                  pl.BlockSpec((1, LANES), lambda b, cb: (0, cb))],
        out_specs=pl.BlockSpec((None, seq, LANES), lambda b, cb: (b, 0, cb)),
        out_shape=_sds((n_seq, seq, D_CONV), F32),
        scratch_shapes=[pltpu.VMEM((seq + 2 * CONV_HALO, LANES), F32),
                        pltpu.VMEM((8, seq + 2 * CONV_HALO, LANES), F32)],
        compiler_params=_params(("parallel", "parallel")),
    )(proj3, proj3, w_dw, b_dw)


def _conv_bwd(duc3, proj3, w_dw):
    _, n_seq, seq, _ = proj3.shape
    n_cb = D_CONV // LANES

    def body(duc_ref, a_ref, g_ref, w_ref, da_ref, dg_ref, dw_ref, db_ref, xp, sh):
        @pl.when(pl.program_id(1) == 0)
        def _():
            dw_ref[...] = jnp.zeros_like(dw_ref)
            db_ref[...] = jnp.zeros_like(db_ref)

        zeros = jnp.zeros((CONV_HALO, LANES), F32)
        xp[pl.ds(0, CONV_HALO), :] = zeros
        xp[pl.ds(CONV_HALO + seq, CONV_HALO), :] = zeros
        xp[pl.ds(CONV_HALO, seq), :] = a_ref[...] * _sig(g_ref[...])
        _fill_shifted(xp, sh, seq)
        for j0 in range(0, CONV_WIDTH, 8):
            taps = range(j0, min(j0 + 8, CONV_WIDTH))

            def wblk(i, accs, taps=taps):
                for u in range(CONV_DW_UNROLL):
                    t0 = pl.multiple_of((i * CONV_DW_UNROLL + u) * CONV_DW_ROWS, CONV_DW_ROWS)
                    d = duc_ref[pl.ds(t0, CONV_DW_ROWS), :]
                    accs = tuple(acc + d * sh[(j + 1) % 8, pl.ds(t0 + 8 * ((j + 1) // 8), CONV_DW_ROWS), :]
                                 for acc, j in zip(accs, taps))
                return accs

            accs = lax.fori_loop(0, seq // (CONV_DW_ROWS * CONV_DW_UNROLL), wblk,
                                 tuple(jnp.zeros((CONV_DW_ROWS, LANES), F32) for _ in taps))
            for acc, j in zip(accs, taps):
                dw_ref[j:j + 1, :] += jnp.sum(acc, axis=0, keepdims=True)
        db_ref[0:1, :] += jnp.sum(duc_ref[...], axis=0, keepdims=True)
        xp[pl.ds(CONV_HALO, seq), :] = duc_ref[...]
        _fill_shifted(xp, sh, seq)

        def ublk(i, carry):
            t0 = pl.multiple_of(i * CONV_ROWS, CONV_ROWS)
            acc = jnp.zeros((CONV_ROWS, LANES), F32)
            for j in range(CONV_WIDTH):
                jj = CONV_WIDTH - j
                acc = acc + sh[jj % 8, pl.ds(t0 + 8 * (jj // 8), CONV_ROWS), :] * w_ref[j:j + 1, :]
            av = a_ref[pl.ds(t0, CONV_ROWS), :]
            sg = _sig(g_ref[pl.ds(t0, CONV_ROWS), :])
            da_ref[pl.ds(t0, CONV_ROWS), :] = (acc * sg).astype(BF16)
            dg_ref[pl.ds(t0, CONV_ROWS), :] = (acc * av * sg * (1.0 - sg)).astype(BF16)
            return carry

        lax.fori_loop(0, seq // CONV_ROWS, ublk, 0)

    return _call(
        body, name="conv_bwd", grid=(n_cb, n_seq),
        in_specs=[pl.BlockSpec((None, seq, LANES), lambda cb, b: (b, 0, cb)),
                  pl.BlockSpec((None, None, seq, LANES), lambda cb, b: (cb, b, 0, 0)),
                  pl.BlockSpec((None, None, seq, LANES), lambda cb, b: (n_cb + cb, b, 0, 0)),
                  pl.BlockSpec((CONV_WIDTH, LANES), lambda cb, b: (0, cb))],
        out_specs=[pl.BlockSpec((None, seq, LANES), lambda cb, b: (b, 0, cb)),
                   pl.BlockSpec((None, seq, LANES), lambda cb, b: (b, 0, cb)),
                   pl.BlockSpec((32, LANES), lambda cb, b: (0, cb)),
                   pl.BlockSpec((8, LANES), lambda cb, b: (0, cb))],
        out_shape=[_sds((n_seq, seq, D_CONV), BF16), _sds((n_seq, seq, D_CONV), BF16),
                   _sds((32, D_CONV), F32), _sds((8, D_CONV), F32)],
        scratch_shapes=[pltpu.VMEM((seq + 2 * CONV_HALO, LANES), F32),
                        pltpu.VMEM((8, seq + 2 * CONV_HALO, LANES), F32)],
        compiler_params=_params(("parallel", "arbitrary")),
    )(duc3, proj3, proj3, w_dw)


MASKED = 1e30
ATT_ROWS = 1024
ATT_BWD_ROWS = 2048
ATT_UNROLL = 8
ATT_FWD_UNROLL = 8


def _distance_mats(dil, seg_len):
    kw = min(2 * Q_BLOCK, seg_len)
    offsets = (0, -RADIUS, -2 * RADIUS) if kw == 2 * Q_BLOCK else (0,)
    a = np.arange(Q_BLOCK)[:, None]
    b = np.arange(kw)[None, :]
    mats = []
    for off in offsets:
        rel = np.abs(b + off - a)
        mats.append(np.where(rel <= RADIUS, dil * rel, MASKED))
    return jnp.asarray(np.stack(mats).astype(np.float32))


def _alibi_rows():
    s = np.zeros((4, 8, LANES), np.float32)
    for hp in range(4):
        for hl in range(2):
            s[hp, hl, :] = 2.0 ** (-(2 * hp + hl + 1))
    return jnp.asarray(s)


def _window(n, seg_len):
    i0 = pl.multiple_of(n * Q_BLOCK, Q_BLOCK)
    if seg_len <= Q_BLOCK:
        return i0, i0, 0
    per_seg = seg_len // Q_BLOCK
    j = n % per_seg
    seg0 = (n // per_seg) * seg_len
    ks_local = jnp.clip(j * Q_BLOCK - RADIUS, 0, seg_len - 2 * Q_BLOCK)
    ks = pl.multiple_of(seg0 + ks_local, RADIUS)
    var = jnp.where(j == 0, 0, jnp.where(j == per_seg - 1, 2, 1))
    return i0, ks, var


def _first_head(rows):
    return lax.broadcasted_iota(jnp.int32, (rows, LANES), 1) < HEAD_DIM


def _same_head():
    head = np.arange(LANES) // HEAD_DIM
    return jnp.asarray((head[:, None] == head[None, :]).astype(np.float32)).astype(BF16)


def _head_sum(x, same_ref):
    return jnp.dot(x.astype(BF16), same_ref[...], preferred_element_type=F32)


def _head_mean(x, same_ref):
    return _head_sum(x, same_ref) * (1.0 / HEAD_DIM)


def _per_head(x, first):
    swapped = pltpu.roll(x, HEAD_DIM, 1)
    return jnp.where(first, x, swapped), jnp.where(first, swapped, x)


STRIDE = 4


def _gather_segments(src, dil, seq, tmp, put, staged=False):
    if dil == 1:
        put(0, seq, src[pl.ds(0, seq), :])
    elif dil == STRIDE:
        seg = seq // dil
        for r in range(dil):
            put(r * seg, seg, src[pl.ds(r, seg, stride=dil), :])
    else:
        part, seg = seq // STRIDE, seq // dil
        if staged:
            tmp = src
        else:
            for b in range(STRIDE):
                tmp[pl.ds(b * part, part), :] = src[pl.ds(b, part, stride=STRIDE), :]
        for b in range(STRIDE):
            for a in range(dil // STRIDE):
                put(b * part + a * seg, seg, tmp[pl.ds(b * part + a, seg, stride=dil // STRIDE), :])


def _scatter_segments(dst, get, dil, seq, tmp, accumulate):
    def write(rows, val):
        if accumulate:
            dst[rows, :] += val
        else:
            dst[rows, :] = val

    if dil == 1:
        write(pl.ds(0, seq), get(0, seq))
    elif dil == STRIDE:
        seg = seq // dil
        for r in range(dil):
            write(pl.ds(r, seg, stride=dil), get(r * seg, seg))
    else:
        part, seg = seq // STRIDE, seq // dil
        for b in range(STRIDE):
            for a in range(dil // STRIDE):
                tmp[pl.ds(b * part + a, seg, stride=dil // STRIDE), :] = get(b * part + a * seg, seg)
        for b in range(STRIDE):
            write(pl.ds(b, part, stride=STRIDE), tmp[pl.ds(b * part, part), :])


def _permute_rows(dst, src, dil, seq, tmp, staged=False):
    def put(start, size, val):
        dst[pl.ds(start, size), :] = val.astype(dst.dtype)

    _gather_segments(src, dil, seq, tmp, put, staged)


def _permute_rows_by_head(dst, src, dil, seq, tmp):
    def put(start, size, val):
        first = _first_head(size)
        dst[0, pl.ds(start, size), :] = jnp.where(first, val, 0.0).astype(dst.dtype)
        dst[1, pl.ds(start, size), :] = jnp.where(first, 0.0, val).astype(dst.dtype)

    _gather_segments(src, dil, seq, tmp, put)


def _qk_normalise(q_ref, g2_ref, same_ref, dst, seq, scale, step):
    def chunk(ci, carry):
        rows = pl.ds(pl.multiple_of(ci * step, step), step)
        qv = q_ref[rows, :]
        r = lax.rsqrt(_head_mean(qv * qv, same_ref) + EPS)
        dst[rows, :] = qv * r * (g2_ref[...] * scale)
        return carry

    lax.fori_loop(0, seq // step, chunk, 0)


def _attn_fwd(proj3, g_q2, g_k2, ride):
    _, n_seq, seq, _ = proj3.shape
    dms = [_distance_mats(d, seq // d) for d in DILATIONS]
    same = _same_head()
    col0 = 2 * D_CONV // LANES
    n_hp = D_ATT // LANES

    n_ride = len(ride)
    assert not any(scatter for _, scatter in ride), "the forward's ride is an all-gather"

    def body(*refs):
        q_ref, k_ref, v_ref, gq_ref, gk_ref, sl_ref, dm1, dm4, dm16, same_ref = refs[:10]
        ride_in = refs[10:10 + n_ride]
        y_ref, lse_ref = refs[10 + n_ride:12 + n_ride]
        ride_out = refs[12 + n_ride:12 + 2 * n_ride]
        (qf, kf, qp, kp, vp, oml_p, oml_1, o4, o16, m4, m16, l4, l16,
         tmp) = refs[12 + 2 * n_ride:26 + 2 * n_ride]
        natural = {1: (o4, m4, l4), 2: (o16, m16, l16)}
        ride_args = (ride_in, ride_out) + tuple(refs[26 + 2 * n_ride:])
        step = pl.program_id(0) * n_hp + pl.program_id(1)
        n_steps = n_seq * n_hp

        if n_ride:
            for phase, at in enumerate((0, (3 * n_steps) // 4)):
                @pl.when(step == at)
                def _(phase=phase):
                    _gather_by_chip_phase(phase, *ride_args)

        dm_refs = (dm1, dm4, dm16)
        _qk_normalise(q_ref, gq_ref, same_ref, qf, seq, HEAD_DIM ** -0.5, ATT_ROWS)
        _qk_normalise(k_ref, gk_ref, same_ref, kf, seq, 1.0, ATT_ROWS)
        slopes = (sl_ref[0:1, 0:1], sl_ref[1:2, 0:1])
        for pi, dil in enumerate(DILATIONS):
            seg = seq // dil
            kw = min(2 * Q_BLOCK, seg)
            _permute_rows_by_head(qp, qf, dil, seq, tmp)
            _permute_rows(kp, kf, dil, seq, tmp)
            _permute_rows(vp, v_ref, dil, seq, tmp)

            def blk(it, carry, seg=seg, kw=kw, pi=pi, dst=oml_1 if pi == 0 else oml_p):
                first = _first_head(Q_BLOCK)
                chains = [(sub, h) for sub in range(ATT_FWD_UNROLL) for h in range(2)]
                win = [_window(it * ATT_FWD_UNROLL + sub, seg) for sub in range(ATT_FWD_UNROLL)]
                s = {}
                for sub, h in chains:
                    i0, ks, var = win[sub]
                    s[sub, h] = lax.dot_general(qp[h, pl.ds(i0, Q_BLOCK), :], kp[pl.ds(ks, kw), :], NT,
                                                preferred_element_type=F32) - slopes[h] * dm_refs[pi][var]
                m, l, p = {}, {}, {}
                for c in chains:
                    m[c] = jnp.max(s[c], axis=1, keepdims=True)
                    e = jnp.exp(s[c] - m[c])
                    l[c] = jnp.sum(e, axis=1, keepdims=True)
                    p[c] = e.astype(BF16)
                o = {}
                for sub, h in chains:
                    o[sub, h] = jnp.dot(p[sub, h], vp[pl.ds(win[sub][1], kw), :], preferred_element_type=F32)
                packed = [jnp.concatenate([jnp.where(first, t[sub, 0], t[sub, 1]) for t in (o, m, l)], axis=1)
                          for sub in range(ATT_FWD_UNROLL)]
                span = ATT_FWD_UNROLL * Q_BLOCK
                dst[pl.ds(pl.multiple_of(it * span, span), span), :] = jnp.concatenate(packed, axis=0)
                return carry

            lax.fori_loop(0, seq // (Q_BLOCK * ATT_FWD_UNROLL), blk, 0)
            for n, nat in enumerate(natural.get(pi, ())):
                _scatter_segments(nat, lambda start, size, n=n: oml_p[pl.ds(start, size), pl.ds(n * LANES, LANES)],
                                  dil, seq, tmp, accumulate=False)

        def merge(ci, carry):
            rows = pl.ds(pl.multiple_of(ci * ATT_ROWS, ATT_ROWS), ATT_ROWS)

            def part(pi, n):
                return oml_1[rows, pl.ds(n * LANES, LANES)] if pi == 0 else natural[pi][n][rows, :]

            ms = [part(pi, 1) for pi in range(3)]
            m_all = jnp.maximum(jnp.maximum(ms[0], ms[1]), ms[2])
            es = [jnp.exp(m - m_all) for m in ms]
            l_all = sum(part(pi, 2) * es[pi] for pi in range(3))
            inv = 1.0 / l_all
            o = sum(part(pi, 0) * (es[pi] * inv) for pi in range(3))
            y_ref[rows, :] = o.astype(BF16)
            lse_ref[rows, :] = m_all + jnp.log(l_all)
            return carry

        lax.fori_loop(0, seq // ATT_ROWS, merge, 0)

        if n_ride:
            @pl.when(step == n_steps - 1)
            def _():
                _gather_by_chip_phase(2, *ride_args)

    def col(off):
        return pl.BlockSpec((None, None, seq, LANES), lambda b, hp: (col0 + off * n_hp + hp, b, 0, 0))

    def whole(arr):
        return pl.BlockSpec(arr.shape, lambda b, hp: (0,) * arr.ndim)

    rows_f32 = pltpu.VMEM((seq, LANES), F32)
    rows_bf16 = pltpu.VMEM((seq, LANES), BF16)
    return _call(
        body, name="attn_fwd", grid=(n_seq, n_hp),
        in_specs=[col(0), col(1), col(2), whole(g_q2), whole(g_k2),
                  pl.BlockSpec((None, 8, LANES), lambda b, hp: (hp, 0, 0)),
                  whole(dms[0]), whole(dms[1]), whole(dms[2]), whole(same)] + [ANY_SPEC] * n_ride,
        out_specs=[pl.BlockSpec((None, seq, LANES), lambda b, hp: (b, 0, hp)),
                   pl.BlockSpec((None, seq, LANES), lambda b, hp: (b, 0, hp))] + [ANY_SPEC] * n_ride,
        out_shape=[_sds((n_seq, seq, D_ATT), BF16), _sds((n_seq, seq, D_ATT), F32)] + _exchange_shapes(ride),
        scratch_shapes=[rows_f32, rows_f32, pltpu.VMEM((2, seq, LANES), BF16), rows_bf16, rows_bf16]
        + [pltpu.VMEM((seq, 3 * LANES), F32)] * 2 + [rows_f32] * 7 + _exchange_sems(n_ride),
        compiler_params=_params(("arbitrary", "arbitrary")),
    )(proj3, proj3, proj3, g_q2, g_k2, _alibi_rows(), *dms, same, *[a for a, _ in ride])


def _attn_bwd(proj3, do3, y_att3, lse3, g_q2, g_k2, ride):
    _, n_seq, seq, _ = proj3.shape
    dms = [_distance_mats(d, seq // d) for d in DILATIONS]
    same = _same_head()
    col0 = 2 * D_CONV // LANES
    n_hp = D_ATT // LANES

    n_ride = len(ride)
    ride_scatter = [s for _, s in ride]

    def body(*refs):
        (q_ref, k_ref, v_ref, do_ref, o_ref, lse_ref, gq_ref, gk_ref, sl_ref, dm1, dm4, dm16,
         same_ref) = refs[:13]
        ride_in = refs[13:13 + n_ride]
        dq_ref, dk_ref, dv_ref, dg_ref = refs[13 + n_ride:17 + n_ride]
        ride_out = refs[17 + n_ride:17 + 2 * n_ride]
        (qf, kf, qp, dop, kp, vp, sn, sp, dqp, dkp, dvp, dqn, dkn, dvn,
         tmp) = refs[17 + 2 * n_ride:32 + 2 * n_ride]
        ride_args = (ride_scatter, ride_in, ride_out) + tuple(refs[32 + 2 * n_ride:])
        dm_refs = (dm1, dm4, dm16)
        step = pl.program_id(0) * n_hp + pl.program_id(1)

        @pl.when(step == 0)
        def _():
            _exchange_start(*ride_args)
            dg_ref[...] = jnp.zeros_like(dg_ref)

        _qk_normalise(q_ref, gq_ref, same_ref, qf, seq, HEAD_DIM ** -0.5, ATT_BWD_ROWS)
        _qk_normalise(k_ref, gk_ref, same_ref, kf, seq, 1.0, ATT_BWD_ROWS)

        def stats(ci, carry):
            rows = pl.ds(pl.multiple_of(ci * ATT_BWD_ROWS, ATT_BWD_ROWS), ATT_BWD_ROWS)
            first = _first_head(ATT_BWD_ROWS)
            sn[0, rows, :], sn[1, rows, :] = _per_head(lse_ref[rows, :], first)
            prod = do_ref[rows, :] * o_ref[rows, :].astype(F32)
            sn[2, rows, :], sn[3, rows, :] = _per_head(_head_sum(prod, same_ref), first)
            return carry

        lax.fori_loop(0, seq // ATT_BWD_ROWS, stats, 0)
        slopes = (sl_ref[0:1, 0:1], sl_ref[1:2, 0:1])
        half = seq // (Q_BLOCK * ATT_UNROLL)
        region = seq // ATT_UNROLL

        for pi, dil in enumerate(DILATIONS):
            seg = seq // dil
            kw = min(2 * Q_BLOCK, seg)
            _permute_rows_by_head(qp, qf, dil, seq, tmp)
            _permute_rows_by_head(dop, do_ref, dil, seq, tmp)
            _permute_rows(kp, kf, dil, seq, tmp)
            _permute_rows(vp, v_ref, dil, seq, tmp)
            if dil == 1:
                st = sn
            else:
                staged = dil > STRIDE
                assert not staged or DILATIONS[pi - 1] == STRIDE
                st, earlier = (sn, sp) if staged else (sp, sn)
                for n in range(4):
                    _permute_rows(st.at[n], earlier.at[n], dil, seq, tmp, staged)
            def touched(sub, seg=seg):
                lo, hi = sub * region, (sub + 1) * region
                if seg < region:
                    return lo, hi
                seg0 = lo // seg * seg
                return max(lo - RADIUS, seg0), min(hi + RADIUS, seg0 + seg)

            def summed(acc, start, size, touched=touched):
                pieces = []
                for c0 in range(start, start + size, RADIUS):
                    owners = [s for s in range(ATT_UNROLL) if touched(s)[0] <= c0 and c0 + RADIUS <= touched(s)[1]]
                    if pieces and pieces[-1][2] == owners:
                        pieces[-1][1] += RADIUS
                    else:
                        pieces.append([c0, RADIUS, owners])
                vals = [sum(acc[o, pl.ds(c0, n), :] for o in owners) for c0, n, owners in pieces]
                return vals[0] if len(vals) == 1 else jnp.concatenate(vals, axis=0)

            for sub in range(ATT_UNROLL):
                lo, hi = touched(sub)
                dkp[sub, pl.ds(lo, hi - lo), :] = jnp.zeros((hi - lo, LANES), F32)
                dvp[sub, pl.ds(lo, hi - lo), :] = jnp.zeros((hi - lo, LANES), F32)

            def blk(it, carry, seg=seg, kw=kw, pi=pi, st=st):
                first = _first_head(Q_BLOCK)
                chains = [(sub, h) for sub in range(ATT_UNROLL) for h in range(2)]
                win = [_window(it + sub * half, seg) for sub in range(ATT_UNROLL)]
                qrows = [pl.ds(w[0], Q_BLOCK) for w in win]
                krows = [pl.ds(w[1], kw) for w in win]

                def over_keys(n, sub):
                    t = st[n, qrows[sub], :]
                    return t if kw == LANES else jnp.concatenate([t] * (kw // LANES), axis=1)

                s, dp = {}, {}
                for sub, h in chains:
                    s[sub, h] = lax.dot_general(qp[h, qrows[sub], :], kp[krows[sub], :], NT,
                                                preferred_element_type=F32) - slopes[h] * dm_refs[pi][win[sub][2]]
                    dp[sub, h] = lax.dot_general(dop[h, qrows[sub], :], vp[krows[sub], :], NT,
                                                 preferred_element_type=F32)
                p, ds = {}, {}
                for sub, h in chains:
                    e = jnp.exp(s[sub, h] - over_keys(h, sub))
                    ds[sub, h] = (e * (dp[sub, h] - over_keys(2 + h, sub))).astype(BF16)
                    p[sub, h] = e.astype(BF16)
                dq, dk, dv = {}, {}, {}
                for sub, h in chains:
                    dq[sub, h] = jnp.dot(ds[sub, h], kp[krows[sub], :], preferred_element_type=F32)
                    dk[sub, h] = lax.dot_general(ds[sub, h], qp[h, qrows[sub], :], TN, preferred_element_type=F32)
                    dv[sub, h] = lax.dot_general(p[sub, h], dop[h, qrows[sub], :], TN, preferred_element_type=F32)
                dq_dst = dqn if pi == 0 else dqp
                for sub in range(ATT_UNROLL):
                    dq_dst[qrows[sub], :] = jnp.where(first, dq[sub, 0], dq[sub, 1])
                    dkp[sub, krows[sub], :] += dk[sub, 0] + dk[sub, 1]
                    dvp[sub, krows[sub], :] += dv[sub, 0] + dv[sub, 1]
                return carry

            lax.fori_loop(0, half, blk, 0)
            first_pattern = pi == 0
            if first_pattern:
                for r0 in range(0, seq, region):
                    rows = pl.ds(r0, region)
                    dkn[rows, :] = summed(dkp, r0, region)
                    dvn[rows, :] = summed(dvp, r0, region)
            else:
                _scatter_segments(dqn, lambda start, size: dqp[pl.ds(start, size), :], dil, seq, tmp, accumulate=True)
                for nat, acc in ((dkn, dkp), (dvn, dvp)):
                    _scatter_segments(nat, lambda start, size, acc=acc: summed(acc, start, size),
                                      dil, seq, tmp, accumulate=True)

        def finish(ci, carry):
            rows = pl.ds(pl.multiple_of(ci * ATT_BWD_ROWS, ATT_BWD_ROWS), ATT_BWD_ROWS)
            for src_ref, g_ref, dn, dst_ref, scale, row in (
                    (q_ref, gq_ref, dqn, dq_ref, HEAD_DIM ** -0.5, 0), (k_ref, gk_ref, dkn, dk_ref, 1.0, 1)):
                xv = src_ref[rows, :]
                r = lax.rsqrt(_head_mean(xv * xv, same_ref) + EPS)
                xhat = xv * r
                d = dn[rows, :] * scale
                dg_ref[row:row + 1, :] += jnp.sum(d * xhat, axis=0, keepdims=True)
                dxh = d * g_ref[...]
                dst_ref[rows, :] = (r * (dxh - xhat * _head_mean(dxh * xhat, same_ref))).astype(BF16)
            dv_ref[rows, :] = dvn[rows, :].astype(BF16)
            return carry

        lax.fori_loop(0, seq // ATT_BWD_ROWS, finish, 0)

        @pl.when(step == n_seq * n_hp - 1)
        def _():
            _exchange_wait(*ride_args)

    def col(off):
        return pl.BlockSpec((None, None, seq, LANES), lambda b, hp: (col0 + off * n_hp + hp, b, 0, 0))

    def whole(arr):
        return pl.BlockSpec(arr.shape, lambda b, hp: (0,) * arr.ndim)

    att = pl.BlockSpec((None, seq, LANES), lambda b, hp: (b, 0, hp))
    rows_f32 = pltpu.VMEM((seq, LANES), F32)
    rows_bf16 = pltpu.VMEM((seq, LANES), BF16)
    by_head_bf16 = pltpu.VMEM((2, seq, LANES), BF16)
    per_sub_f32 = pltpu.VMEM((ATT_UNROLL, seq, LANES), F32)
    stats_f32 = pltpu.VMEM((4, seq, LANES), F32)
    return _call(
        body, name="attn_bwd", grid=(n_seq, n_hp),
        in_specs=[col(0), col(1), col(2), att, att, att, whole(g_q2), whole(g_k2),
                  pl.BlockSpec((None, 8, LANES), lambda b, hp: (hp, 0, 0)),
                  whole(dms[0]), whole(dms[1]), whole(dms[2]), whole(same)] + [ANY_SPEC] * n_ride,
        out_specs=[att, att, att, pl.BlockSpec((8, LANES), lambda b, hp: (0, 0))] + [ANY_SPEC] * n_ride,
        out_shape=[_sds((n_seq, seq, D_ATT), BF16)] * 3 + [_sds((8, LANES), F32)] + _exchange_shapes(ride),
        scratch_shapes=[rows_f32, rows_f32, by_head_bf16, by_head_bf16, rows_bf16, rows_bf16, stats_f32, stats_f32,
                        rows_f32, per_sub_f32, per_sub_f32, rows_f32, rows_f32, rows_f32, rows_f32]
        + _exchange_sems(n_ride),
        compiler_params=_params(("arbitrary", "arbitrary")),
    )(proj3, proj3, proj3, do3, y_att3, lse3, g_q2, g_k2, _alibi_rows(), *dms, same, *[a for a, _ in ride])


def _mix_out(uc2, y_att2, x2, mod8, g_ln, b_ln, g_ffn, w_out, seq, tm=512):
    tokens = x2.shape[0]
    per_seq = seq // tm

    def body(uc_ref, ya_ref, x_ref, m_ref, gl_ref, bl_ref, gf_ref, w_ref, yc_ref, mix_ref, x1_ref, h2_ref):
        uc = uc_ref[...]
        mu = jnp.mean(uc, axis=-1, keepdims=True)
        cen = uc - mu
        rs = lax.rsqrt(jnp.mean(cen * cen, axis=-1, keepdims=True) + EPS)
        z = cen * rs * gl_ref[...] + bl_ref[...]
        yc = (z * _sig(z)).astype(BF16)
        yc_ref[...] = yc
        mix = (jnp.dot(yc, w_ref[pl.ds(0, D_CONV), :], preferred_element_type=F32)
               + jnp.dot(ya_ref[...], w_ref[pl.ds(D_CONV, D_ATT), :], preferred_element_type=F32))
        mix_ref[...] = mix.astype(BF16)
        x1 = x_ref[...] + m_ref[2:3, :] * mix
        x1_ref[...] = x1
        r = lax.rsqrt(jnp.mean(x1 * x1, axis=-1, keepdims=True) + EPS)
        h2_ref[...] = ((x1 * r * gf_ref[...]) * (1.0 + m_ref[4:5, :]) + m_ref[3:4, :]).astype(BF16)

    def rows(width):
        return pl.BlockSpec((tm, width), lambda i: (i, 0))

    def vec(width):
        return pl.BlockSpec((1, width), lambda i: (0, 0))

    return _call(
        body, name="mix_out", grid=(tokens // tm,),
        in_specs=[rows(D_CONV), rows(D_ATT), rows(D_MODEL),
                  pl.BlockSpec((None, 8, D_MODEL), lambda i: (i // per_seq, 0, 0)),
                  vec(D_CONV), vec(D_CONV), vec(D_MODEL),
                  pl.BlockSpec((D_MODEL, D_MODEL), lambda i: (0, 0))],
        out_specs=[rows(D_CONV), rows(D_MODEL), rows(D_MODEL), rows(D_MODEL)],
        out_shape=[_sds((tokens, D_CONV), BF16), _sds((tokens, D_MODEL), BF16),
                   _sds((tokens, D_MODEL), F32), _sds((tokens, D_MODEL), BF16)],
        compiler_params=_params(("parallel",)),
    )(uc2, y_att2, x2, mod8, g_ln, b_ln, g_ffn, w_out)


def _mix_out_bwd(dmix, uc2, g_ln, b_ln, w_out, tm=512):
    tokens = dmix.shape[0]

    def body(dm_ref, uc_ref, gl_ref, bl_ref, w_ref, duc_ref, do_ref, dgb_ref):
        @pl.when(pl.program_id(0) == 0)
        def _():
            dgb_ref[...] = jnp.zeros_like(dgb_ref)

        dmv = dm_ref[...]
        dyc = lax.dot_general(dmv, w_ref[pl.ds(0, D_CONV), :], NT, preferred_element_type=F32)
        do_ref[...] = lax.dot_general(dmv, w_ref[pl.ds(D_CONV, D_ATT), :], NT, preferred_element_type=F32)
        uc = uc_ref[...]
        mu = jnp.mean(uc, axis=-1, keepdims=True)
        cen = uc - mu
        rs = lax.rsqrt(jnp.mean(cen * cen, axis=-1, keepdims=True) + EPS)
        xh = cen * rs
        z = xh * gl_ref[...] + bl_ref[...]
        sg = _sig(z)
        dz = dyc * (sg * (1.0 + z * (1.0 - sg)))
        dgb_ref[0:1, :] += jnp.sum(dz * xh, axis=0, keepdims=True)
        dgb_ref[1:2, :] += jnp.sum(dz, axis=0, keepdims=True)
        dxh = dz * gl_ref[...]
        duc_ref[...] = rs * (dxh - jnp.mean(dxh, axis=-1, keepdims=True)
                             - xh * jnp.mean(dxh * xh, axis=-1, keepdims=True))

    return _call(
        body, name="mix_out_bwd", grid=(tokens // tm,),
        in_specs=[pl.BlockSpec((tm, D_MODEL), lambda i: (i, 0)),
                  pl.BlockSpec((tm, D_CONV), lambda i: (i, 0)),
                  pl.BlockSpec((1, D_CONV), lambda i: (0, 0)),
                  pl.BlockSpec((1, D_CONV), lambda i: (0, 0)),
                  pl.BlockSpec((D_MODEL, D_MODEL), lambda i: (0, 0))],
        out_specs=[pl.BlockSpec((tm, D_CONV), lambda i: (i, 0)),
                   pl.BlockSpec((tm, D_ATT), lambda i: (i, 0)),
                   pl.BlockSpec((8, D_CONV), lambda i: (0, 0))],
        out_shape=[_sds((tokens, D_CONV), F32), _sds((tokens, D_ATT), F32), _sds((8, D_CONV), F32)],
        compiler_params=_params(("arbitrary",)),
    )(dmix, uc2, g_ln, b_ln, w_out)


FF_TILE = 256
FF_TILES = D_FF // FF_TILE


def _load_once(hbm_refs, vmem_refs, sems):
    @pl.when(pl.program_id(0) == 0)
    def _():
        copies = [pltpu.make_async_copy(src, dst, sems.at[n]) for n, (src, dst) in enumerate(zip(hbm_refs, vmem_refs))]
        for cp in copies:
            cp.start()
        for cp in copies:
            cp.wait()


def _ffn_fwd(h2, w_gate_t, w_up_t, w_down, x1, target, mod8, seq, tm=512):
    tokens = h2.shape[0]
    per_seq = seq // tm
    n_seq = tokens // seq

    def body(h_ref, m_ref, x1_ref, t_ref, wg_hbm, wu_hbm, wd_hbm, gate_ref, up_ref, dy_ref, df_ref, sq_ref, dgf_ref,
             wg, wu, wd, w_sems):
        i = pl.program_id(0)
        _load_once((wg_hbm, wu_hbm, wd_hbm), (wg, wu, wd), w_sems)

        @pl.when(i == 0)
        def _():
            sq_ref[...] = jnp.zeros_like(sq_ref)

        @pl.when(i % per_seq == 0)
        def _():
            dgf_ref[...] = jnp.zeros_like(dgf_ref)

        hv = h_ref[...]

        def gate_up(t):
            rows = pl.ds(t * FF_TILE, FF_TILE)
            return (lax.dot_general(hv, wg[rows, :], NT, preferred_element_type=F32),
                    lax.dot_general(hv, wu[rows, :], NT, preferred_element_type=F32))

        fv = jnp.zeros((tm, D_MODEL), F32)
        ahead = gate_up(0)
        for t in range(FF_TILES):
            gate, up = ahead
            if t + 1 < FF_TILES:
                ahead = gate_up(t + 1)
            gate_ref[t] = gate.astype(BF16)
            up_ref[t] = up.astype(BF16)
            act = (gate * _sig(gate) * up).astype(BF16)
            fv = fv + jnp.dot(act, wd[pl.ds(t * FF_TILE, FF_TILE), :], preferred_element_type=F32)
        gate_f = m_ref[5:6, :]
        diff = x1_ref[...] + gate_f * fv - t_ref[...]
        sq_ref[0:1, :] += jnp.sum(diff * diff, axis=0, keepdims=True)
        dy = diff * (1.0 / D_MODEL)
        dy_ref[...] = dy
        df_ref[...] = (gate_f * dy).astype(BF16)
        dgf_ref[0:1, :] += jnp.sum(dy * fv, axis=0, keepdims=True)

    rows_spec = pl.BlockSpec((tm, D_MODEL), lambda i: (i, 0))
    per = pl.BlockSpec((None, 8, D_MODEL), lambda i: (i // per_seq, 0, 0))
    tiles = pl.BlockSpec((FF_TILES, tm, FF_TILE), lambda i: (0, i, 0))
    weight = pltpu.VMEM((D_FF, D_MODEL), BF16)
    return _call(
        body, name="ffn_fwd", grid=(tokens // tm,),
        in_specs=[rows_spec, per, rows_spec, rows_spec, ANY_SPEC, ANY_SPEC, ANY_SPEC],
        out_specs=[tiles, tiles, rows_spec, rows_spec, pl.BlockSpec((8, D_MODEL), lambda i: (0, 0)), per],
        out_shape=[_sds((FF_TILES, tokens, FF_TILE), BF16), _sds((FF_TILES, tokens, FF_TILE), BF16),
                   _sds((tokens, D_MODEL), F32), _sds((tokens, D_MODEL), BF16),
                   _sds((8, D_MODEL), F32), _sds((n_seq, 8, D_MODEL), F32)],
        scratch_shapes=[weight, weight, weight, pltpu.SemaphoreType.DMA((3,))],
        compiler_params=_params(("arbitrary",)),
    )(h2, mod8, x1, target, w_gate_t, w_up_t, w_down)


def _ffn_bwd(df, gate, up, w_gate_t, w_up_t, w_down, x1, dy, mix, mod8, g_ffn, seq, tm=256):
    tokens = df.shape[0]
    per_seq = seq // tm
    n_seq = tokens // seq

    def body(df_ref, gate_ref, up_ref, m_ref, g_ref, x1_ref, dy_ref, mix_ref, wg_hbm, wu_hbm, wd_hbm,
             dgate_ref, dup_ref, act_ref, dx1_ref, dmix_ref, dg_ref, dm_ref, wg, wu, wd, w_sems):
        i = pl.program_id(0)
        _load_once((wg_hbm, wu_hbm, wd_hbm), (wg, wu, wd), w_sems)

        @pl.when(i == 0)
        def _():
            dg_ref[...] = jnp.zeros_like(dg_ref)

        @pl.when(i % per_seq == 0)
        def _():
            dm_ref[...] = jnp.zeros_like(dm_ref)

        dfv = df_ref[...]

        def d_act(t):
            return lax.dot_general(dfv, wd[pl.ds(t * FF_TILE, FF_TILE), :], NT, preferred_element_type=F32)

        dh = jnp.zeros((tm, D_MODEL), F32)
        ahead = d_act(0)
        for t in range(FF_TILES):
            dact = ahead
            if t + 1 < FF_TILES:
                ahead = d_act(t + 1)
            rows = pl.ds(t * FF_TILE, FF_TILE)
            gv = gate_ref[t].astype(F32)
            uv = up_ref[t].astype(F32)
            sg = _sig(gv)
            silu = gv * sg
            act_ref[t] = (silu * uv).astype(BF16)
            dup = (dact * silu).astype(BF16)
            dgate = (dact * uv * (sg * (1.0 + gv * (1.0 - sg)))).astype(BF16)
            dup_ref[t] = dup
            dgate_ref[t] = dgate
            dh = dh + (jnp.dot(dgate, wg[rows, :], preferred_element_type=F32)
                       + jnp.dot(dup, wu[rows, :], preferred_element_type=F32))
        g = g_ref[...]
        x1v = x1_ref[...]
        rs = lax.rsqrt(jnp.mean(x1v * x1v, axis=-1, keepdims=True) + EPS)
        xhat = x1v * rs
        dm_ref[0:1, :] += jnp.sum(dh, axis=0, keepdims=True)
        dm_ref[1:2, :] += jnp.sum(dh * (xhat * g), axis=0, keepdims=True)
        dn = dh * (1.0 + m_ref[4:5, :])
        dg_ref[0:1, :] += jnp.sum(dn * xhat, axis=0, keepdims=True)
        dxh = dn * g
        dx1 = dy_ref[...] + rs * (dxh - xhat * jnp.mean(dxh * xhat, axis=-1, keepdims=True))
        dx1_ref[...] = dx1
        dm_ref[2:3, :] += jnp.sum(dx1 * mix_ref[...].astype(F32), axis=0, keepdims=True)
        dmix_ref[...] = (m_ref[2:3, :] * dx1).astype(BF16)

    rows_spec = pl.BlockSpec((tm, D_MODEL), lambda i: (i, 0))
    per = pl.BlockSpec((None, 8, D_MODEL), lambda i: (i // per_seq, 0, 0))
    tiles = pl.BlockSpec((FF_TILES, tm, FF_TILE), lambda i: (0, i, 0))
    weight = pltpu.VMEM((D_FF, D_MODEL), BF16)
    return _call(
        body, name="ffn_bwd", grid=(tokens // tm,),
        in_specs=[rows_spec, tiles, tiles, per, pl.BlockSpec((1, D_MODEL), lambda i: (0, 0)),
                  rows_spec, rows_spec, rows_spec, ANY_SPEC, ANY_SPEC, ANY_SPEC],
        out_specs=[tiles, tiles, tiles, rows_spec, rows_spec, pl.BlockSpec((8, D_MODEL), lambda i: (0, 0)), per],
        out_shape=[_sds((FF_TILES, tokens, FF_TILE), BF16)] * 3
        + [_sds((tokens, D_MODEL), F32), _sds((tokens, D_MODEL), BF16),
           _sds((8, D_MODEL), F32), _sds((n_seq, 8, D_MODEL), F32)],
        scratch_shapes=[weight, weight, weight, pltpu.SemaphoreType.DMA((3,))],
        compiler_params=_params(("arbitrary",)),
    )(df, gate, up, mod8, g_ffn, x1, dy, mix, w_gate_t, w_up_t, w_down)


def _mix_in_bwd(d_a, d_g, d_q, d_k, d_v, w_in, x2, dx1, mod8, g_mix, seq, ride, tm=512):
    tokens = x2.shape[0]
    per_seq = seq // tm
    n_seq = tokens // seq
    parts = (d_a, d_g, d_q, d_k, d_v)
    width = D_CONV
    n_ride = len(ride)
    ride_scatter = [s for _, s in ride]

    def body(*refs):
        da_ref, dg_ref, dq_ref, dk_ref, dv_ref, w_ref, x_ref, dx1_ref, m_ref, g_ref = refs[:10]
        ride_in = refs[10:10 + n_ride]
        gx_ref, dgm_ref, dm_ref = refs[10 + n_ride:13 + n_ride]
        ride_args = (ride_scatter, ride_in, refs[13 + n_ride:13 + 2 * n_ride]) + tuple(refs[13 + 2 * n_ride:])
        i = pl.program_id(0)

        @pl.when(i == 0)
        def _():
            _exchange_start(*ride_args)
            dgm_ref[...] = jnp.zeros_like(dgm_ref)

        @pl.when(i % per_seq == 0)
        def _():
            dm_ref[...] = jnp.zeros_like(dm_ref)

        dh = jnp.zeros((tm, D_MODEL), F32)
        for n, ref in enumerate((da_ref, dg_ref, dq_ref, dk_ref, dv_ref)):
            dh = dh + jnp.dot(ref[...], w_ref[pl.ds(n * width, width), :], preferred_element_type=F32)
        xv = x_ref[...]
        r = lax.rsqrt(jnp.mean(xv * xv, axis=-1, keepdims=True) + EPS)
        xhat = xv * r
        g = g_ref[...]
        dm_ref[0:1, :] += jnp.sum(dh, axis=0, keepdims=True)
        dm_ref[1:2, :] += jnp.sum(dh * (xhat * g), axis=0, keepdims=True)
        dn = dh * (1.0 + m_ref[1:2, :])
        dgm_ref[0:1, :] += jnp.sum(dn * xhat, axis=0, keepdims=True)
        dxh = dn * g
        gx_ref[...] = dx1_ref[...] + r * (dxh - xhat * jnp.mean(dxh * xhat, axis=-1, keepdims=True))

        @pl.when(i == tokens // tm - 1)
        def _():
            _exchange_wait(*ride_args)

    rows = pl.BlockSpec((tm, D_MODEL), lambda i: (i, 0))
    half = pl.BlockSpec((tm, width), lambda i: (i, 0))
    per = pl.BlockSpec((None, 8, D_MODEL), lambda i: (i // per_seq, 0, 0))
    return _call(
        body, name="mix_in_bwd", grid=(tokens // tm,),
        in_specs=[half] * 5 + [pl.BlockSpec((D_IN, D_MODEL), lambda i: (0, 0)), rows, rows, per,
                               pl.BlockSpec((1, D_MODEL), lambda i: (0, 0))] + [ANY_SPEC] * n_ride,
        out_specs=[rows, pl.BlockSpec((8, D_MODEL), lambda i: (0, 0)), per] + [ANY_SPEC] * n_ride,
        out_shape=[_sds((tokens, D_MODEL), F32), _sds((8, D_MODEL), F32), _sds((n_seq, 8, D_MODEL), F32)]
        + _exchange_shapes(ride),
        scratch_shapes=_exchange_sems(n_ride),
        compiler_params=_params(("arbitrary",)),
    )(*parts, w_in, x2, dx1, mod8, g_mix, *[a for a, _ in ride])


def _grad_matmul_parts(a_parts, b_parts, name, tk=1024):
    tokens = a_parts[0].shape[0]
    na, nb = len(a_parts), len(b_parts)
    ma, nbw = a_parts[0].shape[1], b_parts[0].shape[1]

    n_k = tokens // tk

    def body(*refs):
        a_refs, b_refs, o_ref, acc = refs[:na], refs[na:na + nb], refs[na + nb], refs[na + nb + 1]

        @pl.when(pl.program_id(0) == 0)
        def _():
            acc[...] = jnp.zeros_like(acc)

        for i in range(na):
            for j in range(nb):
                acc[pl.ds(i * ma, ma), pl.ds(j * nbw, nbw)] += lax.dot_general(
                    a_refs[i][...], b_refs[j][...], TN, preferred_element_type=F32)

        @pl.when(pl.program_id(0) == n_k - 1)
        def _():
            o_ref[...] = acc[...].astype(o_ref.dtype)

    return _call(
        body, name=name, grid=(n_k,),
        in_specs=[pl.BlockSpec((tk, ma), lambda k: (k, 0))] * na + [pl.BlockSpec((tk, nbw), lambda k: (k, 0))] * nb,
        out_specs=pl.BlockSpec((na * ma, nb * nbw), lambda k: (0, 0)),
        out_shape=_sds((na * ma, nb * nbw), BF16),
        scratch_shapes=[pltpu.VMEM((na * ma, nb * nbw), F32)],
        compiler_params=_params(("arbitrary",)),
    )(*a_parts, *b_parts)


def _grad_matmul_tiles(a, b, name, tk=1024):
    tiled_b = b.ndim == 3
    tiles, tokens, width = b.shape if tiled_b else a.shape
    other = a.shape[1] if tiled_b else b.shape[1]
    out_tile = (other, width) if tiled_b else (width, other)
    n_k = tokens // tk

    def body(a_ref, b_ref, o_ref, acc):
        @pl.when(pl.program_id(0) == 0)
        def _():
            acc[...] = jnp.zeros_like(acc)

        for t in range(tiles):
            lhs = a_ref[...] if tiled_b else a_ref[t]
            rhs = b_ref[t] if tiled_b else b_ref[...]
            acc[t] += lax.dot_general(lhs, rhs, TN, preferred_element_type=F32)

        @pl.when(pl.program_id(0) == n_k - 1)
        def _():
            o_ref[...] = acc[...].astype(o_ref.dtype)

    flat = pl.BlockSpec((tk, other), lambda k: (k, 0))
    tiled = pl.BlockSpec((tiles, tk, width), lambda k: (0, k, 0))
    return _call(
        body, name=name, grid=(n_k,),
        in_specs=[flat, tiled] if tiled_b else [tiled, flat],
        out_specs=pl.BlockSpec((tiles,) + out_tile, lambda k: (0, 0, 0)),
        out_shape=_sds((tiles,) + out_tile, BF16),
        scratch_shapes=[pltpu.VMEM((tiles,) + out_tile, F32)],
        compiler_params=_params(("arbitrary",)),
    )(a, b)


def _adamw(w, m, v, g, name, n_parts=0, tr=256):
    rows, cols = w.shape
    tr = min(tr, rows)
    c1 = 1.0 - ADAM_B1 ** ADAM_STEP
    c2 = 1.0 - ADAM_B2 ** ADAM_STEP

    def body(w_ref, m_ref, v_ref, g_ref, go_ref, d_ref, mo_ref, vo_ref):
        if n_parts:
            gv = g_ref[0].astype(F32)
            for p in range(1, n_parts):
                gv = gv + g_ref[p].astype(F32)
        else:
            gv = g_ref[...]
        go_ref[...] = gv
        mn = ADAM_B1 * m_ref[...] + (1.0 - ADAM_B1) * gv
        vn = ADAM_B2 * v_ref[...] + (1.0 - ADAM_B2) * (gv * gv)
        mo_ref[...] = mn
        vo_ref[...] = vn
        d_ref[...] = -ADAM_LR * ((mn / c1) / (jnp.sqrt(vn / c2) + ADAM_EPS) + ADAM_WD * w_ref[...])

    blk = pl.BlockSpec((tr, cols), lambda i: (i, 0))
    g_spec = pl.BlockSpec((n_parts, tr, cols), lambda i: (0, i, 0)) if n_parts else blk
    return _call(
        body, name=name, grid=(rows // tr,),
        in_specs=[blk, blk, blk, g_spec], out_specs=[blk] * 4,
        out_shape=[_sds((rows, cols), F32)] * 4,
        compiler_params=_params(("parallel",)),
    )(w, m, v, g)


def _cols_to_full(blocks):
    n, r, c = blocks.shape
    return jnp.transpose(blocks, (1, 0, 2)).reshape(r, n * c)


def _pad_lanes(v, width):
    return jnp.pad(v, ((0, 0), (0, width - v.shape[1])))


def kernel(x, c, w_ada, b_ada, g_mix, w_in, w_dw, b_dw, g_conv_ln, b_conv_ln, g_q, g_k, w_out, g_ffn, w_gate, w_up, w_down, loss_target, m_w_ada, m_b_ada, m_g_mix, m_w_in, m_w_dw, m_b_dw, m_g_conv_ln, m_b_conv_ln, m_g_q, m_g_k, m_w_out, m_g_ffn, m_w_gate, m_w_up, m_w_down, v_w_ada, v_b_ada, v_g_mix, v_w_in, v_w_dw, v_b_dw, v_g_conv_ln, v_b_conv_ln, v_g_q, v_g_k, v_w_out, v_g_ffn, v_w_gate, v_w_up, v_w_down):
    n_seq, seq, _ = x.shape
    tokens = n_seq * seq
    me = 4 * lax.axis_index("x") + 2 * lax.axis_index("y") + lax.axis_index("c")
    ada_cols = w_ada.shape[2]
    dw_cols = w_dw.shape[2]

    def transposed(w):
        return jnp.transpose(w[0])

    (c_g, w_in_g, w_dw_g) = _gather_by_chip([c, transposed(w_in).astype(BF16), w_dw[0]], "gather_weights")
    c_all = c_g.reshape(N_DEV * n_seq, D_MODEL)
    w_in_t = w_in_g.reshape(D_IN, D_MODEL)
    w_dw_f = _cols_to_full(w_dw_g)

    b_cols = lax.dynamic_slice(b_ada, (0, me * ada_cols), (1, ada_cols))
    mod_cols = _ada_fwd(c_all, w_ada[0], b_cols)
    (mod_g,) = _exchange([(mod_cols, False)], "gather_mod")
    mod_mine = lax.dynamic_slice(mod_g, (0, me * n_seq, 0), (N_DEV, n_seq, ada_cols))
    mod = jnp.transpose(mod_mine, (1, 0, 2)).reshape(n_seq, N_MOD, D_MODEL)
    mod8 = jnp.pad(mod, ((0, 0), (0, 8 - N_MOD), (0, 0)))

    x2 = x.reshape(tokens, D_MODEL)
    h1, proj = _mix_in(x2, mod8, g_mix, w_in_t, seq)
    proj3 = proj.reshape(D_IN // LANES, n_seq, seq, LANES)
    uc3 = _conv_fwd(proj3, w_dw_f, b_dw)
    g_q2, g_k2 = jnp.tile(g_q, (1, 2)), jnp.tile(g_k, (1, 2))
    y_att3, lse3, w_out_g, w_gate_g, w_up_g, w_down_g = _attn_fwd(
        proj3, g_q2, g_k2,
        [(w_out[0].astype(BF16), False), (transposed(w_gate).astype(BF16), False),
         (transposed(w_up).astype(BF16), False), (w_down[0].astype(BF16), False)])
    w_out_f = w_out_g.reshape(D_MODEL, D_MODEL)
    w_gate_f = w_gate_g.reshape(D_FF, D_MODEL)
    w_up_f = w_up_g.reshape(D_FF, D_MODEL)
    w_down_f = w_down_g.reshape(D_FF, D_MODEL)
    uc2 = uc3.reshape(tokens, D_CONV)
    y_att2 = y_att3.reshape(tokens, D_ATT)
    y_conv, mix, x1, h2 = _mix_out(uc2, y_att2, x2, mod8, g_conv_ln, b_conv_ln, g_ffn, w_out_f, seq)
    gate, up, dy, df, sq, dgate_f = _ffn_fwd(
        h2, w_gate_f, w_up_f, w_down_f, x1, loss_target.reshape(tokens, D_MODEL), mod8, seq)

    dgate, dup, act, dx1, dmix, dg_ffn, dmod_f = _ffn_bwd(
        df, gate, up, w_gate_f, w_up_f, w_down_f, x1, dy, mix, mod8, g_ffn, seq)
    duc2, do2, dgb_ln = _mix_out_bwd(dmix, uc2, g_conv_ln, b_conv_ln, w_out_f)
    d_a3, d_g3, dw_dw_p, db_dw_p = _conv_bwd(duc2.reshape(n_seq, seq, D_CONV), proj3, w_dw_f)
    gw_gate = _grad_matmul_tiles(dgate, h2, "grad_w_gate")
    gw_up = _grad_matmul_tiles(dup, h2, "grad_w_up")
    gw_down = _grad_matmul_tiles(act, df, "grad_w_down")
    gw_out = _grad_matmul_parts([y_conv, y_att2], [dmix], "grad_w_out")
    d_q3, d_k3, d_v3, dg_qk, p_gate, p_up, p_down, p_out = _attn_bwd(
        proj3, do2.reshape(n_seq, seq, D_ATT), y_att3, lse3, g_q2, g_k2,
        [(gw_gate.reshape(N_DEV, D_FF // N_DEV, D_MODEL), True), (gw_up.reshape(N_DEV, D_FF // N_DEV, D_MODEL), True),
         (gw_down.reshape(N_DEV, D_FF // N_DEV, D_MODEL), True),
         (gw_out.reshape(N_DEV, D_MODEL // N_DEV, D_MODEL), True)])
    flat = lambda t: t.reshape(tokens, t.shape[-1])
    d_a, d_g, d_q, d_k, d_v = flat(d_a3), flat(d_g3), flat(d_q3), flat(d_k3), flat(d_v3)
    gw_in = _grad_matmul_parts([d_a, d_g, d_q, d_k, d_v], [h1], "grad_w_in")
    grad_x2, dg_mix, dmod_m, p_in = _mix_in_bwd(
        d_a, d_g, d_q, d_k, d_v, w_in_t, x2, dx1, mod8, g_mix, seq,
        [(gw_in.reshape(N_DEV, D_IN // N_DEV, D_MODEL), True)])

    dmod = jnp.concatenate([dmod_m[:, 0], dmod_m[:, 1], dmod_f[:, 2], dmod_f[:, 0], dmod_f[:, 1], dgate_f[:, 0]], axis=1)
    dg_q = dg_qk[0:1, 0:HEAD_DIM] + dg_qk[0:1, HEAD_DIM:]
    dg_k = dg_qk[1:2, 0:HEAD_DIM] + dg_qk[1:2, HEAD_DIM:]
    loss_part = (0.5 / D_MODEL) * jnp.sum(sq[0:1, :], axis=1, keepdims=True)
    small = jnp.concatenate(
        [dg_mix[0:1], dg_ffn[0:1], db_dw_p[0:1], dgb_ln[0:1], dgb_ln[1:2],
         _pad_lanes(dg_q, LANES), _pad_lanes(dg_k, LANES), _pad_lanes(loss_part, LANES)], axis=1)
    n_small = small.shape[1] - LANES

    (dmod_g, small_g, dw_g) = _exchange([(dmod, False), (small, False), (dw_dw_p, False)], "gather_small_grads")

    dmod_all = dmod_g.reshape(N_DEV * n_seq, N_MOD * D_MODEL)
    dmod_cols = lax.dynamic_slice(dmod_all, (0, me * ada_cols), (N_DEV * n_seq, ada_cols))
    gw_ada, gb_ada = _ada_bwd(c_all, dmod_cols, dmod_all)

    res = {}
    res["w_ada"] = _adamw(w_ada[0], m_w_ada[0], v_w_ada[0], gw_ada, "adamw_w_ada")
    res["b_ada"] = _adamw(b_ada, m_b_ada, v_b_ada, gb_ada, "adamw_b_ada")
    def adamw_transposed(w, m, v, parts, name, tr):
        outs = _adamw(transposed(w), transposed(m), transposed(v), parts, name, N_DEV, tr=tr)
        return tuple(jnp.transpose(o) for o in outs)

    res["w_in"] = adamw_transposed(w_in, m_w_in, v_w_in, p_in, "adamw_w_in", 160)
    res["w_out"] = _adamw(w_out[0], m_w_out[0], v_w_out[0], p_out, "adamw_w_out", N_DEV)
    res["w_gate"] = adamw_transposed(w_gate, m_w_gate, v_w_gate, p_gate, "adamw_w_gate", 176)
    res["w_up"] = adamw_transposed(w_up, m_w_up, v_w_up, p_up, "adamw_w_up", 176)
    res["w_down"] = _adamw(w_down[0], m_w_down[0], v_w_down[0], p_down, "adamw_w_down", N_DEV, tr=176)
    dw_mine = lax.dynamic_slice(dw_g, (0, 0, me * dw_cols), (N_DEV, CONV_WIDTH, dw_cols))
    res["w_dw"] = _adamw(w_dw[0], m_w_dw[0], v_w_dw[0], dw_mine, "adamw_w_dw", N_DEV)

    small_names = ["g_mix", "g_ffn", "b_dw", "g_conv_ln", "b_conv_ln", "g_q", "g_k"]
    small_w = {"g_mix": (g_mix, m_g_mix, v_g_mix), "g_ffn": (g_ffn, m_g_ffn, v_g_ffn), "b_dw": (b_dw, m_b_dw, v_b_dw),
               "g_conv_ln": (g_conv_ln, m_g_conv_ln, v_g_conv_ln), "b_conv_ln": (b_conv_ln, m_b_conv_ln, v_b_conv_ln),
               "g_q": (g_q, m_g_q, v_g_q), "g_k": (g_k, m_g_k, v_g_k)}
    widths = [max(small_w[n][0].shape[1], LANES) for n in small_names]
    packed = [jnp.concatenate([_pad_lanes(small_w[n][i], wd) for n, wd in zip(small_names, widths)], axis=1) for i in range(3)]
    outs = _adamw(packed[0], packed[1], packed[2], small_g[:, :, :n_small], "adamw_small", N_DEV)
    off = 0
    for n, wd in zip(small_names, widths):
        real = small_w[n][0].shape[1]
        res[n] = tuple(o[:, off:off + real] for o in outs)
        off += wd
    loss = jnp.sum(small_g[:, 0, n_small])

    order = ["w_ada", "b_ada", "g_mix", "w_in", "w_dw", "b_dw", "g_conv_ln", "b_conv_ln", "g_q", "g_k",
             "w_out", "g_ffn", "w_gate", "w_up", "w_down"]
    lead = {"w_ada", "w_in", "w_dw", "w_out", "w_gate", "w_up", "w_down"}
    grads, deltas, new_m, new_v = [], [], [], []
    for n in order:
        g, d, mn, vn = res[n]
        g, d, mn, vn = (t[None] if n in lead else t for t in (g, d, mn, vn))
        grads.append(g)
        deltas.append(d)
        new_m.append(mn)
        new_v.append(vn)
    return (loss, grad_x2.reshape(n_seq, seq, D_MODEL), *grads, *deltas, *new_m, *new_v)
```

```python
import numpy as np
import jax
import jax.numpy as jnp
from jax import lax
from jax.experimental import pallas as pl
from jax.experimental.pallas import tpu as pltpu

F32 = jnp.float32
BF16 = jnp.bfloat16

N_DEV = 8
D_MODEL = 1024
D_CONV = 512
D_ATT = 512
HEAD_DIM = 64
CONV_WIDTH = 31
D_IN = 2 * D_CONV + 3 * D_ATT
D_FF = 2816
N_MOD = 6
EPS = 1e-6
RADIUS = 64
DILATIONS = (1, 4, 16)
Q_BLOCK = 128
LANES = 128
VMEM_LIMIT = 56 * 1024 * 1024

ADAM_LR = 0.001
ADAM_B1 = 0.9
ADAM_B2 = 0.999
ADAM_EPS = 1e-08
ADAM_WD = 0.01
ADAM_STEP = 10

NT = (((1,), (1,)), ((), ()))
TN = (((0,), (0,)), ((), ()))


def _call(body, **kw):
    return pl.pallas_call(body, **kw)


def _params(sem=None, vmem=VMEM_LIMIT):
    return pltpu.CompilerParams(dimension_semantics=sem, vmem_limit_bytes=vmem)


def _sig(x):
    return 1.0 / (1.0 + jnp.exp(-x))


def _sds(shape, dtype):
    return jax.ShapeDtypeStruct(shape, dtype)


N_PEER = N_DEV - 1
ANY_SPEC = pl.BlockSpec(memory_space=pl.ANY)


def _exchange_copies(scatter, ins, outs, *sems):
    n = len(ins)
    if n == 0:
        return [], []
    send_sems, recv_sems, local_sems = sems
    x, y, c = lax.axis_index("x"), lax.axis_index("y"), lax.axis_index("c")
    me = 4 * x + 2 * y + c

    def src(a, slot):
        return ins[a].at[slot] if scatter[a] else ins[a]

    local = [pltpu.make_async_copy(src(a, me), outs[a].at[me], local_sems.at[a]) for a in range(n)]
    flights = []
    for k in range(1, N_DEV):
        px = 1 - x if k & 4 else x
        py = 1 - y if k & 2 else y
        pc = 1 - c if k & 1 else c
        pid = 4 * px + 2 * py + pc
        for a in range(n):
            i = a * N_PEER + k - 1
            send, recv = (pltpu.make_async_remote_copy(
                src_ref=src(a, pid), dst_ref=outs[a].at[slot],
                send_sem=send_sems.at[i], recv_sem=recv_sems.at[i],
                device_id=(px, py, pc), device_id_type=pl.DeviceIdType.MESH) for slot in (me, pid))
            flights.append((send, recv))
    return local, flights


def _exchange_start(*args):
    local, flights = _exchange_copies(*args)
    for cp in local:
        cp.start()
    for send, _ in flights:
        send.start()


def _exchange_wait(*args):
    local, flights = _exchange_copies(*args)
    for send, recv in flights:
        send.wait_send()
        recv.wait_recv()
    for cp in local:
        cp.wait()


def _exchange_shapes(items):
    return [_sds((N_DEV,) + tuple(arr.shape[1:] if scatter else arr.shape), arr.dtype) for arr, scatter in items]


def _exchange_sems(n):
    if n == 0:
        return []
    return [pltpu.SemaphoreType.DMA((n * N_PEER,)), pltpu.SemaphoreType.DMA((n * N_PEER,)),
            pltpu.SemaphoreType.DMA((n,))]


def _gather_by_chip_phase(phase, ins, outs, send_sems, recv_sems, local_sems):
    n = len(ins)
    per = N_PEER
    x, y, c = lax.axis_index("x"), lax.axis_index("y"), lax.axis_index("c")
    me, sibling = (x, y, c), (x, y, 1 - c)
    chips = [(1 - x, y), (x, 1 - y), (1 - x, 1 - y)]

    def slot(px, py, pc):
        return 4 * px + 2 * py + pc

    def copy(a, k, block, to, src=None):
        dst = outs[a].at[slot(*block)]
        return pltpu.make_async_remote_copy(
            src_ref=dst if src is None else src, dst_ref=dst,
            send_sem=send_sems.at[a * per + k], recv_sem=recv_sems.at[a * per + k],
            device_id=to, device_id_type=pl.DeviceIdType.MESH)

    local = [pltpu.make_async_copy(ins[a], outs[a].at[slot(*me)], local_sems.at[a]) for a in range(n)]
    first = []
    for a in range(n):
        first.append(copy(a, 0, me, sibling, src=ins[a]))
        first += [copy(a, 1 + j, me, (*chip, c), src=ins[a]) for j, chip in enumerate(chips)]
    passed = [copy(a, 4 + j, (*chip, c), sibling) for j, chip in enumerate(chips) for a in range(n)]
    if phase == 0:
        for cp in local + first:
            cp.start()
    elif phase == 1:
        for j, chip in enumerate(chips):
            for a in range(n):
                copy(a, 1 + j, (*chip, c), me).wait_recv()
        for cp in passed:
            cp.start()
    else:
        for a in range(n):
            copy(a, 0, sibling, me).wait_recv()
            for j, chip in enumerate(chips):
                copy(a, 4 + j, (*chip, 1 - c), me).wait_recv()
        for cp in first + passed:
            cp.wait_send()
        for cp in local:
            cp.wait()


def _gather_by_chip(arrays, name):
    n = len(arrays)

    def body(*refs):
        for phase in range(3):
            _gather_by_chip_phase(phase, refs[:n], refs[n:2 * n], *refs[2 * n:])

    return _call(
        body, name=name, out_shape=_exchange_shapes([(arr, False) for arr in arrays]),
        in_specs=[ANY_SPEC] * n, out_specs=[ANY_SPEC] * n, scratch_shapes=_exchange_sems(n),
    )(*arrays)


def _exchange(items, name):
    n = len(items)
    scatter = [s for _, s in items]

    def body(*refs):
        args = (scatter, refs[:n], refs[n:2 * n]) + tuple(refs[2 * n:])
        _exchange_start(*args)
        _exchange_wait(*args)

    return _call(
        body, name=name, out_shape=_exchange_shapes(items),
        in_specs=[ANY_SPEC] * n, out_specs=[ANY_SPEC] * n, scratch_shapes=_exchange_sems(n),
    )(*[a for a, _ in items])


def _ada_fwd(c_all, w_ada, b_cols):
    def body(c_ref, w_ref, b_ref, o_ref):
        cv = c_ref[...]
        sc = (cv * _sig(cv)).astype(BF16)
        o_ref[...] = jnp.dot(sc, w_ref[...].astype(BF16), preferred_element_type=F32) + b_ref[...]

    return _call(body, name="ada_fwd", out_shape=_sds((c_all.shape[0], w_ada.shape[1]), F32),
                 compiler_params=_params())(c_all, w_ada, b_cols)


def _ada_bwd(c_all, dmod_cols, dmod_all):
    def body(c_ref, dc_ref, da_ref, gw_ref, gb_ref):
        cv = c_ref[...]
        sc = (cv * _sig(cv)).astype(BF16)
        gw_ref[...] = lax.dot_general(sc, dc_ref[...].astype(BF16), TN, preferred_element_type=F32)
        gb_ref[...] = jnp.sum(da_ref[...], axis=0, keepdims=True)

    return _call(body, name="ada_bwd",
                 out_shape=[_sds((c_all.shape[1], dmod_cols.shape[1]), F32), _sds((1, dmod_all.shape[1]), F32)],
                 compiler_params=_params())(c_all, dmod_cols, dmod_all)


MIX_ROWS = 128


def _mix_in(x2, mod8, g_mix, w_in, seq, tm=512):
    tokens = x2.shape[0]
    per_seq = seq // tm

    def body(x_ref, m_ref, g_ref, wt_ref, h_ref, p_ref, w_ref):
        @pl.when(pl.program_id(0) == 0)
        def _():
            w_ref[...] = wt_ref[...].T

        def normed(c):
            rows = pl.ds(c * MIX_ROWS, MIX_ROWS)
            xv = x_ref[rows, :]
            r = lax.rsqrt(jnp.mean(xv * xv, axis=-1, keepdims=True) + EPS)
            hb = ((xv * r * g_ref[...]) * (1.0 + m_ref[1:2, :]) + m_ref[0:1, :]).astype(BF16)
            h_ref[rows, :] = hb
            return hb

        ahead = normed(0)
        for c in range(tm // MIX_ROWS):
            hb = ahead
            if c + 1 < tm // MIX_ROWS:
                ahead = normed(c + 1)
            p = jnp.dot(hb, w_ref[...], preferred_element_type=F32)
            for cb in range(D_IN // LANES):
                p_ref[cb, pl.ds(c * MIX_ROWS, MIX_ROWS), :] = p[:, cb * LANES:(cb + 1) * LANES]

    return _call(
        body, name="mix_in", grid=(tokens // tm,),
        in_specs=[pl.BlockSpec((tm, D_MODEL), lambda i: (i, 0)),
                  pl.BlockSpec((None, 8, D_MODEL), lambda i: (i // per_seq, 0, 0)),
                  pl.BlockSpec((1, D_MODEL), lambda i: (0, 0)),
                  pl.BlockSpec((D_IN, D_MODEL), lambda i: (0, 0))],
        out_specs=[pl.BlockSpec((tm, D_MODEL), lambda i: (i, 0)),
                   pl.BlockSpec((D_IN // LANES, tm, LANES), lambda i: (0, i, 0))],
        out_shape=[_sds((tokens, D_MODEL), BF16), _sds((D_IN // LANES, tokens, LANES), F32)],
        scratch_shapes=[pltpu.VMEM((D_MODEL, D_IN), BF16)],
        compiler_params=_params(("arbitrary",)),
    )(x2, mod8, g_mix, w_in)


CONV_ROWS = 128
CONV_DW_ROWS = 32
CONV_DW_UNROLL = 8
CONV_HALO = 16


def _fill_shifted(xp, sh, seq):
    for b in range(8):
        sh[b, pl.ds(0, seq + 24), :] = xp[pl.ds(b, seq + 24), :]


def _conv_fwd(proj3, w_dw, b_dw):
    _, n_seq, seq, _ = proj3.shape
    n_cb = D_CONV // LANES

    def body(a_ref, g_ref, w_ref, b_ref, uc_ref, xp, sh):
        zeros = jnp.zeros((CONV_HALO, LANES), F32)
        xp[pl.ds(0, CONV_HALO), :] = zeros
        xp[pl.ds(CONV_HALO + seq, CONV_HALO), :] = zeros
        xp[pl.ds(CONV_HALO, seq), :] = a_ref[...] * _sig(g_ref[...])
        _fill_shifted(xp, sh, seq)

        def blk(i, carry):
            t0 = pl.multiple_of(i * CONV_ROWS, CONV_ROWS)
            acc = jnp.zeros((CONV_ROWS, LANES), F32)
            for j in range(CONV_WIDTH):
                jj = j + 1
                acc = acc + sh[jj % 8, pl.ds(t0 + 8 * (jj // 8), CONV_ROWS), :] * w_ref[j:j + 1, :]
            uc_ref[pl.ds(t0, CONV_ROWS), :] = acc + b_ref[...]
            return carry

        lax.fori_loop(0, seq // CONV_ROWS, blk, 0)

    return _call(
        body, name="conv_fwd", grid=(n_seq, n_cb),
        in_specs=[pl.BlockSpec((None, None, seq, LANES), lambda b, cb: (cb, b, 0, 0)),
                  pl.BlockSpec((None, None, seq, LANES), lambda b, cb: (n_cb + cb, b, 0, 0)),
                  pl.BlockSpec((CONV_WIDTH, LANES), lambda b, cb: (0, cb)),
                  pl.BlockSpec((1, LANES), lambda b, cb: (0, cb))],
        out_specs=pl.BlockSpec((None, seq, LANES), lambda b, cb: (b, 0, cb)),
        out_shape=_sds((n_seq, seq, D_CONV), F32),
        scratch_shapes=[pltpu.VMEM((seq + 2 * CONV_HALO, LANES), F32),
                        pltpu.VMEM((8, seq + 2 * CONV_HALO, LANES), F32)],
        compiler_params=_params(("parallel", "parallel")),
    )(proj3, proj3, w_dw, b_dw)


def _conv_bwd(duc3, proj3, w_dw):
    _, n_seq, seq, _ = proj3.shape
    n_cb = D_CONV // LANES

    def body(duc_ref, a_ref, g_ref, w_ref, da_ref, dg_ref, dw_ref, db_ref, xp, sh):
        @pl.when(pl.program_id(1) == 0)
        def _():
            dw_ref[...] = jnp.zeros_like(dw_ref)
            db_ref[...] = jnp.zeros_like(db_ref)

        zeros = jnp.zeros((CONV_HALO, LANES), F32)
        xp[pl.ds(0, CONV_HALO), :] = zeros
        xp[pl.ds(CONV_HALO + seq, CONV_HALO), :] = zeros
        xp[pl.ds(CONV_HALO, seq), :] = a_ref[...] * _sig(g_ref[...])
        _fill_shifted(xp, sh, seq)
        for j0 in range(0, CONV_WIDTH, 8):
            taps = range(j0, min(j0 + 8, CONV_WIDTH))

            def wblk(i, accs, taps=taps):
                for u in range(CONV_DW_UNROLL):
                    t0 = pl.multiple_of((i * CONV_DW_UNROLL + u) * CONV_DW_ROWS, CONV_DW_ROWS)
                    d = duc_ref[pl.ds(t0, CONV_DW_ROWS), :]
                    accs = tuple(acc + d * sh[(j + 1) % 8, pl.ds(t0 + 8 * ((j + 1) // 8), CONV_DW_ROWS), :]
                                 for acc, j in zip(accs, taps))
                return accs

            accs = lax.fori_loop(0, seq // (CONV_DW_ROWS * CONV_DW_UNROLL), wblk,
                                 tuple(jnp.zeros((CONV_DW_ROWS, LANES), F32) for _ in taps))
            for acc, j in zip(accs, taps):
                dw_ref[j:j + 1, :] += jnp.sum(acc, axis=0, keepdims=True)
        db_ref[0:1, :] += jnp.sum(duc_ref[...], axis=0, keepdims=True)
        xp[pl.ds(CONV_HALO, seq), :] = duc_ref[...]
        _fill_shifted(xp, sh, seq)

        def ublk(i, carry):
            t0 = pl.multiple_of(i * CONV_ROWS, CONV_ROWS)
            acc = jnp.zeros((CONV_ROWS, LANES), F32)
            for j in range(CONV_WIDTH):
                jj = CONV_WIDTH - j
                acc = acc + sh[jj % 8, pl.ds(t0 + 8 * (jj // 8), CONV_ROWS), :] * w_ref[j:j + 1, :]
            av = a_ref[pl.ds(t0, CONV_ROWS), :]
            sg = _sig(g_ref[pl.ds(t0, CONV_ROWS), :])
            da_ref[pl.ds(t0, CONV_ROWS), :] = (acc * sg).astype(BF16)
            dg_ref[pl.ds(t0, CONV_ROWS), :] = (acc * av * sg * (1.0 - sg)).astype(BF16)
            return carry

        lax.fori_loop(0, seq // CONV_ROWS, ublk, 0)

    return _call(
        body, name="conv_bwd", grid=(n_cb, n_seq),
        in_specs=[pl.BlockSpec((None, seq, LANES), lambda cb, b: (b, 0, cb)),
                  pl.BlockSpec((None, None, seq, LANES), lambda cb, b: (cb, b, 0, 0)),
                  pl.BlockSpec((None, None, seq, LANES), lambda cb, b: (n_cb + cb, b, 0, 0)),
                  pl.BlockSpec((CONV_WIDTH, LANES), lambda cb, b: (0, cb))],
        out_specs=[pl.BlockSpec((None, seq, LANES), lambda cb, b: (b, 0, cb)),
                   pl.BlockSpec((None, seq, LANES), lambda cb, b: (b, 0, cb)),
                   pl.BlockSpec((32, LANES), lambda cb, b: (0, cb)),
                   pl.BlockSpec((8, LANES), lambda cb, b: (0, cb))],
        out_shape=[_sds((n_seq, seq, D_CONV), BF16), _sds((n_seq, seq, D_CONV), BF16),
                   _sds((32, D_CONV), F32), _sds((8, D_CONV), F32)],
        scratch_shapes=[pltpu.VMEM((seq + 2 * CONV_HALO, LANES), F32),
                        pltpu.VMEM((8, seq + 2 * CONV_HALO, LANES), F32)],
        compiler_params=_params(("parallel", "arbitrary")),
    )(duc3, proj3, proj3, w_dw)


MASKED = 1e30
ATT_ROWS = 1024
ATT_BWD_ROWS = 2048
ATT_UNROLL = 8
ATT_FWD_UNROLL = 8


def _distance_mats(dil, seg_len):
    kw = min(2 * Q_BLOCK, seg_len)
    offsets = (0, -RADIUS, -2 * RADIUS) if kw == 2 * Q_BLOCK else (0,)
    a = np.arange(Q_BLOCK)[:, None]
    b = np.arange(kw)[None, :]
    mats = []
    for off in offsets:
        rel = np.abs(b + off - a)
        mats.append(np.where(rel <= RADIUS, dil * rel, MASKED))
    return jnp.asarray(np.stack(mats).astype(np.float32))


def _alibi_rows():
    s = np.zeros((4, 8, LANES), np.float32)
    for hp in range(4):
        for hl in range(2):
            s[hp, hl, :] = 2.0 ** (-(2 * hp + hl + 1))
    return jnp.asarray(s)


def _window(n, seg_len):
    static = isinstance(n, int)
    i0 = n * Q_BLOCK if static else pl.multiple_of(n * Q_BLOCK, Q_BLOCK)
    if seg_len <= Q_BLOCK:
        return i0, i0, 0
    per_seg = seg_len // Q_BLOCK
    j = n % per_seg
    seg0 = (n // per_seg) * seg_len
    if static:
        ks_local = min(max(j * Q_BLOCK - RADIUS, 0), seg_len - 2 * Q_BLOCK)
        return i0, seg0 + ks_local, 0 if j == 0 else 2 if j == per_seg - 1 else 1
    ks_local = jnp.clip(j * Q_BLOCK - RADIUS, 0, seg_len - 2 * Q_BLOCK)
    ks = pl.multiple_of(seg0 + ks_local, RADIUS)
    var = jnp.where(j == 0, 0, jnp.where(j == per_seg - 1, 2, 1))
    return i0, ks, var


def _first_head(rows):
    return lax.broadcasted_iota(jnp.int32, (rows, LANES), 1) < HEAD_DIM


def _same_head():
    head = np.arange(LANES) // HEAD_DIM
    return jnp.asarray((head[:, None] == head[None, :]).astype(np.float32)).astype(BF16)


def _head_sum(x, same_ref):
    return jnp.dot(x.astype(BF16), same_ref[...], preferred_element_type=F32)


def _head_mean(x, same_ref):
    return _head_sum(x, same_ref) * (1.0 / HEAD_DIM)


def _per_head(x, first):
    swapped = pltpu.roll(x, HEAD_DIM, 1)
    return jnp.where(first, x, swapped), jnp.where(first, swapped, x)


STRIDE = 4


def _gather_segments(src, dil, seq, tmp, put, staged=False):
    if dil == 1:
        put(0, seq, src[pl.ds(0, seq), :])
    elif dil == STRIDE:
        seg = seq // dil
        for r in range(dil):
            put(r * seg, seg, src[pl.ds(r, seg, stride=dil), :])
    else:
        part, seg = seq // STRIDE, seq // dil
        if staged:
            tmp = src
        else:
            for b in range(STRIDE):
                tmp[pl.ds(b * part, part), :] = src[pl.ds(b, part, stride=STRIDE), :]
        for b in range(STRIDE):
            for a in range(dil // STRIDE):
                put(b * part + a * seg, seg, tmp[pl.ds(b * part + a, seg, stride=dil // STRIDE), :])


def _scatter_segments(dst, get, dil, seq, tmp, accumulate):
    def write(rows, val):
        if accumulate:
            dst[rows, :] += val
        else:
            dst[rows, :] = val

    if dil == 1:
        write(pl.ds(0, seq), get(0, seq))
    elif dil == STRIDE:
        seg = seq // dil
        for r in range(dil):
            write(pl.ds(r, seg, stride=dil), get(r * seg, seg))
    else:
        part, seg = seq // STRIDE, seq // dil
        for b in range(STRIDE):
            for a in range(dil // STRIDE):
                tmp[pl.ds(b * part + a, seg, stride=dil // STRIDE), :] = get(b * part + a * seg, seg)
        for b in range(STRIDE):
            write(pl.ds(b, part, stride=STRIDE), tmp[pl.ds(b * part, part), :])


def _permute_rows(dst, src, dil, seq, tmp, staged=False):
    def put(start, size, val):
        dst[pl.ds(start, size), :] = val.astype(dst.dtype)

    _gather_segments(src, dil, seq, tmp, put, staged)


def _permute_rows_by_head(dst, src, dil, seq, tmp):
    def put(start, size, val):
        first = _first_head(size)
        dst[0, pl.ds(start, size), :] = jnp.where(first, val, 0.0).astype(dst.dtype)
        dst[1, pl.ds(start, size), :] = jnp.where(first, 0.0, val).astype(dst.dtype)

    _gather_segments(src, dil, seq, tmp, put)


def _qk_normalise(q_ref, g2_ref, same_ref, dst, seq, scale, step):
    def chunk(ci, carry):
        rows = pl.ds(pl.multiple_of(ci * step, step), step)
        qv = q_ref[rows, :]
        r = lax.rsqrt(_head_mean(qv * qv, same_ref) + EPS)
        dst[rows, :] = qv * r * (g2_ref[...] * scale)
        return carry

    lax.fori_loop(0, seq // step, chunk, 0)


def _attn_fwd(proj3, g_q2, g_k2, ride):
    _, n_seq, seq, _ = proj3.shape
    dms = [_distance_mats(d, seq // d) for d in DILATIONS]
    same = _same_head()
    col0 = 2 * D_CONV // LANES
    n_hp = D_ATT // LANES

    n_ride = len(ride)
    assert not any(scatter for _, scatter in ride), "the forward's ride is an all-gather"

    def body(*refs):
        q_ref, k_ref, v_ref, gq_ref, gk_ref, sl_ref, dm1, dm4, dm16, same_ref = refs[:10]
        ride_in = refs[10:10 + n_ride]
        y_ref, lse_ref = refs[10 + n_ride:12 + n_ride]
        ride_out = refs[12 + n_ride:12 + 2 * n_ride]
        (qf, kf, qp, kp, vp, oml_p, oml_1, o4, o16, m4, m16, l4, l16,
         tmp) = refs[12 + 2 * n_ride:26 + 2 * n_ride]
        natural = {1: (o4, m4, l4), 2: (o16, m16, l16)}
        ride_args = (ride_in, ride_out) + tuple(refs[26 + 2 * n_ride:])
        step = pl.program_id(0) * n_hp + pl.program_id(1)
        n_steps = n_seq * n_hp

        if n_ride:
            for phase, at in enumerate((0, (3 * n_steps) // 4)):
                @pl.when(step == at)
                def _(phase=phase):
                    _gather_by_chip_phase(phase, *ride_args)

        dm_refs = (dm1, dm4, dm16)
        _qk_normalise(q_ref, gq_ref, same_ref, qf, seq, HEAD_DIM ** -0.5, ATT_ROWS)
        _qk_normalise(k_ref, gk_ref, same_ref, kf, seq, 1.0, ATT_ROWS)
        slopes = (sl_ref[0:1, 0:1], sl_ref[1:2, 0:1])
        for pi, dil in enumerate(DILATIONS):
            seg = seq // dil
            kw = min(2 * Q_BLOCK, seg)
            _permute_rows_by_head(qp, qf, dil, seq, tmp)
            _permute_rows(kp, kf, dil, seq, tmp)
            _permute_rows(vp, v_ref, dil, seq, tmp)

            def blk(it, carry, seg=seg, kw=kw, pi=pi, dst=oml_1 if pi == 0 else oml_p):
                first = _first_head(Q_BLOCK)
                chains = [(sub, h) for sub in range(ATT_FWD_UNROLL) for h in range(2)]
                win = [_window(it * ATT_FWD_UNROLL + sub, seg) for sub in range(ATT_FWD_UNROLL)]
                s = {}
                for sub, h in chains:
                    i0, ks, var = win[sub]
                    s[sub, h] = lax.dot_general(qp[h, pl.ds(i0, Q_BLOCK), :], kp[pl.ds(ks, kw), :], NT,
                                                preferred_element_type=F32) - slopes[h] * dm_refs[pi][var]
                m, l, p = {}, {}, {}
                for c in chains:
                    m[c] = jnp.max(s[c], axis=1, keepdims=True)
                    e = jnp.exp(s[c] - m[c])
                    l[c] = jnp.sum(e, axis=1, keepdims=True)
                    p[c] = e.astype(BF16)
                o = {}
                for sub, h in chains:
                    o[sub, h] = jnp.dot(p[sub, h], vp[pl.ds(win[sub][1], kw), :], preferred_element_type=F32)
                packed = [jnp.concatenate([jnp.where(first, t[sub, 0], t[sub, 1]) for t in (o, m, l)], axis=1)
                          for sub in range(ATT_FWD_UNROLL)]
                span = ATT_FWD_UNROLL * Q_BLOCK
                dst[pl.ds(pl.multiple_of(it * span, span), span), :] = jnp.concatenate(packed, axis=0)
                return carry

            lax.fori_loop(0, seq // (Q_BLOCK * ATT_FWD_UNROLL), blk, 0)
            for n, nat in enumerate(natural.get(pi, ())):
                _scatter_segments(nat, lambda start, size, n=n: oml_p[pl.ds(start, size), pl.ds(n * LANES, LANES)],
                                  dil, seq, tmp, accumulate=False)

        def merge(ci, carry):
            rows = pl.ds(pl.multiple_of(ci * ATT_ROWS, ATT_ROWS), ATT_ROWS)

            def part(pi, n):
                return oml_1[rows, pl.ds(n * LANES, LANES)] if pi == 0 else natural[pi][n][rows, :]

            ms = [part(pi, 1) for pi in range(3)]
            m_all = jnp.maximum(jnp.maximum(ms[0], ms[1]), ms[2])
            es = [jnp.exp(m - m_all) for m in ms]
            l_all = sum(part(pi, 2) * es[pi] for pi in range(3))
            inv = 1.0 / l_all
            o = sum(part(pi, 0) * (es[pi] * inv) for pi in range(3))
            y_ref[rows, :] = o.astype(BF16)
            lse_ref[rows, :] = m_all + jnp.log(l_all)
            return carry

        lax.fori_loop(0, seq // ATT_ROWS, merge, 0)

        if n_ride:
            @pl.when(step == n_steps - 1)
            def _():
                _gather_by_chip_phase(2, *ride_args)

    def col(off):
        return pl.BlockSpec((None, None, seq, LANES), lambda b, hp: (col0 + off * n_hp + hp, b, 0, 0))

    def whole(arr):
        return pl.BlockSpec(arr.shape, lambda b, hp: (0,) * arr.ndim)

    rows_f32 = pltpu.VMEM((seq, LANES), F32)
    rows_bf16 = pltpu.VMEM((seq, LANES), BF16)
    return _call(
        body, name="attn_fwd", grid=(n_seq, n_hp),
        in_specs=[col(0), col(1), col(2), whole(g_q2), whole(g_k2),
                  pl.BlockSpec((None, 8, LANES), lambda b, hp: (hp, 0, 0)),
                  whole(dms[0]), whole(dms[1]), whole(dms[2]), whole(same)] + [ANY_SPEC] * n_ride,
        out_specs=[pl.BlockSpec((None, seq, LANES), lambda b, hp: (b, 0, hp)),
                   pl.BlockSpec((None, seq, LANES), lambda b, hp: (b, 0, hp))] + [ANY_SPEC] * n_ride,
        out_shape=[_sds((n_seq, seq, D_ATT), BF16), _sds((n_seq, seq, D_ATT), F32)] + _exchange_shapes(ride),
        scratch_shapes=[rows_f32, rows_f32, pltpu.VMEM((2, seq, LANES), BF16), rows_bf16, rows_bf16]
        + [pltpu.VMEM((seq, 3 * LANES), F32)] * 2 + [rows_f32] * 7 + _exchange_sems(n_ride),
        compiler_params=_params(("arbitrary", "arbitrary")),
    )(proj3, proj3, proj3, g_q2, g_k2, _alibi_rows(), *dms, same, *[a for a, _ in ride])


def _attn_bwd(proj3, do3, y_att3, lse3, g_q2, g_k2, ride):
    _, n_seq, seq, _ = proj3.shape
    dms = [_distance_mats(d, seq // d) for d in DILATIONS]
    same = _same_head()
    col0 = 2 * D_CONV // LANES
    n_hp = D_ATT // LANES

    n_ride = len(ride)
    ride_scatter = [s for _, s in ride]

    def body(*refs):
        (q_ref, k_ref, v_ref, do_ref, o_ref, lse_ref, gq_ref, gk_ref, sl_ref, dm1, dm4, dm16,
         same_ref) = refs[:13]
        ride_in = refs[13:13 + n_ride]
        dq_ref, dk_ref, dv_ref, dg_ref = refs[13 + n_ride:17 + n_ride]
        ride_out = refs[17 + n_ride:17 + 2 * n_ride]
        (qf, kf, qp, dop, kp, vp, sn, sp, dqp, dkp, dvp, dqn, dkn, dvn,
         tmp) = refs[17 + 2 * n_ride:32 + 2 * n_ride]
        ride_args = (ride_scatter, ride_in, ride_out) + tuple(refs[32 + 2 * n_ride:])
        dm_refs = (dm1, dm4, dm16)
        step = pl.program_id(0) * n_hp + pl.program_id(1)

        @pl.when(step == 0)
        def _():
            _exchange_start(*ride_args)
            dg_ref[...] = jnp.zeros_like(dg_ref)

        _qk_normalise(q_ref, gq_ref, same_ref, qf, seq, HEAD_DIM ** -0.5, ATT_BWD_ROWS)
        _qk_normalise(k_ref, gk_ref, same_ref, kf, seq, 1.0, ATT_BWD_ROWS)

        def stats(ci, carry):
            rows = pl.ds(pl.multiple_of(ci * ATT_BWD_ROWS, ATT_BWD_ROWS), ATT_BWD_ROWS)
            first = _first_head(ATT_BWD_ROWS)
            sn[0, rows, :], sn[1, rows, :] = _per_head(lse_ref[rows, :], first)
            prod = do_ref[rows, :] * o_ref[rows, :].astype(F32)
            sn[2, rows, :], sn[3, rows, :] = _per_head(_head_sum(prod, same_ref), first)
            return carry

        lax.fori_loop(0, seq // ATT_BWD_ROWS, stats, 0)
        slopes = (sl_ref[0:1, 0:1], sl_ref[1:2, 0:1])
        half = seq // (Q_BLOCK * ATT_UNROLL)
        region = seq // ATT_UNROLL

        for pi, dil in enumerate(DILATIONS):
            seg = seq // dil
            kw = min(2 * Q_BLOCK, seg)
            _permute_rows_by_head(qp, qf, dil, seq, tmp)
            _permute_rows_by_head(dop, do_ref, dil, seq, tmp)
            _permute_rows(kp, kf, dil, seq, tmp)
            _permute_rows(vp, v_ref, dil, seq, tmp)
            if dil == 1:
                st = sn
            else:
                staged = dil > STRIDE
                assert not staged or DILATIONS[pi - 1] == STRIDE
                st, earlier = (sn, sp) if staged else (sp, sn)
                for n in range(4):
                    _permute_rows(st.at[n], earlier.at[n], dil, seq, tmp, staged)
            def touched(sub, seg=seg):
                lo, hi = sub * region, (sub + 1) * region
                if seg < region:
                    return lo, hi
                seg0 = lo // seg * seg
                return max(lo - RADIUS, seg0), min(hi + RADIUS, seg0 + seg)

            def summed(acc, start, size, touched=touched):
                pieces = []
                for c0 in range(start, start + size, RADIUS):
                    owners = [s for s in range(ATT_UNROLL) if touched(s)[0] <= c0 and c0 + RADIUS <= touched(s)[1]]
                    if pieces and pieces[-1][2] == owners:
                        pieces[-1][1] += RADIUS
                    else:
                        pieces.append([c0, RADIUS, owners])
                vals = [sum(acc[o, pl.ds(c0, n), :] for o in owners) for c0, n, owners in pieces]
                return vals[0] if len(vals) == 1 else jnp.concatenate(vals, axis=0)

            for sub in range(ATT_UNROLL):
                lo, hi = touched(sub)
                w0 = _window(sub * half, seg)[1]
                assert lo <= w0 and w0 + kw <= hi
                for z0, z1 in ((lo, w0), (w0 + kw, hi)):
                    if z1 > z0:
                        dkp[sub, pl.ds(z0, z1 - z0), :] = jnp.zeros((z1 - z0, LANES), F32)
                        dvp[sub, pl.ds(z0, z1 - z0), :] = jnp.zeros((z1 - z0, LANES), F32)

            def blk(it, carry, seg=seg, kw=kw, pi=pi, st=st):
                first = _first_head(Q_BLOCK)
                chains = [(sub, h) for sub in range(ATT_UNROLL) for h in range(2)]
                win = [_window(it + sub * half, seg) for sub in range(ATT_UNROLL)]
                qrows = [pl.ds(w[0], Q_BLOCK) for w in win]
                krows = [pl.ds(w[1], kw) for w in win]

                def over_keys(n, sub):
                    t = st[n, qrows[sub], :]
                    return t if kw == LANES else jnp.concatenate([t] * (kw // LANES), axis=1)

                s, dp = {}, {}
                for sub, h in chains:
                    s[sub, h] = lax.dot_general(qp[h, qrows[sub], :], kp[krows[sub], :], NT,
                                                preferred_element_type=F32) - slopes[h] * dm_refs[pi][win[sub][2]]
                    dp[sub, h] = lax.dot_general(dop[h, qrows[sub], :], vp[krows[sub], :], NT,
                                                 preferred_element_type=F32)
                p, ds = {}, {}
                for sub, h in chains:
                    e = jnp.exp(s[sub, h] - over_keys(h, sub))
                    ds[sub, h] = (e * (dp[sub, h] - over_keys(2 + h, sub))).astype(BF16)
                    p[sub, h] = e.astype(BF16)
                dq, dk, dv = {}, {}, {}
                for sub, h in chains:
                    dq[sub, h] = jnp.dot(ds[sub, h], kp[krows[sub], :], preferred_element_type=F32)
                    dk[sub, h] = lax.dot_general(ds[sub, h], qp[h, qrows[sub], :], TN, preferred_element_type=F32)
                    dv[sub, h] = lax.dot_general(p[sub, h], dop[h, qrows[sub], :], TN, preferred_element_type=F32)
                dq_dst = dqn if pi == 0 else dqp
                for sub in range(ATT_UNROLL):
                    dq_dst[qrows[sub], :] = jnp.where(first, dq[sub, 0], dq[sub, 1])
                    if isinstance(it, int):
                        dkp[sub, krows[sub], :] = dk[sub, 0] + dk[sub, 1]
                        dvp[sub, krows[sub], :] = dv[sub, 0] + dv[sub, 1]
                    else:
                        dkp[sub, krows[sub], :] += dk[sub, 0] + dk[sub, 1]
                        dvp[sub, krows[sub], :] += dv[sub, 0] + dv[sub, 1]
                return carry

            blk(0, 0)
            lax.fori_loop(1, half, blk, 0)
            first_pattern = pi == 0
            if first_pattern:
                for r0 in range(0, seq, region):
                    rows = pl.ds(r0, region)
                    dkn[rows, :] = summed(dkp, r0, region)
                    dvn[rows, :] = summed(dvp, r0, region)
            else:
                _scatter_segments(dqn, lambda start, size: dqp[pl.ds(start, size), :], dil, seq, tmp, accumulate=True)
                for nat, acc in ((dkn, dkp), (dvn, dvp)):
                    _scatter_segments(nat, lambda start, size, acc=acc: summed(acc, start, size),
                                      dil, seq, tmp, accumulate=True)

        def finish(ci, carry):
            rows = pl.ds(pl.multiple_of(ci * ATT_BWD_ROWS, ATT_BWD_ROWS), ATT_BWD_ROWS)
            for src_ref, g_ref, dn, dst_ref, scale, row in (
                    (q_ref, gq_ref, dqn, dq_ref, HEAD_DIM ** -0.5, 0), (k_ref, gk_ref, dkn, dk_ref, 1.0, 1)):
                xv = src_ref[rows, :]
                r = lax.rsqrt(_head_mean(xv * xv, same_ref) + EPS)
                xhat = xv * r
                d = dn[rows, :] * scale
                dg_ref[row:row + 1, :] += jnp.sum(d * xhat, axis=0, keepdims=True)
                dxh = d * g_ref[...]
                dst_ref[rows, :] = (r * (dxh - xhat * _head_mean(dxh * xhat, same_ref))).astype(BF16)
            dv_ref[rows, :] = dvn[rows, :].astype(BF16)
            return carry

        lax.fori_loop(0, seq // ATT_BWD_ROWS, finish, 0)

        @pl.when(step == n_seq * n_hp - 1)
        def _():
            _exchange_wait(*ride_args)

    def col(off):
        return pl.BlockSpec((None, None, seq, LANES), lambda b, hp: (col0 + off * n_hp + hp, b, 0, 0))

    def whole(arr):
        return pl.BlockSpec(arr.shape, lambda b, hp: (0,) * arr.ndim)

    att = pl.BlockSpec((None, seq, LANES), lambda b, hp: (b, 0, hp))
    rows_f32 = pltpu.VMEM((seq, LANES), F32)
    rows_bf16 = pltpu.VMEM((seq, LANES), BF16)
    by_head_bf16 = pltpu.VMEM((2, seq, LANES), BF16)
    per_sub_f32 = pltpu.VMEM((ATT_UNROLL, seq, LANES), F32)
    stats_f32 = pltpu.VMEM((4, seq, LANES), F32)
    return _call(
        body, name="attn_bwd", grid=(n_seq, n_hp),
        in_specs=[col(0), col(1), col(2), att, att, att, whole(g_q2), whole(g_k2),
                  pl.BlockSpec((None, 8, LANES), lambda b, hp: (hp, 0, 0)),
                  whole(dms[0]), whole(dms[1]), whole(dms[2]), whole(same)] + [ANY_SPEC] * n_ride,
        out_specs=[att, att, att, pl.BlockSpec((8, LANES), lambda b, hp: (0, 0))] + [ANY_SPEC] * n_ride,
        out_shape=[_sds((n_seq, seq, D_ATT), BF16)] * 3 + [_sds((8, LANES), F32)] + _exchange_shapes(ride),
        scratch_shapes=[rows_f32, rows_f32, by_head_bf16, by_head_bf16, rows_bf16, rows_bf16, stats_f32, stats_f32,
                        rows_f32, per_sub_f32, per_sub_f32, rows_f32, rows_f32, rows_f32, rows_f32]
        + _exchange_sems(n_ride),
        compiler_params=_params(("arbitrary", "arbitrary")),
    )(proj3, proj3, proj3, do3, y_att3, lse3, g_q2, g_k2, _alibi_rows(), *dms, same, *[a for a, _ in ride])


def _mix_out(uc2, y_att2, x2, mod8, g_ln, b_ln, g_ffn, w_out, seq, tm=512):
    tokens = x2.shape[0]
    per_seq = seq // tm

    def body(uc_ref, ya_ref, x_ref, m_ref, gl_ref, bl_ref, gf_ref, w_ref, yc_ref, mix_ref, x1_ref, h2_ref):
        uc = uc_ref[...]
        mu = jnp.mean(uc, axis=-1, keepdims=True)
        cen = uc - mu
        rs = lax.rsqrt(jnp.mean(cen * cen, axis=-1, keepdims=True) + EPS)
        z = cen * rs * gl_ref[...] + bl_ref[...]
        yc = (z * _sig(z)).astype(BF16)
        yc_ref[...] = yc
        mix = (jnp.dot(yc, w_ref[pl.ds(0, D_CONV), :], preferred_element_type=F32)
               + jnp.dot(ya_ref[...], w_ref[pl.ds(D_CONV, D_ATT), :], preferred_element_type=F32))
        mix_ref[...] = mix.astype(BF16)
        x1 = x_ref[...] + m_ref[2:3, :] * mix
        x1_ref[...] = x1
        r = lax.rsqrt(jnp.mean(x1 * x1, axis=-1, keepdims=True) + EPS)
        h2_ref[...] = ((x1 * r * gf_ref[...]) * (1.0 + m_ref[4:5, :]) + m_ref[3:4, :]).astype(BF16)

    def rows(width):
        return pl.BlockSpec((tm, width), lambda i: (i, 0))

    def vec(width):
        return pl.BlockSpec((1, width), lambda i: (0, 0))

    return _call(
        body, name="mix_out", grid=(tokens // tm,),
        in_specs=[rows(D_CONV), rows(D_ATT), rows(D_MODEL),
                  pl.BlockSpec((None, 8, D_MODEL), lambda i: (i // per_seq, 0, 0)),
                  vec(D_CONV), vec(D_CONV), vec(D_MODEL),
                  pl.BlockSpec((D_MODEL, D_MODEL), lambda i: (0, 0))],
        out_specs=[rows(D_CONV), rows(D_MODEL), rows(D_MODEL), rows(D_MODEL)],
        out_shape=[_sds((tokens, D_CONV), BF16), _sds((tokens, D_MODEL), BF16),
                   _sds((tokens, D_MODEL), F32), _sds((tokens, D_MODEL), BF16)],
        compiler_params=_params(("parallel",)),
    )(uc2, y_att2, x2, mod8, g_ln, b_ln, g_ffn, w_out)


def _mix_out_bwd(dmix, uc2, g_ln, b_ln, w_out, tm=512):
    tokens = dmix.shape[0]

    def body(dm_ref, uc_ref, gl_ref, bl_ref, w_ref, duc_ref, do_ref, dgb_ref):
        @pl.when(pl.program_id(0) == 0)
        def _():
            dgb_ref[...] = jnp.zeros_like(dgb_ref)

        dmv = dm_ref[...]
        dyc = lax.dot_general(dmv, w_ref[pl.ds(0, D_CONV), :], NT, preferred_element_type=F32)
        do_ref[...] = lax.dot_general(dmv, w_ref[pl.ds(D_CONV, D_ATT), :], NT, preferred_element_type=F32)
        uc = uc_ref[...]
        mu = jnp.mean(uc, axis=-1, keepdims=True)
        cen = uc - mu
        rs = lax.rsqrt(jnp.mean(cen * cen, axis=-1, keepdims=True) + EPS)
        xh = cen * rs
        z = xh * gl_ref[...] + bl_ref[...]
        sg = _sig(z)
        dz = dyc * (sg * (1.0 + z * (1.0 - sg)))
        dgb_ref[0:1, :] += jnp.sum(dz * xh, axis=0, keepdims=True)
        dgb_ref[1:2, :] += jnp.sum(dz, axis=0, keepdims=True)
        dxh = dz * gl_ref[...]
        duc_ref[...] = rs * (dxh - jnp.mean(dxh, axis=-1, keepdims=True)
                             - xh * jnp.mean(dxh * xh, axis=-1, keepdims=True))

    return _call(
        body, name="mix_out_bwd", grid=(tokens // tm,),
        in_specs=[pl.BlockSpec((tm, D_MODEL), lambda i: (i, 0)),
                  pl.BlockSpec((tm, D_CONV), lambda i: (i, 0)),
                  pl.BlockSpec((1, D_CONV), lambda i: (0, 0)),
                  pl.BlockSpec((1, D_CONV), lambda i: (0, 0)),
                  pl.BlockSpec((D_MODEL, D_MODEL), lambda i: (0, 0))],
        out_specs=[pl.BlockSpec((tm, D_CONV), lambda i: (i, 0)),
                   pl.BlockSpec((tm, D_ATT), lambda i: (i, 0)),
                   pl.BlockSpec((8, D_CONV), lambda i: (0, 0))],
        out_shape=[_sds((tokens, D_CONV), F32), _sds((tokens, D_ATT), F32), _sds((8, D_CONV), F32)],
        compiler_params=_params(("arbitrary",)),
    )(dmix, uc2, g_ln, b_ln, w_out)


FF_TILE = 256
FF_TILES = D_FF // FF_TILE


def _load_once(hbm_refs, vmem_refs, sems):
    @pl.when(pl.program_id(0) == 0)
    def _():
        copies = [pltpu.make_async_copy(src, dst, sems.at[n]) for n, (src, dst) in enumerate(zip(hbm_refs, vmem_refs))]
        for cp in copies:
            cp.start()
        for cp in copies:
            cp.wait()


def _ffn_fwd(h2, w_gate_t, w_up_t, w_down, x1, target, mod8, seq, tm=512):
    tokens = h2.shape[0]
    per_seq = seq // tm
    n_seq = tokens // seq

    def body(h_ref, m_ref, x1_ref, t_ref, wg_hbm, wu_hbm, wd_hbm, gate_ref, up_ref, dy_ref, df_ref, sq_ref, dgf_ref,
             wg, wu, wd, w_sems):
        i = pl.program_id(0)
        _load_once((wg_hbm, wu_hbm, wd_hbm), (wg, wu, wd), w_sems)

        @pl.when(i == 0)
        def _():
            sq_ref[...] = jnp.zeros_like(sq_ref)

        @pl.when(i % per_seq == 0)
        def _():
            dgf_ref[...] = jnp.zeros_like(dgf_ref)

        hv = h_ref[...]

        def gate_up(t):
            rows = pl.ds(t * FF_TILE, FF_TILE)
            return (lax.dot_general(hv, wg[rows, :], NT, preferred_element_type=F32),
                    lax.dot_general(hv, wu[rows, :], NT, preferred_element_type=F32))

        fv = jnp.zeros((tm, D_MODEL), F32)
        ahead = gate_up(0)
        for t in range(FF_TILES):
            gate, up = ahead
            if t + 1 < FF_TILES:
                ahead = gate_up(t + 1)
            gate_ref[t] = gate.astype(BF16)
            up_ref[t] = up.astype(BF16)
            act = (gate * _sig(gate) * up).astype(BF16)
            fv = fv + jnp.dot(act, wd[pl.ds(t * FF_TILE, FF_TILE), :], preferred_element_type=F32)
        gate_f = m_ref[5:6, :]
        diff = x1_ref[...] + gate_f * fv - t_ref[...]
        sq_ref[0:1, :] += jnp.sum(diff * diff, axis=0, keepdims=True)
        dy = diff * (1.0 / D_MODEL)
        dy_ref[...] = dy
        df_ref[...] = (gate_f * dy).astype(BF16)
        dgf_ref[0:1, :] += jnp.sum(dy * fv, axis=0, keepdims=True)

    rows_spec = pl.BlockSpec((tm, D_MODEL), lambda i: (i, 0))
    per = pl.BlockSpec((None, 8, D_MODEL), lambda i: (i // per_seq, 0, 0))
    tiles = pl.BlockSpec((FF_TILES, tm, FF_TILE), lambda i: (0, i, 0))
    weight = pltpu.VMEM((D_FF, D_MODEL), BF16)
    return _call(
        body, name="ffn_fwd", grid=(tokens // tm,),
        in_specs=[rows_spec, per, rows_spec, rows_spec, ANY_SPEC, ANY_SPEC, ANY_SPEC],
        out_specs=[tiles, tiles, rows_spec, rows_spec, pl.BlockSpec((8, D_MODEL), lambda i: (0, 0)), per],
        out_shape=[_sds((FF_TILES, tokens, FF_TILE), BF16), _sds((FF_TILES, tokens, FF_TILE), BF16),
                   _sds((tokens, D_MODEL), F32), _sds((tokens, D_MODEL), BF16),
                   _sds((8, D_MODEL), F32), _sds((n_seq, 8, D_MODEL), F32)],
        scratch_shapes=[weight, weight, weight, pltpu.SemaphoreType.DMA((3,))],
        compiler_params=_params(("arbitrary",)),
    )(h2, mod8, x1, target, w_gate_t, w_up_t, w_down)


def _ffn_bwd(df, gate, up, w_gate_t, w_up_t, w_down, x1, dy, mix, mod8, g_ffn, seq, tm=256):
    tokens = df.shape[0]
    per_seq = seq // tm
    n_seq = tokens // seq

    def body(df_ref, gate_ref, up_ref, m_ref, g_ref, x1_ref, dy_ref, mix_ref, wg_hbm, wu_hbm, wd_hbm,
             dgate_ref, dup_ref, act_ref, dx1_ref, dmix_ref, dg_ref, dm_ref, wg, wu, wd, w_sems):
        i = pl.program_id(0)
        _load_once((wg_hbm, wu_hbm, wd_hbm), (wg, wu, wd), w_sems)

        @pl.when(i == 0)
        def _():
            dg_ref[...] = jnp.zeros_like(dg_ref)

        @pl.when(i % per_seq == 0)
        def _():
            dm_ref[...] = jnp.zeros_like(dm_ref)

        dfv = df_ref[...]

        def d_act(t):
            return lax.dot_general(dfv, wd[pl.ds(t * FF_TILE, FF_TILE), :], NT, preferred_element_type=F32)

        dh = jnp.zeros((tm, D_MODEL), F32)
        ahead = d_act(0)
        for t in range(FF_TILES):
            dact = ahead
            if t + 1 < FF_TILES:
                ahead = d_act(t + 1)
            rows = pl.ds(t * FF_TILE, FF_TILE)
            gv = gate_ref[t].astype(F32)
            uv = up_ref[t].astype(F32)
            sg = _sig(gv)
            silu = gv * sg
            act_ref[t] = (silu * uv).astype(BF16)
            dup = (dact * silu).astype(BF16)
            dgate = (dact * uv * (sg * (1.0 + gv * (1.0 - sg)))).astype(BF16)
            dup_ref[t] = dup
            dgate_ref[t] = dgate
            dh = dh + (jnp.dot(dgate, wg[rows, :], preferred_element_type=F32)
                       + jnp.dot(dup, wu[rows, :], preferred_element_type=F32))
        g = g_ref[...]
        x1v = x1_ref[...]
        rs = lax.rsqrt(jnp.mean(x1v * x1v, axis=-1, keepdims=True) + EPS)
        xhat = x1v * rs
        dm_ref[0:1, :] += jnp.sum(dh, axis=0, keepdims=True)
        dm_ref[1:2, :] += jnp.sum(dh * (xhat * g), axis=0, keepdims=True)
        dn = dh * (1.0 + m_ref[4:5, :])
        dg_ref[0:1, :] += jnp.sum(dn * xhat, axis=0, keepdims=True)
        dxh = dn * g
        dx1 = dy_ref[...] + rs * (dxh - xhat * jnp.mean(dxh * xhat, axis=-1, keepdims=True))
        dx1_ref[...] = dx1
        dm_ref[2:3, :] += jnp.sum(dx1 * mix_ref[...].astype(F32), axis=0, keepdims=True)
        dmix_ref[...] = (m_ref[2:3, :] * dx1).astype(BF16)

    rows_spec = pl.BlockSpec((tm, D_MODEL), lambda i: (i, 0))
    per = pl.BlockSpec((None, 8, D_MODEL), lambda i: (i // per_seq, 0, 0))
    tiles = pl.BlockSpec((FF_TILES, tm, FF_TILE), lambda i: (0, i, 0))
    weight = pltpu.VMEM((D_FF, D_MODEL), BF16)
    return _call(
        body, name="ffn_bwd", grid=(tokens // tm,),
        in_specs=[rows_spec, tiles, tiles, per, pl.BlockSpec((1, D_MODEL), lambda i: (0, 0)),
                  rows_spec, rows_spec, rows_spec, ANY_SPEC, ANY_SPEC, ANY_SPEC],
        out_specs=[tiles, tiles, tiles, rows_spec, rows_spec, pl.BlockSpec((8, D_MODEL), lambda i: (0, 0)), per],
        out_shape=[_sds((FF_TILES, tokens, FF_TILE), BF16)] * 3
        + [_sds((tokens, D_MODEL), F32), _sds((tokens, D_MODEL), BF16),
           _sds((8, D_MODEL), F32), _sds((n_seq, 8, D_MODEL), F32)],
        scratch_shapes=[weight, weight, weight, pltpu.SemaphoreType.DMA((3,))],
        compiler_params=_params(("arbitrary",)),
    )(df, gate, up, mod8, g_ffn, x1, dy, mix, w_gate_t, w_up_t, w_down)


def _mix_in_bwd(d_a, d_g, d_q, d_k, d_v, w_in, x2, dx1, mod8, g_mix, seq, ride, tm=512):
    tokens = x2.shape[0]
    per_seq = seq // tm
    n_seq = tokens // seq
    parts = (d_a, d_g, d_q, d_k, d_v)
    width = D_CONV
    n_ride = len(ride)
    ride_scatter = [s for _, s in ride]

    def body(*refs):
        da_ref, dg_ref, dq_ref, dk_ref, dv_ref, w_ref, x_ref, dx1_ref, m_ref, g_ref = refs[:10]
        ride_in = refs[10:10 + n_ride]
        gx_ref, dgm_ref, dm_ref = refs[10 + n_ride:13 + n_ride]
        ride_args = (ride_scatter, ride_in, refs[13 + n_ride:13 + 2 * n_ride]) + tuple(refs[13 + 2 * n_ride:])
        i = pl.program_id(0)

        @pl.when(i == 0)
        def _():
            _exchange_start(*ride_args)
            dgm_ref[...] = jnp.zeros_like(dgm_ref)

        @pl.when(i % per_seq == 0)
        def _():
            dm_ref[...] = jnp.zeros_like(dm_ref)

        dh = jnp.zeros((tm, D_MODEL), F32)
        for n, ref in enumerate((da_ref, dg_ref, dq_ref, dk_ref, dv_ref)):
            dh = dh + jnp.dot(ref[...], w_ref[pl.ds(n * width, width), :], preferred_element_type=F32)
        xv = x_ref[...]
        r = lax.rsqrt(jnp.mean(xv * xv, axis=-1, keepdims=True) + EPS)
        xhat = xv * r
        g = g_ref[...]
        dm_ref[0:1, :] += jnp.sum(dh, axis=0, keepdims=True)
        dm_ref[1:2, :] += jnp.sum(dh * (xhat * g), axis=0, keepdims=True)
        dn = dh * (1.0 + m_ref[1:2, :])
        dgm_ref[0:1, :] += jnp.sum(dn * xhat, axis=0, keepdims=True)
        dxh = dn * g
        gx_ref[...] = dx1_ref[...] + r * (dxh - xhat * jnp.mean(dxh * xhat, axis=-1, keepdims=True))

        @pl.when(i == tokens // tm - 1)
        def _():
            _exchange_wait(*ride_args)

    rows = pl.BlockSpec((tm, D_MODEL), lambda i: (i, 0))
    half = pl.BlockSpec((tm, width), lambda i: (i, 0))
    per = pl.BlockSpec((None, 8, D_MODEL), lambda i: (i // per_seq, 0, 0))
    return _call(
        body, name="mix_in_bwd", grid=(tokens // tm,),
        in_specs=[half] * 5 + [pl.BlockSpec((D_IN, D_MODEL), lambda i: (0, 0)), rows, rows, per,
                               pl.BlockSpec((1, D_MODEL), lambda i: (0, 0))] + [ANY_SPEC] * n_ride,
        out_specs=[rows, pl.BlockSpec((8, D_MODEL), lambda i: (0, 0)), per] + [ANY_SPEC] * n_ride,
        out_shape=[_sds((tokens, D_MODEL), F32), _sds((8, D_MODEL), F32), _sds((n_seq, 8, D_MODEL), F32)]
        + _exchange_shapes(ride),
        scratch_shapes=_exchange_sems(n_ride),
        compiler_params=_params(("arbitrary",)),
    )(*parts, w_in, x2, dx1, mod8, g_mix, *[a for a, _ in ride])


def _grad_matmul_parts(a_parts, b_parts, name, tk=1024):
    tokens = a_parts[0].shape[0]
    na, nb = len(a_parts), len(b_parts)
    ma, nbw = a_parts[0].shape[1], b_parts[0].shape[1]

    n_k = tokens // tk

    def body(*refs):
        a_refs, b_refs, o_ref, acc = refs[:na], refs[na:na + nb], refs[na + nb], refs[na + nb + 1]

        @pl.when(pl.program_id(0) == 0)
        def _():
            acc[...] = jnp.zeros_like(acc)

        for i in range(na):
            for j in range(nb):
                acc[pl.ds(i * ma, ma), pl.ds(j * nbw, nbw)] += lax.dot_general(
                    a_refs[i][...], b_refs[j][...], TN, preferred_element_type=F32)

        @pl.when(pl.program_id(0) == n_k - 1)
        def _():
            o_ref[...] = acc[...].astype(o_ref.dtype)

    return _call(
        body, name=name, grid=(n_k,),
        in_specs=[pl.BlockSpec((tk, ma), lambda k: (k, 0))] * na + [pl.BlockSpec((tk, nbw), lambda k: (k, 0))] * nb,
        out_specs=pl.BlockSpec((na * ma, nb * nbw), lambda k: (0, 0)),
        out_shape=_sds((na * ma, nb * nbw), BF16),
        scratch_shapes=[pltpu.VMEM((na * ma, nb * nbw), F32)],
        compiler_params=_params(("arbitrary",)),
    )(*a_parts, *b_parts)


def _grad_matmul_tiles(a, b, name, tk=1024):
    tiled_b = b.ndim == 3
    tiles, tokens, width = b.shape if tiled_b else a.shape
    other = a.shape[1] if tiled_b else b.shape[1]
    out_tile = (other, width) if tiled_b else (width, other)
    n_k = tokens // tk

    def body(a_ref, b_ref, o_ref, acc):
        @pl.when(pl.program_id(0) == 0)
        def _():
            acc[...] = jnp.zeros_like(acc)

        for t in range(tiles):
            lhs = a_ref[...] if tiled_b else a_ref[t]
            rhs = b_ref[t] if tiled_b else b_ref[...]
            acc[t] += lax.dot_general(lhs, rhs, TN, preferred_element_type=F32)

        @pl.when(pl.program_id(0) == n_k - 1)
        def _():
            o_ref[...] = acc[...].astype(o_ref.dtype)

    flat = pl.BlockSpec((tk, other), lambda k: (k, 0))
    tiled = pl.BlockSpec((tiles, tk, width), lambda k: (0, k, 0))
    return _call(
        body, name=name, grid=(n_k,),
        in_specs=[flat, tiled] if tiled_b else [tiled, flat],
        out_specs=pl.BlockSpec((tiles,) + out_tile, lambda k: (0, 0, 0)),
        out_shape=_sds((tiles,) + out_tile, BF16),
        scratch_shapes=[pltpu.VMEM((tiles,) + out_tile, F32)],
        compiler_params=_params(("arbitrary",)),
    )(a, b)


def _adamw(w, m, v, g, name, n_parts=0, tr=256):
    rows, cols = w.shape
    tr = min(tr, rows)
    c1 = 1.0 - ADAM_B1 ** ADAM_STEP
    c2 = 1.0 - ADAM_B2 ** ADAM_STEP

    def body(w_ref, m_ref, v_ref, g_ref, go_ref, d_ref, mo_ref, vo_ref):
        if n_parts:
            gv = g_ref[0].astype(F32)
            for p in range(1, n_parts):
                gv = gv + g_ref[p].astype(F32)
        else:
            gv = g_ref[...]
        go_ref[...] = gv
        mn = ADAM_B1 * m_ref[...] + (1.0 - ADAM_B1) * gv
        vn = ADAM_B2 * v_ref[...] + (1.0 - ADAM_B2) * (gv * gv)
        mo_ref[...] = mn
        vo_ref[...] = vn
        d_ref[...] = -ADAM_LR * ((mn / c1) / (jnp.sqrt(vn / c2) + ADAM_EPS) + ADAM_WD * w_ref[...])

    blk = pl.BlockSpec((tr, cols), lambda i: (i, 0))
    g_spec = pl.BlockSpec((n_parts, tr, cols), lambda i: (0, i, 0)) if n_parts else blk
    return _call(
        body, name=name, grid=(rows // tr,),
        in_specs=[blk, blk, blk, g_spec], out_specs=[blk] * 4,
        out_shape=[_sds((rows, cols), F32)] * 4,
        compiler_params=_params(("parallel",)),
    )(w, m, v, g)


def _cols_to_full(blocks):
    n, r, c = blocks.shape
    return jnp.transpose(blocks, (1, 0, 2)).reshape(r, n * c)


def _pad_lanes(v, width):
    return jnp.pad(v, ((0, 0), (0, width - v.shape[1])))


def kernel(x, c, w_ada, b_ada, g_mix, w_in, w_dw, b_dw, g_conv_ln, b_conv_ln, g_q, g_k, w_out, g_ffn, w_gate, w_up, w_down, loss_target, m_w_ada, m_b_ada, m_g_mix, m_w_in, m_w_dw, m_b_dw, m_g_conv_ln, m_b_conv_ln, m_g_q, m_g_k, m_w_out, m_g_ffn, m_w_gate, m_w_up, m_w_down, v_w_ada, v_b_ada, v_g_mix, v_w_in, v_w_dw, v_b_dw, v_g_conv_ln, v_b_conv_ln, v_g_q, v_g_k, v_w_out, v_g_ffn, v_w_gate, v_w_up, v_w_down):
    n_seq, seq, _ = x.shape
    tokens = n_seq * seq
    me = 4 * lax.axis_index("x") + 2 * lax.axis_index("y") + lax.axis_index("c")
    ada_cols = w_ada.shape[2]
    dw_cols = w_dw.shape[2]

    def transposed(w):
        return jnp.transpose(w[0])

    (c_g, w_in_g, w_dw_g) = _gather_by_chip([c, transposed(w_in).astype(BF16), w_dw[0]], "gather_weights")
    c_all = c_g.reshape(N_DEV * n_seq, D_MODEL)
    w_in_t = w_in_g.reshape(D_IN, D_MODEL)
    w_dw_f = _cols_to_full(w_dw_g)

    b_cols = lax.dynamic_slice(b_ada, (0, me * ada_cols), (1, ada_cols))
    mod_cols = _ada_fwd(c_all, w_ada[0], b_cols)
    (mod_g,) = _exchange([(mod_cols, False)], "gather_mod")
    mod_mine = lax.dynamic_slice(mod_g, (0, me * n_seq, 0), (N_DEV, n_seq, ada_cols))
    mod = jnp.transpose(mod_mine, (1, 0, 2)).reshape(n_seq, N_MOD, D_MODEL)
    mod8 = jnp.pad(mod, ((0, 0), (0, 8 - N_MOD), (0, 0)))

    x2 = x.reshape(tokens, D_MODEL)
    h1, proj = _mix_in(x2, mod8, g_mix, w_in_t, seq)
    proj3 = proj.reshape(D_IN // LANES, n_seq, seq, LANES)
    uc3 = _conv_fwd(proj3, w_dw_f, b_dw)
    g_q2, g_k2 = jnp.tile(g_q, (1, 2)), jnp.tile(g_k, (1, 2))
    y_att3, lse3, w_out_g, w_gate_g, w_up_g, w_down_g = _attn_fwd(
        proj3, g_q2, g_k2,
        [(w_out[0].astype(BF16), False), (transposed(w_gate).astype(BF16), False),
         (transposed(w_up).astype(BF16), False), (w_down[0].astype(BF16), False)])
    w_out_f = w_out_g.reshape(D_MODEL, D_MODEL)
    w_gate_f = w_gate_g.reshape(D_FF, D_MODEL)
    w_up_f = w_up_g.reshape(D_FF, D_MODEL)
    w_down_f = w_down_g.reshape(D_FF, D_MODEL)
    uc2 = uc3.reshape(tokens, D_CONV)
    y_att2 = y_att3.reshape(tokens, D_ATT)
    y_conv, mix, x1, h2 = _mix_out(uc2, y_att2, x2, mod8, g_conv_ln, b_conv_ln, g_ffn, w_out_f, seq)
    gate, up, dy, df, sq, dgate_f = _ffn_fwd(
        h2, w_gate_f, w_up_f, w_down_f, x1, loss_target.reshape(tokens, D_MODEL), mod8, seq)

    dgate, dup, act, dx1, dmix, dg_ffn, dmod_f = _ffn_bwd(
        df, gate, up, w_gate_f, w_up_f, w_down_f, x1, dy, mix, mod8, g_ffn, seq)
    duc2, do2, dgb_ln = _mix_out_bwd(dmix, uc2, g_conv_ln, b_conv_ln, w_out_f)
    d_a3, d_g3, dw_dw_p, db_dw_p = _conv_bwd(duc2.reshape(n_seq, seq, D_CONV), proj3, w_dw_f)
    gw_gate = _grad_matmul_tiles(dgate, h2, "grad_w_gate")
    gw_up = _grad_matmul_tiles(dup, h2, "grad_w_up")
    gw_down = _grad_matmul_tiles(act, df, "grad_w_down")
    gw_out = _grad_matmul_parts([y_conv, y_att2], [dmix], "grad_w_out")
    d_q3, d_k3, d_v3, dg_qk, p_gate, p_up, p_down, p_out = _attn_bwd(
        proj3, do2.reshape(n_seq, seq, D_ATT), y_att3, lse3, g_q2, g_k2,
        [(gw_gate.reshape(N_DEV, D_FF // N_DEV, D_MODEL), True), (gw_up.reshape(N_DEV, D_FF // N_DEV, D_MODEL), True),
         (gw_down.reshape(N_DEV, D_FF // N_DEV, D_MODEL), True),
         (gw_out.reshape(N_DEV, D_MODEL // N_DEV, D_MODEL), True)])
    flat = lambda t: t.reshape(tokens, t.shape[-1])
    d_a, d_g, d_q, d_k, d_v = flat(d_a3), flat(d_g3), flat(d_q3), flat(d_k3), flat(d_v3)
    gw_in = _grad_matmul_parts([d_a, d_g, d_q, d_k, d_v], [h1], "grad_w_in")
    grad_x2, dg_mix, dmod_m, p_in = _mix_in_bwd(
        d_a, d_g, d_q, d_k, d_v, w_in_t, x2, dx1, mod8, g_mix, seq,
        [(gw_in.reshape(N_DEV, D_IN // N_DEV, D_MODEL), True)])

    dmod = jnp.concatenate([dmod_m[:, 0], dmod_m[:, 1], dmod_f[:, 2], dmod_f[:, 0], dmod_f[:, 1], dgate_f[:, 0]], axis=1)
    dg_q = dg_qk[0:1, 0:HEAD_DIM] + dg_qk[0:1, HEAD_DIM:]
    dg_k = dg_qk[1:2, 0:HEAD_DIM] + dg_qk[1:2, HEAD_DIM:]
    loss_part = (0.5 / D_MODEL) * jnp.sum(sq[0:1, :], axis=1, keepdims=True)
    small = jnp.concatenate(
        [dg_mix[0:1], dg_ffn[0:1], db_dw_p[0:1], dgb_ln[0:1], dgb_ln[1:2],
         _pad_lanes(dg_q, LANES), _pad_lanes(dg_k, LANES), _pad_lanes(loss_part, LANES)], axis=1)
    n_small = small.shape[1] - LANES

    (dmod_g, small_g, dw_g) = _exchange([(dmod, False), (small, False), (dw_dw_p, False)], "gather_small_grads")

    dmod_all = dmod_g.reshape(N_DEV * n_seq, N_MOD * D_MODEL)
    dmod_cols = lax.dynamic_slice(dmod_all, (0, me * ada_cols), (N_DEV * n_seq, ada_cols))
    gw_ada, gb_ada = _ada_bwd(c_all, dmod_cols, dmod_all)

    res = {}
    res["w_ada"] = _adamw(w_ada[0], m_w_ada[0], v_w_ada[0], gw_ada, "adamw_w_ada")
    res["b_ada"] = _adamw(b_ada, m_b_ada, v_b_ada, gb_ada, "adamw_b_ada")
    def adamw_transposed(w, m, v, parts, name, tr):
        outs = _adamw(transposed(w), transposed(m), transposed(v), parts, name, N_DEV, tr=tr)
        return tuple(jnp.transpose(o) for o in outs)

    res["w_in"] = adamw_transposed(w_in, m_w_in, v_w_in, p_in, "adamw_w_in", 160)
    res["w_out"] = _adamw(w_out[0], m_w_out[0], v_w_out[0], p_out, "adamw_w_out", N_DEV)
    res["w_gate"] = adamw_transposed(w_gate, m_w_gate, v_w_gate, p_gate, "adamw_w_gate", 176)
    res["w_up"] = adamw_transposed(w_up, m_w_up, v_w_up, p_up, "adamw_w_up", 176)
    res["w_down"] = _adamw(w_down[0], m_w_down[0], v_w_down[0], p_down, "adamw_w_down", N_DEV, tr=176)
    dw_mine = lax.dynamic_slice(dw_g, (0, 0, me * dw_cols), (N_DEV, CONV_WIDTH, dw_cols))
    res["w_dw"] = _adamw(w_dw[0], m_w_dw[0], v_w_dw[0], dw_mine, "adamw_w_dw", N_DEV)

    small_names = ["g_mix", "g_ffn", "b_dw", "g_conv_ln", "b_conv_ln", "g_q", "g_k"]
    small_w = {"g_mix": (g_mix, m_g_mix, v_g_mix), "g_ffn": (g_ffn, m_g_ffn, v_g_ffn), "b_dw": (b_dw, m_b_dw, v_b_dw),
               "g_conv_ln": (g_conv_ln, m_g_conv_ln, v_g_conv_ln), "b_conv_ln": (b_conv_ln, m_b_conv_ln, v_b_conv_ln),
               "g_q": (g_q, m_g_q, v_g_q), "g_k": (g_k, m_g_k, v_g_k)}
    widths = [max(small_w[n][0].shape[1], LANES) for n in small_names]
    packed = [jnp.concatenate([_pad_lanes(small_w[n][i], wd) for n, wd in zip(small_names, widths)], axis=1) for i in range(3)]
    outs = _adamw(packed[0], packed[1], packed[2], small_g[:, :, :n_small], "adamw_small", N_DEV)
    off = 0
    for n, wd in zip(small_names, widths):
        real = small_w[n][0].shape[1]
        res[n] = tuple(o[:, off:off + real] for o in outs)
        off += wd
    loss = jnp.sum(small_g[:, 0, n_small])

    order = ["w_ada", "b_ada", "g_mix", "w_in", "w_dw", "b_dw", "g_conv_ln", "b_conv_ln", "g_q", "g_k",
             "w_out", "g_ffn", "w_gate", "w_up", "w_down"]
    lead = {"w_ada", "w_in", "w_dw", "w_out", "w_gate", "w_up", "w_down"}
    grads, deltas, new_m, new_v = [], [], [], []
    for n in order:
        g, d, mn, vn = res[n]
        g, d, mn, vn = (t[None] if n in lead else t for t in (g, d, mn, vn))
        grads.append(g)
        deltas.append(d)
        new_m.append(mn)
        new_v.append(vn)
    return (loss, grad_x2.reshape(n_seq, seq, D_MODEL), *grads, *deltas, *new_m, *new_v)
```

```python
import numpy as np
import jax
import jax.numpy as jnp
from jax import lax
from jax.experimental import pallas as pl
from jax.experimental.pallas import tpu as pltpu

F32 = jnp.float32
BF16 = jnp.bfloat16

N_DEV = 8
D_MODEL = 1024
D_CONV = 512
D_ATT = 512
HEAD_DIM = 64
CONV_WIDTH = 31
D_IN = 2 * D_CONV + 3 * D_ATT
D_FF = 2816
N_MOD = 6
EPS = 1e-6
RADIUS = 64
DILATIONS = (1, 4, 16)
Q_BLOCK = 128
LANES = 128
VMEM_LIMIT = 56 * 1024 * 1024

ADAM_LR = 0.001
ADAM_B1 = 0.9
ADAM_B2 = 0.999
ADAM_EPS = 1e-08
ADAM_WD = 0.01
ADAM_STEP = 10

NT = (((1,), (1,)), ((), ()))
TN = (((0,), (0,)), ((), ()))


def _call(body, **kw):
    return pl.pallas_call(body, **kw)


def _params(sem=None, vmem=VMEM_LIMIT):
    return pltpu.CompilerParams(dimension_semantics=sem, vmem_limit_bytes=vmem)


def _sig(x):
    return 1.0 / (1.0 + jnp.exp(-x))


def _sds(shape, dtype):
    return jax.ShapeDtypeStruct(shape, dtype)


N_PEER = N_DEV - 1
ANY_SPEC = pl.BlockSpec(memory_space=pl.ANY)


def _exchange_copies(scatter, ins, outs, *sems):
    n = len(ins)
    if n == 0:
        return [], []
    send_sems, recv_sems, local_sems = sems
    x, y, c = lax.axis_index("x"), lax.axis_index("y"), lax.axis_index("c")
    me = 4 * x + 2 * y + c

    def src(a, slot):
        return ins[a].at[slot] if scatter[a] else ins[a]

    local = [pltpu.make_async_copy(src(a, me), outs[a].at[me], local_sems.at[a]) for a in range(n)]
    flights = []
    for k in range(1, N_DEV):
        px = 1 - x if k & 4 else x
        py = 1 - y if k & 2 else y
        pc = 1 - c if k & 1 else c
        pid = 4 * px + 2 * py + pc
        for a in range(n):
            i = a * N_PEER + k - 1
            send, recv = (pltpu.make_async_remote_copy(
                src_ref=src(a, pid), dst_ref=outs[a].at[slot],
                send_sem=send_sems.at[i], recv_sem=recv_sems.at[i],
                device_id=(px, py, pc), device_id_type=pl.DeviceIdType.MESH) for slot in (me, pid))
            flights.append((send, recv))
    return local, flights


def _exchange_start(*args):
    local, flights = _exchange_copies(*args)
    for cp in local:
        cp.start()
    for send, _ in flights:
        send.start()


def _exchange_wait(*args):
    local, flights = _exchange_copies(*args)
    for send, recv in flights:
        send.wait_send()
        recv.wait_recv()
    for cp in local:
        cp.wait()


def _exchange_shapes(items):
    return [_sds((N_DEV,) + tuple(arr.shape[1:] if scatter else arr.shape), arr.dtype) for arr, scatter in items]


def _exchange_sems(n):
    if n == 0:
        return []
    return [pltpu.SemaphoreType.DMA((n * N_PEER,)), pltpu.SemaphoreType.DMA((n * N_PEER,)),
            pltpu.SemaphoreType.DMA((n,))]


def _gather_by_chip_phase(phase, ins, outs, send_sems, recv_sems, local_sems):
    n = len(ins)
    per = N_PEER
    x, y, c = lax.axis_index("x"), lax.axis_index("y"), lax.axis_index("c")
    me, sibling = (x, y, c), (x, y, 1 - c)
    chips = [(1 - x, y), (x, 1 - y), (1 - x, 1 - y)]

    def slot(px, py, pc):
        return 4 * px + 2 * py + pc

    def copy(a, k, block, to, src=None):
        dst = outs[a].at[slot(*block)]
        return pltpu.make_async_remote_copy(
            src_ref=dst if src is None else src, dst_ref=dst,
            send_sem=send_sems.at[a * per + k], recv_sem=recv_sems.at[a * per + k],
            device_id=to, device_id_type=pl.DeviceIdType.MESH)

    local = [pltpu.make_async_copy(ins[a], outs[a].at[slot(*me)], local_sems.at[a]) for a in range(n)]
    first = []
    for a in range(n):
        first.append(copy(a, 0, me, sibling, src=ins[a]))
        first += [copy(a, 1 + j, me, (*chip, c), src=ins[a]) for j, chip in enumerate(chips)]
    passed = [copy(a, 4 + j, (*chip, c), sibling) for j, chip in enumerate(chips) for a in range(n)]
    if phase == 0:
        for cp in local + first:
            cp.start()
    elif phase == 1:
        for j, chip in enumerate(chips):
            for a in range(n):
                copy(a, 1 + j, (*chip, c), me).wait_recv()
        for cp in passed:
            cp.start()
    else:
        for a in range(n):
            copy(a, 0, sibling, me).wait_recv()
            for j, chip in enumerate(chips):
                copy(a, 4 + j, (*chip, 1 - c), me).wait_recv()
        for cp in first + passed:
            cp.wait_send()
        for cp in local:
            cp.wait()


def _gather_by_chip(arrays, name):
    n = len(arrays)

    def body(*refs):
        for phase in range(3):
            _gather_by_chip_phase(phase, refs[:n], refs[n:2 * n], *refs[2 * n:])

    return _call(
        body, name=name, out_shape=_exchange_shapes([(arr, False) for arr in arrays]),
        in_specs=[ANY_SPEC] * n, out_specs=[ANY_SPEC] * n, scratch_shapes=_exchange_sems(n),
    )(*arrays)


def _exchange(items, name):
    n = len(items)
    scatter = [s for _, s in items]

    def body(*refs):
        args = (scatter, refs[:n], refs[n:2 * n]) + tuple(refs[2 * n:])
        _exchange_start(*args)
        _exchange_wait(*args)

    return _call(
        body, name=name, out_shape=_exchange_shapes(items),
        in_specs=[ANY_SPEC] * n, out_specs=[ANY_SPEC] * n, scratch_shapes=_exchange_sems(n),
    )(*[a for a, _ in items])


def _ada_fwd(c_all, w_ada, b_cols):
    def body(c_ref, w_ref, b_ref, o_ref):
        cv = c_ref[...]
        sc = (cv * _sig(cv)).astype(BF16)
        o_ref[...] = jnp.dot(sc, w_ref[...].astype(BF16), preferred_element_type=F32) + b_ref[...]

    return _call(body, name="ada_fwd", out_shape=_sds((c_all.shape[0], w_ada.shape[1]), F32),
                 compiler_params=_params())(c_all, w_ada, b_cols)


def _ada_bwd(c_all, dmod_cols, dmod_all):
    def body(c_ref, dc_ref, da_ref, gw_ref, gb_ref):
        cv = c_ref[...]
        sc = (cv * _sig(cv)).astype(BF16)
        gw_ref[...] = lax.dot_general(sc, dc_ref[...].astype(BF16), TN, preferred_element_type=F32)
        gb_ref[...] = jnp.sum(da_ref[...], axis=0, keepdims=True)

    return _call(body, name="ada_bwd",
                 out_shape=[_sds((c_all.shape[1], dmod_cols.shape[1]), F32), _sds((1, dmod_all.shape[1]), F32)],
                 compiler_params=_params())(c_all, dmod_cols, dmod_all)


MIX_ROWS = 128


def _mix_in(x2, mod8, g_mix, w_in, seq, tm=512):
    tokens = x2.shape[0]
    per_seq = seq // tm

    def body(x_ref, m_ref, g_ref, wt_ref, h_ref, p_ref, w_ref):
        @pl.when(pl.program_id(0) == 0)
        def _():
            w_ref[...] = wt_ref[...].T

        def normed(c):
            rows = pl.ds(c * MIX_ROWS, MIX_ROWS)
            xv = x_ref[rows, :]
            r = lax.rsqrt(jnp.mean(xv * xv, axis=-1, keepdims=True) + EPS)
            hb = ((xv * r * g_ref[...]) * (1.0 + m_ref[1:2, :]) + m_ref[0:1, :]).astype(BF16)
            h_ref[rows, :] = hb
            return hb

        ahead = normed(0)
        for c in range(tm // MIX_ROWS):
            hb = ahead
            if c + 1 < tm // MIX_ROWS:
                ahead = normed(c + 1)
            p = jnp.dot(hb, w_ref[...], preferred_element_type=F32)
            for cb in range(D_IN // LANES):
                p_ref[cb, pl.ds(c * MIX_ROWS, MIX_ROWS), :] = p[:, cb * LANES:(cb + 1) * LANES]

    return _call(
        body, name="mix_in", grid=(tokens // tm,),
        in_specs=[pl.BlockSpec((tm, D_MODEL), lambda i: (i, 0)),
                  pl.BlockSpec((None, 8, D_MODEL), lambda i: (i // per_seq, 0, 0)),
                  pl.BlockSpec((1, D_MODEL), lambda i: (0, 0)),
                  pl.BlockSpec((D_IN, D_MODEL), lambda i: (0, 0))],
        out_specs=[pl.BlockSpec((tm, D_MODEL), lambda i: (i, 0)),
                   pl.BlockSpec((D_IN // LANES, tm, LANES), lambda i: (0, i, 0))],
        out_shape=[_sds((tokens, D_MODEL), BF16), _sds((D_IN // LANES, tokens, LANES), F32)],
        scratch_shapes=[pltpu.VMEM((D_MODEL, D_IN), BF16)],
        compiler_params=_params(("arbitrary",)),
    )(x2, mod8, g_mix, w_in)


CONV_ROWS = 128
CONV_DW_ROWS = 32
CONV_DW_UNROLL = 8
CONV_HALO = 16


def _fill_shifted(xp, sh, seq):
    for b in range(8):
        sh[b, pl.ds(0, seq + 24), :] = xp[pl.ds(b, seq + 24), :]


def _conv_fwd(proj3, w_dw, b_dw):
    _, n_seq, seq, _ = proj3.shape
    n_cb = D_CONV // LANES

    def body(a_ref, g_ref, w_ref, b_ref, uc_ref, xp, sh):
        zeros = jnp.zeros((CONV_HALO, LANES), F32)
        xp[pl.ds(0, CONV_HALO), :] = zeros
        xp[pl.ds(CONV_HALO + seq, CONV_HALO), :] = zeros
        xp[pl.ds(CONV_HALO, seq), :] = a_ref[...] * _sig(g_ref[...])
        _fill_shifted(xp, sh, seq)

        def blk(i, carry):
            t0 = pl.multiple_of(i * CONV_ROWS, CONV_ROWS)
            acc = jnp.zeros((CONV_ROWS, LANES), F32)
            for j in range(CONV_WIDTH):
                jj = j + 1
                acc = acc + sh[jj % 8, pl.ds(t0 + 8 * (jj // 8), CONV_ROWS), :] * w_ref[j:j + 1, :]
            uc_ref[pl.ds(t0, CONV_ROWS), :] = acc + b_ref[...]
            return carry

        lax.fori_loop(0, seq // CONV_ROWS, blk, 0)

    return _call(
        body, name="conv_fwd", grid=(n_seq, n_cb),
        in_specs=[pl.BlockSpec((None, None, seq, LANES), lambda b, cb: (cb, b, 0, 0)),
                  pl.BlockSpec((None, None, seq, LANES), lambda b, cb: (n_cb + cb, b, 0, 0)),
                  pl.BlockSpec((CONV_WIDTH, LANES), lambda b, cb: (0, cb)),
                  pl.BlockSpec((1, LANES), lambda b, cb: (0, cb))],
        out_specs=pl.BlockSpec((None, seq, LANES), lambda b, cb: (b, 0, cb)),
        out_shape=_sds((n_seq, seq, D_CONV), F32),
        scratch_shapes=[pltpu.VMEM((seq + 2 * CONV_HALO, LANES), F32),
                        pltpu.VMEM((8, seq + 2 * CONV_HALO, LANES), F32)],
        compiler_params=_params(("parallel", "parallel")),
    )(proj3, proj3, w_dw, b_dw)


def _conv_bwd(duc3, proj3, w_dw):
    _, n_seq, seq, _ = proj3.shape
    n_cb = D_CONV // LANES

    def body(duc_ref, a_ref, g_ref, w_ref, da_ref, dg_ref, dw_ref, db_ref, xp, sh):
        @pl.when(pl.program_id(1) == 0)
        def _():
            dw_ref[...] = jnp.zeros_like(dw_ref)
            db_ref[...] = jnp.zeros_like(db_ref)

        zeros = jnp.zeros((CONV_HALO, LANES), F32)
        xp[pl.ds(0, CONV_HALO), :] = zeros
        xp[pl.ds(CONV_HALO + seq, CONV_HALO), :] = zeros
        xp[pl.ds(CONV_HALO, seq), :] = a_ref[...] * _sig(g_ref[...])
        _fill_shifted(xp, sh, seq)
        for j0 in range(0, CONV_WIDTH, 8):
            taps = range(j0, min(j0 + 8, CONV_WIDTH))

            def wblk(i, accs, taps=taps):
                for u in range(CONV_DW_UNROLL):
                    t0 = pl.multiple_of((i * CONV_DW_UNROLL + u) * CONV_DW_ROWS, CONV_DW_ROWS)
                    d = duc_ref[pl.ds(t0, CONV_DW_ROWS), :]
                    accs = tuple(acc + d * sh[(j + 1) % 8, pl.ds(t0 + 8 * ((j + 1) // 8), CONV_DW_ROWS), :]
                                 for acc, j in zip(accs, taps))
                return accs

            accs = lax.fori_loop(0, seq // (CONV_DW_ROWS * CONV_DW_UNROLL), wblk,
                                 tuple(jnp.zeros((CONV_DW_ROWS, LANES), F32) for _ in taps))
            for acc, j in zip(accs, taps):
                dw_ref[j:j + 1, :] += jnp.sum(acc, axis=0, keepdims=True)
        db_ref[0:1, :] += jnp.sum(duc_ref[...], axis=0, keepdims=True)
        xp[pl.ds(CONV_HALO, seq), :] = duc_ref[...]
        _fill_shifted(xp, sh, seq)

        def ublk(i, carry):
            t0 = pl.multiple_of(i * CONV_ROWS, CONV_ROWS)
            acc = jnp.zeros((CONV_ROWS, LANES), F32)
            for j in range(CONV_WIDTH):
                jj = CONV_WIDTH - j
                acc = acc + sh[jj % 8, pl.ds(t0 + 8 * (jj // 8), CONV_ROWS), :] * w_ref[j:j + 1, :]
            av = a_ref[pl.ds(t0, CONV_ROWS), :]
            sg = _sig(g_ref[pl.ds(t0, CONV_ROWS), :])
            da_ref[pl.ds(t0, CONV_ROWS), :] = (acc * sg).astype(BF16)
            dg_ref[pl.ds(t0, CONV_ROWS), :] = (acc * av * sg * (1.0 - sg)).astype(BF16)
            return carry

        lax.fori_loop(0, seq // CONV_ROWS, ublk, 0)

    return _call(
        body, name="conv_bwd", grid=(n_cb, n_seq),
        in_specs=[pl.BlockSpec((None, seq, LANES), lambda cb, b: (b, 0, cb)),
                  pl.BlockSpec((None, None, seq, LANES), lambda cb, b: (cb, b, 0, 0)),
                  pl.BlockSpec((None, None, seq, LANES), lambda cb, b: (n_cb + cb, b, 0, 0)),
                  pl.BlockSpec((CONV_WIDTH, LANES), lambda cb, b: (0, cb))],
        out_specs=[pl.BlockSpec((None, seq, LANES), lambda cb, b: (b, 0, cb)),
                   pl.BlockSpec((None, seq, LANES), lambda cb, b: (b, 0, cb)),
                   pl.BlockSpec((32, LANES), lambda cb, b: (0, cb)),
                   pl.BlockSpec((8, LANES), lambda cb, b: (0, cb))],
        out_shape=[_sds((n_seq, seq, D_CONV), BF16), _sds((n_seq, seq, D_CONV), BF16),
                   _sds((32, D_CONV), F32), _sds((8, D_CONV), F32)],
        scratch_shapes=[pltpu.VMEM((seq + 2 * CONV_HALO, LANES), F32),
                        pltpu.VMEM((8, seq + 2 * CONV_HALO, LANES), F32)],
        compiler_params=_params(("parallel", "arbitrary")),
    )(duc3, proj3, proj3, w_dw)


MASKED = 1e30
ATT_ROWS = 1024
ATT_BWD_ROWS = 2048
ATT_UNROLL = 8
ATT_FWD_UNROLL = 8


def _distance_mats(dil, seg_len):
    kw = min(2 * Q_BLOCK, seg_len)
    offsets = (0, -RADIUS, -2 * RADIUS) if kw == 2 * Q_BLOCK else (0,)
    a = np.arange(Q_BLOCK)[:, None]
    b = np.arange(kw)[None, :]
    mats = []
    for off in offsets:
        rel = np.abs(b + off - a)
        mats.append(np.where(rel <= RADIUS, dil * rel, MASKED))
    return jnp.asarray(np.stack(mats).astype(np.float32))


def _alibi_rows():
    s = np.zeros((4, 8, LANES), np.float32)
    for hp in range(4):
        for hl in range(2):
            s[hp, hl, :] = 2.0 ** (-(2 * hp + hl + 1))
    return jnp.asarray(s)


def _window(n, seg_len):
    static = isinstance(n, int)
    i0 = n * Q_BLOCK if static else pl.multiple_of(n * Q_BLOCK, Q_BLOCK)
    if seg_len <= Q_BLOCK:
        return i0, i0, 0
    per_seg = seg_len // Q_BLOCK
    j = n % per_seg
    seg0 = (n // per_seg) * seg_len
    if static:
        ks_local = min(max(j * Q_BLOCK - RADIUS, 0), seg_len - 2 * Q_BLOCK)
        return i0, seg0 + ks_local, 0 if j == 0 else 2 if j == per_seg - 1 else 1
    ks_local = jnp.clip(j * Q_BLOCK - RADIUS, 0, seg_len - 2 * Q_BLOCK)
    ks = pl.multiple_of(seg0 + ks_local, RADIUS)
    var = jnp.where(j == 0, 0, jnp.where(j == per_seg - 1, 2, 1))
    return i0, ks, var


def _first_head(rows):
    return lax.broadcasted_iota(jnp.int32, (rows, LANES), 1) < HEAD_DIM


def _same_head():
    head = np.arange(LANES) // HEAD_DIM
    return jnp.asarray((head[:, None] == head[None, :]).astype(np.float32)).astype(BF16)


def _head_sum(x, same_ref):
    return jnp.dot(x.astype(BF16), same_ref[...], preferred_element_type=F32)


def _head_mean(x, same_ref):
    return _head_sum(x, same_ref) * (1.0 / HEAD_DIM)


def _per_head(x, first):
    swapped = pltpu.roll(x, HEAD_DIM, 1)
    return jnp.where(first, x, swapped), jnp.where(first, swapped, x)


STRIDE = 4


def _gather_segments(src, dil, seq, tmp, put, staged=False):
    if dil == 1:
        put(0, seq, src[pl.ds(0, seq), :])
    elif dil == STRIDE:
        seg = seq // dil
        for r in range(dil):
            put(r * seg, seg, src[pl.ds(r, seg, stride=dil), :])
    else:
        part, seg = seq // STRIDE, seq // dil
        if staged:
            tmp = src
        else:
            for b in range(STRIDE):
                tmp[pl.ds(b * part, part), :] = src[pl.ds(b, part, stride=STRIDE), :]
        for b in range(STRIDE):
            for a in range(dil // STRIDE):
                put(b * part + a * seg, seg, tmp[pl.ds(b * part + a, seg, stride=dil // STRIDE), :])


def _scatter_segments(dst, get, dil, seq, tmp, accumulate):
    def write(rows, val):
        if accumulate:
            dst[rows, :] += val
        else:
            dst[rows, :] = val

    if dil == 1:
        write(pl.ds(0, seq), get(0, seq))
    elif dil == STRIDE:
        seg = seq // dil
        for r in range(dil):
            write(pl.ds(r, seg, stride=dil), get(r * seg, seg))
    else:
        part, seg = seq // STRIDE, seq // dil
        for b in range(STRIDE):
            for a in range(dil // STRIDE):
                tmp[pl.ds(b * part + a, seg, stride=dil // STRIDE), :] = get(b * part + a * seg, seg)
        for b in range(STRIDE):
            write(pl.ds(b, part, stride=STRIDE), tmp[pl.ds(b * part, part), :])


def _permute_rows(dst, src, dil, seq, tmp, staged=False):
    def put(start, size, val):
        dst[pl.ds(start, size), :] = val.astype(dst.dtype)

    _gather_segments(src, dil, seq, tmp, put, staged)


def _permute_rows_by_head(dst, src, dil, seq, tmp):
    def put(start, size, val):
        first = _first_head(size)
        dst[0, pl.ds(start, size), :] = jnp.where(first, val, 0.0).astype(dst.dtype)
        dst[1, pl.ds(start, size), :] = jnp.where(first, 0.0, val).astype(dst.dtype)

    _gather_segments(src, dil, seq, tmp, put)


def _qk_normalise(q_ref, g2_ref, same_ref, dst, seq, scale, step):
    def chunk(ci, carry):
        rows = pl.ds(pl.multiple_of(ci * step, step), step)
        qv = q_ref[rows, :]
        r = lax.rsqrt(_head_mean(qv * qv, same_ref) + EPS)
        dst[rows, :] = qv * r * (g2_ref[...] * scale)
        return carry

    lax.fori_loop(0, seq // step, chunk, 0)


def _attn_fwd(proj3, g_q2, g_k2, ride):
    _, n_seq, seq, _ = proj3.shape
    dms = [_distance_mats(d, seq // d) for d in DILATIONS]
    same = _same_head()
    col0 = 2 * D_CONV // LANES
    n_hp = D_ATT // LANES

    n_ride = len(ride)
    assert not any(scatter for _, scatter in ride), "the forward's ride is an all-gather"

    def body(*refs):
        q_ref, k_ref, v_ref, gq_ref, gk_ref, sl_ref, dm1, dm4, dm16, same_ref = refs[:10]
        ride_in = refs[10:10 + n_ride]
        y_ref, lse_ref = refs[10 + n_ride:12 + n_ride]
        ride_out = refs[12 + n_ride:12 + 2 * n_ride]
        (qf, kf, qp, kp, vp, oml_p, oml_1, o4, o16, m4, m16, l4, l16,
         tmp) = refs[12 + 2 * n_ride:26 + 2 * n_ride]
        natural = {1: (o4, m4, l4), 2: (o16, m16, l16)}
        ride_args = (ride_in, ride_out) + tuple(refs[26 + 2 * n_ride:])
        step = pl.program_id(0) * n_hp + pl.program_id(1)
        n_steps = n_seq * n_hp

        if n_ride:
            for phase, at in enumerate((0, (3 * n_steps) // 4)):
                @pl.when(step == at)
                def _(phase=phase):
                    _gather_by_chip_phase(phase, *ride_args)

        dm_refs = (dm1, dm4, dm16)
        _qk_normalise(q_ref, gq_ref, same_ref, qf, seq, HEAD_DIM ** -0.5, ATT_ROWS)
        _qk_normalise(k_ref, gk_ref, same_ref, kf, seq, 1.0, ATT_ROWS)
        slopes = (sl_ref[0:1, 0:1], sl_ref[1:2, 0:1])
        for pi, dil in enumerate(DILATIONS):
            seg = seq // dil
            kw = min(2 * Q_BLOCK, seg)
            _permute_rows_by_head(qp, qf, dil, seq, tmp)
            _permute_rows(kp, kf, dil, seq, tmp)
            _permute_rows(vp, v_ref, dil, seq, tmp)

            def blk(it, carry, seg=seg, kw=kw, pi=pi, dst=oml_1 if pi == 0 else oml_p):
                first = _first_head(Q_BLOCK)
                chains = [(sub, h) for sub in range(ATT_FWD_UNROLL) for h in range(2)]
                win = [_window(it * ATT_FWD_UNROLL + sub, seg) for sub in range(ATT_FWD_UNROLL)]
                s = {}
                for sub, h in chains:
                    i0, ks, var = win[sub]
                    s[sub, h] = lax.dot_general(qp[h, pl.ds(i0, Q_BLOCK), :], kp[pl.ds(ks, kw), :], NT,
                                                preferred_element_type=F32) - slopes[h] * dm_refs[pi][var]
                m, l, p = {}, {}, {}
                for c in chains:
                    m[c] = jnp.max(s[c], axis=1, keepdims=True)
                    e = jnp.exp(s[c] - m[c])
                    l[c] = jnp.sum(e, axis=1, keepdims=True)
                    p[c] = e.astype(BF16)
                o = {}
                for sub, h in chains:
                    o[sub, h] = jnp.dot(p[sub, h], vp[pl.ds(win[sub][1], kw), :], preferred_element_type=F32)
                packed = [jnp.concatenate([jnp.where(first, t[sub, 0], t[sub, 1]) for t in (o, m, l)], axis=1)
                          for sub in range(ATT_FWD_UNROLL)]
                span = ATT_FWD_UNROLL * Q_BLOCK
                start = it * span if isinstance(it, int) else pl.multiple_of(it * span, span)
                dst[pl.ds(start, span), :] = jnp.concatenate(packed, axis=0)
                return carry

            blk(0, 0)
            lax.fori_loop(1, seq // (Q_BLOCK * ATT_FWD_UNROLL), blk, 0)
            for n, nat in enumerate(natural.get(pi, ())):
                _scatter_segments(nat, lambda start, size, n=n: oml_p[pl.ds(start, size), pl.ds(n * LANES, LANES)],
                                  dil, seq, tmp, accumulate=False)

        def merge(ci, carry):
            rows = pl.ds(pl.multiple_of(ci * ATT_ROWS, ATT_ROWS), ATT_ROWS)

            def part(pi, n):
                return oml_1[rows, pl.ds(n * LANES, LANES)] if pi == 0 else natural[pi][n][rows, :]

            ms = [part(pi, 1) for pi in range(3)]
            m_all = jnp.maximum(jnp.maximum(ms[0], ms[1]), ms[2])
            es = [jnp.exp(m - m_all) for m in ms]
            l_all = sum(part(pi, 2) * es[pi] for pi in range(3))
            inv = 1.0 / l_all
            o = sum(part(pi, 0) * (es[pi] * inv) for pi in range(3))
            y_ref[rows, :] = o.astype(BF16)
            lse_ref[rows, :] = m_all + jnp.log(l_all)
            return carry

        lax.fori_loop(0, seq // ATT_ROWS, merge, 0)

        if n_ride:
            @pl.when(step == n_steps - 1)
            def _():
                _gather_by_chip_phase(2, *ride_args)

    def col(off):
        return pl.BlockSpec((None, None, seq, LANES), lambda b, hp: (col0 + off * n_hp + hp, b, 0, 0))

    def whole(arr):
        return pl.BlockSpec(arr.shape, lambda b, hp: (0,) * arr.ndim)

    rows_f32 = pltpu.VMEM((seq, LANES), F32)
    rows_bf16 = pltpu.VMEM((seq, LANES), BF16)
    return _call(
        body, name="attn_fwd", grid=(n_seq, n_hp),
        in_specs=[col(0), col(1), col(2), whole(g_q2), whole(g_k2),
                  pl.BlockSpec((None, 8, LANES), lambda b, hp: (hp, 0, 0)),
                  whole(dms[0]), whole(dms[1]), whole(dms[2]), whole(same)] + [ANY_SPEC] * n_ride,
        out_specs=[pl.BlockSpec((None, seq, LANES), lambda b, hp: (b, 0, hp)),
                   pl.BlockSpec((None, seq, LANES), lambda b, hp: (b, 0, hp))] + [ANY_SPEC] * n_ride,
        out_shape=[_sds((n_seq, seq, D_ATT), BF16), _sds((n_seq, seq, D_ATT), F32)] + _exchange_shapes(ride),
        scratch_shapes=[rows_f32, rows_f32, pltpu.VMEM((2, seq, LANES), BF16), rows_bf16, rows_bf16]
        + [pltpu.VMEM((seq, 3 * LANES), F32)] * 2 + [rows_f32] * 7 + _exchange_sems(n_ride),
        compiler_params=_params(("arbitrary", "arbitrary")),
    )(proj3, proj3, proj3, g_q2, g_k2, _alibi_rows(), *dms, same, *[a for a, _ in ride])


def _attn_bwd(proj3, do3, y_att3, lse3, g_q2, g_k2, ride):
    _, n_seq, seq, _ = proj3.shape
    dms = [_distance_mats(d, seq // d) for d in DILATIONS]
    same = _same_head()
    col0 = 2 * D_CONV // LANES
    n_hp = D_ATT // LANES

    n_ride = len(ride)
    ride_scatter = [s for _, s in ride]

    def body(*refs):
        (q_ref, k_ref, v_ref, do_ref, o_ref, lse_ref, gq_ref, gk_ref, sl_ref, dm1, dm4, dm16,
         same_ref) = refs[:13]
        ride_in = refs[13:13 + n_ride]
        dq_ref, dk_ref, dv_ref, dg_ref = refs[13 + n_ride:17 + n_ride]
        ride_out = refs[17 + n_ride:17 + 2 * n_ride]
        (qf, kf, qp, dop, kp, vp, sn, sp, dqp, dkp, dvp, dqn, dkn, dvn,
         tmp) = refs[17 + 2 * n_ride:32 + 2 * n_ride]
        ride_args = (ride_scatter, ride_in, ride_out) + tuple(refs[32 + 2 * n_ride:])
        dm_refs = (dm1, dm4, dm16)
        step = pl.program_id(0) * n_hp + pl.program_id(1)

        @pl.when(step == 0)
        def _():
            _exchange_start(*ride_args)
            dg_ref[...] = jnp.zeros_like(dg_ref)

        _qk_normalise(q_ref, gq_ref, same_ref, qf, seq, HEAD_DIM ** -0.5, ATT_BWD_ROWS)
        _qk_normalise(k_ref, gk_ref, same_ref, kf, seq, 1.0, ATT_BWD_ROWS)

        def stats(ci, carry):
            rows = pl.ds(pl.multiple_of(ci * ATT_BWD_ROWS, ATT_BWD_ROWS), ATT_BWD_ROWS)
            first = _first_head(ATT_BWD_ROWS)
            sn[0, rows, :], sn[1, rows, :] = _per_head(lse_ref[rows, :], first)
            prod = do_ref[rows, :] * o_ref[rows, :].astype(F32)
            sn[2, rows, :], sn[3, rows, :] = _per_head(_head_sum(prod, same_ref), first)
            return carry

        lax.fori_loop(0, seq // ATT_BWD_ROWS, stats, 0)
        slopes = (sl_ref[0:1, 0:1], sl_ref[1:2, 0:1])
        half = seq // (Q_BLOCK * ATT_UNROLL)
        region = seq // ATT_UNROLL

        for pi, dil in enumerate(DILATIONS):
            seg = seq // dil
            kw = min(2 * Q_BLOCK, seg)
            _permute_rows_by_head(qp, qf, dil, seq, tmp)
            _permute_rows_by_head(dop, do_ref, dil, seq, tmp)
            _permute_rows(kp, kf, dil, seq, tmp)
            _permute_rows(vp, v_ref, dil, seq, tmp)
            if dil == 1:
                st = sn
            else:
                staged = dil > STRIDE
                assert not staged or DILATIONS[pi - 1] == STRIDE
                st, earlier = (sn, sp) if staged else (sp, sn)
                for n in range(4):
                    _permute_rows(st.at[n], earlier.at[n], dil, seq, tmp, staged)
            def touched(sub, seg=seg):
                lo, hi = sub * region, (sub + 1) * region
                if seg < region:
                    return lo, hi
                seg0 = lo // seg * seg
                return max(lo - RADIUS, seg0), min(hi + RADIUS, seg0 + seg)

            def summed(acc, start, size, touched=touched):
                pieces = []
                for c0 in range(start, start + size, RADIUS):
                    owners = [s for s in range(ATT_UNROLL) if touched(s)[0] <= c0 and c0 + RADIUS <= touched(s)[1]]
                    if pieces and pieces[-1][2] == owners:
                        pieces[-1][1] += RADIUS
                    else:
                        pieces.append([c0, RADIUS, owners])
                vals = [sum(acc[o, pl.ds(c0, n), :] for o in owners) for c0, n, owners in pieces]
                return vals[0] if len(vals) == 1 else jnp.concatenate(vals, axis=0)

            for sub in range(ATT_UNROLL):
                lo, hi = touched(sub)
                w0 = _window(sub * half, seg)[1]
                assert lo <= w0 and w0 + kw <= hi
                for z0, z1 in ((lo, w0), (w0 + kw, hi)):
                    if z1 > z0:
                        dkp[sub, pl.ds(z0, z1 - z0), :] = jnp.zeros((z1 - z0, LANES), F32)
                        dvp[sub, pl.ds(z0, z1 - z0), :] = jnp.zeros((z1 - z0, LANES), F32)

            def blk(it, carry, seg=seg, kw=kw, pi=pi, st=st):
                first = _first_head(Q_BLOCK)
                chains = [(sub, h) for sub in range(ATT_UNROLL) for h in range(2)]
                win = [_window(it + sub * half, seg) for sub in range(ATT_UNROLL)]
                qrows = [pl.ds(w[0], Q_BLOCK) for w in win]
                krows = [pl.ds(w[1], kw) for w in win]

                def over_keys(n, sub):
                    t = st[n, qrows[sub], :]
                    return t if kw == LANES else jnp.concatenate([t] * (kw // LANES), axis=1)

                s, dp = {}, {}
                for sub, h in chains:
                    s[sub, h] = lax.dot_general(qp[h, qrows[sub], :], kp[krows[sub], :], NT,
                                                preferred_element_type=F32) - slopes[h] * dm_refs[pi][win[sub][2]]
                    dp[sub, h] = lax.dot_general(dop[h, qrows[sub], :], vp[krows[sub], :], NT,
                                                 preferred_element_type=F32)
                p, ds = {}, {}
                for sub, h in chains:
                    e = jnp.exp(s[sub, h] - over_keys(h, sub))
                    ds[sub, h] = (e * (dp[sub, h] - over_keys(2 + h, sub))).astype(BF16)
                    p[sub, h] = e.astype(BF16)
                dq, dk, dv = {}, {}, {}
                for sub, h in chains:
                    dq[sub, h] = jnp.dot(ds[sub, h], kp[krows[sub], :], preferred_element_type=F32)
                    dk[sub, h] = lax.dot_general(ds[sub, h], qp[h, qrows[sub], :], TN, preferred_element_type=F32)
                    dv[sub, h] = lax.dot_general(p[sub, h], dop[h, qrows[sub], :], TN, preferred_element_type=F32)
                dq_dst = dqn if pi == 0 else dqp
                for sub in range(ATT_UNROLL):
                    dq_dst[qrows[sub], :] = jnp.where(first, dq[sub, 0], dq[sub, 1])
                    if isinstance(it, int):
                        dkp[sub, krows[sub], :] = dk[sub, 0] + dk[sub, 1]
                        dvp[sub, krows[sub], :] = dv[sub, 0] + dv[sub, 1]
                    else:
                        dkp[sub, krows[sub], :] += dk[sub, 0] + dk[sub, 1]
                        dvp[sub, krows[sub], :] += dv[sub, 0] + dv[sub, 1]
                return carry

            blk(0, 0)
            lax.fori_loop(1, half, blk, 0)
            first_pattern = pi == 0
            if first_pattern:
                for r0 in range(0, seq, region):
                    rows = pl.ds(r0, region)
                    dkn[rows, :] = summed(dkp, r0, region)
                    dvn[rows, :] = summed(dvp, r0, region)
            else:
                _scatter_segments(dqn, lambda start, size: dqp[pl.ds(start, size), :], dil, seq, tmp, accumulate=True)
                for nat, acc in ((dkn, dkp), (dvn, dvp)):
                    _scatter_segments(nat, lambda start, size, acc=acc: summed(acc, start, size),
                                      dil, seq, tmp, accumulate=True)

        def finish(ci, carry):
            rows = pl.ds(pl.multiple_of(ci * ATT_BWD_ROWS, ATT_BWD_ROWS), ATT_BWD_ROWS)
            for src_ref, g_ref, dn, dst_ref, scale, row in (
                    (q_ref, gq_ref, dqn, dq_ref, HEAD_DIM ** -0.5, 0), (k_ref, gk_ref, dkn, dk_ref, 1.0, 1)):
                xv = src_ref[rows, :]
                r = lax.rsqrt(_head_mean(xv * xv, same_ref) + EPS)
                xhat = xv * r
                d = dn[rows, :] * scale
                dg_ref[row:row + 1, :] += jnp.sum(d * xhat, axis=0, keepdims=True)
                dxh = d * g_ref[...]
                dst_ref[rows, :] = (r * (dxh - xhat * _head_mean(dxh * xhat, same_ref))).astype(BF16)
            dv_ref[rows, :] = dvn[rows, :].astype(BF16)
            return carry

        lax.fori_loop(0, seq // ATT_BWD_ROWS, finish, 0)

        @pl.when(step == n_seq * n_hp - 1)
        def _():
            _exchange_wait(*ride_args)

    def col(off):
        return pl.BlockSpec((None, None, seq, LANES), lambda b, hp: (col0 + off * n_hp + hp, b, 0, 0))

    def whole(arr):
        return pl.BlockSpec(arr.shape, lambda b, hp: (0,) * arr.ndim)

    att = pl.BlockSpec((None, seq, LANES), lambda b, hp: (b, 0, hp))
    rows_f32 = pltpu.VMEM((seq, LANES), F32)
    rows_bf16 = pltpu.VMEM((seq, LANES), BF16)
    by_head_bf16 = pltpu.VMEM((2, seq, LANES), BF16)
    per_sub_f32 = pltpu.VMEM((ATT_UNROLL, seq, LANES), F32)
    stats_f32 = pltpu.VMEM((4, seq, LANES), F32)
    return _call(
        body, name="attn_bwd", grid=(n_seq, n_hp),
        in_specs=[col(0), col(1), col(2), att, att, att, whole(g_q2), whole(g_k2),
                  pl.BlockSpec((None, 8, LANES), lambda b, hp: (hp, 0, 0)),
                  whole(dms[0]), whole(dms[1]), whole(dms[2]), whole(same)] + [ANY_SPEC] * n_ride,
        out_specs=[att, att, att, pl.BlockSpec((8, LANES), lambda b, hp: (0, 0))] + [ANY_SPEC] * n_ride,
        out_shape=[_sds((n_seq, seq, D_ATT), BF16)] * 3 + [_sds((8, LANES), F32)] + _exchange_shapes(ride),
        scratch_shapes=[rows_f32, rows_f32, by_head_bf16, by_head_bf16, rows_bf16, rows_bf16, stats_f32, stats_f32,
                        rows_f32, per_sub_f32, per_sub_f32, rows_f32, rows_f32, rows_f32, rows_f32]
        + _exchange_sems(n_ride),
        compiler_params=_params(("arbitrary", "arbitrary")),
    )(proj3, proj3, proj3, do3, y_att3, lse3, g_q2, g_k2, _alibi_rows(), *dms, same, *[a for a, _ in ride])


def _mix_out(uc2, y_att2, x2, mod8, g_ln, b_ln, g_ffn, w_out, seq, tm=512):
    tokens = x2.shape[0]
    per_seq = seq // tm

    def body(uc_ref, ya_ref, x_ref, m_ref, gl_ref, bl_ref, gf_ref, w_ref, yc_ref, mix_ref, x1_ref, h2_ref):
        uc = uc_ref[...]
        mu = jnp.mean(uc, axis=-1, keepdims=True)
        cen = uc - mu
        rs = lax.rsqrt(jnp.mean(cen * cen, axis=-1, keepdims=True) + EPS)
        z = cen * rs * gl_ref[...] + bl_ref[...]
        yc = (z * _sig(z)).astype(BF16)
        yc_ref[...] = yc
        mix = (jnp.dot(yc, w_ref[pl.ds(0, D_CONV), :], preferred_element_type=F32)
               + jnp.dot(ya_ref[...], w_ref[pl.ds(D_CONV, D_ATT), :], preferred_element_type=F32))
        mix_ref[...] = mix.astype(BF16)
        x1 = x_ref[...] + m_ref[2:3, :] * mix
        x1_ref[...] = x1
        r = lax.rsqrt(jnp.mean(x1 * x1, axis=-1, keepdims=True) + EPS)
        h2_ref[...] = ((x1 * r * gf_ref[...]) * (1.0 + m_ref[4:5, :]) + m_ref[3:4, :]).astype(BF16)

    def rows(width):
        return pl.BlockSpec((tm, width), lambda i: (i, 0))

    def vec(width):
        return pl.BlockSpec((1, width), lambda i: (0, 0))

    return _call(
        body, name="mix_out", grid=(tokens // tm,),
        in_specs=[rows(D_CONV), rows(D_ATT), rows(D_MODEL),
                  pl.BlockSpec((None, 8, D_MODEL), lambda i: (i // per_seq, 0, 0)),
                  vec(D_CONV), vec(D_CONV), vec(D_MODEL),
                  pl.BlockSpec((D_MODEL, D_MODEL), lambda i: (0, 0))],
        out_specs=[rows(D_CONV), rows(D_MODEL), rows(D_MODEL), rows(D_MODEL)],
        out_shape=[_sds((tokens, D_CONV), BF16), _sds((tokens, D_MODEL), BF16),
                   _sds((tokens, D_MODEL), F32), _sds((tokens, D_MODEL), BF16)],
        compiler_params=_params(("parallel",)),
    )(uc2, y_att2, x2, mod8, g_ln, b_ln, g_ffn, w_out)


def _mix_out_bwd(dmix, uc2, g_ln, b_ln, w_out, tm=512):
    tokens = dmix.shape[0]

    def body(dm_ref, uc_ref, gl_ref, bl_ref, w_ref, duc_ref, do_ref, dgb_ref):
        @pl.when(pl.program_id(0) == 0)
        def _():
            dgb_ref[...] = jnp.zeros_like(dgb_ref)

        dmv = dm_ref[...]
        dyc = lax.dot_general(dmv, w_ref[pl.ds(0, D_CONV), :], NT, preferred_element_type=F32)
        do_ref[...] = lax.dot_general(dmv, w_ref[pl.ds(D_CONV, D_ATT), :], NT, preferred_element_type=F32)
        uc = uc_ref[...]
        mu = jnp.mean(uc, axis=-1, keepdims=True)
        cen = uc - mu
        rs = lax.rsqrt(jnp.mean(cen * cen, axis=-1, keepdims=True) + EPS)
        xh = cen * rs
        z = xh * gl_ref[...] + bl_ref[...]
        sg = _sig(z)
        dz = dyc * (sg * (1.0 + z * (1.0 - sg)))
        dgb_ref[0:1, :] += jnp.sum(dz * xh, axis=0, keepdims=True)
        dgb_ref[1:2, :] += jnp.sum(dz, axis=0, keepdims=True)
        dxh = dz * gl_ref[...]
        duc_ref[...] = rs * (dxh - jnp.mean(dxh, axis=-1, keepdims=True)
                             - xh * jnp.mean(dxh * xh, axis=-1, keepdims=True))

    return _call(
        body, name="mix_out_bwd", grid=(tokens // tm,),
        in_specs=[pl.BlockSpec((tm, D_MODEL), lambda i: (i, 0)),
                  pl.BlockSpec((tm, D_CONV), lambda i: (i, 0)),
                  pl.BlockSpec((1, D_CONV), lambda i: (0, 0)),
                  pl.BlockSpec((1, D_CONV), lambda i: (0, 0)),
                  pl.BlockSpec((D_MODEL, D_MODEL), lambda i: (0, 0))],
        out_specs=[pl.BlockSpec((tm, D_CONV), lambda i: (i, 0)),
                   pl.BlockSpec((tm, D_ATT), lambda i: (i, 0)),
                   pl.BlockSpec((8, D_CONV), lambda i: (0, 0))],
        out_shape=[_sds((tokens, D_CONV), F32), _sds((tokens, D_ATT), F32), _sds((8, D_CONV), F32)],
        compiler_params=_params(("arbitrary",)),
    )(dmix, uc2, g_ln, b_ln, w_out)


FF_TILE = 256
FF_TILES = D_FF // FF_TILE


def _load_once(hbm_refs, vmem_refs, sems):
    @pl.when(pl.program_id(0) == 0)
    def _():
        copies = [pltpu.make_async_copy(src, dst, sems.at[n]) for n, (src, dst) in enumerate(zip(hbm_refs, vmem_refs))]
        for cp in copies:
            cp.start()
        for cp in copies:
            cp.wait()


def _ffn_fwd(h2, w_gate_t, w_up_t, w_down, x1, target, mod8, seq, tm=512):
    tokens = h2.shape[0]
    per_seq = seq // tm
    n_seq = tokens // seq

    def body(h_ref, m_ref, x1_ref, t_ref, wg_hbm, wu_hbm, wd_hbm, gate_ref, up_ref, dy_ref, df_ref, sq_ref, dgf_ref,
             wg, wu, wd, w_sems):
        i = pl.program_id(0)
        _load_once((wg_hbm, wu_hbm, wd_hbm), (wg, wu, wd), w_sems)

        @pl.when(i == 0)
        def _():
            sq_ref[...] = jnp.zeros_like(sq_ref)

        @pl.when(i % per_seq == 0)
        def _():
            dgf_ref[...] = jnp.zeros_like(dgf_ref)

        hv = h_ref[...]

        def gate_up(t):
            rows = pl.ds(t * FF_TILE, FF_TILE)
            return (lax.dot_general(hv, wg[rows, :], NT, preferred_element_type=F32),
                    lax.dot_general(hv, wu[rows, :], NT, preferred_element_type=F32))

        fv = jnp.zeros((tm, D_MODEL), F32)
        ahead = gate_up(0)
        for t in range(FF_TILES):
            gate, up = ahead
            if t + 1 < FF_TILES:
                ahead = gate_up(t + 1)
            gate_ref[t] = gate.astype(BF16)
            up_ref[t] = up.astype(BF16)
            act = (gate * _sig(gate) * up).astype(BF16)
            fv = fv + jnp.dot(act, wd[pl.ds(t * FF_TILE, FF_TILE), :], preferred_element_type=F32)
        gate_f = m_ref[5:6, :]
        diff = x1_ref[...] + gate_f * fv - t_ref[...]
        sq_ref[0:1, :] += jnp.sum(diff * diff, axis=0, keepdims=True)
        dy = diff * (1.0 / D_MODEL)
        dy_ref[...] = dy
        df_ref[...] = (gate_f * dy).astype(BF16)
        dgf_ref[0:1, :] += jnp.sum(dy * fv, axis=0, keepdims=True)

    rows_spec = pl.BlockSpec((tm, D_MODEL), lambda i: (i, 0))
    per = pl.BlockSpec((None, 8, D_MODEL), lambda i: (i // per_seq, 0, 0))
    tiles = pl.BlockSpec((FF_TILES, tm, FF_TILE), lambda i: (0, i, 0))
    weight = pltpu.VMEM((D_FF, D_MODEL), BF16)
    return _call(
        body, name="ffn_fwd", grid=(tokens // tm,),
        in_specs=[rows_spec, per, rows_spec, rows_spec, ANY_SPEC, ANY_SPEC, ANY_SPEC],
        out_specs=[tiles, tiles, rows_spec, rows_spec, pl.BlockSpec((8, D_MODEL), lambda i: (0, 0)), per],
        out_shape=[_sds((FF_TILES, tokens, FF_TILE), BF16), _sds((FF_TILES, tokens, FF_TILE), BF16),
                   _sds((tokens, D_MODEL), F32), _sds((tokens, D_MODEL), BF16),
                   _sds((8, D_MODEL), F32), _sds((n_seq, 8, D_MODEL), F32)],
        scratch_shapes=[weight, weight, weight, pltpu.SemaphoreType.DMA((3,))],
        compiler_params=_params(("arbitrary",)),
    )(h2, mod8, x1, target, w_gate_t, w_up_t, w_down)


def _ffn_bwd(df, gate, up, w_gate_t, w_up_t, w_down, x1, dy, mix, mod8, g_ffn, seq, tm=256):
    tokens = df.shape[0]
    per_seq = seq // tm
    n_seq = tokens // seq

    def body(df_ref, gate_ref, up_ref, m_ref, g_ref, x1_ref, dy_ref, mix_ref, wg_hbm, wu_hbm, wd_hbm,
             dgate_ref, dup_ref, act_ref, dx1_ref, dmix_ref, dg_ref, dm_ref, wg, wu, wd, w_sems):
        i = pl.program_id(0)
        _load_once((wg_hbm, wu_hbm, wd_hbm), (wg, wu, wd), w_sems)

        @pl.when(i == 0)
        def _():
            dg_ref[...] = jnp.zeros_like(dg_ref)

        @pl.when(i % per_seq == 0)
        def _():
            dm_ref[...] = jnp.zeros_like(dm_ref)

        dfv = df_ref[...]

        def d_act(t):
            return lax.dot_general(dfv, wd[pl.ds(t * FF_TILE, FF_TILE), :], NT, preferred_element_type=F32)

        dh = jnp.zeros((tm, D_MODEL), F32)
        ahead = d_act(0)
        for t in range(FF_TILES):
            dact = ahead
            if t + 1 < FF_TILES:
                ahead = d_act(t + 1)
            rows = pl.ds(t * FF_TILE, FF_TILE)
            gv = gate_ref[t].astype(F32)
            uv = up_ref[t].astype(F32)
            sg = _sig(gv)
            silu = gv * sg
            act_ref[t] = (silu * uv).astype(BF16)
            dup = (dact * silu).astype(BF16)
            dgate = (dact * uv * (sg * (1.0 + gv * (1.0 - sg)))).astype(BF16)
            dup_ref[t] = dup
            dgate_ref[t] = dgate
            dh = dh + (jnp.dot(dgate, wg[rows, :], preferred_element_type=F32)
                       + jnp.dot(dup, wu[rows, :], preferred_element_type=F32))
        g = g_ref[...]
        x1v = x1_ref[...]
        rs = lax.rsqrt(jnp.mean(x1v * x1v, axis=-1, keepdims=True) + EPS)
        xhat = x1v * rs
        dm_ref[0:1, :] += jnp.sum(dh, axis=0, keepdims=True)
        dm_ref[1:2, :] += jnp.sum(dh * (xhat * g), axis=0, keepdims=True)
        dn = dh * (1.0 + m_ref[4:5, :])
        dg_ref[0:1, :] += jnp.sum(dn * xhat, axis=0, keepdims=True)
        dxh = dn * g
        dx1 = dy_ref[...] + rs * (dxh - xhat * jnp.mean(dxh * xhat, axis=-1, keepdims=True))
        dx1_ref[...] = dx1
        dm_ref[2:3, :] += jnp.sum(dx1 * mix_ref[...].astype(F32), axis=0, keepdims=True)
        dmix_ref[...] = (m_ref[2:3, :] * dx1).astype(BF16)

    rows_spec = pl.BlockSpec((tm, D_MODEL), lambda i: (i, 0))
    per = pl.BlockSpec((None, 8, D_MODEL), lambda i: (i // per_seq, 0, 0))
    tiles = pl.BlockSpec((FF_TILES, tm, FF_TILE), lambda i: (0, i, 0))
    weight = pltpu.VMEM((D_FF, D_MODEL), BF16)
    return _call(
        body, name="ffn_bwd", grid=(tokens // tm,),
        in_specs=[rows_spec, tiles, tiles, per, pl.BlockSpec((1, D_MODEL), lambda i: (0, 0)),
                  rows_spec, rows_spec, rows_spec, ANY_SPEC, ANY_SPEC, ANY_SPEC],
        out_specs=[tiles, tiles, tiles, rows_spec, rows_spec, pl.BlockSpec((8, D_MODEL), lambda i: (0, 0)), per],
        out_shape=[_sds((FF_TILES, tokens, FF_TILE), BF16)] * 3
        + [_sds((tokens, D_MODEL), F32), _sds((tokens, D_MODEL), BF16),
           _sds((8, D_MODEL), F32), _sds((n_seq, 8, D_MODEL), F32)],
        scratch_shapes=[weight, weight, weight, pltpu.SemaphoreType.DMA((3,))],
        compiler_params=_params(("arbitrary",)),
    )(df, gate, up, mod8, g_ffn, x1, dy, mix, w_gate_t, w_up_t, w_down)


def _mix_in_bwd(d_a, d_g, d_q, d_k, d_v, w_in, x2, dx1, mod8, g_mix, seq, ride, tm=512):
    tokens = x2.shape[0]
    per_seq = seq // tm
    n_seq = tokens // seq
    parts = (d_a, d_g, d_q, d_k, d_v)
    width = D_CONV
    n_ride = len(ride)
    ride_scatter = [s for _, s in ride]

    def body(*refs):
        da_ref, dg_ref, dq_ref, dk_ref, dv_ref, w_ref, x_ref, dx1_ref, m_ref, g_ref = refs[:10]
        ride_in = refs[10:10 + n_ride]
        gx_ref, dgm_ref, dm_ref = refs[10 + n_ride:13 + n_ride]
        ride_args = (ride_scatter, ride_in, refs[13 + n_ride:13 + 2 * n_ride]) + tuple(refs[13 + 2 * n_ride:])
        i = pl.program_id(0)

        @pl.when(i == 0)
        def _():
            _exchange_start(*ride_args)
            dgm_ref[...] = jnp.zeros_like(dgm_ref)

        @pl.when(i % per_seq == 0)
        def _():
            dm_ref[...] = jnp.zeros_like(dm_ref)

        dh = jnp.zeros((tm, D_MODEL), F32)
        for n, ref in enumerate((da_ref, dg_ref, dq_ref, dk_ref, dv_ref)):
            dh = dh + jnp.dot(ref[...], w_ref[pl.ds(n * width, width), :], preferred_element_type=F32)
        xv = x_ref[...]
        r = lax.rsqrt(jnp.mean(xv * xv, axis=-1, keepdims=True) + EPS)
        xhat = xv * r
        g = g_ref[...]
        dm_ref[0:1, :] += jnp.sum(dh, axis=0, keepdims=True)
        dm_ref[1:2, :] += jnp.sum(dh * (xhat * g), axis=0, keepdims=True)
        dn = dh * (1.0 + m_ref[1:2, :])
        dgm_ref[0:1, :] += jnp.sum(dn * xhat, axis=0, keepdims=True)
        dxh = dn * g
        gx_ref[...] = dx1_ref[...] + r * (dxh - xhat * jnp.mean(dxh * xhat, axis=-1, keepdims=True))

        @pl.when(i == tokens // tm - 1)
        def _():
            _exchange_wait(*ride_args)

    rows = pl.BlockSpec((tm, D_MODEL), lambda i: (i, 0))
    half = pl.BlockSpec((tm, width), lambda i: (i, 0))
    per = pl.BlockSpec((None, 8, D_MODEL), lambda i: (i // per_seq, 0, 0))
    return _call(
        body, name="mix_in_bwd", grid=(tokens // tm,),
        in_specs=[half] * 5 + [pl.BlockSpec((D_IN, D_MODEL), lambda i: (0, 0)), rows, rows, per,
                               pl.BlockSpec((1, D_MODEL), lambda i: (0, 0))] + [ANY_SPEC] * n_ride,
        out_specs=[rows, pl.BlockSpec((8, D_MODEL), lambda i: (0, 0)), per] + [ANY_SPEC] * n_ride,
        out_shape=[_sds((tokens, D_MODEL), F32), _sds((8, D_MODEL), F32), _sds((n_seq, 8, D_MODEL), F32)]
        + _exchange_shapes(ride),
        scratch_shapes=_exchange_sems(n_ride),
        compiler_params=_params(("arbitrary",)),
    )(*parts, w_in, x2, dx1, mod8, g_mix, *[a for a, _ in ride])


def _grad_matmul_parts(a_parts, b_parts, name, tk=1024):
    tokens = a_parts[0].shape[0]
    na, nb = len(a_parts), len(b_parts)
    ma, nbw = a_parts[0].shape[1], b_parts[0].shape[1]

    n_k = tokens // tk

    def body(*refs):
        a_refs, b_refs, o_ref, acc = refs[:na], refs[na:na + nb], refs[na + nb], refs[na + nb + 1]

        @pl.when(pl.program_id(0) == 0)
        def _():
            acc[...] = jnp.zeros_like(acc)

        for i in range(na):
            for j in range(nb):
                acc[pl.ds(i * ma, ma), pl.ds(j * nbw, nbw)] += lax.dot_general(
                    a_refs[i][...], b_refs[j][...], TN, preferred_element_type=F32)

        @pl.when(pl.program_id(0) == n_k - 1)
        def _():
            o_ref[...] = acc[...].astype(o_ref.dtype)

    return _call(
        body, name=name, grid=(n_k,),
        in_specs=[pl.BlockSpec((tk, ma), lambda k: (k, 0))] * na + [pl.BlockSpec((tk, nbw), lambda k: (k, 0))] * nb,
        out_specs=pl.BlockSpec((na * ma, nb * nbw), lambda k: (0, 0)),
        out_shape=_sds((na * ma, nb * nbw), BF16),
        scratch_shapes=[pltpu.VMEM((na * ma, nb * nbw), F32)],
        compiler_params=_params(("arbitrary",)),
    )(*a_parts, *b_parts)


def _grad_matmul_tiles(a, b, name, tk=1024):
    tiled_b = b.ndim == 3
    tiles, tokens, width = b.shape if tiled_b else a.shape
    other = a.shape[1] if tiled_b else b.shape[1]
    out_tile = (other, width) if tiled_b else (width, other)
    n_k = tokens // tk

    def body(a_ref, b_ref, o_ref, acc):
        @pl.when(pl.program_id(0) == 0)
        def _():
            acc[...] = jnp.zeros_like(acc)

        for t in range(tiles):
            lhs = a_ref[...] if tiled_b else a_ref[t]
            rhs = b_ref[t] if tiled_b else b_ref[...]
            acc[t] += lax.dot_general(lhs, rhs, TN, preferred_element_type=F32)

        @pl.when(pl.program_id(0) == n_k - 1)
        def _():
            o_ref[...] = acc[...].astype(o_ref.dtype)

    flat = pl.BlockSpec((tk, other), lambda k: (k, 0))
    tiled = pl.BlockSpec((tiles, tk, width), lambda k: (0, k, 0))
    return _call(
        body, name=name, grid=(n_k,),
        in_specs=[flat, tiled] if tiled_b else [tiled, flat],
        out_specs=pl.BlockSpec((tiles,) + out_tile, lambda k: (0, 0, 0)),
        out_shape=_sds((tiles,) + out_tile, BF16),
        scratch_shapes=[pltpu.VMEM((tiles,) + out_tile, F32)],
        compiler_params=_params(("arbitrary",)),
    )(a, b)


def _adamw(w, m, v, g, name, n_parts=0, tr=256):
    rows, cols = w.shape
    tr = min(tr, rows)
    c1 = 1.0 - ADAM_B1 ** ADAM_STEP
    c2 = 1.0 - ADAM_B2 ** ADAM_STEP

    def body(w_ref, m_ref, v_ref, g_ref, go_ref, d_ref, mo_ref, vo_ref):
        if n_parts:
            gv = g_ref[0].astype(F32)
            for p in range(1, n_parts):
                gv = gv + g_ref[p].astype(F32)
        else:
            gv = g_ref[...]
        go_ref[...] = gv
        mn = ADAM_B1 * m_ref[...] + (1.0 - ADAM_B1) * gv
        vn = ADAM_B2 * v_ref[...] + (1.0 - ADAM_B2) * (gv * gv)
        mo_ref[...] = mn
        vo_ref[...] = vn
        d_ref[...] = -ADAM_LR * ((mn / c1) / (jnp.sqrt(vn / c2) + ADAM_EPS) + ADAM_WD * w_ref[...])

    blk = pl.BlockSpec((tr, cols), lambda i: (i, 0))
    g_spec = pl.BlockSpec((n_parts, tr, cols), lambda i: (0, i, 0)) if n_parts else blk
    return _call(
        body, name=name, grid=(rows // tr,),
        in_specs=[blk, blk, blk, g_spec], out_specs=[blk] * 4,
        out_shape=[_sds((rows, cols), F32)] * 4,
        compiler_params=_params(("parallel",)),
    )(w, m, v, g)


def _cols_to_full(blocks):
    n, r, c = blocks.shape
    return jnp.transpose(blocks, (1, 0, 2)).reshape(r, n * c)


def _pad_lanes(v, width):
    return jnp.pad(v, ((0, 0), (0, width - v.shape[1])))


def kernel(x, c, w_ada, b_ada, g_mix, w_in, w_dw, b_dw, g_conv_ln, b_conv_ln, g_q, g_k, w_out, g_ffn, w_gate, w_up, w_down, loss_target, m_w_ada, m_b_ada, m_g_mix, m_w_in, m_w_dw, m_b_dw, m_g_conv_ln, m_b_conv_ln, m_g_q, m_g_k, m_w_out, m_g_ffn, m_w_gate, m_w_up, m_w_down, v_w_ada, v_b_ada, v_g_mix, v_w_in, v_w_dw, v_b_dw, v_g_conv_ln, v_b_conv_ln, v_g_q, v_g_k, v_w_out, v_g_ffn, v_w_gate, v_w_up, v_w_down):
    n_seq, seq, _ = x.shape
    tokens = n_seq * seq
    me = 4 * lax.axis_index("x") + 2 * lax.axis_index("y") + lax.axis_index("c")
    ada_cols = w_ada.shape[2]
    dw_cols = w_dw.shape[2]

    def transposed(w):
        return jnp.transpose(w[0])

    (c_g, w_in_g, w_dw_g) = _gather_by_chip([c, transposed(w_in).astype(BF16), w_dw[0]], "gather_weights")
    c_all = c_g.reshape(N_DEV * n_seq, D_MODEL)
    w_in_t = w_in_g.reshape(D_IN, D_MODEL)
    w_dw_f = _cols_to_full(w_dw_g)

    b_cols = lax.dynamic_slice(b_ada, (0, me * ada_cols), (1, ada_cols))
    mod_cols = _ada_fwd(c_all, w_ada[0], b_cols)
    (mod_g,) = _exchange([(mod_cols, False)], "gather_mod")
    mod_mine = lax.dynamic_slice(mod_g, (0, me * n_seq, 0), (N_DEV, n_seq, ada_cols))
    mod = jnp.transpose(mod_mine, (1, 0, 2)).reshape(n_seq, N_MOD, D_MODEL)
    mod8 = jnp.pad(mod, ((0, 0), (0, 8 - N_MOD), (0, 0)))

    x2 = x.reshape(tokens, D_MODEL)
    h1, proj = _mix_in(x2, mod8, g_mix, w_in_t, seq)
    proj3 = proj.reshape(D_IN // LANES, n_seq, seq, LANES)
    uc3 = _conv_fwd(proj3, w_dw_f, b_dw)
    g_q2, g_k2 = jnp.tile(g_q, (1, 2)), jnp.tile(g_k, (1, 2))
    y_att3, lse3, w_out_g, w_gate_g, w_up_g, w_down_g = _attn_fwd(
        proj3, g_q2, g_k2,
        [(w_out[0].astype(BF16), False), (transposed(w_gate).astype(BF16), False),
         (transposed(w_up).astype(BF16), False), (w_down[0].astype(BF16), False)])
    w_out_f = w_out_g.reshape(D_MODEL, D_MODEL)
    w_gate_f = w_gate_g.reshape(D_FF, D_MODEL)
    w_up_f = w_up_g.reshape(D_FF, D_MODEL)
    w_down_f = w_down_g.reshape(D_FF, D_MODEL)
    uc2 = uc3.reshape(tokens, D_CONV)
    y_att2 = y_att3.reshape(tokens, D_ATT)
    y_conv, mix, x1, h2 = _mix_out(uc2, y_att2, x2, mod8, g_conv_ln, b_conv_ln, g_ffn, w_out_f, seq)
    gate, up, dy, df, sq, dgate_f = _ffn_fwd(
        h2, w_gate_f, w_up_f, w_down_f, x1, loss_target.reshape(tokens, D_MODEL), mod8, seq)

    dgate, dup, act, dx1, dmix, dg_ffn, dmod_f = _ffn_bwd(
        df, gate, up, w_gate_f, w_up_f, w_down_f, x1, dy, mix, mod8, g_ffn, seq)
    duc2, do2, dgb_ln = _mix_out_bwd(dmix, uc2, g_conv_ln, b_conv_ln, w_out_f)
    d_a3, d_g3, dw_dw_p, db_dw_p = _conv_bwd(duc2.reshape(n_seq, seq, D_CONV), proj3, w_dw_f)
    gw_gate = _grad_matmul_tiles(dgate, h2, "grad_w_gate")
    gw_up = _grad_matmul_tiles(dup, h2, "grad_w_up")
    gw_down = _grad_matmul_tiles(act, df, "grad_w_down")
    gw_out = _grad_matmul_parts([y_conv, y_att2], [dmix], "grad_w_out")
    d_q3, d_k3, d_v3, dg_qk, p_gate, p_up, p_down, p_out = _attn_bwd(
        proj3, do2.reshape(n_seq, seq, D_ATT), y_att3, lse3, g_q2, g_k2,
        [(gw_gate.reshape(N_DEV, D_FF // N_DEV, D_MODEL), True), (gw_up.reshape(N_DEV, D_FF // N_DEV, D_MODEL), True),
         (gw_down.reshape(N_DEV, D_FF // N_DEV, D_MODEL), True),
         (gw_out.reshape(N_DEV, D_MODEL // N_DEV, D_MODEL), True)])
    flat = lambda t: t.reshape(tokens, t.shape[-1])
    d_a, d_g, d_q, d_k, d_v = flat(d_a3), flat(d_g3), flat(d_q3), flat(d_k3), flat(d_v3)
    gw_in = _grad_matmul_parts([d_a, d_g, d_q, d_k, d_v], [h1], "grad_w_in")
    grad_x2, dg_mix, dmod_m, p_in = _mix_in_bwd(
        d_a, d_g, d_q, d_k, d_v, w_in_t, x2, dx1, mod8, g_mix, seq,
        [(gw_in.reshape(N_DEV, D_IN // N_DEV, D_MODEL), True)])

    dmod = jnp.concatenate([dmod_m[:, 0], dmod_m[:, 1], dmod_f[:, 2], dmod_f[:, 0], dmod_f[:, 1], dgate_f[:, 0]], axis=1)
    dg_q = dg_qk[0:1, 0:HEAD_DIM] + dg_qk[0:1, HEAD_DIM:]
    dg_k = dg_qk[1:2, 0:HEAD_DIM] + dg_qk[1:2, HEAD_DIM:]
    loss_part = (0.5 / D_MODEL) * jnp.sum(sq[0:1, :], axis=1, keepdims=True)
    small = jnp.concatenate(
        [dg_mix[0:1], dg_ffn[0:1], db_dw_p[0:1], dgb_ln[0:1], dgb_ln[1:2],
         _pad_lanes(dg_q, LANES), _pad_lanes(dg_k, LANES), _pad_lanes(loss_part, LANES)], axis=1)
    n_small = small.shape[1] - LANES

    (dmod_g, small_g, dw_g) = _exchange([(dmod, False), (small, False), (dw_dw_p, False)], "gather_small_grads")

    dmod_all = dmod_g.reshape(N_DEV * n_seq, N_MOD * D_MODEL)
    dmod_cols = lax.dynamic_slice(dmod_all, (0, me * ada_cols), (N_DEV * n_seq, ada_cols))
    gw_ada, gb_ada = _ada_bwd(c_all, dmod_cols, dmod_all)

    res = {}
    res["w_ada"] = _adamw(w_ada[0], m_w_ada[0], v_w_ada[0], gw_ada, "adamw_w_ada")
    res["b_ada"] = _adamw(b_ada, m_b_ada, v_b_ada, gb_ada, "adamw_b_ada")
    def adamw_transposed(w, m, v, parts, name, tr):
        outs = _adamw(transposed(w), transposed(m), transposed(v), parts, name, N_DEV, tr=tr)
        return tuple(jnp.transpose(o) for o in outs)

    res["w_in"] = adamw_transposed(w_in, m_w_in, v_w_in, p_in, "adamw_w_in", 160)
    res["w_out"] = _adamw(w_out[0], m_w_out[0], v_w_out[0], p_out, "adamw_w_out", N_DEV)
    res["w_gate"] = adamw_transposed(w_gate, m_w_gate, v_w_gate, p_gate, "adamw_w_gate", 176)
    res["w_up"] = adamw_transposed(w_up, m_w_up, v_w_up, p_up, "adamw_w_up", 176)
    res["w_down"] = _adamw(w_down[0], m_w_down[0], v_w_down[0], p_down, "adamw_w_down", N_DEV, tr=176)
    dw_mine = lax.dynamic_slice(dw_g, (0, 0, me * dw_cols), (N_DEV, CONV_WIDTH, dw_cols))
    res["w_dw"] = _adamw(w_dw[0], m_w_dw[0], v_w_dw[0], dw_mine, "adamw_w_dw", N_DEV)

    small_names = ["g_mix", "g_ffn", "b_dw", "g_conv_ln", "b_conv_ln", "g_q", "g_k"]
    small_w = {"g_mix": (g_mix, m_g_mix, v_g_mix), "g_ffn": (g_ffn, m_g_ffn, v_g_ffn), "b_dw": (b_dw, m_b_dw, v_b_dw),
               "g_conv_ln": (g_conv_ln, m_g_conv_ln, v_g_conv_ln), "b_conv_ln": (b_conv_ln, m_b_conv_ln, v_b_conv_ln),
               "g_q": (g_q, m_g_q, v_g_q), "g_k": (g_k, m_g_k, v_g_k)}
    widths = [max(small_w[n][0].shape[1], LANES) for n in small_names]
    packed = [jnp.concatenate([_pad_lanes(small_w[n][i], wd) for n, wd in zip(small_names, widths)], axis=1) for i in range(3)]
    outs = _adamw(packed[0], packed[1], packed[2], small_g[:, :, :n_small], "adamw_small", N_DEV)
    off = 0
    for n, wd in zip(small_names, widths):
        real = small_w[n][0].shape[1]
        res[n] = tuple(o[:, off:off + real] for o in outs)
        off += wd
    loss = jnp.sum(small_g[:, 0, n_small])

    order = ["w_ada", "b_ada", "g_mix", "w_in", "w_dw", "b_dw", "g_conv_ln", "b_conv_ln", "g_q", "g_k",
             "w_out", "g_ffn", "w_gate", "w_up", "w_down"]
    lead = {"w_ada", "w_in", "w_dw", "w_out", "w_gate", "w_up", "w_down"}
    grads, deltas, new_m, new_v = [], [], [], []
    for n in order:
        g, d, mn, vn = res[n]
        g, d, mn, vn = (t[None] if n in lead else t for t in (g, d, mn, vn))
        grads.append(g)
        deltas.append(d)
        new_m.append(mn)
        new_v.append(vn)
    return (loss, grad_x2.reshape(n_seq, seq, D_MODEL), *grads, *deltas, *new_m, *new_v)
```

```python
import numpy as np
import jax
import jax.numpy as jnp
from jax import lax
from jax.experimental import pallas as pl
from jax.experimental.pallas import tpu as pltpu

F32 = jnp.float32
BF16 = jnp.bfloat16

N_DEV = 8
D_MODEL = 1024
D_CONV = 512
D_ATT = 512
HEAD_DIM = 64
CONV_WIDTH = 31
D_IN = 2 * D_CONV + 3 * D_ATT
D_FF = 2816
N_MOD = 6
EPS = 1e-6
RADIUS = 64
DILATIONS = (1, 4, 16)
Q_BLOCK = 128
LANES = 128
VMEM_LIMIT = 56 * 1024 * 1024

ADAM_LR = 0.001
ADAM_B1 = 0.9
ADAM_B2 = 0.999
ADAM_EPS = 1e-08
ADAM_WD = 0.01
ADAM_STEP = 10

NT = (((1,), (1,)), ((), ()))
TN = (((0,), (0,)), ((), ()))


def _call(body, **kw):
    return pl.pallas_call(body, **kw)


def _params(sem=None, vmem=VMEM_LIMIT):
    return pltpu.CompilerParams(dimension_semantics=sem, vmem_limit_bytes=vmem)


def _sig(x):
    return 1.0 / (1.0 + jnp.exp(-x))


def _sds(shape, dtype):
    return jax.ShapeDtypeStruct(shape, dtype)


N_PEER = N_DEV - 1
ANY_SPEC = pl.BlockSpec(memory_space=pl.ANY)


def _exchange_copies(scatter, ins, outs, *sems):
    n = len(ins)
    if n == 0:
        return [], []
    send_sems, recv_sems, local_sems = sems
    x, y, c = lax.axis_index("x"), lax.axis_index("y"), lax.axis_index("c")
    me = 4 * x + 2 * y + c

    def src(a, slot):
        return ins[a].at[slot] if scatter[a] else ins[a]

    local = [pltpu.make_async_copy(src(a, me), outs[a].at[me], local_sems.at[a]) for a in range(n)]
    flights = []
    for k in range(1, N_DEV):
        px = 1 - x if k & 4 else x
        py = 1 - y if k & 2 else y
        pc = 1 - c if k & 1 else c
        pid = 4 * px + 2 * py + pc
        for a in range(n):
            i = a * N_PEER + k - 1
            send, recv = (pltpu.make_async_remote_copy(
                src_ref=src(a, pid), dst_ref=outs[a].at[slot],
                send_sem=send_sems.at[i], recv_sem=recv_sems.at[i],
                device_id=(px, py, pc), device_id_type=pl.DeviceIdType.MESH) for slot in (me, pid))
            flights.append((send, recv))
    return local, flights


def _exchange_start(*args):
    local, flights = _exchange_copies(*args)
    for cp in local:
        cp.start()
    for send, _ in flights:
        send.start()


def _exchange_wait(*args):
    local, flights = _exchange_copies(*args)
    for send, recv in flights:
        send.wait_send()
        recv.wait_recv()
    for cp in local:
        cp.wait()


def _exchange_shapes(items):
    return [_sds((N_DEV,) + tuple(arr.shape[1:] if scatter else arr.shape), arr.dtype) for arr, scatter in items]


def _exchange_sems(n):
    if n == 0:
        return []
    return [pltpu.SemaphoreType.DMA((n * N_PEER,)), pltpu.SemaphoreType.DMA((n * N_PEER,)),
            pltpu.SemaphoreType.DMA((n,))]


def _gather_by_chip_phase(phase, ins, outs, send_sems, recv_sems, local_sems):
    n = len(ins)
    per = N_PEER
    x, y, c = lax.axis_index("x"), lax.axis_index("y"), lax.axis_index("c")
    me, sibling = (x, y, c), (x, y, 1 - c)
    chips = [(1 - x, y), (x, 1 - y), (1 - x, 1 - y)]

    def slot(px, py, pc):
        return 4 * px + 2 * py + pc

    def copy(a, k, block, to, src=None):
        dst = outs[a].at[slot(*block)]
        return pltpu.make_async_remote_copy(
            src_ref=dst if src is None else src, dst_ref=dst,
            send_sem=send_sems.at[a * per + k], recv_sem=recv_sems.at[a * per + k],
            device_id=to, device_id_type=pl.DeviceIdType.MESH)

    local = [pltpu.make_async_copy(ins[a], outs[a].at[slot(*me)], local_sems.at[a]) for a in range(n)]
    first = []
    for a in range(n):
        first.append(copy(a, 0, me, sibling, src=ins[a]))
        first += [copy(a, 1 + j, me, (*chip, c), src=ins[a]) for j, chip in enumerate(chips)]
    passed = [copy(a, 4 + j, (*chip, c), sibling) for j, chip in enumerate(chips) for a in range(n)]
    if phase == 0:
        for cp in local + first:
            cp.start()
    elif phase == 1:
        for j, chip in enumerate(chips):
            for a in range(n):
                copy(a, 1 + j, (*chip, c), me).wait_recv()
        for cp in passed:
            cp.start()
    else:
        for a in range(n):
            copy(a, 0, sibling, me).wait_recv()
            for j, chip in enumerate(chips):
                copy(a, 4 + j, (*chip, 1 - c), me).wait_recv()
        for cp in first + passed:
            cp.wait_send()
        for cp in local:
            cp.wait()


def _gather_by_chip(arrays, name):
    n = len(arrays)

    def body(*refs):
        for phase in range(3):
            _gather_by_chip_phase(phase, refs[:n], refs[n:2 * n], *refs[2 * n:])

    return _call(
        body, name=name, out_shape=_exchange_shapes([(arr, False) for arr in arrays]),
        in_specs=[ANY_SPEC] * n, out_specs=[ANY_SPEC] * n, scratch_shapes=_exchange_sems(n),
    )(*arrays)


def _exchange(items, name):
    n = len(items)
    scatter = [s for _, s in items]

    def body(*refs):
        args = (scatter, refs[:n], refs[n:2 * n]) + tuple(refs[2 * n:])
        _exchange_start(*args)
        _exchange_wait(*args)

    return _call(
        body, name=name, out_shape=_exchange_shapes(items),
        in_specs=[ANY_SPEC] * n, out_specs=[ANY_SPEC] * n, scratch_shapes=_exchange_sems(n),
    )(*[a for a, _ in items])


def _ada_fwd(c_all, w_ada, b_cols):
    def body(c_ref, w_ref, b_ref, o_ref):
        cv = c_ref[...]
        sc = (cv * _sig(cv)).astype(BF16)
        o_ref[...] = jnp.dot(sc, w_ref[...].astype(BF16), preferred_element_type=F32) + b_ref[...]

    return _call(body, name="ada_fwd", out_shape=_sds((c_all.shape[0], w_ada.shape[1]), F32),
                 compiler_params=_params())(c_all, w_ada, b_cols)


def _ada_bwd(c_all, dmod_cols, dmod_all):
    def body(c_ref, dc_ref, da_ref, gw_ref, gb_ref):
        cv = c_ref[...]
        sc = (cv * _sig(cv)).astype(BF16)
        gw_ref[...] = lax.dot_general(sc, dc_ref[...].astype(BF16), TN, preferred_element_type=F32)
        gb_ref[...] = jnp.sum(da_ref[...], axis=0, keepdims=True)

    return _call(body, name="ada_bwd",
                 out_shape=[_sds((c_all.shape[1], dmod_cols.shape[1]), F32), _sds((1, dmod_all.shape[1]), F32)],
                 compiler_params=_params())(c_all, dmod_cols, dmod_all)


MIX_ROWS = 128


def _mix_in(x2, mod8, g_mix, w_in, seq, tm=512):
    tokens = x2.shape[0]
    per_seq = seq // tm

    def body(x_ref, m_ref, g_ref, wt_ref, h_ref, p_ref, w_ref):
        @pl.when(pl.program_id(0) == 0)
        def _():
            w_ref[...] = wt_ref[...].T

        def normed(c):
            rows = pl.ds(c * MIX_ROWS, MIX_ROWS)
            xv = x_ref[rows, :]
            r = lax.rsqrt(jnp.mean(xv * xv, axis=-1, keepdims=True) + EPS)
            hb = ((xv * r * g_ref[...]) * (1.0 + m_ref[1:2, :]) + m_ref[0:1, :]).astype(BF16)
            h_ref[rows, :] = hb
            return hb

        ahead = normed(0)
        for c in range(tm // MIX_ROWS):
            hb = ahead
            if c + 1 < tm // MIX_ROWS:
                ahead = normed(c + 1)
            p = jnp.dot(hb, w_ref[...], preferred_element_type=F32)
            for cb in range(D_IN // LANES):
                p_ref[cb, pl.ds(c * MIX_ROWS, MIX_ROWS), :] = p[:, cb * LANES:(cb + 1) * LANES]

    return _call(
        body, name="mix_in", grid=(tokens // tm,),
        in_specs=[pl.BlockSpec((tm, D_MODEL), lambda i: (i, 0)),
                  pl.BlockSpec((None, 8, D_MODEL), lambda i: (i // per_seq, 0, 0)),
                  pl.BlockSpec((1, D_MODEL), lambda i: (0, 0)),
                  pl.BlockSpec((D_IN, D_MODEL), lambda i: (0, 0))],
        out_specs=[pl.BlockSpec((tm, D_MODEL), lambda i: (i, 0)),
                   pl.BlockSpec((D_IN // LANES, tm, LANES), lambda i: (0, i, 0))],
        out_shape=[_sds((tokens, D_MODEL), BF16), _sds((D_IN // LANES, tokens, LANES), F32)],
        scratch_shapes=[pltpu.VMEM((D_MODEL, D_IN), BF16)],
        compiler_params=_params(("arbitrary",)),
    )(x2, mod8, g_mix, w_in)


CONV_ROWS = 128
CONV_DW_ROWS = 32
CONV_DW_UNROLL = 8
CONV_HALO = 16


def _fill_shifted(xp, sh, seq):
    for b in range(8):
        sh[b, pl.ds(0, seq + 24), :] = xp[pl.ds(b, seq + 24), :]


def _conv_fwd(proj3, w_dw, b_dw):
    _, n_seq, seq, _ = proj3.shape
    n_cb = D_CONV // LANES

    def body(a_ref, g_ref, w_ref, b_ref, uc_ref, xp, sh):
        zeros = jnp.zeros((CONV_HALO, LANES), F32)
        xp[pl.ds(0, CONV_HALO), :] = zeros
        xp[pl.ds(CONV_HALO + seq, CONV_HALO), :] = zeros
        xp[pl.ds(CONV_HALO, seq), :] = a_ref[...] * _sig(g_ref[...])
        _fill_shifted(xp, sh, seq)

        def blk(i, carry):
            t0 = pl.multiple_of(i * CONV_ROWS, CONV_ROWS)
            acc = jnp.zeros((CONV_ROWS, LANES), F32)
            for j in range(CONV_WIDTH):
                jj = j + 1
                acc = acc + sh[jj % 8, pl.ds(t0 + 8 * (jj // 8), CONV_ROWS), :] * w_ref[j:j + 1, :]
            uc_ref[pl.ds(t0, CONV_ROWS), :] = acc + b_ref[...]
            return carry

        lax.fori_loop(0, seq // CONV_ROWS, blk, 0)

    return _call(
        body, name="conv_fwd", grid=(n_seq, n_cb),
        in_specs=[pl.BlockSpec((None, None, seq, LANES), lambda b, cb: (cb, b, 0, 0)),
                  pl.BlockSpec((None, None, seq, LANES), lambda b, cb: (n_cb + cb, b, 0, 0)),
                  pl.BlockSpec((CONV_WIDTH, LANES), lambda b, cb: (0, cb)),
                  pl.BlockSpec((1, LANES), lambda b, cb: (0, cb))],
        out_specs=pl.BlockSpec((None, seq, LANES), lambda b, cb: (b, 0, cb)),
        out_shape=_sds((n_seq, seq, D_CONV), F32),
        scratch_shapes=[pltpu.VMEM((seq + 2 * CONV_HALO, LANES), F32),
                        pltpu.VMEM((8, seq + 2 * CONV_HALO, LANES), F32)],
        compiler_params=_params(("parallel", "parallel")),
    )(proj3, proj3, w_dw, b_dw)


def _conv_bwd(duc3, proj3, w_dw):
    _, n_seq, seq, _ = proj3.shape
    n_cb = D_CONV // LANES

    def body(duc_ref, a_ref, g_ref, w_ref, da_ref, dg_ref, dw_ref, db_ref, xp, sh):
        @pl.when(pl.program_id(1) == 0)
        def _():
            dw_ref[...] = jnp.zeros_like(dw_ref)
            db_ref[...] = jnp.zeros_like(db_ref)

        zeros = jnp.zeros((CONV_HALO, LANES), F32)
        xp[pl.ds(0, CONV_HALO), :] = zeros
        xp[pl.ds(CONV_HALO + seq, CONV_HALO), :] = zeros
        xp[pl.ds(CONV_HALO, seq), :] = a_ref[...] * _sig(g_ref[...])
        _fill_shifted(xp, sh, seq)
        for j0 in range(0, CONV_WIDTH, 8):
            taps = range(j0, min(j0 + 8, CONV_WIDTH))

            def wblk(i, accs, taps=taps):
                for u in range(CONV_DW_UNROLL):
                    t0 = pl.multiple_of((i * CONV_DW_UNROLL + u) * CONV_DW_ROWS, CONV_DW_ROWS)
                    d = duc_ref[pl.ds(t0, CONV_DW_ROWS), :]
                    accs = tuple(acc + d * sh[(j + 1) % 8, pl.ds(t0 + 8 * ((j + 1) // 8), CONV_DW_ROWS), :]
                                 for acc, j in zip(accs, taps))
                return accs

            accs = lax.fori_loop(0, seq // (CONV_DW_ROWS * CONV_DW_UNROLL), wblk,
                                 tuple(jnp.zeros((CONV_DW_ROWS, LANES), F32) for _ in taps))
            for acc, j in zip(accs, taps):
                dw_ref[j:j + 1, :] += jnp.sum(acc, axis=0, keepdims=True)
        db_ref[0:1, :] += jnp.sum(duc_ref[...], axis=0, keepdims=True)
        xp[pl.ds(CONV_HALO, seq), :] = duc_ref[...]
        _fill_shifted(xp, sh, seq)

        def ublk(i, carry):
            t0 = pl.multiple_of(i * CONV_ROWS, CONV_ROWS)
            acc = jnp.zeros((CONV_ROWS, LANES), F32)
            for j in range(CONV_WIDTH):
                jj = CONV_WIDTH - j
                acc = acc + sh[jj % 8, pl.ds(t0 + 8 * (jj // 8), CONV_ROWS), :] * w_ref[j:j + 1, :]
            av = a_ref[pl.ds(t0, CONV_ROWS), :]
            sg = _sig(g_ref[pl.ds(t0, CONV_ROWS), :])
            da_ref[pl.ds(t0, CONV_ROWS), :] = (acc * sg).astype(BF16)
            dg_ref[pl.ds(t0, CONV_ROWS), :] = (acc * av * sg * (1.0 - sg)).astype(BF16)
            return carry

        lax.fori_loop(0, seq // CONV_ROWS, ublk, 0)

    return _call(
        body, name="conv_bwd", grid=(n_cb, n_seq),
        in_specs=[pl.BlockSpec((None, seq, LANES), lambda cb, b: (b, 0, cb)),
                  pl.BlockSpec((None, None, seq, LANES), lambda cb, b: (cb, b, 0, 0)),
                  pl.BlockSpec((None, None, seq, LANES), lambda cb, b: (n_cb + cb, b, 0, 0)),
                  pl.BlockSpec((CONV_WIDTH, LANES), lambda cb, b: (0, cb))],
        out_specs=[pl.BlockSpec((None, seq, LANES), lambda cb, b: (b, 0, cb)),
                   pl.BlockSpec((None, seq, LANES), lambda cb, b: (b, 0, cb)),
                   pl.BlockSpec((32, LANES), lambda cb, b: (0, cb)),
                   pl.BlockSpec((8, LANES), lambda cb, b: (0, cb))],
        out_shape=[_sds((n_seq, seq, D_CONV), BF16), _sds((n_seq, seq, D_CONV), BF16),
                   _sds((32, D_CONV), F32), _sds((8, D_CONV), F32)],
        scratch_shapes=[pltpu.VMEM((seq + 2 * CONV_HALO, LANES), F32),
                        pltpu.VMEM((8, seq + 2 * CONV_HALO, LANES), F32)],
        compiler_params=_params(("parallel", "arbitrary")),
    )(duc3, proj3, proj3, w_dw)


MASKED = 1e30
ATT_ROWS = 2048
ATT_BWD_ROWS = 2048
ATT_UNROLL = 8
ATT_FWD_UNROLL = 8


def _distance_mats(dil, seg_len):
    kw = min(2 * Q_BLOCK, seg_len)
    offsets = (0, -RADIUS, -2 * RADIUS) if kw == 2 * Q_BLOCK else (0,)
    a = np.arange(Q_BLOCK)[:, None]
    b = np.arange(kw)[None, :]
    mats = []
    for off in offsets:
        rel = np.abs(b + off - a)
        mats.append(np.where(rel <= RADIUS, dil * rel, MASKED))
    return jnp.asarray(np.stack(mats).astype(np.float32))


def _alibi_rows():
    s = np.zeros((4, 8, LANES), np.float32)
    for hp in range(4):
        for hl in range(2):
            s[hp, hl, :] = 2.0 ** (-(2 * hp + hl + 1))
    return jnp.asarray(s)


def _window(n, seg_len):
    static = isinstance(n, int)
    i0 = n * Q_BLOCK if static else pl.multiple_of(n * Q_BLOCK, Q_BLOCK)
    if seg_len <= Q_BLOCK:
        return i0, i0, 0
    per_seg = seg_len // Q_BLOCK
    j = n % per_seg
    seg0 = (n // per_seg) * seg_len
    if static:
        ks_local = min(max(j * Q_BLOCK - RADIUS, 0), seg_len - 2 * Q_BLOCK)
        return i0, seg0 + ks_local, 0 if j == 0 else 2 if j == per_seg - 1 else 1
    ks_local = jnp.clip(j * Q_BLOCK - RADIUS, 0, seg_len - 2 * Q_BLOCK)
    ks = pl.multiple_of(seg0 + ks_local, RADIUS)
    var = jnp.where(j == 0, 0, jnp.where(j == per_seg - 1, 2, 1))
    return i0, ks, var


def _first_head(rows):
    return lax.broadcasted_iota(jnp.int32, (rows, LANES), 1) < HEAD_DIM


def _same_head():
    head = np.arange(LANES) // HEAD_DIM
    return jnp.asarray((head[:, None] == head[None, :]).astype(np.float32)).astype(BF16)


def _head_sum(x, same_ref):
    return jnp.dot(x.astype(BF16), same_ref[...], preferred_element_type=F32)


def _head_mean(x, same_ref):
    return _head_sum(x, same_ref) * (1.0 / HEAD_DIM)


def _per_head(x, first):
    swapped = pltpu.roll(x, HEAD_DIM, 1)
    return jnp.where(first, x, swapped), jnp.where(first, swapped, x)


STRIDE = 4


def _gather_segments(src, dil, seq, tmp, put, staged=False):
    if dil == 1:
        put(0, seq, src[pl.ds(0, seq), :])
    elif dil == STRIDE:
        seg = seq // dil
        for r in range(dil):
            put(r * seg, seg, src[pl.ds(r, seg, stride=dil), :])
    else:
        part, seg = seq // STRIDE, seq // dil
        if staged:
            tmp = src
        else:
            for b in range(STRIDE):
                tmp[pl.ds(b * part, part), :] = src[pl.ds(b, part, stride=STRIDE), :]
        for b in range(STRIDE):
            for a in range(dil // STRIDE):
                put(b * part + a * seg, seg, tmp[pl.ds(b * part + a, seg, stride=dil // STRIDE), :])


def _scatter_segments(dst, get, dil, seq, tmp, accumulate):
    def write(rows, val):
        if accumulate:
            dst[rows, :] += val
        else:
            dst[rows, :] = val

    if dil == 1:
        write(pl.ds(0, seq), get(0, seq))
    elif dil == STRIDE:
        seg = seq // dil
        for r in range(dil):
            write(pl.ds(r, seg, stride=dil), get(r * seg, seg))
    else:
        part, seg = seq // STRIDE, seq // dil
        for b in range(STRIDE):
            for a in range(dil // STRIDE):
                tmp[pl.ds(b * part + a, seg, stride=dil // STRIDE), :] = get(b * part + a * seg, seg)
        for b in range(STRIDE):
            write(pl.ds(b, part, stride=STRIDE), tmp[pl.ds(b * part, part), :])


def _permute_rows(dst, src, dil, seq, tmp, staged=False):
    def put(start, size, val):
        dst[pl.ds(start, size), :] = val.astype(dst.dtype)

    _gather_segments(src, dil, seq, tmp, put, staged)


def _permute_rows_by_head(dst, src, dil, seq, tmp):
    def put(start, size, val):
        first = _first_head(size)
        dst[0, pl.ds(start, size), :] = jnp.where(first, val, 0.0).astype(dst.dtype)
        dst[1, pl.ds(start, size), :] = jnp.where(first, 0.0, val).astype(dst.dtype)

    _gather_segments(src, dil, seq, tmp, put)


def _qk_normalise(q_ref, g2_ref, same_ref, dst, seq, scale, step):
    def chunk(ci, carry):
        rows = pl.ds(pl.multiple_of(ci * step, step), step)
        qv = q_ref[rows, :]
        r = lax.rsqrt(_head_mean(qv * qv, same_ref) + EPS)
        dst[rows, :] = qv * r * (g2_ref[...] * scale)
        return carry

    lax.fori_loop(0, seq // step, chunk, 0)


def _attn_fwd(proj3, g_q2, g_k2, ride):
    _, n_seq, seq, _ = proj3.shape
    dms = [_distance_mats(d, seq // d) for d in DILATIONS]
    same = _same_head()
    col0 = 2 * D_CONV // LANES
    n_hp = D_ATT // LANES

    n_ride = len(ride)
    assert not any(scatter for _, scatter in ride), "the forward's ride is an all-gather"

    def body(*refs):
        q_ref, k_ref, v_ref, gq_ref, gk_ref, sl_ref, dm1, dm4, dm16, same_ref = refs[:10]
        ride_in = refs[10:10 + n_ride]
        y_ref, lse_ref = refs[10 + n_ride:12 + n_ride]
        ride_out = refs[12 + n_ride:12 + 2 * n_ride]
        (qf, kf, qp, kp, vp, oml_p, oml_1, o4, o16, m4, m16, l4, l16,
         tmp) = refs[12 + 2 * n_ride:26 + 2 * n_ride]
        natural = {1: (o4, m4, l4), 2: (o16, m16, l16)}
        ride_args = (ride_in, ride_out) + tuple(refs[26 + 2 * n_ride:])
        step = pl.program_id(0) * n_hp + pl.program_id(1)
        n_steps = n_seq * n_hp

        if n_ride:
            for phase, at in enumerate((0, (3 * n_steps) // 4)):
                @pl.when(step == at)
                def _(phase=phase):
                    _gather_by_chip_phase(phase, *ride_args)

        dm_refs = (dm1, dm4, dm16)
        _qk_normalise(q_ref, gq_ref, same_ref, qf, seq, HEAD_DIM ** -0.5, ATT_ROWS)
        _qk_normalise(k_ref, gk_ref, same_ref, kf, seq, 1.0, ATT_ROWS)
        slopes = (sl_ref[0:1, 0:1], sl_ref[1:2, 0:1])
        for pi, dil in enumerate(DILATIONS):
            seg = seq // dil
            kw = min(2 * Q_BLOCK, seg)
            _permute_rows_by_head(qp, qf, dil, seq, tmp)
            _permute_rows(kp, kf, dil, seq, tmp)
            _permute_rows(vp, v_ref, dil, seq, tmp)

            def blk(it, carry, seg=seg, kw=kw, pi=pi, dst=oml_1 if pi == 0 else oml_p):
                first = _first_head(Q_BLOCK)
                chains = [(sub, h) for sub in range(ATT_FWD_UNROLL) for h in range(2)]
                win = [_window(it * ATT_FWD_UNROLL + sub, seg) for sub in range(ATT_FWD_UNROLL)]
                s = {}
                for sub, h in chains:
                    i0, ks, var = win[sub]
                    s[sub, h] = lax.dot_general(qp[h, pl.ds(i0, Q_BLOCK), :], kp[pl.ds(ks, kw), :], NT,
                                                preferred_element_type=F32) - slopes[h] * dm_refs[pi][var]
                m, l, p = {}, {}, {}
                for c in chains:
                    m[c] = jnp.max(s[c], axis=1, keepdims=True)
                    e = jnp.exp(s[c] - m[c])
                    l[c] = jnp.sum(e, axis=1, keepdims=True)
                    p[c] = e.astype(BF16)
                o = {}
                for sub, h in chains:
                    o[sub, h] = jnp.dot(p[sub, h], vp[pl.ds(win[sub][1], kw), :], preferred_element_type=F32)
                packed = [jnp.concatenate([jnp.where(first, t[sub, 0], t[sub, 1]) for t in (o, m, l)], axis=1)
                          for sub in range(ATT_FWD_UNROLL)]
                span = ATT_FWD_UNROLL * Q_BLOCK
                start = it * span if isinstance(it, int) else pl.multiple_of(it * span, span)
                dst[pl.ds(start, span), :] = jnp.concatenate(packed, axis=0)
                return carry

            blk(0, 0)
            lax.fori_loop(1, seq // (Q_BLOCK * ATT_FWD_UNROLL), blk, 0)
            for n, nat in enumerate(natural.get(pi, ())):
                _scatter_segments(nat, lambda start, size, n=n: oml_p[pl.ds(start, size), pl.ds(n * LANES, LANES)],
                                  dil, seq, tmp, accumulate=False)

        def merge(ci, carry):
            rows = pl.ds(pl.multiple_of(ci * ATT_ROWS, ATT_ROWS), ATT_ROWS)

            def part(pi, n):
                return oml_1[rows, pl.ds(n * LANES, LANES)] if pi == 0 else natural[pi][n][rows, :]

            ms = [part(pi, 1) for pi in range(3)]
            m_all = jnp.maximum(jnp.maximum(ms[0], ms[1]), ms[2])
            es = [jnp.exp(m - m_all) for m in ms]
            l_all = sum(part(pi, 2) * es[pi] for pi in range(3))
            inv = 1.0 / l_all
            o = sum(part(pi, 0) * (es[pi] * inv) for pi in range(3))
            y_ref[rows, :] = o.astype(BF16)
            lse_ref[rows, :] = m_all + jnp.log(l_all)
            return carry

        lax.fori_loop(0, seq // ATT_ROWS, merge, 0)

        if n_ride:
            @pl.when(step == n_steps - 1)
            def _():
                _gather_by_chip_phase(2, *ride_args)

    def col(off):
        return pl.BlockSpec((None, None, seq, LANES), lambda b, hp: (col0 + off * n_hp + hp, b, 0, 0))

    def whole(arr):
        return pl.BlockSpec(arr.shape, lambda b, hp: (0,) * arr.ndim)

    rows_f32 = pltpu.VMEM((seq, LANES), F32)
    rows_bf16 = pltpu.VMEM((seq, LANES), BF16)
    return _call(
        body, name="attn_fwd", grid=(n_seq, n_hp),
        in_specs=[col(0), col(1), col(2), whole(g_q2), whole(g_k2),
                  pl.BlockSpec((None, 8, LANES), lambda b, hp: (hp, 0, 0)),
                  whole(dms[0]), whole(dms[1]), whole(dms[2]), whole(same)] + [ANY_SPEC] * n_ride,
        out_specs=[pl.BlockSpec((None, seq, LANES), lambda b, hp: (b, 0, hp)),
                   pl.BlockSpec((None, seq, LANES), lambda b, hp: (b, 0, hp))] + [ANY_SPEC] * n_ride,
        out_shape=[_sds((n_seq, seq, D_ATT), BF16), _sds((n_seq, seq, D_ATT), F32)] + _exchange_shapes(ride),
        scratch_shapes=[rows_f32, rows_f32, pltpu.VMEM((2, seq, LANES), BF16), rows_bf16, rows_bf16]
        + [pltpu.VMEM((seq, 3 * LANES), F32)] * 2 + [rows_f32] * 7 + _exchange_sems(n_ride),
        compiler_params=_params(("arbitrary", "arbitrary")),
    )(proj3, proj3, proj3, g_q2, g_k2, _alibi_rows(), *dms, same, *[a for a, _ in ride])


def _attn_bwd(proj3, do3, y_att3, lse3, g_q2, g_k2, ride):
    _, n_seq, seq, _ = proj3.shape
    dms = [_distance_mats(d, seq // d) for d in DILATIONS]
    same = _same_head()
    col0 = 2 * D_CONV // LANES
    n_hp = D_ATT // LANES

    n_ride = len(ride)
    ride_scatter = [s for _, s in ride]

    def body(*refs):
        (q_ref, k_ref, v_ref, do_ref, o_ref, lse_ref, gq_ref, gk_ref, sl_ref, dm1, dm4, dm16,
         same_ref) = refs[:13]
        ride_in = refs[13:13 + n_ride]
        dq_ref, dk_ref, dv_ref, dg_ref = refs[13 + n_ride:17 + n_ride]
        ride_out = refs[17 + n_ride:17 + 2 * n_ride]
        (qf, kf, qp, dop, kp, vp, sn, sp, dqp, dkp, dvp, dqn, dkn, dvn,
         tmp) = refs[17 + 2 * n_ride:32 + 2 * n_ride]
        ride_args = (ride_scatter, ride_in, ride_out) + tuple(refs[32 + 2 * n_ride:])
        dm_refs = (dm1, dm4, dm16)
        step = pl.program_id(0) * n_hp + pl.program_id(1)

        @pl.when(step == 0)
        def _():
            _exchange_start(*ride_args)
            dg_ref[...] = jnp.zeros_like(dg_ref)

        _qk_normalise(q_ref, gq_ref, same_ref, qf, seq, HEAD_DIM ** -0.5, ATT_BWD_ROWS)
        _qk_normalise(k_ref, gk_ref, same_ref, kf, seq, 1.0, ATT_BWD_ROWS)

        def stats(ci, carry):
            rows = pl.ds(pl.multiple_of(ci * ATT_BWD_ROWS, ATT_BWD_ROWS), ATT_BWD_ROWS)
            first = _first_head(ATT_BWD_ROWS)
            sn[0, rows, :], sn[1, rows, :] = _per_head(lse_ref[rows, :], first)
            prod = do_ref[rows, :] * o_ref[rows, :].astype(F32)
            sn[2, rows, :], sn[3, rows, :] = _per_head(_head_sum(prod, same_ref), first)
            return carry

        lax.fori_loop(0, seq // ATT_BWD_ROWS, stats, 0)
        slopes = (sl_ref[0:1, 0:1], sl_ref[1:2, 0:1])
        half = seq // (Q_BLOCK * ATT_UNROLL)
        region = seq // ATT_UNROLL

        for pi, dil in enumerate(DILATIONS):
            seg = seq // dil
            kw = min(2 * Q_BLOCK, seg)
            _permute_rows_by_head(qp, qf, dil, seq, tmp)
            _permute_rows_by_head(dop, do_ref, dil, seq, tmp)
            _permute_rows(kp, kf, dil, seq, tmp)
            _permute_rows(vp, v_ref, dil, seq, tmp)
            if dil == 1:
                st = sn
            else:
                staged = dil > STRIDE
                assert not staged or DILATIONS[pi - 1] == STRIDE
                st, earlier = (sn, sp) if staged else (sp, sn)
                for n in range(4):
                    _permute_rows(st.at[n], earlier.at[n], dil, seq, tmp, staged)
            def touched(sub, seg=seg):
                lo, hi = sub * region, (sub + 1) * region
                if seg < region:
                    return lo, hi
                seg0 = lo // seg * seg
                return max(lo - RADIUS, seg0), min(hi + RADIUS, seg0 + seg)

            def summed(acc, start, size, touched=touched):
                pieces = []
                for c0 in range(start, start + size, RADIUS):
                    owners = [s for s in range(ATT_UNROLL) if touched(s)[0] <= c0 and c0 + RADIUS <= touched(s)[1]]
                    if pieces and pieces[-1][2] == owners:
                        pieces[-1][1] += RADIUS
                    else:
                        pieces.append([c0, RADIUS, owners])
                vals = [sum(acc[o, pl.ds(c0, n), :] for o in owners) for c0, n, owners in pieces]
                return vals[0] if len(vals) == 1 else jnp.concatenate(vals, axis=0)

            for sub in range(ATT_UNROLL):
                lo, hi = touched(sub)
                w0 = _window(sub * half, seg)[1]
                assert lo <= w0 and w0 + kw <= hi
                for z0, z1 in ((lo, w0), (w0 + kw, hi)):
                    if z1 > z0:
                        dkp[sub, pl.ds(z0, z1 - z0), :] = jnp.zeros((z1 - z0, LANES), F32)
                        dvp[sub, pl.ds(z0, z1 - z0), :] = jnp.zeros((z1 - z0, LANES), F32)

            def blk(it, carry, seg=seg, kw=kw, pi=pi, st=st):
                first = _first_head(Q_BLOCK)
                chains = [(sub, h) for sub in range(ATT_UNROLL) for h in range(2)]
                win = [_window(it + sub * half, seg) for sub in range(ATT_UNROLL)]
                qrows = [pl.ds(w[0], Q_BLOCK) for w in win]
                krows = [pl.ds(w[1], kw) for w in win]

                def over_keys(n, sub):
                    t = st[n, qrows[sub], :]
                    return t if kw == LANES else jnp.concatenate([t] * (kw // LANES), axis=1)

                s, dp = {}, {}
                for sub, h in chains:
                    s[sub, h] = lax.dot_general(qp[h, qrows[sub], :], kp[krows[sub], :], NT,
                                                preferred_element_type=F32) - slopes[h] * dm_refs[pi][win[sub][2]]
                    dp[sub, h] = lax.dot_general(dop[h, qrows[sub], :], vp[krows[sub], :], NT,
                                                 preferred_element_type=F32)
                p, ds = {}, {}
                for sub, h in chains:
                    e = jnp.exp(s[sub, h] - over_keys(h, sub))
                    ds[sub, h] = (e * (dp[sub, h] - over_keys(2 + h, sub))).astype(BF16)
                    p[sub, h] = e.astype(BF16)
                dq, dk, dv = {}, {}, {}
                for sub, h in chains:
                    dq[sub, h] = jnp.dot(ds[sub, h], kp[krows[sub], :], preferred_element_type=F32)
                    dk[sub, h] = lax.dot_general(ds[sub, h], qp[h, qrows[sub], :], TN, preferred_element_type=F32)
                    dv[sub, h] = lax.dot_general(p[sub, h], dop[h, qrows[sub], :], TN, preferred_element_type=F32)
                dq_dst = dqn if pi == 0 else dqp
                for sub in range(ATT_UNROLL):
                    dq_dst[qrows[sub], :] = jnp.where(first, dq[sub, 0], dq[sub, 1])
                    if isinstance(it, int):
                        dkp[sub, krows[sub], :] = dk[sub, 0] + dk[sub, 1]
                        dvp[sub, krows[sub], :] = dv[sub, 0] + dv[sub, 1]
                    else:
                        dkp[sub, krows[sub], :] += dk[sub, 0] + dk[sub, 1]
                        dvp[sub, krows[sub], :] += dv[sub, 0] + dv[sub, 1]
                return carry

            blk(0, 0)
            lax.fori_loop(1, half, blk, 0)
            first_pattern = pi == 0
            if first_pattern:
                for r0 in range(0, seq, region):
                    rows = pl.ds(r0, region)
                    dkn[rows, :] = summed(dkp, r0, region)
                    dvn[rows, :] = summed(dvp, r0, region)
            else:
                _scatter_segments(dqn, lambda start, size: dqp[pl.ds(start, size), :], dil, seq, tmp, accumulate=True)
                for nat, acc in ((dkn, dkp), (dvn, dvp)):
                    _scatter_segments(nat, lambda start, size, acc=acc: summed(acc, start, size),
                                      dil, seq, tmp, accumulate=True)

        def finish(ci, carry):
            rows = pl.ds(pl.multiple_of(ci * ATT_BWD_ROWS, ATT_BWD_ROWS), ATT_BWD_ROWS)
            for src_ref, g_ref, dn, dst_ref, scale, row in (
                    (q_ref, gq_ref, dqn, dq_ref, HEAD_DIM ** -0.5, 0), (k_ref, gk_ref, dkn, dk_ref, 1.0, 1)):
                xv = src_ref[rows, :]
                r = lax.rsqrt(_head_mean(xv * xv, same_ref) + EPS)
                xhat = xv * r
                d = dn[rows, :] * scale
                dg_ref[row:row + 1, :] += jnp.sum(d * xhat, axis=0, keepdims=True)
                dxh = d * g_ref[...]
                dst_ref[rows, :] = (r * (dxh - xhat * _head_mean(dxh * xhat, same_ref))).astype(BF16)
            dv_ref[rows, :] = dvn[rows, :].astype(BF16)
            return carry

        lax.fori_loop(0, seq // ATT_BWD_ROWS, finish, 0)

        @pl.when(step == n_seq * n_hp - 1)
        def _():
            _exchange_wait(*ride_args)

    def col(off):
        return pl.BlockSpec((None, None, seq, LANES), lambda b, hp: (col0 + off * n_hp + hp, b, 0, 0))

    def whole(arr):
        return pl.BlockSpec(arr.shape, lambda b, hp: (0,) * arr.ndim)

    att = pl.BlockSpec((None, seq, LANES), lambda b, hp: (b, 0, hp))
    rows_f32 = pltpu.VMEM((seq, LANES), F32)
    rows_bf16 = pltpu.VMEM((seq, LANES), BF16)
    by_head_bf16 = pltpu.VMEM((2, seq, LANES), BF16)
    per_sub_f32 = pltpu.VMEM((ATT_UNROLL, seq, LANES), F32)
    stats_f32 = pltpu.VMEM((4, seq, LANES), F32)
    return _call(
        body, name="attn_bwd", grid=(n_seq, n_hp),
        in_specs=[col(0), col(1), col(2), att, att, att, whole(g_q2), whole(g_k2),
                  pl.BlockSpec((None, 8, LANES), lambda b, hp: (hp, 0, 0)),
                  whole(dms[0]), whole(dms[1]), whole(dms[2]), whole(same)] + [ANY_SPEC] * n_ride,
        out_specs=[att, att, att, pl.BlockSpec((8, LANES), lambda b, hp: (0, 0))] + [ANY_SPEC] * n_ride,
        out_shape=[_sds((n_seq, seq, D_ATT), BF16)] * 3 + [_sds((8, LANES), F32)] + _exchange_shapes(ride),
        scratch_shapes=[rows_f32, rows_f32, by_head_bf16, by_head_bf16, rows_bf16, rows_bf16, stats_f32, stats_f32,
                        rows_f32, per_sub_f32, per_sub_f32, rows_f32, rows_f32, rows_f32, rows_f32]
        + _exchange_sems(n_ride),
        compiler_params=_params(("arbitrary", "arbitrary")),
    )(proj3, proj3, proj3, do3, y_att3, lse3, g_q2, g_k2, _alibi_rows(), *dms, same, *[a for a, _ in ride])


def _mix_out(uc2, y_att2, x2, mod8, g_ln, b_ln, g_ffn, w_out, seq, tm=512):
    tokens = x2.shape[0]
    per_seq = seq // tm

    def body(uc_ref, ya_ref, x_ref, m_ref, gl_ref, bl_ref, gf_ref, w_ref, yc_ref, mix_ref, x1_ref, h2_ref):
        uc = uc_ref[...]
        mu = jnp.mean(uc, axis=-1, keepdims=True)
        cen = uc - mu
        rs = lax.rsqrt(jnp.mean(cen * cen, axis=-1, keepdims=True) + EPS)
        z = cen * rs * gl_ref[...] + bl_ref[...]
        yc = (z * _sig(z)).astype(BF16)
        yc_ref[...] = yc
        mix = (jnp.dot(yc, w_ref[pl.ds(0, D_CONV), :], preferred_element_type=F32)
               + jnp.dot(ya_ref[...], w_ref[pl.ds(D_CONV, D_ATT), :], preferred_element_type=F32))
        mix_ref[...] = mix.astype(BF16)
        x1 = x_ref[...] + m_ref[2:3, :] * mix
        x1_ref[...] = x1
        r = lax.rsqrt(jnp.mean(x1 * x1, axis=-1, keepdims=True) + EPS)
        h2_ref[...] = ((x1 * r * gf_ref[...]) * (1.0 + m_ref[4:5, :]) + m_ref[3:4, :]).astype(BF16)

    def rows(width):
        return pl.BlockSpec((tm, width), lambda i: (i, 0))

    def vec(width):
        return pl.BlockSpec((1, width), lambda i: (0, 0))

    return _call(
        body, name="mix_out", grid=(tokens // tm,),
        in_specs=[rows(D_CONV), rows(D_ATT), rows(D_MODEL),
                  pl.BlockSpec((None, 8, D_MODEL), lambda i: (i // per_seq, 0, 0)),
                  vec(D_CONV), vec(D_CONV), vec(D_MODEL),
                  pl.BlockSpec((D_MODEL, D_MODEL), lambda i: (0, 0))],
        out_specs=[rows(D_CONV), rows(D_MODEL), rows(D_MODEL), rows(D_MODEL)],
        out_shape=[_sds((tokens, D_CONV), BF16), _sds((tokens, D_MODEL), BF16),
                   _sds((tokens, D_MODEL), F32), _sds((tokens, D_MODEL), BF16)],
        compiler_params=_params(("parallel",)),
    )(uc2, y_att2, x2, mod8, g_ln, b_ln, g_ffn, w_out)


def _mix_out_bwd(dmix, uc2, g_ln, b_ln, w_out, tm=512):
    tokens = dmix.shape[0]

    def body(dm_ref, uc_ref, gl_ref, bl_ref, w_ref, duc_ref, do_ref, dgb_ref):
        @pl.when(pl.program_id(0) == 0)
        def _():
            dgb_ref[...] = jnp.zeros_like(dgb_ref)

        dmv = dm_ref[...]
        dyc = lax.dot_general(dmv, w_ref[pl.ds(0, D_CONV), :], NT, preferred_element_type=F32)
        do_ref[...] = lax.dot_general(dmv, w_ref[pl.ds(D_CONV, D_ATT), :], NT, preferred_element_type=F32)
        uc = uc_ref[...]
        mu = jnp.mean(uc, axis=-1, keepdims=True)
        cen = uc - mu
        rs = lax.rsqrt(jnp.mean(cen * cen, axis=-1, keepdims=True) + EPS)
        xh = cen * rs
        z = xh * gl_ref[...] + bl_ref[...]
        sg = _sig(z)
        dz = dyc * (sg * (1.0 + z * (1.0 - sg)))
        dgb_ref[0:1, :] += jnp.sum(dz * xh, axis=0, keepdims=True)
        dgb_ref[1:2, :] += jnp.sum(dz, axis=0, keepdims=True)
        dxh = dz * gl_ref[...]
        duc_ref[...] = rs * (dxh - jnp.mean(dxh, axis=-1, keepdims=True)
                             - xh * jnp.mean(dxh * xh, axis=-1, keepdims=True))

    return _call(
        body, name="mix_out_bwd", grid=(tokens // tm,),
        in_specs=[pl.BlockSpec((tm, D_MODEL), lambda i: (i, 0)),
                  pl.BlockSpec((tm, D_CONV), lambda i: (i, 0)),
                  pl.BlockSpec((1, D_CONV), lambda i: (0, 0)),
                  pl.BlockSpec((1, D_CONV), lambda i: (0, 0)),
                  pl.BlockSpec((D_MODEL, D_MODEL), lambda i: (0, 0))],
        out_specs=[pl.BlockSpec((tm, D_CONV), lambda i: (i, 0)),
                   pl.BlockSpec((tm, D_ATT), lambda i: (i, 0)),
                   pl.BlockSpec((8, D_CONV), lambda i: (0, 0))],
        out_shape=[_sds((tokens, D_CONV), F32), _sds((tokens, D_ATT), F32), _sds((8, D_CONV), F32)],
        compiler_params=_params(("arbitrary",)),
    )(dmix, uc2, g_ln, b_ln, w_out)


FF_TILE = 256
FF_TILES = D_FF // FF_TILE


def _load_once(hbm_refs, vmem_refs, sems):
    @pl.when(pl.program_id(0) == 0)
    def _():
        copies = [pltpu.make_async_copy(src, dst, sems.at[n]) for n, (src, dst) in enumerate(zip(hbm_refs, vmem_refs))]
        for cp in copies:
            cp.start()
        for cp in copies:
            cp.wait()


def _ffn_fwd(h2, w_gate_t, w_up_t, w_down, x1, target, mod8, seq, tm=512):
    tokens = h2.shape[0]
    per_seq = seq // tm
    n_seq = tokens // seq

    def body(h_ref, m_ref, x1_ref, t_ref, wg_hbm, wu_hbm, wd_hbm, gate_ref, up_ref, dy_ref, df_ref, sq_ref, dgf_ref,
             wg, wu, wd, w_sems):
        i = pl.program_id(0)
        _load_once((wg_hbm, wu_hbm, wd_hbm), (wg, wu, wd), w_sems)

        @pl.when(i == 0)
        def _():
            sq_ref[...] = jnp.zeros_like(sq_ref)

        @pl.when(i % per_seq == 0)
        def _():
            dgf_ref[...] = jnp.zeros_like(dgf_ref)

        hv = h_ref[...]

        def gate_up(t):
            rows = pl.ds(t * FF_TILE, FF_TILE)
            return (lax.dot_general(hv, wg[rows, :], NT, preferred_element_type=F32),
                    lax.dot_general(hv, wu[rows, :], NT, preferred_element_type=F32))

        fv = jnp.zeros((tm, D_MODEL), F32)
        ahead = gate_up(0)
        for t in range(FF_TILES):
            gate, up = ahead
            if t + 1 < FF_TILES:
                ahead = gate_up(t + 1)
            gate_ref[t] = gate.astype(BF16)
            up_ref[t] = up.astype(BF16)
            act = (gate * _sig(gate) * up).astype(BF16)
            fv = fv + jnp.dot(act, wd[pl.ds(t * FF_TILE, FF_TILE), :], preferred_element_type=F32)
        gate_f = m_ref[5:6, :]
        diff = x1_ref[...] + gate_f * fv - t_ref[...]
        sq_ref[0:1, :] += jnp.sum(diff * diff, axis=0, keepdims=True)
        dy = diff * (1.0 / D_MODEL)
        dy_ref[...] = dy
        df_ref[...] = (gate_f * dy).astype(BF16)
        dgf_ref[0:1, :] += jnp.sum(dy * fv, axis=0, keepdims=True)

    rows_spec = pl.BlockSpec((tm, D_MODEL), lambda i: (i, 0))
    per = pl.BlockSpec((None, 8, D_MODEL), lambda i: (i // per_seq, 0, 0))
    tiles = pl.BlockSpec((FF_TILES, tm, FF_TILE), lambda i: (0, i, 0))
    weight = pltpu.VMEM((D_FF, D_MODEL), BF16)
    return _call(
        body, name="ffn_fwd", grid=(tokens // tm,),
        in_specs=[rows_spec, per, rows_spec, rows_spec, ANY_SPEC, ANY_SPEC, ANY_SPEC],
        out_specs=[tiles, tiles, rows_spec, rows_spec, pl.BlockSpec((8, D_MODEL), lambda i: (0, 0)), per],
        out_shape=[_sds((FF_TILES, tokens, FF_TILE), BF16), _sds((FF_TILES, tokens, FF_TILE), BF16),
                   _sds((tokens, D_MODEL), F32), _sds((tokens, D_MODEL), BF16),
                   _sds((8, D_MODEL), F32), _sds((n_seq, 8, D_MODEL), F32)],
        scratch_shapes=[weight, weight, weight, pltpu.SemaphoreType.DMA((3,))],
        compiler_params=_params(("arbitrary",)),
    )(h2, mod8, x1, target, w_gate_t, w_up_t, w_down)


def _ffn_bwd(df, gate, up, w_gate_t, w_up_t, w_down, x1, dy, mix, mod8, g_ffn, seq, tm=256):
    tokens = df.shape[0]
    per_seq = seq // tm
    n_seq = tokens // seq

    def body(df_ref, gate_ref, up_ref, m_ref, g_ref, x1_ref, dy_ref, mix_ref, wg_hbm, wu_hbm, wd_hbm,
             dgate_ref, dup_ref, act_ref, dx1_ref, dmix_ref, dg_ref, dm_ref, wg, wu, wd, w_sems):
        i = pl.program_id(0)
        _load_once((wg_hbm, wu_hbm, wd_hbm), (wg, wu, wd), w_sems)

        @pl.when(i == 0)
        def _():
            dg_ref[...] = jnp.zeros_like(dg_ref)

        @pl.when(i % per_seq == 0)
        def _():
            dm_ref[...] = jnp.zeros_like(dm_ref)

        dfv = df_ref[...]

        def d_act(t):
            return lax.dot_general(dfv, wd[pl.ds(t * FF_TILE, FF_TILE), :], NT, preferred_element_type=F32)

        dh = jnp.zeros((tm, D_MODEL), F32)
        ahead = d_act(0)
        for t in range(FF_TILES):
            dact = ahead
            if t + 1 < FF_TILES:
                ahead = d_act(t + 1)
            rows = pl.ds(t * FF_TILE, FF_TILE)
            gv = gate_ref[t].astype(F32)
            uv = up_ref[t].astype(F32)
            sg = _sig(gv)
            silu = gv * sg
            act_ref[t] = (silu * uv).astype(BF16)
            dup = (dact * silu).astype(BF16)
            dgate = (dact * uv * (sg * (1.0 + gv * (1.0 - sg)))).astype(BF16)
            dup_ref[t] = dup
            dgate_ref[t] = dgate
            dh = dh + (jnp.dot(dgate, wg[rows, :], preferred_element_type=F32)
                       + jnp.dot(dup, wu[rows, :], preferred_element_type=F32))
        g = g_ref[...]
        x1v = x1_ref[...]
        rs = lax.rsqrt(jnp.mean(x1v * x1v, axis=-1, keepdims=True) + EPS)
        xhat = x1v * rs
        dm_ref[0:1, :] += jnp.sum(dh, axis=0, keepdims=True)
        dm_ref[1:2, :] += jnp.sum(dh * (xhat * g), axis=0, keepdims=True)
        dn = dh * (1.0 + m_ref[4:5, :])
        dg_ref[0:1, :] += jnp.sum(dn * xhat, axis=0, keepdims=True)
        dxh = dn * g
        dx1 = dy_ref[...] + rs * (dxh - xhat * jnp.mean(dxh * xhat, axis=-1, keepdims=True))
        dx1_ref[...] = dx1
        dm_ref[2:3, :] += jnp.sum(dx1 * mix_ref[...].astype(F32), axis=0, keepdims=True)
        dmix_ref[...] = (m_ref[2:3, :] * dx1).astype(BF16)

    rows_spec = pl.BlockSpec((tm, D_MODEL), lambda i: (i, 0))
    per = pl.BlockSpec((None, 8, D_MODEL), lambda i: (i // per_seq, 0, 0))
    tiles = pl.BlockSpec((FF_TILES, tm, FF_TILE), lambda i: (0, i, 0))
    weight = pltpu.VMEM((D_FF, D_MODEL), BF16)
    return _call(
        body, name="ffn_bwd", grid=(tokens // tm,),
        in_specs=[rows_spec, tiles, tiles, per, pl.BlockSpec((1, D_MODEL), lambda i: (0, 0)),
                  rows_spec, rows_spec, rows_spec, ANY_SPEC, ANY_SPEC, ANY_SPEC],
        out_specs=[tiles, tiles, tiles, rows_spec, rows_spec, pl.BlockSpec((8, D_MODEL), lambda i: (0, 0)), per],
        out_shape=[_sds((FF_TILES, tokens, FF_TILE), BF16)] * 3
        + [_sds((tokens, D_MODEL), F32), _sds((tokens, D_MODEL), BF16),
           _sds((8, D_MODEL), F32), _sds((n_seq, 8, D_MODEL), F32)],
        scratch_shapes=[weight, weight, weight, pltpu.SemaphoreType.DMA((3,))],
        compiler_params=_params(("arbitrary",)),
    )(df, gate, up, mod8, g_ffn, x1, dy, mix, w_gate_t, w_up_t, w_down)


def _mix_in_bwd(d_a, d_g, d_q, d_k, d_v, w_in, x2, dx1, mod8, g_mix, seq, ride, tm=512):
    tokens = x2.shape[0]
    per_seq = seq // tm
    n_seq = tokens // seq
    parts = (d_a, d_g, d_q, d_k, d_v)
    width = D_CONV
    n_ride = len(ride)
    ride_scatter = [s for _, s in ride]

    def body(*refs):
        da_ref, dg_ref, dq_ref, dk_ref, dv_ref, w_ref, x_ref, dx1_ref, m_ref, g_ref = refs[:10]
        ride_in = refs[10:10 + n_ride]
        gx_ref, dgm_ref, dm_ref = refs[10 + n_ride:13 + n_ride]
        ride_args = (ride_scatter, ride_in, refs[13 + n_ride:13 + 2 * n_ride]) + tuple(refs[13 + 2 * n_ride:])
        i = pl.program_id(0)

        @pl.when(i == 0)
        def _():
            _exchange_start(*ride_args)
            dgm_ref[...] = jnp.zeros_like(dgm_ref)

        @pl.when(i % per_seq == 0)
        def _():
            dm_ref[...] = jnp.zeros_like(dm_ref)

        dh = jnp.zeros((tm, D_MODEL), F32)
        for n, ref in enumerate((da_ref, dg_ref, dq_ref, dk_ref, dv_ref)):
            dh = dh + jnp.dot(ref[...], w_ref[pl.ds(n * width, width), :], preferred_element_type=F32)
        xv = x_ref[...]
        r = lax.rsqrt(jnp.mean(xv * xv, axis=-1, keepdims=True) + EPS)
        xhat = xv * r
        g = g_ref[...]
        dm_ref[0:1, :] += jnp.sum(dh, axis=0, keepdims=True)
        dm_ref[1:2, :] += jnp.sum(dh * (xhat * g), axis=0, keepdims=True)
        dn = dh * (1.0 + m_ref[1:2, :])
        dgm_ref[0:1, :] += jnp.sum(dn * xhat, axis=0, keepdims=True)
        dxh = dn * g
        gx_ref[...] = dx1_ref[...] + r * (dxh - xhat * jnp.mean(dxh * xhat, axis=-1, keepdims=True))

        @pl.when(i == tokens // tm - 1)
        def _():
            _exchange_wait(*ride_args)

    rows = pl.BlockSpec((tm, D_MODEL), lambda i: (i, 0))
    half = pl.BlockSpec((tm, width), lambda i: (i, 0))
    per = pl.BlockSpec((None, 8, D_MODEL), lambda i: (i // per_seq, 0, 0))
    return _call(
        body, name="mix_in_bwd", grid=(tokens // tm,),
        in_specs=[half] * 5 + [pl.BlockSpec((D_IN, D_MODEL), lambda i: (0, 0)), rows, rows, per,
                               pl.BlockSpec((1, D_MODEL), lambda i: (0, 0))] + [ANY_SPEC] * n_ride,
        out_specs=[rows, pl.BlockSpec((8, D_MODEL), lambda i: (0, 0)), per] + [ANY_SPEC] * n_ride,
        out_shape=[_sds((tokens, D_MODEL), F32), _sds((8, D_MODEL), F32), _sds((n_seq, 8, D_MODEL), F32)]
        + _exchange_shapes(ride),
        scratch_shapes=_exchange_sems(n_ride),
        compiler_params=_params(("arbitrary",)),
    )(*parts, w_in, x2, dx1, mod8, g_mix, *[a for a, _ in ride])


def _grad_matmul_parts(a_parts, b_parts, name, tk=1024):
    tokens = a_parts[0].shape[0]
    na, nb = len(a_parts), len(b_parts)
    ma, nbw = a_parts[0].shape[1], b_parts[0].shape[1]

    n_k = tokens // tk

    def body(*refs):
        a_refs, b_refs, o_ref, acc = refs[:na], refs[na:na + nb], refs[na + nb], refs[na + nb + 1]

        @pl.when(pl.program_id(0) == 0)
        def _():
            acc[...] = jnp.zeros_like(acc)

        for i in range(na):
            for j in range(nb):
                acc[pl.ds(i * ma, ma), pl.ds(j * nbw, nbw)] += lax.dot_general(
                    a_refs[i][...], b_refs[j][...], TN, preferred_element_type=F32)

        @pl.when(pl.program_id(0) == n_k - 1)
        def _():
            o_ref[...] = acc[...].astype(o_ref.dtype)

    return _call(
        body, name=name, grid=(n_k,),
        in_specs=[pl.BlockSpec((tk, ma), lambda k: (k, 0))] * na + [pl.BlockSpec((tk, nbw), lambda k: (k, 0))] * nb,
        out_specs=pl.BlockSpec((na * ma, nb * nbw), lambda k: (0, 0)),
        out_shape=_sds((na * ma, nb * nbw), BF16),
        scratch_shapes=[pltpu.VMEM((na * ma, nb * nbw), F32)],
        compiler_params=_params(("arbitrary",)),
    )(*a_parts, *b_parts)


def _grad_matmul_tiles(a, b, name, tk=1024):
    tiled_b = b.ndim == 3
    tiles, tokens, width = b.shape if tiled_b else a.shape
    other = a.shape[1] if tiled_b else b.shape[1]
    out_tile = (other, width) if tiled_b else (width, other)
    n_k = tokens // tk

    def body(a_ref, b_ref, o_ref, acc):
        @pl.when(pl.program_id(0) == 0)
        def _():
            acc[...] = jnp.zeros_like(acc)

        for t in range(tiles):
            lhs = a_ref[...] if tiled_b else a_ref[t]
            rhs = b_ref[t] if tiled_b else b_ref[...]
            acc[t] += lax.dot_general(lhs, rhs, TN, preferred_element_type=F32)

        @pl.when(pl.program_id(0) == n_k - 1)
        def _():
            o_ref[...] = acc[...].astype(o_ref.dtype)

    flat = pl.BlockSpec((tk, other), lambda k: (k, 0))
    tiled = pl.BlockSpec((tiles, tk, width), lambda k: (0, k, 0))
    return _call(
        body, name=name, grid=(n_k,),
        in_specs=[flat, tiled] if tiled_b else [tiled, flat],
        out_specs=pl.BlockSpec((tiles,) + out_tile, lambda k: (0, 0, 0)),
        out_shape=_sds((tiles,) + out_tile, BF16),
        scratch_shapes=[pltpu.VMEM((tiles,) + out_tile, F32)],
        compiler_params=_params(("arbitrary",)),
    )(a, b)


def _adamw(w, m, v, g, name, n_parts=0, tr=256):
    rows, cols = w.shape
    tr = min(tr, rows)
    c1 = 1.0 - ADAM_B1 ** ADAM_STEP
    c2 = 1.0 - ADAM_B2 ** ADAM_STEP

    def body(w_ref, m_ref, v_ref, g_ref, go_ref, d_ref, mo_ref, vo_ref):
        if n_parts:
            gv = g_ref[0].astype(F32)
            for p in range(1, n_parts):
                gv = gv + g_ref[p].astype(F32)
        else:
            gv = g_ref[...]
        go_ref[...] = gv
        mn = ADAM_B1 * m_ref[...] + (1.0 - ADAM_B1) * gv
        vn = ADAM_B2 * v_ref[...] + (1.0 - ADAM_B2) * (gv * gv)
        mo_ref[...] = mn
        vo_ref[...] = vn
        d_ref[...] = -ADAM_LR * ((mn / c1) / (jnp.sqrt(vn / c2) + ADAM_EPS) + ADAM_WD * w_ref[...])

    blk = pl.BlockSpec((tr, cols), lambda i: (i, 0))
    g_spec = pl.BlockSpec((n_parts, tr, cols), lambda i: (0, i, 0)) if n_parts else blk
    return _call(
        body, name=name, grid=(rows // tr,),
        in_specs=[blk, blk, blk, g_spec], out_specs=[blk] * 4,
        out_shape=[_sds((rows, cols), F32)] * 4,
        compiler_params=_params(("parallel",)),
    )(w, m, v, g)


def _cols_to_full(blocks):
    n, r, c = blocks.shape
    return jnp.transpose(blocks, (1, 0, 2)).reshape(r, n * c)


def _pad_lanes(v, width):
    return jnp.pad(v, ((0, 0), (0, width - v.shape[1])))


def kernel(x, c, w_ada, b_ada, g_mix, w_in, w_dw, b_dw, g_conv_ln, b_conv_ln, g_q, g_k, w_out, g_ffn, w_gate, w_up, w_down, loss_target, m_w_ada, m_b_ada, m_g_mix, m_w_in, m_w_dw, m_b_dw, m_g_conv_ln, m_b_conv_ln, m_g_q, m_g_k, m_w_out, m_g_ffn, m_w_gate, m_w_up, m_w_down, v_w_ada, v_b_ada, v_g_mix, v_w_in, v_w_dw, v_b_dw, v_g_conv_ln, v_b_conv_ln, v_g_q, v_g_k, v_w_out, v_g_ffn, v_w_gate, v_w_up, v_w_down):
    n_seq, seq, _ = x.shape
    tokens = n_seq * seq
    me = 4 * lax.axis_index("x") + 2 * lax.axis_index("y") + lax.axis_index("c")
    ada_cols = w_ada.shape[2]
    dw_cols = w_dw.shape[2]

    def transposed(w):
        return jnp.transpose(w[0])

    (c_g, w_in_g, w_dw_g) = _gather_by_chip([c, transposed(w_in).astype(BF16), w_dw[0]], "gather_weights")
    c_all = c_g.reshape(N_DEV * n_seq, D_MODEL)
    w_in_t = w_in_g.reshape(D_IN, D_MODEL)
    w_dw_f = _cols_to_full(w_dw_g)

    b_cols = lax.dynamic_slice(b_ada, (0, me * ada_cols), (1, ada_cols))
    mod_cols = _ada_fwd(c_all, w_ada[0], b_cols)
    (mod_g,) = _exchange([(mod_cols, False)], "gather_mod")
    mod_mine = lax.dynamic_slice(mod_g, (0, me * n_seq, 0), (N_DEV, n_seq, ada_cols))
    mod = jnp.transpose(mod_mine, (1, 0, 2)).reshape(n_seq, N_MOD, D_MODEL)
    mod8 = jnp.pad(mod, ((0, 0), (0, 8 - N_MOD), (0, 0)))

    x2 = x.reshape(tokens, D_MODEL)
    h1, proj = _mix_in(x2, mod8, g_mix, w_in_t, seq)
    proj3 = proj.reshape(D_IN // LANES, n_seq, seq, LANES)
    uc3 = _conv_fwd(proj3, w_dw_f, b_dw)
    g_q2, g_k2 = jnp.tile(g_q, (1, 2)), jnp.tile(g_k, (1, 2))
    y_att3, lse3, w_out_g, w_gate_g, w_up_g, w_down_g = _attn_fwd(
        proj3, g_q2, g_k2,
        [(w_out[0].astype(BF16), False), (transposed(w_gate).astype(BF16), False),
         (transposed(w_up).astype(BF16), False), (w_down[0].astype(BF16), False)])
    w_out_f = w_out_g.reshape(D_MODEL, D_MODEL)
    w_gate_f = w_gate_g.reshape(D_FF, D_MODEL)
    w_up_f = w_up_g.reshape(D_FF, D_MODEL)
    w_down_f = w_down_g.reshape(D_FF, D_MODEL)
    uc2 = uc3.reshape(tokens, D_CONV)
    y_att2 = y_att3.reshape(tokens, D_ATT)
    y_conv, mix, x1, h2 = _mix_out(uc2, y_att2, x2, mod8, g_conv_ln, b_conv_ln, g_ffn, w_out_f, seq)
    gate, up, dy, df, sq, dgate_f = _ffn_fwd(
        h2, w_gate_f, w_up_f, w_down_f, x1, loss_target.reshape(tokens, D_MODEL), mod8, seq)

    dgate, dup, act, dx1, dmix, dg_ffn, dmod_f = _ffn_bwd(
        df, gate, up, w_gate_f, w_up_f, w_down_f, x1, dy, mix, mod8, g_ffn, seq)
    duc2, do2, dgb_ln = _mix_out_bwd(dmix, uc2, g_conv_ln, b_conv_ln, w_out_f)
    d_a3, d_g3, dw_dw_p, db_dw_p = _conv_bwd(duc2.reshape(n_seq, seq, D_CONV), proj3, w_dw_f)
    gw_gate = _grad_matmul_tiles(dgate, h2, "grad_w_gate")
    gw_up = _grad_matmul_tiles(dup, h2, "grad_w_up")
    gw_down = _grad_matmul_tiles(act, df, "grad_w_down")
    gw_out = _grad_matmul_parts([y_conv, y_att2], [dmix], "grad_w_out")
    d_q3, d_k3, d_v3, dg_qk, p_gate, p_up, p_down, p_out = _attn_bwd(
        proj3, do2.reshape(n_seq, seq, D_ATT), y_att3, lse3, g_q2, g_k2,
        [(gw_gate.reshape(N_DEV, D_FF // N_DEV, D_MODEL), True), (gw_up.reshape(N_DEV, D_FF // N_DEV, D_MODEL), True),
         (gw_down.reshape(N_DEV, D_FF // N_DEV, D_MODEL), True),
         (gw_out.reshape(N_DEV, D_MODEL // N_DEV, D_MODEL), True)])
    flat = lambda t: t.reshape(tokens, t.shape[-1])
    d_a, d_g, d_q, d_k, d_v = flat(d_a3), flat(d_g3), flat(d_q3), flat(d_k3), flat(d_v3)
    gw_in = _grad_matmul_parts([d_a, d_g, d_q, d_k, d_v], [h1], "grad_w_in")
    grad_x2, dg_mix, dmod_m, p_in = _mix_in_bwd(
        d_a, d_g, d_q, d_k, d_v, w_in_t, x2, dx1, mod8, g_mix, seq,
        [(gw_in.reshape(N_DEV, D_IN // N_DEV, D_MODEL), True)])

    dmod = jnp.concatenate([dmod_m[:, 0], dmod_m[:, 1], dmod_f[:, 2], dmod_f[:, 0], dmod_f[:, 1], dgate_f[:, 0]], axis=1)
    dg_q = dg_qk[0:1, 0:HEAD_DIM] + dg_qk[0:1, HEAD_DIM:]
    dg_k = dg_qk[1:2, 0:HEAD_DIM] + dg_qk[1:2, HEAD_DIM:]
    loss_part = (0.5 / D_MODEL) * jnp.sum(sq[0:1, :], axis=1, keepdims=True)
    small = jnp.concatenate(
        [dg_mix[0:1], dg_ffn[0:1], db_dw_p[0:1], dgb_ln[0:1], dgb_ln[1:2],
         _pad_lanes(dg_q, LANES), _pad_lanes(dg_k, LANES), _pad_lanes(loss_part, LANES)], axis=1)
    n_small = small.shape[1] - LANES

    (dmod_g, small_g, dw_g) = _exchange([(dmod, False), (small, False), (dw_dw_p, False)], "gather_small_grads")

    dmod_all = dmod_g.reshape(N_DEV * n_seq, N_MOD * D_MODEL)
    dmod_cols = lax.dynamic_slice(dmod_all, (0, me * ada_cols), (N_DEV * n_seq, ada_cols))
    gw_ada, gb_ada = _ada_bwd(c_all, dmod_cols, dmod_all)

    res = {}
    res["w_ada"] = _adamw(w_ada[0], m_w_ada[0], v_w_ada[0], gw_ada, "adamw_w_ada")
    res["b_ada"] = _adamw(b_ada, m_b_ada, v_b_ada, gb_ada, "adamw_b_ada")
    def adamw_transposed(w, m, v, parts, name, tr):
        outs = _adamw(transposed(w), transposed(m), transposed(v), parts, name, N_DEV, tr=tr)
        return tuple(jnp.transpose(o) for o in outs)

    res["w_in"] = adamw_transposed(w_in, m_w_in, v_w_in, p_in, "adamw_w_in", 160)
    res["w_out"] = _adamw(w_out[0], m_w_out[0], v_w_out[0], p_out, "adamw_w_out", N_DEV)
    res["w_gate"] = adamw_transposed(w_gate, m_w_gate, v_w_gate, p_gate, "adamw_w_gate", 176)
    res["w_up"] = adamw_transposed(w_up, m_w_up, v_w_up, p_up, "adamw_w_up", 176)
    res["w_down"] = _adamw(w_down[0], m_w_down[0], v_w_down[0], p_down, "adamw_w_down", N_DEV, tr=176)
    dw_mine = lax.dynamic_slice(dw_g, (0, 0, me * dw_cols), (N_DEV, CONV_WIDTH, dw_cols))
    res["w_dw"] = _adamw(w_dw[0], m_w_dw[0], v_w_dw[0], dw_mine, "adamw_w_dw", N_DEV)

    small_names = ["g_mix", "g_ffn", "b_dw", "g_conv_ln", "b_conv_ln", "g_q", "g_k"]
    small_w = {"g_mix": (g_mix, m_g_mix, v_g_mix), "g_ffn": (g_ffn, m_g_ffn, v_g_ffn), "b_dw": (b_dw, m_b_dw, v_b_dw),
               "g_conv_ln": (g_conv_ln, m_g_conv_ln, v_g_conv_ln), "b_conv_ln": (b_conv_ln, m_b_conv_ln, v_b_conv_ln),
               "g_q": (g_q, m_g_q, v_g_q), "g_k": (g_k, m_g_k, v_g_k)}
    widths = [max(small_w[n][0].shape[1], LANES) for n in small_names]
    packed = [jnp.concatenate([_pad_lanes(small_w[n][i], wd) for n, wd in zip(small_names, widths)], axis=1) for i in range(3)]
    outs = _adamw(packed[0], packed[1], packed[2], small_g[:, :, :n_small], "adamw_small", N_DEV)
    off = 0
    for n, wd in zip(small_names, widths):
        real = small_w[n][0].shape[1]
        res[n] = tuple(o[:, off:off + real] for o in outs)
        off += wd
    loss = jnp.sum(small_g[:, 0, n_small])

    order = ["w_ada", "b_ada", "g_mix", "w_in", "w_dw", "b_dw", "g_conv_ln", "b_conv_ln", "g_q", "g_k",
             "w_out", "g_ffn", "w_gate", "w_up", "w_down"]
    lead = {"w_ada", "w_in", "w_dw", "w_out", "w_gate", "w_up", "w_down"}
    grads, deltas, new_m, new_v = [], [], [], []
    for n in order:
        g, d, mn, vn = res[n]
        g, d, mn, vn = (t[None] if n in lead else t for t in (g, d, mn, vn))
        grads.append(g)
        deltas.append(d)
        new_m.append(mn)
        new_v.append(vn)
    return (loss, grad_x2.reshape(n_seq, seq, D_MODEL), *grads, *deltas, *new_m, *new_v)
```
